```python
import jax
import jax.numpy as jnp
from jax import lax
import numpy as np

D_MODEL = 1024
BATCH = 8
SEQ = 2048
DEPTH = 2

N_META = 16
RMS_EPS = 1e-6
LN_EPS = 1e-5
D_A = D_MODEL // 2
D_B = D_MODEL // 2
CONV_A_WIDTH = 31
CONV_B_WIDTH = 3
EVEN_COLS = 2 * D_A + 3 * D_B
HEAD_DIM = 64
N_Q_HEADS = 8
N_KV_HEADS = 2
GQA_GROUP = N_Q_HEADS // N_KV_HEADS
D_ATT = N_Q_HEADS * HEAD_DIM
D_KV = N_KV_HEADS * HEAD_DIM
WINDOW = 128
BLOCK = 128
ROPE_THETA = 10000.0
RWKV_HEAD = 64
D_R = D_MODEL // 2
N_R_HEADS = D_R // RWKV_HEAD
LORA_W = 64
LORA_A = 64
LORA_G = 128
RWKV_GN_EPS = 64e-5
ATT_COLS = D_ATT + 2 * D_KV
RWKV_COLS = 3 * D_R + LORA_W + LORA_A + LORA_G
ODD_COLS = ATT_COLS + RWKV_COLS
D_FF = 2816
FF_CONV_WIDTH = 3
NEG_INF = -1e30

kernel_name = 'hybrid_conv_swa_rwkv7_block'


def rms_norm(x, g):
    xf = x.astype(jnp.float32)
    y = xf * lax.rsqrt(jnp.mean(xf * xf, axis=-1, keepdims=True) + RMS_EPS)
    return (y * g.astype(jnp.float32)).astype(x.dtype)


def layer_norm(x, g, b):
    xf = x.astype(jnp.float32)
    mu = jnp.mean(xf, axis=-1, keepdims=True)
    var = jnp.mean(jnp.square(xf - mu), axis=-1, keepdims=True)
    y = (xf - mu) * lax.rsqrt(var + LN_EPS)
    return (y * g.astype(jnp.float32) + b.astype(jnp.float32)).astype(x.dtype)


def causal_dwconv(x, w):
    k_width, ch = w.shape
    return lax.conv_general_dilated(
        x, w[:, None, :].astype(x.dtype), window_strides=(1,),
        padding=[(k_width - 1, 0)], dimension_numbers=('NWC', 'WIO', 'NWC'),
        feature_group_count=ch)


def rope(x, pos):
    half = x.shape[-1] // 2
    inv = ROPE_THETA ** (-jnp.arange(half, dtype=jnp.float32) / half)
    ang = pos.astype(jnp.float32)[:, None] * inv[None, :]
    cos = jnp.cos(ang)[None, :, None, :]
    sin = jnp.sin(ang)[None, :, None, :]
    xf = x.astype(jnp.float32)
    x1, x2 = xf[..., :half], xf[..., half:]
    return jnp.concatenate([x1 * cos - x2 * sin, x2 * cos + x1 * sin], axis=-1).astype(x.dtype)


def conformer_conv_group(a_val, a_gate, conv_w, ln_g, ln_b):
    u = a_val * jax.nn.sigmoid(a_gate)
    u = causal_dwconv(u, conv_w)
    return jax.nn.silu(layer_norm(u, ln_g, ln_b))


def short_conv_group(gate_b, gate_c, x_in, conv_w):
    return gate_b * causal_dwconv(gate_c * x_in, conv_w)


def even_mixer(h, w_in, conv_a, ln_a_g, ln_a_b, conv_b, w_out):
    p = h @ w_in
    a_val, a_gate, g_b, g_c, x_in = jnp.split(
        p, [D_A, 2 * D_A, 2 * D_A + D_B, 2 * D_A + 2 * D_B], axis=-1)
    y = jnp.concatenate([conformer_conv_group(a_val, a_gate, conv_a, ln_a_g, ln_a_b),
                         short_conv_group(g_b, g_c, x_in, conv_b)], axis=-1)
    return y @ w_out


def swa_sink_attention(q, k, v, sinks):
    bsz, t_len = q.shape[:2]
    pad = BLOCK - N_META
    t_pad = t_len + pad
    nb = t_pad // BLOCK
    padt = lambda z: jnp.pad(z, ((0, 0), (pad, 0), (0, 0), (0, 0)))
    qb = padt(q).reshape(bsz, nb, BLOCK, N_KV_HEADS, GQA_GROUP, HEAD_DIM)
    kb = padt(k).reshape(bsz, nb, BLOCK, N_KV_HEADS, HEAD_DIM)
    vb = padt(v).reshape(bsz, nb, BLOCK, N_KV_HEADS, HEAD_DIM)

    def band(z):
        prev = jnp.pad(z, ((0, 0), (1, 0), (0, 0), (0, 0), (0, 0)))[:, :-1]
        return jnp.concatenate([prev, z], axis=2)

    k_band, v_band = band(kb), band(vb)
    k_meta, v_meta = k[:, :N_META], v[:, :N_META]
    scale = HEAD_DIM ** -0.5
    s_band = jnp.einsum('bnqkgd,bnskd->bnkgqs', qb, k_band).astype(jnp.float32) * scale
    s_meta = jnp.einsum('bnqkgd,bmkd->bnkgqm', qb, k_meta).astype(jnp.float32) * scale

    blk0 = jnp.arange(nb)[:, None] * BLOCK
    t_pos = blk0 + jnp.arange(BLOCK)[None, :]
    s_pos = blk0 - BLOCK + jnp.arange(2 * BLOCK)[None, :]
    dist = t_pos[:, :, None] - s_pos[:, None, :]
    band_ok = (s_pos[:, None, :] >= BLOCK) & (dist >= 0) & (dist < WINDOW)
    meta_ok = (pad + jnp.arange(N_META))[None, None, :] <= t_pos[:, :, None]
    s_band = jnp.where(band_ok[None, :, None, None], s_band, NEG_INF)
    s_meta = jnp.where(meta_ok[None, :, None, None], s_meta, NEG_INF)
    s_sink = jnp.broadcast_to(
        sinks.astype(jnp.float32).reshape(1, 1, N_KV_HEADS, GQA_GROUP, 1, 1),
        s_band.shape[:-1] + (1,))
    prob = jax.nn.softmax(jnp.concatenate([s_band, s_meta, s_sink], axis=-1), axis=-1)
    p_band = prob[..., :2 * BLOCK].astype(v.dtype)
    p_meta = prob[..., 2 * BLOCK:2 * BLOCK + N_META].astype(v.dtype)
    out = (jnp.einsum('bnkgqs,bnskd->bnqkgd', p_band, v_band)
           + jnp.einsum('bnkgqm,bmkd->bnqkgd', p_meta, v_meta))
    return out.reshape(bsz, t_pad, D_ATT)[:, pad:]


def wkv7_scan(r, w, k, v, a, b):
    bsz, _, nh, n = r.shape

    def step(s, inp):
        r_t, w_t, k_t, v_t, a_t, b_t = inp
        sa = jnp.einsum('bhij,bhj->bhi', s, a_t)
        s = s * w_t[:, :, None, :] + sa[..., None] * b_t[:, :, None, :] + v_t[..., None] * k_t[:, :, None, :]
        return s, jnp.einsum('bhij,bhj->bhi', s, r_t)

    xs = tuple(jnp.moveaxis(z, 1, 0) for z in (r, w, k, v, a, b))
    s0 = jnp.zeros((bsz, nh, n, n), jnp.float32)
    _, y = lax.scan(step, s0, xs)
    return jnp.moveaxis(y, 0, 1)


def rwkv7_group(pr, mu, w0, w2, a0, a2, g2, k_k, k_a, r_k, lnx_g, lnx_b):
    f32 = jnp.float32
    bsz, t_len, _ = pr.shape
    pr = pr.astype(f32)
    prev = jnp.pad(pr, ((0, 0), (1, 0), (0, 0)))[:, :-1]
    pr = pr + (prev - pr) * mu.astype(f32)
    r, k, v, wd, ad, gd = jnp.split(
        pr, [D_R, 2 * D_R, 3 * D_R, 3 * D_R + LORA_W, 3 * D_R + LORA_W + LORA_A], axis=-1)
    w_log = -jax.nn.softplus(-(w0.astype(f32) + jnp.tanh(wd) @ w2.astype(f32))) - 0.5
    decay = jnp.exp(-jnp.exp(w_log))
    alpha = jax.nn.sigmoid(a0.astype(f32) + ad @ a2.astype(f32))
    g = jax.nn.sigmoid(gd) @ g2.astype(f32)
    heads = lambda z: z.reshape(bsz, t_len, N_R_HEADS, RWKV_HEAD)
    kk = heads(k * k_k.astype(f32))
    kk = kk / jnp.maximum(jnp.sqrt(jnp.sum(kk * kk, axis=-1, keepdims=True)), 1e-12)
    k = k * (1.0 + (alpha - 1.0) * k_a.astype(f32))
    r_h, k_h, v_h, a_h = heads(r), heads(k), heads(v), heads(alpha)
    y = wkv7_scan(r_h, heads(decay), k_h, v_h, -kk, kk * a_h)
    mean = jnp.mean(y, axis=-1, keepdims=True)
    var = jnp.mean(jnp.square(y - mean), axis=-1, keepdims=True)
    y = ((y - mean) * lax.rsqrt(var + RWKV_GN_EPS)).reshape(bsz, t_len, D_R)
    y = y * lnx_g.astype(f32) + lnx_b.astype(f32)
    bonus = jnp.sum(r_h * k_h * r_k.astype(f32), axis=-1, keepdims=True) * v_h
    y = y + bonus.reshape(bsz, t_len, D_R)
    return y * g


def odd_mixer(h, w_in, sinks, mu, w0, w2, a0, a2, g2, k_k, k_a, r_k, lnx_g, lnx_b, w_out):
    bsz, t_len, _ = h.shape
    p = h @ w_in
    q = p[..., :D_ATT].reshape(bsz, t_len, N_Q_HEADS, HEAD_DIM)
    k = p[..., D_ATT:D_ATT + D_KV].reshape(bsz, t_len, N_KV_HEADS, HEAD_DIM)
    v = p[..., D_ATT + D_KV:ATT_COLS].reshape(bsz, t_len, N_KV_HEADS, HEAD_DIM)
    pos = jnp.arange(t_len)
    y_att = swa_sink_attention(rope(q, pos), rope(k, pos), v, sinks)
    y_rwkv = rwkv7_group(p[..., ATT_COLS:], mu, w0, w2, a0, a2, g2, k_k, k_a, r_k, lnx_g, lnx_b)
    y = jnp.concatenate([y_att.astype(h.dtype), y_rwkv.astype(h.dtype)], axis=-1)
    return y @ w_out


def conv_glu(h, w_up, conv_w, conv_b, w_down):
    u = h @ w_up
    gate, val = u[..., :D_FF], u[..., D_FF:]
    gate = causal_dwconv(gate, conv_w) + conv_b.astype(h.dtype)
    return (jax.nn.silu(gate) * val) @ w_down


def _fwd_setup_inputs(seed: int = 0) -> dict:
    key = jax.random.key(seed)
    ks = iter(jax.random.split(key, 32))
    f32 = jnp.float32
    nrm = lambda shape, s: jax.random.normal(next(ks), shape, f32) * s
    uni = lambda shape, lo, hi: jax.random.uniform(next(ks), shape, f32, lo, hi)
    ne = (DEPTH + 1) // 2
    no = DEPTH // 2
    return {
        'x': nrm((BATCH, SEQ, D_MODEL), 1.0),
        'meta_tokens': nrm((N_META, D_MODEL), 1.0),
        'norm_mix': 1.0 + nrm((DEPTH, D_MODEL), 0.02),
        'norm_ffn': 1.0 + nrm((DEPTH, D_MODEL), 0.02),
        'norm_final': 1.0 + nrm((D_MODEL,), 0.02),
        'ev_w_in': nrm((ne, D_MODEL, EVEN_COLS), D_MODEL ** -0.5),
        'ev_conv_a': nrm((ne, CONV_A_WIDTH, D_A), CONV_A_WIDTH ** -0.5),
        'ev_ln_a_g': 1.0 + nrm((ne, D_A), 0.02),
        'ev_ln_a_b': nrm((ne, D_A), 0.02),
        'ev_conv_b': nrm((ne, CONV_B_WIDTH, D_B), CONV_B_WIDTH ** -0.5),
        'ev_w_out': nrm((ne, D_A + D_B, D_MODEL), (D_A + D_B) ** -0.5),
        'od_w_in': nrm((no, D_MODEL, ODD_COLS), D_MODEL ** -0.5),
        'od_sinks': nrm((no, N_Q_HEADS), 0.5),
        'od_mu': uni((no, RWKV_COLS), 0.0, 1.0),
        'od_w0': uni((no, D_R), -6.0, -1.0),
        'od_w2': nrm((no, LORA_W, D_R), 0.1),
        'od_a0': nrm((no, D_R), 0.1),
        'od_a2': nrm((no, LORA_A, D_R), 0.1),
        'od_g2': nrm((no, LORA_G, D_R), LORA_G ** -0.5),
        'od_k_k': 0.85 + nrm((no, D_R), 0.02),
        'od_k_a': 1.0 + nrm((no, D_R), 0.02),
        'od_r_k': nrm((no, N_R_HEADS, RWKV_HEAD), 0.1),
        'od_lnx_g': 1.0 + nrm((no, D_R), 0.02),
        'od_lnx_b': nrm((no, D_R), 0.02),
        'od_w_out': nrm((no, D_ATT + D_R, D_MODEL), (D_ATT + D_R) ** -0.5),
        'ff_w_up': nrm((DEPTH, D_MODEL, 2 * D_FF), D_MODEL ** -0.5),
        'ff_conv': nrm((DEPTH, FF_CONV_WIDTH, D_FF), FF_CONV_WIDTH ** -0.5),
        'ff_conv_b': nrm((DEPTH, D_FF), 0.02),
        'ff_w_down': nrm((DEPTH, D_FF, D_MODEL), D_FF ** -0.5),
    }


def _fwd_reference(x, meta_tokens, norm_mix, norm_ffn, norm_final,
              ev_w_in, ev_conv_a, ev_ln_a_g, ev_ln_a_b, ev_conv_b, ev_w_out,
              od_w_in, od_sinks, od_mu, od_w0, od_w2, od_a0, od_a2, od_g2,
              od_k_k, od_k_a, od_r_k, od_lnx_g, od_lnx_b, od_w_out,
              ff_w_up, ff_conv, ff_conv_b, ff_w_down):
    bsz = x.shape[0]
    meta = jnp.broadcast_to(meta_tokens[None].astype(x.dtype), (bsz, N_META, D_MODEL))
    h = jnp.concatenate([meta, x], axis=1)
    for i in range(DEPTH):
        hn = rms_norm(h, norm_mix[i])
        j = i // 2
        if i % 2 == 0:
            h = h + even_mixer(hn, ev_w_in[j], ev_conv_a[j], ev_ln_a_g[j], ev_ln_a_b[j],
                               ev_conv_b[j], ev_w_out[j])
        else:
            h = h + odd_mixer(hn, od_w_in[j], od_sinks[j], od_mu[j], od_w0[j], od_w2[j],
                              od_a0[j], od_a2[j], od_g2[j], od_k_k[j], od_k_a[j], od_r_k[j],
                              od_lnx_g[j], od_lnx_b[j], od_w_out[j])
        h = h + conv_glu(rms_norm(h, norm_ffn[i]), ff_w_up[i], ff_conv[i], ff_conv_b[i], ff_w_down[i])
    return rms_norm(h, norm_final)[:, N_META:]


import jax as _jax
import jax.numpy as _jnp

TWIN_FORMAT = 'train_step'
FWD_PARAMS = ['x', 'meta_tokens', 'norm_mix', 'norm_ffn', 'norm_final', 'ev_w_in', 'ev_conv_a', 'ev_ln_a_g', 'ev_ln_a_b', 'ev_conv_b', 'ev_w_out', 'od_w_in', 'od_sinks', 'od_mu', 'od_w0', 'od_w2', 'od_a0', 'od_a2', 'od_g2', 'od_k_k', 'od_k_a', 'od_r_k', 'od_lnx_g', 'od_lnx_b', 'od_w_out', 'ff_w_up', 'ff_conv', 'ff_conv_b', 'ff_w_down']
TWIN_WEIGHTS = ['meta_tokens', 'norm_mix', 'norm_ffn', 'norm_final', 'ev_w_in', 'ev_conv_a', 'ev_ln_a_g', 'ev_ln_a_b', 'ev_conv_b', 'ev_w_out', 'od_w_in', 'od_sinks', 'od_mu', 'od_w0', 'od_w2', 'od_a0', 'od_a2', 'od_g2', 'od_k_k', 'od_k_a', 'od_r_k', 'od_lnx_g', 'od_lnx_b', 'od_w_out', 'ff_w_up', 'ff_conv', 'ff_conv_b', 'ff_w_down']
TWIN_DIFF_INPUT = 'x'
TWIN_INPUTS = ['x', 'meta_tokens', 'norm_mix', 'norm_ffn', 'norm_final', 'ev_w_in', 'ev_conv_a', 'ev_ln_a_g', 'ev_ln_a_b', 'ev_conv_b', 'ev_w_out', 'od_w_in', 'od_sinks', 'od_mu', 'od_w0', 'od_w2', 'od_a0', 'od_a2', 'od_g2', 'od_k_k', 'od_k_a', 'od_r_k', 'od_lnx_g', 'od_lnx_b', 'od_w_out', 'ff_w_up', 'ff_conv', 'ff_conv_b', 'ff_w_down', 'loss_target', 'm_meta_tokens', 'm_norm_mix', 'm_norm_ffn', 'm_norm_final', 'm_ev_w_in', 'm_ev_conv_a', 'm_ev_ln_a_g', 'm_ev_ln_a_b', 'm_ev_conv_b', 'm_ev_w_out', 'm_od_w_in', 'm_od_sinks', 'm_od_mu', 'm_od_w0', 'm_od_w2', 'm_od_a0', 'm_od_a2', 'm_od_g2', 'm_od_k_k', 'm_od_k_a', 'm_od_r_k', 'm_od_lnx_g', 'm_od_lnx_b', 'm_od_w_out', 'm_ff_w_up', 'm_ff_conv', 'm_ff_conv_b', 'm_ff_w_down', 'v_meta_tokens', 'v_norm_mix', 'v_norm_ffn', 'v_norm_final', 'v_ev_w_in', 'v_ev_conv_a', 'v_ev_ln_a_g', 'v_ev_ln_a_b', 'v_ev_conv_b', 'v_ev_w_out', 'v_od_w_in', 'v_od_sinks', 'v_od_mu', 'v_od_w0', 'v_od_w2', 'v_od_a0', 'v_od_a2', 'v_od_g2', 'v_od_k_k', 'v_od_k_a', 'v_od_r_k', 'v_od_lnx_g', 'v_od_lnx_b', 'v_od_w_out', 'v_ff_w_up', 'v_ff_conv', 'v_ff_conv_b', 'v_ff_w_down']
TWIN_OUTPUTS = ['loss', 'grad_x', 'grad_meta_tokens', 'grad_norm_mix', 'grad_norm_ffn', 'grad_norm_final', 'grad_ev_w_in', 'grad_ev_conv_a', 'grad_ev_ln_a_g', 'grad_ev_ln_a_b', 'grad_ev_conv_b', 'grad_ev_w_out', 'grad_od_w_in', 'grad_od_sinks', 'grad_od_mu', 'grad_od_w0', 'grad_od_w2', 'grad_od_a0', 'grad_od_a2', 'grad_od_g2', 'grad_od_k_k', 'grad_od_k_a', 'grad_od_r_k', 'grad_od_lnx_g', 'grad_od_lnx_b', 'grad_od_w_out', 'grad_ff_w_up', 'grad_ff_conv', 'grad_ff_conv_b', 'grad_ff_w_down', 'delta_meta_tokens', 'delta_norm_mix', 'delta_norm_ffn', 'delta_norm_final', 'delta_ev_w_in', 'delta_ev_conv_a', 'delta_ev_ln_a_g', 'delta_ev_ln_a_b', 'delta_ev_conv_b', 'delta_ev_w_out', 'delta_od_w_in', 'delta_od_sinks', 'delta_od_mu', 'delta_od_w0', 'delta_od_w2', 'delta_od_a0', 'delta_od_a2', 'delta_od_g2', 'delta_od_k_k', 'delta_od_k_a', 'delta_od_r_k', 'delta_od_lnx_g', 'delta_od_lnx_b', 'delta_od_w_out', 'delta_ff_w_up', 'delta_ff_conv', 'delta_ff_conv_b', 'delta_ff_w_down', 'new_m_meta_tokens', 'new_m_norm_mix', 'new_m_norm_ffn', 'new_m_norm_final', 'new_m_ev_w_in', 'new_m_ev_conv_a', 'new_m_ev_ln_a_g', 'new_m_ev_ln_a_b', 'new_m_ev_conv_b', 'new_m_ev_w_out', 'new_m_od_w_in', 'new_m_od_sinks', 'new_m_od_mu', 'new_m_od_w0', 'new_m_od_w2', 'new_m_od_a0', 'new_m_od_a2', 'new_m_od_g2', 'new_m_od_k_k', 'new_m_od_k_a', 'new_m_od_r_k', 'new_m_od_lnx_g', 'new_m_od_lnx_b', 'new_m_od_w_out', 'new_m_ff_w_up', 'new_m_ff_conv', 'new_m_ff_conv_b', 'new_m_ff_w_down', 'new_v_meta_tokens', 'new_v_norm_mix', 'new_v_norm_ffn', 'new_v_norm_final', 'new_v_ev_w_in', 'new_v_ev_conv_a', 'new_v_ev_ln_a_g', 'new_v_ev_ln_a_b', 'new_v_ev_conv_b', 'new_v_ev_w_out', 'new_v_od_w_in', 'new_v_od_sinks', 'new_v_od_mu', 'new_v_od_w0', 'new_v_od_w2', 'new_v_od_a0', 'new_v_od_a2', 'new_v_od_g2', 'new_v_od_k_k', 'new_v_od_k_a', 'new_v_od_r_k', 'new_v_od_lnx_g', 'new_v_od_lnx_b', 'new_v_od_w_out', 'new_v_ff_w_up', 'new_v_ff_conv', 'new_v_ff_conv_b', 'new_v_ff_w_down']
TWIN_LEAF_KINDS = {'loss': 'loss', 'grad_x': 'grad_x', 'grad_meta_tokens': 'grad_w', 'grad_norm_mix': 'grad_w', 'grad_norm_ffn': 'grad_w', 'grad_norm_final': 'grad_w', 'grad_ev_w_in': 'grad_w', 'grad_ev_conv_a': 'grad_w', 'grad_ev_ln_a_g': 'grad_w', 'grad_ev_ln_a_b': 'grad_w', 'grad_ev_conv_b': 'grad_w', 'grad_ev_w_out': 'grad_w', 'grad_od_w_in': 'grad_w', 'grad_od_sinks': 'grad_w', 'grad_od_mu': 'grad_w', 'grad_od_w0': 'grad_w', 'grad_od_w2': 'grad_w', 'grad_od_a0': 'grad_w', 'grad_od_a2': 'grad_w', 'grad_od_g2': 'grad_w', 'grad_od_k_k': 'grad_w', 'grad_od_k_a': 'grad_w', 'grad_od_r_k': 'grad_w', 'grad_od_lnx_g': 'grad_w', 'grad_od_lnx_b': 'grad_w', 'grad_od_w_out': 'grad_w', 'grad_ff_w_up': 'grad_w', 'grad_ff_conv': 'grad_w', 'grad_ff_conv_b': 'grad_w', 'grad_ff_w_down': 'grad_w', 'delta_meta_tokens': 'delta_w', 'delta_norm_mix': 'delta_w', 'delta_norm_ffn': 'delta_w', 'delta_norm_final': 'delta_w', 'delta_ev_w_in': 'delta_w', 'delta_ev_conv_a': 'delta_w', 'delta_ev_ln_a_g': 'delta_w', 'delta_ev_ln_a_b': 'delta_w', 'delta_ev_conv_b': 'delta_w', 'delta_ev_w_out': 'delta_w', 'delta_od_w_in': 'delta_w', 'delta_od_sinks': 'delta_w', 'delta_od_mu': 'delta_w', 'delta_od_w0': 'delta_w', 'delta_od_w2': 'delta_w', 'delta_od_a0': 'delta_w', 'delta_od_a2': 'delta_w', 'delta_od_g2': 'delta_w', 'delta_od_k_k': 'delta_w', 'delta_od_k_a': 'delta_w', 'delta_od_r_k': 'delta_w', 'delta_od_lnx_g': 'delta_w', 'delta_od_lnx_b': 'delta_w', 'delta_od_w_out': 'delta_w', 'delta_ff_w_up': 'delta_w', 'delta_ff_conv': 'delta_w', 'delta_ff_conv_b': 'delta_w', 'delta_ff_w_down': 'delta_w', 'new_m_meta_tokens': 'new_m', 'new_m_norm_mix': 'new_m', 'new_m_norm_ffn': 'new_m', 'new_m_norm_final': 'new_m', 'new_m_ev_w_in': 'new_m', 'new_m_ev_conv_a': 'new_m', 'new_m_ev_ln_a_g': 'new_m', 'new_m_ev_ln_a_b': 'new_m', 'new_m_ev_conv_b': 'new_m', 'new_m_ev_w_out': 'new_m', 'new_m_od_w_in': 'new_m', 'new_m_od_sinks': 'new_m', 'new_m_od_mu': 'new_m', 'new_m_od_w0': 'new_m', 'new_m_od_w2': 'new_m', 'new_m_od_a0': 'new_m', 'new_m_od_a2': 'new_m', 'new_m_od_g2': 'new_m', 'new_m_od_k_k': 'new_m', 'new_m_od_k_a': 'new_m', 'new_m_od_r_k': 'new_m', 'new_m_od_lnx_g': 'new_m', 'new_m_od_lnx_b': 'new_m', 'new_m_od_w_out': 'new_m', 'new_m_ff_w_up': 'new_m', 'new_m_ff_conv': 'new_m', 'new_m_ff_conv_b': 'new_m', 'new_m_ff_w_down': 'new_m', 'new_v_meta_tokens': 'new_v', 'new_v_norm_mix': 'new_v', 'new_v_norm_ffn': 'new_v', 'new_v_norm_final': 'new_v', 'new_v_ev_w_in': 'new_v', 'new_v_ev_conv_a': 'new_v', 'new_v_ev_ln_a_g': 'new_v', 'new_v_ev_ln_a_b': 'new_v', 'new_v_ev_conv_b': 'new_v', 'new_v_ev_w_out': 'new_v', 'new_v_od_w_in': 'new_v', 'new_v_od_sinks': 'new_v', 'new_v_od_mu': 'new_v', 'new_v_od_w0': 'new_v', 'new_v_od_w2': 'new_v', 'new_v_od_a0': 'new_v', 'new_v_od_a2': 'new_v', 'new_v_od_g2': 'new_v', 'new_v_od_k_k': 'new_v', 'new_v_od_k_a': 'new_v', 'new_v_od_r_k': 'new_v', 'new_v_od_lnx_g': 'new_v', 'new_v_od_lnx_b': 'new_v', 'new_v_od_w_out': 'new_v', 'new_v_ff_w_up': 'new_v', 'new_v_ff_conv': 'new_v', 'new_v_ff_conv_b': 'new_v', 'new_v_ff_w_down': 'new_v'}


def _forward(args):
    return _fwd_reference(*[args[k] for k in FWD_PARAMS])


def _output_shape():
    out = _jax.eval_shape(lambda: _forward(_fwd_setup_inputs(0)))
    return out.shape, out.dtype

N_MICROBATCH = 1
ADAM_LR = 0.001
ADAM_B1 = 0.9
ADAM_B2 = 0.999
ADAM_EPS = 1e-08
ADAM_WD = 0.01
ADAM_STEP = 10
PER_EXAMPLE_BATCH_AXIS = {'x': 0, 'loss_target': 0}
SHARED_INPUTS = []
_WEIGHT_DTYPES = {'meta_tokens': _jnp.float32, 'norm_mix': _jnp.float32, 'norm_ffn': _jnp.float32, 'norm_final': _jnp.float32, 'ev_w_in': _jnp.float32, 'ev_conv_a': _jnp.float32, 'ev_ln_a_g': _jnp.float32, 'ev_ln_a_b': _jnp.float32, 'ev_conv_b': _jnp.float32, 'ev_w_out': _jnp.float32, 'od_w_in': _jnp.float32, 'od_sinks': _jnp.float32, 'od_mu': _jnp.float32, 'od_w0': _jnp.float32, 'od_w2': _jnp.float32, 'od_a0': _jnp.float32, 'od_a2': _jnp.float32, 'od_g2': _jnp.float32, 'od_k_k': _jnp.float32, 'od_k_a': _jnp.float32, 'od_r_k': _jnp.float32, 'od_lnx_g': _jnp.float32, 'od_lnx_b': _jnp.float32, 'od_w_out': _jnp.float32, 'ff_w_up': _jnp.float32, 'ff_conv': _jnp.float32, 'ff_conv_b': _jnp.float32, 'ff_w_down': _jnp.float32}
MOMENT_SCALE = {'meta_tokens': 6.987653e-03, 'norm_mix': 1.274188e-01, 'norm_ffn': 8.058007e-02, 'norm_final': 1.599969e+01, 'ev_w_in': 1.030468e-01, 'ev_conv_a': 7.674258e-02, 'ev_ln_a_g': 8.989143e-02, 'ev_ln_a_b': 8.636350e-02, 'ev_conv_b': 1.225057e-01, 'ev_w_out': 1.028611e-01, 'od_w_in': 4.331485e-02, 'od_sinks': 1.488315e-03, 'od_mu': 8.395579e-02, 'od_w0': 1.924947e-02, 'od_w2': 2.853426e-03, 'od_a0': 1.917158e-02, 'od_a2': 1.825882e-02, 'od_g2': 5.318744e-02, 'od_k_k': 5.503805e-02, 'od_k_a': 5.287721e-02, 'od_r_k': 1.232454e-01, 'od_lnx_g': 5.704682e-02, 'od_lnx_b': 5.377463e-02, 'od_w_out': 3.745293e-02, 'ff_w_up': 3.381125e-02, 'ff_conv': 3.463263e-02, 'ff_conv_b': 3.289040e-02, 'ff_w_down': 5.523010e-02}


def _to_microbatches(a, axis):
    t = _jnp.moveaxis(a, axis, 0)
    t = t.reshape((N_MICROBATCH, t.shape[0] // N_MICROBATCH) + t.shape[1:])
    return _jnp.moveaxis(t, 1, axis + 1)


def setup_inputs(seed: int = 0) -> dict:
    inp = _fwd_setup_inputs(seed)
    key = _jax.random.fold_in(_jax.random.key(seed), 7919)
    shape, _ = _output_shape()
    out = dict(inp)
    out["loss_target"] = _jax.random.normal(_jax.random.fold_in(key, 0), shape, _jnp.float32)
    for i, name in enumerate(TWIN_WEIGHTS):
        w = inp[name].astype(_jnp.float32)
        if MOMENT_SCALE is None:
            s = _jnp.sqrt(_jnp.mean(_jnp.square(w)) + 1e-30)
        else:
            s = MOMENT_SCALE[name]
        km, kv = _jax.random.split(_jax.random.fold_in(key, i + 1))
        out[name] = w
        out["m_" + name] = s * _jax.random.normal(km, w.shape, _jnp.float32)
        out["v_" + name] = (s * s) * _jax.random.uniform(kv, w.shape, _jnp.float32, 0.5, 1.5)
    if N_MICROBATCH > 1:
        for name, axis in PER_EXAMPLE_BATCH_AXIS.items():
            out[name] = _to_microbatches(out[name], axis)
    return {'x': out['x'], 'meta_tokens': out['meta_tokens'], 'norm_mix': out['norm_mix'], 'norm_ffn': out['norm_ffn'], 'norm_final': out['norm_final'], 'ev_w_in': out['ev_w_in'], 'ev_conv_a': out['ev_conv_a'], 'ev_ln_a_g': out['ev_ln_a_g'], 'ev_ln_a_b': out['ev_ln_a_b'], 'ev_conv_b': out['ev_conv_b'], 'ev_w_out': out['ev_w_out'], 'od_w_in': out['od_w_in'], 'od_sinks': out['od_sinks'], 'od_mu': out['od_mu'], 'od_w0': out['od_w0'], 'od_w2': out['od_w2'], 'od_a0': out['od_a0'], 'od_a2': out['od_a2'], 'od_g2': out['od_g2'], 'od_k_k': out['od_k_k'], 'od_k_a': out['od_k_a'], 'od_r_k': out['od_r_k'], 'od_lnx_g': out['od_lnx_g'], 'od_lnx_b': out['od_lnx_b'], 'od_w_out': out['od_w_out'], 'ff_w_up': out['ff_w_up'], 'ff_conv': out['ff_conv'], 'ff_conv_b': out['ff_conv_b'], 'ff_w_down': out['ff_w_down'], 'loss_target': out['loss_target'], 'm_meta_tokens': out['m_meta_tokens'], 'm_norm_mix': out['m_norm_mix'], 'm_norm_ffn': out['m_norm_ffn'], 'm_norm_final': out['m_norm_final'], 'm_ev_w_in': out['m_ev_w_in'], 'm_ev_conv_a': out['m_ev_conv_a'], 'm_ev_ln_a_g': out['m_ev_ln_a_g'], 'm_ev_ln_a_b': out['m_ev_ln_a_b'], 'm_ev_conv_b': out['m_ev_conv_b'], 'm_ev_w_out': out['m_ev_w_out'], 'm_od_w_in': out['m_od_w_in'], 'm_od_sinks': out['m_od_sinks'], 'm_od_mu': out['m_od_mu'], 'm_od_w0': out['m_od_w0'], 'm_od_w2': out['m_od_w2'], 'm_od_a0': out['m_od_a0'], 'm_od_a2': out['m_od_a2'], 'm_od_g2': out['m_od_g2'], 'm_od_k_k': out['m_od_k_k'], 'm_od_k_a': out['m_od_k_a'], 'm_od_r_k': out['m_od_r_k'], 'm_od_lnx_g': out['m_od_lnx_g'], 'm_od_lnx_b': out['m_od_lnx_b'], 'm_od_w_out': out['m_od_w_out'], 'm_ff_w_up': out['m_ff_w_up'], 'm_ff_conv': out['m_ff_conv'], 'm_ff_conv_b': out['m_ff_conv_b'], 'm_ff_w_down': out['m_ff_w_down'], 'v_meta_tokens': out['v_meta_tokens'], 'v_norm_mix': out['v_norm_mix'], 'v_norm_ffn': out['v_norm_ffn'], 'v_norm_final': out['v_norm_final'], 'v_ev_w_in': out['v_ev_w_in'], 'v_ev_conv_a': out['v_ev_conv_a'], 'v_ev_ln_a_g': out['v_ev_ln_a_g'], 'v_ev_ln_a_b': out['v_ev_ln_a_b'], 'v_ev_conv_b': out['v_ev_conv_b'], 'v_ev_w_out': out['v_ev_w_out'], 'v_od_w_in': out['v_od_w_in'], 'v_od_sinks': out['v_od_sinks'], 'v_od_mu': out['v_od_mu'], 'v_od_w0': out['v_od_w0'], 'v_od_w2': out['v_od_w2'], 'v_od_a0': out['v_od_a0'], 'v_od_a2': out['v_od_a2'], 'v_od_g2': out['v_od_g2'], 'v_od_k_k': out['v_od_k_k'], 'v_od_k_a': out['v_od_k_a'], 'v_od_r_k': out['v_od_r_k'], 'v_od_lnx_g': out['v_od_lnx_g'], 'v_od_lnx_b': out['v_od_lnx_b'], 'v_od_w_out': out['v_od_w_out'], 'v_ff_w_up': out['v_ff_w_up'], 'v_ff_conv': out['v_ff_conv'], 'v_ff_conv_b': out['v_ff_conv_b'], 'v_ff_w_down': out['v_ff_w_down']}


def _loss(weights, diff, rest, loss_target):
    with _jax.named_scope("forward"):
        args = {**rest, TWIN_DIFF_INPUT: diff, **{k: w.astype(_WEIGHT_DTYPES[k]) for k, w in weights.items()}}
        y = _forward(args)
    with _jax.named_scope("loss_head"):
        err = _jnp.square(y.astype(_jnp.float32) - loss_target)
        return 0.5 * _jnp.sum(_jnp.mean(err, axis=-1)) if err.ndim else 0.5 * err


def _adamw(w, g, m, v):
    m = ADAM_B1 * m + (1.0 - ADAM_B1) * g
    v = ADAM_B2 * v + (1.0 - ADAM_B2) * _jnp.square(g)
    m_hat = m / (1.0 - ADAM_B1 ** ADAM_STEP)
    v_hat = v / (1.0 - ADAM_B2 ** ADAM_STEP)
    delta = -ADAM_LR * (m_hat / (_jnp.sqrt(v_hat) + ADAM_EPS) + ADAM_WD * w)
    return delta, m, v


def reference(x, meta_tokens, norm_mix, norm_ffn, norm_final, ev_w_in, ev_conv_a, ev_ln_a_g, ev_ln_a_b, ev_conv_b, ev_w_out, od_w_in, od_sinks, od_mu, od_w0, od_w2, od_a0, od_a2, od_g2, od_k_k, od_k_a, od_r_k, od_lnx_g, od_lnx_b, od_w_out, ff_w_up, ff_conv, ff_conv_b, ff_w_down, loss_target, m_meta_tokens, m_norm_mix, m_norm_ffn, m_norm_final, m_ev_w_in, m_ev_conv_a, m_ev_ln_a_g, m_ev_ln_a_b, m_ev_conv_b, m_ev_w_out, m_od_w_in, m_od_sinks, m_od_mu, m_od_w0, m_od_w2, m_od_a0, m_od_a2, m_od_g2, m_od_k_k, m_od_k_a, m_od_r_k, m_od_lnx_g, m_od_lnx_b, m_od_w_out, m_ff_w_up, m_ff_conv, m_ff_conv_b, m_ff_w_down, v_meta_tokens, v_norm_mix, v_norm_ffn, v_norm_final, v_ev_w_in, v_ev_conv_a, v_ev_ln_a_g, v_ev_ln_a_b, v_ev_conv_b, v_ev_w_out, v_od_w_in, v_od_sinks, v_od_mu, v_od_w0, v_od_w2, v_od_a0, v_od_a2, v_od_g2, v_od_k_k, v_od_k_a, v_od_r_k, v_od_lnx_g, v_od_lnx_b, v_od_w_out, v_ff_w_up, v_ff_conv, v_ff_conv_b, v_ff_w_down):
    given = dict(x=x, meta_tokens=meta_tokens, norm_mix=norm_mix, norm_ffn=norm_ffn, norm_final=norm_final, ev_w_in=ev_w_in, ev_conv_a=ev_conv_a, ev_ln_a_g=ev_ln_a_g, ev_ln_a_b=ev_ln_a_b, ev_conv_b=ev_conv_b, ev_w_out=ev_w_out, od_w_in=od_w_in, od_sinks=od_sinks, od_mu=od_mu, od_w0=od_w0, od_w2=od_w2, od_a0=od_a0, od_a2=od_a2, od_g2=od_g2, od_k_k=od_k_k, od_k_a=od_k_a, od_r_k=od_r_k, od_lnx_g=od_lnx_g, od_lnx_b=od_lnx_b, od_w_out=od_w_out, ff_w_up=ff_w_up, ff_conv=ff_conv, ff_conv_b=ff_conv_b, ff_w_down=ff_w_down, loss_target=loss_target, m_meta_tokens=m_meta_tokens, m_norm_mix=m_norm_mix, m_norm_ffn=m_norm_ffn, m_norm_final=m_norm_final, m_ev_w_in=m_ev_w_in, m_ev_conv_a=m_ev_conv_a, m_ev_ln_a_g=m_ev_ln_a_g, m_ev_ln_a_b=m_ev_ln_a_b, m_ev_conv_b=m_ev_conv_b, m_ev_w_out=m_ev_w_out, m_od_w_in=m_od_w_in, m_od_sinks=m_od_sinks, m_od_mu=m_od_mu, m_od_w0=m_od_w0, m_od_w2=m_od_w2, m_od_a0=m_od_a0, m_od_a2=m_od_a2, m_od_g2=m_od_g2, m_od_k_k=m_od_k_k, m_od_k_a=m_od_k_a, m_od_r_k=m_od_r_k, m_od_lnx_g=m_od_lnx_g, m_od_lnx_b=m_od_lnx_b, m_od_w_out=m_od_w_out, m_ff_w_up=m_ff_w_up, m_ff_conv=m_ff_conv, m_ff_conv_b=m_ff_conv_b, m_ff_w_down=m_ff_w_down, v_meta_tokens=v_meta_tokens, v_norm_mix=v_norm_mix, v_norm_ffn=v_norm_ffn, v_norm_final=v_norm_final, v_ev_w_in=v_ev_w_in, v_ev_conv_a=v_ev_conv_a, v_ev_ln_a_g=v_ev_ln_a_g, v_ev_ln_a_b=v_ev_ln_a_b, v_ev_conv_b=v_ev_conv_b, v_ev_w_out=v_ev_w_out, v_od_w_in=v_od_w_in, v_od_sinks=v_od_sinks, v_od_mu=v_od_mu, v_od_w0=v_od_w0, v_od_w2=v_od_w2, v_od_a0=v_od_a0, v_od_a2=v_od_a2, v_od_g2=v_od_g2, v_od_k_k=v_od_k_k, v_od_k_a=v_od_k_a, v_od_r_k=v_od_r_k, v_od_lnx_g=v_od_lnx_g, v_od_lnx_b=v_od_lnx_b, v_od_w_out=v_od_w_out, v_ff_w_up=v_ff_w_up, v_ff_conv=v_ff_conv, v_ff_conv_b=v_ff_conv_b, v_ff_w_down=v_ff_w_down)
    weights = {n: given[n] for n in TWIN_WEIGHTS}
    shared = {n: given[n] for n in SHARED_INPUTS}
    per_example = {n: given[n] for n in ['x']}
    grad_fn = _jax.value_and_grad(_loss, argnums=(0, 1))

    def one_microbatch(ex, loss_target):
        ex = dict(ex)
        diff = ex.pop(TWIN_DIFF_INPUT)
        return grad_fn(weights, diff, {**shared, **ex}, loss_target)

    if N_MICROBATCH == 1:
        loss, (grad_w, grad_x) = one_microbatch(per_example, given["loss_target"])
    else:
        def body(carry, xs):
            loss_sum, grad_sum = carry
            l_k, (gw_k, gx_k) = one_microbatch(xs[0], xs[1])
            with _jax.named_scope("update"):
                return (loss_sum + l_k, _jax.tree.map(_jnp.add, grad_sum, gw_k)), gx_k

        init = (_jnp.zeros((), _jnp.float32), _jax.tree.map(_jnp.zeros_like, weights))
        (loss, grad_w), grad_x = _jax.lax.scan(body, init, (per_example, given["loss_target"]))
    with _jax.named_scope("update"):
        delta_w, new_m, new_v = {}, {}, {}
        for n in TWIN_WEIGHTS:
            delta_w[n], new_m[n], new_v[n] = _adamw(weights[n], grad_w[n], given["m_" + n], given["v_" + n])
    return (loss, grad_x, *[grad_w[n] for n in TWIN_WEIGHTS], *[delta_w[n] for n in TWIN_WEIGHTS],
            *[new_m[n] for n in TWIN_WEIGHTS], *[new_v[n] for n in TWIN_WEIGHTS])
```

```python
import functools

import jax
import jax.numpy as jnp
from jax import lax
from jax.experimental import pallas as pl
from jax.experimental.pallas import tpu as pltpu

F32 = jnp.float32
BF16 = jnp.bfloat16
HI = lax.Precision.HIGHEST
MXU_DTYPE = BF16

D_MODEL = 1024
N_META = 16
RMS_EPS = 1e-6
LN_EPS = 1e-5
D_A = 512
CONV_A_WIDTH = 31
CONV_B_WIDTH = 3
HEAD_DIM = 64
N_Q_HEADS = 8
N_KV_HEADS = 2
GQA_GROUP = 4
D_ATT = 512
D_KV = 128
BLOCK = 128
ROPE_THETA = 10000.0
D_R = 512
N_R_HEADS = 8
LORA_W = 64
LORA_A = 64
LORA_G = 128
RWKV_GN_EPS = 64e-5
RWKV_COLS = 3 * D_R + LORA_W + LORA_A + LORA_G
D_FF = 2816
NEG_INF = -1e30
ADAM_LR = 0.001
ADAM_B1 = 0.9
ADAM_B2 = 0.999
ADAM_EPS = 1e-08
ADAM_WD = 0.01
ADAM_STEP = 10

N_CHIPS = 4
LANES = 128
CONV_PAD = 32
VMEM_LIMIT_V7X = 56 * 1024 * 1024
MESH = pl.DeviceIdType.MESH


def _cparams(sem=None):
    return pltpu.CompilerParams(dimension_semantics=sem, vmem_limit_bytes=VMEM_LIMIT_V7X)


def _row_tile(t, cap):
    for d in range(min(t, cap), 0, -1):
        if t % d == 0 and d % 16 == 0:
            return d
    return t


def _chunk_len(t):
    for d in (64, 48, 32, 16, 8):
        if t % d == 0:
            return d
    raise ValueError(t)


def _call(fn, name, grid, ins, outs, acc_axis=None, sem=None):
    n_in, n_out = len(ins), len(outs)

    def body(*refs):
        vals = fn(*[r[...] for r in refs[:n_in]])
        if not isinstance(vals, (tuple, list)):
            vals = (vals,)
        for r, v, o in zip(refs[n_in:n_in + n_out], vals, outs):
            if o[3]:
                first = pl.program_id(acc_axis) == 0

                @pl.when(first)
                def _(r=r, v=v):
                    r[...] = v

                @pl.when(jnp.logical_not(first))
                def _(r=r, v=v):
                    r[...] += v
            else:
                r[...] = v

    res = pl.pallas_call(
        body, name=name, grid=grid,
        in_specs=[pl.BlockSpec(b, m) for _, b, m in ins],
        out_specs=[pl.BlockSpec(o[1], o[2]) for o in outs],
        out_shape=[jax.ShapeDtypeStruct(o[0], F32) for o in outs],
        compiler_params=_cparams(sem),
    )(*[a for a, _, _ in ins])
    return res if n_out > 1 else res[0]


def _matmul(name, a, b, *, dims, grid, a_spec, b_spec, o_shape, o_spec, acc_shape, nk, k_axis,
            add=None, add_spec=None):
    def body(*refs):
        if add is None:
            a_ref, b_ref, o_ref, acc = refs
        else:
            a_ref, b_ref, add_ref, o_ref, acc = refs
        k = pl.program_id(k_axis)

        @pl.when(k == 0)
        def _():
            if add is None:
                acc[...] = jnp.zeros(acc.shape, F32)
            else:
                acc[...] = add_ref[...]

        acc[...] += lax.dot_general(a_ref[...].astype(MXU_DTYPE), b_ref[...].astype(MXU_DTYPE), dims,
                                    preferred_element_type=F32)

        @pl.when(k == nk - 1)
        def _():
            o_ref[...] = acc[...]

    args = [a, b] + ([] if add is None else [add])
    specs = [a_spec, b_spec] + ([] if add is None else [add_spec])
    return pl.pallas_call(
        body, name=name, grid=grid, in_specs=specs, out_specs=o_spec,
        out_shape=jax.ShapeDtypeStruct(o_shape, F32),
        scratch_shapes=[pltpu.VMEM(acc_shape, F32)],
        compiler_params=_cparams(None),
    )(*args)


_NN = (((1,), (0,)), ((), ()))
_NT = (((1,), (1,)), ((), ()))
_TN = (((0,), (0,)), ((), ()))


def _mm_cs(name, x, wg, tm):
    t, k = x.shape
    s, _, n = wg.shape
    return _matmul(name, x, wg, dims=_NN, grid=(s, t // tm, 1),
                   a_spec=pl.BlockSpec((tm, k), lambda j, i, kk: (i, 0)),
                   b_spec=pl.BlockSpec((None, k, n), lambda j, i, kk: (j, 0, 0)),
                   o_shape=(t, s * n), o_spec=pl.BlockSpec((tm, n), lambda j, i, kk: (i, j)),
                   acc_shape=(tm, n), nk=1, k_axis=2)


def _mm_full(name, x, w, tm, tk, add=None):
    t, k = x.shape
    n = w.shape[1]
    nk = k // tk
    return _matmul(name, x, w, dims=_NN, grid=(t // tm, 1, nk),
                   a_spec=pl.BlockSpec((tm, tk), lambda i, j, kk: (i, kk)),
                   b_spec=pl.BlockSpec((tk, n), lambda i, j, kk: (kk, 0)),
                   o_shape=(t, n), o_spec=pl.BlockSpec((tm, n), lambda i, j, kk: (i, 0)),
                   acc_shape=(tm, n), nk=nk, k_axis=2,
                   add=add, add_spec=pl.BlockSpec((tm, n), lambda i, j, kk: (i, 0)))


def _mm_nt_cs(name, dy, wg, tm, add=None):
    t = dy.shape[0]
    s, k, n = wg.shape
    return _matmul(name, dy, wg, dims=_NT, grid=(t // tm, 1, s),
                   a_spec=pl.BlockSpec((tm, n), lambda i, j, kk: (i, kk)),
                   b_spec=pl.BlockSpec((None, k, n), lambda i, j, kk: (kk, 0, 0)),
                   o_shape=(t, k), o_spec=pl.BlockSpec((tm, k), lambda i, j, kk: (i, 0)),
                   acc_shape=(tm, k), nk=s, k_axis=2,
                   add=add, add_spec=pl.BlockSpec((tm, k), lambda i, j, kk: (i, 0)))


def _mm_nt_full(name, dy, w, tm, tko):
    t, n = dy.shape
    k = w.shape[0]
    return _matmul(name, dy, w, dims=_NT, grid=(t // tm, k // tko, 1),
                   a_spec=pl.BlockSpec((tm, n), lambda i, j, kk: (i, 0)),
                   b_spec=pl.BlockSpec((tko, n), lambda i, j, kk: (j, 0)),
                   o_shape=(t, k), o_spec=pl.BlockSpec((tm, tko), lambda i, j, kk: (i, j)),
                   acc_shape=(tm, tko), nk=1, k_axis=2)


def _mm_tn_cs(name, x, dy, s, tk):
    t, k = x.shape
    n = dy.shape[1] // s
    nk = t // tk
    return _matmul(name, x, dy, dims=_TN, grid=(s, 1, nk),
                   a_spec=pl.BlockSpec((tk, k), lambda j, i, kk: (kk, 0)),
                   b_spec=pl.BlockSpec((tk, n), lambda j, i, kk: (kk, j)),
                   o_shape=(s, k, n), o_spec=pl.BlockSpec((None, k, n), lambda j, i, kk: (j, 0, 0)),
                   acc_shape=(k, n), nk=nk, k_axis=2)


def _mm_tn_full(name, y, dh, tk, tko):
    t, k = y.shape
    n = dh.shape[1]
    nk = t // tk
    return _matmul(name, y, dh, dims=_TN, grid=(k // tko, 1, nk),
                   a_spec=pl.BlockSpec((tk, tko), lambda j, i, kk: (kk, j)),
                   b_spec=pl.BlockSpec((tk, n), lambda j, i, kk: (kk, 0)),
                   o_shape=(k, n), o_spec=pl.BlockSpec((tko, n), lambda j, i, kk: (j, 0)),
                   acc_shape=(tko, n), nk=nk, k_axis=2)


def _sigmoid(x):
    return 1.0 / (1.0 + jnp.exp(-x))


def _rms_fwd(name, h, g, tr):
    t, d = h.shape

    def fn(hv, gv):
        r = lax.rsqrt(jnp.mean(hv * hv, axis=-1, keepdims=True) + RMS_EPS)
        return hv * r * gv

    return _call(fn, name, (t // tr,), [(h, (tr, d), lambda i: (i, 0)), (g, (1, d), lambda i: (0, 0))],
                 [((t, d), (tr, d), lambda i: (i, 0), False)])


def _rms_bwd(name, h, g, dhn, dh, tr):
    t, d = h.shape

    def fn(hv, gv, dy, dh_in):
        r = lax.rsqrt(jnp.mean(hv * hv, axis=-1, keepdims=True) + RMS_EPS)
        xh = hv * r
        dg = jnp.sum(dy * xh, axis=0, keepdims=True)
        dxh = dy * gv
        dx = r * (dxh - xh * jnp.mean(dxh * xh, axis=-1, keepdims=True))
        return dh_in + dx, dg

    row = lambda i: (i, 0)
    return _call(fn, name, (t // tr,),
                 [(h, (tr, d), row), (g, (1, d), lambda i: (0, 0)), (dhn, (tr, d), row), (dh, (tr, d), row)],
                 [((t, d), (tr, d), row, False), ((1, d), (1, d), lambda i: (0, 0), True)], acc_axis=0)


def _final_loss(name, h, g, tgt, tr):
    t, d = h.shape

    def fn(hv, gv, tv):
        r = lax.rsqrt(jnp.mean(hv * hv, axis=-1, keepdims=True) + RMS_EPS)
        xh = hv * r
        row = pl.program_id(0) * tr + lax.broadcasted_iota(jnp.int32, (tr, 1), 0)
        e = jnp.where(row >= N_META, xh * gv - tv, 0.0)
        loss = jnp.broadcast_to(0.5 * jnp.sum(jnp.sum(e * e, axis=-1, keepdims=True), axis=0, keepdims=True) / d,
                                (8, LANES))
        dout = e / d
        dg = jnp.sum(dout * xh, axis=0, keepdims=True)
        dxh = dout * gv
        dx = r * (dxh - xh * jnp.mean(dxh * xh, axis=-1, keepdims=True))
        return loss, dx, dg

    row = lambda i: (i, 0)
    fix = lambda i: (0, 0)
    return _call(fn, name, (t // tr,), [(h, (tr, d), row), (g, (1, d), fix), (tgt, (tr, d), row)],
                 [((8, LANES), (8, LANES), fix, True), ((t, d), (tr, d), row, False), ((1, d), (1, d), fix, True)],
                 acc_axis=0)


def _silu_ln(uc, g, b):
    mu = jnp.mean(uc, axis=-1, keepdims=True)
    xc = uc - mu
    rs = lax.rsqrt(jnp.mean(xc * xc, axis=-1, keepdims=True) + LN_EPS)
    ln = xc * rs * g + b
    return ln * _sigmoid(ln)


def _even_ln_fwd(name, uc, g, b, tr):
    t, d = uc.shape
    row, fix = (lambda i: (i, 0)), (lambda i: (0, 0))
    return _call(_silu_ln, name, (t // tr,), [(uc, (tr, d), row), (g, (1, d), fix), (b, (1, d), fix)],
                 [((t, d), (tr, d), row, False)])


def _even_ln_bwd(name, uc, g, b, dy, dy_col, tr):
    t, d = uc.shape

    def fn(ucv, gv, bv, dyv):
        mu = jnp.mean(ucv, axis=-1, keepdims=True)
        xc = ucv - mu
        rs = lax.rsqrt(jnp.mean(xc * xc, axis=-1, keepdims=True) + LN_EPS)
        xh = xc * rs
        ln = xh * gv + bv
        s = _sigmoid(ln)
        dln = dyv * (s * (1.0 + ln * (1.0 - s)))
        dg = jnp.sum(dln * xh, axis=0, keepdims=True)
        db = jnp.sum(dln, axis=0, keepdims=True)
        dxh = dln * gv
        duc = rs * (dxh - jnp.mean(dxh, axis=-1, keepdims=True) - xh * jnp.mean(dxh * xh, axis=-1, keepdims=True))
        return duc, dg, db

    row, fix = (lambda i: (i, 0)), (lambda i: (0, 0))
    return _call(fn, name, (t // tr,),
                 [(uc, (tr, d), row), (g, (1, d), fix), (b, (1, d), fix), (dy, (tr, d), lambda i: (i, dy_col))],
                 [((t, d), (tr, d), row, False), ((1, d), (1, d), fix, True), ((1, d), (1, d), fix, True)], acc_axis=0)


def _conv_fwd(xp, w_ref, width, t):
    acc = None
    for j in range(width):
        term = xp[pl.ds(CONV_PAD - (width - 1) + j, t), :] * w_ref[pl.ds(j, 1), :]
        acc = term if acc is None else acc + term
    return acc


def _conv_bwd_in(dyp, w_ref, width, t):
    acc = None
    for j in range(width):
        term = dyp[pl.ds(width - 1 - j, t), :] * w_ref[pl.ds(j, 1), :]
        acc = term if acc is None else acc + term
    return acc


def _conv_bwd_w(dy, xp, dw_ref, width, t):
    for j in range(width):
        dw_ref[pl.ds(j, 1), :] = jnp.sum(dy * xp[pl.ds(CONV_PAD - (width - 1) + j, t), :], axis=0, keepdims=True)


def _store_front(xp, x, t):
    xp[pl.ds(0, CONV_PAD), :] = jnp.zeros((CONV_PAD, LANES), F32)
    xp[pl.ds(CONV_PAD, t), :] = x


def _store_back(xp, x, t):
    xp[pl.ds(0, t), :] = x
    xp[pl.ds(t, CONV_PAD), :] = jnp.zeros((CONV_PAD, LANES), F32)


def _col_call(body, name, ncol, ins, outs, t, n_scratch):
    def spec(rows, off):
        return pl.BlockSpec((rows, LANES), lambda j, off=off: (0, j + off))

    res = pl.pallas_call(
        body, name=name, grid=(ncol,),
        in_specs=[spec(r, off) for _, r, off in ins],
        out_specs=[spec(r, 0) for r, _ in outs],
        out_shape=[jax.ShapeDtypeStruct((r, c), F32) for r, c in outs],
        scratch_shapes=[pltpu.VMEM((t + CONV_PAD, LANES), F32) for _ in range(n_scratch)],
        compiler_params=_cparams(None),
    )(*[a for a, _, _ in ins])
    return res


def _even_col_fwd(name, p, conv_a, conv_b):
    t = p.shape[0]
    nc = D_A // LANES

    def body(av, ag, gb, gc, xi, ca, cb, uc_ref, yb_ref, xp):
        _store_front(xp, av[...] * _sigmoid(ag[...]), t)
        uc_ref[...] = _conv_fwd(xp, ca, CONV_A_WIDTH, t)
        _store_front(xp, gc[...] * xi[...], t)
        yb_ref[...] = gb[...] * _conv_fwd(xp, cb, CONV_B_WIDTH, t)

    ins = [(p, t, k * nc) for k in range(5)] + [(conv_a, CONV_A_WIDTH, 0), (conv_b, CONV_B_WIDTH, 0)]
    return _col_call(body, name, nc, ins, [(t, D_A), (t, D_A)], t, 1)


def _even_col_bwd(name, p, duc, dy, conv_a, conv_b):
    t = p.shape[0]
    nc = D_A // LANES

    def body(av, ag, gb, gc, xi, duc_ref, dyb_ref, ca, cb, dav, dag, dgb, dgc, dxi, dca, dcb, xp, dyp):
        sig = _sigmoid(ag[...])
        _store_front(xp, av[...] * sig, t)
        _store_back(dyp, duc_ref[...], t)
        _conv_bwd_w(duc_ref[...], xp, dca, CONV_A_WIDTH, t)
        du = _conv_bwd_in(dyp, ca, CONV_A_WIDTH, t)
        dav[...] = du * sig
        dag[...] = du * av[...] * sig * (1.0 - sig)
        _store_front(xp, gc[...] * xi[...], t)
        zc = _conv_fwd(xp, cb, CONV_B_WIDTH, t)
        dgb[...] = dyb_ref[...] * zc
        dzc = dyb_ref[...] * gb[...]
        _conv_bwd_w(dzc, xp, dcb, CONV_B_WIDTH, t)
        _store_back(dyp, dzc, t)
        dz = _conv_bwd_in(dyp, cb, CONV_B_WIDTH, t)
        dgc[...] = dz * xi[...]
        dxi[...] = dz * gc[...]

    ins = ([(p, t, k * nc) for k in range(5)] + [(duc, t, 0), (dy, t, nc)]
           + [(conv_a, CONV_A_WIDTH, 0), (conv_b, CONV_B_WIDTH, 0)])
    outs = [(t, D_A)] * 5 + [(CONV_A_WIDTH, D_A), (CONV_B_WIDTH, D_A)]
    return _col_call(body, name, nc, ins, outs, t, 2)


def _ffn_col_fwd(name, gate, val, conv, bias):
    t, dff = gate.shape

    def body(g_ref, v_ref, cw, b_ref, a_ref, xp):
        _store_front(xp, g_ref[...], t)
        gc = _conv_fwd(xp, cw, CONV_B_WIDTH, t) + b_ref[...]
        a_ref[...] = gc * _sigmoid(gc) * v_ref[...]

    ins = [(gate, t, 0), (val, t, 0), (conv, CONV_B_WIDTH, 0), (bias, 1, 0)]
    return _col_call(body, name, dff // LANES, ins, [(t, dff)], t, 1)[0]


def _ffn_col_bwd(name, gate, val, da, conv, bias):
    t, dff = gate.shape

    def body(g_ref, v_ref, da_ref, cw, b_ref, dg_ref, dv_ref, dcw, db_ref, xp, dyp):
        _store_front(xp, g_ref[...], t)
        gc = _conv_fwd(xp, cw, CONV_B_WIDTH, t) + b_ref[...]
        s = _sigmoid(gc)
        dv_ref[...] = da_ref[...] * gc * s
        dgc = da_ref[...] * v_ref[...] * (s * (1.0 + gc * (1.0 - s)))
        db_ref[...] = jnp.sum(dgc, axis=0, keepdims=True)
        _conv_bwd_w(dgc, xp, dcw, CONV_B_WIDTH, t)
        _store_back(dyp, dgc, t)
        dg_ref[...] = _conv_bwd_in(dyp, cw, CONV_B_WIDTH, t)

    ins = [(gate, t, 0), (val, t, 0), (da, t, 0), (conv, CONV_B_WIDTH, 0), (bias, 1, 0)]
    outs = [(t, dff), (t, dff), (CONV_B_WIDTH, dff), (1, dff)]
    return _col_call(body, name, dff // LANES, ins, outs, t, 2)


def _shift_fwd(name, p, col0, mu):
    t = p.shape[0]

    def body(x_ref, mu_ref, o_ref, xp):
        _store_front(xp, x_ref[...], t)
        prev = xp[pl.ds(CONV_PAD - 1, t), :]
        o_ref[...] = x_ref[...] + (prev - x_ref[...]) * mu_ref[...]

    return _col_call(body, name, RWKV_COLS // LANES, [(p, t, col0), (mu, 1, 0)], [(t, RWKV_COLS)], t, 1)[0]


def _shift_bwd(name, p, col0, mu, dprs):
    t = p.shape[0]

    def body(x_ref, mu_ref, d_ref, dx_ref, dmu_ref, xp, dyp):
        _store_front(xp, x_ref[...], t)
        prev = xp[pl.ds(CONV_PAD - 1, t), :]
        dmu_ref[...] = jnp.sum(d_ref[...] * (prev - x_ref[...]), axis=0, keepdims=True)
        dm = d_ref[...] * mu_ref[...]
        _store_back(dyp, dm, t)
        dx_ref[...] = d_ref[...] - dm + dyp[pl.ds(1, t), :]

    ins = [(p, t, col0), (mu, 1, 0), (dprs, t, 0)]
    return _col_call(body, name, RWKV_COLS // LANES, ins, [(t, RWKV_COLS), (1, RWKV_COLS)], t, 2)


def _dotm(a, b):
    return jnp.dot(a.astype(MXU_DTYPE), b.astype(MXU_DTYPE), preferred_element_type=F32)


def _doth(a, b, dims=_NN):
    return lax.dot_general(a, b, dims, precision=HI, preferred_element_type=F32)


def _softplus(x):
    return jnp.where(x > 0, x, 0.0) + jnp.log(1.0 + jnp.exp(jnp.where(x > 0, -x, x)))


def _rwkv_pre(k, xl, gd, w0, w2p, a0, a2p, g2, k_k, k_a, seg):
    z = w0 + _dotm(jnp.tanh(xl), w2p)
    lw = -jnp.exp(-_softplus(-z) - 0.5)
    alpha = _sigmoid(a0 + _dotm(xl, a2p))
    g = _dotm(_sigmoid(gd), g2)
    kk = k * k_k
    kk = kk / jnp.maximum(jnp.sqrt(_doth(kk * kk, seg)), 1e-12)
    k2 = k * (1.0 + (alpha - 1.0) * k_a)
    return lw, k2, -kk, kk * alpha, g


def _rwkv_post(y, r, k2, v, g, lnx_g, lnx_b, r_k, seg):
    mean = _doth(y, seg) * (1.0 / HEAD_DIM)
    yc = y - mean
    var = _doth(yc * yc, seg) * (1.0 / HEAD_DIM)
    yo = yc * lax.rsqrt(var + RWKV_GN_EPS) * lnx_g + lnx_b
    bonus = _doth(r * k2 * r_k, seg) * v
    return (yo + bonus) * g


def _rwkv_pre_fwd(name, prs, prm, seg, tr):
    t = prs.shape[0]
    row = lambda i: (i, 0)
    fix = lambda i: (0, 0)
    ins = [(prs, (tr, D_R), lambda i: (i, 1)), (prs, (tr, LANES), lambda i: (i, 12)), (prs, (tr, LANES), lambda i: (i, 13)),
           (prm["w0"], (1, D_R), fix), (prm["w2p"], (LANES, D_R), fix), (prm["a0"], (1, D_R), fix),
           (prm["a2p"], (LANES, D_R), fix), (prm["g2"], (LANES, D_R), fix), (prm["k_k"], (1, D_R), fix),
           (prm["k_a"], (1, D_R), fix), (seg, (D_R, D_R), fix)]
    return _call(_rwkv_pre, name, (t // tr,), ins, [((t, D_R), (tr, D_R), row, False)] * 5)


def _rwkv_pre_bwd(name, prs, prm, seg, cts, tr):
    t = prs.shape[0]

    def fn(k, xl, gd, w0, w2p, a0, a2p, g2, k_k, k_a, segv, *ct):
        _, vjp = jax.vjp(lambda *a: _rwkv_pre(*a, segv), k, xl, gd, w0, w2p, a0, a2p, g2, k_k, k_a)
        return vjp(tuple(ct))

    row = lambda i: (i, 0)
    fix = lambda i: (0, 0)
    ins = [(prs, (tr, D_R), lambda i: (i, 1)), (prs, (tr, LANES), lambda i: (i, 12)), (prs, (tr, LANES), lambda i: (i, 13)),
           (prm["w0"], (1, D_R), fix), (prm["w2p"], (LANES, D_R), fix), (prm["a0"], (1, D_R), fix),
           (prm["a2p"], (LANES, D_R), fix), (prm["g2"], (LANES, D_R), fix), (prm["k_k"], (1, D_R), fix),
           (prm["k_a"], (1, D_R), fix), (seg, (D_R, D_R), fix)] + [(c, (tr, D_R), row) for c in cts]
    outs = [((t, D_R), (tr, D_R), row, False), ((t, LANES), (tr, LANES), row, False), ((t, LANES), (tr, LANES), row, False),
            ((1, D_R), (1, D_R), fix, True), ((LANES, D_R), (LANES, D_R), fix, True), ((1, D_R), (1, D_R), fix, True),
            ((LANES, D_R), (LANES, D_R), fix, True), ((LANES, D_R), (LANES, D_R), fix, True),
            ((1, D_R), (1, D_R), fix, True), ((1, D_R), (1, D_R), fix, True)]
    return _call(fn, name, (t // tr,), ins, outs, acc_axis=0)


def _rwkv_post_ins(y, prs, k2, g, prm, seg, tr):
    row = lambda i: (i, 0)
    fix = lambda i: (0, 0)
    return [(y, (tr, D_R), row), (prs, (tr, D_R), row), (k2, (tr, D_R), row), (prs, (tr, D_R), lambda i: (i, 2)),
            (g, (tr, D_R), row), (prm["lnx_g"], (1, D_R), fix), (prm["lnx_b"], (1, D_R), fix), (prm["r_k"], (1, D_R), fix),
            (seg, (D_R, D_R), fix)]


def _rwkv_post_fwd(name, y, prs, k2, g, prm, seg, tr):
    t = y.shape[0]
    return _call(_rwkv_post, name, (t // tr,), _rwkv_post_ins(y, prs, k2, g, prm, seg, tr),
                 [((t, D_R), (tr, D_R), lambda i: (i, 0), False)])


def _rwkv_post_bwd(name, y, prs, k2, g, prm, seg, dy, dy_col, tr):
    t = y.shape[0]

    def fn(yv, r, k2v, v, gv, lg, lb, rk, segv, ct):
        _, vjp = jax.vjp(lambda *a: _rwkv_post(*a, segv), yv, r, k2v, v, gv, lg, lb, rk)
        return vjp(ct)

    row = lambda i: (i, 0)
    fix = lambda i: (0, 0)
    ins = _rwkv_post_ins(y, prs, k2, g, prm, seg, tr) + [(dy, (tr, D_R), lambda i: (i, dy_col))]
    outs = [((t, D_R), (tr, D_R), row, False)] * 5 + [((1, D_R), (1, D_R), fix, True)] * 3
    return _call(fn, name, (t // tr,), ins, outs, acc_axis=0)


def _wkv_chunk(st0, r, lw, k, v, a, b):
    c = r.shape[0]
    ii = lax.broadcasted_iota(jnp.int32, (c, c), 0)
    jj = lax.broadcasted_iota(jnp.int32, (c, c), 1)
    incl = ii >= jj
    strict = ii > jj
    cum = _doth(incl.astype(F32), lw)
    cum_x = cum - lw
    tot = _doth(jnp.ones((c, c), F32), lw)
    tot_col = _doth(lw, jnp.ones((c, HEAD_DIM), F32), _TN)
    e_inv = jnp.exp(-cum)
    a_t = a * jnp.exp(cum_x)
    r_t = r * jnp.exp(cum)
    b_t = b * e_inv
    k_t = k * e_inv
    m_ab = jnp.where(strict, _doth(a_t, b_t, _NT), 0.0)
    m_ak = jnp.where(strict, _doth(a_t, k_t, _NT), 0.0)
    n_rb = jnp.where(incl, _doth(r_t, b_t, _NT), 0.0)
    n_rk = jnp.where(incl, _doth(r_t, k_t, _NT), 0.0)
    u = _doth(a_t, st0) + _doth(m_ak, v)
    mp = m_ab
    steps = max(1, (c - 1).bit_length())
    for s in range(steps):
        u = u + _doth(mp, u)
        if s + 1 < steps:
            mp = _doth(mp, mp)
    y = _doth(r_t, st0) + _doth(n_rb, u) + _doth(n_rk, v)
    dec = jnp.exp(tot - cum)
    st1 = jnp.exp(tot_col) * st0 + _doth(b * dec, u, _TN) + _doth(k * dec, v, _TN)
    return y, st1


WKV_HEADS_PER_STEP = 2


def _wkv_fwd(name, xs):
    h, t, n = xs[0].shape
    c = _chunk_len(t)
    nc = t // c
    hb = WKV_HEADS_PER_STEP

    def body(r, lw, k, v, a, b, y_ref, st_ref, state):
        @pl.when(pl.program_id(1) == 0)
        def _():
            state[...] = jnp.zeros(state.shape, F32)

        for i in range(hb):
            st0 = state[i]
            st_ref[i] = st0
            y, st1 = _wkv_chunk(st0, r[i], lw[i], k[i], v[i], a[i], b[i])
            y_ref[i] = y
            state[i] = st1

    seq = pl.BlockSpec((hb, c, n), lambda g, j: (g, j, 0))
    return pl.pallas_call(
        body, name=name, grid=(h // hb, nc), in_specs=[seq] * 6,
        out_specs=[seq, pl.BlockSpec((hb, None, n, n), lambda g, j: (g, j, 0, 0))],
        out_shape=[jax.ShapeDtypeStruct((h, t, n), F32), jax.ShapeDtypeStruct((h, nc, n, n), F32)],
        scratch_shapes=[pltpu.VMEM((hb, n, n), F32)],
        compiler_params=_cparams(None),
    )(*xs)


def _wkv_bwd(name, xs, st, dy):
    h, t, n = xs[0].shape
    c = _chunk_len(t)
    nc = t // c
    hb = WKV_HEADS_PER_STEP

    def body(r, lw, k, v, a, b, st_ref, dy_ref, dr, dlw, dk, dv, da, db, dstate):
        @pl.when(pl.program_id(1) == 0)
        def _():
            dstate[...] = jnp.zeros(dstate.shape, F32)

        for i in range(hb):
            _, vjp = jax.vjp(_wkv_chunk, st_ref[i], r[i], lw[i], k[i], v[i], a[i], b[i])
            dst0, *dxs = vjp((dy_ref[i], dstate[i]))
            for ref, val in zip((dr, dlw, dk, dv, da, db), dxs):
                ref[i] = val
            dstate[i] = dst0

    seq = pl.BlockSpec((hb, c, n), lambda g, j: (g, nc - 1 - j, 0))
    return pl.pallas_call(
        body, name=name, grid=(h // hb, nc),
        in_specs=[seq] * 6 + [pl.BlockSpec((hb, None, n, n), lambda g, j: (g, nc - 1 - j, 0, 0)), seq],
        out_specs=[seq] * 6,
        out_shape=[jax.ShapeDtypeStruct((h, t, n), F32)] * 6,
        scratch_shapes=[pltpu.VMEM((hb, n, n), F32)],
        compiler_params=_cparams(None),
    )(*xs, st, dy)


def _rope(x, cos, sin, rot):
    return x * cos + _doth(x, rot) * sin


def _attn_block(nb, q4, kp, kc, km, vp, vc, vm, sk, cq, sq, cp, sp, cm, sm, rot):
    scale = HEAD_DIM ** -0.5
    kpr = _rope(kp, cp, sp, rot).astype(MXU_DTYPE)
    kcr = _rope(kc, cq, sq, rot).astype(MXU_DTYPE)
    kmr = _rope(km, cm, sm, rot).astype(MXU_DTYPE)
    i = lax.broadcasted_iota(jnp.int32, (BLOCK, BLOCK), 0)
    j = lax.broadcasted_iota(jnp.int32, (BLOCK, BLOCK), 1)
    nbv = jnp.zeros((BLOCK, BLOCK), jnp.int32) + nb
    ok_p = (j > i) & (nbv >= 2)
    ok_c = (j <= i) & (nbv >= 1)
    ok_m = (j >= BLOCK - N_META) & ((nbv >= 1) | (j <= i))
    outs = []
    for h in range(GQA_GROUP):
        qr = _rope(q4[h], cq, sq, rot).astype(MXU_DTYPE)
        ntdot = lambda kk: lax.dot_general(qr, kk, _NT, preferred_element_type=F32) * scale
        s_p = jnp.where(ok_p, ntdot(kpr), NEG_INF)
        s_c = jnp.where(ok_c, ntdot(kcr), NEG_INF)
        s_m = jnp.where(ok_m, ntdot(kmr), NEG_INF)
        rmax = lambda s: jnp.max(s, axis=-1, keepdims=True)
        m = lax.stop_gradient(jnp.maximum(jnp.maximum(rmax(s_p), rmax(s_c)), jnp.maximum(rmax(s_m), sk[h])))
        e_p, e_c, e_m = jnp.exp(s_p - m), jnp.exp(s_c - m), jnp.exp(s_m - m)
        rsum = lambda e: jnp.sum(e, axis=-1, keepdims=True)
        inv = 1.0 / (rsum(e_p) + rsum(e_c) + rsum(e_m) + jnp.exp(sk[h] - m))
        outs.append(_dotm(e_p * inv, vp) + _dotm(e_c * inv, vc) + _dotm(e_m * inv, vm))
    return tuple(outs)


def _attn_specs():
    cur = lambda g, n: (g, n, 0)
    prev = lambda g, n: (g, jnp.maximum(n - 1, 0), 0)
    meta = lambda g, n: (g, 0, 0)
    kv = lambda m: pl.BlockSpec((None, BLOCK, HEAD_DIM), m)
    tab = lambda m: pl.BlockSpec((BLOCK, HEAD_DIM), m)
    tcur, tprev, tmeta = (lambda g, n: (n, 0)), (lambda g, n: (jnp.maximum(n - 1, 0), 0)), (lambda g, n: (0, 0))
    qspec = pl.BlockSpec((GQA_GROUP, BLOCK, HEAD_DIM), cur)
    sspec = pl.BlockSpec((GQA_GROUP, 8, LANES), meta)
    specs = [qspec, kv(prev), kv(cur), kv(meta), kv(prev), kv(cur), kv(meta), sspec,
             tab(tcur), tab(tcur), tab(tprev), tab(tprev), tab(tmeta), tab(tmeta),
             pl.BlockSpec((HEAD_DIM, HEAD_DIM), lambda g, n: (0, 0))]
    return specs, qspec, sspec, kv


def _attn_args(q, k, v, sinks_b, cos, sin, rot):
    return (q, k, k, k, v, v, v, sinks_b, cos, sin, cos, sin, cos, sin, rot)


def _attn_fwd(name, q, k, v, sinks_b, cos, sin, rot):
    tp = q.shape[1]
    specs, qspec, _, _ = _attn_specs()

    def body(q_ref, kp, kc, km, vp, vc, vm, s_ref, cq, sq, cp, sp, cm, sm, rot_ref, o_ref):
        q4 = tuple(q_ref[h] for h in range(GQA_GROUP))
        sk = tuple(s_ref[h][0:1, 0:1] for h in range(GQA_GROUP))
        outs = _attn_block(pl.program_id(1), q4, kp[...], kc[...], km[...], vp[...], vc[...], vm[...], sk,
                           cq[...], sq[...], cp[...], sp[...], cm[...], sm[...], rot_ref[...])
        for h in range(GQA_GROUP):
            o_ref[h] = outs[h]

    return pl.pallas_call(
        body, name=name, grid=(N_KV_HEADS, tp // BLOCK), in_specs=specs, out_specs=qspec,
        out_shape=jax.ShapeDtypeStruct(q.shape, F32), compiler_params=_cparams(None),
    )(*_attn_args(q, k, v, sinks_b, cos, sin, rot))


def _attn_bwd(name, q, k, v, sinks_b, cos, sin, rot, do):
    tp = q.shape[1]
    nb = tp // BLOCK
    specs, qspec, sspec, kv = _attn_specs()

    def body(q_ref, kp, kc, km, vp, vc, vm, s_ref, cq, sq, cp, sp, cm, sm, rot_ref, do_ref,
             dq_ref, dkp, dkc, dvp, dvc, dkm, dvm, ds_ref):
        n = pl.program_id(1)
        q4 = tuple(q_ref[h] for h in range(GQA_GROUP))
        sk = tuple(s_ref[h][0:1, 0:1] for h in range(GQA_GROUP))
        tabs = (cq[...], sq[...], cp[...], sp[...], cm[...], sm[...], rot_ref[...])
        _, vjp = jax.vjp(lambda *a: _attn_block(n, *a, *tabs), q4, kp[...], kc[...], km[...], vp[...], vc[...], vm[...], sk)
        dq4, gkp, gkc, gkm, gvp, gvc, gvm, dsk = vjp(tuple(do_ref[h] for h in range(GQA_GROUP)))
        dkp[...] = gkp
        dkc[...] = gkc
        dvp[...] = gvp
        dvc[...] = gvc
        for h in range(GQA_GROUP):
            dq_ref[h] = dq4[h]

        @pl.when(n == 0)
        def _():
            dkm[...] = gkm
            dvm[...] = gvm
            for h in range(GQA_GROUP):
                ds_ref[h] = jnp.broadcast_to(dsk[h], (8, LANES))

        @pl.when(n != 0)
        def _():
            dkm[...] += gkm
            dvm[...] += gvm
            for h in range(GQA_GROUP):
                ds_ref[h] += jnp.broadcast_to(dsk[h], (8, LANES))

    part = pl.BlockSpec((None, None, BLOCK, HEAD_DIM), lambda g, n: (g, n, 0, 0))
    part_shape = jax.ShapeDtypeStruct((N_KV_HEADS, nb, BLOCK, HEAD_DIM), F32)
    meta_shape = jax.ShapeDtypeStruct((N_KV_HEADS, BLOCK, HEAD_DIM), F32)
    return pl.pallas_call(
        body, name=name, grid=(N_KV_HEADS, nb), in_specs=specs + [qspec],
        out_specs=[qspec, part, part, part, part, kv(lambda g, n: (g, 0, 0)), kv(lambda g, n: (g, 0, 0)), sspec],
        out_shape=[jax.ShapeDtypeStruct(q.shape, F32), part_shape, part_shape, part_shape, part_shape,
                   meta_shape, meta_shape, jax.ShapeDtypeStruct(sinks_b.shape, F32)],
        compiler_params=_cparams(None),
    )(*_attn_args(q, k, v, sinks_b, cos, sin, rot), do)


def _kv_combine(name, prev_part, own_part, meta):
    g, nb = own_part.shape[:2]

    def fn(own, nxt, mt):
        m = pl.program_id(1)
        one = jnp.ones((BLOCK, HEAD_DIM), F32)
        use_next = jnp.where(one * m < nb - 1, 1.0, 0.0)
        use_meta = jnp.where(one * m < 1, 1.0, 0.0)
        return own + nxt * use_next + mt * use_meta

    blk = (None, None, BLOCK, HEAD_DIM)
    return _call(fn, name, (g, nb),
                 [(own_part, blk, lambda a, m: (a, m, 0, 0)),
                  (prev_part, blk, lambda a, m: (a, jnp.minimum(m + 1, nb - 1), 0, 0)),
                  (meta, (None, BLOCK, HEAD_DIM), lambda a, m: (a, 0, 0))],
                 [((g, nb * BLOCK, HEAD_DIM), (None, BLOCK, HEAD_DIM), lambda a, m: (a, m, 0), False)])


PACK_W = 1024
PACK_ROWS = 256


def _adamw(name, w, g, m, v):
    rows = w.shape[0]

    def fn(wv, gv, mv, vv):
        m1 = ADAM_B1 * mv + (1.0 - ADAM_B1) * gv
        v1 = ADAM_B2 * vv + (1.0 - ADAM_B2) * (gv * gv)
        m_hat = m1 / (1.0 - ADAM_B1 ** ADAM_STEP)
        v_hat = v1 / (1.0 - ADAM_B2 ** ADAM_STEP)
        return -ADAM_LR * (m_hat / (jnp.sqrt(v_hat) + ADAM_EPS) + ADAM_WD * wv), m1, v1

    blk = (PACK_ROWS, PACK_W)
    row = lambda i: (i, 0)
    return _call(fn, name, (rows // PACK_ROWS,), [(a, blk, row) for a in (w, g, m, v)],
                 [((rows, PACK_W), blk, row, False)] * 3)


def _pair_add(name, g2, recv, c_idx):
    _, s, rows, w = g2.shape

    def body(c_ref, a_ref, b_ref, o_ref):
        o_ref[...] = a_ref[...] + b_ref[...]

    blk = (None, PACK_ROWS, w)
    return pl.pallas_call(
        body, name=name,
        grid_spec=pltpu.PrefetchScalarGridSpec(
            num_scalar_prefetch=1, grid=(s, rows // PACK_ROWS),
            in_specs=[pl.BlockSpec((None, None, PACK_ROWS, w), lambda j, i, c: (c[0], j, i, 0)),
                      pl.BlockSpec(blk, lambda j, i, c: (j, i, 0))],
            out_specs=pl.BlockSpec(blk, lambda j, i, c: (j, i, 0))),
        out_shape=jax.ShapeDtypeStruct((s, rows, w), F32), compiler_params=_cparams(None),
    )(c_idx, g2, recv)


def _sum_chips(name, parts):
    _, rows, w = parts.shape

    def fn(a, b, c, d):
        return ((a + b) + c) + d

    return _call(fn, name, (rows // PACK_ROWS,),
                 [(parts, (None, PACK_ROWS, w), lambda i, k=k: (k, i, 0)) for k in range(N_CHIPS)],
                 [((rows, w), (PACK_ROWS, w), lambda i: (i, 0), False)])


def _mesh_pos():
    return lax.axis_index("x"), lax.axis_index("y"), lax.axis_index("c")


def _other_chips(x, y):
    return [(1 - x, y), (x, 1 - y), (1 - x, 1 - y)]


_ANY = pl.BlockSpec(memory_space=pl.ANY)


def _all_gather_shards(name, mine):
    def body(x_ref, out_ref, send_sems, recv_sems, local_sem):
        x, y, c = _mesh_pos()
        me = 2 * x + y
        sibling = (x, y, 1 - c)
        chips = _other_chips(x, y)

        def copy(k, src, chip_idx, half, to):
            return pltpu.make_async_remote_copy(src_ref=src, dst_ref=out_ref.at[chip_idx, half],
                                                send_sem=send_sems.at[k], recv_sem=recv_sems.at[k],
                                                device_id=to, device_id_type=MESH)

        local = pltpu.make_async_copy(x_ref, out_ref.at[me], local_sem)
        local.start()
        first = [copy(j, x_ref.at[c], me, c, (*chip, c)) for j, chip in enumerate(chips)]
        for cp in first:
            cp.start()
        passed = []
        for j, (cx, cy) in enumerate(chips):
            idx = 2 * cx + cy
            copy(j, x_ref.at[c], idx, c, sibling).wait_recv()
            fwd = copy(3 + j, out_ref.at[idx, c], idx, c, sibling)
            fwd.start()
            passed.append(fwd)
        for j, (cx, cy) in enumerate(chips):
            copy(3 + j, x_ref.at[c], 2 * cx + cy, 1 - c, sibling).wait_recv()
        for cp in first + passed:
            cp.wait_send()
        local.wait()

    return pl.pallas_call(
        body, name=name, in_specs=[_ANY], out_specs=_ANY,
        out_shape=jax.ShapeDtypeStruct((N_CHIPS,) + mine.shape, mine.dtype),
        scratch_shapes=[pltpu.SemaphoreType.DMA((6,)), pltpu.SemaphoreType.DMA((6,)), pltpu.SemaphoreType.DMA(())],
        compiler_params=pltpu.CompilerParams(has_side_effects=True),
    )(mine)


def _send_half_to_sibling(name, g2):
    def body(g_ref, out_ref, send_sem, recv_sem):
        x, y, c = _mesh_pos()
        cp = pltpu.make_async_remote_copy(src_ref=g_ref.at[1 - c], dst_ref=out_ref, send_sem=send_sem, recv_sem=recv_sem,
                                          device_id=(x, y, 1 - c), device_id_type=MESH)
        cp.start()
        cp.wait()

    return pl.pallas_call(
        body, name=name, in_specs=[_ANY], out_specs=_ANY,
        out_shape=jax.ShapeDtypeStruct(g2.shape[1:], g2.dtype),
        scratch_shapes=[pltpu.SemaphoreType.DMA(()), pltpu.SemaphoreType.DMA(())],
        compiler_params=pltpu.CompilerParams(has_side_effects=True),
    )(g2)


def _scatter_to_chips(name, hsum):
    def body(h_ref, out_ref, send_sems, recv_sems, local_sem):
        x, y, c = _mesh_pos()
        me = 2 * x + y
        local = pltpu.make_async_copy(h_ref.at[me], out_ref.at[me], local_sem)
        local.start()
        cps = []
        for j, (cx, cy) in enumerate(_other_chips(x, y)):
            cp = pltpu.make_async_remote_copy(src_ref=h_ref.at[2 * cx + cy], dst_ref=out_ref.at[me],
                                              send_sem=send_sems.at[j], recv_sem=recv_sems.at[j],
                                              device_id=(cx, cy, c), device_id_type=MESH)
            cp.start()
            cps.append(cp)
        for j, (cx, cy) in enumerate(_other_chips(x, y)):
            pltpu.make_async_remote_copy(src_ref=h_ref.at[me], dst_ref=out_ref.at[2 * cx + cy],
                                         send_sem=send_sems.at[j], recv_sem=recv_sems.at[j],
                                         device_id=(cx, cy, c), device_id_type=MESH).wait_recv()
        for cp in cps:
            cp.wait_send()
        local.wait()

    return pl.pallas_call(
        body, name=name, in_specs=[_ANY], out_specs=_ANY,
        out_shape=jax.ShapeDtypeStruct(hsum.shape, hsum.dtype),
        scratch_shapes=[pltpu.SemaphoreType.DMA((3,)), pltpu.SemaphoreType.DMA((3,)), pltpu.SemaphoreType.DMA(())],
        compiler_params=pltpu.CompilerParams(has_side_effects=True),
    )(hsum)


def _join_halves(name, half):
    def body(h_ref, out_ref, send_sem, recv_sem, local_sem):
        x, y, c = _mesh_pos()
        local = pltpu.make_async_copy(h_ref, out_ref.at[c], local_sem)
        local.start()
        cp = pltpu.make_async_remote_copy(src_ref=h_ref, dst_ref=out_ref.at[c], send_sem=send_sem, recv_sem=recv_sem,
                                          device_id=(x, y, 1 - c), device_id_type=MESH)
        cp.start()
        pltpu.make_async_remote_copy(src_ref=h_ref, dst_ref=out_ref.at[1 - c], send_sem=send_sem, recv_sem=recv_sem,
                                     device_id=(x, y, 1 - c), device_id_type=MESH).wait_recv()
        cp.wait_send()
        local.wait()

    return pl.pallas_call(
        body, name=name, in_specs=[_ANY], out_specs=_ANY,
        out_shape=jax.ShapeDtypeStruct((2,) + half.shape, half.dtype),
        scratch_shapes=[pltpu.SemaphoreType.DMA(()), pltpu.SemaphoreType.DMA(()), pltpu.SemaphoreType.DMA(())],
        compiler_params=pltpu.CompilerParams(has_side_effects=True),
    )(half)


def _pack(arrays, dtype, rows_multiple):
    flat = jnp.concatenate([a.reshape(-1).astype(dtype) for a in arrays])
    unit = rows_multiple * PACK_W
    total = -(-flat.shape[0] // unit) * unit
    return jnp.pad(flat, (0, total - flat.shape[0])).reshape(total // PACK_W, PACK_W)


def _unpack(flat, shapes):
    out, off = [], 0
    for s in shapes:
        n = 1
        for d in s:
            n *= d
        out.append(flat[..., off:off + n].reshape(flat.shape[:-1] + tuple(s)))
        off += n
    return out


def _ffn_fwd(tag, h, g, w_gate, w_val, conv, bias, w_down, tm):
    hn = _rms_fwd(f"{tag}_norm", h, g, tm)
    gate = _mm_cs(f"{tag}_up_gate", hn, w_gate, tm)
    val = _mm_cs(f"{tag}_up_val", hn, w_val, tm)
    act = _ffn_col_fwd(f"{tag}_glu", gate, val, conv, bias)
    h_out = _mm_full(f"{tag}_down", act, w_down, tm, w_down.shape[0] // 2, add=h)
    return h_out, (hn, gate, val, act)


def _ffn_bwd(tag, h, g, w_gate, w_val, conv, bias, w_down, saved, dh, tm):
    hn, gate, val, act = saved
    da = _mm_nt_full(f"{tag}_down_dx", dh, w_down, tm, w_down.shape[0] // 2)
    dw_down = _mm_tn_full(f"{tag}_down_dw", act, dh, tm, w_down.shape[0] // 2)
    dgate, dval, dconv, dbias = _ffn_col_bwd(f"{tag}_glu_bwd", gate, val, da, conv, bias)
    dw_gate = _mm_tn_cs(f"{tag}_up_gate_dw", hn, dgate, 2, tm)
    dw_val = _mm_tn_cs(f"{tag}_up_val_dw", hn, dval, 2, tm)
    dhn = _mm_nt_cs(f"{tag}_up_gate_dx", dgate, w_gate, tm)
    dhn = _mm_nt_cs(f"{tag}_up_val_dx", dval, w_val, tm, add=dhn)
    dh, dg = _rms_bwd(f"{tag}_norm_bwd", h, g, dhn, dh, tm)
    return dh, dict(norm=dg, w_up=jnp.concatenate([dw_gate, dw_val], axis=0), conv=dconv, bias=dbias, w_down=dw_down)


def _to_heads(z, nh, pad):
    t = z.shape[0]
    return jnp.pad(z.reshape(t, nh, HEAD_DIM).transpose(1, 0, 2), ((0, 0), (pad, 0), (0, 0)))


def _from_heads(z, pad):
    nh, tp, _ = z.shape
    return z[:, pad:].transpose(1, 0, 2).reshape(tp - pad, nh * HEAD_DIM)


def _rope_tables(tp, pad):
    half = HEAD_DIM // 2
    inv = ROPE_THETA ** (-jnp.arange(half, dtype=F32) / half)
    ang = (jnp.arange(tp, dtype=F32) - pad)[:, None] * inv[None, :]
    cos, sin = jnp.cos(ang), jnp.sin(ang)
    rot = jnp.zeros((HEAD_DIM, HEAD_DIM), F32)
    idx = jnp.arange(half)
    rot = rot.at[idx + half, idx].set(-1.0).at[idx, idx + half].set(1.0)
    return jnp.concatenate([cos, cos], axis=1), jnp.concatenate([sin, sin], axis=1), rot


def _local_step(x, tgt, w):
    seq = x.shape[0]
    t = seq + N_META
    tm = _row_tile(t, 704)
    tr = _row_tile(t, 352)
    pad = BLOCK - N_META
    grads = {}

    h0 = jnp.concatenate([w["meta_tokens"], x], axis=0)
    tgt_p = jnp.pad(tgt, ((N_META, 0), (0, 0)))

    hn0 = _rms_fwd("l0_norm", h0, w["norm_mix"][0:1], tm)
    p0 = _mm_cs("l0_in", hn0, w["ev_w_in"], tm)
    uc, yb = _even_col_fwd("l0_convs", p0, w["ev_conv_a"], w["ev_conv_b"])
    ya = _even_ln_fwd("l0_ln", uc, w["ev_ln_a_g"], w["ev_ln_a_b"], tm)
    y0 = jnp.concatenate([ya, yb], axis=1)
    h1 = _mm_full("l0_out", y0, w["ev_w_out"], tm, D_MODEL, add=h0)
    f0 = (w["norm_ffn"][0:1], w["ff_w_gate"][0], w["ff_w_val"][0], w["ff_conv"][0], w["ff_conv_b"][0:1], w["ff_w_down"][0])
    h2, ffn0 = _ffn_fwd("f0", h1, *f0, tm)

    hn2 = _rms_fwd("l1_norm", h2, w["norm_mix"][1:2], tm)
    p1 = _mm_cs("l1_in", hn2, w["od_w_in"], tm)
    cos, sin, rot = _rope_tables(t + pad, pad)
    qh = _to_heads(p1[:, :D_ATT], N_Q_HEADS, pad)
    kh = _to_heads(p1[:, D_ATT:D_ATT + D_KV], N_KV_HEADS, pad)
    vh = _to_heads(p1[:, D_ATT + D_KV:D_ATT + 2 * D_KV], N_KV_HEADS, pad)
    sinks_b = jnp.broadcast_to(w["od_sinks"].reshape(N_Q_HEADS, 1, 1), (N_Q_HEADS, 8, LANES))
    y_att = _from_heads(_attn_fwd("l1_attn", qh, kh, vh, sinks_b, cos, sin, rot), pad)

    col0 = (D_ATT + 2 * D_KV) // LANES
    ch = jnp.arange(D_R) // HEAD_DIM
    seg = (ch[:, None] == ch[None, :]).astype(F32)
    prm = dict(w0=w["od_w0"], a0=w["od_a0"], g2=w["od_g2"], k_k=w["od_k_k"], k_a=w["od_k_a"],
               lnx_g=w["od_lnx_g"], lnx_b=w["od_lnx_b"], r_k=w["od_r_k"].reshape(1, D_R),
               w2p=jnp.concatenate([w["od_w2"], jnp.zeros((LORA_A, D_R), F32)], axis=0),
               a2p=jnp.concatenate([jnp.zeros((LORA_W, D_R), F32), w["od_a2"]], axis=0))
    prs = _shift_fwd("l1_shift", p1, col0, w["od_mu"])
    lw, k2, a_, b_, gate_r = _rwkv_pre_fwd("l1_rwkv_pre", prs, prm, seg, tr)
    heads = lambda z: z.reshape(t, N_R_HEADS, HEAD_DIM).transpose(1, 0, 2)
    unheads = lambda z: z.transpose(1, 0, 2).reshape(t, D_R)
    scan_in = [heads(z) for z in (prs[:, :D_R], lw, k2, prs[:, 2 * D_R:3 * D_R], a_, b_)]
    y_scan_h, states = _wkv_fwd("l1_wkv", scan_in)
    y_scan = unheads(y_scan_h)
    y_rwkv = _rwkv_post_fwd("l1_rwkv_post", y_scan, prs, k2, gate_r, prm, seg, tr)
    y1 = jnp.concatenate([y_att, y_rwkv], axis=1)
    h3 = _mm_full("l1_out", y1, w["od_w_out"], tm, D_MODEL, add=h2)
    f1 = (w["norm_ffn"][1:2], w["ff_w_gate"][1], w["ff_w_val"][1], w["ff_conv"][1], w["ff_conv_b"][1:2], w["ff_w_down"][1])
    h4, ffn1 = _ffn_fwd("f1", h3, *f1, tm)

    loss_blk, dh, d_norm_final = _final_loss("final", h4, w["norm_final"], tgt_p, tm)
    grads["norm_final"] = d_norm_final

    dh, gf1 = _ffn_bwd("f1", h3, *f1, ffn1, dh, tm)
    dy1 = _mm_nt_full("l1_out_dx", dh, w["od_w_out"], tm, D_MODEL)
    grads["od_w_out"] = _mm_tn_full("l1_out_dw", y1, dh, tm, D_MODEL // 2)
    dy_scan, dr_p, dk2_p, dv_p, dgate_r, grads["od_lnx_g"], grads["od_lnx_b"], d_rk = _rwkv_post_bwd(
        "l1_rwkv_post_bwd", y_scan, prs, k2, gate_r, prm, seg, dy1, 1, tr)
    grads["od_r_k"] = d_rk.reshape(N_R_HEADS, HEAD_DIM)
    dscan = _wkv_bwd("l1_wkv_bwd", scan_in, states, heads(dy_scan))
    dr_s, dlw, dk2_s, dv_s, da_, db_ = [unheads(z) for z in dscan]
    dk, dxl, dgd, grads["od_w0"], dw2p, grads["od_a0"], da2p, grads["od_g2"], grads["od_k_k"], grads["od_k_a"] = (
        _rwkv_pre_bwd("l1_rwkv_pre_bwd", prs, prm, seg, (dlw, dk2_s + dk2_p, da_, db_, dgate_r), tr))
    grads["od_w2"] = dw2p[:LORA_W]
    grads["od_a2"] = da2p[LORA_W:]
    dprs = jnp.concatenate([dr_s + dr_p, dk, dv_s + dv_p, dxl, dgd], axis=1)
    dpr, grads["od_mu"] = _shift_bwd("l1_shift_bwd", p1, col0, w["od_mu"], dprs)
    doh = _to_heads(dy1[:, :D_ATT], N_Q_HEADS, pad)
    dqh, dkp, dkc, dvp, dvc, dkm, dvm, dsinks = _attn_bwd("l1_attn_bwd", qh, kh, vh, sinks_b, cos, sin, rot, doh)
    grads["od_sinks"] = dsinks[:, 0, 0].reshape(1, N_Q_HEADS)
    dkh = _kv_combine("l1_attn_dk", dkp, dkc, dkm)
    dvh = _kv_combine("l1_attn_dv", dvp, dvc, dvm)
    dp1 = jnp.concatenate([_from_heads(dqh, pad), _from_heads(dkh, pad), _from_heads(dvh, pad), dpr], axis=1)
    grads["od_w_in"] = _mm_tn_cs("l1_in_dw", hn2, dp1, N_CHIPS, tm)
    dhn2 = _mm_nt_cs("l1_in_dx", dp1, w["od_w_in"], tm)
    dh, d_mix1 = _rms_bwd("l1_norm_bwd", h2, w["norm_mix"][1:2], dhn2, dh, tm)

    dh, gf0 = _ffn_bwd("f0", h1, *f0, ffn0, dh, tm)
    dy0 = _mm_nt_full("l0_out_dx", dh, w["ev_w_out"], tm, D_MODEL)
    grads["ev_w_out"] = _mm_tn_full("l0_out_dw", y0, dh, tm, D_MODEL // 2)
    duc, grads["ev_ln_a_g"], grads["ev_ln_a_b"] = _even_ln_bwd("l0_ln_bwd", uc, w["ev_ln_a_g"], w["ev_ln_a_b"], dy0, 0, tm)
    *dparts, grads["ev_conv_a"], grads["ev_conv_b"] = _even_col_bwd("l0_convs_bwd", p0, duc, dy0, w["ev_conv_a"], w["ev_conv_b"])
    dp0 = jnp.concatenate(dparts, axis=1)
    grads["ev_w_in"] = _mm_tn_cs("l0_in_dw", hn0, dp0, N_CHIPS, tm)
    dhn0 = _mm_nt_cs("l0_in_dx", dp0, w["ev_w_in"], tm)
    dh, d_mix0 = _rms_bwd("l0_norm_bwd", h0, w["norm_mix"][0:1], dhn0, dh, tm)

    grads["norm_mix"] = jnp.concatenate([d_mix0, d_mix1], axis=0)
    grads["norm_ffn"] = jnp.concatenate([gf0["norm"], gf1["norm"]], axis=0)
    grads["ff_w_up"] = jnp.stack([gf0["w_up"], gf1["w_up"]])
    grads["ff_conv"] = jnp.stack([gf0["conv"], gf1["conv"]])
    grads["ff_conv_b"] = jnp.concatenate([gf0["bias"], gf1["bias"]], axis=0)
    grads["ff_w_down"] = jnp.stack([gf0["w_down"], gf1["w_down"]])
    grads["meta_tokens"] = dh[:N_META]
    return loss_blk[0, 0], dh[N_META:], grads


SHARD_AXIS = {
    "meta_tokens": 1, "norm_mix": None, "norm_ffn": None, "norm_final": None,
    "ev_w_in": 2, "ev_conv_a": 2, "ev_ln_a_g": None, "ev_ln_a_b": None, "ev_conv_b": 2, "ev_w_out": 1,
    "od_w_in": 2, "od_sinks": None, "od_mu": 1, "od_w0": 1, "od_w2": 2, "od_a0": 1, "od_a2": 2, "od_g2": 2,
    "od_k_k": 1, "od_k_a": 1, "od_r_k": None, "od_lnx_g": 1, "od_lnx_b": 1, "od_w_out": 1,
    "ff_w_up": 2, "ff_conv": 2, "ff_conv_b": None, "ff_w_down": 1,
}
WEIGHTS = list(SHARD_AXIS)
BIG = ("ev_w_in", "ev_w_out", "od_w_in", "od_w_out", "ff_w_up", "ff_w_down")
SHARDED = [n for n in WEIGHTS if SHARD_AXIS[n] is not None]
SMALL = [n for n in SHARDED if n not in BIG]
REPLICATED = [n for n in WEIGHTS if SHARD_AXIS[n] is None]


def _join(g, axis):
    return jnp.concatenate([g[k] for k in range(N_CHIPS)], axis=axis)


def _split(full, axis):
    return jnp.stack(jnp.split(full, N_CHIPS, axis=axis))


def _full_weights(gathered, repl):
    w = {}
    sq = lambda a: a.reshape(a.shape[1:]) if a.shape[0] == 1 else a
    for n in REPLICATED:
        w[n] = repl[n]
    w["norm_final"] = repl["norm_final"].reshape(1, D_MODEL)
    for n in ("ev_ln_a_g", "ev_ln_a_b"):
        w[n] = repl[n].reshape(1, D_A)
    w["od_r_k"] = repl["od_r_k"][0]
    w["meta_tokens"] = _join(gathered["meta_tokens"], 1)
    w["ev_w_in"] = gathered["ev_w_in"][:, 0]
    w["od_w_in"] = gathered["od_w_in"][:, 0]
    w["ev_w_out"] = gathered["ev_w_out"].reshape(D_MODEL, D_MODEL)
    w["od_w_out"] = gathered["od_w_out"].reshape(D_MODEL, D_MODEL)
    for n in ("ev_conv_a", "ev_conv_b", "od_w2", "od_a2", "od_g2"):
        w[n] = sq(_join(gathered[n], 2))
    for n in ("od_mu", "od_w0", "od_a0", "od_k_k", "od_k_a", "od_lnx_g", "od_lnx_b"):
        w[n] = _join(gathered[n], 1)
    up = gathered["ff_w_up"]
    w["ff_w_gate"] = up[:2].transpose(1, 0, 2, 3)
    w["ff_w_val"] = up[2:].transpose(1, 0, 2, 3)
    w["ff_conv"] = _join(gathered["ff_conv"], 2)
    w["ff_w_down"] = gathered["ff_w_down"].transpose(1, 0, 2, 3).reshape(2, D_FF, D_MODEL)
    return w


def _shard_grads(grads):
    out = {}
    for n in REPLICATED:
        out[n] = grads[n]
    out["norm_final"] = grads["norm_final"].reshape(D_MODEL)
    out["od_r_k"] = grads["od_r_k"][None]
    out["meta_tokens"] = _split(grads["meta_tokens"], 1)
    out["ev_w_in"] = grads["ev_w_in"][:, None]
    out["od_w_in"] = grads["od_w_in"][:, None]
    out["ev_w_out"] = grads["ev_w_out"].reshape(N_CHIPS, 1, D_MODEL // N_CHIPS, D_MODEL)
    out["od_w_out"] = grads["od_w_out"].reshape(N_CHIPS, 1, D_MODEL // N_CHIPS, D_MODEL)
    for n in ("ev_conv_a", "ev_conv_b", "od_w2", "od_a2", "od_g2"):
        out[n] = _split(grads[n][None], 2)
    for n in ("od_mu", "od_w0", "od_a0", "od_k_k", "od_k_a", "od_lnx_g", "od_lnx_b"):
        out[n] = _split(grads[n], 1)
    out["ff_w_up"] = grads["ff_w_up"].transpose(1, 0, 2, 3)
    out["ff_conv"] = _split(grads["ff_conv"], 2)
    out["ff_w_down"] = grads["ff_w_down"].reshape(2, N_CHIPS, D_FF // N_CHIPS, D_MODEL).transpose(1, 0, 2, 3)
    return out


def kernel(x, meta_tokens, norm_mix, norm_ffn, norm_final, ev_w_in, ev_conv_a, ev_ln_a_g, ev_ln_a_b, ev_conv_b, ev_w_out, od_w_in, od_sinks, od_mu, od_w0, od_w2, od_a0, od_a2, od_g2, od_k_k, od_k_a, od_r_k, od_lnx_g, od_lnx_b, od_w_out, ff_w_up, ff_conv, ff_conv_b, ff_w_down, loss_target, m_meta_tokens, m_norm_mix, m_norm_ffn, m_norm_final, m_ev_w_in, m_ev_conv_a, m_ev_ln_a_g, m_ev_ln_a_b, m_ev_conv_b, m_ev_w_out, m_od_w_in, m_od_sinks, m_od_mu, m_od_w0, m_od_w2, m_od_a0, m_od_a2, m_od_g2, m_od_k_k, m_od_k_a, m_od_r_k, m_od_lnx_g, m_od_lnx_b, m_od_w_out, m_ff_w_up, m_ff_conv, m_ff_conv_b, m_ff_w_down, v_meta_tokens, v_norm_mix, v_norm_ffn, v_norm_final, v_ev_w_in, v_ev_conv_a, v_ev_ln_a_g, v_ev_ln_a_b, v_ev_conv_b, v_ev_w_out, v_od_w_in, v_od_sinks, v_od_mu, v_od_w0, v_od_w2, v_od_a0, v_od_a2, v_od_g2, v_od_k_k, v_od_k_a, v_od_r_k, v_od_lnx_g, v_od_lnx_b, v_od_w_out, v_ff_w_up, v_ff_conv, v_ff_conv_b, v_ff_w_down):
    wts = dict(meta_tokens=meta_tokens, norm_mix=norm_mix, norm_ffn=norm_ffn, norm_final=norm_final, ev_w_in=ev_w_in, ev_conv_a=ev_conv_a, ev_ln_a_g=ev_ln_a_g, ev_ln_a_b=ev_ln_a_b, ev_conv_b=ev_conv_b, ev_w_out=ev_w_out, od_w_in=od_w_in, od_sinks=od_sinks, od_mu=od_mu, od_w0=od_w0, od_w2=od_w2, od_a0=od_a0, od_a2=od_a2, od_g2=od_g2, od_k_k=od_k_k, od_k_a=od_k_a, od_r_k=od_r_k, od_lnx_g=od_lnx_g, od_lnx_b=od_lnx_b, od_w_out=od_w_out, ff_w_up=ff_w_up, ff_conv=ff_conv, ff_conv_b=ff_conv_b, ff_w_down=ff_w_down)
    mom = dict(meta_tokens=m_meta_tokens, norm_mix=m_norm_mix, norm_ffn=m_norm_ffn, norm_final=m_norm_final, ev_w_in=m_ev_w_in, ev_conv_a=m_ev_conv_a, ev_ln_a_g=m_ev_ln_a_g, ev_ln_a_b=m_ev_ln_a_b, ev_conv_b=m_ev_conv_b, ev_w_out=m_ev_w_out, od_w_in=m_od_w_in, od_sinks=m_od_sinks, od_mu=m_od_mu, od_w0=m_od_w0, od_w2=m_od_w2, od_a0=m_od_a0, od_a2=m_od_a2, od_g2=m_od_g2, od_k_k=m_od_k_k, od_k_a=m_od_k_a, od_r_k=m_od_r_k, od_lnx_g=m_od_lnx_g, od_lnx_b=m_od_lnx_b, od_w_out=m_od_w_out, ff_w_up=m_ff_w_up, ff_conv=m_ff_conv, ff_conv_b=m_ff_conv_b, ff_w_down=m_ff_w_down)
    var = dict(meta_tokens=v_meta_tokens, norm_mix=v_norm_mix, norm_ffn=v_norm_ffn, norm_final=v_norm_final, ev_w_in=v_ev_w_in, ev_conv_a=v_ev_conv_a, ev_ln_a_g=v_ev_ln_a_g, ev_ln_a_b=v_ev_ln_a_b, ev_conv_b=v_ev_conv_b, ev_w_out=v_ev_w_out, od_w_in=v_od_w_in, od_sinks=v_od_sinks, od_mu=v_od_mu, od_w0=v_od_w0, od_w2=v_od_w2, od_a0=v_od_a0, od_a2=v_od_a2, od_g2=v_od_g2, od_k_k=v_od_k_k, od_k_a=v_od_k_a, od_r_k=v_od_r_k, od_lnx_g=v_od_lnx_g, od_lnx_b=v_od_lnx_b, od_w_out=v_od_w_out, ff_w_up=v_ff_w_up, ff_conv=v_ff_conv, ff_conv_b=v_ff_conv_b, ff_w_down=v_ff_w_down)

    gathered = {}
    for tag, names, dtype in (("mats", BIG, MXU_DTYPE), ("small", SMALL, F32)):
        mine = _pack([wts[n] for n in names], dtype, 2 * 16)
        rows = mine.shape[0] // 2
        got = _all_gather_shards("gather_" + tag, mine.reshape(2, rows, PACK_W))
        parts = _unpack(got.reshape(N_CHIPS, 2 * rows * PACK_W), [wts[n].shape for n in names])
        gathered.update(dict(zip(names, parts)))
    w_full = _full_weights(gathered, wts)

    loss_local, grad_x, grads = _local_step(x[0], loss_target[0], w_full)
    loss = lax.psum(loss_local, ("x", "y", "c"))

    sg = _shard_grads(grads)
    per_chip = [jnp.concatenate([sg[n][k].reshape(-1) for n in SHARDED] + [sg[n].reshape(-1) for n in REPLICATED])
                for k in range(N_CHIPS)]
    n_el = per_chip[0].shape[0]
    unit = 2 * PACK_ROWS * PACK_W
    total = -(-n_el // unit) * unit
    rows = total // (2 * PACK_W)
    g2 = jnp.stack([jnp.pad(p, (0, total - n_el)).reshape(2, rows, PACK_W) for p in per_chip], axis=1)
    c_idx = lax.axis_index("c").astype(jnp.int32).reshape(1)
    from_sibling = _send_half_to_sibling("grads_to_sibling", g2)
    chip_sum = _pair_add("grads_pair_add", g2, from_sibling, c_idx)
    from_chips = _scatter_to_chips("grads_to_chips", chip_sum)
    half = _sum_chips("grads_chip_sum", from_chips)
    g_all = _join_halves("grads_join", half).reshape(2 * rows, PACK_W)

    order = SHARDED + REPLICATED
    packed = lambda d: jnp.pad(jnp.concatenate([d[n].reshape(-1) for n in order]), (0, total - n_el)).reshape(2 * rows, PACK_W)
    delta, new_m, new_v = _adamw("adamw", packed(wts), g_all, packed(mom), packed(var))
    shapes = [wts[n].shape for n in order]
    outs = {}
    for tag, arr in (("grad", g_all), ("delta", delta), ("new_m", new_m), ("new_v", new_v)):
        outs[tag] = dict(zip(order, _unpack(arr.reshape(-1), shapes)))
    return (loss, grad_x[None], *[outs["grad"][n] for n in WEIGHTS], *[outs["delta"][n] for n in WEIGHTS],
            *[outs["new_m"][n] for n in WEIGHTS], *[outs["new_v"][n] for n in WEIGHTS])
```

```python
import functools

import jax
import jax.numpy as jnp
from jax import lax
from jax.experimental import pallas as pl
from jax.experimental.pallas import tpu as pltpu

F32 = jnp.float32
BF16 = jnp.bfloat16
HI = lax.Precision.HIGHEST
MXU_DTYPE = BF16

D_MODEL = 1024
N_META = 16
RMS_EPS = 1e-6
LN_EPS = 1e-5
D_A = 512
CONV_A_WIDTH = 31
CONV_B_WIDTH = 3
HEAD_DIM = 64
N_Q_HEADS = 8
N_KV_HEADS = 2
GQA_GROUP = 4
D_ATT = 512
D_KV = 128
BLOCK = 128
ROPE_THETA = 10000.0
D_R = 512
N_R_HEADS = 8
LORA_W = 64
LORA_A = 64
LORA_G = 128
RWKV_GN_EPS = 64e-5
RWKV_COLS = 3 * D_R + LORA_W + LORA_A + LORA_G
D_FF = 2816
NEG_INF = -1e30
ADAM_LR = 0.001
ADAM_B1 = 0.9
ADAM_B2 = 0.999
ADAM_EPS = 1e-08
ADAM_WD = 0.01
ADAM_STEP = 10

N_CHIPS = 4
LANES = 128
CONV_PAD = 32
VMEM_LIMIT_V7X = 56 * 1024 * 1024
MESH = pl.DeviceIdType.MESH


def _cparams(sem=None):
    return pltpu.CompilerParams(dimension_semantics=sem, vmem_limit_bytes=VMEM_LIMIT_V7X)


def _row_tile(t, cap):
    for d in range(min(t, cap), 0, -1):
        if t % d == 0 and d % 16 == 0:
            return d
    return t


def _chunk_len(t):
    for d in (64, 48, 32, 16, 8):
        if t % d == 0:
            return d
    raise ValueError(t)


def _call(fn, name, grid, ins, outs, acc_axis=None, sem=None):
    n_in, n_out = len(ins), len(outs)

    def body(*refs):
        vals = fn(*[r[...] for r in refs[:n_in]])
        if not isinstance(vals, (tuple, list)):
            vals = (vals,)
        for r, v, o in zip(refs[n_in:n_in + n_out], vals, outs):
            if o[3]:
                first = pl.program_id(acc_axis) == 0

                @pl.when(first)
                def _(r=r, v=v):
                    r[...] = v

                @pl.when(jnp.logical_not(first))
                def _(r=r, v=v):
                    r[...] += v
            else:
                r[...] = v

    res = pl.pallas_call(
        body, name=name, grid=grid,
        in_specs=[pl.BlockSpec(b, m) for _, b, m in ins],
        out_specs=[pl.BlockSpec(o[1], o[2]) for o in outs],
        out_shape=[jax.ShapeDtypeStruct(o[0], F32) for o in outs],
        compiler_params=_cparams(sem),
    )(*[a for a, _, _ in ins])
    return res if n_out > 1 else res[0]


def _matmul(name, a, b, *, dims, grid, a_spec, b_spec, o_shape, o_spec, acc_shape, nk, k_axis,
            add=None, add_spec=None):
    def body(*refs):
        if add is None:
            a_ref, b_ref, o_ref, acc = refs
        else:
            a_ref, b_ref, add_ref, o_ref, acc = refs
        k = pl.program_id(k_axis)

        @pl.when(k == 0)
        def _():
            if add is None:
                acc[...] = jnp.zeros(acc.shape, F32)
            else:
                acc[...] = add_ref[...]

        acc[...] += lax.dot_general(a_ref[...].astype(MXU_DTYPE), b_ref[...].astype(MXU_DTYPE), dims,
                                    preferred_element_type=F32)

        @pl.when(k == nk - 1)
        def _():
            o_ref[...] = acc[...]

    args = [a, b] + ([] if add is None else [add])
    specs = [a_spec, b_spec] + ([] if add is None else [add_spec])
    return pl.pallas_call(
        body, name=name, grid=grid, in_specs=specs, out_specs=o_spec,
        out_shape=jax.ShapeDtypeStruct(o_shape, F32),
        scratch_shapes=[pltpu.VMEM(acc_shape, F32)],
        compiler_params=_cparams(None),
    )(*args)


_NN = (((1,), (0,)), ((), ()))
_NT = (((1,), (1,)), ((), ()))
_TN = (((0,), (0,)), ((), ()))


def _mm_cs(name, x, wg, l, tm):
    t, k = x.shape
    s, _, _, n = wg.shape
    return _matmul(name, x, wg, dims=_NN, grid=(s, t // tm, 1),
                   a_spec=pl.BlockSpec((tm, k), lambda j, i, kk: (i, 0)),
                   b_spec=pl.BlockSpec((None, None, k, n), lambda j, i, kk: (j, l, 0, 0)),
                   o_shape=(t, s * n), o_spec=pl.BlockSpec((tm, n), lambda j, i, kk: (i, j)),
                   acc_shape=(tm, n), nk=1, k_axis=2)


def _mm_full(name, x, w, l, tm, tk, add=None):
    t, k = x.shape
    n = w.shape[2]
    nk = k // tk
    return _matmul(name, x, w, dims=_NN, grid=(t // tm, 1, nk),
                   a_spec=pl.BlockSpec((tm, tk), lambda i, j, kk: (i, kk)),
                   b_spec=pl.BlockSpec((None, tk, n), lambda i, j, kk: (l, kk, 0)),
                   o_shape=(t, n), o_spec=pl.BlockSpec((tm, n), lambda i, j, kk: (i, 0)),
                   acc_shape=(tm, n), nk=nk, k_axis=2,
                   add=add, add_spec=pl.BlockSpec((tm, n), lambda i, j, kk: (i, 0)))


def _mm_nt_cs(name, dy, wg, l, tm, add=None):
    t = dy.shape[0]
    s, _, k, n = wg.shape
    return _matmul(name, dy, wg, dims=_NT, grid=(t // tm, 1, s),
                   a_spec=pl.BlockSpec((tm, n), lambda i, j, kk: (i, kk)),
                   b_spec=pl.BlockSpec((None, None, k, n), lambda i, j, kk: (kk, l, 0, 0)),
                   o_shape=(t, k), o_spec=pl.BlockSpec((tm, k), lambda i, j, kk: (i, 0)),
                   acc_shape=(tm, k), nk=s, k_axis=2,
                   add=add, add_spec=pl.BlockSpec((tm, k), lambda i, j, kk: (i, 0)))


def _mm_nt_full(name, dy, w, l, tm, tko):
    t, n = dy.shape
    k = w.shape[1]
    return _matmul(name, dy, w, dims=_NT, grid=(t // tm, k // tko, 1),
                   a_spec=pl.BlockSpec((tm, n), lambda i, j, kk: (i, 0)),
                   b_spec=pl.BlockSpec((None, tko, n), lambda i, j, kk: (l, j, 0)),
                   o_shape=(t, k), o_spec=pl.BlockSpec((tm, tko), lambda i, j, kk: (i, j)),
                   acc_shape=(tm, tko), nk=1, k_axis=2)


def _mm_tn_cs(name, x, dy, s, tk):
    t, k = x.shape
    n = dy.shape[1] // s
    nk = t // tk
    return _matmul(name, x, dy, dims=_TN, grid=(s, 1, nk),
                   a_spec=pl.BlockSpec((tk, k), lambda j, i, kk: (kk, 0)),
                   b_spec=pl.BlockSpec((tk, n), lambda j, i, kk: (kk, j)),
                   o_shape=(s, k, n), o_spec=pl.BlockSpec((None, k, n), lambda j, i, kk: (j, 0, 0)),
                   acc_shape=(k, n), nk=nk, k_axis=2)


def _mm_tn_full(name, y, dh, tk, tko):
    t, k = y.shape
    n = dh.shape[1]
    nk = t // tk
    return _matmul(name, y, dh, dims=_TN, grid=(k // tko, 1, nk),
                   a_spec=pl.BlockSpec((tk, tko), lambda j, i, kk: (kk, j)),
                   b_spec=pl.BlockSpec((tk, n), lambda j, i, kk: (kk, 0)),
                   o_shape=(k, n), o_spec=pl.BlockSpec((tko, n), lambda j, i, kk: (j, 0)),
                   acc_shape=(tko, n), nk=nk, k_axis=2)


def _sigmoid(x):
    return 1.0 / (1.0 + jnp.exp(-x))


def _rms_fwd(name, h, g, tr):
    t, d = h.shape

    def fn(hv, gv):
        r = lax.rsqrt(jnp.mean(hv * hv, axis=-1, keepdims=True) + RMS_EPS)
        return hv * r * gv

    return _call(fn, name, (t // tr,), [(h, (tr, d), lambda i: (i, 0)), (g, (1, d), lambda i: (0, 0))],
                 [((t, d), (tr, d), lambda i: (i, 0), False)])


def _rms_bwd(name, h, g, dhn, dh, tr):
    t, d = h.shape

    def fn(hv, gv, dy, dh_in):
        r = lax.rsqrt(jnp.mean(hv * hv, axis=-1, keepdims=True) + RMS_EPS)
        xh = hv * r
        dg = jnp.sum(dy * xh, axis=0, keepdims=True)
        dxh = dy * gv
        dx = r * (dxh - xh * jnp.mean(dxh * xh, axis=-1, keepdims=True))
        return dh_in + dx, dg

    row = lambda i: (i, 0)
    return _call(fn, name, (t // tr,),
                 [(h, (tr, d), row), (g, (1, d), lambda i: (0, 0)), (dhn, (tr, d), row), (dh, (tr, d), row)],
                 [((t, d), (tr, d), row, False), ((1, d), (1, d), lambda i: (0, 0), True)], acc_axis=0)


def _final_loss(name, h, g, tgt, tr):
    t, d = h.shape

    def fn(hv, gv, tv):
        r = lax.rsqrt(jnp.mean(hv * hv, axis=-1, keepdims=True) + RMS_EPS)
        xh = hv * r
        row = pl.program_id(0) * tr + lax.broadcasted_iota(jnp.int32, (tr, 1), 0)
        e = jnp.where(row >= N_META, xh * gv - tv, 0.0)
        loss = jnp.broadcast_to(0.5 * jnp.sum(jnp.sum(e * e, axis=-1, keepdims=True), axis=0, keepdims=True) / d,
                                (8, LANES))
        dout = e / d
        dg = jnp.sum(dout * xh, axis=0, keepdims=True)
        dxh = dout * gv
        dx = r * (dxh - xh * jnp.mean(dxh * xh, axis=-1, keepdims=True))
        return loss, dx, dg

    row = lambda i: (i, 0)
    fix = lambda i: (0, 0)
    return _call(fn, name, (t // tr,), [(h, (tr, d), row), (g, (1, d), fix), (tgt, (tr, d), row)],
                 [((8, LANES), (8, LANES), fix, True), ((t, d), (tr, d), row, False), ((1, d), (1, d), fix, True)],
                 acc_axis=0)


def _silu_ln(uc, g, b):
    mu = jnp.mean(uc, axis=-1, keepdims=True)
    xc = uc - mu
    rs = lax.rsqrt(jnp.mean(xc * xc, axis=-1, keepdims=True) + LN_EPS)
    ln = xc * rs * g + b
    return ln * _sigmoid(ln)


def _even_ln_fwd(name, uc, g, b, tr):
    t, d = uc.shape
    row, fix = (lambda i: (i, 0)), (lambda i: (0, 0))
    return _call(_silu_ln, name, (t // tr,), [(uc, (tr, d), row), (g, (1, d), fix), (b, (1, d), fix)],
                 [((t, d), (tr, d), row, False)])


def _even_ln_bwd(name, uc, g, b, dy, dy_col, tr):
    t, d = uc.shape

    def fn(ucv, gv, bv, dyv):
        mu = jnp.mean(ucv, axis=-1, keepdims=True)
        xc = ucv - mu
        rs = lax.rsqrt(jnp.mean(xc * xc, axis=-1, keepdims=True) + LN_EPS)
        xh = xc * rs
        ln = xh * gv + bv
        s = _sigmoid(ln)
        dln = dyv * (s * (1.0 + ln * (1.0 - s)))
        dg = jnp.sum(dln * xh, axis=0, keepdims=True)
        db = jnp.sum(dln, axis=0, keepdims=True)
        dxh = dln * gv
        duc = rs * (dxh - jnp.mean(dxh, axis=-1, keepdims=True) - xh * jnp.mean(dxh * xh, axis=-1, keepdims=True))
        return duc, dg, db

    row, fix = (lambda i: (i, 0)), (lambda i: (0, 0))
    return _call(fn, name, (t // tr,),
                 [(uc, (tr, d), row), (g, (1, d), fix), (b, (1, d), fix), (dy, (tr, d), lambda i: (i, dy_col))],
                 [((t, d), (tr, d), row, False), ((1, d), (1, d), fix, True), ((1, d), (1, d), fix, True)], acc_axis=0)


def _conv_fwd(xp, w_ref, width, t):
    acc = None
    for j in range(width):
        term = xp[pl.ds(CONV_PAD - (width - 1) + j, t), :] * w_ref[pl.ds(j, 1), :]
        acc = term if acc is None else acc + term
    return acc


def _conv_bwd_in(dyp, w_ref, width, t):
    acc = None
    for j in range(width):
        term = dyp[pl.ds(width - 1 - j, t), :] * w_ref[pl.ds(j, 1), :]
        acc = term if acc is None else acc + term
    return acc


def _conv_bwd_w(dy, xp, dw_ref, width, t):
    for j in range(width):
        dw_ref[pl.ds(j, 1), :] = jnp.sum(dy * xp[pl.ds(CONV_PAD - (width - 1) + j, t), :], axis=0, keepdims=True)


def _store_front(xp, x, t):
    xp[pl.ds(0, CONV_PAD), :] = jnp.zeros((CONV_PAD, LANES), F32)
    xp[pl.ds(CONV_PAD, t), :] = x


def _store_back(xp, x, t):
    xp[pl.ds(0, t), :] = x
    xp[pl.ds(t, CONV_PAD), :] = jnp.zeros((CONV_PAD, LANES), F32)


def _col_call(body, name, ncol, ins, outs, t, n_scratch):
    def spec(rows, off):
        return pl.BlockSpec((rows, LANES), lambda j, off=off: (0, j + off))

    res = pl.pallas_call(
        body, name=name, grid=(ncol,),
        in_specs=[spec(r, off) for _, r, off in ins],
        out_specs=[spec(r, 0) for r, _ in outs],
        out_shape=[jax.ShapeDtypeStruct((r, c), F32) for r, c in outs],
        scratch_shapes=[pltpu.VMEM((t + CONV_PAD, LANES), F32) for _ in range(n_scratch)],
        compiler_params=_cparams(None),
    )(*[a for a, _, _ in ins])
    return res


def _even_col_fwd(name, p, conv_a, conv_b):
    t = p.shape[0]
    nc = D_A // LANES

    def body(av, ag, gb, gc, xi, ca, cb, uc_ref, yb_ref, xp):
        _store_front(xp, av[...] * _sigmoid(ag[...]), t)
        uc_ref[...] = _conv_fwd(xp, ca, CONV_A_WIDTH, t)
        _store_front(xp, gc[...] * xi[...], t)
        yb_ref[...] = gb[...] * _conv_fwd(xp, cb, CONV_B_WIDTH, t)

    ins = [(p, t, k * nc) for k in range(5)] + [(conv_a, CONV_A_WIDTH, 0), (conv_b, CONV_B_WIDTH, 0)]
    return _col_call(body, name, nc, ins, [(t, D_A), (t, D_A)], t, 1)


def _even_col_bwd(name, p, duc, dy, conv_a, conv_b):
    t = p.shape[0]
    nc = D_A // LANES

    def body(av, ag, gb, gc, xi, duc_ref, dyb_ref, ca, cb, dav, dag, dgb, dgc, dxi, dca, dcb, xp, dyp):
        sig = _sigmoid(ag[...])
        _store_front(xp, av[...] * sig, t)
        _store_back(dyp, duc_ref[...], t)
        _conv_bwd_w(duc_ref[...], xp, dca, CONV_A_WIDTH, t)
        du = _conv_bwd_in(dyp, ca, CONV_A_WIDTH, t)
        dav[...] = du * sig
        dag[...] = du * av[...] * sig * (1.0 - sig)
        _store_front(xp, gc[...] * xi[...], t)
        zc = _conv_fwd(xp, cb, CONV_B_WIDTH, t)
        dgb[...] = dyb_ref[...] * zc
        dzc = dyb_ref[...] * gb[...]
        _conv_bwd_w(dzc, xp, dcb, CONV_B_WIDTH, t)
        _store_back(dyp, dzc, t)
        dz = _conv_bwd_in(dyp, cb, CONV_B_WIDTH, t)
        dgc[...] = dz * xi[...]
        dxi[...] = dz * gc[...]

    ins = ([(p, t, k * nc) for k in range(5)] + [(duc, t, 0), (dy, t, nc)]
           + [(conv_a, CONV_A_WIDTH, 0), (conv_b, CONV_B_WIDTH, 0)])
    outs = [(t, D_A)] * 5 + [(CONV_A_WIDTH, D_A), (CONV_B_WIDTH, D_A)]
    return _col_call(body, name, nc, ins, outs, t, 2)


def _ffn_col_fwd(name, u, conv, bias):
    t = u.shape[0]
    nc = D_FF // LANES

    def body(g_ref, v_ref, cw, b_ref, a_ref, xp):
        _store_front(xp, g_ref[...], t)
        gc = _conv_fwd(xp, cw, CONV_B_WIDTH, t) + b_ref[...]
        a_ref[...] = gc * _sigmoid(gc) * v_ref[...]

    ins = [(u, t, 0), (u, t, nc), (conv, CONV_B_WIDTH, 0), (bias, 1, 0)]
    return _col_call(body, name, nc, ins, [(t, D_FF)], t, 1)[0]


def _ffn_col_bwd(name, u, da, conv, bias):
    t = u.shape[0]
    nc = D_FF // LANES

    def body(g_ref, v_ref, da_ref, cw, b_ref, du_ref, dcw, db_ref, xp, dyp):
        _store_front(xp, g_ref[...], t)
        gc = _conv_fwd(xp, cw, CONV_B_WIDTH, t) + b_ref[...]
        s = _sigmoid(gc)

        @pl.when(pl.program_id(1) == 0)
        def _():
            dgc = da_ref[...] * v_ref[...] * (s * (1.0 + gc * (1.0 - s)))
            db_ref[...] = jnp.sum(dgc, axis=0, keepdims=True)
            _conv_bwd_w(dgc, xp, dcw, CONV_B_WIDTH, t)
            _store_back(dyp, dgc, t)
            du_ref[...] = _conv_bwd_in(dyp, cw, CONV_B_WIDTH, t)

        @pl.when(pl.program_id(1) == 1)
        def _():
            du_ref[...] = da_ref[...] * gc * s

    col = lambda rows, off: pl.BlockSpec((rows, LANES), lambda j, p: (0, j + off))
    return pl.pallas_call(
        body, name=name, grid=(nc, 2),
        in_specs=[col(t, 0), col(t, nc), col(t, 0), col(CONV_B_WIDTH, 0), col(1, 0)],
        out_specs=[pl.BlockSpec((t, LANES), lambda j, p: (0, j + nc * p)), col(CONV_B_WIDTH, 0), col(1, 0)],
        out_shape=[jax.ShapeDtypeStruct((t, 2 * D_FF), F32), jax.ShapeDtypeStruct((CONV_B_WIDTH, D_FF), F32),
                   jax.ShapeDtypeStruct((1, D_FF), F32)],
        scratch_shapes=[pltpu.VMEM((t + CONV_PAD, LANES), F32) for _ in range(2)],
        compiler_params=_cparams(None),
    )(u, u, da, conv, bias)


def _shift_fwd(name, p, col0, mu):
    t = p.shape[0]

    def body(x_ref, mu_ref, o_ref, xp):
        _store_front(xp, x_ref[...], t)
        prev = xp[pl.ds(CONV_PAD - 1, t), :]
        o_ref[...] = x_ref[...] + (prev - x_ref[...]) * mu_ref[...]

    return _col_call(body, name, RWKV_COLS // LANES, [(p, t, col0), (mu, 1, 0)], [(t, RWKV_COLS)], t, 1)[0]


def _shift_bwd(name, p, col0, mu, dprs):
    t = p.shape[0]

    def body(x_ref, mu_ref, d_ref, dx_ref, dmu_ref, xp, dyp):
        _store_front(xp, x_ref[...], t)
        prev = xp[pl.ds(CONV_PAD - 1, t), :]
        dmu_ref[...] = jnp.sum(d_ref[...] * (prev - x_ref[...]), axis=0, keepdims=True)
        dm = d_ref[...] * mu_ref[...]
        _store_back(dyp, dm, t)
        dx_ref[...] = d_ref[...] - dm + dyp[pl.ds(1, t), :]

    ins = [(p, t, col0), (mu, 1, 0), (dprs, t, 0)]
    return _col_call(body, name, RWKV_COLS // LANES, ins, [(t, RWKV_COLS), (1, RWKV_COLS)], t, 2)


def _dotm(a, b):
    return jnp.dot(a.astype(MXU_DTYPE), b.astype(MXU_DTYPE), preferred_element_type=F32)


def _doth(a, b, dims=_NN):
    return lax.dot_general(a, b, dims, precision=HI, preferred_element_type=F32)


def _softplus(x):
    return jnp.where(x > 0, x, 0.0) + jnp.log(1.0 + jnp.exp(jnp.where(x > 0, -x, x)))


def _rwkv_pre(k, xl, gd, w0, w2p, a0, a2p, g2, k_k, k_a, seg):
    z = w0 + _dotm(jnp.tanh(xl), w2p)
    lw = -jnp.exp(-_softplus(-z) - 0.5)
    alpha = _sigmoid(a0 + _dotm(xl, a2p))
    g = _dotm(_sigmoid(gd), g2)
    kk = k * k_k
    kk = kk / jnp.maximum(jnp.sqrt(_doth(kk * kk, seg)), 1e-12)
    k2 = k * (1.0 + (alpha - 1.0) * k_a)
    return lw, k2, -kk, kk * alpha, g


def _rwkv_post(y, r, k2, v, g, lnx_g, lnx_b, r_k, seg):
    mean = _doth(y, seg) * (1.0 / HEAD_DIM)
    yc = y - mean
    var = _doth(yc * yc, seg) * (1.0 / HEAD_DIM)
    yo = yc * lax.rsqrt(var + RWKV_GN_EPS) * lnx_g + lnx_b
    bonus = _doth(r * k2 * r_k, seg) * v
    return (yo + bonus) * g


def _rwkv_pre_fwd(name, prs, prm, seg, tr):
    t = prs.shape[0]
    row = lambda i: (i, 0)
    fix = lambda i: (0, 0)
    ins = [(prs, (tr, D_R), lambda i: (i, 1)), (prs, (tr, LANES), lambda i: (i, 12)), (prs, (tr, LANES), lambda i: (i, 13)),
           (prm["w0"], (1, D_R), fix), (prm["w2p"], (LANES, D_R), fix), (prm["a0"], (1, D_R), fix),
           (prm["a2p"], (LANES, D_R), fix), (prm["g2"], (LANES, D_R), fix), (prm["k_k"], (1, D_R), fix),
           (prm["k_a"], (1, D_R), fix), (seg, (D_R, D_R), fix)]
    return _call(_rwkv_pre, name, (t // tr,), ins, [((t, D_R), (tr, D_R), row, False)] * 5)


def _rwkv_pre_bwd(name, prs, prm, seg, cts, tr):
    t = prs.shape[0]

    def fn(k, xl, gd, w0, w2p, a0, a2p, g2, k_k, k_a, segv, *ct):
        _, vjp = jax.vjp(lambda *a: _rwkv_pre(*a, segv), k, xl, gd, w0, w2p, a0, a2p, g2, k_k, k_a)
        return vjp(tuple(ct))

    row = lambda i: (i, 0)
    fix = lambda i: (0, 0)
    ins = [(prs, (tr, D_R), lambda i: (i, 1)), (prs, (tr, LANES), lambda i: (i, 12)), (prs, (tr, LANES), lambda i: (i, 13)),
           (prm["w0"], (1, D_R), fix), (prm["w2p"], (LANES, D_R), fix), (prm["a0"], (1, D_R), fix),
           (prm["a2p"], (LANES, D_R), fix), (prm["g2"], (LANES, D_R), fix), (prm["k_k"], (1, D_R), fix),
           (prm["k_a"], (1, D_R), fix), (seg, (D_R, D_R), fix)] + [(c, (tr, D_R), row) for c in cts]
    outs = [((t, D_R), (tr, D_R), row, False), ((t, LANES), (tr, LANES), row, False), ((t, LANES), (tr, LANES), row, False),
            ((1, D_R), (1, D_R), fix, True), ((LANES, D_R), (LANES, D_R), fix, True), ((1, D_R), (1, D_R), fix, True),
            ((LANES, D_R), (LANES, D_R), fix, True), ((LANES, D_R), (LANES, D_R), fix, True),
            ((1, D_R), (1, D_R), fix, True), ((1, D_R), (1, D_R), fix, True)]
    return _call(fn, name, (t // tr,), ins, outs, acc_axis=0)


def _rwkv_post_ins(y, prs, k2, g, prm, seg, tr):
    row = lambda i: (i, 0)
    fix = lambda i: (0, 0)
    return [(y, (tr, D_R), row), (prs, (tr, D_R), row), (k2, (tr, D_R), row), (prs, (tr, D_R), lambda i: (i, 2)),
            (g, (tr, D_R), row), (prm["lnx_g"], (1, D_R), fix), (prm["lnx_b"], (1, D_R), fix), (prm["r_k"], (1, D_R), fix),
            (seg, (D_R, D_R), fix)]


def _rwkv_post_fwd(name, y, prs, k2, g, prm, seg, tr):
    t = y.shape[0]
    return _call(_rwkv_post, name, (t // tr,), _rwkv_post_ins(y, prs, k2, g, prm, seg, tr),
                 [((t, D_R), (tr, D_R), lambda i: (i, 0), False)])


def _rwkv_post_bwd(name, y, prs, k2, g, prm, seg, dy, dy_col, tr):
    t = y.shape[0]

    def fn(yv, r, k2v, v, gv, lg, lb, rk, segv, ct):
        _, vjp = jax.vjp(lambda *a: _rwkv_post(*a, segv), yv, r, k2v, v, gv, lg, lb, rk)
        return vjp(ct)

    row = lambda i: (i, 0)
    fix = lambda i: (0, 0)
    ins = _rwkv_post_ins(y, prs, k2, g, prm, seg, tr) + [(dy, (tr, D_R), lambda i: (i, dy_col))]
    outs = [((t, D_R), (tr, D_R), row, False)] * 5 + [((1, D_R), (1, D_R), fix, True)] * 3
    return _call(fn, name, (t // tr,), ins, outs, acc_axis=0)


def _wkv_chunk(st0, r, lw, k, v, a, b):
    c = r.shape[0]
    ii = lax.broadcasted_iota(jnp.int32, (c, c), 0)
    jj = lax.broadcasted_iota(jnp.int32, (c, c), 1)
    incl = ii >= jj
    strict = ii > jj
    cum = _doth(incl.astype(F32), lw)
    cum_x = cum - lw
    tot = _doth(jnp.ones((c, c), F32), lw)
    tot_col = _doth(lw, jnp.ones((c, HEAD_DIM), F32), _TN)
    e_inv = jnp.exp(-cum)
    a_t = a * jnp.exp(cum_x)
    r_t = r * jnp.exp(cum)
    b_t = b * e_inv
    k_t = k * e_inv
    m_ab = jnp.where(strict, _doth(a_t, b_t, _NT), 0.0)
    m_ak = jnp.where(strict, _doth(a_t, k_t, _NT), 0.0)
    n_rb = jnp.where(incl, _doth(r_t, b_t, _NT), 0.0)
    n_rk = jnp.where(incl, _doth(r_t, k_t, _NT), 0.0)
    u = _doth(a_t, st0) + _doth(m_ak, v)
    mp = m_ab
    steps = max(1, (c - 1).bit_length())
    for s in range(steps):
        u = u + _doth(mp, u)
        if s + 1 < steps:
            mp = _doth(mp, mp)
    y = _doth(r_t, st0) + _doth(n_rb, u) + _doth(n_rk, v)
    dec = jnp.exp(tot - cum)
    st1 = jnp.exp(tot_col) * st0 + _doth(b * dec, u, _TN) + _doth(k * dec, v, _TN)
    return y, st1


WKV_HEADS_PER_STEP = 2


def _wkv_fwd(name, xs):
    h, t, n = xs[0].shape
    c = _chunk_len(t)
    nc = t // c
    hb = WKV_HEADS_PER_STEP

    def body(r, lw, k, v, a, b, y_ref, st_ref, state):
        @pl.when(pl.program_id(1) == 0)
        def _():
            state[...] = jnp.zeros(state.shape, F32)

        for i in range(hb):
            st0 = state[i]
            st_ref[i] = st0
            y, st1 = _wkv_chunk(st0, r[i], lw[i], k[i], v[i], a[i], b[i])
            y_ref[i] = y
            state[i] = st1

    seq = pl.BlockSpec((hb, c, n), lambda g, j: (g, j, 0))
    return pl.pallas_call(
        body, name=name, grid=(h // hb, nc), in_specs=[seq] * 6,
        out_specs=[seq, pl.BlockSpec((hb, None, n, n), lambda g, j: (g, j, 0, 0))],
        out_shape=[jax.ShapeDtypeStruct((h, t, n), F32), jax.ShapeDtypeStruct((h, nc, n, n), F32)],
        scratch_shapes=[pltpu.VMEM((hb, n, n), F32)],
        compiler_params=_cparams(None),
    )(*xs)


def _wkv_bwd(name, xs, st, dy):
    h, t, n = xs[0].shape
    c = _chunk_len(t)
    nc = t // c
    hb = WKV_HEADS_PER_STEP

    def body(r, lw, k, v, a, b, st_ref, dy_ref, dr, dlw, dk, dv, da, db, dstate):
        @pl.when(pl.program_id(1) == 0)
        def _():
            dstate[...] = jnp.zeros(dstate.shape, F32)

        for i in range(hb):
            _, vjp = jax.vjp(_wkv_chunk, st_ref[i], r[i], lw[i], k[i], v[i], a[i], b[i])
            dst0, *dxs = vjp((dy_ref[i], dstate[i]))
            for ref, val in zip((dr, dlw, dk, dv, da, db), dxs):
                ref[i] = val
            dstate[i] = dst0

    seq = pl.BlockSpec((hb, c, n), lambda g, j: (g, nc - 1 - j, 0))
    return pl.pallas_call(
        body, name=name, grid=(h // hb, nc),
        in_specs=[seq] * 6 + [pl.BlockSpec((hb, None, n, n), lambda g, j: (g, nc - 1 - j, 0, 0)), seq],
        out_specs=[seq] * 6,
        out_shape=[jax.ShapeDtypeStruct((h, t, n), F32)] * 6,
        scratch_shapes=[pltpu.VMEM((hb, n, n), F32)],
        compiler_params=_cparams(None),
    )(*xs, st, dy)


def _rope(x, cos, sin, rot):
    return x * cos + _doth(x, rot) * sin


def _attn_block(nb, q4, kp, kc, km, vp, vc, vm, sk, cq, sq, cp, sp, cm, sm, rot):
    scale = HEAD_DIM ** -0.5
    kpr = _rope(kp, cp, sp, rot).astype(MXU_DTYPE)
    kcr = _rope(kc, cq, sq, rot).astype(MXU_DTYPE)
    kmr = _rope(km, cm, sm, rot).astype(MXU_DTYPE)
    i = lax.broadcasted_iota(jnp.int32, (BLOCK, BLOCK), 0)
    j = lax.broadcasted_iota(jnp.int32, (BLOCK, BLOCK), 1)
    nbv = jnp.zeros((BLOCK, BLOCK), jnp.int32) + nb
    ok_p = (j > i) & (nbv >= 2)
    ok_c = (j <= i) & (nbv >= 1)
    ok_m = (j >= BLOCK - N_META) & ((nbv >= 1) | (j <= i))
    outs = []
    for h in range(GQA_GROUP):
        qr = _rope(q4[h], cq, sq, rot).astype(MXU_DTYPE)
        ntdot = lambda kk: lax.dot_general(qr, kk, _NT, preferred_element_type=F32) * scale
        s_p = jnp.where(ok_p, ntdot(kpr), NEG_INF)
        s_c = jnp.where(ok_c, ntdot(kcr), NEG_INF)
        s_m = jnp.where(ok_m, ntdot(kmr), NEG_INF)
        rmax = lambda s: jnp.max(s, axis=-1, keepdims=True)
        m = lax.stop_gradient(jnp.maximum(jnp.maximum(rmax(s_p), rmax(s_c)), jnp.maximum(rmax(s_m), sk[h])))
        e_p, e_c, e_m = jnp.exp(s_p - m), jnp.exp(s_c - m), jnp.exp(s_m - m)
        rsum = lambda e: jnp.sum(e, axis=-1, keepdims=True)
        inv = 1.0 / (rsum(e_p) + rsum(e_c) + rsum(e_m) + jnp.exp(sk[h] - m))
        outs.append(_dotm(e_p * inv, vp) + _dotm(e_c * inv, vc) + _dotm(e_m * inv, vm))
    return tuple(outs)


def _attn_specs():
    cur = lambda g, n: (g, n, 0)
    prev = lambda g, n: (g, jnp.maximum(n - 1, 0), 0)
    meta = lambda g, n: (g, 0, 0)
    kv = lambda m: pl.BlockSpec((None, BLOCK, HEAD_DIM), m)
    tab = lambda m: pl.BlockSpec((BLOCK, HEAD_DIM), m)
    tcur, tprev, tmeta = (lambda g, n: (n, 0)), (lambda g, n: (jnp.maximum(n - 1, 0), 0)), (lambda g, n: (0, 0))
    qspec = pl.BlockSpec((GQA_GROUP, BLOCK, HEAD_DIM), cur)
    sspec = pl.BlockSpec((GQA_GROUP, 8, LANES), meta)
    specs = [qspec, kv(prev), kv(cur), kv(meta), kv(prev), kv(cur), kv(meta), sspec,
             tab(tcur), tab(tcur), tab(tprev), tab(tprev), tab(tmeta), tab(tmeta),
             pl.BlockSpec((HEAD_DIM, HEAD_DIM), lambda g, n: (0, 0))]
    return specs, qspec, sspec, kv


def _attn_args(q, k, v, sinks_b, cos, sin, rot):
    return (q, k, k, k, v, v, v, sinks_b, cos, sin, cos, sin, cos, sin, rot)


def _attn_fwd(name, q, k, v, sinks_b, cos, sin, rot):
    tp = q.shape[1]
    specs, qspec, _, _ = _attn_specs()

    def body(q_ref, kp, kc, km, vp, vc, vm, s_ref, cq, sq, cp, sp, cm, sm, rot_ref, o_ref):
        q4 = tuple(q_ref[h] for h in range(GQA_GROUP))
        sk = tuple(s_ref[h][0:1, 0:1] for h in range(GQA_GROUP))
        outs = _attn_block(pl.program_id(1), q4, kp[...], kc[...], km[...], vp[...], vc[...], vm[...], sk,
                           cq[...], sq[...], cp[...], sp[...], cm[...], sm[...], rot_ref[...])
        for h in range(GQA_GROUP):
            o_ref[h] = outs[h]

    return pl.pallas_call(
        body, name=name, grid=(N_KV_HEADS, tp // BLOCK), in_specs=specs, out_specs=qspec,
        out_shape=jax.ShapeDtypeStruct(q.shape, F32), compiler_params=_cparams(None),
    )(*_attn_args(q, k, v, sinks_b, cos, sin, rot))


def _attn_bwd(name, q, k, v, sinks_b, cos, sin, rot, do):
    tp = q.shape[1]
    nb = tp // BLOCK
    specs, qspec, sspec, kv = _attn_specs()

    def body(q_ref, kp, kc, km, vp, vc, vm, s_ref, cq, sq, cp, sp, cm, sm, rot_ref, do_ref,
             dq_ref, dkp, dkc, dvp, dvc, dkm, dvm, ds_ref):
        n = pl.program_id(1)
        q4 = tuple(q_ref[h] for h in range(GQA_GROUP))
        sk = tuple(s_ref[h][0:1, 0:1] for h in range(GQA_GROUP))
        tabs = (cq[...], sq[...], cp[...], sp[...], cm[...], sm[...], rot_ref[...])
        _, vjp = jax.vjp(lambda *a: _attn_block(n, *a, *tabs), q4, kp[...], kc[...], km[...], vp[...], vc[...], vm[...], sk)
        dq4, gkp, gkc, gkm, gvp, gvc, gvm, dsk = vjp(tuple(do_ref[h] for h in range(GQA_GROUP)))
        dkp[...] = gkp
        dkc[...] = gkc
        dvp[...] = gvp
        dvc[...] = gvc
        for h in range(GQA_GROUP):
            dq_ref[h] = dq4[h]

        @pl.when(n == 0)
        def _():
            dkm[...] = gkm
            dvm[...] = gvm
            for h in range(GQA_GROUP):
                ds_ref[h] = jnp.broadcast_to(dsk[h], (8, LANES))

        @pl.when(n != 0)
        def _():
            dkm[...] += gkm
            dvm[...] += gvm
            for h in range(GQA_GROUP):
                ds_ref[h] += jnp.broadcast_to(dsk[h], (8, LANES))

    part = pl.BlockSpec((None, None, BLOCK, HEAD_DIM), lambda g, n: (g, n, 0, 0))
    part_shape = jax.ShapeDtypeStruct((N_KV_HEADS, nb, BLOCK, HEAD_DIM), F32)
    meta_shape = jax.ShapeDtypeStruct((N_KV_HEADS, BLOCK, HEAD_DIM), F32)
    return pl.pallas_call(
        body, name=name, grid=(N_KV_HEADS, nb), in_specs=specs + [qspec],
        out_specs=[qspec, part, part, part, part, kv(lambda g, n: (g, 0, 0)), kv(lambda g, n: (g, 0, 0)), sspec],
        out_shape=[jax.ShapeDtypeStruct(q.shape, F32), part_shape, part_shape, part_shape, part_shape,
                   meta_shape, meta_shape, jax.ShapeDtypeStruct(sinks_b.shape, F32)],
        compiler_params=_cparams(None),
    )(*_attn_args(q, k, v, sinks_b, cos, sin, rot), do)


def _kv_combine(name, prev_part, own_part, meta):
    g, nb = own_part.shape[:2]

    def fn(own, nxt, mt):
        m = pl.program_id(1)
        one = jnp.ones((BLOCK, HEAD_DIM), F32)
        use_next = jnp.where(one * m < nb - 1, 1.0, 0.0)
        use_meta = jnp.where(one * m < 1, 1.0, 0.0)
        return own + nxt * use_next + mt * use_meta

    blk = (None, None, BLOCK, HEAD_DIM)
    return _call(fn, name, (g, nb),
                 [(own_part, blk, lambda a, m: (a, m, 0, 0)),
                  (prev_part, blk, lambda a, m: (a, jnp.minimum(m + 1, nb - 1), 0, 0)),
                  (meta, (None, BLOCK, HEAD_DIM), lambda a, m: (a, 0, 0))],
                 [((g, nb * BLOCK, HEAD_DIM), (None, BLOCK, HEAD_DIM), lambda a, m: (a, m, 0), False)])


PACK_W = 1024
ELEMENTWISE_BLOCK_BYTES = 1 << 21


def _rows_tile(rows, cols):
    cap = max(8, ELEMENTWISE_BLOCK_BYTES // (4 * cols))
    for d in range(min(rows, cap), 0, -1):
        if rows % d == 0 and d % 8 == 0:
            return d
    return rows


def _adamw(name, w, g, m, v):
    rows, cols = w.shape
    tr = _rows_tile(rows, cols)

    def fn(wv, gv, mv, vv):
        m1 = ADAM_B1 * mv + (1.0 - ADAM_B1) * gv
        v1 = ADAM_B2 * vv + (1.0 - ADAM_B2) * (gv * gv)
        m_hat = m1 / (1.0 - ADAM_B1 ** ADAM_STEP)
        v_hat = v1 / (1.0 - ADAM_B2 ** ADAM_STEP)
        return -ADAM_LR * (m_hat / (jnp.sqrt(v_hat) + ADAM_EPS) + ADAM_WD * wv), m1, v1

    blk = (tr, cols)
    row = lambda i: (i, 0)
    return _call(fn, name, (rows // tr,), [(a, blk, row) for a in (w, g, m, v)], [((rows, cols), blk, row, False)] * 3)


def _pair_add(name, g, recv, c_idx):
    s, a, b = g.shape
    half = a // 2

    def body(c_ref, a_ref, b_ref, o_ref):
        o_ref[...] = a_ref[...] + b_ref[...]

    blk = (None, half, b)
    return pl.pallas_call(
        body, name=name,
        grid_spec=pltpu.PrefetchScalarGridSpec(
            num_scalar_prefetch=1, grid=(s,),
            in_specs=[pl.BlockSpec(blk, lambda j, c: (j, c[0], 0)), pl.BlockSpec(blk, lambda j, c: (j, 0, 0))],
            out_specs=pl.BlockSpec(blk, lambda j, c: (j, 0, 0))),
        out_shape=jax.ShapeDtypeStruct((s, half, b), F32), compiler_params=_cparams(None),
    )(c_idx, g, recv)


def _sum_chips(name, parts):
    _, a, b = parts.shape
    tr = _rows_tile(a, b)

    def fn(p0, p1, p2, p3):
        return ((p0 + p1) + p2) + p3

    return _call(fn, name, (a // tr,),
                 [(parts, (None, tr, b), lambda i, k=k: (k, i, 0)) for k in range(N_CHIPS)],
                 [((a, b), (tr, b), lambda i: (i, 0), False)])


def _mesh_pos():
    return lax.axis_index("x"), lax.axis_index("y"), lax.axis_index("c")


def _other_chips(x, y):
    return [(1 - x, y), (x, 1 - y), (1 - x, 1 - y)]


_ANY = pl.BlockSpec(memory_space=pl.ANY)


def _gather_weights(name, mine, half_major):
    n = len(mine)

    def body(*refs):
        x_refs, out_refs = refs[:n], refs[n:2 * n]
        send_sems, recv_sems, local_sems = refs[2 * n:]
        x, y, c = _mesh_pos()
        me = 2 * x + y
        sibling = (x, y, 1 - c)
        chips = _other_chips(x, y)

        def dst(i, chip_idx, half):
            return out_refs[i].at[half, chip_idx] if half_major[i] else out_refs[i].at[chip_idx, half]

        def copy(i, k, src, chip_idx, half, to):
            return pltpu.make_async_remote_copy(src_ref=src, dst_ref=dst(i, chip_idx, half),
                                                send_sem=send_sems.at[6 * i + k], recv_sem=recv_sems.at[6 * i + k],
                                                device_id=to, device_id_type=MESH)

        local = [pltpu.make_async_copy(x_refs[i].at[h], dst(i, me, h), local_sems.at[2 * i + h])
                 for i in range(n) for h in range(2)]
        for cp in local:
            cp.start()
        first = [copy(i, j, x_refs[i].at[c], me, c, (*chip, c)) for i in range(n) for j, chip in enumerate(chips)]
        for cp in first:
            cp.start()
        passed = []
        for i in range(n):
            for j, (cx, cy) in enumerate(chips):
                idx = 2 * cx + cy
                copy(i, j, x_refs[i].at[c], idx, c, sibling).wait_recv()
                fwd = copy(i, 3 + j, dst(i, idx, c), idx, c, sibling)
                fwd.start()
                passed.append(fwd)
        for i in range(n):
            for j, (cx, cy) in enumerate(chips):
                copy(i, 3 + j, x_refs[i].at[c], 2 * cx + cy, 1 - c, sibling).wait_recv()
        for cp in first + passed:
            cp.wait_send()
        for cp in local:
            cp.wait()

    def out_shape(i):
        two, a, b = mine[i].shape
        return jax.ShapeDtypeStruct((two, N_CHIPS, a, b) if half_major[i] else (N_CHIPS, two, a, b), mine[i].dtype)

    return pl.pallas_call(
        body, name=name, in_specs=[_ANY] * n, out_specs=[_ANY] * n, out_shape=[out_shape(i) for i in range(n)],
        scratch_shapes=[pltpu.SemaphoreType.DMA((6 * n,)), pltpu.SemaphoreType.DMA((6 * n,)),
                        pltpu.SemaphoreType.DMA((2 * n,))],
        compiler_params=pltpu.CompilerParams(has_side_effects=True),
    )(*mine)


def _halves_to_sibling(name, units):
    n = len(units)

    def body(*refs):
        g_refs, out_refs = refs[:n], refs[n:2 * n]
        send_sems, recv_sems = refs[2 * n:]
        x, y, c = _mesh_pos()
        cps = []
        for i in range(n):
            half = units[i].shape[1] // 2
            src = g_refs[i].at[pl.ds(0, N_CHIPS), pl.ds((1 - c) * half, half)]
            cp = pltpu.make_async_remote_copy(src_ref=src, dst_ref=out_refs[i], send_sem=send_sems.at[i],
                                              recv_sem=recv_sems.at[i], device_id=(x, y, 1 - c), device_id_type=MESH)
            cp.start()
            cps.append(cp)
        for cp in cps:
            cp.wait()

    return pl.pallas_call(
        body, name=name, in_specs=[_ANY] * n, out_specs=[_ANY] * n,
        out_shape=[jax.ShapeDtypeStruct((u.shape[0], u.shape[1] // 2, u.shape[2]), u.dtype) for u in units],
        scratch_shapes=[pltpu.SemaphoreType.DMA((n,)), pltpu.SemaphoreType.DMA((n,))],
        compiler_params=pltpu.CompilerParams(has_side_effects=True),
    )(*units)


def _scatter_to_chips(name, sums):
    n = len(sums)

    def body(*refs):
        h_refs, out_refs = refs[:n], refs[n:2 * n]
        send_sems, recv_sems, local_sems = refs[2 * n:]
        x, y, c = _mesh_pos()
        me = 2 * x + y
        chips = _other_chips(x, y)
        local = [pltpu.make_async_copy(h_refs[i].at[me], out_refs[i].at[me], local_sems.at[i]) for i in range(n)]
        for cp in local:
            cp.start()

        def copy(i, j, src_idx, dst_idx):
            cx, cy = chips[j]
            return pltpu.make_async_remote_copy(src_ref=h_refs[i].at[src_idx], dst_ref=out_refs[i].at[dst_idx],
                                                send_sem=send_sems.at[3 * i + j], recv_sem=recv_sems.at[3 * i + j],
                                                device_id=(cx, cy, c), device_id_type=MESH)

        cps = [copy(i, j, 2 * chips[j][0] + chips[j][1], me) for i in range(n) for j in range(3)]
        for cp in cps:
            cp.start()
        for i in range(n):
            for j in range(3):
                copy(i, j, me, 2 * chips[j][0] + chips[j][1]).wait_recv()
        for cp in cps:
            cp.wait_send()
        for cp in local:
            cp.wait()

    return pl.pallas_call(
        body, name=name, in_specs=[_ANY] * n, out_specs=[_ANY] * n,
        out_shape=[jax.ShapeDtypeStruct(s.shape, s.dtype) for s in sums],
        scratch_shapes=[pltpu.SemaphoreType.DMA((3 * n,)), pltpu.SemaphoreType.DMA((3 * n,)), pltpu.SemaphoreType.DMA((n,))],
        compiler_params=pltpu.CompilerParams(has_side_effects=True),
    )(*sums)


def _join_halves(name, halves, dests, result_shapes):
    n = len(halves)
    nr = len(result_shapes)

    def body(*refs):
        h_refs, out_refs = refs[:n], refs[n:n + nr]
        send_sems, recv_sems, local_sems = refs[n + nr:]
        x, y, c = _mesh_pos()

        def place(i, half):
            r, l = dests[i]
            return out_refs[r].at[l, half]

        local = [pltpu.make_async_copy(h_refs[i], place(i, c), local_sems.at[i]) for i in range(n)]
        for cp in local:
            cp.start()

        def copy(i, half):
            return pltpu.make_async_remote_copy(src_ref=h_refs[i], dst_ref=place(i, half), send_sem=send_sems.at[i],
                                                recv_sem=recv_sems.at[i], device_id=(x, y, 1 - c), device_id_type=MESH)

        cps = [copy(i, c) for i in range(n)]
        for cp in cps:
            cp.start()
        for i in range(n):
            copy(i, 1 - c).wait_recv()
        for cp in cps:
            cp.wait_send()
        for cp in local:
            cp.wait()

    return pl.pallas_call(
        body, name=name, in_specs=[_ANY] * n, out_specs=[_ANY] * nr,
        out_shape=[jax.ShapeDtypeStruct(s, F32) for s in result_shapes],
        scratch_shapes=[pltpu.SemaphoreType.DMA((n,)), pltpu.SemaphoreType.DMA((n,)), pltpu.SemaphoreType.DMA((n,))],
        compiler_params=pltpu.CompilerParams(has_side_effects=True),
    )(*halves)


def _pack(arrays, dtype, rows_multiple):
    flat = jnp.concatenate([a.reshape(-1).astype(dtype) for a in arrays])
    unit = rows_multiple * PACK_W
    total = -(-flat.shape[0] // unit) * unit
    return jnp.pad(flat, (0, total - flat.shape[0])).reshape(total // PACK_W, PACK_W)


def _unpack(flat, shapes):
    out, off = [], 0
    for s in shapes:
        n = 1
        for d in s:
            n *= d
        out.append(flat[..., off:off + n].reshape(flat.shape[:-1] + tuple(s)))
        off += n
    return out


def _ffn_fwd(tag, l, h, g, w_up, conv, bias, w_down, tm):
    hn = _rms_fwd(f"{tag}_norm", h, g, tm)
    u = _mm_cs(f"{tag}_up", hn, w_up, l, tm)
    act = _ffn_col_fwd(f"{tag}_glu", u, conv, bias)
    h_out = _mm_full(f"{tag}_down", act, w_down, l, tm, D_FF // 2, add=h)
    return h_out, (hn, u, act)


def _ffn_bwd(tag, l, h, g, w_up, conv, bias, w_down, saved, dh, tm):
    hn, u, act = saved
    da = _mm_nt_full(f"{tag}_down_dx", dh, w_down, l, tm, D_FF // 2)
    dw_down = _mm_tn_full(f"{tag}_down_dw", act, dh, tm, D_FF // 2)
    du, dconv, dbias = _ffn_col_bwd(f"{tag}_glu_bwd", u, da, conv, bias)
    dw_up = _mm_tn_cs(f"{tag}_up_dw", hn, du, N_CHIPS, tm)
    dhn = _mm_nt_cs(f"{tag}_up_dx", du, w_up, l, tm)
    dh, dg = _rms_bwd(f"{tag}_norm_bwd", h, g, dhn, dh, tm)
    return dh, dict(norm=dg, w_up=dw_up, conv=dconv, bias=dbias, w_down=dw_down)


def _to_heads(z, nh, pad):
    t = z.shape[0]
    return jnp.pad(z.reshape(t, nh, HEAD_DIM).transpose(1, 0, 2), ((0, 0), (pad, 0), (0, 0)))


def _from_heads(z, pad):
    nh, tp, _ = z.shape
    return z[:, pad:].transpose(1, 0, 2).reshape(tp - pad, nh * HEAD_DIM)


def _rope_tables(tp, pad):
    half = HEAD_DIM // 2
    inv = ROPE_THETA ** (-jnp.arange(half, dtype=F32) / half)
    ang = (jnp.arange(tp, dtype=F32) - pad)[:, None] * inv[None, :]
    cos, sin = jnp.cos(ang), jnp.sin(ang)
    rot = jnp.zeros((HEAD_DIM, HEAD_DIM), F32)
    idx = jnp.arange(half)
    rot = rot.at[idx + half, idx].set(-1.0).at[idx, idx + half].set(1.0)
    return jnp.concatenate([cos, cos], axis=1), jnp.concatenate([sin, sin], axis=1), rot


def _local_step(x, tgt, w):
    seq = x.shape[0]
    t = seq + N_META
    tm = _row_tile(t, 704)
    tr = _row_tile(t, 352)
    pad = BLOCK - N_META
    grads = {}

    h0 = jnp.concatenate([w["meta_tokens"], x], axis=0)
    tgt_p = jnp.pad(tgt, ((N_META, 0), (0, 0)))

    hn0 = _rms_fwd("l0_norm", h0, w["norm_mix"][0:1], tm)
    p0 = _mm_cs("l0_in", hn0, w["ev_w_in"], 0, tm)
    uc, yb = _even_col_fwd("l0_convs", p0, w["ev_conv_a"], w["ev_conv_b"])
    ya = _even_ln_fwd("l0_ln", uc, w["ev_ln_a_g"], w["ev_ln_a_b"], tm)
    y0 = jnp.concatenate([ya, yb], axis=1)
    h1 = _mm_full("l0_out", y0, w["ev_w_out"], 0, tm, D_MODEL, add=h0)
    f0 = (0, h1, w["norm_ffn"][0:1], w["ff_w_up"], w["ff_conv"][0], w["ff_conv_b"][0:1], w["ff_w_down"])
    h2, ffn0 = _ffn_fwd("f0", *f0, tm)

    hn2 = _rms_fwd("l1_norm", h2, w["norm_mix"][1:2], tm)
    p1 = _mm_cs("l1_in", hn2, w["od_w_in"], 0, tm)
    cos, sin, rot = _rope_tables(t + pad, pad)
    qh = _to_heads(p1[:, :D_ATT], N_Q_HEADS, pad)
    kh = _to_heads(p1[:, D_ATT:D_ATT + D_KV], N_KV_HEADS, pad)
    vh = _to_heads(p1[:, D_ATT + D_KV:D_ATT + 2 * D_KV], N_KV_HEADS, pad)
    sinks_b = jnp.broadcast_to(w["od_sinks"].reshape(N_Q_HEADS, 1, 1), (N_Q_HEADS, 8, LANES))
    y_att = _from_heads(_attn_fwd("l1_attn", qh, kh, vh, sinks_b, cos, sin, rot), pad)

    col0 = (D_ATT + 2 * D_KV) // LANES
    ch = jnp.arange(D_R) // HEAD_DIM
    seg = (ch[:, None] == ch[None, :]).astype(F32)
    prm = dict(w0=w["od_w0"], a0=w["od_a0"], g2=w["od_g2"], k_k=w["od_k_k"], k_a=w["od_k_a"],
               lnx_g=w["od_lnx_g"], lnx_b=w["od_lnx_b"], r_k=w["od_r_k"].reshape(1, D_R),
               w2p=jnp.concatenate([w["od_w2"], jnp.zeros((LORA_A, D_R), F32)], axis=0),
               a2p=jnp.concatenate([jnp.zeros((LORA_W, D_R), F32), w["od_a2"]], axis=0))
    prs = _shift_fwd("l1_shift", p1, col0, w["od_mu"])
    lw, k2, a_, b_, gate_r = _rwkv_pre_fwd("l1_rwkv_pre", prs, prm, seg, tr)
    heads = lambda z: z.reshape(t, N_R_HEADS, HEAD_DIM).transpose(1, 0, 2)
    unheads = lambda z: z.transpose(1, 0, 2).reshape(t, D_R)
    scan_in = [heads(z) for z in (prs[:, :D_R], lw, k2, prs[:, 2 * D_R:3 * D_R], a_, b_)]
    y_scan_h, states = _wkv_fwd("l1_wkv", scan_in)
    y_scan = unheads(y_scan_h)
    y_rwkv = _rwkv_post_fwd("l1_rwkv_post", y_scan, prs, k2, gate_r, prm, seg, tr)
    y1 = jnp.concatenate([y_att, y_rwkv], axis=1)
    h3 = _mm_full("l1_out", y1, w["od_w_out"], 0, tm, D_MODEL, add=h2)
    f1 = (1, h3, w["norm_ffn"][1:2], w["ff_w_up"], w["ff_conv"][1], w["ff_conv_b"][1:2], w["ff_w_down"])
    h4, ffn1 = _ffn_fwd("f1", *f1, tm)

    loss_blk, dh, d_norm_final = _final_loss("final", h4, w["norm_final"], tgt_p, tm)
    grads["norm_final"] = d_norm_final

    dh, gf1 = _ffn_bwd("f1", *f1, ffn1, dh, tm)
    dy1 = _mm_nt_full("l1_out_dx", dh, w["od_w_out"], 0, tm, D_MODEL)
    grads["od_w_out"] = _mm_tn_full("l1_out_dw", y1, dh, tm, D_MODEL // 2)
    dy_scan, dr_p, dk2_p, dv_p, dgate_r, grads["od_lnx_g"], grads["od_lnx_b"], d_rk = _rwkv_post_bwd(
        "l1_rwkv_post_bwd", y_scan, prs, k2, gate_r, prm, seg, dy1, 1, tr)
    grads["od_r_k"] = d_rk.reshape(N_R_HEADS, HEAD_DIM)
    dscan = _wkv_bwd("l1_wkv_bwd", scan_in, states, heads(dy_scan))
    dr_s, dlw, dk2_s, dv_s, da_, db_ = [unheads(z) for z in dscan]
    dk, dxl, dgd, grads["od_w0"], dw2p, grads["od_a0"], da2p, grads["od_g2"], grads["od_k_k"], grads["od_k_a"] = (
        _rwkv_pre_bwd("l1_rwkv_pre_bwd", prs, prm, seg, (dlw, dk2_s + dk2_p, da_, db_, dgate_r), tr))
    grads["od_w2"] = dw2p[:LORA_W]
    grads["od_a2"] = da2p[LORA_W:]
    dprs = jnp.concatenate([dr_s + dr_p, dk, dv_s + dv_p, dxl, dgd], axis=1)
    dpr, grads["od_mu"] = _shift_bwd("l1_shift_bwd", p1, col0, w["od_mu"], dprs)
    doh = _to_heads(dy1[:, :D_ATT], N_Q_HEADS, pad)
    dqh, dkp, dkc, dvp, dvc, dkm, dvm, dsinks = _attn_bwd("l1_attn_bwd", qh, kh, vh, sinks_b, cos, sin, rot, doh)
    grads["od_sinks"] = dsinks[:, 0, 0].reshape(1, N_Q_HEADS)
    dkh = _kv_combine("l1_attn_dk", dkp, dkc, dkm)
    dvh = _kv_combine("l1_attn_dv", dvp, dvc, dvm)
    dp1 = jnp.concatenate([_from_heads(dqh, pad), _from_heads(dkh, pad), _from_heads(dvh, pad), dpr], axis=1)
    grads["od_w_in"] = _mm_tn_cs("l1_in_dw", hn2, dp1, N_CHIPS, tm)
    dhn2 = _mm_nt_cs("l1_in_dx", dp1, w["od_w_in"], 0, tm)
    dh, d_mix1 = _rms_bwd("l1_norm_bwd", h2, w["norm_mix"][1:2], dhn2, dh, tm)

    dh, gf0 = _ffn_bwd("f0", *f0, ffn0, dh, tm)
    dy0 = _mm_nt_full("l0_out_dx", dh, w["ev_w_out"], 0, tm, D_MODEL)
    grads["ev_w_out"] = _mm_tn_full("l0_out_dw", y0, dh, tm, D_MODEL // 2)
    duc, grads["ev_ln_a_g"], grads["ev_ln_a_b"] = _even_ln_bwd("l0_ln_bwd", uc, w["ev_ln_a_g"], w["ev_ln_a_b"], dy0, 0, tm)
    *dparts, grads["ev_conv_a"], grads["ev_conv_b"] = _even_col_bwd("l0_convs_bwd", p0, duc, dy0, w["ev_conv_a"], w["ev_conv_b"])
    dp0 = jnp.concatenate(dparts, axis=1)
    grads["ev_w_in"] = _mm_tn_cs("l0_in_dw", hn0, dp0, N_CHIPS, tm)
    dhn0 = _mm_nt_cs("l0_in_dx", dp0, w["ev_w_in"], 0, tm)
    dh, d_mix0 = _rms_bwd("l0_norm_bwd", h0, w["norm_mix"][0:1], dhn0, dh, tm)

    grads["norm_mix"] = jnp.concatenate([d_mix0, d_mix1], axis=0)
    grads["norm_ffn"] = jnp.concatenate([gf0["norm"], gf1["norm"]], axis=0)
    grads["ff_w_up"] = [gf0["w_up"], gf1["w_up"]]
    grads["ff_conv"] = jnp.stack([gf0["conv"], gf1["conv"]])
    grads["ff_conv_b"] = jnp.concatenate([gf0["bias"], gf1["bias"]], axis=0)
    grads["ff_w_down"] = [gf0["w_down"], gf1["w_down"]]
    grads["meta_tokens"] = dh[:N_META]
    return loss_blk[0, 0], dh[N_META:], grads


SHARD_AXIS = {
    "meta_tokens": 1, "norm_mix": None, "norm_ffn": None, "norm_final": None,
    "ev_w_in": 2, "ev_conv_a": 2, "ev_ln_a_g": None, "ev_ln_a_b": None, "ev_conv_b": 2, "ev_w_out": 1,
    "od_w_in": 2, "od_sinks": None, "od_mu": 1, "od_w0": 1, "od_w2": 2, "od_a0": 1, "od_a2": 2, "od_g2": 2,
    "od_k_k": 1, "od_k_a": 1, "od_r_k": None, "od_lnx_g": 1, "od_lnx_b": 1, "od_w_out": 1,
    "ff_w_up": 2, "ff_conv": 2, "ff_conv_b": None, "ff_w_down": 1,
}
WEIGHTS = list(SHARD_AXIS)
BIG = ("ev_w_in", "ev_w_out", "od_w_in", "od_w_out", "ff_w_up", "ff_w_down")
SHARDED = [n for n in WEIGHTS if SHARD_AXIS[n] is not None]
SMALL = [n for n in SHARDED if n not in BIG]
REPLICATED = [n for n in WEIGHTS if SHARD_AXIS[n] is None]


def _join(g, axis):
    return jnp.concatenate([g[k] for k in range(N_CHIPS)], axis=axis)


def _split(full, axis):
    return jnp.stack(jnp.split(full, N_CHIPS, axis=axis))


def _full_weights(gathered, repl):
    w = {}
    sq = lambda a: a.reshape(a.shape[1:]) if a.shape[0] == 1 else a
    for n in REPLICATED:
        w[n] = repl[n]
    w["norm_final"] = repl["norm_final"].reshape(1, D_MODEL)
    for n in ("ev_ln_a_g", "ev_ln_a_b"):
        w[n] = repl[n].reshape(1, D_A)
    w["od_r_k"] = repl["od_r_k"][0]
    w["meta_tokens"] = _join(gathered["meta_tokens"], 1)
    for n in ("ev_conv_a", "ev_conv_b", "od_w2", "od_a2", "od_g2"):
        w[n] = sq(_join(gathered[n], 2))
    for n in ("od_mu", "od_w0", "od_a0", "od_k_k", "od_k_a", "od_lnx_g", "od_lnx_b"):
        w[n] = _join(gathered[n], 1)
    w["ff_conv"] = _join(gathered["ff_conv"], 2)
    return w


def _shard_grads(grads):
    out = {}
    for n in REPLICATED:
        out[n] = grads[n]
    out["norm_final"] = grads["norm_final"].reshape(D_MODEL)
    out["od_r_k"] = grads["od_r_k"][None]
    out["meta_tokens"] = _split(grads["meta_tokens"], 1)
    for n in ("ev_conv_a", "ev_conv_b", "od_w2", "od_a2", "od_g2"):
        out[n] = _split(grads[n][None], 2)
    for n in ("od_mu", "od_w0", "od_a0", "od_k_k", "od_k_a", "od_lnx_g", "od_lnx_b"):
        out[n] = _split(grads[n], 1)
    out["ff_conv"] = _split(grads["ff_conv"], 2)
    return out


def kernel(x, meta_tokens, norm_mix, norm_ffn, norm_final, ev_w_in, ev_conv_a, ev_ln_a_g, ev_ln_a_b, ev_conv_b, ev_w_out, od_w_in, od_sinks, od_mu, od_w0, od_w2, od_a0, od_a2, od_g2, od_k_k, od_k_a, od_r_k, od_lnx_g, od_lnx_b, od_w_out, ff_w_up, ff_conv, ff_conv_b, ff_w_down, loss_target, m_meta_tokens, m_norm_mix, m_norm_ffn, m_norm_final, m_ev_w_in, m_ev_conv_a, m_ev_ln_a_g, m_ev_ln_a_b, m_ev_conv_b, m_ev_w_out, m_od_w_in, m_od_sinks, m_od_mu, m_od_w0, m_od_w2, m_od_a0, m_od_a2, m_od_g2, m_od_k_k, m_od_k_a, m_od_r_k, m_od_lnx_g, m_od_lnx_b, m_od_w_out, m_ff_w_up, m_ff_conv, m_ff_conv_b, m_ff_w_down, v_meta_tokens, v_norm_mix, v_norm_ffn, v_norm_final, v_ev_w_in, v_ev_conv_a, v_ev_ln_a_g, v_ev_ln_a_b, v_ev_conv_b, v_ev_w_out, v_od_w_in, v_od_sinks, v_od_mu, v_od_w0, v_od_w2, v_od_a0, v_od_a2, v_od_g2, v_od_k_k, v_od_k_a, v_od_r_k, v_od_lnx_g, v_od_lnx_b, v_od_w_out, v_ff_w_up, v_ff_conv, v_ff_conv_b, v_ff_w_down):
    wts = dict(meta_tokens=meta_tokens, norm_mix=norm_mix, norm_ffn=norm_ffn, norm_final=norm_final, ev_w_in=ev_w_in, ev_conv_a=ev_conv_a, ev_ln_a_g=ev_ln_a_g, ev_ln_a_b=ev_ln_a_b, ev_conv_b=ev_conv_b, ev_w_out=ev_w_out, od_w_in=od_w_in, od_sinks=od_sinks, od_mu=od_mu, od_w0=od_w0, od_w2=od_w2, od_a0=od_a0, od_a2=od_a2, od_g2=od_g2, od_k_k=od_k_k, od_k_a=od_k_a, od_r_k=od_r_k, od_lnx_g=od_lnx_g, od_lnx_b=od_lnx_b, od_w_out=od_w_out, ff_w_up=ff_w_up, ff_conv=ff_conv, ff_conv_b=ff_conv_b, ff_w_down=ff_w_down)
    mom = dict(meta_tokens=m_meta_tokens, norm_mix=m_norm_mix, norm_ffn=m_norm_ffn, norm_final=m_norm_final, ev_w_in=m_ev_w_in, ev_conv_a=m_ev_conv_a, ev_ln_a_g=m_ev_ln_a_g, ev_ln_a_b=m_ev_ln_a_b, ev_conv_b=m_ev_conv_b, ev_w_out=m_ev_w_out, od_w_in=m_od_w_in, od_sinks=m_od_sinks, od_mu=m_od_mu, od_w0=m_od_w0, od_w2=m_od_w2, od_a0=m_od_a0, od_a2=m_od_a2, od_g2=m_od_g2, od_k_k=m_od_k_k, od_k_a=m_od_k_a, od_r_k=m_od_r_k, od_lnx_g=m_od_lnx_g, od_lnx_b=m_od_lnx_b, od_w_out=m_od_w_out, ff_w_up=m_ff_w_up, ff_conv=m_ff_conv, ff_conv_b=m_ff_conv_b, ff_w_down=m_ff_w_down)
    var = dict(meta_tokens=v_meta_tokens, norm_mix=v_norm_mix, norm_ffn=v_norm_ffn, norm_final=v_norm_final, ev_w_in=v_ev_w_in, ev_conv_a=v_ev_conv_a, ev_ln_a_g=v_ev_ln_a_g, ev_ln_a_b=v_ev_ln_a_b, ev_conv_b=v_ev_conv_b, ev_w_out=v_ev_w_out, od_w_in=v_od_w_in, od_sinks=v_od_sinks, od_mu=v_od_mu, od_w0=v_od_w0, od_w2=v_od_w2, od_a0=v_od_a0, od_a2=v_od_a2, od_g2=v_od_g2, od_k_k=v_od_k_k, od_k_a=v_od_k_a, od_r_k=v_od_r_k, od_lnx_g=v_od_lnx_g, od_lnx_b=v_od_lnx_b, od_w_out=v_od_w_out, ff_w_up=v_ff_w_up, ff_conv=v_ff_conv, ff_conv_b=v_ff_conv_b, ff_w_down=v_ff_w_down)

    def halves(a):
        l, rows, cols = a.shape
        return a if l == 2 else a.reshape(2, rows // 2, cols)

    small_mine = _pack([wts[n] for n in SMALL], F32, 2 * 8)
    mine = [halves(wts[n].astype(MXU_DTYPE)) for n in BIG] + [small_mine.reshape(2, -1, PACK_W)]
    got = _gather_weights("gather_weights", mine, [n == "ff_w_down" for n in BIG] + [False])
    gathered = dict(zip(SMALL, _unpack(got[-1].reshape(N_CHIPS, -1), [wts[n].shape for n in SMALL])))
    w_full = _full_weights(gathered, wts)
    for n, g in zip(BIG, got):
        if n == "ff_w_down":
            w_full[n] = g.reshape(2, D_FF, D_MODEL)
        elif n in ("ev_w_out", "od_w_out"):
            w_full[n] = g.reshape(1, D_MODEL, D_MODEL)
        else:
            w_full[n] = g.reshape((N_CHIPS,) + wts[n].shape)

    loss_local, grad_x, grads = _local_step(x[0], loss_target[0], w_full)
    loss = lax.psum(loss_local, ("x", "y", "c"))

    sg = _shard_grads(grads)
    small_rows = [jnp.concatenate([sg[n][k].reshape(-1) for n in SMALL] + [sg[n].reshape(-1) for n in REPLICATED])
                  for k in range(N_CHIPS)]
    n_el = small_rows[0].shape[0]
    n_rows = -(-n_el // (16 * PACK_W)) * 16
    small_unit = jnp.stack([jnp.pad(r, (0, n_rows * PACK_W - n_el)).reshape(n_rows, PACK_W) for r in small_rows])
    out_rows = D_MODEL // N_CHIPS
    ff_rows = D_FF // N_CHIPS
    units = [grads["ev_w_in"], grads["od_w_in"],
             grads["ev_w_out"].reshape(N_CHIPS, out_rows, D_MODEL), grads["od_w_out"].reshape(N_CHIPS, out_rows, D_MODEL),
             grads["ff_w_up"][0], grads["ff_w_up"][1],
             grads["ff_w_down"][0].reshape(N_CHIPS, ff_rows, D_MODEL), grads["ff_w_down"][1].reshape(N_CHIPS, ff_rows, D_MODEL),
             small_unit]
    dests = [(0, 0), (1, 0), (2, 0), (3, 0), (4, 0), (4, 1), (5, 0), (5, 1), (6, 0)]
    results = ["ev_w_in", "od_w_in", "ev_w_out", "od_w_out", "ff_w_up", "ff_w_down", None]
    result_shapes = [None] * len(results)
    for u, (r, l) in zip(units, dests):
        result_shapes[r] = (l + 1, 2, u.shape[1] // 2, u.shape[2])
    c_idx = lax.axis_index("c").astype(jnp.int32).reshape(1)
    from_sibling = _halves_to_sibling("grads_to_sibling", units)
    chip_sums = [_pair_add(f"grads_pair_add{i}", u, r, c_idx) for i, (u, r) in enumerate(zip(units, from_sibling))]
    from_chips = _scatter_to_chips("grads_to_chips", chip_sums)
    reduced = [_sum_chips(f"grads_chip_sum{i}", p) for i, p in enumerate(from_chips)]
    joined = _join_halves("grads_join", reduced, dests, result_shapes)

    outs = {"grad": {}, "delta": {}, "new_m": {}, "new_v": {}}
    for n, g in zip(results[:-1], joined):
        shape = wts[n].shape
        flat = lambda a: a.reshape(-1, shape[-1])
        new = _adamw("adamw_" + n, flat(wts[n]), flat(g), flat(mom[n]), flat(var[n]))
        for tag, arr in zip(("grad", "delta", "new_m", "new_v"), (g,) + tuple(new)):
            outs[tag][n] = arr.reshape(shape)
    order = SMALL + REPLICATED
    packed = lambda d: jnp.pad(jnp.concatenate([d[n].reshape(-1) for n in order]),
                               (0, n_rows * PACK_W - n_el)).reshape(n_rows, PACK_W)
    g_small = joined[-1].reshape(n_rows, PACK_W)
    new = _adamw("adamw_small", packed(wts), g_small, packed(mom), packed(var))
    for tag, arr in zip(("grad", "delta", "new_m", "new_v"), (g_small,) + tuple(new)):
        outs[tag].update(zip(order, _unpack(arr.reshape(-1), [wts[n].shape for n in order])))
    return (loss, grad_x[None], *[outs["grad"][n] for n in WEIGHTS], *[outs["delta"][n] for n in WEIGHTS],
            *[outs["new_m"][n] for n in WEIGHTS], *[outs["new_v"][n] for n in WEIGHTS])
```

```python
import functools

import jax
import jax.numpy as jnp
from jax import lax
from jax.experimental import pallas as pl
from jax.experimental.pallas import tpu as pltpu

F32 = jnp.float32
BF16 = jnp.bfloat16
HI = lax.Precision.HIGHEST
MXU_DTYPE = BF16

D_MODEL = 1024
N_META = 16
RMS_EPS = 1e-6
LN_EPS = 1e-5
D_A = 512
CONV_A_WIDTH = 31
CONV_B_WIDTH = 3
HEAD_DIM = 64
N_Q_HEADS = 8
N_KV_HEADS = 2
GQA_GROUP = 4
D_ATT = 512
D_KV = 128
BLOCK = 128
ROPE_THETA = 10000.0
D_R = 512
N_R_HEADS = 8
LORA_W = 64
LORA_A = 64
LORA_G = 128
RWKV_GN_EPS = 64e-5
RWKV_COLS = 3 * D_R + LORA_W + LORA_A + LORA_G
D_FF = 2816
NEG_INF = -1e30
ADAM_LR = 0.001
ADAM_B1 = 0.9
ADAM_B2 = 0.999
ADAM_EPS = 1e-08
ADAM_WD = 0.01
ADAM_STEP = 10

N_CHIPS = 4
LANES = 128
CONV_PAD = 32
VMEM_LIMIT_V7X = 56 * 1024 * 1024
MESH = pl.DeviceIdType.MESH


def _cparams(sem=None):
    return pltpu.CompilerParams(dimension_semantics=sem, vmem_limit_bytes=VMEM_LIMIT_V7X)


def _row_tile(t, cap):
    for d in range(min(t, cap), 0, -1):
        if t % d == 0 and d % 16 == 0:
            return d
    return t


def _chunk_len(t):
    for d in (64, 48, 32, 16, 8):
        if t % d == 0:
            return d
    raise ValueError(t)


def _call(fn, name, grid, ins, outs, acc_axis=None, sem=None):
    n_in, n_out = len(ins), len(outs)

    def body(*refs):
        vals = fn(*[r[...] for r in refs[:n_in]])
        if not isinstance(vals, (tuple, list)):
            vals = (vals,)
        for r, v, o in zip(refs[n_in:n_in + n_out], vals, outs):
            if o[3]:
                first = pl.program_id(acc_axis) == 0

                @pl.when(first)
                def _(r=r, v=v):
                    r[...] = v

                @pl.when(jnp.logical_not(first))
                def _(r=r, v=v):
                    r[...] += v
            else:
                r[...] = v

    res = pl.pallas_call(
        body, name=name, grid=grid,
        in_specs=[pl.BlockSpec(b, m) for _, b, m in ins],
        out_specs=[pl.BlockSpec(o[1], o[2]) for o in outs],
        out_shape=[jax.ShapeDtypeStruct(o[0], F32) for o in outs],
        compiler_params=_cparams(sem),
    )(*[a for a, _, _ in ins])
    return res if n_out > 1 else res[0]


def _matmul(name, a, b, *, dims, grid, a_spec, b_spec, o_shape, o_spec, acc_shape, nk, k_axis,
            add=None, add_spec=None):
    def body(*refs):
        if add is None:
            a_ref, b_ref, o_ref, acc = refs
        else:
            a_ref, b_ref, add_ref, o_ref, acc = refs
        k = pl.program_id(k_axis)

        @pl.when(k == 0)
        def _():
            if add is None:
                acc[...] = jnp.zeros(acc.shape, F32)
            else:
                acc[...] = add_ref[...]

        acc[...] += lax.dot_general(a_ref[...].astype(MXU_DTYPE), b_ref[...].astype(MXU_DTYPE), dims,
                                    preferred_element_type=F32)

        @pl.when(k == nk - 1)
        def _():
            o_ref[...] = acc[...]

    args = [a, b] + ([] if add is None else [add])
    specs = [a_spec, b_spec] + ([] if add is None else [add_spec])
    return pl.pallas_call(
        body, name=name, grid=grid, in_specs=specs, out_specs=o_spec,
        out_shape=jax.ShapeDtypeStruct(o_shape, F32),
        scratch_shapes=[pltpu.VMEM(acc_shape, F32)],
        compiler_params=_cparams(None),
    )(*args)


_NN = (((1,), (0,)), ((), ()))
_NT = (((1,), (1,)), ((), ()))
_TN = (((0,), (0,)), ((), ()))


def _mm_cs(name, x, wg, l, tm):
    t, k = x.shape
    s, _, _, n = wg.shape
    return _matmul(name, x, wg, dims=_NN, grid=(s, t // tm, 1),
                   a_spec=pl.BlockSpec((tm, k), lambda j, i, kk: (i, 0)),
                   b_spec=pl.BlockSpec((None, None, k, n), lambda j, i, kk: (j, l, 0, 0)),
                   o_shape=(t, s * n), o_spec=pl.BlockSpec((tm, n), lambda j, i, kk: (i, j)),
                   acc_shape=(tm, n), nk=1, k_axis=2)


def _mm_full(name, x, w, l, tm, tk, add=None):
    t, k = x.shape
    n = w.shape[2]
    nk = k // tk
    return _matmul(name, x, w, dims=_NN, grid=(t // tm, 1, nk),
                   a_spec=pl.BlockSpec((tm, tk), lambda i, j, kk: (i, kk)),
                   b_spec=pl.BlockSpec((None, tk, n), lambda i, j, kk: (l, kk, 0)),
                   o_shape=(t, n), o_spec=pl.BlockSpec((tm, n), lambda i, j, kk: (i, 0)),
                   acc_shape=(tm, n), nk=nk, k_axis=2,
                   add=add, add_spec=pl.BlockSpec((tm, n), lambda i, j, kk: (i, 0)))


def _mm_nt_cs(name, dy, wg, l, tm, add=None):
    t = dy.shape[0]
    s, _, k, n = wg.shape
    return _matmul(name, dy, wg, dims=_NT, grid=(t // tm, 1, s),
                   a_spec=pl.BlockSpec((tm, n), lambda i, j, kk: (i, kk)),
                   b_spec=pl.BlockSpec((None, None, k, n), lambda i, j, kk: (kk, l, 0, 0)),
                   o_shape=(t, k), o_spec=pl.BlockSpec((tm, k), lambda i, j, kk: (i, 0)),
                   acc_shape=(tm, k), nk=s, k_axis=2,
                   add=add, add_spec=pl.BlockSpec((tm, k), lambda i, j, kk: (i, 0)))


def _mm_nt_full(name, dy, w, l, tm, tko):
    t, n = dy.shape
    k = w.shape[1]
    return _matmul(name, dy, w, dims=_NT, grid=(t // tm, k // tko, 1),
                   a_spec=pl.BlockSpec((tm, n), lambda i, j, kk: (i, 0)),
                   b_spec=pl.BlockSpec((None, tko, n), lambda i, j, kk: (l, j, 0)),
                   o_shape=(t, k), o_spec=pl.BlockSpec((tm, tko), lambda i, j, kk: (i, j)),
                   acc_shape=(tm, tko), nk=1, k_axis=2)


def _mm_tn_cs(name, x, dy, s, tk):
    t, k = x.shape
    n = dy.shape[1] // s
    nk = t // tk
    return _matmul(name, x, dy, dims=_TN, grid=(s, 1, nk),
                   a_spec=pl.BlockSpec((tk, k), lambda j, i, kk: (kk, 0)),
                   b_spec=pl.BlockSpec((tk, n), lambda j, i, kk: (kk, j)),
                   o_shape=(s, k, n), o_spec=pl.BlockSpec((None, k, n), lambda j, i, kk: (j, 0, 0)),
                   acc_shape=(k, n), nk=nk, k_axis=2)


def _mm_tn_full(name, y, dh, tk, tko):
    t, k = y.shape
    n = dh.shape[1]
    nk = t // tk
    return _matmul(name, y, dh, dims=_TN, grid=(k // tko, 1, nk),
                   a_spec=pl.BlockSpec((tk, tko), lambda j, i, kk: (kk, j)),
                   b_spec=pl.BlockSpec((tk, n), lambda j, i, kk: (kk, 0)),
                   o_shape=(k, n), o_spec=pl.BlockSpec((tko, n), lambda j, i, kk: (j, 0)),
                   acc_shape=(tko, n), nk=nk, k_axis=2)


def _sigmoid(x):
    return 1.0 / (1.0 + jnp.exp(-x))


def _rms_fwd(name, h, g, tr):
    t, d = h.shape

    def fn(hv, gv):
        r = lax.rsqrt(jnp.mean(hv * hv, axis=-1, keepdims=True) + RMS_EPS)
        return hv * r * gv

    return _call(fn, name, (t // tr,), [(h, (tr, d), lambda i: (i, 0)), (g, (1, d), lambda i: (0, 0))],
                 [((t, d), (tr, d), lambda i: (i, 0), False)])


def _rms_bwd(name, h, g, dhn, dh, tr):
    t, d = h.shape

    def fn(hv, gv, dy, dh_in):
        r = lax.rsqrt(jnp.mean(hv * hv, axis=-1, keepdims=True) + RMS_EPS)
        xh = hv * r
        dg = jnp.sum(dy * xh, axis=0, keepdims=True)
        dxh = dy * gv
        dx = r * (dxh - xh * jnp.mean(dxh * xh, axis=-1, keepdims=True))
        return dh_in + dx, dg

    row = lambda i: (i, 0)
    return _call(fn, name, (t // tr,),
                 [(h, (tr, d), row), (g, (1, d), lambda i: (0, 0)), (dhn, (tr, d), row), (dh, (tr, d), row)],
                 [((t, d), (tr, d), row, False), ((1, d), (1, d), lambda i: (0, 0), True)], acc_axis=0)


def _final_loss(name, h, g, tgt, tr):
    t, d = h.shape

    def fn(hv, gv, tv):
        r = lax.rsqrt(jnp.mean(hv * hv, axis=-1, keepdims=True) + RMS_EPS)
        xh = hv * r
        row = pl.program_id(0) * tr + lax.broadcasted_iota(jnp.int32, (tr, 1), 0)
        e = jnp.where(row >= N_META, xh * gv - tv, 0.0)
        loss = jnp.broadcast_to(0.5 * jnp.sum(jnp.sum(e * e, axis=-1, keepdims=True), axis=0, keepdims=True) / d,
                                (8, LANES))
        dout = e / d
        dg = jnp.sum(dout * xh, axis=0, keepdims=True)
        dxh = dout * gv
        dx = r * (dxh - xh * jnp.mean(dxh * xh, axis=-1, keepdims=True))
        return loss, dx, dg

    row = lambda i: (i, 0)
    fix = lambda i: (0, 0)
    return _call(fn, name, (t // tr,), [(h, (tr, d), row), (g, (1, d), fix), (tgt, (tr, d), row)],
                 [((8, LANES), (8, LANES), fix, True), ((t, d), (tr, d), row, False), ((1, d), (1, d), fix, True)],
                 acc_axis=0)


def _silu_ln(uc, g, b):
    mu = jnp.mean(uc, axis=-1, keepdims=True)
    xc = uc - mu
    rs = lax.rsqrt(jnp.mean(xc * xc, axis=-1, keepdims=True) + LN_EPS)
    ln = xc * rs * g + b
    return ln * _sigmoid(ln)


def _even_ln_fwd(name, uc, g, b, tr):
    t, d = uc.shape
    row, fix = (lambda i: (i, 0)), (lambda i: (0, 0))
    return _call(_silu_ln, name, (t // tr,), [(uc, (tr, d), row), (g, (1, d), fix), (b, (1, d), fix)],
                 [((t, d), (tr, d), row, False)])


def _even_ln_bwd(name, uc, g, b, dy, dy_col, tr):
    t, d = uc.shape

    def fn(ucv, gv, bv, dyv):
        mu = jnp.mean(ucv, axis=-1, keepdims=True)
        xc = ucv - mu
        rs = lax.rsqrt(jnp.mean(xc * xc, axis=-1, keepdims=True) + LN_EPS)
        xh = xc * rs
        ln = xh * gv + bv
        s = _sigmoid(ln)
        dln = dyv * (s * (1.0 + ln * (1.0 - s)))
        dg = jnp.sum(dln * xh, axis=0, keepdims=True)
        db = jnp.sum(dln, axis=0, keepdims=True)
        dxh = dln * gv
        duc = rs * (dxh - jnp.mean(dxh, axis=-1, keepdims=True) - xh * jnp.mean(dxh * xh, axis=-1, keepdims=True))
        return duc, dg, db

    row, fix = (lambda i: (i, 0)), (lambda i: (0, 0))
    return _call(fn, name, (t // tr,),
                 [(uc, (tr, d), row), (g, (1, d), fix), (b, (1, d), fix), (dy, (tr, d), lambda i: (i, dy_col))],
                 [((t, d), (tr, d), row, False), ((1, d), (1, d), fix, True), ((1, d), (1, d), fix, True)], acc_axis=0)


def _conv_fwd(xp, w_ref, width, t):
    acc = None
    for j in range(width):
        term = xp[pl.ds(CONV_PAD - (width - 1) + j, t), :] * w_ref[pl.ds(j, 1), :]
        acc = term if acc is None else acc + term
    return acc


def _conv_bwd_in(dyp, w_ref, width, t):
    acc = None
    for j in range(width):
        term = dyp[pl.ds(width - 1 - j, t), :] * w_ref[pl.ds(j, 1), :]
        acc = term if acc is None else acc + term
    return acc


def _conv_bwd_w(dy, xp, dw_ref, width, t):
    for j in range(width):
        dw_ref[pl.ds(j, 1), :] = jnp.sum(dy * xp[pl.ds(CONV_PAD - (width - 1) + j, t), :], axis=0, keepdims=True)


def _store_front(xp, x, t):
    xp[pl.ds(0, CONV_PAD), :] = jnp.zeros((CONV_PAD, LANES), F32)
    xp[pl.ds(CONV_PAD, t), :] = x


def _store_back(xp, x, t):
    xp[pl.ds(0, t), :] = x
    xp[pl.ds(t, CONV_PAD), :] = jnp.zeros((CONV_PAD, LANES), F32)


def _col_call(body, name, ncol, ins, outs, t, n_scratch):
    def spec(rows, off):
        return pl.BlockSpec((rows, LANES), lambda j, off=off: (0, j + off))

    res = pl.pallas_call(
        body, name=name, grid=(ncol,),
        in_specs=[spec(r, off) for _, r, off in ins],
        out_specs=[spec(r, 0) for r, _ in outs],
        out_shape=[jax.ShapeDtypeStruct((r, c), F32) for r, c in outs],
        scratch_shapes=[pltpu.VMEM((t + CONV_PAD, LANES), F32) for _ in range(n_scratch)],
        compiler_params=_cparams(None),
    )(*[a for a, _, _ in ins])
    return res


def _even_col_fwd(name, p, conv_a, conv_b):
    t = p.shape[0]
    nc = D_A // LANES

    def body(av, ag, gb, gc, xi, ca, cb, uc_ref, yb_ref, xp):
        _store_front(xp, av[...] * _sigmoid(ag[...]), t)
        uc_ref[...] = _conv_fwd(xp, ca, CONV_A_WIDTH, t)
        _store_front(xp, gc[...] * xi[...], t)
        yb_ref[...] = gb[...] * _conv_fwd(xp, cb, CONV_B_WIDTH, t)

    ins = [(p, t, k * nc) for k in range(5)] + [(conv_a, CONV_A_WIDTH, 0), (conv_b, CONV_B_WIDTH, 0)]
    return _col_call(body, name, nc, ins, [(t, D_A), (t, D_A)], t, 1)


def _even_col_bwd(name, p, duc, dy, conv_a, conv_b):
    t = p.shape[0]
    nc = D_A // LANES

    def body(av, ag, gb, gc, xi, duc_ref, dyb_ref, ca, cb, dav, dag, dgb, dgc, dxi, dca, dcb, xp, dyp):
        sig = _sigmoid(ag[...])
        _store_front(xp, av[...] * sig, t)
        _store_back(dyp, duc_ref[...], t)
        _conv_bwd_w(duc_ref[...], xp, dca, CONV_A_WIDTH, t)
        du = _conv_bwd_in(dyp, ca, CONV_A_WIDTH, t)
        dav[...] = du * sig
        dag[...] = du * av[...] * sig * (1.0 - sig)
        _store_front(xp, gc[...] * xi[...], t)
        zc = _conv_fwd(xp, cb, CONV_B_WIDTH, t)
        dgb[...] = dyb_ref[...] * zc
        dzc = dyb_ref[...] * gb[...]
        _conv_bwd_w(dzc, xp, dcb, CONV_B_WIDTH, t)
        _store_back(dyp, dzc, t)
        dz = _conv_bwd_in(dyp, cb, CONV_B_WIDTH, t)
        dgc[...] = dz * xi[...]
        dxi[...] = dz * gc[...]

    ins = ([(p, t, k * nc) for k in range(5)] + [(duc, t, 0), (dy, t, nc)]
           + [(conv_a, CONV_A_WIDTH, 0), (conv_b, CONV_B_WIDTH, 0)])
    outs = [(t, D_A)] * 5 + [(CONV_A_WIDTH, D_A), (CONV_B_WIDTH, D_A)]
    return _col_call(body, name, nc, ins, outs, t, 2)


def _ffn_col_fwd(name, u, conv, bias):
    t = u.shape[0]
    nc = D_FF // LANES

    def body(g_ref, v_ref, cw, b_ref, a_ref, xp):
        _store_front(xp, g_ref[...], t)
        gc = _conv_fwd(xp, cw, CONV_B_WIDTH, t) + b_ref[...]
        a_ref[...] = gc * _sigmoid(gc) * v_ref[...]

    ins = [(u, t, 0), (u, t, nc), (conv, CONV_B_WIDTH, 0), (bias, 1, 0)]
    return _col_call(body, name, nc, ins, [(t, D_FF)], t, 1)[0]


def _ffn_col_bwd(name, u, da, conv, bias):
    t = u.shape[0]
    nc = D_FF // LANES

    def body(g_ref, v_ref, da_ref, cw, b_ref, du_ref, dcw, db_ref, xp, dyp):
        _store_front(xp, g_ref[...], t)
        gc = _conv_fwd(xp, cw, CONV_B_WIDTH, t) + b_ref[...]
        s = _sigmoid(gc)

        @pl.when(pl.program_id(1) == 0)
        def _():
            dgc = da_ref[...] * v_ref[...] * (s * (1.0 + gc * (1.0 - s)))
            db_ref[...] = jnp.sum(dgc, axis=0, keepdims=True)
            _conv_bwd_w(dgc, xp, dcw, CONV_B_WIDTH, t)
            _store_back(dyp, dgc, t)
            du_ref[...] = _conv_bwd_in(dyp, cw, CONV_B_WIDTH, t)

        @pl.when(pl.program_id(1) == 1)
        def _():
            du_ref[...] = da_ref[...] * gc * s

    col = lambda rows, off: pl.BlockSpec((rows, LANES), lambda j, p: (0, j + off))
    return pl.pallas_call(
        body, name=name, grid=(nc, 2),
        in_specs=[col(t, 0), col(t, nc), col(t, 0), col(CONV_B_WIDTH, 0), col(1, 0)],
        out_specs=[pl.BlockSpec((t, LANES), lambda j, p: (0, j + nc * p)), col(CONV_B_WIDTH, 0), col(1, 0)],
        out_shape=[jax.ShapeDtypeStruct((t, 2 * D_FF), F32), jax.ShapeDtypeStruct((CONV_B_WIDTH, D_FF), F32),
                   jax.ShapeDtypeStruct((1, D_FF), F32)],
        scratch_shapes=[pltpu.VMEM((t + CONV_PAD, LANES), F32) for _ in range(2)],
        compiler_params=_cparams(None),
    )(u, u, da, conv, bias)


def _shift_fwd(name, p, col0, mu):
    t = p.shape[0]

    def body(x_ref, mu_ref, o_ref, xp):
        _store_front(xp, x_ref[...], t)
        prev = xp[pl.ds(CONV_PAD - 1, t), :]
        o_ref[...] = x_ref[...] + (prev - x_ref[...]) * mu_ref[...]

    return _col_call(body, name, RWKV_COLS // LANES, [(p, t, col0), (mu, 1, 0)], [(t, RWKV_COLS)], t, 1)[0]


def _shift_bwd(name, p, col0, mu, dprs):
    t = p.shape[0]

    def body(x_ref, mu_ref, d_ref, dx_ref, dmu_ref, xp, dyp):
        _store_front(xp, x_ref[...], t)
        prev = xp[pl.ds(CONV_PAD - 1, t), :]
        dmu_ref[...] = jnp.sum(d_ref[...] * (prev - x_ref[...]), axis=0, keepdims=True)
        dm = d_ref[...] * mu_ref[...]
        _store_back(dyp, dm, t)
        dx_ref[...] = d_ref[...] - dm + dyp[pl.ds(1, t), :]

    ins = [(p, t, col0), (mu, 1, 0), (dprs, t, 0)]
    return _col_call(body, name, RWKV_COLS // LANES, ins, [(t, RWKV_COLS), (1, RWKV_COLS)], t, 2)


def _dotm(a, b):
    return jnp.dot(a.astype(MXU_DTYPE), b.astype(MXU_DTYPE), preferred_element_type=F32)


def _hi_lo(x):
    hi = x.astype(BF16)
    return hi, (x - hi.astype(F32)).astype(BF16)


def _dot3(a, b, dims):
    ah, al = _hi_lo(a)
    bh, bl = _hi_lo(b)
    d = lambda p, q: lax.dot_general(p, q, dims, preferred_element_type=F32)
    return d(ah, bh) + (d(ah, bl) + d(al, bh))


@functools.partial(jax.custom_vjp, nondiff_argnums=(2,))
def _dot3_vjp(a, b, dims):
    return _dot3(a, b, dims)


def _dot3_fwd(a, b, dims):
    return _dot3(a, b, dims), (a, b)


def _dot3_bwd(dims, res, g):
    a, b = res
    if dims == _NN:
        return _dot3(g, b, _NT), _dot3(a, g, _TN)
    if dims == _NT:
        return _dot3(g, b, _NN), _dot3(g, a, _TN)
    return _dot3(b, g, _NT), _dot3(a, g, _NN)


_dot3_vjp.defvjp(_dot3_fwd, _dot3_bwd)


def _doth(a, b, dims=_NN):
    return _dot3_vjp(a, b, dims)


def _softplus(x):
    return jnp.where(x > 0, x, 0.0) + jnp.log(1.0 + jnp.exp(jnp.where(x > 0, -x, x)))


def _rwkv_pre(k, xl, gd, w0, w2p, a0, a2p, g2, k_k, k_a, seg):
    z = w0 + _dotm(jnp.tanh(xl), w2p)
    lw = -jnp.exp(-_softplus(-z) - 0.5)
    alpha = _sigmoid(a0 + _dotm(xl, a2p))
    g = _dotm(_sigmoid(gd), g2)
    kk = k * k_k
    kk = kk / jnp.maximum(jnp.sqrt(_doth(kk * kk, seg)), 1e-12)
    k2 = k * (1.0 + (alpha - 1.0) * k_a)
    return lw, k2, -kk, kk * alpha, g


def _rwkv_post(y, r, k2, v, g, lnx_g, lnx_b, r_k, seg):
    mean = _doth(y, seg) * (1.0 / HEAD_DIM)
    yc = y - mean
    var = _doth(yc * yc, seg) * (1.0 / HEAD_DIM)
    yo = yc * lax.rsqrt(var + RWKV_GN_EPS) * lnx_g + lnx_b
    bonus = _doth(r * k2 * r_k, seg) * v
    return (yo + bonus) * g


def _rwkv_pre_fwd(name, prs, prm, seg, tr):
    t = prs.shape[0]
    row = lambda i: (i, 0)
    fix = lambda i: (0, 0)
    ins = [(prs, (tr, D_R), lambda i: (i, 1)), (prs, (tr, LANES), lambda i: (i, 12)), (prs, (tr, LANES), lambda i: (i, 13)),
           (prm["w0"], (1, D_R), fix), (prm["w2p"], (LANES, D_R), fix), (prm["a0"], (1, D_R), fix),
           (prm["a2p"], (LANES, D_R), fix), (prm["g2"], (LANES, D_R), fix), (prm["k_k"], (1, D_R), fix),
           (prm["k_a"], (1, D_R), fix), (seg, (D_R, D_R), fix)]
    return _call(_rwkv_pre, name, (t // tr,), ins, [((t, D_R), (tr, D_R), row, False)] * 5)


def _rwkv_pre_bwd(name, prs, prm, seg, cts, tr):
    t = prs.shape[0]

    def fn(k, xl, gd, w0, w2p, a0, a2p, g2, k_k, k_a, segv, *ct):
        _, vjp = jax.vjp(lambda *a: _rwkv_pre(*a, segv), k, xl, gd, w0, w2p, a0, a2p, g2, k_k, k_a)
        return vjp(tuple(ct))

    row = lambda i: (i, 0)
    fix = lambda i: (0, 0)
    ins = [(prs, (tr, D_R), lambda i: (i, 1)), (prs, (tr, LANES), lambda i: (i, 12)), (prs, (tr, LANES), lambda i: (i, 13)),
           (prm["w0"], (1, D_R), fix), (prm["w2p"], (LANES, D_R), fix), (prm["a0"], (1, D_R), fix),
           (prm["a2p"], (LANES, D_R), fix), (prm["g2"], (LANES, D_R), fix), (prm["k_k"], (1, D_R), fix),
           (prm["k_a"], (1, D_R), fix), (seg, (D_R, D_R), fix)] + [(c, (tr, D_R), row) for c in cts]
    outs = [((t, D_R), (tr, D_R), row, False), ((t, LANES), (tr, LANES), row, False), ((t, LANES), (tr, LANES), row, False),
            ((1, D_R), (1, D_R), fix, True), ((LANES, D_R), (LANES, D_R), fix, True), ((1, D_R), (1, D_R), fix, True),
            ((LANES, D_R), (LANES, D_R), fix, True), ((LANES, D_R), (LANES, D_R), fix, True),
            ((1, D_R), (1, D_R), fix, True), ((1, D_R), (1, D_R), fix, True)]
    return _call(fn, name, (t // tr,), ins, outs, acc_axis=0)


def _rwkv_post_ins(y, prs, k2, g, prm, seg, tr):
    row = lambda i: (i, 0)
    fix = lambda i: (0, 0)
    return [(y, (tr, D_R), row), (prs, (tr, D_R), row), (k2, (tr, D_R), row), (prs, (tr, D_R), lambda i: (i, 2)),
            (g, (tr, D_R), row), (prm["lnx_g"], (1, D_R), fix), (prm["lnx_b"], (1, D_R), fix), (prm["r_k"], (1, D_R), fix),
            (seg, (D_R, D_R), fix)]


def _rwkv_post_fwd(name, y, prs, k2, g, prm, seg, tr):
    t = y.shape[0]
    return _call(_rwkv_post, name, (t // tr,), _rwkv_post_ins(y, prs, k2, g, prm, seg, tr),
                 [((t, D_R), (tr, D_R), lambda i: (i, 0), False)])


def _rwkv_post_bwd(name, y, prs, k2, g, prm, seg, dy, dy_col, tr):
    t = y.shape[0]

    def fn(yv, r, k2v, v, gv, lg, lb, rk, segv, ct):
        _, vjp = jax.vjp(lambda *a: _rwkv_post(*a, segv), yv, r, k2v, v, gv, lg, lb, rk)
        return vjp(ct)

    row = lambda i: (i, 0)
    fix = lambda i: (0, 0)
    ins = _rwkv_post_ins(y, prs, k2, g, prm, seg, tr) + [(dy, (tr, D_R), lambda i: (i, dy_col))]
    outs = [((t, D_R), (tr, D_R), row, False)] * 5 + [((1, D_R), (1, D_R), fix, True)] * 3
    return _call(fn, name, (t // tr,), ins, outs, acc_axis=0)


def _wkv_chunk(s0, r, lw, k, v, a, b):
    c = r[0].shape[0]
    lane = lax.broadcasted_iota(jnp.int32, (1, 2 * HEAD_DIM), 1)
    first = (lane < HEAD_DIM).astype(F32)
    per_head = lambda x: jnp.concatenate([x * first, x * (1.0 - first)], axis=0)

    def time_of(shape, dim):
        i = lax.broadcasted_iota(jnp.int32, shape, dim)
        return jnp.where(i >= c, i - c, i)

    incl = (lax.broadcasted_iota(jnp.int32, (c, c), 0) >= lax.broadcasted_iota(jnp.int32, (c, c), 1)).astype(F32)
    strict2 = time_of((2 * c, 2 * c), 0) > time_of((2 * c, 2 * c), 1)
    incl2 = lax.broadcasted_iota(jnp.int32, (c, 2 * c), 0) >= time_of((c, 2 * c), 1)
    each = lambda f, *xs: [f(*x) for x in zip(*xs)]
    cum = each(lambda x: _doth(incl, x), lw)
    tot = each(lambda x: jnp.sum(x, axis=0, keepdims=True), lw)
    e_inv = each(lambda x: jnp.exp(-x), cum)
    a_st = each(lambda x, cm, l: per_head(x * jnp.exp(cm - l)), a, cum, lw)
    r_t = each(lambda x, cm: x * jnp.exp(cm), r, cum)
    b_st = each(lambda x, e: per_head(x * e), b, e_inv)
    k_st = each(lambda x, e: per_head(x * e), k, e_inv)
    v_st = each(per_head, v)
    m = each(lambda x, w: jnp.where(strict2, _doth(x, w, _NT), 0.0), a_st, b_st)
    m_k = each(lambda x, w: jnp.where(strict2, _doth(x, w, _NT), 0.0), a_st, k_st)
    u = each(lambda x, s, mk, w: _doth(x, s, _NT) + _doth(mk, w), a_st, s0, m_k, v_st)
    steps = (c - 1).bit_length()
    for s in range(steps):
        u = each(lambda x, w: x + _doth(w, x), u, m)
        if s + 1 < steps:
            m = each(lambda w: _doth(w, w), m)
    n_b = each(lambda x, w: jnp.where(incl2, _doth(x, w, _NT), 0.0), r_t, b_st)
    n_k = each(lambda x, w: jnp.where(incl2, _doth(x, w, _NT), 0.0), r_t, k_st)
    y = each(lambda x, s, nb, uu, nk, w: _doth(x, s, _NT) + _doth(nb, uu) + _doth(nk, w), r_t, s0, n_b, u, n_k, v_st)
    dec = each(lambda tt, cm: jnp.exp(tt - cm), tot, cum)
    s1 = each(lambda s, tt, uu, x, d, w, kk: s * jnp.exp(tt) + _doth(uu, per_head(x * d), _TN) + _doth(w, per_head(kk * d), _TN),
              s0, tot, u, b, dec, v_st, k)
    return tuple(y), tuple(s1)


WKV_PAIRS_PER_STEP = 4
PAIR = 2 * HEAD_DIM


def _wkv_fwd(name, srcs):
    t = srcs[0][0].shape[0]
    c = _chunk_len(t)
    nc = t // c
    pp = WKV_PAIRS_PER_STEP
    n_pairs = D_R // PAIR

    def body(r, lw, k, v, a, b, y_ref, st_ref, state):
        @pl.when(pl.program_id(1) == 0)
        def _():
            state[...] = jnp.zeros(state.shape, F32)

        pairs = lambda ref: tuple(ref[:, pl.ds(i * PAIR, PAIR)] for i in range(pp))
        s0 = tuple(state[i] for i in range(pp))
        y, s1 = _wkv_chunk(s0, pairs(r), pairs(lw), pairs(k), pairs(v), pairs(a), pairs(b))
        for i in range(pp):
            st_ref[i] = s0[i]
            y_ref[:, pl.ds(i * PAIR, PAIR)] = y[i]
            state[i] = s1[i]

    seq = lambda off: pl.BlockSpec((c, pp * PAIR), lambda g, j: (j, off + g))
    return pl.pallas_call(
        body, name=name, grid=(n_pairs // pp, nc), in_specs=[seq(off) for _, off in srcs],
        out_specs=[seq(0), pl.BlockSpec((pp, None, PAIR, PAIR), lambda g, j: (g, j, 0, 0))],
        out_shape=[jax.ShapeDtypeStruct((t, D_R), F32), jax.ShapeDtypeStruct((n_pairs, nc, PAIR, PAIR), F32)],
        scratch_shapes=[pltpu.VMEM((pp, PAIR, PAIR), F32)],
        compiler_params=_cparams(None),
    )(*[a for a, _ in srcs])


def _wkv_bwd(name, srcs, st, dy):
    t = srcs[0][0].shape[0]
    c = _chunk_len(t)
    nc = t // c
    pp = WKV_PAIRS_PER_STEP
    n_pairs = D_R // PAIR

    def body(r, lw, k, v, a, b, st_ref, dy_ref, dr, dlw, dk, dv, da, db, dstate):
        @pl.when(pl.program_id(1) == 0)
        def _():
            dstate[...] = jnp.zeros(dstate.shape, F32)

        half = lax.broadcasted_iota(jnp.int32, (PAIR, PAIR), 0) < HEAD_DIM
        same_head = half == (lax.broadcasted_iota(jnp.int32, (PAIR, PAIR), 1) < HEAD_DIM)
        pairs = lambda ref: tuple(ref[:, pl.ds(i * PAIR, PAIR)] for i in range(pp))
        s0 = tuple(st_ref[i] for i in range(pp))
        _, vjp = jax.vjp(_wkv_chunk, s0, pairs(r), pairs(lw), pairs(k), pairs(v), pairs(a), pairs(b))
        ds0, *dxs = vjp((pairs(dy_ref), tuple(dstate[i] for i in range(pp))))
        for i in range(pp):
            for ref, val in zip((dr, dlw, dk, dv, da, db), dxs):
                ref[:, pl.ds(i * PAIR, PAIR)] = val[i]
            dstate[i] = jnp.where(same_head, ds0[i], 0.0)

    seq = lambda off: pl.BlockSpec((c, pp * PAIR), lambda g, j: (nc - 1 - j, off + g))
    return pl.pallas_call(
        body, name=name, grid=(n_pairs // pp, nc),
        in_specs=[seq(off) for _, off in srcs]
        + [pl.BlockSpec((pp, None, PAIR, PAIR), lambda g, j: (g, nc - 1 - j, 0, 0)), seq(dy[1])],
        out_specs=[seq(0)] * 6,
        out_shape=[jax.ShapeDtypeStruct((t, D_R), F32)] * 6,
        scratch_shapes=[pltpu.VMEM((pp, PAIR, PAIR), F32)],
        compiler_params=_cparams(None),
    )(*[a for a, _ in srcs], st, dy[0])


def _rope(x, cos, sin, rot):
    return x * cos + _doth(x, rot) * sin


def _attn_block(nb, q4, kp, kc, km, vp, vc, vm, sk, cq, sq, cp, sp, cm, sm, rot):
    scale = HEAD_DIM ** -0.5
    kpr = _rope(kp, cp, sp, rot).astype(MXU_DTYPE)
    kcr = _rope(kc, cq, sq, rot).astype(MXU_DTYPE)
    kmr = _rope(km, cm, sm, rot).astype(MXU_DTYPE)
    i = lax.broadcasted_iota(jnp.int32, (BLOCK, BLOCK), 0)
    j = lax.broadcasted_iota(jnp.int32, (BLOCK, BLOCK), 1)
    nbv = jnp.zeros((BLOCK, BLOCK), jnp.int32) + nb
    ok_p = (j > i) & (nbv >= 2)
    ok_c = (j <= i) & (nbv >= 1)
    ok_m = (j >= BLOCK - N_META) & ((nbv >= 1) | (j <= i))
    outs = []
    for h in range(GQA_GROUP):
        qr = _rope(q4[h], cq, sq, rot).astype(MXU_DTYPE)
        ntdot = lambda kk: lax.dot_general(qr, kk, _NT, preferred_element_type=F32) * scale
        s_p = jnp.where(ok_p, ntdot(kpr), NEG_INF)
        s_c = jnp.where(ok_c, ntdot(kcr), NEG_INF)
        s_m = jnp.where(ok_m, ntdot(kmr), NEG_INF)
        rmax = lambda s: jnp.max(s, axis=-1, keepdims=True)
        m = lax.stop_gradient(jnp.maximum(jnp.maximum(rmax(s_p), rmax(s_c)), jnp.maximum(rmax(s_m), sk[h])))
        e_p, e_c, e_m = jnp.exp(s_p - m), jnp.exp(s_c - m), jnp.exp(s_m - m)
        rsum = lambda e: jnp.sum(e, axis=-1, keepdims=True)
        inv = 1.0 / (rsum(e_p) + rsum(e_c) + rsum(e_m) + jnp.exp(sk[h] - m))
        outs.append(_dotm(e_p * inv, vp) + _dotm(e_c * inv, vc) + _dotm(e_m * inv, vm))
    return tuple(outs)


def _attn_specs():
    cur = lambda g, n: (g, n, 0)
    prev = lambda g, n: (g, jnp.maximum(n - 1, 0), 0)
    meta = lambda g, n: (g, 0, 0)
    kv = lambda m: pl.BlockSpec((None, BLOCK, HEAD_DIM), m)
    tab = lambda m: pl.BlockSpec((BLOCK, HEAD_DIM), m)
    tcur, tprev, tmeta = (lambda g, n: (n, 0)), (lambda g, n: (jnp.maximum(n - 1, 0), 0)), (lambda g, n: (0, 0))
    qspec = pl.BlockSpec((GQA_GROUP, BLOCK, HEAD_DIM), cur)
    sspec = pl.BlockSpec((GQA_GROUP, 8, LANES), meta)
    specs = [qspec, kv(prev), kv(cur), kv(meta), kv(prev), kv(cur), kv(meta), sspec,
             tab(tcur), tab(tcur), tab(tprev), tab(tprev), tab(tmeta), tab(tmeta),
             pl.BlockSpec((HEAD_DIM, HEAD_DIM), lambda g, n: (0, 0))]
    return specs, qspec, sspec, kv


def _attn_args(q, k, v, sinks_b, cos, sin, rot):
    return (q, k, k, k, v, v, v, sinks_b, cos, sin, cos, sin, cos, sin, rot)


def _attn_fwd(name, q, k, v, sinks_b, cos, sin, rot):
    tp = q.shape[1]
    specs, qspec, _, _ = _attn_specs()

    def body(q_ref, kp, kc, km, vp, vc, vm, s_ref, cq, sq, cp, sp, cm, sm, rot_ref, o_ref):
        q4 = tuple(q_ref[h] for h in range(GQA_GROUP))
        sk = tuple(s_ref[h][0:1, 0:1] for h in range(GQA_GROUP))
        outs = _attn_block(pl.program_id(1), q4, kp[...], kc[...], km[...], vp[...], vc[...], vm[...], sk,
                           cq[...], sq[...], cp[...], sp[...], cm[...], sm[...], rot_ref[...])
        for h in range(GQA_GROUP):
            o_ref[h] = outs[h]

    return pl.pallas_call(
        body, name=name, grid=(N_KV_HEADS, tp // BLOCK), in_specs=specs, out_specs=qspec,
        out_shape=jax.ShapeDtypeStruct(q.shape, F32), compiler_params=_cparams(None),
    )(*_attn_args(q, k, v, sinks_b, cos, sin, rot))


def _attn_bwd(name, q, k, v, sinks_b, cos, sin, rot, do):
    tp = q.shape[1]
    nb = tp // BLOCK
    specs, qspec, sspec, kv = _attn_specs()

    def body(q_ref, kp, kc, km, vp, vc, vm, s_ref, cq, sq, cp, sp, cm, sm, rot_ref, do_ref,
             dq_ref, dkp, dkc, dvp, dvc, dkm, dvm, ds_ref):
        n = pl.program_id(1)
        q4 = tuple(q_ref[h] for h in range(GQA_GROUP))
        sk = tuple(s_ref[h][0:1, 0:1] for h in range(GQA_GROUP))
        tabs = (cq[...], sq[...], cp[...], sp[...], cm[...], sm[...], rot_ref[...])
        _, vjp = jax.vjp(lambda *a: _attn_block(n, *a, *tabs), q4, kp[...], kc[...], km[...], vp[...], vc[...], vm[...], sk)
        dq4, gkp, gkc, gkm, gvp, gvc, gvm, dsk = vjp(tuple(do_ref[h] for h in range(GQA_GROUP)))
        dkp[...] = gkp
        dkc[...] = gkc
        dvp[...] = gvp
        dvc[...] = gvc
        for h in range(GQA_GROUP):
            dq_ref[h] = dq4[h]

        @pl.when(n == 0)
        def _():
            dkm[...] = gkm
            dvm[...] = gvm
            for h in range(GQA_GROUP):
                ds_ref[h] = jnp.broadcast_to(dsk[h], (8, LANES))

        @pl.when(n != 0)
        def _():
            dkm[...] += gkm
            dvm[...] += gvm
            for h in range(GQA_GROUP):
                ds_ref[h] += jnp.broadcast_to(dsk[h], (8, LANES))

    part = pl.BlockSpec((None, None, BLOCK, HEAD_DIM), lambda g, n: (g, n, 0, 0))
    part_shape = jax.ShapeDtypeStruct((N_KV_HEADS, nb, BLOCK, HEAD_DIM), F32)
    meta_shape = jax.ShapeDtypeStruct((N_KV_HEADS, BLOCK, HEAD_DIM), F32)
    return pl.pallas_call(
        body, name=name, grid=(N_KV_HEADS, nb), in_specs=specs + [qspec],
        out_specs=[qspec, part, part, part, part, kv(lambda g, n: (g, 0, 0)), kv(lambda g, n: (g, 0, 0)), sspec],
        out_shape=[jax.ShapeDtypeStruct(q.shape, F32), part_shape, part_shape, part_shape, part_shape,
                   meta_shape, meta_shape, jax.ShapeDtypeStruct(sinks_b.shape, F32)],
        compiler_params=_cparams(None),
    )(*_attn_args(q, k, v, sinks_b, cos, sin, rot), do)


def _kv_combine(name, prev_part, own_part, meta):
    g, nb = own_part.shape[:2]

    def fn(own, nxt, mt):
        m = pl.program_id(1)
        one = jnp.ones((BLOCK, HEAD_DIM), F32)
        use_next = jnp.where(one * m < nb - 1, 1.0, 0.0)
        use_meta = jnp.where(one * m < 1, 1.0, 0.0)
        return own + nxt * use_next + mt * use_meta

    blk = (None, None, BLOCK, HEAD_DIM)
    return _call(fn, name, (g, nb),
                 [(own_part, blk, lambda a, m: (a, m, 0, 0)),
                  (prev_part, blk, lambda a, m: (a, jnp.minimum(m + 1, nb - 1), 0, 0)),
                  (meta, (None, BLOCK, HEAD_DIM), lambda a, m: (a, 0, 0))],
                 [((g, nb * BLOCK, HEAD_DIM), (None, BLOCK, HEAD_DIM), lambda a, m: (a, m, 0), False)])


PACK_W = 1024
ELEMENTWISE_BLOCK_BYTES = 1 << 21


def _rows_tile(rows, cols):
    cap = max(8, ELEMENTWISE_BLOCK_BYTES // (4 * cols))
    for d in range(min(rows, cap), 0, -1):
        if rows % d == 0 and d % 8 == 0:
            return d
    return rows


def _adamw(name, w, g, m, v):
    rows, cols = w.shape
    tr = _rows_tile(rows, cols)

    def fn(wv, gv, mv, vv):
        m1 = ADAM_B1 * mv + (1.0 - ADAM_B1) * gv
        v1 = ADAM_B2 * vv + (1.0 - ADAM_B2) * (gv * gv)
        m_hat = m1 / (1.0 - ADAM_B1 ** ADAM_STEP)
        v_hat = v1 / (1.0 - ADAM_B2 ** ADAM_STEP)
        return -ADAM_LR * (m_hat / (jnp.sqrt(v_hat) + ADAM_EPS) + ADAM_WD * wv), m1, v1

    blk = (tr, cols)
    row = lambda i: (i, 0)
    return _call(fn, name, (rows // tr,), [(a, blk, row) for a in (w, g, m, v)], [((rows, cols), blk, row, False)] * 3)


def _pair_add(name, g, recv, c_idx):
    s, a, b = g.shape
    half = a // 2

    def body(c_ref, a_ref, b_ref, o_ref):
        o_ref[...] = a_ref[...] + b_ref[...]

    blk = (None, half, b)
    return pl.pallas_call(
        body, name=name,
        grid_spec=pltpu.PrefetchScalarGridSpec(
            num_scalar_prefetch=1, grid=(s,),
            in_specs=[pl.BlockSpec(blk, lambda j, c: (j, c[0], 0)), pl.BlockSpec(blk, lambda j, c: (j, 0, 0))],
            out_specs=pl.BlockSpec(blk, lambda j, c: (j, 0, 0))),
        out_shape=jax.ShapeDtypeStruct((s, half, b), F32), compiler_params=_cparams(None),
    )(c_idx, g, recv)


def _sum_chips(name, parts):
    _, a, b = parts.shape
    tr = _rows_tile(a, b)

    def fn(p0, p1, p2, p3):
        return ((p0 + p1) + p2) + p3

    return _call(fn, name, (a // tr,),
                 [(parts, (None, tr, b), lambda i, k=k: (k, i, 0)) for k in range(N_CHIPS)],
                 [((a, b), (tr, b), lambda i: (i, 0), False)])


def _mesh_pos():
    return lax.axis_index("x"), lax.axis_index("y"), lax.axis_index("c")


def _other_chips(x, y):
    return [(1 - x, y), (x, 1 - y), (1 - x, 1 - y)]


_ANY = pl.BlockSpec(memory_space=pl.ANY)


def _gather_weights(name, mine, half_major):
    n = len(mine)

    def body(*refs):
        x_refs, out_refs = refs[:n], refs[n:2 * n]
        send_sems, recv_sems, local_sems = refs[2 * n:]
        x, y, c = _mesh_pos()
        me = 2 * x + y
        sibling = (x, y, 1 - c)
        chips = _other_chips(x, y)

        def dst(i, chip_idx, half):
            return out_refs[i].at[half, chip_idx] if half_major[i] else out_refs[i].at[chip_idx, half]

        def copy(i, k, src, chip_idx, half, to):
            return pltpu.make_async_remote_copy(src_ref=src, dst_ref=dst(i, chip_idx, half),
                                                send_sem=send_sems.at[6 * i + k], recv_sem=recv_sems.at[6 * i + k],
                                                device_id=to, device_id_type=MESH)

        local = [pltpu.make_async_copy(x_refs[i].at[h], dst(i, me, h), local_sems.at[2 * i + h])
                 for i in range(n) for h in range(2)]
        for cp in local:
            cp.start()
        first = [copy(i, j, x_refs[i].at[c], me, c, (*chip, c)) for i in range(n) for j, chip in enumerate(chips)]
        for cp in first:
            cp.start()
        passed = []
        for i in range(n):
            for j, (cx, cy) in enumerate(chips):
                idx = 2 * cx + cy
                copy(i, j, x_refs[i].at[c], idx, c, sibling).wait_recv()
                fwd = copy(i, 3 + j, dst(i, idx, c), idx, c, sibling)
                fwd.start()
                passed.append(fwd)
        for i in range(n):
            for j, (cx, cy) in enumerate(chips):
                copy(i, 3 + j, x_refs[i].at[c], 2 * cx + cy, 1 - c, sibling).wait_recv()
        for cp in first + passed:
            cp.wait_send()
        for cp in local:
            cp.wait()

    def out_shape(i):
        two, a, b = mine[i].shape
        return jax.ShapeDtypeStruct((two, N_CHIPS, a, b) if half_major[i] else (N_CHIPS, two, a, b), mine[i].dtype)

    return pl.pallas_call(
        body, name=name, in_specs=[_ANY] * n, out_specs=[_ANY] * n, out_shape=[out_shape(i) for i in range(n)],
        scratch_shapes=[pltpu.SemaphoreType.DMA((6 * n,)), pltpu.SemaphoreType.DMA((6 * n,)),
                        pltpu.SemaphoreType.DMA((2 * n,))],
        compiler_params=pltpu.CompilerParams(has_side_effects=True),
    )(*mine)


def _halves_to_sibling(name, units):
    n = len(units)

    def body(*refs):
        g_refs, out_refs = refs[:n], refs[n:2 * n]
        send_sems, recv_sems = refs[2 * n:]
        x, y, c = _mesh_pos()
        cps = []
        for i in range(n):
            half = units[i].shape[1] // 2
            src = g_refs[i].at[pl.ds(0, N_CHIPS), pl.ds((1 - c) * half, half)]
            cp = pltpu.make_async_remote_copy(src_ref=src, dst_ref=out_refs[i], send_sem=send_sems.at[i],
                                              recv_sem=recv_sems.at[i], device_id=(x, y, 1 - c), device_id_type=MESH)
            cp.start()
            cps.append(cp)
        for cp in cps:
            cp.wait()

    return pl.pallas_call(
        body, name=name, in_specs=[_ANY] * n, out_specs=[_ANY] * n,
        out_shape=[jax.ShapeDtypeStruct((u.shape[0], u.shape[1] // 2, u.shape[2]), u.dtype) for u in units],
        scratch_shapes=[pltpu.SemaphoreType.DMA((n,)), pltpu.SemaphoreType.DMA((n,))],
        compiler_params=pltpu.CompilerParams(has_side_effects=True),
    )(*units)


def _scatter_to_chips(name, sums):
    n = len(sums)

    def body(*refs):
        h_refs, out_refs = refs[:n], refs[n:2 * n]
        send_sems, recv_sems, local_sems = refs[2 * n:]
        x, y, c = _mesh_pos()
        me = 2 * x + y
        chips = _other_chips(x, y)
        local = [pltpu.make_async_copy(h_refs[i].at[me], out_refs[i].at[me], local_sems.at[i]) for i in range(n)]
        for cp in local:
            cp.start()

        def copy(i, j, src_idx, dst_idx):
            cx, cy = chips[j]
            return pltpu.make_async_remote_copy(src_ref=h_refs[i].at[src_idx], dst_ref=out_refs[i].at[dst_idx],
                                                send_sem=send_sems.at[3 * i + j], recv_sem=recv_sems.at[3 * i + j],
                                                device_id=(cx, cy, c), device_id_type=MESH)

        cps = [copy(i, j, 2 * chips[j][0] + chips[j][1], me) for i in range(n) for j in range(3)]
        for cp in cps:
            cp.start()
        for i in range(n):
            for j in range(3):
                copy(i, j, me, 2 * chips[j][0] + chips[j][1]).wait_recv()
        for cp in cps:
            cp.wait_send()
        for cp in local:
            cp.wait()

    return pl.pallas_call(
        body, name=name, in_specs=[_ANY] * n, out_specs=[_ANY] * n,
        out_shape=[jax.ShapeDtypeStruct(s.shape, s.dtype) for s in sums],
        scratch_shapes=[pltpu.SemaphoreType.DMA((3 * n,)), pltpu.SemaphoreType.DMA((3 * n,)), pltpu.SemaphoreType.DMA((n,))],
        compiler_params=pltpu.CompilerParams(has_side_effects=True),
    )(*sums)


def _join_halves(name, halves, dests, result_shapes):
    n = len(halves)
    nr = len(result_shapes)

    def body(*refs):
        h_refs, out_refs = refs[:n], refs[n:n + nr]
        send_sems, recv_sems, local_sems = refs[n + nr:]
        x, y, c = _mesh_pos()

        def place(i, half):
            r, l = dests[i]
            return out_refs[r].at[l, half]

        local = [pltpu.make_async_copy(h_refs[i], place(i, c), local_sems.at[i]) for i in range(n)]
        for cp in local:
            cp.start()

        def copy(i, half):
            return pltpu.make_async_remote_copy(src_ref=h_refs[i], dst_ref=place(i, half), send_sem=send_sems.at[i],
                                                recv_sem=recv_sems.at[i], device_id=(x, y, 1 - c), device_id_type=MESH)

        cps = [copy(i, c) for i in range(n)]
        for cp in cps:
            cp.start()
        for i in range(n):
            copy(i, 1 - c).wait_recv()
        for cp in cps:
            cp.wait_send()
        for cp in local:
            cp.wait()

    return pl.pallas_call(
        body, name=name, in_specs=[_ANY] * n, out_specs=[_ANY] * nr,
        out_shape=[jax.ShapeDtypeStruct(s, F32) for s in result_shapes],
        scratch_shapes=[pltpu.SemaphoreType.DMA((n,)), pltpu.SemaphoreType.DMA((n,)), pltpu.SemaphoreType.DMA((n,))],
        compiler_params=pltpu.CompilerParams(has_side_effects=True),
    )(*halves)


def _pack(arrays, dtype, rows_multiple):
    flat = jnp.concatenate([a.reshape(-1).astype(dtype) for a in arrays])
    unit = rows_multiple * PACK_W
    total = -(-flat.shape[0] // unit) * unit
    return jnp.pad(flat, (0, total - flat.shape[0])).reshape(total // PACK_W, PACK_W)


def _unpack(flat, shapes):
    out, off = [], 0
    for s in shapes:
        n = 1
        for d in s:
            n *= d
        out.append(flat[..., off:off + n].reshape(flat.shape[:-1] + tuple(s)))
        off += n
    return out


def _ffn_fwd(tag, l, h, g, w_up, conv, bias, w_down, tm):
    hn = _rms_fwd(f"{tag}_norm", h, g, tm)
    u = _mm_cs(f"{tag}_up", hn, w_up, l, tm)
    act = _ffn_col_fwd(f"{tag}_glu", u, conv, bias)
    h_out = _mm_full(f"{tag}_down", act, w_down, l, tm, D_FF // 2, add=h)
    return h_out, (hn, u, act)


def _ffn_bwd(tag, l, h, g, w_up, conv, bias, w_down, saved, dh, tm):
    hn, u, act = saved
    da = _mm_nt_full(f"{tag}_down_dx", dh, w_down, l, tm, D_FF // 2)
    dw_down = _mm_tn_full(f"{tag}_down_dw", act, dh, tm, D_FF // 2)
    du, dconv, dbias = _ffn_col_bwd(f"{tag}_glu_bwd", u, da, conv, bias)
    dw_up = _mm_tn_cs(f"{tag}_up_dw", hn, du, N_CHIPS, tm)
    dhn = _mm_nt_cs(f"{tag}_up_dx", du, w_up, l, tm)
    dh, dg = _rms_bwd(f"{tag}_norm_bwd", h, g, dhn, dh, tm)
    return dh, dict(norm=dg, w_up=dw_up, conv=dconv, bias=dbias, w_down=dw_down)


def _to_heads(z, nh, pad):
    t = z.shape[0]
    return jnp.pad(z.reshape(t, nh, HEAD_DIM).transpose(1, 0, 2), ((0, 0), (pad, 0), (0, 0)))


def _from_heads(z, pad):
    nh, tp, _ = z.shape
    return z[:, pad:].transpose(1, 0, 2).reshape(tp - pad, nh * HEAD_DIM)


def _rope_tables(tp, pad):
    half = HEAD_DIM // 2
    inv = ROPE_THETA ** (-jnp.arange(half, dtype=F32) / half)
    ang = (jnp.arange(tp, dtype=F32) - pad)[:, None] * inv[None, :]
    cos, sin = jnp.cos(ang), jnp.sin(ang)
    rot = jnp.zeros((HEAD_DIM, HEAD_DIM), F32)
    idx = jnp.arange(half)
    rot = rot.at[idx + half, idx].set(-1.0).at[idx, idx + half].set(1.0)
    return jnp.concatenate([cos, cos], axis=1), jnp.concatenate([sin, sin], axis=1), rot


def _local_step(x, tgt, w):
    seq = x.shape[0]
    t = seq + N_META
    tm = _row_tile(t, 704)
    tr = _row_tile(t, 352)
    pad = BLOCK - N_META
    grads = {}

    h0 = jnp.concatenate([w["meta_tokens"], x], axis=0)
    tgt_p = jnp.pad(tgt, ((N_META, 0), (0, 0)))

    hn0 = _rms_fwd("l0_norm", h0, w["norm_mix"][0:1], tm)
    p0 = _mm_cs("l0_in", hn0, w["ev_w_in"], 0, tm)
    uc, yb = _even_col_fwd("l0_convs", p0, w["ev_conv_a"], w["ev_conv_b"])
    ya = _even_ln_fwd("l0_ln", uc, w["ev_ln_a_g"], w["ev_ln_a_b"], tm)
    y0 = jnp.concatenate([ya, yb], axis=1)
    h1 = _mm_full("l0_out", y0, w["ev_w_out"], 0, tm, D_MODEL, add=h0)
    f0 = (0, h1, w["norm_ffn"][0:1], w["ff_w_up"], w["ff_conv"][0], w["ff_conv_b"][0:1], w["ff_w_down"])
    h2, ffn0 = _ffn_fwd("f0", *f0, tm)

    hn2 = _rms_fwd("l1_norm", h2, w["norm_mix"][1:2], tm)
    p1 = _mm_cs("l1_in", hn2, w["od_w_in"], 0, tm)
    cos, sin, rot = _rope_tables(t + pad, pad)
    qh = _to_heads(p1[:, :D_ATT], N_Q_HEADS, pad)
    kh = _to_heads(p1[:, D_ATT:D_ATT + D_KV], N_KV_HEADS, pad)
    vh = _to_heads(p1[:, D_ATT + D_KV:D_ATT + 2 * D_KV], N_KV_HEADS, pad)
    sinks_b = jnp.broadcast_to(w["od_sinks"].reshape(N_Q_HEADS, 1, 1), (N_Q_HEADS, 8, LANES))
    y_att = _from_heads(_attn_fwd("l1_attn", qh, kh, vh, sinks_b, cos, sin, rot), pad)

    col0 = (D_ATT + 2 * D_KV) // LANES
    ch = jnp.arange(D_R) // HEAD_DIM
    seg = (ch[:, None] == ch[None, :]).astype(F32)
    prm = dict(w0=w["od_w0"], a0=w["od_a0"], g2=w["od_g2"], k_k=w["od_k_k"], k_a=w["od_k_a"],
               lnx_g=w["od_lnx_g"], lnx_b=w["od_lnx_b"], r_k=w["od_r_k"].reshape(1, D_R),
               w2p=jnp.concatenate([w["od_w2"], jnp.zeros((LORA_A, D_R), F32)], axis=0),
               a2p=jnp.concatenate([jnp.zeros((LORA_W, D_R), F32), w["od_a2"]], axis=0))
    prs = _shift_fwd("l1_shift", p1, col0, w["od_mu"])
    lw, k2, a_, b_, gate_r = _rwkv_pre_fwd("l1_rwkv_pre", prs, prm, seg, tr)
    v_off = 2 * D_R // (WKV_PAIRS_PER_STEP * PAIR)
    scan_in = [(prs, 0), (lw, 0), (k2, 0), (prs, v_off), (a_, 0), (b_, 0)]
    y_scan, states = _wkv_fwd("l1_wkv", scan_in)
    y_rwkv = _rwkv_post_fwd("l1_rwkv_post", y_scan, prs, k2, gate_r, prm, seg, tr)
    y1 = jnp.concatenate([y_att, y_rwkv], axis=1)
    h3 = _mm_full("l1_out", y1, w["od_w_out"], 0, tm, D_MODEL, add=h2)
    f1 = (1, h3, w["norm_ffn"][1:2], w["ff_w_up"], w["ff_conv"][1], w["ff_conv_b"][1:2], w["ff_w_down"])
    h4, ffn1 = _ffn_fwd("f1", *f1, tm)

    loss_blk, dh, d_norm_final = _final_loss("final", h4, w["norm_final"], tgt_p, tm)
    grads["norm_final"] = d_norm_final

    dh, gf1 = _ffn_bwd("f1", *f1, ffn1, dh, tm)
    dy1 = _mm_nt_full("l1_out_dx", dh, w["od_w_out"], 0, tm, D_MODEL)
    grads["od_w_out"] = _mm_tn_full("l1_out_dw", y1, dh, tm, D_MODEL // 2)
    dy_scan, dr_p, dk2_p, dv_p, dgate_r, grads["od_lnx_g"], grads["od_lnx_b"], d_rk = _rwkv_post_bwd(
        "l1_rwkv_post_bwd", y_scan, prs, k2, gate_r, prm, seg, dy1, 1, tr)
    grads["od_r_k"] = d_rk.reshape(N_R_HEADS, HEAD_DIM)
    dr_s, dlw, dk2_s, dv_s, da_, db_ = _wkv_bwd("l1_wkv_bwd", scan_in, states, (dy_scan, 0))
    dk, dxl, dgd, grads["od_w0"], dw2p, grads["od_a0"], da2p, grads["od_g2"], grads["od_k_k"], grads["od_k_a"] = (
        _rwkv_pre_bwd("l1_rwkv_pre_bwd", prs, prm, seg, (dlw, dk2_s + dk2_p, da_, db_, dgate_r), tr))
    grads["od_w2"] = dw2p[:LORA_W]
    grads["od_a2"] = da2p[LORA_W:]
    dprs = jnp.concatenate([dr_s + dr_p, dk, dv_s + dv_p, dxl, dgd], axis=1)
    dpr, grads["od_mu"] = _shift_bwd("l1_shift_bwd", p1, col0, w["od_mu"], dprs)
    doh = _to_heads(dy1[:, :D_ATT], N_Q_HEADS, pad)
    dqh, dkp, dkc, dvp, dvc, dkm, dvm, dsinks = _attn_bwd("l1_attn_bwd", qh, kh, vh, sinks_b, cos, sin, rot, doh)
    grads["od_sinks"] = dsinks[:, 0, 0].reshape(1, N_Q_HEADS)
    dkh = _kv_combine("l1_attn_dk", dkp, dkc, dkm)
    dvh = _kv_combine("l1_attn_dv", dvp, dvc, dvm)
    dp1 = jnp.concatenate([_from_heads(dqh, pad), _from_heads(dkh, pad), _from_heads(dvh, pad), dpr], axis=1)
    grads["od_w_in"] = _mm_tn_cs("l1_in_dw", hn2, dp1, N_CHIPS, tm)
    dhn2 = _mm_nt_cs("l1_in_dx", dp1, w["od_w_in"], 0, tm)
    dh, d_mix1 = _rms_bwd("l1_norm_bwd", h2, w["norm_mix"][1:2], dhn2, dh, tm)

    dh, gf0 = _ffn_bwd("f0", *f0, ffn0, dh, tm)
    dy0 = _mm_nt_full("l0_out_dx", dh, w["ev_w_out"], 0, tm, D_MODEL)
    grads["ev_w_out"] = _mm_tn_full("l0_out_dw", y0, dh, tm, D_MODEL // 2)
    duc, grads["ev_ln_a_g"], grads["ev_ln_a_b"] = _even_ln_bwd("l0_ln_bwd", uc, w["ev_ln_a_g"], w["ev_ln_a_b"], dy0, 0, tm)
    *dparts, grads["ev_conv_a"], grads["ev_conv_b"] = _even_col_bwd("l0_convs_bwd", p0, duc, dy0, w["ev_conv_a"], w["ev_conv_b"])
    dp0 = jnp.concatenate(dparts, axis=1)
    grads["ev_w_in"] = _mm_tn_cs("l0_in_dw", hn0, dp0, N_CHIPS, tm)
    dhn0 = _mm_nt_cs("l0_in_dx", dp0, w["ev_w_in"], 0, tm)
    dh, d_mix0 = _rms_bwd("l0_norm_bwd", h0, w["norm_mix"][0:1], dhn0, dh, tm)

    grads["norm_mix"] = jnp.concatenate([d_mix0, d_mix1], axis=0)
    grads["norm_ffn"] = jnp.concatenate([gf0["norm"], gf1["norm"]], axis=0)
    grads["ff_w_up"] = [gf0["w_up"], gf1["w_up"]]
    grads["ff_conv"] = jnp.stack([gf0["conv"], gf1["conv"]])
    grads["ff_conv_b"] = jnp.concatenate([gf0["bias"], gf1["bias"]], axis=0)
    grads["ff_w_down"] = [gf0["w_down"], gf1["w_down"]]
    grads["meta_tokens"] = dh[:N_META]
    return loss_blk[0, 0], dh[N_META:], grads


SHARD_AXIS = {
    "meta_tokens": 1, "norm_mix": None, "norm_ffn": None, "norm_final": None,
    "ev_w_in": 2, "ev_conv_a": 2, "ev_ln_a_g": None, "ev_ln_a_b": None, "ev_conv_b": 2, "ev_w_out": 1,
    "od_w_in": 2, "od_sinks": None, "od_mu": 1, "od_w0": 1, "od_w2": 2, "od_a0": 1, "od_a2": 2, "od_g2": 2,
    "od_k_k": 1, "od_k_a": 1, "od_r_k": None, "od_lnx_g": 1, "od_lnx_b": 1, "od_w_out": 1,
    "ff_w_up": 2, "ff_conv": 2, "ff_conv_b": None, "ff_w_down": 1,
}
WEIGHTS = list(SHARD_AXIS)
BIG = ("ev_w_in", "ev_w_out", "od_w_in", "od_w_out", "ff_w_up", "ff_w_down")
SHARDED = [n for n in WEIGHTS if SHARD_AXIS[n] is not None]
SMALL = [n for n in SHARDED if n not in BIG]
REPLICATED = [n for n in WEIGHTS if SHARD_AXIS[n] is None]


def _join(g, axis):
    return jnp.concatenate([g[k] for k in range(N_CHIPS)], axis=axis)


def _split(full, axis):
    return jnp.stack(jnp.split(full, N_CHIPS, axis=axis))


def _full_weights(gathered, repl):
    w = {}
    sq = lambda a: a.reshape(a.shape[1:]) if a.shape[0] == 1 else a
    for n in REPLICATED:
        w[n] = repl[n]
    w["norm_final"] = repl["norm_final"].reshape(1, D_MODEL)
    for n in ("ev_ln_a_g", "ev_ln_a_b"):
        w[n] = repl[n].reshape(1, D_A)
    w["od_r_k"] = repl["od_r_k"][0]
    w["meta_tokens"] = _join(gathered["meta_tokens"], 1)
    for n in ("ev_conv_a", "ev_conv_b", "od_w2", "od_a2", "od_g2"):
        w[n] = sq(_join(gathered[n], 2))
    for n in ("od_mu", "od_w0", "od_a0", "od_k_k", "od_k_a", "od_lnx_g", "od_lnx_b"):
        w[n] = _join(gathered[n], 1)
    w["ff_conv"] = _join(gathered["ff_conv"], 2)
    return w


def _shard_grads(grads):
    out = {}
    for n in REPLICATED:
        out[n] = grads[n]
    out["norm_final"] = grads["norm_final"].reshape(D_MODEL)
    out["od_r_k"] = grads["od_r_k"][None]
    out["meta_tokens"] = _split(grads["meta_tokens"], 1)
    for n in ("ev_conv_a", "ev_conv_b", "od_w2", "od_a2", "od_g2"):
        out[n] = _split(grads[n][None], 2)
    for n in ("od_mu", "od_w0", "od_a0", "od_k_k", "od_k_a", "od_lnx_g", "od_lnx_b"):
        out[n] = _split(grads[n], 1)
    out["ff_conv"] = _split(grads["ff_conv"], 2)
    return out


def kernel(x, meta_tokens, norm_mix, norm_ffn, norm_final, ev_w_in, ev_conv_a, ev_ln_a_g, ev_ln_a_b, ev_conv_b, ev_w_out, od_w_in, od_sinks, od_mu, od_w0, od_w2, od_a0, od_a2, od_g2, od_k_k, od_k_a, od_r_k, od_lnx_g, od_lnx_b, od_w_out, ff_w_up, ff_conv, ff_conv_b, ff_w_down, loss_target, m_meta_tokens, m_norm_mix, m_norm_ffn, m_norm_final, m_ev_w_in, m_ev_conv_a, m_ev_ln_a_g, m_ev_ln_a_b, m_ev_conv_b, m_ev_w_out, m_od_w_in, m_od_sinks, m_od_mu, m_od_w0, m_od_w2, m_od_a0, m_od_a2, m_od_g2, m_od_k_k, m_od_k_a, m_od_r_k, m_od_lnx_g, m_od_lnx_b, m_od_w_out, m_ff_w_up, m_ff_conv, m_ff_conv_b, m_ff_w_down, v_meta_tokens, v_norm_mix, v_norm_ffn, v_norm_final, v_ev_w_in, v_ev_conv_a, v_ev_ln_a_g, v_ev_ln_a_b, v_ev_conv_b, v_ev_w_out, v_od_w_in, v_od_sinks, v_od_mu, v_od_w0, v_od_w2, v_od_a0, v_od_a2, v_od_g2, v_od_k_k, v_od_k_a, v_od_r_k, v_od_lnx_g, v_od_lnx_b, v_od_w_out, v_ff_w_up, v_ff_conv, v_ff_conv_b, v_ff_w_down):
    wts = dict(meta_tokens=meta_tokens, norm_mix=norm_mix, norm_ffn=norm_ffn, norm_final=norm_final, ev_w_in=ev_w_in, ev_conv_a=ev_conv_a, ev_ln_a_g=ev_ln_a_g, ev_ln_a_b=ev_ln_a_b, ev_conv_b=ev_conv_b, ev_w_out=ev_w_out, od_w_in=od_w_in, od_sinks=od_sinks, od_mu=od_mu, od_w0=od_w0, od_w2=od_w2, od_a0=od_a0, od_a2=od_a2, od_g2=od_g2, od_k_k=od_k_k, od_k_a=od_k_a, od_r_k=od_r_k, od_lnx_g=od_lnx_g, od_lnx_b=od_lnx_b, od_w_out=od_w_out, ff_w_up=ff_w_up, ff_conv=ff_conv, ff_conv_b=ff_conv_b, ff_w_down=ff_w_down)
    mom = dict(meta_tokens=m_meta_tokens, norm_mix=m_norm_mix, norm_ffn=m_norm_ffn, norm_final=m_norm_final, ev_w_in=m_ev_w_in, ev_conv_a=m_ev_conv_a, ev_ln_a_g=m_ev_ln_a_g, ev_ln_a_b=m_ev_ln_a_b, ev_conv_b=m_ev_conv_b, ev_w_out=m_ev_w_out, od_w_in=m_od_w_in, od_sinks=m_od_sinks, od_mu=m_od_mu, od_w0=m_od_w0, od_w2=m_od_w2, od_a0=m_od_a0, od_a2=m_od_a2, od_g2=m_od_g2, od_k_k=m_od_k_k, od_k_a=m_od_k_a, od_r_k=m_od_r_k, od_lnx_g=m_od_lnx_g, od_lnx_b=m_od_lnx_b, od_w_out=m_od_w_out, ff_w_up=m_ff_w_up, ff_conv=m_ff_conv, ff_conv_b=m_ff_conv_b, ff_w_down=m_ff_w_down)
    var = dict(meta_tokens=v_meta_tokens, norm_mix=v_norm_mix, norm_ffn=v_norm_ffn, norm_final=v_norm_final, ev_w_in=v_ev_w_in, ev_conv_a=v_ev_conv_a, ev_ln_a_g=v_ev_ln_a_g, ev_ln_a_b=v_ev_ln_a_b, ev_conv_b=v_ev_conv_b, ev_w_out=v_ev_w_out, od_w_in=v_od_w_in, od_sinks=v_od_sinks, od_mu=v_od_mu, od_w0=v_od_w0, od_w2=v_od_w2, od_a0=v_od_a0, od_a2=v_od_a2, od_g2=v_od_g2, od_k_k=v_od_k_k, od_k_a=v_od_k_a, od_r_k=v_od_r_k, od_lnx_g=v_od_lnx_g, od_lnx_b=v_od_lnx_b, od_w_out=v_od_w_out, ff_w_up=v_ff_w_up, ff_conv=v_ff_conv, ff_conv_b=v_ff_conv_b, ff_w_down=v_ff_w_down)

    def halves(a):
        l, rows, cols = a.shape
        return a if l == 2 else a.reshape(2, rows // 2, cols)

    small_mine = _pack([wts[n] for n in SMALL], F32, 2 * 8)
    mine = [halves(wts[n].astype(MXU_DTYPE)) for n in BIG] + [small_mine.reshape(2, -1, PACK_W)]
    got = _gather_weights("gather_weights", mine, [n == "ff_w_down" for n in BIG] + [False])
    gathered = dict(zip(SMALL, _unpack(got[-1].reshape(N_CHIPS, -1), [wts[n].shape for n in SMALL])))
    w_full = _full_weights(gathered, wts)
    for n, g in zip(BIG, got):
        if n == "ff_w_down":
            w_full[n] = g.reshape(2, D_FF, D_MODEL)
        elif n in ("ev_w_out", "od_w_out"):
            w_full[n] = g.reshape(1, D_MODEL, D_MODEL)
        else:
            w_full[n] = g.reshape((N_CHIPS,) + wts[n].shape)

    loss_local, grad_x, grads = _local_step(x[0], loss_target[0], w_full)
    loss = lax.psum(loss_local, ("x", "y", "c"))

    sg = _shard_grads(grads)
    small_rows = [jnp.concatenate([sg[n][k].reshape(-1) for n in SMALL] + [sg[n].reshape(-1) for n in REPLICATED])
                  for k in range(N_CHIPS)]
    n_el = small_rows[0].shape[0]
    n_rows = -(-n_el // (16 * PACK_W)) * 16
    small_unit = jnp.stack([jnp.pad(r, (0, n_rows * PACK_W - n_el)).reshape(n_rows, PACK_W) for r in small_rows])
    out_rows = D_MODEL // N_CHIPS
    ff_rows = D_FF // N_CHIPS
    units = [grads["ev_w_in"], grads["od_w_in"],
             grads["ev_w_out"].reshape(N_CHIPS, out_rows, D_MODEL), grads["od_w_out"].reshape(N_CHIPS, out_rows, D_MODEL),
             grads["ff_w_up"][0], grads["ff_w_up"][1],
             grads["ff_w_down"][0].reshape(N_CHIPS, ff_rows, D_MODEL), grads["ff_w_down"][1].reshape(N_CHIPS, ff_rows, D_MODEL),
             small_unit]
    dests = [(0, 0), (1, 0), (2, 0), (3, 0), (4, 0), (4, 1), (5, 0), (5, 1), (6, 0)]
    results = ["ev_w_in", "od_w_in", "ev_w_out", "od_w_out", "ff_w_up", "ff_w_down", None]
    result_shapes = [None] * len(results)
    for u, (r, l) in zip(units, dests):
        result_shapes[r] = (l + 1, 2, u.shape[1] // 2, u.shape[2])
    c_idx = lax.axis_index("c").astype(jnp.int32).reshape(1)
    from_sibling = _halves_to_sibling("grads_to_sibling", units)
    chip_sums = [_pair_add(f"grads_pair_add{i}", u, r, c_idx) for i, (u, r) in enumerate(zip(units, from_sibling))]
    from_chips = _scatter_to_chips("grads_to_chips", chip_sums)
    reduced = [_sum_chips(f"grads_chip_sum{i}", p) for i, p in enumerate(from_chips)]
    joined = _join_halves("grads_join", reduced, dests, result_shapes)

    outs = {"grad": {}, "delta": {}, "new_m": {}, "new_v": {}}
    for n, g in zip(results[:-1], joined):
        shape = wts[n].shape
        flat = lambda a: a.reshape(-1, shape[-1])
        new = _adamw("adamw_" + n, flat(wts[n]), flat(g), flat(mom[n]), flat(var[n]))
        for tag, arr in zip(("grad", "delta", "new_m", "new_v"), (g,) + tuple(new)):
            outs[tag][n] = arr.reshape(shape)
    order = SMALL + REPLICATED
    packed = lambda d: jnp.pad(jnp.concatenate([d[n].reshape(-1) for n in order]),
                               (0, n_rows * PACK_W - n_el)).reshape(n_rows, PACK_W)
    g_small = joined[-1].reshape(n_rows, PACK_W)
    new = _adamw("adamw_small", packed(wts), g_small, packed(mom), packed(var))
    for tag, arr in zip(("grad", "delta", "new_m", "new_v"), (g_small,) + tuple(new)):
        outs[tag].update(zip(order, _unpack(arr.reshape(-1), [wts[n].shape for n in order])))
    return (loss, grad_x[None], *[outs["grad"][n] for n in WEIGHTS], *[outs["delta"][n] for n in WEIGHTS],
            *[outs["new_m"][n] for n in WEIGHTS], *[outs["new_v"][n] for n in WEIGHTS])
```

```python
import functools

import jax
import jax.numpy as jnp
from jax import lax
from jax.experimental import pallas as pl
from jax.experimental.pallas import tpu as pltpu

F32 = jnp.float32
BF16 = jnp.bfloat16
HI = lax.Precision.HIGHEST
MXU_DTYPE = BF16
GRAD_WIRE_DTYPE = BF16

D_MODEL = 1024
N_META = 16
RMS_EPS = 1e-6
LN_EPS = 1e-5
D_A = 512
CONV_A_WIDTH = 31
CONV_B_WIDTH = 3
HEAD_DIM = 64
N_Q_HEADS = 8
N_KV_HEADS = 2
GQA_GROUP = 4
D_ATT = 512
D_KV = 128
BLOCK = 128
ROPE_THETA = 10000.0
D_R = 512
N_R_HEADS = 8
LORA_W = 64
LORA_A = 64
LORA_G = 128
RWKV_GN_EPS = 64e-5
RWKV_COLS = 3 * D_R + LORA_W + LORA_A + LORA_G
D_FF = 2816
NEG_INF = -1e30
ADAM_LR = 0.001
ADAM_B1 = 0.9
ADAM_B2 = 0.999
ADAM_EPS = 1e-08
ADAM_WD = 0.01
ADAM_STEP = 10

N_CHIPS = 4
LANES = 128
CONV_PAD = 32
VMEM_LIMIT_V7X = 56 * 1024 * 1024
MESH = pl.DeviceIdType.MESH


def _cparams(sem=None):
    return pltpu.CompilerParams(dimension_semantics=sem, vmem_limit_bytes=VMEM_LIMIT_V7X)


def _row_tile(t, cap):
    for d in range(min(t, cap), 0, -1):
        if t % d == 0 and d % 16 == 0:
            return d
    return t


def _chunk_len(t):
    for d in (64, 48, 32, 16, 8):
        if t % d == 0:
            return d
    raise ValueError(t)


def _call(fn, name, grid, ins, outs, acc_axis=None, sem=None):
    n_in, n_out = len(ins), len(outs)

    def body(*refs):
        vals = fn(*[r[...] for r in refs[:n_in]])
        if not isinstance(vals, (tuple, list)):
            vals = (vals,)
        for r, v, o in zip(refs[n_in:n_in + n_out], vals, outs):
            if o[3]:
                first = pl.program_id(acc_axis) == 0

                @pl.when(first)
                def _(r=r, v=v):
                    r[...] = v

                @pl.when(jnp.logical_not(first))
                def _(r=r, v=v):
                    r[...] += v
            else:
                r[...] = v

    res = pl.pallas_call(
        body, name=name, grid=grid,
        in_specs=[pl.BlockSpec(b, m) for _, b, m in ins],
        out_specs=[pl.BlockSpec(o[1], o[2]) for o in outs],
        out_shape=[jax.ShapeDtypeStruct(o[0], F32) for o in outs],
        compiler_params=_cparams(sem),
    )(*[a for a, _, _ in ins])
    return res if n_out > 1 else res[0]


def _matmul(name, a, b, *, dims, grid, a_spec, b_spec, o_shape, o_spec, acc_shape, nk, k_axis,
            add=None, add_spec=None):
    def body(*refs):
        if add is None:
            a_ref, b_ref, o_ref, acc = refs
        else:
            a_ref, b_ref, add_ref, o_ref, acc = refs
        k = pl.program_id(k_axis)

        @pl.when(k == 0)
        def _():
            if add is None:
                acc[...] = jnp.zeros(acc.shape, F32)
            else:
                acc[...] = add_ref[...]

        acc[...] += lax.dot_general(a_ref[...].astype(MXU_DTYPE), b_ref[...].astype(MXU_DTYPE), dims,
                                    preferred_element_type=F32)

        @pl.when(k == nk - 1)
        def _():
            o_ref[...] = acc[...]

    args = [a, b] + ([] if add is None else [add])
    specs = [a_spec, b_spec] + ([] if add is None else [add_spec])
    return pl.pallas_call(
        body, name=name, grid=grid, in_specs=specs, out_specs=o_spec,
        out_shape=jax.ShapeDtypeStruct(o_shape, F32),
        scratch_shapes=[pltpu.VMEM(acc_shape, F32)],
        compiler_params=_cparams(None),
    )(*args)


_NN = (((1,), (0,)), ((), ()))
_NT = (((1,), (1,)), ((), ()))
_TN = (((0,), (0,)), ((), ()))


def _mm_cs(name, x, wg, l, tm):
    t, k = x.shape
    s, _, _, n = wg.shape
    return _matmul(name, x, wg, dims=_NN, grid=(s, t // tm, 1),
                   a_spec=pl.BlockSpec((tm, k), lambda j, i, kk: (i, 0)),
                   b_spec=pl.BlockSpec((None, None, k, n), lambda j, i, kk: (j, l, 0, 0)),
                   o_shape=(t, s * n), o_spec=pl.BlockSpec((tm, n), lambda j, i, kk: (i, j)),
                   acc_shape=(tm, n), nk=1, k_axis=2)


def _mm_full(name, x, w, l, tm, tk, add=None):
    t, k = x.shape
    n = w.shape[2]
    nk = k // tk
    return _matmul(name, x, w, dims=_NN, grid=(t // tm, 1, nk),
                   a_spec=pl.BlockSpec((tm, tk), lambda i, j, kk: (i, kk)),
                   b_spec=pl.BlockSpec((None, tk, n), lambda i, j, kk: (l, kk, 0)),
                   o_shape=(t, n), o_spec=pl.BlockSpec((tm, n), lambda i, j, kk: (i, 0)),
                   acc_shape=(tm, n), nk=nk, k_axis=2,
                   add=add, add_spec=pl.BlockSpec((tm, n), lambda i, j, kk: (i, 0)))


def _mm_nt_cs(name, dy, wg, l, tm, add=None):
    t = dy.shape[0]
    s, _, k, n = wg.shape
    return _matmul(name, dy, wg, dims=_NT, grid=(t // tm, 1, s),
                   a_spec=pl.BlockSpec((tm, n), lambda i, j, kk: (i, kk)),
                   b_spec=pl.BlockSpec((None, None, k, n), lambda i, j, kk: (kk, l, 0, 0)),
                   o_shape=(t, k), o_spec=pl.BlockSpec((tm, k), lambda i, j, kk: (i, 0)),
                   acc_shape=(tm, k), nk=s, k_axis=2,
                   add=add, add_spec=pl.BlockSpec((tm, k), lambda i, j, kk: (i, 0)))


def _mm_nt_full(name, dy, w, l, tm, tko):
    t, n = dy.shape
    k = w.shape[1]
    return _matmul(name, dy, w, dims=_NT, grid=(t // tm, k // tko, 1),
                   a_spec=pl.BlockSpec((tm, n), lambda i, j, kk: (i, 0)),
                   b_spec=pl.BlockSpec((None, tko, n), lambda i, j, kk: (l, j, 0)),
                   o_shape=(t, k), o_spec=pl.BlockSpec((tm, tko), lambda i, j, kk: (i, j)),
                   acc_shape=(tm, tko), nk=1, k_axis=2)


def _mm_tn_cs(name, x, dy, s, tk):
    t, k = x.shape
    n = dy.shape[1] // s
    nk = t // tk
    return _matmul(name, x, dy, dims=_TN, grid=(s, 1, nk),
                   a_spec=pl.BlockSpec((tk, k), lambda j, i, kk: (kk, 0)),
                   b_spec=pl.BlockSpec((tk, n), lambda j, i, kk: (kk, j)),
                   o_shape=(s, k, n), o_spec=pl.BlockSpec((None, k, n), lambda j, i, kk: (j, 0, 0)),
                   acc_shape=(k, n), nk=nk, k_axis=2)


def _mm_tn_full(name, y, dh, tk, tko):
    t, k = y.shape
    n = dh.shape[1]
    nk = t // tk
    return _matmul(name, y, dh, dims=_TN, grid=(k // tko, 1, nk),
                   a_spec=pl.BlockSpec((tk, tko), lambda j, i, kk: (kk, j)),
                   b_spec=pl.BlockSpec((tk, n), lambda j, i, kk: (kk, 0)),
                   o_shape=(k, n), o_spec=pl.BlockSpec((tko, n), lambda j, i, kk: (j, 0)),
                   acc_shape=(tko, n), nk=nk, k_axis=2)


def _sigmoid(x):
    return 1.0 / (1.0 + jnp.exp(-x))


def _rms_fwd(name, h, g, tr):
    t, d = h.shape

    def fn(hv, gv):
        r = lax.rsqrt(jnp.mean(hv * hv, axis=-1, keepdims=True) + RMS_EPS)
        return hv * r * gv

    return _call(fn, name, (t // tr,), [(h, (tr, d), lambda i: (i, 0)), (g, (1, d), lambda i: (0, 0))],
                 [((t, d), (tr, d), lambda i: (i, 0), False)])


def _rms_bwd(name, h, g, dhn, dh, tr):
    t, d = h.shape

    def fn(hv, gv, dy, dh_in):
        r = lax.rsqrt(jnp.mean(hv * hv, axis=-1, keepdims=True) + RMS_EPS)
        xh = hv * r
        dg = jnp.sum(dy * xh, axis=0, keepdims=True)
        dxh = dy * gv
        dx = r * (dxh - xh * jnp.mean(dxh * xh, axis=-1, keepdims=True))
        return dh_in + dx, dg

    row = lambda i: (i, 0)
    return _call(fn, name, (t // tr,),
                 [(h, (tr, d), row), (g, (1, d), lambda i: (0, 0)), (dhn, (tr, d), row), (dh, (tr, d), row)],
                 [((t, d), (tr, d), row, False), ((1, d), (1, d), lambda i: (0, 0), True)], acc_axis=0)


def _final_loss(name, h, g, tgt, tr):
    t, d = h.shape

    def fn(hv, gv, tv):
        r = lax.rsqrt(jnp.mean(hv * hv, axis=-1, keepdims=True) + RMS_EPS)
        xh = hv * r
        row = pl.program_id(0) * tr + lax.broadcasted_iota(jnp.int32, (tr, 1), 0)
        e = jnp.where(row >= N_META, xh * gv - tv, 0.0)
        loss = jnp.broadcast_to(0.5 * jnp.sum(jnp.sum(e * e, axis=-1, keepdims=True), axis=0, keepdims=True) / d,
                                (8, LANES))
        dout = e / d
        dg = jnp.sum(dout * xh, axis=0, keepdims=True)
        dxh = dout * gv
        dx = r * (dxh - xh * jnp.mean(dxh * xh, axis=-1, keepdims=True))
        return loss, dx, dg

    row = lambda i: (i, 0)
    fix = lambda i: (0, 0)
    return _call(fn, name, (t // tr,), [(h, (tr, d), row), (g, (1, d), fix), (tgt, (tr, d), row)],
                 [((8, LANES), (8, LANES), fix, True), ((t, d), (tr, d), row, False), ((1, d), (1, d), fix, True)],
                 acc_axis=0)


def _silu_ln(uc, g, b):
    mu = jnp.mean(uc, axis=-1, keepdims=True)
    xc = uc - mu
    rs = lax.rsqrt(jnp.mean(xc * xc, axis=-1, keepdims=True) + LN_EPS)
    ln = xc * rs * g + b
    return ln * _sigmoid(ln)


def _even_ln_fwd(name, uc, g, b, tr):
    t, d = uc.shape
    row, fix = (lambda i: (i, 0)), (lambda i: (0, 0))
    return _call(_silu_ln, name, (t // tr,), [(uc, (tr, d), row), (g, (1, d), fix), (b, (1, d), fix)],
                 [((t, d), (tr, d), row, False)])


def _even_ln_bwd(name, uc, g, b, dy, dy_col, tr):
    t, d = uc.shape

    def fn(ucv, gv, bv, dyv):
        mu = jnp.mean(ucv, axis=-1, keepdims=True)
        xc = ucv - mu
        rs = lax.rsqrt(jnp.mean(xc * xc, axis=-1, keepdims=True) + LN_EPS)
        xh = xc * rs
        ln = xh * gv + bv
        s = _sigmoid(ln)
        dln = dyv * (s * (1.0 + ln * (1.0 - s)))
        dg = jnp.sum(dln * xh, axis=0, keepdims=True)
        db = jnp.sum(dln, axis=0, keepdims=True)
        dxh = dln * gv
        duc = rs * (dxh - jnp.mean(dxh, axis=-1, keepdims=True) - xh * jnp.mean(dxh * xh, axis=-1, keepdims=True))
        return duc, dg, db

    row, fix = (lambda i: (i, 0)), (lambda i: (0, 0))
    return _call(fn, name, (t // tr,),
                 [(uc, (tr, d), row), (g, (1, d), fix), (b, (1, d), fix), (dy, (tr, d), lambda i: (i, dy_col))],
                 [((t, d), (tr, d), row, False), ((1, d), (1, d), fix, True), ((1, d), (1, d), fix, True)], acc_axis=0)


def _conv_fwd(xp, w_ref, width, t):
    acc = None
    for j in range(width):
        term = xp[pl.ds(CONV_PAD - (width - 1) + j, t), :] * w_ref[pl.ds(j, 1), :]
        acc = term if acc is None else acc + term
    return acc


def _conv_bwd_in(dyp, w_ref, width, t):
    acc = None
    for j in range(width):
        term = dyp[pl.ds(width - 1 - j, t), :] * w_ref[pl.ds(j, 1), :]
        acc = term if acc is None else acc + term
    return acc


def _conv_bwd_w(dy, xp, dw_ref, width, t):
    for j in range(width):
        dw_ref[pl.ds(j, 1), :] = jnp.sum(dy * xp[pl.ds(CONV_PAD - (width - 1) + j, t), :], axis=0, keepdims=True)


def _store_front(xp, x, t):
    xp[pl.ds(0, CONV_PAD), :] = jnp.zeros((CONV_PAD, LANES), F32)
    xp[pl.ds(CONV_PAD, t), :] = x


def _store_back(xp, x, t):
    xp[pl.ds(0, t), :] = x
    xp[pl.ds(t, CONV_PAD), :] = jnp.zeros((CONV_PAD, LANES), F32)


def _col_call(body, name, ncol, ins, outs, t, n_scratch):
    def spec(rows, off):
        return pl.BlockSpec((rows, LANES), lambda j, off=off: (0, j + off))

    res = pl.pallas_call(
        body, name=name, grid=(ncol,),
        in_specs=[spec(r, off) for _, r, off in ins],
        out_specs=[spec(r, 0) for r, _ in outs],
        out_shape=[jax.ShapeDtypeStruct((r, c), F32) for r, c in outs],
        scratch_shapes=[pltpu.VMEM((t + CONV_PAD, LANES), F32) for _ in range(n_scratch)],
        compiler_params=_cparams(None),
    )(*[a for a, _, _ in ins])
    return res


def _even_col_fwd(name, p, conv_a, conv_b):
    t = p.shape[0]
    nc = D_A // LANES

    def body(av, ag, gb, gc, xi, ca, cb, uc_ref, yb_ref, xp):
        _store_front(xp, av[...] * _sigmoid(ag[...]), t)
        uc_ref[...] = _conv_fwd(xp, ca, CONV_A_WIDTH, t)
        _store_front(xp, gc[...] * xi[...], t)
        yb_ref[...] = gb[...] * _conv_fwd(xp, cb, CONV_B_WIDTH, t)

    ins = [(p, t, k * nc) for k in range(5)] + [(conv_a, CONV_A_WIDTH, 0), (conv_b, CONV_B_WIDTH, 0)]
    return _col_call(body, name, nc, ins, [(t, D_A), (t, D_A)], t, 1)


def _even_col_bwd(name, p, duc, dy, conv_a, conv_b):
    t = p.shape[0]
    nc = D_A // LANES

    def body(av, ag, gb, gc, xi, duc_ref, dyb_ref, ca, cb, dav, dag, dgb, dgc, dxi, dca, dcb, xp, dyp):
        sig = _sigmoid(ag[...])
        _store_front(xp, av[...] * sig, t)
        _store_back(dyp, duc_ref[...], t)
        _conv_bwd_w(duc_ref[...], xp, dca, CONV_A_WIDTH, t)
        du = _conv_bwd_in(dyp, ca, CONV_A_WIDTH, t)
        dav[...] = du * sig
        dag[...] = du * av[...] * sig * (1.0 - sig)
        _store_front(xp, gc[...] * xi[...], t)
        zc = _conv_fwd(xp, cb, CONV_B_WIDTH, t)
        dgb[...] = dyb_ref[...] * zc
        dzc = dyb_ref[...] * gb[...]
        _conv_bwd_w(dzc, xp, dcb, CONV_B_WIDTH, t)
        _store_back(dyp, dzc, t)
        dz = _conv_bwd_in(dyp, cb, CONV_B_WIDTH, t)
        dgc[...] = dz * xi[...]
        dxi[...] = dz * gc[...]

    ins = ([(p, t, k * nc) for k in range(5)] + [(duc, t, 0), (dy, t, nc)]
           + [(conv_a, CONV_A_WIDTH, 0), (conv_b, CONV_B_WIDTH, 0)])
    outs = [(t, D_A)] * 5 + [(CONV_A_WIDTH, D_A), (CONV_B_WIDTH, D_A)]
    return _col_call(body, name, nc, ins, outs, t, 2)


def _ffn_col_fwd(name, u, conv, bias):
    t = u.shape[0]
    nc = D_FF // LANES

    def body(g_ref, v_ref, cw, b_ref, a_ref, xp):
        _store_front(xp, g_ref[...], t)
        gc = _conv_fwd(xp, cw, CONV_B_WIDTH, t) + b_ref[...]
        a_ref[...] = gc * _sigmoid(gc) * v_ref[...]

    ins = [(u, t, 0), (u, t, nc), (conv, CONV_B_WIDTH, 0), (bias, 1, 0)]
    return _col_call(body, name, nc, ins, [(t, D_FF)], t, 1)[0]


def _ffn_col_bwd(name, u, da, conv, bias):
    t = u.shape[0]
    nc = D_FF // LANES

    def body(g_ref, v_ref, da_ref, cw, b_ref, du_ref, dcw, db_ref, xp, dyp):
        _store_front(xp, g_ref[...], t)
        gc = _conv_fwd(xp, cw, CONV_B_WIDTH, t) + b_ref[...]
        s = _sigmoid(gc)

        @pl.when(pl.program_id(1) == 0)
        def _():
            dgc = da_ref[...] * v_ref[...] * (s * (1.0 + gc * (1.0 - s)))
            db_ref[...] = jnp.sum(dgc, axis=0, keepdims=True)
            _conv_bwd_w(dgc, xp, dcw, CONV_B_WIDTH, t)
            _store_back(dyp, dgc, t)
            du_ref[...] = _conv_bwd_in(dyp, cw, CONV_B_WIDTH, t)

        @pl.when(pl.program_id(1) == 1)
        def _():
            du_ref[...] = da_ref[...] * gc * s

    col = lambda rows, off: pl.BlockSpec((rows, LANES), lambda j, p: (0, j + off))
    return pl.pallas_call(
        body, name=name, grid=(nc, 2),
        in_specs=[col(t, 0), col(t, nc), col(t, 0), col(CONV_B_WIDTH, 0), col(1, 0)],
        out_specs=[pl.BlockSpec((t, LANES), lambda j, p: (0, j + nc * p)), col(CONV_B_WIDTH, 0), col(1, 0)],
        out_shape=[jax.ShapeDtypeStruct((t, 2 * D_FF), F32), jax.ShapeDtypeStruct((CONV_B_WIDTH, D_FF), F32),
                   jax.ShapeDtypeStruct((1, D_FF), F32)],
        scratch_shapes=[pltpu.VMEM((t + CONV_PAD, LANES), F32) for _ in range(2)],
        compiler_params=_cparams(None),
    )(u, u, da, conv, bias)


def _shift_fwd(name, p, col0, mu):
    t = p.shape[0]

    def body(x_ref, mu_ref, o_ref, xp):
        _store_front(xp, x_ref[...], t)
        prev = xp[pl.ds(CONV_PAD - 1, t), :]
        o_ref[...] = x_ref[...] + (prev - x_ref[...]) * mu_ref[...]

    return _col_call(body, name, RWKV_COLS // LANES, [(p, t, col0), (mu, 1, 0)], [(t, RWKV_COLS)], t, 1)[0]


def _shift_bwd(name, p, col0, mu, dprs):
    t = p.shape[0]

    def body(x_ref, mu_ref, d_ref, dx_ref, dmu_ref, xp, dyp):
        _store_front(xp, x_ref[...], t)
        prev = xp[pl.ds(CONV_PAD - 1, t), :]
        dmu_ref[...] = jnp.sum(d_ref[...] * (prev - x_ref[...]), axis=0, keepdims=True)
        dm = d_ref[...] * mu_ref[...]
        _store_back(dyp, dm, t)
        dx_ref[...] = d_ref[...] - dm + dyp[pl.ds(1, t), :]

    ins = [(p, t, col0), (mu, 1, 0), (dprs, t, 0)]
    return _col_call(body, name, RWKV_COLS // LANES, ins, [(t, RWKV_COLS), (1, RWKV_COLS)], t, 2)


def _dotm(a, b):
    return jnp.dot(a.astype(MXU_DTYPE), b.astype(MXU_DTYPE), preferred_element_type=F32)


def _hi_lo(x):
    hi = x.astype(BF16)
    return hi, (x - hi.astype(F32)).astype(BF16)


def _dot3(a, b, dims):
    ah, al = _hi_lo(a)
    bh, bl = _hi_lo(b)
    d = lambda p, q: lax.dot_general(p, q, dims, preferred_element_type=F32)
    return d(ah, bh) + (d(ah, bl) + d(al, bh))


@functools.partial(jax.custom_vjp, nondiff_argnums=(2,))
def _dot3_vjp(a, b, dims):
    return _dot3(a, b, dims)


def _dot3_fwd(a, b, dims):
    return _dot3(a, b, dims), (a, b)


def _dot3_bwd(dims, res, g):
    a, b = res
    if dims == _NN:
        return _dot3(g, b, _NT), _dot3(a, g, _TN)
    if dims == _NT:
        return _dot3(g, b, _NN), _dot3(g, a, _TN)
    return _dot3(b, g, _NT), _dot3(a, g, _NN)


_dot3_vjp.defvjp(_dot3_fwd, _dot3_bwd)


def _doth(a, b, dims=_NN):
    return _dot3_vjp(a, b, dims)


def _softplus(x):
    return jnp.where(x > 0, x, 0.0) + jnp.log(1.0 + jnp.exp(jnp.where(x > 0, -x, x)))


def _rwkv_pre(k, xl, gd, w0, w2p, a0, a2p, g2, k_k, k_a, seg):
    z = w0 + _dotm(jnp.tanh(xl), w2p)
    lw = -jnp.exp(-_softplus(-z) - 0.5)
    alpha = _sigmoid(a0 + _dotm(xl, a2p))
    g = _dotm(_sigmoid(gd), g2)
    kk = k * k_k
    kk = kk / jnp.maximum(jnp.sqrt(_doth(kk * kk, seg)), 1e-12)
    k2 = k * (1.0 + (alpha - 1.0) * k_a)
    return lw, k2, -kk, kk * alpha, g


def _rwkv_post(y, r, k2, v, g, lnx_g, lnx_b, r_k, seg):
    mean = _doth(y, seg) * (1.0 / HEAD_DIM)
    yc = y - mean
    var = _doth(yc * yc, seg) * (1.0 / HEAD_DIM)
    yo = yc * lax.rsqrt(var + RWKV_GN_EPS) * lnx_g + lnx_b
    bonus = _doth(r * k2 * r_k, seg) * v
    return (yo + bonus) * g


def _rwkv_pre_fwd(name, prs, prm, seg, tr):
    t = prs.shape[0]
    row = lambda i: (i, 0)
    fix = lambda i: (0, 0)
    ins = [(prs, (tr, D_R), lambda i: (i, 1)), (prs, (tr, LANES), lambda i: (i, 12)), (prs, (tr, LANES), lambda i: (i, 13)),
           (prm["w0"], (1, D_R), fix), (prm["w2p"], (LANES, D_R), fix), (prm["a0"], (1, D_R), fix),
           (prm["a2p"], (LANES, D_R), fix), (prm["g2"], (LANES, D_R), fix), (prm["k_k"], (1, D_R), fix),
           (prm["k_a"], (1, D_R), fix), (seg, (D_R, D_R), fix)]
    return _call(_rwkv_pre, name, (t // tr,), ins, [((t, D_R), (tr, D_R), row, False)] * 5)


def _rwkv_pre_bwd(name, prs, prm, seg, cts, tr):
    t = prs.shape[0]

    def fn(k, xl, gd, w0, w2p, a0, a2p, g2, k_k, k_a, segv, *ct):
        _, vjp = jax.vjp(lambda *a: _rwkv_pre(*a, segv), k, xl, gd, w0, w2p, a0, a2p, g2, k_k, k_a)
        return vjp(tuple(ct))

    row = lambda i: (i, 0)
    fix = lambda i: (0, 0)
    ins = [(prs, (tr, D_R), lambda i: (i, 1)), (prs, (tr, LANES), lambda i: (i, 12)), (prs, (tr, LANES), lambda i: (i, 13)),
           (prm["w0"], (1, D_R), fix), (prm["w2p"], (LANES, D_R), fix), (prm["a0"], (1, D_R), fix),
           (prm["a2p"], (LANES, D_R), fix), (prm["g2"], (LANES, D_R), fix), (prm["k_k"], (1, D_R), fix),
           (prm["k_a"], (1, D_R), fix), (seg, (D_R, D_R), fix)] + [(c, (tr, D_R), row) for c in cts]
    outs = [((t, D_R), (tr, D_R), row, False), ((t, LANES), (tr, LANES), row, False), ((t, LANES), (tr, LANES), row, False),
            ((1, D_R), (1, D_R), fix, True), ((LANES, D_R), (LANES, D_R), fix, True), ((1, D_R), (1, D_R), fix, True),
            ((LANES, D_R), (LANES, D_R), fix, True), ((LANES, D_R), (LANES, D_R), fix, True),
            ((1, D_R), (1, D_R), fix, True), ((1, D_R), (1, D_R), fix, True)]
    return _call(fn, name, (t // tr,), ins, outs, acc_axis=0)


def _rwkv_post_ins(y, prs, k2, g, prm, seg, tr):
    row = lambda i: (i, 0)
    fix = lambda i: (0, 0)
    return [(y, (tr, D_R), row), (prs, (tr, D_R), row), (k2, (tr, D_R), row), (prs, (tr, D_R), lambda i: (i, 2)),
            (g, (tr, D_R), row), (prm["lnx_g"], (1, D_R), fix), (prm["lnx_b"], (1, D_R), fix), (prm["r_k"], (1, D_R), fix),
            (seg, (D_R, D_R), fix)]


def _rwkv_post_fwd(name, y, prs, k2, g, prm, seg, tr):
    t = y.shape[0]
    return _call(_rwkv_post, name, (t // tr,), _rwkv_post_ins(y, prs, k2, g, prm, seg, tr),
                 [((t, D_R), (tr, D_R), lambda i: (i, 0), False)])


def _rwkv_post_bwd(name, y, prs, k2, g, prm, seg, dy, dy_col, tr):
    t = y.shape[0]

    def fn(yv, r, k2v, v, gv, lg, lb, rk, segv, ct):
        _, vjp = jax.vjp(lambda *a: _rwkv_post(*a, segv), yv, r, k2v, v, gv, lg, lb, rk)
        return vjp(ct)

    row = lambda i: (i, 0)
    fix = lambda i: (0, 0)
    ins = _rwkv_post_ins(y, prs, k2, g, prm, seg, tr) + [(dy, (tr, D_R), lambda i: (i, dy_col))]
    outs = [((t, D_R), (tr, D_R), row, False)] * 5 + [((1, D_R), (1, D_R), fix, True)] * 3
    return _call(fn, name, (t // tr,), ins, outs, acc_axis=0)


def _wkv_chunk(s0, r, lw, k, v, a, b):
    c = r[0].shape[0]
    lane = lax.broadcasted_iota(jnp.int32, (1, 2 * HEAD_DIM), 1)
    first = (lane < HEAD_DIM).astype(F32)
    per_head = lambda x: jnp.concatenate([x * first, x * (1.0 - first)], axis=0)

    def time_of(shape, dim):
        i = lax.broadcasted_iota(jnp.int32, shape, dim)
        return jnp.where(i >= c, i - c, i)

    incl = (lax.broadcasted_iota(jnp.int32, (c, c), 0) >= lax.broadcasted_iota(jnp.int32, (c, c), 1)).astype(F32)
    strict2 = time_of((2 * c, 2 * c), 0) > time_of((2 * c, 2 * c), 1)
    incl2 = lax.broadcasted_iota(jnp.int32, (c, 2 * c), 0) >= time_of((c, 2 * c), 1)
    each = lambda f, *xs: [f(*x) for x in zip(*xs)]
    cum = each(lambda x: _doth(incl, x), lw)
    tot = each(lambda x: jnp.sum(x, axis=0, keepdims=True), lw)
    e_inv = each(lambda x: jnp.exp(-x), cum)
    a_st = each(lambda x, cm, l: per_head(x * jnp.exp(cm - l)), a, cum, lw)
    r_t = each(lambda x, cm: x * jnp.exp(cm), r, cum)
    b_st = each(lambda x, e: per_head(x * e), b, e_inv)
    k_st = each(lambda x, e: per_head(x * e), k, e_inv)
    v_st = each(per_head, v)
    m = each(lambda x, w: jnp.where(strict2, _doth(x, w, _NT), 0.0), a_st, b_st)
    m_k = each(lambda x, w: jnp.where(strict2, _doth(x, w, _NT), 0.0), a_st, k_st)
    u = each(lambda x, s, mk, w: _doth(x, s, _NT) + _doth(mk, w), a_st, s0, m_k, v_st)
    steps = (c - 1).bit_length()
    for s in range(steps):
        u = each(lambda x, w: x + _doth(w, x), u, m)
        if s + 1 < steps:
            m = each(lambda w: _doth(w, w), m)
    n_b = each(lambda x, w: jnp.where(incl2, _doth(x, w, _NT), 0.0), r_t, b_st)
    n_k = each(lambda x, w: jnp.where(incl2, _doth(x, w, _NT), 0.0), r_t, k_st)
    y = each(lambda x, s, nb, uu, nk, w: _doth(x, s, _NT) + _doth(nb, uu) + _doth(nk, w), r_t, s0, n_b, u, n_k, v_st)
    dec = each(lambda tt, cm: jnp.exp(tt - cm), tot, cum)
    s1 = each(lambda s, tt, uu, x, d, w, kk: s * jnp.exp(tt) + _doth(uu, per_head(x * d), _TN) + _doth(w, per_head(kk * d), _TN),
              s0, tot, u, b, dec, v_st, k)
    return tuple(y), tuple(s1)


WKV_PAIRS_PER_STEP = 4
PAIR = 2 * HEAD_DIM


def _wkv_fwd(name, srcs):
    t = srcs[0][0].shape[0]
    c = _chunk_len(t)
    nc = t // c
    pp = WKV_PAIRS_PER_STEP
    n_pairs = D_R // PAIR

    def body(r, lw, k, v, a, b, y_ref, st_ref, state):
        @pl.when(pl.program_id(1) == 0)
        def _():
            state[...] = jnp.zeros(state.shape, F32)

        pairs = lambda ref: tuple(ref[:, pl.ds(i * PAIR, PAIR)] for i in range(pp))
        s0 = tuple(state[i] for i in range(pp))
        y, s1 = _wkv_chunk(s0, pairs(r), pairs(lw), pairs(k), pairs(v), pairs(a), pairs(b))
        for i in range(pp):
            st_ref[i] = s0[i]
            y_ref[:, pl.ds(i * PAIR, PAIR)] = y[i]
            state[i] = s1[i]

    seq = lambda off: pl.BlockSpec((c, pp * PAIR), lambda g, j: (j, off + g))
    return pl.pallas_call(
        body, name=name, grid=(n_pairs // pp, nc), in_specs=[seq(off) for _, off in srcs],
        out_specs=[seq(0), pl.BlockSpec((pp, None, PAIR, PAIR), lambda g, j: (g, j, 0, 0))],
        out_shape=[jax.ShapeDtypeStruct((t, D_R), F32), jax.ShapeDtypeStruct((n_pairs, nc, PAIR, PAIR), F32)],
        scratch_shapes=[pltpu.VMEM((pp, PAIR, PAIR), F32)],
        compiler_params=_cparams(None),
    )(*[a for a, _ in srcs])


def _wkv_bwd(name, srcs, st, dy):
    t = srcs[0][0].shape[0]
    c = _chunk_len(t)
    nc = t // c
    pp = WKV_PAIRS_PER_STEP
    n_pairs = D_R // PAIR

    def body(r, lw, k, v, a, b, st_ref, dy_ref, dr, dlw, dk, dv, da, db, dstate):
        @pl.when(pl.program_id(1) == 0)
        def _():
            dstate[...] = jnp.zeros(dstate.shape, F32)

        half = lax.broadcasted_iota(jnp.int32, (PAIR, PAIR), 0) < HEAD_DIM
        same_head = half == (lax.broadcasted_iota(jnp.int32, (PAIR, PAIR), 1) < HEAD_DIM)
        pairs = lambda ref: tuple(ref[:, pl.ds(i * PAIR, PAIR)] for i in range(pp))
        s0 = tuple(st_ref[i] for i in range(pp))
        _, vjp = jax.vjp(_wkv_chunk, s0, pairs(r), pairs(lw), pairs(k), pairs(v), pairs(a), pairs(b))
        ds0, *dxs = vjp((pairs(dy_ref), tuple(dstate[i] for i in range(pp))))
        for i in range(pp):
            for ref, val in zip((dr, dlw, dk, dv, da, db), dxs):
                ref[:, pl.ds(i * PAIR, PAIR)] = val[i]
            dstate[i] = jnp.where(same_head, ds0[i], 0.0)

    seq = lambda off: pl.BlockSpec((c, pp * PAIR), lambda g, j: (nc - 1 - j, off + g))
    return pl.pallas_call(
        body, name=name, grid=(n_pairs // pp, nc),
        in_specs=[seq(off) for _, off in srcs]
        + [pl.BlockSpec((pp, None, PAIR, PAIR), lambda g, j: (g, nc - 1 - j, 0, 0)), seq(dy[1])],
        out_specs=[seq(0)] * 6,
        out_shape=[jax.ShapeDtypeStruct((t, D_R), F32)] * 6,
        scratch_shapes=[pltpu.VMEM((pp, PAIR, PAIR), F32)],
        compiler_params=_cparams(None),
    )(*[a for a, _ in srcs], st, dy[0])


def _rope(x, cos, sin, rot):
    return x * cos + _doth(x, rot) * sin


def _attn_block(nb, q4, kp, kc, km, vp, vc, vm, sk, cq, sq, cp, sp, cm, sm, rot):
    scale = HEAD_DIM ** -0.5
    kpr = _rope(kp, cp, sp, rot).astype(MXU_DTYPE)
    kcr = _rope(kc, cq, sq, rot).astype(MXU_DTYPE)
    kmr = _rope(km, cm, sm, rot).astype(MXU_DTYPE)
    i = lax.broadcasted_iota(jnp.int32, (BLOCK, BLOCK), 0)
    j = lax.broadcasted_iota(jnp.int32, (BLOCK, BLOCK), 1)
    nbv = jnp.zeros((BLOCK, BLOCK), jnp.int32) + nb
    ok_p = (j > i) & (nbv >= 2)
    ok_c = (j <= i) & (nbv >= 1)
    ok_m = (j >= BLOCK - N_META) & ((nbv >= 1) | (j <= i))
    outs = []
    for h in range(GQA_GROUP):
        qr = _rope(q4[h], cq, sq, rot).astype(MXU_DTYPE)
        ntdot = lambda kk: lax.dot_general(qr, kk, _NT, preferred_element_type=F32) * scale
        s_p = jnp.where(ok_p, ntdot(kpr), NEG_INF)
        s_c = jnp.where(ok_c, ntdot(kcr), NEG_INF)
        s_m = jnp.where(ok_m, ntdot(kmr), NEG_INF)
        rmax = lambda s: jnp.max(s, axis=-1, keepdims=True)
        m = lax.stop_gradient(jnp.maximum(jnp.maximum(rmax(s_p), rmax(s_c)), jnp.maximum(rmax(s_m), sk[h])))
        e_p, e_c, e_m = jnp.exp(s_p - m), jnp.exp(s_c - m), jnp.exp(s_m - m)
        rsum = lambda e: jnp.sum(e, axis=-1, keepdims=True)
        inv = 1.0 / (rsum(e_p) + rsum(e_c) + rsum(e_m) + jnp.exp(sk[h] - m))
        outs.append(_dotm(e_p * inv, vp) + _dotm(e_c * inv, vc) + _dotm(e_m * inv, vm))
    return tuple(outs)


def _attn_specs():
    cur = lambda g, n: (g, n, 0)
    prev = lambda g, n: (g, jnp.maximum(n - 1, 0), 0)
    meta = lambda g, n: (g, 0, 0)
    kv = lambda m: pl.BlockSpec((None, BLOCK, HEAD_DIM), m)
    tab = lambda m: pl.BlockSpec((BLOCK, HEAD_DIM), m)
    tcur, tprev, tmeta = (lambda g, n: (n, 0)), (lambda g, n: (jnp.maximum(n - 1, 0), 0)), (lambda g, n: (0, 0))
    qspec = pl.BlockSpec((GQA_GROUP, BLOCK, HEAD_DIM), cur)
    sspec = pl.BlockSpec((GQA_GROUP, 8, LANES), meta)
    specs = [qspec, kv(prev), kv(cur), kv(meta), kv(prev), kv(cur), kv(meta), sspec,
             tab(tcur), tab(tcur), tab(tprev), tab(tprev), tab(tmeta), tab(tmeta),
             pl.BlockSpec((HEAD_DIM, HEAD_DIM), lambda g, n: (0, 0))]
    return specs, qspec, sspec, kv


def _attn_args(q, k, v, sinks_b, cos, sin, rot):
    return (q, k, k, k, v, v, v, sinks_b, cos, sin, cos, sin, cos, sin, rot)


def _attn_fwd(name, q, k, v, sinks_b, cos, sin, rot):
    tp = q.shape[1]
    specs, qspec, _, _ = _attn_specs()

    def body(q_ref, kp, kc, km, vp, vc, vm, s_ref, cq, sq, cp, sp, cm, sm, rot_ref, o_ref):
        q4 = tuple(q_ref[h] for h in range(GQA_GROUP))
        sk = tuple(s_ref[h][0:1, 0:1] for h in range(GQA_GROUP))
        outs = _attn_block(pl.program_id(1), q4, kp[...], kc[...], km[...], vp[...], vc[...], vm[...], sk,
                           cq[...], sq[...], cp[...], sp[...], cm[...], sm[...], rot_ref[...])
        for h in range(GQA_GROUP):
            o_ref[h] = outs[h]

    return pl.pallas_call(
        body, name=name, grid=(N_KV_HEADS, tp // BLOCK), in_specs=specs, out_specs=qspec,
        out_shape=jax.ShapeDtypeStruct(q.shape, F32), compiler_params=_cparams(None),
    )(*_attn_args(q, k, v, sinks_b, cos, sin, rot))


def _attn_bwd(name, q, k, v, sinks_b, cos, sin, rot, do):
    tp = q.shape[1]
    nb = tp // BLOCK
    specs, qspec, sspec, kv = _attn_specs()

    def body(q_ref, kp, kc, km, vp, vc, vm, s_ref, cq, sq, cp, sp, cm, sm, rot_ref, do_ref,
             dq_ref, dkp, dkc, dvp, dvc, dkm, dvm, ds_ref):
        n = pl.program_id(1)
        q4 = tuple(q_ref[h] for h in range(GQA_GROUP))
        sk = tuple(s_ref[h][0:1, 0:1] for h in range(GQA_GROUP))
        tabs = (cq[...], sq[...], cp[...], sp[...], cm[...], sm[...], rot_ref[...])
        _, vjp = jax.vjp(lambda *a: _attn_block(n, *a, *tabs), q4, kp[...], kc[...], km[...], vp[...], vc[...], vm[...], sk)
        dq4, gkp, gkc, gkm, gvp, gvc, gvm, dsk = vjp(tuple(do_ref[h] for h in range(GQA_GROUP)))
        dkp[...] = gkp
        dkc[...] = gkc
        dvp[...] = gvp
        dvc[...] = gvc
        for h in range(GQA_GROUP):
            dq_ref[h] = dq4[h]

        @pl.when(n == 0)
        def _():
            dkm[...] = gkm
            dvm[...] = gvm
            for h in range(GQA_GROUP):
                ds_ref[h] = jnp.broadcast_to(dsk[h], (8, LANES))

        @pl.when(n != 0)
        def _():
            dkm[...] += gkm
            dvm[...] += gvm
            for h in range(GQA_GROUP):
                ds_ref[h] += jnp.broadcast_to(dsk[h], (8, LANES))

    part = pl.BlockSpec((None, None, BLOCK, HEAD_DIM), lambda g, n: (g, n, 0, 0))
    part_shape = jax.ShapeDtypeStruct((N_KV_HEADS, nb, BLOCK, HEAD_DIM), F32)
    meta_shape = jax.ShapeDtypeStruct((N_KV_HEADS, BLOCK, HEAD_DIM), F32)
    return pl.pallas_call(
        body, name=name, grid=(N_KV_HEADS, nb), in_specs=specs + [qspec],
        out_specs=[qspec, part, part, part, part, kv(lambda g, n: (g, 0, 0)), kv(lambda g, n: (g, 0, 0)), sspec],
        out_shape=[jax.ShapeDtypeStruct(q.shape, F32), part_shape, part_shape, part_shape, part_shape,
                   meta_shape, meta_shape, jax.ShapeDtypeStruct(sinks_b.shape, F32)],
        compiler_params=_cparams(None),
    )(*_attn_args(q, k, v, sinks_b, cos, sin, rot), do)


def _kv_combine(name, prev_part, own_part, meta):
    g, nb = own_part.shape[:2]

    def fn(own, nxt, mt):
        m = pl.program_id(1)
        one = jnp.ones((BLOCK, HEAD_DIM), F32)
        use_next = jnp.where(one * m < nb - 1, 1.0, 0.0)
        use_meta = jnp.where(one * m < 1, 1.0, 0.0)
        return own + nxt * use_next + mt * use_meta

    blk = (None, None, BLOCK, HEAD_DIM)
    return _call(fn, name, (g, nb),
                 [(own_part, blk, lambda a, m: (a, m, 0, 0)),
                  (prev_part, blk, lambda a, m: (a, jnp.minimum(m + 1, nb - 1), 0, 0)),
                  (meta, (None, BLOCK, HEAD_DIM), lambda a, m: (a, 0, 0))],
                 [((g, nb * BLOCK, HEAD_DIM), (None, BLOCK, HEAD_DIM), lambda a, m: (a, m, 0), False)])


PACK_W = 1024
ELEMENTWISE_BLOCK_BYTES = 1 << 21


def _rows_tile(rows, cols):
    cap = max(8, ELEMENTWISE_BLOCK_BYTES // (4 * cols))
    for d in range(min(rows, cap), 0, -1):
        if rows % d == 0 and d % 8 == 0:
            return d
    return rows


def _adamw(name, w, g, m, v):
    rows, cols = w.shape
    tr = _rows_tile(rows, cols)

    def fn(wv, gv, mv, vv):
        m1 = ADAM_B1 * mv + (1.0 - ADAM_B1) * gv
        v1 = ADAM_B2 * vv + (1.0 - ADAM_B2) * (gv * gv)
        m_hat = m1 / (1.0 - ADAM_B1 ** ADAM_STEP)
        v_hat = v1 / (1.0 - ADAM_B2 ** ADAM_STEP)
        return -ADAM_LR * (m_hat / (jnp.sqrt(v_hat) + ADAM_EPS) + ADAM_WD * wv), m1, v1

    blk = (tr, cols)
    row = lambda i: (i, 0)
    return _call(fn, name, (rows // tr,), [(a, blk, row) for a in (w, g, m, v)], [((rows, cols), blk, row, False)] * 3)


def _pair_add(name, g, recv, c_idx, out_dtype):
    s, a, b = g.shape
    half = a // 2

    def body(c_ref, a_ref, b_ref, o_ref):
        o_ref[...] = (a_ref[...] + b_ref[...]).astype(out_dtype)

    blk = (None, half, b)
    return pl.pallas_call(
        body, name=name,
        grid_spec=pltpu.PrefetchScalarGridSpec(
            num_scalar_prefetch=1, grid=(s,),
            in_specs=[pl.BlockSpec(blk, lambda j, c: (j, c[0], 0)), pl.BlockSpec(blk, lambda j, c: (j, 0, 0))],
            out_specs=pl.BlockSpec(blk, lambda j, c: (j, 0, 0))),
        out_shape=jax.ShapeDtypeStruct((s, half, b), out_dtype), compiler_params=_cparams(None),
    )(c_idx, g, recv)


def _sum_chips(name, parts, c_idx, layer, n_layers, into=None):
    _, a, b = parts.shape
    tr = _rows_tile(a, b)

    def body(c_ref, p0, p1, p2, p3, *rest):
        o_ref = rest[-1]
        up = lambda p: p[...].astype(F32)
        o_ref[...] = ((up(p0) + up(p1)) + up(p2)) + up(p3)

    in_specs = [pl.BlockSpec((None, tr, b), lambda i, c, k=k: (k, i, 0)) for k in range(N_CHIPS)]
    args = [c_idx] + [parts] * N_CHIPS
    aliases = {}
    if into is not None:
        in_specs.append(_ANY)
        args.append(into)
        aliases = {1 + N_CHIPS: 0}
    return pl.pallas_call(
        body, name=name,
        grid_spec=pltpu.PrefetchScalarGridSpec(
            num_scalar_prefetch=1, grid=(a // tr,), in_specs=in_specs,
            out_specs=pl.BlockSpec((None, None, tr, b), lambda i, c: (layer, c[0], i, 0))),
        out_shape=jax.ShapeDtypeStruct((n_layers, 2, a, b), F32), input_output_aliases=aliases,
        compiler_params=_cparams(None),
    )(*args)


def _place_own_block(name, w, me_idx, half_major, dtype):
    _, a, b = w.shape
    tr = _rows_tile(a, b)
    if half_major:
        shape, index = (2, N_CHIPS, a, b), (lambda h, i, me: (h, me[0], i, 0))
    else:
        shape, index = (N_CHIPS, 2, a, b), (lambda h, i, me: (me[0], h, i, 0))

    def body(me_ref, w_ref, o_ref):
        o_ref[...] = w_ref[...].astype(dtype)

    return pl.pallas_call(
        body, name=name,
        grid_spec=pltpu.PrefetchScalarGridSpec(
            num_scalar_prefetch=1, grid=(2, a // tr),
            in_specs=[pl.BlockSpec((None, tr, b), lambda h, i, me: (h, i, 0))],
            out_specs=pl.BlockSpec((None, None, tr, b), index)),
        out_shape=jax.ShapeDtypeStruct(shape, dtype), compiler_params=_cparams(None),
    )(me_idx, w)


def _mesh_pos():
    return lax.axis_index("x"), lax.axis_index("y"), lax.axis_index("c")


def _other_chips(x, y):
    return [(1 - x, y), (x, 1 - y), (1 - x, 1 - y)]


_ANY = pl.BlockSpec(memory_space=pl.ANY)


def _gather_weights(name, bufs, half_major):
    n = len(bufs)

    def body(*refs):
        out_refs = refs[n:2 * n]
        send_sems, recv_sems = refs[2 * n:]
        x, y, c = _mesh_pos()
        me = 2 * x + y
        sibling = (x, y, 1 - c)
        chips = _other_chips(x, y)

        def place(i, chip_idx, half):
            return out_refs[i].at[half, chip_idx] if half_major[i] else out_refs[i].at[chip_idx, half]

        def copy(i, k, chip_idx, half, to):
            return pltpu.make_async_remote_copy(src_ref=place(i, chip_idx, half), dst_ref=place(i, chip_idx, half),
                                                send_sem=send_sems.at[6 * i + k], recv_sem=recv_sems.at[6 * i + k],
                                                device_id=to, device_id_type=MESH)

        first = [copy(i, j, me, c, (*chip, c)) for i in range(n) for j, chip in enumerate(chips)]
        for cp in first:
            cp.start()
        passed = []
        for i in range(n):
            for j, (cx, cy) in enumerate(chips):
                idx = 2 * cx + cy
                copy(i, j, idx, c, sibling).wait_recv()
                fwd = copy(i, 3 + j, idx, c, sibling)
                fwd.start()
                passed.append(fwd)
        for i in range(n):
            for j, (cx, cy) in enumerate(chips):
                copy(i, 3 + j, 2 * cx + cy, 1 - c, sibling).wait_recv()
        for cp in first + passed:
            cp.wait_send()

    return pl.pallas_call(
        body, name=name, in_specs=[_ANY] * n, out_specs=[_ANY] * n,
        out_shape=[jax.ShapeDtypeStruct(b.shape, b.dtype) for b in bufs],
        input_output_aliases={i: i for i in range(n)},
        scratch_shapes=[pltpu.SemaphoreType.DMA((6 * n,)), pltpu.SemaphoreType.DMA((6 * n,))],
        compiler_params=pltpu.CompilerParams(has_side_effects=True),
    )(*bufs)


def _halves_to_sibling(name, units):
    n = len(units)

    def body(*refs):
        g_refs, out_refs = refs[:n], refs[n:2 * n]
        send_sems, recv_sems = refs[2 * n:]
        x, y, c = _mesh_pos()
        cps = []
        for i in range(n):
            half = units[i].shape[1] // 2
            src = g_refs[i].at[pl.ds(0, N_CHIPS), pl.ds((1 - c) * half, half)]
            cp = pltpu.make_async_remote_copy(src_ref=src, dst_ref=out_refs[i], send_sem=send_sems.at[i],
                                              recv_sem=recv_sems.at[i], device_id=(x, y, 1 - c), device_id_type=MESH)
            cp.start()
            cps.append(cp)
        for cp in cps:
            cp.wait()

    return pl.pallas_call(
        body, name=name, in_specs=[_ANY] * n, out_specs=[_ANY] * n,
        out_shape=[jax.ShapeDtypeStruct((u.shape[0], u.shape[1] // 2, u.shape[2]), u.dtype) for u in units],
        scratch_shapes=[pltpu.SemaphoreType.DMA((n,)), pltpu.SemaphoreType.DMA((n,))],
        compiler_params=pltpu.CompilerParams(has_side_effects=True),
    )(*units)


def _scatter_to_chips(name, sums):
    n = len(sums)

    def body(*refs):
        h_refs, out_refs = refs[:n], refs[n:2 * n]
        send_sems, recv_sems, local_sems = refs[2 * n:]
        x, y, c = _mesh_pos()
        me = 2 * x + y
        chips = _other_chips(x, y)
        local = [pltpu.make_async_copy(h_refs[i].at[me], out_refs[i].at[me], local_sems.at[i]) for i in range(n)]
        for cp in local:
            cp.start()

        def copy(i, j, src_idx, dst_idx):
            cx, cy = chips[j]
            return pltpu.make_async_remote_copy(src_ref=h_refs[i].at[src_idx], dst_ref=out_refs[i].at[dst_idx],
                                                send_sem=send_sems.at[3 * i + j], recv_sem=recv_sems.at[3 * i + j],
                                                device_id=(cx, cy, c), device_id_type=MESH)

        cps = [copy(i, j, 2 * chips[j][0] + chips[j][1], me) for i in range(n) for j in range(3)]
        for cp in cps:
            cp.start()
        for i in range(n):
            for j in range(3):
                copy(i, j, me, 2 * chips[j][0] + chips[j][1]).wait_recv()
        for cp in cps:
            cp.wait_send()
        for cp in local:
            cp.wait()

    return pl.pallas_call(
        body, name=name, in_specs=[_ANY] * n, out_specs=[_ANY] * n,
        out_shape=[jax.ShapeDtypeStruct(s.shape, s.dtype) for s in sums],
        scratch_shapes=[pltpu.SemaphoreType.DMA((3 * n,)), pltpu.SemaphoreType.DMA((3 * n,)), pltpu.SemaphoreType.DMA((n,))],
        compiler_params=pltpu.CompilerParams(has_side_effects=True),
    )(*sums)


def _join_halves(name, results):
    n = len(results)
    pieces = [(i, l) for i in range(n) for l in range(results[i].shape[0])]

    def body(*refs):
        out_refs = refs[n:2 * n]
        send_sems, recv_sems = refs[2 * n:]
        x, y, c = _mesh_pos()

        def copy(k, half):
            i, l = pieces[k]
            return pltpu.make_async_remote_copy(src_ref=out_refs[i].at[l, half], dst_ref=out_refs[i].at[l, half],
                                                send_sem=send_sems.at[k], recv_sem=recv_sems.at[k],
                                                device_id=(x, y, 1 - c), device_id_type=MESH)

        cps = [copy(k, c) for k in range(len(pieces))]
        for cp in cps:
            cp.start()
        for k in range(len(pieces)):
            copy(k, 1 - c).wait_recv()
        for cp in cps:
            cp.wait_send()

    return pl.pallas_call(
        body, name=name, in_specs=[_ANY] * n, out_specs=[_ANY] * n,
        out_shape=[jax.ShapeDtypeStruct(r.shape, r.dtype) for r in results],
        input_output_aliases={i: i for i in range(n)},
        scratch_shapes=[pltpu.SemaphoreType.DMA((len(pieces),)), pltpu.SemaphoreType.DMA((len(pieces),))],
        compiler_params=pltpu.CompilerParams(has_side_effects=True),
    )(*results)


def _pack(arrays, dtype, rows_multiple):
    flat = jnp.concatenate([a.reshape(-1).astype(dtype) for a in arrays])
    unit = rows_multiple * PACK_W
    total = -(-flat.shape[0] // unit) * unit
    return jnp.pad(flat, (0, total - flat.shape[0])).reshape(total // PACK_W, PACK_W)


def _unpack(flat, shapes):
    out, off = [], 0
    for s in shapes:
        n = 1
        for d in s:
            n *= d
        out.append(flat[..., off:off + n].reshape(flat.shape[:-1] + tuple(s)))
        off += n
    return out


def _ffn_fwd(tag, l, h, g, w_up, conv, bias, w_down, tm):
    hn = _rms_fwd(f"{tag}_norm", h, g, tm)
    u = _mm_cs(f"{tag}_up", hn, w_up, l, tm)
    act = _ffn_col_fwd(f"{tag}_glu", u, conv, bias)
    h_out = _mm_full(f"{tag}_down", act, w_down, l, tm, D_FF // 2, add=h)
    return h_out, (hn, u, act)


def _ffn_bwd(tag, l, h, g, w_up, conv, bias, w_down, saved, dh, tm):
    hn, u, act = saved
    da = _mm_nt_full(f"{tag}_down_dx", dh, w_down, l, tm, D_FF // 2)
    dw_down = _mm_tn_full(f"{tag}_down_dw", act, dh, tm, D_FF // 2)
    du, dconv, dbias = _ffn_col_bwd(f"{tag}_glu_bwd", u, da, conv, bias)
    dw_up = _mm_tn_cs(f"{tag}_up_dw", hn, du, N_CHIPS, tm)
    dhn = _mm_nt_cs(f"{tag}_up_dx", du, w_up, l, tm)
    dh, dg = _rms_bwd(f"{tag}_norm_bwd", h, g, dhn, dh, tm)
    return dh, dict(norm=dg, w_up=dw_up, conv=dconv, bias=dbias, w_down=dw_down)


def _to_heads(z, nh, pad):
    t = z.shape[0]
    return jnp.pad(z.reshape(t, nh, HEAD_DIM).transpose(1, 0, 2), ((0, 0), (pad, 0), (0, 0)))


def _from_heads(z, pad):
    nh, tp, _ = z.shape
    return z[:, pad:].transpose(1, 0, 2).reshape(tp - pad, nh * HEAD_DIM)


def _rope_tables(tp, pad):
    half = HEAD_DIM // 2
    inv = ROPE_THETA ** (-jnp.arange(half, dtype=F32) / half)
    ang = (jnp.arange(tp, dtype=F32) - pad)[:, None] * inv[None, :]
    cos, sin = jnp.cos(ang), jnp.sin(ang)
    rot = jnp.zeros((HEAD_DIM, HEAD_DIM), F32)
    idx = jnp.arange(half)
    rot = rot.at[idx + half, idx].set(-1.0).at[idx, idx + half].set(1.0)
    return jnp.concatenate([cos, cos], axis=1), jnp.concatenate([sin, sin], axis=1), rot


def _local_step(x, tgt, w):
    seq = x.shape[0]
    t = seq + N_META
    tm = _row_tile(t, 704)
    tr = _row_tile(t, 352)
    pad = BLOCK - N_META
    grads = {}

    h0 = jnp.concatenate([w["meta_tokens"], x], axis=0)
    tgt_p = jnp.pad(tgt, ((N_META, 0), (0, 0)))

    hn0 = _rms_fwd("l0_norm", h0, w["norm_mix"][0:1], tm)
    p0 = _mm_cs("l0_in", hn0, w["ev_w_in"], 0, tm)
    uc, yb = _even_col_fwd("l0_convs", p0, w["ev_conv_a"], w["ev_conv_b"])
    ya = _even_ln_fwd("l0_ln", uc, w["ev_ln_a_g"], w["ev_ln_a_b"], tm)
    y0 = jnp.concatenate([ya, yb], axis=1)
    h1 = _mm_full("l0_out", y0, w["ev_w_out"], 0, tm, D_MODEL, add=h0)
    f0 = (0, h1, w["norm_ffn"][0:1], w["ff_w_up"], w["ff_conv"][0], w["ff_conv_b"][0:1], w["ff_w_down"])
    h2, ffn0 = _ffn_fwd("f0", *f0, tm)

    hn2 = _rms_fwd("l1_norm", h2, w["norm_mix"][1:2], tm)
    p1 = _mm_cs("l1_in", hn2, w["od_w_in"], 0, tm)
    cos, sin, rot = _rope_tables(t + pad, pad)
    qh = _to_heads(p1[:, :D_ATT], N_Q_HEADS, pad)
    kh = _to_heads(p1[:, D_ATT:D_ATT + D_KV], N_KV_HEADS, pad)
    vh = _to_heads(p1[:, D_ATT + D_KV:D_ATT + 2 * D_KV], N_KV_HEADS, pad)
    sinks_b = jnp.broadcast_to(w["od_sinks"].reshape(N_Q_HEADS, 1, 1), (N_Q_HEADS, 8, LANES))
    y_att = _from_heads(_attn_fwd("l1_attn", qh, kh, vh, sinks_b, cos, sin, rot), pad)

    col0 = (D_ATT + 2 * D_KV) // LANES
    ch = jnp.arange(D_R) // HEAD_DIM
    seg = (ch[:, None] == ch[None, :]).astype(F32)
    prm = dict(w0=w["od_w0"], a0=w["od_a0"], g2=w["od_g2"], k_k=w["od_k_k"], k_a=w["od_k_a"],
               lnx_g=w["od_lnx_g"], lnx_b=w["od_lnx_b"], r_k=w["od_r_k"].reshape(1, D_R),
               w2p=jnp.concatenate([w["od_w2"], jnp.zeros((LORA_A, D_R), F32)], axis=0),
               a2p=jnp.concatenate([jnp.zeros((LORA_W, D_R), F32), w["od_a2"]], axis=0))
    prs = _shift_fwd("l1_shift", p1, col0, w["od_mu"])
    lw, k2, a_, b_, gate_r = _rwkv_pre_fwd("l1_rwkv_pre", prs, prm, seg, tr)
    v_off = 2 * D_R // (WKV_PAIRS_PER_STEP * PAIR)
    scan_in = [(prs, 0), (lw, 0), (k2, 0), (prs, v_off), (a_, 0), (b_, 0)]
    y_scan, states = _wkv_fwd("l1_wkv", scan_in)
    y_rwkv = _rwkv_post_fwd("l1_rwkv_post", y_scan, prs, k2, gate_r, prm, seg, tr)
    y1 = jnp.concatenate([y_att, y_rwkv], axis=1)
    h3 = _mm_full("l1_out", y1, w["od_w_out"], 0, tm, D_MODEL, add=h2)
    f1 = (1, h3, w["norm_ffn"][1:2], w["ff_w_up"], w["ff_conv"][1], w["ff_conv_b"][1:2], w["ff_w_down"])
    h4, ffn1 = _ffn_fwd("f1", *f1, tm)

    loss_blk, dh, d_norm_final = _final_loss("final", h4, w["norm_final"], tgt_p, tm)
    grads["norm_final"] = d_norm_final

    dh, gf1 = _ffn_bwd("f1", *f1, ffn1, dh, tm)
    dy1 = _mm_nt_full("l1_out_dx", dh, w["od_w_out"], 0, tm, D_MODEL)
    grads["od_w_out"] = _mm_tn_full("l1_out_dw", y1, dh, tm, D_MODEL // 2)
    dy_scan, dr_p, dk2_p, dv_p, dgate_r, grads["od_lnx_g"], grads["od_lnx_b"], d_rk = _rwkv_post_bwd(
        "l1_rwkv_post_bwd", y_scan, prs, k2, gate_r, prm, seg, dy1, 1, tr)
    grads["od_r_k"] = d_rk.reshape(N_R_HEADS, HEAD_DIM)
    dr_s, dlw, dk2_s, dv_s, da_, db_ = _wkv_bwd("l1_wkv_bwd", scan_in, states, (dy_scan, 0))
    dk, dxl, dgd, grads["od_w0"], dw2p, grads["od_a0"], da2p, grads["od_g2"], grads["od_k_k"], grads["od_k_a"] = (
        _rwkv_pre_bwd("l1_rwkv_pre_bwd", prs, prm, seg, (dlw, dk2_s + dk2_p, da_, db_, dgate_r), tr))
    grads["od_w2"] = dw2p[:LORA_W]
    grads["od_a2"] = da2p[LORA_W:]
    dprs = jnp.concatenate([dr_s + dr_p, dk, dv_s + dv_p, dxl, dgd], axis=1)
    dpr, grads["od_mu"] = _shift_bwd("l1_shift_bwd", p1, col0, w["od_mu"], dprs)
    doh = _to_heads(dy1[:, :D_ATT], N_Q_HEADS, pad)
    dqh, dkp, dkc, dvp, dvc, dkm, dvm, dsinks = _attn_bwd("l1_attn_bwd", qh, kh, vh, sinks_b, cos, sin, rot, doh)
    grads["od_sinks"] = dsinks[:, 0, 0].reshape(1, N_Q_HEADS)
    dkh = _kv_combine("l1_attn_dk", dkp, dkc, dkm)
    dvh = _kv_combine("l1_attn_dv", dvp, dvc, dvm)
    dp1 = jnp.concatenate([_from_heads(dqh, pad), _from_heads(dkh, pad), _from_heads(dvh, pad), dpr], axis=1)
    grads["od_w_in"] = _mm_tn_cs("l1_in_dw", hn2, dp1, N_CHIPS, tm)
    dhn2 = _mm_nt_cs("l1_in_dx", dp1, w["od_w_in"], 0, tm)
    dh, d_mix1 = _rms_bwd("l1_norm_bwd", h2, w["norm_mix"][1:2], dhn2, dh, tm)

    dh, gf0 = _ffn_bwd("f0", *f0, ffn0, dh, tm)
    dy0 = _mm_nt_full("l0_out_dx", dh, w["ev_w_out"], 0, tm, D_MODEL)
    grads["ev_w_out"] = _mm_tn_full("l0_out_dw", y0, dh, tm, D_MODEL // 2)
    duc, grads["ev_ln_a_g"], grads["ev_ln_a_b"] = _even_ln_bwd("l0_ln_bwd", uc, w["ev_ln_a_g"], w["ev_ln_a_b"], dy0, 0, tm)
    *dparts, grads["ev_conv_a"], grads["ev_conv_b"] = _even_col_bwd("l0_convs_bwd", p0, duc, dy0, w["ev_conv_a"], w["ev_conv_b"])
    dp0 = jnp.concatenate(dparts, axis=1)
    grads["ev_w_in"] = _mm_tn_cs("l0_in_dw", hn0, dp0, N_CHIPS, tm)
    dhn0 = _mm_nt_cs("l0_in_dx", dp0, w["ev_w_in"], 0, tm)
    dh, d_mix0 = _rms_bwd("l0_norm_bwd", h0, w["norm_mix"][0:1], dhn0, dh, tm)

    grads["norm_mix"] = jnp.concatenate([d_mix0, d_mix1], axis=0)
    grads["norm_ffn"] = jnp.concatenate([gf0["norm"], gf1["norm"]], axis=0)
    grads["ff_w_up"] = [gf0["w_up"], gf1["w_up"]]
    grads["ff_conv"] = jnp.stack([gf0["conv"], gf1["conv"]])
    grads["ff_conv_b"] = jnp.concatenate([gf0["bias"], gf1["bias"]], axis=0)
    grads["ff_w_down"] = [gf0["w_down"], gf1["w_down"]]
    grads["meta_tokens"] = dh[:N_META]
    return loss_blk[0, 0], dh[N_META:], grads


SHARD_AXIS = {
    "meta_tokens": 1, "norm_mix": None, "norm_ffn": None, "norm_final": None,
    "ev_w_in": 2, "ev_conv_a": 2, "ev_ln_a_g": None, "ev_ln_a_b": None, "ev_conv_b": 2, "ev_w_out": 1,
    "od_w_in": 2, "od_sinks": None, "od_mu": 1, "od_w0": 1, "od_w2": 2, "od_a0": 1, "od_a2": 2, "od_g2": 2,
    "od_k_k": 1, "od_k_a": 1, "od_r_k": None, "od_lnx_g": 1, "od_lnx_b": 1, "od_w_out": 1,
    "ff_w_up": 2, "ff_conv": 2, "ff_conv_b": None, "ff_w_down": 1,
}
WEIGHTS = list(SHARD_AXIS)
BIG = ("ev_w_in", "ev_w_out", "od_w_in", "od_w_out", "ff_w_up", "ff_w_down")
SHARDED = [n for n in WEIGHTS if SHARD_AXIS[n] is not None]
SMALL = [n for n in SHARDED if n not in BIG]
REPLICATED = [n for n in WEIGHTS if SHARD_AXIS[n] is None]


def _join(g, axis):
    return jnp.concatenate([g[k] for k in range(N_CHIPS)], axis=axis)


def _split(full, axis):
    return jnp.stack(jnp.split(full, N_CHIPS, axis=axis))


def _full_weights(gathered, repl):
    w = {}
    sq = lambda a: a.reshape(a.shape[1:]) if a.shape[0] == 1 else a
    for n in REPLICATED:
        w[n] = repl[n]
    w["norm_final"] = repl["norm_final"].reshape(1, D_MODEL)
    for n in ("ev_ln_a_g", "ev_ln_a_b"):
        w[n] = repl[n].reshape(1, D_A)
    w["od_r_k"] = repl["od_r_k"][0]
    w["meta_tokens"] = _join(gathered["meta_tokens"], 1)
    for n in ("ev_conv_a", "ev_conv_b", "od_w2", "od_a2", "od_g2"):
        w[n] = sq(_join(gathered[n], 2))
    for n in ("od_mu", "od_w0", "od_a0", "od_k_k", "od_k_a", "od_lnx_g", "od_lnx_b"):
        w[n] = _join(gathered[n], 1)
    w["ff_conv"] = _join(gathered["ff_conv"], 2)
    return w


def _shard_grads(grads):
    out = {}
    for n in REPLICATED:
        out[n] = grads[n]
    out["norm_final"] = grads["norm_final"].reshape(D_MODEL)
    out["od_r_k"] = grads["od_r_k"][None]
    out["meta_tokens"] = _split(grads["meta_tokens"], 1)
    for n in ("ev_conv_a", "ev_conv_b", "od_w2", "od_a2", "od_g2"):
        out[n] = _split(grads[n][None], 2)
    for n in ("od_mu", "od_w0", "od_a0", "od_k_k", "od_k_a", "od_lnx_g", "od_lnx_b"):
        out[n] = _split(grads[n], 1)
    out["ff_conv"] = _split(grads["ff_conv"], 2)
    return out


def kernel(x, meta_tokens, norm_mix, norm_ffn, norm_final, ev_w_in, ev_conv_a, ev_ln_a_g, ev_ln_a_b, ev_conv_b, ev_w_out, od_w_in, od_sinks, od_mu, od_w0, od_w2, od_a0, od_a2, od_g2, od_k_k, od_k_a, od_r_k, od_lnx_g, od_lnx_b, od_w_out, ff_w_up, ff_conv, ff_conv_b, ff_w_down, loss_target, m_meta_tokens, m_norm_mix, m_norm_ffn, m_norm_final, m_ev_w_in, m_ev_conv_a, m_ev_ln_a_g, m_ev_ln_a_b, m_ev_conv_b, m_ev_w_out, m_od_w_in, m_od_sinks, m_od_mu, m_od_w0, m_od_w2, m_od_a0, m_od_a2, m_od_g2, m_od_k_k, m_od_k_a, m_od_r_k, m_od_lnx_g, m_od_lnx_b, m_od_w_out, m_ff_w_up, m_ff_conv, m_ff_conv_b, m_ff_w_down, v_meta_tokens, v_norm_mix, v_norm_ffn, v_norm_final, v_ev_w_in, v_ev_conv_a, v_ev_ln_a_g, v_ev_ln_a_b, v_ev_conv_b, v_ev_w_out, v_od_w_in, v_od_sinks, v_od_mu, v_od_w0, v_od_w2, v_od_a0, v_od_a2, v_od_g2, v_od_k_k, v_od_k_a, v_od_r_k, v_od_lnx_g, v_od_lnx_b, v_od_w_out, v_ff_w_up, v_ff_conv, v_ff_conv_b, v_ff_w_down):
    wts = dict(meta_tokens=meta_tokens, norm_mix=norm_mix, norm_ffn=norm_ffn, norm_final=norm_final, ev_w_in=ev_w_in, ev_conv_a=ev_conv_a, ev_ln_a_g=ev_ln_a_g, ev_ln_a_b=ev_ln_a_b, ev_conv_b=ev_conv_b, ev_w_out=ev_w_out, od_w_in=od_w_in, od_sinks=od_sinks, od_mu=od_mu, od_w0=od_w0, od_w2=od_w2, od_a0=od_a0, od_a2=od_a2, od_g2=od_g2, od_k_k=od_k_k, od_k_a=od_k_a, od_r_k=od_r_k, od_lnx_g=od_lnx_g, od_lnx_b=od_lnx_b, od_w_out=od_w_out, ff_w_up=ff_w_up, ff_conv=ff_conv, ff_conv_b=ff_conv_b, ff_w_down=ff_w_down)
    mom = dict(meta_tokens=m_meta_tokens, norm_mix=m_norm_mix, norm_ffn=m_norm_ffn, norm_final=m_norm_final, ev_w_in=m_ev_w_in, ev_conv_a=m_ev_conv_a, ev_ln_a_g=m_ev_ln_a_g, ev_ln_a_b=m_ev_ln_a_b, ev_conv_b=m_ev_conv_b, ev_w_out=m_ev_w_out, od_w_in=m_od_w_in, od_sinks=m_od_sinks, od_mu=m_od_mu, od_w0=m_od_w0, od_w2=m_od_w2, od_a0=m_od_a0, od_a2=m_od_a2, od_g2=m_od_g2, od_k_k=m_od_k_k, od_k_a=m_od_k_a, od_r_k=m_od_r_k, od_lnx_g=m_od_lnx_g, od_lnx_b=m_od_lnx_b, od_w_out=m_od_w_out, ff_w_up=m_ff_w_up, ff_conv=m_ff_conv, ff_conv_b=m_ff_conv_b, ff_w_down=m_ff_w_down)
    var = dict(meta_tokens=v_meta_tokens, norm_mix=v_norm_mix, norm_ffn=v_norm_ffn, norm_final=v_norm_final, ev_w_in=v_ev_w_in, ev_conv_a=v_ev_conv_a, ev_ln_a_g=v_ev_ln_a_g, ev_ln_a_b=v_ev_ln_a_b, ev_conv_b=v_ev_conv_b, ev_w_out=v_ev_w_out, od_w_in=v_od_w_in, od_sinks=v_od_sinks, od_mu=v_od_mu, od_w0=v_od_w0, od_w2=v_od_w2, od_a0=v_od_a0, od_a2=v_od_a2, od_g2=v_od_g2, od_k_k=v_od_k_k, od_k_a=v_od_k_a, od_r_k=v_od_r_k, od_lnx_g=v_od_lnx_g, od_lnx_b=v_od_lnx_b, od_w_out=v_od_w_out, ff_w_up=v_ff_w_up, ff_conv=v_ff_conv, ff_conv_b=v_ff_conv_b, ff_w_down=v_ff_w_down)

    def halves(a):
        l, rows, cols = a.shape
        return a if l == 2 else a.reshape(2, rows // 2, cols)

    me_idx = (2 * lax.axis_index("x") + lax.axis_index("y")).astype(jnp.int32).reshape(1)
    c_idx = lax.axis_index("c").astype(jnp.int32).reshape(1)
    small_mine = _pack([wts[n] for n in SMALL], F32, 2 * 8)
    half_major = [n == "ff_w_down" for n in BIG] + [False]
    mine = [halves(wts[n]) for n in BIG] + [small_mine.reshape(2, -1, PACK_W)]
    bufs = [_place_own_block(f"place_weight{i}", w, me_idx, hm, MXU_DTYPE if i < len(BIG) else F32)
            for i, (w, hm) in enumerate(zip(mine, half_major))]
    got = _gather_weights("gather_weights", bufs, half_major)
    gathered = dict(zip(SMALL, _unpack(got[-1].reshape(N_CHIPS, -1), [wts[n].shape for n in SMALL])))
    w_full = _full_weights(gathered, wts)
    for n, g in zip(BIG, got):
        if n == "ff_w_down":
            w_full[n] = g.reshape(2, D_FF, D_MODEL)
        elif n in ("ev_w_out", "od_w_out"):
            w_full[n] = g.reshape(1, D_MODEL, D_MODEL)
        else:
            w_full[n] = g.reshape((N_CHIPS,) + wts[n].shape)

    loss_local, grad_x, grads = _local_step(x[0], loss_target[0], w_full)
    loss = lax.psum(loss_local, ("x", "y", "c"))

    sg = _shard_grads(grads)
    small_rows = [jnp.concatenate([sg[n][k].reshape(-1) for n in SMALL] + [sg[n].reshape(-1) for n in REPLICATED])
                  for k in range(N_CHIPS)]
    n_el = small_rows[0].shape[0]
    n_rows = -(-n_el // (16 * PACK_W)) * 16
    small_unit = jnp.stack([jnp.pad(r, (0, n_rows * PACK_W - n_el)).reshape(n_rows, PACK_W) for r in small_rows])
    out_rows = D_MODEL // N_CHIPS
    ff_rows = D_FF // N_CHIPS
    units = [grads["ev_w_in"], grads["od_w_in"],
             grads["ev_w_out"].reshape(N_CHIPS, out_rows, D_MODEL), grads["od_w_out"].reshape(N_CHIPS, out_rows, D_MODEL),
             grads["ff_w_up"][0], grads["ff_w_up"][1],
             grads["ff_w_down"][0].reshape(N_CHIPS, ff_rows, D_MODEL), grads["ff_w_down"][1].reshape(N_CHIPS, ff_rows, D_MODEL),
             small_unit]
    dests = [(0, 0), (1, 0), (2, 0), (3, 0), (4, 0), (4, 1), (5, 0), (5, 1), (6, 0)]
    results = ["ev_w_in", "od_w_in", "ev_w_out", "od_w_out", "ff_w_up", "ff_w_down", None]
    n_layers = [1, 1, 1, 1, 2, 2, 1]
    from_sibling = _halves_to_sibling("grads_to_sibling", units)
    chip_sums = [_pair_add(f"grads_pair_add{i}", u, r, c_idx, F32 if i == len(units) - 1 else GRAD_WIRE_DTYPE)
                 for i, (u, r) in enumerate(zip(units, from_sibling))]
    from_chips = _scatter_to_chips("grads_to_chips", chip_sums)
    reduced = [None] * len(results)
    for i, (p, (r, l)) in enumerate(zip(from_chips, dests)):
        reduced[r] = _sum_chips(f"grads_chip_sum{i}", p, c_idx, l, n_layers[r], into=reduced[r])
    joined = _join_halves("grads_join", reduced)

    outs = {"grad": {}, "delta": {}, "new_m": {}, "new_v": {}}
    for n, g in zip(results[:-1], joined):
        shape = wts[n].shape
        flat = lambda a: a.reshape(-1, shape[-1])
        new = _adamw("adamw_" + n, flat(wts[n]), flat(g), flat(mom[n]), flat(var[n]))
        for tag, arr in zip(("grad", "delta", "new_m", "new_v"), (g,) + tuple(new)):
            outs[tag][n] = arr.reshape(shape)
    order = SMALL + REPLICATED
    packed = lambda d: jnp.pad(jnp.concatenate([d[n].reshape(-1) for n in order]),
                               (0, n_rows * PACK_W - n_el)).reshape(n_rows, PACK_W)
    g_small = joined[-1].reshape(n_rows, PACK_W)
    new = _adamw("adamw_small", packed(wts), g_small, packed(mom), packed(var))
    for tag, arr in zip(("grad", "delta", "new_m", "new_v"), (g_small,) + tuple(new)):
        outs[tag].update(zip(order, _unpack(arr.reshape(-1), [wts[n].shape for n in order])))
    return (loss, grad_x[None], *[outs["grad"][n] for n in WEIGHTS], *[outs["delta"][n] for n in WEIGHTS],
            *[outs["new_m"][n] for n in WEIGHTS], *[outs["new_v"][n] for n in WEIGHTS])
```

```python
import functools

import jax
import jax.numpy as jnp
from jax import lax
from jax.experimental import pallas as pl
from jax.experimental.pallas import tpu as pltpu

F32 = jnp.float32
BF16 = jnp.bfloat16
HI = lax.Precision.HIGHEST
MXU_DTYPE = BF16
GRAD_WIRE_DTYPE = BF16

D_MODEL = 1024
N_META = 16
RMS_EPS = 1e-6
LN_EPS = 1e-5
D_A = 512
CONV_A_WIDTH = 31
CONV_B_WIDTH = 3
HEAD_DIM = 64
N_Q_HEADS = 8
N_KV_HEADS = 2
GQA_GROUP = 4
D_ATT = 512
D_KV = 128
BLOCK = 128
ROPE_THETA = 10000.0
D_R = 512
N_R_HEADS = 8
LORA_W = 64
LORA_A = 64
LORA_G = 128
RWKV_GN_EPS = 64e-5
RWKV_COLS = 3 * D_R + LORA_W + LORA_A + LORA_G
D_FF = 2816
NEG_INF = -1e30
ADAM_LR = 0.001
ADAM_B1 = 0.9
ADAM_B2 = 0.999
ADAM_EPS = 1e-08
ADAM_WD = 0.01
ADAM_STEP = 10

N_CHIPS = 4
LANES = 128
CONV_PAD = 32
VMEM_LIMIT_V7X = 56 * 1024 * 1024
MESH = pl.DeviceIdType.MESH


def _cparams(sem=None):
    return pltpu.CompilerParams(dimension_semantics=sem, vmem_limit_bytes=VMEM_LIMIT_V7X)


def _row_tile(t, cap):
    for d in range(min(t, cap), 0, -1):
        if t % d == 0 and d % 16 == 0:
            return d
    return t


def _chunk_len(t):
    for d in (64, 48, 32, 16, 8):
        if t % d == 0:
            return d
    raise ValueError(t)


def _call(fn, name, grid, ins, outs, acc_axis=None, sem=None):
    n_in, n_out = len(ins), len(outs)

    def body(*refs):
        vals = fn(*[r[...] for r in refs[:n_in]])
        if not isinstance(vals, (tuple, list)):
            vals = (vals,)
        for r, v, o in zip(refs[n_in:n_in + n_out], vals, outs):
            if o[3]:
                first = pl.program_id(acc_axis) == 0

                @pl.when(first)
                def _(r=r, v=v):
                    r[...] = v

                @pl.when(jnp.logical_not(first))
                def _(r=r, v=v):
                    r[...] += v
            else:
                r[...] = v

    res = pl.pallas_call(
        body, name=name, grid=grid,
        in_specs=[pl.BlockSpec(b, m) for _, b, m in ins],
        out_specs=[pl.BlockSpec(o[1], o[2]) for o in outs],
        out_shape=[jax.ShapeDtypeStruct(o[0], F32) for o in outs],
        compiler_params=_cparams(sem),
    )(*[a for a, _, _ in ins])
    return res if n_out > 1 else res[0]


def _matmul(name, a, b, *, dims, grid, a_spec, b_spec, o_shape, o_spec, acc_shape, nk, k_axis,
            add=None, add_spec=None):
    def body(*refs):
        if add is None:
            a_ref, b_ref, o_ref, acc = refs
        else:
            a_ref, b_ref, add_ref, o_ref, acc = refs
        k = pl.program_id(k_axis)

        @pl.when(k == 0)
        def _():
            if add is None:
                acc[...] = jnp.zeros(acc.shape, F32)
            else:
                acc[...] = add_ref[...]

        acc[...] += lax.dot_general(a_ref[...].astype(MXU_DTYPE), b_ref[...].astype(MXU_DTYPE), dims,
                                    preferred_element_type=F32)

        @pl.when(k == nk - 1)
        def _():
            o_ref[...] = acc[...]

    args = [a, b] + ([] if add is None else [add])
    specs = [a_spec, b_spec] + ([] if add is None else [add_spec])
    return pl.pallas_call(
        body, name=name, grid=grid, in_specs=specs, out_specs=o_spec,
        out_shape=jax.ShapeDtypeStruct(o_shape, F32),
        scratch_shapes=[pltpu.VMEM(acc_shape, F32)],
        compiler_params=_cparams(None),
    )(*args)


_NN = (((1,), (0,)), ((), ()))
_NT = (((1,), (1,)), ((), ()))
_TN = (((0,), (0,)), ((), ()))


def _mm_cs(name, x, wg, l, tm):
    t, k = x.shape
    s, _, _, n = wg.shape
    return _matmul(name, x, wg, dims=_NN, grid=(s, t // tm, 1),
                   a_spec=pl.BlockSpec((tm, k), lambda j, i, kk: (i, 0)),
                   b_spec=pl.BlockSpec((None, None, k, n), lambda j, i, kk: (j, l, 0, 0)),
                   o_shape=(t, s * n), o_spec=pl.BlockSpec((tm, n), lambda j, i, kk: (i, j)),
                   acc_shape=(tm, n), nk=1, k_axis=2)


def _mm_full(name, x, w, l, tm, tk, add=None):
    t, k = x.shape
    n = w.shape[2]
    nk = k // tk
    return _matmul(name, x, w, dims=_NN, grid=(t // tm, 1, nk),
                   a_spec=pl.BlockSpec((tm, tk), lambda i, j, kk: (i, kk)),
                   b_spec=pl.BlockSpec((None, tk, n), lambda i, j, kk: (l, kk, 0)),
                   o_shape=(t, n), o_spec=pl.BlockSpec((tm, n), lambda i, j, kk: (i, 0)),
                   acc_shape=(tm, n), nk=nk, k_axis=2,
                   add=add, add_spec=pl.BlockSpec((tm, n), lambda i, j, kk: (i, 0)))


def _mm_nt_cs(name, dy, wg, l, tm, add=None):
    t = dy.shape[0]
    s, _, k, n = wg.shape
    return _matmul(name, dy, wg, dims=_NT, grid=(t // tm, 1, s),
                   a_spec=pl.BlockSpec((tm, n), lambda i, j, kk: (i, kk)),
                   b_spec=pl.BlockSpec((None, None, k, n), lambda i, j, kk: (kk, l, 0, 0)),
                   o_shape=(t, k), o_spec=pl.BlockSpec((tm, k), lambda i, j, kk: (i, 0)),
                   acc_shape=(tm, k), nk=s, k_axis=2,
                   add=add, add_spec=pl.BlockSpec((tm, k), lambda i, j, kk: (i, 0)))


def _mm_nt_full(name, dy, w, l, tm, tko):
    t, n = dy.shape
    k = w.shape[1]
    return _matmul(name, dy, w, dims=_NT, grid=(t // tm, k // tko, 1),
                   a_spec=pl.BlockSpec((tm, n), lambda i, j, kk: (i, 0)),
                   b_spec=pl.BlockSpec((None, tko, n), lambda i, j, kk: (l, j, 0)),
                   o_shape=(t, k), o_spec=pl.BlockSpec((tm, tko), lambda i, j, kk: (i, j)),
                   acc_shape=(tm, tko), nk=1, k_axis=2)


def _mm_tn_cs(name, x, dy, s, tk):
    t, k = x.shape
    n = dy.shape[1] // s
    nk = t // tk
    return _matmul(name, x, dy, dims=_TN, grid=(s, 1, nk),
                   a_spec=pl.BlockSpec((tk, k), lambda j, i, kk: (kk, 0)),
                   b_spec=pl.BlockSpec((tk, n), lambda j, i, kk: (kk, j)),
                   o_shape=(s, k, n), o_spec=pl.BlockSpec((None, k, n), lambda j, i, kk: (j, 0, 0)),
                   acc_shape=(k, n), nk=nk, k_axis=2)


def _mm_tn_full(name, y, dh, tk, tko):
    t, k = y.shape
    n = dh.shape[1]
    nk = t // tk
    return _matmul(name, y, dh, dims=_TN, grid=(k // tko, 1, nk),
                   a_spec=pl.BlockSpec((tk, tko), lambda j, i, kk: (kk, j)),
                   b_spec=pl.BlockSpec((tk, n), lambda j, i, kk: (kk, 0)),
                   o_shape=(k, n), o_spec=pl.BlockSpec((tko, n), lambda j, i, kk: (j, 0)),
                   acc_shape=(tko, n), nk=nk, k_axis=2)


def _sigmoid(x):
    return 1.0 / (1.0 + jnp.exp(-x))


def _rms_fwd(name, h, g, tr):
    t, d = h.shape

    def fn(hv, gv):
        r = lax.rsqrt(jnp.mean(hv * hv, axis=-1, keepdims=True) + RMS_EPS)
        return hv * r * gv

    return _call(fn, name, (t // tr,), [(h, (tr, d), lambda i: (i, 0)), (g, (1, d), lambda i: (0, 0))],
                 [((t, d), (tr, d), lambda i: (i, 0), False)])


def _rms_bwd(name, h, g, dhn, dh, tr):
    t, d = h.shape

    def fn(hv, gv, dy, dh_in):
        r = lax.rsqrt(jnp.mean(hv * hv, axis=-1, keepdims=True) + RMS_EPS)
        xh = hv * r
        dg = jnp.sum(dy * xh, axis=0, keepdims=True)
        dxh = dy * gv
        dx = r * (dxh - xh * jnp.mean(dxh * xh, axis=-1, keepdims=True))
        return dh_in + dx, dg

    row = lambda i: (i, 0)
    return _call(fn, name, (t // tr,),
                 [(h, (tr, d), row), (g, (1, d), lambda i: (0, 0)), (dhn, (tr, d), row), (dh, (tr, d), row)],
                 [((t, d), (tr, d), row, False), ((1, d), (1, d), lambda i: (0, 0), True)], acc_axis=0)


def _final_loss(name, h, g, tgt, tr):
    t, d = h.shape

    def fn(hv, gv, tv):
        r = lax.rsqrt(jnp.mean(hv * hv, axis=-1, keepdims=True) + RMS_EPS)
        xh = hv * r
        row = pl.program_id(0) * tr + lax.broadcasted_iota(jnp.int32, (tr, 1), 0)
        e = jnp.where(row >= N_META, xh * gv - tv, 0.0)
        loss = jnp.broadcast_to(0.5 * jnp.sum(jnp.sum(e * e, axis=-1, keepdims=True), axis=0, keepdims=True) / d,
                                (8, LANES))
        dout = e / d
        dg = jnp.sum(dout * xh, axis=0, keepdims=True)
        dxh = dout * gv
        dx = r * (dxh - xh * jnp.mean(dxh * xh, axis=-1, keepdims=True))
        return loss, dx, dg

    row = lambda i: (i, 0)
    fix = lambda i: (0, 0)
    return _call(fn, name, (t // tr,), [(h, (tr, d), row), (g, (1, d), fix), (tgt, (tr, d), row)],
                 [((8, LANES), (8, LANES), fix, True), ((t, d), (tr, d), row, False), ((1, d), (1, d), fix, True)],
                 acc_axis=0)


def _silu_ln(uc, g, b):
    mu = jnp.mean(uc, axis=-1, keepdims=True)
    xc = uc - mu
    rs = lax.rsqrt(jnp.mean(xc * xc, axis=-1, keepdims=True) + LN_EPS)
    ln = xc * rs * g + b
    return ln * _sigmoid(ln)


def _even_ln_fwd(name, uc, g, b, tr):
    t, d = uc.shape
    row, fix = (lambda i: (i, 0)), (lambda i: (0, 0))
    return _call(_silu_ln, name, (t // tr,), [(uc, (tr, d), row), (g, (1, d), fix), (b, (1, d), fix)],
                 [((t, d), (tr, d), row, False)])


def _even_ln_bwd(name, uc, g, b, dy, dy_col, tr):
    t, d = uc.shape

    def fn(ucv, gv, bv, dyv):
        mu = jnp.mean(ucv, axis=-1, keepdims=True)
        xc = ucv - mu
        rs = lax.rsqrt(jnp.mean(xc * xc, axis=-1, keepdims=True) + LN_EPS)
        xh = xc * rs
        ln = xh * gv + bv
        s = _sigmoid(ln)
        dln = dyv * (s * (1.0 + ln * (1.0 - s)))
        dg = jnp.sum(dln * xh, axis=0, keepdims=True)
        db = jnp.sum(dln, axis=0, keepdims=True)
        dxh = dln * gv
        duc = rs * (dxh - jnp.mean(dxh, axis=-1, keepdims=True) - xh * jnp.mean(dxh * xh, axis=-1, keepdims=True))
        return duc, dg, db

    row, fix = (lambda i: (i, 0)), (lambda i: (0, 0))
    return _call(fn, name, (t // tr,),
                 [(uc, (tr, d), row), (g, (1, d), fix), (b, (1, d), fix), (dy, (tr, d), lambda i: (i, dy_col))],
                 [((t, d), (tr, d), row, False), ((1, d), (1, d), fix, True), ((1, d), (1, d), fix, True)], acc_axis=0)


def _conv_fwd(xp, w_ref, width, t):
    acc = None
    for j in range(width):
        term = xp[pl.ds(CONV_PAD - (width - 1) + j, t), :] * w_ref[pl.ds(j, 1), :]
        acc = term if acc is None else acc + term
    return acc


def _conv_bwd_in(dyp, w_ref, width, t):
    acc = None
    for j in range(width):
        term = dyp[pl.ds(width - 1 - j, t), :] * w_ref[pl.ds(j, 1), :]
        acc = term if acc is None else acc + term
    return acc


def _conv_bwd_w(dy, xp, dw_ref, width, t):
    for j in range(width):
        dw_ref[pl.ds(j, 1), :] = jnp.sum(dy * xp[pl.ds(CONV_PAD - (width - 1) + j, t), :], axis=0, keepdims=True)


def _store_front(xp, x, t):
    xp[pl.ds(0, CONV_PAD), :] = jnp.zeros((CONV_PAD, LANES), F32)
    xp[pl.ds(CONV_PAD, t), :] = x


def _store_back(xp, x, t):
    xp[pl.ds(0, t), :] = x
    xp[pl.ds(t, CONV_PAD), :] = jnp.zeros((CONV_PAD, LANES), F32)


def _col_call(body, name, ncol, ins, outs, t, n_scratch):
    def spec(rows, off):
        return pl.BlockSpec((rows, LANES), lambda j, off=off: (0, j + off))

    res = pl.pallas_call(
        body, name=name, grid=(ncol,),
        in_specs=[spec(r, off) for _, r, off in ins],
        out_specs=[spec(r, 0) for r, _ in outs],
        out_shape=[jax.ShapeDtypeStruct((r, c), F32) for r, c in outs],
        scratch_shapes=[pltpu.VMEM((t + CONV_PAD, LANES), F32) for _ in range(n_scratch)],
        compiler_params=_cparams(None),
    )(*[a for a, _, _ in ins])
    return res


def _even_col_fwd(name, p, conv_a, conv_b):
    t = p.shape[0]
    nc = D_A // LANES

    def body(av, ag, gb, gc, xi, ca, cb, uc_ref, yb_ref, xp):
        _store_front(xp, av[...] * _sigmoid(ag[...]), t)
        uc_ref[...] = _conv_fwd(xp, ca, CONV_A_WIDTH, t)
        _store_front(xp, gc[...] * xi[...], t)
        yb_ref[...] = gb[...] * _conv_fwd(xp, cb, CONV_B_WIDTH, t)

    ins = [(p, t, k * nc) for k in range(5)] + [(conv_a, CONV_A_WIDTH, 0), (conv_b, CONV_B_WIDTH, 0)]
    return _col_call(body, name, nc, ins, [(t, D_A), (t, D_A)], t, 1)


def _even_col_bwd(name, p, duc, dy, conv_a, conv_b):
    t = p.shape[0]
    nc = D_A // LANES

    def body(av, ag, gb, gc, xi, duc_ref, dyb_ref, ca, cb, dav, dag, dgb, dgc, dxi, dca, dcb, xp, dyp):
        sig = _sigmoid(ag[...])
        _store_front(xp, av[...] * sig, t)
        _store_back(dyp, duc_ref[...], t)
        _conv_bwd_w(duc_ref[...], xp, dca, CONV_A_WIDTH, t)
        du = _conv_bwd_in(dyp, ca, CONV_A_WIDTH, t)
        dav[...] = du * sig
        dag[...] = du * av[...] * sig * (1.0 - sig)
        _store_front(xp, gc[...] * xi[...], t)
        zc = _conv_fwd(xp, cb, CONV_B_WIDTH, t)
        dgb[...] = dyb_ref[...] * zc
        dzc = dyb_ref[...] * gb[...]
        _conv_bwd_w(dzc, xp, dcb, CONV_B_WIDTH, t)
        _store_back(dyp, dzc, t)
        dz = _conv_bwd_in(dyp, cb, CONV_B_WIDTH, t)
        dgc[...] = dz * xi[...]
        dxi[...] = dz * gc[...]

    ins = ([(p, t, k * nc) for k in range(5)] + [(duc, t, 0), (dy, t, nc)]
           + [(conv_a, CONV_A_WIDTH, 0), (conv_b, CONV_B_WIDTH, 0)])
    outs = [(t, D_A)] * 5 + [(CONV_A_WIDTH, D_A), (CONV_B_WIDTH, D_A)]
    return _col_call(body, name, nc, ins, outs, t, 2)


def _ffn_col_fwd(name, u, conv, bias):
    t = u.shape[0]
    nc = D_FF // LANES

    def body(g_ref, v_ref, cw, b_ref, a_ref, xp):
        _store_front(xp, g_ref[...], t)
        gc = _conv_fwd(xp, cw, CONV_B_WIDTH, t) + b_ref[...]
        a_ref[...] = gc * _sigmoid(gc) * v_ref[...]

    ins = [(u, t, 0), (u, t, nc), (conv, CONV_B_WIDTH, 0), (bias, 1, 0)]
    return _col_call(body, name, nc, ins, [(t, D_FF)], t, 1)[0]


def _ffn_col_bwd(name, u, da, conv, bias):
    t = u.shape[0]
    nc = D_FF // LANES

    def body(g_ref, v_ref, da_ref, cw, b_ref, du_ref, dcw, db_ref, xp, dyp):
        _store_front(xp, g_ref[...], t)
        gc = _conv_fwd(xp, cw, CONV_B_WIDTH, t) + b_ref[...]
        s = _sigmoid(gc)

        @pl.when(pl.program_id(1) == 0)
        def _():
            dgc = da_ref[...] * v_ref[...] * (s * (1.0 + gc * (1.0 - s)))
            db_ref[...] = jnp.sum(dgc, axis=0, keepdims=True)
            _conv_bwd_w(dgc, xp, dcw, CONV_B_WIDTH, t)
            _store_back(dyp, dgc, t)
            du_ref[...] = _conv_bwd_in(dyp, cw, CONV_B_WIDTH, t)

        @pl.when(pl.program_id(1) == 1)
        def _():
            du_ref[...] = da_ref[...] * gc * s

    col = lambda rows, off: pl.BlockSpec((rows, LANES), lambda j, p: (0, j + off))
    return pl.pallas_call(
        body, name=name, grid=(nc, 2),
        in_specs=[col(t, 0), col(t, nc), col(t, 0), col(CONV_B_WIDTH, 0), col(1, 0)],
        out_specs=[pl.BlockSpec((t, LANES), lambda j, p: (0, j + nc * p)), col(CONV_B_WIDTH, 0), col(1, 0)],
        out_shape=[jax.ShapeDtypeStruct((t, 2 * D_FF), F32), jax.ShapeDtypeStruct((CONV_B_WIDTH, D_FF), F32),
                   jax.ShapeDtypeStruct((1, D_FF), F32)],
        scratch_shapes=[pltpu.VMEM((t + CONV_PAD, LANES), F32) for _ in range(2)],
        compiler_params=_cparams(None),
    )(u, u, da, conv, bias)


def _shift_fwd(name, p, col0, mu):
    t = p.shape[0]

    def body(x_ref, mu_ref, o_ref, xp):
        _store_front(xp, x_ref[...], t)
        prev = xp[pl.ds(CONV_PAD - 1, t), :]
        o_ref[...] = x_ref[...] + (prev - x_ref[...]) * mu_ref[...]

    return _col_call(body, name, RWKV_COLS // LANES, [(p, t, col0), (mu, 1, 0)], [(t, RWKV_COLS)], t, 1)[0]


def _shift_bwd(name, p, col0, mu, dprs):
    t = p.shape[0]

    def body(x_ref, mu_ref, d_ref, dx_ref, dmu_ref, xp, dyp):
        _store_front(xp, x_ref[...], t)
        prev = xp[pl.ds(CONV_PAD - 1, t), :]
        dmu_ref[...] = jnp.sum(d_ref[...] * (prev - x_ref[...]), axis=0, keepdims=True)
        dm = d_ref[...] * mu_ref[...]
        _store_back(dyp, dm, t)
        dx_ref[...] = d_ref[...] - dm + dyp[pl.ds(1, t), :]

    ins = [(p, t, col0), (mu, 1, 0), (dprs, t, 0)]
    return _col_call(body, name, RWKV_COLS // LANES, ins, [(t, RWKV_COLS), (1, RWKV_COLS)], t, 2)


def _hi_lo(x):
    hi = x.astype(BF16)
    return hi, (x - hi.astype(F32)).astype(BF16)


def _dot_passes(a, b, dims, passes):
    d = lambda p, q: lax.dot_general(p, q, dims, preferred_element_type=F32)
    if passes == 1:
        return d(a.astype(MXU_DTYPE), b.astype(MXU_DTYPE))
    ah, al = _hi_lo(a)
    bh, bl = _hi_lo(b)
    return d(ah, bh) + (d(ah, bl) + d(al, bh))


@functools.partial(jax.custom_vjp, nondiff_argnums=(2, 3))
def _dot_vjp(a, b, dims, passes):
    return _dot_passes(a, b, dims, passes)


def _dot_fwd(a, b, dims, passes):
    return _dot_passes(a, b, dims, passes), (a, b)


def _dot_bwd(dims, passes, res, g):
    a, b = res
    if dims == _NN:
        return _dot_passes(g, b, _NT, passes), _dot_passes(a, g, _TN, passes)
    if dims == _NT:
        return _dot_passes(g, b, _NN, passes), _dot_passes(g, a, _TN, passes)
    return _dot_passes(b, g, _NT, passes), _dot_passes(a, g, _NN, passes)


_dot_vjp.defvjp(_dot_fwd, _dot_bwd)


def _doth(a, b, dims=_NN):
    return _dot_vjp(a, b, dims, 3)


def _dotb(a, b, dims=_NN):
    return _dot_vjp(a, b, dims, 1)


def _softplus(x):
    return jnp.where(x > 0, x, 0.0) + jnp.log(1.0 + jnp.exp(jnp.where(x > 0, -x, x)))


def _rwkv_pre(k, xl, gd, w0, w2p, a0, a2p, g2, k_k, k_a, seg):
    z = w0 + _dotb(jnp.tanh(xl), w2p)
    lw = -jnp.exp(-_softplus(-z) - 0.5)
    alpha = _sigmoid(a0 + _dotb(xl, a2p))
    g = _dotb(_sigmoid(gd), g2)
    kk = k * k_k
    kk = kk / jnp.maximum(jnp.sqrt(_dotb(kk * kk, seg)), 1e-12)
    k2 = k * (1.0 + (alpha - 1.0) * k_a)
    return lw, k2, -kk, kk * alpha, g


def _rwkv_post(y, r, k2, v, g, lnx_g, lnx_b, r_k, seg):
    mean = _dotb(y, seg) * (1.0 / HEAD_DIM)
    yc = y - mean
    var = _dotb(yc * yc, seg) * (1.0 / HEAD_DIM)
    yo = yc * lax.rsqrt(var + RWKV_GN_EPS) * lnx_g + lnx_b
    bonus = _dotb(r * k2 * r_k, seg) * v
    return (yo + bonus) * g


def _rwkv_pre_fwd(name, prs, prm, seg, tr):
    t = prs.shape[0]
    row = lambda i: (i, 0)
    fix = lambda i: (0, 0)
    ins = [(prs, (tr, D_R), lambda i: (i, 1)), (prs, (tr, LANES), lambda i: (i, 12)), (prs, (tr, LANES), lambda i: (i, 13)),
           (prm["w0"], (1, D_R), fix), (prm["w2p"], (LANES, D_R), fix), (prm["a0"], (1, D_R), fix),
           (prm["a2p"], (LANES, D_R), fix), (prm["g2"], (LANES, D_R), fix), (prm["k_k"], (1, D_R), fix),
           (prm["k_a"], (1, D_R), fix), (seg, (D_R, D_R), fix)]
    return _call(_rwkv_pre, name, (t // tr,), ins, [((t, D_R), (tr, D_R), row, False)] * 5)


def _rwkv_pre_bwd(name, prs, prm, seg, cts, tr):
    t = prs.shape[0]

    def fn(k, xl, gd, w0, w2p, a0, a2p, g2, k_k, k_a, segv, *ct):
        _, vjp = jax.vjp(lambda *a: _rwkv_pre(*a, segv), k, xl, gd, w0, w2p, a0, a2p, g2, k_k, k_a)
        return vjp(tuple(ct))

    row = lambda i: (i, 0)
    fix = lambda i: (0, 0)
    ins = [(prs, (tr, D_R), lambda i: (i, 1)), (prs, (tr, LANES), lambda i: (i, 12)), (prs, (tr, LANES), lambda i: (i, 13)),
           (prm["w0"], (1, D_R), fix), (prm["w2p"], (LANES, D_R), fix), (prm["a0"], (1, D_R), fix),
           (prm["a2p"], (LANES, D_R), fix), (prm["g2"], (LANES, D_R), fix), (prm["k_k"], (1, D_R), fix),
           (prm["k_a"], (1, D_R), fix), (seg, (D_R, D_R), fix)] + [(c, (tr, D_R), row) for c in cts]
    outs = [((t, D_R), (tr, D_R), row, False), ((t, LANES), (tr, LANES), row, False), ((t, LANES), (tr, LANES), row, False),
            ((1, D_R), (1, D_R), fix, True), ((LANES, D_R), (LANES, D_R), fix, True), ((1, D_R), (1, D_R), fix, True),
            ((LANES, D_R), (LANES, D_R), fix, True), ((LANES, D_R), (LANES, D_R), fix, True),
            ((1, D_R), (1, D_R), fix, True), ((1, D_R), (1, D_R), fix, True)]
    return _call(fn, name, (t // tr,), ins, outs, acc_axis=0)


def _rwkv_post_ins(y, prs, k2, g, prm, seg, tr):
    row = lambda i: (i, 0)
    fix = lambda i: (0, 0)
    return [(y, (tr, D_R), row), (prs, (tr, D_R), row), (k2, (tr, D_R), row), (prs, (tr, D_R), lambda i: (i, 2)),
            (g, (tr, D_R), row), (prm["lnx_g"], (1, D_R), fix), (prm["lnx_b"], (1, D_R), fix), (prm["r_k"], (1, D_R), fix),
            (seg, (D_R, D_R), fix)]


def _rwkv_post_fwd(name, y, prs, k2, g, prm, seg, tr):
    t = y.shape[0]
    return _call(_rwkv_post, name, (t // tr,), _rwkv_post_ins(y, prs, k2, g, prm, seg, tr),
                 [((t, D_R), (tr, D_R), lambda i: (i, 0), False)])


def _rwkv_post_bwd(name, y, prs, k2, g, prm, seg, dy, dy_col, tr):
    t = y.shape[0]

    def fn(yv, r, k2v, v, gv, lg, lb, rk, segv, ct):
        _, vjp = jax.vjp(lambda *a: _rwkv_post(*a, segv), yv, r, k2v, v, gv, lg, lb, rk)
        return vjp(ct)

    row = lambda i: (i, 0)
    fix = lambda i: (0, 0)
    ins = _rwkv_post_ins(y, prs, k2, g, prm, seg, tr) + [(dy, (tr, D_R), lambda i: (i, dy_col))]
    outs = [((t, D_R), (tr, D_R), row, False)] * 5 + [((1, D_R), (1, D_R), fix, True)] * 3
    return _call(fn, name, (t // tr,), ins, outs, acc_axis=0)


def _wkv_chunk(s0, r, lw, k, v, a, b):
    c = r[0].shape[0]
    lane = lax.broadcasted_iota(jnp.int32, (1, 2 * HEAD_DIM), 1)
    first = (lane < HEAD_DIM).astype(F32)
    per_head = lambda x: jnp.concatenate([x * first, x * (1.0 - first)], axis=0)

    def time_of(shape, dim):
        i = lax.broadcasted_iota(jnp.int32, shape, dim)
        return jnp.where(i >= c, i - c, i)

    incl = (lax.broadcasted_iota(jnp.int32, (c, c), 0) >= lax.broadcasted_iota(jnp.int32, (c, c), 1)).astype(F32)
    strict2 = time_of((2 * c, 2 * c), 0) > time_of((2 * c, 2 * c), 1)
    incl2 = lax.broadcasted_iota(jnp.int32, (c, 2 * c), 0) >= time_of((c, 2 * c), 1)
    each = lambda f, *xs: [f(*x) for x in zip(*xs)]
    cum = each(lambda x: _doth(incl, x), lw)
    tot = each(lambda x: jnp.sum(x, axis=0, keepdims=True), lw)
    e_inv = each(lambda x: jnp.exp(-x), cum)
    a_st = each(lambda x, cm, l: per_head(x * jnp.exp(cm - l)), a, cum, lw)
    r_t = each(lambda x, cm: x * jnp.exp(cm), r, cum)
    b_st = each(lambda x, e: per_head(x * e), b, e_inv)
    k_st = each(lambda x, e: per_head(x * e), k, e_inv)
    v_st = each(per_head, v)
    m = each(lambda x, w: jnp.where(strict2, _dotb(x, w, _NT), 0.0), a_st, b_st)
    m_k = each(lambda x, w: jnp.where(strict2, _dotb(x, w, _NT), 0.0), a_st, k_st)
    u = each(lambda x, s, mk, w: _dotb(x, s, _NT) + _dotb(mk, w), a_st, s0, m_k, v_st)
    steps = (c - 1).bit_length()
    for s in range(steps):
        u = each(lambda x, w: x + _dotb(w, x), u, m)
        if s + 1 < steps:
            m = each(lambda w: _dotb(w, w), m)
    n_b = each(lambda x, w: jnp.where(incl2, _dotb(x, w, _NT), 0.0), r_t, b_st)
    n_k = each(lambda x, w: jnp.where(incl2, _dotb(x, w, _NT), 0.0), r_t, k_st)
    y = each(lambda x, s, nb, uu, nk, w: _dotb(x, s, _NT) + _dotb(nb, uu) + _dotb(nk, w), r_t, s0, n_b, u, n_k, v_st)
    dec = each(lambda tt, cm: jnp.exp(tt - cm), tot, cum)
    s1 = each(lambda s, tt, uu, x, d, w, kk: s * jnp.exp(tt) + _dotb(uu, per_head(x * d), _TN) + _dotb(w, per_head(kk * d), _TN),
              s0, tot, u, b, dec, v_st, k)
    return tuple(y), tuple(s1)


WKV_PAIRS_PER_STEP = 4
PAIR = 2 * HEAD_DIM


def _wkv_fwd(name, srcs):
    t = srcs[0][0].shape[0]
    c = _chunk_len(t)
    nc = t // c
    pp = WKV_PAIRS_PER_STEP
    n_pairs = D_R // PAIR

    def body(r, lw, k, v, a, b, y_ref, st_ref, state):
        @pl.when(pl.program_id(1) == 0)
        def _():
            state[...] = jnp.zeros(state.shape, F32)

        pairs = lambda ref: tuple(ref[:, pl.ds(i * PAIR, PAIR)] for i in range(pp))
        s0 = tuple(state[i] for i in range(pp))
        y, s1 = _wkv_chunk(s0, pairs(r), pairs(lw), pairs(k), pairs(v), pairs(a), pairs(b))
        for i in range(pp):
            st_ref[i] = s0[i]
            y_ref[:, pl.ds(i * PAIR, PAIR)] = y[i]
            state[i] = s1[i]

    seq = lambda off: pl.BlockSpec((c, pp * PAIR), lambda g, j: (j, off + g))
    return pl.pallas_call(
        body, name=name, grid=(n_pairs // pp, nc), in_specs=[seq(off) for _, off in srcs],
        out_specs=[seq(0), pl.BlockSpec((pp, None, PAIR, PAIR), lambda g, j: (g, j, 0, 0))],
        out_shape=[jax.ShapeDtypeStruct((t, D_R), F32), jax.ShapeDtypeStruct((n_pairs, nc, PAIR, PAIR), F32)],
        scratch_shapes=[pltpu.VMEM((pp, PAIR, PAIR), F32)],
        compiler_params=_cparams(None),
    )(*[a for a, _ in srcs])


def _wkv_bwd(name, srcs, st, dy):
    t = srcs[0][0].shape[0]
    c = _chunk_len(t)
    nc = t // c
    pp = WKV_PAIRS_PER_STEP
    n_pairs = D_R // PAIR

    def body(r, lw, k, v, a, b, st_ref, dy_ref, dr, dlw, dk, dv, da, db, dstate):
        @pl.when(pl.program_id(1) == 0)
        def _():
            dstate[...] = jnp.zeros(dstate.shape, F32)

        half = lax.broadcasted_iota(jnp.int32, (PAIR, PAIR), 0) < HEAD_DIM
        same_head = half == (lax.broadcasted_iota(jnp.int32, (PAIR, PAIR), 1) < HEAD_DIM)
        pairs = lambda ref: tuple(ref[:, pl.ds(i * PAIR, PAIR)] for i in range(pp))
        s0 = tuple(st_ref[i] for i in range(pp))
        _, vjp = jax.vjp(_wkv_chunk, s0, pairs(r), pairs(lw), pairs(k), pairs(v), pairs(a), pairs(b))
        ds0, *dxs = vjp((pairs(dy_ref), tuple(dstate[i] for i in range(pp))))
        for i in range(pp):
            for ref, val in zip((dr, dlw, dk, dv, da, db), dxs):
                ref[:, pl.ds(i * PAIR, PAIR)] = val[i]
            dstate[i] = jnp.where(same_head, ds0[i], 0.0)

    seq = lambda off: pl.BlockSpec((c, pp * PAIR), lambda g, j: (nc - 1 - j, off + g))
    return pl.pallas_call(
        body, name=name, grid=(n_pairs // pp, nc),
        in_specs=[seq(off) for _, off in srcs]
        + [pl.BlockSpec((pp, None, PAIR, PAIR), lambda g, j: (g, nc - 1 - j, 0, 0)), seq(dy[1])],
        out_specs=[seq(0)] * 6,
        out_shape=[jax.ShapeDtypeStruct((t, D_R), F32)] * 6,
        scratch_shapes=[pltpu.VMEM((pp, PAIR, PAIR), F32)],
        compiler_params=_cparams(None),
    )(*[a for a, _ in srcs], st, dy[0])


def _rope(x, cos, sin, rot):
    return x * cos + _dotb(x, rot) * sin


def _attn_block(nb, q, kp, kc, km, vp, vc, vm, sk, cq, sq, cp, sp, cm, sm, rot):
    g = GQA_GROUP
    scale = HEAD_DIM ** -0.5
    down = lambda x: jnp.concatenate([x] * g, axis=0)
    kpr = _rope(kp, cp, sp, rot)
    kcr = _rope(kc, cq, sq, rot)
    kmr = _rope(km, cm, sm, rot)
    qr = _rope(q, down(cq), down(sq), rot)
    i = lax.broadcasted_iota(jnp.int32, (g * BLOCK, BLOCK), 0)
    i = i - BLOCK * ((i >= BLOCK).astype(jnp.int32) + (i >= 2 * BLOCK).astype(jnp.int32) + (i >= 3 * BLOCK).astype(jnp.int32))
    j = lax.broadcasted_iota(jnp.int32, (g * BLOCK, BLOCK), 1)
    nbv = jnp.zeros((g * BLOCK, BLOCK), jnp.int32) + nb
    ok_p = (j > i) & (nbv >= 2)
    ok_c = (j <= i) & (nbv >= 1)
    ok_m = (j >= BLOCK - N_META) & ((nbv >= 1) | (j <= i))
    sink = jnp.concatenate([jnp.broadcast_to(s, (BLOCK, 1)) for s in sk], axis=0)
    s_p = jnp.where(ok_p, _dotb(qr, kpr, _NT) * scale, NEG_INF)
    s_c = jnp.where(ok_c, _dotb(qr, kcr, _NT) * scale, NEG_INF)
    s_m = jnp.where(ok_m, _dotb(qr, kmr, _NT) * scale, NEG_INF)
    rmax = lambda s: jnp.max(s, axis=-1, keepdims=True)
    m = lax.stop_gradient(jnp.maximum(jnp.maximum(rmax(s_p), rmax(s_c)), jnp.maximum(rmax(s_m), sink)))
    e_p, e_c, e_m = jnp.exp(s_p - m), jnp.exp(s_c - m), jnp.exp(s_m - m)
    rsum = lambda e: jnp.sum(e, axis=-1, keepdims=True)
    inv = 1.0 / (rsum(e_p) + rsum(e_c) + rsum(e_m) + jnp.exp(sink - m))
    return _dotb(e_p * inv, vp) + _dotb(e_c * inv, vc) + _dotb(e_m * inv, vm)


def _attn_specs():
    cur = lambda g, n: (g, n, 0)
    prev = lambda g, n: (g, jnp.maximum(n - 1, 0), 0)
    meta = lambda g, n: (g, 0, 0)
    kv = lambda m: pl.BlockSpec((None, BLOCK, HEAD_DIM), m)
    tab = lambda m: pl.BlockSpec((BLOCK, HEAD_DIM), m)
    tcur, tprev, tmeta = (lambda g, n: (n, 0)), (lambda g, n: (jnp.maximum(n - 1, 0), 0)), (lambda g, n: (0, 0))
    qspec = pl.BlockSpec((GQA_GROUP, BLOCK, HEAD_DIM), cur)
    sspec = pl.BlockSpec((GQA_GROUP, 8, LANES), meta)
    specs = [qspec, kv(prev), kv(cur), kv(meta), kv(prev), kv(cur), kv(meta), sspec,
             tab(tcur), tab(tcur), tab(tprev), tab(tprev), tab(tmeta), tab(tmeta),
             pl.BlockSpec((HEAD_DIM, HEAD_DIM), lambda g, n: (0, 0))]
    return specs, qspec, sspec, kv


def _attn_args(q, k, v, sinks_b, cos, sin, rot):
    return (q, k, k, k, v, v, v, sinks_b, cos, sin, cos, sin, cos, sin, rot)


def _attn_fwd(name, q, k, v, sinks_b, cos, sin, rot):
    tp = q.shape[1]
    specs, qspec, _, _ = _attn_specs()

    def body(q_ref, kp, kc, km, vp, vc, vm, s_ref, cq, sq, cp, sp, cm, sm, rot_ref, o_ref):
        q = jnp.concatenate([q_ref[h] for h in range(GQA_GROUP)], axis=0)
        sk = tuple(s_ref[h][0:1, 0:1] for h in range(GQA_GROUP))
        out = _attn_block(pl.program_id(1), q, kp[...], kc[...], km[...], vp[...], vc[...], vm[...], sk,
                          cq[...], sq[...], cp[...], sp[...], cm[...], sm[...], rot_ref[...])
        for h in range(GQA_GROUP):
            o_ref[h] = out[h * BLOCK:(h + 1) * BLOCK]

    return pl.pallas_call(
        body, name=name, grid=(N_KV_HEADS, tp // BLOCK), in_specs=specs, out_specs=qspec,
        out_shape=jax.ShapeDtypeStruct(q.shape, F32), compiler_params=_cparams(None),
    )(*_attn_args(q, k, v, sinks_b, cos, sin, rot))


def _attn_bwd(name, q, k, v, sinks_b, cos, sin, rot, do):
    tp = q.shape[1]
    nb = tp // BLOCK
    specs, qspec, sspec, kv = _attn_specs()

    def body(q_ref, kp, kc, km, vp, vc, vm, s_ref, cq, sq, cp, sp, cm, sm, rot_ref, do_ref,
             dq_ref, dkp, dkc, dvp, dvc, dkm, dvm, ds_ref):
        n = pl.program_id(1)
        q = jnp.concatenate([q_ref[h] for h in range(GQA_GROUP)], axis=0)
        sk = tuple(s_ref[h][0:1, 0:1] for h in range(GQA_GROUP))
        tabs = (cq[...], sq[...], cp[...], sp[...], cm[...], sm[...], rot_ref[...])
        _, vjp = jax.vjp(lambda *a: _attn_block(n, *a, *tabs), q, kp[...], kc[...], km[...], vp[...], vc[...], vm[...], sk)
        dq, gkp, gkc, gkm, gvp, gvc, gvm, dsk = vjp(jnp.concatenate([do_ref[h] for h in range(GQA_GROUP)], axis=0))
        dkp[...] = gkp
        dkc[...] = gkc
        dvp[...] = gvp
        dvc[...] = gvc
        for h in range(GQA_GROUP):
            dq_ref[h] = dq[h * BLOCK:(h + 1) * BLOCK]

        @pl.when(n == 0)
        def _():
            dkm[...] = gkm
            dvm[...] = gvm
            for h in range(GQA_GROUP):
                ds_ref[h] = jnp.broadcast_to(dsk[h], (8, LANES))

        @pl.when(n != 0)
        def _():
            dkm[...] += gkm
            dvm[...] += gvm
            for h in range(GQA_GROUP):
                ds_ref[h] += jnp.broadcast_to(dsk[h], (8, LANES))

    part = pl.BlockSpec((None, None, BLOCK, HEAD_DIM), lambda g, n: (g, n, 0, 0))
    part_shape = jax.ShapeDtypeStruct((N_KV_HEADS, nb, BLOCK, HEAD_DIM), F32)
    meta_shape = jax.ShapeDtypeStruct((N_KV_HEADS, BLOCK, HEAD_DIM), F32)
    return pl.pallas_call(
        body, name=name, grid=(N_KV_HEADS, nb), in_specs=specs + [qspec],
        out_specs=[qspec, part, part, part, part, kv(lambda g, n: (g, 0, 0)), kv(lambda g, n: (g, 0, 0)), sspec],
        out_shape=[jax.ShapeDtypeStruct(q.shape, F32), part_shape, part_shape, part_shape, part_shape,
                   meta_shape, meta_shape, jax.ShapeDtypeStruct(sinks_b.shape, F32)],
        compiler_params=_cparams(None),
    )(*_attn_args(q, k, v, sinks_b, cos, sin, rot), do)


def _kv_combine(name, prev_part, own_part, meta):
    g, nb = own_part.shape[:2]

    def fn(own, nxt, mt):
        m = pl.program_id(1)
        one = jnp.ones((BLOCK, HEAD_DIM), F32)
        use_next = jnp.where(one * m < nb - 1, 1.0, 0.0)
        use_meta = jnp.where(one * m < 1, 1.0, 0.0)
        return own + nxt * use_next + mt * use_meta

    blk = (None, None, BLOCK, HEAD_DIM)
    return _call(fn, name, (g, nb),
                 [(own_part, blk, lambda a, m: (a, m, 0, 0)),
                  (prev_part, blk, lambda a, m: (a, jnp.minimum(m + 1, nb - 1), 0, 0)),
                  (meta, (None, BLOCK, HEAD_DIM), lambda a, m: (a, 0, 0))],
                 [((g, nb * BLOCK, HEAD_DIM), (None, BLOCK, HEAD_DIM), lambda a, m: (a, m, 0), False)])


PACK_W = 1024
ELEMENTWISE_BLOCK_BYTES = 1 << 21


def _rows_tile(rows, cols):
    cap = max(8, ELEMENTWISE_BLOCK_BYTES // (4 * cols))
    for d in range(min(rows, cap), 0, -1):
        if rows % d == 0 and d % 8 == 0:
            return d
    return rows


def _adamw(name, w, g, m, v):
    rows, cols = w.shape
    tr = _rows_tile(rows, cols)

    def fn(wv, gv, mv, vv):
        m1 = ADAM_B1 * mv + (1.0 - ADAM_B1) * gv
        v1 = ADAM_B2 * vv + (1.0 - ADAM_B2) * (gv * gv)
        m_hat = m1 / (1.0 - ADAM_B1 ** ADAM_STEP)
        v_hat = v1 / (1.0 - ADAM_B2 ** ADAM_STEP)
        return -ADAM_LR * (m_hat / (jnp.sqrt(v_hat) + ADAM_EPS) + ADAM_WD * wv), m1, v1

    blk = (tr, cols)
    row = lambda i: (i, 0)
    return _call(fn, name, (rows // tr,), [(a, blk, row) for a in (w, g, m, v)], [((rows, cols), blk, row, False)] * 3)


def _pair_add(name, g, recv, c_idx, out_dtype):
    s, a, b = g.shape
    half = a // 2

    def body(c_ref, a_ref, b_ref, o_ref):
        o_ref[...] = (a_ref[...] + b_ref[...]).astype(out_dtype)

    blk = (None, half, b)
    return pl.pallas_call(
        body, name=name,
        grid_spec=pltpu.PrefetchScalarGridSpec(
            num_scalar_prefetch=1, grid=(s,),
            in_specs=[pl.BlockSpec(blk, lambda j, c: (j, c[0], 0)), pl.BlockSpec(blk, lambda j, c: (j, 0, 0))],
            out_specs=pl.BlockSpec(blk, lambda j, c: (j, 0, 0))),
        out_shape=jax.ShapeDtypeStruct((s, half, b), out_dtype), compiler_params=_cparams(None),
    )(c_idx, g, recv)


def _sum_chips(name, parts, c_idx, layer, n_layers, into=None):
    _, a, b = parts.shape
    tr = _rows_tile(a, b)

    def body(c_ref, p0, p1, p2, p3, *rest):
        o_ref = rest[-1]
        up = lambda p: p[...].astype(F32)
        o_ref[...] = ((up(p0) + up(p1)) + up(p2)) + up(p3)

    in_specs = [pl.BlockSpec((None, tr, b), lambda i, c, k=k: (k, i, 0)) for k in range(N_CHIPS)]
    args = [c_idx] + [parts] * N_CHIPS
    aliases = {}
    if into is not None:
        in_specs.append(_ANY)
        args.append(into)
        aliases = {1 + N_CHIPS: 0}
    return pl.pallas_call(
        body, name=name,
        grid_spec=pltpu.PrefetchScalarGridSpec(
            num_scalar_prefetch=1, grid=(a // tr,), in_specs=in_specs,
            out_specs=pl.BlockSpec((None, None, tr, b), lambda i, c: (layer, c[0], i, 0))),
        out_shape=jax.ShapeDtypeStruct((n_layers, 2, a, b), F32), input_output_aliases=aliases,
        compiler_params=_cparams(None),
    )(*args)


def _place_own_block(name, w, me_idx, half_major, dtype):
    _, a, b = w.shape
    tr = _rows_tile(a, b)
    if half_major:
        shape, index = (2, N_CHIPS, a, b), (lambda h, i, me: (h, me[0], i, 0))
    else:
        shape, index = (N_CHIPS, 2, a, b), (lambda h, i, me: (me[0], h, i, 0))

    def body(me_ref, w_ref, o_ref):
        o_ref[...] = w_ref[...].astype(dtype)

    return pl.pallas_call(
        body, name=name,
        grid_spec=pltpu.PrefetchScalarGridSpec(
            num_scalar_prefetch=1, grid=(2, a // tr),
            in_specs=[pl.BlockSpec((None, tr, b), lambda h, i, me: (h, i, 0))],
            out_specs=pl.BlockSpec((None, None, tr, b), index)),
        out_shape=jax.ShapeDtypeStruct(shape, dtype), compiler_params=_cparams(None),
    )(me_idx, w)


def _mesh_pos():
    return lax.axis_index("x"), lax.axis_index("y"), lax.axis_index("c")


def _other_chips(x, y):
    return [(1 - x, y), (x, 1 - y), (1 - x, 1 - y)]


_ANY = pl.BlockSpec(memory_space=pl.ANY)


def _gather_weights(name, bufs, half_major):
    n = len(bufs)

    def body(*refs):
        out_refs = refs[n:2 * n]
        send_sems, recv_sems = refs[2 * n:]
        x, y, c = _mesh_pos()
        me = 2 * x + y
        sibling = (x, y, 1 - c)
        chips = _other_chips(x, y)

        def place(i, chip_idx, half):
            return out_refs[i].at[half, chip_idx] if half_major[i] else out_refs[i].at[chip_idx, half]

        def copy(i, k, chip_idx, half, to):
            return pltpu.make_async_remote_copy(src_ref=place(i, chip_idx, half), dst_ref=place(i, chip_idx, half),
                                                send_sem=send_sems.at[6 * i + k], recv_sem=recv_sems.at[6 * i + k],
                                                device_id=to, device_id_type=MESH)

        first = [copy(i, j, me, c, (*chip, c)) for i in range(n) for j, chip in enumerate(chips)]
        for cp in first:
            cp.start()
        passed = []
        for i in range(n):
            for j, (cx, cy) in enumerate(chips):
                idx = 2 * cx + cy
                copy(i, j, idx, c, sibling).wait_recv()
                fwd = copy(i, 3 + j, idx, c, sibling)
                fwd.start()
                passed.append(fwd)
        for i in range(n):
            for j, (cx, cy) in enumerate(chips):
                copy(i, 3 + j, 2 * cx + cy, 1 - c, sibling).wait_recv()
        for cp in first + passed:
            cp.wait_send()

    return pl.pallas_call(
        body, name=name, in_specs=[_ANY] * n, out_specs=[_ANY] * n,
        out_shape=[jax.ShapeDtypeStruct(b.shape, b.dtype) for b in bufs],
        input_output_aliases={i: i for i in range(n)},
        scratch_shapes=[pltpu.SemaphoreType.DMA((6 * n,)), pltpu.SemaphoreType.DMA((6 * n,))],
        compiler_params=pltpu.CompilerParams(has_side_effects=True),
    )(*bufs)


def _halves_to_sibling(name, units):
    n = len(units)

    def body(*refs):
        g_refs, out_refs = refs[:n], refs[n:2 * n]
        send_sems, recv_sems = refs[2 * n:]
        x, y, c = _mesh_pos()
        cps = []
        for i in range(n):
            half = units[i].shape[1] // 2
            src = g_refs[i].at[pl.ds(0, N_CHIPS), pl.ds((1 - c) * half, half)]
            cp = pltpu.make_async_remote_copy(src_ref=src, dst_ref=out_refs[i], send_sem=send_sems.at[i],
                                              recv_sem=recv_sems.at[i], device_id=(x, y, 1 - c), device_id_type=MESH)
            cp.start()
            cps.append(cp)
        for cp in cps:
            cp.wait()

    return pl.pallas_call(
        body, name=name, in_specs=[_ANY] * n, out_specs=[_ANY] * n,
        out_shape=[jax.ShapeDtypeStruct((u.shape[0], u.shape[1] // 2, u.shape[2]), u.dtype) for u in units],
        scratch_shapes=[pltpu.SemaphoreType.DMA((n,)), pltpu.SemaphoreType.DMA((n,))],
        compiler_params=pltpu.CompilerParams(has_side_effects=True),
    )(*units)


def _scatter_to_chips(name, sums):
    n = len(sums)

    def body(*refs):
        h_refs, out_refs = refs[:n], refs[n:2 * n]
        send_sems, recv_sems, local_sems = refs[2 * n:]
        x, y, c = _mesh_pos()
        me = 2 * x + y
        chips = _other_chips(x, y)
        local = [pltpu.make_async_copy(h_refs[i].at[me], out_refs[i].at[me], local_sems.at[i]) for i in range(n)]
        for cp in local:
            cp.start()

        def copy(i, j, src_idx, dst_idx):
            cx, cy = chips[j]
            return pltpu.make_async_remote_copy(src_ref=h_refs[i].at[src_idx], dst_ref=out_refs[i].at[dst_idx],
                                                send_sem=send_sems.at[3 * i + j], recv_sem=recv_sems.at[3 * i + j],
                                                device_id=(cx, cy, c), device_id_type=MESH)

        cps = [copy(i, j, 2 * chips[j][0] + chips[j][1], me) for i in range(n) for j in range(3)]
        for cp in cps:
            cp.start()
        for i in range(n):
            for j in range(3):
                copy(i, j, me, 2 * chips[j][0] + chips[j][1]).wait_recv()
        for cp in cps:
            cp.wait_send()
        for cp in local:
            cp.wait()

    return pl.pallas_call(
        body, name=name, in_specs=[_ANY] * n, out_specs=[_ANY] * n,
        out_shape=[jax.ShapeDtypeStruct(s.shape, s.dtype) for s in sums],
        scratch_shapes=[pltpu.SemaphoreType.DMA((3 * n,)), pltpu.SemaphoreType.DMA((3 * n,)), pltpu.SemaphoreType.DMA((n,))],
        compiler_params=pltpu.CompilerParams(has_side_effects=True),
    )(*sums)


def _join_halves(name, results):
    n = len(results)
    pieces = [(i, l) for i in range(n) for l in range(results[i].shape[0])]

    def body(*refs):
        out_refs = refs[n:2 * n]
        send_sems, recv_sems = refs[2 * n:]
        x, y, c = _mesh_pos()

        def copy(k, half):
            i, l = pieces[k]
            return pltpu.make_async_remote_copy(src_ref=out_refs[i].at[l, half], dst_ref=out_refs[i].at[l, half],
                                                send_sem=send_sems.at[k], recv_sem=recv_sems.at[k],
                                                device_id=(x, y, 1 - c), device_id_type=MESH)

        cps = [copy(k, c) for k in range(len(pieces))]
        for cp in cps:
            cp.start()
        for k in range(len(pieces)):
            copy(k, 1 - c).wait_recv()
        for cp in cps:
            cp.wait_send()

    return pl.pallas_call(
        body, name=name, in_specs=[_ANY] * n, out_specs=[_ANY] * n,
        out_shape=[jax.ShapeDtypeStruct(r.shape, r.dtype) for r in results],
        input_output_aliases={i: i for i in range(n)},
        scratch_shapes=[pltpu.SemaphoreType.DMA((len(pieces),)), pltpu.SemaphoreType.DMA((len(pieces),))],
        compiler_params=pltpu.CompilerParams(has_side_effects=True),
    )(*results)


def _pack(arrays, dtype, rows_multiple):
    flat = jnp.concatenate([a.reshape(-1).astype(dtype) for a in arrays])
    unit = rows_multiple * PACK_W
    total = -(-flat.shape[0] // unit) * unit
    return jnp.pad(flat, (0, total - flat.shape[0])).reshape(total // PACK_W, PACK_W)


def _unpack(flat, shapes):
    out, off = [], 0
    for s in shapes:
        n = 1
        for d in s:
            n *= d
        out.append(flat[..., off:off + n].reshape(flat.shape[:-1] + tuple(s)))
        off += n
    return out


def _ffn_fwd(tag, l, h, g, w_up, conv, bias, w_down, tm):
    hn = _rms_fwd(f"{tag}_norm", h, g, tm)
    u = _mm_cs(f"{tag}_up", hn, w_up, l, tm)
    act = _ffn_col_fwd(f"{tag}_glu", u, conv, bias)
    h_out = _mm_full(f"{tag}_down", act, w_down, l, tm, D_FF // 2, add=h)
    return h_out, (hn, u, act)


def _ffn_bwd(tag, l, h, g, w_up, conv, bias, w_down, saved, dh, tm):
    hn, u, act = saved
    da = _mm_nt_full(f"{tag}_down_dx", dh, w_down, l, tm, D_FF // 2)
    dw_down = _mm_tn_full(f"{tag}_down_dw", act, dh, tm, D_FF // 2)
    du, dconv, dbias = _ffn_col_bwd(f"{tag}_glu_bwd", u, da, conv, bias)
    dw_up = _mm_tn_cs(f"{tag}_up_dw", hn, du, N_CHIPS, tm)
    dhn = _mm_nt_cs(f"{tag}_up_dx", du, w_up, l, tm)
    dh, dg = _rms_bwd(f"{tag}_norm_bwd", h, g, dhn, dh, tm)
    return dh, dict(norm=dg, w_up=dw_up, conv=dconv, bias=dbias, w_down=dw_down)


def _to_heads(z, nh, pad):
    t = z.shape[0]
    return jnp.pad(z.reshape(t, nh, HEAD_DIM).transpose(1, 0, 2), ((0, 0), (pad, 0), (0, 0)))


def _from_heads(z, pad):
    nh, tp, _ = z.shape
    return z[:, pad:].transpose(1, 0, 2).reshape(tp - pad, nh * HEAD_DIM)


def _rope_tables(tp, pad):
    half = HEAD_DIM // 2
    inv = ROPE_THETA ** (-jnp.arange(half, dtype=F32) / half)
    ang = (jnp.arange(tp, dtype=F32) - pad)[:, None] * inv[None, :]
    cos, sin = jnp.cos(ang), jnp.sin(ang)
    rot = jnp.zeros((HEAD_DIM, HEAD_DIM), F32)
    idx = jnp.arange(half)
    rot = rot.at[idx + half, idx].set(-1.0).at[idx, idx + half].set(1.0)
    return jnp.concatenate([cos, cos], axis=1), jnp.concatenate([sin, sin], axis=1), rot


def _local_step(x, tgt, w):
    seq = x.shape[0]
    t = seq + N_META
    tm = _row_tile(t, 704)
    tr = _row_tile(t, 352)
    pad = BLOCK - N_META
    grads = {}

    h0 = jnp.concatenate([w["meta_tokens"], x], axis=0)
    tgt_p = jnp.pad(tgt, ((N_META, 0), (0, 0)))

    hn0 = _rms_fwd("l0_norm", h0, w["norm_mix"][0:1], tm)
    p0 = _mm_cs("l0_in", hn0, w["ev_w_in"], 0, tm)
    uc, yb = _even_col_fwd("l0_convs", p0, w["ev_conv_a"], w["ev_conv_b"])
    ya = _even_ln_fwd("l0_ln", uc, w["ev_ln_a_g"], w["ev_ln_a_b"], tm)
    y0 = jnp.concatenate([ya, yb], axis=1)
    h1 = _mm_full("l0_out", y0, w["ev_w_out"], 0, tm, D_MODEL, add=h0)
    f0 = (0, h1, w["norm_ffn"][0:1], w["ff_w_up"], w["ff_conv"][0], w["ff_conv_b"][0:1], w["ff_w_down"])
    h2, ffn0 = _ffn_fwd("f0", *f0, tm)

    hn2 = _rms_fwd("l1_norm", h2, w["norm_mix"][1:2], tm)
    p1 = _mm_cs("l1_in", hn2, w["od_w_in"], 0, tm)
    cos, sin, rot = _rope_tables(t + pad, pad)
    qh = _to_heads(p1[:, :D_ATT], N_Q_HEADS, pad)
    kh = _to_heads(p1[:, D_ATT:D_ATT + D_KV], N_KV_HEADS, pad)
    vh = _to_heads(p1[:, D_ATT + D_KV:D_ATT + 2 * D_KV], N_KV_HEADS, pad)
    sinks_b = jnp.broadcast_to(w["od_sinks"].reshape(N_Q_HEADS, 1, 1), (N_Q_HEADS, 8, LANES))
    y_att = _from_heads(_attn_fwd("l1_attn", qh, kh, vh, sinks_b, cos, sin, rot), pad)

    col0 = (D_ATT + 2 * D_KV) // LANES
    ch = jnp.arange(D_R) // HEAD_DIM
    seg = (ch[:, None] == ch[None, :]).astype(F32)
    prm = dict(w0=w["od_w0"], a0=w["od_a0"], g2=w["od_g2"], k_k=w["od_k_k"], k_a=w["od_k_a"],
               lnx_g=w["od_lnx_g"], lnx_b=w["od_lnx_b"], r_k=w["od_r_k"].reshape(1, D_R),
               w2p=jnp.concatenate([w["od_w2"], jnp.zeros((LORA_A, D_R), F32)], axis=0),
               a2p=jnp.concatenate([jnp.zeros((LORA_W, D_R), F32), w["od_a2"]], axis=0))
    prs = _shift_fwd("l1_shift", p1, col0, w["od_mu"])
    lw, k2, a_, b_, gate_r = _rwkv_pre_fwd("l1_rwkv_pre", prs, prm, seg, tr)
    v_off = 2 * D_R // (WKV_PAIRS_PER_STEP * PAIR)
    scan_in = [(prs, 0), (lw, 0), (k2, 0), (prs, v_off), (a_, 0), (b_, 0)]
    y_scan, states = _wkv_fwd("l1_wkv", scan_in)
    y_rwkv = _rwkv_post_fwd("l1_rwkv_post", y_scan, prs, k2, gate_r, prm, seg, tr)
    y1 = jnp.concatenate([y_att, y_rwkv], axis=1)
    h3 = _mm_full("l1_out", y1, w["od_w_out"], 0, tm, D_MODEL, add=h2)
    f1 = (1, h3, w["norm_ffn"][1:2], w["ff_w_up"], w["ff_conv"][1], w["ff_conv_b"][1:2], w["ff_w_down"])
    h4, ffn1 = _ffn_fwd("f1", *f1, tm)

    loss_blk, dh, d_norm_final = _final_loss("final", h4, w["norm_final"], tgt_p, tm)
    grads["norm_final"] = d_norm_final

    dh, gf1 = _ffn_bwd("f1", *f1, ffn1, dh, tm)
    dy1 = _mm_nt_full("l1_out_dx", dh, w["od_w_out"], 0, tm, D_MODEL)
    grads["od_w_out"] = _mm_tn_full("l1_out_dw", y1, dh, tm, D_MODEL // 2)
    dy_scan, dr_p, dk2_p, dv_p, dgate_r, grads["od_lnx_g"], grads["od_lnx_b"], d_rk = _rwkv_post_bwd(
        "l1_rwkv_post_bwd", y_scan, prs, k2, gate_r, prm, seg, dy1, 1, tr)
    grads["od_r_k"] = d_rk.reshape(N_R_HEADS, HEAD_DIM)
    dr_s, dlw, dk2_s, dv_s, da_, db_ = _wkv_bwd("l1_wkv_bwd", scan_in, states, (dy_scan, 0))
    dk, dxl, dgd, grads["od_w0"], dw2p, grads["od_a0"], da2p, grads["od_g2"], grads["od_k_k"], grads["od_k_a"] = (
        _rwkv_pre_bwd("l1_rwkv_pre_bwd", prs, prm, seg, (dlw, dk2_s + dk2_p, da_, db_, dgate_r), tr))
    grads["od_w2"] = dw2p[:LORA_W]
    grads["od_a2"] = da2p[LORA_W:]
    dprs = jnp.concatenate([dr_s + dr_p, dk, dv_s + dv_p, dxl, dgd], axis=1)
    dpr, grads["od_mu"] = _shift_bwd("l1_shift_bwd", p1, col0, w["od_mu"], dprs)
    doh = _to_heads(dy1[:, :D_ATT], N_Q_HEADS, pad)
    dqh, dkp, dkc, dvp, dvc, dkm, dvm, dsinks = _attn_bwd("l1_attn_bwd", qh, kh, vh, sinks_b, cos, sin, rot, doh)
    grads["od_sinks"] = dsinks[:, 0, 0].reshape(1, N_Q_HEADS)
    dkh = _kv_combine("l1_attn_dk", dkp, dkc, dkm)
    dvh = _kv_combine("l1_attn_dv", dvp, dvc, dvm)
    dp1 = jnp.concatenate([_from_heads(dqh, pad), _from_heads(dkh, pad), _from_heads(dvh, pad), dpr], axis=1)
    grads["od_w_in"] = _mm_tn_cs("l1_in_dw", hn2, dp1, N_CHIPS, tm)
    dhn2 = _mm_nt_cs("l1_in_dx", dp1, w["od_w_in"], 0, tm)
    dh, d_mix1 = _rms_bwd("l1_norm_bwd", h2, w["norm_mix"][1:2], dhn2, dh, tm)

    dh, gf0 = _ffn_bwd("f0", *f0, ffn0, dh, tm)
    dy0 = _mm_nt_full("l0_out_dx", dh, w["ev_w_out"], 0, tm, D_MODEL)
    grads["ev_w_out"] = _mm_tn_full("l0_out_dw", y0, dh, tm, D_MODEL // 2)
    duc, grads["ev_ln_a_g"], grads["ev_ln_a_b"] = _even_ln_bwd("l0_ln_bwd", uc, w["ev_ln_a_g"], w["ev_ln_a_b"], dy0, 0, tm)
    *dparts, grads["ev_conv_a"], grads["ev_conv_b"] = _even_col_bwd("l0_convs_bwd", p0, duc, dy0, w["ev_conv_a"], w["ev_conv_b"])
    dp0 = jnp.concatenate(dparts, axis=1)
    grads["ev_w_in"] = _mm_tn_cs("l0_in_dw", hn0, dp0, N_CHIPS, tm)
    dhn0 = _mm_nt_cs("l0_in_dx", dp0, w["ev_w_in"], 0, tm)
    dh, d_mix0 = _rms_bwd("l0_norm_bwd", h0, w["norm_mix"][0:1], dhn0, dh, tm)

    grads["norm_mix"] = jnp.concatenate([d_mix0, d_mix1], axis=0)
    grads["norm_ffn"] = jnp.concatenate([gf0["norm"], gf1["norm"]], axis=0)
    grads["ff_w_up"] = [gf0["w_up"], gf1["w_up"]]
    grads["ff_conv"] = jnp.stack([gf0["conv"], gf1["conv"]])
    grads["ff_conv_b"] = jnp.concatenate([gf0["bias"], gf1["bias"]], axis=0)
    grads["ff_w_down"] = [gf0["w_down"], gf1["w_down"]]
    grads["meta_tokens"] = dh[:N_META]
    return loss_blk[0, 0], dh[N_META:], grads


SHARD_AXIS = {
    "meta_tokens": 1, "norm_mix": None, "norm_ffn": None, "norm_final": None,
    "ev_w_in": 2, "ev_conv_a": 2, "ev_ln_a_g": None, "ev_ln_a_b": None, "ev_conv_b": 2, "ev_w_out": 1,
    "od_w_in": 2, "od_sinks": None, "od_mu": 1, "od_w0": 1, "od_w2": 2, "od_a0": 1, "od_a2": 2, "od_g2": 2,
    "od_k_k": 1, "od_k_a": 1, "od_r_k": None, "od_lnx_g": 1, "od_lnx_b": 1, "od_w_out": 1,
    "ff_w_up": 2, "ff_conv": 2, "ff_conv_b": None, "ff_w_down": 1,
}
WEIGHTS = list(SHARD_AXIS)
BIG = ("ev_w_in", "ev_w_out", "od_w_in", "od_w_out", "ff_w_up", "ff_w_down")
SHARDED = [n for n in WEIGHTS if SHARD_AXIS[n] is not None]
SMALL = [n for n in SHARDED if n not in BIG]
REPLICATED = [n for n in WEIGHTS if SHARD_AXIS[n] is None]


def _join(g, axis):
    return jnp.concatenate([g[k] for k in range(N_CHIPS)], axis=axis)


def _split(full, axis):
    return jnp.stack(jnp.split(full, N_CHIPS, axis=axis))


def _full_weights(gathered, repl):
    w = {}
    sq = lambda a: a.reshape(a.shape[1:]) if a.shape[0] == 1 else a
    for n in REPLICATED:
        w[n] = repl[n]
    w["norm_final"] = repl["norm_final"].reshape(1, D_MODEL)
    for n in ("ev_ln_a_g", "ev_ln_a_b"):
        w[n] = repl[n].reshape(1, D_A)
    w["od_r_k"] = repl["od_r_k"][0]
    w["meta_tokens"] = _join(gathered["meta_tokens"], 1)
    for n in ("ev_conv_a", "ev_conv_b", "od_w2", "od_a2", "od_g2"):
        w[n] = sq(_join(gathered[n], 2))
    for n in ("od_mu", "od_w0", "od_a0", "od_k_k", "od_k_a", "od_lnx_g", "od_lnx_b"):
        w[n] = _join(gathered[n], 1)
    w["ff_conv"] = _join(gathered["ff_conv"], 2)
    return w


def _shard_grads(grads):
    out = {}
    for n in REPLICATED:
        out[n] = grads[n]
    out["norm_final"] = grads["norm_final"].reshape(D_MODEL)
    out["od_r_k"] = grads["od_r_k"][None]
    out["meta_tokens"] = _split(grads["meta_tokens"], 1)
    for n in ("ev_conv_a", "ev_conv_b", "od_w2", "od_a2", "od_g2"):
        out[n] = _split(grads[n][None], 2)
    for n in ("od_mu", "od_w0", "od_a0", "od_k_k", "od_k_a", "od_lnx_g", "od_lnx_b"):
        out[n] = _split(grads[n], 1)
    out["ff_conv"] = _split(grads["ff_conv"], 2)
    return out


def kernel(x, meta_tokens, norm_mix, norm_ffn, norm_final, ev_w_in, ev_conv_a, ev_ln_a_g, ev_ln_a_b, ev_conv_b, ev_w_out, od_w_in, od_sinks, od_mu, od_w0, od_w2, od_a0, od_a2, od_g2, od_k_k, od_k_a, od_r_k, od_lnx_g, od_lnx_b, od_w_out, ff_w_up, ff_conv, ff_conv_b, ff_w_down, loss_target, m_meta_tokens, m_norm_mix, m_norm_ffn, m_norm_final, m_ev_w_in, m_ev_conv_a, m_ev_ln_a_g, m_ev_ln_a_b, m_ev_conv_b, m_ev_w_out, m_od_w_in, m_od_sinks, m_od_mu, m_od_w0, m_od_w2, m_od_a0, m_od_a2, m_od_g2, m_od_k_k, m_od_k_a, m_od_r_k, m_od_lnx_g, m_od_lnx_b, m_od_w_out, m_ff_w_up, m_ff_conv, m_ff_conv_b, m_ff_w_down, v_meta_tokens, v_norm_mix, v_norm_ffn, v_norm_final, v_ev_w_in, v_ev_conv_a, v_ev_ln_a_g, v_ev_ln_a_b, v_ev_conv_b, v_ev_w_out, v_od_w_in, v_od_sinks, v_od_mu, v_od_w0, v_od_w2, v_od_a0, v_od_a2, v_od_g2, v_od_k_k, v_od_k_a, v_od_r_k, v_od_lnx_g, v_od_lnx_b, v_od_w_out, v_ff_w_up, v_ff_conv, v_ff_conv_b, v_ff_w_down):
    wts = dict(meta_tokens=meta_tokens, norm_mix=norm_mix, norm_ffn=norm_ffn, norm_final=norm_final, ev_w_in=ev_w_in, ev_conv_a=ev_conv_a, ev_ln_a_g=ev_ln_a_g, ev_ln_a_b=ev_ln_a_b, ev_conv_b=ev_conv_b, ev_w_out=ev_w_out, od_w_in=od_w_in, od_sinks=od_sinks, od_mu=od_mu, od_w0=od_w0, od_w2=od_w2, od_a0=od_a0, od_a2=od_a2, od_g2=od_g2, od_k_k=od_k_k, od_k_a=od_k_a, od_r_k=od_r_k, od_lnx_g=od_lnx_g, od_lnx_b=od_lnx_b, od_w_out=od_w_out, ff_w_up=ff_w_up, ff_conv=ff_conv, ff_conv_b=ff_conv_b, ff_w_down=ff_w_down)
    mom = dict(meta_tokens=m_meta_tokens, norm_mix=m_norm_mix, norm_ffn=m_norm_ffn, norm_final=m_norm_final, ev_w_in=m_ev_w_in, ev_conv_a=m_ev_conv_a, ev_ln_a_g=m_ev_ln_a_g, ev_ln_a_b=m_ev_ln_a_b, ev_conv_b=m_ev_conv_b, ev_w_out=m_ev_w_out, od_w_in=m_od_w_in, od_sinks=m_od_sinks, od_mu=m_od_mu, od_w0=m_od_w0, od_w2=m_od_w2, od_a0=m_od_a0, od_a2=m_od_a2, od_g2=m_od_g2, od_k_k=m_od_k_k, od_k_a=m_od_k_a, od_r_k=m_od_r_k, od_lnx_g=m_od_lnx_g, od_lnx_b=m_od_lnx_b, od_w_out=m_od_w_out, ff_w_up=m_ff_w_up, ff_conv=m_ff_conv, ff_conv_b=m_ff_conv_b, ff_w_down=m_ff_w_down)
    var = dict(meta_tokens=v_meta_tokens, norm_mix=v_norm_mix, norm_ffn=v_norm_ffn, norm_final=v_norm_final, ev_w_in=v_ev_w_in, ev_conv_a=v_ev_conv_a, ev_ln_a_g=v_ev_ln_a_g, ev_ln_a_b=v_ev_ln_a_b, ev_conv_b=v_ev_conv_b, ev_w_out=v_ev_w_out, od_w_in=v_od_w_in, od_sinks=v_od_sinks, od_mu=v_od_mu, od_w0=v_od_w0, od_w2=v_od_w2, od_a0=v_od_a0, od_a2=v_od_a2, od_g2=v_od_g2, od_k_k=v_od_k_k, od_k_a=v_od_k_a, od_r_k=v_od_r_k, od_lnx_g=v_od_lnx_g, od_lnx_b=v_od_lnx_b, od_w_out=v_od_w_out, ff_w_up=v_ff_w_up, ff_conv=v_ff_conv, ff_conv_b=v_ff_conv_b, ff_w_down=v_ff_w_down)

    def halves(a):
        l, rows, cols = a.shape
        return a if l == 2 else a.reshape(2, rows // 2, cols)

    me_idx = (2 * lax.axis_index("x") + lax.axis_index("y")).astype(jnp.int32).reshape(1)
    c_idx = lax.axis_index("c").astype(jnp.int32).reshape(1)
    small_mine = _pack([wts[n] for n in SMALL], F32, 2 * 8)
    half_major = [n == "ff_w_down" for n in BIG] + [False]
    mine = [halves(wts[n]) for n in BIG] + [small_mine.reshape(2, -1, PACK_W)]
    bufs = [_place_own_block(f"place_weight{i}", w, me_idx, hm, MXU_DTYPE if i < len(BIG) else F32)
            for i, (w, hm) in enumerate(zip(mine, half_major))]
    got = _gather_weights("gather_weights", bufs, half_major)
    gathered = dict(zip(SMALL, _unpack(got[-1].reshape(N_CHIPS, -1), [wts[n].shape for n in SMALL])))
    w_full = _full_weights(gathered, wts)
    for n, g in zip(BIG, got):
        if n == "ff_w_down":
            w_full[n] = g.reshape(2, D_FF, D_MODEL)
        elif n in ("ev_w_out", "od_w_out"):
            w_full[n] = g.reshape(1, D_MODEL, D_MODEL)
        else:
            w_full[n] = g.reshape((N_CHIPS,) + wts[n].shape)

    loss_local, grad_x, grads = _local_step(x[0], loss_target[0], w_full)
    loss = lax.psum(loss_local, ("x", "y", "c"))

    sg = _shard_grads(grads)
    small_rows = [jnp.concatenate([sg[n][k].reshape(-1) for n in SMALL] + [sg[n].reshape(-1) for n in REPLICATED])
                  for k in range(N_CHIPS)]
    n_el = small_rows[0].shape[0]
    n_rows = -(-n_el // (16 * PACK_W)) * 16
    small_unit = jnp.stack([jnp.pad(r, (0, n_rows * PACK_W - n_el)).reshape(n_rows, PACK_W) for r in small_rows])
    out_rows = D_MODEL // N_CHIPS
    ff_rows = D_FF // N_CHIPS
    units = [grads["ev_w_in"], grads["od_w_in"],
             grads["ev_w_out"].reshape(N_CHIPS, out_rows, D_MODEL), grads["od_w_out"].reshape(N_CHIPS, out_rows, D_MODEL),
             grads["ff_w_up"][0], grads["ff_w_up"][1],
             grads["ff_w_down"][0].reshape(N_CHIPS, ff_rows, D_MODEL), grads["ff_w_down"][1].reshape(N_CHIPS, ff_rows, D_MODEL),
             small_unit]
    dests = [(0, 0), (1, 0), (2, 0), (3, 0), (4, 0), (4, 1), (5, 0), (5, 1), (6, 0)]
    results = ["ev_w_in", "od_w_in", "ev_w_out", "od_w_out", "ff_w_up", "ff_w_down", None]
    n_layers = [1, 1, 1, 1, 2, 2, 1]
    from_sibling = _halves_to_sibling("grads_to_sibling", units)
    chip_sums = [_pair_add(f"grads_pair_add{i}", u, r, c_idx, F32 if i == len(units) - 1 else GRAD_WIRE_DTYPE)
                 for i, (u, r) in enumerate(zip(units, from_sibling))]
    from_chips = _scatter_to_chips("grads_to_chips", chip_sums)
    reduced = [None] * len(results)
    for i, (p, (r, l)) in enumerate(zip(from_chips, dests)):
        reduced[r] = _sum_chips(f"grads_chip_sum{i}", p, c_idx, l, n_layers[r], into=reduced[r])
    joined = _join_halves("grads_join", reduced)

    outs = {"grad": {}, "delta": {}, "new_m": {}, "new_v": {}}
    for n, g in zip(results[:-1], joined):
        shape = wts[n].shape
        flat = lambda a: a.reshape(-1, shape[-1])
        new = _adamw("adamw_" + n, flat(wts[n]), flat(g), flat(mom[n]), flat(var[n]))
        for tag, arr in zip(("grad", "delta", "new_m", "new_v"), (g,) + tuple(new)):
            outs[tag][n] = arr.reshape(shape)
    order = SMALL + REPLICATED
    packed = lambda d: jnp.pad(jnp.concatenate([d[n].reshape(-1) for n in order]),
                               (0, n_rows * PACK_W - n_el)).reshape(n_rows, PACK_W)
    g_small = joined[-1].reshape(n_rows, PACK_W)
    new = _adamw("adamw_small", packed(wts), g_small, packed(mom), packed(var))
    for tag, arr in zip(("grad", "delta", "new_m", "new_v"), (g_small,) + tuple(new)):
        outs[tag].update(zip(order, _unpack(arr.reshape(-1), [wts[n].shape for n in order])))
    return (loss, grad_x[None], *[outs["grad"][n] for n in WEIGHTS], *[outs["delta"][n] for n in WEIGHTS],
            *[outs["new_m"][n] for n in WEIGHTS], *[outs["new_v"][n] for n in WEIGHTS])
```

```python
import functools

import jax
import jax.numpy as jnp
from jax import lax
from jax.experimental import pallas as pl
from jax.experimental.pallas import tpu as pltpu

F32 = jnp.float32
BF16 = jnp.bfloat16
HI = lax.Precision.HIGHEST
MXU_DTYPE = BF16
GRAD_WIRE_DTYPE = BF16

D_MODEL = 1024
N_META = 16
RMS_EPS = 1e-6
LN_EPS = 1e-5
D_A = 512
CONV_A_WIDTH = 31
CONV_B_WIDTH = 3
HEAD_DIM = 64
N_Q_HEADS = 8
N_KV_HEADS = 2
GQA_GROUP = 4
D_ATT = 512
D_KV = 128
BLOCK = 128
ROPE_THETA = 10000.0
D_R = 512
N_R_HEADS = 8
LORA_W = 64
LORA_A = 64
LORA_G = 128
RWKV_GN_EPS = 64e-5
RWKV_COLS = 3 * D_R + LORA_W + LORA_A + LORA_G
D_FF = 2816
NEG_INF = -1e30
ADAM_LR = 0.001
ADAM_B1 = 0.9
ADAM_B2 = 0.999
ADAM_EPS = 1e-08
ADAM_WD = 0.01
ADAM_STEP = 10

N_CHIPS = 4
LANES = 128
CONV_PAD = 32
VMEM_LIMIT_V7X = 56 * 1024 * 1024
MESH = pl.DeviceIdType.MESH


def _cparams(sem=None):
    return pltpu.CompilerParams(dimension_semantics=sem, vmem_limit_bytes=VMEM_LIMIT_V7X)


def _row_tile(t, cap):
    for d in range(min(t, cap), 0, -1):
        if t % d == 0 and d % 16 == 0:
            return d
    return t


def _chunk_len(t):
    for d in (64, 48, 32, 16, 8):
        if t % d == 0:
            return d
    raise ValueError(t)


def _call(fn, name, grid, ins, outs, acc_axis=None, sem=None):
    n_in, n_out = len(ins), len(outs)

    def body(*refs):
        vals = fn(*[r[...] for r in refs[:n_in]])
        if not isinstance(vals, (tuple, list)):
            vals = (vals,)
        for r, v, o in zip(refs[n_in:n_in + n_out], vals, outs):
            if o[3]:
                first = pl.program_id(acc_axis) == 0

                @pl.when(first)
                def _(r=r, v=v):
                    r[...] = v

                @pl.when(jnp.logical_not(first))
                def _(r=r, v=v):
                    r[...] += v
            else:
                r[...] = v

    res = pl.pallas_call(
        body, name=name, grid=grid,
        in_specs=[pl.BlockSpec(b, m) for _, b, m in ins],
        out_specs=[pl.BlockSpec(o[1], o[2]) for o in outs],
        out_shape=[jax.ShapeDtypeStruct(o[0], F32) for o in outs],
        compiler_params=_cparams(sem),
    )(*[a for a, _, _ in ins])
    return res if n_out > 1 else res[0]


def _matmul(name, a, b, *, dims, grid, a_spec, b_spec, o_shape, o_spec, acc_shape, nk, k_axis,
            add=None, add_spec=None):
    def body(*refs):
        if add is None:
            a_ref, b_ref, o_ref, acc = refs
        else:
            a_ref, b_ref, add_ref, o_ref, acc = refs
        k = pl.program_id(k_axis)

        @pl.when(k == 0)
        def _():
            if add is None:
                acc[...] = jnp.zeros(acc.shape, F32)
            else:
                acc[...] = add_ref[...]

        acc[...] += lax.dot_general(a_ref[...].astype(MXU_DTYPE), b_ref[...].astype(MXU_DTYPE), dims,
                                    preferred_element_type=F32)

        @pl.when(k == nk - 1)
        def _():
            o_ref[...] = acc[...]

    args = [a, b] + ([] if add is None else [add])
    specs = [a_spec, b_spec] + ([] if add is None else [add_spec])
    return pl.pallas_call(
        body, name=name, grid=grid, in_specs=specs, out_specs=o_spec,
        out_shape=jax.ShapeDtypeStruct(o_shape, F32),
        scratch_shapes=[pltpu.VMEM(acc_shape, F32)],
        compiler_params=_cparams(None),
    )(*args)


_NN = (((1,), (0,)), ((), ()))
_NT = (((1,), (1,)), ((), ()))
_TN = (((0,), (0,)), ((), ()))


def _mm_cs(name, x, wg, l, tm):
    t, k = x.shape
    s, _, _, n = wg.shape
    return _matmul(name, x, wg, dims=_NN, grid=(s, t // tm, 1),
                   a_spec=pl.BlockSpec((tm, k), lambda j, i, kk: (i, 0)),
                   b_spec=pl.BlockSpec((None, None, k, n), lambda j, i, kk: (j, l, 0, 0)),
                   o_shape=(t, s * n), o_spec=pl.BlockSpec((tm, n), lambda j, i, kk: (i, j)),
                   acc_shape=(tm, n), nk=1, k_axis=2)


def _mm_full(name, x, w, l, tm, tk, add=None):
    t, k = x.shape
    n = w.shape[2]
    nk = k // tk
    return _matmul(name, x, w, dims=_NN, grid=(t // tm, 1, nk),
                   a_spec=pl.BlockSpec((tm, tk), lambda i, j, kk: (i, kk)),
                   b_spec=pl.BlockSpec((None, tk, n), lambda i, j, kk: (l, kk, 0)),
                   o_shape=(t, n), o_spec=pl.BlockSpec((tm, n), lambda i, j, kk: (i, 0)),
                   acc_shape=(tm, n), nk=nk, k_axis=2,
                   add=add, add_spec=pl.BlockSpec((tm, n), lambda i, j, kk: (i, 0)))


def _mm_nt_cs(name, dy, wg, l, tm, add=None):
    t = dy.shape[0]
    s, _, k, n = wg.shape
    return _matmul(name, dy, wg, dims=_NT, grid=(t // tm, 1, s),
                   a_spec=pl.BlockSpec((tm, n), lambda i, j, kk: (i, kk)),
                   b_spec=pl.BlockSpec((None, None, k, n), lambda i, j, kk: (kk, l, 0, 0)),
                   o_shape=(t, k), o_spec=pl.BlockSpec((tm, k), lambda i, j, kk: (i, 0)),
                   acc_shape=(tm, k), nk=s, k_axis=2,
                   add=add, add_spec=pl.BlockSpec((tm, k), lambda i, j, kk: (i, 0)))


def _mm_nt_full(name, dy, w, l, tm, tko):
    t, n = dy.shape
    k = w.shape[1]
    return _matmul(name, dy, w, dims=_NT, grid=(t // tm, k // tko, 1),
                   a_spec=pl.BlockSpec((tm, n), lambda i, j, kk: (i, 0)),
                   b_spec=pl.BlockSpec((None, tko, n), lambda i, j, kk: (l, j, 0)),
                   o_shape=(t, k), o_spec=pl.BlockSpec((tm, tko), lambda i, j, kk: (i, j)),
                   acc_shape=(tm, tko), nk=1, k_axis=2)


def _mm_tn_cs(name, x, dy, s, tk):
    t, k = x.shape
    n = dy.shape[1] // s
    nk = t // tk
    return _matmul(name, x, dy, dims=_TN, grid=(s, 1, nk),
                   a_spec=pl.BlockSpec((tk, k), lambda j, i, kk: (kk, 0)),
                   b_spec=pl.BlockSpec((tk, n), lambda j, i, kk: (kk, j)),
                   o_shape=(s, k, n), o_spec=pl.BlockSpec((None, k, n), lambda j, i, kk: (j, 0, 0)),
                   acc_shape=(k, n), nk=nk, k_axis=2)


def _mm_tn_full(name, y, dh, tk, tko):
    t, k = y.shape
    n = dh.shape[1]
    nk = t // tk
    return _matmul(name, y, dh, dims=_TN, grid=(k // tko, 1, nk),
                   a_spec=pl.BlockSpec((tk, tko), lambda j, i, kk: (kk, j)),
                   b_spec=pl.BlockSpec((tk, n), lambda j, i, kk: (kk, 0)),
                   o_shape=(k, n), o_spec=pl.BlockSpec((tko, n), lambda j, i, kk: (j, 0)),
                   acc_shape=(tko, n), nk=nk, k_axis=2)


def _sigmoid(x):
    return 1.0 / (1.0 + jnp.exp(-x))


def _rms_fwd(name, h, g, tr):
    t, d = h.shape

    def fn(hv, gv):
        r = lax.rsqrt(jnp.mean(hv * hv, axis=-1, keepdims=True) + RMS_EPS)
        return hv * r * gv

    return _call(fn, name, (t // tr,), [(h, (tr, d), lambda i: (i, 0)), (g, (1, d), lambda i: (0, 0))],
                 [((t, d), (tr, d), lambda i: (i, 0), False)])


def _rms_bwd(name, h, g, dhn, dh, tr):
    t, d = h.shape

    def fn(hv, gv, dy, dh_in):
        r = lax.rsqrt(jnp.mean(hv * hv, axis=-1, keepdims=True) + RMS_EPS)
        xh = hv * r
        dg = jnp.sum(dy * xh, axis=0, keepdims=True)
        dxh = dy * gv
        dx = r * (dxh - xh * jnp.mean(dxh * xh, axis=-1, keepdims=True))
        return dh_in + dx, dg

    row = lambda i: (i, 0)
    return _call(fn, name, (t // tr,),
                 [(h, (tr, d), row), (g, (1, d), lambda i: (0, 0)), (dhn, (tr, d), row), (dh, (tr, d), row)],
                 [((t, d), (tr, d), row, False), ((1, d), (1, d), lambda i: (0, 0), True)], acc_axis=0)


def _final_loss(name, h, g, tgt, tr):
    t, d = h.shape

    def fn(hv, gv, tv):
        r = lax.rsqrt(jnp.mean(hv * hv, axis=-1, keepdims=True) + RMS_EPS)
        xh = hv * r
        row = pl.program_id(0) * tr + lax.broadcasted_iota(jnp.int32, (tr, 1), 0)
        e = jnp.where(row >= N_META, xh * gv - tv, 0.0)
        loss = jnp.broadcast_to(0.5 * jnp.sum(jnp.sum(e * e, axis=-1, keepdims=True), axis=0, keepdims=True) / d,
                                (8, LANES))
        dout = e / d
        dg = jnp.sum(dout * xh, axis=0, keepdims=True)
        dxh = dout * gv
        dx = r * (dxh - xh * jnp.mean(dxh * xh, axis=-1, keepdims=True))
        return loss, dx, dg

    row = lambda i: (i, 0)
    fix = lambda i: (0, 0)
    return _call(fn, name, (t // tr,), [(h, (tr, d), row), (g, (1, d), fix), (tgt, (tr, d), row)],
                 [((8, LANES), (8, LANES), fix, True), ((t, d), (tr, d), row, False), ((1, d), (1, d), fix, True)],
                 acc_axis=0)


def _silu_ln(uc, g, b):
    mu = jnp.mean(uc, axis=-1, keepdims=True)
    xc = uc - mu
    rs = lax.rsqrt(jnp.mean(xc * xc, axis=-1, keepdims=True) + LN_EPS)
    ln = xc * rs * g + b
    return ln * _sigmoid(ln)


def _even_ln_fwd(name, uc, g, b, tr):
    t, d = uc.shape
    row, fix = (lambda i: (i, 0)), (lambda i: (0, 0))
    return _call(_silu_ln, name, (t // tr,), [(uc, (tr, d), row), (g, (1, d), fix), (b, (1, d), fix)],
                 [((t, d), (tr, d), row, False)])


def _even_ln_bwd(name, uc, g, b, dy, dy_col, tr):
    t, d = uc.shape

    def fn(ucv, gv, bv, dyv):
        mu = jnp.mean(ucv, axis=-1, keepdims=True)
        xc = ucv - mu
        rs = lax.rsqrt(jnp.mean(xc * xc, axis=-1, keepdims=True) + LN_EPS)
        xh = xc * rs
        ln = xh * gv + bv
        s = _sigmoid(ln)
        dln = dyv * (s * (1.0 + ln * (1.0 - s)))
        dg = jnp.sum(dln * xh, axis=0, keepdims=True)
        db = jnp.sum(dln, axis=0, keepdims=True)
        dxh = dln * gv
        duc = rs * (dxh - jnp.mean(dxh, axis=-1, keepdims=True) - xh * jnp.mean(dxh * xh, axis=-1, keepdims=True))
        return duc, dg, db

    row, fix = (lambda i: (i, 0)), (lambda i: (0, 0))
    return _call(fn, name, (t // tr,),
                 [(uc, (tr, d), row), (g, (1, d), fix), (b, (1, d), fix), (dy, (tr, d), lambda i: (i, dy_col))],
                 [((t, d), (tr, d), row, False), ((1, d), (1, d), fix, True), ((1, d), (1, d), fix, True)], acc_axis=0)


def _conv_fwd(xp, w_ref, width, t):
    acc = None
    for j in range(width):
        term = xp[pl.ds(CONV_PAD - (width - 1) + j, t), :] * w_ref[pl.ds(j, 1), :]
        acc = term if acc is None else acc + term
    return acc


def _conv_bwd_in(dyp, w_ref, width, t):
    acc = None
    for j in range(width):
        term = dyp[pl.ds(width - 1 - j, t), :] * w_ref[pl.ds(j, 1), :]
        acc = term if acc is None else acc + term
    return acc


def _conv_bwd_w(dy, xp, dw_ref, width, t):
    for j in range(width):
        dw_ref[pl.ds(j, 1), :] = jnp.sum(dy * xp[pl.ds(CONV_PAD - (width - 1) + j, t), :], axis=0, keepdims=True)


def _store_front(xp, x, t):
    xp[pl.ds(0, CONV_PAD), :] = jnp.zeros((CONV_PAD, LANES), F32)
    xp[pl.ds(CONV_PAD, t), :] = x


def _store_back(xp, x, t):
    xp[pl.ds(0, t), :] = x
    xp[pl.ds(t, CONV_PAD), :] = jnp.zeros((CONV_PAD, LANES), F32)


def _col_call(body, name, ncol, ins, outs, t, n_scratch):
    def spec(rows, off):
        return pl.BlockSpec((rows, LANES), lambda j, off=off: (0, j + off))

    res = pl.pallas_call(
        body, name=name, grid=(ncol,),
        in_specs=[spec(r, off) for _, r, off in ins],
        out_specs=[spec(r, 0) for r, _ in outs],
        out_shape=[jax.ShapeDtypeStruct((r, c), F32) for r, c in outs],
        scratch_shapes=[pltpu.VMEM((t + CONV_PAD, LANES), F32) for _ in range(n_scratch)],
        compiler_params=_cparams(None),
    )(*[a for a, _, _ in ins])
    return res


def _even_col_fwd(name, p, conv_a, conv_b):
    t = p.shape[0]
    nc = D_A // LANES

    def body(av, ag, gb, gc, xi, ca, cb, uc_ref, yb_ref, xp):
        _store_front(xp, av[...] * _sigmoid(ag[...]), t)
        uc_ref[...] = _conv_fwd(xp, ca, CONV_A_WIDTH, t)
        _store_front(xp, gc[...] * xi[...], t)
        yb_ref[...] = gb[...] * _conv_fwd(xp, cb, CONV_B_WIDTH, t)

    ins = [(p, t, k * nc) for k in range(5)] + [(conv_a, CONV_A_WIDTH, 0), (conv_b, CONV_B_WIDTH, 0)]
    return _col_call(body, name, nc, ins, [(t, D_A), (t, D_A)], t, 1)


def _even_col_bwd(name, p, duc, dy, conv_a, conv_b):
    t = p.shape[0]
    nc = D_A // LANES

    def body(av, ag, gb, gc, xi, duc_ref, dyb_ref, ca, cb, dav, dag, dgb, dgc, dxi, dca, dcb, xp, dyp):
        sig = _sigmoid(ag[...])
        _store_front(xp, av[...] * sig, t)
        _store_back(dyp, duc_ref[...], t)
        _conv_bwd_w(duc_ref[...], xp, dca, CONV_A_WIDTH, t)
        du = _conv_bwd_in(dyp, ca, CONV_A_WIDTH, t)
        dav[...] = du * sig
        dag[...] = du * av[...] * sig * (1.0 - sig)
        _store_front(xp, gc[...] * xi[...], t)
        zc = _conv_fwd(xp, cb, CONV_B_WIDTH, t)
        dgb[...] = dyb_ref[...] * zc
        dzc = dyb_ref[...] * gb[...]
        _conv_bwd_w(dzc, xp, dcb, CONV_B_WIDTH, t)
        _store_back(dyp, dzc, t)
        dz = _conv_bwd_in(dyp, cb, CONV_B_WIDTH, t)
        dgc[...] = dz * xi[...]
        dxi[...] = dz * gc[...]

    ins = ([(p, t, k * nc) for k in range(5)] + [(duc, t, 0), (dy, t, nc)]
           + [(conv_a, CONV_A_WIDTH, 0), (conv_b, CONV_B_WIDTH, 0)])
    outs = [(t, D_A)] * 5 + [(CONV_A_WIDTH, D_A), (CONV_B_WIDTH, D_A)]
    return _col_call(body, name, nc, ins, outs, t, 2)


def _ffn_col_fwd(name, u, conv, bias):
    t = u.shape[0]
    nc = D_FF // LANES

    def body(g_ref, v_ref, cw, b_ref, a_ref, xp):
        _store_front(xp, g_ref[...], t)
        gc = _conv_fwd(xp, cw, CONV_B_WIDTH, t) + b_ref[...]
        a_ref[...] = gc * _sigmoid(gc) * v_ref[...]

    ins = [(u, t, 0), (u, t, nc), (conv, CONV_B_WIDTH, 0), (bias, 1, 0)]
    return _col_call(body, name, nc, ins, [(t, D_FF)], t, 1)[0]


def _ffn_col_bwd(name, u, da, conv, bias):
    t = u.shape[0]
    nc = D_FF // LANES

    def body(g_ref, v_ref, da_ref, cw, b_ref, du_ref, dcw, db_ref, xp, dyp):
        _store_front(xp, g_ref[...], t)
        gc = _conv_fwd(xp, cw, CONV_B_WIDTH, t) + b_ref[...]
        s = _sigmoid(gc)

        @pl.when(pl.program_id(1) == 0)
        def _():
            dgc = da_ref[...] * v_ref[...] * (s * (1.0 + gc * (1.0 - s)))
            db_ref[...] = jnp.sum(dgc, axis=0, keepdims=True)
            _conv_bwd_w(dgc, xp, dcw, CONV_B_WIDTH, t)
            _store_back(dyp, dgc, t)
            du_ref[...] = _conv_bwd_in(dyp, cw, CONV_B_WIDTH, t)

        @pl.when(pl.program_id(1) == 1)
        def _():
            du_ref[...] = da_ref[...] * gc * s

    col = lambda rows, off: pl.BlockSpec((rows, LANES), lambda j, p: (0, j + off))
    return pl.pallas_call(
        body, name=name, grid=(nc, 2),
        in_specs=[col(t, 0), col(t, nc), col(t, 0), col(CONV_B_WIDTH, 0), col(1, 0)],
        out_specs=[pl.BlockSpec((t, LANES), lambda j, p: (0, j + nc * p)), col(CONV_B_WIDTH, 0), col(1, 0)],
        out_shape=[jax.ShapeDtypeStruct((t, 2 * D_FF), F32), jax.ShapeDtypeStruct((CONV_B_WIDTH, D_FF), F32),
                   jax.ShapeDtypeStruct((1, D_FF), F32)],
        scratch_shapes=[pltpu.VMEM((t + CONV_PAD, LANES), F32) for _ in range(2)],
        compiler_params=_cparams(None),
    )(u, u, da, conv, bias)


def _shift_fwd(name, p, col0, mu):
    t = p.shape[0]

    def body(x_ref, mu_ref, o_ref, xp):
        _store_front(xp, x_ref[...], t)
        prev = xp[pl.ds(CONV_PAD - 1, t), :]
        o_ref[...] = x_ref[...] + (prev - x_ref[...]) * mu_ref[...]

    return _col_call(body, name, RWKV_COLS // LANES, [(p, t, col0), (mu, 1, 0)], [(t, RWKV_COLS)], t, 1)[0]


def _shift_bwd(name, p, col0, mu, dprs):
    t = p.shape[0]

    def body(x_ref, mu_ref, d_ref, dx_ref, dmu_ref, xp, dyp):
        _store_front(xp, x_ref[...], t)
        prev = xp[pl.ds(CONV_PAD - 1, t), :]
        dmu_ref[...] = jnp.sum(d_ref[...] * (prev - x_ref[...]), axis=0, keepdims=True)
        dm = d_ref[...] * mu_ref[...]
        _store_back(dyp, dm, t)
        dx_ref[...] = d_ref[...] - dm + dyp[pl.ds(1, t), :]

    ins = [(p, t, col0), (mu, 1, 0), (dprs, t, 0)]
    return _col_call(body, name, RWKV_COLS // LANES, ins, [(t, RWKV_COLS), (1, RWKV_COLS)], t, 2)


def _hi_lo(x):
    hi = x.astype(BF16)
    return hi, (x - hi.astype(F32)).astype(BF16)


def _dot_passes(a, b, dims, passes):
    d = lambda p, q: lax.dot_general(p, q, dims, preferred_element_type=F32)
    if passes == 1:
        return d(a.astype(MXU_DTYPE), b.astype(MXU_DTYPE))
    ah, al = _hi_lo(a)
    bh, bl = _hi_lo(b)
    return d(ah, bh) + (d(ah, bl) + d(al, bh))


@functools.partial(jax.custom_vjp, nondiff_argnums=(2, 3))
def _dot_vjp(a, b, dims, passes):
    return _dot_passes(a, b, dims, passes)


def _dot_fwd(a, b, dims, passes):
    return _dot_passes(a, b, dims, passes), (a, b)


def _dot_bwd(dims, passes, res, g):
    a, b = res
    if dims == _NN:
        return _dot_passes(g, b, _NT, passes), _dot_passes(a, g, _TN, passes)
    if dims == _NT:
        return _dot_passes(g, b, _NN, passes), _dot_passes(g, a, _TN, passes)
    return _dot_passes(b, g, _NT, passes), _dot_passes(a, g, _NN, passes)


_dot_vjp.defvjp(_dot_fwd, _dot_bwd)


def _doth(a, b, dims=_NN):
    return _dot_vjp(a, b, dims, 3)


def _dotb(a, b, dims=_NN):
    return _dot_vjp(a, b, dims, 1)


def _softplus(x):
    return jnp.where(x > 0, x, 0.0) + jnp.log(1.0 + jnp.exp(jnp.where(x > 0, -x, x)))


def _rwkv_pre(k, xl, gd, w0, w2p, a0, a2p, g2, k_k, k_a, seg):
    z = w0 + _dotb(jnp.tanh(xl), w2p)
    lw = -jnp.exp(-_softplus(-z) - 0.5)
    alpha = _sigmoid(a0 + _dotb(xl, a2p))
    g = _dotb(_sigmoid(gd), g2)
    kk = k * k_k
    kk = kk / jnp.maximum(jnp.sqrt(_dotb(kk * kk, seg)), 1e-12)
    k2 = k * (1.0 + (alpha - 1.0) * k_a)
    return lw, k2, -kk, kk * alpha, g


def _rwkv_post(y, r, k2, v, g, lnx_g, lnx_b, r_k, seg):
    mean = _dotb(y, seg) * (1.0 / HEAD_DIM)
    yc = y - mean
    var = _dotb(yc * yc, seg) * (1.0 / HEAD_DIM)
    yo = yc * lax.rsqrt(var + RWKV_GN_EPS) * lnx_g + lnx_b
    bonus = _dotb(r * k2 * r_k, seg) * v
    return (yo + bonus) * g


def _rwkv_pre_fwd(name, prs, prm, seg, tr):
    t = prs.shape[0]
    row = lambda i: (i, 0)
    fix = lambda i: (0, 0)
    ins = [(prs, (tr, D_R), lambda i: (i, 1)), (prs, (tr, LANES), lambda i: (i, 12)), (prs, (tr, LANES), lambda i: (i, 13)),
           (prm["w0"], (1, D_R), fix), (prm["w2p"], (LANES, D_R), fix), (prm["a0"], (1, D_R), fix),
           (prm["a2p"], (LANES, D_R), fix), (prm["g2"], (LANES, D_R), fix), (prm["k_k"], (1, D_R), fix),
           (prm["k_a"], (1, D_R), fix), (seg, (D_R, D_R), fix)]
    return _call(_rwkv_pre, name, (t // tr,), ins, [((t, D_R), (tr, D_R), row, False)] * 5)


def _rwkv_pre_bwd(name, prs, prm, seg, cts, tr):
    t = prs.shape[0]

    def fn(k, xl, gd, w0, w2p, a0, a2p, g2, k_k, k_a, segv, *ct):
        _, vjp = jax.vjp(lambda *a: _rwkv_pre(*a, segv), k, xl, gd, w0, w2p, a0, a2p, g2, k_k, k_a)
        return vjp(tuple(ct))

    row = lambda i: (i, 0)
    fix = lambda i: (0, 0)
    ins = [(prs, (tr, D_R), lambda i: (i, 1)), (prs, (tr, LANES), lambda i: (i, 12)), (prs, (tr, LANES), lambda i: (i, 13)),
           (prm["w0"], (1, D_R), fix), (prm["w2p"], (LANES, D_R), fix), (prm["a0"], (1, D_R), fix),
           (prm["a2p"], (LANES, D_R), fix), (prm["g2"], (LANES, D_R), fix), (prm["k_k"], (1, D_R), fix),
           (prm["k_a"], (1, D_R), fix), (seg, (D_R, D_R), fix)] + [(c, (tr, D_R), row) for c in cts]
    outs = [((t, D_R), (tr, D_R), row, False), ((t, LANES), (tr, LANES), row, False), ((t, LANES), (tr, LANES), row, False),
            ((1, D_R), (1, D_R), fix, True), ((LANES, D_R), (LANES, D_R), fix, True), ((1, D_R), (1, D_R), fix, True),
            ((LANES, D_R), (LANES, D_R), fix, True), ((LANES, D_R), (LANES, D_R), fix, True),
            ((1, D_R), (1, D_R), fix, True), ((1, D_R), (1, D_R), fix, True)]
    return _call(fn, name, (t // tr,), ins, outs, acc_axis=0)


def _rwkv_post_ins(y, prs, k2, g, prm, seg, tr):
    row = lambda i: (i, 0)
    fix = lambda i: (0, 0)
    return [(y, (tr, D_R), row), (prs, (tr, D_R), row), (k2, (tr, D_R), row), (prs, (tr, D_R), lambda i: (i, 2)),
            (g, (tr, D_R), row), (prm["lnx_g"], (1, D_R), fix), (prm["lnx_b"], (1, D_R), fix), (prm["r_k"], (1, D_R), fix),
            (seg, (D_R, D_R), fix)]


def _rwkv_post_fwd(name, y, prs, k2, g, prm, seg, tr):
    t = y.shape[0]
    return _call(_rwkv_post, name, (t // tr,), _rwkv_post_ins(y, prs, k2, g, prm, seg, tr),
                 [((t, D_R), (tr, D_R), lambda i: (i, 0), False)])


def _rwkv_post_bwd(name, y, prs, k2, g, prm, seg, dy, dy_col, tr):
    t = y.shape[0]

    def fn(yv, r, k2v, v, gv, lg, lb, rk, segv, ct):
        _, vjp = jax.vjp(lambda *a: _rwkv_post(*a, segv), yv, r, k2v, v, gv, lg, lb, rk)
        return vjp(ct)

    row = lambda i: (i, 0)
    fix = lambda i: (0, 0)
    ins = _rwkv_post_ins(y, prs, k2, g, prm, seg, tr) + [(dy, (tr, D_R), lambda i: (i, dy_col))]
    outs = [((t, D_R), (tr, D_R), row, False)] * 5 + [((1, D_R), (1, D_R), fix, True)] * 3
    return _call(fn, name, (t // tr,), ins, outs, acc_axis=0)


def _wkv_chunk(s0, r, lw, k, v, a, b):
    c = r[0].shape[0]
    lane = lax.broadcasted_iota(jnp.int32, (1, 2 * HEAD_DIM), 1)
    first = (lane < HEAD_DIM).astype(F32)
    per_head = lambda x: jnp.concatenate([x * first, x * (1.0 - first)], axis=0)

    def time_of(shape, dim):
        i = lax.broadcasted_iota(jnp.int32, shape, dim)
        return jnp.where(i >= c, i - c, i)

    incl = (lax.broadcasted_iota(jnp.int32, (c, c), 0) >= lax.broadcasted_iota(jnp.int32, (c, c), 1)).astype(F32)
    strict2 = time_of((2 * c, 2 * c), 0) > time_of((2 * c, 2 * c), 1)
    incl2 = lax.broadcasted_iota(jnp.int32, (c, 2 * c), 0) >= time_of((c, 2 * c), 1)
    each = lambda f, *xs: [f(*x) for x in zip(*xs)]
    cum = each(lambda x: _doth(incl, x), lw)
    tot = each(lambda x: jnp.sum(x, axis=0, keepdims=True), lw)
    e_inv = each(lambda x: jnp.exp(-x), cum)
    a_st = each(lambda x, cm, l: per_head(x * jnp.exp(cm - l)), a, cum, lw)
    r_t = each(lambda x, cm: x * jnp.exp(cm), r, cum)
    b_st = each(lambda x, e: per_head(x * e), b, e_inv)
    k_st = each(lambda x, e: per_head(x * e), k, e_inv)
    v_st = each(per_head, v)
    m = each(lambda x, w: jnp.where(strict2, _dotb(x, w, _NT), 0.0), a_st, b_st)
    m_k = each(lambda x, w: jnp.where(strict2, _dotb(x, w, _NT), 0.0), a_st, k_st)
    u = each(lambda x, s, mk, w: _dotb(x, s, _NT) + _dotb(mk, w), a_st, s0, m_k, v_st)
    steps = (c - 1).bit_length()
    for s in range(steps):
        u = each(lambda x, w: x + _dotb(w, x), u, m)
        if s + 1 < steps:
            m = each(lambda w: _dotb(w, w), m)
    n_b = each(lambda x, w: jnp.where(incl2, _dotb(x, w, _NT), 0.0), r_t, b_st)
    n_k = each(lambda x, w: jnp.where(incl2, _dotb(x, w, _NT), 0.0), r_t, k_st)
    y = each(lambda x, s, nb, uu, nk, w: _dotb(x, s, _NT) + _dotb(nb, uu) + _dotb(nk, w), r_t, s0, n_b, u, n_k, v_st)
    dec = each(lambda tt, cm: jnp.exp(tt - cm), tot, cum)
    s1 = each(lambda s, tt, uu, x, d, w, kk: s * jnp.exp(tt) + _dotb(uu, per_head(x * d), _TN) + _dotb(w, per_head(kk * d), _TN),
              s0, tot, u, b, dec, v_st, k)
    return tuple(y), tuple(s1)


WKV_PAIRS_PER_STEP = 4
PAIR = 2 * HEAD_DIM


def _wkv_fwd(name, srcs):
    t = srcs[0][0].shape[0]
    c = _chunk_len(t)
    nc = t // c
    pp = WKV_PAIRS_PER_STEP
    n_pairs = D_R // PAIR

    def body(r, lw, k, v, a, b, y_ref, st_ref, state):
        @pl.when(pl.program_id(1) == 0)
        def _():
            state[...] = jnp.zeros(state.shape, F32)

        pairs = lambda ref: tuple(ref[:, pl.ds(i * PAIR, PAIR)] for i in range(pp))
        s0 = tuple(state[i] for i in range(pp))
        y, s1 = _wkv_chunk(s0, pairs(r), pairs(lw), pairs(k), pairs(v), pairs(a), pairs(b))
        for i in range(pp):
            st_ref[i] = s0[i]
            y_ref[:, pl.ds(i * PAIR, PAIR)] = y[i]
            state[i] = s1[i]

    seq = lambda off: pl.BlockSpec((c, pp * PAIR), lambda g, j: (j, off + g))
    return pl.pallas_call(
        body, name=name, grid=(n_pairs // pp, nc), in_specs=[seq(off) for _, off in srcs],
        out_specs=[seq(0), pl.BlockSpec((pp, None, PAIR, PAIR), lambda g, j: (g, j, 0, 0))],
        out_shape=[jax.ShapeDtypeStruct((t, D_R), F32), jax.ShapeDtypeStruct((n_pairs, nc, PAIR, PAIR), F32)],
        scratch_shapes=[pltpu.VMEM((pp, PAIR, PAIR), F32)],
        compiler_params=_cparams(None),
    )(*[a for a, _ in srcs])


def _wkv_bwd(name, srcs, st, dy):
    t = srcs[0][0].shape[0]
    c = _chunk_len(t)
    nc = t // c
    pp = WKV_PAIRS_PER_STEP
    n_pairs = D_R // PAIR

    def body(r, lw, k, v, a, b, st_ref, dy_ref, dr, dlw, dk, dv, da, db, dstate):
        @pl.when(pl.program_id(1) == 0)
        def _():
            dstate[...] = jnp.zeros(dstate.shape, F32)

        half = lax.broadcasted_iota(jnp.int32, (PAIR, PAIR), 0) < HEAD_DIM
        same_head = half == (lax.broadcasted_iota(jnp.int32, (PAIR, PAIR), 1) < HEAD_DIM)
        pairs = lambda ref: tuple(ref[:, pl.ds(i * PAIR, PAIR)] for i in range(pp))
        s0 = tuple(st_ref[i] for i in range(pp))
        _, vjp = jax.vjp(_wkv_chunk, s0, pairs(r), pairs(lw), pairs(k), pairs(v), pairs(a), pairs(b))
        ds0, *dxs = vjp((pairs(dy_ref), tuple(dstate[i] for i in range(pp))))
        for i in range(pp):
            for ref, val in zip((dr, dlw, dk, dv, da, db), dxs):
                ref[:, pl.ds(i * PAIR, PAIR)] = val[i]
            dstate[i] = jnp.where(same_head, ds0[i], 0.0)

    seq = lambda off: pl.BlockSpec((c, pp * PAIR), lambda g, j: (nc - 1 - j, off + g))
    return pl.pallas_call(
        body, name=name, grid=(n_pairs // pp, nc),
        in_specs=[seq(off) for _, off in srcs]
        + [pl.BlockSpec((pp, None, PAIR, PAIR), lambda g, j: (g, nc - 1 - j, 0, 0)), seq(dy[1])],
        out_specs=[seq(0)] * 6,
        out_shape=[jax.ShapeDtypeStruct((t, D_R), F32)] * 6,
        scratch_shapes=[pltpu.VMEM((pp, PAIR, PAIR), F32)],
        compiler_params=_cparams(None),
    )(*[a for a, _ in srcs], st, dy[0])


def _rope(x, cos, sin, rot):
    return x * cos + _dotb(x, rot) * sin


def _attn_block(nb, q, kp, kc, km, vp, vc, vm, sk, cq, sq, cp, sp, cm, sm, rot):
    g = GQA_GROUP
    scale = HEAD_DIM ** -0.5
    down = lambda x: jnp.concatenate([x] * g, axis=0)
    kpr = _rope(kp, cp, sp, rot)
    kcr = _rope(kc, cq, sq, rot)
    kmr = _rope(km, cm, sm, rot)
    qr = _rope(q, down(cq), down(sq), rot)
    i = lax.broadcasted_iota(jnp.int32, (g * BLOCK, BLOCK), 0)
    i = i - BLOCK * ((i >= BLOCK).astype(jnp.int32) + (i >= 2 * BLOCK).astype(jnp.int32) + (i >= 3 * BLOCK).astype(jnp.int32))
    j = lax.broadcasted_iota(jnp.int32, (g * BLOCK, BLOCK), 1)
    nbv = jnp.zeros((g * BLOCK, BLOCK), jnp.int32) + nb
    ok_p = (j > i) & (nbv >= 2)
    ok_c = (j <= i) & (nbv >= 1)
    ok_m = (j >= BLOCK - N_META) & ((nbv >= 1) | (j <= i))
    sink = jnp.concatenate([jnp.broadcast_to(s, (BLOCK, 1)) for s in sk], axis=0)
    s_p = jnp.where(ok_p, _dotb(qr, kpr, _NT) * scale, NEG_INF)
    s_c = jnp.where(ok_c, _dotb(qr, kcr, _NT) * scale, NEG_INF)
    s_m = jnp.where(ok_m, _dotb(qr, kmr, _NT) * scale, NEG_INF)
    rmax = lambda s: jnp.max(s, axis=-1, keepdims=True)
    m = lax.stop_gradient(jnp.maximum(jnp.maximum(rmax(s_p), rmax(s_c)), jnp.maximum(rmax(s_m), sink)))
    e_p, e_c, e_m = jnp.exp(s_p - m), jnp.exp(s_c - m), jnp.exp(s_m - m)
    rsum = lambda e: jnp.sum(e, axis=-1, keepdims=True)
    inv = 1.0 / (rsum(e_p) + rsum(e_c) + rsum(e_m) + jnp.exp(sink - m))
    return _dotb(e_p * inv, vp) + _dotb(e_c * inv, vc) + _dotb(e_m * inv, vm)


def _attn_specs():
    cur = lambda g, n: (g, n, 0)
    prev = lambda g, n: (g, jnp.maximum(n - 1, 0), 0)
    meta = lambda g, n: (g, 0, 0)
    kv = lambda m: pl.BlockSpec((None, BLOCK, HEAD_DIM), m)
    tab = lambda m: pl.BlockSpec((BLOCK, HEAD_DIM), m)
    tcur, tprev, tmeta = (lambda g, n: (n, 0)), (lambda g, n: (jnp.maximum(n - 1, 0), 0)), (lambda g, n: (0, 0))
    qspec = pl.BlockSpec((GQA_GROUP, BLOCK, HEAD_DIM), cur)
    sspec = pl.BlockSpec((GQA_GROUP, 8, LANES), meta)
    specs = [qspec, kv(prev), kv(cur), kv(meta), kv(prev), kv(cur), kv(meta), sspec,
             tab(tcur), tab(tcur), tab(tprev), tab(tprev), tab(tmeta), tab(tmeta),
             pl.BlockSpec((HEAD_DIM, HEAD_DIM), lambda g, n: (0, 0))]
    return specs, qspec, sspec, kv


def _attn_args(q, k, v, sinks_b, cos, sin, rot):
    return (q, k, k, k, v, v, v, sinks_b, cos, sin, cos, sin, cos, sin, rot)


def _attn_fwd(name, q, k, v, sinks_b, cos, sin, rot):
    tp = q.shape[1]
    specs, qspec, _, _ = _attn_specs()

    def body(q_ref, kp, kc, km, vp, vc, vm, s_ref, cq, sq, cp, sp, cm, sm, rot_ref, o_ref):
        q = jnp.concatenate([q_ref[h] for h in range(GQA_GROUP)], axis=0)
        sk = tuple(s_ref[h][0:1, 0:1] for h in range(GQA_GROUP))
        out = _attn_block(pl.program_id(1), q, kp[...], kc[...], km[...], vp[...], vc[...], vm[...], sk,
                          cq[...], sq[...], cp[...], sp[...], cm[...], sm[...], rot_ref[...])
        for h in range(GQA_GROUP):
            o_ref[h] = out[h * BLOCK:(h + 1) * BLOCK]

    return pl.pallas_call(
        body, name=name, grid=(N_KV_HEADS, tp // BLOCK), in_specs=specs, out_specs=qspec,
        out_shape=jax.ShapeDtypeStruct(q.shape, F32), compiler_params=_cparams(None),
    )(*_attn_args(q, k, v, sinks_b, cos, sin, rot))


def _attn_bwd(name, q, k, v, sinks_b, cos, sin, rot, do):
    tp = q.shape[1]
    nb = tp // BLOCK
    specs, qspec, sspec, kv = _attn_specs()

    def body(q_ref, kp, kc, km, vp, vc, vm, s_ref, cq, sq, cp, sp, cm, sm, rot_ref, do_ref,
             dq_ref, dkp, dkc, dvp, dvc, dkm, dvm, ds_ref):
        n = pl.program_id(1)
        q = jnp.concatenate([q_ref[h] for h in range(GQA_GROUP)], axis=0)
        sk = tuple(s_ref[h][0:1, 0:1] for h in range(GQA_GROUP))
        tabs = (cq[...], sq[...], cp[...], sp[...], cm[...], sm[...], rot_ref[...])
        _, vjp = jax.vjp(lambda *a: _attn_block(n, *a, *tabs), q, kp[...], kc[...], km[...], vp[...], vc[...], vm[...], sk)
        dq, gkp, gkc, gkm, gvp, gvc, gvm, dsk = vjp(jnp.concatenate([do_ref[h] for h in range(GQA_GROUP)], axis=0))
        dkp[...] = gkp
        dkc[...] = gkc
        dvp[...] = gvp
        dvc[...] = gvc
        for h in range(GQA_GROUP):
            dq_ref[h] = dq[h * BLOCK:(h + 1) * BLOCK]

        @pl.when(n == 0)
        def _():
            dkm[...] = gkm
            dvm[...] = gvm
            for h in range(GQA_GROUP):
                ds_ref[h] = jnp.broadcast_to(dsk[h], (8, LANES))

        @pl.when(n != 0)
        def _():
            dkm[...] += gkm
            dvm[...] += gvm
            for h in range(GQA_GROUP):
                ds_ref[h] += jnp.broadcast_to(dsk[h], (8, LANES))

    part = pl.BlockSpec((None, None, BLOCK, HEAD_DIM), lambda g, n: (g, n, 0, 0))
    part_shape = jax.ShapeDtypeStruct((N_KV_HEADS, nb, BLOCK, HEAD_DIM), F32)
    meta_shape = jax.ShapeDtypeStruct((N_KV_HEADS, BLOCK, HEAD_DIM), F32)
    return pl.pallas_call(
        body, name=name, grid=(N_KV_HEADS, nb), in_specs=specs + [qspec],
        out_specs=[qspec, part, part, part, part, kv(lambda g, n: (g, 0, 0)), kv(lambda g, n: (g, 0, 0)), sspec],
        out_shape=[jax.ShapeDtypeStruct(q.shape, F32), part_shape, part_shape, part_shape, part_shape,
                   meta_shape, meta_shape, jax.ShapeDtypeStruct(sinks_b.shape, F32)],
        compiler_params=_cparams(None),
    )(*_attn_args(q, k, v, sinks_b, cos, sin, rot), do)


def _kv_combine(name, prev_part, own_part, meta):
    g, nb = own_part.shape[:2]

    def fn(own, nxt, mt):
        m = pl.program_id(1)
        one = jnp.ones((BLOCK, HEAD_DIM), F32)
        use_next = jnp.where(one * m < nb - 1, 1.0, 0.0)
        use_meta = jnp.where(one * m < 1, 1.0, 0.0)
        return own + nxt * use_next + mt * use_meta

    blk = (None, None, BLOCK, HEAD_DIM)
    return _call(fn, name, (g, nb),
                 [(own_part, blk, lambda a, m: (a, m, 0, 0)),
                  (prev_part, blk, lambda a, m: (a, jnp.minimum(m + 1, nb - 1), 0, 0)),
                  (meta, (None, BLOCK, HEAD_DIM), lambda a, m: (a, 0, 0))],
                 [((g, nb * BLOCK, HEAD_DIM), (None, BLOCK, HEAD_DIM), lambda a, m: (a, m, 0), False)])


PACK_W = 1024
ELEMENTWISE_BLOCK_BYTES = 1 << 21


def _rows_tile(rows, cols):
    cap = max(8, ELEMENTWISE_BLOCK_BYTES // (4 * cols))
    for d in range(min(rows, cap), 0, -1):
        if rows % d == 0 and d % 8 == 0:
            return d
    return rows


def _adamw(name, w, g, m, v):
    rows, cols = w.shape
    tr = _rows_tile(rows, cols)

    def fn(wv, gv, mv, vv):
        m1 = ADAM_B1 * mv + (1.0 - ADAM_B1) * gv
        v1 = ADAM_B2 * vv + (1.0 - ADAM_B2) * (gv * gv)
        m_hat = m1 / (1.0 - ADAM_B1 ** ADAM_STEP)
        v_hat = v1 / (1.0 - ADAM_B2 ** ADAM_STEP)
        return -ADAM_LR * (m_hat / (jnp.sqrt(v_hat) + ADAM_EPS) + ADAM_WD * wv), m1, v1

    blk = (tr, cols)
    row = lambda i: (i, 0)
    return _call(fn, name, (rows // tr,), [(a, blk, row) for a in (w, g, m, v)], [((rows, cols), blk, row, False)] * 3)


def _pair_add(name, g, recv, c_idx, out_dtype):
    s, a, b = g.shape
    half = a // 2

    def body(c_ref, a_ref, b_ref, o_ref):
        o_ref[...] = (a_ref[...] + b_ref[...]).astype(out_dtype)

    blk = (None, half, b)
    return pl.pallas_call(
        body, name=name,
        grid_spec=pltpu.PrefetchScalarGridSpec(
            num_scalar_prefetch=1, grid=(s,),
            in_specs=[pl.BlockSpec(blk, lambda j, c: (j, c[0], 0)), pl.BlockSpec(blk, lambda j, c: (j, 0, 0))],
            out_specs=pl.BlockSpec(blk, lambda j, c: (j, 0, 0))),
        out_shape=jax.ShapeDtypeStruct((s, half, b), out_dtype), compiler_params=_cparams(None),
    )(c_idx, g, recv)


def _pair_add_placed(name, g, recv, cm_idx, out_dtype):
    s, a, b = g.shape
    half = a // 2

    def body(cm_ref, a_ref, b_ref, o_ref, own_ref):
        val = (a_ref[...] + b_ref[...]).astype(out_dtype)
        o_ref[...] = val

        @pl.when(pl.program_id(0) == cm_ref[1])
        def _():
            own_ref[...] = val

    blk = (None, half, b)
    shape = jax.ShapeDtypeStruct((s, half, b), out_dtype)
    return pl.pallas_call(
        body, name=name,
        grid_spec=pltpu.PrefetchScalarGridSpec(
            num_scalar_prefetch=1, grid=(s,),
            in_specs=[pl.BlockSpec(blk, lambda j, cm: (j, cm[0], 0)), pl.BlockSpec(blk, lambda j, cm: (j, 0, 0))],
            out_specs=[pl.BlockSpec(blk, lambda j, cm: (j, 0, 0)), pl.BlockSpec(blk, lambda j, cm: (cm[1], 0, 0))]),
        out_shape=[shape, shape], compiler_params=_cparams(None),
    )(cm_idx, g, recv)


def _sum_chips(name, parts, c_idx, layer, n_layers, into=None):
    _, a, b = parts.shape
    tr = _rows_tile(a, b)

    def body(c_ref, p0, p1, p2, p3, *rest):
        o_ref = rest[-1]
        up = lambda p: p[...].astype(F32)
        o_ref[...] = ((up(p0) + up(p1)) + up(p2)) + up(p3)

    in_specs = [pl.BlockSpec((None, tr, b), lambda i, c, k=k: (k, i, 0)) for k in range(N_CHIPS)]
    args = [c_idx] + [parts] * N_CHIPS
    aliases = {}
    if into is not None:
        in_specs.append(_ANY)
        args.append(into)
        aliases = {1 + N_CHIPS: 0}
    return pl.pallas_call(
        body, name=name,
        grid_spec=pltpu.PrefetchScalarGridSpec(
            num_scalar_prefetch=1, grid=(a // tr,), in_specs=in_specs,
            out_specs=pl.BlockSpec((None, None, tr, b), lambda i, c: (layer, c[0], i, 0))),
        out_shape=jax.ShapeDtypeStruct((n_layers, 2, a, b), F32), input_output_aliases=aliases,
        compiler_params=_cparams(None),
    )(*args)


def _place_own_block(name, w, me_idx, half_major, dtype):
    _, a, b = w.shape
    tr = _rows_tile(a, b)
    if half_major:
        shape, index = (2, N_CHIPS, a, b), (lambda h, i, me: (h, me[0], i, 0))
    else:
        shape, index = (N_CHIPS, 2, a, b), (lambda h, i, me: (me[0], h, i, 0))

    def body(me_ref, w_ref, o_ref):
        o_ref[...] = w_ref[...].astype(dtype)

    return pl.pallas_call(
        body, name=name,
        grid_spec=pltpu.PrefetchScalarGridSpec(
            num_scalar_prefetch=1, grid=(2, a // tr),
            in_specs=[pl.BlockSpec((None, tr, b), lambda h, i, me: (h, i, 0))],
            out_specs=pl.BlockSpec((None, None, tr, b), index)),
        out_shape=jax.ShapeDtypeStruct(shape, dtype), compiler_params=_cparams(None),
    )(me_idx, w)


def _mesh_pos():
    return lax.axis_index("x"), lax.axis_index("y"), lax.axis_index("c")


def _other_chips(x, y):
    return [(1 - x, y), (x, 1 - y), (1 - x, 1 - y)]


_ANY = pl.BlockSpec(memory_space=pl.ANY)


def _gather_weights(name, bufs, half_major):
    n = len(bufs)

    def body(*refs):
        out_refs = refs[n:2 * n]
        send_sems, recv_sems = refs[2 * n:]
        x, y, c = _mesh_pos()
        me = 2 * x + y
        sibling = (x, y, 1 - c)
        chips = _other_chips(x, y)

        def place(i, chip_idx, half):
            return out_refs[i].at[half, chip_idx] if half_major[i] else out_refs[i].at[chip_idx, half]

        def copy(i, k, chip_idx, half, to):
            return pltpu.make_async_remote_copy(src_ref=place(i, chip_idx, half), dst_ref=place(i, chip_idx, half),
                                                send_sem=send_sems.at[6 * i + k], recv_sem=recv_sems.at[6 * i + k],
                                                device_id=to, device_id_type=MESH)

        first = [copy(i, j, me, c, (*chip, c)) for i in range(n) for j, chip in enumerate(chips)]
        for cp in first:
            cp.start()
        passed = []
        for i in range(n):
            for j, (cx, cy) in enumerate(chips):
                idx = 2 * cx + cy
                copy(i, j, idx, c, sibling).wait_recv()
                fwd = copy(i, 3 + j, idx, c, sibling)
                fwd.start()
                passed.append(fwd)
        for i in range(n):
            for j, (cx, cy) in enumerate(chips):
                copy(i, 3 + j, 2 * cx + cy, 1 - c, sibling).wait_recv()
        for cp in first + passed:
            cp.wait_send()

    return pl.pallas_call(
        body, name=name, in_specs=[_ANY] * n, out_specs=[_ANY] * n,
        out_shape=[jax.ShapeDtypeStruct(b.shape, b.dtype) for b in bufs],
        input_output_aliases={i: i for i in range(n)},
        scratch_shapes=[pltpu.SemaphoreType.DMA((6 * n,)), pltpu.SemaphoreType.DMA((6 * n,))],
        compiler_params=pltpu.CompilerParams(has_side_effects=True),
    )(*bufs)


def _halves_to_sibling(name, units):
    n = len(units)

    def body(*refs):
        g_refs, out_refs = refs[:n], refs[n:2 * n]
        send_sems, recv_sems = refs[2 * n:]
        x, y, c = _mesh_pos()
        cps = []
        for i in range(n):
            half = units[i].shape[1] // 2
            src = g_refs[i].at[pl.ds(0, N_CHIPS), pl.ds((1 - c) * half, half)]
            cp = pltpu.make_async_remote_copy(src_ref=src, dst_ref=out_refs[i], send_sem=send_sems.at[i],
                                              recv_sem=recv_sems.at[i], device_id=(x, y, 1 - c), device_id_type=MESH)
            cp.start()
            cps.append(cp)
        for cp in cps:
            cp.wait()

    return pl.pallas_call(
        body, name=name, in_specs=[_ANY] * n, out_specs=[_ANY] * n,
        out_shape=[jax.ShapeDtypeStruct((u.shape[0], u.shape[1] // 2, u.shape[2]), u.dtype) for u in units],
        scratch_shapes=[pltpu.SemaphoreType.DMA((n,)), pltpu.SemaphoreType.DMA((n,))],
        compiler_params=pltpu.CompilerParams(has_side_effects=True),
    )(*units)


def _scatter_to_chips(name, sums):
    n = len(sums)

    def body(*refs):
        h_refs, out_refs = refs[:n], refs[n:2 * n]
        send_sems, recv_sems, local_sems = refs[2 * n:]
        x, y, c = _mesh_pos()
        me = 2 * x + y
        chips = _other_chips(x, y)
        local = [pltpu.make_async_copy(h_refs[i].at[me], out_refs[i].at[me], local_sems.at[i]) for i in range(n)]
        for cp in local:
            cp.start()

        def copy(i, j, src_idx, dst_idx):
            cx, cy = chips[j]
            return pltpu.make_async_remote_copy(src_ref=h_refs[i].at[src_idx], dst_ref=out_refs[i].at[dst_idx],
                                                send_sem=send_sems.at[3 * i + j], recv_sem=recv_sems.at[3 * i + j],
                                                device_id=(cx, cy, c), device_id_type=MESH)

        cps = [copy(i, j, 2 * chips[j][0] + chips[j][1], me) for i in range(n) for j in range(3)]
        for cp in cps:
            cp.start()
        for i in range(n):
            for j in range(3):
                copy(i, j, me, 2 * chips[j][0] + chips[j][1]).wait_recv()
        for cp in cps:
            cp.wait_send()
        for cp in local:
            cp.wait()

    return pl.pallas_call(
        body, name=name, in_specs=[_ANY] * n, out_specs=[_ANY] * n,
        out_shape=[jax.ShapeDtypeStruct(s.shape, s.dtype) for s in sums],
        scratch_shapes=[pltpu.SemaphoreType.DMA((3 * n,)), pltpu.SemaphoreType.DMA((3 * n,)), pltpu.SemaphoreType.DMA((n,))],
        compiler_params=pltpu.CompilerParams(has_side_effects=True),
    )(*sums)


_HBM = pl.BlockSpec(memory_space=pltpu.HBM)
_SEM = pl.BlockSpec(memory_space=pltpu.SEMAPHORE)
_DATAFLOW = pltpu.SideEffectType.DATAFLOW_SIDE_EFFECTING


def _scatter_start(name, sums, zones):
    n = len(sums)

    def body(*refs):
        h_refs, z_refs = refs[:n], refs[n:2 * n]
        send_sems, recv_sems = refs[2 * n], refs[2 * n + 1]
        token = refs[-1]
        x, y, c = _mesh_pos()
        me = 2 * x + y
        for i in range(n):
            for j, (cx, cy) in enumerate(_other_chips(x, y)):
                pltpu.make_async_remote_copy(src_ref=h_refs[i].at[2 * cx + cy], dst_ref=z_refs[i].at[me],
                                             send_sem=send_sems.at[3 * i + j], recv_sem=recv_sems.at[3 * i + j],
                                             device_id=(cx, cy, c), device_id_type=MESH).start()
        token[...] = jnp.zeros(token.shape, F32)

    hbm = lambda a: pltpu.HBM(a.shape, a.dtype)
    res = pl.pallas_call(
        body, name=name,
        out_shape=(pltpu.SemaphoreType.DMA((3 * n,)), pltpu.SemaphoreType.DMA((3 * n,)),
                   *[hbm(a) for a in sums], *[hbm(a) for a in zones], jax.ShapeDtypeStruct((8, LANES), F32)),
        in_specs=[_HBM] * (2 * n),
        out_specs=(_SEM, _SEM, *[_HBM] * (2 * n), pl.BlockSpec(memory_space=pltpu.VMEM)),
        input_output_aliases={i: 2 + i for i in range(2 * n)},
        compiler_params=pltpu.CompilerParams(has_side_effects=_DATAFLOW),
    )(*[pltpu.with_memory_space_constraint(a, pltpu.HBM) for a in list(sums) + list(zones)])
    return res[0], res[1], res[2:2 + n], res[2 + n:2 + 2 * n], res[-1]


def _scatter_wait(name, send_sems, recv_sems, sums, zones, after):
    n = len(sums)

    def body(*refs):
        h_refs, z_refs = refs[:n], refs[n:2 * n]
        s_sems, r_sems = refs[2 * n], refs[2 * n + 1]
        x, y, c = _mesh_pos()
        me = 2 * x + y
        for i in range(n):
            for j, (cx, cy) in enumerate(_other_chips(x, y)):
                idx = 2 * cx + cy
                copy = pltpu.make_async_remote_copy(src_ref=h_refs[i].at[idx], dst_ref=z_refs[i].at[idx],
                                                    send_sem=s_sems.at[3 * i + j], recv_sem=r_sems.at[3 * i + j],
                                                    device_id=(cx, cy, c), device_id_type=MESH)
                copy.wait_send()
                copy.wait_recv()

    hbm = lambda a: pltpu.HBM(a.shape, a.dtype)
    res = pl.pallas_call(
        body, name=name,
        out_shape=(*[hbm(a) for a in sums], *[hbm(a) for a in zones]),
        in_specs=[_HBM] * (2 * n) + [_SEM, _SEM, _ANY],
        out_specs=tuple([_HBM] * (2 * n)),
        input_output_aliases={i: i for i in range(2 * n)},
        compiler_params=pltpu.CompilerParams(has_side_effects=_DATAFLOW),
    )(*sums, *zones, send_sems, recv_sems, after)
    return res[n:]


def _join_halves(name, results):
    n = len(results)
    pieces = [(i, l) for i in range(n) for l in range(results[i].shape[0])]

    def body(*refs):
        out_refs = refs[n:2 * n]
        send_sems, recv_sems = refs[2 * n:]
        x, y, c = _mesh_pos()

        def copy(k, half):
            i, l = pieces[k]
            return pltpu.make_async_remote_copy(src_ref=out_refs[i].at[l, half], dst_ref=out_refs[i].at[l, half],
                                                send_sem=send_sems.at[k], recv_sem=recv_sems.at[k],
                                                device_id=(x, y, 1 - c), device_id_type=MESH)

        cps = [copy(k, c) for k in range(len(pieces))]
        for cp in cps:
            cp.start()
        for k in range(len(pieces)):
            copy(k, 1 - c).wait_recv()
        for cp in cps:
            cp.wait_send()

    return pl.pallas_call(
        body, name=name, in_specs=[_ANY] * n, out_specs=[_ANY] * n,
        out_shape=[jax.ShapeDtypeStruct(r.shape, r.dtype) for r in results],
        input_output_aliases={i: i for i in range(n)},
        scratch_shapes=[pltpu.SemaphoreType.DMA((len(pieces),)), pltpu.SemaphoreType.DMA((len(pieces),))],
        compiler_params=pltpu.CompilerParams(has_side_effects=True),
    )(*results)


def _pack(arrays, dtype, rows_multiple):
    flat = jnp.concatenate([a.reshape(-1).astype(dtype) for a in arrays])
    unit = rows_multiple * PACK_W
    total = -(-flat.shape[0] // unit) * unit
    return jnp.pad(flat, (0, total - flat.shape[0])).reshape(total // PACK_W, PACK_W)


def _unpack(flat, shapes):
    out, off = [], 0
    for s in shapes:
        n = 1
        for d in s:
            n *= d
        out.append(flat[..., off:off + n].reshape(flat.shape[:-1] + tuple(s)))
        off += n
    return out


def _ffn_fwd(tag, l, h, g, w_up, conv, bias, w_down, tm):
    hn = _rms_fwd(f"{tag}_norm", h, g, tm)
    u = _mm_cs(f"{tag}_up", hn, w_up, l, tm)
    act = _ffn_col_fwd(f"{tag}_glu", u, conv, bias)
    h_out = _mm_full(f"{tag}_down", act, w_down, l, tm, D_FF // 2, add=h)
    return h_out, (hn, u, act)


def _ffn_bwd(tag, l, h, g, w_up, conv, bias, w_down, saved, dh, tm):
    hn, u, act = saved
    da = _mm_nt_full(f"{tag}_down_dx", dh, w_down, l, tm, D_FF // 2)
    dw_down = _mm_tn_full(f"{tag}_down_dw", act, dh, tm, D_FF // 2)
    du, dconv, dbias = _ffn_col_bwd(f"{tag}_glu_bwd", u, da, conv, bias)
    dw_up = _mm_tn_cs(f"{tag}_up_dw", hn, du, N_CHIPS, tm)
    dhn = _mm_nt_cs(f"{tag}_up_dx", du, w_up, l, tm)
    dh, dg = _rms_bwd(f"{tag}_norm_bwd", h, g, dhn, dh, tm)
    return dh, dict(norm=dg, w_up=dw_up, conv=dconv, bias=dbias, w_down=dw_down)


def _to_heads(z, nh, pad):
    t = z.shape[0]
    return jnp.pad(z.reshape(t, nh, HEAD_DIM).transpose(1, 0, 2), ((0, 0), (pad, 0), (0, 0)))


def _from_heads(z, pad):
    nh, tp, _ = z.shape
    return z[:, pad:].transpose(1, 0, 2).reshape(tp - pad, nh * HEAD_DIM)


def _rope_tables(tp, pad):
    half = HEAD_DIM // 2
    inv = ROPE_THETA ** (-jnp.arange(half, dtype=F32) / half)
    ang = (jnp.arange(tp, dtype=F32) - pad)[:, None] * inv[None, :]
    cos, sin = jnp.cos(ang), jnp.sin(ang)
    rot = jnp.zeros((HEAD_DIM, HEAD_DIM), F32)
    idx = jnp.arange(half)
    rot = rot.at[idx + half, idx].set(-1.0).at[idx, idx + half].set(1.0)
    return jnp.concatenate([cos, cos], axis=1), jnp.concatenate([sin, sin], axis=1), rot


def _local_step(x, tgt, w, on_grads=None):
    emit = on_grads if on_grads is not None else (lambda tag, units: 0.0)
    seq = x.shape[0]
    t = seq + N_META
    tm = _row_tile(t, 704)
    tr = _row_tile(t, 352)
    pad = BLOCK - N_META
    grads = {}

    h0 = jnp.concatenate([w["meta_tokens"], x], axis=0)
    tgt_p = jnp.pad(tgt, ((N_META, 0), (0, 0)))

    hn0 = _rms_fwd("l0_norm", h0, w["norm_mix"][0:1], tm)
    p0 = _mm_cs("l0_in", hn0, w["ev_w_in"], 0, tm)
    uc, yb = _even_col_fwd("l0_convs", p0, w["ev_conv_a"], w["ev_conv_b"])
    ya = _even_ln_fwd("l0_ln", uc, w["ev_ln_a_g"], w["ev_ln_a_b"], tm)
    y0 = jnp.concatenate([ya, yb], axis=1)
    h1 = _mm_full("l0_out", y0, w["ev_w_out"], 0, tm, D_MODEL, add=h0)
    f0 = (0, h1, w["norm_ffn"][0:1], w["ff_w_up"], w["ff_conv"][0], w["ff_conv_b"][0:1], w["ff_w_down"])
    h2, ffn0 = _ffn_fwd("f0", *f0, tm)

    hn2 = _rms_fwd("l1_norm", h2, w["norm_mix"][1:2], tm)
    p1 = _mm_cs("l1_in", hn2, w["od_w_in"], 0, tm)
    cos, sin, rot = _rope_tables(t + pad, pad)
    qh = _to_heads(p1[:, :D_ATT], N_Q_HEADS, pad)
    kh = _to_heads(p1[:, D_ATT:D_ATT + D_KV], N_KV_HEADS, pad)
    vh = _to_heads(p1[:, D_ATT + D_KV:D_ATT + 2 * D_KV], N_KV_HEADS, pad)
    sinks_b = jnp.broadcast_to(w["od_sinks"].reshape(N_Q_HEADS, 1, 1), (N_Q_HEADS, 8, LANES))
    y_att = _from_heads(_attn_fwd("l1_attn", qh, kh, vh, sinks_b, cos, sin, rot), pad)

    col0 = (D_ATT + 2 * D_KV) // LANES
    ch = jnp.arange(D_R) // HEAD_DIM
    seg = (ch[:, None] == ch[None, :]).astype(F32)
    prm = dict(w0=w["od_w0"], a0=w["od_a0"], g2=w["od_g2"], k_k=w["od_k_k"], k_a=w["od_k_a"],
               lnx_g=w["od_lnx_g"], lnx_b=w["od_lnx_b"], r_k=w["od_r_k"].reshape(1, D_R),
               w2p=jnp.concatenate([w["od_w2"], jnp.zeros((LORA_A, D_R), F32)], axis=0),
               a2p=jnp.concatenate([jnp.zeros((LORA_W, D_R), F32), w["od_a2"]], axis=0))
    prs = _shift_fwd("l1_shift", p1, col0, w["od_mu"])
    lw, k2, a_, b_, gate_r = _rwkv_pre_fwd("l1_rwkv_pre", prs, prm, seg, tr)
    v_off = 2 * D_R // (WKV_PAIRS_PER_STEP * PAIR)
    scan_in = [(prs, 0), (lw, 0), (k2, 0), (prs, v_off), (a_, 0), (b_, 0)]
    y_scan, states = _wkv_fwd("l1_wkv", scan_in)
    y_rwkv = _rwkv_post_fwd("l1_rwkv_post", y_scan, prs, k2, gate_r, prm, seg, tr)
    y1 = jnp.concatenate([y_att, y_rwkv], axis=1)
    h3 = _mm_full("l1_out", y1, w["od_w_out"], 0, tm, D_MODEL, add=h2)
    f1 = (1, h3, w["norm_ffn"][1:2], w["ff_w_up"], w["ff_conv"][1], w["ff_conv_b"][1:2], w["ff_w_down"])
    h4, ffn1 = _ffn_fwd("f1", *f1, tm)

    loss_blk, dh, d_norm_final = _final_loss("final", h4, w["norm_final"], tgt_p, tm)
    grads["norm_final"] = d_norm_final

    dh, gf1 = _ffn_bwd("f1", *f1, ffn1, dh, tm)
    zero = emit("f1", {"ff_w_up1": gf1["w_up"], "ff_w_down1": gf1["w_down"].reshape(N_CHIPS, D_FF // N_CHIPS, D_MODEL)})
    prm = dict(prm, lnx_g=prm["lnx_g"] + zero)
    dy1 = _mm_nt_full("l1_out_dx", dh, w["od_w_out"], 0, tm, D_MODEL)
    grads["od_w_out"] = _mm_tn_full("l1_out_dw", y1, dh, tm, D_MODEL // 2)
    dy_scan, dr_p, dk2_p, dv_p, dgate_r, grads["od_lnx_g"], grads["od_lnx_b"], d_rk = _rwkv_post_bwd(
        "l1_rwkv_post_bwd", y_scan, prs, k2, gate_r, prm, seg, dy1, 1, tr)
    grads["od_r_k"] = d_rk.reshape(N_R_HEADS, HEAD_DIM)
    dr_s, dlw, dk2_s, dv_s, da_, db_ = _wkv_bwd("l1_wkv_bwd", scan_in, states, (dy_scan, 0))
    dk, dxl, dgd, grads["od_w0"], dw2p, grads["od_a0"], da2p, grads["od_g2"], grads["od_k_k"], grads["od_k_a"] = (
        _rwkv_pre_bwd("l1_rwkv_pre_bwd", prs, prm, seg, (dlw, dk2_s + dk2_p, da_, db_, dgate_r), tr))
    grads["od_w2"] = dw2p[:LORA_W]
    grads["od_a2"] = da2p[LORA_W:]
    dprs = jnp.concatenate([dr_s + dr_p, dk, dv_s + dv_p, dxl, dgd], axis=1)
    dpr, grads["od_mu"] = _shift_bwd("l1_shift_bwd", p1, col0, w["od_mu"], dprs)
    doh = _to_heads(dy1[:, :D_ATT], N_Q_HEADS, pad)
    dqh, dkp, dkc, dvp, dvc, dkm, dvm, dsinks = _attn_bwd("l1_attn_bwd", qh, kh, vh, sinks_b, cos, sin, rot, doh)
    grads["od_sinks"] = dsinks[:, 0, 0].reshape(1, N_Q_HEADS)
    dkh = _kv_combine("l1_attn_dk", dkp, dkc, dkm)
    dvh = _kv_combine("l1_attn_dv", dvp, dvc, dvm)
    dp1 = jnp.concatenate([_from_heads(dqh, pad), _from_heads(dkh, pad), _from_heads(dvh, pad), dpr], axis=1)
    grads["od_w_in"] = _mm_tn_cs("l1_in_dw", hn2, dp1, N_CHIPS, tm)
    dhn2 = _mm_nt_cs("l1_in_dx", dp1, w["od_w_in"], 0, tm)
    dh, d_mix1 = _rms_bwd("l1_norm_bwd", h2, w["norm_mix"][1:2], dhn2, dh, tm)

    zero = emit("od", {"od_w_out": grads["od_w_out"].reshape(N_CHIPS, D_MODEL // N_CHIPS, D_MODEL), "od_w_in": grads["od_w_in"]})
    f0 = f0[:5] + (f0[5] + zero,) + f0[6:]
    dh, gf0 = _ffn_bwd("f0", *f0, ffn0, dh, tm)
    zero = emit("f0", {"ff_w_up0": gf0["w_up"], "ff_w_down0": gf0["w_down"].reshape(N_CHIPS, D_FF // N_CHIPS, D_MODEL)})
    w = dict(w, ev_ln_a_g=w["ev_ln_a_g"] + zero)
    dy0 = _mm_nt_full("l0_out_dx", dh, w["ev_w_out"], 0, tm, D_MODEL)
    grads["ev_w_out"] = _mm_tn_full("l0_out_dw", y0, dh, tm, D_MODEL // 2)
    duc, grads["ev_ln_a_g"], grads["ev_ln_a_b"] = _even_ln_bwd("l0_ln_bwd", uc, w["ev_ln_a_g"], w["ev_ln_a_b"], dy0, 0, tm)
    *dparts, grads["ev_conv_a"], grads["ev_conv_b"] = _even_col_bwd("l0_convs_bwd", p0, duc, dy0, w["ev_conv_a"], w["ev_conv_b"])
    dp0 = jnp.concatenate(dparts, axis=1)
    grads["ev_w_in"] = _mm_tn_cs("l0_in_dw", hn0, dp0, N_CHIPS, tm)
    dhn0 = _mm_nt_cs("l0_in_dx", dp0, w["ev_w_in"], 0, tm)
    dh, d_mix0 = _rms_bwd("l0_norm_bwd", h0, w["norm_mix"][0:1], dhn0, dh, tm)

    grads["norm_mix"] = jnp.concatenate([d_mix0, d_mix1], axis=0)
    grads["norm_ffn"] = jnp.concatenate([gf0["norm"], gf1["norm"]], axis=0)
    grads["ff_w_up"] = [gf0["w_up"], gf1["w_up"]]
    grads["ff_conv"] = jnp.stack([gf0["conv"], gf1["conv"]])
    grads["ff_conv_b"] = jnp.concatenate([gf0["bias"], gf1["bias"]], axis=0)
    grads["ff_w_down"] = [gf0["w_down"], gf1["w_down"]]
    grads["meta_tokens"] = dh[:N_META]
    return loss_blk[0, 0], dh[N_META:], grads


SHARD_AXIS = {
    "meta_tokens": 1, "norm_mix": None, "norm_ffn": None, "norm_final": None,
    "ev_w_in": 2, "ev_conv_a": 2, "ev_ln_a_g": None, "ev_ln_a_b": None, "ev_conv_b": 2, "ev_w_out": 1,
    "od_w_in": 2, "od_sinks": None, "od_mu": 1, "od_w0": 1, "od_w2": 2, "od_a0": 1, "od_a2": 2, "od_g2": 2,
    "od_k_k": 1, "od_k_a": 1, "od_r_k": None, "od_lnx_g": 1, "od_lnx_b": 1, "od_w_out": 1,
    "ff_w_up": 2, "ff_conv": 2, "ff_conv_b": None, "ff_w_down": 1,
}
WEIGHTS = list(SHARD_AXIS)
BIG = ("ev_w_in", "ev_w_out", "od_w_in", "od_w_out", "ff_w_up", "ff_w_down")
SHARDED = [n for n in WEIGHTS if SHARD_AXIS[n] is not None]
SMALL = [n for n in SHARDED if n not in BIG]
REPLICATED = [n for n in WEIGHTS if SHARD_AXIS[n] is None]


def _join(g, axis):
    return jnp.concatenate([g[k] for k in range(N_CHIPS)], axis=axis)


def _split(full, axis):
    return jnp.stack(jnp.split(full, N_CHIPS, axis=axis))


def _full_weights(gathered, repl):
    w = {}
    sq = lambda a: a.reshape(a.shape[1:]) if a.shape[0] == 1 else a
    for n in REPLICATED:
        w[n] = repl[n]
    w["norm_final"] = repl["norm_final"].reshape(1, D_MODEL)
    for n in ("ev_ln_a_g", "ev_ln_a_b"):
        w[n] = repl[n].reshape(1, D_A)
    w["od_r_k"] = repl["od_r_k"][0]
    w["meta_tokens"] = _join(gathered["meta_tokens"], 1)
    for n in ("ev_conv_a", "ev_conv_b", "od_w2", "od_a2", "od_g2"):
        w[n] = sq(_join(gathered[n], 2))
    for n in ("od_mu", "od_w0", "od_a0", "od_k_k", "od_k_a", "od_lnx_g", "od_lnx_b"):
        w[n] = _join(gathered[n], 1)
    w["ff_conv"] = _join(gathered["ff_conv"], 2)
    return w


def _shard_grads(grads):
    out = {}
    for n in REPLICATED:
        out[n] = grads[n]
    out["norm_final"] = grads["norm_final"].reshape(D_MODEL)
    out["od_r_k"] = grads["od_r_k"][None]
    out["meta_tokens"] = _split(grads["meta_tokens"], 1)
    for n in ("ev_conv_a", "ev_conv_b", "od_w2", "od_a2", "od_g2"):
        out[n] = _split(grads[n][None], 2)
    for n in ("od_mu", "od_w0", "od_a0", "od_k_k", "od_k_a", "od_lnx_g", "od_lnx_b"):
        out[n] = _split(grads[n], 1)
    out["ff_conv"] = _split(grads["ff_conv"], 2)
    return out


def kernel(x, meta_tokens, norm_mix, norm_ffn, norm_final, ev_w_in, ev_conv_a, ev_ln_a_g, ev_ln_a_b, ev_conv_b, ev_w_out, od_w_in, od_sinks, od_mu, od_w0, od_w2, od_a0, od_a2, od_g2, od_k_k, od_k_a, od_r_k, od_lnx_g, od_lnx_b, od_w_out, ff_w_up, ff_conv, ff_conv_b, ff_w_down, loss_target, m_meta_tokens, m_norm_mix, m_norm_ffn, m_norm_final, m_ev_w_in, m_ev_conv_a, m_ev_ln_a_g, m_ev_ln_a_b, m_ev_conv_b, m_ev_w_out, m_od_w_in, m_od_sinks, m_od_mu, m_od_w0, m_od_w2, m_od_a0, m_od_a2, m_od_g2, m_od_k_k, m_od_k_a, m_od_r_k, m_od_lnx_g, m_od_lnx_b, m_od_w_out, m_ff_w_up, m_ff_conv, m_ff_conv_b, m_ff_w_down, v_meta_tokens, v_norm_mix, v_norm_ffn, v_norm_final, v_ev_w_in, v_ev_conv_a, v_ev_ln_a_g, v_ev_ln_a_b, v_ev_conv_b, v_ev_w_out, v_od_w_in, v_od_sinks, v_od_mu, v_od_w0, v_od_w2, v_od_a0, v_od_a2, v_od_g2, v_od_k_k, v_od_k_a, v_od_r_k, v_od_lnx_g, v_od_lnx_b, v_od_w_out, v_ff_w_up, v_ff_conv, v_ff_conv_b, v_ff_w_down):
    wts = dict(meta_tokens=meta_tokens, norm_mix=norm_mix, norm_ffn=norm_ffn, norm_final=norm_final, ev_w_in=ev_w_in, ev_conv_a=ev_conv_a, ev_ln_a_g=ev_ln_a_g, ev_ln_a_b=ev_ln_a_b, ev_conv_b=ev_conv_b, ev_w_out=ev_w_out, od_w_in=od_w_in, od_sinks=od_sinks, od_mu=od_mu, od_w0=od_w0, od_w2=od_w2, od_a0=od_a0, od_a2=od_a2, od_g2=od_g2, od_k_k=od_k_k, od_k_a=od_k_a, od_r_k=od_r_k, od_lnx_g=od_lnx_g, od_lnx_b=od_lnx_b, od_w_out=od_w_out, ff_w_up=ff_w_up, ff_conv=ff_conv, ff_conv_b=ff_conv_b, ff_w_down=ff_w_down)
    mom = dict(meta_tokens=m_meta_tokens, norm_mix=m_norm_mix, norm_ffn=m_norm_ffn, norm_final=m_norm_final, ev_w_in=m_ev_w_in, ev_conv_a=m_ev_conv_a, ev_ln_a_g=m_ev_ln_a_g, ev_ln_a_b=m_ev_ln_a_b, ev_conv_b=m_ev_conv_b, ev_w_out=m_ev_w_out, od_w_in=m_od_w_in, od_sinks=m_od_sinks, od_mu=m_od_mu, od_w0=m_od_w0, od_w2=m_od_w2, od_a0=m_od_a0, od_a2=m_od_a2, od_g2=m_od_g2, od_k_k=m_od_k_k, od_k_a=m_od_k_a, od_r_k=m_od_r_k, od_lnx_g=m_od_lnx_g, od_lnx_b=m_od_lnx_b, od_w_out=m_od_w_out, ff_w_up=m_ff_w_up, ff_conv=m_ff_conv, ff_conv_b=m_ff_conv_b, ff_w_down=m_ff_w_down)
    var = dict(meta_tokens=v_meta_tokens, norm_mix=v_norm_mix, norm_ffn=v_norm_ffn, norm_final=v_norm_final, ev_w_in=v_ev_w_in, ev_conv_a=v_ev_conv_a, ev_ln_a_g=v_ev_ln_a_g, ev_ln_a_b=v_ev_ln_a_b, ev_conv_b=v_ev_conv_b, ev_w_out=v_ev_w_out, od_w_in=v_od_w_in, od_sinks=v_od_sinks, od_mu=v_od_mu, od_w0=v_od_w0, od_w2=v_od_w2, od_a0=v_od_a0, od_a2=v_od_a2, od_g2=v_od_g2, od_k_k=v_od_k_k, od_k_a=v_od_k_a, od_r_k=v_od_r_k, od_lnx_g=v_od_lnx_g, od_lnx_b=v_od_lnx_b, od_w_out=v_od_w_out, ff_w_up=v_ff_w_up, ff_conv=v_ff_conv, ff_conv_b=v_ff_conv_b, ff_w_down=v_ff_w_down)

    def halves(a):
        l, rows, cols = a.shape
        return a if l == 2 else a.reshape(2, rows // 2, cols)

    me_idx = (2 * lax.axis_index("x") + lax.axis_index("y")).astype(jnp.int32).reshape(1)
    c_idx = lax.axis_index("c").astype(jnp.int32).reshape(1)
    small_mine = _pack([wts[n] for n in SMALL], F32, 2 * 8)
    half_major = [n == "ff_w_down" for n in BIG] + [False]
    mine = [halves(wts[n]) for n in BIG] + [small_mine.reshape(2, -1, PACK_W)]
    bufs = [_place_own_block(f"place_weight{i}", w, me_idx, hm, MXU_DTYPE if i < len(BIG) else F32)
            for i, (w, hm) in enumerate(zip(mine, half_major))]
    got = _gather_weights("gather_weights", bufs, half_major)
    gathered = dict(zip(SMALL, _unpack(got[-1].reshape(N_CHIPS, -1), [wts[n].shape for n in SMALL])))
    w_full = _full_weights(gathered, wts)
    for n, g in zip(BIG, got):
        if n == "ff_w_down":
            w_full[n] = g.reshape(2, D_FF, D_MODEL)
        elif n in ("ev_w_out", "od_w_out"):
            w_full[n] = g.reshape(1, D_MODEL, D_MODEL)
        else:
            w_full[n] = g.reshape((N_CHIPS,) + wts[n].shape)

    cm_idx = jnp.concatenate([c_idx, me_idx])
    started = []

    def start_reduction(tag, units):
        names = list(units)
        from_sibling = _halves_to_sibling(f"grads_to_sibling_{tag}", [units[n] for n in names])
        pairs = [_pair_add_placed(f"grads_pair_add_{n}", units[n], r, cm_idx, GRAD_WIRE_DTYPE) for n, r in zip(names, from_sibling)]
        send_sems, recv_sems, sums, zones, token = _scatter_start(
            f"grads_to_chips_start_{tag}", [p[0] for p in pairs], [p[1] for p in pairs])
        started.append((tag, names, send_sems, recv_sems, sums, zones))
        return token[0, 0]

    loss_local, grad_x, grads = _local_step(x[0], loss_target[0], w_full, start_reduction)
    loss = lax.psum(loss_local, ("x", "y", "c"))

    sg = _shard_grads(grads)
    small_rows = [jnp.concatenate([sg[n][k].reshape(-1) for n in SMALL] + [sg[n].reshape(-1) for n in REPLICATED])
                  for k in range(N_CHIPS)]
    n_el = small_rows[0].shape[0]
    n_rows = -(-n_el // (16 * PACK_W)) * 16
    small_unit = jnp.stack([jnp.pad(r, (0, n_rows * PACK_W - n_el)).reshape(n_rows, PACK_W) for r in small_rows])
    last = {"ev_w_out": grads["ev_w_out"].reshape(N_CHIPS, D_MODEL // N_CHIPS, D_MODEL), "ev_w_in": grads["ev_w_in"],
            "small": small_unit}
    from_sibling = _halves_to_sibling("grads_to_sibling_ev", list(last.values()))
    chip_sums = [_pair_add(f"grads_pair_add_{n}", u, r, c_idx, F32 if n == "small" else GRAD_WIRE_DTYPE)
                 for (n, u), r in zip(last.items(), from_sibling)]
    from_chips = dict(zip(last, _scatter_to_chips("grads_to_chips_ev", chip_sums)))
    for tag, names, send_sems, recv_sems, sums, zones in started:
        from_chips.update(zip(names, _scatter_wait(f"grads_to_chips_wait_{tag}", send_sems, recv_sems, sums, zones,
                                                   from_chips["small"])))
    dests = {"ev_w_in": ("ev_w_in", 0), "od_w_in": ("od_w_in", 0), "ev_w_out": ("ev_w_out", 0), "od_w_out": ("od_w_out", 0),
             "ff_w_up0": ("ff_w_up", 0), "ff_w_up1": ("ff_w_up", 1), "ff_w_down0": ("ff_w_down", 0),
             "ff_w_down1": ("ff_w_down", 1), "small": ("small", 0)}
    results = ["ev_w_in", "od_w_in", "ev_w_out", "od_w_out", "ff_w_up", "ff_w_down", "small"]
    reduced = {}
    for n, (r, l) in dests.items():
        reduced[r] = _sum_chips(f"grads_chip_sum_{n}", from_chips[n], c_idx, l, 2 if r.startswith("ff_w") else 1,
                                into=reduced.get(r))
    joined = _join_halves("grads_join", [reduced[r] for r in results])

    outs = {"grad": {}, "delta": {}, "new_m": {}, "new_v": {}}
    for n, g in zip(results[:-1], joined):
        shape = wts[n].shape
        flat = lambda a: a.reshape(-1, shape[-1])
        new = _adamw("adamw_" + n, flat(wts[n]), flat(g), flat(mom[n]), flat(var[n]))
        for tag, arr in zip(("grad", "delta", "new_m", "new_v"), (g,) + tuple(new)):
            outs[tag][n] = arr.reshape(shape)
    order = SMALL + REPLICATED
    packed = lambda d: jnp.pad(jnp.concatenate([d[n].reshape(-1) for n in order]),
                               (0, n_rows * PACK_W - n_el)).reshape(n_rows, PACK_W)
    g_small = joined[-1].reshape(n_rows, PACK_W)
    new = _adamw("adamw_small", packed(wts), g_small, packed(mom), packed(var))
    for tag, arr in zip(("grad", "delta", "new_m", "new_v"), (g_small,) + tuple(new)):
        outs[tag].update(zip(order, _unpack(arr.reshape(-1), [wts[n].shape for n in order])))
    return (loss, grad_x[None], *[outs["grad"][n] for n in WEIGHTS], *[outs["delta"][n] for n in WEIGHTS],
            *[outs["new_m"][n] for n in WEIGHTS], *[outs["new_v"][n] for n in WEIGHTS])
```

```python
import functools

import jax
import jax.numpy as jnp
from jax import lax
from jax.experimental import pallas as pl
from jax.experimental.pallas import tpu as pltpu

F32 = jnp.float32
BF16 = jnp.bfloat16
HI = lax.Precision.HIGHEST
MXU_DTYPE = BF16
GRAD_WIRE_DTYPE = BF16

D_MODEL = 1024
N_META = 16
RMS_EPS = 1e-6
LN_EPS = 1e-5
D_A = 512
CONV_A_WIDTH = 31
CONV_B_WIDTH = 3
HEAD_DIM = 64
N_Q_HEADS = 8
N_KV_HEADS = 2
GQA_GROUP = 4
D_ATT = 512
D_KV = 128
BLOCK = 128
ROPE_THETA = 10000.0
D_R = 512
N_R_HEADS = 8
LORA_W = 64
LORA_A = 64
LORA_G = 128
RWKV_GN_EPS = 64e-5
RWKV_COLS = 3 * D_R + LORA_W + LORA_A + LORA_G
D_FF = 2816
NEG_INF = -1e30
ADAM_LR = 0.001
ADAM_B1 = 0.9
ADAM_B2 = 0.999
ADAM_EPS = 1e-08
ADAM_WD = 0.01
ADAM_STEP = 10

N_CHIPS = 4
LANES = 128
CONV_PAD = 32
VMEM_LIMIT_V7X = 56 * 1024 * 1024
MESH = pl.DeviceIdType.MESH


def _cparams(sem=None):
    return pltpu.CompilerParams(dimension_semantics=sem, vmem_limit_bytes=VMEM_LIMIT_V7X)


def _row_tile(t, cap):
    for d in range(min(t, cap), 0, -1):
        if t % d == 0 and d % 16 == 0:
            return d
    return t


def _chunk_len(t):
    for d in (64, 48, 32, 16, 8):
        if t % d == 0:
            return d
    raise ValueError(t)


def _call(fn, name, grid, ins, outs, acc_axis=None, sem=None):
    n_in, n_out = len(ins), len(outs)

    def body(*refs):
        vals = fn(*[r[...] for r in refs[:n_in]])
        if not isinstance(vals, (tuple, list)):
            vals = (vals,)
        for r, v, o in zip(refs[n_in:n_in + n_out], vals, outs):
            if o[3]:
                first = pl.program_id(acc_axis) == 0

                @pl.when(first)
                def _(r=r, v=v):
                    r[...] = v

                @pl.when(jnp.logical_not(first))
                def _(r=r, v=v):
                    r[...] += v
            else:
                r[...] = v

    res = pl.pallas_call(
        body, name=name, grid=grid,
        in_specs=[pl.BlockSpec(b, m) for _, b, m in ins],
        out_specs=[pl.BlockSpec(o[1], o[2]) for o in outs],
        out_shape=[jax.ShapeDtypeStruct(o[0], F32) for o in outs],
        compiler_params=_cparams(sem),
    )(*[a for a, _, _ in ins])
    return res if n_out > 1 else res[0]


def _matmul(name, a, b, *, dims, grid, a_spec, b_spec, o_shape, o_spec, acc_shape, nk, k_axis,
            add=None, add_spec=None):
    def body(*refs):
        if add is None:
            a_ref, b_ref, o_ref, acc = refs
        else:
            a_ref, b_ref, add_ref, o_ref, acc = refs
        k = pl.program_id(k_axis)

        @pl.when(k == 0)
        def _():
            if add is None:
                acc[...] = jnp.zeros(acc.shape, F32)
            else:
                acc[...] = add_ref[...]

        acc[...] += lax.dot_general(a_ref[...].astype(MXU_DTYPE), b_ref[...].astype(MXU_DTYPE), dims,
                                    preferred_element_type=F32)

        @pl.when(k == nk - 1)
        def _():
            o_ref[...] = acc[...]

    args = [a, b] + ([] if add is None else [add])
    specs = [a_spec, b_spec] + ([] if add is None else [add_spec])
    return pl.pallas_call(
        body, name=name, grid=grid, in_specs=specs, out_specs=o_spec,
        out_shape=jax.ShapeDtypeStruct(o_shape, F32),
        scratch_shapes=[pltpu.VMEM(acc_shape, F32)],
        compiler_params=_cparams(None),
    )(*args)


_NN = (((1,), (0,)), ((), ()))
_NT = (((1,), (1,)), ((), ()))
_TN = (((0,), (0,)), ((), ()))


def _mm_cs(name, x, wg, l, tm):
    t, k = x.shape
    s, _, _, n = wg.shape
    return _matmul(name, x, wg, dims=_NN, grid=(s, t // tm, 1),
                   a_spec=pl.BlockSpec((tm, k), lambda j, i, kk: (i, 0)),
                   b_spec=pl.BlockSpec((None, None, k, n), lambda j, i, kk: (j, l, 0, 0)),
                   o_shape=(t, s * n), o_spec=pl.BlockSpec((tm, n), lambda j, i, kk: (i, j)),
                   acc_shape=(tm, n), nk=1, k_axis=2)


def _mm_full(name, x, w, l, tm, tk, add=None):
    t, k = x.shape
    n = w.shape[2]
    nk = k // tk
    return _matmul(name, x, w, dims=_NN, grid=(t // tm, 1, nk),
                   a_spec=pl.BlockSpec((tm, tk), lambda i, j, kk: (i, kk)),
                   b_spec=pl.BlockSpec((None, tk, n), lambda i, j, kk: (l, kk, 0)),
                   o_shape=(t, n), o_spec=pl.BlockSpec((tm, n), lambda i, j, kk: (i, 0)),
                   acc_shape=(tm, n), nk=nk, k_axis=2,
                   add=add, add_spec=pl.BlockSpec((tm, n), lambda i, j, kk: (i, 0)))


def _mm_nt_cs(name, dy, wg, l, tm, add=None):
    t = dy.shape[0]
    s, _, k, n = wg.shape
    return _matmul(name, dy, wg, dims=_NT, grid=(t // tm, 1, s),
                   a_spec=pl.BlockSpec((tm, n), lambda i, j, kk: (i, kk)),
                   b_spec=pl.BlockSpec((None, None, k, n), lambda i, j, kk: (kk, l, 0, 0)),
                   o_shape=(t, k), o_spec=pl.BlockSpec((tm, k), lambda i, j, kk: (i, 0)),
                   acc_shape=(tm, k), nk=s, k_axis=2,
                   add=add, add_spec=pl.BlockSpec((tm, k), lambda i, j, kk: (i, 0)))


def _mm_nt_full(name, dy, w, l, tm, tko):
    t, n = dy.shape
    k = w.shape[1]
    return _matmul(name, dy, w, dims=_NT, grid=(t // tm, k // tko, 1),
                   a_spec=pl.BlockSpec((tm, n), lambda i, j, kk: (i, 0)),
                   b_spec=pl.BlockSpec((None, tko, n), lambda i, j, kk: (l, j, 0)),
                   o_shape=(t, k), o_spec=pl.BlockSpec((tm, tko), lambda i, j, kk: (i, j)),
                   acc_shape=(tm, tko), nk=1, k_axis=2)


def _mm_tn_cs(name, x, dy, s, tk):
    t, k = x.shape
    n = dy.shape[1] // s
    nk = t // tk
    return _matmul(name, x, dy, dims=_TN, grid=(s, 1, nk),
                   a_spec=pl.BlockSpec((tk, k), lambda j, i, kk: (kk, 0)),
                   b_spec=pl.BlockSpec((tk, n), lambda j, i, kk: (kk, j)),
                   o_shape=(s, k, n), o_spec=pl.BlockSpec((None, k, n), lambda j, i, kk: (j, 0, 0)),
                   acc_shape=(k, n), nk=nk, k_axis=2)


def _mm_tn_full(name, y, dh, tk, tko):
    t, k = y.shape
    n = dh.shape[1]
    nk = t // tk
    return _matmul(name, y, dh, dims=_TN, grid=(k // tko, 1, nk),
                   a_spec=pl.BlockSpec((tk, tko), lambda j, i, kk: (kk, j)),
                   b_spec=pl.BlockSpec((tk, n), lambda j, i, kk: (kk, 0)),
                   o_shape=(k, n), o_spec=pl.BlockSpec((tko, n), lambda j, i, kk: (j, 0)),
                   acc_shape=(tko, n), nk=nk, k_axis=2)


def _sigmoid(x):
    return 1.0 / (1.0 + jnp.exp(-x))


def _rms_fwd(name, h, g, tr):
    t, d = h.shape

    def fn(hv, gv):
        r = lax.rsqrt(jnp.mean(hv * hv, axis=-1, keepdims=True) + RMS_EPS)
        return hv * r * gv

    return _call(fn, name, (t // tr,), [(h, (tr, d), lambda i: (i, 0)), (g, (1, d), lambda i: (0, 0))],
                 [((t, d), (tr, d), lambda i: (i, 0), False)])


def _rms_bwd(name, h, g, dhn, dh, tr):
    t, d = h.shape

    def fn(hv, gv, dy, dh_in):
        r = lax.rsqrt(jnp.mean(hv * hv, axis=-1, keepdims=True) + RMS_EPS)
        xh = hv * r
        dg = jnp.sum(dy * xh, axis=0, keepdims=True)
        dxh = dy * gv
        dx = r * (dxh - xh * jnp.mean(dxh * xh, axis=-1, keepdims=True))
        return dh_in + dx, dg

    row = lambda i: (i, 0)
    return _call(fn, name, (t // tr,),
                 [(h, (tr, d), row), (g, (1, d), lambda i: (0, 0)), (dhn, (tr, d), row), (dh, (tr, d), row)],
                 [((t, d), (tr, d), row, False), ((1, d), (1, d), lambda i: (0, 0), True)], acc_axis=0)


def _final_loss(name, h, g, tgt, tr):
    t, d = h.shape

    def fn(hv, gv, tv):
        r = lax.rsqrt(jnp.mean(hv * hv, axis=-1, keepdims=True) + RMS_EPS)
        xh = hv * r
        row = pl.program_id(0) * tr + lax.broadcasted_iota(jnp.int32, (tr, 1), 0)
        e = jnp.where(row >= N_META, xh * gv - tv, 0.0)
        loss = jnp.broadcast_to(0.5 * jnp.sum(jnp.sum(e * e, axis=-1, keepdims=True), axis=0, keepdims=True) / d,
                                (8, LANES))
        dout = e / d
        dg = jnp.sum(dout * xh, axis=0, keepdims=True)
        dxh = dout * gv
        dx = r * (dxh - xh * jnp.mean(dxh * xh, axis=-1, keepdims=True))
        return loss, dx, dg

    row = lambda i: (i, 0)
    fix = lambda i: (0, 0)
    return _call(fn, name, (t // tr,), [(h, (tr, d), row), (g, (1, d), fix), (tgt, (tr, d), row)],
                 [((8, LANES), (8, LANES), fix, True), ((t, d), (tr, d), row, False), ((1, d), (1, d), fix, True)],
                 acc_axis=0)


def _silu_ln(uc, g, b):
    mu = jnp.mean(uc, axis=-1, keepdims=True)
    xc = uc - mu
    rs = lax.rsqrt(jnp.mean(xc * xc, axis=-1, keepdims=True) + LN_EPS)
    ln = xc * rs * g + b
    return ln * _sigmoid(ln)


def _even_ln_fwd(name, uc, g, b, tr):
    t, d = uc.shape
    row, fix = (lambda i: (i, 0)), (lambda i: (0, 0))
    return _call(_silu_ln, name, (t // tr,), [(uc, (tr, d), row), (g, (1, d), fix), (b, (1, d), fix)],
                 [((t, d), (tr, d), row, False)])


def _even_ln_bwd(name, uc, g, b, dy, dy_col, tr):
    t, d = uc.shape

    def fn(ucv, gv, bv, dyv):
        mu = jnp.mean(ucv, axis=-1, keepdims=True)
        xc = ucv - mu
        rs = lax.rsqrt(jnp.mean(xc * xc, axis=-1, keepdims=True) + LN_EPS)
        xh = xc * rs
        ln = xh * gv + bv
        s = _sigmoid(ln)
        dln = dyv * (s * (1.0 + ln * (1.0 - s)))
        dg = jnp.sum(dln * xh, axis=0, keepdims=True)
        db = jnp.sum(dln, axis=0, keepdims=True)
        dxh = dln * gv
        duc = rs * (dxh - jnp.mean(dxh, axis=-1, keepdims=True) - xh * jnp.mean(dxh * xh, axis=-1, keepdims=True))
        return duc, dg, db

    row, fix = (lambda i: (i, 0)), (lambda i: (0, 0))
    return _call(fn, name, (t // tr,),
                 [(uc, (tr, d), row), (g, (1, d), fix), (b, (1, d), fix), (dy, (tr, d), lambda i: (i, dy_col))],
                 [((t, d), (tr, d), row, False), ((1, d), (1, d), fix, True), ((1, d), (1, d), fix, True)], acc_axis=0)


def _conv_fwd(xp, w_ref, width, t):
    acc = None
    for j in range(width):
        term = xp[pl.ds(CONV_PAD - (width - 1) + j, t), :] * w_ref[pl.ds(j, 1), :]
        acc = term if acc is None else acc + term
    return acc


def _conv_bwd_in(dyp, w_ref, width, t):
    acc = None
    for j in range(width):
        term = dyp[pl.ds(width - 1 - j, t), :] * w_ref[pl.ds(j, 1), :]
        acc = term if acc is None else acc + term
    return acc


def _conv_bwd_w(dy, xp, dw_ref, width, t):
    for j in range(width):
        dw_ref[pl.ds(j, 1), :] = jnp.sum(dy * xp[pl.ds(CONV_PAD - (width - 1) + j, t), :], axis=0, keepdims=True)


def _store_front(xp, x, t):
    xp[pl.ds(0, CONV_PAD), :] = jnp.zeros((CONV_PAD, LANES), F32)
    xp[pl.ds(CONV_PAD, t), :] = x


def _store_back(xp, x, t):
    xp[pl.ds(0, t), :] = x
    xp[pl.ds(t, CONV_PAD), :] = jnp.zeros((CONV_PAD, LANES), F32)


def _col_call(body, name, ncol, ins, outs, t, n_scratch):
    def spec(rows, off):
        return pl.BlockSpec((rows, LANES), lambda j, off=off: (0, j + off))

    res = pl.pallas_call(
        body, name=name, grid=(ncol,),
        in_specs=[spec(r, off) for _, r, off in ins],
        out_specs=[spec(r, 0) for r, _ in outs],
        out_shape=[jax.ShapeDtypeStruct((r, c), F32) for r, c in outs],
        scratch_shapes=[pltpu.VMEM((t + CONV_PAD, LANES), F32) for _ in range(n_scratch)],
        compiler_params=_cparams(None),
    )(*[a for a, _, _ in ins])
    return res


def _even_col_fwd(name, p, conv_a, conv_b):
    t = p.shape[0]
    nc = D_A // LANES

    def body(av, ag, gb, gc, xi, ca, cb, uc_ref, yb_ref, xp):
        _store_front(xp, av[...] * _sigmoid(ag[...]), t)
        uc_ref[...] = _conv_fwd(xp, ca, CONV_A_WIDTH, t)
        _store_front(xp, gc[...] * xi[...], t)
        yb_ref[...] = gb[...] * _conv_fwd(xp, cb, CONV_B_WIDTH, t)

    ins = [(p, t, k * nc) for k in range(5)] + [(conv_a, CONV_A_WIDTH, 0), (conv_b, CONV_B_WIDTH, 0)]
    return _col_call(body, name, nc, ins, [(t, D_A), (t, D_A)], t, 1)


def _even_col_bwd(name, p, duc, dy, conv_a, conv_b):
    t = p.shape[0]
    nc = D_A // LANES

    def body(av, ag, gb, gc, xi, duc_ref, dyb_ref, ca, cb, dav, dag, dgb, dgc, dxi, dca, dcb, xp, dyp):
        sig = _sigmoid(ag[...])
        _store_front(xp, av[...] * sig, t)
        _store_back(dyp, duc_ref[...], t)
        _conv_bwd_w(duc_ref[...], xp, dca, CONV_A_WIDTH, t)
        du = _conv_bwd_in(dyp, ca, CONV_A_WIDTH, t)
        dav[...] = du * sig
        dag[...] = du * av[...] * sig * (1.0 - sig)
        _store_front(xp, gc[...] * xi[...], t)
        zc = _conv_fwd(xp, cb, CONV_B_WIDTH, t)
        dgb[...] = dyb_ref[...] * zc
        dzc = dyb_ref[...] * gb[...]
        _conv_bwd_w(dzc, xp, dcb, CONV_B_WIDTH, t)
        _store_back(dyp, dzc, t)
        dz = _conv_bwd_in(dyp, cb, CONV_B_WIDTH, t)
        dgc[...] = dz * xi[...]
        dxi[...] = dz * gc[...]

    ins = ([(p, t, k * nc) for k in range(5)] + [(duc, t, 0), (dy, t, nc)]
           + [(conv_a, CONV_A_WIDTH, 0), (conv_b, CONV_B_WIDTH, 0)])
    outs = [(t, D_A)] * 5 + [(CONV_A_WIDTH, D_A), (CONV_B_WIDTH, D_A)]
    return _col_call(body, name, nc, ins, outs, t, 2)


def _ffn_col_fwd(name, u, conv, bias):
    t = u.shape[0]
    nc = D_FF // LANES

    def body(g_ref, v_ref, cw, b_ref, a_ref, xp):
        _store_front(xp, g_ref[...], t)
        gc = _conv_fwd(xp, cw, CONV_B_WIDTH, t) + b_ref[...]
        a_ref[...] = gc * _sigmoid(gc) * v_ref[...]

    ins = [(u, t, 0), (u, t, nc), (conv, CONV_B_WIDTH, 0), (bias, 1, 0)]
    return _col_call(body, name, nc, ins, [(t, D_FF)], t, 1)[0]


def _ffn_col_bwd(name, u, da, conv, bias):
    t = u.shape[0]
    nc = D_FF // LANES

    def body(g_ref, v_ref, da_ref, cw, b_ref, du_ref, dcw, db_ref, xp, dyp):
        _store_front(xp, g_ref[...], t)
        gc = _conv_fwd(xp, cw, CONV_B_WIDTH, t) + b_ref[...]
        s = _sigmoid(gc)

        @pl.when(pl.program_id(1) == 0)
        def _():
            dgc = da_ref[...] * v_ref[...] * (s * (1.0 + gc * (1.0 - s)))
            db_ref[...] = jnp.sum(dgc, axis=0, keepdims=True)
            _conv_bwd_w(dgc, xp, dcw, CONV_B_WIDTH, t)
            _store_back(dyp, dgc, t)
            du_ref[...] = _conv_bwd_in(dyp, cw, CONV_B_WIDTH, t)

        @pl.when(pl.program_id(1) == 1)
        def _():
            du_ref[...] = da_ref[...] * gc * s

    col = lambda rows, off: pl.BlockSpec((rows, LANES), lambda j, p: (0, j + off))
    return pl.pallas_call(
        body, name=name, grid=(nc, 2),
        in_specs=[col(t, 0), col(t, nc), col(t, 0), col(CONV_B_WIDTH, 0), col(1, 0)],
        out_specs=[pl.BlockSpec((t, LANES), lambda j, p: (0, j + nc * p)), col(CONV_B_WIDTH, 0), col(1, 0)],
        out_shape=[jax.ShapeDtypeStruct((t, 2 * D_FF), F32), jax.ShapeDtypeStruct((CONV_B_WIDTH, D_FF), F32),
                   jax.ShapeDtypeStruct((1, D_FF), F32)],
        scratch_shapes=[pltpu.VMEM((t + CONV_PAD, LANES), F32) for _ in range(2)],
        compiler_params=_cparams(None),
    )(u, u, da, conv, bias)


def _shift_fwd(name, p, col0, mu):
    t = p.shape[0]

    def body(x_ref, mu_ref, o_ref, xp):
        _store_front(xp, x_ref[...], t)
        prev = xp[pl.ds(CONV_PAD - 1, t), :]
        o_ref[...] = x_ref[...] + (prev - x_ref[...]) * mu_ref[...]

    return _col_call(body, name, RWKV_COLS // LANES, [(p, t, col0), (mu, 1, 0)], [(t, RWKV_COLS)], t, 1)[0]


def _shift_bwd(name, p, col0, mu, dprs):
    t = p.shape[0]

    def body(x_ref, mu_ref, d_ref, dx_ref, dmu_ref, xp, dyp):
        _store_front(xp, x_ref[...], t)
        prev = xp[pl.ds(CONV_PAD - 1, t), :]
        dmu_ref[...] = jnp.sum(d_ref[...] * (prev - x_ref[...]), axis=0, keepdims=True)
        dm = d_ref[...] * mu_ref[...]
        _store_back(dyp, dm, t)
        dx_ref[...] = d_ref[...] - dm + dyp[pl.ds(1, t), :]

    ins = [(p, t, col0), (mu, 1, 0), (dprs, t, 0)]
    return _col_call(body, name, RWKV_COLS // LANES, ins, [(t, RWKV_COLS), (1, RWKV_COLS)], t, 2)


def _hi_lo(x):
    hi = x.astype(BF16)
    return hi, (x - hi.astype(F32)).astype(BF16)


def _dot_passes(a, b, dims, passes):
    d = lambda p, q: lax.dot_general(p, q, dims, preferred_element_type=F32)
    if passes == 1:
        return d(a.astype(MXU_DTYPE), b.astype(MXU_DTYPE))
    ah, al = _hi_lo(a)
    bh, bl = _hi_lo(b)
    return d(ah, bh) + (d(ah, bl) + d(al, bh))


@functools.partial(jax.custom_vjp, nondiff_argnums=(2, 3))
def _dot_vjp(a, b, dims, passes):
    return _dot_passes(a, b, dims, passes)


def _dot_fwd(a, b, dims, passes):
    return _dot_passes(a, b, dims, passes), (a, b)


def _dot_bwd(dims, passes, res, g):
    a, b = res
    if dims == _NN:
        return _dot_passes(g, b, _NT, passes), _dot_passes(a, g, _TN, passes)
    if dims == _NT:
        return _dot_passes(g, b, _NN, passes), _dot_passes(g, a, _TN, passes)
    return _dot_passes(b, g, _NT, passes), _dot_passes(a, g, _NN, passes)


_dot_vjp.defvjp(_dot_fwd, _dot_bwd)


def _doth(a, b, dims=_NN):
    return _dot_vjp(a, b, dims, 3)


def _dotb(a, b, dims=_NN):
    return _dot_vjp(a, b, dims, 1)


def _softplus(x):
    return jnp.where(x > 0, x, 0.0) + jnp.log(1.0 + jnp.exp(jnp.where(x > 0, -x, x)))


def _rwkv_pre(k, xl, gd, w0, w2p, a0, a2p, g2, k_k, k_a, seg):
    z = w0 + _dotb(jnp.tanh(xl), w2p)
    lw = -jnp.exp(-_softplus(-z) - 0.5)
    alpha = _sigmoid(a0 + _dotb(xl, a2p))
    g = _dotb(_sigmoid(gd), g2)
    kk = k * k_k
    kk = kk / jnp.maximum(jnp.sqrt(_dotb(kk * kk, seg)), 1e-12)
    k2 = k * (1.0 + (alpha - 1.0) * k_a)
    return lw, k2, -kk, kk * alpha, g


def _rwkv_post(y, r, k2, v, g, lnx_g, lnx_b, r_k, seg):
    mean = _dotb(y, seg) * (1.0 / HEAD_DIM)
    yc = y - mean
    var = _dotb(yc * yc, seg) * (1.0 / HEAD_DIM)
    yo = yc * lax.rsqrt(var + RWKV_GN_EPS) * lnx_g + lnx_b
    bonus = _dotb(r * k2 * r_k, seg) * v
    return (yo + bonus) * g


def _rwkv_pre_fwd(name, prs, prm, seg, tr):
    t = prs.shape[0]
    row = lambda i: (i, 0)
    fix = lambda i: (0, 0)
    ins = [(prs, (tr, D_R), lambda i: (i, 1)), (prs, (tr, LANES), lambda i: (i, 12)), (prs, (tr, LANES), lambda i: (i, 13)),
           (prm["w0"], (1, D_R), fix), (prm["w2p"], (LANES, D_R), fix), (prm["a0"], (1, D_R), fix),
           (prm["a2p"], (LANES, D_R), fix), (prm["g2"], (LANES, D_R), fix), (prm["k_k"], (1, D_R), fix),
           (prm["k_a"], (1, D_R), fix), (seg, (D_R, D_R), fix)]
    return _call(_rwkv_pre, name, (t // tr,), ins, [((t, D_R), (tr, D_R), row, False)] * 5)


def _rwkv_pre_bwd(name, prs, prm, seg, cts, tr):
    t = prs.shape[0]

    def fn(k, xl, gd, w0, w2p, a0, a2p, g2, k_k, k_a, segv, *ct):
        _, vjp = jax.vjp(lambda *a: _rwkv_pre(*a, segv), k, xl, gd, w0, w2p, a0, a2p, g2, k_k, k_a)
        return vjp(tuple(ct))

    row = lambda i: (i, 0)
    fix = lambda i: (0, 0)
    ins = [(prs, (tr, D_R), lambda i: (i, 1)), (prs, (tr, LANES), lambda i: (i, 12)), (prs, (tr, LANES), lambda i: (i, 13)),
           (prm["w0"], (1, D_R), fix), (prm["w2p"], (LANES, D_R), fix), (prm["a0"], (1, D_R), fix),
           (prm["a2p"], (LANES, D_R), fix), (prm["g2"], (LANES, D_R), fix), (prm["k_k"], (1, D_R), fix),
           (prm["k_a"], (1, D_R), fix), (seg, (D_R, D_R), fix)] + [(c, (tr, D_R), row) for c in cts]
    outs = [((t, D_R), (tr, D_R), row, False), ((t, LANES), (tr, LANES), row, False), ((t, LANES), (tr, LANES), row, False),
            ((1, D_R), (1, D_R), fix, True), ((LANES, D_R), (LANES, D_R), fix, True), ((1, D_R), (1, D_R), fix, True),
            ((LANES, D_R), (LANES, D_R), fix, True), ((LANES, D_R), (LANES, D_R), fix, True),
            ((1, D_R), (1, D_R), fix, True), ((1, D_R), (1, D_R), fix, True)]
    return _call(fn, name, (t // tr,), ins, outs, acc_axis=0)


def _rwkv_post_ins(y, prs, k2, g, prm, seg, tr):
    row = lambda i: (i, 0)
    fix = lambda i: (0, 0)
    return [(y, (tr, D_R), row), (prs, (tr, D_R), row), (k2, (tr, D_R), row), (prs, (tr, D_R), lambda i: (i, 2)),
            (g, (tr, D_R), row), (prm["lnx_g"], (1, D_R), fix), (prm["lnx_b"], (1, D_R), fix), (prm["r_k"], (1, D_R), fix),
            (seg, (D_R, D_R), fix)]


def _rwkv_post_fwd(name, y, prs, k2, g, prm, seg, tr):
    t = y.shape[0]
    return _call(_rwkv_post, name, (t // tr,), _rwkv_post_ins(y, prs, k2, g, prm, seg, tr),
                 [((t, D_R), (tr, D_R), lambda i: (i, 0), False)])


def _rwkv_post_bwd(name, y, prs, k2, g, prm, seg, dy, dy_col, tr):
    t = y.shape[0]

    def fn(yv, r, k2v, v, gv, lg, lb, rk, segv, ct):
        _, vjp = jax.vjp(lambda *a: _rwkv_post(*a, segv), yv, r, k2v, v, gv, lg, lb, rk)
        return vjp(ct)

    row = lambda i: (i, 0)
    fix = lambda i: (0, 0)
    ins = _rwkv_post_ins(y, prs, k2, g, prm, seg, tr) + [(dy, (tr, D_R), lambda i: (i, dy_col))]
    outs = [((t, D_R), (tr, D_R), row, False)] * 5 + [((1, D_R), (1, D_R), fix, True)] * 3
    return _call(fn, name, (t // tr,), ins, outs, acc_axis=0)


def _wkv_chunk(s0, r, lw, k, v, a, b):
    c = r[0].shape[0]
    lane = lax.broadcasted_iota(jnp.int32, (1, 2 * HEAD_DIM), 1)
    first = (lane < HEAD_DIM).astype(F32)
    per_head = lambda x: jnp.concatenate([x * first, x * (1.0 - first)], axis=0)

    def time_of(shape, dim):
        i = lax.broadcasted_iota(jnp.int32, shape, dim)
        return jnp.where(i >= c, i - c, i)

    incl = (lax.broadcasted_iota(jnp.int32, (c, c), 0) >= lax.broadcasted_iota(jnp.int32, (c, c), 1)).astype(F32)
    strict2 = time_of((2 * c, 2 * c), 0) > time_of((2 * c, 2 * c), 1)
    incl2 = lax.broadcasted_iota(jnp.int32, (c, 2 * c), 0) >= time_of((c, 2 * c), 1)
    each = lambda f, *xs: [f(*x) for x in zip(*xs)]
    cum = each(lambda x: _doth(incl, x), lw)
    tot = each(lambda x: jnp.sum(x, axis=0, keepdims=True), lw)
    e_inv = each(lambda x: jnp.exp(-x), cum)
    a_st = each(lambda x, cm, l: per_head(x * jnp.exp(cm - l)), a, cum, lw)
    r_t = each(lambda x, cm: x * jnp.exp(cm), r, cum)
    b_st = each(lambda x, e: per_head(x * e), b, e_inv)
    k_st = each(lambda x, e: per_head(x * e), k, e_inv)
    v_st = each(per_head, v)
    m = each(lambda x, w: jnp.where(strict2, _dotb(x, w, _NT), 0.0), a_st, b_st)
    m_k = each(lambda x, w: jnp.where(strict2, _dotb(x, w, _NT), 0.0), a_st, k_st)
    u = each(lambda x, s, mk, w: _dotb(x, s, _NT) + _dotb(mk, w), a_st, s0, m_k, v_st)
    steps = (c - 1).bit_length()
    for s in range(steps):
        u = each(lambda x, w: x + _dotb(w, x), u, m)
        if s + 1 < steps:
            m = each(lambda w: _dotb(w, w), m)
    n_b = each(lambda x, w: jnp.where(incl2, _dotb(x, w, _NT), 0.0), r_t, b_st)
    n_k = each(lambda x, w: jnp.where(incl2, _dotb(x, w, _NT), 0.0), r_t, k_st)
    y = each(lambda x, s, nb, uu, nk, w: _dotb(x, s, _NT) + _dotb(nb, uu) + _dotb(nk, w), r_t, s0, n_b, u, n_k, v_st)
    dec = each(lambda tt, cm: jnp.exp(tt - cm), tot, cum)
    s1 = each(lambda s, tt, uu, x, d, w, kk: s * jnp.exp(tt) + _dotb(uu, per_head(x * d), _TN) + _dotb(w, per_head(kk * d), _TN),
              s0, tot, u, b, dec, v_st, k)
    return tuple(y), tuple(s1)


WKV_PAIRS_PER_STEP = 4
PAIR = 2 * HEAD_DIM


def _wkv_fwd(name, srcs):
    t = srcs[0][0].shape[0]
    c = _chunk_len(t)
    nc = t // c
    pp = WKV_PAIRS_PER_STEP
    n_pairs = D_R // PAIR

    def body(r, lw, k, v, a, b, y_ref, st_ref, state):
        @pl.when(pl.program_id(1) == 0)
        def _():
            state[...] = jnp.zeros(state.shape, F32)

        pairs = lambda ref: tuple(ref[:, pl.ds(i * PAIR, PAIR)] for i in range(pp))
        s0 = tuple(state[i] for i in range(pp))
        y, s1 = _wkv_chunk(s0, pairs(r), pairs(lw), pairs(k), pairs(v), pairs(a), pairs(b))
        for i in range(pp):
            st_ref[i] = s0[i]
            y_ref[:, pl.ds(i * PAIR, PAIR)] = y[i]
            state[i] = s1[i]

    seq = lambda off: pl.BlockSpec((c, pp * PAIR), lambda g, j: (j, off + g))
    return pl.pallas_call(
        body, name=name, grid=(n_pairs // pp, nc), in_specs=[seq(off) for _, off in srcs],
        out_specs=[seq(0), pl.BlockSpec((pp, None, PAIR, PAIR), lambda g, j: (g, j, 0, 0))],
        out_shape=[jax.ShapeDtypeStruct((t, D_R), F32), jax.ShapeDtypeStruct((n_pairs, nc, PAIR, PAIR), F32)],
        scratch_shapes=[pltpu.VMEM((pp, PAIR, PAIR), F32)],
        compiler_params=_cparams(None),
    )(*[a for a, _ in srcs])


def _wkv_bwd(name, srcs, st, dy):
    t = srcs[0][0].shape[0]
    c = _chunk_len(t)
    nc = t // c
    pp = WKV_PAIRS_PER_STEP
    n_pairs = D_R // PAIR

    def body(r, lw, k, v, a, b, st_ref, dy_ref, dr, dlw, dk, dv, da, db, dstate):
        @pl.when(pl.program_id(1) == 0)
        def _():
            dstate[...] = jnp.zeros(dstate.shape, F32)

        half = lax.broadcasted_iota(jnp.int32, (PAIR, PAIR), 0) < HEAD_DIM
        same_head = half == (lax.broadcasted_iota(jnp.int32, (PAIR, PAIR), 1) < HEAD_DIM)
        pairs = lambda ref: tuple(ref[:, pl.ds(i * PAIR, PAIR)] for i in range(pp))
        s0 = tuple(st_ref[i] for i in range(pp))
        _, vjp = jax.vjp(_wkv_chunk, s0, pairs(r), pairs(lw), pairs(k), pairs(v), pairs(a), pairs(b))
        ds0, *dxs = vjp((pairs(dy_ref), tuple(dstate[i] for i in range(pp))))
        for i in range(pp):
            for ref, val in zip((dr, dlw, dk, dv, da, db), dxs):
                ref[:, pl.ds(i * PAIR, PAIR)] = val[i]
            dstate[i] = jnp.where(same_head, ds0[i], 0.0)

    seq = lambda off: pl.BlockSpec((c, pp * PAIR), lambda g, j: (nc - 1 - j, off + g))
    return pl.pallas_call(
        body, name=name, grid=(n_pairs // pp, nc),
        in_specs=[seq(off) for _, off in srcs]
        + [pl.BlockSpec((pp, None, PAIR, PAIR), lambda g, j: (g, nc - 1 - j, 0, 0)), seq(dy[1])],
        out_specs=[seq(0)] * 6,
        out_shape=[jax.ShapeDtypeStruct((t, D_R), F32)] * 6,
        scratch_shapes=[pltpu.VMEM((pp, PAIR, PAIR), F32)],
        compiler_params=_cparams(None),
    )(*[a for a, _ in srcs], st, dy[0])


def _rope(x, cos, sin, rot):
    return x * cos + _dotb(x, rot) * sin


def _attn_block(nb, q, kp, kc, km, vp, vc, vm, sk, cq, sq, cp, sp, cm, sm, rot):
    g = GQA_GROUP
    scale = HEAD_DIM ** -0.5
    down = lambda x: jnp.concatenate([x] * g, axis=0)
    kpr = _rope(kp, cp, sp, rot)
    kcr = _rope(kc, cq, sq, rot)
    kmr = _rope(km, cm, sm, rot)
    qr = _rope(q, down(cq), down(sq), rot)
    i = lax.broadcasted_iota(jnp.int32, (g * BLOCK, BLOCK), 0)
    i = i - BLOCK * ((i >= BLOCK).astype(jnp.int32) + (i >= 2 * BLOCK).astype(jnp.int32) + (i >= 3 * BLOCK).astype(jnp.int32))
    j = lax.broadcasted_iota(jnp.int32, (g * BLOCK, BLOCK), 1)
    nbv = jnp.zeros((g * BLOCK, BLOCK), jnp.int32) + nb
    ok_p = (j > i) & (nbv >= 2)
    ok_c = (j <= i) & (nbv >= 1)
    ok_m = (j >= BLOCK - N_META) & ((nbv >= 1) | (j <= i))
    sink = jnp.concatenate([jnp.broadcast_to(s, (BLOCK, 1)) for s in sk], axis=0)
    s_p = jnp.where(ok_p, _dotb(qr, kpr, _NT) * scale, NEG_INF)
    s_c = jnp.where(ok_c, _dotb(qr, kcr, _NT) * scale, NEG_INF)
    s_m = jnp.where(ok_m, _dotb(qr, kmr, _NT) * scale, NEG_INF)
    rmax = lambda s: jnp.max(s, axis=-1, keepdims=True)
    m = lax.stop_gradient(jnp.maximum(jnp.maximum(rmax(s_p), rmax(s_c)), jnp.maximum(rmax(s_m), sink)))
    e_p, e_c, e_m = jnp.exp(s_p - m), jnp.exp(s_c - m), jnp.exp(s_m - m)
    rsum = lambda e: jnp.sum(e, axis=-1, keepdims=True)
    inv = 1.0 / (rsum(e_p) + rsum(e_c) + rsum(e_m) + jnp.exp(sink - m))
    return _dotb(e_p * inv, vp) + _dotb(e_c * inv, vc) + _dotb(e_m * inv, vm)


def _attn_specs():
    cur = lambda g, n: (g, n, 0)
    prev = lambda g, n: (g, jnp.maximum(n - 1, 0), 0)
    meta = lambda g, n: (g, 0, 0)
    kv = lambda m: pl.BlockSpec((None, BLOCK, HEAD_DIM), m)
    tab = lambda m: pl.BlockSpec((BLOCK, HEAD_DIM), m)
    tcur, tprev, tmeta = (lambda g, n: (n, 0)), (lambda g, n: (jnp.maximum(n - 1, 0), 0)), (lambda g, n: (0, 0))
    qspec = pl.BlockSpec((GQA_GROUP, BLOCK, HEAD_DIM), cur)
    sspec = pl.BlockSpec((GQA_GROUP, 8, LANES), meta)
    specs = [qspec, kv(prev), kv(cur), kv(meta), kv(prev), kv(cur), kv(meta), sspec,
             tab(tcur), tab(tcur), tab(tprev), tab(tprev), tab(tmeta), tab(tmeta),
             pl.BlockSpec((HEAD_DIM, HEAD_DIM), lambda g, n: (0, 0))]
    return specs, qspec, sspec, kv


def _attn_args(q, k, v, sinks_b, cos, sin, rot):
    return (q, k, k, k, v, v, v, sinks_b, cos, sin, cos, sin, cos, sin, rot)


def _attn_fwd(name, q, k, v, sinks_b, cos, sin, rot):
    tp = q.shape[1]
    specs, qspec, _, _ = _attn_specs()

    def body(q_ref, kp, kc, km, vp, vc, vm, s_ref, cq, sq, cp, sp, cm, sm, rot_ref, o_ref):
        q = jnp.concatenate([q_ref[h] for h in range(GQA_GROUP)], axis=0)
        sk = tuple(s_ref[h][0:1, 0:1] for h in range(GQA_GROUP))
        out = _attn_block(pl.program_id(1), q, kp[...], kc[...], km[...], vp[...], vc[...], vm[...], sk,
                          cq[...], sq[...], cp[...], sp[...], cm[...], sm[...], rot_ref[...])
        for h in range(GQA_GROUP):
            o_ref[h] = out[h * BLOCK:(h + 1) * BLOCK]

    return pl.pallas_call(
        body, name=name, grid=(N_KV_HEADS, tp // BLOCK), in_specs=specs, out_specs=qspec,
        out_shape=jax.ShapeDtypeStruct(q.shape, F32), compiler_params=_cparams(None),
    )(*_attn_args(q, k, v, sinks_b, cos, sin, rot))


def _attn_bwd(name, q, k, v, sinks_b, cos, sin, rot, do):
    tp = q.shape[1]
    nb = tp // BLOCK
    specs, qspec, sspec, kv = _attn_specs()

    def body(q_ref, kp, kc, km, vp, vc, vm, s_ref, cq, sq, cp, sp, cm, sm, rot_ref, do_ref,
             dq_ref, dkp, dkc, dvp, dvc, dkm, dvm, ds_ref):
        n = pl.program_id(1)
        q = jnp.concatenate([q_ref[h] for h in range(GQA_GROUP)], axis=0)
        sk = tuple(s_ref[h][0:1, 0:1] for h in range(GQA_GROUP))
        tabs = (cq[...], sq[...], cp[...], sp[...], cm[...], sm[...], rot_ref[...])
        _, vjp = jax.vjp(lambda *a: _attn_block(n, *a, *tabs), q, kp[...], kc[...], km[...], vp[...], vc[...], vm[...], sk)
        dq, gkp, gkc, gkm, gvp, gvc, gvm, dsk = vjp(jnp.concatenate([do_ref[h] for h in range(GQA_GROUP)], axis=0))
        dkp[...] = gkp
        dkc[...] = gkc
        dvp[...] = gvp
        dvc[...] = gvc
        for h in range(GQA_GROUP):
            dq_ref[h] = dq[h * BLOCK:(h + 1) * BLOCK]

        @pl.when(n == 0)
        def _():
            dkm[...] = gkm
            dvm[...] = gvm
            for h in range(GQA_GROUP):
                ds_ref[h] = jnp.broadcast_to(dsk[h], (8, LANES))

        @pl.when(n != 0)
        def _():
            dkm[...] += gkm
            dvm[...] += gvm
            for h in range(GQA_GROUP):
                ds_ref[h] += jnp.broadcast_to(dsk[h], (8, LANES))

    part = pl.BlockSpec((None, None, BLOCK, HEAD_DIM), lambda g, n: (g, n, 0, 0))
    part_shape = jax.ShapeDtypeStruct((N_KV_HEADS, nb, BLOCK, HEAD_DIM), F32)
    meta_shape = jax.ShapeDtypeStruct((N_KV_HEADS, BLOCK, HEAD_DIM), F32)
    return pl.pallas_call(
        body, name=name, grid=(N_KV_HEADS, nb), in_specs=specs + [qspec],
        out_specs=[qspec, part, part, part, part, kv(lambda g, n: (g, 0, 0)), kv(lambda g, n: (g, 0, 0)), sspec],
        out_shape=[jax.ShapeDtypeStruct(q.shape, F32), part_shape, part_shape, part_shape, part_shape,
                   meta_shape, meta_shape, jax.ShapeDtypeStruct(sinks_b.shape, F32)],
        compiler_params=_cparams(None),
    )(*_attn_args(q, k, v, sinks_b, cos, sin, rot), do)


def _kv_combine(name, prev_part, own_part, meta):
    g, nb = own_part.shape[:2]

    def fn(own, nxt, mt):
        m = pl.program_id(1)
        one = jnp.ones((BLOCK, HEAD_DIM), F32)
        use_next = jnp.where(one * m < nb - 1, 1.0, 0.0)
        use_meta = jnp.where(one * m < 1, 1.0, 0.0)
        return own + nxt * use_next + mt * use_meta

    blk = (None, None, BLOCK, HEAD_DIM)
    return _call(fn, name, (g, nb),
                 [(own_part, blk, lambda a, m: (a, m, 0, 0)),
                  (prev_part, blk, lambda a, m: (a, jnp.minimum(m + 1, nb - 1), 0, 0)),
                  (meta, (None, BLOCK, HEAD_DIM), lambda a, m: (a, 0, 0))],
                 [((g, nb * BLOCK, HEAD_DIM), (None, BLOCK, HEAD_DIM), lambda a, m: (a, m, 0), False)])


PACK_W = 1024
ELEMENTWISE_BLOCK_BYTES = 1 << 21


def _rows_tile(rows, cols):
    cap = max(8, ELEMENTWISE_BLOCK_BYTES // (4 * cols))
    for d in range(min(rows, cap), 0, -1):
        if rows % d == 0 and d % 8 == 0:
            return d
    return rows


def _adamw(name, w, g, m, v):
    rows, cols = w.shape
    tr = _rows_tile(rows, cols)

    def fn(wv, gv, mv, vv):
        m1 = ADAM_B1 * mv + (1.0 - ADAM_B1) * gv
        v1 = ADAM_B2 * vv + (1.0 - ADAM_B2) * (gv * gv)
        m_hat = m1 / (1.0 - ADAM_B1 ** ADAM_STEP)
        v_hat = v1 / (1.0 - ADAM_B2 ** ADAM_STEP)
        return -ADAM_LR * (m_hat / (jnp.sqrt(v_hat) + ADAM_EPS) + ADAM_WD * wv), m1, v1

    blk = (tr, cols)
    row = lambda i: (i, 0)
    return _call(fn, name, (rows // tr,), [(a, blk, row) for a in (w, g, m, v)], [((rows, cols), blk, row, False)] * 3)


def _pair_add(name, g, recv, c_idx, out_dtype):
    s, a, b = g.shape
    half = a // 2

    def body(c_ref, a_ref, b_ref, o_ref):
        o_ref[...] = (a_ref[...] + b_ref[...]).astype(out_dtype)

    blk = (None, half, b)
    return pl.pallas_call(
        body, name=name,
        grid_spec=pltpu.PrefetchScalarGridSpec(
            num_scalar_prefetch=1, grid=(s,),
            in_specs=[pl.BlockSpec(blk, lambda j, c: (j, c[0], 0)), pl.BlockSpec(blk, lambda j, c: (j, 0, 0))],
            out_specs=pl.BlockSpec(blk, lambda j, c: (j, 0, 0))),
        out_shape=jax.ShapeDtypeStruct((s, half, b), out_dtype), compiler_params=_cparams(None),
    )(c_idx, g, recv)


def _pair_add_placed(name, g, recv, cm_idx, out_dtype):
    s, a, b = g.shape
    half = a // 2

    def body(cm_ref, a_ref, b_ref, o_ref, own_ref):
        val = (a_ref[...] + b_ref[...]).astype(out_dtype)
        o_ref[...] = val

        @pl.when(pl.program_id(0) == cm_ref[1])
        def _():
            own_ref[...] = val

    blk = (None, half, b)
    shape = jax.ShapeDtypeStruct((s, half, b), out_dtype)
    return pl.pallas_call(
        body, name=name,
        grid_spec=pltpu.PrefetchScalarGridSpec(
            num_scalar_prefetch=1, grid=(s,),
            in_specs=[pl.BlockSpec(blk, lambda j, cm: (j, cm[0], 0)), pl.BlockSpec(blk, lambda j, cm: (j, 0, 0))],
            out_specs=[pl.BlockSpec(blk, lambda j, cm: (j, 0, 0)), pl.BlockSpec(blk, lambda j, cm: (cm[1], 0, 0))]),
        out_shape=[shape, shape], compiler_params=_cparams(None),
    )(cm_idx, g, recv)


def _sum_chips(name, parts, c_idx, layer, n_layers, into=None):
    _, a, b = parts.shape
    tr = _rows_tile(a, b)

    def body(c_ref, p0, p1, p2, p3, *rest):
        o_ref = rest[-1]
        up = lambda p: p[...].astype(F32)
        o_ref[...] = ((up(p0) + up(p1)) + up(p2)) + up(p3)

    in_specs = [pl.BlockSpec((None, tr, b), lambda i, c, k=k: (k, i, 0)) for k in range(N_CHIPS)]
    args = [c_idx] + [parts] * N_CHIPS
    aliases = {}
    if into is not None:
        in_specs.append(_ANY)
        args.append(into)
        aliases = {1 + N_CHIPS: 0}
    return pl.pallas_call(
        body, name=name,
        grid_spec=pltpu.PrefetchScalarGridSpec(
            num_scalar_prefetch=1, grid=(a // tr,), in_specs=in_specs,
            out_specs=pl.BlockSpec((None, None, tr, b), lambda i, c: (layer, c[0], i, 0))),
        out_shape=jax.ShapeDtypeStruct((n_layers, 2, a, b), F32), input_output_aliases=aliases,
        compiler_params=_cparams(None),
    )(*args)


def _place_own_block(name, w, layer, me_idx, dtype):
    _, a2, b = w.shape
    a = a2 // 2
    tr = _rows_tile(a, b)
    nb = a // tr

    def body(me_ref, w_ref, o_ref):
        o_ref[...] = w_ref[...].astype(dtype)

    return pl.pallas_call(
        body, name=name,
        grid_spec=pltpu.PrefetchScalarGridSpec(
            num_scalar_prefetch=1, grid=(2, nb),
            in_specs=[pl.BlockSpec((None, tr, b), lambda h, i, me: (layer, h * nb + i, 0))],
            out_specs=pl.BlockSpec((None, None, tr, b), lambda h, i, me: (me[0], h, i, 0))),
        out_shape=jax.ShapeDtypeStruct((N_CHIPS, 2, a, b), dtype), compiler_params=_cparams(None),
    )(me_idx, w)


def _mesh_pos():
    return lax.axis_index("x"), lax.axis_index("y"), lax.axis_index("c")


def _other_chips(x, y):
    return [(1 - x, y), (x, 1 - y), (1 - x, 1 - y)]


_ANY = pl.BlockSpec(memory_space=pl.ANY)


def _gather_weights(name, bufs, from_chips=True):
    n = len(bufs)

    def body(*refs):
        out_refs = refs[n:2 * n]
        send_sems, recv_sems = refs[2 * n:]
        x, y, c = _mesh_pos()
        me = 2 * x + y
        sibling = (x, y, 1 - c)
        chips = _other_chips(x, y)

        def copy(i, k, chip_idx, half, to):
            return pltpu.make_async_remote_copy(src_ref=out_refs[i].at[chip_idx, half], dst_ref=out_refs[i].at[chip_idx, half],
                                                send_sem=send_sems.at[6 * i + k], recv_sem=recv_sems.at[6 * i + k],
                                                device_id=to, device_id_type=MESH)

        first = [copy(i, j, me, c, (*chip, c)) for i in range(n) for j, chip in enumerate(chips)] if from_chips else []
        for cp in first:
            cp.start()
        passed = []
        for i in range(n):
            for j, (cx, cy) in enumerate(chips):
                idx = 2 * cx + cy
                if from_chips:
                    copy(i, j, idx, c, sibling).wait_recv()
                fwd = copy(i, 3 + j, idx, c, sibling)
                fwd.start()
                passed.append(fwd)
        for i in range(n):
            for j, (cx, cy) in enumerate(chips):
                copy(i, 3 + j, 2 * cx + cy, 1 - c, sibling).wait_recv()
        for cp in first + passed:
            cp.wait_send()

    return pl.pallas_call(
        body, name=name, in_specs=[_ANY] * n, out_specs=[_ANY] * n,
        out_shape=[jax.ShapeDtypeStruct(b.shape, b.dtype) for b in bufs],
        input_output_aliases={i: i for i in range(n)},
        scratch_shapes=[pltpu.SemaphoreType.DMA((6 * n,)), pltpu.SemaphoreType.DMA((6 * n,))],
        compiler_params=pltpu.CompilerParams(has_side_effects=True),
    )(*bufs)


def _gather_start(name, groups):
    bufs = [b for g in groups for b in g]
    n = len(bufs)
    ng = len(groups)

    def body(*refs):
        b_refs = refs[:n]
        sems = refs[n:n + 2 * ng]
        token = refs[-1]
        x, y, c = _mesh_pos()
        me = 2 * x + y
        i = 0
        for gi, g in enumerate(groups):
            for k in range(len(g)):
                for j, (cx, cy) in enumerate(_other_chips(x, y)):
                    pltpu.make_async_remote_copy(src_ref=b_refs[i].at[me, c], dst_ref=b_refs[i].at[me, c],
                                                 send_sem=sems[2 * gi].at[3 * k + j], recv_sem=sems[2 * gi + 1].at[3 * k + j],
                                                 device_id=(cx, cy, c), device_id_type=MESH).start()
                i += 1
        token[...] = jnp.zeros(token.shape, F32)

    sem_shapes = [pltpu.SemaphoreType.DMA((3 * len(g),)) for g in groups for _ in range(2)]
    res = pl.pallas_call(
        body, name=name,
        out_shape=(*sem_shapes, *[pltpu.HBM(b.shape, b.dtype) for b in bufs], jax.ShapeDtypeStruct((8, LANES), F32)),
        in_specs=[_HBM] * n,
        out_specs=(*[_SEM] * (2 * ng), *[_HBM] * n, pl.BlockSpec(memory_space=pltpu.VMEM)),
        input_output_aliases={i: 2 * ng + i for i in range(n)},
        compiler_params=pltpu.CompilerParams(has_side_effects=_DATAFLOW),
    )(*[pltpu.with_memory_space_constraint(b, pltpu.HBM) for b in bufs])
    out, i = [], 2 * ng
    for gi, g in enumerate(groups):
        out.append((res[2 * gi], res[2 * gi + 1], list(res[i:i + len(g)])))
        i += len(g)
    return out, res[-1]


def _gather_wait(name, send_sems, recv_sems, bufs, after):
    n = len(bufs)

    def body(*refs):
        b_refs = refs[:n]
        s_sems, r_sems = refs[n], refs[n + 1]
        x, y, c = _mesh_pos()
        me = 2 * x + y
        for k in range(n):
            for j, (cx, cy) in enumerate(_other_chips(x, y)):
                idx = 2 * cx + cy
                copy = pltpu.make_async_remote_copy(src_ref=b_refs[k].at[me, c], dst_ref=b_refs[k].at[idx, c],
                                                    send_sem=s_sems.at[3 * k + j], recv_sem=r_sems.at[3 * k + j],
                                                    device_id=(cx, cy, c), device_id_type=MESH)
                copy.wait_send()
                copy.wait_recv()

    res = pl.pallas_call(
        body, name=name,
        out_shape=tuple(pltpu.HBM(b.shape, b.dtype) for b in bufs),
        in_specs=[_HBM] * n + [_SEM, _SEM, _ANY],
        out_specs=tuple([_HBM] * n),
        input_output_aliases={i: i for i in range(n)},
        compiler_params=pltpu.CompilerParams(has_side_effects=_DATAFLOW),
    )(*bufs, send_sems, recv_sems, after)
    return list(res)


def _halves_to_sibling(name, units):
    n = len(units)

    def body(*refs):
        g_refs, out_refs = refs[:n], refs[n:2 * n]
        send_sems, recv_sems = refs[2 * n:]
        x, y, c = _mesh_pos()
        cps = []
        for i in range(n):
            half = units[i].shape[1] // 2
            src = g_refs[i].at[pl.ds(0, N_CHIPS), pl.ds((1 - c) * half, half)]
            cp = pltpu.make_async_remote_copy(src_ref=src, dst_ref=out_refs[i], send_sem=send_sems.at[i],
                                              recv_sem=recv_sems.at[i], device_id=(x, y, 1 - c), device_id_type=MESH)
            cp.start()
            cps.append(cp)
        for cp in cps:
            cp.wait()

    return pl.pallas_call(
        body, name=name, in_specs=[_ANY] * n, out_specs=[_ANY] * n,
        out_shape=[jax.ShapeDtypeStruct((u.shape[0], u.shape[1] // 2, u.shape[2]), u.dtype) for u in units],
        scratch_shapes=[pltpu.SemaphoreType.DMA((n,)), pltpu.SemaphoreType.DMA((n,))],
        compiler_params=pltpu.CompilerParams(has_side_effects=True),
    )(*units)


def _scatter_to_chips(name, sums):
    n = len(sums)

    def body(*refs):
        h_refs, out_refs = refs[:n], refs[n:2 * n]
        send_sems, recv_sems, local_sems = refs[2 * n:]
        x, y, c = _mesh_pos()
        me = 2 * x + y
        chips = _other_chips(x, y)
        local = [pltpu.make_async_copy(h_refs[i].at[me], out_refs[i].at[me], local_sems.at[i]) for i in range(n)]
        for cp in local:
            cp.start()

        def copy(i, j, src_idx, dst_idx):
            cx, cy = chips[j]
            return pltpu.make_async_remote_copy(src_ref=h_refs[i].at[src_idx], dst_ref=out_refs[i].at[dst_idx],
                                                send_sem=send_sems.at[3 * i + j], recv_sem=recv_sems.at[3 * i + j],
                                                device_id=(cx, cy, c), device_id_type=MESH)

        cps = [copy(i, j, 2 * chips[j][0] + chips[j][1], me) for i in range(n) for j in range(3)]
        for cp in cps:
            cp.start()
        for i in range(n):
            for j in range(3):
                copy(i, j, me, 2 * chips[j][0] + chips[j][1]).wait_recv()
        for cp in cps:
            cp.wait_send()
        for cp in local:
            cp.wait()

    return pl.pallas_call(
        body, name=name, in_specs=[_ANY] * n, out_specs=[_ANY] * n,
        out_shape=[jax.ShapeDtypeStruct(s.shape, s.dtype) for s in sums],
        scratch_shapes=[pltpu.SemaphoreType.DMA((3 * n,)), pltpu.SemaphoreType.DMA((3 * n,)), pltpu.SemaphoreType.DMA((n,))],
        compiler_params=pltpu.CompilerParams(has_side_effects=True),
    )(*sums)


_HBM = pl.BlockSpec(memory_space=pltpu.HBM)
_SEM = pl.BlockSpec(memory_space=pltpu.SEMAPHORE)
_DATAFLOW = pltpu.SideEffectType.DATAFLOW_SIDE_EFFECTING


def _scatter_start(name, sums, zones):
    n = len(sums)

    def body(*refs):
        h_refs, z_refs = refs[:n], refs[n:2 * n]
        send_sems, recv_sems = refs[2 * n], refs[2 * n + 1]
        token = refs[-1]
        x, y, c = _mesh_pos()
        me = 2 * x + y
        for i in range(n):
            for j, (cx, cy) in enumerate(_other_chips(x, y)):
                pltpu.make_async_remote_copy(src_ref=h_refs[i].at[2 * cx + cy], dst_ref=z_refs[i].at[me],
                                             send_sem=send_sems.at[3 * i + j], recv_sem=recv_sems.at[3 * i + j],
                                             device_id=(cx, cy, c), device_id_type=MESH).start()
        token[...] = jnp.zeros(token.shape, F32)

    hbm = lambda a: pltpu.HBM(a.shape, a.dtype)
    res = pl.pallas_call(
        body, name=name,
        out_shape=(pltpu.SemaphoreType.DMA((3 * n,)), pltpu.SemaphoreType.DMA((3 * n,)),
                   *[hbm(a) for a in sums], *[hbm(a) for a in zones], jax.ShapeDtypeStruct((8, LANES), F32)),
        in_specs=[_HBM] * (2 * n),
        out_specs=(_SEM, _SEM, *[_HBM] * (2 * n), pl.BlockSpec(memory_space=pltpu.VMEM)),
        input_output_aliases={i: 2 + i for i in range(2 * n)},
        compiler_params=pltpu.CompilerParams(has_side_effects=_DATAFLOW),
    )(*[pltpu.with_memory_space_constraint(a, pltpu.HBM) for a in list(sums) + list(zones)])
    return res[0], res[1], res[2:2 + n], res[2 + n:2 + 2 * n], res[-1]


def _scatter_wait(name, send_sems, recv_sems, sums, zones, after):
    n = len(sums)

    def body(*refs):
        h_refs, z_refs = refs[:n], refs[n:2 * n]
        s_sems, r_sems = refs[2 * n], refs[2 * n + 1]
        x, y, c = _mesh_pos()
        me = 2 * x + y
        for i in range(n):
            for j, (cx, cy) in enumerate(_other_chips(x, y)):
                idx = 2 * cx + cy
                copy = pltpu.make_async_remote_copy(src_ref=h_refs[i].at[idx], dst_ref=z_refs[i].at[idx],
                                                    send_sem=s_sems.at[3 * i + j], recv_sem=r_sems.at[3 * i + j],
                                                    device_id=(cx, cy, c), device_id_type=MESH)
                copy.wait_send()
                copy.wait_recv()

    hbm = lambda a: pltpu.HBM(a.shape, a.dtype)
    res = pl.pallas_call(
        body, name=name,
        out_shape=(*[hbm(a) for a in sums], *[hbm(a) for a in zones]),
        in_specs=[_HBM] * (2 * n) + [_SEM, _SEM, _ANY],
        out_specs=tuple([_HBM] * (2 * n)),
        input_output_aliases={i: i for i in range(2 * n)},
        compiler_params=pltpu.CompilerParams(has_side_effects=_DATAFLOW),
    )(*sums, *zones, send_sems, recv_sems, after)
    return res[n:]


def _join_halves(name, results):
    n = len(results)
    pieces = [(i, l) for i in range(n) for l in range(results[i].shape[0])]

    def body(*refs):
        out_refs = refs[n:2 * n]
        send_sems, recv_sems = refs[2 * n:]
        x, y, c = _mesh_pos()

        def copy(k, half):
            i, l = pieces[k]
            return pltpu.make_async_remote_copy(src_ref=out_refs[i].at[l, half], dst_ref=out_refs[i].at[l, half],
                                                send_sem=send_sems.at[k], recv_sem=recv_sems.at[k],
                                                device_id=(x, y, 1 - c), device_id_type=MESH)

        cps = [copy(k, c) for k in range(len(pieces))]
        for cp in cps:
            cp.start()
        for k in range(len(pieces)):
            copy(k, 1 - c).wait_recv()
        for cp in cps:
            cp.wait_send()

    return pl.pallas_call(
        body, name=name, in_specs=[_ANY] * n, out_specs=[_ANY] * n,
        out_shape=[jax.ShapeDtypeStruct(r.shape, r.dtype) for r in results],
        input_output_aliases={i: i for i in range(n)},
        scratch_shapes=[pltpu.SemaphoreType.DMA((len(pieces),)), pltpu.SemaphoreType.DMA((len(pieces),))],
        compiler_params=pltpu.CompilerParams(has_side_effects=True),
    )(*results)


def _pack(arrays, dtype, rows_multiple):
    flat = jnp.concatenate([a.reshape(-1).astype(dtype) for a in arrays])
    unit = rows_multiple * PACK_W
    total = -(-flat.shape[0] // unit) * unit
    return jnp.pad(flat, (0, total - flat.shape[0])).reshape(total // PACK_W, PACK_W)


def _unpack(flat, shapes):
    out, off = [], 0
    for s in shapes:
        n = 1
        for d in s:
            n *= d
        out.append(flat[..., off:off + n].reshape(flat.shape[:-1] + tuple(s)))
        off += n
    return out


def _ffn_fwd(tag, l, h, g, w_up, conv, bias, w_down, tm):
    hn = _rms_fwd(f"{tag}_norm", h, g, tm)
    u = _mm_cs(f"{tag}_up", hn, w_up, l, tm)
    act = _ffn_col_fwd(f"{tag}_glu", u, conv, bias)
    h_out = _mm_full(f"{tag}_down", act, w_down, l, tm, D_FF // 2, add=h)
    return h_out, (hn, u, act)


def _ffn_bwd(tag, l, h, g, w_up, conv, bias, w_down, saved, dh, tm):
    hn, u, act = saved
    da = _mm_nt_full(f"{tag}_down_dx", dh, w_down, l, tm, D_FF // 2)
    dw_down = _mm_tn_full(f"{tag}_down_dw", act, dh, tm, D_FF // 2)
    du, dconv, dbias = _ffn_col_bwd(f"{tag}_glu_bwd", u, da, conv, bias)
    dw_up = _mm_tn_cs(f"{tag}_up_dw", hn, du, N_CHIPS, tm)
    dhn = _mm_nt_cs(f"{tag}_up_dx", du, w_up, l, tm)
    dh, dg = _rms_bwd(f"{tag}_norm_bwd", h, g, dhn, dh, tm)
    return dh, dict(norm=dg, w_up=dw_up, conv=dconv, bias=dbias, w_down=dw_down)


def _to_heads(z, nh, pad):
    t = z.shape[0]
    return jnp.pad(z.reshape(t, nh, HEAD_DIM).transpose(1, 0, 2), ((0, 0), (pad, 0), (0, 0)))


def _from_heads(z, pad):
    nh, tp, _ = z.shape
    return z[:, pad:].transpose(1, 0, 2).reshape(tp - pad, nh * HEAD_DIM)


def _rope_tables(tp, pad):
    half = HEAD_DIM // 2
    inv = ROPE_THETA ** (-jnp.arange(half, dtype=F32) / half)
    ang = (jnp.arange(tp, dtype=F32) - pad)[:, None] * inv[None, :]
    cos, sin = jnp.cos(ang), jnp.sin(ang)
    rot = jnp.zeros((HEAD_DIM, HEAD_DIM), F32)
    idx = jnp.arange(half)
    rot = rot.at[idx + half, idx].set(-1.0).at[idx, idx + half].set(1.0)
    return jnp.concatenate([cos, cos], axis=1), jnp.concatenate([sin, sin], axis=1), rot


def _local_step(x, tgt, w, on_grads=None, fetch=None):
    emit = on_grads if on_grads is not None else (lambda tag, units: 0.0)
    need = (lambda tag, after: w) if fetch is None else (lambda tag, after: {**w, **fetch(tag, after)})
    seq = x.shape[0]
    t = seq + N_META
    tm = _row_tile(t, 704)
    tr = _row_tile(t, 352)
    pad = BLOCK - N_META
    grads = {}

    h0 = jnp.concatenate([w["meta_tokens"], x], axis=0)
    tgt_p = jnp.pad(tgt, ((N_META, 0), (0, 0)))

    hn0 = _rms_fwd("l0_norm", h0, w["norm_mix"][0:1], tm)
    p0 = _mm_cs("l0_in", hn0, w["ev_w_in"], 0, tm)
    uc, yb = _even_col_fwd("l0_convs", p0, w["ev_conv_a"], w["ev_conv_b"])
    ya = _even_ln_fwd("l0_ln", uc, w["ev_ln_a_g"], w["ev_ln_a_b"], tm)
    y0 = jnp.concatenate([ya, yb], axis=1)
    w = need("ev_out", y0)
    h1 = _mm_full("l0_out", y0, w["ev_w_out"], 0, tm, D_MODEL, add=h0)
    w = need("f0", h1)
    f0 = (0, h1, w["norm_ffn"][0:1], w["ff_w_up0"], w["ff_conv"][0], w["ff_conv_b"][0:1], w["ff_w_down0"])
    h2, ffn0 = _ffn_fwd("f0", *f0, tm)
    w = need("od", h2)

    hn2 = _rms_fwd("l1_norm", h2, w["norm_mix"][1:2], tm)
    p1 = _mm_cs("l1_in", hn2, w["od_w_in"], 0, tm)
    cos, sin, rot = _rope_tables(t + pad, pad)
    qh = _to_heads(p1[:, :D_ATT], N_Q_HEADS, pad)
    kh = _to_heads(p1[:, D_ATT:D_ATT + D_KV], N_KV_HEADS, pad)
    vh = _to_heads(p1[:, D_ATT + D_KV:D_ATT + 2 * D_KV], N_KV_HEADS, pad)
    sinks_b = jnp.broadcast_to(w["od_sinks"].reshape(N_Q_HEADS, 1, 1), (N_Q_HEADS, 8, LANES))
    y_att = _from_heads(_attn_fwd("l1_attn", qh, kh, vh, sinks_b, cos, sin, rot), pad)

    col0 = (D_ATT + 2 * D_KV) // LANES
    ch = jnp.arange(D_R) // HEAD_DIM
    seg = (ch[:, None] == ch[None, :]).astype(F32)
    prm = dict(w0=w["od_w0"], a0=w["od_a0"], g2=w["od_g2"], k_k=w["od_k_k"], k_a=w["od_k_a"],
               lnx_g=w["od_lnx_g"], lnx_b=w["od_lnx_b"], r_k=w["od_r_k"].reshape(1, D_R),
               w2p=jnp.concatenate([w["od_w2"], jnp.zeros((LORA_A, D_R), F32)], axis=0),
               a2p=jnp.concatenate([jnp.zeros((LORA_W, D_R), F32), w["od_a2"]], axis=0))
    prs = _shift_fwd("l1_shift", p1, col0, w["od_mu"])
    lw, k2, a_, b_, gate_r = _rwkv_pre_fwd("l1_rwkv_pre", prs, prm, seg, tr)
    v_off = 2 * D_R // (WKV_PAIRS_PER_STEP * PAIR)
    scan_in = [(prs, 0), (lw, 0), (k2, 0), (prs, v_off), (a_, 0), (b_, 0)]
    y_scan, states = _wkv_fwd("l1_wkv", scan_in)
    y_rwkv = _rwkv_post_fwd("l1_rwkv_post", y_scan, prs, k2, gate_r, prm, seg, tr)
    y1 = jnp.concatenate([y_att, y_rwkv], axis=1)
    h3 = _mm_full("l1_out", y1, w["od_w_out"], 0, tm, D_MODEL, add=h2)
    w = need("f1", h3)
    f1 = (0, h3, w["norm_ffn"][1:2], w["ff_w_up1"], w["ff_conv"][1], w["ff_conv_b"][1:2], w["ff_w_down1"])
    h4, ffn1 = _ffn_fwd("f1", *f1, tm)

    loss_blk, dh, d_norm_final = _final_loss("final", h4, w["norm_final"], tgt_p, tm)
    grads["norm_final"] = d_norm_final

    dh, gf1 = _ffn_bwd("f1", *f1, ffn1, dh, tm)
    zero = emit("f1", {"ff_w_up1": gf1["w_up"], "ff_w_down1": gf1["w_down"].reshape(N_CHIPS, D_FF // N_CHIPS, D_MODEL)})
    prm = dict(prm, lnx_g=prm["lnx_g"] + zero)
    dy1 = _mm_nt_full("l1_out_dx", dh, w["od_w_out"], 0, tm, D_MODEL)
    grads["od_w_out"] = _mm_tn_full("l1_out_dw", y1, dh, tm, D_MODEL // 2)
    dy_scan, dr_p, dk2_p, dv_p, dgate_r, grads["od_lnx_g"], grads["od_lnx_b"], d_rk = _rwkv_post_bwd(
        "l1_rwkv_post_bwd", y_scan, prs, k2, gate_r, prm, seg, dy1, 1, tr)
    grads["od_r_k"] = d_rk.reshape(N_R_HEADS, HEAD_DIM)
    dr_s, dlw, dk2_s, dv_s, da_, db_ = _wkv_bwd("l1_wkv_bwd", scan_in, states, (dy_scan, 0))
    dk, dxl, dgd, grads["od_w0"], dw2p, grads["od_a0"], da2p, grads["od_g2"], grads["od_k_k"], grads["od_k_a"] = (
        _rwkv_pre_bwd("l1_rwkv_pre_bwd", prs, prm, seg, (dlw, dk2_s + dk2_p, da_, db_, dgate_r), tr))
    grads["od_w2"] = dw2p[:LORA_W]
    grads["od_a2"] = da2p[LORA_W:]
    dprs = jnp.concatenate([dr_s + dr_p, dk, dv_s + dv_p, dxl, dgd], axis=1)
    dpr, grads["od_mu"] = _shift_bwd("l1_shift_bwd", p1, col0, w["od_mu"], dprs)
    doh = _to_heads(dy1[:, :D_ATT], N_Q_HEADS, pad)
    dqh, dkp, dkc, dvp, dvc, dkm, dvm, dsinks = _attn_bwd("l1_attn_bwd", qh, kh, vh, sinks_b, cos, sin, rot, doh)
    grads["od_sinks"] = dsinks[:, 0, 0].reshape(1, N_Q_HEADS)
    dkh = _kv_combine("l1_attn_dk", dkp, dkc, dkm)
    dvh = _kv_combine("l1_attn_dv", dvp, dvc, dvm)
    dp1 = jnp.concatenate([_from_heads(dqh, pad), _from_heads(dkh, pad), _from_heads(dvh, pad), dpr], axis=1)
    grads["od_w_in"] = _mm_tn_cs("l1_in_dw", hn2, dp1, N_CHIPS, tm)
    dhn2 = _mm_nt_cs("l1_in_dx", dp1, w["od_w_in"], 0, tm)
    dh, d_mix1 = _rms_bwd("l1_norm_bwd", h2, w["norm_mix"][1:2], dhn2, dh, tm)

    zero = emit("od", {"od_w_out": grads["od_w_out"].reshape(N_CHIPS, D_MODEL // N_CHIPS, D_MODEL), "od_w_in": grads["od_w_in"]})
    f0 = f0[:5] + (f0[5] + zero,) + f0[6:]
    dh, gf0 = _ffn_bwd("f0", *f0, ffn0, dh, tm)
    zero = emit("f0", {"ff_w_up0": gf0["w_up"], "ff_w_down0": gf0["w_down"].reshape(N_CHIPS, D_FF // N_CHIPS, D_MODEL)})
    w = dict(w, ev_ln_a_g=w["ev_ln_a_g"] + zero)
    dy0 = _mm_nt_full("l0_out_dx", dh, w["ev_w_out"], 0, tm, D_MODEL)
    grads["ev_w_out"] = _mm_tn_full("l0_out_dw", y0, dh, tm, D_MODEL // 2)
    duc, grads["ev_ln_a_g"], grads["ev_ln_a_b"] = _even_ln_bwd("l0_ln_bwd", uc, w["ev_ln_a_g"], w["ev_ln_a_b"], dy0, 0, tm)
    *dparts, grads["ev_conv_a"], grads["ev_conv_b"] = _even_col_bwd("l0_convs_bwd", p0, duc, dy0, w["ev_conv_a"], w["ev_conv_b"])
    dp0 = jnp.concatenate(dparts, axis=1)
    grads["ev_w_in"] = _mm_tn_cs("l0_in_dw", hn0, dp0, N_CHIPS, tm)
    dhn0 = _mm_nt_cs("l0_in_dx", dp0, w["ev_w_in"], 0, tm)
    dh, d_mix0 = _rms_bwd("l0_norm_bwd", h0, w["norm_mix"][0:1], dhn0, dh, tm)

    grads["norm_mix"] = jnp.concatenate([d_mix0, d_mix1], axis=0)
    grads["norm_ffn"] = jnp.concatenate([gf0["norm"], gf1["norm"]], axis=0)
    grads["ff_w_up"] = [gf0["w_up"], gf1["w_up"]]
    grads["ff_conv"] = jnp.stack([gf0["conv"], gf1["conv"]])
    grads["ff_conv_b"] = jnp.concatenate([gf0["bias"], gf1["bias"]], axis=0)
    grads["ff_w_down"] = [gf0["w_down"], gf1["w_down"]]
    grads["meta_tokens"] = dh[:N_META]
    return loss_blk[0, 0], dh[N_META:], grads


SHARD_AXIS = {
    "meta_tokens": 1, "norm_mix": None, "norm_ffn": None, "norm_final": None,
    "ev_w_in": 2, "ev_conv_a": 2, "ev_ln_a_g": None, "ev_ln_a_b": None, "ev_conv_b": 2, "ev_w_out": 1,
    "od_w_in": 2, "od_sinks": None, "od_mu": 1, "od_w0": 1, "od_w2": 2, "od_a0": 1, "od_a2": 2, "od_g2": 2,
    "od_k_k": 1, "od_k_a": 1, "od_r_k": None, "od_lnx_g": 1, "od_lnx_b": 1, "od_w_out": 1,
    "ff_w_up": 2, "ff_conv": 2, "ff_conv_b": None, "ff_w_down": 1,
}
WEIGHTS = list(SHARD_AXIS)
BIG = ("ev_w_in", "ev_w_out", "od_w_in", "od_w_out", "ff_w_up", "ff_w_down")
SHARDED = [n for n in WEIGHTS if SHARD_AXIS[n] is not None]
SMALL = [n for n in SHARDED if n not in BIG]
REPLICATED = [n for n in WEIGHTS if SHARD_AXIS[n] is None]


def _join(g, axis):
    return jnp.concatenate([g[k] for k in range(N_CHIPS)], axis=axis)


def _split(full, axis):
    return jnp.stack(jnp.split(full, N_CHIPS, axis=axis))


def _full_weights(gathered, repl):
    w = {}
    sq = lambda a: a.reshape(a.shape[1:]) if a.shape[0] == 1 else a
    for n in REPLICATED:
        w[n] = repl[n]
    w["norm_final"] = repl["norm_final"].reshape(1, D_MODEL)
    for n in ("ev_ln_a_g", "ev_ln_a_b"):
        w[n] = repl[n].reshape(1, D_A)
    w["od_r_k"] = repl["od_r_k"][0]
    w["meta_tokens"] = _join(gathered["meta_tokens"], 1)
    for n in ("ev_conv_a", "ev_conv_b", "od_w2", "od_a2", "od_g2"):
        w[n] = sq(_join(gathered[n], 2))
    for n in ("od_mu", "od_w0", "od_a0", "od_k_k", "od_k_a", "od_lnx_g", "od_lnx_b"):
        w[n] = _join(gathered[n], 1)
    w["ff_conv"] = _join(gathered["ff_conv"], 2)
    return w


def _shard_grads(grads):
    out = {}
    for n in REPLICATED:
        out[n] = grads[n]
    out["norm_final"] = grads["norm_final"].reshape(D_MODEL)
    out["od_r_k"] = grads["od_r_k"][None]
    out["meta_tokens"] = _split(grads["meta_tokens"], 1)
    for n in ("ev_conv_a", "ev_conv_b", "od_w2", "od_a2", "od_g2"):
        out[n] = _split(grads[n][None], 2)
    for n in ("od_mu", "od_w0", "od_a0", "od_k_k", "od_k_a", "od_lnx_g", "od_lnx_b"):
        out[n] = _split(grads[n], 1)
    out["ff_conv"] = _split(grads["ff_conv"], 2)
    return out


def kernel(x, meta_tokens, norm_mix, norm_ffn, norm_final, ev_w_in, ev_conv_a, ev_ln_a_g, ev_ln_a_b, ev_conv_b, ev_w_out, od_w_in, od_sinks, od_mu, od_w0, od_w2, od_a0, od_a2, od_g2, od_k_k, od_k_a, od_r_k, od_lnx_g, od_lnx_b, od_w_out, ff_w_up, ff_conv, ff_conv_b, ff_w_down, loss_target, m_meta_tokens, m_norm_mix, m_norm_ffn, m_norm_final, m_ev_w_in, m_ev_conv_a, m_ev_ln_a_g, m_ev_ln_a_b, m_ev_conv_b, m_ev_w_out, m_od_w_in, m_od_sinks, m_od_mu, m_od_w0, m_od_w2, m_od_a0, m_od_a2, m_od_g2, m_od_k_k, m_od_k_a, m_od_r_k, m_od_lnx_g, m_od_lnx_b, m_od_w_out, m_ff_w_up, m_ff_conv, m_ff_conv_b, m_ff_w_down, v_meta_tokens, v_norm_mix, v_norm_ffn, v_norm_final, v_ev_w_in, v_ev_conv_a, v_ev_ln_a_g, v_ev_ln_a_b, v_ev_conv_b, v_ev_w_out, v_od_w_in, v_od_sinks, v_od_mu, v_od_w0, v_od_w2, v_od_a0, v_od_a2, v_od_g2, v_od_k_k, v_od_k_a, v_od_r_k, v_od_lnx_g, v_od_lnx_b, v_od_w_out, v_ff_w_up, v_ff_conv, v_ff_conv_b, v_ff_w_down):
    wts = dict(meta_tokens=meta_tokens, norm_mix=norm_mix, norm_ffn=norm_ffn, norm_final=norm_final, ev_w_in=ev_w_in, ev_conv_a=ev_conv_a, ev_ln_a_g=ev_ln_a_g, ev_ln_a_b=ev_ln_a_b, ev_conv_b=ev_conv_b, ev_w_out=ev_w_out, od_w_in=od_w_in, od_sinks=od_sinks, od_mu=od_mu, od_w0=od_w0, od_w2=od_w2, od_a0=od_a0, od_a2=od_a2, od_g2=od_g2, od_k_k=od_k_k, od_k_a=od_k_a, od_r_k=od_r_k, od_lnx_g=od_lnx_g, od_lnx_b=od_lnx_b, od_w_out=od_w_out, ff_w_up=ff_w_up, ff_conv=ff_conv, ff_conv_b=ff_conv_b, ff_w_down=ff_w_down)
    mom = dict(meta_tokens=m_meta_tokens, norm_mix=m_norm_mix, norm_ffn=m_norm_ffn, norm_final=m_norm_final, ev_w_in=m_ev_w_in, ev_conv_a=m_ev_conv_a, ev_ln_a_g=m_ev_ln_a_g, ev_ln_a_b=m_ev_ln_a_b, ev_conv_b=m_ev_conv_b, ev_w_out=m_ev_w_out, od_w_in=m_od_w_in, od_sinks=m_od_sinks, od_mu=m_od_mu, od_w0=m_od_w0, od_w2=m_od_w2, od_a0=m_od_a0, od_a2=m_od_a2, od_g2=m_od_g2, od_k_k=m_od_k_k, od_k_a=m_od_k_a, od_r_k=m_od_r_k, od_lnx_g=m_od_lnx_g, od_lnx_b=m_od_lnx_b, od_w_out=m_od_w_out, ff_w_up=m_ff_w_up, ff_conv=m_ff_conv, ff_conv_b=m_ff_conv_b, ff_w_down=m_ff_w_down)
    var = dict(meta_tokens=v_meta_tokens, norm_mix=v_norm_mix, norm_ffn=v_norm_ffn, norm_final=v_norm_final, ev_w_in=v_ev_w_in, ev_conv_a=v_ev_conv_a, ev_ln_a_g=v_ev_ln_a_g, ev_ln_a_b=v_ev_ln_a_b, ev_conv_b=v_ev_conv_b, ev_w_out=v_ev_w_out, od_w_in=v_od_w_in, od_sinks=v_od_sinks, od_mu=v_od_mu, od_w0=v_od_w0, od_w2=v_od_w2, od_a0=v_od_a0, od_a2=v_od_a2, od_g2=v_od_g2, od_k_k=v_od_k_k, od_k_a=v_od_k_a, od_r_k=v_od_r_k, od_lnx_g=v_od_lnx_g, od_lnx_b=v_od_lnx_b, od_w_out=v_od_w_out, ff_w_up=v_ff_w_up, ff_conv=v_ff_conv, ff_conv_b=v_ff_conv_b, ff_w_down=v_ff_w_down)

    me_idx = (2 * lax.axis_index("x") + lax.axis_index("y")).astype(jnp.int32).reshape(1)
    c_idx = lax.axis_index("c").astype(jnp.int32).reshape(1)
    small_mine = _pack([wts[n] for n in SMALL], F32, 2 * 8)
    sources = {"ev_w_in": (ev_w_in, 0), "small": (small_mine[None], 0), "ev_w_out": (ev_w_out, 0),
               "ff_w_up0": (ff_w_up, 0), "ff_w_down0": (ff_w_down, 0), "od_w_in": (od_w_in, 0), "od_w_out": (od_w_out, 0),
               "ff_w_up1": (ff_w_up, 1), "ff_w_down1": (ff_w_down, 1)}
    bufs = {n: _place_own_block("place_" + n, a, l, me_idx, F32 if n == "small" else MXU_DTYPE)
            for n, (a, l) in sources.items()}

    def as_used(n, g):
        if n in ("ev_w_out", "od_w_out", "ff_w_down0", "ff_w_down1"):
            return g.reshape(1, -1, g.shape[-1])
        return g.reshape(N_CHIPS, 1, -1, g.shape[-1])

    first = dict(zip(("ev_w_in", "small"), _gather_weights("gather_first", [bufs["ev_w_in"], bufs["small"]])))
    gathered = dict(zip(SMALL, _unpack(first["small"].reshape(N_CHIPS, -1), [wts[n].shape for n in SMALL])))
    w_full = _full_weights(gathered, wts)
    w_full["ev_w_in"] = as_used("ev_w_in", first["ev_w_in"])
    groups = {"ev_out": ["ev_w_out"], "f0": ["ff_w_up0", "ff_w_down0"], "od": ["od_w_in", "od_w_out"],
              "f1": ["ff_w_up1", "ff_w_down1"]}
    started_gathers, token = _gather_start("gather_start", [[bufs[n] for n in g] for g in groups.values()])
    started_gathers = dict(zip(groups, started_gathers))
    w_full["norm_mix"] = w_full["norm_mix"] + token[0, 0]

    def fetch(tag, after):
        send_sems, recv_sems, group_bufs = started_gathers[tag]
        landed = _gather_wait("gather_wait_" + tag, send_sems, recv_sems, group_bufs, after)
        whole = _gather_weights("gather_siblings_" + tag, landed, from_chips=False)
        return {n: as_used(n, g) for n, g in zip(groups[tag], whole)}

    cm_idx = jnp.concatenate([c_idx, me_idx])
    started = []

    def start_reduction(tag, units):
        names = list(units)
        from_sibling = _halves_to_sibling(f"grads_to_sibling_{tag}", [units[n] for n in names])
        pairs = [_pair_add_placed(f"grads_pair_add_{n}", units[n], r, cm_idx, GRAD_WIRE_DTYPE) for n, r in zip(names, from_sibling)]
        send_sems, recv_sems, sums, zones, token = _scatter_start(
            f"grads_to_chips_start_{tag}", [p[0] for p in pairs], [p[1] for p in pairs])
        started.append((tag, names, send_sems, recv_sems, sums, zones))
        return token[0, 0]

    loss_local, grad_x, grads = _local_step(x[0], loss_target[0], w_full, start_reduction, fetch)
    loss = lax.psum(loss_local, ("x", "y", "c"))

    sg = _shard_grads(grads)
    small_rows = [jnp.concatenate([sg[n][k].reshape(-1) for n in SMALL] + [sg[n].reshape(-1) for n in REPLICATED])
                  for k in range(N_CHIPS)]
    n_el = small_rows[0].shape[0]
    n_rows = -(-n_el // (16 * PACK_W)) * 16
    small_unit = jnp.stack([jnp.pad(r, (0, n_rows * PACK_W - n_el)).reshape(n_rows, PACK_W) for r in small_rows])
    last = {"ev_w_out": grads["ev_w_out"].reshape(N_CHIPS, D_MODEL // N_CHIPS, D_MODEL), "ev_w_in": grads["ev_w_in"],
            "small": small_unit}
    from_sibling = _halves_to_sibling("grads_to_sibling_ev", list(last.values()))
    chip_sums = [_pair_add(f"grads_pair_add_{n}", u, r, c_idx, F32 if n == "small" else GRAD_WIRE_DTYPE)
                 for (n, u), r in zip(last.items(), from_sibling)]
    from_chips = dict(zip(last, _scatter_to_chips("grads_to_chips_ev", chip_sums)))
    for tag, names, send_sems, recv_sems, sums, zones in started:
        from_chips.update(zip(names, _scatter_wait(f"grads_to_chips_wait_{tag}", send_sems, recv_sems, sums, zones,
                                                   from_chips["small"])))
    dests = {"ev_w_in": ("ev_w_in", 0), "od_w_in": ("od_w_in", 0), "ev_w_out": ("ev_w_out", 0), "od_w_out": ("od_w_out", 0),
             "ff_w_up0": ("ff_w_up", 0), "ff_w_up1": ("ff_w_up", 1), "ff_w_down0": ("ff_w_down", 0),
             "ff_w_down1": ("ff_w_down", 1), "small": ("small", 0)}
    results = ["ev_w_in", "od_w_in", "ev_w_out", "od_w_out", "ff_w_up", "ff_w_down", "small"]
    reduced = {}
    for n, (r, l) in dests.items():
        reduced[r] = _sum_chips(f"grads_chip_sum_{n}", from_chips[n], c_idx, l, 2 if r.startswith("ff_w") else 1,
                                into=reduced.get(r))
    joined = _join_halves("grads_join", [reduced[r] for r in results])

    outs = {"grad": {}, "delta": {}, "new_m": {}, "new_v": {}}
    for n, g in zip(results[:-1], joined):
        shape = wts[n].shape
        flat = lambda a: a.reshape(-1, shape[-1])
        new = _adamw("adamw_" + n, flat(wts[n]), flat(g), flat(mom[n]), flat(var[n]))
        for tag, arr in zip(("grad", "delta", "new_m", "new_v"), (g,) + tuple(new)):
            outs[tag][n] = arr.reshape(shape)
    order = SMALL + REPLICATED
    packed = lambda d: jnp.pad(jnp.concatenate([d[n].reshape(-1) for n in order]),
                               (0, n_rows * PACK_W - n_el)).reshape(n_rows, PACK_W)
    g_small = joined[-1].reshape(n_rows, PACK_W)
    new = _adamw("adamw_small", packed(wts), g_small, packed(mom), packed(var))
    for tag, arr in zip(("grad", "delta", "new_m", "new_v"), (g_small,) + tuple(new)):
        outs[tag].update(zip(order, _unpack(arr.reshape(-1), [wts[n].shape for n in order])))
    return (loss, grad_x[None], *[outs["grad"][n] for n in WEIGHTS], *[outs["delta"][n] for n in WEIGHTS],
            *[outs["new_m"][n] for n in WEIGHTS], *[outs["new_v"][n] for n in WEIGHTS])
```

```python
import functools

import jax
import jax.numpy as jnp
from jax import lax
from jax.experimental import pallas as pl
from jax.experimental.pallas import tpu as pltpu

F32 = jnp.float32
BF16 = jnp.bfloat16
HI = lax.Precision.HIGHEST
MXU_DTYPE = BF16
GRAD_WIRE_DTYPE = BF16

D_MODEL = 1024
N_META = 16
RMS_EPS = 1e-6
LN_EPS = 1e-5
D_A = 512
CONV_A_WIDTH = 31
CONV_B_WIDTH = 3
HEAD_DIM = 64
N_Q_HEADS = 8
N_KV_HEADS = 2
GQA_GROUP = 4
D_ATT = 512
D_KV = 128
BLOCK = 128
ROPE_THETA = 10000.0
D_R = 512
N_R_HEADS = 8
LORA_W = 64
LORA_A = 64
LORA_G = 128
RWKV_GN_EPS = 64e-5
RWKV_COLS = 3 * D_R + LORA_W + LORA_A + LORA_G
D_FF = 2816
NEG_INF = -1e30
ADAM_LR = 0.001
ADAM_B1 = 0.9
ADAM_B2 = 0.999
ADAM_EPS = 1e-08
ADAM_WD = 0.01
ADAM_STEP = 10

N_CHIPS = 4
LANES = 128
CONV_PAD = 32
VMEM_LIMIT_V7X = 56 * 1024 * 1024
MESH = pl.DeviceIdType.MESH


def _cparams(sem=None):
    return pltpu.CompilerParams(dimension_semantics=sem, vmem_limit_bytes=VMEM_LIMIT_V7X)


def _row_tile(t, cap):
    for d in range(min(t, cap), 0, -1):
        if t % d == 0 and d % 16 == 0:
            return d
    return t


def _chunk_len(t):
    for d in (64, 48, 32, 16, 8):
        if t % d == 0:
            return d
    raise ValueError(t)


def _call(fn, name, grid, ins, outs, acc_axis=None, sem=None):
    n_in, n_out = len(ins), len(outs)
    dtype = lambda o: o[4] if len(o) > 4 else F32

    def body(*refs):
        vals = fn(*[r[...] for r in refs[:n_in]])
        if not isinstance(vals, (tuple, list)):
            vals = (vals,)
        for r, v, o in zip(refs[n_in:n_in + n_out], vals, outs):
            if o[3]:
                first = pl.program_id(acc_axis) == 0

                @pl.when(first)
                def _(r=r, v=v):
                    r[...] = v

                @pl.when(jnp.logical_not(first))
                def _(r=r, v=v):
                    r[...] += v
            else:
                r[...] = v.astype(dtype(o))

    res = pl.pallas_call(
        body, name=name, grid=grid,
        in_specs=[pl.BlockSpec(b, m) for _, b, m in ins],
        out_specs=[pl.BlockSpec(o[1], o[2]) for o in outs],
        out_shape=[jax.ShapeDtypeStruct(o[0], dtype(o)) for o in outs],
        compiler_params=_cparams(sem),
    )(*[a for a, _, _ in ins])
    return res if n_out > 1 else res[0]


def _matmul(name, a, b, *, dims, grid, a_spec, b_spec, o_shape, o_spec, acc_shape, nk, k_axis,
            add=None, add_spec=None):
    def product(a_ref, b_ref):
        return lax.dot_general(a_ref[...].astype(MXU_DTYPE), b_ref[...].astype(MXU_DTYPE), dims, preferred_element_type=F32)

    def body_single(*refs):
        a_ref, b_ref, o_ref = refs[0], refs[1], refs[-1]
        o_ref[...] = product(a_ref, b_ref) if add is None else product(a_ref, b_ref) + refs[2][...]

    def body_steps(*refs):
        a_ref, b_ref, o_ref, acc = refs[0], refs[1], refs[-2], refs[-1]
        k = pl.program_id(k_axis)

        @pl.when(k == 0)
        def _():
            if add is None:
                acc[...] = jnp.zeros(acc.shape, F32)
            else:
                acc[...] = refs[2][...]

        acc[...] += product(a_ref, b_ref)

        @pl.when(k == nk - 1)
        def _():
            o_ref[...] = acc[...]

    args = [a, b] + ([] if add is None else [add])
    specs = [a_spec, b_spec] + ([] if add is None else [add_spec])
    return pl.pallas_call(
        body_single if nk == 1 else body_steps, name=name, grid=grid, in_specs=specs, out_specs=o_spec,
        out_shape=jax.ShapeDtypeStruct(o_shape, F32),
        scratch_shapes=[] if nk == 1 else [pltpu.VMEM(acc_shape, F32)],
        compiler_params=_cparams(None),
    )(*args)


_NN = (((1,), (0,)), ((), ()))
_NT = (((1,), (1,)), ((), ()))
_TN = (((0,), (0,)), ((), ()))


def _mm_cs(name, x, wg, l, tm):
    t, k = x.shape
    s, _, _, n = wg.shape
    return _matmul(name, x, wg, dims=_NN, grid=(s, t // tm, 1),
                   a_spec=pl.BlockSpec((tm, k), lambda j, i, kk: (i, 0)),
                   b_spec=pl.BlockSpec((None, None, k, n), lambda j, i, kk: (j, l, 0, 0)),
                   o_shape=(t, s * n), o_spec=pl.BlockSpec((tm, n), lambda j, i, kk: (i, j)),
                   acc_shape=(tm, n), nk=1, k_axis=2)


def _mm_full(name, x, w, l, tm, tk, add=None):
    t, k = x.shape
    n = w.shape[2]
    nk = k // tk
    return _matmul(name, x, w, dims=_NN, grid=(t // tm, 1, nk),
                   a_spec=pl.BlockSpec((tm, tk), lambda i, j, kk: (i, kk)),
                   b_spec=pl.BlockSpec((None, tk, n), lambda i, j, kk: (l, kk, 0)),
                   o_shape=(t, n), o_spec=pl.BlockSpec((tm, n), lambda i, j, kk: (i, 0)),
                   acc_shape=(tm, n), nk=nk, k_axis=2,
                   add=add, add_spec=pl.BlockSpec((tm, n), lambda i, j, kk: (i, 0)))


def _mm_nt_cs(name, dy, wg, l, tm, add=None):
    t = dy.shape[0]
    s, _, k, n = wg.shape
    return _matmul(name, dy, wg, dims=_NT, grid=(t // tm, 1, s),
                   a_spec=pl.BlockSpec((tm, n), lambda i, j, kk: (i, kk)),
                   b_spec=pl.BlockSpec((None, None, k, n), lambda i, j, kk: (kk, l, 0, 0)),
                   o_shape=(t, k), o_spec=pl.BlockSpec((tm, k), lambda i, j, kk: (i, 0)),
                   acc_shape=(tm, k), nk=s, k_axis=2,
                   add=add, add_spec=pl.BlockSpec((tm, k), lambda i, j, kk: (i, 0)))


def _mm_nt_full(name, dy, w, l, tm, tko):
    t, n = dy.shape
    k = w.shape[1]
    return _matmul(name, dy, w, dims=_NT, grid=(t // tm, k // tko, 1),
                   a_spec=pl.BlockSpec((tm, n), lambda i, j, kk: (i, 0)),
                   b_spec=pl.BlockSpec((None, tko, n), lambda i, j, kk: (l, j, 0)),
                   o_shape=(t, k), o_spec=pl.BlockSpec((tm, tko), lambda i, j, kk: (i, j)),
                   acc_shape=(tm, tko), nk=1, k_axis=2)


def _mm_tn_cs(name, x, dy, s, tk):
    t, k = x.shape
    n = dy.shape[1] // s
    nk = t // tk
    return _matmul(name, x, dy, dims=_TN, grid=(s, 1, nk),
                   a_spec=pl.BlockSpec((tk, k), lambda j, i, kk: (kk, 0)),
                   b_spec=pl.BlockSpec((tk, n), lambda j, i, kk: (kk, j)),
                   o_shape=(s, k, n), o_spec=pl.BlockSpec((None, k, n), lambda j, i, kk: (j, 0, 0)),
                   acc_shape=(k, n), nk=nk, k_axis=2)


def _mm_tn_full(name, y, dh, tk, tko):
    t, k = y.shape
    n = dh.shape[1]
    nk = t // tk
    return _matmul(name, y, dh, dims=_TN, grid=(k // tko, 1, nk),
                   a_spec=pl.BlockSpec((tk, tko), lambda j, i, kk: (kk, j)),
                   b_spec=pl.BlockSpec((tk, n), lambda j, i, kk: (kk, 0)),
                   o_shape=(k, n), o_spec=pl.BlockSpec((tko, n), lambda j, i, kk: (j, 0)),
                   acc_shape=(tko, n), nk=nk, k_axis=2)


def _sigmoid(x):
    return 1.0 / (1.0 + jnp.exp(-x))


def _rms_fwd(name, h, g, tr):
    t, d = h.shape

    def fn(hv, gv):
        r = lax.rsqrt(jnp.mean(hv * hv, axis=-1, keepdims=True) + RMS_EPS)
        return hv * r * gv

    return _call(fn, name, (t // tr,), [(h, (tr, d), lambda i: (i, 0)), (g, (1, d), lambda i: (0, 0))],
                 [((t, d), (tr, d), lambda i: (i, 0), False, MXU_DTYPE)])


def _rms_bwd(name, h, g, dhn, dh, tr):
    t, d = h.shape

    def fn(hv, gv, dy, dh_in):
        r = lax.rsqrt(jnp.mean(hv * hv, axis=-1, keepdims=True) + RMS_EPS)
        xh = hv * r
        dg = jnp.sum(dy * xh, axis=0, keepdims=True)
        dxh = dy * gv
        dx = r * (dxh - xh * jnp.mean(dxh * xh, axis=-1, keepdims=True))
        return dh_in + dx, dg

    row = lambda i: (i, 0)
    return _call(fn, name, (t // tr,),
                 [(h, (tr, d), row), (g, (1, d), lambda i: (0, 0)), (dhn, (tr, d), row), (dh, (tr, d), row)],
                 [((t, d), (tr, d), row, False), ((1, d), (1, d), lambda i: (0, 0), True)], acc_axis=0)


def _final_loss(name, h, g, tgt, tr):
    t, d = h.shape

    def fn(hv, gv, tv):
        r = lax.rsqrt(jnp.mean(hv * hv, axis=-1, keepdims=True) + RMS_EPS)
        xh = hv * r
        row = pl.program_id(0) * tr + lax.broadcasted_iota(jnp.int32, (tr, 1), 0)
        e = jnp.where(row >= N_META, xh * gv - tv, 0.0)
        loss = jnp.broadcast_to(0.5 * jnp.sum(jnp.sum(e * e, axis=-1, keepdims=True), axis=0, keepdims=True) / d,
                                (8, LANES))
        dout = e / d
        dg = jnp.sum(dout * xh, axis=0, keepdims=True)
        dxh = dout * gv
        dx = r * (dxh - xh * jnp.mean(dxh * xh, axis=-1, keepdims=True))
        return loss, dx, dg

    row = lambda i: (i, 0)
    fix = lambda i: (0, 0)
    return _call(fn, name, (t // tr,), [(h, (tr, d), row), (g, (1, d), fix), (tgt, (tr, d), row)],
                 [((8, LANES), (8, LANES), fix, True), ((t, d), (tr, d), row, False), ((1, d), (1, d), fix, True)],
                 acc_axis=0)


def _silu_ln(uc, g, b):
    mu = jnp.mean(uc, axis=-1, keepdims=True)
    xc = uc - mu
    rs = lax.rsqrt(jnp.mean(xc * xc, axis=-1, keepdims=True) + LN_EPS)
    ln = xc * rs * g + b
    return ln * _sigmoid(ln)


def _even_ln_fwd(name, uc, g, b, tr):
    t, d = uc.shape
    row, fix = (lambda i: (i, 0)), (lambda i: (0, 0))
    return _call(_silu_ln, name, (t // tr,), [(uc, (tr, d), row), (g, (1, d), fix), (b, (1, d), fix)],
                 [((t, d), (tr, d), row, False, MXU_DTYPE)])


def _even_ln_bwd(name, uc, g, b, dy, dy_col, tr):
    t, d = uc.shape

    def fn(ucv, gv, bv, dyv):
        mu = jnp.mean(ucv, axis=-1, keepdims=True)
        xc = ucv - mu
        rs = lax.rsqrt(jnp.mean(xc * xc, axis=-1, keepdims=True) + LN_EPS)
        xh = xc * rs
        ln = xh * gv + bv
        s = _sigmoid(ln)
        dln = dyv * (s * (1.0 + ln * (1.0 - s)))
        dg = jnp.sum(dln * xh, axis=0, keepdims=True)
        db = jnp.sum(dln, axis=0, keepdims=True)
        dxh = dln * gv
        duc = rs * (dxh - jnp.mean(dxh, axis=-1, keepdims=True) - xh * jnp.mean(dxh * xh, axis=-1, keepdims=True))
        return duc, dg, db

    row, fix = (lambda i: (i, 0)), (lambda i: (0, 0))
    return _call(fn, name, (t // tr,),
                 [(uc, (tr, d), row), (g, (1, d), fix), (b, (1, d), fix), (dy, (tr, d), lambda i: (i, dy_col))],
                 [((t, d), (tr, d), row, False), ((1, d), (1, d), fix, True), ((1, d), (1, d), fix, True)], acc_axis=0)


def _conv_fwd(xp, w_ref, width, t):
    acc = None
    for j in range(width):
        term = xp[pl.ds(CONV_PAD - (width - 1) + j, t), :] * w_ref[pl.ds(j, 1), :]
        acc = term if acc is None else acc + term
    return acc


def _conv_bwd_in(dyp, w_ref, width, t):
    acc = None
    for j in range(width):
        term = dyp[pl.ds(width - 1 - j, t), :] * w_ref[pl.ds(j, 1), :]
        acc = term if acc is None else acc + term
    return acc


def _conv_bwd_w(dy, xp, dw_ref, width, t):
    for j in range(width):
        dw_ref[pl.ds(j, 1), :] = jnp.sum(dy * xp[pl.ds(CONV_PAD - (width - 1) + j, t), :], axis=0, keepdims=True)


def _store_front(xp, x, t):
    xp[pl.ds(0, CONV_PAD), :] = jnp.zeros((CONV_PAD, LANES), F32)
    xp[pl.ds(CONV_PAD, t), :] = x


def _store_back(xp, x, t):
    xp[pl.ds(0, t), :] = x
    xp[pl.ds(t, CONV_PAD), :] = jnp.zeros((CONV_PAD, LANES), F32)


def _col_call(body, name, ncol, ins, outs, t, n_scratch):
    def spec(rows, off):
        return pl.BlockSpec((rows, LANES), lambda j, off=off: (0, j + off))

    res = pl.pallas_call(
        body, name=name, grid=(ncol,),
        in_specs=[spec(r, off) for _, r, off in ins],
        out_specs=[spec(o[0], 0) for o in outs],
        out_shape=[jax.ShapeDtypeStruct(o[:2], o[2] if len(o) > 2 else F32) for o in outs],
        scratch_shapes=[pltpu.VMEM((t + CONV_PAD, LANES), F32) for _ in range(n_scratch)],
        compiler_params=_cparams(None),
    )(*[a for a, _, _ in ins])
    return res


def _even_col_fwd(name, p, conv_a, conv_b):
    t = p.shape[0]
    nc = D_A // LANES

    def body(av, ag, gb, gc, xi, ca, cb, uc_ref, yb_ref, xp):
        _store_front(xp, av[...] * _sigmoid(ag[...]), t)
        uc_ref[...] = _conv_fwd(xp, ca, CONV_A_WIDTH, t)
        _store_front(xp, gc[...] * xi[...], t)
        yb_ref[...] = (gb[...] * _conv_fwd(xp, cb, CONV_B_WIDTH, t)).astype(yb_ref.dtype)

    ins = [(p, t, k * nc) for k in range(5)] + [(conv_a, CONV_A_WIDTH, 0), (conv_b, CONV_B_WIDTH, 0)]
    return _col_call(body, name, nc, ins, [(t, D_A), (t, D_A, MXU_DTYPE)], t, 1)


def _even_col_bwd(name, p, duc, dy, conv_a, conv_b):
    t = p.shape[0]
    nc = D_A // LANES

    def body(av, ag, gb, gc, xi, duc_ref, dyb_ref, ca, cb, dav, dag, dgb, dgc, dxi, dca, dcb, xp, dyp):
        sig = _sigmoid(ag[...])
        _store_front(xp, av[...] * sig, t)
        _store_back(dyp, duc_ref[...], t)
        _conv_bwd_w(duc_ref[...], xp, dca, CONV_A_WIDTH, t)
        du = _conv_bwd_in(dyp, ca, CONV_A_WIDTH, t)
        dav[...] = (du * sig).astype(dav.dtype)
        dag[...] = (du * av[...] * sig * (1.0 - sig)).astype(dag.dtype)
        _store_front(xp, gc[...] * xi[...], t)
        zc = _conv_fwd(xp, cb, CONV_B_WIDTH, t)
        dgb[...] = (dyb_ref[...] * zc).astype(dgb.dtype)
        dzc = dyb_ref[...] * gb[...]
        _conv_bwd_w(dzc, xp, dcb, CONV_B_WIDTH, t)
        _store_back(dyp, dzc, t)
        dz = _conv_bwd_in(dyp, cb, CONV_B_WIDTH, t)
        dgc[...] = (dz * xi[...]).astype(dgc.dtype)
        dxi[...] = (dz * gc[...]).astype(dxi.dtype)

    ins = ([(p, t, k * nc) for k in range(5)] + [(duc, t, 0), (dy, t, nc)]
           + [(conv_a, CONV_A_WIDTH, 0), (conv_b, CONV_B_WIDTH, 0)])
    outs = [(t, D_A, MXU_DTYPE)] * 5 + [(CONV_A_WIDTH, D_A), (CONV_B_WIDTH, D_A)]
    return _col_call(body, name, nc, ins, outs, t, 2)


def _ffn_col_fwd(name, u, conv, bias):
    t = u.shape[0]
    nc = D_FF // LANES

    def body(g_ref, v_ref, cw, b_ref, a_ref, xp):
        _store_front(xp, g_ref[...], t)
        gc = _conv_fwd(xp, cw, CONV_B_WIDTH, t) + b_ref[...]
        a_ref[...] = (gc * _sigmoid(gc) * v_ref[...]).astype(a_ref.dtype)

    ins = [(u, t, 0), (u, t, nc), (conv, CONV_B_WIDTH, 0), (bias, 1, 0)]
    return _col_call(body, name, nc, ins, [(t, D_FF, MXU_DTYPE)], t, 1)[0]


def _ffn_col_bwd(name, u, da, conv, bias):
    t = u.shape[0]
    nc = D_FF // LANES

    def body(g_ref, v_ref, da_ref, cw, b_ref, du_ref, dcw, db_ref, xp, dyp, dval):
        @pl.when(pl.program_id(1) == 0)
        def _():
            _store_front(xp, g_ref[...], t)
            gc = _conv_fwd(xp, cw, CONV_B_WIDTH, t) + b_ref[...]
            s = _sigmoid(gc)
            dval[...] = da_ref[...] * gc * s
            dgc = da_ref[...] * v_ref[...] * (s * (1.0 + gc * (1.0 - s)))
            db_ref[...] = jnp.sum(dgc, axis=0, keepdims=True)
            _conv_bwd_w(dgc, xp, dcw, CONV_B_WIDTH, t)
            _store_back(dyp, dgc, t)
            du_ref[...] = _conv_bwd_in(dyp, cw, CONV_B_WIDTH, t).astype(du_ref.dtype)

        @pl.when(pl.program_id(1) == 1)
        def _():
            du_ref[...] = dval[...].astype(du_ref.dtype)

    col = lambda rows, off: pl.BlockSpec((rows, LANES), lambda j, p: (0, j + off))
    return pl.pallas_call(
        body, name=name, grid=(nc, 2),
        in_specs=[col(t, 0), col(t, nc), col(t, 0), col(CONV_B_WIDTH, 0), col(1, 0)],
        out_specs=[pl.BlockSpec((t, LANES), lambda j, p: (0, j + nc * p)), col(CONV_B_WIDTH, 0), col(1, 0)],
        out_shape=[jax.ShapeDtypeStruct((t, 2 * D_FF), MXU_DTYPE), jax.ShapeDtypeStruct((CONV_B_WIDTH, D_FF), F32),
                   jax.ShapeDtypeStruct((1, D_FF), F32)],
        scratch_shapes=[pltpu.VMEM((t + CONV_PAD, LANES), F32) for _ in range(2)] + [pltpu.VMEM((t, LANES), F32)],
        compiler_params=_cparams(None),
    )(u, u, da, conv, bias)


def _shift_fwd(name, p, col0, mu):
    t = p.shape[0]

    def body(x_ref, mu_ref, o_ref, xp):
        _store_front(xp, x_ref[...], t)
        prev = xp[pl.ds(CONV_PAD - 1, t), :]
        o_ref[...] = x_ref[...] + (prev - x_ref[...]) * mu_ref[...]

    return _col_call(body, name, RWKV_COLS // LANES, [(p, t, col0), (mu, 1, 0)], [(t, RWKV_COLS)], t, 1)[0]


def _shift_bwd(name, p, col0, mu, dprs):
    t = p.shape[0]

    def body(x_ref, mu_ref, d_ref, dx_ref, dmu_ref, xp, dyp):
        _store_front(xp, x_ref[...], t)
        prev = xp[pl.ds(CONV_PAD - 1, t), :]
        dmu_ref[...] = jnp.sum(d_ref[...] * (prev - x_ref[...]), axis=0, keepdims=True)
        dm = d_ref[...] * mu_ref[...]
        _store_back(dyp, dm, t)
        dx_ref[...] = d_ref[...] - dm + dyp[pl.ds(1, t), :]

    ins = [(p, t, col0), (mu, 1, 0), (dprs, t, 0)]
    return _col_call(body, name, RWKV_COLS // LANES, ins, [(t, RWKV_COLS), (1, RWKV_COLS)], t, 2)


def _hi_lo(x):
    hi = x.astype(BF16)
    return hi, (x - hi.astype(F32)).astype(BF16)


def _dot_passes(a, b, dims, passes):
    d = lambda p, q: lax.dot_general(p, q, dims, preferred_element_type=F32)
    if passes == 1:
        return d(a.astype(MXU_DTYPE), b.astype(MXU_DTYPE))
    ah, al = _hi_lo(a)
    bh, bl = _hi_lo(b)
    return d(ah, bh) + (d(ah, bl) + d(al, bh))


@functools.partial(jax.custom_vjp, nondiff_argnums=(2, 3))
def _dot_vjp(a, b, dims, passes):
    return _dot_passes(a, b, dims, passes)


def _dot_fwd(a, b, dims, passes):
    return _dot_passes(a, b, dims, passes), (a, b)


def _dot_bwd(dims, passes, res, g):
    a, b = res
    if dims == _NN:
        return _dot_passes(g, b, _NT, passes), _dot_passes(a, g, _TN, passes)
    if dims == _NT:
        return _dot_passes(g, b, _NN, passes), _dot_passes(g, a, _TN, passes)
    return _dot_passes(b, g, _NT, passes), _dot_passes(a, g, _NN, passes)


_dot_vjp.defvjp(_dot_fwd, _dot_bwd)


def _doth(a, b, dims=_NN):
    return _dot_vjp(a, b, dims, 3)


def _dotb(a, b, dims=_NN):
    return _dot_vjp(a, b, dims, 1)


def _softplus(x):
    return jnp.where(x > 0, x, 0.0) + jnp.log(1.0 + jnp.exp(jnp.where(x > 0, -x, x)))


def _rwkv_pre(k, xl, gd, w0, w2p, a0, a2p, g2, k_k, k_a, seg):
    z = w0 + _dotb(jnp.tanh(xl), w2p)
    lw = -jnp.exp(-_softplus(-z) - 0.5)
    alpha = _sigmoid(a0 + _dotb(xl, a2p))
    g = _dotb(_sigmoid(gd), g2)
    kk = k * k_k
    kk = kk / jnp.maximum(jnp.sqrt(_dotb(kk * kk, seg)), 1e-12)
    k2 = k * (1.0 + (alpha - 1.0) * k_a)
    return lw, k2, -kk, kk * alpha, g


def _rwkv_post(y, r, k2, v, g, lnx_g, lnx_b, r_k, seg):
    mean = _dotb(y, seg) * (1.0 / HEAD_DIM)
    yc = y - mean
    var = _dotb(yc * yc, seg) * (1.0 / HEAD_DIM)
    yo = yc * lax.rsqrt(var + RWKV_GN_EPS) * lnx_g + lnx_b
    bonus = _dotb(r * k2 * r_k, seg) * v
    return (yo + bonus) * g


def _rwkv_pre_fwd(name, prs, prm, seg, tr):
    t = prs.shape[0]
    row = lambda i: (i, 0)
    fix = lambda i: (0, 0)
    ins = [(prs, (tr, D_R), lambda i: (i, 1)), (prs, (tr, LANES), lambda i: (i, 12)), (prs, (tr, LANES), lambda i: (i, 13)),
           (prm["w0"], (1, D_R), fix), (prm["w2p"], (LANES, D_R), fix), (prm["a0"], (1, D_R), fix),
           (prm["a2p"], (LANES, D_R), fix), (prm["g2"], (LANES, D_R), fix), (prm["k_k"], (1, D_R), fix),
           (prm["k_a"], (1, D_R), fix), (seg, (D_R, D_R), fix)]
    return _call(_rwkv_pre, name, (t // tr,), ins, [((t, D_R), (tr, D_R), row, False)] * 5)


def _rwkv_pre_bwd(name, prs, prm, seg, cts, tr):
    t = prs.shape[0]

    def fn(k, xl, gd, w0, w2p, a0, a2p, g2, k_k, k_a, segv, *ct):
        _, vjp = jax.vjp(lambda *a: _rwkv_pre(*a, segv), k, xl, gd, w0, w2p, a0, a2p, g2, k_k, k_a)
        return vjp(tuple(ct))

    row = lambda i: (i, 0)
    fix = lambda i: (0, 0)
    ins = [(prs, (tr, D_R), lambda i: (i, 1)), (prs, (tr, LANES), lambda i: (i, 12)), (prs, (tr, LANES), lambda i: (i, 13)),
           (prm["w0"], (1, D_R), fix), (prm["w2p"], (LANES, D_R), fix), (prm["a0"], (1, D_R), fix),
           (prm["a2p"], (LANES, D_R), fix), (prm["g2"], (LANES, D_R), fix), (prm["k_k"], (1, D_R), fix),
           (prm["k_a"], (1, D_R), fix), (seg, (D_R, D_R), fix)] + [(c, (tr, D_R), row) for c in cts]
    outs = [((t, D_R), (tr, D_R), row, False), ((t, LANES), (tr, LANES), row, False), ((t, LANES), (tr, LANES), row, False),
            ((1, D_R), (1, D_R), fix, True), ((LANES, D_R), (LANES, D_R), fix, True), ((1, D_R), (1, D_R), fix, True),
            ((LANES, D_R), (LANES, D_R), fix, True), ((LANES, D_R), (LANES, D_R), fix, True),
            ((1, D_R), (1, D_R), fix, True), ((1, D_R), (1, D_R), fix, True)]
    return _call(fn, name, (t // tr,), ins, outs, acc_axis=0)


def _rwkv_post_ins(y, prs, k2, g, prm, seg, tr):
    row = lambda i: (i, 0)
    fix = lambda i: (0, 0)
    return [(y, (tr, D_R), row), (prs, (tr, D_R), row), (k2, (tr, D_R), row), (prs, (tr, D_R), lambda i: (i, 2)),
            (g, (tr, D_R), row), (prm["lnx_g"], (1, D_R), fix), (prm["lnx_b"], (1, D_R), fix), (prm["r_k"], (1, D_R), fix),
            (seg, (D_R, D_R), fix)]


def _rwkv_post_fwd(name, y, prs, k2, g, prm, seg, tr):
    t = y.shape[0]
    return _call(_rwkv_post, name, (t // tr,), _rwkv_post_ins(y, prs, k2, g, prm, seg, tr),
                 [((t, D_R), (tr, D_R), lambda i: (i, 0), False)])


def _rwkv_post_bwd(name, y, prs, k2, g, prm, seg, dy, dy_col, tr):
    t = y.shape[0]

    def fn(yv, r, k2v, v, gv, lg, lb, rk, segv, ct):
        _, vjp = jax.vjp(lambda *a: _rwkv_post(*a, segv), yv, r, k2v, v, gv, lg, lb, rk)
        return vjp(ct)

    row = lambda i: (i, 0)
    fix = lambda i: (0, 0)
    ins = _rwkv_post_ins(y, prs, k2, g, prm, seg, tr) + [(dy, (tr, D_R), lambda i: (i, dy_col))]
    outs = [((t, D_R), (tr, D_R), row, False)] * 5 + [((1, D_R), (1, D_R), fix, True)] * 3
    return _call(fn, name, (t // tr,), ins, outs, acc_axis=0)


def _wkv_chunk(s0, r, lw, k, v, a, b):
    c = r[0].shape[0]
    lane = lax.broadcasted_iota(jnp.int32, (1, 2 * HEAD_DIM), 1)
    first = (lane < HEAD_DIM).astype(F32)
    per_head = lambda x: jnp.concatenate([x * first, x * (1.0 - first)], axis=0)

    def time_of(shape, dim):
        i = lax.broadcasted_iota(jnp.int32, shape, dim)
        return jnp.where(i >= c, i - c, i)

    incl = (lax.broadcasted_iota(jnp.int32, (c, c), 0) >= lax.broadcasted_iota(jnp.int32, (c, c), 1)).astype(F32)
    strict2 = time_of((2 * c, 2 * c), 0) > time_of((2 * c, 2 * c), 1)
    incl2 = lax.broadcasted_iota(jnp.int32, (c, 2 * c), 0) >= time_of((c, 2 * c), 1)
    each = lambda f, *xs: [f(*x) for x in zip(*xs)]
    cum = each(lambda x: _doth(incl, x), lw)
    tot = each(lambda x: jnp.sum(x, axis=0, keepdims=True), lw)
    e_inv = each(lambda x: jnp.exp(-x), cum)
    a_st = each(lambda x, cm, l: per_head(x * jnp.exp(cm - l)), a, cum, lw)
    r_t = each(lambda x, cm: x * jnp.exp(cm), r, cum)
    b_st = each(lambda x, e: per_head(x * e), b, e_inv)
    k_st = each(lambda x, e: per_head(x * e), k, e_inv)
    v_st = each(per_head, v)
    m = each(lambda x, w: jnp.where(strict2, _dotb(x, w, _NT), 0.0), a_st, b_st)
    m_k = each(lambda x, w: jnp.where(strict2, _dotb(x, w, _NT), 0.0), a_st, k_st)
    u = each(lambda x, s, mk, w: _dotb(x, s, _NT) + _dotb(mk, w), a_st, s0, m_k, v_st)
    steps = (c - 1).bit_length()
    for s in range(steps):
        u = each(lambda x, w: x + _dotb(w, x), u, m)
        if s + 1 < steps:
            m = each(lambda w: _dotb(w, w), m)
    n_b = each(lambda x, w: jnp.where(incl2, _dotb(x, w, _NT), 0.0), r_t, b_st)
    n_k = each(lambda x, w: jnp.where(incl2, _dotb(x, w, _NT), 0.0), r_t, k_st)
    y = each(lambda x, s, nb, uu, nk, w: _dotb(x, s, _NT) + _dotb(nb, uu) + _dotb(nk, w), r_t, s0, n_b, u, n_k, v_st)
    dec = each(lambda tt, cm: jnp.exp(tt - cm), tot, cum)
    s1 = each(lambda s, tt, uu, x, d, w, kk: s * jnp.exp(tt) + _dotb(uu, per_head(x * d), _TN) + _dotb(w, per_head(kk * d), _TN),
              s0, tot, u, b, dec, v_st, k)
    return tuple(y), tuple(s1)


WKV_PAIRS_PER_STEP = 4
PAIR = 2 * HEAD_DIM


def _wkv_fwd(name, srcs):
    t = srcs[0][0].shape[0]
    c = _chunk_len(t)
    nc = t // c
    pp = WKV_PAIRS_PER_STEP
    n_pairs = D_R // PAIR

    def body(r, lw, k, v, a, b, y_ref, st_ref, state):
        @pl.when(pl.program_id(1) == 0)
        def _():
            state[...] = jnp.zeros(state.shape, F32)

        pairs = lambda ref: tuple(ref[:, pl.ds(i * PAIR, PAIR)] for i in range(pp))
        s0 = tuple(state[i] for i in range(pp))
        y, s1 = _wkv_chunk(s0, pairs(r), pairs(lw), pairs(k), pairs(v), pairs(a), pairs(b))
        for i in range(pp):
            st_ref[i] = s0[i]
            y_ref[:, pl.ds(i * PAIR, PAIR)] = y[i]
            state[i] = s1[i]

    seq = lambda off: pl.BlockSpec((c, pp * PAIR), lambda g, j: (j, off + g))
    return pl.pallas_call(
        body, name=name, grid=(n_pairs // pp, nc), in_specs=[seq(off) for _, off in srcs],
        out_specs=[seq(0), pl.BlockSpec((pp, None, PAIR, PAIR), lambda g, j: (g, j, 0, 0))],
        out_shape=[jax.ShapeDtypeStruct((t, D_R), F32), jax.ShapeDtypeStruct((n_pairs, nc, PAIR, PAIR), F32)],
        scratch_shapes=[pltpu.VMEM((pp, PAIR, PAIR), F32)],
        compiler_params=_cparams(None),
    )(*[a for a, _ in srcs])


def _wkv_bwd(name, srcs, st, dy):
    t = srcs[0][0].shape[0]
    c = _chunk_len(t)
    nc = t // c
    pp = WKV_PAIRS_PER_STEP
    n_pairs = D_R // PAIR

    def body(r, lw, k, v, a, b, st_ref, dy_ref, dr, dlw, dk, dv, da, db, dstate):
        @pl.when(pl.program_id(1) == 0)
        def _():
            dstate[...] = jnp.zeros(dstate.shape, F32)

        half = lax.broadcasted_iota(jnp.int32, (PAIR, PAIR), 0) < HEAD_DIM
        same_head = half == (lax.broadcasted_iota(jnp.int32, (PAIR, PAIR), 1) < HEAD_DIM)
        pairs = lambda ref: tuple(ref[:, pl.ds(i * PAIR, PAIR)] for i in range(pp))
        s0 = tuple(st_ref[i] for i in range(pp))
        _, vjp = jax.vjp(_wkv_chunk, s0, pairs(r), pairs(lw), pairs(k), pairs(v), pairs(a), pairs(b))
        ds0, *dxs = vjp((pairs(dy_ref), tuple(dstate[i] for i in range(pp))))
        for i in range(pp):
            for ref, val in zip((dr, dlw, dk, dv, da, db), dxs):
                ref[:, pl.ds(i * PAIR, PAIR)] = val[i]
            dstate[i] = jnp.where(same_head, ds0[i], 0.0)

    seq = lambda off: pl.BlockSpec((c, pp * PAIR), lambda g, j: (nc - 1 - j, off + g))
    return pl.pallas_call(
        body, name=name, grid=(n_pairs // pp, nc),
        in_specs=[seq(off) for _, off in srcs]
        + [pl.BlockSpec((pp, None, PAIR, PAIR), lambda g, j: (g, nc - 1 - j, 0, 0)), seq(dy[1])],
        out_specs=[seq(0)] * 6,
        out_shape=[jax.ShapeDtypeStruct((t, D_R), F32)] * 6,
        scratch_shapes=[pltpu.VMEM((pp, PAIR, PAIR), F32)],
        compiler_params=_cparams(None),
    )(*[a for a, _ in srcs], st, dy[0])


def _rope(x, cos, sin, rot):
    return x * cos + _dotb(x, rot) * sin


def _attn_block(nb, q, kp, kc, km, vp, vc, vm, sk, cq, sq, cp, sp, cm, sm, rot):
    g = GQA_GROUP
    scale = HEAD_DIM ** -0.5
    down = lambda x: jnp.concatenate([x] * g, axis=0)
    kpr = _rope(kp, cp, sp, rot)
    kcr = _rope(kc, cq, sq, rot)
    kmr = _rope(km, cm, sm, rot)
    qr = _rope(q, down(cq), down(sq), rot)
    i = lax.broadcasted_iota(jnp.int32, (g * BLOCK, BLOCK), 0)
    i = i - BLOCK * ((i >= BLOCK).astype(jnp.int32) + (i >= 2 * BLOCK).astype(jnp.int32) + (i >= 3 * BLOCK).astype(jnp.int32))
    j = lax.broadcasted_iota(jnp.int32, (g * BLOCK, BLOCK), 1)
    nbv = jnp.zeros((g * BLOCK, BLOCK), jnp.int32) + nb
    ok_p = (j > i) & (nbv >= 2)
    ok_c = (j <= i) & (nbv >= 1)
    ok_m = (j >= BLOCK - N_META) & ((nbv >= 1) | (j <= i))
    sink = jnp.concatenate([jnp.broadcast_to(s, (BLOCK, 1)) for s in sk], axis=0)
    s_p = jnp.where(ok_p, _dotb(qr, kpr, _NT) * scale, NEG_INF)
    s_c = jnp.where(ok_c, _dotb(qr, kcr, _NT) * scale, NEG_INF)
    s_m = jnp.where(ok_m, _dotb(qr, kmr, _NT) * scale, NEG_INF)
    rmax = lambda s: jnp.max(s, axis=-1, keepdims=True)
    m = lax.stop_gradient(jnp.maximum(jnp.maximum(rmax(s_p), rmax(s_c)), jnp.maximum(rmax(s_m), sink)))
    e_p, e_c, e_m = jnp.exp(s_p - m), jnp.exp(s_c - m), jnp.exp(s_m - m)
    rsum = lambda e: jnp.sum(e, axis=-1, keepdims=True)
    inv = 1.0 / (rsum(e_p) + rsum(e_c) + rsum(e_m) + jnp.exp(sink - m))
    return _dotb(e_p * inv, vp) + _dotb(e_c * inv, vc) + _dotb(e_m * inv, vm)


def _attn_specs():
    cur = lambda g, n: (g, n, 0)
    prev = lambda g, n: (g, jnp.maximum(n - 1, 0), 0)
    meta = lambda g, n: (g, 0, 0)
    kv = lambda m: pl.BlockSpec((None, BLOCK, HEAD_DIM), m)
    tab = lambda m: pl.BlockSpec((BLOCK, HEAD_DIM), m)
    tcur, tprev, tmeta = (lambda g, n: (n, 0)), (lambda g, n: (jnp.maximum(n - 1, 0), 0)), (lambda g, n: (0, 0))
    qspec = pl.BlockSpec((GQA_GROUP, BLOCK, HEAD_DIM), cur)
    sspec = pl.BlockSpec((GQA_GROUP, 8, LANES), meta)
    specs = [qspec, kv(prev), kv(cur), kv(meta), kv(prev), kv(cur), kv(meta), sspec,
             tab(tcur), tab(tcur), tab(tprev), tab(tprev), tab(tmeta), tab(tmeta),
             pl.BlockSpec((HEAD_DIM, HEAD_DIM), lambda g, n: (0, 0))]
    return specs, qspec, sspec, kv


def _attn_args(q, k, v, sinks_b, cos, sin, rot):
    return (q, k, k, k, v, v, v, sinks_b, cos, sin, cos, sin, cos, sin, rot)


def _attn_fwd(name, q, k, v, sinks_b, cos, sin, rot):
    tp = q.shape[1]
    specs, qspec, _, _ = _attn_specs()

    def body(q_ref, kp, kc, km, vp, vc, vm, s_ref, cq, sq, cp, sp, cm, sm, rot_ref, o_ref):
        q = jnp.concatenate([q_ref[h] for h in range(GQA_GROUP)], axis=0)
        sk = tuple(s_ref[h][0:1, 0:1] for h in range(GQA_GROUP))
        out = _attn_block(pl.program_id(1), q, kp[...], kc[...], km[...], vp[...], vc[...], vm[...], sk,
                          cq[...], sq[...], cp[...], sp[...], cm[...], sm[...], rot_ref[...])
        for h in range(GQA_GROUP):
            o_ref[h] = out[h * BLOCK:(h + 1) * BLOCK]

    return pl.pallas_call(
        body, name=name, grid=(N_KV_HEADS, tp // BLOCK), in_specs=specs, out_specs=qspec,
        out_shape=jax.ShapeDtypeStruct(q.shape, F32), compiler_params=_cparams(None),
    )(*_attn_args(q, k, v, sinks_b, cos, sin, rot))


def _attn_bwd(name, q, k, v, sinks_b, cos, sin, rot, do):
    tp = q.shape[1]
    nb = tp // BLOCK
    specs, qspec, sspec, kv = _attn_specs()

    def body(q_ref, kp, kc, km, vp, vc, vm, s_ref, cq, sq, cp, sp, cm, sm, rot_ref, do_ref,
             dq_ref, dkp, dkc, dvp, dvc, dkm, dvm, ds_ref):
        n = pl.program_id(1)
        q = jnp.concatenate([q_ref[h] for h in range(GQA_GROUP)], axis=0)
        sk = tuple(s_ref[h][0:1, 0:1] for h in range(GQA_GROUP))
        tabs = (cq[...], sq[...], cp[...], sp[...], cm[...], sm[...], rot_ref[...])
        _, vjp = jax.vjp(lambda *a: _attn_block(n, *a, *tabs), q, kp[...], kc[...], km[...], vp[...], vc[...], vm[...], sk)
        dq, gkp, gkc, gkm, gvp, gvc, gvm, dsk = vjp(jnp.concatenate([do_ref[h] for h in range(GQA_GROUP)], axis=0))
        dkp[...] = gkp
        dkc[...] = gkc
        dvp[...] = gvp
        dvc[...] = gvc
        for h in range(GQA_GROUP):
            dq_ref[h] = dq[h * BLOCK:(h + 1) * BLOCK]

        @pl.when(n == 0)
        def _():
            dkm[...] = gkm
            dvm[...] = gvm
            for h in range(GQA_GROUP):
                ds_ref[h] = jnp.broadcast_to(dsk[h], (8, LANES))

        @pl.when(n != 0)
        def _():
            dkm[...] += gkm
            dvm[...] += gvm
            for h in range(GQA_GROUP):
                ds_ref[h] += jnp.broadcast_to(dsk[h], (8, LANES))

    part = pl.BlockSpec((None, None, BLOCK, HEAD_DIM), lambda g, n: (g, n, 0, 0))
    part_shape = jax.ShapeDtypeStruct((N_KV_HEADS, nb, BLOCK, HEAD_DIM), F32)
    meta_shape = jax.ShapeDtypeStruct((N_KV_HEADS, BLOCK, HEAD_DIM), F32)
    return pl.pallas_call(
        body, name=name, grid=(N_KV_HEADS, nb), in_specs=specs + [qspec],
        out_specs=[qspec, part, part, part, part, kv(lambda g, n: (g, 0, 0)), kv(lambda g, n: (g, 0, 0)), sspec],
        out_shape=[jax.ShapeDtypeStruct(q.shape, F32), part_shape, part_shape, part_shape, part_shape,
                   meta_shape, meta_shape, jax.ShapeDtypeStruct(sinks_b.shape, F32)],
        compiler_params=_cparams(None),
    )(*_attn_args(q, k, v, sinks_b, cos, sin, rot), do)


def _kv_combine(name, prev_part, own_part, meta):
    g, nb = own_part.shape[:2]

    def fn(own, nxt, mt):
        m = pl.program_id(1)
        one = jnp.ones((BLOCK, HEAD_DIM), F32)
        use_next = jnp.where(one * m < nb - 1, 1.0, 0.0)
        use_meta = jnp.where(one * m < 1, 1.0, 0.0)
        return own + nxt * use_next + mt * use_meta

    blk = (None, None, BLOCK, HEAD_DIM)
    return _call(fn, name, (g, nb),
                 [(own_part, blk, lambda a, m: (a, m, 0, 0)),
                  (prev_part, blk, lambda a, m: (a, jnp.minimum(m + 1, nb - 1), 0, 0)),
                  (meta, (None, BLOCK, HEAD_DIM), lambda a, m: (a, 0, 0))],
                 [((g, nb * BLOCK, HEAD_DIM), (None, BLOCK, HEAD_DIM), lambda a, m: (a, m, 0), False)])


PACK_W = 1024
ELEMENTWISE_BLOCK_BYTES = 1 << 21


def _rows_tile(rows, cols):
    cap = max(8, ELEMENTWISE_BLOCK_BYTES // (4 * cols))
    for d in range(min(rows, cap), 0, -1):
        if rows % d == 0 and d % 8 == 0:
            return d
    return rows


def _adamw(name, w, g, m, v):
    rows, cols = w.shape
    tr = _rows_tile(rows, cols)

    def fn(wv, gv, mv, vv):
        m1 = ADAM_B1 * mv + (1.0 - ADAM_B1) * gv
        v1 = ADAM_B2 * vv + (1.0 - ADAM_B2) * (gv * gv)
        m_hat = m1 / (1.0 - ADAM_B1 ** ADAM_STEP)
        v_hat = v1 / (1.0 - ADAM_B2 ** ADAM_STEP)
        return -ADAM_LR * (m_hat / (jnp.sqrt(v_hat) + ADAM_EPS) + ADAM_WD * wv), m1, v1

    blk = (tr, cols)
    row = lambda i: (i, 0)
    return _call(fn, name, (rows // tr,), [(a, blk, row) for a in (w, g, m, v)], [((rows, cols), blk, row, False)] * 3)


def _pair_add(name, g, recv, c_idx, out_dtype):
    s, a, b = g.shape
    half = a // 2

    def body(c_ref, a_ref, b_ref, o_ref):
        o_ref[...] = (a_ref[...] + b_ref[...]).astype(out_dtype)

    blk = (None, half, b)
    return pl.pallas_call(
        body, name=name,
        grid_spec=pltpu.PrefetchScalarGridSpec(
            num_scalar_prefetch=1, grid=(s,),
            in_specs=[pl.BlockSpec(blk, lambda j, c: (j, c[0], 0)), pl.BlockSpec(blk, lambda j, c: (j, 0, 0))],
            out_specs=pl.BlockSpec(blk, lambda j, c: (j, 0, 0))),
        out_shape=jax.ShapeDtypeStruct((s, half, b), out_dtype), compiler_params=_cparams(None),
    )(c_idx, g, recv)


def _pair_add_placed(name, g, recv, cm_idx, out_dtype):
    s, a, b = g.shape
    half = a // 2

    def body(cm_ref, a_ref, b_ref, o_ref, own_ref):
        val = (a_ref[...] + b_ref[...]).astype(out_dtype)
        o_ref[...] = val

        @pl.when(pl.program_id(0) == cm_ref[1])
        def _():
            own_ref[...] = val

    blk = (None, half, b)
    shape = jax.ShapeDtypeStruct((s, half, b), out_dtype)
    return pl.pallas_call(
        body, name=name,
        grid_spec=pltpu.PrefetchScalarGridSpec(
            num_scalar_prefetch=1, grid=(s,),
            in_specs=[pl.BlockSpec(blk, lambda j, cm: (j, cm[0], 0)), pl.BlockSpec(blk, lambda j, cm: (j, 0, 0))],
            out_specs=[pl.BlockSpec(blk, lambda j, cm: (j, 0, 0)), pl.BlockSpec(blk, lambda j, cm: (cm[1], 0, 0))]),
        out_shape=[shape, shape], compiler_params=_cparams(None),
    )(cm_idx, g, recv)


def _sum_chips(name, parts, c_idx, layer, n_layers, into=None):
    _, a, b = parts.shape
    tr = _rows_tile(a, b)

    def body(c_ref, p0, p1, p2, p3, *rest):
        o_ref = rest[-1]
        up = lambda p: p[...].astype(F32)
        o_ref[...] = ((up(p0) + up(p1)) + up(p2)) + up(p3)

    in_specs = [pl.BlockSpec((None, tr, b), lambda i, c, k=k: (k, i, 0)) for k in range(N_CHIPS)]
    args = [c_idx] + [parts] * N_CHIPS
    aliases = {}
    if into is not None:
        in_specs.append(_ANY)
        args.append(into)
        aliases = {1 + N_CHIPS: 0}
    return pl.pallas_call(
        body, name=name,
        grid_spec=pltpu.PrefetchScalarGridSpec(
            num_scalar_prefetch=1, grid=(a // tr,), in_specs=in_specs,
            out_specs=pl.BlockSpec((None, None, tr, b), lambda i, c: (layer, c[0], i, 0))),
        out_shape=jax.ShapeDtypeStruct((n_layers, 2, a, b), F32), input_output_aliases=aliases,
        compiler_params=_cparams(None),
    )(*args)


def _place_own_block(name, w, layer, me_idx, dtype):
    _, a2, b = w.shape
    a = a2 // 2
    tr = _rows_tile(a, b)
    nb = a // tr

    def body(me_ref, w_ref, o_ref):
        o_ref[...] = w_ref[...].astype(dtype)

    return pl.pallas_call(
        body, name=name,
        grid_spec=pltpu.PrefetchScalarGridSpec(
            num_scalar_prefetch=1, grid=(2, nb),
            in_specs=[pl.BlockSpec((None, tr, b), lambda h, i, me: (layer, h * nb + i, 0))],
            out_specs=pl.BlockSpec((None, None, tr, b), lambda h, i, me: (me[0], h, i, 0))),
        out_shape=jax.ShapeDtypeStruct((N_CHIPS, 2, a, b), dtype), compiler_params=_cparams(None),
    )(me_idx, w)


def _mesh_pos():
    return lax.axis_index("x"), lax.axis_index("y"), lax.axis_index("c")


def _other_chips(x, y):
    return [(1 - x, y), (x, 1 - y), (1 - x, 1 - y)]


_ANY = pl.BlockSpec(memory_space=pl.ANY)


def _gather_weights(name, bufs, from_chips=True):
    n = len(bufs)

    def body(*refs):
        out_refs = refs[n:2 * n]
        send_sems, recv_sems = refs[2 * n:]
        x, y, c = _mesh_pos()
        me = 2 * x + y
        sibling = (x, y, 1 - c)
        chips = _other_chips(x, y)

        def copy(i, k, chip_idx, half, to):
            return pltpu.make_async_remote_copy(src_ref=out_refs[i].at[chip_idx, half], dst_ref=out_refs[i].at[chip_idx, half],
                                                send_sem=send_sems.at[6 * i + k], recv_sem=recv_sems.at[6 * i + k],
                                                device_id=to, device_id_type=MESH)

        first = [copy(i, j, me, c, (*chip, c)) for i in range(n) for j, chip in enumerate(chips)] if from_chips else []
        for cp in first:
            cp.start()
        passed = []
        for i in range(n):
            for j, (cx, cy) in enumerate(chips):
                idx = 2 * cx + cy
                if from_chips:
                    copy(i, j, idx, c, sibling).wait_recv()
                fwd = copy(i, 3 + j, idx, c, sibling)
                fwd.start()
                passed.append(fwd)
        for i in range(n):
            for j, (cx, cy) in enumerate(chips):
                copy(i, 3 + j, 2 * cx + cy, 1 - c, sibling).wait_recv()
        for cp in first + passed:
            cp.wait_send()

    return pl.pallas_call(
        body, name=name, in_specs=[_ANY] * n, out_specs=[_ANY] * n,
        out_shape=[jax.ShapeDtypeStruct(b.shape, b.dtype) for b in bufs],
        input_output_aliases={i: i for i in range(n)},
        scratch_shapes=[pltpu.SemaphoreType.DMA((6 * n,)), pltpu.SemaphoreType.DMA((6 * n,))],
        compiler_params=pltpu.CompilerParams(has_side_effects=True),
    )(*bufs)


def _gather_start(name, groups):
    bufs = [b for g in groups for b in g]
    n = len(bufs)
    ng = len(groups)

    def body(*refs):
        b_refs = refs[:n]
        sems = refs[n:n + 2 * ng]
        token = refs[-1]
        x, y, c = _mesh_pos()
        me = 2 * x + y
        i = 0
        for gi, g in enumerate(groups):
            for k in range(len(g)):
                for j, (cx, cy) in enumerate(_other_chips(x, y)):
                    pltpu.make_async_remote_copy(src_ref=b_refs[i].at[me, c], dst_ref=b_refs[i].at[me, c],
                                                 send_sem=sems[2 * gi].at[3 * k + j], recv_sem=sems[2 * gi + 1].at[3 * k + j],
                                                 device_id=(cx, cy, c), device_id_type=MESH).start()
                i += 1
        token[...] = jnp.zeros(token.shape, F32)

    sem_shapes = [pltpu.SemaphoreType.DMA((3 * len(g),)) for g in groups for _ in range(2)]
    res = pl.pallas_call(
        body, name=name,
        out_shape=(*sem_shapes, *[pltpu.HBM(b.shape, b.dtype) for b in bufs], jax.ShapeDtypeStruct((8, LANES), F32)),
        in_specs=[_HBM] * n,
        out_specs=(*[_SEM] * (2 * ng), *[_HBM] * n, pl.BlockSpec(memory_space=pltpu.VMEM)),
        input_output_aliases={i: 2 * ng + i for i in range(n)},
        compiler_params=pltpu.CompilerParams(has_side_effects=_DATAFLOW),
    )(*[pltpu.with_memory_space_constraint(b, pltpu.HBM) for b in bufs])
    out, i = [], 2 * ng
    for gi, g in enumerate(groups):
        out.append((res[2 * gi], res[2 * gi + 1], list(res[i:i + len(g)])))
        i += len(g)
    return out, res[-1]


def _gather_wait(name, send_sems, recv_sems, bufs, after):
    n = len(bufs)

    def body(*refs):
        b_refs = refs[:n]
        s_sems, r_sems = refs[n], refs[n + 1]
        x, y, c = _mesh_pos()
        me = 2 * x + y
        for k in range(n):
            for j, (cx, cy) in enumerate(_other_chips(x, y)):
                idx = 2 * cx + cy
                copy = pltpu.make_async_remote_copy(src_ref=b_refs[k].at[me, c], dst_ref=b_refs[k].at[idx, c],
                                                    send_sem=s_sems.at[3 * k + j], recv_sem=r_sems.at[3 * k + j],
                                                    device_id=(cx, cy, c), device_id_type=MESH)
                copy.wait_send()
                copy.wait_recv()

    res = pl.pallas_call(
        body, name=name,
        out_shape=tuple(pltpu.HBM(b.shape, b.dtype) for b in bufs),
        in_specs=[_HBM] * n + [_SEM, _SEM, _ANY],
        out_specs=tuple([_HBM] * n),
        input_output_aliases={i: i for i in range(n)},
        compiler_params=pltpu.CompilerParams(has_side_effects=_DATAFLOW),
    )(*bufs, send_sems, recv_sems, after)
    return list(res)


def _halves_to_sibling(name, units):
    n = len(units)

    def body(*refs):
        g_refs, out_refs = refs[:n], refs[n:2 * n]
        send_sems, recv_sems = refs[2 * n:]
        x, y, c = _mesh_pos()
        cps = []
        for i in range(n):
            half = units[i].shape[1] // 2
            src = g_refs[i].at[pl.ds(0, N_CHIPS), pl.ds((1 - c) * half, half)]
            cp = pltpu.make_async_remote_copy(src_ref=src, dst_ref=out_refs[i], send_sem=send_sems.at[i],
                                              recv_sem=recv_sems.at[i], device_id=(x, y, 1 - c), device_id_type=MESH)
            cp.start()
            cps.append(cp)
        for cp in cps:
            cp.wait()

    return pl.pallas_call(
        body, name=name, in_specs=[_ANY] * n, out_specs=[_ANY] * n,
        out_shape=[jax.ShapeDtypeStruct((u.shape[0], u.shape[1] // 2, u.shape[2]), u.dtype) for u in units],
        scratch_shapes=[pltpu.SemaphoreType.DMA((n,)), pltpu.SemaphoreType.DMA((n,))],
        compiler_params=pltpu.CompilerParams(has_side_effects=True),
    )(*units)


def _scatter_to_chips(name, sums):
    n = len(sums)

    def body(*refs):
        h_refs, out_refs = refs[:n], refs[n:2 * n]
        send_sems, recv_sems, local_sems = refs[2 * n:]
        x, y, c = _mesh_pos()
        me = 2 * x + y
        chips = _other_chips(x, y)
        local = [pltpu.make_async_copy(h_refs[i].at[me], out_refs[i].at[me], local_sems.at[i]) for i in range(n)]
        for cp in local:
            cp.start()

        def copy(i, j, src_idx, dst_idx):
            cx, cy = chips[j]
            return pltpu.make_async_remote_copy(src_ref=h_refs[i].at[src_idx], dst_ref=out_refs[i].at[dst_idx],
                                                send_sem=send_sems.at[3 * i + j], recv_sem=recv_sems.at[3 * i + j],
                                                device_id=(cx, cy, c), device_id_type=MESH)

        cps = [copy(i, j, 2 * chips[j][0] + chips[j][1], me) for i in range(n) for j in range(3)]
        for cp in cps:
            cp.start()
        for i in range(n):
            for j in range(3):
                copy(i, j, me, 2 * chips[j][0] + chips[j][1]).wait_recv()
        for cp in cps:
            cp.wait_send()
        for cp in local:
            cp.wait()

    return pl.pallas_call(
        body, name=name, in_specs=[_ANY] * n, out_specs=[_ANY] * n,
        out_shape=[jax.ShapeDtypeStruct(s.shape, s.dtype) for s in sums],
        scratch_shapes=[pltpu.SemaphoreType.DMA((3 * n,)), pltpu.SemaphoreType.DMA((3 * n,)), pltpu.SemaphoreType.DMA((n,))],
        compiler_params=pltpu.CompilerParams(has_side_effects=True),
    )(*sums)


_HBM = pl.BlockSpec(memory_space=pltpu.HBM)
_SEM = pl.BlockSpec(memory_space=pltpu.SEMAPHORE)
_DATAFLOW = pltpu.SideEffectType.DATAFLOW_SIDE_EFFECTING


def _scatter_start(name, sums, zones):
    n = len(sums)

    def body(*refs):
        h_refs, z_refs = refs[:n], refs[n:2 * n]
        send_sems, recv_sems = refs[2 * n], refs[2 * n + 1]
        token = refs[-1]
        x, y, c = _mesh_pos()
        me = 2 * x + y
        for i in range(n):
            for j, (cx, cy) in enumerate(_other_chips(x, y)):
                pltpu.make_async_remote_copy(src_ref=h_refs[i].at[2 * cx + cy], dst_ref=z_refs[i].at[me],
                                             send_sem=send_sems.at[3 * i + j], recv_sem=recv_sems.at[3 * i + j],
                                             device_id=(cx, cy, c), device_id_type=MESH).start()
        token[...] = jnp.zeros(token.shape, F32)

    hbm = lambda a: pltpu.HBM(a.shape, a.dtype)
    res = pl.pallas_call(
        body, name=name,
        out_shape=(pltpu.SemaphoreType.DMA((3 * n,)), pltpu.SemaphoreType.DMA((3 * n,)),
                   *[hbm(a) for a in sums], *[hbm(a) for a in zones], jax.ShapeDtypeStruct((8, LANES), F32)),
        in_specs=[_HBM] * (2 * n),
        out_specs=(_SEM, _SEM, *[_HBM] * (2 * n), pl.BlockSpec(memory_space=pltpu.VMEM)),
        input_output_aliases={i: 2 + i for i in range(2 * n)},
        compiler_params=pltpu.CompilerParams(has_side_effects=_DATAFLOW),
    )(*[pltpu.with_memory_space_constraint(a, pltpu.HBM) for a in list(sums) + list(zones)])
    return res[0], res[1], res[2:2 + n], res[2 + n:2 + 2 * n], res[-1]


def _scatter_wait(name, send_sems, recv_sems, sums, zones, after):
    n = len(sums)

    def body(*refs):
        h_refs, z_refs = refs[:n], refs[n:2 * n]
        s_sems, r_sems = refs[2 * n], refs[2 * n + 1]
        x, y, c = _mesh_pos()
        me = 2 * x + y
        for i in range(n):
            for j, (cx, cy) in enumerate(_other_chips(x, y)):
                idx = 2 * cx + cy
                copy = pltpu.make_async_remote_copy(src_ref=h_refs[i].at[idx], dst_ref=z_refs[i].at[idx],
                                                    send_sem=s_sems.at[3 * i + j], recv_sem=r_sems.at[3 * i + j],
                                                    device_id=(cx, cy, c), device_id_type=MESH)
                copy.wait_send()
                copy.wait_recv()

    hbm = lambda a: pltpu.HBM(a.shape, a.dtype)
    res = pl.pallas_call(
        body, name=name,
        out_shape=(*[hbm(a) for a in sums], *[hbm(a) for a in zones]),
        in_specs=[_HBM] * (2 * n) + [_SEM, _SEM, _ANY],
        out_specs=tuple([_HBM] * (2 * n)),
        input_output_aliases={i: i for i in range(2 * n)},
        compiler_params=pltpu.CompilerParams(has_side_effects=_DATAFLOW),
    )(*sums, *zones, send_sems, recv_sems, after)
    return res[n:]


def _join_halves(name, results):
    n = len(results)
    pieces = [(i, l) for i in range(n) for l in range(results[i].shape[0])]

    def body(*refs):
        out_refs = refs[n:2 * n]
        send_sems, recv_sems = refs[2 * n:]
        x, y, c = _mesh_pos()

        def copy(k, half):
            i, l = pieces[k]
            return pltpu.make_async_remote_copy(src_ref=out_refs[i].at[l, half], dst_ref=out_refs[i].at[l, half],
                                                send_sem=send_sems.at[k], recv_sem=recv_sems.at[k],
                                                device_id=(x, y, 1 - c), device_id_type=MESH)

        cps = [copy(k, c) for k in range(len(pieces))]
        for cp in cps:
            cp.start()
        for k in range(len(pieces)):
            copy(k, 1 - c).wait_recv()
        for cp in cps:
            cp.wait_send()

    return pl.pallas_call(
        body, name=name, in_specs=[_ANY] * n, out_specs=[_ANY] * n,
        out_shape=[jax.ShapeDtypeStruct(r.shape, r.dtype) for r in results],
        input_output_aliases={i: i for i in range(n)},
        scratch_shapes=[pltpu.SemaphoreType.DMA((len(pieces),)), pltpu.SemaphoreType.DMA((len(pieces),))],
        compiler_params=pltpu.CompilerParams(has_side_effects=True),
    )(*results)


def _pack(arrays, dtype, rows_multiple):
    flat = jnp.concatenate([a.reshape(-1).astype(dtype) for a in arrays])
    unit = rows_multiple * PACK_W
    total = -(-flat.shape[0] // unit) * unit
    return jnp.pad(flat, (0, total - flat.shape[0])).reshape(total // PACK_W, PACK_W)


def _unpack(flat, shapes):
    out, off = [], 0
    for s in shapes:
        n = 1
        for d in s:
            n *= d
        out.append(flat[..., off:off + n].reshape(flat.shape[:-1] + tuple(s)))
        off += n
    return out


def _ffn_fwd(tag, l, h, g, w_up, conv, bias, w_down, tm):
    hn = _rms_fwd(f"{tag}_norm", h, g, tm)
    u = _mm_cs(f"{tag}_up", hn, w_up, l, tm)
    act = _ffn_col_fwd(f"{tag}_glu", u, conv, bias)
    h_out = _mm_full(f"{tag}_down", act, w_down, l, tm, D_FF // 2, add=h)
    return h_out, (hn, u, act)


def _ffn_bwd(tag, l, h, g, w_up, conv, bias, w_down, saved, dh, tm):
    hn, u, act = saved
    da = _mm_nt_full(f"{tag}_down_dx", dh, w_down, l, tm, D_FF // 2)
    dw_down = _mm_tn_full(f"{tag}_down_dw", act, dh, tm, D_FF // 2)
    du, dconv, dbias = _ffn_col_bwd(f"{tag}_glu_bwd", u, da, conv, bias)
    dw_up = _mm_tn_cs(f"{tag}_up_dw", hn, du, N_CHIPS, tm)
    dhn = _mm_nt_cs(f"{tag}_up_dx", du, w_up, l, tm)
    dh, dg = _rms_bwd(f"{tag}_norm_bwd", h, g, dhn, dh, tm)
    return dh, dict(norm=dg, w_up=dw_up, conv=dconv, bias=dbias, w_down=dw_down)


def _to_heads(z, nh, pad):
    t = z.shape[0]
    return jnp.pad(z.reshape(t, nh, HEAD_DIM).transpose(1, 0, 2), ((0, 0), (pad, 0), (0, 0)))


def _from_heads(z, pad):
    nh, tp, _ = z.shape
    return z[:, pad:].transpose(1, 0, 2).reshape(tp - pad, nh * HEAD_DIM)


def _rope_tables(tp, pad):
    half = HEAD_DIM // 2
    inv = ROPE_THETA ** (-jnp.arange(half, dtype=F32) / half)
    ang = (jnp.arange(tp, dtype=F32) - pad)[:, None] * inv[None, :]
    cos, sin = jnp.cos(ang), jnp.sin(ang)
    rot = jnp.zeros((HEAD_DIM, HEAD_DIM), F32)
    idx = jnp.arange(half)
    rot = rot.at[idx + half, idx].set(-1.0).at[idx, idx + half].set(1.0)
    return jnp.concatenate([cos, cos], axis=1), jnp.concatenate([sin, sin], axis=1), rot


def _local_step(x, tgt, w, on_grads=None, fetch=None):
    emit = on_grads if on_grads is not None else (lambda tag, units: 0.0)
    need = (lambda tag, after: w) if fetch is None else (lambda tag, after: {**w, **fetch(tag, after)})
    seq = x.shape[0]
    t = seq + N_META
    tm = _row_tile(t, 704)
    tr = _row_tile(t, 352)
    pad = BLOCK - N_META
    grads = {}

    h0 = jnp.concatenate([w["meta_tokens"], x], axis=0)
    tgt_p = jnp.pad(tgt, ((N_META, 0), (0, 0)))

    hn0 = _rms_fwd("l0_norm", h0, w["norm_mix"][0:1], tm)
    p0 = _mm_cs("l0_in", hn0, w["ev_w_in"], 0, tm)
    uc, yb = _even_col_fwd("l0_convs", p0, w["ev_conv_a"], w["ev_conv_b"])
    ya = _even_ln_fwd("l0_ln", uc, w["ev_ln_a_g"], w["ev_ln_a_b"], tm)
    y0 = jnp.concatenate([ya, yb], axis=1)
    w = need("ev_out", y0)
    h1 = _mm_full("l0_out", y0, w["ev_w_out"], 0, tm, D_MODEL, add=h0)
    w = need("f0", h1)
    f0 = (0, h1, w["norm_ffn"][0:1], w["ff_w_up0"], w["ff_conv"][0], w["ff_conv_b"][0:1], w["ff_w_down0"])
    h2, ffn0 = _ffn_fwd("f0", *f0, tm)
    w = need("od", h2)

    hn2 = _rms_fwd("l1_norm", h2, w["norm_mix"][1:2], tm)
    p1 = _mm_cs("l1_in", hn2, w["od_w_in"], 0, tm)
    cos, sin, rot = _rope_tables(t + pad, pad)
    qh = _to_heads(p1[:, :D_ATT], N_Q_HEADS, pad)
    kh = _to_heads(p1[:, D_ATT:D_ATT + D_KV], N_KV_HEADS, pad)
    vh = _to_heads(p1[:, D_ATT + D_KV:D_ATT + 2 * D_KV], N_KV_HEADS, pad)
    sinks_b = jnp.broadcast_to(w["od_sinks"].reshape(N_Q_HEADS, 1, 1), (N_Q_HEADS, 8, LANES))
    y_att = _from_heads(_attn_fwd("l1_attn", qh, kh, vh, sinks_b, cos, sin, rot), pad)

    col0 = (D_ATT + 2 * D_KV) // LANES
    ch = jnp.arange(D_R) // HEAD_DIM
    seg = (ch[:, None] == ch[None, :]).astype(F32)
    prm = dict(w0=w["od_w0"], a0=w["od_a0"], g2=w["od_g2"], k_k=w["od_k_k"], k_a=w["od_k_a"],
               lnx_g=w["od_lnx_g"], lnx_b=w["od_lnx_b"], r_k=w["od_r_k"].reshape(1, D_R),
               w2p=jnp.concatenate([w["od_w2"], jnp.zeros((LORA_A, D_R), F32)], axis=0),
               a2p=jnp.concatenate([jnp.zeros((LORA_W, D_R), F32), w["od_a2"]], axis=0))
    prs = _shift_fwd("l1_shift", p1, col0, w["od_mu"])
    lw, k2, a_, b_, gate_r = _rwkv_pre_fwd("l1_rwkv_pre", prs, prm, seg, tr)
    v_off = 2 * D_R // (WKV_PAIRS_PER_STEP * PAIR)
    scan_in = [(prs, 0), (lw, 0), (k2, 0), (prs, v_off), (a_, 0), (b_, 0)]
    y_scan, states = _wkv_fwd("l1_wkv", scan_in)
    y_rwkv = _rwkv_post_fwd("l1_rwkv_post", y_scan, prs, k2, gate_r, prm, seg, tr)
    y1 = jnp.concatenate([y_att, y_rwkv], axis=1).astype(MXU_DTYPE)
    h3 = _mm_full("l1_out", y1, w["od_w_out"], 0, tm, D_MODEL, add=h2)
    w = need("f1", h3)
    f1 = (0, h3, w["norm_ffn"][1:2], w["ff_w_up1"], w["ff_conv"][1], w["ff_conv_b"][1:2], w["ff_w_down1"])
    h4, ffn1 = _ffn_fwd("f1", *f1, tm)

    loss_blk, dh, d_norm_final = _final_loss("final", h4, w["norm_final"], tgt_p, tm)
    grads["norm_final"] = d_norm_final

    dh, gf1 = _ffn_bwd("f1", *f1, ffn1, dh, tm)
    zero = emit("f1", {"ff_w_up1": gf1["w_up"], "ff_w_down1": gf1["w_down"].reshape(N_CHIPS, D_FF // N_CHIPS, D_MODEL)})
    prm = dict(prm, lnx_g=prm["lnx_g"] + zero)
    dy1 = _mm_nt_full("l1_out_dx", dh, w["od_w_out"], 0, tm, D_MODEL)
    grads["od_w_out"] = _mm_tn_full("l1_out_dw", y1, dh, tm, D_MODEL // 2)
    dy_scan, dr_p, dk2_p, dv_p, dgate_r, grads["od_lnx_g"], grads["od_lnx_b"], d_rk = _rwkv_post_bwd(
        "l1_rwkv_post_bwd", y_scan, prs, k2, gate_r, prm, seg, dy1, 1, tr)
    grads["od_r_k"] = d_rk.reshape(N_R_HEADS, HEAD_DIM)
    dr_s, dlw, dk2_s, dv_s, da_, db_ = _wkv_bwd("l1_wkv_bwd", scan_in, states, (dy_scan, 0))
    dk, dxl, dgd, grads["od_w0"], dw2p, grads["od_a0"], da2p, grads["od_g2"], grads["od_k_k"], grads["od_k_a"] = (
        _rwkv_pre_bwd("l1_rwkv_pre_bwd", prs, prm, seg, (dlw, dk2_s + dk2_p, da_, db_, dgate_r), tr))
    grads["od_w2"] = dw2p[:LORA_W]
    grads["od_a2"] = da2p[LORA_W:]
    dprs = jnp.concatenate([dr_s + dr_p, dk, dv_s + dv_p, dxl, dgd], axis=1)
    dpr, grads["od_mu"] = _shift_bwd("l1_shift_bwd", p1, col0, w["od_mu"], dprs)
    doh = _to_heads(dy1[:, :D_ATT], N_Q_HEADS, pad)
    dqh, dkp, dkc, dvp, dvc, dkm, dvm, dsinks = _attn_bwd("l1_attn_bwd", qh, kh, vh, sinks_b, cos, sin, rot, doh)
    grads["od_sinks"] = dsinks[:, 0, 0].reshape(1, N_Q_HEADS)
    dkh = _kv_combine("l1_attn_dk", dkp, dkc, dkm)
    dvh = _kv_combine("l1_attn_dv", dvp, dvc, dvm)
    dp1 = jnp.concatenate([_from_heads(dqh, pad), _from_heads(dkh, pad), _from_heads(dvh, pad), dpr], axis=1).astype(MXU_DTYPE)
    grads["od_w_in"] = _mm_tn_cs("l1_in_dw", hn2, dp1, N_CHIPS, tm)
    dhn2 = _mm_nt_cs("l1_in_dx", dp1, w["od_w_in"], 0, tm)
    dh, d_mix1 = _rms_bwd("l1_norm_bwd", h2, w["norm_mix"][1:2], dhn2, dh, tm)

    zero = emit("od", {"od_w_out": grads["od_w_out"].reshape(N_CHIPS, D_MODEL // N_CHIPS, D_MODEL), "od_w_in": grads["od_w_in"]})
    f0 = f0[:5] + (f0[5] + zero,) + f0[6:]
    dh, gf0 = _ffn_bwd("f0", *f0, ffn0, dh, tm)
    zero = emit("f0", {"ff_w_up0": gf0["w_up"], "ff_w_down0": gf0["w_down"].reshape(N_CHIPS, D_FF // N_CHIPS, D_MODEL)})
    w = dict(w, ev_ln_a_g=w["ev_ln_a_g"] + zero)
    dy0 = _mm_nt_full("l0_out_dx", dh, w["ev_w_out"], 0, tm, D_MODEL)
    grads["ev_w_out"] = _mm_tn_full("l0_out_dw", y0, dh, tm, D_MODEL // 2)
    duc, grads["ev_ln_a_g"], grads["ev_ln_a_b"] = _even_ln_bwd("l0_ln_bwd", uc, w["ev_ln_a_g"], w["ev_ln_a_b"], dy0, 0, tm)
    *dparts, grads["ev_conv_a"], grads["ev_conv_b"] = _even_col_bwd("l0_convs_bwd", p0, duc, dy0, w["ev_conv_a"], w["ev_conv_b"])
    dp0 = jnp.concatenate(dparts, axis=1)
    grads["ev_w_in"] = _mm_tn_cs("l0_in_dw", hn0, dp0, N_CHIPS, tm)
    dhn0 = _mm_nt_cs("l0_in_dx", dp0, w["ev_w_in"], 0, tm)
    dh, d_mix0 = _rms_bwd("l0_norm_bwd", h0, w["norm_mix"][0:1], dhn0, dh, tm)

    grads["norm_mix"] = jnp.concatenate([d_mix0, d_mix1], axis=0)
    grads["norm_ffn"] = jnp.concatenate([gf0["norm"], gf1["norm"]], axis=0)
    grads["ff_w_up"] = [gf0["w_up"], gf1["w_up"]]
    grads["ff_conv"] = jnp.stack([gf0["conv"], gf1["conv"]])
    grads["ff_conv_b"] = jnp.concatenate([gf0["bias"], gf1["bias"]], axis=0)
    grads["ff_w_down"] = [gf0["w_down"], gf1["w_down"]]
    grads["meta_tokens"] = dh[:N_META]
    return loss_blk[0, 0], dh[N_META:], grads


SHARD_AXIS = {
    "meta_tokens": 1, "norm_mix": None, "norm_ffn": None, "norm_final": None,
    "ev_w_in": 2, "ev_conv_a": 2, "ev_ln_a_g": None, "ev_ln_a_b": None, "ev_conv_b": 2, "ev_w_out": 1,
    "od_w_in": 2, "od_sinks": None, "od_mu": 1, "od_w0": 1, "od_w2": 2, "od_a0": 1, "od_a2": 2, "od_g2": 2,
    "od_k_k": 1, "od_k_a": 1, "od_r_k": None, "od_lnx_g": 1, "od_lnx_b": 1, "od_w_out": 1,
    "ff_w_up": 2, "ff_conv": 2, "ff_conv_b": None, "ff_w_down": 1,
}
WEIGHTS = list(SHARD_AXIS)
BIG = ("ev_w_in", "ev_w_out", "od_w_in", "od_w_out", "ff_w_up", "ff_w_down")
SHARDED = [n for n in WEIGHTS if SHARD_AXIS[n] is not None]
SMALL = [n for n in SHARDED if n not in BIG]
REPLICATED = [n for n in WEIGHTS if SHARD_AXIS[n] is None]


def _join(g, axis):
    return jnp.concatenate([g[k] for k in range(N_CHIPS)], axis=axis)


def _split(full, axis):
    return jnp.stack(jnp.split(full, N_CHIPS, axis=axis))


def _full_weights(gathered, repl):
    w = {}
    sq = lambda a: a.reshape(a.shape[1:]) if a.shape[0] == 1 else a
    for n in REPLICATED:
        w[n] = repl[n]
    w["norm_final"] = repl["norm_final"].reshape(1, D_MODEL)
    for n in ("ev_ln_a_g", "ev_ln_a_b"):
        w[n] = repl[n].reshape(1, D_A)
    w["od_r_k"] = repl["od_r_k"][0]
    w["meta_tokens"] = _join(gathered["meta_tokens"], 1)
    for n in ("ev_conv_a", "ev_conv_b", "od_w2", "od_a2", "od_g2"):
        w[n] = sq(_join(gathered[n], 2))
    for n in ("od_mu", "od_w0", "od_a0", "od_k_k", "od_k_a", "od_lnx_g", "od_lnx_b"):
        w[n] = _join(gathered[n], 1)
    w["ff_conv"] = _join(gathered["ff_conv"], 2)
    return w


def _shard_grads(grads):
    out = {}
    for n in REPLICATED:
        out[n] = grads[n]
    out["norm_final"] = grads["norm_final"].reshape(D_MODEL)
    out["od_r_k"] = grads["od_r_k"][None]
    out["meta_tokens"] = _split(grads["meta_tokens"], 1)
    for n in ("ev_conv_a", "ev_conv_b", "od_w2", "od_a2", "od_g2"):
        out[n] = _split(grads[n][None], 2)
    for n in ("od_mu", "od_w0", "od_a0", "od_k_k", "od_k_a", "od_lnx_g", "od_lnx_b"):
        out[n] = _split(grads[n], 1)
    out["ff_conv"] = _split(grads["ff_conv"], 2)
    return out


def kernel(x, meta_tokens, norm_mix, norm_ffn, norm_final, ev_w_in, ev_conv_a, ev_ln_a_g, ev_ln_a_b, ev_conv_b, ev_w_out, od_w_in, od_sinks, od_mu, od_w0, od_w2, od_a0, od_a2, od_g2, od_k_k, od_k_a, od_r_k, od_lnx_g, od_lnx_b, od_w_out, ff_w_up, ff_conv, ff_conv_b, ff_w_down, loss_target, m_meta_tokens, m_norm_mix, m_norm_ffn, m_norm_final, m_ev_w_in, m_ev_conv_a, m_ev_ln_a_g, m_ev_ln_a_b, m_ev_conv_b, m_ev_w_out, m_od_w_in, m_od_sinks, m_od_mu, m_od_w0, m_od_w2, m_od_a0, m_od_a2, m_od_g2, m_od_k_k, m_od_k_a, m_od_r_k, m_od_lnx_g, m_od_lnx_b, m_od_w_out, m_ff_w_up, m_ff_conv, m_ff_conv_b, m_ff_w_down, v_meta_tokens, v_norm_mix, v_norm_ffn, v_norm_final, v_ev_w_in, v_ev_conv_a, v_ev_ln_a_g, v_ev_ln_a_b, v_ev_conv_b, v_ev_w_out, v_od_w_in, v_od_sinks, v_od_mu, v_od_w0, v_od_w2, v_od_a0, v_od_a2, v_od_g2, v_od_k_k, v_od_k_a, v_od_r_k, v_od_lnx_g, v_od_lnx_b, v_od_w_out, v_ff_w_up, v_ff_conv, v_ff_conv_b, v_ff_w_down):
    wts = dict(meta_tokens=meta_tokens, norm_mix=norm_mix, norm_ffn=norm_ffn, norm_final=norm_final, ev_w_in=ev_w_in, ev_conv_a=ev_conv_a, ev_ln_a_g=ev_ln_a_g, ev_ln_a_b=ev_ln_a_b, ev_conv_b=ev_conv_b, ev_w_out=ev_w_out, od_w_in=od_w_in, od_sinks=od_sinks, od_mu=od_mu, od_w0=od_w0, od_w2=od_w2, od_a0=od_a0, od_a2=od_a2, od_g2=od_g2, od_k_k=od_k_k, od_k_a=od_k_a, od_r_k=od_r_k, od_lnx_g=od_lnx_g, od_lnx_b=od_lnx_b, od_w_out=od_w_out, ff_w_up=ff_w_up, ff_conv=ff_conv, ff_conv_b=ff_conv_b, ff_w_down=ff_w_down)
    mom = dict(meta_tokens=m_meta_tokens, norm_mix=m_norm_mix, norm_ffn=m_norm_ffn, norm_final=m_norm_final, ev_w_in=m_ev_w_in, ev_conv_a=m_ev_conv_a, ev_ln_a_g=m_ev_ln_a_g, ev_ln_a_b=m_ev_ln_a_b, ev_conv_b=m_ev_conv_b, ev_w_out=m_ev_w_out, od_w_in=m_od_w_in, od_sinks=m_od_sinks, od_mu=m_od_mu, od_w0=m_od_w0, od_w2=m_od_w2, od_a0=m_od_a0, od_a2=m_od_a2, od_g2=m_od_g2, od_k_k=m_od_k_k, od_k_a=m_od_k_a, od_r_k=m_od_r_k, od_lnx_g=m_od_lnx_g, od_lnx_b=m_od_lnx_b, od_w_out=m_od_w_out, ff_w_up=m_ff_w_up, ff_conv=m_ff_conv, ff_conv_b=m_ff_conv_b, ff_w_down=m_ff_w_down)
    var = dict(meta_tokens=v_meta_tokens, norm_mix=v_norm_mix, norm_ffn=v_norm_ffn, norm_final=v_norm_final, ev_w_in=v_ev_w_in, ev_conv_a=v_ev_conv_a, ev_ln_a_g=v_ev_ln_a_g, ev_ln_a_b=v_ev_ln_a_b, ev_conv_b=v_ev_conv_b, ev_w_out=v_ev_w_out, od_w_in=v_od_w_in, od_sinks=v_od_sinks, od_mu=v_od_mu, od_w0=v_od_w0, od_w2=v_od_w2, od_a0=v_od_a0, od_a2=v_od_a2, od_g2=v_od_g2, od_k_k=v_od_k_k, od_k_a=v_od_k_a, od_r_k=v_od_r_k, od_lnx_g=v_od_lnx_g, od_lnx_b=v_od_lnx_b, od_w_out=v_od_w_out, ff_w_up=v_ff_w_up, ff_conv=v_ff_conv, ff_conv_b=v_ff_conv_b, ff_w_down=v_ff_w_down)

    me_idx = (2 * lax.axis_index("x") + lax.axis_index("y")).astype(jnp.int32).reshape(1)
    c_idx = lax.axis_index("c").astype(jnp.int32).reshape(1)
    small_mine = _pack([wts[n] for n in SMALL], F32, 2 * 8)
    sources = {"ev_w_in": (ev_w_in, 0), "small": (small_mine[None], 0), "ev_w_out": (ev_w_out, 0),
               "ff_w_up0": (ff_w_up, 0), "ff_w_down0": (ff_w_down, 0), "od_w_in": (od_w_in, 0), "od_w_out": (od_w_out, 0),
               "ff_w_up1": (ff_w_up, 1), "ff_w_down1": (ff_w_down, 1)}
    bufs = {n: _place_own_block("place_" + n, a, l, me_idx, F32 if n == "small" else MXU_DTYPE)
            for n, (a, l) in sources.items()}

    def as_used(n, g):
        if n in ("ev_w_out", "od_w_out", "ff_w_down0", "ff_w_down1"):
            return g.reshape(1, -1, g.shape[-1])
        return g.reshape(N_CHIPS, 1, -1, g.shape[-1])

    first = dict(zip(("ev_w_in", "small"), _gather_weights("gather_first", [bufs["ev_w_in"], bufs["small"]])))
    gathered = dict(zip(SMALL, _unpack(first["small"].reshape(N_CHIPS, -1), [wts[n].shape for n in SMALL])))
    w_full = _full_weights(gathered, wts)
    w_full["ev_w_in"] = as_used("ev_w_in", first["ev_w_in"])
    groups = {"ev_out": ["ev_w_out"], "f0": ["ff_w_up0", "ff_w_down0"], "od": ["od_w_in", "od_w_out"],
              "f1": ["ff_w_up1", "ff_w_down1"]}
    started_gathers, token = _gather_start("gather_start", [[bufs[n] for n in g] for g in groups.values()])
    started_gathers = dict(zip(groups, started_gathers))
    w_full["norm_mix"] = w_full["norm_mix"] + token[0, 0]

    def fetch(tag, after):
        send_sems, recv_sems, group_bufs = started_gathers[tag]
        landed = _gather_wait("gather_wait_" + tag, send_sems, recv_sems, group_bufs, after)
        whole = _gather_weights("gather_siblings_" + tag, landed, from_chips=False)
        return {n: as_used(n, g) for n, g in zip(groups[tag], whole)}

    cm_idx = jnp.concatenate([c_idx, me_idx])
    started = []

    def start_reduction(tag, units):
        names = list(units)
        from_sibling = _halves_to_sibling(f"grads_to_sibling_{tag}", [units[n] for n in names])
        pairs = [_pair_add_placed(f"grads_pair_add_{n}", units[n], r, cm_idx, GRAD_WIRE_DTYPE) for n, r in zip(names, from_sibling)]
        send_sems, recv_sems, sums, zones, token = _scatter_start(
            f"grads_to_chips_start_{tag}", [p[0] for p in pairs], [p[1] for p in pairs])
        started.append((tag, names, send_sems, recv_sems, sums, zones))
        return token[0, 0]

    loss_local, grad_x, grads = _local_step(x[0], loss_target[0], w_full, start_reduction, fetch)
    loss = lax.psum(loss_local, ("x", "y", "c"))

    sg = _shard_grads(grads)
    small_rows = [jnp.concatenate([sg[n][k].reshape(-1) for n in SMALL] + [sg[n].reshape(-1) for n in REPLICATED])
                  for k in range(N_CHIPS)]
    n_el = small_rows[0].shape[0]
    n_rows = -(-n_el // (16 * PACK_W)) * 16
    small_unit = jnp.stack([jnp.pad(r, (0, n_rows * PACK_W - n_el)).reshape(n_rows, PACK_W) for r in small_rows])
    last = {"ev_w_out": grads["ev_w_out"].reshape(N_CHIPS, D_MODEL // N_CHIPS, D_MODEL), "ev_w_in": grads["ev_w_in"],
            "small": small_unit}
    from_sibling = _halves_to_sibling("grads_to_sibling_ev", list(last.values()))
    chip_sums = [_pair_add(f"grads_pair_add_{n}", u, r, c_idx, F32 if n == "small" else GRAD_WIRE_DTYPE)
                 for (n, u), r in zip(last.items(), from_sibling)]
    from_chips = dict(zip(last, _scatter_to_chips("grads_to_chips_ev", chip_sums)))
    for tag, names, send_sems, recv_sems, sums, zones in started:
        from_chips.update(zip(names, _scatter_wait(f"grads_to_chips_wait_{tag}", send_sems, recv_sems, sums, zones,
                                                   from_chips["small"])))
    dests = {"ev_w_in": ("ev_w_in", 0), "od_w_in": ("od_w_in", 0), "ev_w_out": ("ev_w_out", 0), "od_w_out": ("od_w_out", 0),
             "ff_w_up0": ("ff_w_up", 0), "ff_w_up1": ("ff_w_up", 1), "ff_w_down0": ("ff_w_down", 0),
             "ff_w_down1": ("ff_w_down", 1), "small": ("small", 0)}
    results = ["ev_w_in", "od_w_in", "ev_w_out", "od_w_out", "ff_w_up", "ff_w_down", "small"]
    reduced = {}
    for n, (r, l) in dests.items():
        reduced[r] = _sum_chips(f"grads_chip_sum_{n}", from_chips[n], c_idx, l, 2 if r.startswith("ff_w") else 1,
                                into=reduced.get(r))
    joined = _join_halves("grads_join", [reduced[r] for r in results])

    outs = {"grad": {}, "delta": {}, "new_m": {}, "new_v": {}}
    for n, g in zip(results[:-1], joined):
        shape = wts[n].shape
        flat = lambda a: a.reshape(-1, shape[-1])
        new = _adamw("adamw_" + n, flat(wts[n]), flat(g), flat(mom[n]), flat(var[n]))
        for tag, arr in zip(("grad", "delta", "new_m", "new_v"), (g,) + tuple(new)):
            outs[tag][n] = arr.reshape(shape)
    order = SMALL + REPLICATED
    packed = lambda d: jnp.pad(jnp.concatenate([d[n].reshape(-1) for n in order]),
                               (0, n_rows * PACK_W - n_el)).reshape(n_rows, PACK_W)
    g_small = joined[-1].reshape(n_rows, PACK_W)
    new = _adamw("adamw_small", packed(wts), g_small, packed(mom), packed(var))
    for tag, arr in zip(("grad", "delta", "new_m", "new_v"), (g_small,) + tuple(new)):
        outs[tag].update(zip(order, _unpack(arr.reshape(-1), [wts[n].shape for n in order])))
    return (loss, grad_x[None], *[outs["grad"][n] for n in WEIGHTS], *[outs["delta"][n] for n in WEIGHTS],
            *[outs["new_m"][n] for n in WEIGHTS], *[outs["new_v"][n] for n in WEIGHTS])
```

```python
import functools

import jax
import jax.numpy as jnp
from jax import lax
from jax.experimental import pallas as pl
from jax.experimental.pallas import tpu as pltpu

F32 = jnp.float32
BF16 = jnp.bfloat16
HI = lax.Precision.HIGHEST
MXU_DTYPE = BF16
GRAD_WIRE_DTYPE = BF16

D_MODEL = 1024
N_META = 16
RMS_EPS = 1e-6
LN_EPS = 1e-5
D_A = 512
CONV_A_WIDTH = 31
CONV_B_WIDTH = 3
HEAD_DIM = 64
N_Q_HEADS = 8
N_KV_HEADS = 2
GQA_GROUP = 4
D_ATT = 512
D_KV = 128
BLOCK = 128
ROPE_THETA = 10000.0
D_R = 512
N_R_HEADS = 8
LORA_W = 64
LORA_A = 64
LORA_G = 128
RWKV_GN_EPS = 64e-5
RWKV_COLS = 3 * D_R + LORA_W + LORA_A + LORA_G
D_FF = 2816
NEG_INF = -1e30
ADAM_LR = 0.001
ADAM_B1 = 0.9
ADAM_B2 = 0.999
ADAM_EPS = 1e-08
ADAM_WD = 0.01
ADAM_STEP = 10

N_CHIPS = 4
LANES = 128
CONV_PAD = 32
VMEM_LIMIT_V7X = 56 * 1024 * 1024
MESH = pl.DeviceIdType.MESH


def _cparams(sem=None):
    return pltpu.CompilerParams(dimension_semantics=sem, vmem_limit_bytes=VMEM_LIMIT_V7X)


def _row_tile(t, cap):
    for d in range(min(t, cap), 0, -1):
        if t % d == 0 and d % 16 == 0:
            return d
    return t


def _chunk_len(t):
    for d in (64, 48, 32, 16, 8):
        if t % d == 0:
            return d
    raise ValueError(t)


def _call(fn, name, grid, ins, outs, acc_axis=None, sem=None):
    n_in, n_out = len(ins), len(outs)
    dtype = lambda o: o[4] if len(o) > 4 else F32

    def body(*refs):
        vals = fn(*[r[...] for r in refs[:n_in]])
        if not isinstance(vals, (tuple, list)):
            vals = (vals,)
        for r, v, o in zip(refs[n_in:n_in + n_out], vals, outs):
            if o[3]:
                first = pl.program_id(acc_axis) == 0

                @pl.when(first)
                def _(r=r, v=v):
                    r[...] = v

                @pl.when(jnp.logical_not(first))
                def _(r=r, v=v):
                    r[...] += v
            else:
                r[...] = v.astype(dtype(o))

    res = pl.pallas_call(
        body, name=name, grid=grid,
        in_specs=[pl.BlockSpec(b, m) for _, b, m in ins],
        out_specs=[pl.BlockSpec(o[1], o[2]) for o in outs],
        out_shape=[jax.ShapeDtypeStruct(o[0], dtype(o)) for o in outs],
        compiler_params=_cparams(sem),
    )(*[a for a, _, _ in ins])
    return res if n_out > 1 else res[0]


def _matmul(name, a, b, *, dims, grid, a_spec, b_spec, o_shape, o_spec, acc_shape, nk, k_axis,
            add=None, add_spec=None):
    def product(a_ref, b_ref):
        return lax.dot_general(a_ref[...].astype(MXU_DTYPE), b_ref[...].astype(MXU_DTYPE), dims, preferred_element_type=F32)

    def body_single(*refs):
        a_ref, b_ref, o_ref = refs[0], refs[1], refs[-1]
        o_ref[...] = product(a_ref, b_ref) if add is None else product(a_ref, b_ref) + refs[2][...]

    def body_steps(*refs):
        a_ref, b_ref, o_ref, acc = refs[0], refs[1], refs[-2], refs[-1]
        k = pl.program_id(k_axis)

        @pl.when(k == 0)
        def _():
            if add is None:
                acc[...] = jnp.zeros(acc.shape, F32)
            else:
                acc[...] = refs[2][...]

        acc[...] += product(a_ref, b_ref)

        @pl.when(k == nk - 1)
        def _():
            o_ref[...] = acc[...]

    args = [a, b] + ([] if add is None else [add])
    specs = [a_spec, b_spec] + ([] if add is None else [add_spec])
    return pl.pallas_call(
        body_single if nk == 1 else body_steps, name=name, grid=grid, in_specs=specs, out_specs=o_spec,
        out_shape=jax.ShapeDtypeStruct(o_shape, F32),
        scratch_shapes=[] if nk == 1 else [pltpu.VMEM(acc_shape, F32)],
        compiler_params=_cparams(None),
    )(*args)


_NN = (((1,), (0,)), ((), ()))
_NT = (((1,), (1,)), ((), ()))
_TN = (((0,), (0,)), ((), ()))


def _mm_cs(name, x, wg, l, tm):
    t, k = x.shape
    s, _, _, n = wg.shape
    return _matmul(name, x, wg, dims=_NN, grid=(s, t // tm, 1),
                   a_spec=pl.BlockSpec((tm, k), lambda j, i, kk: (i, 0)),
                   b_spec=pl.BlockSpec((None, None, k, n), lambda j, i, kk: (j, l, 0, 0)),
                   o_shape=(t, s * n), o_spec=pl.BlockSpec((tm, n), lambda j, i, kk: (i, j)),
                   acc_shape=(tm, n), nk=1, k_axis=2)


def _mm_full(name, x, w, l, tm, tk, add=None):
    t, k = x.shape
    n = w.shape[2]
    nk = k // tk
    return _matmul(name, x, w, dims=_NN, grid=(t // tm, 1, nk),
                   a_spec=pl.BlockSpec((tm, tk), lambda i, j, kk: (i, kk)),
                   b_spec=pl.BlockSpec((None, tk, n), lambda i, j, kk: (l, kk, 0)),
                   o_shape=(t, n), o_spec=pl.BlockSpec((tm, n), lambda i, j, kk: (i, 0)),
                   acc_shape=(tm, n), nk=nk, k_axis=2,
                   add=add, add_spec=pl.BlockSpec((tm, n), lambda i, j, kk: (i, 0)))


def _mm_nt_cs(name, dy, wg, l, tm, add=None):
    t = dy.shape[0]
    s, _, k, n = wg.shape
    return _matmul(name, dy, wg, dims=_NT, grid=(t // tm, 1, s),
                   a_spec=pl.BlockSpec((tm, n), lambda i, j, kk: (i, kk)),
                   b_spec=pl.BlockSpec((None, None, k, n), lambda i, j, kk: (kk, l, 0, 0)),
                   o_shape=(t, k), o_spec=pl.BlockSpec((tm, k), lambda i, j, kk: (i, 0)),
                   acc_shape=(tm, k), nk=s, k_axis=2,
                   add=add, add_spec=pl.BlockSpec((tm, k), lambda i, j, kk: (i, 0)))


def _mm_nt_full(name, dy, w, l, tm, tko):
    t, n = dy.shape
    k = w.shape[1]
    return _matmul(name, dy, w, dims=_NT, grid=(t // tm, k // tko, 1),
                   a_spec=pl.BlockSpec((tm, n), lambda i, j, kk: (i, 0)),
                   b_spec=pl.BlockSpec((None, tko, n), lambda i, j, kk: (l, j, 0)),
                   o_shape=(t, k), o_spec=pl.BlockSpec((tm, tko), lambda i, j, kk: (i, j)),
                   acc_shape=(tm, tko), nk=1, k_axis=2)


def _mm_tn_cs(name, x, dy, s, tk):
    t, k = x.shape
    n = dy.shape[1] // s
    nk = t // tk
    return _matmul(name, x, dy, dims=_TN, grid=(s, 1, nk),
                   a_spec=pl.BlockSpec((tk, k), lambda j, i, kk: (kk, 0)),
                   b_spec=pl.BlockSpec((tk, n), lambda j, i, kk: (kk, j)),
                   o_shape=(s, k, n), o_spec=pl.BlockSpec((None, k, n), lambda j, i, kk: (j, 0, 0)),
                   acc_shape=(k, n), nk=nk, k_axis=2)


def _mm_tn_full(name, y, dh, tk, tko):
    t, k = y.shape
    n = dh.shape[1]
    nk = t // tk
    return _matmul(name, y, dh, dims=_TN, grid=(k // tko, 1, nk),
                   a_spec=pl.BlockSpec((tk, tko), lambda j, i, kk: (kk, j)),
                   b_spec=pl.BlockSpec((tk, n), lambda j, i, kk: (kk, 0)),
                   o_shape=(k, n), o_spec=pl.BlockSpec((tko, n), lambda j, i, kk: (j, 0)),
                   acc_shape=(tko, n), nk=nk, k_axis=2)


def _sigmoid(x):
    return 1.0 / (1.0 + jnp.exp(-x))


def _rms_fwd(name, h, g, tr):
    t, d = h.shape

    def fn(hv, gv):
        r = lax.rsqrt(jnp.mean(hv * hv, axis=-1, keepdims=True) + RMS_EPS)
        return hv * r * gv

    return _call(fn, name, (t // tr,), [(h, (tr, d), lambda i: (i, 0)), (g, (1, d), lambda i: (0, 0))],
                 [((t, d), (tr, d), lambda i: (i, 0), False, MXU_DTYPE)])


def _rms_bwd(name, h, g, dhn, dh, tr):
    t, d = h.shape

    def fn(hv, gv, dy, dh_in):
        r = lax.rsqrt(jnp.mean(hv * hv, axis=-1, keepdims=True) + RMS_EPS)
        xh = hv * r
        dg = jnp.sum(dy * xh, axis=0, keepdims=True)
        dxh = dy * gv
        dx = r * (dxh - xh * jnp.mean(dxh * xh, axis=-1, keepdims=True))
        return dh_in + dx, dg

    row = lambda i: (i, 0)
    return _call(fn, name, (t // tr,),
                 [(h, (tr, d), row), (g, (1, d), lambda i: (0, 0)), (dhn, (tr, d), row), (dh, (tr, d), row)],
                 [((t, d), (tr, d), row, False), ((1, d), (1, d), lambda i: (0, 0), True)], acc_axis=0)


def _final_loss(name, h, g, tgt, tr):
    t, d = h.shape

    def fn(hv, gv, tv):
        r = lax.rsqrt(jnp.mean(hv * hv, axis=-1, keepdims=True) + RMS_EPS)
        xh = hv * r
        row = pl.program_id(0) * tr + lax.broadcasted_iota(jnp.int32, (tr, 1), 0)
        e = jnp.where(row >= N_META, xh * gv - tv, 0.0)
        loss = jnp.broadcast_to(0.5 * jnp.sum(jnp.sum(e * e, axis=-1, keepdims=True), axis=0, keepdims=True) / d,
                                (8, LANES))
        dout = e / d
        dg = jnp.sum(dout * xh, axis=0, keepdims=True)
        dxh = dout * gv
        dx = r * (dxh - xh * jnp.mean(dxh * xh, axis=-1, keepdims=True))
        return loss, dx, dg

    row = lambda i: (i, 0)
    fix = lambda i: (0, 0)
    return _call(fn, name, (t // tr,), [(h, (tr, d), row), (g, (1, d), fix), (tgt, (tr, d), row)],
                 [((8, LANES), (8, LANES), fix, True), ((t, d), (tr, d), row, False), ((1, d), (1, d), fix, True)],
                 acc_axis=0)


def _silu_ln(uc, g, b):
    mu = jnp.mean(uc, axis=-1, keepdims=True)
    xc = uc - mu
    rs = lax.rsqrt(jnp.mean(xc * xc, axis=-1, keepdims=True) + LN_EPS)
    ln = xc * rs * g + b
    return ln * _sigmoid(ln)


def _even_ln_fwd(name, uc, g, b, tr):
    t, d = uc.shape
    row, fix = (lambda i: (i, 0)), (lambda i: (0, 0))
    return _call(_silu_ln, name, (t // tr,), [(uc, (tr, d), row), (g, (1, d), fix), (b, (1, d), fix)],
                 [((t, d), (tr, d), row, False, MXU_DTYPE)])


def _even_ln_bwd(name, uc, g, b, dy, dy_col, tr):
    t, d = uc.shape

    def fn(ucv, gv, bv, dyv):
        mu = jnp.mean(ucv, axis=-1, keepdims=True)
        xc = ucv - mu
        rs = lax.rsqrt(jnp.mean(xc * xc, axis=-1, keepdims=True) + LN_EPS)
        xh = xc * rs
        ln = xh * gv + bv
        s = _sigmoid(ln)
        dln = dyv * (s * (1.0 + ln * (1.0 - s)))
        dg = jnp.sum(dln * xh, axis=0, keepdims=True)
        db = jnp.sum(dln, axis=0, keepdims=True)
        dxh = dln * gv
        duc = rs * (dxh - jnp.mean(dxh, axis=-1, keepdims=True) - xh * jnp.mean(dxh * xh, axis=-1, keepdims=True))
        return duc, dg, db

    row, fix = (lambda i: (i, 0)), (lambda i: (0, 0))
    return _call(fn, name, (t // tr,),
                 [(uc, (tr, d), row), (g, (1, d), fix), (b, (1, d), fix), (dy, (tr, d), lambda i: (i, dy_col))],
                 [((t, d), (tr, d), row, False), ((1, d), (1, d), fix, True), ((1, d), (1, d), fix, True)], acc_axis=0)


def _windows(t):
    rc = _chunk_len(t)
    return [(r0, rc) for r0 in range(0, t, rc)]


def _taps(w_ref, width):
    return [w_ref[pl.ds(j, 1), :] for j in range(width)]


def _conv_at(xp, taps, r0, rc):
    width = len(taps)
    acc = None
    for j in range(width):
        term = xp[pl.ds(CONV_PAD - (width - 1) + j + r0, rc), :] * taps[j]
        acc = term if acc is None else acc + term
    return acc


def _conv_bwd_in_at(dyp, taps, r0, rc):
    width = len(taps)
    acc = None
    for j in range(width):
        term = dyp[pl.ds(width - 1 - j + r0, rc), :] * taps[j]
        acc = term if acc is None else acc + term
    return acc


def _fold(x):
    acc = x[0:8]
    for i in range(1, x.shape[0] // 8):
        acc = acc + x[8 * i:8 * (i + 1)]
    return acc


def _add_to(accs, vals):
    return vals if accs is None else [a + v for a, v in zip(accs, vals)]


def _conv_bwd_w_at(dy, xp, width, r0, rc):
    return [_fold(dy * xp[pl.ds(CONV_PAD - (width - 1) + j + r0, rc), :]) for j in range(width)]


def _store_taps(dw_ref, accs):
    for j, a in enumerate(accs):
        dw_ref[pl.ds(j, 1), :] = jnp.sum(a, axis=0, keepdims=True)


def _zero_front(xp):
    xp[pl.ds(0, CONV_PAD), :] = jnp.zeros((CONV_PAD, LANES), F32)


def _zero_back(dyp, t):
    dyp[pl.ds(t, CONV_PAD), :] = jnp.zeros((CONV_PAD, LANES), F32)


def _col_call(body, name, ncol, ins, outs, t, n_scratch):
    def spec(rows, off):
        return pl.BlockSpec((rows, LANES), lambda j, off=off: (0, j + off))

    res = pl.pallas_call(
        body, name=name, grid=(ncol,),
        in_specs=[spec(r, off) for _, r, off in ins],
        out_specs=[spec(o[0], 0) for o in outs],
        out_shape=[jax.ShapeDtypeStruct(o[:2], o[2] if len(o) > 2 else F32) for o in outs],
        scratch_shapes=[pltpu.VMEM((t + CONV_PAD, LANES), F32) for _ in range(n_scratch)],
        compiler_params=_cparams(None),
    )(*[a for a, _, _ in ins])
    return res


def _even_col_fwd(name, p, conv_a, conv_b):
    t = p.shape[0]
    nc = D_A // LANES
    wins = _windows(t)

    def body(av, ag, gb, gc, xi, ca, cb, uc_ref, yb_ref, xp):
        _zero_front(xp)
        for r0, rc in wins:
            rows = pl.ds(r0, rc)
            xp[pl.ds(CONV_PAD + r0, rc), :] = av[rows, :] * _sigmoid(ag[rows, :])
        taps = _taps(ca, CONV_A_WIDTH)
        for r0, rc in wins:
            uc_ref[pl.ds(r0, rc), :] = _conv_at(xp, taps, r0, rc)
        for r0, rc in wins:
            rows = pl.ds(r0, rc)
            xp[pl.ds(CONV_PAD + r0, rc), :] = gc[rows, :] * xi[rows, :]
        taps = _taps(cb, CONV_B_WIDTH)
        for r0, rc in wins:
            rows = pl.ds(r0, rc)
            yb_ref[rows, :] = (gb[rows, :] * _conv_at(xp, taps, r0, rc)).astype(yb_ref.dtype)

    ins = [(p, t, k * nc) for k in range(5)] + [(conv_a, CONV_A_WIDTH, 0), (conv_b, CONV_B_WIDTH, 0)]
    return _col_call(body, name, nc, ins, [(t, D_A), (t, D_A, MXU_DTYPE)], t, 1)


def _even_col_bwd(name, p, duc, dy, conv_a, conv_b):
    t = p.shape[0]
    nc = D_A // LANES
    wins = _windows(t)

    def body(av, ag, gb, gc, xi, duc_ref, dyb_ref, ca, cb, dav, dag, dgb, dgc, dxi, dca, dcb, xp, dyp):
        _zero_front(xp)
        _zero_back(dyp, t)
        for r0, rc in wins:
            rows = pl.ds(r0, rc)
            xp[pl.ds(CONV_PAD + r0, rc), :] = av[rows, :] * _sigmoid(ag[rows, :])
            dyp[rows, :] = duc_ref[rows, :]
        taps = _taps(ca, CONV_A_WIDTH)
        accs = None
        for r0, rc in wins:
            rows = pl.ds(r0, rc)
            accs = _add_to(accs, _conv_bwd_w_at(duc_ref[rows, :], xp, CONV_A_WIDTH, r0, rc))
            du = _conv_bwd_in_at(dyp, taps, r0, rc)
            sig = _sigmoid(ag[rows, :])
            dav[rows, :] = (du * sig).astype(dav.dtype)
            dag[rows, :] = (du * av[rows, :] * sig * (1.0 - sig)).astype(dag.dtype)
        _store_taps(dca, accs)
        for r0, rc in wins:
            rows = pl.ds(r0, rc)
            xp[pl.ds(CONV_PAD + r0, rc), :] = gc[rows, :] * xi[rows, :]
        taps = _taps(cb, CONV_B_WIDTH)
        accs = None
        for r0, rc in wins:
            rows = pl.ds(r0, rc)
            dgb[rows, :] = (dyb_ref[rows, :] * _conv_at(xp, taps, r0, rc)).astype(dgb.dtype)
            dzc = dyb_ref[rows, :] * gb[rows, :]
            dyp[rows, :] = dzc
            accs = _add_to(accs, _conv_bwd_w_at(dzc, xp, CONV_B_WIDTH, r0, rc))
        _store_taps(dcb, accs)
        for r0, rc in wins:
            rows = pl.ds(r0, rc)
            dz = _conv_bwd_in_at(dyp, taps, r0, rc)
            dgc[rows, :] = (dz * xi[rows, :]).astype(dgc.dtype)
            dxi[rows, :] = (dz * gc[rows, :]).astype(dxi.dtype)

    ins = ([(p, t, k * nc) for k in range(5)] + [(duc, t, 0), (dy, t, nc)]
           + [(conv_a, CONV_A_WIDTH, 0), (conv_b, CONV_B_WIDTH, 0)])
    outs = [(t, D_A, MXU_DTYPE)] * 5 + [(CONV_A_WIDTH, D_A), (CONV_B_WIDTH, D_A)]
    return _col_call(body, name, nc, ins, outs, t, 2)


def _ffn_col_fwd(name, u, conv, bias):
    t = u.shape[0]
    nc = D_FF // LANES
    wins = _windows(t)

    def body(g_ref, v_ref, cw, b_ref, a_ref, xp):
        _zero_front(xp)
        xp[pl.ds(CONV_PAD, t), :] = g_ref[...]
        taps = _taps(cw, CONV_B_WIDTH)
        b = b_ref[...]
        for r0, rc in wins:
            rows = pl.ds(r0, rc)
            gc = _conv_at(xp, taps, r0, rc) + b
            a_ref[rows, :] = (gc * _sigmoid(gc) * v_ref[rows, :]).astype(a_ref.dtype)

    ins = [(u, t, 0), (u, t, nc), (conv, CONV_B_WIDTH, 0), (bias, 1, 0)]
    return _col_call(body, name, nc, ins, [(t, D_FF, MXU_DTYPE)], t, 1)[0]


def _ffn_col_bwd(name, u, da, conv, bias):
    t = u.shape[0]
    nc = D_FF // LANES
    wins = _windows(t)

    def body(g_ref, v_ref, da_ref, cw, b_ref, du_ref, dcw, db_ref, xp, dyp, dval):
        @pl.when(pl.program_id(1) == 0)
        def _():
            _zero_front(xp)
            _zero_back(dyp, t)
            xp[pl.ds(CONV_PAD, t), :] = g_ref[...]
            taps = _taps(cw, CONV_B_WIDTH)
            b = b_ref[...]
            accs, bias_acc = None, None
            for r0, rc in wins:
                rows = pl.ds(r0, rc)
                gc = _conv_at(xp, taps, r0, rc) + b
                s = _sigmoid(gc)
                d = da_ref[rows, :]
                dval[rows, :] = d * gc * s
                dgc = d * v_ref[rows, :] * (s * (1.0 + gc * (1.0 - s)))
                dyp[rows, :] = dgc
                bias_acc = _add_to(bias_acc, [_fold(dgc)])
                accs = _add_to(accs, _conv_bwd_w_at(dgc, xp, CONV_B_WIDTH, r0, rc))
            db_ref[...] = jnp.sum(bias_acc[0], axis=0, keepdims=True)
            _store_taps(dcw, accs)
            for r0, rc in wins:
                du_ref[pl.ds(r0, rc), :] = _conv_bwd_in_at(dyp, taps, r0, rc).astype(du_ref.dtype)

        @pl.when(pl.program_id(1) == 1)
        def _():
            du_ref[...] = dval[...].astype(du_ref.dtype)

    col = lambda rows, off: pl.BlockSpec((rows, LANES), lambda j, p: (0, j + off))
    return pl.pallas_call(
        body, name=name, grid=(nc, 2),
        in_specs=[col(t, 0), col(t, nc), col(t, 0), col(CONV_B_WIDTH, 0), col(1, 0)],
        out_specs=[pl.BlockSpec((t, LANES), lambda j, p: (0, j + nc * p)), col(CONV_B_WIDTH, 0), col(1, 0)],
        out_shape=[jax.ShapeDtypeStruct((t, 2 * D_FF), MXU_DTYPE), jax.ShapeDtypeStruct((CONV_B_WIDTH, D_FF), F32),
                   jax.ShapeDtypeStruct((1, D_FF), F32)],
        scratch_shapes=[pltpu.VMEM((t + CONV_PAD, LANES), F32) for _ in range(2)] + [pltpu.VMEM((t, LANES), F32)],
        compiler_params=_cparams(None),
    )(u, u, da, conv, bias)


def _shift_fwd(name, p, col0, mu):
    t = p.shape[0]
    wins = _windows(t)

    def body(x_ref, mu_ref, o_ref, xp):
        _zero_front(xp)
        xp[pl.ds(CONV_PAD, t), :] = x_ref[...]
        mu_v = mu_ref[...]
        for r0, rc in wins:
            rows = pl.ds(r0, rc)
            x = x_ref[rows, :]
            o_ref[rows, :] = x + (xp[pl.ds(CONV_PAD - 1 + r0, rc), :] - x) * mu_v

    return _col_call(body, name, RWKV_COLS // LANES, [(p, t, col0), (mu, 1, 0)], [(t, RWKV_COLS)], t, 1)[0]


def _shift_bwd(name, p, col0, mu, dprs):
    t = p.shape[0]
    wins = _windows(t)

    def body(x_ref, mu_ref, d_ref, dx_ref, dmu_ref, xp, dyp):
        _zero_front(xp)
        _zero_back(dyp, t)
        xp[pl.ds(CONV_PAD, t), :] = x_ref[...]
        mu_v = mu_ref[...]
        acc = None
        for r0, rc in wins:
            rows = pl.ds(r0, rc)
            d = d_ref[rows, :]
            acc = _add_to(acc, [_fold(d * (xp[pl.ds(CONV_PAD - 1 + r0, rc), :] - x_ref[rows, :]))])
            dyp[rows, :] = d * mu_v
        dmu_ref[...] = jnp.sum(acc[0], axis=0, keepdims=True)
        for r0, rc in wins:
            rows = pl.ds(r0, rc)
            dx_ref[rows, :] = d_ref[rows, :] - dyp[rows, :] + dyp[pl.ds(1 + r0, rc), :]

    ins = [(p, t, col0), (mu, 1, 0), (dprs, t, 0)]
    return _col_call(body, name, RWKV_COLS // LANES, ins, [(t, RWKV_COLS), (1, RWKV_COLS)], t, 2)


def _hi_lo(x):
    hi = x.astype(BF16)
    return hi, (x - hi.astype(F32)).astype(BF16)


def _dot_passes(a, b, dims, passes):
    d = lambda p, q: lax.dot_general(p, q, dims, preferred_element_type=F32)
    if passes == 1:
        return d(a.astype(MXU_DTYPE), b.astype(MXU_DTYPE))
    ah, al = _hi_lo(a)
    bh, bl = _hi_lo(b)
    return d(ah, bh) + (d(ah, bl) + d(al, bh))


@functools.partial(jax.custom_vjp, nondiff_argnums=(2, 3))
def _dot_vjp(a, b, dims, passes):
    return _dot_passes(a, b, dims, passes)


def _dot_fwd(a, b, dims, passes):
    return _dot_passes(a, b, dims, passes), (a, b)


def _dot_bwd(dims, passes, res, g):
    a, b = res
    if dims == _NN:
        return _dot_passes(g, b, _NT, passes), _dot_passes(a, g, _TN, passes)
    if dims == _NT:
        return _dot_passes(g, b, _NN, passes), _dot_passes(g, a, _TN, passes)
    return _dot_passes(b, g, _NT, passes), _dot_passes(a, g, _NN, passes)


_dot_vjp.defvjp(_dot_fwd, _dot_bwd)


def _doth(a, b, dims=_NN):
    return _dot_vjp(a, b, dims, 3)


def _dotb(a, b, dims=_NN):
    return _dot_vjp(a, b, dims, 1)


def _softplus(x):
    return jnp.where(x > 0, x, 0.0) + jnp.log(1.0 + jnp.exp(jnp.where(x > 0, -x, x)))


def _rwkv_pre(k, xl, gd, w0, w2p, a0, a2p, g2, k_k, k_a, seg):
    z = w0 + _dotb(jnp.tanh(xl), w2p)
    lw = -jnp.exp(-_softplus(-z) - 0.5)
    alpha = _sigmoid(a0 + _dotb(xl, a2p))
    g = _dotb(_sigmoid(gd), g2)
    kk = k * k_k
    kk = kk / jnp.maximum(jnp.sqrt(_dotb(kk * kk, seg)), 1e-12)
    k2 = k * (1.0 + (alpha - 1.0) * k_a)
    return lw, k2, -kk, kk * alpha, g


def _rwkv_post(y, r, k2, v, g, lnx_g, lnx_b, r_k, seg):
    mean = _dotb(y, seg) * (1.0 / HEAD_DIM)
    yc = y - mean
    var = _dotb(yc * yc, seg) * (1.0 / HEAD_DIM)
    yo = yc * lax.rsqrt(var + RWKV_GN_EPS) * lnx_g + lnx_b
    bonus = _dotb(r * k2 * r_k, seg) * v
    return (yo + bonus) * g


def _rwkv_pre_fwd(name, prs, prm, seg, tr):
    t = prs.shape[0]
    row = lambda i: (i, 0)
    fix = lambda i: (0, 0)
    ins = [(prs, (tr, D_R), lambda i: (i, 1)), (prs, (tr, LANES), lambda i: (i, 12)), (prs, (tr, LANES), lambda i: (i, 13)),
           (prm["w0"], (1, D_R), fix), (prm["w2p"], (LANES, D_R), fix), (prm["a0"], (1, D_R), fix),
           (prm["a2p"], (LANES, D_R), fix), (prm["g2"], (LANES, D_R), fix), (prm["k_k"], (1, D_R), fix),
           (prm["k_a"], (1, D_R), fix), (seg, (D_R, D_R), fix)]
    return _call(_rwkv_pre, name, (t // tr,), ins, [((t, D_R), (tr, D_R), row, False)] * 5)


def _rwkv_pre_bwd(name, prs, prm, seg, cts, tr):
    t = prs.shape[0]

    def fn(k, xl, gd, w0, w2p, a0, a2p, g2, k_k, k_a, segv, *ct):
        _, vjp = jax.vjp(lambda *a: _rwkv_pre(*a, segv), k, xl, gd, w0, w2p, a0, a2p, g2, k_k, k_a)
        return vjp(tuple(ct))

    row = lambda i: (i, 0)
    fix = lambda i: (0, 0)
    ins = [(prs, (tr, D_R), lambda i: (i, 1)), (prs, (tr, LANES), lambda i: (i, 12)), (prs, (tr, LANES), lambda i: (i, 13)),
           (prm["w0"], (1, D_R), fix), (prm["w2p"], (LANES, D_R), fix), (prm["a0"], (1, D_R), fix),
           (prm["a2p"], (LANES, D_R), fix), (prm["g2"], (LANES, D_R), fix), (prm["k_k"], (1, D_R), fix),
           (prm["k_a"], (1, D_R), fix), (seg, (D_R, D_R), fix)] + [(c, (tr, D_R), row) for c in cts]
    outs = [((t, D_R), (tr, D_R), row, False), ((t, LANES), (tr, LANES), row, False), ((t, LANES), (tr, LANES), row, False),
            ((1, D_R), (1, D_R), fix, True), ((LANES, D_R), (LANES, D_R), fix, True), ((1, D_R), (1, D_R), fix, True),
            ((LANES, D_R), (LANES, D_R), fix, True), ((LANES, D_R), (LANES, D_R), fix, True),
            ((1, D_R), (1, D_R), fix, True), ((1, D_R), (1, D_R), fix, True)]
    return _call(fn, name, (t // tr,), ins, outs, acc_axis=0)


def _rwkv_post_ins(y, prs, k2, g, prm, seg, tr):
    row = lambda i: (i, 0)
    fix = lambda i: (0, 0)
    return [(y, (tr, D_R), row), (prs, (tr, D_R), row), (k2, (tr, D_R), row), (prs, (tr, D_R), lambda i: (i, 2)),
            (g, (tr, D_R), row), (prm["lnx_g"], (1, D_R), fix), (prm["lnx_b"], (1, D_R), fix), (prm["r_k"], (1, D_R), fix),
            (seg, (D_R, D_R), fix)]


def _rwkv_post_fwd(name, y, prs, k2, g, prm, seg, tr):
    t = y.shape[0]
    return _call(_rwkv_post, name, (t // tr,), _rwkv_post_ins(y, prs, k2, g, prm, seg, tr),
                 [((t, D_R), (tr, D_R), lambda i: (i, 0), False)])


def _rwkv_post_bwd(name, y, prs, k2, g, prm, seg, dy, dy_col, tr):
    t = y.shape[0]

    def fn(yv, r, k2v, v, gv, lg, lb, rk, segv, ct):
        _, vjp = jax.vjp(lambda *a: _rwkv_post(*a, segv), yv, r, k2v, v, gv, lg, lb, rk)
        return vjp(ct)

    row = lambda i: (i, 0)
    fix = lambda i: (0, 0)
    ins = _rwkv_post_ins(y, prs, k2, g, prm, seg, tr) + [(dy, (tr, D_R), lambda i: (i, dy_col))]
    outs = [((t, D_R), (tr, D_R), row, False)] * 5 + [((1, D_R), (1, D_R), fix, True)] * 3
    return _call(fn, name, (t // tr,), ins, outs, acc_axis=0)


def _wkv_chunk(s0, r, lw, k, v, a, b):
    c = r[0].shape[0]
    lane = lax.broadcasted_iota(jnp.int32, (1, 2 * HEAD_DIM), 1)
    first = (lane < HEAD_DIM).astype(F32)
    per_head = lambda x: jnp.concatenate([x * first, x * (1.0 - first)], axis=0)

    def time_of(shape, dim):
        i = lax.broadcasted_iota(jnp.int32, shape, dim)
        return jnp.where(i >= c, i - c, i)

    incl = (lax.broadcasted_iota(jnp.int32, (c, c), 0) >= lax.broadcasted_iota(jnp.int32, (c, c), 1)).astype(F32)
    strict2 = time_of((2 * c, 2 * c), 0) > time_of((2 * c, 2 * c), 1)
    incl2 = lax.broadcasted_iota(jnp.int32, (c, 2 * c), 0) >= time_of((c, 2 * c), 1)
    each = lambda f, *xs: [f(*x) for x in zip(*xs)]
    cum = each(lambda x: _doth(incl, x), lw)
    tot = each(lambda x: jnp.sum(x, axis=0, keepdims=True), lw)
    e_inv = each(lambda x: jnp.exp(-x), cum)
    a_st = each(lambda x, cm, l: per_head(x * jnp.exp(cm - l)), a, cum, lw)
    r_t = each(lambda x, cm: x * jnp.exp(cm), r, cum)
    b_st = each(lambda x, e: per_head(x * e), b, e_inv)
    k_st = each(lambda x, e: per_head(x * e), k, e_inv)
    v_st = each(per_head, v)
    m = each(lambda x, w: jnp.where(strict2, _dotb(x, w, _NT), 0.0), a_st, b_st)
    m_k = each(lambda x, w: jnp.where(strict2, _dotb(x, w, _NT), 0.0), a_st, k_st)
    u = each(lambda x, s, mk, w: _dotb(x, s, _NT) + _dotb(mk, w), a_st, s0, m_k, v_st)
    steps = (c - 1).bit_length()
    for s in range(steps):
        u = each(lambda x, w: x + _dotb(w, x), u, m)
        if s + 1 < steps:
            m = each(lambda w: _dotb(w, w), m)
    n_b = each(lambda x, w: jnp.where(incl2, _dotb(x, w, _NT), 0.0), r_t, b_st)
    n_k = each(lambda x, w: jnp.where(incl2, _dotb(x, w, _NT), 0.0), r_t, k_st)
    y = each(lambda x, s, nb, uu, nk, w: _dotb(x, s, _NT) + _dotb(nb, uu) + _dotb(nk, w), r_t, s0, n_b, u, n_k, v_st)
    dec = each(lambda tt, cm: jnp.exp(tt - cm), tot, cum)
    s1 = each(lambda s, tt, uu, x, d, w, kk: s * jnp.exp(tt) + _dotb(uu, per_head(x * d), _TN) + _dotb(w, per_head(kk * d), _TN),
              s0, tot, u, b, dec, v_st, k)
    return tuple(y), tuple(s1)


WKV_PAIRS_PER_STEP = 4
PAIR = 2 * HEAD_DIM


def _wkv_fwd(name, srcs):
    t = srcs[0][0].shape[0]
    c = _chunk_len(t)
    nc = t // c
    pp = WKV_PAIRS_PER_STEP
    n_pairs = D_R // PAIR

    def body(r, lw, k, v, a, b, y_ref, st_ref, state):
        @pl.when(pl.program_id(1) == 0)
        def _():
            state[...] = jnp.zeros(state.shape, F32)

        pairs = lambda ref: tuple(ref[:, pl.ds(i * PAIR, PAIR)] for i in range(pp))
        s0 = tuple(state[i] for i in range(pp))
        y, s1 = _wkv_chunk(s0, pairs(r), pairs(lw), pairs(k), pairs(v), pairs(a), pairs(b))
        for i in range(pp):
            st_ref[i] = s0[i]
            y_ref[:, pl.ds(i * PAIR, PAIR)] = y[i]
            state[i] = s1[i]

    seq = lambda off: pl.BlockSpec((c, pp * PAIR), lambda g, j: (j, off + g))
    return pl.pallas_call(
        body, name=name, grid=(n_pairs // pp, nc), in_specs=[seq(off) for _, off in srcs],
        out_specs=[seq(0), pl.BlockSpec((pp, None, PAIR, PAIR), lambda g, j: (g, j, 0, 0))],
        out_shape=[jax.ShapeDtypeStruct((t, D_R), F32), jax.ShapeDtypeStruct((n_pairs, nc, PAIR, PAIR), F32)],
        scratch_shapes=[pltpu.VMEM((pp, PAIR, PAIR), F32)],
        compiler_params=_cparams(None),
    )(*[a for a, _ in srcs])


def _wkv_bwd(name, srcs, st, dy):
    t = srcs[0][0].shape[0]
    c = _chunk_len(t)
    nc = t // c
    pp = WKV_PAIRS_PER_STEP
    n_pairs = D_R // PAIR

    def body(r, lw, k, v, a, b, st_ref, dy_ref, dr, dlw, dk, dv, da, db, dstate):
        @pl.when(pl.program_id(1) == 0)
        def _():
            dstate[...] = jnp.zeros(dstate.shape, F32)

        half = lax.broadcasted_iota(jnp.int32, (PAIR, PAIR), 0) < HEAD_DIM
        same_head = half == (lax.broadcasted_iota(jnp.int32, (PAIR, PAIR), 1) < HEAD_DIM)
        pairs = lambda ref: tuple(ref[:, pl.ds(i * PAIR, PAIR)] for i in range(pp))
        s0 = tuple(st_ref[i] for i in range(pp))
        _, vjp = jax.vjp(_wkv_chunk, s0, pairs(r), pairs(lw), pairs(k), pairs(v), pairs(a), pairs(b))
        ds0, *dxs = vjp((pairs(dy_ref), tuple(dstate[i] for i in range(pp))))
        for i in range(pp):
            for ref, val in zip((dr, dlw, dk, dv, da, db), dxs):
                ref[:, pl.ds(i * PAIR, PAIR)] = val[i]
            dstate[i] = jnp.where(same_head, ds0[i], 0.0)

    seq = lambda off: pl.BlockSpec((c, pp * PAIR), lambda g, j: (nc - 1 - j, off + g))
    return pl.pallas_call(
        body, name=name, grid=(n_pairs // pp, nc),
        in_specs=[seq(off) for _, off in srcs]
        + [pl.BlockSpec((pp, None, PAIR, PAIR), lambda g, j: (g, nc - 1 - j, 0, 0)), seq(dy[1])],
        out_specs=[seq(0)] * 6,
        out_shape=[jax.ShapeDtypeStruct((t, D_R), F32)] * 6,
        scratch_shapes=[pltpu.VMEM((pp, PAIR, PAIR), F32)],
        compiler_params=_cparams(None),
    )(*[a for a, _ in srcs], st, dy[0])


def _rope(x, cos, sin, rot):
    return x * cos + _dotb(x, rot) * sin


def _attn_block(nb, q, kp, kc, km, vp, vc, vm, sk, cq, sq, cp, sp, cm, sm, rot):
    g = GQA_GROUP
    scale = HEAD_DIM ** -0.5
    down = lambda x: jnp.concatenate([x] * g, axis=0)
    kpr = _rope(kp, cp, sp, rot)
    kcr = _rope(kc, cq, sq, rot)
    kmr = _rope(km, cm, sm, rot)
    qr = _rope(q, down(cq), down(sq), rot)
    i = lax.broadcasted_iota(jnp.int32, (g * BLOCK, BLOCK), 0)
    i = i - BLOCK * ((i >= BLOCK).astype(jnp.int32) + (i >= 2 * BLOCK).astype(jnp.int32) + (i >= 3 * BLOCK).astype(jnp.int32))
    j = lax.broadcasted_iota(jnp.int32, (g * BLOCK, BLOCK), 1)
    nbv = jnp.zeros((g * BLOCK, BLOCK), jnp.int32) + nb
    ok_p = (j > i) & (nbv >= 2)
    ok_c = (j <= i) & (nbv >= 1)
    ok_m = (j >= BLOCK - N_META) & ((nbv >= 1) | (j <= i))
    sink = jnp.concatenate([jnp.broadcast_to(s, (BLOCK, 1)) for s in sk], axis=0)
    s_p = jnp.where(ok_p, _dotb(qr, kpr, _NT) * scale, NEG_INF)
    s_c = jnp.where(ok_c, _dotb(qr, kcr, _NT) * scale, NEG_INF)
    s_m = jnp.where(ok_m, _dotb(qr, kmr, _NT) * scale, NEG_INF)
    rmax = lambda s: jnp.max(s, axis=-1, keepdims=True)
    m = lax.stop_gradient(jnp.maximum(jnp.maximum(rmax(s_p), rmax(s_c)), jnp.maximum(rmax(s_m), sink)))
    e_p, e_c, e_m = jnp.exp(s_p - m), jnp.exp(s_c - m), jnp.exp(s_m - m)
    rsum = lambda e: jnp.sum(e, axis=-1, keepdims=True)
    inv = 1.0 / (rsum(e_p) + rsum(e_c) + rsum(e_m) + jnp.exp(sink - m))
    return _dotb(e_p * inv, vp) + _dotb(e_c * inv, vc) + _dotb(e_m * inv, vm)


def _attn_specs():
    cur = lambda g, n: (g, n, 0)
    prev = lambda g, n: (g, jnp.maximum(n - 1, 0), 0)
    meta = lambda g, n: (g, 0, 0)
    kv = lambda m: pl.BlockSpec((None, BLOCK, HEAD_DIM), m)
    tab = lambda m: pl.BlockSpec((BLOCK, HEAD_DIM), m)
    tcur, tprev, tmeta = (lambda g, n: (n, 0)), (lambda g, n: (jnp.maximum(n - 1, 0), 0)), (lambda g, n: (0, 0))
    qspec = pl.BlockSpec((GQA_GROUP, BLOCK, HEAD_DIM), cur)
    sspec = pl.BlockSpec((GQA_GROUP, 8, LANES), meta)
    specs = [qspec, kv(prev), kv(cur), kv(meta), kv(prev), kv(cur), kv(meta), sspec,
             tab(tcur), tab(tcur), tab(tprev), tab(tprev), tab(tmeta), tab(tmeta),
             pl.BlockSpec((HEAD_DIM, HEAD_DIM), lambda g, n: (0, 0))]
    return specs, qspec, sspec, kv


def _attn_args(q, k, v, sinks_b, cos, sin, rot):
    return (q, k, k, k, v, v, v, sinks_b, cos, sin, cos, sin, cos, sin, rot)


def _attn_fwd(name, q, k, v, sinks_b, cos, sin, rot):
    tp = q.shape[1]
    specs, qspec, _, _ = _attn_specs()

    def body(q_ref, kp, kc, km, vp, vc, vm, s_ref, cq, sq, cp, sp, cm, sm, rot_ref, o_ref):
        q = jnp.concatenate([q_ref[h] for h in range(GQA_GROUP)], axis=0)
        sk = tuple(s_ref[h][0:1, 0:1] for h in range(GQA_GROUP))
        out = _attn_block(pl.program_id(1), q, kp[...], kc[...], km[...], vp[...], vc[...], vm[...], sk,
                          cq[...], sq[...], cp[...], sp[...], cm[...], sm[...], rot_ref[...])
        for h in range(GQA_GROUP):
            o_ref[h] = out[h * BLOCK:(h + 1) * BLOCK]

    return pl.pallas_call(
        body, name=name, grid=(N_KV_HEADS, tp // BLOCK), in_specs=specs, out_specs=qspec,
        out_shape=jax.ShapeDtypeStruct(q.shape, F32), compiler_params=_cparams(None),
    )(*_attn_args(q, k, v, sinks_b, cos, sin, rot))


def _attn_bwd(name, q, k, v, sinks_b, cos, sin, rot, do):
    tp = q.shape[1]
    nb = tp // BLOCK
    specs, qspec, sspec, kv = _attn_specs()

    def body(q_ref, kp, kc, km, vp, vc, vm, s_ref, cq, sq, cp, sp, cm, sm, rot_ref, do_ref,
             dq_ref, dkp, dkc, dvp, dvc, dkm, dvm, ds_ref):
        n = pl.program_id(1)
        q = jnp.concatenate([q_ref[h] for h in range(GQA_GROUP)], axis=0)
        sk = tuple(s_ref[h][0:1, 0:1] for h in range(GQA_GROUP))
        tabs = (cq[...], sq[...], cp[...], sp[...], cm[...], sm[...], rot_ref[...])
        _, vjp = jax.vjp(lambda *a: _attn_block(n, *a, *tabs), q, kp[...], kc[...], km[...], vp[...], vc[...], vm[...], sk)
        dq, gkp, gkc, gkm, gvp, gvc, gvm, dsk = vjp(jnp.concatenate([do_ref[h] for h in range(GQA_GROUP)], axis=0))
        dkp[...] = gkp
        dkc[...] = gkc
        dvp[...] = gvp
        dvc[...] = gvc
        for h in range(GQA_GROUP):
            dq_ref[h] = dq[h * BLOCK:(h + 1) * BLOCK]

        @pl.when(n == 0)
        def _():
            dkm[...] = gkm
            dvm[...] = gvm
            for h in range(GQA_GROUP):
                ds_ref[h] = jnp.broadcast_to(dsk[h], (8, LANES))

        @pl.when(n != 0)
        def _():
            dkm[...] += gkm
            dvm[...] += gvm
            for h in range(GQA_GROUP):
                ds_ref[h] += jnp.broadcast_to(dsk[h], (8, LANES))

    part = pl.BlockSpec((None, None, BLOCK, HEAD_DIM), lambda g, n: (g, n, 0, 0))
    part_shape = jax.ShapeDtypeStruct((N_KV_HEADS, nb, BLOCK, HEAD_DIM), F32)
    meta_shape = jax.ShapeDtypeStruct((N_KV_HEADS, BLOCK, HEAD_DIM), F32)
    return pl.pallas_call(
        body, name=name, grid=(N_KV_HEADS, nb), in_specs=specs + [qspec],
        out_specs=[qspec, part, part, part, part, kv(lambda g, n: (g, 0, 0)), kv(lambda g, n: (g, 0, 0)), sspec],
        out_shape=[jax.ShapeDtypeStruct(q.shape, F32), part_shape, part_shape, part_shape, part_shape,
                   meta_shape, meta_shape, jax.ShapeDtypeStruct(sinks_b.shape, F32)],
        compiler_params=_cparams(None),
    )(*_attn_args(q, k, v, sinks_b, cos, sin, rot), do)


def _kv_combine(name, prev_part, own_part, meta):
    g, nb = own_part.shape[:2]

    def fn(own, nxt, mt):
        m = pl.program_id(1)
        one = jnp.ones((BLOCK, HEAD_DIM), F32)
        use_next = jnp.where(one * m < nb - 1, 1.0, 0.0)
        use_meta = jnp.where(one * m < 1, 1.0, 0.0)
        return own + nxt * use_next + mt * use_meta

    blk = (None, None, BLOCK, HEAD_DIM)
    return _call(fn, name, (g, nb),
                 [(own_part, blk, lambda a, m: (a, m, 0, 0)),
                  (prev_part, blk, lambda a, m: (a, jnp.minimum(m + 1, nb - 1), 0, 0)),
                  (meta, (None, BLOCK, HEAD_DIM), lambda a, m: (a, 0, 0))],
                 [((g, nb * BLOCK, HEAD_DIM), (None, BLOCK, HEAD_DIM), lambda a, m: (a, m, 0), False)])


PACK_W = 1024
ELEMENTWISE_BLOCK_BYTES = 1 << 21


def _rows_tile(rows, cols):
    cap = max(8, ELEMENTWISE_BLOCK_BYTES // (4 * cols))
    for d in range(min(rows, cap), 0, -1):
        if rows % d == 0 and d % 8 == 0:
            return d
    return rows


def _adamw(name, w, g, m, v):
    rows, cols = w.shape
    tr = _rows_tile(rows, cols)

    def fn(wv, gv, mv, vv):
        m1 = ADAM_B1 * mv + (1.0 - ADAM_B1) * gv
        v1 = ADAM_B2 * vv + (1.0 - ADAM_B2) * (gv * gv)
        m_hat = m1 / (1.0 - ADAM_B1 ** ADAM_STEP)
        v_hat = v1 / (1.0 - ADAM_B2 ** ADAM_STEP)
        return -ADAM_LR * (m_hat / (jnp.sqrt(v_hat) + ADAM_EPS) + ADAM_WD * wv), m1, v1

    blk = (tr, cols)
    row = lambda i: (i, 0)
    return _call(fn, name, (rows // tr,), [(a, blk, row) for a in (w, g, m, v)], [((rows, cols), blk, row, False)] * 3)


def _pair_add(name, g, recv, c_idx, out_dtype):
    s, a, b = g.shape
    half = a // 2

    def body(c_ref, a_ref, b_ref, o_ref):
        o_ref[...] = (a_ref[...] + b_ref[...]).astype(out_dtype)

    blk = (None, half, b)
    return pl.pallas_call(
        body, name=name,
        grid_spec=pltpu.PrefetchScalarGridSpec(
            num_scalar_prefetch=1, grid=(s,),
            in_specs=[pl.BlockSpec(blk, lambda j, c: (j, c[0], 0)), pl.BlockSpec(blk, lambda j, c: (j, 0, 0))],
            out_specs=pl.BlockSpec(blk, lambda j, c: (j, 0, 0))),
        out_shape=jax.ShapeDtypeStruct((s, half, b), out_dtype), compiler_params=_cparams(None),
    )(c_idx, g, recv)


def _pair_add_placed(name, g, recv, cm_idx, out_dtype):
    s, a, b = g.shape
    half = a // 2

    def body(cm_ref, a_ref, b_ref, o_ref, own_ref):
        val = (a_ref[...] + b_ref[...]).astype(out_dtype)
        o_ref[...] = val

        @pl.when(pl.program_id(0) == cm_ref[1])
        def _():
            own_ref[...] = val

    blk = (None, half, b)
    shape = jax.ShapeDtypeStruct((s, half, b), out_dtype)
    return pl.pallas_call(
        body, name=name,
        grid_spec=pltpu.PrefetchScalarGridSpec(
            num_scalar_prefetch=1, grid=(s,),
            in_specs=[pl.BlockSpec(blk, lambda j, cm: (j, cm[0], 0)), pl.BlockSpec(blk, lambda j, cm: (j, 0, 0))],
            out_specs=[pl.BlockSpec(blk, lambda j, cm: (j, 0, 0)), pl.BlockSpec(blk, lambda j, cm: (cm[1], 0, 0))]),
        out_shape=[shape, shape], compiler_params=_cparams(None),
    )(cm_idx, g, recv)


def _sum_chips(name, parts, c_idx, layer, n_layers, into=None):
    _, a, b = parts.shape
    tr = _rows_tile(a, b)

    def body(c_ref, p0, p1, p2, p3, *rest):
        o_ref = rest[-1]
        up = lambda p: p[...].astype(F32)
        o_ref[...] = ((up(p0) + up(p1)) + up(p2)) + up(p3)

    in_specs = [pl.BlockSpec((None, tr, b), lambda i, c, k=k: (k, i, 0)) for k in range(N_CHIPS)]
    args = [c_idx] + [parts] * N_CHIPS
    aliases = {}
    if into is not None:
        in_specs.append(_ANY)
        args.append(into)
        aliases = {1 + N_CHIPS: 0}
    return pl.pallas_call(
        body, name=name,
        grid_spec=pltpu.PrefetchScalarGridSpec(
            num_scalar_prefetch=1, grid=(a // tr,), in_specs=in_specs,
            out_specs=pl.BlockSpec((None, None, tr, b), lambda i, c: (layer, c[0], i, 0))),
        out_shape=jax.ShapeDtypeStruct((n_layers, 2, a, b), F32), input_output_aliases=aliases,
        compiler_params=_cparams(None),
    )(*args)


def _place_own_block(name, w, layer, me_idx, dtype):
    _, a2, b = w.shape
    a = a2 // 2
    tr = _rows_tile(a, b)
    nb = a // tr

    def body(me_ref, w_ref, o_ref):
        o_ref[...] = w_ref[...].astype(dtype)

    return pl.pallas_call(
        body, name=name,
        grid_spec=pltpu.PrefetchScalarGridSpec(
            num_scalar_prefetch=1, grid=(2, nb),
            in_specs=[pl.BlockSpec((None, tr, b), lambda h, i, me: (layer, h * nb + i, 0))],
            out_specs=pl.BlockSpec((None, None, tr, b), lambda h, i, me: (me[0], h, i, 0))),
        out_shape=jax.ShapeDtypeStruct((N_CHIPS, 2, a, b), dtype), compiler_params=_cparams(None),
    )(me_idx, w)


def _mesh_pos():
    return lax.axis_index("x"), lax.axis_index("y"), lax.axis_index("c")


def _other_chips(x, y):
    return [(1 - x, y), (x, 1 - y), (1 - x, 1 - y)]


_ANY = pl.BlockSpec(memory_space=pl.ANY)


def _gather_weights(name, bufs, from_chips=True):
    n = len(bufs)

    def body(*refs):
        out_refs = refs[n:2 * n]
        send_sems, recv_sems = refs[2 * n:]
        x, y, c = _mesh_pos()
        me = 2 * x + y
        sibling = (x, y, 1 - c)
        chips = _other_chips(x, y)

        def copy(i, k, chip_idx, half, to):
            return pltpu.make_async_remote_copy(src_ref=out_refs[i].at[chip_idx, half], dst_ref=out_refs[i].at[chip_idx, half],
                                                send_sem=send_sems.at[6 * i + k], recv_sem=recv_sems.at[6 * i + k],
                                                device_id=to, device_id_type=MESH)

        first = [copy(i, j, me, c, (*chip, c)) for i in range(n) for j, chip in enumerate(chips)] if from_chips else []
        for cp in first:
            cp.start()
        passed = []
        for i in range(n):
            for j, (cx, cy) in enumerate(chips):
                idx = 2 * cx + cy
                if from_chips:
                    copy(i, j, idx, c, sibling).wait_recv()
                fwd = copy(i, 3 + j, idx, c, sibling)
                fwd.start()
                passed.append(fwd)
        for i in range(n):
            for j, (cx, cy) in enumerate(chips):
                copy(i, 3 + j, 2 * cx + cy, 1 - c, sibling).wait_recv()
        for cp in first + passed:
            cp.wait_send()

    return pl.pallas_call(
        body, name=name, in_specs=[_ANY] * n, out_specs=[_ANY] * n,
        out_shape=[jax.ShapeDtypeStruct(b.shape, b.dtype) for b in bufs],
        input_output_aliases={i: i for i in range(n)},
        scratch_shapes=[pltpu.SemaphoreType.DMA((6 * n,)), pltpu.SemaphoreType.DMA((6 * n,))],
        compiler_params=pltpu.CompilerParams(has_side_effects=True),
    )(*bufs)


def _gather_start(name, groups):
    bufs = [b for g in groups for b in g]
    n = len(bufs)
    ng = len(groups)

    def body(*refs):
        b_refs = refs[:n]
        sems = refs[n:n + 2 * ng]
        token = refs[-1]
        x, y, c = _mesh_pos()
        me = 2 * x + y
        i = 0
        for gi, g in enumerate(groups):
            for k in range(len(g)):
                for j, (cx, cy) in enumerate(_other_chips(x, y)):
                    pltpu.make_async_remote_copy(src_ref=b_refs[i].at[me, c], dst_ref=b_refs[i].at[me, c],
                                                 send_sem=sems[2 * gi].at[3 * k + j], recv_sem=sems[2 * gi + 1].at[3 * k + j],
                                                 device_id=(cx, cy, c), device_id_type=MESH).start()
                i += 1
        token[...] = jnp.zeros(token.shape, F32)

    sem_shapes = [pltpu.SemaphoreType.DMA((3 * len(g),)) for g in groups for _ in range(2)]
    res = pl.pallas_call(
        body, name=name,
        out_shape=(*sem_shapes, *[pltpu.HBM(b.shape, b.dtype) for b in bufs], jax.ShapeDtypeStruct((8, LANES), F32)),
        in_specs=[_HBM] * n,
        out_specs=(*[_SEM] * (2 * ng), *[_HBM] * n, pl.BlockSpec(memory_space=pltpu.VMEM)),
        input_output_aliases={i: 2 * ng + i for i in range(n)},
        compiler_params=pltpu.CompilerParams(has_side_effects=_DATAFLOW),
    )(*[pltpu.with_memory_space_constraint(b, pltpu.HBM) for b in bufs])
    out, i = [], 2 * ng
    for gi, g in enumerate(groups):
        out.append((res[2 * gi], res[2 * gi + 1], list(res[i:i + len(g)])))
        i += len(g)
    return out, res[-1]


def _gather_wait(name, send_sems, recv_sems, bufs, after):
    n = len(bufs)

    def body(*refs):
        b_refs = refs[:n]
        s_sems, r_sems = refs[n], refs[n + 1]
        x, y, c = _mesh_pos()
        me = 2 * x + y
        for k in range(n):
            for j, (cx, cy) in enumerate(_other_chips(x, y)):
                idx = 2 * cx + cy
                copy = pltpu.make_async_remote_copy(src_ref=b_refs[k].at[me, c], dst_ref=b_refs[k].at[idx, c],
                                                    send_sem=s_sems.at[3 * k + j], recv_sem=r_sems.at[3 * k + j],
                                                    device_id=(cx, cy, c), device_id_type=MESH)
                copy.wait_send()
                copy.wait_recv()

    res = pl.pallas_call(
        body, name=name,
        out_shape=tuple(pltpu.HBM(b.shape, b.dtype) for b in bufs),
        in_specs=[_HBM] * n + [_SEM, _SEM, _ANY],
        out_specs=tuple([_HBM] * n),
        input_output_aliases={i: i for i in range(n)},
        compiler_params=pltpu.CompilerParams(has_side_effects=_DATAFLOW),
    )(*bufs, send_sems, recv_sems, after)
    return list(res)


def _halves_to_sibling(name, units):
    n = len(units)

    def body(*refs):
        g_refs, out_refs = refs[:n], refs[n:2 * n]
        send_sems, recv_sems = refs[2 * n:]
        x, y, c = _mesh_pos()
        cps = []
        for i in range(n):
            half = units[i].shape[1] // 2
            src = g_refs[i].at[pl.ds(0, N_CHIPS), pl.ds((1 - c) * half, half)]
            cp = pltpu.make_async_remote_copy(src_ref=src, dst_ref=out_refs[i], send_sem=send_sems.at[i],
                                              recv_sem=recv_sems.at[i], device_id=(x, y, 1 - c), device_id_type=MESH)
            cp.start()
            cps.append(cp)
        for cp in cps:
            cp.wait()

    return pl.pallas_call(
        body, name=name, in_specs=[_ANY] * n, out_specs=[_ANY] * n,
        out_shape=[jax.ShapeDtypeStruct((u.shape[0], u.shape[1] // 2, u.shape[2]), u.dtype) for u in units],
        scratch_shapes=[pltpu.SemaphoreType.DMA((n,)), pltpu.SemaphoreType.DMA((n,))],
        compiler_params=pltpu.CompilerParams(has_side_effects=True),
    )(*units)


def _scatter_to_chips(name, sums):
    n = len(sums)

    def body(*refs):
        h_refs, out_refs = refs[:n], refs[n:2 * n]
        send_sems, recv_sems, local_sems = refs[2 * n:]
        x, y, c = _mesh_pos()
        me = 2 * x + y
        chips = _other_chips(x, y)
        local = [pltpu.make_async_copy(h_refs[i].at[me], out_refs[i].at[me], local_sems.at[i]) for i in range(n)]
        for cp in local:
            cp.start()

        def copy(i, j, src_idx, dst_idx):
            cx, cy = chips[j]
            return pltpu.make_async_remote_copy(src_ref=h_refs[i].at[src_idx], dst_ref=out_refs[i].at[dst_idx],
                                                send_sem=send_sems.at[3 * i + j], recv_sem=recv_sems.at[3 * i + j],
                                                device_id=(cx, cy, c), device_id_type=MESH)

        cps = [copy(i, j, 2 * chips[j][0] + chips[j][1], me) for i in range(n) for j in range(3)]
        for cp in cps:
            cp.start()
        for i in range(n):
            for j in range(3):
                copy(i, j, me, 2 * chips[j][0] + chips[j][1]).wait_recv()
        for cp in cps:
            cp.wait_send()
        for cp in local:
            cp.wait()

    return pl.pallas_call(
        body, name=name, in_specs=[_ANY] * n, out_specs=[_ANY] * n,
        out_shape=[jax.ShapeDtypeStruct(s.shape, s.dtype) for s in sums],
        scratch_shapes=[pltpu.SemaphoreType.DMA((3 * n,)), pltpu.SemaphoreType.DMA((3 * n,)), pltpu.SemaphoreType.DMA((n,))],
        compiler_params=pltpu.CompilerParams(has_side_effects=True),
    )(*sums)


_HBM = pl.BlockSpec(memory_space=pltpu.HBM)
_SEM = pl.BlockSpec(memory_space=pltpu.SEMAPHORE)
_DATAFLOW = pltpu.SideEffectType.DATAFLOW_SIDE_EFFECTING


def _scatter_start(name, sums, zones):
    n = len(sums)

    def body(*refs):
        h_refs, z_refs = refs[:n], refs[n:2 * n]
        send_sems, recv_sems = refs[2 * n], refs[2 * n + 1]
        token = refs[-1]
        x, y, c = _mesh_pos()
        me = 2 * x + y
        for i in range(n):
            for j, (cx, cy) in enumerate(_other_chips(x, y)):
                pltpu.make_async_remote_copy(src_ref=h_refs[i].at[2 * cx + cy], dst_ref=z_refs[i].at[me],
                                             send_sem=send_sems.at[3 * i + j], recv_sem=recv_sems.at[3 * i + j],
                                             device_id=(cx, cy, c), device_id_type=MESH).start()
        token[...] = jnp.zeros(token.shape, F32)

    hbm = lambda a: pltpu.HBM(a.shape, a.dtype)
    res = pl.pallas_call(
        body, name=name,
        out_shape=(pltpu.SemaphoreType.DMA((3 * n,)), pltpu.SemaphoreType.DMA((3 * n,)),
                   *[hbm(a) for a in sums], *[hbm(a) for a in zones], jax.ShapeDtypeStruct((8, LANES), F32)),
        in_specs=[_HBM] * (2 * n),
        out_specs=(_SEM, _SEM, *[_HBM] * (2 * n), pl.BlockSpec(memory_space=pltpu.VMEM)),
        input_output_aliases={i: 2 + i for i in range(2 * n)},
        compiler_params=pltpu.CompilerParams(has_side_effects=_DATAFLOW),
    )(*[pltpu.with_memory_space_constraint(a, pltpu.HBM) for a in list(sums) + list(zones)])
    return res[0], res[1], res[2:2 + n], res[2 + n:2 + 2 * n], res[-1]


def _scatter_wait(name, send_sems, recv_sems, sums, zones, after):
    n = len(sums)

    def body(*refs):
        h_refs, z_refs = refs[:n], refs[n:2 * n]
        s_sems, r_sems = refs[2 * n], refs[2 * n + 1]
        x, y, c = _mesh_pos()
        me = 2 * x + y
        for i in range(n):
            for j, (cx, cy) in enumerate(_other_chips(x, y)):
                idx = 2 * cx + cy
                copy = pltpu.make_async_remote_copy(src_ref=h_refs[i].at[idx], dst_ref=z_refs[i].at[idx],
                                                    send_sem=s_sems.at[3 * i + j], recv_sem=r_sems.at[3 * i + j],
                                                    device_id=(cx, cy, c), device_id_type=MESH)
                copy.wait_send()
                copy.wait_recv()

    hbm = lambda a: pltpu.HBM(a.shape, a.dtype)
    res = pl.pallas_call(
        body, name=name,
        out_shape=(*[hbm(a) for a in sums], *[hbm(a) for a in zones]),
        in_specs=[_HBM] * (2 * n) + [_SEM, _SEM, _ANY],
        out_specs=tuple([_HBM] * (2 * n)),
        input_output_aliases={i: i for i in range(2 * n)},
        compiler_params=pltpu.CompilerParams(has_side_effects=_DATAFLOW),
    )(*sums, *zones, send_sems, recv_sems, after)
    return res[n:]


def _join_halves(name, results):
    n = len(results)
    pieces = [(i, l) for i in range(n) for l in range(results[i].shape[0])]

    def body(*refs):
        out_refs = refs[n:2 * n]
        send_sems, recv_sems = refs[2 * n:]
        x, y, c = _mesh_pos()

        def copy(k, half):
            i, l = pieces[k]
            return pltpu.make_async_remote_copy(src_ref=out_refs[i].at[l, half], dst_ref=out_refs[i].at[l, half],
                                                send_sem=send_sems.at[k], recv_sem=recv_sems.at[k],
                                                device_id=(x, y, 1 - c), device_id_type=MESH)

        cps = [copy(k, c) for k in range(len(pieces))]
        for cp in cps:
            cp.start()
        for k in range(len(pieces)):
            copy(k, 1 - c).wait_recv()
        for cp in cps:
            cp.wait_send()

    return pl.pallas_call(
        body, name=name, in_specs=[_ANY] * n, out_specs=[_ANY] * n,
        out_shape=[jax.ShapeDtypeStruct(r.shape, r.dtype) for r in results],
        input_output_aliases={i: i for i in range(n)},
        scratch_shapes=[pltpu.SemaphoreType.DMA((len(pieces),)), pltpu.SemaphoreType.DMA((len(pieces),))],
        compiler_params=pltpu.CompilerParams(has_side_effects=True),
    )(*results)


def _pack(arrays, dtype, rows_multiple):
    flat = jnp.concatenate([a.reshape(-1).astype(dtype) for a in arrays])
    unit = rows_multiple * PACK_W
    total = -(-flat.shape[0] // unit) * unit
    return jnp.pad(flat, (0, total - flat.shape[0])).reshape(total // PACK_W, PACK_W)


def _unpack(flat, shapes):
    out, off = [], 0
    for s in shapes:
        n = 1
        for d in s:
            n *= d
        out.append(flat[..., off:off + n].reshape(flat.shape[:-1] + tuple(s)))
        off += n
    return out


def _ffn_fwd(tag, l, h, g, w_up, conv, bias, w_down, tm):
    hn = _rms_fwd(f"{tag}_norm", h, g, tm)
    u = _mm_cs(f"{tag}_up", hn, w_up, l, tm)
    act = _ffn_col_fwd(f"{tag}_glu", u, conv, bias)
    h_out = _mm_full(f"{tag}_down", act, w_down, l, tm, D_FF // 2, add=h)
    return h_out, (hn, u, act)


def _ffn_bwd(tag, l, h, g, w_up, conv, bias, w_down, saved, dh, tm):
    hn, u, act = saved
    da = _mm_nt_full(f"{tag}_down_dx", dh, w_down, l, tm, D_FF // 2)
    dw_down = _mm_tn_full(f"{tag}_down_dw", act, dh, tm, D_FF // 2)
    du, dconv, dbias = _ffn_col_bwd(f"{tag}_glu_bwd", u, da, conv, bias)
    dw_up = _mm_tn_cs(f"{tag}_up_dw", hn, du, N_CHIPS, tm)
    dhn = _mm_nt_cs(f"{tag}_up_dx", du, w_up, l, tm)
    dh, dg = _rms_bwd(f"{tag}_norm_bwd", h, g, dhn, dh, tm)
    return dh, dict(norm=dg, w_up=dw_up, conv=dconv, bias=dbias, w_down=dw_down)


def _to_heads(z, nh, pad):
    t = z.shape[0]
    return jnp.pad(z.reshape(t, nh, HEAD_DIM).transpose(1, 0, 2), ((0, 0), (pad, 0), (0, 0)))


def _from_heads(z, pad):
    nh, tp, _ = z.shape
    return z[:, pad:].transpose(1, 0, 2).reshape(tp - pad, nh * HEAD_DIM)


def _rope_tables(tp, pad):
    half = HEAD_DIM // 2
    inv = ROPE_THETA ** (-jnp.arange(half, dtype=F32) / half)
    ang = (jnp.arange(tp, dtype=F32) - pad)[:, None] * inv[None, :]
    cos, sin = jnp.cos(ang), jnp.sin(ang)
    rot = jnp.zeros((HEAD_DIM, HEAD_DIM), F32)
    idx = jnp.arange(half)
    rot = rot.at[idx + half, idx].set(-1.0).at[idx, idx + half].set(1.0)
    return jnp.concatenate([cos, cos], axis=1), jnp.concatenate([sin, sin], axis=1), rot


def _local_step(x, tgt, w, on_grads=None, fetch=None):
    emit = on_grads if on_grads is not None else (lambda tag, units: 0.0)
    need = (lambda tag, after: w) if fetch is None else (lambda tag, after: {**w, **fetch(tag, after)})
    seq = x.shape[0]
    t = seq + N_META
    tm = _row_tile(t, 704)
    tr = _row_tile(t, 352)
    pad = BLOCK - N_META
    grads = {}

    h0 = jnp.concatenate([w["meta_tokens"], x], axis=0)
    tgt_p = jnp.pad(tgt, ((N_META, 0), (0, 0)))

    hn0 = _rms_fwd("l0_norm", h0, w["norm_mix"][0:1], tm)
    p0 = _mm_cs("l0_in", hn0, w["ev_w_in"], 0, tm)
    uc, yb = _even_col_fwd("l0_convs", p0, w["ev_conv_a"], w["ev_conv_b"])
    ya = _even_ln_fwd("l0_ln", uc, w["ev_ln_a_g"], w["ev_ln_a_b"], tm)
    y0 = jnp.concatenate([ya, yb], axis=1)
    w = need("ev_out", y0)
    h1 = _mm_full("l0_out", y0, w["ev_w_out"], 0, tm, D_MODEL, add=h0)
    w = need("f0", h1)
    f0 = (0, h1, w["norm_ffn"][0:1], w["ff_w_up0"], w["ff_conv"][0], w["ff_conv_b"][0:1], w["ff_w_down0"])
    h2, ffn0 = _ffn_fwd("f0", *f0, tm)
    w = need("od", h2)

    hn2 = _rms_fwd("l1_norm", h2, w["norm_mix"][1:2], tm)
    p1 = _mm_cs("l1_in", hn2, w["od_w_in"], 0, tm)
    cos, sin, rot = _rope_tables(t + pad, pad)
    qh = _to_heads(p1[:, :D_ATT], N_Q_HEADS, pad)
    kh = _to_heads(p1[:, D_ATT:D_ATT + D_KV], N_KV_HEADS, pad)
    vh = _to_heads(p1[:, D_ATT + D_KV:D_ATT + 2 * D_KV], N_KV_HEADS, pad)
    sinks_b = jnp.broadcast_to(w["od_sinks"].reshape(N_Q_HEADS, 1, 1), (N_Q_HEADS, 8, LANES))
    y_att = _from_heads(_attn_fwd("l1_attn", qh, kh, vh, sinks_b, cos, sin, rot), pad)

    col0 = (D_ATT + 2 * D_KV) // LANES
    ch = jnp.arange(D_R) // HEAD_DIM
    seg = (ch[:, None] == ch[None, :]).astype(F32)
    prm = dict(w0=w["od_w0"], a0=w["od_a0"], g2=w["od_g2"], k_k=w["od_k_k"], k_a=w["od_k_a"],
               lnx_g=w["od_lnx_g"], lnx_b=w["od_lnx_b"], r_k=w["od_r_k"].reshape(1, D_R),
               w2p=jnp.concatenate([w["od_w2"], jnp.zeros((LORA_A, D_R), F32)], axis=0),
               a2p=jnp.concatenate([jnp.zeros((LORA_W, D_R), F32), w["od_a2"]], axis=0))
    prs = _shift_fwd("l1_shift", p1, col0, w["od_mu"])
    lw, k2, a_, b_, gate_r = _rwkv_pre_fwd("l1_rwkv_pre", prs, prm, seg, tr)
    v_off = 2 * D_R // (WKV_PAIRS_PER_STEP * PAIR)
    scan_in = [(prs, 0), (lw, 0), (k2, 0), (prs, v_off), (a_, 0), (b_, 0)]
    y_scan, states = _wkv_fwd("l1_wkv", scan_in)
    y_rwkv = _rwkv_post_fwd("l1_rwkv_post", y_scan, prs, k2, gate_r, prm, seg, tr)
    y1 = jnp.concatenate([y_att, y_rwkv], axis=1).astype(MXU_DTYPE)
    h3 = _mm_full("l1_out", y1, w["od_w_out"], 0, tm, D_MODEL, add=h2)
    w = need("f1", h3)
    f1 = (0, h3, w["norm_ffn"][1:2], w["ff_w_up1"], w["ff_conv"][1], w["ff_conv_b"][1:2], w["ff_w_down1"])
    h4, ffn1 = _ffn_fwd("f1", *f1, tm)

    loss_blk, dh, d_norm_final = _final_loss("final", h4, w["norm_final"], tgt_p, tm)
    grads["norm_final"] = d_norm_final

    dh, gf1 = _ffn_bwd("f1", *f1, ffn1, dh, tm)
    zero = emit("f1", {"ff_w_up1": gf1["w_up"], "ff_w_down1": gf1["w_down"].reshape(N_CHIPS, D_FF // N_CHIPS, D_MODEL)})
    prm = dict(prm, lnx_g=prm["lnx_g"] + zero)
    dy1 = _mm_nt_full("l1_out_dx", dh, w["od_w_out"], 0, tm, D_MODEL)
    grads["od_w_out"] = _mm_tn_full("l1_out_dw", y1, dh, tm, D_MODEL // 2)
    dy_scan, dr_p, dk2_p, dv_p, dgate_r, grads["od_lnx_g"], grads["od_lnx_b"], d_rk = _rwkv_post_bwd(
        "l1_rwkv_post_bwd", y_scan, prs, k2, gate_r, prm, seg, dy1, 1, tr)
    grads["od_r_k"] = d_rk.reshape(N_R_HEADS, HEAD_DIM)
    dr_s, dlw, dk2_s, dv_s, da_, db_ = _wkv_bwd("l1_wkv_bwd", scan_in, states, (dy_scan, 0))
    dk, dxl, dgd, grads["od_w0"], dw2p, grads["od_a0"], da2p, grads["od_g2"], grads["od_k_k"], grads["od_k_a"] = (
        _rwkv_pre_bwd("l1_rwkv_pre_bwd", prs, prm, seg, (dlw, dk2_s + dk2_p, da_, db_, dgate_r), tr))
    grads["od_w2"] = dw2p[:LORA_W]
    grads["od_a2"] = da2p[LORA_W:]
    dprs = jnp.concatenate([dr_s + dr_p, dk, dv_s + dv_p, dxl, dgd], axis=1)
    dpr, grads["od_mu"] = _shift_bwd("l1_shift_bwd", p1, col0, w["od_mu"], dprs)
    doh = _to_heads(dy1[:, :D_ATT], N_Q_HEADS, pad)
    dqh, dkp, dkc, dvp, dvc, dkm, dvm, dsinks = _attn_bwd("l1_attn_bwd", qh, kh, vh, sinks_b, cos, sin, rot, doh)
    grads["od_sinks"] = dsinks[:, 0, 0].reshape(1, N_Q_HEADS)
    dkh = _kv_combine("l1_attn_dk", dkp, dkc, dkm)
    dvh = _kv_combine("l1_attn_dv", dvp, dvc, dvm)
    dp1 = jnp.concatenate([_from_heads(dqh, pad), _from_heads(dkh, pad), _from_heads(dvh, pad), dpr], axis=1).astype(MXU_DTYPE)
    grads["od_w_in"] = _mm_tn_cs("l1_in_dw", hn2, dp1, N_CHIPS, tm)
    dhn2 = _mm_nt_cs("l1_in_dx", dp1, w["od_w_in"], 0, tm)
    dh, d_mix1 = _rms_bwd("l1_norm_bwd", h2, w["norm_mix"][1:2], dhn2, dh, tm)

    zero = emit("od", {"od_w_out": grads["od_w_out"].reshape(N_CHIPS, D_MODEL // N_CHIPS, D_MODEL), "od_w_in": grads["od_w_in"]})
    f0 = f0[:5] + (f0[5] + zero,) + f0[6:]
    dh, gf0 = _ffn_bwd("f0", *f0, ffn0, dh, tm)
    zero = emit("f0", {"ff_w_up0": gf0["w_up"], "ff_w_down0": gf0["w_down"].reshape(N_CHIPS, D_FF // N_CHIPS, D_MODEL)})
    w = dict(w, ev_ln_a_g=w["ev_ln_a_g"] + zero)
    dy0 = _mm_nt_full("l0_out_dx", dh, w["ev_w_out"], 0, tm, D_MODEL)
    grads["ev_w_out"] = _mm_tn_full("l0_out_dw", y0, dh, tm, D_MODEL // 2)
    duc, grads["ev_ln_a_g"], grads["ev_ln_a_b"] = _even_ln_bwd("l0_ln_bwd", uc, w["ev_ln_a_g"], w["ev_ln_a_b"], dy0, 0, tm)
    *dparts, grads["ev_conv_a"], grads["ev_conv_b"] = _even_col_bwd("l0_convs_bwd", p0, duc, dy0, w["ev_conv_a"], w["ev_conv_b"])
    dp0 = jnp.concatenate(dparts, axis=1)
    grads["ev_w_in"] = _mm_tn_cs("l0_in_dw", hn0, dp0, N_CHIPS, tm)
    dhn0 = _mm_nt_cs("l0_in_dx", dp0, w["ev_w_in"], 0, tm)
    dh, d_mix0 = _rms_bwd("l0_norm_bwd", h0, w["norm_mix"][0:1], dhn0, dh, tm)

    grads["norm_mix"] = jnp.concatenate([d_mix0, d_mix1], axis=0)
    grads["norm_ffn"] = jnp.concatenate([gf0["norm"], gf1["norm"]], axis=0)
    grads["ff_w_up"] = [gf0["w_up"], gf1["w_up"]]
    grads["ff_conv"] = jnp.stack([gf0["conv"], gf1["conv"]])
    grads["ff_conv_b"] = jnp.concatenate([gf0["bias"], gf1["bias"]], axis=0)
    grads["ff_w_down"] = [gf0["w_down"], gf1["w_down"]]
    grads["meta_tokens"] = dh[:N_META]
    return loss_blk[0, 0], dh[N_META:], grads


SHARD_AXIS = {
    "meta_tokens": 1, "norm_mix": None, "norm_ffn": None, "norm_final": None,
    "ev_w_in": 2, "ev_conv_a": 2, "ev_ln_a_g": None, "ev_ln_a_b": None, "ev_conv_b": 2, "ev_w_out": 1,
    "od_w_in": 2, "od_sinks": None, "od_mu": 1, "od_w0": 1, "od_w2": 2, "od_a0": 1, "od_a2": 2, "od_g2": 2,
    "od_k_k": 1, "od_k_a": 1, "od_r_k": None, "od_lnx_g": 1, "od_lnx_b": 1, "od_w_out": 1,
    "ff_w_up": 2, "ff_conv": 2, "ff_conv_b": None, "ff_w_down": 1,
}
WEIGHTS = list(SHARD_AXIS)
BIG = ("ev_w_in", "ev_w_out", "od_w_in", "od_w_out", "ff_w_up", "ff_w_down")
SHARDED = [n for n in WEIGHTS if SHARD_AXIS[n] is not None]
SMALL = [n for n in SHARDED if n not in BIG]
REPLICATED = [n for n in WEIGHTS if SHARD_AXIS[n] is None]


def _join(g, axis):
    return jnp.concatenate([g[k] for k in range(N_CHIPS)], axis=axis)


def _split(full, axis):
    return jnp.stack(jnp.split(full, N_CHIPS, axis=axis))


def _full_weights(gathered, repl):
    w = {}
    sq = lambda a: a.reshape(a.shape[1:]) if a.shape[0] == 1 else a
    for n in REPLICATED:
        w[n] = repl[n]
    w["norm_final"] = repl["norm_final"].reshape(1, D_MODEL)
    for n in ("ev_ln_a_g", "ev_ln_a_b"):
        w[n] = repl[n].reshape(1, D_A)
    w["od_r_k"] = repl["od_r_k"][0]
    w["meta_tokens"] = _join(gathered["meta_tokens"], 1)
    for n in ("ev_conv_a", "ev_conv_b", "od_w2", "od_a2", "od_g2"):
        w[n] = sq(_join(gathered[n], 2))
    for n in ("od_mu", "od_w0", "od_a0", "od_k_k", "od_k_a", "od_lnx_g", "od_lnx_b"):
        w[n] = _join(gathered[n], 1)
    w["ff_conv"] = _join(gathered["ff_conv"], 2)
    return w


def _shard_grads(grads):
    out = {}
    for n in REPLICATED:
        out[n] = grads[n]
    out["norm_final"] = grads["norm_final"].reshape(D_MODEL)
    out["od_r_k"] = grads["od_r_k"][None]
    out["meta_tokens"] = _split(grads["meta_tokens"], 1)
    for n in ("ev_conv_a", "ev_conv_b", "od_w2", "od_a2", "od_g2"):
        out[n] = _split(grads[n][None], 2)
    for n in ("od_mu", "od_w0", "od_a0", "od_k_k", "od_k_a", "od_lnx_g", "od_lnx_b"):
        out[n] = _split(grads[n], 1)
    out["ff_conv"] = _split(grads["ff_conv"], 2)
    return out


def kernel(x, meta_tokens, norm_mix, norm_ffn, norm_final, ev_w_in, ev_conv_a, ev_ln_a_g, ev_ln_a_b, ev_conv_b, ev_w_out, od_w_in, od_sinks, od_mu, od_w0, od_w2, od_a0, od_a2, od_g2, od_k_k, od_k_a, od_r_k, od_lnx_g, od_lnx_b, od_w_out, ff_w_up, ff_conv, ff_conv_b, ff_w_down, loss_target, m_meta_tokens, m_norm_mix, m_norm_ffn, m_norm_final, m_ev_w_in, m_ev_conv_a, m_ev_ln_a_g, m_ev_ln_a_b, m_ev_conv_b, m_ev_w_out, m_od_w_in, m_od_sinks, m_od_mu, m_od_w0, m_od_w2, m_od_a0, m_od_a2, m_od_g2, m_od_k_k, m_od_k_a, m_od_r_k, m_od_lnx_g, m_od_lnx_b, m_od_w_out, m_ff_w_up, m_ff_conv, m_ff_conv_b, m_ff_w_down, v_meta_tokens, v_norm_mix, v_norm_ffn, v_norm_final, v_ev_w_in, v_ev_conv_a, v_ev_ln_a_g, v_ev_ln_a_b, v_ev_conv_b, v_ev_w_out, v_od_w_in, v_od_sinks, v_od_mu, v_od_w0, v_od_w2, v_od_a0, v_od_a2, v_od_g2, v_od_k_k, v_od_k_a, v_od_r_k, v_od_lnx_g, v_od_lnx_b, v_od_w_out, v_ff_w_up, v_ff_conv, v_ff_conv_b, v_ff_w_down):
    wts = dict(meta_tokens=meta_tokens, norm_mix=norm_mix, norm_ffn=norm_ffn, norm_final=norm_final, ev_w_in=ev_w_in, ev_conv_a=ev_conv_a, ev_ln_a_g=ev_ln_a_g, ev_ln_a_b=ev_ln_a_b, ev_conv_b=ev_conv_b, ev_w_out=ev_w_out, od_w_in=od_w_in, od_sinks=od_sinks, od_mu=od_mu, od_w0=od_w0, od_w2=od_w2, od_a0=od_a0, od_a2=od_a2, od_g2=od_g2, od_k_k=od_k_k, od_k_a=od_k_a, od_r_k=od_r_k, od_lnx_g=od_lnx_g, od_lnx_b=od_lnx_b, od_w_out=od_w_out, ff_w_up=ff_w_up, ff_conv=ff_conv, ff_conv_b=ff_conv_b, ff_w_down=ff_w_down)
    mom = dict(meta_tokens=m_meta_tokens, norm_mix=m_norm_mix, norm_ffn=m_norm_ffn, norm_final=m_norm_final, ev_w_in=m_ev_w_in, ev_conv_a=m_ev_conv_a, ev_ln_a_g=m_ev_ln_a_g, ev_ln_a_b=m_ev_ln_a_b, ev_conv_b=m_ev_conv_b, ev_w_out=m_ev_w_out, od_w_in=m_od_w_in, od_sinks=m_od_sinks, od_mu=m_od_mu, od_w0=m_od_w0, od_w2=m_od_w2, od_a0=m_od_a0, od_a2=m_od_a2, od_g2=m_od_g2, od_k_k=m_od_k_k, od_k_a=m_od_k_a, od_r_k=m_od_r_k, od_lnx_g=m_od_lnx_g, od_lnx_b=m_od_lnx_b, od_w_out=m_od_w_out, ff_w_up=m_ff_w_up, ff_conv=m_ff_conv, ff_conv_b=m_ff_conv_b, ff_w_down=m_ff_w_down)
    var = dict(meta_tokens=v_meta_tokens, norm_mix=v_norm_mix, norm_ffn=v_norm_ffn, norm_final=v_norm_final, ev_w_in=v_ev_w_in, ev_conv_a=v_ev_conv_a, ev_ln_a_g=v_ev_ln_a_g, ev_ln_a_b=v_ev_ln_a_b, ev_conv_b=v_ev_conv_b, ev_w_out=v_ev_w_out, od_w_in=v_od_w_in, od_sinks=v_od_sinks, od_mu=v_od_mu, od_w0=v_od_w0, od_w2=v_od_w2, od_a0=v_od_a0, od_a2=v_od_a2, od_g2=v_od_g2, od_k_k=v_od_k_k, od_k_a=v_od_k_a, od_r_k=v_od_r_k, od_lnx_g=v_od_lnx_g, od_lnx_b=v_od_lnx_b, od_w_out=v_od_w_out, ff_w_up=v_ff_w_up, ff_conv=v_ff_conv, ff_conv_b=v_ff_conv_b, ff_w_down=v_ff_w_down)

    me_idx = (2 * lax.axis_index("x") + lax.axis_index("y")).astype(jnp.int32).reshape(1)
    c_idx = lax.axis_index("c").astype(jnp.int32).reshape(1)
    small_mine = _pack([wts[n] for n in SMALL], F32, 2 * 8)
    sources = {"ev_w_in": (ev_w_in, 0), "small": (small_mine[None], 0), "ev_w_out": (ev_w_out, 0),
               "ff_w_up0": (ff_w_up, 0), "ff_w_down0": (ff_w_down, 0), "od_w_in": (od_w_in, 0), "od_w_out": (od_w_out, 0),
               "ff_w_up1": (ff_w_up, 1), "ff_w_down1": (ff_w_down, 1)}
    bufs = {n: _place_own_block("place_" + n, a, l, me_idx, F32 if n == "small" else MXU_DTYPE)
            for n, (a, l) in sources.items()}

    def as_used(n, g):
        if n in ("ev_w_out", "od_w_out", "ff_w_down0", "ff_w_down1"):
            return g.reshape(1, -1, g.shape[-1])
        return g.reshape(N_CHIPS, 1, -1, g.shape[-1])

    first = dict(zip(("ev_w_in", "small"), _gather_weights("gather_first", [bufs["ev_w_in"], bufs["small"]])))
    gathered = dict(zip(SMALL, _unpack(first["small"].reshape(N_CHIPS, -1), [wts[n].shape for n in SMALL])))
    w_full = _full_weights(gathered, wts)
    w_full["ev_w_in"] = as_used("ev_w_in", first["ev_w_in"])
    groups = {"ev_out": ["ev_w_out"], "f0": ["ff_w_up0", "ff_w_down0"], "od": ["od_w_in", "od_w_out"],
              "f1": ["ff_w_up1", "ff_w_down1"]}
    started_gathers, token = _gather_start("gather_start", [[bufs[n] for n in g] for g in groups.values()])
    started_gathers = dict(zip(groups, started_gathers))
    w_full["norm_mix"] = w_full["norm_mix"] + token[0, 0]

    def fetch(tag, after):
        send_sems, recv_sems, group_bufs = started_gathers[tag]
        landed = _gather_wait("gather_wait_" + tag, send_sems, recv_sems, group_bufs, after)
        whole = _gather_weights("gather_siblings_" + tag, landed, from_chips=False)
        return {n: as_used(n, g) for n, g in zip(groups[tag], whole)}

    cm_idx = jnp.concatenate([c_idx, me_idx])
    started = []

    def start_reduction(tag, units):
        names = list(units)
        from_sibling = _halves_to_sibling(f"grads_to_sibling_{tag}", [units[n] for n in names])
        pairs = [_pair_add_placed(f"grads_pair_add_{n}", units[n], r, cm_idx, GRAD_WIRE_DTYPE) for n, r in zip(names, from_sibling)]
        send_sems, recv_sems, sums, zones, token = _scatter_start(
            f"grads_to_chips_start_{tag}", [p[0] for p in pairs], [p[1] for p in pairs])
        started.append((tag, names, send_sems, recv_sems, sums, zones))
        return token[0, 0]

    loss_local, grad_x, grads = _local_step(x[0], loss_target[0], w_full, start_reduction, fetch)
    loss = lax.psum(loss_local, ("x", "y", "c"))

    sg = _shard_grads(grads)
    small_rows = [jnp.concatenate([sg[n][k].reshape(-1) for n in SMALL] + [sg[n].reshape(-1) for n in REPLICATED])
                  for k in range(N_CHIPS)]
    n_el = small_rows[0].shape[0]
    n_rows = -(-n_el // (16 * PACK_W)) * 16
    small_unit = jnp.stack([jnp.pad(r, (0, n_rows * PACK_W - n_el)).reshape(n_rows, PACK_W) for r in small_rows])
    last = {"ev_w_out": grads["ev_w_out"].reshape(N_CHIPS, D_MODEL // N_CHIPS, D_MODEL), "ev_w_in": grads["ev_w_in"],
            "small": small_unit}
    from_sibling = _halves_to_sibling("grads_to_sibling_ev", list(last.values()))
    chip_sums = [_pair_add(f"grads_pair_add_{n}", u, r, c_idx, F32 if n == "small" else GRAD_WIRE_DTYPE)
                 for (n, u), r in zip(last.items(), from_sibling)]
    from_chips = dict(zip(last, _scatter_to_chips("grads_to_chips_ev", chip_sums)))
    for tag, names, send_sems, recv_sems, sums, zones in started:
        from_chips.update(zip(names, _scatter_wait(f"grads_to_chips_wait_{tag}", send_sems, recv_sems, sums, zones,
                                                   from_chips["small"])))
    dests = {"ev_w_in": ("ev_w_in", 0), "od_w_in": ("od_w_in", 0), "ev_w_out": ("ev_w_out", 0), "od_w_out": ("od_w_out", 0),
             "ff_w_up0": ("ff_w_up", 0), "ff_w_up1": ("ff_w_up", 1), "ff_w_down0": ("ff_w_down", 0),
             "ff_w_down1": ("ff_w_down", 1), "small": ("small", 0)}
    results = ["ev_w_in", "od_w_in", "ev_w_out", "od_w_out", "ff_w_up", "ff_w_down", "small"]
    reduced = {}
    for n, (r, l) in dests.items():
        reduced[r] = _sum_chips(f"grads_chip_sum_{n}", from_chips[n], c_idx, l, 2 if r.startswith("ff_w") else 1,
                                into=reduced.get(r))
    joined = _join_halves("grads_join", [reduced[r] for r in results])

    outs = {"grad": {}, "delta": {}, "new_m": {}, "new_v": {}}
    for n, g in zip(results[:-1], joined):
        shape = wts[n].shape
        flat = lambda a: a.reshape(-1, shape[-1])
        new = _adamw("adamw_" + n, flat(wts[n]), flat(g), flat(mom[n]), flat(var[n]))
        for tag, arr in zip(("grad", "delta", "new_m", "new_v"), (g,) + tuple(new)):
            outs[tag][n] = arr.reshape(shape)
    order = SMALL + REPLICATED
    packed = lambda d: jnp.pad(jnp.concatenate([d[n].reshape(-1) for n in order]),
                               (0, n_rows * PACK_W - n_el)).reshape(n_rows, PACK_W)
    g_small = joined[-1].reshape(n_rows, PACK_W)
    new = _adamw("adamw_small", packed(wts), g_small, packed(mom), packed(var))
    for tag, arr in zip(("grad", "delta", "new_m", "new_v"), (g_small,) + tuple(new)):
        outs[tag].update(zip(order, _unpack(arr.reshape(-1), [wts[n].shape for n in order])))
    return (loss, grad_x[None], *[outs["grad"][n] for n in WEIGHTS], *[outs["delta"][n] for n in WEIGHTS],
            *[outs["new_m"][n] for n in WEIGHTS], *[outs["new_v"][n] for n in WEIGHTS])
```

```python
import functools

import jax
import jax.numpy as jnp
from jax import lax
from jax.experimental import pallas as pl
from jax.experimental.pallas import tpu as pltpu

F32 = jnp.float32
BF16 = jnp.bfloat16
HI = lax.Precision.HIGHEST
MXU_DTYPE = BF16
GRAD_WIRE_DTYPE = BF16

D_MODEL = 1024
N_META = 16
RMS_EPS = 1e-6
LN_EPS = 1e-5
D_A = 512
CONV_A_WIDTH = 31
CONV_B_WIDTH = 3
HEAD_DIM = 64
N_Q_HEADS = 8
N_KV_HEADS = 2
GQA_GROUP = 4
D_ATT = 512
D_KV = 128
BLOCK = 128
ROPE_THETA = 10000.0
D_R = 512
N_R_HEADS = 8
LORA_W = 64
LORA_A = 64
LORA_G = 128
RWKV_GN_EPS = 64e-5
RWKV_COLS = 3 * D_R + LORA_W + LORA_A + LORA_G
D_FF = 2816
NEG_INF = -1e30
ADAM_LR = 0.001
ADAM_B1 = 0.9
ADAM_B2 = 0.999
ADAM_EPS = 1e-08
ADAM_WD = 0.01
ADAM_STEP = 10

N_CHIPS = 4
LANES = 128
CONV_PAD = 32
VMEM_LIMIT_V7X = 56 * 1024 * 1024
MESH = pl.DeviceIdType.MESH


def _cparams(sem=None):
    return pltpu.CompilerParams(dimension_semantics=sem, vmem_limit_bytes=VMEM_LIMIT_V7X)


def _row_tile(t, cap):
    for d in range(min(t, cap), 0, -1):
        if t % d == 0 and d % 16 == 0:
            return d
    return t


def _chunk_len(t):
    for d in (64, 48, 32, 16, 8):
        if t % d == 0:
            return d
    raise ValueError(t)


def _call(fn, name, grid, ins, outs, acc_axis=None, sem=None):
    n_in, n_out = len(ins), len(outs)
    dtype = lambda o: o[4] if len(o) > 4 else F32

    def body(*refs):
        vals = fn(*[r[...] for r in refs[:n_in]])
        if not isinstance(vals, (tuple, list)):
            vals = (vals,)
        for r, v, o in zip(refs[n_in:n_in + n_out], vals, outs):
            if o[3]:
                first = pl.program_id(acc_axis) == 0

                @pl.when(first)
                def _(r=r, v=v):
                    r[...] = v

                @pl.when(jnp.logical_not(first))
                def _(r=r, v=v):
                    r[...] += v
            else:
                r[...] = v.astype(dtype(o))

    res = pl.pallas_call(
        body, name=name, grid=grid,
        in_specs=[pl.BlockSpec(b, m) for _, b, m in ins],
        out_specs=[pl.BlockSpec(o[1], o[2]) for o in outs],
        out_shape=[jax.ShapeDtypeStruct(o[0], dtype(o)) for o in outs],
        compiler_params=_cparams(sem),
    )(*[a for a, _, _ in ins])
    return res if n_out > 1 else res[0]


def _matmul(name, a, b, *, dims, grid, a_spec, b_spec, o_shape, o_spec, acc_shape, nk, k_axis,
            add=None, add_spec=None):
    def product(a_ref, b_ref):
        return lax.dot_general(a_ref[...].astype(MXU_DTYPE), b_ref[...].astype(MXU_DTYPE), dims, preferred_element_type=F32)

    def body_single(*refs):
        a_ref, b_ref, o_ref = refs[0], refs[1], refs[-1]
        o_ref[...] = product(a_ref, b_ref) if add is None else product(a_ref, b_ref) + refs[2][...]

    def body_steps(*refs):
        a_ref, b_ref, o_ref, acc = refs[0], refs[1], refs[-2], refs[-1]
        k = pl.program_id(k_axis)

        @pl.when(k == 0)
        def _():
            if add is None:
                acc[...] = jnp.zeros(acc.shape, F32)
            else:
                acc[...] = refs[2][...]

        acc[...] += product(a_ref, b_ref)

        @pl.when(k == nk - 1)
        def _():
            o_ref[...] = acc[...]

    args = [a, b] + ([] if add is None else [add])
    specs = [a_spec, b_spec] + ([] if add is None else [add_spec])
    return pl.pallas_call(
        body_single if nk == 1 else body_steps, name=name, grid=grid, in_specs=specs, out_specs=o_spec,
        out_shape=jax.ShapeDtypeStruct(o_shape, F32),
        scratch_shapes=[] if nk == 1 else [pltpu.VMEM(acc_shape, F32)],
        compiler_params=_cparams(None),
    )(*args)


MATMUL_BLOCKS_BYTES = 46 * 1024 * 1024


def _whole_if_fits(t, tile, need_bytes):
    return t if need_bytes <= MATMUL_BLOCKS_BYTES else tile


_NN = (((1,), (0,)), ((), ()))
_NT = (((1,), (1,)), ((), ()))
_TN = (((0,), (0,)), ((), ()))


def _mm_cs(name, x, wg, l, tm):
    t, k = x.shape
    s, _, _, n = wg.shape
    tm = _whole_if_fits(t, tm, 2 * (t * k * x.dtype.itemsize + k * n * wg.dtype.itemsize + t * n * 4))
    return _matmul(name, x, wg, dims=_NN, grid=(s, t // tm, 1),
                   a_spec=pl.BlockSpec((tm, k), lambda j, i, kk: (i, 0)),
                   b_spec=pl.BlockSpec((None, None, k, n), lambda j, i, kk: (j, l, 0, 0)),
                   o_shape=(t, s * n), o_spec=pl.BlockSpec((tm, n), lambda j, i, kk: (i, j)),
                   acc_shape=(tm, n), nk=1, k_axis=2)


def _mm_full(name, x, w, l, tm, tk, add=None):
    t, k = x.shape
    n = w.shape[2]
    nk = k // tk
    tm = _whole_if_fits(t, tm, 2 * (t * tk * x.dtype.itemsize + tk * n * w.dtype.itemsize + t * n * 4 * (1 if add is None else 2))
                        + (t * n * 4 if nk > 1 else 0))
    return _matmul(name, x, w, dims=_NN, grid=(t // tm, 1, nk),
                   a_spec=pl.BlockSpec((tm, tk), lambda i, j, kk: (i, kk)),
                   b_spec=pl.BlockSpec((None, tk, n), lambda i, j, kk: (l, kk, 0)),
                   o_shape=(t, n), o_spec=pl.BlockSpec((tm, n), lambda i, j, kk: (i, 0)),
                   acc_shape=(tm, n), nk=nk, k_axis=2,
                   add=add, add_spec=pl.BlockSpec((tm, n), lambda i, j, kk: (i, 0)))


def _mm_nt_cs(name, dy, wg, l, tm, add=None):
    t = dy.shape[0]
    s, _, k, n = wg.shape
    tm = _whole_if_fits(t, tm, 2 * (t * n * dy.dtype.itemsize + k * n * wg.dtype.itemsize + t * k * 4 * (1 if add is None else 2))
                        + t * k * 4)
    return _matmul(name, dy, wg, dims=_NT, grid=(t // tm, 1, s),
                   a_spec=pl.BlockSpec((tm, n), lambda i, j, kk: (i, kk)),
                   b_spec=pl.BlockSpec((None, None, k, n), lambda i, j, kk: (kk, l, 0, 0)),
                   o_shape=(t, k), o_spec=pl.BlockSpec((tm, k), lambda i, j, kk: (i, 0)),
                   acc_shape=(tm, k), nk=s, k_axis=2,
                   add=add, add_spec=pl.BlockSpec((tm, k), lambda i, j, kk: (i, 0)))


def _mm_nt_full(name, dy, w, l, tm, tko):
    t, n = dy.shape
    k = w.shape[1]
    tm = _whole_if_fits(t, tm, 2 * (t * n * dy.dtype.itemsize + tko * n * w.dtype.itemsize + t * tko * 4))
    return _matmul(name, dy, w, dims=_NT, grid=(t // tm, k // tko, 1),
                   a_spec=pl.BlockSpec((tm, n), lambda i, j, kk: (i, 0)),
                   b_spec=pl.BlockSpec((None, tko, n), lambda i, j, kk: (l, j, 0)),
                   o_shape=(t, k), o_spec=pl.BlockSpec((tm, tko), lambda i, j, kk: (i, j)),
                   acc_shape=(tm, tko), nk=1, k_axis=2)


def _mm_tn_cs(name, x, dy, s, tk):
    t, k = x.shape
    n = dy.shape[1] // s
    tk = _whole_if_fits(t, tk, 2 * (t * k * x.dtype.itemsize + t * n * dy.dtype.itemsize + k * n * 4))
    nk = t // tk
    return _matmul(name, x, dy, dims=_TN, grid=(s, 1, nk),
                   a_spec=pl.BlockSpec((tk, k), lambda j, i, kk: (kk, 0)),
                   b_spec=pl.BlockSpec((tk, n), lambda j, i, kk: (kk, j)),
                   o_shape=(s, k, n), o_spec=pl.BlockSpec((None, k, n), lambda j, i, kk: (j, 0, 0)),
                   acc_shape=(k, n), nk=nk, k_axis=2)


def _mm_tn_full(name, y, dh, tk, tko):
    t, k = y.shape
    n = dh.shape[1]
    tk = _whole_if_fits(t, tk, 2 * (t * tko * y.dtype.itemsize + t * n * dh.dtype.itemsize + tko * n * 4))
    nk = t // tk
    return _matmul(name, y, dh, dims=_TN, grid=(k // tko, 1, nk),
                   a_spec=pl.BlockSpec((tk, tko), lambda j, i, kk: (kk, j)),
                   b_spec=pl.BlockSpec((tk, n), lambda j, i, kk: (kk, 0)),
                   o_shape=(k, n), o_spec=pl.BlockSpec((tko, n), lambda j, i, kk: (j, 0)),
                   acc_shape=(tko, n), nk=nk, k_axis=2)


def _sigmoid(x):
    return 1.0 / (1.0 + jnp.exp(-x))


def _rms_fwd(name, h, g, tr):
    t, d = h.shape

    def fn(hv, gv):
        r = lax.rsqrt(jnp.mean(hv * hv, axis=-1, keepdims=True) + RMS_EPS)
        return hv * r * gv

    return _call(fn, name, (t // tr,), [(h, (tr, d), lambda i: (i, 0)), (g, (1, d), lambda i: (0, 0))],
                 [((t, d), (tr, d), lambda i: (i, 0), False, MXU_DTYPE)])


def _rms_bwd(name, h, g, dhn, dh, tr):
    t, d = h.shape

    def fn(hv, gv, dy, dh_in):
        r = lax.rsqrt(jnp.mean(hv * hv, axis=-1, keepdims=True) + RMS_EPS)
        xh = hv * r
        dg = jnp.sum(dy * xh, axis=0, keepdims=True)
        dxh = dy * gv
        dx = r * (dxh - xh * jnp.mean(dxh * xh, axis=-1, keepdims=True))
        return dh_in + dx, dg

    row = lambda i: (i, 0)
    return _call(fn, name, (t // tr,),
                 [(h, (tr, d), row), (g, (1, d), lambda i: (0, 0)), (dhn, (tr, d), row), (dh, (tr, d), row)],
                 [((t, d), (tr, d), row, False), ((1, d), (1, d), lambda i: (0, 0), True)], acc_axis=0)


def _final_loss(name, h, g, tgt, tr):
    t, d = h.shape

    def fn(hv, gv, tv):
        r = lax.rsqrt(jnp.mean(hv * hv, axis=-1, keepdims=True) + RMS_EPS)
        xh = hv * r
        row = pl.program_id(0) * tr + lax.broadcasted_iota(jnp.int32, (tr, 1), 0)
        e = jnp.where(row >= N_META, xh * gv - tv, 0.0)
        loss = jnp.broadcast_to(0.5 * jnp.sum(jnp.sum(e * e, axis=-1, keepdims=True), axis=0, keepdims=True) / d,
                                (8, LANES))
        dout = e / d
        dg = jnp.sum(dout * xh, axis=0, keepdims=True)
        dxh = dout * gv
        dx = r * (dxh - xh * jnp.mean(dxh * xh, axis=-1, keepdims=True))
        return loss, dx, dg

    row = lambda i: (i, 0)
    fix = lambda i: (0, 0)
    return _call(fn, name, (t // tr,), [(h, (tr, d), row), (g, (1, d), fix), (tgt, (tr, d), row)],
                 [((8, LANES), (8, LANES), fix, True), ((t, d), (tr, d), row, False), ((1, d), (1, d), fix, True)],
                 acc_axis=0)


def _silu_ln(uc, g, b):
    mu = jnp.mean(uc, axis=-1, keepdims=True)
    xc = uc - mu
    rs = lax.rsqrt(jnp.mean(xc * xc, axis=-1, keepdims=True) + LN_EPS)
    ln = xc * rs * g + b
    return ln * _sigmoid(ln)


def _even_ln_fwd(name, uc, g, b, tr):
    t, d = uc.shape
    row, fix = (lambda i: (i, 0)), (lambda i: (0, 0))
    return _call(_silu_ln, name, (t // tr,), [(uc, (tr, d), row), (g, (1, d), fix), (b, (1, d), fix)],
                 [((t, d), (tr, d), row, False, MXU_DTYPE)])


def _even_ln_bwd(name, uc, g, b, dy, dy_col, tr):
    t, d = uc.shape

    def fn(ucv, gv, bv, dyv):
        mu = jnp.mean(ucv, axis=-1, keepdims=True)
        xc = ucv - mu
        rs = lax.rsqrt(jnp.mean(xc * xc, axis=-1, keepdims=True) + LN_EPS)
        xh = xc * rs
        ln = xh * gv + bv
        s = _sigmoid(ln)
        dln = dyv * (s * (1.0 + ln * (1.0 - s)))
        dg = jnp.sum(dln * xh, axis=0, keepdims=True)
        db = jnp.sum(dln, axis=0, keepdims=True)
        dxh = dln * gv
        duc = rs * (dxh - jnp.mean(dxh, axis=-1, keepdims=True) - xh * jnp.mean(dxh * xh, axis=-1, keepdims=True))
        return duc, dg, db

    row, fix = (lambda i: (i, 0)), (lambda i: (0, 0))
    return _call(fn, name, (t // tr,),
                 [(uc, (tr, d), row), (g, (1, d), fix), (b, (1, d), fix), (dy, (tr, d), lambda i: (i, dy_col))],
                 [((t, d), (tr, d), row, False), ((1, d), (1, d), fix, True), ((1, d), (1, d), fix, True)], acc_axis=0)


def _windows(t):
    rc = _chunk_len(t)
    return [(r0, rc) for r0 in range(0, t, rc)]


def _taps(w_ref, width):
    return [w_ref[pl.ds(j, 1), :] for j in range(width)]


def _conv_at(xp, taps, r0, rc):
    width = len(taps)
    acc = None
    for j in range(width):
        term = xp[pl.ds(CONV_PAD - (width - 1) + j + r0, rc), :] * taps[j]
        acc = term if acc is None else acc + term
    return acc


def _conv_bwd_in_at(dyp, taps, r0, rc):
    width = len(taps)
    acc = None
    for j in range(width):
        term = dyp[pl.ds(width - 1 - j + r0, rc), :] * taps[j]
        acc = term if acc is None else acc + term
    return acc


def _fold(x):
    acc = x[0:8]
    for i in range(1, x.shape[0] // 8):
        acc = acc + x[8 * i:8 * (i + 1)]
    return acc


def _add_to(accs, vals):
    return vals if accs is None else [a + v for a, v in zip(accs, vals)]


def _conv_bwd_w_at(dy, xp, width, r0, rc):
    return [_fold(dy * xp[pl.ds(CONV_PAD - (width - 1) + j + r0, rc), :]) for j in range(width)]


def _store_taps(dw_ref, accs):
    for j, a in enumerate(accs):
        dw_ref[pl.ds(j, 1), :] = jnp.sum(a, axis=0, keepdims=True)


def _zero_front(xp):
    xp[pl.ds(0, CONV_PAD), :] = jnp.zeros((CONV_PAD, LANES), F32)


def _zero_back(dyp, t):
    dyp[pl.ds(t, CONV_PAD), :] = jnp.zeros((CONV_PAD, LANES), F32)


def _col_call(body, name, ncol, ins, outs, t, n_scratch):
    def spec(rows, off):
        return pl.BlockSpec((rows, LANES), lambda j, off=off: (0, j + off))

    res = pl.pallas_call(
        body, name=name, grid=(ncol,),
        in_specs=[spec(r, off) for _, r, off in ins],
        out_specs=[spec(o[0], 0) for o in outs],
        out_shape=[jax.ShapeDtypeStruct(o[:2], o[2] if len(o) > 2 else F32) for o in outs],
        scratch_shapes=[pltpu.VMEM((t + CONV_PAD, LANES), F32) for _ in range(n_scratch)],
        compiler_params=_cparams(None),
    )(*[a for a, _, _ in ins])
    return res


def _even_col_fwd(name, p, conv_a, conv_b):
    t = p.shape[0]
    nc = D_A // LANES
    wins = _windows(t)

    def body(av, ag, gb, gc, xi, ca, cb, uc_ref, yb_ref, xp):
        _zero_front(xp)
        for r0, rc in wins:
            rows = pl.ds(r0, rc)
            xp[pl.ds(CONV_PAD + r0, rc), :] = av[rows, :] * _sigmoid(ag[rows, :])
        taps = _taps(ca, CONV_A_WIDTH)
        for r0, rc in wins:
            uc_ref[pl.ds(r0, rc), :] = _conv_at(xp, taps, r0, rc)
        for r0, rc in wins:
            rows = pl.ds(r0, rc)
            xp[pl.ds(CONV_PAD + r0, rc), :] = gc[rows, :] * xi[rows, :]
        taps = _taps(cb, CONV_B_WIDTH)
        for r0, rc in wins:
            rows = pl.ds(r0, rc)
            yb_ref[rows, :] = (gb[rows, :] * _conv_at(xp, taps, r0, rc)).astype(yb_ref.dtype)

    ins = [(p, t, k * nc) for k in range(5)] + [(conv_a, CONV_A_WIDTH, 0), (conv_b, CONV_B_WIDTH, 0)]
    return _col_call(body, name, nc, ins, [(t, D_A), (t, D_A, MXU_DTYPE)], t, 1)


def _even_col_bwd(name, p, duc, dy, conv_a, conv_b):
    t = p.shape[0]
    nc = D_A // LANES
    wins = _windows(t)

    def body(av, ag, gb, gc, xi, duc_ref, dyb_ref, ca, cb, dav, dag, dgb, dgc, dxi, dca, dcb, xp, dyp):
        _zero_front(xp)
        _zero_back(dyp, t)
        for r0, rc in wins:
            rows = pl.ds(r0, rc)
            xp[pl.ds(CONV_PAD + r0, rc), :] = av[rows, :] * _sigmoid(ag[rows, :])
            dyp[rows, :] = duc_ref[rows, :]
        taps = _taps(ca, CONV_A_WIDTH)
        accs = None
        for r0, rc in wins:
            rows = pl.ds(r0, rc)
            accs = _add_to(accs, _conv_bwd_w_at(duc_ref[rows, :], xp, CONV_A_WIDTH, r0, rc))
            du = _conv_bwd_in_at(dyp, taps, r0, rc)
            sig = _sigmoid(ag[rows, :])
            dav[rows, :] = (du * sig).astype(dav.dtype)
            dag[rows, :] = (du * av[rows, :] * sig * (1.0 - sig)).astype(dag.dtype)
        _store_taps(dca, accs)
        for r0, rc in wins:
            rows = pl.ds(r0, rc)
            xp[pl.ds(CONV_PAD + r0, rc), :] = gc[rows, :] * xi[rows, :]
        taps = _taps(cb, CONV_B_WIDTH)
        accs = None
        for r0, rc in wins:
            rows = pl.ds(r0, rc)
            dgb[rows, :] = (dyb_ref[rows, :] * _conv_at(xp, taps, r0, rc)).astype(dgb.dtype)
            dzc = dyb_ref[rows, :] * gb[rows, :]
            dyp[rows, :] = dzc
            accs = _add_to(accs, _conv_bwd_w_at(dzc, xp, CONV_B_WIDTH, r0, rc))
        _store_taps(dcb, accs)
        for r0, rc in wins:
            rows = pl.ds(r0, rc)
            dz = _conv_bwd_in_at(dyp, taps, r0, rc)
            dgc[rows, :] = (dz * xi[rows, :]).astype(dgc.dtype)
            dxi[rows, :] = (dz * gc[rows, :]).astype(dxi.dtype)

    ins = ([(p, t, k * nc) for k in range(5)] + [(duc, t, 0), (dy, t, nc)]
           + [(conv_a, CONV_A_WIDTH, 0), (conv_b, CONV_B_WIDTH, 0)])
    outs = [(t, D_A, MXU_DTYPE)] * 5 + [(CONV_A_WIDTH, D_A), (CONV_B_WIDTH, D_A)]
    return _col_call(body, name, nc, ins, outs, t, 2)


def _ffn_col_fwd(name, u, conv, bias):
    t = u.shape[0]
    nc = D_FF // LANES
    wins = _windows(t)

    def body(g_ref, v_ref, cw, b_ref, a_ref, xp):
        _zero_front(xp)
        xp[pl.ds(CONV_PAD, t), :] = g_ref[...]
        taps = _taps(cw, CONV_B_WIDTH)
        b = b_ref[...]
        for r0, rc in wins:
            rows = pl.ds(r0, rc)
            gc = _conv_at(xp, taps, r0, rc) + b
            a_ref[rows, :] = (gc * _sigmoid(gc) * v_ref[rows, :]).astype(a_ref.dtype)

    ins = [(u, t, 0), (u, t, nc), (conv, CONV_B_WIDTH, 0), (bias, 1, 0)]
    return _col_call(body, name, nc, ins, [(t, D_FF, MXU_DTYPE)], t, 1)[0]


def _ffn_col_bwd(name, u, da, conv, bias):
    t = u.shape[0]
    nc = D_FF // LANES
    wins = _windows(t)

    def body(g_ref, v_ref, da_ref, cw, b_ref, du_ref, dcw, db_ref, xp, dyp, dval):
        @pl.when(pl.program_id(1) == 0)
        def _():
            _zero_front(xp)
            _zero_back(dyp, t)
            xp[pl.ds(CONV_PAD, t), :] = g_ref[...]
            taps = _taps(cw, CONV_B_WIDTH)
            b = b_ref[...]
            accs, bias_acc = None, None
            for r0, rc in wins:
                rows = pl.ds(r0, rc)
                gc = _conv_at(xp, taps, r0, rc) + b
                s = _sigmoid(gc)
                d = da_ref[rows, :]
                dval[rows, :] = d * gc * s
                dgc = d * v_ref[rows, :] * (s * (1.0 + gc * (1.0 - s)))
                dyp[rows, :] = dgc
                bias_acc = _add_to(bias_acc, [_fold(dgc)])
                accs = _add_to(accs, _conv_bwd_w_at(dgc, xp, CONV_B_WIDTH, r0, rc))
            db_ref[...] = jnp.sum(bias_acc[0], axis=0, keepdims=True)
            _store_taps(dcw, accs)
            for r0, rc in wins:
                du_ref[pl.ds(r0, rc), :] = _conv_bwd_in_at(dyp, taps, r0, rc).astype(du_ref.dtype)

        @pl.when(pl.program_id(1) == 1)
        def _():
            du_ref[...] = dval[...].astype(du_ref.dtype)

    col = lambda rows, off: pl.BlockSpec((rows, LANES), lambda j, p: (0, j + off))
    return pl.pallas_call(
        body, name=name, grid=(nc, 2),
        in_specs=[col(t, 0), col(t, nc), col(t, 0), col(CONV_B_WIDTH, 0), col(1, 0)],
        out_specs=[pl.BlockSpec((t, LANES), lambda j, p: (0, j + nc * p)), col(CONV_B_WIDTH, 0), col(1, 0)],
        out_shape=[jax.ShapeDtypeStruct((t, 2 * D_FF), MXU_DTYPE), jax.ShapeDtypeStruct((CONV_B_WIDTH, D_FF), F32),
                   jax.ShapeDtypeStruct((1, D_FF), F32)],
        scratch_shapes=[pltpu.VMEM((t + CONV_PAD, LANES), F32) for _ in range(2)] + [pltpu.VMEM((t, LANES), F32)],
        compiler_params=_cparams(None),
    )(u, u, da, conv, bias)


def _shift_fwd(name, p, col0, mu):
    t = p.shape[0]
    wins = _windows(t)

    def body(x_ref, mu_ref, o_ref, xp):
        _zero_front(xp)
        xp[pl.ds(CONV_PAD, t), :] = x_ref[...]
        mu_v = mu_ref[...]
        for r0, rc in wins:
            rows = pl.ds(r0, rc)
            x = x_ref[rows, :]
            o_ref[rows, :] = x + (xp[pl.ds(CONV_PAD - 1 + r0, rc), :] - x) * mu_v

    return _col_call(body, name, RWKV_COLS // LANES, [(p, t, col0), (mu, 1, 0)], [(t, RWKV_COLS)], t, 1)[0]


def _shift_bwd(name, p, col0, mu, dprs):
    t = p.shape[0]
    wins = _windows(t)

    def body(x_ref, mu_ref, d_ref, dx_ref, dmu_ref, xp, dyp):
        _zero_front(xp)
        _zero_back(dyp, t)
        xp[pl.ds(CONV_PAD, t), :] = x_ref[...]
        mu_v = mu_ref[...]
        acc = None
        for r0, rc in wins:
            rows = pl.ds(r0, rc)
            d = d_ref[rows, :]
            acc = _add_to(acc, [_fold(d * (xp[pl.ds(CONV_PAD - 1 + r0, rc), :] - x_ref[rows, :]))])
            dyp[rows, :] = d * mu_v
        dmu_ref[...] = jnp.sum(acc[0], axis=0, keepdims=True)
        for r0, rc in wins:
            rows = pl.ds(r0, rc)
            dx_ref[rows, :] = d_ref[rows, :] - dyp[rows, :] + dyp[pl.ds(1 + r0, rc), :]

    ins = [(p, t, col0), (mu, 1, 0), (dprs, t, 0)]
    return _col_call(body, name, RWKV_COLS // LANES, ins, [(t, RWKV_COLS), (1, RWKV_COLS)], t, 2)


def _hi_lo(x):
    hi = x.astype(BF16)
    return hi, (x - hi.astype(F32)).astype(BF16)


def _dot_passes(a, b, dims, passes):
    d = lambda p, q: lax.dot_general(p, q, dims, preferred_element_type=F32)
    if passes == 1:
        return d(a.astype(MXU_DTYPE), b.astype(MXU_DTYPE))
    ah, al = _hi_lo(a)
    bh, bl = _hi_lo(b)
    return d(ah, bh) + (d(ah, bl) + d(al, bh))


@functools.partial(jax.custom_vjp, nondiff_argnums=(2, 3))
def _dot_vjp(a, b, dims, passes):
    return _dot_passes(a, b, dims, passes)


def _dot_fwd(a, b, dims, passes):
    return _dot_passes(a, b, dims, passes), (a, b)


def _dot_bwd(dims, passes, res, g):
    a, b = res
    if dims == _NN:
        return _dot_passes(g, b, _NT, passes), _dot_passes(a, g, _TN, passes)
    if dims == _NT:
        return _dot_passes(g, b, _NN, passes), _dot_passes(g, a, _TN, passes)
    return _dot_passes(b, g, _NT, passes), _dot_passes(a, g, _NN, passes)


_dot_vjp.defvjp(_dot_fwd, _dot_bwd)


def _doth(a, b, dims=_NN):
    return _dot_vjp(a, b, dims, 3)


def _dotb(a, b, dims=_NN):
    return _dot_vjp(a, b, dims, 1)


def _softplus(x):
    return jnp.where(x > 0, x, 0.0) + jnp.log(1.0 + jnp.exp(jnp.where(x > 0, -x, x)))


def _rwkv_pre(k, xl, gd, w0, w2p, a0, a2p, g2, k_k, k_a, seg):
    z = w0 + _dotb(jnp.tanh(xl), w2p)
    lw = -jnp.exp(-_softplus(-z) - 0.5)
    alpha = _sigmoid(a0 + _dotb(xl, a2p))
    g = _dotb(_sigmoid(gd), g2)
    kk = k * k_k
    kk = kk / jnp.maximum(jnp.sqrt(_dotb(kk * kk, seg)), 1e-12)
    k2 = k * (1.0 + (alpha - 1.0) * k_a)
    return lw, k2, -kk, kk * alpha, g


def _rwkv_post(y, r, k2, v, g, lnx_g, lnx_b, r_k, seg):
    mean = _dotb(y, seg) * (1.0 / HEAD_DIM)
    yc = y - mean
    var = _dotb(yc * yc, seg) * (1.0 / HEAD_DIM)
    yo = yc * lax.rsqrt(var + RWKV_GN_EPS) * lnx_g + lnx_b
    bonus = _dotb(r * k2 * r_k, seg) * v
    return (yo + bonus) * g


def _rwkv_pre_fwd(name, prs, prm, seg, tr):
    t = prs.shape[0]
    row = lambda i: (i, 0)
    fix = lambda i: (0, 0)
    ins = [(prs, (tr, D_R), lambda i: (i, 1)), (prs, (tr, LANES), lambda i: (i, 12)), (prs, (tr, LANES), lambda i: (i, 13)),
           (prm["w0"], (1, D_R), fix), (prm["w2p"], (LANES, D_R), fix), (prm["a0"], (1, D_R), fix),
           (prm["a2p"], (LANES, D_R), fix), (prm["g2"], (LANES, D_R), fix), (prm["k_k"], (1, D_R), fix),
           (prm["k_a"], (1, D_R), fix), (seg, (D_R, D_R), fix)]
    return _call(_rwkv_pre, name, (t // tr,), ins, [((t, D_R), (tr, D_R), row, False)] * 5)


def _rwkv_pre_bwd(name, prs, prm, seg, cts, tr):
    t = prs.shape[0]

    def fn(k, xl, gd, w0, w2p, a0, a2p, g2, k_k, k_a, segv, *ct):
        _, vjp = jax.vjp(lambda *a: _rwkv_pre(*a, segv), k, xl, gd, w0, w2p, a0, a2p, g2, k_k, k_a)
        return vjp(tuple(ct))

    row = lambda i: (i, 0)
    fix = lambda i: (0, 0)
    ins = [(prs, (tr, D_R), lambda i: (i, 1)), (prs, (tr, LANES), lambda i: (i, 12)), (prs, (tr, LANES), lambda i: (i, 13)),
           (prm["w0"], (1, D_R), fix), (prm["w2p"], (LANES, D_R), fix), (prm["a0"], (1, D_R), fix),
           (prm["a2p"], (LANES, D_R), fix), (prm["g2"], (LANES, D_R), fix), (prm["k_k"], (1, D_R), fix),
           (prm["k_a"], (1, D_R), fix), (seg, (D_R, D_R), fix)] + [(c, (tr, D_R), row) for c in cts]
    outs = [((t, D_R), (tr, D_R), row, False), ((t, LANES), (tr, LANES), row, False), ((t, LANES), (tr, LANES), row, False),
            ((1, D_R), (1, D_R), fix, True), ((LANES, D_R), (LANES, D_R), fix, True), ((1, D_R), (1, D_R), fix, True),
            ((LANES, D_R), (LANES, D_R), fix, True), ((LANES, D_R), (LANES, D_R), fix, True),
            ((1, D_R), (1, D_R), fix, True), ((1, D_R), (1, D_R), fix, True)]
    return _call(fn, name, (t // tr,), ins, outs, acc_axis=0)


def _rwkv_post_ins(y, prs, k2, g, prm, seg, tr):
    row = lambda i: (i, 0)
    fix = lambda i: (0, 0)
    return [(y, (tr, D_R), row), (prs, (tr, D_R), row), (k2, (tr, D_R), row), (prs, (tr, D_R), lambda i: (i, 2)),
            (g, (tr, D_R), row), (prm["lnx_g"], (1, D_R), fix), (prm["lnx_b"], (1, D_R), fix), (prm["r_k"], (1, D_R), fix),
            (seg, (D_R, D_R), fix)]


def _rwkv_post_fwd(name, y, prs, k2, g, prm, seg, tr):
    t = y.shape[0]
    return _call(_rwkv_post, name, (t // tr,), _rwkv_post_ins(y, prs, k2, g, prm, seg, tr),
                 [((t, D_R), (tr, D_R), lambda i: (i, 0), False)])


def _rwkv_post_bwd(name, y, prs, k2, g, prm, seg, dy, dy_col, tr):
    t = y.shape[0]

    def fn(yv, r, k2v, v, gv, lg, lb, rk, segv, ct):
        _, vjp = jax.vjp(lambda *a: _rwkv_post(*a, segv), yv, r, k2v, v, gv, lg, lb, rk)
        return vjp(ct)

    row = lambda i: (i, 0)
    fix = lambda i: (0, 0)
    ins = _rwkv_post_ins(y, prs, k2, g, prm, seg, tr) + [(dy, (tr, D_R), lambda i: (i, dy_col))]
    outs = [((t, D_R), (tr, D_R), row, False)] * 5 + [((1, D_R), (1, D_R), fix, True)] * 3
    return _call(fn, name, (t // tr,), ins, outs, acc_axis=0)


def _wkv_chunk(s0, r, lw, k, v, a, b):
    c = r[0].shape[0]
    lane = lax.broadcasted_iota(jnp.int32, (1, 2 * HEAD_DIM), 1)
    first = (lane < HEAD_DIM).astype(F32)
    per_head = lambda x: jnp.concatenate([x * first, x * (1.0 - first)], axis=0)

    def time_of(shape, dim):
        i = lax.broadcasted_iota(jnp.int32, shape, dim)
        return jnp.where(i >= c, i - c, i)

    incl = (lax.broadcasted_iota(jnp.int32, (c, c), 0) >= lax.broadcasted_iota(jnp.int32, (c, c), 1)).astype(F32)
    strict2 = time_of((2 * c, 2 * c), 0) > time_of((2 * c, 2 * c), 1)
    incl2 = lax.broadcasted_iota(jnp.int32, (c, 2 * c), 0) >= time_of((c, 2 * c), 1)
    each = lambda f, *xs: [f(*x) for x in zip(*xs)]
    cum = each(lambda x: _doth(incl, x), lw)
    tot = each(lambda x: jnp.sum(x, axis=0, keepdims=True), lw)
    e_inv = each(lambda x: jnp.exp(-x), cum)
    a_st = each(lambda x, cm, l: per_head(x * jnp.exp(cm - l)), a, cum, lw)
    r_t = each(lambda x, cm: x * jnp.exp(cm), r, cum)
    b_st = each(lambda x, e: per_head(x * e), b, e_inv)
    k_st = each(lambda x, e: per_head(x * e), k, e_inv)
    v_st = each(per_head, v)
    m = each(lambda x, w: jnp.where(strict2, _dotb(x, w, _NT), 0.0), a_st, b_st)
    m_k = each(lambda x, w: jnp.where(strict2, _dotb(x, w, _NT), 0.0), a_st, k_st)
    u = each(lambda x, s, mk, w: _dotb(x, s, _NT) + _dotb(mk, w), a_st, s0, m_k, v_st)
    steps = (c - 1).bit_length()
    for s in range(steps):
        u = each(lambda x, w: x + _dotb(w, x), u, m)
        if s + 1 < steps:
            m = each(lambda w: _dotb(w, w), m)
    n_b = each(lambda x, w: jnp.where(incl2, _dotb(x, w, _NT), 0.0), r_t, b_st)
    n_k = each(lambda x, w: jnp.where(incl2, _dotb(x, w, _NT), 0.0), r_t, k_st)
    y = each(lambda x, s, nb, uu, nk, w: _dotb(x, s, _NT) + _dotb(nb, uu) + _dotb(nk, w), r_t, s0, n_b, u, n_k, v_st)
    dec = each(lambda tt, cm: jnp.exp(tt - cm), tot, cum)
    s1 = each(lambda s, tt, uu, x, d, w, kk: s * jnp.exp(tt) + _dotb(uu, per_head(x * d), _TN) + _dotb(w, per_head(kk * d), _TN),
              s0, tot, u, b, dec, v_st, k)
    return tuple(y), tuple(s1)


WKV_PAIRS_PER_STEP = 4
PAIR = 2 * HEAD_DIM


def _wkv_fwd(name, srcs):
    t = srcs[0][0].shape[0]
    c = _chunk_len(t)
    nc = t // c
    pp = WKV_PAIRS_PER_STEP
    n_pairs = D_R // PAIR

    def body(r, lw, k, v, a, b, y_ref, st_ref, state):
        @pl.when(pl.program_id(1) == 0)
        def _():
            state[...] = jnp.zeros(state.shape, F32)

        pairs = lambda ref: tuple(ref[:, pl.ds(i * PAIR, PAIR)] for i in range(pp))
        s0 = tuple(state[i] for i in range(pp))
        y, s1 = _wkv_chunk(s0, pairs(r), pairs(lw), pairs(k), pairs(v), pairs(a), pairs(b))
        for i in range(pp):
            st_ref[i] = s0[i]
            y_ref[:, pl.ds(i * PAIR, PAIR)] = y[i]
            state[i] = s1[i]

    seq = lambda off: pl.BlockSpec((c, pp * PAIR), lambda g, j: (j, off + g))
    return pl.pallas_call(
        body, name=name, grid=(n_pairs // pp, nc), in_specs=[seq(off) for _, off in srcs],
        out_specs=[seq(0), pl.BlockSpec((pp, None, PAIR, PAIR), lambda g, j: (g, j, 0, 0))],
        out_shape=[jax.ShapeDtypeStruct((t, D_R), F32), jax.ShapeDtypeStruct((n_pairs, nc, PAIR, PAIR), F32)],
        scratch_shapes=[pltpu.VMEM((pp, PAIR, PAIR), F32)],
        compiler_params=_cparams(None),
    )(*[a for a, _ in srcs])


def _wkv_bwd(name, srcs, st, dy):
    t = srcs[0][0].shape[0]
    c = _chunk_len(t)
    nc = t // c
    pp = WKV_PAIRS_PER_STEP
    n_pairs = D_R // PAIR

    def body(r, lw, k, v, a, b, st_ref, dy_ref, dr, dlw, dk, dv, da, db, dstate):
        @pl.when(pl.program_id(1) == 0)
        def _():
            dstate[...] = jnp.zeros(dstate.shape, F32)

        half = lax.broadcasted_iota(jnp.int32, (PAIR, PAIR), 0) < HEAD_DIM
        same_head = half == (lax.broadcasted_iota(jnp.int32, (PAIR, PAIR), 1) < HEAD_DIM)
        pairs = lambda ref: tuple(ref[:, pl.ds(i * PAIR, PAIR)] for i in range(pp))
        s0 = tuple(st_ref[i] for i in range(pp))
        _, vjp = jax.vjp(_wkv_chunk, s0, pairs(r), pairs(lw), pairs(k), pairs(v), pairs(a), pairs(b))
        ds0, *dxs = vjp((pairs(dy_ref), tuple(dstate[i] for i in range(pp))))
        for i in range(pp):
            for ref, val in zip((dr, dlw, dk, dv, da, db), dxs):
                ref[:, pl.ds(i * PAIR, PAIR)] = val[i]
            dstate[i] = jnp.where(same_head, ds0[i], 0.0)

    seq = lambda off: pl.BlockSpec((c, pp * PAIR), lambda g, j: (nc - 1 - j, off + g))
    return pl.pallas_call(
        body, name=name, grid=(n_pairs // pp, nc),
        in_specs=[seq(off) for _, off in srcs]
        + [pl.BlockSpec((pp, None, PAIR, PAIR), lambda g, j: (g, nc - 1 - j, 0, 0)), seq(dy[1])],
        out_specs=[seq(0)] * 6,
        out_shape=[jax.ShapeDtypeStruct((t, D_R), F32)] * 6,
        scratch_shapes=[pltpu.VMEM((pp, PAIR, PAIR), F32)],
        compiler_params=_cparams(None),
    )(*[a for a, _ in srcs], st, dy[0])


def _rope(x, cos, sin, rot):
    return x * cos + _dotb(x, rot) * sin


def _attn_block(nb, q, kp, kc, km, vp, vc, vm, sk, cq, sq, cp, sp, cm, sm, rot):
    g = GQA_GROUP
    scale = HEAD_DIM ** -0.5
    down = lambda x: jnp.concatenate([x] * g, axis=0)
    kpr = _rope(kp, cp, sp, rot)
    kcr = _rope(kc, cq, sq, rot)
    kmr = _rope(km, cm, sm, rot)
    qr = _rope(q, down(cq), down(sq), rot)
    i = lax.broadcasted_iota(jnp.int32, (g * BLOCK, BLOCK), 0)
    i = i - BLOCK * ((i >= BLOCK).astype(jnp.int32) + (i >= 2 * BLOCK).astype(jnp.int32) + (i >= 3 * BLOCK).astype(jnp.int32))
    j = lax.broadcasted_iota(jnp.int32, (g * BLOCK, BLOCK), 1)
    nbv = jnp.zeros((g * BLOCK, BLOCK), jnp.int32) + nb
    ok_p = (j > i) & (nbv >= 2)
    ok_c = (j <= i) & (nbv >= 1)
    ok_m = (j >= BLOCK - N_META) & ((nbv >= 1) | (j <= i))
    sink = jnp.concatenate([jnp.broadcast_to(s, (BLOCK, 1)) for s in sk], axis=0)
    s_p = jnp.where(ok_p, _dotb(qr, kpr, _NT) * scale, NEG_INF)
    s_c = jnp.where(ok_c, _dotb(qr, kcr, _NT) * scale, NEG_INF)
    s_m = jnp.where(ok_m, _dotb(qr, kmr, _NT) * scale, NEG_INF)
    rmax = lambda s: jnp.max(s, axis=-1, keepdims=True)
    m = lax.stop_gradient(jnp.maximum(jnp.maximum(rmax(s_p), rmax(s_c)), jnp.maximum(rmax(s_m), sink)))
    e_p, e_c, e_m = jnp.exp(s_p - m), jnp.exp(s_c - m), jnp.exp(s_m - m)
    rsum = lambda e: jnp.sum(e, axis=-1, keepdims=True)
    inv = 1.0 / (rsum(e_p) + rsum(e_c) + rsum(e_m) + jnp.exp(sink - m))
    return _dotb(e_p * inv, vp) + _dotb(e_c * inv, vc) + _dotb(e_m * inv, vm)


def _attn_specs():
    cur = lambda g, n: (g, n, 0)
    prev = lambda g, n: (g, jnp.maximum(n - 1, 0), 0)
    meta = lambda g, n: (g, 0, 0)
    kv = lambda m: pl.BlockSpec((None, BLOCK, HEAD_DIM), m)
    tab = lambda m: pl.BlockSpec((BLOCK, HEAD_DIM), m)
    tcur, tprev, tmeta = (lambda g, n: (n, 0)), (lambda g, n: (jnp.maximum(n - 1, 0), 0)), (lambda g, n: (0, 0))
    qspec = pl.BlockSpec((GQA_GROUP, BLOCK, HEAD_DIM), cur)
    sspec = pl.BlockSpec((GQA_GROUP, 8, LANES), meta)
    specs = [qspec, kv(prev), kv(cur), kv(meta), kv(prev), kv(cur), kv(meta), sspec,
             tab(tcur), tab(tcur), tab(tprev), tab(tprev), tab(tmeta), tab(tmeta),
             pl.BlockSpec((HEAD_DIM, HEAD_DIM), lambda g, n: (0, 0))]
    return specs, qspec, sspec, kv


def _attn_args(q, k, v, sinks_b, cos, sin, rot):
    return (q, k, k, k, v, v, v, sinks_b, cos, sin, cos, sin, cos, sin, rot)


def _attn_fwd(name, q, k, v, sinks_b, cos, sin, rot):
    tp = q.shape[1]
    specs, qspec, _, _ = _attn_specs()

    def body(q_ref, kp, kc, km, vp, vc, vm, s_ref, cq, sq, cp, sp, cm, sm, rot_ref, o_ref):
        q = jnp.concatenate([q_ref[h] for h in range(GQA_GROUP)], axis=0)
        sk = tuple(s_ref[h][0:1, 0:1] for h in range(GQA_GROUP))
        out = _attn_block(pl.program_id(1), q, kp[...], kc[...], km[...], vp[...], vc[...], vm[...], sk,
                          cq[...], sq[...], cp[...], sp[...], cm[...], sm[...], rot_ref[...])
        for h in range(GQA_GROUP):
            o_ref[h] = out[h * BLOCK:(h + 1) * BLOCK]

    return pl.pallas_call(
        body, name=name, grid=(N_KV_HEADS, tp // BLOCK), in_specs=specs, out_specs=qspec,
        out_shape=jax.ShapeDtypeStruct(q.shape, F32), compiler_params=_cparams(None),
    )(*_attn_args(q, k, v, sinks_b, cos, sin, rot))


def _attn_bwd(name, q, k, v, sinks_b, cos, sin, rot, do):
    tp = q.shape[1]
    nb = tp // BLOCK
    specs, qspec, sspec, kv = _attn_specs()

    def body(q_ref, kp, kc, km, vp, vc, vm, s_ref, cq, sq, cp, sp, cm, sm, rot_ref, do_ref,
             dq_ref, dkp, dkc, dvp, dvc, dkm, dvm, ds_ref):
        n = pl.program_id(1)
        q = jnp.concatenate([q_ref[h] for h in range(GQA_GROUP)], axis=0)
        sk = tuple(s_ref[h][0:1, 0:1] for h in range(GQA_GROUP))
        tabs = (cq[...], sq[...], cp[...], sp[...], cm[...], sm[...], rot_ref[...])
        _, vjp = jax.vjp(lambda *a: _attn_block(n, *a, *tabs), q, kp[...], kc[...], km[...], vp[...], vc[...], vm[...], sk)
        dq, gkp, gkc, gkm, gvp, gvc, gvm, dsk = vjp(jnp.concatenate([do_ref[h] for h in range(GQA_GROUP)], axis=0))
        dkp[...] = gkp
        dkc[...] = gkc
        dvp[...] = gvp
        dvc[...] = gvc
        for h in range(GQA_GROUP):
            dq_ref[h] = dq[h * BLOCK:(h + 1) * BLOCK]

        @pl.when(n == 0)
        def _():
            dkm[...] = gkm
            dvm[...] = gvm
            for h in range(GQA_GROUP):
                ds_ref[h] = jnp.broadcast_to(dsk[h], (8, LANES))

        @pl.when(n != 0)
        def _():
            dkm[...] += gkm
            dvm[...] += gvm
            for h in range(GQA_GROUP):
                ds_ref[h] += jnp.broadcast_to(dsk[h], (8, LANES))

    part = pl.BlockSpec((None, None, BLOCK, HEAD_DIM), lambda g, n: (g, n, 0, 0))
    part_shape = jax.ShapeDtypeStruct((N_KV_HEADS, nb, BLOCK, HEAD_DIM), F32)
    meta_shape = jax.ShapeDtypeStruct((N_KV_HEADS, BLOCK, HEAD_DIM), F32)
    return pl.pallas_call(
        body, name=name, grid=(N_KV_HEADS, nb), in_specs=specs + [qspec],
        out_specs=[qspec, part, part, part, part, kv(lambda g, n: (g, 0, 0)), kv(lambda g, n: (g, 0, 0)), sspec],
        out_shape=[jax.ShapeDtypeStruct(q.shape, F32), part_shape, part_shape, part_shape, part_shape,
                   meta_shape, meta_shape, jax.ShapeDtypeStruct(sinks_b.shape, F32)],
        compiler_params=_cparams(None),
    )(*_attn_args(q, k, v, sinks_b, cos, sin, rot), do)


def _kv_combine(name, prev_part, own_part, meta):
    g, nb = own_part.shape[:2]

    def fn(own, nxt, mt):
        m = pl.program_id(1)
        one = jnp.ones((BLOCK, HEAD_DIM), F32)
        use_next = jnp.where(one * m < nb - 1, 1.0, 0.0)
        use_meta = jnp.where(one * m < 1, 1.0, 0.0)
        return own + nxt * use_next + mt * use_meta

    blk = (None, None, BLOCK, HEAD_DIM)
    return _call(fn, name, (g, nb),
                 [(own_part, blk, lambda a, m: (a, m, 0, 0)),
                  (prev_part, blk, lambda a, m: (a, jnp.minimum(m + 1, nb - 1), 0, 0)),
                  (meta, (None, BLOCK, HEAD_DIM), lambda a, m: (a, 0, 0))],
                 [((g, nb * BLOCK, HEAD_DIM), (None, BLOCK, HEAD_DIM), lambda a, m: (a, m, 0), False)])


PACK_W = 1024
ELEMENTWISE_BLOCK_BYTES = 1 << 21


def _rows_tile(rows, cols):
    cap = max(8, ELEMENTWISE_BLOCK_BYTES // (4 * cols))
    for d in range(min(rows, cap), 0, -1):
        if rows % d == 0 and d % 8 == 0:
            return d
    return rows


def _adamw(name, w, g, m, v):
    rows, cols = w.shape
    tr = _rows_tile(rows, cols)

    def fn(wv, gv, mv, vv):
        m1 = ADAM_B1 * mv + (1.0 - ADAM_B1) * gv
        v1 = ADAM_B2 * vv + (1.0 - ADAM_B2) * (gv * gv)
        m_hat = m1 / (1.0 - ADAM_B1 ** ADAM_STEP)
        v_hat = v1 / (1.0 - ADAM_B2 ** ADAM_STEP)
        return -ADAM_LR * (m_hat / (jnp.sqrt(v_hat) + ADAM_EPS) + ADAM_WD * wv), m1, v1

    blk = (tr, cols)
    row = lambda i: (i, 0)
    return _call(fn, name, (rows // tr,), [(a, blk, row) for a in (w, g, m, v)], [((rows, cols), blk, row, False)] * 3)


def _pair_add(name, g, recv, c_idx, out_dtype):
    s, a, b = g.shape
    half = a // 2

    def body(c_ref, a_ref, b_ref, o_ref):
        o_ref[...] = (a_ref[...] + b_ref[...]).astype(out_dtype)

    blk = (None, half, b)
    return pl.pallas_call(
        body, name=name,
        grid_spec=pltpu.PrefetchScalarGridSpec(
            num_scalar_prefetch=1, grid=(s,),
            in_specs=[pl.BlockSpec(blk, lambda j, c: (j, c[0], 0)), pl.BlockSpec(blk, lambda j, c: (j, 0, 0))],
            out_specs=pl.BlockSpec(blk, lambda j, c: (j, 0, 0))),
        out_shape=jax.ShapeDtypeStruct((s, half, b), out_dtype), compiler_params=_cparams(None),
    )(c_idx, g, recv)


def _pair_add_placed(name, g, recv, cm_idx, out_dtype):
    s, a, b = g.shape
    half = a // 2

    def body(cm_ref, a_ref, b_ref, o_ref, own_ref):
        val = (a_ref[...] + b_ref[...]).astype(out_dtype)
        o_ref[...] = val

        @pl.when(pl.program_id(0) == cm_ref[1])
        def _():
            own_ref[...] = val

    blk = (None, half, b)
    shape = jax.ShapeDtypeStruct((s, half, b), out_dtype)
    return pl.pallas_call(
        body, name=name,
        grid_spec=pltpu.PrefetchScalarGridSpec(
            num_scalar_prefetch=1, grid=(s,),
            in_specs=[pl.BlockSpec(blk, lambda j, cm: (j, cm[0], 0)), pl.BlockSpec(blk, lambda j, cm: (j, 0, 0))],
            out_specs=[pl.BlockSpec(blk, lambda j, cm: (j, 0, 0)), pl.BlockSpec(blk, lambda j, cm: (cm[1], 0, 0))]),
        out_shape=[shape, shape], compiler_params=_cparams(None),
    )(cm_idx, g, recv)


def _sum_chips(name, parts, c_idx, layer, n_layers, into=None):
    _, a, b = parts.shape
    tr = _rows_tile(a, b)

    def body(c_ref, p0, p1, p2, p3, *rest):
        o_ref = rest[-1]
        up = lambda p: p[...].astype(F32)
        o_ref[...] = ((up(p0) + up(p1)) + up(p2)) + up(p3)

    in_specs = [pl.BlockSpec((None, tr, b), lambda i, c, k=k: (k, i, 0)) for k in range(N_CHIPS)]
    args = [c_idx] + [parts] * N_CHIPS
    aliases = {}
    if into is not None:
        in_specs.append(_ANY)
        args.append(into)
        aliases = {1 + N_CHIPS: 0}
    return pl.pallas_call(
        body, name=name,
        grid_spec=pltpu.PrefetchScalarGridSpec(
            num_scalar_prefetch=1, grid=(a // tr,), in_specs=in_specs,
            out_specs=pl.BlockSpec((None, None, tr, b), lambda i, c: (layer, c[0], i, 0))),
        out_shape=jax.ShapeDtypeStruct((n_layers, 2, a, b), F32), input_output_aliases=aliases,
        compiler_params=_cparams(None),
    )(*args)


def _place_own_block(name, w, layer, me_idx, dtype):
    _, a2, b = w.shape
    a = a2 // 2
    tr = _rows_tile(a, b)
    nb = a // tr

    def body(me_ref, w_ref, o_ref):
        o_ref[...] = w_ref[...].astype(dtype)

    return pl.pallas_call(
        body, name=name,
        grid_spec=pltpu.PrefetchScalarGridSpec(
            num_scalar_prefetch=1, grid=(2, nb),
            in_specs=[pl.BlockSpec((None, tr, b), lambda h, i, me: (layer, h * nb + i, 0))],
            out_specs=pl.BlockSpec((None, None, tr, b), lambda h, i, me: (me[0], h, i, 0))),
        out_shape=jax.ShapeDtypeStruct((N_CHIPS, 2, a, b), dtype), compiler_params=_cparams(None),
    )(me_idx, w)


def _mesh_pos():
    return lax.axis_index("x"), lax.axis_index("y"), lax.axis_index("c")


def _other_chips(x, y):
    return [(1 - x, y), (x, 1 - y), (1 - x, 1 - y)]


_ANY = pl.BlockSpec(memory_space=pl.ANY)


def _gather_weights(name, bufs, from_chips=True):
    n = len(bufs)

    def body(*refs):
        out_refs = refs[n:2 * n]
        send_sems, recv_sems = refs[2 * n:]
        x, y, c = _mesh_pos()
        me = 2 * x + y
        sibling = (x, y, 1 - c)
        chips = _other_chips(x, y)

        def copy(i, k, chip_idx, half, to):
            return pltpu.make_async_remote_copy(src_ref=out_refs[i].at[chip_idx, half], dst_ref=out_refs[i].at[chip_idx, half],
                                                send_sem=send_sems.at[6 * i + k], recv_sem=recv_sems.at[6 * i + k],
                                                device_id=to, device_id_type=MESH)

        first = [copy(i, j, me, c, (*chip, c)) for i in range(n) for j, chip in enumerate(chips)] if from_chips else []
        for cp in first:
            cp.start()
        passed = []
        for i in range(n):
            for j, (cx, cy) in enumerate(chips):
                idx = 2 * cx + cy
                if from_chips:
                    copy(i, j, idx, c, sibling).wait_recv()
                fwd = copy(i, 3 + j, idx, c, sibling)
                fwd.start()
                passed.append(fwd)
        for i in range(n):
            for j, (cx, cy) in enumerate(chips):
                copy(i, 3 + j, 2 * cx + cy, 1 - c, sibling).wait_recv()
        for cp in first + passed:
            cp.wait_send()

    return pl.pallas_call(
        body, name=name, in_specs=[_ANY] * n, out_specs=[_ANY] * n,
        out_shape=[jax.ShapeDtypeStruct(b.shape, b.dtype) for b in bufs],
        input_output_aliases={i: i for i in range(n)},
        scratch_shapes=[pltpu.SemaphoreType.DMA((6 * n,)), pltpu.SemaphoreType.DMA((6 * n,))],
        compiler_params=pltpu.CompilerParams(has_side_effects=True),
    )(*bufs)


def _gather_start(name, groups):
    bufs = [b for g in groups for b in g]
    n = len(bufs)
    ng = len(groups)

    def body(*refs):
        b_refs = refs[:n]
        sems = refs[n:n + 2 * ng]
        token = refs[-1]
        x, y, c = _mesh_pos()
        me = 2 * x + y
        i = 0
        for gi, g in enumerate(groups):
            for k in range(len(g)):
                for j, (cx, cy) in enumerate(_other_chips(x, y)):
                    pltpu.make_async_remote_copy(src_ref=b_refs[i].at[me, c], dst_ref=b_refs[i].at[me, c],
                                                 send_sem=sems[2 * gi].at[3 * k + j], recv_sem=sems[2 * gi + 1].at[3 * k + j],
                                                 device_id=(cx, cy, c), device_id_type=MESH).start()
                i += 1
        token[...] = jnp.zeros(token.shape, F32)

    sem_shapes = [pltpu.SemaphoreType.DMA((3 * len(g),)) for g in groups for _ in range(2)]
    res = pl.pallas_call(
        body, name=name,
        out_shape=(*sem_shapes, *[pltpu.HBM(b.shape, b.dtype) for b in bufs], jax.ShapeDtypeStruct((8, LANES), F32)),
        in_specs=[_HBM] * n,
        out_specs=(*[_SEM] * (2 * ng), *[_HBM] * n, pl.BlockSpec(memory_space=pltpu.VMEM)),
        input_output_aliases={i: 2 * ng + i for i in range(n)},
        compiler_params=pltpu.CompilerParams(has_side_effects=_DATAFLOW),
    )(*[pltpu.with_memory_space_constraint(b, pltpu.HBM) for b in bufs])
    out, i = [], 2 * ng
    for gi, g in enumerate(groups):
        out.append((res[2 * gi], res[2 * gi + 1], list(res[i:i + len(g)])))
        i += len(g)
    return out, res[-1]


def _gather_wait(name, send_sems, recv_sems, bufs, after):
    n = len(bufs)

    def body(*refs):
        b_refs = refs[:n]
        s_sems, r_sems = refs[n], refs[n + 1]
        x, y, c = _mesh_pos()
        me = 2 * x + y
        for k in range(n):
            for j, (cx, cy) in enumerate(_other_chips(x, y)):
                idx = 2 * cx + cy
                copy = pltpu.make_async_remote_copy(src_ref=b_refs[k].at[me, c], dst_ref=b_refs[k].at[idx, c],
                                                    send_sem=s_sems.at[3 * k + j], recv_sem=r_sems.at[3 * k + j],
                                                    device_id=(cx, cy, c), device_id_type=MESH)
                copy.wait_send()
                copy.wait_recv()

    res = pl.pallas_call(
        body, name=name,
        out_shape=tuple(pltpu.HBM(b.shape, b.dtype) for b in bufs),
        in_specs=[_HBM] * n + [_SEM, _SEM, _ANY],
        out_specs=tuple([_HBM] * n),
        input_output_aliases={i: i for i in range(n)},
        compiler_params=pltpu.CompilerParams(has_side_effects=_DATAFLOW),
    )(*bufs, send_sems, recv_sems, after)
    return list(res)


def _halves_to_sibling(name, units):
    n = len(units)

    def body(*refs):
        g_refs, out_refs = refs[:n], refs[n:2 * n]
        send_sems, recv_sems = refs[2 * n:]
        x, y, c = _mesh_pos()
        cps = []
        for i in range(n):
            half = units[i].shape[1] // 2
            src = g_refs[i].at[pl.ds(0, N_CHIPS), pl.ds((1 - c) * half, half)]
            cp = pltpu.make_async_remote_copy(src_ref=src, dst_ref=out_refs[i], send_sem=send_sems.at[i],
                                              recv_sem=recv_sems.at[i], device_id=(x, y, 1 - c), device_id_type=MESH)
            cp.start()
            cps.append(cp)
        for cp in cps:
            cp.wait()

    return pl.pallas_call(
        body, name=name, in_specs=[_ANY] * n, out_specs=[_ANY] * n,
        out_shape=[jax.ShapeDtypeStruct((u.shape[0], u.shape[1] // 2, u.shape[2]), u.dtype) for u in units],
        scratch_shapes=[pltpu.SemaphoreType.DMA((n,)), pltpu.SemaphoreType.DMA((n,))],
        compiler_params=pltpu.CompilerParams(has_side_effects=True),
    )(*units)


def _scatter_to_chips(name, sums):
    n = len(sums)

    def body(*refs):
        h_refs, out_refs = refs[:n], refs[n:2 * n]
        send_sems, recv_sems, local_sems = refs[2 * n:]
        x, y, c = _mesh_pos()
        me = 2 * x + y
        chips = _other_chips(x, y)
        local = [pltpu.make_async_copy(h_refs[i].at[me], out_refs[i].at[me], local_sems.at[i]) for i in range(n)]
        for cp in local:
            cp.start()

        def copy(i, j, src_idx, dst_idx):
            cx, cy = chips[j]
            return pltpu.make_async_remote_copy(src_ref=h_refs[i].at[src_idx], dst_ref=out_refs[i].at[dst_idx],
                                                send_sem=send_sems.at[3 * i + j], recv_sem=recv_sems.at[3 * i + j],
                                                device_id=(cx, cy, c), device_id_type=MESH)

        cps = [copy(i, j, 2 * chips[j][0] + chips[j][1], me) for i in range(n) for j in range(3)]
        for cp in cps:
            cp.start()
        for i in range(n):
            for j in range(3):
                copy(i, j, me, 2 * chips[j][0] + chips[j][1]).wait_recv()
        for cp in cps:
            cp.wait_send()
        for cp in local:
            cp.wait()

    return pl.pallas_call(
        body, name=name, in_specs=[_ANY] * n, out_specs=[_ANY] * n,
        out_shape=[jax.ShapeDtypeStruct(s.shape, s.dtype) for s in sums],
        scratch_shapes=[pltpu.SemaphoreType.DMA((3 * n,)), pltpu.SemaphoreType.DMA((3 * n,)), pltpu.SemaphoreType.DMA((n,))],
        compiler_params=pltpu.CompilerParams(has_side_effects=True),
    )(*sums)


_HBM = pl.BlockSpec(memory_space=pltpu.HBM)
_SEM = pl.BlockSpec(memory_space=pltpu.SEMAPHORE)
_DATAFLOW = pltpu.SideEffectType.DATAFLOW_SIDE_EFFECTING


def _scatter_start(name, sums, zones):
    n = len(sums)

    def body(*refs):
        h_refs, z_refs = refs[:n], refs[n:2 * n]
        send_sems, recv_sems = refs[2 * n], refs[2 * n + 1]
        token = refs[-1]
        x, y, c = _mesh_pos()
        me = 2 * x + y
        for i in range(n):
            for j, (cx, cy) in enumerate(_other_chips(x, y)):
                pltpu.make_async_remote_copy(src_ref=h_refs[i].at[2 * cx + cy], dst_ref=z_refs[i].at[me],
                                             send_sem=send_sems.at[3 * i + j], recv_sem=recv_sems.at[3 * i + j],
                                             device_id=(cx, cy, c), device_id_type=MESH).start()
        token[...] = jnp.zeros(token.shape, F32)

    hbm = lambda a: pltpu.HBM(a.shape, a.dtype)
    res = pl.pallas_call(
        body, name=name,
        out_shape=(pltpu.SemaphoreType.DMA((3 * n,)), pltpu.SemaphoreType.DMA((3 * n,)),
                   *[hbm(a) for a in sums], *[hbm(a) for a in zones], jax.ShapeDtypeStruct((8, LANES), F32)),
        in_specs=[_HBM] * (2 * n),
        out_specs=(_SEM, _SEM, *[_HBM] * (2 * n), pl.BlockSpec(memory_space=pltpu.VMEM)),
        input_output_aliases={i: 2 + i for i in range(2 * n)},
        compiler_params=pltpu.CompilerParams(has_side_effects=_DATAFLOW),
    )(*[pltpu.with_memory_space_constraint(a, pltpu.HBM) for a in list(sums) + list(zones)])
    return res[0], res[1], res[2:2 + n], res[2 + n:2 + 2 * n], res[-1]


def _scatter_wait(name, send_sems, recv_sems, sums, zones, after):
    n = len(sums)

    def body(*refs):
        h_refs, z_refs = refs[:n], refs[n:2 * n]
        s_sems, r_sems = refs[2 * n], refs[2 * n + 1]
        x, y, c = _mesh_pos()
        me = 2 * x + y
        for i in range(n):
            for j, (cx, cy) in enumerate(_other_chips(x, y)):
                idx = 2 * cx + cy
                copy = pltpu.make_async_remote_copy(src_ref=h_refs[i].at[idx], dst_ref=z_refs[i].at[idx],
                                                    send_sem=s_sems.at[3 * i + j], recv_sem=r_sems.at[3 * i + j],
                                                    device_id=(cx, cy, c), device_id_type=MESH)
                copy.wait_send()
                copy.wait_recv()

    hbm = lambda a: pltpu.HBM(a.shape, a.dtype)
    res = pl.pallas_call(
        body, name=name,
        out_shape=(*[hbm(a) for a in sums], *[hbm(a) for a in zones]),
        in_specs=[_HBM] * (2 * n) + [_SEM, _SEM, _ANY],
        out_specs=tuple([_HBM] * (2 * n)),
        input_output_aliases={i: i for i in range(2 * n)},
        compiler_params=pltpu.CompilerParams(has_side_effects=_DATAFLOW),
    )(*sums, *zones, send_sems, recv_sems, after)
    return res[n:]


def _join_halves(name, results):
    n = len(results)
    pieces = [(i, l) for i in range(n) for l in range(results[i].shape[0])]

    def body(*refs):
        out_refs = refs[n:2 * n]
        send_sems, recv_sems = refs[2 * n:]
        x, y, c = _mesh_pos()

        def copy(k, half):
            i, l = pieces[k]
            return pltpu.make_async_remote_copy(src_ref=out_refs[i].at[l, half], dst_ref=out_refs[i].at[l, half],
                                                send_sem=send_sems.at[k], recv_sem=recv_sems.at[k],
                                                device_id=(x, y, 1 - c), device_id_type=MESH)

        cps = [copy(k, c) for k in range(len(pieces))]
        for cp in cps:
            cp.start()
        for k in range(len(pieces)):
            copy(k, 1 - c).wait_recv()
        for cp in cps:
            cp.wait_send()

    return pl.pallas_call(
        body, name=name, in_specs=[_ANY] * n, out_specs=[_ANY] * n,
        out_shape=[jax.ShapeDtypeStruct(r.shape, r.dtype) for r in results],
        input_output_aliases={i: i for i in range(n)},
        scratch_shapes=[pltpu.SemaphoreType.DMA((len(pieces),)), pltpu.SemaphoreType.DMA((len(pieces),))],
        compiler_params=pltpu.CompilerParams(has_side_effects=True),
    )(*results)


def _pack(arrays, dtype, rows_multiple):
    flat = jnp.concatenate([a.reshape(-1).astype(dtype) for a in arrays])
    unit = rows_multiple * PACK_W
    total = -(-flat.shape[0] // unit) * unit
    return jnp.pad(flat, (0, total - flat.shape[0])).reshape(total // PACK_W, PACK_W)


def _unpack(flat, shapes):
    out, off = [], 0
    for s in shapes:
        n = 1
        for d in s:
            n *= d
        out.append(flat[..., off:off + n].reshape(flat.shape[:-1] + tuple(s)))
        off += n
    return out


def _ffn_fwd(tag, l, h, g, w_up, conv, bias, w_down, tm):
    hn = _rms_fwd(f"{tag}_norm", h, g, tm)
    u = _mm_cs(f"{tag}_up", hn, w_up, l, tm)
    act = _ffn_col_fwd(f"{tag}_glu", u, conv, bias)
    h_out = _mm_full(f"{tag}_down", act, w_down, l, tm, D_FF // 2, add=h)
    return h_out, (hn, u, act)


def _ffn_bwd(tag, l, h, g, w_up, conv, bias, w_down, saved, dh, tm):
    hn, u, act = saved
    da = _mm_nt_full(f"{tag}_down_dx", dh, w_down, l, tm, D_FF // 2)
    dw_down = _mm_tn_full(f"{tag}_down_dw", act, dh, tm, D_FF // 2)
    du, dconv, dbias = _ffn_col_bwd(f"{tag}_glu_bwd", u, da, conv, bias)
    dw_up = _mm_tn_cs(f"{tag}_up_dw", hn, du, N_CHIPS, tm)
    dhn = _mm_nt_cs(f"{tag}_up_dx", du, w_up, l, tm)
    dh, dg = _rms_bwd(f"{tag}_norm_bwd", h, g, dhn, dh, tm)
    return dh, dict(norm=dg, w_up=dw_up, conv=dconv, bias=dbias, w_down=dw_down)


def _to_heads(z, nh, pad):
    t = z.shape[0]
    return jnp.pad(z.reshape(t, nh, HEAD_DIM).transpose(1, 0, 2), ((0, 0), (pad, 0), (0, 0)))


def _from_heads(z, pad):
    nh, tp, _ = z.shape
    return z[:, pad:].transpose(1, 0, 2).reshape(tp - pad, nh * HEAD_DIM)


def _rope_tables(tp, pad):
    half = HEAD_DIM // 2
    inv = ROPE_THETA ** (-jnp.arange(half, dtype=F32) / half)
    ang = (jnp.arange(tp, dtype=F32) - pad)[:, None] * inv[None, :]
    cos, sin = jnp.cos(ang), jnp.sin(ang)
    rot = jnp.zeros((HEAD_DIM, HEAD_DIM), F32)
    idx = jnp.arange(half)
    rot = rot.at[idx + half, idx].set(-1.0).at[idx, idx + half].set(1.0)
    return jnp.concatenate([cos, cos], axis=1), jnp.concatenate([sin, sin], axis=1), rot


def _local_step(x, tgt, w, on_grads=None, fetch=None):
    emit = on_grads if on_grads is not None else (lambda tag, units: 0.0)
    need = (lambda tag, after: w) if fetch is None else (lambda tag, after: {**w, **fetch(tag, after)})
    seq = x.shape[0]
    t = seq + N_META
    tm = _row_tile(t, 704)
    tr = _row_tile(t, 352)
    pad = BLOCK - N_META
    grads = {}

    h0 = jnp.concatenate([w["meta_tokens"], x], axis=0)
    tgt_p = jnp.pad(tgt, ((N_META, 0), (0, 0)))

    hn0 = _rms_fwd("l0_norm", h0, w["norm_mix"][0:1], tm)
    p0 = _mm_cs("l0_in", hn0, w["ev_w_in"], 0, tm)
    uc, yb = _even_col_fwd("l0_convs", p0, w["ev_conv_a"], w["ev_conv_b"])
    ya = _even_ln_fwd("l0_ln", uc, w["ev_ln_a_g"], w["ev_ln_a_b"], tm)
    y0 = jnp.concatenate([ya, yb], axis=1)
    w = need("ev_out", y0)
    h1 = _mm_full("l0_out", y0, w["ev_w_out"], 0, tm, D_MODEL, add=h0)
    w = need("f0", h1)
    f0 = (0, h1, w["norm_ffn"][0:1], w["ff_w_up0"], w["ff_conv"][0], w["ff_conv_b"][0:1], w["ff_w_down0"])
    h2, ffn0 = _ffn_fwd("f0", *f0, tm)
    w = need("od", h2)

    hn2 = _rms_fwd("l1_norm", h2, w["norm_mix"][1:2], tm)
    p1 = _mm_cs("l1_in", hn2, w["od_w_in"], 0, tm)
    cos, sin, rot = _rope_tables(t + pad, pad)
    qh = _to_heads(p1[:, :D_ATT], N_Q_HEADS, pad)
    kh = _to_heads(p1[:, D_ATT:D_ATT + D_KV], N_KV_HEADS, pad)
    vh = _to_heads(p1[:, D_ATT + D_KV:D_ATT + 2 * D_KV], N_KV_HEADS, pad)
    sinks_b = jnp.broadcast_to(w["od_sinks"].reshape(N_Q_HEADS, 1, 1), (N_Q_HEADS, 8, LANES))
    y_att = _from_heads(_attn_fwd("l1_attn", qh, kh, vh, sinks_b, cos, sin, rot), pad)

    col0 = (D_ATT + 2 * D_KV) // LANES
    ch = jnp.arange(D_R) // HEAD_DIM
    seg = (ch[:, None] == ch[None, :]).astype(F32)
    prm = dict(w0=w["od_w0"], a0=w["od_a0"], g2=w["od_g2"], k_k=w["od_k_k"], k_a=w["od_k_a"],
               lnx_g=w["od_lnx_g"], lnx_b=w["od_lnx_b"], r_k=w["od_r_k"].reshape(1, D_R),
               w2p=jnp.concatenate([w["od_w2"], jnp.zeros((LORA_A, D_R), F32)], axis=0),
               a2p=jnp.concatenate([jnp.zeros((LORA_W, D_R), F32), w["od_a2"]], axis=0))
    prs = _shift_fwd("l1_shift", p1, col0, w["od_mu"])
    lw, k2, a_, b_, gate_r = _rwkv_pre_fwd("l1_rwkv_pre", prs, prm, seg, tr)
    v_off = 2 * D_R // (WKV_PAIRS_PER_STEP * PAIR)
    scan_in = [(prs, 0), (lw, 0), (k2, 0), (prs, v_off), (a_, 0), (b_, 0)]
    y_scan, states = _wkv_fwd("l1_wkv", scan_in)
    y_rwkv = _rwkv_post_fwd("l1_rwkv_post", y_scan, prs, k2, gate_r, prm, seg, tr)
    y1 = jnp.concatenate([y_att, y_rwkv], axis=1).astype(MXU_DTYPE)
    h3 = _mm_full("l1_out", y1, w["od_w_out"], 0, tm, D_MODEL, add=h2)
    w = need("f1", h3)
    f1 = (0, h3, w["norm_ffn"][1:2], w["ff_w_up1"], w["ff_conv"][1], w["ff_conv_b"][1:2], w["ff_w_down1"])
    h4, ffn1 = _ffn_fwd("f1", *f1, tm)

    loss_blk, dh, d_norm_final = _final_loss("final", h4, w["norm_final"], tgt_p, tm)
    grads["norm_final"] = d_norm_final

    dh, gf1 = _ffn_bwd("f1", *f1, ffn1, dh, tm)
    zero = emit("f1", {"ff_w_up1": gf1["w_up"], "ff_w_down1": gf1["w_down"].reshape(N_CHIPS, D_FF // N_CHIPS, D_MODEL)})
    prm = dict(prm, lnx_g=prm["lnx_g"] + zero)
    dy1 = _mm_nt_full("l1_out_dx", dh, w["od_w_out"], 0, tm, D_MODEL)
    grads["od_w_out"] = _mm_tn_full("l1_out_dw", y1, dh, tm, D_MODEL // 2)
    dy_scan, dr_p, dk2_p, dv_p, dgate_r, grads["od_lnx_g"], grads["od_lnx_b"], d_rk = _rwkv_post_bwd(
        "l1_rwkv_post_bwd", y_scan, prs, k2, gate_r, prm, seg, dy1, 1, tr)
    grads["od_r_k"] = d_rk.reshape(N_R_HEADS, HEAD_DIM)
    dr_s, dlw, dk2_s, dv_s, da_, db_ = _wkv_bwd("l1_wkv_bwd", scan_in, states, (dy_scan, 0))
    dk, dxl, dgd, grads["od_w0"], dw2p, grads["od_a0"], da2p, grads["od_g2"], grads["od_k_k"], grads["od_k_a"] = (
        _rwkv_pre_bwd("l1_rwkv_pre_bwd", prs, prm, seg, (dlw, dk2_s + dk2_p, da_, db_, dgate_r), tr))
    grads["od_w2"] = dw2p[:LORA_W]
    grads["od_a2"] = da2p[LORA_W:]
    dprs = jnp.concatenate([dr_s + dr_p, dk, dv_s + dv_p, dxl, dgd], axis=1)
    dpr, grads["od_mu"] = _shift_bwd("l1_shift_bwd", p1, col0, w["od_mu"], dprs)
    doh = _to_heads(dy1[:, :D_ATT], N_Q_HEADS, pad)
    dqh, dkp, dkc, dvp, dvc, dkm, dvm, dsinks = _attn_bwd("l1_attn_bwd", qh, kh, vh, sinks_b, cos, sin, rot, doh)
    grads["od_sinks"] = dsinks[:, 0, 0].reshape(1, N_Q_HEADS)
    dkh = _kv_combine("l1_attn_dk", dkp, dkc, dkm)
    dvh = _kv_combine("l1_attn_dv", dvp, dvc, dvm)
    dp1 = jnp.concatenate([_from_heads(dqh, pad), _from_heads(dkh, pad), _from_heads(dvh, pad), dpr], axis=1).astype(MXU_DTYPE)
    grads["od_w_in"] = _mm_tn_cs("l1_in_dw", hn2, dp1, N_CHIPS, tm)
    dhn2 = _mm_nt_cs("l1_in_dx", dp1, w["od_w_in"], 0, tm)
    dh, d_mix1 = _rms_bwd("l1_norm_bwd", h2, w["norm_mix"][1:2], dhn2, dh, tm)

    zero = emit("od", {"od_w_out": grads["od_w_out"].reshape(N_CHIPS, D_MODEL // N_CHIPS, D_MODEL), "od_w_in": grads["od_w_in"]})
    f0 = f0[:5] + (f0[5] + zero,) + f0[6:]
    dh, gf0 = _ffn_bwd("f0", *f0, ffn0, dh, tm)
    zero = emit("f0", {"ff_w_up0": gf0["w_up"], "ff_w_down0": gf0["w_down"].reshape(N_CHIPS, D_FF // N_CHIPS, D_MODEL)})
    w = dict(w, ev_ln_a_g=w["ev_ln_a_g"] + zero)
    dy0 = _mm_nt_full("l0_out_dx", dh, w["ev_w_out"], 0, tm, D_MODEL)
    grads["ev_w_out"] = _mm_tn_full("l0_out_dw", y0, dh, tm, D_MODEL // 2)
    duc, grads["ev_ln_a_g"], grads["ev_ln_a_b"] = _even_ln_bwd("l0_ln_bwd", uc, w["ev_ln_a_g"], w["ev_ln_a_b"], dy0, 0, tm)
    *dparts, grads["ev_conv_a"], grads["ev_conv_b"] = _even_col_bwd("l0_convs_bwd", p0, duc, dy0, w["ev_conv_a"], w["ev_conv_b"])
    dp0 = jnp.concatenate(dparts, axis=1)
    grads["ev_w_in"] = _mm_tn_cs("l0_in_dw", hn0, dp0, N_CHIPS, tm)
    dhn0 = _mm_nt_cs("l0_in_dx", dp0, w["ev_w_in"], 0, tm)
    dh, d_mix0 = _rms_bwd("l0_norm_bwd", h0, w["norm_mix"][0:1], dhn0, dh, tm)

    grads["norm_mix"] = jnp.concatenate([d_mix0, d_mix1], axis=0)
    grads["norm_ffn"] = jnp.concatenate([gf0["norm"], gf1["norm"]], axis=0)
    grads["ff_w_up"] = [gf0["w_up"], gf1["w_up"]]
    grads["ff_conv"] = jnp.stack([gf0["conv"], gf1["conv"]])
    grads["ff_conv_b"] = jnp.concatenate([gf0["bias"], gf1["bias"]], axis=0)
    grads["ff_w_down"] = [gf0["w_down"], gf1["w_down"]]
    grads["meta_tokens"] = dh[:N_META]
    return loss_blk[0, 0], dh[N_META:], grads


SHARD_AXIS = {
    "meta_tokens": 1, "norm_mix": None, "norm_ffn": None, "norm_final": None,
    "ev_w_in": 2, "ev_conv_a": 2, "ev_ln_a_g": None, "ev_ln_a_b": None, "ev_conv_b": 2, "ev_w_out": 1,
    "od_w_in": 2, "od_sinks": None, "od_mu": 1, "od_w0": 1, "od_w2": 2, "od_a0": 1, "od_a2": 2, "od_g2": 2,
    "od_k_k": 1, "od_k_a": 1, "od_r_k": None, "od_lnx_g": 1, "od_lnx_b": 1, "od_w_out": 1,
    "ff_w_up": 2, "ff_conv": 2, "ff_conv_b": None, "ff_w_down": 1,
}
WEIGHTS = list(SHARD_AXIS)
BIG = ("ev_w_in", "ev_w_out", "od_w_in", "od_w_out", "ff_w_up", "ff_w_down")
SHARDED = [n for n in WEIGHTS if SHARD_AXIS[n] is not None]
SMALL = [n for n in SHARDED if n not in BIG]
REPLICATED = [n for n in WEIGHTS if SHARD_AXIS[n] is None]


def _join(g, axis):
    return jnp.concatenate([g[k] for k in range(N_CHIPS)], axis=axis)


def _split(full, axis):
    return jnp.stack(jnp.split(full, N_CHIPS, axis=axis))


def _full_weights(gathered, repl):
    w = {}
    sq = lambda a: a.reshape(a.shape[1:]) if a.shape[0] == 1 else a
    for n in REPLICATED:
        w[n] = repl[n]
    w["norm_final"] = repl["norm_final"].reshape(1, D_MODEL)
    for n in ("ev_ln_a_g", "ev_ln_a_b"):
        w[n] = repl[n].reshape(1, D_A)
    w["od_r_k"] = repl["od_r_k"][0]
    w["meta_tokens"] = _join(gathered["meta_tokens"], 1)
    for n in ("ev_conv_a", "ev_conv_b", "od_w2", "od_a2", "od_g2"):
        w[n] = sq(_join(gathered[n], 2))
    for n in ("od_mu", "od_w0", "od_a0", "od_k_k", "od_k_a", "od_lnx_g", "od_lnx_b"):
        w[n] = _join(gathered[n], 1)
    w["ff_conv"] = _join(gathered["ff_conv"], 2)
    return w


def _shard_grads(grads):
    out = {}
    for n in REPLICATED:
        out[n] = grads[n]
    out["norm_final"] = grads["norm_final"].reshape(D_MODEL)
    out["od_r_k"] = grads["od_r_k"][None]
    out["meta_tokens"] = _split(grads["meta_tokens"], 1)
    for n in ("ev_conv_a", "ev_conv_b", "od_w2", "od_a2", "od_g2"):
        out[n] = _split(grads[n][None], 2)
    for n in ("od_mu", "od_w0", "od_a0", "od_k_k", "od_k_a", "od_lnx_g", "od_lnx_b"):
        out[n] = _split(grads[n], 1)
    out["ff_conv"] = _split(grads["ff_conv"], 2)
    return out


def kernel(x, meta_tokens, norm_mix, norm_ffn, norm_final, ev_w_in, ev_conv_a, ev_ln_a_g, ev_ln_a_b, ev_conv_b, ev_w_out, od_w_in, od_sinks, od_mu, od_w0, od_w2, od_a0, od_a2, od_g2, od_k_k, od_k_a, od_r_k, od_lnx_g, od_lnx_b, od_w_out, ff_w_up, ff_conv, ff_conv_b, ff_w_down, loss_target, m_meta_tokens, m_norm_mix, m_norm_ffn, m_norm_final, m_ev_w_in, m_ev_conv_a, m_ev_ln_a_g, m_ev_ln_a_b, m_ev_conv_b, m_ev_w_out, m_od_w_in, m_od_sinks, m_od_mu, m_od_w0, m_od_w2, m_od_a0, m_od_a2, m_od_g2, m_od_k_k, m_od_k_a, m_od_r_k, m_od_lnx_g, m_od_lnx_b, m_od_w_out, m_ff_w_up, m_ff_conv, m_ff_conv_b, m_ff_w_down, v_meta_tokens, v_norm_mix, v_norm_ffn, v_norm_final, v_ev_w_in, v_ev_conv_a, v_ev_ln_a_g, v_ev_ln_a_b, v_ev_conv_b, v_ev_w_out, v_od_w_in, v_od_sinks, v_od_mu, v_od_w0, v_od_w2, v_od_a0, v_od_a2, v_od_g2, v_od_k_k, v_od_k_a, v_od_r_k, v_od_lnx_g, v_od_lnx_b, v_od_w_out, v_ff_w_up, v_ff_conv, v_ff_conv_b, v_ff_w_down):
    wts = dict(meta_tokens=meta_tokens, norm_mix=norm_mix, norm_ffn=norm_ffn, norm_final=norm_final, ev_w_in=ev_w_in, ev_conv_a=ev_conv_a, ev_ln_a_g=ev_ln_a_g, ev_ln_a_b=ev_ln_a_b, ev_conv_b=ev_conv_b, ev_w_out=ev_w_out, od_w_in=od_w_in, od_sinks=od_sinks, od_mu=od_mu, od_w0=od_w0, od_w2=od_w2, od_a0=od_a0, od_a2=od_a2, od_g2=od_g2, od_k_k=od_k_k, od_k_a=od_k_a, od_r_k=od_r_k, od_lnx_g=od_lnx_g, od_lnx_b=od_lnx_b, od_w_out=od_w_out, ff_w_up=ff_w_up, ff_conv=ff_conv, ff_conv_b=ff_conv_b, ff_w_down=ff_w_down)
    mom = dict(meta_tokens=m_meta_tokens, norm_mix=m_norm_mix, norm_ffn=m_norm_ffn, norm_final=m_norm_final, ev_w_in=m_ev_w_in, ev_conv_a=m_ev_conv_a, ev_ln_a_g=m_ev_ln_a_g, ev_ln_a_b=m_ev_ln_a_b, ev_conv_b=m_ev_conv_b, ev_w_out=m_ev_w_out, od_w_in=m_od_w_in, od_sinks=m_od_sinks, od_mu=m_od_mu, od_w0=m_od_w0, od_w2=m_od_w2, od_a0=m_od_a0, od_a2=m_od_a2, od_g2=m_od_g2, od_k_k=m_od_k_k, od_k_a=m_od_k_a, od_r_k=m_od_r_k, od_lnx_g=m_od_lnx_g, od_lnx_b=m_od_lnx_b, od_w_out=m_od_w_out, ff_w_up=m_ff_w_up, ff_conv=m_ff_conv, ff_conv_b=m_ff_conv_b, ff_w_down=m_ff_w_down)
    var = dict(meta_tokens=v_meta_tokens, norm_mix=v_norm_mix, norm_ffn=v_norm_ffn, norm_final=v_norm_final, ev_w_in=v_ev_w_in, ev_conv_a=v_ev_conv_a, ev_ln_a_g=v_ev_ln_a_g, ev_ln_a_b=v_ev_ln_a_b, ev_conv_b=v_ev_conv_b, ev_w_out=v_ev_w_out, od_w_in=v_od_w_in, od_sinks=v_od_sinks, od_mu=v_od_mu, od_w0=v_od_w0, od_w2=v_od_w2, od_a0=v_od_a0, od_a2=v_od_a2, od_g2=v_od_g2, od_k_k=v_od_k_k, od_k_a=v_od_k_a, od_r_k=v_od_r_k, od_lnx_g=v_od_lnx_g, od_lnx_b=v_od_lnx_b, od_w_out=v_od_w_out, ff_w_up=v_ff_w_up, ff_conv=v_ff_conv, ff_conv_b=v_ff_conv_b, ff_w_down=v_ff_w_down)

    me_idx = (2 * lax.axis_index("x") + lax.axis_index("y")).astype(jnp.int32).reshape(1)
    c_idx = lax.axis_index("c").astype(jnp.int32).reshape(1)
    small_mine = _pack([wts[n] for n in SMALL], F32, 2 * 8)
    sources = {"ev_w_in": (ev_w_in, 0), "small": (small_mine[None], 0), "ev_w_out": (ev_w_out, 0),
               "ff_w_up0": (ff_w_up, 0), "ff_w_down0": (ff_w_down, 0), "od_w_in": (od_w_in, 0), "od_w_out": (od_w_out, 0),
               "ff_w_up1": (ff_w_up, 1), "ff_w_down1": (ff_w_down, 1)}
    bufs = {n: _place_own_block("place_" + n, a, l, me_idx, F32 if n == "small" else MXU_DTYPE)
            for n, (a, l) in sources.items()}

    def as_used(n, g):
        if n in ("ev_w_out", "od_w_out", "ff_w_down0", "ff_w_down1"):
            return g.reshape(1, -1, g.shape[-1])
        return g.reshape(N_CHIPS, 1, -1, g.shape[-1])

    first = dict(zip(("ev_w_in", "small"), _gather_weights("gather_first", [bufs["ev_w_in"], bufs["small"]])))
    gathered = dict(zip(SMALL, _unpack(first["small"].reshape(N_CHIPS, -1), [wts[n].shape for n in SMALL])))
    w_full = _full_weights(gathered, wts)
    w_full["ev_w_in"] = as_used("ev_w_in", first["ev_w_in"])
    groups = {"ev_out": ["ev_w_out"], "f0": ["ff_w_up0", "ff_w_down0"], "od": ["od_w_in", "od_w_out"],
              "f1": ["ff_w_up1", "ff_w_down1"]}
    started_gathers, token = _gather_start("gather_start", [[bufs[n] for n in g] for g in groups.values()])
    started_gathers = dict(zip(groups, started_gathers))
    w_full["norm_mix"] = w_full["norm_mix"] + token[0, 0]

    def fetch(tag, after):
        send_sems, recv_sems, group_bufs = started_gathers[tag]
        landed = _gather_wait("gather_wait_" + tag, send_sems, recv_sems, group_bufs, after)
        whole = _gather_weights("gather_siblings_" + tag, landed, from_chips=False)
        return {n: as_used(n, g) for n, g in zip(groups[tag], whole)}

    cm_idx = jnp.concatenate([c_idx, me_idx])
    started = []

    def start_reduction(tag, units):
        names = list(units)
        from_sibling = _halves_to_sibling(f"grads_to_sibling_{tag}", [units[n] for n in names])
        pairs = [_pair_add_placed(f"grads_pair_add_{n}", units[n], r, cm_idx, GRAD_WIRE_DTYPE) for n, r in zip(names, from_sibling)]
        send_sems, recv_sems, sums, zones, token = _scatter_start(
            f"grads_to_chips_start_{tag}", [p[0] for p in pairs], [p[1] for p in pairs])
        started.append((tag, names, send_sems, recv_sems, sums, zones))
        return token[0, 0]

    loss_local, grad_x, grads = _local_step(x[0], loss_target[0], w_full, start_reduction, fetch)
    loss = lax.psum(loss_local, ("x", "y", "c"))

    sg = _shard_grads(grads)
    small_rows = [jnp.concatenate([sg[n][k].reshape(-1) for n in SMALL] + [sg[n].reshape(-1) for n in REPLICATED])
                  for k in range(N_CHIPS)]
    n_el = small_rows[0].shape[0]
    n_rows = -(-n_el // (16 * PACK_W)) * 16
    small_unit = jnp.stack([jnp.pad(r, (0, n_rows * PACK_W - n_el)).reshape(n_rows, PACK_W) for r in small_rows])
    last = {"ev_w_out": grads["ev_w_out"].reshape(N_CHIPS, D_MODEL // N_CHIPS, D_MODEL), "ev_w_in": grads["ev_w_in"],
            "small": small_unit}
    from_sibling = _halves_to_sibling("grads_to_sibling_ev", list(last.values()))
    pairs = [_pair_add_placed(f"grads_pair_add_{n}", u, r, cm_idx, F32 if n == "small" else GRAD_WIRE_DTYPE)
             for (n, u), r in zip(last.items(), from_sibling)]
    ev_send, ev_recv, ev_sums, ev_zones, token = _scatter_start(
        "grads_to_chips_start_ev", [p[0] for p in pairs], [p[1] for p in pairs])
    dests = {"ev_w_in": ("ev_w_in", 0), "od_w_in": ("od_w_in", 0), "ev_w_out": ("ev_w_out", 0), "od_w_out": ("od_w_out", 0),
             "ff_w_up0": ("ff_w_up", 0), "ff_w_up1": ("ff_w_up", 1), "ff_w_down0": ("ff_w_down", 0),
             "ff_w_down1": ("ff_w_down", 1), "small": ("small", 0)}
    outs = {"grad": {}, "delta": {}, "new_m": {}, "new_v": {}}

    def finish(tag, from_chips, results):
        reduced = {}
        for n, part in from_chips.items():
            r, l = dests[n]
            reduced[r] = _sum_chips(f"grads_chip_sum_{n}", part, c_idx, l, 2 if r.startswith("ff_w") else 1,
                                    into=reduced.get(r))
        joined = dict(zip(results, _join_halves("grads_join_" + tag, [reduced[r] for r in results])))
        for n, g in joined.items():
            if n == "small":
                continue
            shape = wts[n].shape
            flat = lambda a: a.reshape(-1, shape[-1])
            new = _adamw("adamw_" + n, flat(wts[n]), flat(g), flat(mom[n]), flat(var[n]))
            for kind, arr in zip(("grad", "delta", "new_m", "new_v"), (g,) + tuple(new)):
                outs[kind][n] = arr.reshape(shape)
        return joined

    from_chips = {}
    for tag, names, send_sems, recv_sems, sums, zones in started:
        from_chips.update(zip(names, _scatter_wait(f"grads_to_chips_wait_{tag}", send_sems, recv_sems, sums, zones, token)))
    finish("layers", from_chips, ["od_w_in", "od_w_out", "ff_w_up", "ff_w_down"])
    from_chips = dict(zip(last, _scatter_wait("grads_to_chips_wait_ev", ev_send, ev_recv, ev_sums, ev_zones,
                                              outs["delta"]["ff_w_up"])))
    joined = finish("ev", from_chips, ["ev_w_in", "ev_w_out", "small"])

    order = SMALL + REPLICATED
    packed = lambda d: jnp.pad(jnp.concatenate([d[n].reshape(-1) for n in order]),
                               (0, n_rows * PACK_W - n_el)).reshape(n_rows, PACK_W)
    g_small = joined["small"].reshape(n_rows, PACK_W)
    new = _adamw("adamw_small", packed(wts), g_small, packed(mom), packed(var))
    for tag, arr in zip(("grad", "delta", "new_m", "new_v"), (g_small,) + tuple(new)):
        outs[tag].update(zip(order, _unpack(arr.reshape(-1), [wts[n].shape for n in order])))
    return (loss, grad_x[None], *[outs["grad"][n] for n in WEIGHTS], *[outs["delta"][n] for n in WEIGHTS],
            *[outs["new_m"][n] for n in WEIGHTS], *[outs["new_v"][n] for n in WEIGHTS])
```

```python
import functools

import jax
import jax.numpy as jnp
from jax import lax
from jax.experimental import pallas as pl
from jax.experimental.pallas import tpu as pltpu

F32 = jnp.float32
BF16 = jnp.bfloat16
HI = lax.Precision.HIGHEST
MXU_DTYPE = BF16
GRAD_WIRE_DTYPE = BF16

D_MODEL = 1024
N_META = 16
RMS_EPS = 1e-6
LN_EPS = 1e-5
D_A = 512
CONV_A_WIDTH = 31
CONV_B_WIDTH = 3
HEAD_DIM = 64
N_Q_HEADS = 8
N_KV_HEADS = 2
GQA_GROUP = 4
D_ATT = 512
D_KV = 128
BLOCK = 128
ROPE_THETA = 10000.0
D_R = 512
N_R_HEADS = 8
LORA_W = 64
LORA_A = 64
LORA_G = 128
RWKV_GN_EPS = 64e-5
RWKV_COLS = 3 * D_R + LORA_W + LORA_A + LORA_G
D_FF = 2816
NEG_INF = -1e30
ADAM_LR = 0.001
ADAM_B1 = 0.9
ADAM_B2 = 0.999
ADAM_EPS = 1e-08
ADAM_WD = 0.01
ADAM_STEP = 10

N_CHIPS = 4
LANES = 128
CONV_PAD = 32
VMEM_LIMIT_V7X = 56 * 1024 * 1024
MESH = pl.DeviceIdType.MESH


def _cparams(sem=None):
    return pltpu.CompilerParams(dimension_semantics=sem, vmem_limit_bytes=VMEM_LIMIT_V7X)


def _row_tile(t, cap):
    for d in range(min(t, cap), 0, -1):
        if t % d == 0 and d % 16 == 0:
            return d
    return t


def _chunk_len(t):
    for d in (64, 48, 32, 16, 8):
        if t % d == 0:
            return d
    raise ValueError(t)


def _call(fn, name, grid, ins, outs, acc_axis=None, sem=None):
    n_in, n_out = len(ins), len(outs)
    dtype = lambda o: o[4] if len(o) > 4 else F32

    def body(*refs):
        vals = fn(*[r[...] for r in refs[:n_in]])
        if not isinstance(vals, (tuple, list)):
            vals = (vals,)
        for r, v, o in zip(refs[n_in:n_in + n_out], vals, outs):
            if o[3]:
                first = pl.program_id(acc_axis) == 0

                @pl.when(first)
                def _(r=r, v=v):
                    r[...] = v

                @pl.when(jnp.logical_not(first))
                def _(r=r, v=v):
                    r[...] += v
            else:
                r[...] = v.astype(dtype(o))

    res = pl.pallas_call(
        body, name=name, grid=grid,
        in_specs=[pl.BlockSpec(b, m) for _, b, m in ins],
        out_specs=[pl.BlockSpec(o[1], o[2]) for o in outs],
        out_shape=[jax.ShapeDtypeStruct(o[0], dtype(o)) for o in outs],
        compiler_params=_cparams(sem),
    )(*[a for a, _, _ in ins])
    return res if n_out > 1 else res[0]


def _matmul(name, a, b, *, dims, grid, a_spec, b_spec, o_shape, o_spec, acc_shape, nk, k_axis,
            add=None, add_spec=None):
    def product(a_ref, b_ref):
        return lax.dot_general(a_ref[...].astype(MXU_DTYPE), b_ref[...].astype(MXU_DTYPE), dims, preferred_element_type=F32)

    def body_single(*refs):
        a_ref, b_ref, o_ref = refs[0], refs[1], refs[-1]
        o_ref[...] = product(a_ref, b_ref) if add is None else product(a_ref, b_ref) + refs[2][...]

    def body_steps(*refs):
        a_ref, b_ref, o_ref, acc = refs[0], refs[1], refs[-2], refs[-1]
        k = pl.program_id(k_axis)

        @pl.when(k == 0)
        def _():
            if add is None:
                acc[...] = jnp.zeros(acc.shape, F32)
            else:
                acc[...] = refs[2][...]

        acc[...] += product(a_ref, b_ref)

        @pl.when(k == nk - 1)
        def _():
            o_ref[...] = acc[...]

    args = [a, b] + ([] if add is None else [add])
    specs = [a_spec, b_spec] + ([] if add is None else [add_spec])
    return pl.pallas_call(
        body_single if nk == 1 else body_steps, name=name, grid=grid, in_specs=specs, out_specs=o_spec,
        out_shape=jax.ShapeDtypeStruct(o_shape, F32),
        scratch_shapes=[] if nk == 1 else [pltpu.VMEM(acc_shape, F32)],
        compiler_params=_cparams(None),
    )(*args)


MATMUL_BLOCKS_BYTES = 46 * 1024 * 1024


def _whole_if_fits(t, tile, need_bytes):
    return t if need_bytes <= MATMUL_BLOCKS_BYTES else tile


_NN = (((1,), (0,)), ((), ()))
_NT = (((1,), (1,)), ((), ()))
_TN = (((0,), (0,)), ((), ()))


def _mm_cs(name, x, wg, l, tm):
    t, k = x.shape
    s, _, _, n = wg.shape
    tm = _whole_if_fits(t, tm, 2 * (t * k * x.dtype.itemsize + k * n * wg.dtype.itemsize + t * n * 4))
    return _matmul(name, x, wg, dims=_NN, grid=(s, t // tm, 1),
                   a_spec=pl.BlockSpec((tm, k), lambda j, i, kk: (i, 0)),
                   b_spec=pl.BlockSpec((None, None, k, n), lambda j, i, kk: (j, l, 0, 0)),
                   o_shape=(t, s * n), o_spec=pl.BlockSpec((tm, n), lambda j, i, kk: (i, j)),
                   acc_shape=(tm, n), nk=1, k_axis=2)


def _mm_full(name, x, w, l, tm, tk, add=None):
    t, k = x.shape
    n = w.shape[2]
    nk = k // tk
    tm = _whole_if_fits(t, tm, 2 * (t * tk * x.dtype.itemsize + tk * n * w.dtype.itemsize + t * n * 4 * (1 if add is None else 2))
                        + (t * n * 4 if nk > 1 else 0))
    return _matmul(name, x, w, dims=_NN, grid=(t // tm, 1, nk),
                   a_spec=pl.BlockSpec((tm, tk), lambda i, j, kk: (i, kk)),
                   b_spec=pl.BlockSpec((None, tk, n), lambda i, j, kk: (l, kk, 0)),
                   o_shape=(t, n), o_spec=pl.BlockSpec((tm, n), lambda i, j, kk: (i, 0)),
                   acc_shape=(tm, n), nk=nk, k_axis=2,
                   add=add, add_spec=pl.BlockSpec((tm, n), lambda i, j, kk: (i, 0)))


def _mm_nt_cs(name, dy, wg, l, tm, add=None):
    t = dy.shape[0]
    s, _, k, n = wg.shape
    tm = _whole_if_fits(t, tm, 2 * (t * n * dy.dtype.itemsize + k * n * wg.dtype.itemsize + t * k * 4 * (1 if add is None else 2))
                        + t * k * 4)
    return _matmul(name, dy, wg, dims=_NT, grid=(t // tm, 1, s),
                   a_spec=pl.BlockSpec((tm, n), lambda i, j, kk: (i, kk)),
                   b_spec=pl.BlockSpec((None, None, k, n), lambda i, j, kk: (kk, l, 0, 0)),
                   o_shape=(t, k), o_spec=pl.BlockSpec((tm, k), lambda i, j, kk: (i, 0)),
                   acc_shape=(tm, k), nk=s, k_axis=2,
                   add=add, add_spec=pl.BlockSpec((tm, k), lambda i, j, kk: (i, 0)))


def _mm_nt_full(name, dy, w, l, tm, tko):
    t, n = dy.shape
    k = w.shape[1]
    tm = _whole_if_fits(t, tm, 2 * (t * n * dy.dtype.itemsize + tko * n * w.dtype.itemsize + t * tko * 4))
    return _matmul(name, dy, w, dims=_NT, grid=(t // tm, k // tko, 1),
                   a_spec=pl.BlockSpec((tm, n), lambda i, j, kk: (i, 0)),
                   b_spec=pl.BlockSpec((None, tko, n), lambda i, j, kk: (l, j, 0)),
                   o_shape=(t, k), o_spec=pl.BlockSpec((tm, tko), lambda i, j, kk: (i, j)),
                   acc_shape=(tm, tko), nk=1, k_axis=2)


def _mm_tn_cs(name, x, dy, s, tk):
    t, k = x.shape
    n = dy.shape[1] // s
    tk = _whole_if_fits(t, tk, 2 * (t * k * x.dtype.itemsize + t * n * dy.dtype.itemsize + k * n * 4))
    nk = t // tk
    return _matmul(name, x, dy, dims=_TN, grid=(s, 1, nk),
                   a_spec=pl.BlockSpec((tk, k), lambda j, i, kk: (kk, 0)),
                   b_spec=pl.BlockSpec((tk, n), lambda j, i, kk: (kk, j)),
                   o_shape=(s, k, n), o_spec=pl.BlockSpec((None, k, n), lambda j, i, kk: (j, 0, 0)),
                   acc_shape=(k, n), nk=nk, k_axis=2)


def _mm_tn_full(name, y, dh, tk, tko):
    t, k = y.shape
    n = dh.shape[1]
    tk = _whole_if_fits(t, tk, 2 * (t * tko * y.dtype.itemsize + t * n * dh.dtype.itemsize + tko * n * 4))
    nk = t // tk
    return _matmul(name, y, dh, dims=_TN, grid=(k // tko, 1, nk),
                   a_spec=pl.BlockSpec((tk, tko), lambda j, i, kk: (kk, j)),
                   b_spec=pl.BlockSpec((tk, n), lambda j, i, kk: (kk, 0)),
                   o_shape=(k, n), o_spec=pl.BlockSpec((tko, n), lambda j, i, kk: (j, 0)),
                   acc_shape=(tko, n), nk=nk, k_axis=2)


def _sigmoid(x):
    return 1.0 / (1.0 + jnp.exp(-x))


def _rms_fwd(name, h, g, tr):
    t, d = h.shape

    def fn(hv, gv):
        r = lax.rsqrt(jnp.mean(hv * hv, axis=-1, keepdims=True) + RMS_EPS)
        return hv * r * gv

    return _call(fn, name, (t // tr,), [(h, (tr, d), lambda i: (i, 0)), (g, (1, d), lambda i: (0, 0))],
                 [((t, d), (tr, d), lambda i: (i, 0), False, MXU_DTYPE)])


def _rms_bwd(name, h, g, dhn, dh, tr):
    t, d = h.shape

    def fn(hv, gv, dy, dh_in):
        r = lax.rsqrt(jnp.mean(hv * hv, axis=-1, keepdims=True) + RMS_EPS)
        xh = hv * r
        dg = jnp.sum(dy * xh, axis=0, keepdims=True)
        dxh = dy * gv
        dx = r * (dxh - xh * jnp.mean(dxh * xh, axis=-1, keepdims=True))
        return dh_in + dx, dg

    row = lambda i: (i, 0)
    return _call(fn, name, (t // tr,),
                 [(h, (tr, d), row), (g, (1, d), lambda i: (0, 0)), (dhn, (tr, d), row), (dh, (tr, d), row)],
                 [((t, d), (tr, d), row, False), ((1, d), (1, d), lambda i: (0, 0), True)], acc_axis=0)


def _final_loss(name, h, g, tgt, tr):
    t, d = h.shape

    def fn(hv, gv, tv):
        r = lax.rsqrt(jnp.mean(hv * hv, axis=-1, keepdims=True) + RMS_EPS)
        xh = hv * r
        row = pl.program_id(0) * tr + lax.broadcasted_iota(jnp.int32, (tr, 1), 0)
        e = jnp.where(row >= N_META, xh * gv - tv, 0.0)
        loss = jnp.broadcast_to(0.5 * jnp.sum(jnp.sum(e * e, axis=-1, keepdims=True), axis=0, keepdims=True) / d,
                                (8, LANES))
        dout = e / d
        dg = jnp.sum(dout * xh, axis=0, keepdims=True)
        dxh = dout * gv
        dx = r * (dxh - xh * jnp.mean(dxh * xh, axis=-1, keepdims=True))
        return loss, dx, dg

    row = lambda i: (i, 0)
    fix = lambda i: (0, 0)
    return _call(fn, name, (t // tr,), [(h, (tr, d), row), (g, (1, d), fix), (tgt, (tr, d), row)],
                 [((8, LANES), (8, LANES), fix, True), ((t, d), (tr, d), row, False), ((1, d), (1, d), fix, True)],
                 acc_axis=0)


def _silu_ln(uc, g, b):
    mu = jnp.mean(uc, axis=-1, keepdims=True)
    xc = uc - mu
    rs = lax.rsqrt(jnp.mean(xc * xc, axis=-1, keepdims=True) + LN_EPS)
    ln = xc * rs * g + b
    return ln * _sigmoid(ln)


def _even_ln_fwd(name, uc, g, b, tr):
    t, d = uc.shape
    row, fix = (lambda i: (i, 0)), (lambda i: (0, 0))
    return _call(_silu_ln, name, (t // tr,), [(uc, (tr, d), row), (g, (1, d), fix), (b, (1, d), fix)],
                 [((t, d), (tr, d), row, False, MXU_DTYPE)])


def _even_ln_bwd(name, uc, g, b, dy, dy_col, tr):
    t, d = uc.shape

    def fn(ucv, gv, bv, dyv):
        mu = jnp.mean(ucv, axis=-1, keepdims=True)
        xc = ucv - mu
        rs = lax.rsqrt(jnp.mean(xc * xc, axis=-1, keepdims=True) + LN_EPS)
        xh = xc * rs
        ln = xh * gv + bv
        s = _sigmoid(ln)
        dln = dyv * (s * (1.0 + ln * (1.0 - s)))
        dg = jnp.sum(dln * xh, axis=0, keepdims=True)
        db = jnp.sum(dln, axis=0, keepdims=True)
        dxh = dln * gv
        duc = rs * (dxh - jnp.mean(dxh, axis=-1, keepdims=True) - xh * jnp.mean(dxh * xh, axis=-1, keepdims=True))
        return duc, dg, db

    row, fix = (lambda i: (i, 0)), (lambda i: (0, 0))
    return _call(fn, name, (t // tr,),
                 [(uc, (tr, d), row), (g, (1, d), fix), (b, (1, d), fix), (dy, (tr, d), lambda i: (i, dy_col))],
                 [((t, d), (tr, d), row, False), ((1, d), (1, d), fix, True), ((1, d), (1, d), fix, True)], acc_axis=0)


def _windows(t):
    rc = _chunk_len(t)
    return [(r0, rc) for r0 in range(0, t, rc)]


def _taps(w_ref, width):
    return [w_ref[pl.ds(j, 1), :] for j in range(width)]


def _conv_at(xp, taps, r0, rc):
    width = len(taps)
    acc = None
    for j in range(width):
        term = xp[pl.ds(CONV_PAD - (width - 1) + j + r0, rc), :] * taps[j]
        acc = term if acc is None else acc + term
    return acc


def _conv_bwd_in_at(dyp, taps, r0, rc):
    width = len(taps)
    acc = None
    for j in range(width):
        term = dyp[pl.ds(width - 1 - j + r0, rc), :] * taps[j]
        acc = term if acc is None else acc + term
    return acc


def _fold(x):
    acc = x[0:8]
    for i in range(1, x.shape[0] // 8):
        acc = acc + x[8 * i:8 * (i + 1)]
    return acc


def _add_to(accs, vals):
    return vals if accs is None else [a + v for a, v in zip(accs, vals)]


def _conv_bwd_w_at(dy, xp, width, r0, rc):
    return [_fold(dy * xp[pl.ds(CONV_PAD - (width - 1) + j + r0, rc), :]) for j in range(width)]


def _store_taps(dw_ref, accs):
    for j, a in enumerate(accs):
        dw_ref[pl.ds(j, 1), :] = jnp.sum(a, axis=0, keepdims=True)


WIDE_COLS = 2 * LANES


def _zero_front(xp):
    xp[pl.ds(0, CONV_PAD), :] = jnp.zeros((CONV_PAD, xp.shape[1]), F32)


def _zero_back(dyp, t):
    dyp[pl.ds(t, CONV_PAD), :] = jnp.zeros((CONV_PAD, dyp.shape[1]), F32)


def _col_call(body, name, ncol, ins, outs, t, n_scratch, cols=LANES):
    def spec(rows, off):
        return pl.BlockSpec((rows, cols), lambda j, off=off: (0, j + off))

    res = pl.pallas_call(
        body, name=name, grid=(ncol,),
        in_specs=[spec(r, off) for _, r, off in ins],
        out_specs=[spec(o[0], 0) for o in outs],
        out_shape=[jax.ShapeDtypeStruct(o[:2], o[2] if len(o) > 2 else F32) for o in outs],
        scratch_shapes=[pltpu.VMEM((t + CONV_PAD, cols), F32) for _ in range(n_scratch)],
        compiler_params=_cparams(None),
    )(*[a for a, _, _ in ins])
    return res


def _even_col_fwd(name, p, conv_a, conv_b):
    t = p.shape[0]
    nc = D_A // LANES
    wins = _windows(t)

    def body(av, ag, gb, gc, xi, ca, cb, uc_ref, yb_ref, xp):
        _zero_front(xp)
        for r0, rc in wins:
            rows = pl.ds(r0, rc)
            xp[pl.ds(CONV_PAD + r0, rc), :] = av[rows, :] * _sigmoid(ag[rows, :])
        taps = _taps(ca, CONV_A_WIDTH)
        for r0, rc in wins:
            uc_ref[pl.ds(r0, rc), :] = _conv_at(xp, taps, r0, rc)
        for r0, rc in wins:
            rows = pl.ds(r0, rc)
            xp[pl.ds(CONV_PAD + r0, rc), :] = gc[rows, :] * xi[rows, :]
        taps = _taps(cb, CONV_B_WIDTH)
        for r0, rc in wins:
            rows = pl.ds(r0, rc)
            yb_ref[rows, :] = (gb[rows, :] * _conv_at(xp, taps, r0, rc)).astype(yb_ref.dtype)

    ins = [(p, t, k * nc) for k in range(5)] + [(conv_a, CONV_A_WIDTH, 0), (conv_b, CONV_B_WIDTH, 0)]
    return _col_call(body, name, nc, ins, [(t, D_A), (t, D_A, MXU_DTYPE)], t, 1)


def _even_col_bwd(name, p, duc, dy, conv_a, conv_b):
    t = p.shape[0]
    nc = D_A // LANES
    wins = _windows(t)

    def body(av, ag, gb, gc, xi, duc_ref, dyb_ref, ca, cb, dav, dag, dgb, dgc, dxi, dca, dcb, xp, dyp):
        _zero_front(xp)
        _zero_back(dyp, t)
        for r0, rc in wins:
            rows = pl.ds(r0, rc)
            xp[pl.ds(CONV_PAD + r0, rc), :] = av[rows, :] * _sigmoid(ag[rows, :])
            dyp[rows, :] = duc_ref[rows, :]
        taps = _taps(ca, CONV_A_WIDTH)
        accs = None
        for r0, rc in wins:
            rows = pl.ds(r0, rc)
            accs = _add_to(accs, _conv_bwd_w_at(duc_ref[rows, :], xp, CONV_A_WIDTH, r0, rc))
            du = _conv_bwd_in_at(dyp, taps, r0, rc)
            sig = _sigmoid(ag[rows, :])
            dav[rows, :] = (du * sig).astype(dav.dtype)
            dag[rows, :] = (du * av[rows, :] * sig * (1.0 - sig)).astype(dag.dtype)
        _store_taps(dca, accs)
        for r0, rc in wins:
            rows = pl.ds(r0, rc)
            xp[pl.ds(CONV_PAD + r0, rc), :] = gc[rows, :] * xi[rows, :]
        taps = _taps(cb, CONV_B_WIDTH)
        accs = None
        for r0, rc in wins:
            rows = pl.ds(r0, rc)
            dgb[rows, :] = (dyb_ref[rows, :] * _conv_at(xp, taps, r0, rc)).astype(dgb.dtype)
            dzc = dyb_ref[rows, :] * gb[rows, :]
            dyp[rows, :] = dzc
            accs = _add_to(accs, _conv_bwd_w_at(dzc, xp, CONV_B_WIDTH, r0, rc))
        _store_taps(dcb, accs)
        for r0, rc in wins:
            rows = pl.ds(r0, rc)
            dz = _conv_bwd_in_at(dyp, taps, r0, rc)
            dgc[rows, :] = (dz * xi[rows, :]).astype(dgc.dtype)
            dxi[rows, :] = (dz * gc[rows, :]).astype(dxi.dtype)

    ins = ([(p, t, k * nc) for k in range(5)] + [(duc, t, 0), (dy, t, nc)]
           + [(conv_a, CONV_A_WIDTH, 0), (conv_b, CONV_B_WIDTH, 0)])
    outs = [(t, D_A, MXU_DTYPE)] * 5 + [(CONV_A_WIDTH, D_A), (CONV_B_WIDTH, D_A)]
    return _col_call(body, name, nc, ins, outs, t, 2)


def _ffn_col_fwd(name, u, conv, bias):
    t = u.shape[0]
    nc = D_FF // WIDE_COLS
    wins = _windows(t)

    def body(g_ref, v_ref, cw, b_ref, a_ref, xp):
        _zero_front(xp)
        xp[pl.ds(CONV_PAD, t), :] = g_ref[...]
        taps = _taps(cw, CONV_B_WIDTH)
        b = b_ref[...]
        for r0, rc in wins:
            rows = pl.ds(r0, rc)
            gc = _conv_at(xp, taps, r0, rc) + b
            a_ref[rows, :] = (gc * _sigmoid(gc) * v_ref[rows, :]).astype(a_ref.dtype)

    ins = [(u, t, 0), (u, t, nc), (conv, CONV_B_WIDTH, 0), (bias, 1, 0)]
    return _col_call(body, name, nc, ins, [(t, D_FF, MXU_DTYPE)], t, 1, cols=WIDE_COLS)[0]


def _ffn_col_bwd(name, u, da, conv, bias):
    t = u.shape[0]
    nc = D_FF // LANES
    wins = _windows(t)

    def body(g_ref, v_ref, da_ref, cw, b_ref, du_ref, dcw, db_ref, xp, dyp, dval):
        @pl.when(pl.program_id(1) == 0)
        def _():
            _zero_front(xp)
            _zero_back(dyp, t)
            xp[pl.ds(CONV_PAD, t), :] = g_ref[...]
            taps = _taps(cw, CONV_B_WIDTH)
            b = b_ref[...]
            accs, bias_acc = None, None
            for r0, rc in wins:
                rows = pl.ds(r0, rc)
                gc = _conv_at(xp, taps, r0, rc) + b
                s = _sigmoid(gc)
                d = da_ref[rows, :]
                dval[rows, :] = d * gc * s
                dgc = d * v_ref[rows, :] * (s * (1.0 + gc * (1.0 - s)))
                dyp[rows, :] = dgc
                bias_acc = _add_to(bias_acc, [_fold(dgc)])
                accs = _add_to(accs, _conv_bwd_w_at(dgc, xp, CONV_B_WIDTH, r0, rc))
            db_ref[...] = jnp.sum(bias_acc[0], axis=0, keepdims=True)
            _store_taps(dcw, accs)
            for r0, rc in wins:
                du_ref[pl.ds(r0, rc), :] = _conv_bwd_in_at(dyp, taps, r0, rc).astype(du_ref.dtype)

        @pl.when(pl.program_id(1) == 1)
        def _():
            du_ref[...] = dval[...].astype(du_ref.dtype)

    col = lambda rows, off: pl.BlockSpec((rows, LANES), lambda j, p: (0, j + off))
    return pl.pallas_call(
        body, name=name, grid=(nc, 2),
        in_specs=[col(t, 0), col(t, nc), col(t, 0), col(CONV_B_WIDTH, 0), col(1, 0)],
        out_specs=[pl.BlockSpec((t, LANES), lambda j, p: (0, j + nc * p)), col(CONV_B_WIDTH, 0), col(1, 0)],
        out_shape=[jax.ShapeDtypeStruct((t, 2 * D_FF), MXU_DTYPE), jax.ShapeDtypeStruct((CONV_B_WIDTH, D_FF), F32),
                   jax.ShapeDtypeStruct((1, D_FF), F32)],
        scratch_shapes=[pltpu.VMEM((t + CONV_PAD, LANES), F32) for _ in range(2)] + [pltpu.VMEM((t, LANES), F32)],
        compiler_params=_cparams(None),
    )(u, u, da, conv, bias)


def _shift_fwd(name, p, col0, mu):
    t = p.shape[0]
    wins = _windows(t)

    def body(x_ref, mu_ref, o_ref, xp):
        _zero_front(xp)
        xp[pl.ds(CONV_PAD, t), :] = x_ref[...]
        mu_v = mu_ref[...]
        for r0, rc in wins:
            rows = pl.ds(r0, rc)
            x = x_ref[rows, :]
            o_ref[rows, :] = x + (xp[pl.ds(CONV_PAD - 1 + r0, rc), :] - x) * mu_v

    return _col_call(body, name, RWKV_COLS // WIDE_COLS, [(p, t, col0 // WIDE_COLS), (mu, 1, 0)], [(t, RWKV_COLS)], t, 1,
                     cols=WIDE_COLS)[0]


def _shift_bwd(name, p, col0, mu, dprs):
    t = p.shape[0]
    wins = _windows(t)

    def body(x_ref, mu_ref, d_ref, dx_ref, dmu_ref, xp, dyp):
        _zero_front(xp)
        _zero_back(dyp, t)
        xp[pl.ds(CONV_PAD, t), :] = x_ref[...]
        mu_v = mu_ref[...]
        acc = None
        for r0, rc in wins:
            rows = pl.ds(r0, rc)
            d = d_ref[rows, :]
            acc = _add_to(acc, [_fold(d * (xp[pl.ds(CONV_PAD - 1 + r0, rc), :] - x_ref[rows, :]))])
            dyp[rows, :] = d * mu_v
        dmu_ref[...] = jnp.sum(acc[0], axis=0, keepdims=True)
        for r0, rc in wins:
            rows = pl.ds(r0, rc)
            dx_ref[rows, :] = d_ref[rows, :] - dyp[rows, :] + dyp[pl.ds(1 + r0, rc), :]

    ins = [(p, t, col0 // WIDE_COLS), (mu, 1, 0), (dprs, t, 0)]
    return _col_call(body, name, RWKV_COLS // WIDE_COLS, ins, [(t, RWKV_COLS), (1, RWKV_COLS)], t, 2, cols=WIDE_COLS)


def _hi_lo(x):
    hi = x.astype(BF16)
    return hi, (x - hi.astype(F32)).astype(BF16)


def _dot_passes(a, b, dims, passes):
    d = lambda p, q: lax.dot_general(p, q, dims, preferred_element_type=F32)
    if passes == 1:
        return d(a.astype(MXU_DTYPE), b.astype(MXU_DTYPE))
    ah, al = _hi_lo(a)
    bh, bl = _hi_lo(b)
    return d(ah, bh) + (d(ah, bl) + d(al, bh))


@functools.partial(jax.custom_vjp, nondiff_argnums=(2, 3))
def _dot_vjp(a, b, dims, passes):
    return _dot_passes(a, b, dims, passes)


def _dot_fwd(a, b, dims, passes):
    return _dot_passes(a, b, dims, passes), (a, b)


def _dot_bwd(dims, passes, res, g):
    a, b = res
    if dims == _NN:
        return _dot_passes(g, b, _NT, passes), _dot_passes(a, g, _TN, passes)
    if dims == _NT:
        return _dot_passes(g, b, _NN, passes), _dot_passes(g, a, _TN, passes)
    return _dot_passes(b, g, _NT, passes), _dot_passes(a, g, _NN, passes)


_dot_vjp.defvjp(_dot_fwd, _dot_bwd)


def _doth(a, b, dims=_NN):
    return _dot_vjp(a, b, dims, 3)


def _dotb(a, b, dims=_NN):
    return _dot_vjp(a, b, dims, 1)


def _softplus(x):
    return jnp.where(x > 0, x, 0.0) + jnp.log(1.0 + jnp.exp(jnp.where(x > 0, -x, x)))


def _rwkv_pre(k, xl, gd, w0, w2p, a0, a2p, g2, k_k, k_a, seg):
    z = w0 + _dotb(jnp.tanh(xl), w2p)
    lw = -jnp.exp(-_softplus(-z) - 0.5)
    alpha = _sigmoid(a0 + _dotb(xl, a2p))
    g = _dotb(_sigmoid(gd), g2)
    kk = k * k_k
    kk = kk / jnp.maximum(jnp.sqrt(_dotb(kk * kk, seg)), 1e-12)
    k2 = k * (1.0 + (alpha - 1.0) * k_a)
    return lw, k2, -kk, kk * alpha, g


def _rwkv_post(y, r, k2, v, g, lnx_g, lnx_b, r_k, seg):
    mean = _dotb(y, seg) * (1.0 / HEAD_DIM)
    yc = y - mean
    var = _dotb(yc * yc, seg) * (1.0 / HEAD_DIM)
    yo = yc * lax.rsqrt(var + RWKV_GN_EPS) * lnx_g + lnx_b
    bonus = _dotb(r * k2 * r_k, seg) * v
    return (yo + bonus) * g


def _rwkv_pre_fwd(name, prs, prm, seg, tr):
    t = prs.shape[0]
    row = lambda i: (i, 0)
    fix = lambda i: (0, 0)
    ins = [(prs, (tr, D_R), lambda i: (i, 1)), (prs, (tr, LANES), lambda i: (i, 12)), (prs, (tr, LANES), lambda i: (i, 13)),
           (prm["w0"], (1, D_R), fix), (prm["w2p"], (LANES, D_R), fix), (prm["a0"], (1, D_R), fix),
           (prm["a2p"], (LANES, D_R), fix), (prm["g2"], (LANES, D_R), fix), (prm["k_k"], (1, D_R), fix),
           (prm["k_a"], (1, D_R), fix), (seg, (D_R, D_R), fix)]
    return _call(_rwkv_pre, name, (t // tr,), ins, [((t, D_R), (tr, D_R), row, False)] * 5)


def _rwkv_pre_bwd(name, prs, prm, seg, cts, tr):
    t = prs.shape[0]

    def fn(k, xl, gd, w0, w2p, a0, a2p, g2, k_k, k_a, segv, *ct):
        _, vjp = jax.vjp(lambda *a: _rwkv_pre(*a, segv), k, xl, gd, w0, w2p, a0, a2p, g2, k_k, k_a)
        return vjp(tuple(ct))

    row = lambda i: (i, 0)
    fix = lambda i: (0, 0)
    ins = [(prs, (tr, D_R), lambda i: (i, 1)), (prs, (tr, LANES), lambda i: (i, 12)), (prs, (tr, LANES), lambda i: (i, 13)),
           (prm["w0"], (1, D_R), fix), (prm["w2p"], (LANES, D_R), fix), (prm["a0"], (1, D_R), fix),
           (prm["a2p"], (LANES, D_R), fix), (prm["g2"], (LANES, D_R), fix), (prm["k_k"], (1, D_R), fix),
           (prm["k_a"], (1, D_R), fix), (seg, (D_R, D_R), fix)] + [(c, (tr, D_R), row) for c in cts]
    outs = [((t, D_R), (tr, D_R), row, False), ((t, LANES), (tr, LANES), row, False), ((t, LANES), (tr, LANES), row, False),
            ((1, D_R), (1, D_R), fix, True), ((LANES, D_R), (LANES, D_R), fix, True), ((1, D_R), (1, D_R), fix, True),
            ((LANES, D_R), (LANES, D_R), fix, True), ((LANES, D_R), (LANES, D_R), fix, True),
            ((1, D_R), (1, D_R), fix, True), ((1, D_R), (1, D_R), fix, True)]
    return _call(fn, name, (t // tr,), ins, outs, acc_axis=0)


def _rwkv_post_ins(y, prs, k2, g, prm, seg, tr):
    row = lambda i: (i, 0)
    fix = lambda i: (0, 0)
    return [(y, (tr, D_R), row), (prs, (tr, D_R), row), (k2, (tr, D_R), row), (prs, (tr, D_R), lambda i: (i, 2)),
            (g, (tr, D_R), row), (prm["lnx_g"], (1, D_R), fix), (prm["lnx_b"], (1, D_R), fix), (prm["r_k"], (1, D_R), fix),
            (seg, (D_R, D_R), fix)]


def _rwkv_post_fwd(name, y, prs, k2, g, prm, seg, tr):
    t = y.shape[0]
    return _call(_rwkv_post, name, (t // tr,), _rwkv_post_ins(y, prs, k2, g, prm, seg, tr),
                 [((t, D_R), (tr, D_R), lambda i: (i, 0), False)])


def _rwkv_post_bwd(name, y, prs, k2, g, prm, seg, dy, dy_col, tr):
    t = y.shape[0]

    def fn(yv, r, k2v, v, gv, lg, lb, rk, segv, ct):
        _, vjp = jax.vjp(lambda *a: _rwkv_post(*a, segv), yv, r, k2v, v, gv, lg, lb, rk)
        return vjp(ct)

    row = lambda i: (i, 0)
    fix = lambda i: (0, 0)
    ins = _rwkv_post_ins(y, prs, k2, g, prm, seg, tr) + [(dy, (tr, D_R), lambda i: (i, dy_col))]
    outs = [((t, D_R), (tr, D_R), row, False)] * 5 + [((1, D_R), (1, D_R), fix, True)] * 3
    return _call(fn, name, (t // tr,), ins, outs, acc_axis=0)


def _wkv_chunk(s0, r, lw, k, v, a, b):
    c = r[0].shape[0]
    lane = lax.broadcasted_iota(jnp.int32, (1, 2 * HEAD_DIM), 1)
    first = (lane < HEAD_DIM).astype(F32)
    per_head = lambda x: jnp.concatenate([x * first, x * (1.0 - first)], axis=0)

    def time_of(shape, dim):
        i = lax.broadcasted_iota(jnp.int32, shape, dim)
        return jnp.where(i >= c, i - c, i)

    incl = (lax.broadcasted_iota(jnp.int32, (c, c), 0) >= lax.broadcasted_iota(jnp.int32, (c, c), 1)).astype(F32)
    strict2 = time_of((2 * c, 2 * c), 0) > time_of((2 * c, 2 * c), 1)
    incl2 = lax.broadcasted_iota(jnp.int32, (c, 2 * c), 0) >= time_of((c, 2 * c), 1)
    each = lambda f, *xs: [f(*x) for x in zip(*xs)]
    cum = each(lambda x: _doth(incl, x), lw)
    tot = each(lambda x: jnp.sum(x, axis=0, keepdims=True), lw)
    e_inv = each(lambda x: jnp.exp(-x), cum)
    a_st = each(lambda x, cm, l: per_head(x * jnp.exp(cm - l)), a, cum, lw)
    r_t = each(lambda x, cm: x * jnp.exp(cm), r, cum)
    b_st = each(lambda x, e: per_head(x * e), b, e_inv)
    k_st = each(lambda x, e: per_head(x * e), k, e_inv)
    v_st = each(per_head, v)
    m = each(lambda x, w: jnp.where(strict2, _dotb(x, w, _NT), 0.0), a_st, b_st)
    m_k = each(lambda x, w: jnp.where(strict2, _dotb(x, w, _NT), 0.0), a_st, k_st)
    u = each(lambda x, s, mk, w: _dotb(x, s, _NT) + _dotb(mk, w), a_st, s0, m_k, v_st)
    steps = (c - 1).bit_length()
    for s in range(steps):
        u = each(lambda x, w: x + _dotb(w, x), u, m)
        if s + 1 < steps:
            m = each(lambda w: _dotb(w, w), m)
    n_b = each(lambda x, w: jnp.where(incl2, _dotb(x, w, _NT), 0.0), r_t, b_st)
    n_k = each(lambda x, w: jnp.where(incl2, _dotb(x, w, _NT), 0.0), r_t, k_st)
    y = each(lambda x, s, nb, uu, nk, w: _dotb(x, s, _NT) + _dotb(nb, uu) + _dotb(nk, w), r_t, s0, n_b, u, n_k, v_st)
    dec = each(lambda tt, cm: jnp.exp(tt - cm), tot, cum)
    s1 = each(lambda s, tt, uu, x, d, w, kk: s * jnp.exp(tt) + _dotb(uu, per_head(x * d), _TN) + _dotb(w, per_head(kk * d), _TN),
              s0, tot, u, b, dec, v_st, k)
    return tuple(y), tuple(s1)


WKV_PAIRS_PER_STEP = 4
PAIR = 2 * HEAD_DIM


def _wkv_fwd(name, srcs):
    t = srcs[0][0].shape[0]
    c = _chunk_len(t)
    nc = t // c
    pp = WKV_PAIRS_PER_STEP
    n_pairs = D_R // PAIR

    def body(r, lw, k, v, a, b, y_ref, st_ref, state):
        @pl.when(pl.program_id(1) == 0)
        def _():
            state[...] = jnp.zeros(state.shape, F32)

        pairs = lambda ref: tuple(ref[:, pl.ds(i * PAIR, PAIR)] for i in range(pp))
        s0 = tuple(state[i] for i in range(pp))
        y, s1 = _wkv_chunk(s0, pairs(r), pairs(lw), pairs(k), pairs(v), pairs(a), pairs(b))
        for i in range(pp):
            st_ref[i] = s0[i]
            y_ref[:, pl.ds(i * PAIR, PAIR)] = y[i]
            state[i] = s1[i]

    seq = lambda off: pl.BlockSpec((c, pp * PAIR), lambda g, j: (j, off + g))
    return pl.pallas_call(
        body, name=name, grid=(n_pairs // pp, nc), in_specs=[seq(off) for _, off in srcs],
        out_specs=[seq(0), pl.BlockSpec((pp, None, PAIR, PAIR), lambda g, j: (g, j, 0, 0))],
        out_shape=[jax.ShapeDtypeStruct((t, D_R), F32), jax.ShapeDtypeStruct((n_pairs, nc, PAIR, PAIR), F32)],
        scratch_shapes=[pltpu.VMEM((pp, PAIR, PAIR), F32)],
        compiler_params=_cparams(None),
    )(*[a for a, _ in srcs])


def _wkv_bwd(name, srcs, st, dy):
    t = srcs[0][0].shape[0]
    c = _chunk_len(t)
    nc = t // c
    pp = WKV_PAIRS_PER_STEP
    n_pairs = D_R // PAIR

    def body(r, lw, k, v, a, b, st_ref, dy_ref, dr, dlw, dk, dv, da, db, dstate):
        @pl.when(pl.program_id(1) == 0)
        def _():
            dstate[...] = jnp.zeros(dstate.shape, F32)

        half = lax.broadcasted_iota(jnp.int32, (PAIR, PAIR), 0) < HEAD_DIM
        same_head = half == (lax.broadcasted_iota(jnp.int32, (PAIR, PAIR), 1) < HEAD_DIM)
        pairs = lambda ref: tuple(ref[:, pl.ds(i * PAIR, PAIR)] for i in range(pp))
        s0 = tuple(st_ref[i] for i in range(pp))
        _, vjp = jax.vjp(_wkv_chunk, s0, pairs(r), pairs(lw), pairs(k), pairs(v), pairs(a), pairs(b))
        ds0, *dxs = vjp((pairs(dy_ref), tuple(dstate[i] for i in range(pp))))
        for i in range(pp):
            for ref, val in zip((dr, dlw, dk, dv, da, db), dxs):
                ref[:, pl.ds(i * PAIR, PAIR)] = val[i]
            dstate[i] = jnp.where(same_head, ds0[i], 0.0)

    seq = lambda off: pl.BlockSpec((c, pp * PAIR), lambda g, j: (nc - 1 - j, off + g))
    return pl.pallas_call(
        body, name=name, grid=(n_pairs // pp, nc),
        in_specs=[seq(off) for _, off in srcs]
        + [pl.BlockSpec((pp, None, PAIR, PAIR), lambda g, j: (g, nc - 1 - j, 0, 0)), seq(dy[1])],
        out_specs=[seq(0)] * 6,
        out_shape=[jax.ShapeDtypeStruct((t, D_R), F32)] * 6,
        scratch_shapes=[pltpu.VMEM((pp, PAIR, PAIR), F32)],
        compiler_params=_cparams(None),
    )(*[a for a, _ in srcs], st, dy[0])


def _rope(x, cos, sin, rot):
    return x * cos + _dotb(x, rot) * sin


def _attn_block(nb, q, kp, kc, km, vp, vc, vm, sk, cq, sq, cp, sp, cm, sm, rot):
    g = GQA_GROUP
    scale = HEAD_DIM ** -0.5
    each = lambda f, *xs: [f(*x) for x in zip(*xs)]
    down = lambda x: jnp.concatenate([x] * g, axis=0)
    cq4, sq4 = down(cq), down(sq)
    kpr = each(lambda x: _rope(x, cp, sp, rot), kp)
    kcr = each(lambda x: _rope(x, cq, sq, rot), kc)
    kmr = each(lambda x: _rope(x, cm, sm, rot), km)
    qr = each(lambda x: _rope(x, cq4, sq4, rot), q)
    i = lax.broadcasted_iota(jnp.int32, (g * BLOCK, BLOCK), 0)
    i = i - BLOCK * ((i >= BLOCK).astype(jnp.int32) + (i >= 2 * BLOCK).astype(jnp.int32) + (i >= 3 * BLOCK).astype(jnp.int32))
    j = lax.broadcasted_iota(jnp.int32, (g * BLOCK, BLOCK), 1)
    nbv = jnp.zeros((g * BLOCK, BLOCK), jnp.int32) + nb
    ok_p = (j > i) & (nbv >= 2)
    ok_c = (j <= i) & (nbv >= 1)
    ok_m = (j >= BLOCK - N_META) & ((nbv >= 1) | (j <= i))
    sink = each(lambda s4: jnp.concatenate([jnp.broadcast_to(s, (BLOCK, 1)) for s in s4], axis=0), sk)
    s_p = each(lambda x, kk: jnp.where(ok_p, _dotb(x, kk, _NT) * scale, NEG_INF), qr, kpr)
    s_c = each(lambda x, kk: jnp.where(ok_c, _dotb(x, kk, _NT) * scale, NEG_INF), qr, kcr)
    s_m = each(lambda x, kk: jnp.where(ok_m, _dotb(x, kk, _NT) * scale, NEG_INF), qr, kmr)
    rmax = lambda s: jnp.max(s, axis=-1, keepdims=True)
    m = each(lambda a, b, c, d: lax.stop_gradient(jnp.maximum(jnp.maximum(rmax(a), rmax(b)), jnp.maximum(rmax(c), d))),
             s_p, s_c, s_m, sink)
    e_p = each(lambda s, mm: jnp.exp(s - mm), s_p, m)
    e_c = each(lambda s, mm: jnp.exp(s - mm), s_c, m)
    e_m = each(lambda s, mm: jnp.exp(s - mm), s_m, m)
    rsum = lambda e: jnp.sum(e, axis=-1, keepdims=True)
    inv = each(lambda a, b, c, d, mm: 1.0 / (rsum(a) + rsum(b) + rsum(c) + jnp.exp(d - mm)), e_p, e_c, e_m, sink, m)
    return tuple(each(lambda a, b, c, iv, x, y, z: _dotb(a * iv, x) + _dotb(b * iv, y) + _dotb(c * iv, z),
                      e_p, e_c, e_m, inv, vp, vc, vm))


def _attn_specs():
    cur = lambda n: (0, n, 0)
    prev = lambda n: (0, jnp.maximum(n - 1, 0), 0)
    meta = lambda n: (0, 0, 0)
    kv = lambda m: pl.BlockSpec((N_KV_HEADS, BLOCK, HEAD_DIM), m)
    tab = lambda m: pl.BlockSpec((BLOCK, HEAD_DIM), m)
    tcur, tprev, tmeta = (lambda n: (n, 0)), (lambda n: (jnp.maximum(n - 1, 0), 0)), (lambda n: (0, 0))
    qspec = pl.BlockSpec((N_Q_HEADS, BLOCK, HEAD_DIM), cur)
    sspec = pl.BlockSpec((N_Q_HEADS, 8, LANES), meta)
    specs = [qspec, kv(prev), kv(cur), kv(meta), kv(prev), kv(cur), kv(meta), sspec,
             tab(tcur), tab(tcur), tab(tprev), tab(tprev), tab(tmeta), tab(tmeta),
             pl.BlockSpec((HEAD_DIM, HEAD_DIM), lambda n: (0, 0))]
    return specs, qspec, sspec, kv


def _attn_args(q, k, v, sinks_b, cos, sin, rot):
    return (q, k, k, k, v, v, v, sinks_b, cos, sin, cos, sin, cos, sin, rot)


def _attn_operands(q_ref, kp, kc, km, vp, vc, vm, s_ref):
    groups = range(N_KV_HEADS)
    q = tuple(jnp.concatenate([q_ref[GQA_GROUP * i + h] for h in range(GQA_GROUP)], axis=0) for i in groups)
    sk = tuple(tuple(s_ref[GQA_GROUP * i + h][0:1, 0:1] for h in range(GQA_GROUP)) for i in groups)
    per_head = lambda ref: tuple(ref[i] for i in groups)
    return q, per_head(kp), per_head(kc), per_head(km), per_head(vp), per_head(vc), per_head(vm), sk


def _attn_fwd(name, q, k, v, sinks_b, cos, sin, rot):
    tp = q.shape[1]
    specs, qspec, _, _ = _attn_specs()

    def body(q_ref, kp, kc, km, vp, vc, vm, s_ref, cq, sq, cp, sp, cm, sm, rot_ref, o_ref):
        out = _attn_block(pl.program_id(0), *_attn_operands(q_ref, kp, kc, km, vp, vc, vm, s_ref),
                          cq[...], sq[...], cp[...], sp[...], cm[...], sm[...], rot_ref[...])
        for i in range(N_KV_HEADS):
            for h in range(GQA_GROUP):
                o_ref[GQA_GROUP * i + h] = out[i][h * BLOCK:(h + 1) * BLOCK]

    return pl.pallas_call(
        body, name=name, grid=(tp // BLOCK,), in_specs=specs, out_specs=qspec,
        out_shape=jax.ShapeDtypeStruct(q.shape, F32), compiler_params=_cparams(None),
    )(*_attn_args(q, k, v, sinks_b, cos, sin, rot))


def _attn_bwd(name, q, k, v, sinks_b, cos, sin, rot, do):
    tp = q.shape[1]
    nb = tp // BLOCK
    specs, qspec, sspec, kv = _attn_specs()

    def body(q_ref, kp, kc, km, vp, vc, vm, s_ref, cq, sq, cp, sp, cm, sm, rot_ref, do_ref,
             dq_ref, dkp, dkc, dvp, dvc, dkm, dvm, ds_ref):
        n = pl.program_id(0)
        tabs = (cq[...], sq[...], cp[...], sp[...], cm[...], sm[...], rot_ref[...])
        _, vjp = jax.vjp(lambda *a: _attn_block(n, *a, *tabs), *_attn_operands(q_ref, kp, kc, km, vp, vc, vm, s_ref))
        do_all = tuple(jnp.concatenate([do_ref[GQA_GROUP * i + h] for h in range(GQA_GROUP)], axis=0)
                       for i in range(N_KV_HEADS))
        dq, gkp, gkc, gkm, gvp, gvc, gvm, dsk = vjp(do_all)
        for i in range(N_KV_HEADS):
            dkp[i] = gkp[i]
            dkc[i] = gkc[i]
            dvp[i] = gvp[i]
            dvc[i] = gvc[i]
            for h in range(GQA_GROUP):
                dq_ref[GQA_GROUP * i + h] = dq[i][h * BLOCK:(h + 1) * BLOCK]

        @pl.when(n == 0)
        def _():
            for i in range(N_KV_HEADS):
                dkm[i] = gkm[i]
                dvm[i] = gvm[i]
                for h in range(GQA_GROUP):
                    ds_ref[GQA_GROUP * i + h] = jnp.broadcast_to(dsk[i][h], (8, LANES))

        @pl.when(n != 0)
        def _():
            for i in range(N_KV_HEADS):
                dkm[i] += gkm[i]
                dvm[i] += gvm[i]
                for h in range(GQA_GROUP):
                    ds_ref[GQA_GROUP * i + h] += jnp.broadcast_to(dsk[i][h], (8, LANES))

    part = pl.BlockSpec((N_KV_HEADS, None, BLOCK, HEAD_DIM), lambda n: (0, n, 0, 0))
    part_shape = jax.ShapeDtypeStruct((N_KV_HEADS, nb, BLOCK, HEAD_DIM), F32)
    meta_shape = jax.ShapeDtypeStruct((N_KV_HEADS, BLOCK, HEAD_DIM), F32)
    return pl.pallas_call(
        body, name=name, grid=(nb,), in_specs=specs + [qspec],
        out_specs=[qspec, part, part, part, part, kv(lambda n: (0, 0, 0)), kv(lambda n: (0, 0, 0)), sspec],
        out_shape=[jax.ShapeDtypeStruct(q.shape, F32), part_shape, part_shape, part_shape, part_shape,
                   meta_shape, meta_shape, jax.ShapeDtypeStruct(sinks_b.shape, F32)],
        compiler_params=_cparams(None),
    )(*_attn_args(q, k, v, sinks_b, cos, sin, rot), do)


def _kv_combine(name, prev_part, own_part, meta):
    g, nb = own_part.shape[:2]

    def fn(own, nxt, mt):
        m = pl.program_id(1)
        one = jnp.ones((BLOCK, HEAD_DIM), F32)
        use_next = jnp.where(one * m < nb - 1, 1.0, 0.0)
        use_meta = jnp.where(one * m < 1, 1.0, 0.0)
        return own + nxt * use_next + mt * use_meta

    blk = (None, None, BLOCK, HEAD_DIM)
    return _call(fn, name, (g, nb),
                 [(own_part, blk, lambda a, m: (a, m, 0, 0)),
                  (prev_part, blk, lambda a, m: (a, jnp.minimum(m + 1, nb - 1), 0, 0)),
                  (meta, (None, BLOCK, HEAD_DIM), lambda a, m: (a, 0, 0))],
                 [((g, nb * BLOCK, HEAD_DIM), (None, BLOCK, HEAD_DIM), lambda a, m: (a, m, 0), False)])


PACK_W = 1024
ELEMENTWISE_BLOCK_BYTES = 1 << 21


def _rows_tile(rows, cols):
    cap = max(8, ELEMENTWISE_BLOCK_BYTES // (4 * cols))
    for d in range(min(rows, cap), 0, -1):
        if rows % d == 0 and d % 8 == 0:
            return d
    return rows


def _adamw(name, w, g, m, v):
    rows, cols = w.shape
    tr = _rows_tile(rows, cols)

    def fn(wv, gv, mv, vv):
        m1 = ADAM_B1 * mv + (1.0 - ADAM_B1) * gv
        v1 = ADAM_B2 * vv + (1.0 - ADAM_B2) * (gv * gv)
        m_hat = m1 / (1.0 - ADAM_B1 ** ADAM_STEP)
        v_hat = v1 / (1.0 - ADAM_B2 ** ADAM_STEP)
        return -ADAM_LR * (m_hat / (jnp.sqrt(v_hat) + ADAM_EPS) + ADAM_WD * wv), m1, v1

    blk = (tr, cols)
    row = lambda i: (i, 0)
    return _call(fn, name, (rows // tr,), [(a, blk, row) for a in (w, g, m, v)], [((rows, cols), blk, row, False)] * 3)


def _pair_add(name, g, recv, c_idx, out_dtype):
    s, a, b = g.shape
    half = a // 2

    def body(c_ref, a_ref, b_ref, o_ref):
        o_ref[...] = (a_ref[...] + b_ref[...]).astype(out_dtype)

    blk = (None, half, b)
    return pl.pallas_call(
        body, name=name,
        grid_spec=pltpu.PrefetchScalarGridSpec(
            num_scalar_prefetch=1, grid=(s,),
            in_specs=[pl.BlockSpec(blk, lambda j, c: (j, c[0], 0)), pl.BlockSpec(blk, lambda j, c: (j, 0, 0))],
            out_specs=pl.BlockSpec(blk, lambda j, c: (j, 0, 0))),
        out_shape=jax.ShapeDtypeStruct((s, half, b), out_dtype), compiler_params=_cparams(None),
    )(c_idx, g, recv)


def _pair_add_placed(name, g, recv, cm_idx, out_dtype):
    s, a, b = g.shape
    half = a // 2

    def body(cm_ref, a_ref, b_ref, o_ref, own_ref):
        val = (a_ref[...] + b_ref[...]).astype(out_dtype)
        o_ref[...] = val

        @pl.when(pl.program_id(0) == cm_ref[1])
        def _():
            own_ref[...] = val

    blk = (None, half, b)
    shape = jax.ShapeDtypeStruct((s, half, b), out_dtype)
    return pl.pallas_call(
        body, name=name,
        grid_spec=pltpu.PrefetchScalarGridSpec(
            num_scalar_prefetch=1, grid=(s,),
            in_specs=[pl.BlockSpec(blk, lambda j, cm: (j, cm[0], 0)), pl.BlockSpec(blk, lambda j, cm: (j, 0, 0))],
            out_specs=[pl.BlockSpec(blk, lambda j, cm: (j, 0, 0)), pl.BlockSpec(blk, lambda j, cm: (cm[1], 0, 0))]),
        out_shape=[shape, shape], compiler_params=_cparams(None),
    )(cm_idx, g, recv)


def _sum_chips(name, parts, c_idx, layer, n_layers, into=None):
    _, a, b = parts.shape
    tr = _rows_tile(a, b)

    def body(c_ref, p0, p1, p2, p3, *rest):
        o_ref = rest[-1]
        up = lambda p: p[...].astype(F32)
        o_ref[...] = ((up(p0) + up(p1)) + up(p2)) + up(p3)

    in_specs = [pl.BlockSpec((None, tr, b), lambda i, c, k=k: (k, i, 0)) for k in range(N_CHIPS)]
    args = [c_idx] + [parts] * N_CHIPS
    aliases = {}
    if into is not None:
        in_specs.append(_ANY)
        args.append(into)
        aliases = {1 + N_CHIPS: 0}
    return pl.pallas_call(
        body, name=name,
        grid_spec=pltpu.PrefetchScalarGridSpec(
            num_scalar_prefetch=1, grid=(a // tr,), in_specs=in_specs,
            out_specs=pl.BlockSpec((None, None, tr, b), lambda i, c: (layer, c[0], i, 0))),
        out_shape=jax.ShapeDtypeStruct((n_layers, 2, a, b), F32), input_output_aliases=aliases,
        compiler_params=_cparams(None),
    )(*args)


def _place_own_block(name, w, layer, me_idx, dtype):
    _, a2, b = w.shape
    a = a2 // 2
    tr = _rows_tile(a, b)
    nb = a // tr

    def body(me_ref, w_ref, o_ref):
        o_ref[...] = w_ref[...].astype(dtype)

    return pl.pallas_call(
        body, name=name,
        grid_spec=pltpu.PrefetchScalarGridSpec(
            num_scalar_prefetch=1, grid=(2, nb),
            in_specs=[pl.BlockSpec((None, tr, b), lambda h, i, me: (layer, h * nb + i, 0))],
            out_specs=pl.BlockSpec((None, None, tr, b), lambda h, i, me: (me[0], h, i, 0))),
        out_shape=jax.ShapeDtypeStruct((N_CHIPS, 2, a, b), dtype), compiler_params=_cparams(None),
    )(me_idx, w)


def _mesh_pos():
    return lax.axis_index("x"), lax.axis_index("y"), lax.axis_index("c")


def _other_chips(x, y):
    return [(1 - x, y), (x, 1 - y), (1 - x, 1 - y)]


_ANY = pl.BlockSpec(memory_space=pl.ANY)


def _gather_weights(name, bufs, from_chips=True):
    n = len(bufs)

    def body(*refs):
        out_refs = refs[n:2 * n]
        send_sems, recv_sems = refs[2 * n:]
        x, y, c = _mesh_pos()
        me = 2 * x + y
        sibling = (x, y, 1 - c)
        chips = _other_chips(x, y)

        def copy(i, k, chip_idx, half, to):
            return pltpu.make_async_remote_copy(src_ref=out_refs[i].at[chip_idx, half], dst_ref=out_refs[i].at[chip_idx, half],
                                                send_sem=send_sems.at[6 * i + k], recv_sem=recv_sems.at[6 * i + k],
                                                device_id=to, device_id_type=MESH)

        first = [copy(i, j, me, c, (*chip, c)) for i in range(n) for j, chip in enumerate(chips)] if from_chips else []
        for cp in first:
            cp.start()
        passed = []
        for i in range(n):
            for j, (cx, cy) in enumerate(chips):
                idx = 2 * cx + cy
                if from_chips:
                    copy(i, j, idx, c, sibling).wait_recv()
                fwd = copy(i, 3 + j, idx, c, sibling)
                fwd.start()
                passed.append(fwd)
        for i in range(n):
            for j, (cx, cy) in enumerate(chips):
                copy(i, 3 + j, 2 * cx + cy, 1 - c, sibling).wait_recv()
        for cp in first + passed:
            cp.wait_send()

    return pl.pallas_call(
        body, name=name, in_specs=[_ANY] * n, out_specs=[_ANY] * n,
        out_shape=[jax.ShapeDtypeStruct(b.shape, b.dtype) for b in bufs],
        input_output_aliases={i: i for i in range(n)},
        scratch_shapes=[pltpu.SemaphoreType.DMA((6 * n,)), pltpu.SemaphoreType.DMA((6 * n,))],
        compiler_params=pltpu.CompilerParams(has_side_effects=True),
    )(*bufs)


def _gather_start(name, groups):
    bufs = [b for g in groups for b in g]
    n = len(bufs)
    ng = len(groups)

    def body(*refs):
        b_refs = refs[:n]
        sems = refs[n:n + 2 * ng]
        token = refs[-1]
        x, y, c = _mesh_pos()
        me = 2 * x + y
        i = 0
        for gi, g in enumerate(groups):
            for k in range(len(g)):
                for j, (cx, cy) in enumerate(_other_chips(x, y)):
                    pltpu.make_async_remote_copy(src_ref=b_refs[i].at[me, c], dst_ref=b_refs[i].at[me, c],
                                                 send_sem=sems[2 * gi].at[3 * k + j], recv_sem=sems[2 * gi + 1].at[3 * k + j],
                                                 device_id=(cx, cy, c), device_id_type=MESH).start()
                i += 1
        token[...] = jnp.zeros(token.shape, F32)

    sem_shapes = [pltpu.SemaphoreType.DMA((3 * len(g),)) for g in groups for _ in range(2)]
    res = pl.pallas_call(
        body, name=name,
        out_shape=(*sem_shapes, *[pltpu.HBM(b.shape, b.dtype) for b in bufs], jax.ShapeDtypeStruct((8, LANES), F32)),
        in_specs=[_HBM] * n,
        out_specs=(*[_SEM] * (2 * ng), *[_HBM] * n, pl.BlockSpec(memory_space=pltpu.VMEM)),
        input_output_aliases={i: 2 * ng + i for i in range(n)},
        compiler_params=pltpu.CompilerParams(has_side_effects=_DATAFLOW),
    )(*[pltpu.with_memory_space_constraint(b, pltpu.HBM) for b in bufs])
    out, i = [], 2 * ng
    for gi, g in enumerate(groups):
        out.append((res[2 * gi], res[2 * gi + 1], list(res[i:i + len(g)])))
        i += len(g)
    return out, res[-1]


def _gather_wait(name, send_sems, recv_sems, bufs, after):
    n = len(bufs)

    def body(*refs):
        b_refs = refs[:n]
        s_sems, r_sems = refs[n], refs[n + 1]
        x, y, c = _mesh_pos()
        me = 2 * x + y
        for k in range(n):
            for j, (cx, cy) in enumerate(_other_chips(x, y)):
                idx = 2 * cx + cy
                copy = pltpu.make_async_remote_copy(src_ref=b_refs[k].at[me, c], dst_ref=b_refs[k].at[idx, c],
                                                    send_sem=s_sems.at[3 * k + j], recv_sem=r_sems.at[3 * k + j],
                                                    device_id=(cx, cy, c), device_id_type=MESH)
                copy.wait_send()
                copy.wait_recv()

    res = pl.pallas_call(
        body, name=name,
        out_shape=tuple(pltpu.HBM(b.shape, b.dtype) for b in bufs),
        in_specs=[_HBM] * n + [_SEM, _SEM, _ANY],
        out_specs=tuple([_HBM] * n),
        input_output_aliases={i: i for i in range(n)},
        compiler_params=pltpu.CompilerParams(has_side_effects=_DATAFLOW),
    )(*bufs, send_sems, recv_sems, after)
    return list(res)


def _halves_to_sibling(name, units):
    n = len(units)

    def body(*refs):
        g_refs, out_refs = refs[:n], refs[n:2 * n]
        send_sems, recv_sems = refs[2 * n:]
        x, y, c = _mesh_pos()
        cps = []
        for i in range(n):
            half = units[i].shape[1] // 2
            src = g_refs[i].at[pl.ds(0, N_CHIPS), pl.ds((1 - c) * half, half)]
            cp = pltpu.make_async_remote_copy(src_ref=src, dst_ref=out_refs[i], send_sem=send_sems.at[i],
                                              recv_sem=recv_sems.at[i], device_id=(x, y, 1 - c), device_id_type=MESH)
            cp.start()
            cps.append(cp)
        for cp in cps:
            cp.wait()

    return pl.pallas_call(
        body, name=name, in_specs=[_ANY] * n, out_specs=[_ANY] * n,
        out_shape=[jax.ShapeDtypeStruct((u.shape[0], u.shape[1] // 2, u.shape[2]), u.dtype) for u in units],
        scratch_shapes=[pltpu.SemaphoreType.DMA((n,)), pltpu.SemaphoreType.DMA((n,))],
        compiler_params=pltpu.CompilerParams(has_side_effects=True),
    )(*units)


def _scatter_to_chips(name, sums):
    n = len(sums)

    def body(*refs):
        h_refs, out_refs = refs[:n], refs[n:2 * n]
        send_sems, recv_sems, local_sems = refs[2 * n:]
        x, y, c = _mesh_pos()
        me = 2 * x + y
        chips = _other_chips(x, y)
        local = [pltpu.make_async_copy(h_refs[i].at[me], out_refs[i].at[me], local_sems.at[i]) for i in range(n)]
        for cp in local:
            cp.start()

        def copy(i, j, src_idx, dst_idx):
            cx, cy = chips[j]
            return pltpu.make_async_remote_copy(src_ref=h_refs[i].at[src_idx], dst_ref=out_refs[i].at[dst_idx],
                                                send_sem=send_sems.at[3 * i + j], recv_sem=recv_sems.at[3 * i + j],
                                                device_id=(cx, cy, c), device_id_type=MESH)

        cps = [copy(i, j, 2 * chips[j][0] + chips[j][1], me) for i in range(n) for j in range(3)]
        for cp in cps:
            cp.start()
        for i in range(n):
            for j in range(3):
                copy(i, j, me, 2 * chips[j][0] + chips[j][1]).wait_recv()
        for cp in cps:
            cp.wait_send()
        for cp in local:
            cp.wait()

    return pl.pallas_call(
        body, name=name, in_specs=[_ANY] * n, out_specs=[_ANY] * n,
        out_shape=[jax.ShapeDtypeStruct(s.shape, s.dtype) for s in sums],
        scratch_shapes=[pltpu.SemaphoreType.DMA((3 * n,)), pltpu.SemaphoreType.DMA((3 * n,)), pltpu.SemaphoreType.DMA((n,))],
        compiler_params=pltpu.CompilerParams(has_side_effects=True),
    )(*sums)


_HBM = pl.BlockSpec(memory_space=pltpu.HBM)
_SEM = pl.BlockSpec(memory_space=pltpu.SEMAPHORE)
_DATAFLOW = pltpu.SideEffectType.DATAFLOW_SIDE_EFFECTING


def _scatter_start(name, sums, zones):
    n = len(sums)

    def body(*refs):
        h_refs, z_refs = refs[:n], refs[n:2 * n]
        send_sems, recv_sems = refs[2 * n], refs[2 * n + 1]
        token = refs[-1]
        x, y, c = _mesh_pos()
        me = 2 * x + y
        for i in range(n):
            for j, (cx, cy) in enumerate(_other_chips(x, y)):
                pltpu.make_async_remote_copy(src_ref=h_refs[i].at[2 * cx + cy], dst_ref=z_refs[i].at[me],
                                             send_sem=send_sems.at[3 * i + j], recv_sem=recv_sems.at[3 * i + j],
                                             device_id=(cx, cy, c), device_id_type=MESH).start()
        token[...] = jnp.zeros(token.shape, F32)

    hbm = lambda a: pltpu.HBM(a.shape, a.dtype)
    res = pl.pallas_call(
        body, name=name,
        out_shape=(pltpu.SemaphoreType.DMA((3 * n,)), pltpu.SemaphoreType.DMA((3 * n,)),
                   *[hbm(a) for a in sums], *[hbm(a) for a in zones], jax.ShapeDtypeStruct((8, LANES), F32)),
        in_specs=[_HBM] * (2 * n),
        out_specs=(_SEM, _SEM, *[_HBM] * (2 * n), pl.BlockSpec(memory_space=pltpu.VMEM)),
        input_output_aliases={i: 2 + i for i in range(2 * n)},
        compiler_params=pltpu.CompilerParams(has_side_effects=_DATAFLOW),
    )(*[pltpu.with_memory_space_constraint(a, pltpu.HBM) for a in list(sums) + list(zones)])
    return res[0], res[1], res[2:2 + n], res[2 + n:2 + 2 * n], res[-1]


def _scatter_wait(name, send_sems, recv_sems, sums, zones, after):
    n = len(sums)

    def body(*refs):
        h_refs, z_refs = refs[:n], refs[n:2 * n]
        s_sems, r_sems = refs[2 * n], refs[2 * n + 1]
        x, y, c = _mesh_pos()
        me = 2 * x + y
        for i in range(n):
            for j, (cx, cy) in enumerate(_other_chips(x, y)):
                idx = 2 * cx + cy
                copy = pltpu.make_async_remote_copy(src_ref=h_refs[i].at[idx], dst_ref=z_refs[i].at[idx],
                                                    send_sem=s_sems.at[3 * i + j], recv_sem=r_sems.at[3 * i + j],
                                                    device_id=(cx, cy, c), device_id_type=MESH)
                copy.wait_send()
                copy.wait_recv()

    hbm = lambda a: pltpu.HBM(a.shape, a.dtype)
    res = pl.pallas_call(
        body, name=name,
        out_shape=(*[hbm(a) for a in sums], *[hbm(a) for a in zones]),
        in_specs=[_HBM] * (2 * n) + [_SEM, _SEM, _ANY],
        out_specs=tuple([_HBM] * (2 * n)),
        input_output_aliases={i: i for i in range(2 * n)},
        compiler_params=pltpu.CompilerParams(has_side_effects=_DATAFLOW),
    )(*sums, *zones, send_sems, recv_sems, after)
    return res[n:]


def _join_halves(name, results):
    n = len(results)
    pieces = [(i, l) for i in range(n) for l in range(results[i].shape[0])]

    def body(*refs):
        out_refs = refs[n:2 * n]
        send_sems, recv_sems = refs[2 * n:]
        x, y, c = _mesh_pos()

        def copy(k, half):
            i, l = pieces[k]
            return pltpu.make_async_remote_copy(src_ref=out_refs[i].at[l, half], dst_ref=out_refs[i].at[l, half],
                                                send_sem=send_sems.at[k], recv_sem=recv_sems.at[k],
                                                device_id=(x, y, 1 - c), device_id_type=MESH)

        cps = [copy(k, c) for k in range(len(pieces))]
        for cp in cps:
            cp.start()
        for k in range(len(pieces)):
            copy(k, 1 - c).wait_recv()
        for cp in cps:
            cp.wait_send()

    return pl.pallas_call(
        body, name=name, in_specs=[_ANY] * n, out_specs=[_ANY] * n,
        out_shape=[jax.ShapeDtypeStruct(r.shape, r.dtype) for r in results],
        input_output_aliases={i: i for i in range(n)},
        scratch_shapes=[pltpu.SemaphoreType.DMA((len(pieces),)), pltpu.SemaphoreType.DMA((len(pieces),))],
        compiler_params=pltpu.CompilerParams(has_side_effects=True),
    )(*results)


def _pack(arrays, dtype, rows_multiple):
    flat = jnp.concatenate([a.reshape(-1).astype(dtype) for a in arrays])
    unit = rows_multiple * PACK_W
    total = -(-flat.shape[0] // unit) * unit
    return jnp.pad(flat, (0, total - flat.shape[0])).reshape(total // PACK_W, PACK_W)


def _unpack(flat, shapes):
    out, off = [], 0
    for s in shapes:
        n = 1
        for d in s:
            n *= d
        out.append(flat[..., off:off + n].reshape(flat.shape[:-1] + tuple(s)))
        off += n
    return out


def _ffn_fwd(tag, l, h, g, w_up, conv, bias, w_down, tm):
    hn = _rms_fwd(f"{tag}_norm", h, g, tm)
    u = _mm_cs(f"{tag}_up", hn, w_up, l, tm)
    act = _ffn_col_fwd(f"{tag}_glu", u, conv, bias)
    h_out = _mm_full(f"{tag}_down", act, w_down, l, tm, D_FF // 2, add=h)
    return h_out, (hn, u, act)


def _ffn_bwd(tag, l, h, g, w_up, conv, bias, w_down, saved, dh, tm):
    hn, u, act = saved
    da = _mm_nt_full(f"{tag}_down_dx", dh, w_down, l, tm, D_FF // 2)
    dw_down = _mm_tn_full(f"{tag}_down_dw", act, dh, tm, D_FF // 2)
    du, dconv, dbias = _ffn_col_bwd(f"{tag}_glu_bwd", u, da, conv, bias)
    dw_up = _mm_tn_cs(f"{tag}_up_dw", hn, du, N_CHIPS, tm)
    dhn = _mm_nt_cs(f"{tag}_up_dx", du, w_up, l, tm)
    dh, dg = _rms_bwd(f"{tag}_norm_bwd", h, g, dhn, dh, tm)
    return dh, dict(norm=dg, w_up=dw_up, conv=dconv, bias=dbias, w_down=dw_down)


def _to_heads(z, nh, pad):
    t = z.shape[0]
    return jnp.pad(z.reshape(t, nh, HEAD_DIM).transpose(1, 0, 2), ((0, 0), (pad, 0), (0, 0)))


def _from_heads(z, pad):
    nh, tp, _ = z.shape
    return z[:, pad:].transpose(1, 0, 2).reshape(tp - pad, nh * HEAD_DIM)


def _rope_tables(tp, pad):
    half = HEAD_DIM // 2
    inv = ROPE_THETA ** (-jnp.arange(half, dtype=F32) / half)
    ang = (jnp.arange(tp, dtype=F32) - pad)[:, None] * inv[None, :]
    cos, sin = jnp.cos(ang), jnp.sin(ang)
    rot = jnp.zeros((HEAD_DIM, HEAD_DIM), F32)
    idx = jnp.arange(half)
    rot = rot.at[idx + half, idx].set(-1.0).at[idx, idx + half].set(1.0)
    return jnp.concatenate([cos, cos], axis=1), jnp.concatenate([sin, sin], axis=1), rot


def _local_step(x, tgt, w, on_grads=None, fetch=None):
    emit = on_grads if on_grads is not None else (lambda tag, units: 0.0)
    need = (lambda tag, after: w) if fetch is None else (lambda tag, after: {**w, **fetch(tag, after)})
    seq = x.shape[0]
    t = seq + N_META
    tm = _row_tile(t, 704)
    tr = _row_tile(t, 352)
    pad = BLOCK - N_META
    grads = {}

    h0 = jnp.concatenate([w["meta_tokens"], x], axis=0)
    tgt_p = jnp.pad(tgt, ((N_META, 0), (0, 0)))

    hn0 = _rms_fwd("l0_norm", h0, w["norm_mix"][0:1], tm)
    p0 = _mm_cs("l0_in", hn0, w["ev_w_in"], 0, tm)
    uc, yb = _even_col_fwd("l0_convs", p0, w["ev_conv_a"], w["ev_conv_b"])
    ya = _even_ln_fwd("l0_ln", uc, w["ev_ln_a_g"], w["ev_ln_a_b"], tm)
    y0 = jnp.concatenate([ya, yb], axis=1)
    w = need("ev_out", y0)
    h1 = _mm_full("l0_out", y0, w["ev_w_out"], 0, tm, D_MODEL, add=h0)
    w = need("f0", h1)
    f0 = (0, h1, w["norm_ffn"][0:1], w["ff_w_up0"], w["ff_conv"][0], w["ff_conv_b"][0:1], w["ff_w_down0"])
    h2, ffn0 = _ffn_fwd("f0", *f0, tm)
    w = need("od", h2)

    hn2 = _rms_fwd("l1_norm", h2, w["norm_mix"][1:2], tm)
    p1 = _mm_cs("l1_in", hn2, w["od_w_in"], 0, tm)
    cos, sin, rot = _rope_tables(t + pad, pad)
    qh = _to_heads(p1[:, :D_ATT], N_Q_HEADS, pad)
    kh = _to_heads(p1[:, D_ATT:D_ATT + D_KV], N_KV_HEADS, pad)
    vh = _to_heads(p1[:, D_ATT + D_KV:D_ATT + 2 * D_KV], N_KV_HEADS, pad)
    sinks_b = jnp.broadcast_to(w["od_sinks"].reshape(N_Q_HEADS, 1, 1), (N_Q_HEADS, 8, LANES))
    y_att = _from_heads(_attn_fwd("l1_attn", qh, kh, vh, sinks_b, cos, sin, rot), pad)

    col0 = D_ATT + 2 * D_KV
    ch = jnp.arange(D_R) // HEAD_DIM
    seg = (ch[:, None] == ch[None, :]).astype(F32)
    prm = dict(w0=w["od_w0"], a0=w["od_a0"], g2=w["od_g2"], k_k=w["od_k_k"], k_a=w["od_k_a"],
               lnx_g=w["od_lnx_g"], lnx_b=w["od_lnx_b"], r_k=w["od_r_k"].reshape(1, D_R),
               w2p=jnp.concatenate([w["od_w2"], jnp.zeros((LORA_A, D_R), F32)], axis=0),
               a2p=jnp.concatenate([jnp.zeros((LORA_W, D_R), F32), w["od_a2"]], axis=0))
    prs = _shift_fwd("l1_shift", p1, col0, w["od_mu"])
    lw, k2, a_, b_, gate_r = _rwkv_pre_fwd("l1_rwkv_pre", prs, prm, seg, tr)
    v_off = 2 * D_R // (WKV_PAIRS_PER_STEP * PAIR)
    scan_in = [(prs, 0), (lw, 0), (k2, 0), (prs, v_off), (a_, 0), (b_, 0)]
    y_scan, states = _wkv_fwd("l1_wkv", scan_in)
    y_rwkv = _rwkv_post_fwd("l1_rwkv_post", y_scan, prs, k2, gate_r, prm, seg, tr)
    y1 = jnp.concatenate([y_att, y_rwkv], axis=1).astype(MXU_DTYPE)
    h3 = _mm_full("l1_out", y1, w["od_w_out"], 0, tm, D_MODEL, add=h2)
    w = need("f1", h3)
    f1 = (0, h3, w["norm_ffn"][1:2], w["ff_w_up1"], w["ff_conv"][1], w["ff_conv_b"][1:2], w["ff_w_down1"])
    h4, ffn1 = _ffn_fwd("f1", *f1, tm)

    loss_blk, dh, d_norm_final = _final_loss("final", h4, w["norm_final"], tgt_p, tm)
    grads["norm_final"] = d_norm_final

    dh, gf1 = _ffn_bwd("f1", *f1, ffn1, dh, tm)
    zero = emit("f1", {"ff_w_up1": gf1["w_up"], "ff_w_down1": gf1["w_down"].reshape(N_CHIPS, D_FF // N_CHIPS, D_MODEL)})
    prm = dict(prm, lnx_g=prm["lnx_g"] + zero)
    dy1 = _mm_nt_full("l1_out_dx", dh, w["od_w_out"], 0, tm, D_MODEL)
    grads["od_w_out"] = _mm_tn_full("l1_out_dw", y1, dh, tm, D_MODEL // 2)
    dy_scan, dr_p, dk2_p, dv_p, dgate_r, grads["od_lnx_g"], grads["od_lnx_b"], d_rk = _rwkv_post_bwd(
        "l1_rwkv_post_bwd", y_scan, prs, k2, gate_r, prm, seg, dy1, 1, tr)
    grads["od_r_k"] = d_rk.reshape(N_R_HEADS, HEAD_DIM)
    dr_s, dlw, dk2_s, dv_s, da_, db_ = _wkv_bwd("l1_wkv_bwd", scan_in, states, (dy_scan, 0))
    dk, dxl, dgd, grads["od_w0"], dw2p, grads["od_a0"], da2p, grads["od_g2"], grads["od_k_k"], grads["od_k_a"] = (
        _rwkv_pre_bwd("l1_rwkv_pre_bwd", prs, prm, seg, (dlw, dk2_s + dk2_p, da_, db_, dgate_r), tr))
    grads["od_w2"] = dw2p[:LORA_W]
    grads["od_a2"] = da2p[LORA_W:]
    dprs = jnp.concatenate([dr_s + dr_p, dk, dv_s + dv_p, dxl, dgd], axis=1)
    dpr, grads["od_mu"] = _shift_bwd("l1_shift_bwd", p1, col0, w["od_mu"], dprs)
    doh = _to_heads(dy1[:, :D_ATT], N_Q_HEADS, pad)
    dqh, dkp, dkc, dvp, dvc, dkm, dvm, dsinks = _attn_bwd("l1_attn_bwd", qh, kh, vh, sinks_b, cos, sin, rot, doh)
    grads["od_sinks"] = dsinks[:, 0, 0].reshape(1, N_Q_HEADS)
    dkh = _kv_combine("l1_attn_dk", dkp, dkc, dkm)
    dvh = _kv_combine("l1_attn_dv", dvp, dvc, dvm)
    dp1 = jnp.concatenate([_from_heads(dqh, pad), _from_heads(dkh, pad), _from_heads(dvh, pad), dpr], axis=1).astype(MXU_DTYPE)
    grads["od_w_in"] = _mm_tn_cs("l1_in_dw", hn2, dp1, N_CHIPS, tm)
    dhn2 = _mm_nt_cs("l1_in_dx", dp1, w["od_w_in"], 0, tm)
    dh, d_mix1 = _rms_bwd("l1_norm_bwd", h2, w["norm_mix"][1:2], dhn2, dh, tm)

    zero = emit("od", {"od_w_out": grads["od_w_out"].reshape(N_CHIPS, D_MODEL // N_CHIPS, D_MODEL), "od_w_in": grads["od_w_in"]})
    f0 = f0[:5] + (f0[5] + zero,) + f0[6:]
    dh, gf0 = _ffn_bwd("f0", *f0, ffn0, dh, tm)
    zero = emit("f0", {"ff_w_up0": gf0["w_up"], "ff_w_down0": gf0["w_down"].reshape(N_CHIPS, D_FF // N_CHIPS, D_MODEL)})
    w = dict(w, ev_ln_a_g=w["ev_ln_a_g"] + zero)
    dy0 = _mm_nt_full("l0_out_dx", dh, w["ev_w_out"], 0, tm, D_MODEL)
    grads["ev_w_out"] = _mm_tn_full("l0_out_dw", y0, dh, tm, D_MODEL // 2)
    duc, grads["ev_ln_a_g"], grads["ev_ln_a_b"] = _even_ln_bwd("l0_ln_bwd", uc, w["ev_ln_a_g"], w["ev_ln_a_b"], dy0, 0, tm)
    *dparts, grads["ev_conv_a"], grads["ev_conv_b"] = _even_col_bwd("l0_convs_bwd", p0, duc, dy0, w["ev_conv_a"], w["ev_conv_b"])
    dp0 = jnp.concatenate(dparts, axis=1)
    grads["ev_w_in"] = _mm_tn_cs("l0_in_dw", hn0, dp0, N_CHIPS, tm)
    dhn0 = _mm_nt_cs("l0_in_dx", dp0, w["ev_w_in"], 0, tm)
    dh, d_mix0 = _rms_bwd("l0_norm_bwd", h0, w["norm_mix"][0:1], dhn0, dh, tm)

    grads["norm_mix"] = jnp.concatenate([d_mix0, d_mix1], axis=0)
    grads["norm_ffn"] = jnp.concatenate([gf0["norm"], gf1["norm"]], axis=0)
    grads["ff_w_up"] = [gf0["w_up"], gf1["w_up"]]
    grads["ff_conv"] = jnp.stack([gf0["conv"], gf1["conv"]])
    grads["ff_conv_b"] = jnp.concatenate([gf0["bias"], gf1["bias"]], axis=0)
    grads["ff_w_down"] = [gf0["w_down"], gf1["w_down"]]
    grads["meta_tokens"] = dh[:N_META]
    return loss_blk[0, 0], dh[N_META:], grads


SHARD_AXIS = {
    "meta_tokens": 1, "norm_mix": None, "norm_ffn": None, "norm_final": None,
    "ev_w_in": 2, "ev_conv_a": 2, "ev_ln_a_g": None, "ev_ln_a_b": None, "ev_conv_b": 2, "ev_w_out": 1,
    "od_w_in": 2, "od_sinks": None, "od_mu": 1, "od_w0": 1, "od_w2": 2, "od_a0": 1, "od_a2": 2, "od_g2": 2,
    "od_k_k": 1, "od_k_a": 1, "od_r_k": None, "od_lnx_g": 1, "od_lnx_b": 1, "od_w_out": 1,
    "ff_w_up": 2, "ff_conv": 2, "ff_conv_b": None, "ff_w_down": 1,
}
WEIGHTS = list(SHARD_AXIS)
BIG = ("ev_w_in", "ev_w_out", "od_w_in", "od_w_out", "ff_w_up", "ff_w_down")
SHARDED = [n for n in WEIGHTS if SHARD_AXIS[n] is not None]
SMALL = [n for n in SHARDED if n not in BIG]
REPLICATED = [n for n in WEIGHTS if SHARD_AXIS[n] is None]


def _join(g, axis):
    return jnp.concatenate([g[k] for k in range(N_CHIPS)], axis=axis)


def _split(full, axis):
    return jnp.stack(jnp.split(full, N_CHIPS, axis=axis))


def _full_weights(gathered, repl):
    w = {}
    sq = lambda a: a.reshape(a.shape[1:]) if a.shape[0] == 1 else a
    for n in REPLICATED:
        w[n] = repl[n]
    w["norm_final"] = repl["norm_final"].reshape(1, D_MODEL)
    for n in ("ev_ln_a_g", "ev_ln_a_b"):
        w[n] = repl[n].reshape(1, D_A)
    w["od_r_k"] = repl["od_r_k"][0]
    w["meta_tokens"] = _join(gathered["meta_tokens"], 1)
    for n in ("ev_conv_a", "ev_conv_b", "od_w2", "od_a2", "od_g2"):
        w[n] = sq(_join(gathered[n], 2))
    for n in ("od_mu", "od_w0", "od_a0", "od_k_k", "od_k_a", "od_lnx_g", "od_lnx_b"):
        w[n] = _join(gathered[n], 1)
    w["ff_conv"] = _join(gathered["ff_conv"], 2)
    return w


def _shard_grads(grads):
    out = {}
    for n in REPLICATED:
        out[n] = grads[n]
    out["norm_final"] = grads["norm_final"].reshape(D_MODEL)
    out["od_r_k"] = grads["od_r_k"][None]
    out["meta_tokens"] = _split(grads["meta_tokens"], 1)
    for n in ("ev_conv_a", "ev_conv_b", "od_w2", "od_a2", "od_g2"):
        out[n] = _split(grads[n][None], 2)
    for n in ("od_mu", "od_w0", "od_a0", "od_k_k", "od_k_a", "od_lnx_g", "od_lnx_b"):
        out[n] = _split(grads[n], 1)
    out["ff_conv"] = _split(grads["ff_conv"], 2)
    return out


def kernel(x, meta_tokens, norm_mix, norm_ffn, norm_final, ev_w_in, ev_conv_a, ev_ln_a_g, ev_ln_a_b, ev_conv_b, ev_w_out, od_w_in, od_sinks, od_mu, od_w0, od_w2, od_a0, od_a2, od_g2, od_k_k, od_k_a, od_r_k, od_lnx_g, od_lnx_b, od_w_out, ff_w_up, ff_conv, ff_conv_b, ff_w_down, loss_target, m_meta_tokens, m_norm_mix, m_norm_ffn, m_norm_final, m_ev_w_in, m_ev_conv_a, m_ev_ln_a_g, m_ev_ln_a_b, m_ev_conv_b, m_ev_w_out, m_od_w_in, m_od_sinks, m_od_mu, m_od_w0, m_od_w2, m_od_a0, m_od_a2, m_od_g2, m_od_k_k, m_od_k_a, m_od_r_k, m_od_lnx_g, m_od_lnx_b, m_od_w_out, m_ff_w_up, m_ff_conv, m_ff_conv_b, m_ff_w_down, v_meta_tokens, v_norm_mix, v_norm_ffn, v_norm_final, v_ev_w_in, v_ev_conv_a, v_ev_ln_a_g, v_ev_ln_a_b, v_ev_conv_b, v_ev_w_out, v_od_w_in, v_od_sinks, v_od_mu, v_od_w0, v_od_w2, v_od_a0, v_od_a2, v_od_g2, v_od_k_k, v_od_k_a, v_od_r_k, v_od_lnx_g, v_od_lnx_b, v_od_w_out, v_ff_w_up, v_ff_conv, v_ff_conv_b, v_ff_w_down):
    wts = dict(meta_tokens=meta_tokens, norm_mix=norm_mix, norm_ffn=norm_ffn, norm_final=norm_final, ev_w_in=ev_w_in, ev_conv_a=ev_conv_a, ev_ln_a_g=ev_ln_a_g, ev_ln_a_b=ev_ln_a_b, ev_conv_b=ev_conv_b, ev_w_out=ev_w_out, od_w_in=od_w_in, od_sinks=od_sinks, od_mu=od_mu, od_w0=od_w0, od_w2=od_w2, od_a0=od_a0, od_a2=od_a2, od_g2=od_g2, od_k_k=od_k_k, od_k_a=od_k_a, od_r_k=od_r_k, od_lnx_g=od_lnx_g, od_lnx_b=od_lnx_b, od_w_out=od_w_out, ff_w_up=ff_w_up, ff_conv=ff_conv, ff_conv_b=ff_conv_b, ff_w_down=ff_w_down)
    mom = dict(meta_tokens=m_meta_tokens, norm_mix=m_norm_mix, norm_ffn=m_norm_ffn, norm_final=m_norm_final, ev_w_in=m_ev_w_in, ev_conv_a=m_ev_conv_a, ev_ln_a_g=m_ev_ln_a_g, ev_ln_a_b=m_ev_ln_a_b, ev_conv_b=m_ev_conv_b, ev_w_out=m_ev_w_out, od_w_in=m_od_w_in, od_sinks=m_od_sinks, od_mu=m_od_mu, od_w0=m_od_w0, od_w2=m_od_w2, od_a0=m_od_a0, od_a2=m_od_a2, od_g2=m_od_g2, od_k_k=m_od_k_k, od_k_a=m_od_k_a, od_r_k=m_od_r_k, od_lnx_g=m_od_lnx_g, od_lnx_b=m_od_lnx_b, od_w_out=m_od_w_out, ff_w_up=m_ff_w_up, ff_conv=m_ff_conv, ff_conv_b=m_ff_conv_b, ff_w_down=m_ff_w_down)
    var = dict(meta_tokens=v_meta_tokens, norm_mix=v_norm_mix, norm_ffn=v_norm_ffn, norm_final=v_norm_final, ev_w_in=v_ev_w_in, ev_conv_a=v_ev_conv_a, ev_ln_a_g=v_ev_ln_a_g, ev_ln_a_b=v_ev_ln_a_b, ev_conv_b=v_ev_conv_b, ev_w_out=v_ev_w_out, od_w_in=v_od_w_in, od_sinks=v_od_sinks, od_mu=v_od_mu, od_w0=v_od_w0, od_w2=v_od_w2, od_a0=v_od_a0, od_a2=v_od_a2, od_g2=v_od_g2, od_k_k=v_od_k_k, od_k_a=v_od_k_a, od_r_k=v_od_r_k, od_lnx_g=v_od_lnx_g, od_lnx_b=v_od_lnx_b, od_w_out=v_od_w_out, ff_w_up=v_ff_w_up, ff_conv=v_ff_conv, ff_conv_b=v_ff_conv_b, ff_w_down=v_ff_w_down)

    me_idx = (2 * lax.axis_index("x") + lax.axis_index("y")).astype(jnp.int32).reshape(1)
    c_idx = lax.axis_index("c").astype(jnp.int32).reshape(1)
    small_mine = _pack([wts[n] for n in SMALL], F32, 2 * 8)
    sources = {"ev_w_in": (ev_w_in, 0), "small": (small_mine[None], 0), "ev_w_out": (ev_w_out, 0),
               "ff_w_up0": (ff_w_up, 0), "ff_w_down0": (ff_w_down, 0), "od_w_in": (od_w_in, 0), "od_w_out": (od_w_out, 0),
               "ff_w_up1": (ff_w_up, 1), "ff_w_down1": (ff_w_down, 1)}
    bufs = {n: _place_own_block("place_" + n, a, l, me_idx, F32 if n == "small" else MXU_DTYPE)
            for n, (a, l) in sources.items()}

    def as_used(n, g):
        if n in ("ev_w_out", "od_w_out", "ff_w_down0", "ff_w_down1"):
            return g.reshape(1, -1, g.shape[-1])
        return g.reshape(N_CHIPS, 1, -1, g.shape[-1])

    first = dict(zip(("ev_w_in", "small"), _gather_weights("gather_first", [bufs["ev_w_in"], bufs["small"]])))
    gathered = dict(zip(SMALL, _unpack(first["small"].reshape(N_CHIPS, -1), [wts[n].shape for n in SMALL])))
    w_full = _full_weights(gathered, wts)
    w_full["ev_w_in"] = as_used("ev_w_in", first["ev_w_in"])
    groups = {"ev_out": ["ev_w_out"], "f0": ["ff_w_up0", "ff_w_down0"], "od": ["od_w_in", "od_w_out"],
              "f1": ["ff_w_up1", "ff_w_down1"]}
    started_gathers, token = _gather_start("gather_start", [[bufs[n] for n in g] for g in groups.values()])
    started_gathers = dict(zip(groups, started_gathers))
    w_full["norm_mix"] = w_full["norm_mix"] + token[0, 0]

    def fetch(tag, after):
        send_sems, recv_sems, group_bufs = started_gathers[tag]
        landed = _gather_wait("gather_wait_" + tag, send_sems, recv_sems, group_bufs, after)
        whole = _gather_weights("gather_siblings_" + tag, landed, from_chips=False)
        return {n: as_used(n, g) for n, g in zip(groups[tag], whole)}

    cm_idx = jnp.concatenate([c_idx, me_idx])
    started = []

    def start_reduction(tag, units):
        names = list(units)
        from_sibling = _halves_to_sibling(f"grads_to_sibling_{tag}", [units[n] for n in names])
        pairs = [_pair_add_placed(f"grads_pair_add_{n}", units[n], r, cm_idx, GRAD_WIRE_DTYPE) for n, r in zip(names, from_sibling)]
        send_sems, recv_sems, sums, zones, token = _scatter_start(
            f"grads_to_chips_start_{tag}", [p[0] for p in pairs], [p[1] for p in pairs])
        started.append((tag, names, send_sems, recv_sems, sums, zones))
        return token[0, 0]

    loss_local, grad_x, grads = _local_step(x[0], loss_target[0], w_full, start_reduction, fetch)
    loss = lax.psum(loss_local, ("x", "y", "c"))

    sg = _shard_grads(grads)
    small_rows = [jnp.concatenate([sg[n][k].reshape(-1) for n in SMALL] + [sg[n].reshape(-1) for n in REPLICATED])
                  for k in range(N_CHIPS)]
    n_el = small_rows[0].shape[0]
    n_rows = -(-n_el // (16 * PACK_W)) * 16
    small_unit = jnp.stack([jnp.pad(r, (0, n_rows * PACK_W - n_el)).reshape(n_rows, PACK_W) for r in small_rows])
    last = {"ev_w_out": grads["ev_w_out"].reshape(N_CHIPS, D_MODEL // N_CHIPS, D_MODEL), "ev_w_in": grads["ev_w_in"],
            "small": small_unit}
    from_sibling = _halves_to_sibling("grads_to_sibling_ev", list(last.values()))
    pairs = [_pair_add_placed(f"grads_pair_add_{n}", u, r, cm_idx, F32 if n == "small" else GRAD_WIRE_DTYPE)
             for (n, u), r in zip(last.items(), from_sibling)]
    ev_send, ev_recv, ev_sums, ev_zones, token = _scatter_start(
        "grads_to_chips_start_ev", [p[0] for p in pairs], [p[1] for p in pairs])
    dests = {"ev_w_in": ("ev_w_in", 0), "od_w_in": ("od_w_in", 0), "ev_w_out": ("ev_w_out", 0), "od_w_out": ("od_w_out", 0),
             "ff_w_up0": ("ff_w_up", 0), "ff_w_up1": ("ff_w_up", 1), "ff_w_down0": ("ff_w_down", 0),
             "ff_w_down1": ("ff_w_down", 1), "small": ("small", 0)}
    outs = {"grad": {}, "delta": {}, "new_m": {}, "new_v": {}}

    def finish(tag, from_chips, results):
        reduced = {}
        for n, part in from_chips.items():
            r, l = dests[n]
            reduced[r] = _sum_chips(f"grads_chip_sum_{n}", part, c_idx, l, 2 if r.startswith("ff_w") else 1,
                                    into=reduced.get(r))
        joined = dict(zip(results, _join_halves("grads_join_" + tag, [reduced[r] for r in results])))
        for n, g in joined.items():
            if n == "small":
                continue
            shape = wts[n].shape
            flat = lambda a: a.reshape(-1, shape[-1])
            new = _adamw("adamw_" + n, flat(wts[n]), flat(g), flat(mom[n]), flat(var[n]))
            for kind, arr in zip(("grad", "delta", "new_m", "new_v"), (g,) + tuple(new)):
                outs[kind][n] = arr.reshape(shape)
        return joined

    from_chips = {}
    for tag, names, send_sems, recv_sems, sums, zones in started:
        from_chips.update(zip(names, _scatter_wait(f"grads_to_chips_wait_{tag}", send_sems, recv_sems, sums, zones, token)))
    finish("layers", from_chips, ["od_w_in", "od_w_out", "ff_w_up", "ff_w_down"])
    from_chips = dict(zip(last, _scatter_wait("grads_to_chips_wait_ev", ev_send, ev_recv, ev_sums, ev_zones,
                                              outs["delta"]["ff_w_up"])))
    joined = finish("ev", from_chips, ["ev_w_in", "ev_w_out", "small"])

    order = SMALL + REPLICATED
    packed = lambda d: jnp.pad(jnp.concatenate([d[n].reshape(-1) for n in order]),
                               (0, n_rows * PACK_W - n_el)).reshape(n_rows, PACK_W)
    g_small = joined["small"].reshape(n_rows, PACK_W)
    new = _adamw("adamw_small", packed(wts), g_small, packed(mom), packed(var))
    for tag, arr in zip(("grad", "delta", "new_m", "new_v"), (g_small,) + tuple(new)):
        outs[tag].update(zip(order, _unpack(arr.reshape(-1), [wts[n].shape for n in order])))
    return (loss, grad_x[None], *[outs["grad"][n] for n in WEIGHTS], *[outs["delta"][n] for n in WEIGHTS],
            *[outs["new_m"][n] for n in WEIGHTS], *[outs["new_v"][n] for n in WEIGHTS])
```

```python
import functools

import jax
import jax.numpy as jnp
from jax import lax
from jax.experimental import pallas as pl
from jax.experimental.pallas import tpu as pltpu

F32 = jnp.float32
BF16 = jnp.bfloat16
MXU_DTYPE = BF16
GRAD_WIRE_DTYPE = BF16
FFN_HIDDEN_DTYPE = BF16

D_MODEL = 1024
N_META = 16
RMS_EPS = 1e-6
LN_EPS = 1e-5
D_A = 512
CONV_A_WIDTH = 31
CONV_B_WIDTH = 3
HEAD_DIM = 64
N_Q_HEADS = 8
N_KV_HEADS = 2
GQA_GROUP = 4
D_ATT = 512
D_KV = 128
BLOCK = 128
ROPE_THETA = 10000.0
D_R = 512
N_R_HEADS = 8
LORA_W = 64
LORA_A = 64
LORA_G = 128
RWKV_GN_EPS = 64e-5
RWKV_COLS = 3 * D_R + LORA_W + LORA_A + LORA_G
D_FF = 2816
NEG_INF = -1e30
ADAM_LR = 0.001
ADAM_B1 = 0.9
ADAM_B2 = 0.999
ADAM_EPS = 1e-08
ADAM_WD = 0.01
ADAM_STEP = 10

N_CHIPS = 4
LANES = 128
CONV_PAD = 32
VMEM_LIMIT_V7X = 56 * 1024 * 1024
MESH = pl.DeviceIdType.MESH


def _cparams(sem=None):
    return pltpu.CompilerParams(dimension_semantics=sem, vmem_limit_bytes=VMEM_LIMIT_V7X)


def _row_tile(t, cap):
    for d in range(min(t, cap), 0, -1):
        if t % d == 0 and d % 16 == 0:
            return d
    return t


def _chunk_len(t):
    for d in (64, 48, 32, 16, 8):
        if t % d == 0:
            return d
    raise ValueError(t)


def _call(fn, name, grid, ins, outs, acc_axis=None, sem=None):
    n_in, n_out = len(ins), len(outs)
    dtype = lambda o: o[4] if len(o) > 4 else F32

    def body(*refs):
        vals = fn(*[r[...] for r in refs[:n_in]])
        if not isinstance(vals, (tuple, list)):
            vals = (vals,)
        for r, v, o in zip(refs[n_in:n_in + n_out], vals, outs):
            if o[3]:
                first = pl.program_id(acc_axis) == 0

                @pl.when(first)
                def _(r=r, v=v):
                    r[...] = v

                @pl.when(jnp.logical_not(first))
                def _(r=r, v=v):
                    r[...] += v
            else:
                r[...] = v.astype(dtype(o))

    res = pl.pallas_call(
        body, name=name, grid=grid,
        in_specs=[pl.BlockSpec(b, m) for _, b, m in ins],
        out_specs=[pl.BlockSpec(o[1], o[2]) for o in outs],
        out_shape=[jax.ShapeDtypeStruct(o[0], dtype(o)) for o in outs],
        compiler_params=_cparams(sem),
    )(*[a for a, _, _ in ins])
    return res if n_out > 1 else res[0]


def _matmul(name, a, b, *, dims, grid, a_spec, b_spec, o_shape, o_spec, acc_shape, nk, k_axis,
            add=None, add_spec=None, out_dtype=F32):
    def product(a_ref, b_ref):
        return lax.dot_general(a_ref[...].astype(MXU_DTYPE), b_ref[...].astype(MXU_DTYPE), dims, preferred_element_type=F32)

    def body_single(*refs):
        a_ref, b_ref, o_ref = refs[0], refs[1], refs[-1]
        res = product(a_ref, b_ref) if add is None else product(a_ref, b_ref) + refs[2][...]
        o_ref[...] = res.astype(out_dtype)

    def body_steps(*refs):
        a_ref, b_ref, o_ref, acc = refs[0], refs[1], refs[-2], refs[-1]
        k = pl.program_id(k_axis)

        @pl.when(k == 0)
        def _():
            if add is None:
                acc[...] = jnp.zeros(acc.shape, F32)
            else:
                acc[...] = refs[2][...]

        acc[...] += product(a_ref, b_ref)

        @pl.when(k == nk - 1)
        def _():
            o_ref[...] = acc[...].astype(out_dtype)

    args = [a, b] + ([] if add is None else [add])
    specs = [a_spec, b_spec] + ([] if add is None else [add_spec])
    return pl.pallas_call(
        body_single if nk == 1 else body_steps, name=name, grid=grid, in_specs=specs, out_specs=o_spec,
        out_shape=jax.ShapeDtypeStruct(o_shape, out_dtype),
        scratch_shapes=[] if nk == 1 else [pltpu.VMEM(acc_shape, F32)],
        compiler_params=_cparams(None),
    )(*args)


MATMUL_BLOCKS_BYTES = 46 * 1024 * 1024


def _whole_if_fits(t, tile, need_bytes):
    return t if need_bytes <= MATMUL_BLOCKS_BYTES else tile


_NN = (((1,), (0,)), ((), ()))
_NT = (((1,), (1,)), ((), ()))
_TN = (((0,), (0,)), ((), ()))


def _mm_cs(name, x, wg, l, tm, out_dtype=F32):
    t, k = x.shape
    s, _, _, n = wg.shape
    tm = _whole_if_fits(t, tm, 2 * (t * k * x.dtype.itemsize + k * n * wg.dtype.itemsize + t * n * 4))
    return _matmul(name, x, wg, dims=_NN, grid=(s, t // tm, 1),
                   a_spec=pl.BlockSpec((tm, k), lambda j, i, kk: (i, 0)),
                   b_spec=pl.BlockSpec((None, None, k, n), lambda j, i, kk: (j, l, 0, 0)),
                   o_shape=(t, s * n), o_spec=pl.BlockSpec((tm, n), lambda j, i, kk: (i, j)),
                   acc_shape=(tm, n), nk=1, k_axis=2, out_dtype=out_dtype)


def _mm_full(name, x, w, l, tm, tk, add=None):
    t, k = x.shape
    n = w.shape[2]
    nk = k // tk
    tm = _whole_if_fits(t, tm, 2 * (t * tk * x.dtype.itemsize + tk * n * w.dtype.itemsize + t * n * 4 * (1 if add is None else 2))
                        + (t * n * 4 if nk > 1 else 0))
    return _matmul(name, x, w, dims=_NN, grid=(t // tm, 1, nk),
                   a_spec=pl.BlockSpec((tm, tk), lambda i, j, kk: (i, kk)),
                   b_spec=pl.BlockSpec((None, tk, n), lambda i, j, kk: (l, kk, 0)),
                   o_shape=(t, n), o_spec=pl.BlockSpec((tm, n), lambda i, j, kk: (i, 0)),
                   acc_shape=(tm, n), nk=nk, k_axis=2,
                   add=add, add_spec=pl.BlockSpec((tm, n), lambda i, j, kk: (i, 0)))


def _mm_nt_cs(name, dy, wg, l, tm, add=None):
    t = dy.shape[0]
    s, _, k, n = wg.shape
    tm = _whole_if_fits(t, tm, 2 * (t * n * dy.dtype.itemsize + k * n * wg.dtype.itemsize + t * k * 4 * (1 if add is None else 2))
                        + t * k * 4)
    return _matmul(name, dy, wg, dims=_NT, grid=(t // tm, 1, s),
                   a_spec=pl.BlockSpec((tm, n), lambda i, j, kk: (i, kk)),
                   b_spec=pl.BlockSpec((None, None, k, n), lambda i, j, kk: (kk, l, 0, 0)),
                   o_shape=(t, k), o_spec=pl.BlockSpec((tm, k), lambda i, j, kk: (i, 0)),
                   acc_shape=(tm, k), nk=s, k_axis=2,
                   add=add, add_spec=pl.BlockSpec((tm, k), lambda i, j, kk: (i, 0)))


def _mm_nt_full(name, dy, w, l, tm, tko):
    t, n = dy.shape
    k = w.shape[1]
    tm = _whole_if_fits(t, tm, 2 * (t * n * dy.dtype.itemsize + tko * n * w.dtype.itemsize + t * tko * 4))
    return _matmul(name, dy, w, dims=_NT, grid=(t // tm, k // tko, 1),
                   a_spec=pl.BlockSpec((tm, n), lambda i, j, kk: (i, 0)),
                   b_spec=pl.BlockSpec((None, tko, n), lambda i, j, kk: (l, j, 0)),
                   o_shape=(t, k), o_spec=pl.BlockSpec((tm, tko), lambda i, j, kk: (i, j)),
                   acc_shape=(tm, tko), nk=1, k_axis=2)


def _mm_tn_cs(name, x, dy, s, tk):
    t, k = x.shape
    n = dy.shape[1] // s
    tk = _whole_if_fits(t, tk, 2 * (t * k * x.dtype.itemsize + t * n * dy.dtype.itemsize + k * n * 4))
    nk = t // tk
    return _matmul(name, x, dy, dims=_TN, grid=(s, 1, nk),
                   a_spec=pl.BlockSpec((tk, k), lambda j, i, kk: (kk, 0)),
                   b_spec=pl.BlockSpec((tk, n), lambda j, i, kk: (kk, j)),
                   o_shape=(s, k, n), o_spec=pl.BlockSpec((None, k, n), lambda j, i, kk: (j, 0, 0)),
                   acc_shape=(k, n), nk=nk, k_axis=2)


def _mm_tn_full(name, y, dh, tk, tko):
    t, k = y.shape
    n = dh.shape[1]
    tk = _whole_if_fits(t, tk, 2 * (t * tko * y.dtype.itemsize + t * n * dh.dtype.itemsize + tko * n * 4))
    nk = t // tk
    return _matmul(name, y, dh, dims=_TN, grid=(k // tko, 1, nk),
                   a_spec=pl.BlockSpec((tk, tko), lambda j, i, kk: (kk, j)),
                   b_spec=pl.BlockSpec((tk, n), lambda j, i, kk: (kk, 0)),
                   o_shape=(k, n), o_spec=pl.BlockSpec((tko, n), lambda j, i, kk: (j, 0)),
                   acc_shape=(tko, n), nk=nk, k_axis=2)


def _sigmoid(x):
    return 1.0 / (1.0 + jnp.exp(-x))


def _rms_fwd(name, h, g, tr):
    t, d = h.shape

    def fn(hv, gv):
        r = lax.rsqrt(jnp.mean(hv * hv, axis=-1, keepdims=True) + RMS_EPS)
        return hv * r * gv

    return _call(fn, name, (t // tr,), [(h, (tr, d), lambda i: (i, 0)), (g, (1, d), lambda i: (0, 0))],
                 [((t, d), (tr, d), lambda i: (i, 0), False, MXU_DTYPE)])


def _rms_bwd(name, h, g, dhn, dh, tr):
    t, d = h.shape

    def fn(hv, gv, dy, dh_in):
        r = lax.rsqrt(jnp.mean(hv * hv, axis=-1, keepdims=True) + RMS_EPS)
        xh = hv * r
        dg = jnp.sum(dy * xh, axis=0, keepdims=True)
        dxh = dy * gv
        dx = r * (dxh - xh * jnp.mean(dxh * xh, axis=-1, keepdims=True))
        return dh_in + dx, dg

    row = lambda i: (i, 0)
    return _call(fn, name, (t // tr,),
                 [(h, (tr, d), row), (g, (1, d), lambda i: (0, 0)), (dhn, (tr, d), row), (dh, (tr, d), row)],
                 [((t, d), (tr, d), row, False), ((1, d), (1, d), lambda i: (0, 0), True)], acc_axis=0)


def _final_loss(name, h, g, tgt, tr):
    t, d = h.shape

    def fn(hv, gv, tv):
        r = lax.rsqrt(jnp.mean(hv * hv, axis=-1, keepdims=True) + RMS_EPS)
        xh = hv * r
        row = pl.program_id(0) * tr + lax.broadcasted_iota(jnp.int32, (tr, 1), 0)
        e = jnp.where(row >= N_META, xh * gv - tv, 0.0)
        loss = jnp.broadcast_to(0.5 * jnp.sum(jnp.sum(e * e, axis=-1, keepdims=True), axis=0, keepdims=True) / d,
                                (8, LANES))
        dout = e / d
        dg = jnp.sum(dout * xh, axis=0, keepdims=True)
        dxh = dout * gv
        dx = r * (dxh - xh * jnp.mean(dxh * xh, axis=-1, keepdims=True))
        return loss, dx, dg

    row = lambda i: (i, 0)
    fix = lambda i: (0, 0)
    return _call(fn, name, (t // tr,), [(h, (tr, d), row), (g, (1, d), fix), (tgt, (tr, d), row)],
                 [((8, LANES), (8, LANES), fix, True), ((t, d), (tr, d), row, False), ((1, d), (1, d), fix, True)],
                 acc_axis=0)


def _silu_ln(uc, g, b):
    mu = jnp.mean(uc, axis=-1, keepdims=True)
    xc = uc - mu
    rs = lax.rsqrt(jnp.mean(xc * xc, axis=-1, keepdims=True) + LN_EPS)
    ln = xc * rs * g + b
    return ln * _sigmoid(ln)


def _even_ln_fwd(name, uc, g, b, tr):
    t, d = uc.shape
    row, fix = (lambda i: (i, 0)), (lambda i: (0, 0))
    return _call(_silu_ln, name, (t // tr,), [(uc, (tr, d), row), (g, (1, d), fix), (b, (1, d), fix)],
                 [((t, d), (tr, d), row, False, MXU_DTYPE)])


def _even_ln_bwd(name, uc, g, b, dy, dy_col, tr):
    t, d = uc.shape

    def fn(ucv, gv, bv, dyv):
        mu = jnp.mean(ucv, axis=-1, keepdims=True)
        xc = ucv - mu
        rs = lax.rsqrt(jnp.mean(xc * xc, axis=-1, keepdims=True) + LN_EPS)
        xh = xc * rs
        ln = xh * gv + bv
        s = _sigmoid(ln)
        dln = dyv * (s * (1.0 + ln * (1.0 - s)))
        dg = jnp.sum(dln * xh, axis=0, keepdims=True)
        db = jnp.sum(dln, axis=0, keepdims=True)
        dxh = dln * gv
        duc = rs * (dxh - jnp.mean(dxh, axis=-1, keepdims=True) - xh * jnp.mean(dxh * xh, axis=-1, keepdims=True))
        return duc, dg, db

    row, fix = (lambda i: (i, 0)), (lambda i: (0, 0))
    return _call(fn, name, (t // tr,),
                 [(uc, (tr, d), row), (g, (1, d), fix), (b, (1, d), fix), (dy, (tr, d), lambda i: (i, dy_col))],
                 [((t, d), (tr, d), row, False), ((1, d), (1, d), fix, True), ((1, d), (1, d), fix, True)], acc_axis=0)


def _windows(t):
    rc = _chunk_len(t)
    return [(r0, rc) for r0 in range(0, t, rc)]


def _taps(w_ref, width):
    return [w_ref[pl.ds(j, 1), :] for j in range(width)]


def _conv_at(xp, taps, r0, rc):
    width = len(taps)
    acc = None
    for j in range(width):
        term = xp[pl.ds(CONV_PAD - (width - 1) + j + r0, rc), :] * taps[j]
        acc = term if acc is None else acc + term
    return acc


def _conv_bwd_in_at(dyp, taps, r0, rc):
    width = len(taps)
    acc = None
    for j in range(width):
        term = dyp[pl.ds(width - 1 - j + r0, rc), :] * taps[j]
        acc = term if acc is None else acc + term
    return acc


def _fold(x):
    acc = x[0:8]
    for i in range(1, x.shape[0] // 8):
        acc = acc + x[8 * i:8 * (i + 1)]
    return acc


def _add_to(accs, vals):
    return vals if accs is None else [a + v for a, v in zip(accs, vals)]


def _conv_bwd_w_at(dy, xp, width, r0, rc):
    return [_fold(dy * xp[pl.ds(CONV_PAD - (width - 1) + j + r0, rc), :]) for j in range(width)]


def _store_taps(dw_ref, accs):
    for j, a in enumerate(accs):
        dw_ref[pl.ds(j, 1), :] = jnp.sum(a, axis=0, keepdims=True)


WIDE_COLS = 2 * LANES


def _zero_front(xp):
    xp[pl.ds(0, CONV_PAD), :] = jnp.zeros((CONV_PAD, xp.shape[1]), F32)


def _zero_back(dyp, t):
    dyp[pl.ds(t, CONV_PAD), :] = jnp.zeros((CONV_PAD, dyp.shape[1]), F32)


def _col_call(body, name, ncol, ins, outs, t, n_scratch, cols=LANES):
    def spec(rows, off):
        return pl.BlockSpec((rows, cols), lambda j, off=off: (0, j + off))

    res = pl.pallas_call(
        body, name=name, grid=(ncol,),
        in_specs=[spec(r, off) for _, r, off in ins],
        out_specs=[spec(o[0], 0) for o in outs],
        out_shape=[jax.ShapeDtypeStruct(o[:2], o[2] if len(o) > 2 else F32) for o in outs],
        scratch_shapes=[pltpu.VMEM((t + CONV_PAD, cols), F32) for _ in range(n_scratch)],
        compiler_params=_cparams(None),
    )(*[a for a, _, _ in ins])
    return res


def _even_col_fwd(name, p, conv_a, conv_b):
    t = p.shape[0]
    nc = D_A // LANES
    wins = _windows(t)

    def body(av, ag, gb, gc, xi, ca, cb, uc_ref, yb_ref, xp):
        _zero_front(xp)
        for r0, rc in wins:
            rows = pl.ds(r0, rc)
            xp[pl.ds(CONV_PAD + r0, rc), :] = av[rows, :] * _sigmoid(ag[rows, :])
        taps = _taps(ca, CONV_A_WIDTH)
        for r0, rc in wins:
            uc_ref[pl.ds(r0, rc), :] = _conv_at(xp, taps, r0, rc)
        for r0, rc in wins:
            rows = pl.ds(r0, rc)
            xp[pl.ds(CONV_PAD + r0, rc), :] = gc[rows, :] * xi[rows, :]
        taps = _taps(cb, CONV_B_WIDTH)
        for r0, rc in wins:
            rows = pl.ds(r0, rc)
            yb_ref[rows, :] = (gb[rows, :] * _conv_at(xp, taps, r0, rc)).astype(yb_ref.dtype)

    ins = [(p, t, k * nc) for k in range(5)] + [(conv_a, CONV_A_WIDTH, 0), (conv_b, CONV_B_WIDTH, 0)]
    return _col_call(body, name, nc, ins, [(t, D_A), (t, D_A, MXU_DTYPE)], t, 1)


def _even_col_bwd(name, p, duc, dy, conv_a, conv_b):
    t = p.shape[0]
    nc = D_A // LANES
    wins = _windows(t)

    def body(av, ag, gb, gc, xi, duc_ref, dyb_ref, ca, cb, dav, dag, dgb, dgc, dxi, dca, dcb, xp, dyp):
        _zero_front(xp)
        _zero_back(dyp, t)
        for r0, rc in wins:
            rows = pl.ds(r0, rc)
            xp[pl.ds(CONV_PAD + r0, rc), :] = av[rows, :] * _sigmoid(ag[rows, :])
            dyp[rows, :] = duc_ref[rows, :]
        taps = _taps(ca, CONV_A_WIDTH)
        accs = None
        for r0, rc in wins:
            rows = pl.ds(r0, rc)
            accs = _add_to(accs, _conv_bwd_w_at(duc_ref[rows, :], xp, CONV_A_WIDTH, r0, rc))
            du = _conv_bwd_in_at(dyp, taps, r0, rc)
            sig = _sigmoid(ag[rows, :])
            dav[rows, :] = (du * sig).astype(dav.dtype)
            dag[rows, :] = (du * av[rows, :] * sig * (1.0 - sig)).astype(dag.dtype)
        _store_taps(dca, accs)
        for r0, rc in wins:
            rows = pl.ds(r0, rc)
            xp[pl.ds(CONV_PAD + r0, rc), :] = gc[rows, :] * xi[rows, :]
        taps = _taps(cb, CONV_B_WIDTH)
        accs = None
        for r0, rc in wins:
            rows = pl.ds(r0, rc)
            dgb[rows, :] = (dyb_ref[rows, :] * _conv_at(xp, taps, r0, rc)).astype(dgb.dtype)
            dzc = dyb_ref[rows, :] * gb[rows, :]
            dyp[rows, :] = dzc
            accs = _add_to(accs, _conv_bwd_w_at(dzc, xp, CONV_B_WIDTH, r0, rc))
        _store_taps(dcb, accs)
        for r0, rc in wins:
            rows = pl.ds(r0, rc)
            dz = _conv_bwd_in_at(dyp, taps, r0, rc)
            dgc[rows, :] = (dz * xi[rows, :]).astype(dgc.dtype)
            dxi[rows, :] = (dz * gc[rows, :]).astype(dxi.dtype)

    ins = ([(p, t, k * nc) for k in range(5)] + [(duc, t, 0), (dy, t, nc)]
           + [(conv_a, CONV_A_WIDTH, 0), (conv_b, CONV_B_WIDTH, 0)])
    outs = [(t, D_A, MXU_DTYPE)] * 5 + [(CONV_A_WIDTH, D_A), (CONV_B_WIDTH, D_A)]
    return _col_call(body, name, nc, ins, outs, t, 2)


def _ffn_col_fwd(name, u, conv, bias):
    t = u.shape[0]
    nc = D_FF // WIDE_COLS
    wins = _windows(t)

    def body(g_ref, v_ref, cw, b_ref, a_ref, xp):
        _zero_front(xp)
        xp[pl.ds(CONV_PAD, t), :] = g_ref[...].astype(F32)
        taps = _taps(cw, CONV_B_WIDTH)
        b = b_ref[...]
        for r0, rc in wins:
            rows = pl.ds(r0, rc)
            gc = _conv_at(xp, taps, r0, rc) + b
            a_ref[rows, :] = (gc * _sigmoid(gc) * v_ref[rows, :].astype(F32)).astype(a_ref.dtype)

    ins = [(u, t, 0), (u, t, nc), (conv, CONV_B_WIDTH, 0), (bias, 1, 0)]
    return _col_call(body, name, nc, ins, [(t, D_FF, MXU_DTYPE)], t, 1, cols=WIDE_COLS)[0]


def _ffn_col_bwd(name, u, da, conv, bias):
    t = u.shape[0]
    nc = D_FF // LANES
    wins = _windows(t)

    def body(g_ref, v_ref, da_ref, cw, b_ref, du_ref, dcw, db_ref, xp, dyp, dval):
        @pl.when(pl.program_id(1) == 0)
        def _():
            _zero_front(xp)
            _zero_back(dyp, t)
            xp[pl.ds(CONV_PAD, t), :] = g_ref[...].astype(F32)
            taps = _taps(cw, CONV_B_WIDTH)
            b = b_ref[...]
            accs, bias_acc = None, None
            for r0, rc in wins:
                rows = pl.ds(r0, rc)
                gc = _conv_at(xp, taps, r0, rc) + b
                s = _sigmoid(gc)
                d = da_ref[rows, :]
                dval[rows, :] = d * gc * s
                dgc = d * v_ref[rows, :].astype(F32) * (s * (1.0 + gc * (1.0 - s)))
                dyp[rows, :] = dgc
                bias_acc = _add_to(bias_acc, [_fold(dgc)])
                accs = _add_to(accs, _conv_bwd_w_at(dgc, xp, CONV_B_WIDTH, r0, rc))
            db_ref[...] = jnp.sum(bias_acc[0], axis=0, keepdims=True)
            _store_taps(dcw, accs)
            for r0, rc in wins:
                du_ref[pl.ds(r0, rc), :] = _conv_bwd_in_at(dyp, taps, r0, rc).astype(du_ref.dtype)

        @pl.when(pl.program_id(1) == 1)
        def _():
            du_ref[...] = dval[...].astype(du_ref.dtype)

    col = lambda rows, off: pl.BlockSpec((rows, LANES), lambda j, p: (0, j + off))
    return pl.pallas_call(
        body, name=name, grid=(nc, 2),
        in_specs=[col(t, 0), col(t, nc), col(t, 0), col(CONV_B_WIDTH, 0), col(1, 0)],
        out_specs=[pl.BlockSpec((t, LANES), lambda j, p: (0, j + nc * p)), col(CONV_B_WIDTH, 0), col(1, 0)],
        out_shape=[jax.ShapeDtypeStruct((t, 2 * D_FF), MXU_DTYPE), jax.ShapeDtypeStruct((CONV_B_WIDTH, D_FF), F32),
                   jax.ShapeDtypeStruct((1, D_FF), F32)],
        scratch_shapes=[pltpu.VMEM((t + CONV_PAD, LANES), F32) for _ in range(2)] + [pltpu.VMEM((t, LANES), F32)],
        compiler_params=_cparams(None),
    )(u, u, da, conv, bias)


def _shift_fwd(name, p, col0, mu):
    t = p.shape[0]
    wins = _windows(t)

    def body(x_ref, mu_ref, o_ref, xp):
        _zero_front(xp)
        xp[pl.ds(CONV_PAD, t), :] = x_ref[...]
        mu_v = mu_ref[...]
        for r0, rc in wins:
            rows = pl.ds(r0, rc)
            x = x_ref[rows, :]
            o_ref[rows, :] = x + (xp[pl.ds(CONV_PAD - 1 + r0, rc), :] - x) * mu_v

    return _col_call(body, name, RWKV_COLS // WIDE_COLS, [(p, t, col0 // WIDE_COLS), (mu, 1, 0)], [(t, RWKV_COLS)], t, 1,
                     cols=WIDE_COLS)[0]


def _shift_bwd(name, p, col0, mu, dprs):
    t = p.shape[0]
    wins = _windows(t)

    def body(x_ref, mu_ref, d_ref, dx_ref, dmu_ref, xp, dyp):
        _zero_front(xp)
        _zero_back(dyp, t)
        xp[pl.ds(CONV_PAD, t), :] = x_ref[...]
        mu_v = mu_ref[...]
        acc = None
        for r0, rc in wins:
            rows = pl.ds(r0, rc)
            d = d_ref[rows, :]
            acc = _add_to(acc, [_fold(d * (xp[pl.ds(CONV_PAD - 1 + r0, rc), :] - x_ref[rows, :]))])
            dyp[rows, :] = d * mu_v
        dmu_ref[...] = jnp.sum(acc[0], axis=0, keepdims=True)
        for r0, rc in wins:
            rows = pl.ds(r0, rc)
            dx_ref[rows, :] = d_ref[rows, :] - dyp[rows, :] + dyp[pl.ds(1 + r0, rc), :]

    ins = [(p, t, col0 // WIDE_COLS), (mu, 1, 0), (dprs, t, 0)]
    return _col_call(body, name, RWKV_COLS // WIDE_COLS, ins, [(t, RWKV_COLS), (1, RWKV_COLS)], t, 2, cols=WIDE_COLS)


def _hi_lo(x):
    hi = x.astype(BF16)
    return hi, (x - hi.astype(F32)).astype(BF16)


def _dot_passes(a, b, dims, passes):
    d = lambda p, q: lax.dot_general(p, q, dims, preferred_element_type=F32)
    if passes == 1:
        return d(a.astype(MXU_DTYPE), b.astype(MXU_DTYPE))
    ah, al = _hi_lo(a)
    bh, bl = _hi_lo(b)
    return d(ah, bh) + (d(ah, bl) + d(al, bh))


@functools.partial(jax.custom_vjp, nondiff_argnums=(2, 3))
def _dot_vjp(a, b, dims, passes):
    return _dot_passes(a, b, dims, passes)


def _dot_fwd(a, b, dims, passes):
    return _dot_passes(a, b, dims, passes), (a, b)


def _dot_bwd(dims, passes, res, g):
    a, b = res
    if dims == _NN:
        return _dot_passes(g, b, _NT, passes), _dot_passes(a, g, _TN, passes)
    if dims == _NT:
        return _dot_passes(g, b, _NN, passes), _dot_passes(g, a, _TN, passes)
    return _dot_passes(b, g, _NT, passes), _dot_passes(a, g, _NN, passes)


_dot_vjp.defvjp(_dot_fwd, _dot_bwd)


def _doth(a, b, dims=_NN):
    return _dot_vjp(a, b, dims, 3)


def _dotb(a, b, dims=_NN):
    return _dot_vjp(a, b, dims, 1)


def _softplus(x):
    return jnp.where(x > 0, x, 0.0) + jnp.log(1.0 + jnp.exp(jnp.where(x > 0, -x, x)))


def _rwkv_pre(k, xl, gd, w0, w2p, a0, a2p, g2, k_k, k_a, seg):
    z = w0 + _dotb(jnp.tanh(xl), w2p)
    lw = -jnp.exp(-_softplus(-z) - 0.5)
    alpha = _sigmoid(a0 + _dotb(xl, a2p))
    g = _dotb(_sigmoid(gd), g2)
    kk = k * k_k
    kk = kk / jnp.maximum(jnp.sqrt(_dotb(kk * kk, seg)), 1e-12)
    k2 = k * (1.0 + (alpha - 1.0) * k_a)
    return lw, k2, -kk, kk * alpha, g


def _rwkv_post(y, r, k2, v, g, lnx_g, lnx_b, r_k, seg):
    mean = _dotb(y, seg) * (1.0 / HEAD_DIM)
    yc = y - mean
    var = _dotb(yc * yc, seg) * (1.0 / HEAD_DIM)
    yo = yc * lax.rsqrt(var + RWKV_GN_EPS) * lnx_g + lnx_b
    bonus = _dotb(r * k2 * r_k, seg) * v
    return (yo + bonus) * g


def _rwkv_pre_fwd(name, prs, prm, seg, tr):
    t = prs.shape[0]
    row = lambda i: (i, 0)
    fix = lambda i: (0, 0)
    ins = [(prs, (tr, D_R), lambda i: (i, 1)), (prs, (tr, LANES), lambda i: (i, 12)), (prs, (tr, LANES), lambda i: (i, 13)),
           (prm["w0"], (1, D_R), fix), (prm["w2p"], (LANES, D_R), fix), (prm["a0"], (1, D_R), fix),
           (prm["a2p"], (LANES, D_R), fix), (prm["g2"], (LANES, D_R), fix), (prm["k_k"], (1, D_R), fix),
           (prm["k_a"], (1, D_R), fix), (seg, (D_R, D_R), fix)]
    return _call(_rwkv_pre, name, (t // tr,), ins, [((t, D_R), (tr, D_R), row, False)] * 5)


def _rwkv_pre_bwd(name, prs, prm, seg, cts, tr):
    t = prs.shape[0]

    def fn(k, xl, gd, w0, w2p, a0, a2p, g2, k_k, k_a, segv, *ct):
        _, vjp = jax.vjp(lambda *a: _rwkv_pre(*a, segv), k, xl, gd, w0, w2p, a0, a2p, g2, k_k, k_a)
        return vjp(tuple(ct))

    row = lambda i: (i, 0)
    fix = lambda i: (0, 0)
    ins = [(prs, (tr, D_R), lambda i: (i, 1)), (prs, (tr, LANES), lambda i: (i, 12)), (prs, (tr, LANES), lambda i: (i, 13)),
           (prm["w0"], (1, D_R), fix), (prm["w2p"], (LANES, D_R), fix), (prm["a0"], (1, D_R), fix),
           (prm["a2p"], (LANES, D_R), fix), (prm["g2"], (LANES, D_R), fix), (prm["k_k"], (1, D_R), fix),
           (prm["k_a"], (1, D_R), fix), (seg, (D_R, D_R), fix)] + [(c, (tr, D_R), row) for c in cts]
    outs = [((t, D_R), (tr, D_R), row, False), ((t, LANES), (tr, LANES), row, False), ((t, LANES), (tr, LANES), row, False),
            ((1, D_R), (1, D_R), fix, True), ((LANES, D_R), (LANES, D_R), fix, True), ((1, D_R), (1, D_R), fix, True),
            ((LANES, D_R), (LANES, D_R), fix, True), ((LANES, D_R), (LANES, D_R), fix, True),
            ((1, D_R), (1, D_R), fix, True), ((1, D_R), (1, D_R), fix, True)]
    return _call(fn, name, (t // tr,), ins, outs, acc_axis=0)


def _rwkv_post_ins(y, prs, k2, g, prm, seg, tr):
    row = lambda i: (i, 0)
    fix = lambda i: (0, 0)
    return [(y, (tr, D_R), row), (prs, (tr, D_R), row), (k2, (tr, D_R), row), (prs, (tr, D_R), lambda i: (i, 2)),
            (g, (tr, D_R), row), (prm["lnx_g"], (1, D_R), fix), (prm["lnx_b"], (1, D_R), fix), (prm["r_k"], (1, D_R), fix),
            (seg, (D_R, D_R), fix)]


def _rwkv_post_fwd(name, y, prs, k2, g, prm, seg, tr):
    t = y.shape[0]
    return _call(_rwkv_post, name, (t // tr,), _rwkv_post_ins(y, prs, k2, g, prm, seg, tr),
                 [((t, D_R), (tr, D_R), lambda i: (i, 0), False)])


def _rwkv_post_bwd(name, y, prs, k2, g, prm, seg, dy, dy_col, tr):
    t = y.shape[0]

    def fn(yv, r, k2v, v, gv, lg, lb, rk, segv, ct):
        _, vjp = jax.vjp(lambda *a: _rwkv_post(*a, segv), yv, r, k2v, v, gv, lg, lb, rk)
        return vjp(ct)

    row = lambda i: (i, 0)
    fix = lambda i: (0, 0)
    ins = _rwkv_post_ins(y, prs, k2, g, prm, seg, tr) + [(dy, (tr, D_R), lambda i: (i, dy_col))]
    outs = [((t, D_R), (tr, D_R), row, False)] * 5 + [((1, D_R), (1, D_R), fix, True)] * 3
    return _call(fn, name, (t // tr,), ins, outs, acc_axis=0)


def _wkv_chunk(s0, r, lw, k, v, a, b):
    c = r[0].shape[0]
    lane = lax.broadcasted_iota(jnp.int32, (1, 2 * HEAD_DIM), 1)
    first = (lane < HEAD_DIM).astype(F32)
    per_head = lambda x: jnp.concatenate([x * first, x * (1.0 - first)], axis=0)

    def time_of(shape, dim):
        i = lax.broadcasted_iota(jnp.int32, shape, dim)
        return jnp.where(i >= c, i - c, i)

    incl = (lax.broadcasted_iota(jnp.int32, (c, c), 0) >= lax.broadcasted_iota(jnp.int32, (c, c), 1)).astype(F32)
    strict2 = time_of((2 * c, 2 * c), 0) > time_of((2 * c, 2 * c), 1)
    incl2 = lax.broadcasted_iota(jnp.int32, (c, 2 * c), 0) >= time_of((c, 2 * c), 1)
    each = lambda f, *xs: [f(*x) for x in zip(*xs)]
    cum = each(lambda x: _doth(incl, x), lw)
    tot = each(lambda x: jnp.sum(x, axis=0, keepdims=True), lw)
    e_inv = each(lambda x: jnp.exp(-x), cum)
    a_st = each(lambda x, cm, l: per_head(x * jnp.exp(cm - l)), a, cum, lw)
    r_t = each(lambda x, cm: x * jnp.exp(cm), r, cum)
    b_st = each(lambda x, e: per_head(x * e), b, e_inv)
    k_st = each(lambda x, e: per_head(x * e), k, e_inv)
    v_st = each(per_head, v)
    m = each(lambda x, w: jnp.where(strict2, _dotb(x, w, _NT), 0.0), a_st, b_st)
    m_k = each(lambda x, w: jnp.where(strict2, _dotb(x, w, _NT), 0.0), a_st, k_st)
    u = each(lambda x, s, mk, w: _dotb(x, s, _NT) + _dotb(mk, w), a_st, s0, m_k, v_st)
    steps = (c - 1).bit_length()
    for s in range(steps):
        u = each(lambda x, w: x + _dotb(w, x), u, m)
        if s + 1 < steps:
            m = each(lambda w: _dotb(w, w), m)
    n_b = each(lambda x, w: jnp.where(incl2, _dotb(x, w, _NT), 0.0), r_t, b_st)
    n_k = each(lambda x, w: jnp.where(incl2, _dotb(x, w, _NT), 0.0), r_t, k_st)
    y = each(lambda x, s, nb, uu, nk, w: _dotb(x, s, _NT) + _dotb(nb, uu) + _dotb(nk, w), r_t, s0, n_b, u, n_k, v_st)
    dec = each(lambda tt, cm: jnp.exp(tt - cm), tot, cum)
    s1 = each(lambda s, tt, uu, x, d, w, kk: s * jnp.exp(tt) + _dotb(uu, per_head(x * d), _TN) + _dotb(w, per_head(kk * d), _TN),
              s0, tot, u, b, dec, v_st, k)
    return tuple(y), tuple(s1)


WKV_PAIRS_PER_STEP = 4
PAIR = 2 * HEAD_DIM


def _wkv_fwd(name, srcs):
    t = srcs[0][0].shape[0]
    c = _chunk_len(t)
    nc = t // c
    pp = WKV_PAIRS_PER_STEP
    n_pairs = D_R // PAIR

    def body(r, lw, k, v, a, b, y_ref, st_ref, state):
        @pl.when(pl.program_id(1) == 0)
        def _():
            state[...] = jnp.zeros(state.shape, F32)

        pairs = lambda ref: tuple(ref[:, pl.ds(i * PAIR, PAIR)] for i in range(pp))
        s0 = tuple(state[i] for i in range(pp))
        y, s1 = _wkv_chunk(s0, pairs(r), pairs(lw), pairs(k), pairs(v), pairs(a), pairs(b))
        for i in range(pp):
            st_ref[i] = s0[i]
            y_ref[:, pl.ds(i * PAIR, PAIR)] = y[i]
            state[i] = s1[i]

    seq = lambda off: pl.BlockSpec((c, pp * PAIR), lambda g, j: (j, off + g))
    return pl.pallas_call(
        body, name=name, grid=(n_pairs // pp, nc), in_specs=[seq(off) for _, off in srcs],
        out_specs=[seq(0), pl.BlockSpec((pp, None, PAIR, PAIR), lambda g, j: (g, j, 0, 0))],
        out_shape=[jax.ShapeDtypeStruct((t, D_R), F32), jax.ShapeDtypeStruct((n_pairs, nc, PAIR, PAIR), F32)],
        scratch_shapes=[pltpu.VMEM((pp, PAIR, PAIR), F32)],
        compiler_params=_cparams(None),
    )(*[a for a, _ in srcs])


def _wkv_bwd(name, srcs, st, dy):
    t = srcs[0][0].shape[0]
    c = _chunk_len(t)
    nc = t // c
    pp = WKV_PAIRS_PER_STEP
    n_pairs = D_R // PAIR

    def body(r, lw, k, v, a, b, st_ref, dy_ref, dr, dlw, dk, dv, da, db, dstate):
        @pl.when(pl.program_id(1) == 0)
        def _():
            dstate[...] = jnp.zeros(dstate.shape, F32)

        half = lax.broadcasted_iota(jnp.int32, (PAIR, PAIR), 0) < HEAD_DIM
        same_head = half == (lax.broadcasted_iota(jnp.int32, (PAIR, PAIR), 1) < HEAD_DIM)
        pairs = lambda ref: tuple(ref[:, pl.ds(i * PAIR, PAIR)] for i in range(pp))
        s0 = tuple(st_ref[i] for i in range(pp))
        _, vjp = jax.vjp(_wkv_chunk, s0, pairs(r), pairs(lw), pairs(k), pairs(v), pairs(a), pairs(b))
        ds0, *dxs = vjp((pairs(dy_ref), tuple(dstate[i] for i in range(pp))))
        for i in range(pp):
            for ref, val in zip((dr, dlw, dk, dv, da, db), dxs):
                ref[:, pl.ds(i * PAIR, PAIR)] = val[i]
            dstate[i] = jnp.where(same_head, ds0[i], 0.0)

    seq = lambda off: pl.BlockSpec((c, pp * PAIR), lambda g, j: (nc - 1 - j, off + g))
    return pl.pallas_call(
        body, name=name, grid=(n_pairs // pp, nc),
        in_specs=[seq(off) for _, off in srcs]
        + [pl.BlockSpec((pp, None, PAIR, PAIR), lambda g, j: (g, nc - 1 - j, 0, 0)), seq(dy[1])],
        out_specs=[seq(0)] * 6,
        out_shape=[jax.ShapeDtypeStruct((t, D_R), F32)] * 6,
        scratch_shapes=[pltpu.VMEM((pp, PAIR, PAIR), F32)],
        compiler_params=_cparams(None),
    )(*[a for a, _ in srcs], st, dy[0])


def _rope(x, cos, sin, rot):
    return x * cos + _dotb(x, rot) * sin


def _attn_block(nb, q, kp, kc, km, vp, vc, vm, sk, cq, sq, cp, sp, cm, sm, rot):
    g = GQA_GROUP
    scale = HEAD_DIM ** -0.5
    each = lambda f, *xs: [f(*x) for x in zip(*xs)]
    down = lambda x: jnp.concatenate([x] * g, axis=0)
    cq4, sq4 = down(cq), down(sq)
    kpr = each(lambda x: _rope(x, cp, sp, rot), kp)
    kcr = each(lambda x: _rope(x, cq, sq, rot), kc)
    kmr = each(lambda x: _rope(x, cm, sm, rot), km)
    qr = each(lambda x: _rope(x, cq4, sq4, rot), q)
    i = lax.broadcasted_iota(jnp.int32, (g * BLOCK, BLOCK), 0)
    i = i - BLOCK * ((i >= BLOCK).astype(jnp.int32) + (i >= 2 * BLOCK).astype(jnp.int32) + (i >= 3 * BLOCK).astype(jnp.int32))
    j = lax.broadcasted_iota(jnp.int32, (g * BLOCK, BLOCK), 1)
    nbv = jnp.zeros((g * BLOCK, BLOCK), jnp.int32) + nb
    ok_p = (j > i) & (nbv >= 2)
    ok_c = (j <= i) & (nbv >= 1)
    ok_m = (j >= BLOCK - N_META) & ((nbv >= 1) | (j <= i))
    sink = each(lambda s4: jnp.concatenate([jnp.broadcast_to(s, (BLOCK, 1)) for s in s4], axis=0), sk)
    s_p = each(lambda x, kk: jnp.where(ok_p, _dotb(x, kk, _NT) * scale, NEG_INF), qr, kpr)
    s_c = each(lambda x, kk: jnp.where(ok_c, _dotb(x, kk, _NT) * scale, NEG_INF), qr, kcr)
    s_m = each(lambda x, kk: jnp.where(ok_m, _dotb(x, kk, _NT) * scale, NEG_INF), qr, kmr)
    rmax = lambda s: jnp.max(s, axis=-1, keepdims=True)
    m = each(lambda a, b, c, d: lax.stop_gradient(jnp.maximum(jnp.maximum(rmax(a), rmax(b)), jnp.maximum(rmax(c), d))),
             s_p, s_c, s_m, sink)
    e_p = each(lambda s, mm: jnp.exp(s - mm), s_p, m)
    e_c = each(lambda s, mm: jnp.exp(s - mm), s_c, m)
    e_m = each(lambda s, mm: jnp.exp(s - mm), s_m, m)
    rsum = lambda e: jnp.sum(e, axis=-1, keepdims=True)
    inv = each(lambda a, b, c, d, mm: 1.0 / (rsum(a) + rsum(b) + rsum(c) + jnp.exp(d - mm)), e_p, e_c, e_m, sink, m)
    return tuple(each(lambda a, b, c, iv, x, y, z: _dotb(a * iv, x) + _dotb(b * iv, y) + _dotb(c * iv, z),
                      e_p, e_c, e_m, inv, vp, vc, vm))


def _attn_specs():
    cur = lambda n: (0, n, 0)
    prev = lambda n: (0, jnp.maximum(n - 1, 0), 0)
    meta = lambda n: (0, 0, 0)
    kv = lambda m: pl.BlockSpec((N_KV_HEADS, BLOCK, HEAD_DIM), m)
    tab = lambda m: pl.BlockSpec((BLOCK, HEAD_DIM), m)
    tcur, tprev, tmeta = (lambda n: (n, 0)), (lambda n: (jnp.maximum(n - 1, 0), 0)), (lambda n: (0, 0))
    qspec = pl.BlockSpec((N_Q_HEADS, BLOCK, HEAD_DIM), cur)
    sspec = pl.BlockSpec((N_Q_HEADS, 8, LANES), meta)
    specs = [qspec, kv(prev), kv(cur), kv(meta), kv(prev), kv(cur), kv(meta), sspec,
             tab(tcur), tab(tcur), tab(tprev), tab(tprev), tab(tmeta), tab(tmeta),
             pl.BlockSpec((HEAD_DIM, HEAD_DIM), lambda n: (0, 0))]
    return specs, qspec, sspec, kv


def _attn_args(q, k, v, sinks_b, cos, sin, rot):
    return (q, k, k, k, v, v, v, sinks_b, cos, sin, cos, sin, cos, sin, rot)


def _attn_operands(q_ref, kp, kc, km, vp, vc, vm, s_ref):
    groups = range(N_KV_HEADS)
    q = tuple(jnp.concatenate([q_ref[GQA_GROUP * i + h] for h in range(GQA_GROUP)], axis=0) for i in groups)
    sk = tuple(tuple(s_ref[GQA_GROUP * i + h][0:1, 0:1] for h in range(GQA_GROUP)) for i in groups)
    per_head = lambda ref: tuple(ref[i] for i in groups)
    return q, per_head(kp), per_head(kc), per_head(km), per_head(vp), per_head(vc), per_head(vm), sk


def _attn_fwd(name, q, k, v, sinks_b, cos, sin, rot):
    tp = q.shape[1]
    specs, qspec, _, _ = _attn_specs()

    def body(q_ref, kp, kc, km, vp, vc, vm, s_ref, cq, sq, cp, sp, cm, sm, rot_ref, o_ref):
        out = _attn_block(pl.program_id(0), *_attn_operands(q_ref, kp, kc, km, vp, vc, vm, s_ref),
                          cq[...], sq[...], cp[...], sp[...], cm[...], sm[...], rot_ref[...])
        for i in range(N_KV_HEADS):
            for h in range(GQA_GROUP):
                o_ref[GQA_GROUP * i + h] = out[i][h * BLOCK:(h + 1) * BLOCK]

    return pl.pallas_call(
        body, name=name, grid=(tp // BLOCK,), in_specs=specs, out_specs=qspec,
        out_shape=jax.ShapeDtypeStruct(q.shape, F32), compiler_params=_cparams(None),
    )(*_attn_args(q, k, v, sinks_b, cos, sin, rot))


def _attn_bwd(name, q, k, v, sinks_b, cos, sin, rot, do):
    tp = q.shape[1]
    nb = tp // BLOCK
    specs, qspec, sspec, kv = _attn_specs()

    def body(q_ref, kp, kc, km, vp, vc, vm, s_ref, cq, sq, cp, sp, cm, sm, rot_ref, do_ref,
             dq_ref, dkp, dkc, dvp, dvc, dkm, dvm, ds_ref):
        n = pl.program_id(0)
        tabs = (cq[...], sq[...], cp[...], sp[...], cm[...], sm[...], rot_ref[...])
        _, vjp = jax.vjp(lambda *a: _attn_block(n, *a, *tabs), *_attn_operands(q_ref, kp, kc, km, vp, vc, vm, s_ref))
        do_all = tuple(jnp.concatenate([do_ref[GQA_GROUP * i + h] for h in range(GQA_GROUP)], axis=0)
                       for i in range(N_KV_HEADS))
        dq, gkp, gkc, gkm, gvp, gvc, gvm, dsk = vjp(do_all)
        for i in range(N_KV_HEADS):
            dkp[i] = gkp[i]
            dkc[i] = gkc[i]
            dvp[i] = gvp[i]
            dvc[i] = gvc[i]
            for h in range(GQA_GROUP):
                dq_ref[GQA_GROUP * i + h] = dq[i][h * BLOCK:(h + 1) * BLOCK]

        @pl.when(n == 0)
        def _():
            for i in range(N_KV_HEADS):
                dkm[i] = gkm[i]
                dvm[i] = gvm[i]
                for h in range(GQA_GROUP):
                    ds_ref[GQA_GROUP * i + h] = jnp.broadcast_to(dsk[i][h], (8, LANES))

        @pl.when(n != 0)
        def _():
            for i in range(N_KV_HEADS):
                dkm[i] += gkm[i]
                dvm[i] += gvm[i]
                for h in range(GQA_GROUP):
                    ds_ref[GQA_GROUP * i + h] += jnp.broadcast_to(dsk[i][h], (8, LANES))

    part = pl.BlockSpec((N_KV_HEADS, None, BLOCK, HEAD_DIM), lambda n: (0, n, 0, 0))
    part_shape = jax.ShapeDtypeStruct((N_KV_HEADS, nb, BLOCK, HEAD_DIM), F32)
    meta_shape = jax.ShapeDtypeStruct((N_KV_HEADS, BLOCK, HEAD_DIM), F32)
    return pl.pallas_call(
        body, name=name, grid=(nb,), in_specs=specs + [qspec],
        out_specs=[qspec, part, part, part, part, kv(lambda n: (0, 0, 0)), kv(lambda n: (0, 0, 0)), sspec],
        out_shape=[jax.ShapeDtypeStruct(q.shape, F32), part_shape, part_shape, part_shape, part_shape,
                   meta_shape, meta_shape, jax.ShapeDtypeStruct(sinks_b.shape, F32)],
        compiler_params=_cparams(None),
    )(*_attn_args(q, k, v, sinks_b, cos, sin, rot), do)


def _kv_combine(name, prev_part, own_part, meta):
    g, nb = own_part.shape[:2]

    def fn(own, nxt, mt):
        m = pl.program_id(1)
        one = jnp.ones((BLOCK, HEAD_DIM), F32)
        use_next = jnp.where(one * m < nb - 1, 1.0, 0.0)
        use_meta = jnp.where(one * m < 1, 1.0, 0.0)
        return own + nxt * use_next + mt * use_meta

    blk = (None, None, BLOCK, HEAD_DIM)
    return _call(fn, name, (g, nb),
                 [(own_part, blk, lambda a, m: (a, m, 0, 0)),
                  (prev_part, blk, lambda a, m: (a, jnp.minimum(m + 1, nb - 1), 0, 0)),
                  (meta, (None, BLOCK, HEAD_DIM), lambda a, m: (a, 0, 0))],
                 [((g, nb * BLOCK, HEAD_DIM), (None, BLOCK, HEAD_DIM), lambda a, m: (a, m, 0), False)])


PACK_W = 1024
ELEMENTWISE_BLOCK_BYTES = 1 << 21


def _rows_tile(rows, cols):
    cap = max(8, ELEMENTWISE_BLOCK_BYTES // (4 * cols))
    for d in range(min(rows, cap), 0, -1):
        if rows % d == 0 and d % 8 == 0:
            return d
    return rows


def _adamw(name, w, g, m, v):
    rows, cols = w.shape
    tr = _rows_tile(rows, cols)

    def fn(wv, gv, mv, vv):
        m1 = ADAM_B1 * mv + (1.0 - ADAM_B1) * gv
        v1 = ADAM_B2 * vv + (1.0 - ADAM_B2) * (gv * gv)
        m_hat = m1 / (1.0 - ADAM_B1 ** ADAM_STEP)
        v_hat = v1 / (1.0 - ADAM_B2 ** ADAM_STEP)
        return -ADAM_LR * (m_hat / (jnp.sqrt(v_hat) + ADAM_EPS) + ADAM_WD * wv), m1, v1

    blk = (tr, cols)
    row = lambda i: (i, 0)
    return _call(fn, name, (rows // tr,), [(a, blk, row) for a in (w, g, m, v)], [((rows, cols), blk, row, False)] * 3)


def _pair_add_placed(name, g, recv, cm_idx, out_dtype):
    s, a, b = g.shape
    half = a // 2

    def body(cm_ref, a_ref, b_ref, o_ref, own_ref):
        val = (a_ref[...] + b_ref[...]).astype(out_dtype)
        o_ref[...] = val

        @pl.when(pl.program_id(0) == cm_ref[1])
        def _():
            own_ref[...] = val

    blk = (None, half, b)
    shape = jax.ShapeDtypeStruct((s, half, b), out_dtype)
    return pl.pallas_call(
        body, name=name,
        grid_spec=pltpu.PrefetchScalarGridSpec(
            num_scalar_prefetch=1, grid=(s,),
            in_specs=[pl.BlockSpec(blk, lambda j, cm: (j, cm[0], 0)), pl.BlockSpec(blk, lambda j, cm: (j, 0, 0))],
            out_specs=[pl.BlockSpec(blk, lambda j, cm: (j, 0, 0)), pl.BlockSpec(blk, lambda j, cm: (cm[1], 0, 0))]),
        out_shape=[shape, shape], compiler_params=_cparams(None),
    )(cm_idx, g, recv)


def _sum_chips(name, parts, c_idx, layer, n_layers, into=None):
    _, a, b = parts.shape
    tr = _rows_tile(a, b)

    def body(c_ref, p0, p1, p2, p3, *rest):
        o_ref = rest[-1]
        up = lambda p: p[...].astype(F32)
        o_ref[...] = ((up(p0) + up(p1)) + up(p2)) + up(p3)

    in_specs = [pl.BlockSpec((None, tr, b), lambda i, c, k=k: (k, i, 0)) for k in range(N_CHIPS)]
    args = [c_idx] + [parts] * N_CHIPS
    aliases = {}
    if into is not None:
        in_specs.append(_ANY)
        args.append(into)
        aliases = {1 + N_CHIPS: 0}
    return pl.pallas_call(
        body, name=name,
        grid_spec=pltpu.PrefetchScalarGridSpec(
            num_scalar_prefetch=1, grid=(a // tr,), in_specs=in_specs,
            out_specs=pl.BlockSpec((None, None, tr, b), lambda i, c: (layer, c[0], i, 0))),
        out_shape=jax.ShapeDtypeStruct((n_layers, 2, a, b), F32), input_output_aliases=aliases,
        compiler_params=_cparams(None),
    )(*args)


def _place_own_block(name, w, layer, me_idx, dtype):
    _, a2, b = w.shape
    a = a2 // 2
    tr = _rows_tile(a, b)
    nb = a // tr

    def body(me_ref, w_ref, o_ref):
        o_ref[...] = w_ref[...].astype(dtype)

    return pl.pallas_call(
        body, name=name,
        grid_spec=pltpu.PrefetchScalarGridSpec(
            num_scalar_prefetch=1, grid=(2, nb),
            in_specs=[pl.BlockSpec((None, tr, b), lambda h, i, me: (layer, h * nb + i, 0))],
            out_specs=pl.BlockSpec((None, None, tr, b), lambda h, i, me: (me[0], h, i, 0))),
        out_shape=jax.ShapeDtypeStruct((N_CHIPS, 2, a, b), dtype), compiler_params=_cparams(None),
    )(me_idx, w)


def _mesh_pos():
    return lax.axis_index("x"), lax.axis_index("y"), lax.axis_index("c")


def _other_chips(x, y):
    return [(1 - x, y), (x, 1 - y), (1 - x, 1 - y)]


_ANY = pl.BlockSpec(memory_space=pl.ANY)


def _gather_weights(name, bufs, from_chips=True):
    n = len(bufs)

    def body(*refs):
        out_refs = refs[n:2 * n]
        send_sems, recv_sems = refs[2 * n:]
        x, y, c = _mesh_pos()
        me = 2 * x + y
        sibling = (x, y, 1 - c)
        chips = _other_chips(x, y)

        def copy(i, k, chip_idx, half, to):
            return pltpu.make_async_remote_copy(src_ref=out_refs[i].at[chip_idx, half], dst_ref=out_refs[i].at[chip_idx, half],
                                                send_sem=send_sems.at[6 * i + k], recv_sem=recv_sems.at[6 * i + k],
                                                device_id=to, device_id_type=MESH)

        first = [copy(i, j, me, c, (*chip, c)) for i in range(n) for j, chip in enumerate(chips)] if from_chips else []
        for cp in first:
            cp.start()
        passed = []
        for i in range(n):
            for j, (cx, cy) in enumerate(chips):
                idx = 2 * cx + cy
                if from_chips:
                    copy(i, j, idx, c, sibling).wait_recv()
                fwd = copy(i, 3 + j, idx, c, sibling)
                fwd.start()
                passed.append(fwd)
        for i in range(n):
            for j, (cx, cy) in enumerate(chips):
                copy(i, 3 + j, 2 * cx + cy, 1 - c, sibling).wait_recv()
        for cp in first + passed:
            cp.wait_send()

    return pl.pallas_call(
        body, name=name, in_specs=[_ANY] * n, out_specs=[_ANY] * n,
        out_shape=[jax.ShapeDtypeStruct(b.shape, b.dtype) for b in bufs],
        input_output_aliases={i: i for i in range(n)},
        scratch_shapes=[pltpu.SemaphoreType.DMA((6 * n,)), pltpu.SemaphoreType.DMA((6 * n,))],
        compiler_params=pltpu.CompilerParams(has_side_effects=True),
    )(*bufs)


def _gather_start(name, groups):
    bufs = [b for g in groups for b in g]
    n = len(bufs)
    ng = len(groups)

    def body(*refs):
        b_refs = refs[:n]
        sems = refs[n:n + 2 * ng]
        token = refs[-1]
        x, y, c = _mesh_pos()
        me = 2 * x + y
        i = 0
        for gi, g in enumerate(groups):
            for k in range(len(g)):
                for j, (cx, cy) in enumerate(_other_chips(x, y)):
                    pltpu.make_async_remote_copy(src_ref=b_refs[i].at[me, c], dst_ref=b_refs[i].at[me, c],
                                                 send_sem=sems[2 * gi].at[3 * k + j], recv_sem=sems[2 * gi + 1].at[3 * k + j],
                                                 device_id=(cx, cy, c), device_id_type=MESH).start()
                i += 1
        token[...] = jnp.zeros(token.shape, F32)

    sem_shapes = [pltpu.SemaphoreType.DMA((3 * len(g),)) for g in groups for _ in range(2)]
    res = pl.pallas_call(
        body, name=name,
        out_shape=(*sem_shapes, *[pltpu.HBM(b.shape, b.dtype) for b in bufs], jax.ShapeDtypeStruct((8, LANES), F32)),
        in_specs=[_HBM] * n,
        out_specs=(*[_SEM] * (2 * ng), *[_HBM] * n, pl.BlockSpec(memory_space=pltpu.VMEM)),
        input_output_aliases={i: 2 * ng + i for i in range(n)},
        compiler_params=pltpu.CompilerParams(has_side_effects=_DATAFLOW),
    )(*[pltpu.with_memory_space_constraint(b, pltpu.HBM) for b in bufs])
    out, i = [], 2 * ng
    for gi, g in enumerate(groups):
        out.append((res[2 * gi], res[2 * gi + 1], list(res[i:i + len(g)])))
        i += len(g)
    return out, res[-1]


def _gather_wait(name, send_sems, recv_sems, bufs, after):
    n = len(bufs)

    def body(*refs):
        b_refs = refs[:n]
        s_sems, r_sems = refs[n], refs[n + 1]
        x, y, c = _mesh_pos()
        me = 2 * x + y
        for k in range(n):
            for j, (cx, cy) in enumerate(_other_chips(x, y)):
                idx = 2 * cx + cy
                copy = pltpu.make_async_remote_copy(src_ref=b_refs[k].at[me, c], dst_ref=b_refs[k].at[idx, c],
                                                    send_sem=s_sems.at[3 * k + j], recv_sem=r_sems.at[3 * k + j],
                                                    device_id=(cx, cy, c), device_id_type=MESH)
                copy.wait_send()
                copy.wait_recv()

    res = pl.pallas_call(
        body, name=name,
        out_shape=tuple(pltpu.HBM(b.shape, b.dtype) for b in bufs),
        in_specs=[_HBM] * n + [_SEM, _SEM, _ANY],
        out_specs=tuple([_HBM] * n),
        input_output_aliases={i: i for i in range(n)},
        compiler_params=pltpu.CompilerParams(has_side_effects=_DATAFLOW),
    )(*bufs, send_sems, recv_sems, after)
    return list(res)


def _halves_to_sibling(name, units):
    n = len(units)

    def body(*refs):
        g_refs, out_refs = refs[:n], refs[n:2 * n]
        send_sems, recv_sems = refs[2 * n:]
        x, y, c = _mesh_pos()
        cps = []
        for i in range(n):
            half = units[i].shape[1] // 2
            src = g_refs[i].at[pl.ds(0, N_CHIPS), pl.ds((1 - c) * half, half)]
            cp = pltpu.make_async_remote_copy(src_ref=src, dst_ref=out_refs[i], send_sem=send_sems.at[i],
                                              recv_sem=recv_sems.at[i], device_id=(x, y, 1 - c), device_id_type=MESH)
            cp.start()
            cps.append(cp)
        for cp in cps:
            cp.wait()

    return pl.pallas_call(
        body, name=name, in_specs=[_ANY] * n, out_specs=[_ANY] * n,
        out_shape=[jax.ShapeDtypeStruct((u.shape[0], u.shape[1] // 2, u.shape[2]), u.dtype) for u in units],
        scratch_shapes=[pltpu.SemaphoreType.DMA((n,)), pltpu.SemaphoreType.DMA((n,))],
        compiler_params=pltpu.CompilerParams(has_side_effects=True),
    )(*units)


_HBM = pl.BlockSpec(memory_space=pltpu.HBM)
_SEM = pl.BlockSpec(memory_space=pltpu.SEMAPHORE)
_DATAFLOW = pltpu.SideEffectType.DATAFLOW_SIDE_EFFECTING


def _scatter_start(name, sums, zones):
    n = len(sums)

    def body(*refs):
        h_refs, z_refs = refs[:n], refs[n:2 * n]
        send_sems, recv_sems = refs[2 * n], refs[2 * n + 1]
        token = refs[-1]
        x, y, c = _mesh_pos()
        me = 2 * x + y
        for i in range(n):
            for j, (cx, cy) in enumerate(_other_chips(x, y)):
                pltpu.make_async_remote_copy(src_ref=h_refs[i].at[2 * cx + cy], dst_ref=z_refs[i].at[me],
                                             send_sem=send_sems.at[3 * i + j], recv_sem=recv_sems.at[3 * i + j],
                                             device_id=(cx, cy, c), device_id_type=MESH).start()
        token[...] = jnp.zeros(token.shape, F32)

    hbm = lambda a: pltpu.HBM(a.shape, a.dtype)
    res = pl.pallas_call(
        body, name=name,
        out_shape=(pltpu.SemaphoreType.DMA((3 * n,)), pltpu.SemaphoreType.DMA((3 * n,)),
                   *[hbm(a) for a in sums], *[hbm(a) for a in zones], jax.ShapeDtypeStruct((8, LANES), F32)),
        in_specs=[_HBM] * (2 * n),
        out_specs=(_SEM, _SEM, *[_HBM] * (2 * n), pl.BlockSpec(memory_space=pltpu.VMEM)),
        input_output_aliases={i: 2 + i for i in range(2 * n)},
        compiler_params=pltpu.CompilerParams(has_side_effects=_DATAFLOW),
    )(*[pltpu.with_memory_space_constraint(a, pltpu.HBM) for a in list(sums) + list(zones)])
    return res[0], res[1], res[2:2 + n], res[2 + n:2 + 2 * n], res[-1]


def _scatter_wait(name, send_sems, recv_sems, sums, zones, after):
    n = len(sums)

    def body(*refs):
        h_refs, z_refs = refs[:n], refs[n:2 * n]
        s_sems, r_sems = refs[2 * n], refs[2 * n + 1]
        x, y, c = _mesh_pos()
        me = 2 * x + y
        for i in range(n):
            for j, (cx, cy) in enumerate(_other_chips(x, y)):
                idx = 2 * cx + cy
                copy = pltpu.make_async_remote_copy(src_ref=h_refs[i].at[idx], dst_ref=z_refs[i].at[idx],
                                                    send_sem=s_sems.at[3 * i + j], recv_sem=r_sems.at[3 * i + j],
                                                    device_id=(cx, cy, c), device_id_type=MESH)
                copy.wait_send()
                copy.wait_recv()

    hbm = lambda a: pltpu.HBM(a.shape, a.dtype)
    res = pl.pallas_call(
        body, name=name,
        out_shape=(*[hbm(a) for a in sums], *[hbm(a) for a in zones]),
        in_specs=[_HBM] * (2 * n) + [_SEM, _SEM, _ANY],
        out_specs=tuple([_HBM] * (2 * n)),
        input_output_aliases={i: i for i in range(2 * n)},
        compiler_params=pltpu.CompilerParams(has_side_effects=_DATAFLOW),
    )(*sums, *zones, send_sems, recv_sems, after)
    return res[n:]


def _join_halves(name, results):
    n = len(results)
    pieces = [(i, l) for i in range(n) for l in range(results[i].shape[0])]

    def body(*refs):
        out_refs = refs[n:2 * n]
        send_sems, recv_sems = refs[2 * n:]
        x, y, c = _mesh_pos()

        def copy(k, half):
            i, l = pieces[k]
            return pltpu.make_async_remote_copy(src_ref=out_refs[i].at[l, half], dst_ref=out_refs[i].at[l, half],
                                                send_sem=send_sems.at[k], recv_sem=recv_sems.at[k],
                                                device_id=(x, y, 1 - c), device_id_type=MESH)

        cps = [copy(k, c) for k in range(len(pieces))]
        for cp in cps:
            cp.start()
        for k in range(len(pieces)):
            copy(k, 1 - c).wait_recv()
        for cp in cps:
            cp.wait_send()

    return pl.pallas_call(
        body, name=name, in_specs=[_ANY] * n, out_specs=[_ANY] * n,
        out_shape=[jax.ShapeDtypeStruct(r.shape, r.dtype) for r in results],
        input_output_aliases={i: i for i in range(n)},
        scratch_shapes=[pltpu.SemaphoreType.DMA((len(pieces),)), pltpu.SemaphoreType.DMA((len(pieces),))],
        compiler_params=pltpu.CompilerParams(has_side_effects=True),
    )(*results)


def _pack(arrays, dtype, rows_multiple):
    flat = jnp.concatenate([a.reshape(-1).astype(dtype) for a in arrays])
    unit = rows_multiple * PACK_W
    total = -(-flat.shape[0] // unit) * unit
    return jnp.pad(flat, (0, total - flat.shape[0])).reshape(total // PACK_W, PACK_W)


def _unpack(flat, shapes):
    out, off = [], 0
    for s in shapes:
        n = 1
        for d in s:
            n *= d
        out.append(flat[..., off:off + n].reshape(flat.shape[:-1] + tuple(s)))
        off += n
    return out


def _ffn_fwd(tag, l, h, g, w_up, conv, bias, w_down, tm):
    hn = _rms_fwd(f"{tag}_norm", h, g, tm)
    u = _mm_cs(f"{tag}_up", hn, w_up, l, tm, out_dtype=FFN_HIDDEN_DTYPE)
    act = _ffn_col_fwd(f"{tag}_glu", u, conv, bias)
    h_out = _mm_full(f"{tag}_down", act, w_down, l, tm, D_FF // 2, add=h)
    return h_out, (hn, u, act)


def _ffn_bwd(tag, l, h, g, w_up, conv, bias, w_down, saved, dh, tm):
    hn, u, act = saved
    da = _mm_nt_full(f"{tag}_down_dx", dh, w_down, l, tm, D_FF // 2)
    dw_down = _mm_tn_full(f"{tag}_down_dw", act, dh, tm, D_FF // 2)
    du, dconv, dbias = _ffn_col_bwd(f"{tag}_glu_bwd", u, da, conv, bias)
    dw_up = _mm_tn_cs(f"{tag}_up_dw", hn, du, N_CHIPS, tm)
    dhn = _mm_nt_cs(f"{tag}_up_dx", du, w_up, l, tm)
    dh, dg = _rms_bwd(f"{tag}_norm_bwd", h, g, dhn, dh, tm)
    return dh, dict(norm=dg, w_up=dw_up, conv=dconv, bias=dbias, w_down=dw_down)


def _to_heads(z, nh, pad):
    t = z.shape[0]
    return jnp.pad(z.reshape(t, nh, HEAD_DIM).transpose(1, 0, 2), ((0, 0), (pad, 0), (0, 0)))


def _from_heads(z, pad):
    nh, tp, _ = z.shape
    return z[:, pad:].transpose(1, 0, 2).reshape(tp - pad, nh * HEAD_DIM)


def _rope_tables(tp, pad):
    half = HEAD_DIM // 2
    inv = ROPE_THETA ** (-jnp.arange(half, dtype=F32) / half)
    ang = (jnp.arange(tp, dtype=F32) - pad)[:, None] * inv[None, :]
    cos, sin = jnp.cos(ang), jnp.sin(ang)
    rot = jnp.zeros((HEAD_DIM, HEAD_DIM), F32)
    idx = jnp.arange(half)
    rot = rot.at[idx + half, idx].set(-1.0).at[idx, idx + half].set(1.0)
    return jnp.concatenate([cos, cos], axis=1), jnp.concatenate([sin, sin], axis=1), rot


def _local_step(x, tgt, w, on_grads=None, fetch=None):
    emit = on_grads if on_grads is not None else (lambda tag, units: 0.0)
    need = (lambda tag, after: w) if fetch is None else (lambda tag, after: {**w, **fetch(tag, after)})
    seq = x.shape[0]
    t = seq + N_META
    tm = _row_tile(t, 704)
    tr = _row_tile(t, 352)
    pad = BLOCK - N_META
    grads = {}

    h0 = jnp.concatenate([w["meta_tokens"], x], axis=0)
    tgt_p = jnp.pad(tgt, ((N_META, 0), (0, 0)))

    hn0 = _rms_fwd("l0_norm", h0, w["norm_mix"][0:1], tm)
    p0 = _mm_cs("l0_in", hn0, w["ev_w_in"], 0, tm)
    uc, yb = _even_col_fwd("l0_convs", p0, w["ev_conv_a"], w["ev_conv_b"])
    ya = _even_ln_fwd("l0_ln", uc, w["ev_ln_a_g"], w["ev_ln_a_b"], tm)
    y0 = jnp.concatenate([ya, yb], axis=1)
    w = need("ev_out", y0)
    h1 = _mm_full("l0_out", y0, w["ev_w_out"], 0, tm, D_MODEL, add=h0)
    w = need("f0", h1)
    f0 = (0, h1, w["norm_ffn"][0:1], w["ff_w_up0"], w["ff_conv"][0], w["ff_conv_b"][0:1], w["ff_w_down0"])
    h2, ffn0 = _ffn_fwd("f0", *f0, tm)
    w = need("od", h2)

    hn2 = _rms_fwd("l1_norm", h2, w["norm_mix"][1:2], tm)
    p1 = _mm_cs("l1_in", hn2, w["od_w_in"], 0, tm)
    cos, sin, rot = _rope_tables(t + pad, pad)
    qh = _to_heads(p1[:, :D_ATT], N_Q_HEADS, pad)
    kh = _to_heads(p1[:, D_ATT:D_ATT + D_KV], N_KV_HEADS, pad)
    vh = _to_heads(p1[:, D_ATT + D_KV:D_ATT + 2 * D_KV], N_KV_HEADS, pad)
    sinks_b = jnp.broadcast_to(w["od_sinks"].reshape(N_Q_HEADS, 1, 1), (N_Q_HEADS, 8, LANES))
    y_att = _from_heads(_attn_fwd("l1_attn", qh, kh, vh, sinks_b, cos, sin, rot), pad)

    col0 = D_ATT + 2 * D_KV
    ch = jnp.arange(D_R) // HEAD_DIM
    seg = (ch[:, None] == ch[None, :]).astype(F32)
    prm = dict(w0=w["od_w0"], a0=w["od_a0"], g2=w["od_g2"], k_k=w["od_k_k"], k_a=w["od_k_a"],
               lnx_g=w["od_lnx_g"], lnx_b=w["od_lnx_b"], r_k=w["od_r_k"].reshape(1, D_R),
               w2p=jnp.concatenate([w["od_w2"], jnp.zeros((LORA_A, D_R), F32)], axis=0),
               a2p=jnp.concatenate([jnp.zeros((LORA_W, D_R), F32), w["od_a2"]], axis=0))
    prs = _shift_fwd("l1_shift", p1, col0, w["od_mu"])
    lw, k2, a_, b_, gate_r = _rwkv_pre_fwd("l1_rwkv_pre", prs, prm, seg, tr)
    v_off = 2 * D_R // (WKV_PAIRS_PER_STEP * PAIR)
    scan_in = [(prs, 0), (lw, 0), (k2, 0), (prs, v_off), (a_, 0), (b_, 0)]
    y_scan, states = _wkv_fwd("l1_wkv", scan_in)
    y_rwkv = _rwkv_post_fwd("l1_rwkv_post", y_scan, prs, k2, gate_r, prm, seg, tr)
    y1 = jnp.concatenate([y_att, y_rwkv], axis=1).astype(MXU_DTYPE)
    h3 = _mm_full("l1_out", y1, w["od_w_out"], 0, tm, D_MODEL, add=h2)
    w = need("f1", h3)
    f1 = (0, h3, w["norm_ffn"][1:2], w["ff_w_up1"], w["ff_conv"][1], w["ff_conv_b"][1:2], w["ff_w_down1"])
    h4, ffn1 = _ffn_fwd("f1", *f1, tm)

    loss_blk, dh, d_norm_final = _final_loss("final", h4, w["norm_final"], tgt_p, tm)
    grads["norm_final"] = d_norm_final

    dh, gf1 = _ffn_bwd("f1", *f1, ffn1, dh, tm)
    zero = emit("f1", {"ff_w_up1": gf1["w_up"], "ff_w_down1": gf1["w_down"].reshape(N_CHIPS, D_FF // N_CHIPS, D_MODEL)})
    prm = dict(prm, lnx_g=prm["lnx_g"] + zero)
    dy1 = _mm_nt_full("l1_out_dx", dh, w["od_w_out"], 0, tm, D_MODEL)
    grads["od_w_out"] = _mm_tn_full("l1_out_dw", y1, dh, tm, D_MODEL // 2)
    dy_scan, dr_p, dk2_p, dv_p, dgate_r, grads["od_lnx_g"], grads["od_lnx_b"], d_rk = _rwkv_post_bwd(
        "l1_rwkv_post_bwd", y_scan, prs, k2, gate_r, prm, seg, dy1, 1, tr)
    grads["od_r_k"] = d_rk.reshape(N_R_HEADS, HEAD_DIM)
    dr_s, dlw, dk2_s, dv_s, da_, db_ = _wkv_bwd("l1_wkv_bwd", scan_in, states, (dy_scan, 0))
    dk, dxl, dgd, grads["od_w0"], dw2p, grads["od_a0"], da2p, grads["od_g2"], grads["od_k_k"], grads["od_k_a"] = (
        _rwkv_pre_bwd("l1_rwkv_pre_bwd", prs, prm, seg, (dlw, dk2_s + dk2_p, da_, db_, dgate_r), tr))
    grads["od_w2"] = dw2p[:LORA_W]
    grads["od_a2"] = da2p[LORA_W:]
    dprs = jnp.concatenate([dr_s + dr_p, dk, dv_s + dv_p, dxl, dgd], axis=1)
    dpr, grads["od_mu"] = _shift_bwd("l1_shift_bwd", p1, col0, w["od_mu"], dprs)
    doh = _to_heads(dy1[:, :D_ATT], N_Q_HEADS, pad)
    dqh, dkp, dkc, dvp, dvc, dkm, dvm, dsinks = _attn_bwd("l1_attn_bwd", qh, kh, vh, sinks_b, cos, sin, rot, doh)
    grads["od_sinks"] = dsinks[:, 0, 0].reshape(1, N_Q_HEADS)
    dkh = _kv_combine("l1_attn_dk", dkp, dkc, dkm)
    dvh = _kv_combine("l1_attn_dv", dvp, dvc, dvm)
    dp1 = jnp.concatenate([_from_heads(dqh, pad), _from_heads(dkh, pad), _from_heads(dvh, pad), dpr], axis=1).astype(MXU_DTYPE)
    grads["od_w_in"] = _mm_tn_cs("l1_in_dw", hn2, dp1, N_CHIPS, tm)
    dhn2 = _mm_nt_cs("l1_in_dx", dp1, w["od_w_in"], 0, tm)
    dh, d_mix1 = _rms_bwd("l1_norm_bwd", h2, w["norm_mix"][1:2], dhn2, dh, tm)

    zero = emit("od", {"od_w_out": grads["od_w_out"].reshape(N_CHIPS, D_MODEL // N_CHIPS, D_MODEL), "od_w_in": grads["od_w_in"]})
    f0 = f0[:5] + (f0[5] + zero,) + f0[6:]
    dh, gf0 = _ffn_bwd("f0", *f0, ffn0, dh, tm)
    zero = emit("f0", {"ff_w_up0": gf0["w_up"], "ff_w_down0": gf0["w_down"].reshape(N_CHIPS, D_FF // N_CHIPS, D_MODEL)})
    w = dict(w, ev_ln_a_g=w["ev_ln_a_g"] + zero)
    dy0 = _mm_nt_full("l0_out_dx", dh, w["ev_w_out"], 0, tm, D_MODEL)
    grads["ev_w_out"] = _mm_tn_full("l0_out_dw", y0, dh, tm, D_MODEL // 2)
    duc, grads["ev_ln_a_g"], grads["ev_ln_a_b"] = _even_ln_bwd("l0_ln_bwd", uc, w["ev_ln_a_g"], w["ev_ln_a_b"], dy0, 0, tm)
    *dparts, grads["ev_conv_a"], grads["ev_conv_b"] = _even_col_bwd("l0_convs_bwd", p0, duc, dy0, w["ev_conv_a"], w["ev_conv_b"])
    dp0 = jnp.concatenate(dparts, axis=1)
    grads["ev_w_in"] = _mm_tn_cs("l0_in_dw", hn0, dp0, N_CHIPS, tm)
    dhn0 = _mm_nt_cs("l0_in_dx", dp0, w["ev_w_in"], 0, tm)
    dh, d_mix0 = _rms_bwd("l0_norm_bwd", h0, w["norm_mix"][0:1], dhn0, dh, tm)

    grads["norm_mix"] = jnp.concatenate([d_mix0, d_mix1], axis=0)
    grads["norm_ffn"] = jnp.concatenate([gf0["norm"], gf1["norm"]], axis=0)
    grads["ff_w_up"] = [gf0["w_up"], gf1["w_up"]]
    grads["ff_conv"] = jnp.stack([gf0["conv"], gf1["conv"]])
    grads["ff_conv_b"] = jnp.concatenate([gf0["bias"], gf1["bias"]], axis=0)
    grads["ff_w_down"] = [gf0["w_down"], gf1["w_down"]]
    grads["meta_tokens"] = dh[:N_META]
    return loss_blk[0, 0], dh[N_META:], grads


SHARD_AXIS = {
    "meta_tokens": 1, "norm_mix": None, "norm_ffn": None, "norm_final": None,
    "ev_w_in": 2, "ev_conv_a": 2, "ev_ln_a_g": None, "ev_ln_a_b": None, "ev_conv_b": 2, "ev_w_out": 1,
    "od_w_in": 2, "od_sinks": None, "od_mu": 1, "od_w0": 1, "od_w2": 2, "od_a0": 1, "od_a2": 2, "od_g2": 2,
    "od_k_k": 1, "od_k_a": 1, "od_r_k": None, "od_lnx_g": 1, "od_lnx_b": 1, "od_w_out": 1,
    "ff_w_up": 2, "ff_conv": 2, "ff_conv_b": None, "ff_w_down": 1,
}
WEIGHTS = list(SHARD_AXIS)
BIG = ("ev_w_in", "ev_w_out", "od_w_in", "od_w_out", "ff_w_up", "ff_w_down")
SHARDED = [n for n in WEIGHTS if SHARD_AXIS[n] is not None]
SMALL = [n for n in SHARDED if n not in BIG]
REPLICATED = [n for n in WEIGHTS if SHARD_AXIS[n] is None]


def _join(g, axis):
    return jnp.concatenate([g[k] for k in range(N_CHIPS)], axis=axis)


def _split(full, axis):
    return jnp.stack(jnp.split(full, N_CHIPS, axis=axis))


def _full_weights(gathered, repl):
    w = {}
    sq = lambda a: a.reshape(a.shape[1:]) if a.shape[0] == 1 else a
    for n in REPLICATED:
        w[n] = repl[n]
    w["norm_final"] = repl["norm_final"].reshape(1, D_MODEL)
    for n in ("ev_ln_a_g", "ev_ln_a_b"):
        w[n] = repl[n].reshape(1, D_A)
    w["od_r_k"] = repl["od_r_k"][0]
    w["meta_tokens"] = _join(gathered["meta_tokens"], 1)
    for n in ("ev_conv_a", "ev_conv_b", "od_w2", "od_a2", "od_g2"):
        w[n] = sq(_join(gathered[n], 2))
    for n in ("od_mu", "od_w0", "od_a0", "od_k_k", "od_k_a", "od_lnx_g", "od_lnx_b"):
        w[n] = _join(gathered[n], 1)
    w["ff_conv"] = _join(gathered["ff_conv"], 2)
    return w


def _shard_grads(grads):
    out = {}
    for n in REPLICATED:
        out[n] = grads[n]
    out["norm_final"] = grads["norm_final"].reshape(D_MODEL)
    out["od_r_k"] = grads["od_r_k"][None]
    out["meta_tokens"] = _split(grads["meta_tokens"], 1)
    for n in ("ev_conv_a", "ev_conv_b", "od_w2", "od_a2", "od_g2"):
        out[n] = _split(grads[n][None], 2)
    for n in ("od_mu", "od_w0", "od_a0", "od_k_k", "od_k_a", "od_lnx_g", "od_lnx_b"):
        out[n] = _split(grads[n], 1)
    out["ff_conv"] = _split(grads["ff_conv"], 2)
    return out


def kernel(x, meta_tokens, norm_mix, norm_ffn, norm_final, ev_w_in, ev_conv_a, ev_ln_a_g, ev_ln_a_b, ev_conv_b, ev_w_out, od_w_in, od_sinks, od_mu, od_w0, od_w2, od_a0, od_a2, od_g2, od_k_k, od_k_a, od_r_k, od_lnx_g, od_lnx_b, od_w_out, ff_w_up, ff_conv, ff_conv_b, ff_w_down, loss_target, m_meta_tokens, m_norm_mix, m_norm_ffn, m_norm_final, m_ev_w_in, m_ev_conv_a, m_ev_ln_a_g, m_ev_ln_a_b, m_ev_conv_b, m_ev_w_out, m_od_w_in, m_od_sinks, m_od_mu, m_od_w0, m_od_w2, m_od_a0, m_od_a2, m_od_g2, m_od_k_k, m_od_k_a, m_od_r_k, m_od_lnx_g, m_od_lnx_b, m_od_w_out, m_ff_w_up, m_ff_conv, m_ff_conv_b, m_ff_w_down, v_meta_tokens, v_norm_mix, v_norm_ffn, v_norm_final, v_ev_w_in, v_ev_conv_a, v_ev_ln_a_g, v_ev_ln_a_b, v_ev_conv_b, v_ev_w_out, v_od_w_in, v_od_sinks, v_od_mu, v_od_w0, v_od_w2, v_od_a0, v_od_a2, v_od_g2, v_od_k_k, v_od_k_a, v_od_r_k, v_od_lnx_g, v_od_lnx_b, v_od_w_out, v_ff_w_up, v_ff_conv, v_ff_conv_b, v_ff_w_down):
    wts = dict(meta_tokens=meta_tokens, norm_mix=norm_mix, norm_ffn=norm_ffn, norm_final=norm_final, ev_w_in=ev_w_in, ev_conv_a=ev_conv_a, ev_ln_a_g=ev_ln_a_g, ev_ln_a_b=ev_ln_a_b, ev_conv_b=ev_conv_b, ev_w_out=ev_w_out, od_w_in=od_w_in, od_sinks=od_sinks, od_mu=od_mu, od_w0=od_w0, od_w2=od_w2, od_a0=od_a0, od_a2=od_a2, od_g2=od_g2, od_k_k=od_k_k, od_k_a=od_k_a, od_r_k=od_r_k, od_lnx_g=od_lnx_g, od_lnx_b=od_lnx_b, od_w_out=od_w_out, ff_w_up=ff_w_up, ff_conv=ff_conv, ff_conv_b=ff_conv_b, ff_w_down=ff_w_down)
    mom = dict(meta_tokens=m_meta_tokens, norm_mix=m_norm_mix, norm_ffn=m_norm_ffn, norm_final=m_norm_final, ev_w_in=m_ev_w_in, ev_conv_a=m_ev_conv_a, ev_ln_a_g=m_ev_ln_a_g, ev_ln_a_b=m_ev_ln_a_b, ev_conv_b=m_ev_conv_b, ev_w_out=m_ev_w_out, od_w_in=m_od_w_in, od_sinks=m_od_sinks, od_mu=m_od_mu, od_w0=m_od_w0, od_w2=m_od_w2, od_a0=m_od_a0, od_a2=m_od_a2, od_g2=m_od_g2, od_k_k=m_od_k_k, od_k_a=m_od_k_a, od_r_k=m_od_r_k, od_lnx_g=m_od_lnx_g, od_lnx_b=m_od_lnx_b, od_w_out=m_od_w_out, ff_w_up=m_ff_w_up, ff_conv=m_ff_conv, ff_conv_b=m_ff_conv_b, ff_w_down=m_ff_w_down)
    var = dict(meta_tokens=v_meta_tokens, norm_mix=v_norm_mix, norm_ffn=v_norm_ffn, norm_final=v_norm_final, ev_w_in=v_ev_w_in, ev_conv_a=v_ev_conv_a, ev_ln_a_g=v_ev_ln_a_g, ev_ln_a_b=v_ev_ln_a_b, ev_conv_b=v_ev_conv_b, ev_w_out=v_ev_w_out, od_w_in=v_od_w_in, od_sinks=v_od_sinks, od_mu=v_od_mu, od_w0=v_od_w0, od_w2=v_od_w2, od_a0=v_od_a0, od_a2=v_od_a2, od_g2=v_od_g2, od_k_k=v_od_k_k, od_k_a=v_od_k_a, od_r_k=v_od_r_k, od_lnx_g=v_od_lnx_g, od_lnx_b=v_od_lnx_b, od_w_out=v_od_w_out, ff_w_up=v_ff_w_up, ff_conv=v_ff_conv, ff_conv_b=v_ff_conv_b, ff_w_down=v_ff_w_down)

    me_idx = (2 * lax.axis_index("x") + lax.axis_index("y")).astype(jnp.int32).reshape(1)
    c_idx = lax.axis_index("c").astype(jnp.int32).reshape(1)
    small_mine = _pack([wts[n] for n in SMALL], F32, 2 * 8)
    sources = {"ev_w_in": (ev_w_in, 0), "small": (small_mine[None], 0), "ev_w_out": (ev_w_out, 0),
               "ff_w_up0": (ff_w_up, 0), "ff_w_down0": (ff_w_down, 0), "od_w_in": (od_w_in, 0), "od_w_out": (od_w_out, 0),
               "ff_w_up1": (ff_w_up, 1), "ff_w_down1": (ff_w_down, 1)}
    bufs = {n: _place_own_block("place_" + n, a, l, me_idx, F32 if n == "small" else MXU_DTYPE)
            for n, (a, l) in sources.items()}

    def as_used(n, g):
        if n in ("ev_w_out", "od_w_out", "ff_w_down0", "ff_w_down1"):
            return g.reshape(1, -1, g.shape[-1])
        return g.reshape(N_CHIPS, 1, -1, g.shape[-1])

    first = dict(zip(("ev_w_in", "small"), _gather_weights("gather_first", [bufs["ev_w_in"], bufs["small"]])))
    gathered = dict(zip(SMALL, _unpack(first["small"].reshape(N_CHIPS, -1), [wts[n].shape for n in SMALL])))
    w_full = _full_weights(gathered, wts)
    w_full["ev_w_in"] = as_used("ev_w_in", first["ev_w_in"])
    groups = {"ev_out": ["ev_w_out"], "f0": ["ff_w_up0", "ff_w_down0"], "od": ["od_w_in", "od_w_out"],
              "f1": ["ff_w_up1", "ff_w_down1"]}
    started_gathers, token = _gather_start("gather_start", [[bufs[n] for n in g] for g in groups.values()])
    started_gathers = dict(zip(groups, started_gathers))
    w_full["norm_mix"] = w_full["norm_mix"] + token[0, 0]

    def fetch(tag, after):
        send_sems, recv_sems, group_bufs = started_gathers[tag]
        landed = _gather_wait("gather_wait_" + tag, send_sems, recv_sems, group_bufs, after)
        whole = _gather_weights("gather_siblings_" + tag, landed, from_chips=False)
        return {n: as_used(n, g) for n, g in zip(groups[tag], whole)}

    cm_idx = jnp.concatenate([c_idx, me_idx])
    started = []

    def start_reduction(tag, units):
        names = list(units)
        from_sibling = _halves_to_sibling(f"grads_to_sibling_{tag}", [units[n] for n in names])
        pairs = [_pair_add_placed(f"grads_pair_add_{n}", units[n], r, cm_idx, GRAD_WIRE_DTYPE) for n, r in zip(names, from_sibling)]
        send_sems, recv_sems, sums, zones, token = _scatter_start(
            f"grads_to_chips_start_{tag}", [p[0] for p in pairs], [p[1] for p in pairs])
        started.append((tag, names, send_sems, recv_sems, sums, zones))
        return token[0, 0]

    loss_local, grad_x, grads = _local_step(x[0], loss_target[0], w_full, start_reduction, fetch)
    loss = lax.psum(loss_local, ("x", "y", "c"))

    sg = _shard_grads(grads)
    small_rows = [jnp.concatenate([sg[n][k].reshape(-1) for n in SMALL] + [sg[n].reshape(-1) for n in REPLICATED])
                  for k in range(N_CHIPS)]
    n_el = small_rows[0].shape[0]
    n_rows = -(-n_el // (16 * PACK_W)) * 16
    small_unit = jnp.stack([jnp.pad(r, (0, n_rows * PACK_W - n_el)).reshape(n_rows, PACK_W) for r in small_rows])
    last = {"ev_w_out": grads["ev_w_out"].reshape(N_CHIPS, D_MODEL // N_CHIPS, D_MODEL), "ev_w_in": grads["ev_w_in"],
            "small": small_unit}
    from_sibling = _halves_to_sibling("grads_to_sibling_ev", list(last.values()))
    pairs = [_pair_add_placed(f"grads_pair_add_{n}", u, r, cm_idx, F32 if n == "small" else GRAD_WIRE_DTYPE)
             for (n, u), r in zip(last.items(), from_sibling)]
    ev_send, ev_recv, ev_sums, ev_zones, token = _scatter_start(
        "grads_to_chips_start_ev", [p[0] for p in pairs], [p[1] for p in pairs])
    dests = {"ev_w_in": ("ev_w_in", 0), "od_w_in": ("od_w_in", 0), "ev_w_out": ("ev_w_out", 0), "od_w_out": ("od_w_out", 0),
             "ff_w_up0": ("ff_w_up", 0), "ff_w_up1": ("ff_w_up", 1), "ff_w_down0": ("ff_w_down", 0),
             "ff_w_down1": ("ff_w_down", 1), "small": ("small", 0)}
    outs = {"grad": {}, "delta": {}, "new_m": {}, "new_v": {}}

    def finish(tag, from_chips, results):
        reduced = {}
        for n, part in from_chips.items():
            r, l = dests[n]
            reduced[r] = _sum_chips(f"grads_chip_sum_{n}", part, c_idx, l, 2 if r.startswith("ff_w") else 1,
                                    into=reduced.get(r))
        joined = dict(zip(results, _join_halves("grads_join_" + tag, [reduced[r] for r in results])))
        for n, g in joined.items():
            if n == "small":
                continue
            shape = wts[n].shape
            flat = lambda a: a.reshape(-1, shape[-1])
            new = _adamw("adamw_" + n, flat(wts[n]), flat(g), flat(mom[n]), flat(var[n]))
            for kind, arr in zip(("grad", "delta", "new_m", "new_v"), (g,) + tuple(new)):
                outs[kind][n] = arr.reshape(shape)
        return joined

    from_chips = {}
    for tag, names, send_sems, recv_sems, sums, zones in started:
        from_chips.update(zip(names, _scatter_wait(f"grads_to_chips_wait_{tag}", send_sems, recv_sems, sums, zones, token)))
    finish("layers", from_chips, ["od_w_in", "od_w_out", "ff_w_up", "ff_w_down"])
    from_chips = dict(zip(last, _scatter_wait("grads_to_chips_wait_ev", ev_send, ev_recv, ev_sums, ev_zones,
                                              outs["delta"]["ff_w_up"])))
    joined = finish("ev", from_chips, ["ev_w_in", "ev_w_out", "small"])

    order = SMALL + REPLICATED
    packed = lambda d: jnp.pad(jnp.concatenate([d[n].reshape(-1) for n in order]),
                               (0, n_rows * PACK_W - n_el)).reshape(n_rows, PACK_W)
    g_small = joined["small"].reshape(n_rows, PACK_W)
    new = _adamw("adamw_small", packed(wts), g_small, packed(mom), packed(var))
    for tag, arr in zip(("grad", "delta", "new_m", "new_v"), (g_small,) + tuple(new)):
        outs[tag].update(zip(order, _unpack(arr.reshape(-1), [wts[n].shape for n in order])))
    return (loss, grad_x[None], *[outs["grad"][n] for n in WEIGHTS], *[outs["delta"][n] for n in WEIGHTS],
            *[outs["new_m"][n] for n in WEIGHTS], *[outs["new_v"][n] for n in WEIGHTS])
```

```python
import functools

import jax
import jax.numpy as jnp
from jax import lax
from jax.experimental import pallas as pl
from jax.experimental.pallas import tpu as pltpu

F32 = jnp.float32
BF16 = jnp.bfloat16
MXU_DTYPE = BF16
GRAD_WIRE_DTYPE = BF16
FFN_HIDDEN_DTYPE = BF16

D_MODEL = 1024
N_META = 16
RMS_EPS = 1e-6
LN_EPS = 1e-5
D_A = 512
CONV_A_WIDTH = 31
CONV_B_WIDTH = 3
HEAD_DIM = 64
N_Q_HEADS = 8
N_KV_HEADS = 2
GQA_GROUP = 4
D_ATT = 512
D_KV = 128
BLOCK = 128
ROPE_THETA = 10000.0
D_R = 512
N_R_HEADS = 8
LORA_W = 64
LORA_A = 64
LORA_G = 128
RWKV_GN_EPS = 64e-5
RWKV_COLS = 3 * D_R + LORA_W + LORA_A + LORA_G
D_FF = 2816
NEG_INF = -1e30
ADAM_LR = 0.001
ADAM_B1 = 0.9
ADAM_B2 = 0.999
ADAM_EPS = 1e-08
ADAM_WD = 0.01
ADAM_STEP = 10

N_CHIPS = 4
LANES = 128
CONV_PAD = 32
ROW_TILE_CAP = 704
VMEM_LIMIT_V7X = 56 * 1024 * 1024
MESH = pl.DeviceIdType.MESH


def _cparams(sem=None):
    return pltpu.CompilerParams(dimension_semantics=sem, vmem_limit_bytes=VMEM_LIMIT_V7X)


def _row_tile(t, cap):
    for d in range(min(t, cap), 0, -1):
        if t % d == 0 and d % 16 == 0:
            return d
    return t


def _chunk_len(t):
    for d in (64, 48, 32, 16, 8):
        if t % d == 0:
            return d
    raise ValueError(t)


def _call(fn, name, grid, ins, outs, acc_axis=None, sem=None):
    n_in, n_out = len(ins), len(outs)
    dtype = lambda o: o[4] if len(o) > 4 else F32

    def body(*refs):
        vals = fn(*[r[...] for r in refs[:n_in]])
        if not isinstance(vals, (tuple, list)):
            vals = (vals,)
        for r, v, o in zip(refs[n_in:n_in + n_out], vals, outs):
            if o[3]:
                first = pl.program_id(acc_axis) == 0

                @pl.when(first)
                def _(r=r, v=v):
                    r[...] = v

                @pl.when(jnp.logical_not(first))
                def _(r=r, v=v):
                    r[...] += v
            else:
                r[...] = v.astype(dtype(o))

    res = pl.pallas_call(
        body, name=name, grid=grid,
        in_specs=[pl.BlockSpec(b, m) for _, b, m in ins],
        out_specs=[pl.BlockSpec(o[1], o[2]) for o in outs],
        out_shape=[jax.ShapeDtypeStruct(o[0], dtype(o)) for o in outs],
        compiler_params=_cparams(sem),
    )(*[a for a, _, _ in ins])
    return res if n_out > 1 else res[0]


def _matmul(name, a, b, *, dims, grid, a_spec, b_spec, o_shape, o_spec, acc_shape, nk, k_axis,
            add=None, add_spec=None, out_dtype=F32):
    def product(a_ref, b_ref):
        return lax.dot_general(a_ref[...].astype(MXU_DTYPE), b_ref[...].astype(MXU_DTYPE), dims, preferred_element_type=F32)

    def body_single(*refs):
        a_ref, b_ref, o_ref = refs[0], refs[1], refs[-1]
        res = product(a_ref, b_ref) if add is None else product(a_ref, b_ref) + refs[2][...]
        o_ref[...] = res.astype(out_dtype)

    def body_steps(*refs):
        a_ref, b_ref, o_ref, acc = refs[0], refs[1], refs[-2], refs[-1]
        k = pl.program_id(k_axis)

        @pl.when(k == 0)
        def _():
            if add is None:
                acc[...] = jnp.zeros(acc.shape, F32)
            else:
                acc[...] = refs[2][...]

        acc[...] += product(a_ref, b_ref)

        @pl.when(k == nk - 1)
        def _():
            o_ref[...] = acc[...].astype(out_dtype)

    args = [a, b] + ([] if add is None else [add])
    specs = [a_spec, b_spec] + ([] if add is None else [add_spec])
    return pl.pallas_call(
        body_single if nk == 1 else body_steps, name=name, grid=grid, in_specs=specs, out_specs=o_spec,
        out_shape=jax.ShapeDtypeStruct(o_shape, out_dtype),
        scratch_shapes=[] if nk == 1 else [pltpu.VMEM(acc_shape, F32)],
        compiler_params=_cparams(None),
    )(*args)


MATMUL_BLOCKS_BYTES = 46 * 1024 * 1024


def _whole_if_fits(t, tile, need_bytes):
    return t if need_bytes <= MATMUL_BLOCKS_BYTES else tile


_NN = (((1,), (0,)), ((), ()))
_NT = (((1,), (1,)), ((), ()))
_TN = (((0,), (0,)), ((), ()))


def _mm_cs(name, x, wg, l, tm, out_dtype=F32):
    t, k = x.shape
    s, _, _, n = wg.shape
    tm = _whole_if_fits(t, tm, 2 * (t * k * x.dtype.itemsize + k * n * wg.dtype.itemsize + t * n * 4))
    return _matmul(name, x, wg, dims=_NN, grid=(s, t // tm, 1),
                   a_spec=pl.BlockSpec((tm, k), lambda j, i, kk: (i, 0)),
                   b_spec=pl.BlockSpec((None, None, k, n), lambda j, i, kk: (j, l, 0, 0)),
                   o_shape=(t, s * n), o_spec=pl.BlockSpec((tm, n), lambda j, i, kk: (i, j)),
                   acc_shape=(tm, n), nk=1, k_axis=2, out_dtype=out_dtype)


def _mm_full(name, x, w, l, tm, tk, add=None):
    t, k = x.shape
    n = w.shape[2]
    nk = k // tk
    tm = _whole_if_fits(t, tm, 2 * (t * tk * x.dtype.itemsize + tk * n * w.dtype.itemsize + t * n * 4 * (1 if add is None else 2))
                        + (t * n * 4 if nk > 1 else 0))
    return _matmul(name, x, w, dims=_NN, grid=(t // tm, 1, nk),
                   a_spec=pl.BlockSpec((tm, tk), lambda i, j, kk: (i, kk)),
                   b_spec=pl.BlockSpec((None, tk, n), lambda i, j, kk: (l, kk, 0)),
                   o_shape=(t, n), o_spec=pl.BlockSpec((tm, n), lambda i, j, kk: (i, 0)),
                   acc_shape=(tm, n), nk=nk, k_axis=2,
                   add=add, add_spec=pl.BlockSpec((tm, n), lambda i, j, kk: (i, 0)))


def _mm_nt_cs(name, dy, wg, l, tm, add=None):
    t = dy.shape[0]
    s, _, k, n = wg.shape
    tm = _whole_if_fits(t, tm, 2 * (t * n * dy.dtype.itemsize + k * n * wg.dtype.itemsize + t * k * 4 * (1 if add is None else 2))
                        + t * k * 4)
    return _matmul(name, dy, wg, dims=_NT, grid=(t // tm, 1, s),
                   a_spec=pl.BlockSpec((tm, n), lambda i, j, kk: (i, kk)),
                   b_spec=pl.BlockSpec((None, None, k, n), lambda i, j, kk: (kk, l, 0, 0)),
                   o_shape=(t, k), o_spec=pl.BlockSpec((tm, k), lambda i, j, kk: (i, 0)),
                   acc_shape=(tm, k), nk=s, k_axis=2,
                   add=add, add_spec=pl.BlockSpec((tm, k), lambda i, j, kk: (i, 0)))


def _mm_nt_full(name, dy, w, l, tm, tko):
    t, n = dy.shape
    k = w.shape[1]
    tm = _whole_if_fits(t, tm, 2 * (t * n * dy.dtype.itemsize + tko * n * w.dtype.itemsize + t * tko * 4))
    return _matmul(name, dy, w, dims=_NT, grid=(t // tm, k // tko, 1),
                   a_spec=pl.BlockSpec((tm, n), lambda i, j, kk: (i, 0)),
                   b_spec=pl.BlockSpec((None, tko, n), lambda i, j, kk: (l, j, 0)),
                   o_shape=(t, k), o_spec=pl.BlockSpec((tm, tko), lambda i, j, kk: (i, j)),
                   acc_shape=(tm, tko), nk=1, k_axis=2)


def _mm_tn_cs(name, x, dy, s, tk):
    t, k = x.shape
    n = dy.shape[1] // s
    tk = _whole_if_fits(t, tk, 2 * (t * k * x.dtype.itemsize + t * n * dy.dtype.itemsize + k * n * 4))
    nk = t // tk
    return _matmul(name, x, dy, dims=_TN, grid=(s, 1, nk),
                   a_spec=pl.BlockSpec((tk, k), lambda j, i, kk: (kk, 0)),
                   b_spec=pl.BlockSpec((tk, n), lambda j, i, kk: (kk, j)),
                   o_shape=(s, k, n), o_spec=pl.BlockSpec((None, k, n), lambda j, i, kk: (j, 0, 0)),
                   acc_shape=(k, n), nk=nk, k_axis=2)


def _mm_tn_full(name, y, dh, tk, tko):
    t, k = y.shape
    n = dh.shape[1]
    tk = _whole_if_fits(t, tk, 2 * (t * tko * y.dtype.itemsize + t * n * dh.dtype.itemsize + tko * n * 4))
    nk = t // tk
    return _matmul(name, y, dh, dims=_TN, grid=(k // tko, 1, nk),
                   a_spec=pl.BlockSpec((tk, tko), lambda j, i, kk: (kk, j)),
                   b_spec=pl.BlockSpec((tk, n), lambda j, i, kk: (kk, 0)),
                   o_shape=(k, n), o_spec=pl.BlockSpec((tko, n), lambda j, i, kk: (j, 0)),
                   acc_shape=(tko, n), nk=nk, k_axis=2)


def _sigmoid(x):
    return 1.0 / (1.0 + jnp.exp(-x))


def _rms_fwd(name, h, g, tr):
    t, d = h.shape

    def fn(hv, gv):
        r = lax.rsqrt(jnp.mean(hv * hv, axis=-1, keepdims=True) + RMS_EPS)
        return hv * r * gv

    return _call(fn, name, (t // tr,), [(h, (tr, d), lambda i: (i, 0)), (g, (1, d), lambda i: (0, 0))],
                 [((t, d), (tr, d), lambda i: (i, 0), False, MXU_DTYPE)])


def _rms_bwd(name, h, g, dhn, dh, tr):
    t, d = h.shape

    def fn(hv, gv, dy, dh_in):
        r = lax.rsqrt(jnp.mean(hv * hv, axis=-1, keepdims=True) + RMS_EPS)
        xh = hv * r
        dg = jnp.sum(dy * xh, axis=0, keepdims=True)
        dxh = dy * gv
        dx = r * (dxh - xh * jnp.mean(dxh * xh, axis=-1, keepdims=True))
        return dh_in + dx, dg

    row = lambda i: (i, 0)
    return _call(fn, name, (t // tr,),
                 [(h, (tr, d), row), (g, (1, d), lambda i: (0, 0)), (dhn, (tr, d), row), (dh, (tr, d), row)],
                 [((t, d), (tr, d), row, False), ((1, d), (1, d), lambda i: (0, 0), True)], acc_axis=0)


def _final_loss(name, h, g, tgt, tr):
    t, d = h.shape

    def fn(hv, gv, tv):
        r = lax.rsqrt(jnp.mean(hv * hv, axis=-1, keepdims=True) + RMS_EPS)
        xh = hv * r
        row = pl.program_id(0) * tr + lax.broadcasted_iota(jnp.int32, (tr, 1), 0)
        e = jnp.where(row >= N_META, xh * gv - tv, 0.0)
        loss = jnp.broadcast_to(0.5 * jnp.sum(jnp.sum(e * e, axis=-1, keepdims=True), axis=0, keepdims=True) / d,
                                (8, LANES))
        dout = e / d
        dg = jnp.sum(dout * xh, axis=0, keepdims=True)
        dxh = dout * gv
        dx = r * (dxh - xh * jnp.mean(dxh * xh, axis=-1, keepdims=True))
        return loss, dx, dg

    row = lambda i: (i, 0)
    fix = lambda i: (0, 0)
    return _call(fn, name, (t // tr,), [(h, (tr, d), row), (g, (1, d), fix), (tgt, (tr, d), row)],
                 [((8, LANES), (8, LANES), fix, True), ((t, d), (tr, d), row, False), ((1, d), (1, d), fix, True)],
                 acc_axis=0)


def _silu_ln(uc, g, b):
    mu = jnp.mean(uc, axis=-1, keepdims=True)
    xc = uc - mu
    rs = lax.rsqrt(jnp.mean(xc * xc, axis=-1, keepdims=True) + LN_EPS)
    ln = xc * rs * g + b
    return ln * _sigmoid(ln)


def _even_ln_fwd(name, uc, g, b, tr):
    t, d = uc.shape
    row, fix = (lambda i: (i, 0)), (lambda i: (0, 0))
    return _call(_silu_ln, name, (t // tr,), [(uc, (tr, d), row), (g, (1, d), fix), (b, (1, d), fix)],
                 [((t, d), (tr, d), row, False, MXU_DTYPE)])


def _even_ln_bwd(name, uc, g, b, dy, dy_col, tr):
    t, d = uc.shape

    def fn(ucv, gv, bv, dyv):
        mu = jnp.mean(ucv, axis=-1, keepdims=True)
        xc = ucv - mu
        rs = lax.rsqrt(jnp.mean(xc * xc, axis=-1, keepdims=True) + LN_EPS)
        xh = xc * rs
        ln = xh * gv + bv
        s = _sigmoid(ln)
        dln = dyv * (s * (1.0 + ln * (1.0 - s)))
        dg = jnp.sum(dln * xh, axis=0, keepdims=True)
        db = jnp.sum(dln, axis=0, keepdims=True)
        dxh = dln * gv
        duc = rs * (dxh - jnp.mean(dxh, axis=-1, keepdims=True) - xh * jnp.mean(dxh * xh, axis=-1, keepdims=True))
        return duc, dg, db

    row, fix = (lambda i: (i, 0)), (lambda i: (0, 0))
    return _call(fn, name, (t // tr,),
                 [(uc, (tr, d), row), (g, (1, d), fix), (b, (1, d), fix), (dy, (tr, d), lambda i: (i, dy_col))],
                 [((t, d), (tr, d), row, False), ((1, d), (1, d), fix, True), ((1, d), (1, d), fix, True)], acc_axis=0)


def _windows(t):
    rc = _chunk_len(t)
    return [(r0, rc) for r0 in range(0, t, rc)]


def _taps(w_ref, width):
    return [w_ref[pl.ds(j, 1), :] for j in range(width)]


def _conv_at(xp, taps, r0, rc):
    width = len(taps)
    acc = None
    for j in range(width):
        term = xp[pl.ds(CONV_PAD - (width - 1) + j + r0, rc), :] * taps[j]
        acc = term if acc is None else acc + term
    return acc


def _conv_bwd_in_at(dyp, taps, r0, rc):
    width = len(taps)
    acc = None
    for j in range(width):
        term = dyp[pl.ds(width - 1 - j + r0, rc), :] * taps[j]
        acc = term if acc is None else acc + term
    return acc


def _fold(x):
    acc = x[0:8]
    for i in range(1, x.shape[0] // 8):
        acc = acc + x[8 * i:8 * (i + 1)]
    return acc


def _add_to(accs, vals):
    return vals if accs is None else [a + v for a, v in zip(accs, vals)]


def _conv_bwd_w_at(dy, xp, width, r0, rc):
    return [_fold(dy * xp[pl.ds(CONV_PAD - (width - 1) + j + r0, rc), :]) for j in range(width)]


def _store_taps(dw_ref, accs):
    for j, a in enumerate(accs):
        dw_ref[pl.ds(j, 1), :] = jnp.sum(a, axis=0, keepdims=True)


WIDE_COLS = 2 * LANES


def _zero_front(xp):
    xp[pl.ds(0, CONV_PAD), :] = jnp.zeros((CONV_PAD, xp.shape[1]), F32)


def _zero_back(dyp, t):
    dyp[pl.ds(t, CONV_PAD), :] = jnp.zeros((CONV_PAD, dyp.shape[1]), F32)


def _col_call(body, name, ncol, ins, outs, t, n_scratch, cols=LANES):
    def spec(rows, off):
        return pl.BlockSpec((rows, cols), lambda j, off=off: (0, j + off))

    res = pl.pallas_call(
        body, name=name, grid=(ncol,),
        in_specs=[spec(r, off) for _, r, off in ins],
        out_specs=[spec(o[0], 0) for o in outs],
        out_shape=[jax.ShapeDtypeStruct(o[:2], o[2] if len(o) > 2 else F32) for o in outs],
        scratch_shapes=[pltpu.VMEM((t + CONV_PAD, cols), F32) for _ in range(n_scratch)],
        compiler_params=_cparams(None),
    )(*[a for a, _, _ in ins])
    return res


def _even_col_fwd(name, p, conv_a, conv_b):
    t = p.shape[0]
    nc = D_A // LANES
    wins = _windows(t)

    def body(av, ag, gb, gc, xi, ca, cb, uc_ref, yb_ref, xp):
        _zero_front(xp)
        for r0, rc in wins:
            rows = pl.ds(r0, rc)
            xp[pl.ds(CONV_PAD + r0, rc), :] = av[rows, :] * _sigmoid(ag[rows, :])
        taps = _taps(ca, CONV_A_WIDTH)
        for r0, rc in wins:
            uc_ref[pl.ds(r0, rc), :] = _conv_at(xp, taps, r0, rc)
        for r0, rc in wins:
            rows = pl.ds(r0, rc)
            xp[pl.ds(CONV_PAD + r0, rc), :] = gc[rows, :] * xi[rows, :]
        taps = _taps(cb, CONV_B_WIDTH)
        for r0, rc in wins:
            rows = pl.ds(r0, rc)
            yb_ref[rows, :] = (gb[rows, :] * _conv_at(xp, taps, r0, rc)).astype(yb_ref.dtype)

    ins = [(p, t, k * nc) for k in range(5)] + [(conv_a, CONV_A_WIDTH, 0), (conv_b, CONV_B_WIDTH, 0)]
    return _col_call(body, name, nc, ins, [(t, D_A), (t, D_A, MXU_DTYPE)], t, 1)


def _even_col_bwd(name, p, duc, dy, conv_a, conv_b):
    t = p.shape[0]
    nc = D_A // LANES
    wins = _windows(t)

    def body(av, ag, gb, gc, xi, duc_ref, dyb_ref, ca, cb, dav, dag, dgb, dgc, dxi, dca, dcb, xp, dyp):
        _zero_front(xp)
        _zero_back(dyp, t)
        for r0, rc in wins:
            rows = pl.ds(r0, rc)
            xp[pl.ds(CONV_PAD + r0, rc), :] = av[rows, :] * _sigmoid(ag[rows, :])
            dyp[rows, :] = duc_ref[rows, :]
        taps = _taps(ca, CONV_A_WIDTH)
        accs = None
        for r0, rc in wins:
            rows = pl.ds(r0, rc)
            accs = _add_to(accs, _conv_bwd_w_at(duc_ref[rows, :], xp, CONV_A_WIDTH, r0, rc))
            du = _conv_bwd_in_at(dyp, taps, r0, rc)
            sig = _sigmoid(ag[rows, :])
            dav[rows, :] = (du * sig).astype(dav.dtype)
            dag[rows, :] = (du * av[rows, :] * sig * (1.0 - sig)).astype(dag.dtype)
        _store_taps(dca, accs)
        for r0, rc in wins:
            rows = pl.ds(r0, rc)
            xp[pl.ds(CONV_PAD + r0, rc), :] = gc[rows, :] * xi[rows, :]
        taps = _taps(cb, CONV_B_WIDTH)
        accs = None
        for r0, rc in wins:
            rows = pl.ds(r0, rc)
            dgb[rows, :] = (dyb_ref[rows, :] * _conv_at(xp, taps, r0, rc)).astype(dgb.dtype)
            dzc = dyb_ref[rows, :] * gb[rows, :]
            dyp[rows, :] = dzc
            accs = _add_to(accs, _conv_bwd_w_at(dzc, xp, CONV_B_WIDTH, r0, rc))
        _store_taps(dcb, accs)
        for r0, rc in wins:
            rows = pl.ds(r0, rc)
            dz = _conv_bwd_in_at(dyp, taps, r0, rc)
            dgc[rows, :] = (dz * xi[rows, :]).astype(dgc.dtype)
            dxi[rows, :] = (dz * gc[rows, :]).astype(dxi.dtype)

    ins = ([(p, t, k * nc) for k in range(5)] + [(duc, t, 0), (dy, t, nc)]
           + [(conv_a, CONV_A_WIDTH, 0), (conv_b, CONV_B_WIDTH, 0)])
    outs = [(t, D_A, MXU_DTYPE)] * 5 + [(CONV_A_WIDTH, D_A), (CONV_B_WIDTH, D_A)]
    return _col_call(body, name, nc, ins, outs, t, 2)


def _ffn_col_fwd(name, u, conv, bias):
    t = u.shape[0]
    nc = D_FF // WIDE_COLS
    wins = _windows(t)

    def body(g_ref, v_ref, cw, b_ref, a_ref, xp):
        _zero_front(xp)
        xp[pl.ds(CONV_PAD, t), :] = g_ref[...].astype(F32)
        taps = _taps(cw, CONV_B_WIDTH)
        b = b_ref[...]
        for r0, rc in wins:
            rows = pl.ds(r0, rc)
            gc = _conv_at(xp, taps, r0, rc) + b
            a_ref[rows, :] = (gc * _sigmoid(gc) * v_ref[rows, :].astype(F32)).astype(a_ref.dtype)

    ins = [(u, t, 0), (u, t, nc), (conv, CONV_B_WIDTH, 0), (bias, 1, 0)]
    return _col_call(body, name, nc, ins, [(t, D_FF, MXU_DTYPE)], t, 1, cols=WIDE_COLS)[0]


def _ffn_col_bwd(name, u, da, conv, bias):
    t = u.shape[0]
    nc = D_FF // LANES
    wins = _windows(t)

    def body(g_ref, v_ref, da_ref, cw, b_ref, du_ref, dcw, db_ref, xp, dyp, dval):
        @pl.when(pl.program_id(1) == 0)
        def _():
            _zero_front(xp)
            _zero_back(dyp, t)
            xp[pl.ds(CONV_PAD, t), :] = g_ref[...].astype(F32)
            taps = _taps(cw, CONV_B_WIDTH)
            b = b_ref[...]
            accs, bias_acc = None, None
            for r0, rc in wins:
                rows = pl.ds(r0, rc)
                gc = _conv_at(xp, taps, r0, rc) + b
                s = _sigmoid(gc)
                d = da_ref[rows, :]
                dval[rows, :] = d * gc * s
                dgc = d * v_ref[rows, :].astype(F32) * (s * (1.0 + gc * (1.0 - s)))
                dyp[rows, :] = dgc
                bias_acc = _add_to(bias_acc, [_fold(dgc)])
                accs = _add_to(accs, _conv_bwd_w_at(dgc, xp, CONV_B_WIDTH, r0, rc))
            db_ref[...] = jnp.sum(bias_acc[0], axis=0, keepdims=True)
            _store_taps(dcw, accs)
            for r0, rc in wins:
                du_ref[pl.ds(r0, rc), :] = _conv_bwd_in_at(dyp, taps, r0, rc).astype(du_ref.dtype)

        @pl.when(pl.program_id(1) == 1)
        def _():
            du_ref[...] = dval[...].astype(du_ref.dtype)

    col = lambda rows, off: pl.BlockSpec((rows, LANES), lambda j, p: (0, j + off))
    return pl.pallas_call(
        body, name=name, grid=(nc, 2),
        in_specs=[col(t, 0), col(t, nc), col(t, 0), col(CONV_B_WIDTH, 0), col(1, 0)],
        out_specs=[pl.BlockSpec((t, LANES), lambda j, p: (0, j + nc * p)), col(CONV_B_WIDTH, 0), col(1, 0)],
        out_shape=[jax.ShapeDtypeStruct((t, 2 * D_FF), MXU_DTYPE), jax.ShapeDtypeStruct((CONV_B_WIDTH, D_FF), F32),
                   jax.ShapeDtypeStruct((1, D_FF), F32)],
        scratch_shapes=[pltpu.VMEM((t + CONV_PAD, LANES), F32) for _ in range(2)] + [pltpu.VMEM((t, LANES), F32)],
        compiler_params=_cparams(None),
    )(u, u, da, conv, bias)


def _shift_fwd(name, p, col0, mu):
    t = p.shape[0]
    wins = _windows(t)

    def body(x_ref, mu_ref, o_ref, xp):
        _zero_front(xp)
        xp[pl.ds(CONV_PAD, t), :] = x_ref[...]
        mu_v = mu_ref[...]
        for r0, rc in wins:
            rows = pl.ds(r0, rc)
            x = x_ref[rows, :]
            o_ref[rows, :] = x + (xp[pl.ds(CONV_PAD - 1 + r0, rc), :] - x) * mu_v

    return _col_call(body, name, RWKV_COLS // WIDE_COLS, [(p, t, col0 // WIDE_COLS), (mu, 1, 0)], [(t, RWKV_COLS)], t, 1,
                     cols=WIDE_COLS)[0]


def _shift_bwd(name, p, col0, mu, dprs):
    t = p.shape[0]
    wins = _windows(t)

    def body(x_ref, mu_ref, d_ref, dx_ref, dmu_ref, xp, dyp):
        _zero_front(xp)
        _zero_back(dyp, t)
        xp[pl.ds(CONV_PAD, t), :] = x_ref[...]
        mu_v = mu_ref[...]
        acc = None
        for r0, rc in wins:
            rows = pl.ds(r0, rc)
            d = d_ref[rows, :]
            acc = _add_to(acc, [_fold(d * (xp[pl.ds(CONV_PAD - 1 + r0, rc), :] - x_ref[rows, :]))])
            dyp[rows, :] = d * mu_v
        dmu_ref[...] = jnp.sum(acc[0], axis=0, keepdims=True)
        for r0, rc in wins:
            rows = pl.ds(r0, rc)
            dx_ref[rows, :] = d_ref[rows, :] - dyp[rows, :] + dyp[pl.ds(1 + r0, rc), :]

    ins = [(p, t, col0 // WIDE_COLS), (mu, 1, 0), (dprs, t, 0)]
    return _col_call(body, name, RWKV_COLS // WIDE_COLS, ins, [(t, RWKV_COLS), (1, RWKV_COLS)], t, 2, cols=WIDE_COLS)


def _hi_lo(x):
    hi = x.astype(BF16)
    return hi, (x - hi.astype(F32)).astype(BF16)


def _dot_passes(a, b, dims, passes):
    d = lambda p, q: lax.dot_general(p, q, dims, preferred_element_type=F32)
    if passes == 1:
        return d(a.astype(MXU_DTYPE), b.astype(MXU_DTYPE))
    ah, al = _hi_lo(a)
    bh, bl = _hi_lo(b)
    return d(ah, bh) + (d(ah, bl) + d(al, bh))


@functools.partial(jax.custom_vjp, nondiff_argnums=(2, 3))
def _dot_vjp(a, b, dims, passes):
    return _dot_passes(a, b, dims, passes)


def _dot_fwd(a, b, dims, passes):
    return _dot_passes(a, b, dims, passes), (a, b)


def _dot_bwd(dims, passes, res, g):
    a, b = res
    if dims == _NN:
        return _dot_passes(g, b, _NT, passes), _dot_passes(a, g, _TN, passes)
    if dims == _NT:
        return _dot_passes(g, b, _NN, passes), _dot_passes(g, a, _TN, passes)
    return _dot_passes(b, g, _NT, passes), _dot_passes(a, g, _NN, passes)


_dot_vjp.defvjp(_dot_fwd, _dot_bwd)


def _doth(a, b, dims=_NN):
    return _dot_vjp(a, b, dims, 3)


def _dotb(a, b, dims=_NN):
    return _dot_vjp(a, b, dims, 1)


def _softplus(x):
    return jnp.where(x > 0, x, 0.0) + jnp.log(1.0 + jnp.exp(jnp.where(x > 0, -x, x)))


def _rwkv_pre(k, xl, gd, w0, w2p, a0, a2p, g2, k_k, k_a, seg):
    z = w0 + _dotb(jnp.tanh(xl), w2p)
    lw = -jnp.exp(-_softplus(-z) - 0.5)
    alpha = _sigmoid(a0 + _dotb(xl, a2p))
    g = _dotb(_sigmoid(gd), g2)
    kk = k * k_k
    kk = kk / jnp.maximum(jnp.sqrt(_dotb(kk * kk, seg)), 1e-12)
    k2 = k * (1.0 + (alpha - 1.0) * k_a)
    return lw, k2, -kk, kk * alpha, g


def _rwkv_post(y, r, k2, v, g, lnx_g, lnx_b, r_k, seg):
    mean = _dotb(y, seg) * (1.0 / HEAD_DIM)
    yc = y - mean
    var = _dotb(yc * yc, seg) * (1.0 / HEAD_DIM)
    yo = yc * lax.rsqrt(var + RWKV_GN_EPS) * lnx_g + lnx_b
    bonus = _dotb(r * k2 * r_k, seg) * v
    return (yo + bonus) * g


def _rwkv_pre_fwd(name, prs, prm, seg, tr):
    t = prs.shape[0]
    row = lambda i: (i, 0)
    fix = lambda i: (0, 0)
    ins = [(prs, (tr, D_R), lambda i: (i, 1)), (prs, (tr, LANES), lambda i: (i, 12)), (prs, (tr, LANES), lambda i: (i, 13)),
           (prm["w0"], (1, D_R), fix), (prm["w2p"], (LANES, D_R), fix), (prm["a0"], (1, D_R), fix),
           (prm["a2p"], (LANES, D_R), fix), (prm["g2"], (LANES, D_R), fix), (prm["k_k"], (1, D_R), fix),
           (prm["k_a"], (1, D_R), fix), (seg, (D_R, D_R), fix)]
    return _call(_rwkv_pre, name, (t // tr,), ins, [((t, D_R), (tr, D_R), row, False)] * 5)


def _rwkv_pre_bwd(name, prs, prm, seg, cts, tr):
    t = prs.shape[0]

    def fn(k, xl, gd, w0, w2p, a0, a2p, g2, k_k, k_a, segv, *ct):
        _, vjp = jax.vjp(lambda *a: _rwkv_pre(*a, segv), k, xl, gd, w0, w2p, a0, a2p, g2, k_k, k_a)
        return vjp(tuple(ct))

    row = lambda i: (i, 0)
    fix = lambda i: (0, 0)
    ins = [(prs, (tr, D_R), lambda i: (i, 1)), (prs, (tr, LANES), lambda i: (i, 12)), (prs, (tr, LANES), lambda i: (i, 13)),
           (prm["w0"], (1, D_R), fix), (prm["w2p"], (LANES, D_R), fix), (prm["a0"], (1, D_R), fix),
           (prm["a2p"], (LANES, D_R), fix), (prm["g2"], (LANES, D_R), fix), (prm["k_k"], (1, D_R), fix),
           (prm["k_a"], (1, D_R), fix), (seg, (D_R, D_R), fix)] + [(c, (tr, D_R), row) for c in cts]
    outs = [((t, D_R), (tr, D_R), row, False), ((t, LANES), (tr, LANES), row, False), ((t, LANES), (tr, LANES), row, False),
            ((1, D_R), (1, D_R), fix, True), ((LANES, D_R), (LANES, D_R), fix, True), ((1, D_R), (1, D_R), fix, True),
            ((LANES, D_R), (LANES, D_R), fix, True), ((LANES, D_R), (LANES, D_R), fix, True),
            ((1, D_R), (1, D_R), fix, True), ((1, D_R), (1, D_R), fix, True)]
    return _call(fn, name, (t // tr,), ins, outs, acc_axis=0)


def _rwkv_post_ins(y, prs, k2, g, prm, seg, tr):
    row = lambda i: (i, 0)
    fix = lambda i: (0, 0)
    return [(y, (tr, D_R), row), (prs, (tr, D_R), row), (k2, (tr, D_R), row), (prs, (tr, D_R), lambda i: (i, 2)),
            (g, (tr, D_R), row), (prm["lnx_g"], (1, D_R), fix), (prm["lnx_b"], (1, D_R), fix), (prm["r_k"], (1, D_R), fix),
            (seg, (D_R, D_R), fix)]


def _rwkv_post_fwd(name, y, prs, k2, g, prm, seg, tr):
    t = y.shape[0]
    return _call(_rwkv_post, name, (t // tr,), _rwkv_post_ins(y, prs, k2, g, prm, seg, tr),
                 [((t, D_R), (tr, D_R), lambda i: (i, 0), False)])


def _rwkv_post_bwd(name, y, prs, k2, g, prm, seg, dy, dy_col, tr):
    t = y.shape[0]

    def fn(yv, r, k2v, v, gv, lg, lb, rk, segv, ct):
        _, vjp = jax.vjp(lambda *a: _rwkv_post(*a, segv), yv, r, k2v, v, gv, lg, lb, rk)
        return vjp(ct)

    row = lambda i: (i, 0)
    fix = lambda i: (0, 0)
    ins = _rwkv_post_ins(y, prs, k2, g, prm, seg, tr) + [(dy, (tr, D_R), lambda i: (i, dy_col))]
    outs = [((t, D_R), (tr, D_R), row, False)] * 5 + [((1, D_R), (1, D_R), fix, True)] * 3
    return _call(fn, name, (t // tr,), ins, outs, acc_axis=0)


def _wkv_chunk(s0, r, lw, k, v, a, b):
    c = r[0].shape[0]
    lane = lax.broadcasted_iota(jnp.int32, (1, 2 * HEAD_DIM), 1)
    first = (lane < HEAD_DIM).astype(F32)
    per_head = lambda x: jnp.concatenate([x * first, x * (1.0 - first)], axis=0)

    def time_of(shape, dim):
        i = lax.broadcasted_iota(jnp.int32, shape, dim)
        return jnp.where(i >= c, i - c, i)

    incl = (lax.broadcasted_iota(jnp.int32, (c, c), 0) >= lax.broadcasted_iota(jnp.int32, (c, c), 1)).astype(F32)
    strict2 = time_of((2 * c, 2 * c), 0) > time_of((2 * c, 2 * c), 1)
    incl2 = lax.broadcasted_iota(jnp.int32, (c, 2 * c), 0) >= time_of((c, 2 * c), 1)
    each = lambda f, *xs: [f(*x) for x in zip(*xs)]
    cum = each(lambda x: _doth(incl, x), lw)
    tot = each(lambda x: jnp.sum(x, axis=0, keepdims=True), lw)
    e_inv = each(lambda x: jnp.exp(-x), cum)
    a_st = each(lambda x, cm, l: per_head(x * jnp.exp(cm - l)), a, cum, lw)
    r_t = each(lambda x, cm: x * jnp.exp(cm), r, cum)
    b_st = each(lambda x, e: per_head(x * e), b, e_inv)
    k_st = each(lambda x, e: per_head(x * e), k, e_inv)
    v_st = each(per_head, v)
    m = each(lambda x, w: jnp.where(strict2, _dotb(x, w, _NT), 0.0), a_st, b_st)
    m_k = each(lambda x, w: jnp.where(strict2, _dotb(x, w, _NT), 0.0), a_st, k_st)
    u = each(lambda x, s, mk, w: _dotb(x, s, _NT) + _dotb(mk, w), a_st, s0, m_k, v_st)
    steps = (c - 1).bit_length()
    for s in range(steps):
        u = each(lambda x, w: x + _dotb(w, x), u, m)
        if s + 1 < steps:
            m = each(lambda w: _dotb(w, w), m)
    n_b = each(lambda x, w: jnp.where(incl2, _dotb(x, w, _NT), 0.0), r_t, b_st)
    n_k = each(lambda x, w: jnp.where(incl2, _dotb(x, w, _NT), 0.0), r_t, k_st)
    y = each(lambda x, s, nb, uu, nk, w: _dotb(x, s, _NT) + _dotb(nb, uu) + _dotb(nk, w), r_t, s0, n_b, u, n_k, v_st)
    dec = each(lambda tt, cm: jnp.exp(tt - cm), tot, cum)
    s1 = each(lambda s, tt, uu, x, d, w, kk: s * jnp.exp(tt) + _dotb(uu, per_head(x * d), _TN) + _dotb(w, per_head(kk * d), _TN),
              s0, tot, u, b, dec, v_st, k)
    return tuple(y), tuple(s1)


WKV_PAIRS_PER_STEP = 4
PAIR = 2 * HEAD_DIM


def _wkv_fwd(name, srcs):
    t = srcs[0][0].shape[0]
    c = _chunk_len(t)
    nc = t // c
    pp = WKV_PAIRS_PER_STEP
    n_pairs = D_R // PAIR

    def body(r, lw, k, v, a, b, y_ref, st_ref, state):
        @pl.when(pl.program_id(1) == 0)
        def _():
            state[...] = jnp.zeros(state.shape, F32)

        pairs = lambda ref: tuple(ref[:, pl.ds(i * PAIR, PAIR)] for i in range(pp))
        s0 = tuple(state[i] for i in range(pp))
        y, s1 = _wkv_chunk(s0, pairs(r), pairs(lw), pairs(k), pairs(v), pairs(a), pairs(b))
        for i in range(pp):
            st_ref[i] = s0[i]
            y_ref[:, pl.ds(i * PAIR, PAIR)] = y[i]
            state[i] = s1[i]

    seq = lambda off: pl.BlockSpec((c, pp * PAIR), lambda g, j: (j, off + g))
    return pl.pallas_call(
        body, name=name, grid=(n_pairs // pp, nc), in_specs=[seq(off) for _, off in srcs],
        out_specs=[seq(0), pl.BlockSpec((pp, None, PAIR, PAIR), lambda g, j: (g, j, 0, 0))],
        out_shape=[jax.ShapeDtypeStruct((t, D_R), F32), jax.ShapeDtypeStruct((n_pairs, nc, PAIR, PAIR), F32)],
        scratch_shapes=[pltpu.VMEM((pp, PAIR, PAIR), F32)],
        compiler_params=_cparams(None),
    )(*[a for a, _ in srcs])


def _wkv_bwd(name, srcs, st, dy):
    t = srcs[0][0].shape[0]
    c = _chunk_len(t)
    nc = t // c
    pp = WKV_PAIRS_PER_STEP
    n_pairs = D_R // PAIR

    def body(r, lw, k, v, a, b, st_ref, dy_ref, dr, dlw, dk, dv, da, db, dstate):
        @pl.when(pl.program_id(1) == 0)
        def _():
            dstate[...] = jnp.zeros(dstate.shape, F32)

        half = lax.broadcasted_iota(jnp.int32, (PAIR, PAIR), 0) < HEAD_DIM
        same_head = half == (lax.broadcasted_iota(jnp.int32, (PAIR, PAIR), 1) < HEAD_DIM)
        pairs = lambda ref: tuple(ref[:, pl.ds(i * PAIR, PAIR)] for i in range(pp))
        s0 = tuple(st_ref[i] for i in range(pp))
        _, vjp = jax.vjp(_wkv_chunk, s0, pairs(r), pairs(lw), pairs(k), pairs(v), pairs(a), pairs(b))
        ds0, *dxs = vjp((pairs(dy_ref), tuple(dstate[i] for i in range(pp))))
        for i in range(pp):
            for ref, val in zip((dr, dlw, dk, dv, da, db), dxs):
                ref[:, pl.ds(i * PAIR, PAIR)] = val[i]
            dstate[i] = jnp.where(same_head, ds0[i], 0.0)

    seq = lambda off: pl.BlockSpec((c, pp * PAIR), lambda g, j: (nc - 1 - j, off + g))
    return pl.pallas_call(
        body, name=name, grid=(n_pairs // pp, nc),
        in_specs=[seq(off) for _, off in srcs]
        + [pl.BlockSpec((pp, None, PAIR, PAIR), lambda g, j: (g, nc - 1 - j, 0, 0)), seq(dy[1])],
        out_specs=[seq(0)] * 6,
        out_shape=[jax.ShapeDtypeStruct((t, D_R), F32)] * 6,
        scratch_shapes=[pltpu.VMEM((pp, PAIR, PAIR), F32)],
        compiler_params=_cparams(None),
    )(*[a for a, _ in srcs], st, dy[0])


def _rope(x, cos, sin, rot):
    return x * cos + _dotb(x, rot) * sin


def _attn_block(nb, q, kp, kc, km, vp, vc, vm, sk, cq, sq, cp, sp, cm, sm, rot):
    g = GQA_GROUP
    scale = HEAD_DIM ** -0.5
    each = lambda f, *xs: [f(*x) for x in zip(*xs)]
    down = lambda x: jnp.concatenate([x] * g, axis=0)
    cq4, sq4 = down(cq), down(sq)
    kpr = each(lambda x: _rope(x, cp, sp, rot), kp)
    kcr = each(lambda x: _rope(x, cq, sq, rot), kc)
    kmr = each(lambda x: _rope(x, cm, sm, rot), km)
    qr = each(lambda x: _rope(x, cq4, sq4, rot), q)
    i = lax.broadcasted_iota(jnp.int32, (g * BLOCK, BLOCK), 0)
    i = i - BLOCK * ((i >= BLOCK).astype(jnp.int32) + (i >= 2 * BLOCK).astype(jnp.int32) + (i >= 3 * BLOCK).astype(jnp.int32))
    j = lax.broadcasted_iota(jnp.int32, (g * BLOCK, BLOCK), 1)
    nbv = jnp.zeros((g * BLOCK, BLOCK), jnp.int32) + nb
    ok_p = (j > i) & (nbv >= 2)
    ok_c = (j <= i) & (nbv >= 1)
    ok_m = (j >= BLOCK - N_META) & ((nbv >= 1) | (j <= i))
    sink = each(lambda s4: jnp.concatenate([jnp.broadcast_to(s, (BLOCK, 1)) for s in s4], axis=0), sk)
    s_p = each(lambda x, kk: jnp.where(ok_p, _dotb(x, kk, _NT) * scale, NEG_INF), qr, kpr)
    s_c = each(lambda x, kk: jnp.where(ok_c, _dotb(x, kk, _NT) * scale, NEG_INF), qr, kcr)
    s_m = each(lambda x, kk: jnp.where(ok_m, _dotb(x, kk, _NT) * scale, NEG_INF), qr, kmr)
    rmax = lambda s: jnp.max(s, axis=-1, keepdims=True)
    m = each(lambda a, b, c, d: lax.stop_gradient(jnp.maximum(jnp.maximum(rmax(a), rmax(b)), jnp.maximum(rmax(c), d))),
             s_p, s_c, s_m, sink)
    e_p = each(lambda s, mm: jnp.exp(s - mm), s_p, m)
    e_c = each(lambda s, mm: jnp.exp(s - mm), s_c, m)
    e_m = each(lambda s, mm: jnp.exp(s - mm), s_m, m)
    rsum = lambda e: jnp.sum(e, axis=-1, keepdims=True)
    inv = each(lambda a, b, c, d, mm: 1.0 / (rsum(a) + rsum(b) + rsum(c) + jnp.exp(d - mm)), e_p, e_c, e_m, sink, m)
    return tuple(each(lambda a, b, c, iv, x, y, z: _dotb(a * iv, x) + _dotb(b * iv, y) + _dotb(c * iv, z),
                      e_p, e_c, e_m, inv, vp, vc, vm))


def _attn_specs():
    cur = lambda n: (0, n, 0)
    prev = lambda n: (0, jnp.maximum(n - 1, 0), 0)
    meta = lambda n: (0, 0, 0)
    kv = lambda m: pl.BlockSpec((N_KV_HEADS, BLOCK, HEAD_DIM), m)
    tab = lambda m: pl.BlockSpec((BLOCK, HEAD_DIM), m)
    tcur, tprev, tmeta = (lambda n: (n, 0)), (lambda n: (jnp.maximum(n - 1, 0), 0)), (lambda n: (0, 0))
    qspec = pl.BlockSpec((N_Q_HEADS, BLOCK, HEAD_DIM), cur)
    sspec = pl.BlockSpec((N_Q_HEADS, 8, LANES), meta)
    specs = [qspec, kv(prev), kv(cur), kv(meta), kv(prev), kv(cur), kv(meta), sspec,
             tab(tcur), tab(tcur), tab(tprev), tab(tprev), tab(tmeta), tab(tmeta),
             pl.BlockSpec((HEAD_DIM, HEAD_DIM), lambda n: (0, 0))]
    return specs, qspec, sspec, kv


def _attn_args(q, k, v, sinks_b, cos, sin, rot):
    return (q, k, k, k, v, v, v, sinks_b, cos, sin, cos, sin, cos, sin, rot)


def _attn_operands(q_ref, kp, kc, km, vp, vc, vm, s_ref):
    groups = range(N_KV_HEADS)
    q = tuple(jnp.concatenate([q_ref[GQA_GROUP * i + h] for h in range(GQA_GROUP)], axis=0) for i in groups)
    sk = tuple(tuple(s_ref[GQA_GROUP * i + h][0:1, 0:1] for h in range(GQA_GROUP)) for i in groups)
    per_head = lambda ref: tuple(ref[i] for i in groups)
    return q, per_head(kp), per_head(kc), per_head(km), per_head(vp), per_head(vc), per_head(vm), sk


def _attn_fwd(name, q, k, v, sinks_b, cos, sin, rot):
    tp = q.shape[1]
    specs, qspec, _, _ = _attn_specs()

    def body(q_ref, kp, kc, km, vp, vc, vm, s_ref, cq, sq, cp, sp, cm, sm, rot_ref, o_ref):
        out = _attn_block(pl.program_id(0), *_attn_operands(q_ref, kp, kc, km, vp, vc, vm, s_ref),
                          cq[...], sq[...], cp[...], sp[...], cm[...], sm[...], rot_ref[...])
        for i in range(N_KV_HEADS):
            for h in range(GQA_GROUP):
                o_ref[GQA_GROUP * i + h] = out[i][h * BLOCK:(h + 1) * BLOCK]

    return pl.pallas_call(
        body, name=name, grid=(tp // BLOCK,), in_specs=specs, out_specs=qspec,
        out_shape=jax.ShapeDtypeStruct(q.shape, F32), compiler_params=_cparams(None),
    )(*_attn_args(q, k, v, sinks_b, cos, sin, rot))


def _attn_bwd(name, q, k, v, sinks_b, cos, sin, rot, do):
    tp = q.shape[1]
    nb = tp // BLOCK
    specs, qspec, sspec, kv = _attn_specs()

    def body(q_ref, kp, kc, km, vp, vc, vm, s_ref, cq, sq, cp, sp, cm, sm, rot_ref, do_ref,
             dq_ref, dkp, dkc, dvp, dvc, dkm, dvm, ds_ref):
        n = pl.program_id(0)
        tabs = (cq[...], sq[...], cp[...], sp[...], cm[...], sm[...], rot_ref[...])
        _, vjp = jax.vjp(lambda *a: _attn_block(n, *a, *tabs), *_attn_operands(q_ref, kp, kc, km, vp, vc, vm, s_ref))
        do_all = tuple(jnp.concatenate([do_ref[GQA_GROUP * i + h] for h in range(GQA_GROUP)], axis=0)
                       for i in range(N_KV_HEADS))
        dq, gkp, gkc, gkm, gvp, gvc, gvm, dsk = vjp(do_all)
        for i in range(N_KV_HEADS):
            dkp[i] = gkp[i]
            dkc[i] = gkc[i]
            dvp[i] = gvp[i]
            dvc[i] = gvc[i]
            for h in range(GQA_GROUP):
                dq_ref[GQA_GROUP * i + h] = dq[i][h * BLOCK:(h + 1) * BLOCK]

        @pl.when(n == 0)
        def _():
            for i in range(N_KV_HEADS):
                dkm[i] = gkm[i]
                dvm[i] = gvm[i]
                for h in range(GQA_GROUP):
                    ds_ref[GQA_GROUP * i + h] = jnp.broadcast_to(dsk[i][h], (8, LANES))

        @pl.when(n != 0)
        def _():
            for i in range(N_KV_HEADS):
                dkm[i] += gkm[i]
                dvm[i] += gvm[i]
                for h in range(GQA_GROUP):
                    ds_ref[GQA_GROUP * i + h] += jnp.broadcast_to(dsk[i][h], (8, LANES))

    part = pl.BlockSpec((N_KV_HEADS, None, BLOCK, HEAD_DIM), lambda n: (0, n, 0, 0))
    part_shape = jax.ShapeDtypeStruct((N_KV_HEADS, nb, BLOCK, HEAD_DIM), F32)
    meta_shape = jax.ShapeDtypeStruct((N_KV_HEADS, BLOCK, HEAD_DIM), F32)
    return pl.pallas_call(
        body, name=name, grid=(nb,), in_specs=specs + [qspec],
        out_specs=[qspec, part, part, part, part, kv(lambda n: (0, 0, 0)), kv(lambda n: (0, 0, 0)), sspec],
        out_shape=[jax.ShapeDtypeStruct(q.shape, F32), part_shape, part_shape, part_shape, part_shape,
                   meta_shape, meta_shape, jax.ShapeDtypeStruct(sinks_b.shape, F32)],
        compiler_params=_cparams(None),
    )(*_attn_args(q, k, v, sinks_b, cos, sin, rot), do)


def _kv_combine(name, prev_part, own_part, meta):
    g, nb = own_part.shape[:2]

    def fn(own, nxt, mt):
        m = pl.program_id(1)
        one = jnp.ones((BLOCK, HEAD_DIM), F32)
        use_next = jnp.where(one * m < nb - 1, 1.0, 0.0)
        use_meta = jnp.where(one * m < 1, 1.0, 0.0)
        return own + nxt * use_next + mt * use_meta

    blk = (None, None, BLOCK, HEAD_DIM)
    return _call(fn, name, (g, nb),
                 [(own_part, blk, lambda a, m: (a, m, 0, 0)),
                  (prev_part, blk, lambda a, m: (a, jnp.minimum(m + 1, nb - 1), 0, 0)),
                  (meta, (None, BLOCK, HEAD_DIM), lambda a, m: (a, 0, 0))],
                 [((g, nb * BLOCK, HEAD_DIM), (None, BLOCK, HEAD_DIM), lambda a, m: (a, m, 0), False)])


PACK_W = 1024
ELEMENTWISE_BLOCK_BYTES = 1 << 21


def _rows_tile(rows, cols):
    cap = max(8, ELEMENTWISE_BLOCK_BYTES // (4 * cols))
    for d in range(min(rows, cap), 0, -1):
        if rows % d == 0 and d % 8 == 0:
            return d
    return rows


def _adamw(name, w, g, m, v):
    rows, cols = w.shape
    tr = _rows_tile(rows, cols)

    def fn(wv, gv, mv, vv):
        m1 = ADAM_B1 * mv + (1.0 - ADAM_B1) * gv
        v1 = ADAM_B2 * vv + (1.0 - ADAM_B2) * (gv * gv)
        m_hat = m1 / (1.0 - ADAM_B1 ** ADAM_STEP)
        v_hat = v1 / (1.0 - ADAM_B2 ** ADAM_STEP)
        return -ADAM_LR * (m_hat / (jnp.sqrt(v_hat) + ADAM_EPS) + ADAM_WD * wv), m1, v1

    blk = (tr, cols)
    row = lambda i: (i, 0)
    return _call(fn, name, (rows // tr,), [(a, blk, row) for a in (w, g, m, v)], [((rows, cols), blk, row, False)] * 3)


def _pair_add_placed(name, g, recv, cm_idx, out_dtype):
    s, a, b = g.shape
    half = a // 2

    def body(cm_ref, a_ref, b_ref, o_ref, own_ref):
        val = (a_ref[...] + b_ref[...]).astype(out_dtype)
        o_ref[...] = val

        @pl.when(pl.program_id(0) == cm_ref[1])
        def _():
            own_ref[...] = val

    blk = (None, half, b)
    shape = jax.ShapeDtypeStruct((s, half, b), out_dtype)
    return pl.pallas_call(
        body, name=name,
        grid_spec=pltpu.PrefetchScalarGridSpec(
            num_scalar_prefetch=1, grid=(s,),
            in_specs=[pl.BlockSpec(blk, lambda j, cm: (j, cm[0], 0)), pl.BlockSpec(blk, lambda j, cm: (j, 0, 0))],
            out_specs=[pl.BlockSpec(blk, lambda j, cm: (j, 0, 0)), pl.BlockSpec(blk, lambda j, cm: (cm[1], 0, 0))]),
        out_shape=[shape, shape], compiler_params=_cparams(None),
    )(cm_idx, g, recv)


def _sum_chips(name, parts, c_idx, layer, n_layers, into=None):
    _, a, b = parts.shape
    tr = _rows_tile(a, b)

    def body(c_ref, p0, p1, p2, p3, *rest):
        o_ref = rest[-1]
        up = lambda p: p[...].astype(F32)
        o_ref[...] = ((up(p0) + up(p1)) + up(p2)) + up(p3)

    in_specs = [pl.BlockSpec((None, tr, b), lambda i, c, k=k: (k, i, 0)) for k in range(N_CHIPS)]
    args = [c_idx] + [parts] * N_CHIPS
    aliases = {}
    if into is not None:
        in_specs.append(_ANY)
        args.append(into)
        aliases = {1 + N_CHIPS: 0}
    return pl.pallas_call(
        body, name=name,
        grid_spec=pltpu.PrefetchScalarGridSpec(
            num_scalar_prefetch=1, grid=(a // tr,), in_specs=in_specs,
            out_specs=pl.BlockSpec((None, None, tr, b), lambda i, c: (layer, c[0], i, 0))),
        out_shape=jax.ShapeDtypeStruct((n_layers, 2, a, b), F32), input_output_aliases=aliases,
        compiler_params=_cparams(None),
    )(*args)


def _place_own_block(name, w, layer, me_idx, dtype):
    _, a2, b = w.shape
    a = a2 // 2
    tr = _rows_tile(a, b)
    nb = a // tr

    def body(me_ref, w_ref, o_ref):
        o_ref[...] = w_ref[...].astype(dtype)

    return pl.pallas_call(
        body, name=name,
        grid_spec=pltpu.PrefetchScalarGridSpec(
            num_scalar_prefetch=1, grid=(2, nb),
            in_specs=[pl.BlockSpec((None, tr, b), lambda h, i, me: (layer, h * nb + i, 0))],
            out_specs=pl.BlockSpec((None, None, tr, b), lambda h, i, me: (me[0], h, i, 0))),
        out_shape=jax.ShapeDtypeStruct((N_CHIPS, 2, a, b), dtype), compiler_params=_cparams(None),
    )(me_idx, w)


def _mesh_pos():
    return lax.axis_index("x"), lax.axis_index("y"), lax.axis_index("c")


def _other_chips(x, y):
    return [(1 - x, y), (x, 1 - y), (1 - x, 1 - y)]


_ANY = pl.BlockSpec(memory_space=pl.ANY)


def _gather_weights(name, bufs, from_chips=True):
    n = len(bufs)

    def body(*refs):
        out_refs = refs[n:2 * n]
        send_sems, recv_sems = refs[2 * n:]
        x, y, c = _mesh_pos()
        me = 2 * x + y
        sibling = (x, y, 1 - c)
        chips = _other_chips(x, y)

        def copy(i, k, chip_idx, half, to):
            return pltpu.make_async_remote_copy(src_ref=out_refs[i].at[chip_idx, half], dst_ref=out_refs[i].at[chip_idx, half],
                                                send_sem=send_sems.at[6 * i + k], recv_sem=recv_sems.at[6 * i + k],
                                                device_id=to, device_id_type=MESH)

        first = [copy(i, j, me, c, (*chip, c)) for i in range(n) for j, chip in enumerate(chips)] if from_chips else []
        for cp in first:
            cp.start()
        passed = []
        for i in range(n):
            for j, (cx, cy) in enumerate(chips):
                idx = 2 * cx + cy
                if from_chips:
                    copy(i, j, idx, c, sibling).wait_recv()
                fwd = copy(i, 3 + j, idx, c, sibling)
                fwd.start()
                passed.append(fwd)
        for i in range(n):
            for j, (cx, cy) in enumerate(chips):
                copy(i, 3 + j, 2 * cx + cy, 1 - c, sibling).wait_recv()
        for cp in first + passed:
            cp.wait_send()

    return pl.pallas_call(
        body, name=name, in_specs=[_ANY] * n, out_specs=[_ANY] * n,
        out_shape=[jax.ShapeDtypeStruct(b.shape, b.dtype) for b in bufs],
        input_output_aliases={i: i for i in range(n)},
        scratch_shapes=[pltpu.SemaphoreType.DMA((6 * n,)), pltpu.SemaphoreType.DMA((6 * n,))],
        compiler_params=pltpu.CompilerParams(has_side_effects=True),
    )(*bufs)


def _gather_start(name, groups):
    bufs = [b for g in groups for b in g]
    n = len(bufs)
    ng = len(groups)

    def body(*refs):
        b_refs = refs[:n]
        sems = refs[n:n + 2 * ng]
        token = refs[-1]
        x, y, c = _mesh_pos()
        me = 2 * x + y
        i = 0
        for gi, g in enumerate(groups):
            for k in range(len(g)):
                for j, (cx, cy) in enumerate(_other_chips(x, y)):
                    pltpu.make_async_remote_copy(src_ref=b_refs[i].at[me, c], dst_ref=b_refs[i].at[me, c],
                                                 send_sem=sems[2 * gi].at[3 * k + j], recv_sem=sems[2 * gi + 1].at[3 * k + j],
                                                 device_id=(cx, cy, c), device_id_type=MESH).start()
                i += 1
        token[...] = jnp.zeros(token.shape, F32)

    sem_shapes = [pltpu.SemaphoreType.DMA((3 * len(g),)) for g in groups for _ in range(2)]
    res = pl.pallas_call(
        body, name=name,
        out_shape=(*sem_shapes, *[pltpu.HBM(b.shape, b.dtype) for b in bufs], jax.ShapeDtypeStruct((8, LANES), F32)),
        in_specs=[_HBM] * n,
        out_specs=(*[_SEM] * (2 * ng), *[_HBM] * n, pl.BlockSpec(memory_space=pltpu.VMEM)),
        input_output_aliases={i: 2 * ng + i for i in range(n)},
        compiler_params=pltpu.CompilerParams(has_side_effects=_DATAFLOW),
    )(*[pltpu.with_memory_space_constraint(b, pltpu.HBM) for b in bufs])
    out, i = [], 2 * ng
    for gi, g in enumerate(groups):
        out.append((res[2 * gi], res[2 * gi + 1], list(res[i:i + len(g)])))
        i += len(g)
    return out, res[-1]


def _gather_wait(name, send_sems, recv_sems, bufs, after):
    n = len(bufs)

    def body(*refs):
        b_refs = refs[:n]
        s_sems, r_sems = refs[n], refs[n + 1]
        x, y, c = _mesh_pos()
        me = 2 * x + y
        for k in range(n):
            for j, (cx, cy) in enumerate(_other_chips(x, y)):
                idx = 2 * cx + cy
                copy = pltpu.make_async_remote_copy(src_ref=b_refs[k].at[me, c], dst_ref=b_refs[k].at[idx, c],
                                                    send_sem=s_sems.at[3 * k + j], recv_sem=r_sems.at[3 * k + j],
                                                    device_id=(cx, cy, c), device_id_type=MESH)
                copy.wait_send()
                copy.wait_recv()

    res = pl.pallas_call(
        body, name=name,
        out_shape=tuple(pltpu.HBM(b.shape, b.dtype) for b in bufs),
        in_specs=[_HBM] * n + [_SEM, _SEM, _ANY],
        out_specs=tuple([_HBM] * n),
        input_output_aliases={i: i for i in range(n)},
        compiler_params=pltpu.CompilerParams(has_side_effects=_DATAFLOW),
    )(*bufs, send_sems, recv_sems, after)
    return list(res)


def _halves_to_sibling(name, units):
    n = len(units)

    def body(*refs):
        g_refs, out_refs = refs[:n], refs[n:2 * n]
        send_sems, recv_sems = refs[2 * n:]
        x, y, c = _mesh_pos()
        cps = []
        for i in range(n):
            half = units[i].shape[1] // 2
            src = g_refs[i].at[pl.ds(0, N_CHIPS), pl.ds((1 - c) * half, half)]
            cp = pltpu.make_async_remote_copy(src_ref=src, dst_ref=out_refs[i], send_sem=send_sems.at[i],
                                              recv_sem=recv_sems.at[i], device_id=(x, y, 1 - c), device_id_type=MESH)
            cp.start()
            cps.append(cp)
        for cp in cps:
            cp.wait()

    return pl.pallas_call(
        body, name=name, in_specs=[_ANY] * n, out_specs=[_ANY] * n,
        out_shape=[jax.ShapeDtypeStruct((u.shape[0], u.shape[1] // 2, u.shape[2]), u.dtype) for u in units],
        scratch_shapes=[pltpu.SemaphoreType.DMA((n,)), pltpu.SemaphoreType.DMA((n,))],
        compiler_params=pltpu.CompilerParams(has_side_effects=True),
    )(*units)


_HBM = pl.BlockSpec(memory_space=pltpu.HBM)
_SEM = pl.BlockSpec(memory_space=pltpu.SEMAPHORE)
_DATAFLOW = pltpu.SideEffectType.DATAFLOW_SIDE_EFFECTING


def _halves_start(name, units):
    n = len(units)

    def body(*refs):
        g_refs, z_refs = refs[:n], refs[n:2 * n]
        send_sems, recv_sems = refs[2 * n], refs[2 * n + 1]
        token = refs[-1]
        x, y, c = _mesh_pos()
        for i in range(n):
            half = units[i].shape[1] // 2
            src = g_refs[i].at[pl.ds(0, N_CHIPS), pl.ds((1 - c) * half, half)]
            pltpu.make_async_remote_copy(src_ref=src, dst_ref=z_refs[i], send_sem=send_sems.at[i], recv_sem=recv_sems.at[i],
                                         device_id=(x, y, 1 - c), device_id_type=MESH).start()
        token[...] = jnp.zeros(token.shape, F32)

    zones = [lax.empty((u.shape[0], u.shape[1] // 2, u.shape[2]), u.dtype) for u in units]
    hbm = lambda a: pltpu.HBM(a.shape, a.dtype)
    res = pl.pallas_call(
        body, name=name,
        out_shape=(pltpu.SemaphoreType.DMA((n,)), pltpu.SemaphoreType.DMA((n,)),
                   *[hbm(a) for a in units], *[hbm(a) for a in zones], jax.ShapeDtypeStruct((8, LANES), F32)),
        in_specs=[_HBM] * (2 * n),
        out_specs=(_SEM, _SEM, *[_HBM] * (2 * n), pl.BlockSpec(memory_space=pltpu.VMEM)),
        input_output_aliases={i: 2 + i for i in range(2 * n)},
        compiler_params=pltpu.CompilerParams(has_side_effects=_DATAFLOW),
    )(*[pltpu.with_memory_space_constraint(a, pltpu.HBM) for a in list(units) + zones])
    return res[0], res[1], res[2:2 + n], res[2 + n:2 + 2 * n], res[-1]


def _halves_wait(name, send_sems, recv_sems, units, zones, after):
    n = len(units)

    def body(*refs):
        g_refs, z_refs = refs[:n], refs[n:2 * n]
        s_sems, r_sems = refs[2 * n], refs[2 * n + 1]
        x, y, c = _mesh_pos()
        for i in range(n):
            half = units[i].shape[1] // 2
            src = g_refs[i].at[pl.ds(0, N_CHIPS), pl.ds((1 - c) * half, half)]
            copy = pltpu.make_async_remote_copy(src_ref=src, dst_ref=z_refs[i], send_sem=s_sems.at[i], recv_sem=r_sems.at[i],
                                                device_id=(x, y, 1 - c), device_id_type=MESH)
            copy.wait_send()
            copy.wait_recv()

    hbm = lambda a: pltpu.HBM(a.shape, a.dtype)
    res = pl.pallas_call(
        body, name=name,
        out_shape=(*[hbm(a) for a in units], *[hbm(a) for a in zones]),
        in_specs=[_HBM] * (2 * n) + [_SEM, _SEM, _ANY],
        out_specs=tuple([_HBM] * (2 * n)),
        input_output_aliases={i: i for i in range(2 * n)},
        compiler_params=pltpu.CompilerParams(has_side_effects=_DATAFLOW),
    )(*units, *zones, send_sems, recv_sems, after)
    return res[:n], res[n:]


def _scatter_start(name, sums, zones):
    n = len(sums)

    def body(*refs):
        h_refs, z_refs = refs[:n], refs[n:2 * n]
        send_sems, recv_sems = refs[2 * n], refs[2 * n + 1]
        token = refs[-1]
        x, y, c = _mesh_pos()
        me = 2 * x + y
        for i in range(n):
            for j, (cx, cy) in enumerate(_other_chips(x, y)):
                pltpu.make_async_remote_copy(src_ref=h_refs[i].at[2 * cx + cy], dst_ref=z_refs[i].at[me],
                                             send_sem=send_sems.at[3 * i + j], recv_sem=recv_sems.at[3 * i + j],
                                             device_id=(cx, cy, c), device_id_type=MESH).start()
        token[...] = jnp.zeros(token.shape, F32)

    hbm = lambda a: pltpu.HBM(a.shape, a.dtype)
    res = pl.pallas_call(
        body, name=name,
        out_shape=(pltpu.SemaphoreType.DMA((3 * n,)), pltpu.SemaphoreType.DMA((3 * n,)),
                   *[hbm(a) for a in sums], *[hbm(a) for a in zones], jax.ShapeDtypeStruct((8, LANES), F32)),
        in_specs=[_HBM] * (2 * n),
        out_specs=(_SEM, _SEM, *[_HBM] * (2 * n), pl.BlockSpec(memory_space=pltpu.VMEM)),
        input_output_aliases={i: 2 + i for i in range(2 * n)},
        compiler_params=pltpu.CompilerParams(has_side_effects=_DATAFLOW),
    )(*[pltpu.with_memory_space_constraint(a, pltpu.HBM) for a in list(sums) + list(zones)])
    return res[0], res[1], res[2:2 + n], res[2 + n:2 + 2 * n], res[-1]


def _scatter_wait(name, send_sems, recv_sems, sums, zones, after):
    n = len(sums)

    def body(*refs):
        h_refs, z_refs = refs[:n], refs[n:2 * n]
        s_sems, r_sems = refs[2 * n], refs[2 * n + 1]
        x, y, c = _mesh_pos()
        me = 2 * x + y
        for i in range(n):
            for j, (cx, cy) in enumerate(_other_chips(x, y)):
                idx = 2 * cx + cy
                copy = pltpu.make_async_remote_copy(src_ref=h_refs[i].at[idx], dst_ref=z_refs[i].at[idx],
                                                    send_sem=s_sems.at[3 * i + j], recv_sem=r_sems.at[3 * i + j],
                                                    device_id=(cx, cy, c), device_id_type=MESH)
                copy.wait_send()
                copy.wait_recv()

    hbm = lambda a: pltpu.HBM(a.shape, a.dtype)
    res = pl.pallas_call(
        body, name=name,
        out_shape=(*[hbm(a) for a in sums], *[hbm(a) for a in zones]),
        in_specs=[_HBM] * (2 * n) + [_SEM, _SEM, _ANY],
        out_specs=tuple([_HBM] * (2 * n)),
        input_output_aliases={i: i for i in range(2 * n)},
        compiler_params=pltpu.CompilerParams(has_side_effects=_DATAFLOW),
    )(*sums, *zones, send_sems, recv_sems, after)
    return res[n:]


def _join_halves(name, results):
    n = len(results)
    pieces = [(i, l) for i in range(n) for l in range(results[i].shape[0])]

    def body(*refs):
        out_refs = refs[n:2 * n]
        send_sems, recv_sems = refs[2 * n:]
        x, y, c = _mesh_pos()

        def copy(k, half):
            i, l = pieces[k]
            return pltpu.make_async_remote_copy(src_ref=out_refs[i].at[l, half], dst_ref=out_refs[i].at[l, half],
                                                send_sem=send_sems.at[k], recv_sem=recv_sems.at[k],
                                                device_id=(x, y, 1 - c), device_id_type=MESH)

        cps = [copy(k, c) for k in range(len(pieces))]
        for cp in cps:
            cp.start()
        for k in range(len(pieces)):
            copy(k, 1 - c).wait_recv()
        for cp in cps:
            cp.wait_send()

    return pl.pallas_call(
        body, name=name, in_specs=[_ANY] * n, out_specs=[_ANY] * n,
        out_shape=[jax.ShapeDtypeStruct(r.shape, r.dtype) for r in results],
        input_output_aliases={i: i for i in range(n)},
        scratch_shapes=[pltpu.SemaphoreType.DMA((len(pieces),)), pltpu.SemaphoreType.DMA((len(pieces),))],
        compiler_params=pltpu.CompilerParams(has_side_effects=True),
    )(*results)


def _pack(arrays, dtype, rows_multiple):
    flat = jnp.concatenate([a.reshape(-1).astype(dtype) for a in arrays])
    unit = rows_multiple * PACK_W
    total = -(-flat.shape[0] // unit) * unit
    return jnp.pad(flat, (0, total - flat.shape[0])).reshape(total // PACK_W, PACK_W)


def _unpack(flat, shapes):
    out, off = [], 0
    for s in shapes:
        n = 1
        for d in s:
            n *= d
        out.append(flat[..., off:off + n].reshape(flat.shape[:-1] + tuple(s)))
        off += n
    return out


def _ffn_fwd(tag, l, h, g, w_up, conv, bias, w_down, tm):
    hn = _rms_fwd(f"{tag}_norm", h, g, tm)
    u = _mm_cs(f"{tag}_up", hn, w_up, l, tm, out_dtype=FFN_HIDDEN_DTYPE)
    act = _ffn_col_fwd(f"{tag}_glu", u, conv, bias)
    h_out = _mm_full(f"{tag}_down", act, w_down, l, tm, D_FF // 2, add=h)
    return h_out, (hn, u, act)


def _ffn_bwd(tag, l, h, g, w_up, conv, bias, w_down, saved, dh, tm):
    hn, u, act = saved
    da = _mm_nt_full(f"{tag}_down_dx", dh, w_down, l, tm, D_FF // 2)
    dw_down = _mm_tn_full(f"{tag}_down_dw", act, dh, tm, D_FF // 2)
    du, dconv, dbias = _ffn_col_bwd(f"{tag}_glu_bwd", u, da, conv, bias)
    dw_up = _mm_tn_cs(f"{tag}_up_dw", hn, du, N_CHIPS, tm)
    dhn = _mm_nt_cs(f"{tag}_up_dx", du, w_up, l, tm)
    dh, dg = _rms_bwd(f"{tag}_norm_bwd", h, g, dhn, dh, tm)
    return dh, dict(norm=dg, w_up=dw_up, conv=dconv, bias=dbias, w_down=dw_down)


def _to_heads(z, nh, pad):
    t = z.shape[0]
    return jnp.pad(z.reshape(t, nh, HEAD_DIM).transpose(1, 0, 2), ((0, 0), (pad, 0), (0, 0)))


def _from_heads(z, pad):
    nh, tp, _ = z.shape
    return z[:, pad:].transpose(1, 0, 2).reshape(tp - pad, nh * HEAD_DIM)


def _rope_tables(tp, pad):
    half = HEAD_DIM // 2
    inv = ROPE_THETA ** (-jnp.arange(half, dtype=F32) / half)
    ang = (jnp.arange(tp, dtype=F32) - pad)[:, None] * inv[None, :]
    cos, sin = jnp.cos(ang), jnp.sin(ang)
    rot = jnp.zeros((HEAD_DIM, HEAD_DIM), F32)
    idx = jnp.arange(half)
    rot = rot.at[idx + half, idx].set(-1.0).at[idx, idx + half].set(1.0)
    return jnp.concatenate([cos, cos], axis=1), jnp.concatenate([sin, sin], axis=1), rot


def _local_step(x, tgt, w, on_grads=None, fetch=None):
    emit = on_grads if on_grads is not None else (lambda tag, units: 0.0)
    need = (lambda tag, after: w) if fetch is None else (lambda tag, after: {**w, **fetch(tag, after)})
    seq = x.shape[0]
    t = seq + N_META
    tm = _row_tile(t, ROW_TILE_CAP)
    tr = _row_tile(t, ROW_TILE_CAP // 2)
    pad = BLOCK - N_META
    grads = {}

    h0 = jnp.concatenate([w["meta_tokens"], x], axis=0)
    tgt_p = jnp.pad(tgt, ((N_META, 0), (0, 0)))

    hn0 = _rms_fwd("l0_norm", h0, w["norm_mix"][0:1], tm)
    p0 = _mm_cs("l0_in", hn0, w["ev_w_in"], 0, tm)
    uc, yb = _even_col_fwd("l0_convs", p0, w["ev_conv_a"], w["ev_conv_b"])
    ya = _even_ln_fwd("l0_ln", uc, w["ev_ln_a_g"], w["ev_ln_a_b"], tm)
    y0 = jnp.concatenate([ya, yb], axis=1)
    w = need("ev_out", y0)
    h1 = _mm_full("l0_out", y0, w["ev_w_out"], 0, tm, D_MODEL, add=h0)
    w = need("f0", h1)
    f0 = (0, h1, w["norm_ffn"][0:1], w["ff_w_up0"], w["ff_conv"][0], w["ff_conv_b"][0:1], w["ff_w_down0"])
    h2, ffn0 = _ffn_fwd("f0", *f0, tm)
    w = need("od", h2)

    hn2 = _rms_fwd("l1_norm", h2, w["norm_mix"][1:2], tm)
    p1 = _mm_cs("l1_in", hn2, w["od_w_in"], 0, tm)
    cos, sin, rot = _rope_tables(t + pad, pad)
    qh = _to_heads(p1[:, :D_ATT], N_Q_HEADS, pad)
    kh = _to_heads(p1[:, D_ATT:D_ATT + D_KV], N_KV_HEADS, pad)
    vh = _to_heads(p1[:, D_ATT + D_KV:D_ATT + 2 * D_KV], N_KV_HEADS, pad)
    sinks_b = jnp.broadcast_to(w["od_sinks"].reshape(N_Q_HEADS, 1, 1), (N_Q_HEADS, 8, LANES))
    y_att = _from_heads(_attn_fwd("l1_attn", qh, kh, vh, sinks_b, cos, sin, rot), pad)

    col0 = D_ATT + 2 * D_KV
    ch = jnp.arange(D_R) // HEAD_DIM
    seg = (ch[:, None] == ch[None, :]).astype(F32)
    prm = dict(w0=w["od_w0"], a0=w["od_a0"], g2=w["od_g2"], k_k=w["od_k_k"], k_a=w["od_k_a"],
               lnx_g=w["od_lnx_g"], lnx_b=w["od_lnx_b"], r_k=w["od_r_k"].reshape(1, D_R),
               w2p=jnp.concatenate([w["od_w2"], jnp.zeros((LORA_A, D_R), F32)], axis=0),
               a2p=jnp.concatenate([jnp.zeros((LORA_W, D_R), F32), w["od_a2"]], axis=0))
    prs = _shift_fwd("l1_shift", p1, col0, w["od_mu"])
    lw, k2, a_, b_, gate_r = _rwkv_pre_fwd("l1_rwkv_pre", prs, prm, seg, tr)
    v_off = 2 * D_R // (WKV_PAIRS_PER_STEP * PAIR)
    scan_in = [(prs, 0), (lw, 0), (k2, 0), (prs, v_off), (a_, 0), (b_, 0)]
    y_scan, states = _wkv_fwd("l1_wkv", scan_in)
    y_rwkv = _rwkv_post_fwd("l1_rwkv_post", y_scan, prs, k2, gate_r, prm, seg, tr)
    y1 = jnp.concatenate([y_att, y_rwkv], axis=1).astype(MXU_DTYPE)
    h3 = _mm_full("l1_out", y1, w["od_w_out"], 0, tm, D_MODEL, add=h2)
    w = need("f1", h3)
    f1 = (0, h3, w["norm_ffn"][1:2], w["ff_w_up1"], w["ff_conv"][1], w["ff_conv_b"][1:2], w["ff_w_down1"])
    h4, ffn1 = _ffn_fwd("f1", *f1, tm)

    loss_blk, dh, d_norm_final = _final_loss("final", h4, w["norm_final"], tgt_p, tm)
    grads["norm_final"] = d_norm_final

    dh, gf1 = _ffn_bwd("f1", *f1, ffn1, dh, tm)
    zero = emit("f1", {"ff_w_up1": gf1["w_up"], "ff_w_down1": gf1["w_down"].reshape(N_CHIPS, D_FF // N_CHIPS, D_MODEL)})
    prm = dict(prm, lnx_g=prm["lnx_g"] + zero)
    dy1 = _mm_nt_full("l1_out_dx", dh, w["od_w_out"], 0, tm, D_MODEL)
    grads["od_w_out"] = _mm_tn_full("l1_out_dw", y1, dh, tm, D_MODEL // 2)
    dy_scan, dr_p, dk2_p, dv_p, dgate_r, grads["od_lnx_g"], grads["od_lnx_b"], d_rk = _rwkv_post_bwd(
        "l1_rwkv_post_bwd", y_scan, prs, k2, gate_r, prm, seg, dy1, 1, tr)
    grads["od_r_k"] = d_rk.reshape(N_R_HEADS, HEAD_DIM)
    dr_s, dlw, dk2_s, dv_s, da_, db_ = _wkv_bwd("l1_wkv_bwd", scan_in, states, (dy_scan, 0))
    dk, dxl, dgd, grads["od_w0"], dw2p, grads["od_a0"], da2p, grads["od_g2"], grads["od_k_k"], grads["od_k_a"] = (
        _rwkv_pre_bwd("l1_rwkv_pre_bwd", prs, prm, seg, (dlw, dk2_s + dk2_p, da_, db_, dgate_r), tr))
    grads["od_w2"] = dw2p[:LORA_W]
    grads["od_a2"] = da2p[LORA_W:]
    dprs = jnp.concatenate([dr_s + dr_p, dk, dv_s + dv_p, dxl, dgd], axis=1)
    dpr, grads["od_mu"] = _shift_bwd("l1_shift_bwd", p1, col0, w["od_mu"], dprs)
    doh = _to_heads(dy1[:, :D_ATT], N_Q_HEADS, pad)
    dqh, dkp, dkc, dvp, dvc, dkm, dvm, dsinks = _attn_bwd("l1_attn_bwd", qh, kh, vh, sinks_b, cos, sin, rot, doh)
    grads["od_sinks"] = dsinks[:, 0, 0].reshape(1, N_Q_HEADS)
    dkh = _kv_combine("l1_attn_dk", dkp, dkc, dkm)
    dvh = _kv_combine("l1_attn_dv", dvp, dvc, dvm)
    dp1 = jnp.concatenate([_from_heads(dqh, pad), _from_heads(dkh, pad), _from_heads(dvh, pad), dpr], axis=1).astype(MXU_DTYPE)
    grads["od_w_in"] = _mm_tn_cs("l1_in_dw", hn2, dp1, N_CHIPS, tm)
    dhn2 = _mm_nt_cs("l1_in_dx", dp1, w["od_w_in"], 0, tm)
    dh, d_mix1 = _rms_bwd("l1_norm_bwd", h2, w["norm_mix"][1:2], dhn2, dh, tm)

    zero = emit("od", {"od_w_out": grads["od_w_out"].reshape(N_CHIPS, D_MODEL // N_CHIPS, D_MODEL), "od_w_in": grads["od_w_in"]})
    f0 = f0[:5] + (f0[5] + zero,) + f0[6:]
    dh, gf0 = _ffn_bwd("f0", *f0, ffn0, dh, tm)
    zero = emit("f0", {"ff_w_up0": gf0["w_up"], "ff_w_down0": gf0["w_down"].reshape(N_CHIPS, D_FF // N_CHIPS, D_MODEL)})
    w = dict(w, ev_ln_a_g=w["ev_ln_a_g"] + zero)
    dy0 = _mm_nt_full("l0_out_dx", dh, w["ev_w_out"], 0, tm, D_MODEL)
    grads["ev_w_out"] = _mm_tn_full("l0_out_dw", y0, dh, tm, D_MODEL // 2)
    duc, grads["ev_ln_a_g"], grads["ev_ln_a_b"] = _even_ln_bwd("l0_ln_bwd", uc, w["ev_ln_a_g"], w["ev_ln_a_b"], dy0, 0, tm)
    *dparts, grads["ev_conv_a"], grads["ev_conv_b"] = _even_col_bwd("l0_convs_bwd", p0, duc, dy0, w["ev_conv_a"], w["ev_conv_b"])
    dp0 = jnp.concatenate(dparts, axis=1)
    grads["ev_w_in"] = _mm_tn_cs("l0_in_dw", hn0, dp0, N_CHIPS, tm)
    dhn0 = _mm_nt_cs("l0_in_dx", dp0, w["ev_w_in"], 0, tm)
    dh, d_mix0 = _rms_bwd("l0_norm_bwd", h0, w["norm_mix"][0:1], dhn0, dh, tm)

    grads["norm_mix"] = jnp.concatenate([d_mix0, d_mix1], axis=0)
    grads["norm_ffn"] = jnp.concatenate([gf0["norm"], gf1["norm"]], axis=0)
    grads["ff_w_up"] = [gf0["w_up"], gf1["w_up"]]
    grads["ff_conv"] = jnp.stack([gf0["conv"], gf1["conv"]])
    grads["ff_conv_b"] = jnp.concatenate([gf0["bias"], gf1["bias"]], axis=0)
    grads["ff_w_down"] = [gf0["w_down"], gf1["w_down"]]
    grads["meta_tokens"] = dh[:N_META]
    return loss_blk[0, 0], dh[N_META:], grads


SHARD_AXIS = {
    "meta_tokens": 1, "norm_mix": None, "norm_ffn": None, "norm_final": None,
    "ev_w_in": 2, "ev_conv_a": 2, "ev_ln_a_g": None, "ev_ln_a_b": None, "ev_conv_b": 2, "ev_w_out": 1,
    "od_w_in": 2, "od_sinks": None, "od_mu": 1, "od_w0": 1, "od_w2": 2, "od_a0": 1, "od_a2": 2, "od_g2": 2,
    "od_k_k": 1, "od_k_a": 1, "od_r_k": None, "od_lnx_g": 1, "od_lnx_b": 1, "od_w_out": 1,
    "ff_w_up": 2, "ff_conv": 2, "ff_conv_b": None, "ff_w_down": 1,
}
WEIGHTS = list(SHARD_AXIS)
BIG = ("ev_w_in", "ev_w_out", "od_w_in", "od_w_out", "ff_w_up", "ff_w_down")
SHARDED = [n for n in WEIGHTS if SHARD_AXIS[n] is not None]
SMALL = [n for n in SHARDED if n not in BIG]
REPLICATED = [n for n in WEIGHTS if SHARD_AXIS[n] is None]


def _join(g, axis):
    return jnp.concatenate([g[k] for k in range(N_CHIPS)], axis=axis)


def _split(full, axis):
    return jnp.stack(jnp.split(full, N_CHIPS, axis=axis))


def _full_weights(gathered, repl):
    w = {}
    sq = lambda a: a.reshape(a.shape[1:]) if a.shape[0] == 1 else a
    for n in REPLICATED:
        w[n] = repl[n]
    w["norm_final"] = repl["norm_final"].reshape(1, D_MODEL)
    for n in ("ev_ln_a_g", "ev_ln_a_b"):
        w[n] = repl[n].reshape(1, D_A)
    w["od_r_k"] = repl["od_r_k"][0]
    w["meta_tokens"] = _join(gathered["meta_tokens"], 1)
    for n in ("ev_conv_a", "ev_conv_b", "od_w2", "od_a2", "od_g2"):
        w[n] = sq(_join(gathered[n], 2))
    for n in ("od_mu", "od_w0", "od_a0", "od_k_k", "od_k_a", "od_lnx_g", "od_lnx_b"):
        w[n] = _join(gathered[n], 1)
    w["ff_conv"] = _join(gathered["ff_conv"], 2)
    return w


def _shard_grads(grads):
    out = {}
    for n in REPLICATED:
        out[n] = grads[n]
    out["norm_final"] = grads["norm_final"].reshape(D_MODEL)
    out["od_r_k"] = grads["od_r_k"][None]
    out["meta_tokens"] = _split(grads["meta_tokens"], 1)
    for n in ("ev_conv_a", "ev_conv_b", "od_w2", "od_a2", "od_g2"):
        out[n] = _split(grads[n][None], 2)
    for n in ("od_mu", "od_w0", "od_a0", "od_k_k", "od_k_a", "od_lnx_g", "od_lnx_b"):
        out[n] = _split(grads[n], 1)
    out["ff_conv"] = _split(grads["ff_conv"], 2)
    return out


def kernel(x, meta_tokens, norm_mix, norm_ffn, norm_final, ev_w_in, ev_conv_a, ev_ln_a_g, ev_ln_a_b, ev_conv_b, ev_w_out, od_w_in, od_sinks, od_mu, od_w0, od_w2, od_a0, od_a2, od_g2, od_k_k, od_k_a, od_r_k, od_lnx_g, od_lnx_b, od_w_out, ff_w_up, ff_conv, ff_conv_b, ff_w_down, loss_target, m_meta_tokens, m_norm_mix, m_norm_ffn, m_norm_final, m_ev_w_in, m_ev_conv_a, m_ev_ln_a_g, m_ev_ln_a_b, m_ev_conv_b, m_ev_w_out, m_od_w_in, m_od_sinks, m_od_mu, m_od_w0, m_od_w2, m_od_a0, m_od_a2, m_od_g2, m_od_k_k, m_od_k_a, m_od_r_k, m_od_lnx_g, m_od_lnx_b, m_od_w_out, m_ff_w_up, m_ff_conv, m_ff_conv_b, m_ff_w_down, v_meta_tokens, v_norm_mix, v_norm_ffn, v_norm_final, v_ev_w_in, v_ev_conv_a, v_ev_ln_a_g, v_ev_ln_a_b, v_ev_conv_b, v_ev_w_out, v_od_w_in, v_od_sinks, v_od_mu, v_od_w0, v_od_w2, v_od_a0, v_od_a2, v_od_g2, v_od_k_k, v_od_k_a, v_od_r_k, v_od_lnx_g, v_od_lnx_b, v_od_w_out, v_ff_w_up, v_ff_conv, v_ff_conv_b, v_ff_w_down):
    wts = dict(meta_tokens=meta_tokens, norm_mix=norm_mix, norm_ffn=norm_ffn, norm_final=norm_final, ev_w_in=ev_w_in, ev_conv_a=ev_conv_a, ev_ln_a_g=ev_ln_a_g, ev_ln_a_b=ev_ln_a_b, ev_conv_b=ev_conv_b, ev_w_out=ev_w_out, od_w_in=od_w_in, od_sinks=od_sinks, od_mu=od_mu, od_w0=od_w0, od_w2=od_w2, od_a0=od_a0, od_a2=od_a2, od_g2=od_g2, od_k_k=od_k_k, od_k_a=od_k_a, od_r_k=od_r_k, od_lnx_g=od_lnx_g, od_lnx_b=od_lnx_b, od_w_out=od_w_out, ff_w_up=ff_w_up, ff_conv=ff_conv, ff_conv_b=ff_conv_b, ff_w_down=ff_w_down)
    mom = dict(meta_tokens=m_meta_tokens, norm_mix=m_norm_mix, norm_ffn=m_norm_ffn, norm_final=m_norm_final, ev_w_in=m_ev_w_in, ev_conv_a=m_ev_conv_a, ev_ln_a_g=m_ev_ln_a_g, ev_ln_a_b=m_ev_ln_a_b, ev_conv_b=m_ev_conv_b, ev_w_out=m_ev_w_out, od_w_in=m_od_w_in, od_sinks=m_od_sinks, od_mu=m_od_mu, od_w0=m_od_w0, od_w2=m_od_w2, od_a0=m_od_a0, od_a2=m_od_a2, od_g2=m_od_g2, od_k_k=m_od_k_k, od_k_a=m_od_k_a, od_r_k=m_od_r_k, od_lnx_g=m_od_lnx_g, od_lnx_b=m_od_lnx_b, od_w_out=m_od_w_out, ff_w_up=m_ff_w_up, ff_conv=m_ff_conv, ff_conv_b=m_ff_conv_b, ff_w_down=m_ff_w_down)
    var = dict(meta_tokens=v_meta_tokens, norm_mix=v_norm_mix, norm_ffn=v_norm_ffn, norm_final=v_norm_final, ev_w_in=v_ev_w_in, ev_conv_a=v_ev_conv_a, ev_ln_a_g=v_ev_ln_a_g, ev_ln_a_b=v_ev_ln_a_b, ev_conv_b=v_ev_conv_b, ev_w_out=v_ev_w_out, od_w_in=v_od_w_in, od_sinks=v_od_sinks, od_mu=v_od_mu, od_w0=v_od_w0, od_w2=v_od_w2, od_a0=v_od_a0, od_a2=v_od_a2, od_g2=v_od_g2, od_k_k=v_od_k_k, od_k_a=v_od_k_a, od_r_k=v_od_r_k, od_lnx_g=v_od_lnx_g, od_lnx_b=v_od_lnx_b, od_w_out=v_od_w_out, ff_w_up=v_ff_w_up, ff_conv=v_ff_conv, ff_conv_b=v_ff_conv_b, ff_w_down=v_ff_w_down)

    me_idx = (2 * lax.axis_index("x") + lax.axis_index("y")).astype(jnp.int32).reshape(1)
    c_idx = lax.axis_index("c").astype(jnp.int32).reshape(1)
    small_mine = _pack([wts[n] for n in SMALL], F32, 2 * 8)
    sources = {"ev_w_in": (ev_w_in, 0), "small": (small_mine[None], 0), "ev_w_out": (ev_w_out, 0),
               "ff_w_up0": (ff_w_up, 0), "ff_w_down0": (ff_w_down, 0), "od_w_in": (od_w_in, 0), "od_w_out": (od_w_out, 0),
               "ff_w_up1": (ff_w_up, 1), "ff_w_down1": (ff_w_down, 1)}
    bufs = {n: _place_own_block("place_" + n, a, l, me_idx, F32 if n == "small" else MXU_DTYPE)
            for n, (a, l) in sources.items()}

    def as_used(n, g):
        if n in ("ev_w_out", "od_w_out", "ff_w_down0", "ff_w_down1"):
            return g.reshape(1, -1, g.shape[-1])
        return g.reshape(N_CHIPS, 1, -1, g.shape[-1])

    first = dict(zip(("ev_w_in", "small"), _gather_weights("gather_first", [bufs["ev_w_in"], bufs["small"]])))
    gathered = dict(zip(SMALL, _unpack(first["small"].reshape(N_CHIPS, -1), [wts[n].shape for n in SMALL])))
    w_full = _full_weights(gathered, wts)
    w_full["ev_w_in"] = as_used("ev_w_in", first["ev_w_in"])
    groups = {"ev_out": ["ev_w_out"], "f0": ["ff_w_up0", "ff_w_down0"], "od": ["od_w_in", "od_w_out"],
              "f1": ["ff_w_up1", "ff_w_down1"]}
    started_gathers, token = _gather_start("gather_start", [[bufs[n] for n in g] for g in groups.values()])
    started_gathers = dict(zip(groups, started_gathers))
    w_full["norm_mix"] = w_full["norm_mix"] + token[0, 0]

    def fetch(tag, after):
        send_sems, recv_sems, group_bufs = started_gathers[tag]
        landed = _gather_wait("gather_wait_" + tag, send_sems, recv_sems, group_bufs, after)
        whole = _gather_weights("gather_siblings_" + tag, landed, from_chips=False)
        return {n: as_used(n, g) for n, g in zip(groups[tag], whole)}

    cm_idx = jnp.concatenate([c_idx, me_idx])
    started = []
    to_sibling = []

    def to_chips(tag, names, units, from_sibling):
        pairs = [_pair_add_placed(f"grads_pair_add_{n}", u, r, cm_idx, GRAD_WIRE_DTYPE)
                 for n, u, r in zip(names, units, from_sibling)]
        send_sems, recv_sems, sums, zones, token = _scatter_start(
            f"grads_to_chips_start_{tag}", [p[0] for p in pairs], [p[1] for p in pairs])
        started.append((tag, names, send_sems, recv_sems, sums, zones))
        return token[0, 0]

    def start_reduction(tag, units):
        names = list(units)
        arrays = [units[n] for n in names]
        zero = 0.0
        if to_sibling:
            before, bnames, send_sems, recv_sems, thru, zones = to_sibling.pop()
            thru, got = _halves_wait(f"grads_to_sibling_wait_{before}", send_sems, recv_sems, thru, zones, arrays[0])
            zero = zero + to_chips(before, bnames, thru, got)
        if tag == "f0":
            return zero + to_chips(tag, names, arrays, _halves_to_sibling(f"grads_to_sibling_{tag}", arrays))
        send_sems, recv_sems, thru, zones, token = _halves_start(f"grads_to_sibling_start_{tag}", arrays)
        to_sibling.append((tag, names, send_sems, recv_sems, thru, zones))
        return zero + token[0, 0]

    loss_local, grad_x, grads = _local_step(x[0], loss_target[0], w_full, start_reduction, fetch)
    loss = lax.psum(loss_local, ("x", "y", "c"))

    sg = _shard_grads(grads)
    small_rows = [jnp.concatenate([sg[n][k].reshape(-1) for n in SMALL] + [sg[n].reshape(-1) for n in REPLICATED])
                  for k in range(N_CHIPS)]
    n_el = small_rows[0].shape[0]
    n_rows = -(-n_el // (16 * PACK_W)) * 16
    small_unit = jnp.stack([jnp.pad(r, (0, n_rows * PACK_W - n_el)).reshape(n_rows, PACK_W) for r in small_rows])
    last = {"ev_w_out": grads["ev_w_out"].reshape(N_CHIPS, D_MODEL // N_CHIPS, D_MODEL), "ev_w_in": grads["ev_w_in"],
            "small": small_unit}
    from_sibling = _halves_to_sibling("grads_to_sibling_ev", list(last.values()))
    pairs = [_pair_add_placed(f"grads_pair_add_{n}", u, r, cm_idx, F32 if n == "small" else GRAD_WIRE_DTYPE)
             for (n, u), r in zip(last.items(), from_sibling)]
    ev_send, ev_recv, ev_sums, ev_zones, token = _scatter_start(
        "grads_to_chips_start_ev", [p[0] for p in pairs], [p[1] for p in pairs])
    dests = {"ev_w_in": ("ev_w_in", 0), "od_w_in": ("od_w_in", 0), "ev_w_out": ("ev_w_out", 0), "od_w_out": ("od_w_out", 0),
             "ff_w_up0": ("ff_w_up", 0), "ff_w_up1": ("ff_w_up", 1), "ff_w_down0": ("ff_w_down", 0),
             "ff_w_down1": ("ff_w_down", 1), "small": ("small", 0)}
    outs = {"grad": {}, "delta": {}, "new_m": {}, "new_v": {}}

    def finish(tag, from_chips, results):
        reduced = {}
        for n, part in from_chips.items():
            r, l = dests[n]
            reduced[r] = _sum_chips(f"grads_chip_sum_{n}", part, c_idx, l, 2 if r.startswith("ff_w") else 1,
                                    into=reduced.get(r))
        joined = dict(zip(results, _join_halves("grads_join_" + tag, [reduced[r] for r in results])))
        for n, g in joined.items():
            if n == "small":
                continue
            shape = wts[n].shape
            flat = lambda a: a.reshape(-1, shape[-1])
            new = _adamw("adamw_" + n, flat(wts[n]), flat(g), flat(mom[n]), flat(var[n]))
            for kind, arr in zip(("grad", "delta", "new_m", "new_v"), (g,) + tuple(new)):
                outs[kind][n] = arr.reshape(shape)
        return joined

    from_chips = {}
    for tag, names, send_sems, recv_sems, sums, zones in started:
        from_chips.update(zip(names, _scatter_wait(f"grads_to_chips_wait_{tag}", send_sems, recv_sems, sums, zones, token)))
    finish("layers", from_chips, ["od_w_in", "od_w_out", "ff_w_up", "ff_w_down"])
    from_chips = dict(zip(last, _scatter_wait("grads_to_chips_wait_ev", ev_send, ev_recv, ev_sums, ev_zones,
                                              outs["delta"]["ff_w_up"])))
    joined = finish("ev", from_chips, ["ev_w_in", "ev_w_out", "small"])

    order = SMALL + REPLICATED
    packed = lambda d: jnp.pad(jnp.concatenate([d[n].reshape(-1) for n in order]),
                               (0, n_rows * PACK_W - n_el)).reshape(n_rows, PACK_W)
    g_small = joined["small"].reshape(n_rows, PACK_W)
    new = _adamw("adamw_small", packed(wts), g_small, packed(mom), packed(var))
    for tag, arr in zip(("grad", "delta", "new_m", "new_v"), (g_small,) + tuple(new)):
        outs[tag].update(zip(order, _unpack(arr.reshape(-1), [wts[n].shape for n in order])))
    return (loss, grad_x[None], *[outs["grad"][n] for n in WEIGHTS], *[outs["delta"][n] for n in WEIGHTS],
            *[outs["new_m"][n] for n in WEIGHTS], *[outs["new_v"][n] for n in WEIGHTS])
```

```python
import functools

import jax
import jax.numpy as jnp
from jax import lax
from jax.experimental import pallas as pl
from jax.experimental.pallas import tpu as pltpu

F32 = jnp.float32
BF16 = jnp.bfloat16
MXU_DTYPE = BF16
GRAD_WIRE_DTYPE = BF16
FFN_HIDDEN_DTYPE = BF16

D_MODEL = 1024
N_META = 16
RMS_EPS = 1e-6
LN_EPS = 1e-5
D_A = 512
CONV_A_WIDTH = 31
CONV_B_WIDTH = 3
HEAD_DIM = 64
N_Q_HEADS = 8
N_KV_HEADS = 2
GQA_GROUP = 4
D_ATT = 512
D_KV = 128
BLOCK = 128
ROPE_THETA = 10000.0
D_R = 512
N_R_HEADS = 8
LORA_W = 64
LORA_A = 64
LORA_G = 128
RWKV_GN_EPS = 64e-5
RWKV_COLS = 3 * D_R + LORA_W + LORA_A + LORA_G
D_FF = 2816
NEG_INF = -1e30
ADAM_LR = 0.001
ADAM_B1 = 0.9
ADAM_B2 = 0.999
ADAM_EPS = 1e-08
ADAM_WD = 0.01
ADAM_STEP = 10

N_CHIPS = 4
LANES = 128
CONV_PAD = 32
ROW_TILE_CAP = 704
VMEM_LIMIT_V7X = 56 * 1024 * 1024
MESH = pl.DeviceIdType.MESH


def _cparams(sem=None):
    return pltpu.CompilerParams(dimension_semantics=sem, vmem_limit_bytes=VMEM_LIMIT_V7X)


def _row_tile(t, cap):
    for d in range(min(t, cap), 0, -1):
        if t % d == 0 and d % 16 == 0:
            return d
    return t


def _chunk_len(t):
    for d in (64, 48, 32, 16, 8):
        if t % d == 0:
            return d
    raise ValueError(t)


def _call(fn, name, grid, ins, outs, acc_axis=None, sem=None):
    n_in, n_out = len(ins), len(outs)
    dtype = lambda o: o[4] if len(o) > 4 else F32

    def body(*refs):
        vals = fn(*[r[...] for r in refs[:n_in]])
        if not isinstance(vals, (tuple, list)):
            vals = (vals,)
        for r, v, o in zip(refs[n_in:n_in + n_out], vals, outs):
            if o[3]:
                first = pl.program_id(acc_axis) == 0

                @pl.when(first)
                def _(r=r, v=v):
                    r[...] = v

                @pl.when(jnp.logical_not(first))
                def _(r=r, v=v):
                    r[...] += v
            else:
                r[...] = v.astype(dtype(o))

    res = pl.pallas_call(
        body, name=name, grid=grid,
        in_specs=[pl.BlockSpec(b, m) for _, b, m in ins],
        out_specs=[pl.BlockSpec(o[1], o[2]) for o in outs],
        out_shape=[jax.ShapeDtypeStruct(o[0], dtype(o)) for o in outs],
        compiler_params=_cparams(sem),
    )(*[a for a, _, _ in ins])
    return res if n_out > 1 else res[0]


def _matmul(name, a, b, *, dims, grid, a_spec, b_spec, o_shape, o_spec, acc_shape, nk, k_axis,
            add=None, add_spec=None, out_dtype=F32):
    def product(a_ref, b_ref):
        return lax.dot_general(a_ref[...].astype(MXU_DTYPE), b_ref[...].astype(MXU_DTYPE), dims, preferred_element_type=F32)

    def body_single(*refs):
        a_ref, b_ref, o_ref = refs[0], refs[1], refs[-1]
        res = product(a_ref, b_ref) if add is None else product(a_ref, b_ref) + refs[2][...]
        o_ref[...] = res.astype(out_dtype)

    def body_steps(*refs):
        a_ref, b_ref, o_ref, acc = refs[0], refs[1], refs[-2], refs[-1]
        k = pl.program_id(k_axis)

        @pl.when(k == 0)
        def _():
            if add is None:
                acc[...] = jnp.zeros(acc.shape, F32)
            else:
                acc[...] = refs[2][...]

        acc[...] += product(a_ref, b_ref)

        @pl.when(k == nk - 1)
        def _():
            o_ref[...] = acc[...].astype(out_dtype)

    args = [a, b] + ([] if add is None else [add])
    specs = [a_spec, b_spec] + ([] if add is None else [add_spec])
    return pl.pallas_call(
        body_single if nk == 1 else body_steps, name=name, grid=grid, in_specs=specs, out_specs=o_spec,
        out_shape=jax.ShapeDtypeStruct(o_shape, out_dtype),
        scratch_shapes=[] if nk == 1 else [pltpu.VMEM(acc_shape, F32)],
        compiler_params=_cparams(None),
    )(*args)


MATMUL_BLOCKS_BYTES = 46 * 1024 * 1024


def _whole_if_fits(t, tile, need_bytes):
    return t if need_bytes <= MATMUL_BLOCKS_BYTES else tile


_NN = (((1,), (0,)), ((), ()))
_NT = (((1,), (1,)), ((), ()))
_TN = (((0,), (0,)), ((), ()))


def _mm_cs(name, x, wg, l, tm, out_dtype=F32):
    t, k = x.shape
    s, _, _, n = wg.shape
    tm = _whole_if_fits(t, tm, 2 * (t * k * x.dtype.itemsize + k * n * wg.dtype.itemsize + t * n * 4))
    return _matmul(name, x, wg, dims=_NN, grid=(s, t // tm, 1),
                   a_spec=pl.BlockSpec((tm, k), lambda j, i, kk: (i, 0)),
                   b_spec=pl.BlockSpec((None, None, k, n), lambda j, i, kk: (j, l, 0, 0)),
                   o_shape=(t, s * n), o_spec=pl.BlockSpec((tm, n), lambda j, i, kk: (i, j)),
                   acc_shape=(tm, n), nk=1, k_axis=2, out_dtype=out_dtype)


def _mm_full(name, x, w, l, tm, tk, add=None):
    t, k = x.shape
    n = w.shape[2]
    nk = k // tk
    tm = _whole_if_fits(t, tm, 2 * (t * tk * x.dtype.itemsize + tk * n * w.dtype.itemsize + t * n * 4 * (1 if add is None else 2))
                        + (t * n * 4 if nk > 1 else 0))
    return _matmul(name, x, w, dims=_NN, grid=(t // tm, 1, nk),
                   a_spec=pl.BlockSpec((tm, tk), lambda i, j, kk: (i, kk)),
                   b_spec=pl.BlockSpec((None, tk, n), lambda i, j, kk: (l, kk, 0)),
                   o_shape=(t, n), o_spec=pl.BlockSpec((tm, n), lambda i, j, kk: (i, 0)),
                   acc_shape=(tm, n), nk=nk, k_axis=2,
                   add=add, add_spec=pl.BlockSpec((tm, n), lambda i, j, kk: (i, 0)))


def _mm_nt_cs(name, dy, wg, l, tm, add=None):
    t = dy.shape[0]
    s, _, k, n = wg.shape
    tm = _whole_if_fits(t, tm, 2 * (t * n * dy.dtype.itemsize + k * n * wg.dtype.itemsize + t * k * 4 * (1 if add is None else 2))
                        + t * k * 4)
    return _matmul(name, dy, wg, dims=_NT, grid=(t // tm, 1, s),
                   a_spec=pl.BlockSpec((tm, n), lambda i, j, kk: (i, kk)),
                   b_spec=pl.BlockSpec((None, None, k, n), lambda i, j, kk: (kk, l, 0, 0)),
                   o_shape=(t, k), o_spec=pl.BlockSpec((tm, k), lambda i, j, kk: (i, 0)),
                   acc_shape=(tm, k), nk=s, k_axis=2,
                   add=add, add_spec=pl.BlockSpec((tm, k), lambda i, j, kk: (i, 0)))


def _mm_nt_full(name, dy, w, l, tm, tko):
    t, n = dy.shape
    k = w.shape[1]
    tm = _whole_if_fits(t, tm, 2 * (t * n * dy.dtype.itemsize + tko * n * w.dtype.itemsize + t * tko * 4))
    return _matmul(name, dy, w, dims=_NT, grid=(t // tm, k // tko, 1),
                   a_spec=pl.BlockSpec((tm, n), lambda i, j, kk: (i, 0)),
                   b_spec=pl.BlockSpec((None, tko, n), lambda i, j, kk: (l, j, 0)),
                   o_shape=(t, k), o_spec=pl.BlockSpec((tm, tko), lambda i, j, kk: (i, j)),
                   acc_shape=(tm, tko), nk=1, k_axis=2)


def _mm_tn_cs(name, x, dy, s, tk):
    t, k = x.shape
    n = dy.shape[1] // s
    tk = _whole_if_fits(t, tk, 2 * (t * k * x.dtype.itemsize + t * n * dy.dtype.itemsize + k * n * 4))
    nk = t // tk
    return _matmul(name, x, dy, dims=_TN, grid=(s, 1, nk),
                   a_spec=pl.BlockSpec((tk, k), lambda j, i, kk: (kk, 0)),
                   b_spec=pl.BlockSpec((tk, n), lambda j, i, kk: (kk, j)),
                   o_shape=(s, k, n), o_spec=pl.BlockSpec((None, k, n), lambda j, i, kk: (j, 0, 0)),
                   acc_shape=(k, n), nk=nk, k_axis=2)


def _mm_tn_full(name, y, dh, tk, tko):
    t, k = y.shape
    n = dh.shape[1]
    tk = _whole_if_fits(t, tk, 2 * (t * tko * y.dtype.itemsize + t * n * dh.dtype.itemsize + tko * n * 4))
    nk = t // tk
    return _matmul(name, y, dh, dims=_TN, grid=(k // tko, 1, nk),
                   a_spec=pl.BlockSpec((tk, tko), lambda j, i, kk: (kk, j)),
                   b_spec=pl.BlockSpec((tk, n), lambda j, i, kk: (kk, 0)),
                   o_shape=(k, n), o_spec=pl.BlockSpec((tko, n), lambda j, i, kk: (j, 0)),
                   acc_shape=(tko, n), nk=nk, k_axis=2)


def _sigmoid(x):
    return 1.0 / (1.0 + jnp.exp(-x))


def _rms_fwd(name, h, g, tr):
    t, d = h.shape

    def fn(hv, gv):
        r = lax.rsqrt(jnp.mean(hv * hv, axis=-1, keepdims=True) + RMS_EPS)
        return hv * r * gv

    return _call(fn, name, (t // tr,), [(h, (tr, d), lambda i: (i, 0)), (g, (1, d), lambda i: (0, 0))],
                 [((t, d), (tr, d), lambda i: (i, 0), False, MXU_DTYPE)])


def _rms_bwd(name, h, g, dhn, dh, tr):
    t, d = h.shape

    def fn(hv, gv, dy, dh_in):
        r = lax.rsqrt(jnp.mean(hv * hv, axis=-1, keepdims=True) + RMS_EPS)
        xh = hv * r
        dg = jnp.sum(dy * xh, axis=0, keepdims=True)
        dxh = dy * gv
        dx = r * (dxh - xh * jnp.mean(dxh * xh, axis=-1, keepdims=True))
        return dh_in + dx, dg

    row = lambda i: (i, 0)
    return _call(fn, name, (t // tr,),
                 [(h, (tr, d), row), (g, (1, d), lambda i: (0, 0)), (dhn, (tr, d), row), (dh, (tr, d), row)],
                 [((t, d), (tr, d), row, False), ((1, d), (1, d), lambda i: (0, 0), True)], acc_axis=0)


def _final_loss(name, h, g, tgt, tr):
    t, d = h.shape

    def fn(hv, gv, tv):
        r = lax.rsqrt(jnp.mean(hv * hv, axis=-1, keepdims=True) + RMS_EPS)
        xh = hv * r
        row = pl.program_id(0) * tr + lax.broadcasted_iota(jnp.int32, (tr, 1), 0)
        e = jnp.where(row >= N_META, xh * gv - tv, 0.0)
        loss = jnp.broadcast_to(0.5 * jnp.sum(jnp.sum(e * e, axis=-1, keepdims=True), axis=0, keepdims=True) / d,
                                (8, LANES))
        dout = e / d
        dg = jnp.sum(dout * xh, axis=0, keepdims=True)
        dxh = dout * gv
        dx = r * (dxh - xh * jnp.mean(dxh * xh, axis=-1, keepdims=True))
        return loss, dx, dg

    row = lambda i: (i, 0)
    fix = lambda i: (0, 0)
    return _call(fn, name, (t // tr,), [(h, (tr, d), row), (g, (1, d), fix), (tgt, (tr, d), row)],
                 [((8, LANES), (8, LANES), fix, True), ((t, d), (tr, d), row, False), ((1, d), (1, d), fix, True)],
                 acc_axis=0)


def _silu_ln(uc, g, b):
    mu = jnp.mean(uc, axis=-1, keepdims=True)
    xc = uc - mu
    rs = lax.rsqrt(jnp.mean(xc * xc, axis=-1, keepdims=True) + LN_EPS)
    ln = xc * rs * g + b
    return ln * _sigmoid(ln)


def _even_ln_fwd(name, uc, g, b, tr):
    t, d = uc.shape
    row, fix = (lambda i: (i, 0)), (lambda i: (0, 0))
    return _call(_silu_ln, name, (t // tr,), [(uc, (tr, d), row), (g, (1, d), fix), (b, (1, d), fix)],
                 [((t, d), (tr, d), row, False, MXU_DTYPE)])


def _even_ln_bwd(name, uc, g, b, dy, dy_col, tr):
    t, d = uc.shape

    def fn(ucv, gv, bv, dyv):
        mu = jnp.mean(ucv, axis=-1, keepdims=True)
        xc = ucv - mu
        rs = lax.rsqrt(jnp.mean(xc * xc, axis=-1, keepdims=True) + LN_EPS)
        xh = xc * rs
        ln = xh * gv + bv
        s = _sigmoid(ln)
        dln = dyv * (s * (1.0 + ln * (1.0 - s)))
        dg = jnp.sum(dln * xh, axis=0, keepdims=True)
        db = jnp.sum(dln, axis=0, keepdims=True)
        dxh = dln * gv
        duc = rs * (dxh - jnp.mean(dxh, axis=-1, keepdims=True) - xh * jnp.mean(dxh * xh, axis=-1, keepdims=True))
        return duc, dg, db

    row, fix = (lambda i: (i, 0)), (lambda i: (0, 0))
    return _call(fn, name, (t // tr,),
                 [(uc, (tr, d), row), (g, (1, d), fix), (b, (1, d), fix), (dy, (tr, d), lambda i: (i, dy_col))],
                 [((t, d), (tr, d), row, False), ((1, d), (1, d), fix, True), ((1, d), (1, d), fix, True)], acc_axis=0)


def _windows(t):
    rc = _chunk_len(t)
    return [(r0, rc) for r0 in range(0, t, rc)]


def _taps(w_ref, width):
    return [w_ref[pl.ds(j, 1), :] for j in range(width)]


def _conv_at(xp, taps, r0, rc):
    width = len(taps)
    acc = None
    for j in range(width):
        term = xp[pl.ds(CONV_PAD - (width - 1) + j + r0, rc), :] * taps[j]
        acc = term if acc is None else acc + term
    return acc


def _conv_bwd_in_at(dyp, taps, r0, rc):
    width = len(taps)
    acc = None
    for j in range(width):
        term = dyp[pl.ds(width - 1 - j + r0, rc), :] * taps[j]
        acc = term if acc is None else acc + term
    return acc


def _fold(x):
    acc = x[0:8]
    for i in range(1, x.shape[0] // 8):
        acc = acc + x[8 * i:8 * (i + 1)]
    return acc


def _add_to(accs, vals):
    return vals if accs is None else [a + v for a, v in zip(accs, vals)]


def _conv_bwd_w_at(dy, xp, width, r0, rc):
    return [_fold(dy * xp[pl.ds(CONV_PAD - (width - 1) + j + r0, rc), :]) for j in range(width)]


def _store_taps(dw_ref, accs):
    for j, a in enumerate(accs):
        dw_ref[pl.ds(j, 1), :] = jnp.sum(a, axis=0, keepdims=True)


WIDE_COLS = 2 * LANES


def _zero_front(xp):
    xp[pl.ds(0, CONV_PAD), :] = jnp.zeros((CONV_PAD, xp.shape[1]), F32)


def _zero_back(dyp, t):
    dyp[pl.ds(t, CONV_PAD), :] = jnp.zeros((CONV_PAD, dyp.shape[1]), F32)


def _col_call(body, name, ncol, ins, outs, t, n_scratch, cols=LANES):
    def spec(rows, off):
        return pl.BlockSpec((rows, cols), lambda j, off=off: (0, j + off))

    res = pl.pallas_call(
        body, name=name, grid=(ncol,),
        in_specs=[spec(r, off) for _, r, off in ins],
        out_specs=[spec(o[0], 0) for o in outs],
        out_shape=[jax.ShapeDtypeStruct(o[:2], o[2] if len(o) > 2 else F32) for o in outs],
        scratch_shapes=[pltpu.VMEM((t + CONV_PAD, cols), F32) for _ in range(n_scratch)],
        compiler_params=_cparams(None),
    )(*[a for a, _, _ in ins])
    return res


def _even_col_fwd(name, p, conv_a, conv_b):
    t = p.shape[0]
    nc = D_A // LANES
    wins = _windows(t)

    def body(av, ag, gb, gc, xi, ca, cb, uc_ref, yb_ref, xp):
        _zero_front(xp)
        for r0, rc in wins:
            rows = pl.ds(r0, rc)
            xp[pl.ds(CONV_PAD + r0, rc), :] = av[rows, :] * _sigmoid(ag[rows, :])
        taps = _taps(ca, CONV_A_WIDTH)
        for r0, rc in wins:
            uc_ref[pl.ds(r0, rc), :] = _conv_at(xp, taps, r0, rc)
        for r0, rc in wins:
            rows = pl.ds(r0, rc)
            xp[pl.ds(CONV_PAD + r0, rc), :] = gc[rows, :] * xi[rows, :]
        taps = _taps(cb, CONV_B_WIDTH)
        for r0, rc in wins:
            rows = pl.ds(r0, rc)
            yb_ref[rows, :] = (gb[rows, :] * _conv_at(xp, taps, r0, rc)).astype(yb_ref.dtype)

    ins = [(p, t, k * nc) for k in range(5)] + [(conv_a, CONV_A_WIDTH, 0), (conv_b, CONV_B_WIDTH, 0)]
    return _col_call(body, name, nc, ins, [(t, D_A), (t, D_A, MXU_DTYPE)], t, 1)


def _even_col_bwd(name, p, duc, dy, conv_a, conv_b):
    t = p.shape[0]
    nc = D_A // LANES
    wins = _windows(t)

    def body(av, ag, gb, gc, xi, duc_ref, dyb_ref, ca, cb, dav, dag, dgb, dgc, dxi, dca, dcb, xp, dyp):
        _zero_front(xp)
        _zero_back(dyp, t)
        for r0, rc in wins:
            rows = pl.ds(r0, rc)
            xp[pl.ds(CONV_PAD + r0, rc), :] = av[rows, :] * _sigmoid(ag[rows, :])
            dyp[rows, :] = duc_ref[rows, :]
        taps = _taps(ca, CONV_A_WIDTH)
        accs = None
        for r0, rc in wins:
            rows = pl.ds(r0, rc)
            accs = _add_to(accs, _conv_bwd_w_at(duc_ref[rows, :], xp, CONV_A_WIDTH, r0, rc))
            du = _conv_bwd_in_at(dyp, taps, r0, rc)
            sig = _sigmoid(ag[rows, :])
            dav[rows, :] = (du * sig).astype(dav.dtype)
            dag[rows, :] = (du * av[rows, :] * sig * (1.0 - sig)).astype(dag.dtype)
        _store_taps(dca, accs)
        for r0, rc in wins:
            rows = pl.ds(r0, rc)
            xp[pl.ds(CONV_PAD + r0, rc), :] = gc[rows, :] * xi[rows, :]
        taps = _taps(cb, CONV_B_WIDTH)
        accs = None
        for r0, rc in wins:
            rows = pl.ds(r0, rc)
            dgb[rows, :] = (dyb_ref[rows, :] * _conv_at(xp, taps, r0, rc)).astype(dgb.dtype)
            dzc = dyb_ref[rows, :] * gb[rows, :]
            dyp[rows, :] = dzc
            accs = _add_to(accs, _conv_bwd_w_at(dzc, xp, CONV_B_WIDTH, r0, rc))
        _store_taps(dcb, accs)
        for r0, rc in wins:
            rows = pl.ds(r0, rc)
            dz = _conv_bwd_in_at(dyp, taps, r0, rc)
            dgc[rows, :] = (dz * xi[rows, :]).astype(dgc.dtype)
            dxi[rows, :] = (dz * gc[rows, :]).astype(dxi.dtype)

    ins = ([(p, t, k * nc) for k in range(5)] + [(duc, t, 0), (dy, t, nc)]
           + [(conv_a, CONV_A_WIDTH, 0), (conv_b, CONV_B_WIDTH, 0)])
    outs = [(t, D_A, MXU_DTYPE)] * 5 + [(CONV_A_WIDTH, D_A), (CONV_B_WIDTH, D_A)]
    return _col_call(body, name, nc, ins, outs, t, 2)


def _ffn_col_fwd(name, u, conv, bias):
    t = u.shape[0]
    nc = D_FF // WIDE_COLS
    wins = _windows(t)

    def body(g_ref, v_ref, cw, b_ref, a_ref, xp):
        _zero_front(xp)
        xp[pl.ds(CONV_PAD, t), :] = g_ref[...].astype(F32)
        taps = _taps(cw, CONV_B_WIDTH)
        b = b_ref[...]
        for r0, rc in wins:
            rows = pl.ds(r0, rc)
            gc = _conv_at(xp, taps, r0, rc) + b
            a_ref[rows, :] = (gc * _sigmoid(gc) * v_ref[rows, :].astype(F32)).astype(a_ref.dtype)

    ins = [(u, t, 0), (u, t, nc), (conv, CONV_B_WIDTH, 0), (bias, 1, 0)]
    return _col_call(body, name, nc, ins, [(t, D_FF, MXU_DTYPE)], t, 1, cols=WIDE_COLS)[0]


def _ffn_col_bwd(name, u, da, conv, bias):
    t = u.shape[0]
    nc = D_FF // LANES
    wins = _windows(t)

    def body(g_ref, v_ref, da_ref, cw, b_ref, du_ref, dcw, db_ref, xp, dyp, dval):
        @pl.when(pl.program_id(1) == 0)
        def _():
            _zero_front(xp)
            _zero_back(dyp, t)
            xp[pl.ds(CONV_PAD, t), :] = g_ref[...].astype(F32)
            taps = _taps(cw, CONV_B_WIDTH)
            b = b_ref[...]
            accs, bias_acc = None, None
            for r0, rc in wins:
                rows = pl.ds(r0, rc)
                gc = _conv_at(xp, taps, r0, rc) + b
                s = _sigmoid(gc)
                d = da_ref[rows, :]
                dval[rows, :] = d * gc * s
                dgc = d * v_ref[rows, :].astype(F32) * (s * (1.0 + gc * (1.0 - s)))
                dyp[rows, :] = dgc
                bias_acc = _add_to(bias_acc, [_fold(dgc)])
                accs = _add_to(accs, _conv_bwd_w_at(dgc, xp, CONV_B_WIDTH, r0, rc))
            db_ref[...] = jnp.sum(bias_acc[0], axis=0, keepdims=True)
            _store_taps(dcw, accs)
            for r0, rc in wins:
                du_ref[pl.ds(r0, rc), :] = _conv_bwd_in_at(dyp, taps, r0, rc).astype(du_ref.dtype)

        @pl.when(pl.program_id(1) == 1)
        def _():
            du_ref[...] = dval[...].astype(du_ref.dtype)

    col = lambda rows, off: pl.BlockSpec((rows, LANES), lambda j, p: (0, j + off))
    return pl.pallas_call(
        body, name=name, grid=(nc, 2),
        in_specs=[col(t, 0), col(t, nc), col(t, 0), col(CONV_B_WIDTH, 0), col(1, 0)],
        out_specs=[pl.BlockSpec((t, LANES), lambda j, p: (0, j + nc * p)), col(CONV_B_WIDTH, 0), col(1, 0)],
        out_shape=[jax.ShapeDtypeStruct((t, 2 * D_FF), MXU_DTYPE), jax.ShapeDtypeStruct((CONV_B_WIDTH, D_FF), F32),
                   jax.ShapeDtypeStruct((1, D_FF), F32)],
        scratch_shapes=[pltpu.VMEM((t + CONV_PAD, LANES), F32) for _ in range(2)] + [pltpu.VMEM((t, LANES), F32)],
        compiler_params=_cparams(None),
    )(u, u, da, conv, bias)


def _shift_fwd(name, p, col0, mu):
    t = p.shape[0]
    wins = _windows(t)

    def body(x_ref, mu_ref, o_ref, xp):
        _zero_front(xp)
        xp[pl.ds(CONV_PAD, t), :] = x_ref[...]
        mu_v = mu_ref[...]
        for r0, rc in wins:
            rows = pl.ds(r0, rc)
            x = x_ref[rows, :]
            o_ref[rows, :] = x + (xp[pl.ds(CONV_PAD - 1 + r0, rc), :] - x) * mu_v

    return _col_call(body, name, RWKV_COLS // WIDE_COLS, [(p, t, col0 // WIDE_COLS), (mu, 1, 0)], [(t, RWKV_COLS)], t, 1,
                     cols=WIDE_COLS)[0]


def _shift_bwd(name, p, col0, mu, dprs):
    t = p.shape[0]
    wins = _windows(t)

    def body(x_ref, mu_ref, d_ref, dx_ref, dmu_ref, xp, dyp):
        _zero_front(xp)
        _zero_back(dyp, t)
        xp[pl.ds(CONV_PAD, t), :] = x_ref[...]
        mu_v = mu_ref[...]
        acc = None
        for r0, rc in wins:
            rows = pl.ds(r0, rc)
            d = d_ref[rows, :]
            acc = _add_to(acc, [_fold(d * (xp[pl.ds(CONV_PAD - 1 + r0, rc), :] - x_ref[rows, :]))])
            dyp[rows, :] = d * mu_v
        dmu_ref[...] = jnp.sum(acc[0], axis=0, keepdims=True)
        for r0, rc in wins:
            rows = pl.ds(r0, rc)
            dx_ref[rows, :] = d_ref[rows, :] - dyp[rows, :] + dyp[pl.ds(1 + r0, rc), :]

    ins = [(p, t, col0 // WIDE_COLS), (mu, 1, 0), (dprs, t, 0)]
    return _col_call(body, name, RWKV_COLS // WIDE_COLS, ins, [(t, RWKV_COLS), (1, RWKV_COLS)], t, 2, cols=WIDE_COLS)


def _hi_lo(x):
    hi = x.astype(BF16)
    return hi, (x - hi.astype(F32)).astype(BF16)


def _dot_passes(a, b, dims, passes):
    d = lambda p, q: lax.dot_general(p, q, dims, preferred_element_type=F32)
    if passes == 1:
        return d(a.astype(MXU_DTYPE), b.astype(MXU_DTYPE))
    ah, al = _hi_lo(a)
    bh, bl = _hi_lo(b)
    return d(ah, bh) + (d(ah, bl) + d(al, bh))


@functools.partial(jax.custom_vjp, nondiff_argnums=(2, 3))
def _dot_vjp(a, b, dims, passes):
    return _dot_passes(a, b, dims, passes)


def _dot_fwd(a, b, dims, passes):
    return _dot_passes(a, b, dims, passes), (a, b)


def _dot_bwd(dims, passes, res, g):
    a, b = res
    if dims == _NN:
        return _dot_passes(g, b, _NT, passes), _dot_passes(a, g, _TN, passes)
    if dims == _NT:
        return _dot_passes(g, b, _NN, passes), _dot_passes(g, a, _TN, passes)
    return _dot_passes(b, g, _NT, passes), _dot_passes(a, g, _NN, passes)


_dot_vjp.defvjp(_dot_fwd, _dot_bwd)


def _doth(a, b, dims=_NN):
    return _dot_vjp(a, b, dims, 3)


def _dotb(a, b, dims=_NN):
    return _dot_vjp(a, b, dims, 1)


def _softplus(x):
    return jnp.where(x > 0, x, 0.0) + jnp.log(1.0 + jnp.exp(jnp.where(x > 0, -x, x)))


def _rwkv_pre(k, xl, gd, w0, w2p, a0, a2p, g2, k_k, k_a, seg):
    z = w0 + _dotb(jnp.tanh(xl), w2p)
    lw = -jnp.exp(-_softplus(-z) - 0.5)
    alpha = _sigmoid(a0 + _dotb(xl, a2p))
    g = _dotb(_sigmoid(gd), g2)
    kk = k * k_k
    kk = kk / jnp.maximum(jnp.sqrt(_dotb(kk * kk, seg)), 1e-12)
    k2 = k * (1.0 + (alpha - 1.0) * k_a)
    return lw, k2, -kk, kk * alpha, g


def _rwkv_post(y, r, k2, v, g, lnx_g, lnx_b, r_k, seg):
    mean = _dotb(y, seg) * (1.0 / HEAD_DIM)
    yc = y - mean
    var = _dotb(yc * yc, seg) * (1.0 / HEAD_DIM)
    yo = yc * lax.rsqrt(var + RWKV_GN_EPS) * lnx_g + lnx_b
    bonus = _dotb(r * k2 * r_k, seg) * v
    return (yo + bonus) * g


def _rwkv_pre_fwd(name, prs, prm, seg, tr):
    t = prs.shape[0]
    row = lambda i: (i, 0)
    fix = lambda i: (0, 0)
    ins = [(prs, (tr, D_R), lambda i: (i, 1)), (prs, (tr, LANES), lambda i: (i, 12)), (prs, (tr, LANES), lambda i: (i, 13)),
           (prm["w0"], (1, D_R), fix), (prm["w2p"], (LANES, D_R), fix), (prm["a0"], (1, D_R), fix),
           (prm["a2p"], (LANES, D_R), fix), (prm["g2"], (LANES, D_R), fix), (prm["k_k"], (1, D_R), fix),
           (prm["k_a"], (1, D_R), fix), (seg, (D_R, D_R), fix)]
    return _call(_rwkv_pre, name, (t // tr,), ins, [((t, D_R), (tr, D_R), row, False)] * 5)


def _rwkv_pre_bwd(name, prs, prm, seg, cts, tr):
    t = prs.shape[0]

    def fn(k, xl, gd, w0, w2p, a0, a2p, g2, k_k, k_a, segv, *ct):
        _, vjp = jax.vjp(lambda *a: _rwkv_pre(*a, segv), k, xl, gd, w0, w2p, a0, a2p, g2, k_k, k_a)
        return vjp(tuple(ct))

    row = lambda i: (i, 0)
    fix = lambda i: (0, 0)
    ins = [(prs, (tr, D_R), lambda i: (i, 1)), (prs, (tr, LANES), lambda i: (i, 12)), (prs, (tr, LANES), lambda i: (i, 13)),
           (prm["w0"], (1, D_R), fix), (prm["w2p"], (LANES, D_R), fix), (prm["a0"], (1, D_R), fix),
           (prm["a2p"], (LANES, D_R), fix), (prm["g2"], (LANES, D_R), fix), (prm["k_k"], (1, D_R), fix),
           (prm["k_a"], (1, D_R), fix), (seg, (D_R, D_R), fix)] + [(c, (tr, D_R), row) for c in cts]
    outs = [((t, D_R), (tr, D_R), row, False), ((t, LANES), (tr, LANES), row, False), ((t, LANES), (tr, LANES), row, False),
            ((1, D_R), (1, D_R), fix, True), ((LANES, D_R), (LANES, D_R), fix, True), ((1, D_R), (1, D_R), fix, True),
            ((LANES, D_R), (LANES, D_R), fix, True), ((LANES, D_R), (LANES, D_R), fix, True),
            ((1, D_R), (1, D_R), fix, True), ((1, D_R), (1, D_R), fix, True)]
    return _call(fn, name, (t // tr,), ins, outs, acc_axis=0)


def _rwkv_post_ins(y, prs, k2, g, prm, seg, tr):
    row = lambda i: (i, 0)
    fix = lambda i: (0, 0)
    return [(y, (tr, D_R), row), (prs, (tr, D_R), row), (k2, (tr, D_R), row), (prs, (tr, D_R), lambda i: (i, 2)),
            (g, (tr, D_R), row), (prm["lnx_g"], (1, D_R), fix), (prm["lnx_b"], (1, D_R), fix), (prm["r_k"], (1, D_R), fix),
            (seg, (D_R, D_R), fix)]


def _rwkv_post_fwd(name, y, prs, k2, g, prm, seg, tr):
    t = y.shape[0]
    return _call(_rwkv_post, name, (t // tr,), _rwkv_post_ins(y, prs, k2, g, prm, seg, tr),
                 [((t, D_R), (tr, D_R), lambda i: (i, 0), False)])


def _rwkv_post_bwd(name, y, prs, k2, g, prm, seg, dy, dy_col, tr):
    t = y.shape[0]

    def fn(yv, r, k2v, v, gv, lg, lb, rk, segv, ct):
        _, vjp = jax.vjp(lambda *a: _rwkv_post(*a, segv), yv, r, k2v, v, gv, lg, lb, rk)
        return vjp(ct)

    row = lambda i: (i, 0)
    fix = lambda i: (0, 0)
    ins = _rwkv_post_ins(y, prs, k2, g, prm, seg, tr) + [(dy, (tr, D_R), lambda i: (i, dy_col))]
    outs = [((t, D_R), (tr, D_R), row, False)] * 5 + [((1, D_R), (1, D_R), fix, True)] * 3
    return _call(fn, name, (t // tr,), ins, outs, acc_axis=0)


def _wkv_chunk(s0, r, lw, k, v, a, b):
    c = r[0].shape[0]
    lane = lax.broadcasted_iota(jnp.int32, (1, 2 * HEAD_DIM), 1)
    first = (lane < HEAD_DIM).astype(F32)
    per_head = lambda x: jnp.concatenate([x * first, x * (1.0 - first)], axis=0)

    def time_of(shape, dim):
        i = lax.broadcasted_iota(jnp.int32, shape, dim)
        return jnp.where(i >= c, i - c, i)

    incl = (lax.broadcasted_iota(jnp.int32, (c, c), 0) >= lax.broadcasted_iota(jnp.int32, (c, c), 1)).astype(F32)
    strict2 = time_of((2 * c, 2 * c), 0) > time_of((2 * c, 2 * c), 1)
    incl2 = lax.broadcasted_iota(jnp.int32, (c, 2 * c), 0) >= time_of((c, 2 * c), 1)
    each = lambda f, *xs: [f(*x) for x in zip(*xs)]
    cum = each(lambda x: _doth(incl, x), lw)
    tot = each(lambda x: jnp.sum(x, axis=0, keepdims=True), lw)
    e_inv = each(lambda x: jnp.exp(-x), cum)
    a_st = each(lambda x, cm, l: per_head(x * jnp.exp(cm - l)), a, cum, lw)
    r_t = each(lambda x, cm: x * jnp.exp(cm), r, cum)
    b_st = each(lambda x, e: per_head(x * e), b, e_inv)
    k_st = each(lambda x, e: per_head(x * e), k, e_inv)
    v_st = each(per_head, v)
    m = each(lambda x, w: jnp.where(strict2, _dotb(x, w, _NT), 0.0), a_st, b_st)
    m_k = each(lambda x, w: jnp.where(strict2, _dotb(x, w, _NT), 0.0), a_st, k_st)
    u = each(lambda x, s, mk, w: _dotb(x, s, _NT) + _dotb(mk, w), a_st, s0, m_k, v_st)
    steps = (c - 1).bit_length()
    for s in range(steps):
        u = each(lambda x, w: x + _dotb(w, x), u, m)
        if s + 1 < steps:
            m = each(lambda w: _dotb(w, w), m)
    n_b = each(lambda x, w: jnp.where(incl2, _dotb(x, w, _NT), 0.0), r_t, b_st)
    n_k = each(lambda x, w: jnp.where(incl2, _dotb(x, w, _NT), 0.0), r_t, k_st)
    y = each(lambda x, s, nb, uu, nk, w: _dotb(x, s, _NT) + _dotb(nb, uu) + _dotb(nk, w), r_t, s0, n_b, u, n_k, v_st)
    dec = each(lambda tt, cm: jnp.exp(tt - cm), tot, cum)
    s1 = each(lambda s, tt, uu, x, d, w, kk: s * jnp.exp(tt) + _dotb(uu, per_head(x * d), _TN) + _dotb(w, per_head(kk * d), _TN),
              s0, tot, u, b, dec, v_st, k)
    return tuple(y), tuple(s1)


WKV_PAIRS_PER_STEP = 4
PAIR = 2 * HEAD_DIM


def _wkv_fwd(name, srcs):
    t = srcs[0][0].shape[0]
    c = _chunk_len(t)
    nc = t // c
    pp = WKV_PAIRS_PER_STEP
    n_pairs = D_R // PAIR

    def body(r, lw, k, v, a, b, y_ref, st_ref, state):
        @pl.when(pl.program_id(1) == 0)
        def _():
            state[...] = jnp.zeros(state.shape, F32)

        pairs = lambda ref: tuple(ref[:, pl.ds(i * PAIR, PAIR)] for i in range(pp))
        s0 = tuple(state[i] for i in range(pp))
        y, s1 = _wkv_chunk(s0, pairs(r), pairs(lw), pairs(k), pairs(v), pairs(a), pairs(b))
        for i in range(pp):
            st_ref[i] = s0[i]
            y_ref[:, pl.ds(i * PAIR, PAIR)] = y[i]
            state[i] = s1[i]

    seq = lambda off: pl.BlockSpec((c, pp * PAIR), lambda g, j: (j, off + g))
    return pl.pallas_call(
        body, name=name, grid=(n_pairs // pp, nc), in_specs=[seq(off) for _, off in srcs],
        out_specs=[seq(0), pl.BlockSpec((pp, None, PAIR, PAIR), lambda g, j: (g, j, 0, 0))],
        out_shape=[jax.ShapeDtypeStruct((t, D_R), F32), jax.ShapeDtypeStruct((n_pairs, nc, PAIR, PAIR), F32)],
        scratch_shapes=[pltpu.VMEM((pp, PAIR, PAIR), F32)],
        compiler_params=_cparams(None),
    )(*[a for a, _ in srcs])


def _wkv_bwd(name, srcs, st, dy):
    t = srcs[0][0].shape[0]
    c = _chunk_len(t)
    nc = t // c
    pp = WKV_PAIRS_PER_STEP
    n_pairs = D_R // PAIR

    def body(r, lw, k, v, a, b, st_ref, dy_ref, dr, dlw, dk, dv, da, db, dstate):
        @pl.when(pl.program_id(1) == 0)
        def _():
            dstate[...] = jnp.zeros(dstate.shape, F32)

        half = lax.broadcasted_iota(jnp.int32, (PAIR, PAIR), 0) < HEAD_DIM
        same_head = half == (lax.broadcasted_iota(jnp.int32, (PAIR, PAIR), 1) < HEAD_DIM)
        pairs = lambda ref: tuple(ref[:, pl.ds(i * PAIR, PAIR)] for i in range(pp))
        s0 = tuple(st_ref[i] for i in range(pp))
        _, vjp = jax.vjp(_wkv_chunk, s0, pairs(r), pairs(lw), pairs(k), pairs(v), pairs(a), pairs(b))
        ds0, *dxs = vjp((pairs(dy_ref), tuple(dstate[i] for i in range(pp))))
        for i in range(pp):
            for ref, val in zip((dr, dlw, dk, dv, da, db), dxs):
                ref[:, pl.ds(i * PAIR, PAIR)] = val[i]
            dstate[i] = jnp.where(same_head, ds0[i], 0.0)

    seq = lambda off: pl.BlockSpec((c, pp * PAIR), lambda g, j: (nc - 1 - j, off + g))
    return pl.pallas_call(
        body, name=name, grid=(n_pairs // pp, nc),
        in_specs=[seq(off) for _, off in srcs]
        + [pl.BlockSpec((pp, None, PAIR, PAIR), lambda g, j: (g, nc - 1 - j, 0, 0)), seq(dy[1])],
        out_specs=[seq(0)] * 6,
        out_shape=[jax.ShapeDtypeStruct((t, D_R), F32)] * 6,
        scratch_shapes=[pltpu.VMEM((pp, PAIR, PAIR), F32)],
        compiler_params=_cparams(None),
    )(*[a for a, _ in srcs], st, dy[0])


def _rope(x, cos, sin, rot):
    return x * cos + _dotb(x, rot) * sin


def _attn_block(nb, q, kp, kc, km, vp, vc, vm, sk, cq, sq, cp, sp, cm, sm, rot):
    g = GQA_GROUP
    scale = HEAD_DIM ** -0.5
    each = lambda f, *xs: [f(*x) for x in zip(*xs)]
    down = lambda x: jnp.concatenate([x] * g, axis=0)
    cq4, sq4 = down(cq), down(sq)
    kpr = each(lambda x: _rope(x, cp, sp, rot), kp)
    kcr = each(lambda x: _rope(x, cq, sq, rot), kc)
    kmr = each(lambda x: _rope(x, cm, sm, rot), km)
    qr = each(lambda x: _rope(x, cq4, sq4, rot), q)
    i = lax.broadcasted_iota(jnp.int32, (g * BLOCK, BLOCK), 0)
    i = i - BLOCK * ((i >= BLOCK).astype(jnp.int32) + (i >= 2 * BLOCK).astype(jnp.int32) + (i >= 3 * BLOCK).astype(jnp.int32))
    j = lax.broadcasted_iota(jnp.int32, (g * BLOCK, BLOCK), 1)
    nbv = jnp.zeros((g * BLOCK, BLOCK), jnp.int32) + nb
    ok_p = (j > i) & (nbv >= 2)
    ok_c = (j <= i) & (nbv >= 1)
    ok_m = (j >= BLOCK - N_META) & ((nbv >= 1) | (j <= i))
    sink = each(lambda s4: jnp.concatenate([jnp.broadcast_to(s, (BLOCK, 1)) for s in s4], axis=0), sk)
    s_p = each(lambda x, kk: jnp.where(ok_p, _dotb(x, kk, _NT) * scale, NEG_INF), qr, kpr)
    s_c = each(lambda x, kk: jnp.where(ok_c, _dotb(x, kk, _NT) * scale, NEG_INF), qr, kcr)
    s_m = each(lambda x, kk: jnp.where(ok_m, _dotb(x, kk, _NT) * scale, NEG_INF), qr, kmr)
    rmax = lambda s: jnp.max(s, axis=-1, keepdims=True)
    m = each(lambda a, b, c, d: lax.stop_gradient(jnp.maximum(jnp.maximum(rmax(a), rmax(b)), jnp.maximum(rmax(c), d))),
             s_p, s_c, s_m, sink)
    e_p = each(lambda s, mm: jnp.exp(s - mm), s_p, m)
    e_c = each(lambda s, mm: jnp.exp(s - mm), s_c, m)
    e_m = each(lambda s, mm: jnp.exp(s - mm), s_m, m)
    rsum = lambda e: jnp.sum(e, axis=-1, keepdims=True)
    inv = each(lambda a, b, c, d, mm: 1.0 / (rsum(a) + rsum(b) + rsum(c) + jnp.exp(d - mm)), e_p, e_c, e_m, sink, m)
    return tuple(each(lambda a, b, c, iv, x, y, z: _dotb(a * iv, x) + _dotb(b * iv, y) + _dotb(c * iv, z),
                      e_p, e_c, e_m, inv, vp, vc, vm))


def _attn_specs():
    cur = lambda n: (0, n, 0)
    prev = lambda n: (0, jnp.maximum(n - 1, 0), 0)
    meta = lambda n: (0, 0, 0)
    kv = lambda m: pl.BlockSpec((N_KV_HEADS, BLOCK, HEAD_DIM), m)
    tab = lambda m: pl.BlockSpec((BLOCK, HEAD_DIM), m)
    tcur, tprev, tmeta = (lambda n: (n, 0)), (lambda n: (jnp.maximum(n - 1, 0), 0)), (lambda n: (0, 0))
    qspec = pl.BlockSpec((N_Q_HEADS, BLOCK, HEAD_DIM), cur)
    sspec = pl.BlockSpec((N_Q_HEADS, 8, LANES), meta)
    specs = [qspec, kv(prev), kv(cur), kv(meta), kv(prev), kv(cur), kv(meta), sspec,
             tab(tcur), tab(tcur), tab(tprev), tab(tprev), tab(tmeta), tab(tmeta),
             pl.BlockSpec((HEAD_DIM, HEAD_DIM), lambda n: (0, 0))]
    return specs, qspec, sspec, kv


def _attn_args(q, k, v, sinks_b, cos, sin, rot):
    return (q, k, k, k, v, v, v, sinks_b, cos, sin, cos, sin, cos, sin, rot)


def _attn_operands(q_ref, kp, kc, km, vp, vc, vm, s_ref):
    groups = range(N_KV_HEADS)
    q = tuple(jnp.concatenate([q_ref[GQA_GROUP * i + h] for h in range(GQA_GROUP)], axis=0) for i in groups)
    sk = tuple(tuple(s_ref[GQA_GROUP * i + h][0:1, 0:1] for h in range(GQA_GROUP)) for i in groups)
    per_head = lambda ref: tuple(ref[i] for i in groups)
    return q, per_head(kp), per_head(kc), per_head(km), per_head(vp), per_head(vc), per_head(vm), sk


def _attn_fwd(name, q, k, v, sinks_b, cos, sin, rot):
    tp = q.shape[1]
    specs, qspec, _, _ = _attn_specs()

    def body(q_ref, kp, kc, km, vp, vc, vm, s_ref, cq, sq, cp, sp, cm, sm, rot_ref, o_ref):
        out = _attn_block(pl.program_id(0), *_attn_operands(q_ref, kp, kc, km, vp, vc, vm, s_ref),
                          cq[...], sq[...], cp[...], sp[...], cm[...], sm[...], rot_ref[...])
        for i in range(N_KV_HEADS):
            for h in range(GQA_GROUP):
                o_ref[GQA_GROUP * i + h] = out[i][h * BLOCK:(h + 1) * BLOCK]

    return pl.pallas_call(
        body, name=name, grid=(tp // BLOCK,), in_specs=specs, out_specs=qspec,
        out_shape=jax.ShapeDtypeStruct(q.shape, F32), compiler_params=_cparams(None),
    )(*_attn_args(q, k, v, sinks_b, cos, sin, rot))


def _attn_bwd(name, q, k, v, sinks_b, cos, sin, rot, do):
    tp = q.shape[1]
    nb = tp // BLOCK
    specs, qspec, sspec, kv = _attn_specs()

    def body(q_ref, kp, kc, km, vp, vc, vm, s_ref, cq, sq, cp, sp, cm, sm, rot_ref, do_ref,
             dq_ref, dkp, dkc, dvp, dvc, dkm, dvm, ds_ref):
        n = pl.program_id(0)
        tabs = (cq[...], sq[...], cp[...], sp[...], cm[...], sm[...], rot_ref[...])
        _, vjp = jax.vjp(lambda *a: _attn_block(n, *a, *tabs), *_attn_operands(q_ref, kp, kc, km, vp, vc, vm, s_ref))
        do_all = tuple(jnp.concatenate([do_ref[GQA_GROUP * i + h] for h in range(GQA_GROUP)], axis=0)
                       for i in range(N_KV_HEADS))
        dq, gkp, gkc, gkm, gvp, gvc, gvm, dsk = vjp(do_all)
        for i in range(N_KV_HEADS):
            dkp[i] = gkp[i]
            dkc[i] = gkc[i]
            dvp[i] = gvp[i]
            dvc[i] = gvc[i]
            for h in range(GQA_GROUP):
                dq_ref[GQA_GROUP * i + h] = dq[i][h * BLOCK:(h + 1) * BLOCK]

        @pl.when(n == 0)
        def _():
            for i in range(N_KV_HEADS):
                dkm[i] = gkm[i]
                dvm[i] = gvm[i]
                for h in range(GQA_GROUP):
                    ds_ref[GQA_GROUP * i + h] = jnp.broadcast_to(dsk[i][h], (8, LANES))

        @pl.when(n != 0)
        def _():
            for i in range(N_KV_HEADS):
                dkm[i] += gkm[i]
                dvm[i] += gvm[i]
                for h in range(GQA_GROUP):
                    ds_ref[GQA_GROUP * i + h] += jnp.broadcast_to(dsk[i][h], (8, LANES))

    part = pl.BlockSpec((N_KV_HEADS, None, BLOCK, HEAD_DIM), lambda n: (0, n, 0, 0))
    part_shape = jax.ShapeDtypeStruct((N_KV_HEADS, nb, BLOCK, HEAD_DIM), F32)
    meta_shape = jax.ShapeDtypeStruct((N_KV_HEADS, BLOCK, HEAD_DIM), F32)
    return pl.pallas_call(
        body, name=name, grid=(nb,), in_specs=specs + [qspec],
        out_specs=[qspec, part, part, part, part, kv(lambda n: (0, 0, 0)), kv(lambda n: (0, 0, 0)), sspec],
        out_shape=[jax.ShapeDtypeStruct(q.shape, F32), part_shape, part_shape, part_shape, part_shape,
                   meta_shape, meta_shape, jax.ShapeDtypeStruct(sinks_b.shape, F32)],
        compiler_params=_cparams(None),
    )(*_attn_args(q, k, v, sinks_b, cos, sin, rot), do)


def _kv_combine(name, prev_part, own_part, meta):
    g, nb = own_part.shape[:2]

    def fn(own, nxt, mt):
        m = pl.program_id(1)
        one = jnp.ones((BLOCK, HEAD_DIM), F32)
        use_next = jnp.where(one * m < nb - 1, 1.0, 0.0)
        use_meta = jnp.where(one * m < 1, 1.0, 0.0)
        return own + nxt * use_next + mt * use_meta

    blk = (None, None, BLOCK, HEAD_DIM)
    return _call(fn, name, (g, nb),
                 [(own_part, blk, lambda a, m: (a, m, 0, 0)),
                  (prev_part, blk, lambda a, m: (a, jnp.minimum(m + 1, nb - 1), 0, 0)),
                  (meta, (None, BLOCK, HEAD_DIM), lambda a, m: (a, 0, 0))],
                 [((g, nb * BLOCK, HEAD_DIM), (None, BLOCK, HEAD_DIM), lambda a, m: (a, m, 0), False)])


PACK_W = 1024
ELEMENTWISE_BLOCK_BYTES = 1 << 21


def _rows_tile(rows, cols):
    cap = max(8, ELEMENTWISE_BLOCK_BYTES // (4 * cols))
    for d in range(min(rows, cap), 0, -1):
        if rows % d == 0 and d % 8 == 0:
            return d
    return rows


def _adamw(name, w, g, m, v):
    rows, cols = w.shape
    tr = _rows_tile(rows, cols)

    def fn(wv, gv, mv, vv):
        m1 = ADAM_B1 * mv + (1.0 - ADAM_B1) * gv
        v1 = ADAM_B2 * vv + (1.0 - ADAM_B2) * (gv * gv)
        m_hat = m1 / (1.0 - ADAM_B1 ** ADAM_STEP)
        v_hat = v1 / (1.0 - ADAM_B2 ** ADAM_STEP)
        return -ADAM_LR * (m_hat / (jnp.sqrt(v_hat) + ADAM_EPS) + ADAM_WD * wv), m1, v1

    blk = (tr, cols)
    row = lambda i: (i, 0)
    return _call(fn, name, (rows // tr,), [(a, blk, row) for a in (w, g, m, v)], [((rows, cols), blk, row, False)] * 3)


def _pair_add_placed(name, g, recv, cm_idx, out_dtype):
    s, a, b = g.shape
    half = a // 2

    def body(cm_ref, a_ref, b_ref, o_ref, own_ref):
        val = (a_ref[...] + b_ref[...]).astype(out_dtype)
        o_ref[...] = val

        @pl.when(pl.program_id(0) == cm_ref[1])
        def _():
            own_ref[...] = val

    blk = (None, half, b)
    shape = jax.ShapeDtypeStruct((s, half, b), out_dtype)
    return pl.pallas_call(
        body, name=name,
        grid_spec=pltpu.PrefetchScalarGridSpec(
            num_scalar_prefetch=1, grid=(s,),
            in_specs=[pl.BlockSpec(blk, lambda j, cm: (j, cm[0], 0)), pl.BlockSpec(blk, lambda j, cm: (j, 0, 0))],
            out_specs=[pl.BlockSpec(blk, lambda j, cm: (j, 0, 0)), pl.BlockSpec(blk, lambda j, cm: (cm[1], 0, 0))]),
        out_shape=[shape, shape], compiler_params=_cparams(None),
    )(cm_idx, g, recv)


def _sum_chips(name, parts, c_idx, layer, n_layers, into=None):
    _, a, b = parts.shape
    tr = _rows_tile(a, b)

    def body(c_ref, p0, p1, p2, p3, *rest):
        o_ref = rest[-1]
        up = lambda p: p[...].astype(F32)
        o_ref[...] = ((up(p0) + up(p1)) + up(p2)) + up(p3)

    in_specs = [pl.BlockSpec((None, tr, b), lambda i, c, k=k: (k, i, 0)) for k in range(N_CHIPS)]
    args = [c_idx] + [parts] * N_CHIPS
    aliases = {}
    if into is not None:
        in_specs.append(_ANY)
        args.append(into)
        aliases = {1 + N_CHIPS: 0}
    return pl.pallas_call(
        body, name=name,
        grid_spec=pltpu.PrefetchScalarGridSpec(
            num_scalar_prefetch=1, grid=(a // tr,), in_specs=in_specs,
            out_specs=pl.BlockSpec((None, None, tr, b), lambda i, c: (layer, c[0], i, 0))),
        out_shape=jax.ShapeDtypeStruct((n_layers, 2, a, b), F32), input_output_aliases=aliases,
        compiler_params=_cparams(None),
    )(*args)


def _place_own_block(name, w, layer, me_idx, dtype):
    _, a2, b = w.shape
    a = a2 // 2
    tr = _rows_tile(a, b)
    nb = a // tr

    def body(me_ref, w_ref, o_ref):
        o_ref[...] = w_ref[...].astype(dtype)

    return pl.pallas_call(
        body, name=name,
        grid_spec=pltpu.PrefetchScalarGridSpec(
            num_scalar_prefetch=1, grid=(2, nb),
            in_specs=[pl.BlockSpec((None, tr, b), lambda h, i, me: (layer, h * nb + i, 0))],
            out_specs=pl.BlockSpec((None, None, tr, b), lambda h, i, me: (me[0], h, i, 0))),
        out_shape=jax.ShapeDtypeStruct((N_CHIPS, 2, a, b), dtype), compiler_params=_cparams(None),
    )(me_idx, w)


def _mesh_pos():
    return lax.axis_index("x"), lax.axis_index("y"), lax.axis_index("c")


def _other_chips(x, y):
    return [(1 - x, y), (x, 1 - y), (1 - x, 1 - y)]


_ANY = pl.BlockSpec(memory_space=pl.ANY)


def _gather_weights(name, bufs, from_chips=True):
    n = len(bufs)

    def body(*refs):
        out_refs = refs[n:2 * n]
        send_sems, recv_sems = refs[2 * n:]
        x, y, c = _mesh_pos()
        me = 2 * x + y
        sibling = (x, y, 1 - c)
        chips = _other_chips(x, y)

        def copy(i, k, chip_idx, half, to):
            return pltpu.make_async_remote_copy(src_ref=out_refs[i].at[chip_idx, half], dst_ref=out_refs[i].at[chip_idx, half],
                                                send_sem=send_sems.at[6 * i + k], recv_sem=recv_sems.at[6 * i + k],
                                                device_id=to, device_id_type=MESH)

        first = [copy(i, j, me, c, (*chip, c)) for i in range(n) for j, chip in enumerate(chips)] if from_chips else []
        for cp in first:
            cp.start()
        passed = []
        for i in range(n):
            for j, (cx, cy) in enumerate(chips):
                idx = 2 * cx + cy
                if from_chips:
                    copy(i, j, idx, c, sibling).wait_recv()
                fwd = copy(i, 3 + j, idx, c, sibling)
                fwd.start()
                passed.append(fwd)
        for i in range(n):
            for j, (cx, cy) in enumerate(chips):
                copy(i, 3 + j, 2 * cx + cy, 1 - c, sibling).wait_recv()
        for cp in first + passed:
            cp.wait_send()

    return pl.pallas_call(
        body, name=name, in_specs=[_ANY] * n, out_specs=[_ANY] * n,
        out_shape=[jax.ShapeDtypeStruct(b.shape, b.dtype) for b in bufs],
        input_output_aliases={i: i for i in range(n)},
        scratch_shapes=[pltpu.SemaphoreType.DMA((6 * n,)), pltpu.SemaphoreType.DMA((6 * n,))],
        compiler_params=pltpu.CompilerParams(has_side_effects=True),
    )(*bufs)


def _gather_start(name, groups):
    bufs = [b for g in groups for b in g]
    n = len(bufs)
    ng = len(groups)

    def body(*refs):
        b_refs = refs[:n]
        sems = refs[n:n + 2 * ng]
        token = refs[-1]
        x, y, c = _mesh_pos()
        me = 2 * x + y
        i = 0
        for gi, g in enumerate(groups):
            for k in range(len(g)):
                for j, (cx, cy) in enumerate(_other_chips(x, y)):
                    pltpu.make_async_remote_copy(src_ref=b_refs[i].at[me, c], dst_ref=b_refs[i].at[me, c],
                                                 send_sem=sems[2 * gi].at[3 * k + j], recv_sem=sems[2 * gi + 1].at[3 * k + j],
                                                 device_id=(cx, cy, c), device_id_type=MESH).start()
                i += 1
        token[...] = jnp.zeros(token.shape, F32)

    sem_shapes = [pltpu.SemaphoreType.DMA((3 * len(g),)) for g in groups for _ in range(2)]
    res = pl.pallas_call(
        body, name=name,
        out_shape=(*sem_shapes, *[pltpu.HBM(b.shape, b.dtype) for b in bufs], jax.ShapeDtypeStruct((8, LANES), F32)),
        in_specs=[_HBM] * n,
        out_specs=(*[_SEM] * (2 * ng), *[_HBM] * n, pl.BlockSpec(memory_space=pltpu.VMEM)),
        input_output_aliases={i: 2 * ng + i for i in range(n)},
        compiler_params=pltpu.CompilerParams(has_side_effects=_DATAFLOW),
    )(*[pltpu.with_memory_space_constraint(b, pltpu.HBM) for b in bufs])
    out, i = [], 2 * ng
    for gi, g in enumerate(groups):
        out.append((res[2 * gi], res[2 * gi + 1], list(res[i:i + len(g)])))
        i += len(g)
    return out, res[-1]


def _gather_wait(name, send_sems, recv_sems, bufs, after):
    n = len(bufs)

    def body(*refs):
        b_refs = refs[:n]
        s_sems, r_sems = refs[n], refs[n + 1]
        x, y, c = _mesh_pos()
        me = 2 * x + y
        for k in range(n):
            for j, (cx, cy) in enumerate(_other_chips(x, y)):
                idx = 2 * cx + cy
                copy = pltpu.make_async_remote_copy(src_ref=b_refs[k].at[me, c], dst_ref=b_refs[k].at[idx, c],
                                                    send_sem=s_sems.at[3 * k + j], recv_sem=r_sems.at[3 * k + j],
                                                    device_id=(cx, cy, c), device_id_type=MESH)
                copy.wait_send()
                copy.wait_recv()

    res = pl.pallas_call(
        body, name=name,
        out_shape=tuple(pltpu.HBM(b.shape, b.dtype) for b in bufs),
        in_specs=[_HBM] * n + [_SEM, _SEM, _ANY],
        out_specs=tuple([_HBM] * n),
        input_output_aliases={i: i for i in range(n)},
        compiler_params=pltpu.CompilerParams(has_side_effects=_DATAFLOW),
    )(*bufs, send_sems, recv_sems, after)
    return list(res)


def _halves_to_sibling(name, units):
    n = len(units)

    def body(*refs):
        g_refs, out_refs = refs[:n], refs[n:2 * n]
        send_sems, recv_sems = refs[2 * n:]
        x, y, c = _mesh_pos()
        cps = []
        for i in range(n):
            half = units[i].shape[1] // 2
            src = g_refs[i].at[pl.ds(0, N_CHIPS), pl.ds((1 - c) * half, half)]
            cp = pltpu.make_async_remote_copy(src_ref=src, dst_ref=out_refs[i], send_sem=send_sems.at[i],
                                              recv_sem=recv_sems.at[i], device_id=(x, y, 1 - c), device_id_type=MESH)
            cp.start()
            cps.append(cp)
        for cp in cps:
            cp.wait()

    return pl.pallas_call(
        body, name=name, in_specs=[_ANY] * n, out_specs=[_ANY] * n,
        out_shape=[jax.ShapeDtypeStruct((u.shape[0], u.shape[1] // 2, u.shape[2]), u.dtype) for u in units],
        scratch_shapes=[pltpu.SemaphoreType.DMA((n,)), pltpu.SemaphoreType.DMA((n,))],
        compiler_params=pltpu.CompilerParams(has_side_effects=True),
    )(*units)


_HBM = pl.BlockSpec(memory_space=pltpu.HBM)
_SEM = pl.BlockSpec(memory_space=pltpu.SEMAPHORE)
_DATAFLOW = pltpu.SideEffectType.DATAFLOW_SIDE_EFFECTING


def _halves_start(name, units):
    n = len(units)

    def body(*refs):
        g_refs, z_refs = refs[:n], refs[n:2 * n]
        send_sems, recv_sems = refs[2 * n], refs[2 * n + 1]
        token = refs[-1]
        x, y, c = _mesh_pos()
        for i in range(n):
            half = units[i].shape[1] // 2
            src = g_refs[i].at[pl.ds(0, N_CHIPS), pl.ds((1 - c) * half, half)]
            pltpu.make_async_remote_copy(src_ref=src, dst_ref=z_refs[i], send_sem=send_sems.at[i], recv_sem=recv_sems.at[i],
                                         device_id=(x, y, 1 - c), device_id_type=MESH).start()
        token[...] = jnp.zeros(token.shape, F32)

    zones = [lax.empty((u.shape[0], u.shape[1] // 2, u.shape[2]), u.dtype) for u in units]
    hbm = lambda a: pltpu.HBM(a.shape, a.dtype)
    res = pl.pallas_call(
        body, name=name,
        out_shape=(pltpu.SemaphoreType.DMA((n,)), pltpu.SemaphoreType.DMA((n,)),
                   *[hbm(a) for a in units], *[hbm(a) for a in zones], jax.ShapeDtypeStruct((8, LANES), F32)),
        in_specs=[_HBM] * (2 * n),
        out_specs=(_SEM, _SEM, *[_HBM] * (2 * n), pl.BlockSpec(memory_space=pltpu.VMEM)),
        input_output_aliases={i: 2 + i for i in range(2 * n)},
        compiler_params=pltpu.CompilerParams(has_side_effects=_DATAFLOW),
    )(*[pltpu.with_memory_space_constraint(a, pltpu.HBM) for a in list(units) + zones])
    return res[0], res[1], res[2:2 + n], res[2 + n:2 + 2 * n], res[-1]


def _halves_wait(name, send_sems, recv_sems, units, zones, after):
    n = len(units)

    def body(*refs):
        g_refs, z_refs = refs[:n], refs[n:2 * n]
        s_sems, r_sems = refs[2 * n], refs[2 * n + 1]
        x, y, c = _mesh_pos()
        for i in range(n):
            half = units[i].shape[1] // 2
            src = g_refs[i].at[pl.ds(0, N_CHIPS), pl.ds((1 - c) * half, half)]
            copy = pltpu.make_async_remote_copy(src_ref=src, dst_ref=z_refs[i], send_sem=s_sems.at[i], recv_sem=r_sems.at[i],
                                                device_id=(x, y, 1 - c), device_id_type=MESH)
            copy.wait_send()
            copy.wait_recv()

    hbm = lambda a: pltpu.HBM(a.shape, a.dtype)
    res = pl.pallas_call(
        body, name=name,
        out_shape=(*[hbm(a) for a in units], *[hbm(a) for a in zones]),
        in_specs=[_HBM] * (2 * n) + [_SEM, _SEM, _ANY],
        out_specs=tuple([_HBM] * (2 * n)),
        input_output_aliases={i: i for i in range(2 * n)},
        compiler_params=pltpu.CompilerParams(has_side_effects=_DATAFLOW),
    )(*units, *zones, send_sems, recv_sems, after)
    return res[:n], res[n:]


def _scatter_start(name, sums, zones):
    n = len(sums)

    def body(*refs):
        h_refs, z_refs = refs[:n], refs[n:2 * n]
        send_sems, recv_sems = refs[2 * n], refs[2 * n + 1]
        token = refs[-1]
        x, y, c = _mesh_pos()
        me = 2 * x + y
        for i in range(n):
            for j, (cx, cy) in enumerate(_other_chips(x, y)):
                pltpu.make_async_remote_copy(src_ref=h_refs[i].at[2 * cx + cy], dst_ref=z_refs[i].at[me],
                                             send_sem=send_sems.at[3 * i + j], recv_sem=recv_sems.at[3 * i + j],
                                             device_id=(cx, cy, c), device_id_type=MESH).start()
        token[...] = jnp.zeros(token.shape, F32)

    hbm = lambda a: pltpu.HBM(a.shape, a.dtype)
    res = pl.pallas_call(
        body, name=name,
        out_shape=(pltpu.SemaphoreType.DMA((3 * n,)), pltpu.SemaphoreType.DMA((3 * n,)),
                   *[hbm(a) for a in sums], *[hbm(a) for a in zones], jax.ShapeDtypeStruct((8, LANES), F32)),
        in_specs=[_HBM] * (2 * n),
        out_specs=(_SEM, _SEM, *[_HBM] * (2 * n), pl.BlockSpec(memory_space=pltpu.VMEM)),
        input_output_aliases={i: 2 + i for i in range(2 * n)},
        compiler_params=pltpu.CompilerParams(has_side_effects=_DATAFLOW),
    )(*[pltpu.with_memory_space_constraint(a, pltpu.HBM) for a in list(sums) + list(zones)])
    return res[0], res[1], res[2:2 + n], res[2 + n:2 + 2 * n], res[-1]


def _scatter_wait(name, send_sems, recv_sems, sums, zones, after):
    n = len(sums)

    def body(*refs):
        h_refs, z_refs = refs[:n], refs[n:2 * n]
        s_sems, r_sems = refs[2 * n], refs[2 * n + 1]
        x, y, c = _mesh_pos()
        me = 2 * x + y
        for i in range(n):
            for j, (cx, cy) in enumerate(_other_chips(x, y)):
                idx = 2 * cx + cy
                copy = pltpu.make_async_remote_copy(src_ref=h_refs[i].at[idx], dst_ref=z_refs[i].at[idx],
                                                    send_sem=s_sems.at[3 * i + j], recv_sem=r_sems.at[3 * i + j],
                                                    device_id=(cx, cy, c), device_id_type=MESH)
                copy.wait_send()
                copy.wait_recv()

    hbm = lambda a: pltpu.HBM(a.shape, a.dtype)
    res = pl.pallas_call(
        body, name=name,
        out_shape=(*[hbm(a) for a in sums], *[hbm(a) for a in zones]),
        in_specs=[_HBM] * (2 * n) + [_SEM, _SEM, _ANY],
        out_specs=tuple([_HBM] * (2 * n)),
        input_output_aliases={i: i for i in range(2 * n)},
        compiler_params=pltpu.CompilerParams(has_side_effects=_DATAFLOW),
    )(*sums, *zones, send_sems, recv_sems, after)
    return res[n:]


def _join_halves(name, results):
    n = len(results)
    pieces = [(i, l) for i in range(n) for l in range(results[i].shape[0])]

    def body(*refs):
        out_refs = refs[n:2 * n]
        send_sems, recv_sems = refs[2 * n:]
        x, y, c = _mesh_pos()

        def copy(k, half):
            i, l = pieces[k]
            return pltpu.make_async_remote_copy(src_ref=out_refs[i].at[l, half], dst_ref=out_refs[i].at[l, half],
                                                send_sem=send_sems.at[k], recv_sem=recv_sems.at[k],
                                                device_id=(x, y, 1 - c), device_id_type=MESH)

        cps = [copy(k, c) for k in range(len(pieces))]
        for cp in cps:
            cp.start()
        for k in range(len(pieces)):
            copy(k, 1 - c).wait_recv()
        for cp in cps:
            cp.wait_send()

    return pl.pallas_call(
        body, name=name, in_specs=[_ANY] * n, out_specs=[_ANY] * n,
        out_shape=[jax.ShapeDtypeStruct(r.shape, r.dtype) for r in results],
        input_output_aliases={i: i for i in range(n)},
        scratch_shapes=[pltpu.SemaphoreType.DMA((len(pieces),)), pltpu.SemaphoreType.DMA((len(pieces),))],
        compiler_params=pltpu.CompilerParams(has_side_effects=True),
    )(*results)


def _pack(arrays, dtype, rows_multiple):
    flat = jnp.concatenate([a.reshape(-1).astype(dtype) for a in arrays])
    unit = rows_multiple * PACK_W
    total = -(-flat.shape[0] // unit) * unit
    return jnp.pad(flat, (0, total - flat.shape[0])).reshape(total // PACK_W, PACK_W)


def _unpack(flat, shapes):
    out, off = [], 0
    for s in shapes:
        n = 1
        for d in s:
            n *= d
        out.append(flat[..., off:off + n].reshape(flat.shape[:-1] + tuple(s)))
        off += n
    return out


def _ffn_fwd(tag, l, h, g, w_up, conv, bias, w_down, tm):
    hn = _rms_fwd(f"{tag}_norm", h, g, tm)
    u = _mm_cs(f"{tag}_up", hn, w_up, l, tm, out_dtype=FFN_HIDDEN_DTYPE)
    act = _ffn_col_fwd(f"{tag}_glu", u, conv, bias)
    h_out = _mm_full(f"{tag}_down", act, w_down, l, tm, D_FF // 2, add=h)
    return h_out, (hn, u, act)


def _ffn_bwd(tag, l, h, g, w_up, conv, bias, w_down, saved, dh, tm):
    hn, u, act = saved
    da = _mm_nt_full(f"{tag}_down_dx", dh, w_down, l, tm, D_FF // 2)
    dw_down = _mm_tn_full(f"{tag}_down_dw", act, dh, tm, D_FF // 2)
    du, dconv, dbias = _ffn_col_bwd(f"{tag}_glu_bwd", u, da, conv, bias)
    dw_up = _mm_tn_cs(f"{tag}_up_dw", hn, du, N_CHIPS, tm)
    dhn = _mm_nt_cs(f"{tag}_up_dx", du, w_up, l, tm)
    dh, dg = _rms_bwd(f"{tag}_norm_bwd", h, g, dhn, dh, tm)
    return dh, dict(norm=dg, w_up=dw_up, conv=dconv, bias=dbias, w_down=dw_down)


def _to_heads(z, nh, pad):
    t = z.shape[0]
    return jnp.pad(z.reshape(t, nh, HEAD_DIM).transpose(1, 0, 2), ((0, 0), (pad, 0), (0, 0)))


def _from_heads(z, pad):
    nh, tp, _ = z.shape
    return z[:, pad:].transpose(1, 0, 2).reshape(tp - pad, nh * HEAD_DIM)


def _rope_tables(tp, pad):
    half = HEAD_DIM // 2
    inv = ROPE_THETA ** (-jnp.arange(half, dtype=F32) / half)
    ang = (jnp.arange(tp, dtype=F32) - pad)[:, None] * inv[None, :]
    cos, sin = jnp.cos(ang), jnp.sin(ang)
    rot = jnp.zeros((HEAD_DIM, HEAD_DIM), F32)
    idx = jnp.arange(half)
    rot = rot.at[idx + half, idx].set(-1.0).at[idx, idx + half].set(1.0)
    return jnp.concatenate([cos, cos], axis=1), jnp.concatenate([sin, sin], axis=1), rot


def _local_step(x, tgt, w, on_grads=None, fetch=None):
    emit = on_grads if on_grads is not None else (lambda tag, units: 0.0)
    need = (lambda tag, after: w) if fetch is None else (lambda tag, after: {**w, **fetch(tag, after)})
    seq = x.shape[0]
    t = seq + N_META
    tm = _row_tile(t, ROW_TILE_CAP)
    tr = _row_tile(t, ROW_TILE_CAP // 2)
    pad = BLOCK - N_META
    grads = {}

    h0 = jnp.concatenate([w["meta_tokens"], x], axis=0)
    tgt_p = jnp.pad(tgt, ((N_META, 0), (0, 0)))

    hn0 = _rms_fwd("l0_norm", h0, w["norm_mix"][0:1], tm)
    p0 = _mm_cs("l0_in", hn0, w["ev_w_in"], 0, tm)
    uc, yb = _even_col_fwd("l0_convs", p0, w["ev_conv_a"], w["ev_conv_b"])
    ya = _even_ln_fwd("l0_ln", uc, w["ev_ln_a_g"], w["ev_ln_a_b"], tm)
    y0 = jnp.concatenate([ya, yb], axis=1)
    w = need("ev_out", y0)
    h1 = _mm_full("l0_out", y0, w["ev_w_out"], 0, tm, D_MODEL, add=h0)
    w = need("f0", h1)
    f0 = (0, h1, w["norm_ffn"][0:1], w["ff_w_up0"], w["ff_conv"][0], w["ff_conv_b"][0:1], w["ff_w_down0"])
    h2, ffn0 = _ffn_fwd("f0", *f0, tm)
    w = need("od", h2)

    hn2 = _rms_fwd("l1_norm", h2, w["norm_mix"][1:2], tm)
    p1 = _mm_cs("l1_in", hn2, w["od_w_in"], 0, tm)
    cos, sin, rot = _rope_tables(t + pad, pad)
    qh = _to_heads(p1[:, :D_ATT], N_Q_HEADS, pad)
    kh = _to_heads(p1[:, D_ATT:D_ATT + D_KV], N_KV_HEADS, pad)
    vh = _to_heads(p1[:, D_ATT + D_KV:D_ATT + 2 * D_KV], N_KV_HEADS, pad)
    sinks_b = jnp.broadcast_to(w["od_sinks"].reshape(N_Q_HEADS, 1, 1), (N_Q_HEADS, 8, LANES))
    y_att = _from_heads(_attn_fwd("l1_attn", qh, kh, vh, sinks_b, cos, sin, rot), pad)

    col0 = D_ATT + 2 * D_KV
    ch = jnp.arange(D_R) // HEAD_DIM
    seg = (ch[:, None] == ch[None, :]).astype(F32)
    prm = dict(w0=w["od_w0"], a0=w["od_a0"], g2=w["od_g2"], k_k=w["od_k_k"], k_a=w["od_k_a"],
               lnx_g=w["od_lnx_g"], lnx_b=w["od_lnx_b"], r_k=w["od_r_k"].reshape(1, D_R),
               w2p=jnp.concatenate([w["od_w2"], jnp.zeros((LORA_A, D_R), F32)], axis=0),
               a2p=jnp.concatenate([jnp.zeros((LORA_W, D_R), F32), w["od_a2"]], axis=0))
    prs = _shift_fwd("l1_shift", p1, col0, w["od_mu"])
    lw, k2, a_, b_, gate_r = _rwkv_pre_fwd("l1_rwkv_pre", prs, prm, seg, tr)
    v_off = 2 * D_R // (WKV_PAIRS_PER_STEP * PAIR)
    scan_in = [(prs, 0), (lw, 0), (k2, 0), (prs, v_off), (a_, 0), (b_, 0)]
    y_scan, states = _wkv_fwd("l1_wkv", scan_in)
    y_rwkv = _rwkv_post_fwd("l1_rwkv_post", y_scan, prs, k2, gate_r, prm, seg, tr)
    y1 = jnp.concatenate([y_att, y_rwkv], axis=1).astype(MXU_DTYPE)
    h3 = _mm_full("l1_out", y1, w["od_w_out"], 0, tm, D_MODEL, add=h2)
    w = need("f1", h3)
    f1 = (0, h3, w["norm_ffn"][1:2], w["ff_w_up1"], w["ff_conv"][1], w["ff_conv_b"][1:2], w["ff_w_down1"])
    h4, ffn1 = _ffn_fwd("f1", *f1, tm)

    loss_blk, dh, d_norm_final = _final_loss("final", h4, w["norm_final"], tgt_p, tm)
    grads["norm_final"] = d_norm_final

    dh, gf1 = _ffn_bwd("f1", *f1, ffn1, dh, tm)
    zero = emit("f1", {"ff_w_down1": gf1["w_down"].reshape(N_CHIPS, D_FF // N_CHIPS, D_MODEL), "ff_w_up1": gf1["w_up"]})
    prm = dict(prm, lnx_g=prm["lnx_g"] + zero)
    dy1 = _mm_nt_full("l1_out_dx", dh, w["od_w_out"], 0, tm, D_MODEL)
    grads["od_w_out"] = _mm_tn_full("l1_out_dw", y1, dh, tm, D_MODEL // 2)
    dy_scan, dr_p, dk2_p, dv_p, dgate_r, grads["od_lnx_g"], grads["od_lnx_b"], d_rk = _rwkv_post_bwd(
        "l1_rwkv_post_bwd", y_scan, prs, k2, gate_r, prm, seg, dy1, 1, tr)
    grads["od_r_k"] = d_rk.reshape(N_R_HEADS, HEAD_DIM)
    dr_s, dlw, dk2_s, dv_s, da_, db_ = _wkv_bwd("l1_wkv_bwd", scan_in, states, (dy_scan, 0))
    dk, dxl, dgd, grads["od_w0"], dw2p, grads["od_a0"], da2p, grads["od_g2"], grads["od_k_k"], grads["od_k_a"] = (
        _rwkv_pre_bwd("l1_rwkv_pre_bwd", prs, prm, seg, (dlw, dk2_s + dk2_p, da_, db_, dgate_r), tr))
    grads["od_w2"] = dw2p[:LORA_W]
    grads["od_a2"] = da2p[LORA_W:]
    dprs = jnp.concatenate([dr_s + dr_p, dk, dv_s + dv_p, dxl, dgd], axis=1)
    dpr, grads["od_mu"] = _shift_bwd("l1_shift_bwd", p1, col0, w["od_mu"], dprs)
    doh = _to_heads(dy1[:, :D_ATT], N_Q_HEADS, pad)
    dqh, dkp, dkc, dvp, dvc, dkm, dvm, dsinks = _attn_bwd("l1_attn_bwd", qh, kh, vh, sinks_b, cos, sin, rot, doh)
    grads["od_sinks"] = dsinks[:, 0, 0].reshape(1, N_Q_HEADS)
    dkh = _kv_combine("l1_attn_dk", dkp, dkc, dkm)
    dvh = _kv_combine("l1_attn_dv", dvp, dvc, dvm)
    dp1 = jnp.concatenate([_from_heads(dqh, pad), _from_heads(dkh, pad), _from_heads(dvh, pad), dpr], axis=1).astype(MXU_DTYPE)
    grads["od_w_in"] = _mm_tn_cs("l1_in_dw", hn2, dp1, N_CHIPS, tm)
    dhn2 = _mm_nt_cs("l1_in_dx", dp1, w["od_w_in"], 0, tm)
    dh, d_mix1 = _rms_bwd("l1_norm_bwd", h2, w["norm_mix"][1:2], dhn2, dh, tm)

    zero = emit("od", {"od_w_out": grads["od_w_out"].reshape(N_CHIPS, D_MODEL // N_CHIPS, D_MODEL), "od_w_in": grads["od_w_in"]})
    f0 = f0[:5] + (f0[5] + zero,) + f0[6:]
    dh, gf0 = _ffn_bwd("f0", *f0, ffn0, dh, tm)
    zero = emit("f0", {"ff_w_down0": gf0["w_down"].reshape(N_CHIPS, D_FF // N_CHIPS, D_MODEL), "ff_w_up0": gf0["w_up"]})
    w = dict(w, ev_ln_a_g=w["ev_ln_a_g"] + zero)
    dy0 = _mm_nt_full("l0_out_dx", dh, w["ev_w_out"], 0, tm, D_MODEL)
    grads["ev_w_out"] = _mm_tn_full("l0_out_dw", y0, dh, tm, D_MODEL // 2)
    duc, grads["ev_ln_a_g"], grads["ev_ln_a_b"] = _even_ln_bwd("l0_ln_bwd", uc, w["ev_ln_a_g"], w["ev_ln_a_b"], dy0, 0, tm)
    *dparts, grads["ev_conv_a"], grads["ev_conv_b"] = _even_col_bwd("l0_convs_bwd", p0, duc, dy0, w["ev_conv_a"], w["ev_conv_b"])
    dp0 = jnp.concatenate(dparts, axis=1)
    grads["ev_w_in"] = _mm_tn_cs("l0_in_dw", hn0, dp0, N_CHIPS, tm)
    dhn0 = _mm_nt_cs("l0_in_dx", dp0, w["ev_w_in"], 0, tm)
    dh, d_mix0 = _rms_bwd("l0_norm_bwd", h0, w["norm_mix"][0:1], dhn0, dh, tm)

    grads["norm_mix"] = jnp.concatenate([d_mix0, d_mix1], axis=0)
    grads["norm_ffn"] = jnp.concatenate([gf0["norm"], gf1["norm"]], axis=0)
    grads["ff_w_up"] = [gf0["w_up"], gf1["w_up"]]
    grads["ff_conv"] = jnp.stack([gf0["conv"], gf1["conv"]])
    grads["ff_conv_b"] = jnp.concatenate([gf0["bias"], gf1["bias"]], axis=0)
    grads["ff_w_down"] = [gf0["w_down"], gf1["w_down"]]
    grads["meta_tokens"] = dh[:N_META]
    return loss_blk[0, 0], dh[N_META:], grads


SHARD_AXIS = {
    "meta_tokens": 1, "norm_mix": None, "norm_ffn": None, "norm_final": None,
    "ev_w_in": 2, "ev_conv_a": 2, "ev_ln_a_g": None, "ev_ln_a_b": None, "ev_conv_b": 2, "ev_w_out": 1,
    "od_w_in": 2, "od_sinks": None, "od_mu": 1, "od_w0": 1, "od_w2": 2, "od_a0": 1, "od_a2": 2, "od_g2": 2,
    "od_k_k": 1, "od_k_a": 1, "od_r_k": None, "od_lnx_g": 1, "od_lnx_b": 1, "od_w_out": 1,
    "ff_w_up": 2, "ff_conv": 2, "ff_conv_b": None, "ff_w_down": 1,
}
WEIGHTS = list(SHARD_AXIS)
BIG = ("ev_w_in", "ev_w_out", "od_w_in", "od_w_out", "ff_w_up", "ff_w_down")
SHARDED = [n for n in WEIGHTS if SHARD_AXIS[n] is not None]
SMALL = [n for n in SHARDED if n not in BIG]
REPLICATED = [n for n in WEIGHTS if SHARD_AXIS[n] is None]


def _join(g, axis):
    return jnp.concatenate([g[k] for k in range(N_CHIPS)], axis=axis)


def _split(full, axis):
    return jnp.stack(jnp.split(full, N_CHIPS, axis=axis))


def _full_weights(gathered, repl):
    w = {}
    sq = lambda a: a.reshape(a.shape[1:]) if a.shape[0] == 1 else a
    for n in REPLICATED:
        w[n] = repl[n]
    w["norm_final"] = repl["norm_final"].reshape(1, D_MODEL)
    for n in ("ev_ln_a_g", "ev_ln_a_b"):
        w[n] = repl[n].reshape(1, D_A)
    w["od_r_k"] = repl["od_r_k"][0]
    w["meta_tokens"] = _join(gathered["meta_tokens"], 1)
    for n in ("ev_conv_a", "ev_conv_b", "od_w2", "od_a2", "od_g2"):
        w[n] = sq(_join(gathered[n], 2))
    for n in ("od_mu", "od_w0", "od_a0", "od_k_k", "od_k_a", "od_lnx_g", "od_lnx_b"):
        w[n] = _join(gathered[n], 1)
    w["ff_conv"] = _join(gathered["ff_conv"], 2)
    return w


def _shard_grads(grads):
    out = {}
    for n in REPLICATED:
        out[n] = grads[n]
    out["norm_final"] = grads["norm_final"].reshape(D_MODEL)
    out["od_r_k"] = grads["od_r_k"][None]
    out["meta_tokens"] = _split(grads["meta_tokens"], 1)
    for n in ("ev_conv_a", "ev_conv_b", "od_w2", "od_a2", "od_g2"):
        out[n] = _split(grads[n][None], 2)
    for n in ("od_mu", "od_w0", "od_a0", "od_k_k", "od_k_a", "od_lnx_g", "od_lnx_b"):
        out[n] = _split(grads[n], 1)
    out["ff_conv"] = _split(grads["ff_conv"], 2)
    return out


def kernel(x, meta_tokens, norm_mix, norm_ffn, norm_final, ev_w_in, ev_conv_a, ev_ln_a_g, ev_ln_a_b, ev_conv_b, ev_w_out, od_w_in, od_sinks, od_mu, od_w0, od_w2, od_a0, od_a2, od_g2, od_k_k, od_k_a, od_r_k, od_lnx_g, od_lnx_b, od_w_out, ff_w_up, ff_conv, ff_conv_b, ff_w_down, loss_target, m_meta_tokens, m_norm_mix, m_norm_ffn, m_norm_final, m_ev_w_in, m_ev_conv_a, m_ev_ln_a_g, m_ev_ln_a_b, m_ev_conv_b, m_ev_w_out, m_od_w_in, m_od_sinks, m_od_mu, m_od_w0, m_od_w2, m_od_a0, m_od_a2, m_od_g2, m_od_k_k, m_od_k_a, m_od_r_k, m_od_lnx_g, m_od_lnx_b, m_od_w_out, m_ff_w_up, m_ff_conv, m_ff_conv_b, m_ff_w_down, v_meta_tokens, v_norm_mix, v_norm_ffn, v_norm_final, v_ev_w_in, v_ev_conv_a, v_ev_ln_a_g, v_ev_ln_a_b, v_ev_conv_b, v_ev_w_out, v_od_w_in, v_od_sinks, v_od_mu, v_od_w0, v_od_w2, v_od_a0, v_od_a2, v_od_g2, v_od_k_k, v_od_k_a, v_od_r_k, v_od_lnx_g, v_od_lnx_b, v_od_w_out, v_ff_w_up, v_ff_conv, v_ff_conv_b, v_ff_w_down):
    wts = dict(meta_tokens=meta_tokens, norm_mix=norm_mix, norm_ffn=norm_ffn, norm_final=norm_final, ev_w_in=ev_w_in, ev_conv_a=ev_conv_a, ev_ln_a_g=ev_ln_a_g, ev_ln_a_b=ev_ln_a_b, ev_conv_b=ev_conv_b, ev_w_out=ev_w_out, od_w_in=od_w_in, od_sinks=od_sinks, od_mu=od_mu, od_w0=od_w0, od_w2=od_w2, od_a0=od_a0, od_a2=od_a2, od_g2=od_g2, od_k_k=od_k_k, od_k_a=od_k_a, od_r_k=od_r_k, od_lnx_g=od_lnx_g, od_lnx_b=od_lnx_b, od_w_out=od_w_out, ff_w_up=ff_w_up, ff_conv=ff_conv, ff_conv_b=ff_conv_b, ff_w_down=ff_w_down)
    mom = dict(meta_tokens=m_meta_tokens, norm_mix=m_norm_mix, norm_ffn=m_norm_ffn, norm_final=m_norm_final, ev_w_in=m_ev_w_in, ev_conv_a=m_ev_conv_a, ev_ln_a_g=m_ev_ln_a_g, ev_ln_a_b=m_ev_ln_a_b, ev_conv_b=m_ev_conv_b, ev_w_out=m_ev_w_out, od_w_in=m_od_w_in, od_sinks=m_od_sinks, od_mu=m_od_mu, od_w0=m_od_w0, od_w2=m_od_w2, od_a0=m_od_a0, od_a2=m_od_a2, od_g2=m_od_g2, od_k_k=m_od_k_k, od_k_a=m_od_k_a, od_r_k=m_od_r_k, od_lnx_g=m_od_lnx_g, od_lnx_b=m_od_lnx_b, od_w_out=m_od_w_out, ff_w_up=m_ff_w_up, ff_conv=m_ff_conv, ff_conv_b=m_ff_conv_b, ff_w_down=m_ff_w_down)
    var = dict(meta_tokens=v_meta_tokens, norm_mix=v_norm_mix, norm_ffn=v_norm_ffn, norm_final=v_norm_final, ev_w_in=v_ev_w_in, ev_conv_a=v_ev_conv_a, ev_ln_a_g=v_ev_ln_a_g, ev_ln_a_b=v_ev_ln_a_b, ev_conv_b=v_ev_conv_b, ev_w_out=v_ev_w_out, od_w_in=v_od_w_in, od_sinks=v_od_sinks, od_mu=v_od_mu, od_w0=v_od_w0, od_w2=v_od_w2, od_a0=v_od_a0, od_a2=v_od_a2, od_g2=v_od_g2, od_k_k=v_od_k_k, od_k_a=v_od_k_a, od_r_k=v_od_r_k, od_lnx_g=v_od_lnx_g, od_lnx_b=v_od_lnx_b, od_w_out=v_od_w_out, ff_w_up=v_ff_w_up, ff_conv=v_ff_conv, ff_conv_b=v_ff_conv_b, ff_w_down=v_ff_w_down)

    me_idx = (2 * lax.axis_index("x") + lax.axis_index("y")).astype(jnp.int32).reshape(1)
    c_idx = lax.axis_index("c").astype(jnp.int32).reshape(1)
    small_mine = _pack([wts[n] for n in SMALL], F32, 2 * 8)
    sources = {"ev_w_in": (ev_w_in, 0), "small": (small_mine[None], 0), "ev_w_out": (ev_w_out, 0),
               "ff_w_up0": (ff_w_up, 0), "ff_w_down0": (ff_w_down, 0), "od_w_in": (od_w_in, 0), "od_w_out": (od_w_out, 0),
               "ff_w_up1": (ff_w_up, 1), "ff_w_down1": (ff_w_down, 1)}
    bufs = {n: _place_own_block("place_" + n, a, l, me_idx, F32 if n == "small" else MXU_DTYPE)
            for n, (a, l) in sources.items()}

    def as_used(n, g):
        if n in ("ev_w_out", "od_w_out", "ff_w_down0", "ff_w_down1"):
            return g.reshape(1, -1, g.shape[-1])
        return g.reshape(N_CHIPS, 1, -1, g.shape[-1])

    first = dict(zip(("ev_w_in", "small"), _gather_weights("gather_first", [bufs["ev_w_in"], bufs["small"]])))
    gathered = dict(zip(SMALL, _unpack(first["small"].reshape(N_CHIPS, -1), [wts[n].shape for n in SMALL])))
    w_full = _full_weights(gathered, wts)
    w_full["ev_w_in"] = as_used("ev_w_in", first["ev_w_in"])
    groups = {"ev_out": ["ev_w_out"], "f0": ["ff_w_up0", "ff_w_down0"], "od": ["od_w_in", "od_w_out"],
              "f1": ["ff_w_up1", "ff_w_down1"]}
    started_gathers, token = _gather_start("gather_start", [[bufs[n] for n in g] for g in groups.values()])
    started_gathers = dict(zip(groups, started_gathers))
    w_full["norm_mix"] = w_full["norm_mix"] + token[0, 0]

    def fetch(tag, after):
        send_sems, recv_sems, group_bufs = started_gathers[tag]
        landed = _gather_wait("gather_wait_" + tag, send_sems, recv_sems, group_bufs, after)
        whole = _gather_weights("gather_siblings_" + tag, landed, from_chips=False)
        return {n: as_used(n, g) for n, g in zip(groups[tag], whole)}

    cm_idx = jnp.concatenate([c_idx, me_idx])
    started = []
    to_sibling = []

    def to_chips(tag, names, units, from_sibling):
        pairs = [_pair_add_placed(f"grads_pair_add_{n}", u, r, cm_idx, GRAD_WIRE_DTYPE)
                 for n, u, r in zip(names, units, from_sibling)]
        send_sems, recv_sems, sums, zones, token = _scatter_start(
            f"grads_to_chips_start_{tag}", [p[0] for p in pairs], [p[1] for p in pairs])
        started.append((tag, names, send_sems, recv_sems, sums, zones))
        return token[0, 0]

    def start_reduction(tag, units):
        names = list(units)
        arrays = [units[n] for n in names]
        zero = 0.0
        if to_sibling:
            before, bnames, send_sems, recv_sems, thru, zones = to_sibling.pop()
            thru, got = _halves_wait(f"grads_to_sibling_wait_{before}", send_sems, recv_sems, thru, zones, arrays[-1])
            zero = zero + to_chips(before, bnames, thru, got)
        if tag == "f0":
            return zero + to_chips(tag, names, arrays, _halves_to_sibling(f"grads_to_sibling_{tag}", arrays))
        send_sems, recv_sems, thru, zones, token = _halves_start(f"grads_to_sibling_start_{tag}", arrays)
        to_sibling.append((tag, names, send_sems, recv_sems, thru, zones))
        return zero + token[0, 0]

    loss_local, grad_x, grads = _local_step(x[0], loss_target[0], w_full, start_reduction, fetch)
    loss = lax.psum(loss_local, ("x", "y", "c"))

    sg = _shard_grads(grads)
    small_rows = [jnp.concatenate([sg[n][k].reshape(-1) for n in SMALL] + [sg[n].reshape(-1) for n in REPLICATED])
                  for k in range(N_CHIPS)]
    n_el = small_rows[0].shape[0]
    n_rows = -(-n_el // (16 * PACK_W)) * 16
    small_unit = jnp.stack([jnp.pad(r, (0, n_rows * PACK_W - n_el)).reshape(n_rows, PACK_W) for r in small_rows])
    last = {"ev_w_out": grads["ev_w_out"].reshape(N_CHIPS, D_MODEL // N_CHIPS, D_MODEL), "ev_w_in": grads["ev_w_in"],
            "small": small_unit}
    from_sibling = _halves_to_sibling("grads_to_sibling_ev", list(last.values()))
    pairs = [_pair_add_placed(f"grads_pair_add_{n}", u, r, cm_idx, F32 if n == "small" else GRAD_WIRE_DTYPE)
             for (n, u), r in zip(last.items(), from_sibling)]
    ev_send, ev_recv, ev_sums, ev_zones, token = _scatter_start(
        "grads_to_chips_start_ev", [p[0] for p in pairs], [p[1] for p in pairs])
    dests = {"ev_w_in": ("ev_w_in", 0), "od_w_in": ("od_w_in", 0), "ev_w_out": ("ev_w_out", 0), "od_w_out": ("od_w_out", 0),
             "ff_w_up0": ("ff_w_up", 0), "ff_w_up1": ("ff_w_up", 1), "ff_w_down0": ("ff_w_down", 0),
             "ff_w_down1": ("ff_w_down", 1), "small": ("small", 0)}
    outs = {"grad": {}, "delta": {}, "new_m": {}, "new_v": {}}

    def finish(tag, from_chips, results):
        reduced = {}
        for n, part in from_chips.items():
            r, l = dests[n]
            reduced[r] = _sum_chips(f"grads_chip_sum_{n}", part, c_idx, l, 2 if r.startswith("ff_w") else 1,
                                    into=reduced.get(r))
        joined = dict(zip(results, _join_halves("grads_join_" + tag, [reduced[r] for r in results])))
        for n, g in joined.items():
            if n == "small":
                continue
            shape = wts[n].shape
            flat = lambda a: a.reshape(-1, shape[-1])
            new = _adamw("adamw_" + n, flat(wts[n]), flat(g), flat(mom[n]), flat(var[n]))
            for kind, arr in zip(("grad", "delta", "new_m", "new_v"), (g,) + tuple(new)):
                outs[kind][n] = arr.reshape(shape)
        return joined

    from_chips = {}
    for tag, names, send_sems, recv_sems, sums, zones in started:
        from_chips.update(zip(names, _scatter_wait(f"grads_to_chips_wait_{tag}", send_sems, recv_sems, sums, zones, token)))
    finish("layers", from_chips, ["od_w_in", "od_w_out", "ff_w_up", "ff_w_down"])
    from_chips = dict(zip(last, _scatter_wait("grads_to_chips_wait_ev", ev_send, ev_recv, ev_sums, ev_zones,
                                              outs["delta"]["ff_w_up"])))
    joined = finish("ev", from_chips, ["ev_w_in", "ev_w_out", "small"])

    order = SMALL + REPLICATED
    packed = lambda d: jnp.pad(jnp.concatenate([d[n].reshape(-1) for n in order]),
                               (0, n_rows * PACK_W - n_el)).reshape(n_rows, PACK_W)
    g_small = joined["small"].reshape(n_rows, PACK_W)
    new = _adamw("adamw_small", packed(wts), g_small, packed(mom), packed(var))
    for tag, arr in zip(("grad", "delta", "new_m", "new_v"), (g_small,) + tuple(new)):
        outs[tag].update(zip(order, _unpack(arr.reshape(-1), [wts[n].shape for n in order])))
    return (loss, grad_x[None], *[outs["grad"][n] for n in WEIGHTS], *[outs["delta"][n] for n in WEIGHTS],
            *[outs["new_m"][n] for n in WEIGHTS], *[outs["new_v"][n] for n in WEIGHTS])
```

```python
import functools

import jax
import jax.numpy as jnp
from jax import lax
from jax.experimental import pallas as pl
from jax.experimental.pallas import tpu as pltpu

F32 = jnp.float32
BF16 = jnp.bfloat16
MXU_DTYPE = BF16
GRAD_WIRE_DTYPE = BF16
FFN_HIDDEN_DTYPE = BF16

D_MODEL = 1024
N_META = 16
RMS_EPS = 1e-6
LN_EPS = 1e-5
D_A = 512
CONV_A_WIDTH = 31
CONV_B_WIDTH = 3
HEAD_DIM = 64
N_Q_HEADS = 8
N_KV_HEADS = 2
GQA_GROUP = 4
D_ATT = 512
D_KV = 128
BLOCK = 128
ROPE_THETA = 10000.0
D_R = 512
N_R_HEADS = 8
LORA_W = 64
LORA_A = 64
LORA_G = 128
RWKV_GN_EPS = 64e-5
RWKV_COLS = 3 * D_R + LORA_W + LORA_A + LORA_G
D_FF = 2816
NEG_INF = -1e30
ADAM_LR = 0.001
ADAM_B1 = 0.9
ADAM_B2 = 0.999
ADAM_EPS = 1e-08
ADAM_WD = 0.01
ADAM_STEP = 10

N_CHIPS = 4
LANES = 128
CONV_PAD = 32
ROW_TILE_CAP = 704
VMEM_LIMIT_V7X = 56 * 1024 * 1024
MESH = pl.DeviceIdType.MESH


def _cparams(sem=None):
    return pltpu.CompilerParams(dimension_semantics=sem, vmem_limit_bytes=VMEM_LIMIT_V7X)


def _row_tile(t, cap):
    for d in range(min(t, cap), 0, -1):
        if t % d == 0 and d % 16 == 0:
            return d
    return t


def _chunk_len(t):
    for d in (64, 48, 32, 16, 8):
        if t % d == 0:
            return d
    raise ValueError(t)


def _call(fn, name, grid, ins, outs, acc_axis=None, sem=None):
    n_in, n_out = len(ins), len(outs)
    dtype = lambda o: o[4] if len(o) > 4 else F32

    def body(*refs):
        vals = fn(*[r[...] for r in refs[:n_in]])
        if not isinstance(vals, (tuple, list)):
            vals = (vals,)
        for r, v, o in zip(refs[n_in:n_in + n_out], vals, outs):
            if o[3]:
                first = pl.program_id(acc_axis) == 0

                @pl.when(first)
                def _(r=r, v=v):
                    r[...] = v

                @pl.when(jnp.logical_not(first))
                def _(r=r, v=v):
                    r[...] += v
            else:
                r[...] = v.astype(dtype(o))

    res = pl.pallas_call(
        body, name=name, grid=grid,
        in_specs=[pl.BlockSpec(b, m) for _, b, m in ins],
        out_specs=[pl.BlockSpec(o[1], o[2]) for o in outs],
        out_shape=[jax.ShapeDtypeStruct(o[0], dtype(o)) for o in outs],
        compiler_params=_cparams(sem),
    )(*[a for a, _, _ in ins])
    return res if n_out > 1 else res[0]


def _matmul(name, a, b, *, dims, grid, a_spec, b_spec, o_shape, o_spec, acc_shape, nk, k_axis,
            add=None, add_spec=None, out_dtype=F32):
    def product(a_ref, b_ref):
        return lax.dot_general(a_ref[...].astype(MXU_DTYPE), b_ref[...].astype(MXU_DTYPE), dims, preferred_element_type=F32)

    def body_single(*refs):
        a_ref, b_ref, o_ref = refs[0], refs[1], refs[-1]
        res = product(a_ref, b_ref) if add is None else product(a_ref, b_ref) + refs[2][...]
        o_ref[...] = res.astype(out_dtype)

    def body_steps(*refs):
        a_ref, b_ref, o_ref, acc = refs[0], refs[1], refs[-2], refs[-1]
        k = pl.program_id(k_axis)

        @pl.when(k == 0)
        def _():
            if add is None:
                acc[...] = jnp.zeros(acc.shape, F32)
            else:
                acc[...] = refs[2][...]

        acc[...] += product(a_ref, b_ref)

        @pl.when(k == nk - 1)
        def _():
            o_ref[...] = acc[...].astype(out_dtype)

    args = [a, b] + ([] if add is None else [add])
    specs = [a_spec, b_spec] + ([] if add is None else [add_spec])
    return pl.pallas_call(
        body_single if nk == 1 else body_steps, name=name, grid=grid, in_specs=specs, out_specs=o_spec,
        out_shape=jax.ShapeDtypeStruct(o_shape, out_dtype),
        scratch_shapes=[] if nk == 1 else [pltpu.VMEM(acc_shape, F32)],
        compiler_params=_cparams(None),
    )(*args)


MATMUL_BLOCKS_BYTES = 46 * 1024 * 1024


def _whole_if_fits(t, tile, need_bytes):
    return t if need_bytes <= MATMUL_BLOCKS_BYTES else tile


_NN = (((1,), (0,)), ((), ()))
_NT = (((1,), (1,)), ((), ()))
_TN = (((0,), (0,)), ((), ()))


def _mm_cs(name, x, wg, l, tm, out_dtype=F32):
    t, k = x.shape
    s, _, _, n = wg.shape
    tm = _whole_if_fits(t, tm, 2 * (t * k * x.dtype.itemsize + k * n * wg.dtype.itemsize + t * n * 4))
    return _matmul(name, x, wg, dims=_NN, grid=(s, t // tm, 1),
                   a_spec=pl.BlockSpec((tm, k), lambda j, i, kk: (i, 0)),
                   b_spec=pl.BlockSpec((None, None, k, n), lambda j, i, kk: (j, l, 0, 0)),
                   o_shape=(t, s * n), o_spec=pl.BlockSpec((tm, n), lambda j, i, kk: (i, j)),
                   acc_shape=(tm, n), nk=1, k_axis=2, out_dtype=out_dtype)


def _mm_full(name, x, w, l, tm, tk, add=None):
    t, k = x.shape
    n = w.shape[2]
    nk = k // tk
    tm = _whole_if_fits(t, tm, 2 * (t * tk * x.dtype.itemsize + tk * n * w.dtype.itemsize + t * n * 4 * (1 if add is None else 2))
                        + (t * n * 4 if nk > 1 else 0))
    return _matmul(name, x, w, dims=_NN, grid=(t // tm, 1, nk),
                   a_spec=pl.BlockSpec((tm, tk), lambda i, j, kk: (i, kk)),
                   b_spec=pl.BlockSpec((None, tk, n), lambda i, j, kk: (l, kk, 0)),
                   o_shape=(t, n), o_spec=pl.BlockSpec((tm, n), lambda i, j, kk: (i, 0)),
                   acc_shape=(tm, n), nk=nk, k_axis=2,
                   add=add, add_spec=pl.BlockSpec((tm, n), lambda i, j, kk: (i, 0)))


def _mm_nt_cs(name, dy, wg, l, tm, add=None):
    t = dy.shape[0]
    s, _, k, n = wg.shape
    tm = _whole_if_fits(t, tm, 2 * (t * n * dy.dtype.itemsize + k * n * wg.dtype.itemsize + t * k * 4 * (1 if add is None else 2))
                        + t * k * 4)
    return _matmul(name, dy, wg, dims=_NT, grid=(t // tm, 1, s),
                   a_spec=pl.BlockSpec((tm, n), lambda i, j, kk: (i, kk)),
                   b_spec=pl.BlockSpec((None, None, k, n), lambda i, j, kk: (kk, l, 0, 0)),
                   o_shape=(t, k), o_spec=pl.BlockSpec((tm, k), lambda i, j, kk: (i, 0)),
                   acc_shape=(tm, k), nk=s, k_axis=2,
                   add=add, add_spec=pl.BlockSpec((tm, k), lambda i, j, kk: (i, 0)))


def _mm_nt_full(name, dy, w, l, tm, tko):
    t, n = dy.shape
    k = w.shape[1]
    tm = _whole_if_fits(t, tm, 2 * (t * n * dy.dtype.itemsize + tko * n * w.dtype.itemsize + t * tko * 4))
    return _matmul(name, dy, w, dims=_NT, grid=(t // tm, k // tko, 1),
                   a_spec=pl.BlockSpec((tm, n), lambda i, j, kk: (i, 0)),
                   b_spec=pl.BlockSpec((None, tko, n), lambda i, j, kk: (l, j, 0)),
                   o_shape=(t, k), o_spec=pl.BlockSpec((tm, tko), lambda i, j, kk: (i, j)),
                   acc_shape=(tm, tko), nk=1, k_axis=2)


def _mm_tn_cs(name, x, dy, s, tk):
    t, k = x.shape
    n = dy.shape[1] // s
    tk = _whole_if_fits(t, tk, 2 * (t * k * x.dtype.itemsize + t * n * dy.dtype.itemsize + k * n * 4))
    nk = t // tk
    return _matmul(name, x, dy, dims=_TN, grid=(s, 1, nk),
                   a_spec=pl.BlockSpec((tk, k), lambda j, i, kk: (kk, 0)),
                   b_spec=pl.BlockSpec((tk, n), lambda j, i, kk: (kk, j)),
                   o_shape=(s, k, n), o_spec=pl.BlockSpec((None, k, n), lambda j, i, kk: (j, 0, 0)),
                   acc_shape=(k, n), nk=nk, k_axis=2)


def _mm_tn_full(name, y, dh, tk, tko):
    t, k = y.shape
    n = dh.shape[1]
    tk = _whole_if_fits(t, tk, 2 * (t * tko * y.dtype.itemsize + t * n * dh.dtype.itemsize + tko * n * 4))
    nk = t // tk
    return _matmul(name, y, dh, dims=_TN, grid=(k // tko, 1, nk),
                   a_spec=pl.BlockSpec((tk, tko), lambda j, i, kk: (kk, j)),
                   b_spec=pl.BlockSpec((tk, n), lambda j, i, kk: (kk, 0)),
                   o_shape=(k, n), o_spec=pl.BlockSpec((tko, n), lambda j, i, kk: (j, 0)),
                   acc_shape=(tko, n), nk=nk, k_axis=2)


def _sigmoid(x):
    return 1.0 / (1.0 + jnp.exp(-x))


def _rms_fwd(name, h, g, tr):
    t, d = h.shape

    def fn(hv, gv):
        r = lax.rsqrt(jnp.mean(hv * hv, axis=-1, keepdims=True) + RMS_EPS)
        return hv * r * gv

    return _call(fn, name, (t // tr,), [(h, (tr, d), lambda i: (i, 0)), (g, (1, d), lambda i: (0, 0))],
                 [((t, d), (tr, d), lambda i: (i, 0), False, MXU_DTYPE)])


def _rms_bwd(name, h, g, dhn, dh, tr):
    t, d = h.shape

    def fn(hv, gv, dy, dh_in):
        r = lax.rsqrt(jnp.mean(hv * hv, axis=-1, keepdims=True) + RMS_EPS)
        xh = hv * r
        dg = jnp.sum(dy * xh, axis=0, keepdims=True)
        dxh = dy * gv
        dx = r * (dxh - xh * jnp.mean(dxh * xh, axis=-1, keepdims=True))
        return dh_in + dx, dg

    row = lambda i: (i, 0)
    return _call(fn, name, (t // tr,),
                 [(h, (tr, d), row), (g, (1, d), lambda i: (0, 0)), (dhn, (tr, d), row), (dh, (tr, d), row)],
                 [((t, d), (tr, d), row, False), ((1, d), (1, d), lambda i: (0, 0), True)], acc_axis=0)


def _final_loss(name, h, g, tgt, tr):
    t, d = h.shape

    def fn(hv, gv, tv):
        r = lax.rsqrt(jnp.mean(hv * hv, axis=-1, keepdims=True) + RMS_EPS)
        xh = hv * r
        row = pl.program_id(0) * tr + lax.broadcasted_iota(jnp.int32, (tr, 1), 0)
        e = jnp.where(row >= N_META, xh * gv - tv, 0.0)
        loss = jnp.broadcast_to(0.5 * jnp.sum(jnp.sum(e * e, axis=-1, keepdims=True), axis=0, keepdims=True) / d,
                                (8, LANES))
        dout = e / d
        dg = jnp.sum(dout * xh, axis=0, keepdims=True)
        dxh = dout * gv
        dx = r * (dxh - xh * jnp.mean(dxh * xh, axis=-1, keepdims=True))
        return loss, dx, dg

    row = lambda i: (i, 0)
    fix = lambda i: (0, 0)
    return _call(fn, name, (t // tr,), [(h, (tr, d), row), (g, (1, d), fix), (tgt, (tr, d), row)],
                 [((8, LANES), (8, LANES), fix, True), ((t, d), (tr, d), row, False), ((1, d), (1, d), fix, True)],
                 acc_axis=0)


def _silu_ln(uc, g, b):
    mu = jnp.mean(uc, axis=-1, keepdims=True)
    xc = uc - mu
    rs = lax.rsqrt(jnp.mean(xc * xc, axis=-1, keepdims=True) + LN_EPS)
    ln = xc * rs * g + b
    return ln * _sigmoid(ln)


def _even_ln_fwd(name, uc, g, b, tr):
    t, d = uc.shape
    row, fix = (lambda i: (i, 0)), (lambda i: (0, 0))
    return _call(_silu_ln, name, (t // tr,), [(uc, (tr, d), row), (g, (1, d), fix), (b, (1, d), fix)],
                 [((t, d), (tr, d), row, False, MXU_DTYPE)])


def _even_ln_bwd(name, uc, g, b, dy, dy_col, tr):
    t, d = uc.shape

    def fn(ucv, gv, bv, dyv):
        mu = jnp.mean(ucv, axis=-1, keepdims=True)
        xc = ucv - mu
        rs = lax.rsqrt(jnp.mean(xc * xc, axis=-1, keepdims=True) + LN_EPS)
        xh = xc * rs
        ln = xh * gv + bv
        s = _sigmoid(ln)
        dln = dyv * (s * (1.0 + ln * (1.0 - s)))
        dg = jnp.sum(dln * xh, axis=0, keepdims=True)
        db = jnp.sum(dln, axis=0, keepdims=True)
        dxh = dln * gv
        duc = rs * (dxh - jnp.mean(dxh, axis=-1, keepdims=True) - xh * jnp.mean(dxh * xh, axis=-1, keepdims=True))
        return duc, dg, db

    row, fix = (lambda i: (i, 0)), (lambda i: (0, 0))
    return _call(fn, name, (t // tr,),
                 [(uc, (tr, d), row), (g, (1, d), fix), (b, (1, d), fix), (dy, (tr, d), lambda i: (i, dy_col))],
                 [((t, d), (tr, d), row, False), ((1, d), (1, d), fix, True), ((1, d), (1, d), fix, True)], acc_axis=0)


def _windows(t):
    rc = _chunk_len(t)
    return [(r0, rc) for r0 in range(0, t, rc)]


def _taps(w_ref, width):
    return [w_ref[pl.ds(j, 1), :] for j in range(width)]


def _conv_at(xp, taps, r0, rc):
    width = len(taps)
    acc = None
    for j in range(width):
        term = xp[pl.ds(CONV_PAD - (width - 1) + j + r0, rc), :] * taps[j]
        acc = term if acc is None else acc + term
    return acc


def _conv_bwd_in_at(dyp, taps, r0, rc):
    width = len(taps)
    acc = None
    for j in range(width):
        term = dyp[pl.ds(width - 1 - j + r0, rc), :] * taps[j]
        acc = term if acc is None else acc + term
    return acc


def _fold(x):
    acc = x[0:8]
    for i in range(1, x.shape[0] // 8):
        acc = acc + x[8 * i:8 * (i + 1)]
    return acc


def _add_to(accs, vals):
    return vals if accs is None else [a + v for a, v in zip(accs, vals)]


def _conv_bwd_w_at(dy, xp, width, r0, rc):
    return [_fold(dy * xp[pl.ds(CONV_PAD - (width - 1) + j + r0, rc), :]) for j in range(width)]


def _store_taps(dw_ref, accs):
    for j, a in enumerate(accs):
        dw_ref[pl.ds(j, 1), :] = jnp.sum(a, axis=0, keepdims=True)


WIDE_COLS = 2 * LANES


def _zero_front(xp):
    xp[pl.ds(0, CONV_PAD), :] = jnp.zeros((CONV_PAD, xp.shape[1]), F32)


def _zero_back(dyp, t):
    dyp[pl.ds(t, CONV_PAD), :] = jnp.zeros((CONV_PAD, dyp.shape[1]), F32)


def _col_call(body, name, ncol, ins, outs, t, n_scratch, cols=LANES):
    def spec(rows, off):
        return pl.BlockSpec((rows, cols), lambda j, off=off: (0, j + off))

    res = pl.pallas_call(
        body, name=name, grid=(ncol,),
        in_specs=[spec(r, off) for _, r, off in ins],
        out_specs=[spec(o[0], 0) for o in outs],
        out_shape=[jax.ShapeDtypeStruct(o[:2], o[2] if len(o) > 2 else F32) for o in outs],
        scratch_shapes=[pltpu.VMEM((t + CONV_PAD, cols), F32) for _ in range(n_scratch)],
        compiler_params=_cparams(None),
    )(*[a for a, _, _ in ins])
    return res


def _even_col_fwd(name, p, conv_a, conv_b):
    t = p.shape[0]
    nc = D_A // LANES
    wins = _windows(t)

    def body(av, ag, gb, gc, xi, ca, cb, uc_ref, yb_ref, xp):
        _zero_front(xp)
        for r0, rc in wins:
            rows = pl.ds(r0, rc)
            xp[pl.ds(CONV_PAD + r0, rc), :] = av[rows, :] * _sigmoid(ag[rows, :])
        taps = _taps(ca, CONV_A_WIDTH)
        for r0, rc in wins:
            uc_ref[pl.ds(r0, rc), :] = _conv_at(xp, taps, r0, rc)
        for r0, rc in wins:
            rows = pl.ds(r0, rc)
            xp[pl.ds(CONV_PAD + r0, rc), :] = gc[rows, :] * xi[rows, :]
        taps = _taps(cb, CONV_B_WIDTH)
        for r0, rc in wins:
            rows = pl.ds(r0, rc)
            yb_ref[rows, :] = (gb[rows, :] * _conv_at(xp, taps, r0, rc)).astype(yb_ref.dtype)

    ins = [(p, t, k * nc) for k in range(5)] + [(conv_a, CONV_A_WIDTH, 0), (conv_b, CONV_B_WIDTH, 0)]
    return _col_call(body, name, nc, ins, [(t, D_A), (t, D_A, MXU_DTYPE)], t, 1)


def _even_col_bwd(name, p, duc, dy, conv_a, conv_b):
    t = p.shape[0]
    nc = D_A // LANES
    wins = _windows(t)

    def body(av, ag, gb, gc, xi, duc_ref, dyb_ref, ca, cb, dav, dag, dgb, dgc, dxi, dca, dcb, xp, dyp):
        _zero_front(xp)
        _zero_back(dyp, t)
        for r0, rc in wins:
            rows = pl.ds(r0, rc)
            xp[pl.ds(CONV_PAD + r0, rc), :] = av[rows, :] * _sigmoid(ag[rows, :])
            dyp[rows, :] = duc_ref[rows, :]
        taps = _taps(ca, CONV_A_WIDTH)
        accs = None
        for r0, rc in wins:
            rows = pl.ds(r0, rc)
            accs = _add_to(accs, _conv_bwd_w_at(duc_ref[rows, :], xp, CONV_A_WIDTH, r0, rc))
            du = _conv_bwd_in_at(dyp, taps, r0, rc)
            sig = _sigmoid(ag[rows, :])
            dav[rows, :] = (du * sig).astype(dav.dtype)
            dag[rows, :] = (du * av[rows, :] * sig * (1.0 - sig)).astype(dag.dtype)
        _store_taps(dca, accs)
        for r0, rc in wins:
            rows = pl.ds(r0, rc)
            xp[pl.ds(CONV_PAD + r0, rc), :] = gc[rows, :] * xi[rows, :]
        taps = _taps(cb, CONV_B_WIDTH)
        accs = None
        for r0, rc in wins:
            rows = pl.ds(r0, rc)
            dgb[rows, :] = (dyb_ref[rows, :] * _conv_at(xp, taps, r0, rc)).astype(dgb.dtype)
            dzc = dyb_ref[rows, :] * gb[rows, :]
            dyp[rows, :] = dzc
            accs = _add_to(accs, _conv_bwd_w_at(dzc, xp, CONV_B_WIDTH, r0, rc))
        _store_taps(dcb, accs)
        for r0, rc in wins:
            rows = pl.ds(r0, rc)
            dz = _conv_bwd_in_at(dyp, taps, r0, rc)
            dgc[rows, :] = (dz * xi[rows, :]).astype(dgc.dtype)
            dxi[rows, :] = (dz * gc[rows, :]).astype(dxi.dtype)

    ins = ([(p, t, k * nc) for k in range(5)] + [(duc, t, 0), (dy, t, nc)]
           + [(conv_a, CONV_A_WIDTH, 0), (conv_b, CONV_B_WIDTH, 0)])
    outs = [(t, D_A, MXU_DTYPE)] * 5 + [(CONV_A_WIDTH, D_A), (CONV_B_WIDTH, D_A)]
    return _col_call(body, name, nc, ins, outs, t, 2)


def _ffn_col_fwd(name, u, conv, bias):
    t = u.shape[0]
    nc = D_FF // WIDE_COLS
    wins = _windows(t)

    def body(g_ref, v_ref, cw, b_ref, a_ref, xp):
        _zero_front(xp)
        xp[pl.ds(CONV_PAD, t), :] = g_ref[...].astype(F32)
        taps = _taps(cw, CONV_B_WIDTH)
        b = b_ref[...]
        for r0, rc in wins:
            rows = pl.ds(r0, rc)
            gc = _conv_at(xp, taps, r0, rc) + b
            a_ref[rows, :] = (gc * _sigmoid(gc) * v_ref[rows, :].astype(F32)).astype(a_ref.dtype)

    ins = [(u, t, 0), (u, t, nc), (conv, CONV_B_WIDTH, 0), (bias, 1, 0)]
    return _col_call(body, name, nc, ins, [(t, D_FF, MXU_DTYPE)], t, 1, cols=WIDE_COLS)[0]


def _ffn_col_bwd(name, u, da, conv, bias):
    t = u.shape[0]
    nc = D_FF // LANES
    wins = _windows(t)

    def body(g_ref, v_ref, da_ref, cw, b_ref, du_ref, dcw, db_ref, xp, dyp, dval):
        @pl.when(pl.program_id(1) == 0)
        def _():
            _zero_front(xp)
            _zero_back(dyp, t)
            xp[pl.ds(CONV_PAD, t), :] = g_ref[...].astype(F32)
            taps = _taps(cw, CONV_B_WIDTH)
            b = b_ref[...]
            accs, bias_acc = None, None
            for r0, rc in wins:
                rows = pl.ds(r0, rc)
                gc = _conv_at(xp, taps, r0, rc) + b
                s = _sigmoid(gc)
                d = da_ref[rows, :]
                dval[rows, :] = d * gc * s
                dgc = d * v_ref[rows, :].astype(F32) * (s * (1.0 + gc * (1.0 - s)))
                dyp[rows, :] = dgc
                bias_acc = _add_to(bias_acc, [_fold(dgc)])
                accs = _add_to(accs, _conv_bwd_w_at(dgc, xp, CONV_B_WIDTH, r0, rc))
            db_ref[...] = jnp.sum(bias_acc[0], axis=0, keepdims=True)
            _store_taps(dcw, accs)
            for r0, rc in wins:
                du_ref[pl.ds(r0, rc), :] = _conv_bwd_in_at(dyp, taps, r0, rc).astype(du_ref.dtype)

        @pl.when(pl.program_id(1) == 1)
        def _():
            du_ref[...] = dval[...].astype(du_ref.dtype)

    col = lambda rows, off: pl.BlockSpec((rows, LANES), lambda j, p: (0, j + off))
    return pl.pallas_call(
        body, name=name, grid=(nc, 2),
        in_specs=[col(t, 0), col(t, nc), col(t, 0), col(CONV_B_WIDTH, 0), col(1, 0)],
        out_specs=[pl.BlockSpec((t, LANES), lambda j, p: (0, j + nc * p)), col(CONV_B_WIDTH, 0), col(1, 0)],
        out_shape=[jax.ShapeDtypeStruct((t, 2 * D_FF), MXU_DTYPE), jax.ShapeDtypeStruct((CONV_B_WIDTH, D_FF), F32),
                   jax.ShapeDtypeStruct((1, D_FF), F32)],
        scratch_shapes=[pltpu.VMEM((t + CONV_PAD, LANES), F32) for _ in range(2)] + [pltpu.VMEM((t, LANES), F32)],
        compiler_params=_cparams(None),
    )(u, u, da, conv, bias)


def _shift_fwd(name, p, col0, mu):
    t = p.shape[0]
    wins = _windows(t)

    def body(x_ref, mu_ref, o_ref, xp):
        _zero_front(xp)
        xp[pl.ds(CONV_PAD, t), :] = x_ref[...]
        mu_v = mu_ref[...]
        for r0, rc in wins:
            rows = pl.ds(r0, rc)
            x = x_ref[rows, :]
            o_ref[rows, :] = x + (xp[pl.ds(CONV_PAD - 1 + r0, rc), :] - x) * mu_v

    return _col_call(body, name, RWKV_COLS // WIDE_COLS, [(p, t, col0 // WIDE_COLS), (mu, 1, 0)], [(t, RWKV_COLS)], t, 1,
                     cols=WIDE_COLS)[0]


def _shift_bwd(name, p, col0, mu, dprs):
    t = p.shape[0]
    wins = _windows(t)

    def body(x_ref, mu_ref, d_ref, dx_ref, dmu_ref, xp, dyp):
        _zero_front(xp)
        _zero_back(dyp, t)
        xp[pl.ds(CONV_PAD, t), :] = x_ref[...]
        mu_v = mu_ref[...]
        acc = None
        for r0, rc in wins:
            rows = pl.ds(r0, rc)
            d = d_ref[rows, :]
            acc = _add_to(acc, [_fold(d * (xp[pl.ds(CONV_PAD - 1 + r0, rc), :] - x_ref[rows, :]))])
            dyp[rows, :] = d * mu_v
        dmu_ref[...] = jnp.sum(acc[0], axis=0, keepdims=True)
        for r0, rc in wins:
            rows = pl.ds(r0, rc)
            dx_ref[rows, :] = d_ref[rows, :] - dyp[rows, :] + dyp[pl.ds(1 + r0, rc), :]

    ins = [(p, t, col0 // WIDE_COLS), (mu, 1, 0), (dprs, t, 0)]
    return _col_call(body, name, RWKV_COLS // WIDE_COLS, ins, [(t, RWKV_COLS), (1, RWKV_COLS)], t, 2, cols=WIDE_COLS)


def _hi_lo(x):
    hi = x.astype(BF16)
    return hi, (x - hi.astype(F32)).astype(BF16)


def _dot_passes(a, b, dims, passes):
    d = lambda p, q: lax.dot_general(p, q, dims, preferred_element_type=F32)
    if passes == 1:
        return d(a.astype(MXU_DTYPE), b.astype(MXU_DTYPE))
    ah, al = _hi_lo(a)
    bh, bl = _hi_lo(b)
    return d(ah, bh) + (d(ah, bl) + d(al, bh))


@functools.partial(jax.custom_vjp, nondiff_argnums=(2, 3))
def _dot_vjp(a, b, dims, passes):
    return _dot_passes(a, b, dims, passes)


def _dot_fwd(a, b, dims, passes):
    return _dot_passes(a, b, dims, passes), (a, b)


def _dot_bwd(dims, passes, res, g):
    a, b = res
    if dims == _NN:
        return _dot_passes(g, b, _NT, passes), _dot_passes(a, g, _TN, passes)
    if dims == _NT:
        return _dot_passes(g, b, _NN, passes), _dot_passes(g, a, _TN, passes)
    return _dot_passes(b, g, _NT, passes), _dot_passes(a, g, _NN, passes)


_dot_vjp.defvjp(_dot_fwd, _dot_bwd)


def _doth(a, b, dims=_NN):
    return _dot_vjp(a, b, dims, 3)


def _dotb(a, b, dims=_NN):
    return _dot_vjp(a, b, dims, 1)


def _softplus(x):
    return jnp.where(x > 0, x, 0.0) + jnp.log(1.0 + jnp.exp(jnp.where(x > 0, -x, x)))


def _rwkv_pre(k, xl, gd, w0, w2p, a0, a2p, g2, k_k, k_a, seg):
    z = w0 + _dotb(jnp.tanh(xl), w2p)
    lw = -jnp.exp(-_softplus(-z) - 0.5)
    alpha = _sigmoid(a0 + _dotb(xl, a2p))
    g = _dotb(_sigmoid(gd), g2)
    kk = k * k_k
    kk = kk / jnp.maximum(jnp.sqrt(_dotb(kk * kk, seg)), 1e-12)
    k2 = k * (1.0 + (alpha - 1.0) * k_a)
    return lw, k2, -kk, kk * alpha, g


def _rwkv_post(y, r, k2, v, g, lnx_g, lnx_b, r_k, seg):
    mean = _dotb(y, seg) * (1.0 / HEAD_DIM)
    yc = y - mean
    var = _dotb(yc * yc, seg) * (1.0 / HEAD_DIM)
    yo = yc * lax.rsqrt(var + RWKV_GN_EPS) * lnx_g + lnx_b
    bonus = _dotb(r * k2 * r_k, seg) * v
    return (yo + bonus) * g


def _rwkv_pre_fwd(name, prs, prm, seg, tr):
    t = prs.shape[0]
    row = lambda i: (i, 0)
    fix = lambda i: (0, 0)
    ins = [(prs, (tr, D_R), lambda i: (i, 1)), (prs, (tr, LANES), lambda i: (i, 12)), (prs, (tr, LANES), lambda i: (i, 13)),
           (prm["w0"], (1, D_R), fix), (prm["w2p"], (LANES, D_R), fix), (prm["a0"], (1, D_R), fix),
           (prm["a2p"], (LANES, D_R), fix), (prm["g2"], (LANES, D_R), fix), (prm["k_k"], (1, D_R), fix),
           (prm["k_a"], (1, D_R), fix), (seg, (D_R, D_R), fix)]
    return _call(_rwkv_pre, name, (t // tr,), ins, [((t, D_R), (tr, D_R), row, False)] * 5)


def _rwkv_pre_bwd(name, prs, prm, seg, cts, tr):
    t = prs.shape[0]

    def fn(k, xl, gd, w0, w2p, a0, a2p, g2, k_k, k_a, segv, *ct):
        _, vjp = jax.vjp(lambda *a: _rwkv_pre(*a, segv), k, xl, gd, w0, w2p, a0, a2p, g2, k_k, k_a)
        return vjp(tuple(ct))

    row = lambda i: (i, 0)
    fix = lambda i: (0, 0)
    ins = [(prs, (tr, D_R), lambda i: (i, 1)), (prs, (tr, LANES), lambda i: (i, 12)), (prs, (tr, LANES), lambda i: (i, 13)),
           (prm["w0"], (1, D_R), fix), (prm["w2p"], (LANES, D_R), fix), (prm["a0"], (1, D_R), fix),
           (prm["a2p"], (LANES, D_R), fix), (prm["g2"], (LANES, D_R), fix), (prm["k_k"], (1, D_R), fix),
           (prm["k_a"], (1, D_R), fix), (seg, (D_R, D_R), fix)] + [(c, (tr, D_R), row) for c in cts]
    outs = [((t, D_R), (tr, D_R), row, False), ((t, LANES), (tr, LANES), row, False), ((t, LANES), (tr, LANES), row, False),
            ((1, D_R), (1, D_R), fix, True), ((LANES, D_R), (LANES, D_R), fix, True), ((1, D_R), (1, D_R), fix, True),
            ((LANES, D_R), (LANES, D_R), fix, True), ((LANES, D_R), (LANES, D_R), fix, True),
            ((1, D_R), (1, D_R), fix, True), ((1, D_R), (1, D_R), fix, True)]
    return _call(fn, name, (t // tr,), ins, outs, acc_axis=0)


def _rwkv_post_ins(y, prs, k2, g, prm, seg, tr):
    row = lambda i: (i, 0)
    fix = lambda i: (0, 0)
    return [(y, (tr, D_R), row), (prs, (tr, D_R), row), (k2, (tr, D_R), row), (prs, (tr, D_R), lambda i: (i, 2)),
            (g, (tr, D_R), row), (prm["lnx_g"], (1, D_R), fix), (prm["lnx_b"], (1, D_R), fix), (prm["r_k"], (1, D_R), fix),
            (seg, (D_R, D_R), fix)]


def _rwkv_post_fwd(name, y, prs, k2, g, prm, seg, tr):
    t = y.shape[0]
    return _call(_rwkv_post, name, (t // tr,), _rwkv_post_ins(y, prs, k2, g, prm, seg, tr),
                 [((t, D_R), (tr, D_R), lambda i: (i, 0), False)])


def _rwkv_post_bwd(name, y, prs, k2, g, prm, seg, dy, dy_col, tr):
    t = y.shape[0]

    def fn(yv, r, k2v, v, gv, lg, lb, rk, segv, ct):
        _, vjp = jax.vjp(lambda *a: _rwkv_post(*a, segv), yv, r, k2v, v, gv, lg, lb, rk)
        return vjp(ct)

    row = lambda i: (i, 0)
    fix = lambda i: (0, 0)
    ins = _rwkv_post_ins(y, prs, k2, g, prm, seg, tr) + [(dy, (tr, D_R), lambda i: (i, dy_col))]
    outs = [((t, D_R), (tr, D_R), row, False)] * 5 + [((1, D_R), (1, D_R), fix, True)] * 3
    return _call(fn, name, (t // tr,), ins, outs, acc_axis=0)


def _wkv_chunk(s0, r, lw, k, v, a, b):
    c = r[0].shape[0]
    lane = lax.broadcasted_iota(jnp.int32, (1, 2 * HEAD_DIM), 1)
    first = (lane < HEAD_DIM).astype(F32)
    per_head = lambda x: jnp.concatenate([x * first, x * (1.0 - first)], axis=0)

    def time_of(shape, dim):
        i = lax.broadcasted_iota(jnp.int32, shape, dim)
        return jnp.where(i >= c, i - c, i)

    incl = (lax.broadcasted_iota(jnp.int32, (c, c), 0) >= lax.broadcasted_iota(jnp.int32, (c, c), 1)).astype(F32)
    strict2 = time_of((2 * c, 2 * c), 0) > time_of((2 * c, 2 * c), 1)
    incl2 = lax.broadcasted_iota(jnp.int32, (c, 2 * c), 0) >= time_of((c, 2 * c), 1)
    each = lambda f, *xs: [f(*x) for x in zip(*xs)]
    cum = each(lambda x: _doth(incl, x), lw)
    tot = each(lambda x: jnp.sum(x, axis=0, keepdims=True), lw)
    e_inv = each(lambda x: jnp.exp(-x), cum)
    a_st = each(lambda x, cm, l: per_head(x * jnp.exp(cm - l)), a, cum, lw)
    r_t = each(lambda x, cm: x * jnp.exp(cm), r, cum)
    b_st = each(lambda x, e: per_head(x * e), b, e_inv)
    k_st = each(lambda x, e: per_head(x * e), k, e_inv)
    v_st = each(per_head, v)
    m = each(lambda x, w: jnp.where(strict2, _dotb(x, w, _NT), 0.0), a_st, b_st)
    m_k = each(lambda x, w: jnp.where(strict2, _dotb(x, w, _NT), 0.0), a_st, k_st)
    u = each(lambda x, s, mk, w: _dotb(x, s, _NT) + _dotb(mk, w), a_st, s0, m_k, v_st)
    steps = (c - 1).bit_length()
    for s in range(steps):
        u = each(lambda x, w: x + _dotb(w, x), u, m)
        if s + 1 < steps:
            m = each(lambda w: _dotb(w, w), m)
    n_b = each(lambda x, w: jnp.where(incl2, _dotb(x, w, _NT), 0.0), r_t, b_st)
    n_k = each(lambda x, w: jnp.where(incl2, _dotb(x, w, _NT), 0.0), r_t, k_st)
    y = each(lambda x, s, nb, uu, nk, w: _dotb(x, s, _NT) + _dotb(nb, uu) + _dotb(nk, w), r_t, s0, n_b, u, n_k, v_st)
    dec = each(lambda tt, cm: jnp.exp(tt - cm), tot, cum)
    s1 = each(lambda s, tt, uu, x, d, w, kk: s * jnp.exp(tt) + _dotb(uu, per_head(x * d), _TN) + _dotb(w, per_head(kk * d), _TN),
              s0, tot, u, b, dec, v_st, k)
    return tuple(y), tuple(s1)


WKV_PAIRS_PER_STEP = 4
PAIR = 2 * HEAD_DIM


def _wkv_fwd(name, srcs):
    t = srcs[0][0].shape[0]
    c = _chunk_len(t)
    nc = t // c
    pp = WKV_PAIRS_PER_STEP
    n_pairs = D_R // PAIR

    def body(r, lw, k, v, a, b, y_ref, st_ref, state):
        @pl.when(pl.program_id(1) == 0)
        def _():
            state[...] = jnp.zeros(state.shape, F32)

        pairs = lambda ref: tuple(ref[:, pl.ds(i * PAIR, PAIR)] for i in range(pp))
        s0 = tuple(state[i] for i in range(pp))
        y, s1 = _wkv_chunk(s0, pairs(r), pairs(lw), pairs(k), pairs(v), pairs(a), pairs(b))
        for i in range(pp):
            st_ref[i] = s0[i]
            y_ref[:, pl.ds(i * PAIR, PAIR)] = y[i]
            state[i] = s1[i]

    seq = lambda off: pl.BlockSpec((c, pp * PAIR), lambda g, j: (j, off + g))
    return pl.pallas_call(
        body, name=name, grid=(n_pairs // pp, nc), in_specs=[seq(off) for _, off in srcs],
        out_specs=[seq(0), pl.BlockSpec((pp, None, PAIR, PAIR), lambda g, j: (g, j, 0, 0))],
        out_shape=[jax.ShapeDtypeStruct((t, D_R), F32), jax.ShapeDtypeStruct((n_pairs, nc, PAIR, PAIR), F32)],
        scratch_shapes=[pltpu.VMEM((pp, PAIR, PAIR), F32)],
        compiler_params=_cparams(None),
    )(*[a for a, _ in srcs])


def _wkv_bwd(name, srcs, st, dy):
    t = srcs[0][0].shape[0]
    c = _chunk_len(t)
    nc = t // c
    pp = WKV_PAIRS_PER_STEP
    n_pairs = D_R // PAIR

    def body(r, lw, k, v, a, b, st_ref, dy_ref, dr, dlw, dk, dv, da, db, dstate):
        @pl.when(pl.program_id(1) == 0)
        def _():
            dstate[...] = jnp.zeros(dstate.shape, F32)

        half = lax.broadcasted_iota(jnp.int32, (PAIR, PAIR), 0) < HEAD_DIM
        same_head = half == (lax.broadcasted_iota(jnp.int32, (PAIR, PAIR), 1) < HEAD_DIM)
        pairs = lambda ref: tuple(ref[:, pl.ds(i * PAIR, PAIR)] for i in range(pp))
        s0 = tuple(st_ref[i] for i in range(pp))
        _, vjp = jax.vjp(_wkv_chunk, s0, pairs(r), pairs(lw), pairs(k), pairs(v), pairs(a), pairs(b))
        ds0, *dxs = vjp((pairs(dy_ref), tuple(dstate[i] for i in range(pp))))
        for i in range(pp):
            for ref, val in zip((dr, dlw, dk, dv, da, db), dxs):
                ref[:, pl.ds(i * PAIR, PAIR)] = val[i]
            dstate[i] = jnp.where(same_head, ds0[i], 0.0)

    seq = lambda off: pl.BlockSpec((c, pp * PAIR), lambda g, j: (nc - 1 - j, off + g))
    return pl.pallas_call(
        body, name=name, grid=(n_pairs // pp, nc),
        in_specs=[seq(off) for _, off in srcs]
        + [pl.BlockSpec((pp, None, PAIR, PAIR), lambda g, j: (g, nc - 1 - j, 0, 0)), seq(dy[1])],
        out_specs=[seq(0)] * 6,
        out_shape=[jax.ShapeDtypeStruct((t, D_R), F32)] * 6,
        scratch_shapes=[pltpu.VMEM((pp, PAIR, PAIR), F32)],
        compiler_params=_cparams(None),
    )(*[a for a, _ in srcs], st, dy[0])


def _rope(x, cos, sin, rot):
    return x * cos + _dotb(x, rot) * sin


def _attn_block(nb, q, kp, kc, km, vp, vc, vm, sk, cq, sq, cp, sp, cm, sm, rot):
    g = GQA_GROUP
    scale = HEAD_DIM ** -0.5
    each = lambda f, *xs: [f(*x) for x in zip(*xs)]
    down = lambda x: jnp.concatenate([x] * g, axis=0)
    cq4, sq4 = down(cq), down(sq)
    kpr = each(lambda x: _rope(x, cp, sp, rot), kp)
    kcr = each(lambda x: _rope(x, cq, sq, rot), kc)
    kmr = each(lambda x: _rope(x, cm, sm, rot), km)
    qr = each(lambda x: _rope(x, cq4, sq4, rot), q)
    i = lax.broadcasted_iota(jnp.int32, (g * BLOCK, BLOCK), 0)
    i = i - BLOCK * ((i >= BLOCK).astype(jnp.int32) + (i >= 2 * BLOCK).astype(jnp.int32) + (i >= 3 * BLOCK).astype(jnp.int32))
    j = lax.broadcasted_iota(jnp.int32, (g * BLOCK, BLOCK), 1)
    nbv = jnp.zeros((g * BLOCK, BLOCK), jnp.int32) + nb
    ok_p = (j > i) & (nbv >= 2)
    ok_c = (j <= i) & (nbv >= 1)
    ok_m = (j >= BLOCK - N_META) & ((nbv >= 1) | (j <= i))
    sink = each(lambda s4: jnp.concatenate([jnp.broadcast_to(s, (BLOCK, 1)) for s in s4], axis=0), sk)
    s_p = each(lambda x, kk: jnp.where(ok_p, _dotb(x, kk, _NT) * scale, NEG_INF), qr, kpr)
    s_c = each(lambda x, kk: jnp.where(ok_c, _dotb(x, kk, _NT) * scale, NEG_INF), qr, kcr)
    s_m = each(lambda x, kk: jnp.where(ok_m, _dotb(x, kk, _NT) * scale, NEG_INF), qr, kmr)
    rmax = lambda s: jnp.max(s, axis=-1, keepdims=True)
    m = each(lambda a, b, c, d: lax.stop_gradient(jnp.maximum(jnp.maximum(rmax(a), rmax(b)), jnp.maximum(rmax(c), d))),
             s_p, s_c, s_m, sink)
    e_p = each(lambda s, mm: jnp.exp(s - mm), s_p, m)
    e_c = each(lambda s, mm: jnp.exp(s - mm), s_c, m)
    e_m = each(lambda s, mm: jnp.exp(s - mm), s_m, m)
    rsum = lambda e: jnp.sum(e, axis=-1, keepdims=True)
    inv = each(lambda a, b, c, d, mm: 1.0 / (rsum(a) + rsum(b) + rsum(c) + jnp.exp(d - mm)), e_p, e_c, e_m, sink, m)
    return tuple(each(lambda a, b, c, iv, x, y, z: _dotb(a * iv, x) + _dotb(b * iv, y) + _dotb(c * iv, z),
                      e_p, e_c, e_m, inv, vp, vc, vm))


def _attn_specs():
    cur = lambda n: (0, n, 0)
    prev = lambda n: (0, jnp.maximum(n - 1, 0), 0)
    meta = lambda n: (0, 0, 0)
    kv = lambda m: pl.BlockSpec((N_KV_HEADS, BLOCK, HEAD_DIM), m)
    tab = lambda m: pl.BlockSpec((BLOCK, HEAD_DIM), m)
    tcur, tprev, tmeta = (lambda n: (n, 0)), (lambda n: (jnp.maximum(n - 1, 0), 0)), (lambda n: (0, 0))
    qspec = pl.BlockSpec((N_Q_HEADS, BLOCK, HEAD_DIM), cur)
    sspec = pl.BlockSpec((N_Q_HEADS, 8, LANES), meta)
    specs = [qspec, kv(prev), kv(cur), kv(meta), kv(prev), kv(cur), kv(meta), sspec,
             tab(tcur), tab(tcur), tab(tprev), tab(tprev), tab(tmeta), tab(tmeta),
             pl.BlockSpec((HEAD_DIM, HEAD_DIM), lambda n: (0, 0))]
    return specs, qspec, sspec, kv


def _attn_args(q, k, v, sinks_b, cos, sin, rot):
    return (q, k, k, k, v, v, v, sinks_b, cos, sin, cos, sin, cos, sin, rot)


def _attn_operands(q_ref, kp, kc, km, vp, vc, vm, s_ref):
    groups = range(N_KV_HEADS)
    q = tuple(jnp.concatenate([q_ref[GQA_GROUP * i + h] for h in range(GQA_GROUP)], axis=0) for i in groups)
    sk = tuple(tuple(s_ref[GQA_GROUP * i + h][0:1, 0:1] for h in range(GQA_GROUP)) for i in groups)
    per_head = lambda ref: tuple(ref[i] for i in groups)
    return q, per_head(kp), per_head(kc), per_head(km), per_head(vp), per_head(vc), per_head(vm), sk


def _attn_fwd(name, q, k, v, sinks_b, cos, sin, rot):
    tp = q.shape[1]
    specs, qspec, _, _ = _attn_specs()

    def body(q_ref, kp, kc, km, vp, vc, vm, s_ref, cq, sq, cp, sp, cm, sm, rot_ref, o_ref):
        out = _attn_block(pl.program_id(0), *_attn_operands(q_ref, kp, kc, km, vp, vc, vm, s_ref),
                          cq[...], sq[...], cp[...], sp[...], cm[...], sm[...], rot_ref[...])
        for i in range(N_KV_HEADS):
            for h in range(GQA_GROUP):
                o_ref[GQA_GROUP * i + h] = out[i][h * BLOCK:(h + 1) * BLOCK]

    return pl.pallas_call(
        body, name=name, grid=(tp // BLOCK,), in_specs=specs, out_specs=qspec,
        out_shape=jax.ShapeDtypeStruct(q.shape, F32), compiler_params=_cparams(None),
    )(*_attn_args(q, k, v, sinks_b, cos, sin, rot))


def _attn_bwd(name, q, k, v, sinks_b, cos, sin, rot, do):
    tp = q.shape[1]
    nb = tp // BLOCK
    specs, qspec, sspec, kv = _attn_specs()

    def body(q_ref, kp, kc, km, vp, vc, vm, s_ref, cq, sq, cp, sp, cm, sm, rot_ref, do_ref,
             dq_ref, dkp, dkc, dvp, dvc, dkm, dvm, ds_ref):
        n = pl.program_id(0)
        tabs = (cq[...], sq[...], cp[...], sp[...], cm[...], sm[...], rot_ref[...])
        _, vjp = jax.vjp(lambda *a: _attn_block(n, *a, *tabs), *_attn_operands(q_ref, kp, kc, km, vp, vc, vm, s_ref))
        do_all = tuple(jnp.concatenate([do_ref[GQA_GROUP * i + h] for h in range(GQA_GROUP)], axis=0)
                       for i in range(N_KV_HEADS))
        dq, gkp, gkc, gkm, gvp, gvc, gvm, dsk = vjp(do_all)
        for i in range(N_KV_HEADS):
            dkp[i] = gkp[i]
            dkc[i] = gkc[i]
            dvp[i] = gvp[i]
            dvc[i] = gvc[i]
            for h in range(GQA_GROUP):
                dq_ref[GQA_GROUP * i + h] = dq[i][h * BLOCK:(h + 1) * BLOCK]

        @pl.when(n == 0)
        def _():
            for i in range(N_KV_HEADS):
                dkm[i] = gkm[i]
                dvm[i] = gvm[i]
                for h in range(GQA_GROUP):
                    ds_ref[GQA_GROUP * i + h] = jnp.broadcast_to(dsk[i][h], (8, LANES))

        @pl.when(n != 0)
        def _():
            for i in range(N_KV_HEADS):
                dkm[i] += gkm[i]
                dvm[i] += gvm[i]
                for h in range(GQA_GROUP):
                    ds_ref[GQA_GROUP * i + h] += jnp.broadcast_to(dsk[i][h], (8, LANES))

    part = pl.BlockSpec((N_KV_HEADS, None, BLOCK, HEAD_DIM), lambda n: (0, n, 0, 0))
    part_shape = jax.ShapeDtypeStruct((N_KV_HEADS, nb, BLOCK, HEAD_DIM), F32)
    meta_shape = jax.ShapeDtypeStruct((N_KV_HEADS, BLOCK, HEAD_DIM), F32)
    return pl.pallas_call(
        body, name=name, grid=(nb,), in_specs=specs + [qspec],
        out_specs=[qspec, part, part, part, part, kv(lambda n: (0, 0, 0)), kv(lambda n: (0, 0, 0)), sspec],
        out_shape=[jax.ShapeDtypeStruct(q.shape, F32), part_shape, part_shape, part_shape, part_shape,
                   meta_shape, meta_shape, jax.ShapeDtypeStruct(sinks_b.shape, F32)],
        compiler_params=_cparams(None),
    )(*_attn_args(q, k, v, sinks_b, cos, sin, rot), do)


def _kv_combine(name, k_parts, v_parts):
    g, nb = k_parts[1].shape[:2]

    def fn(own_k, nxt_k, mt_k, own_v, nxt_v, mt_v):
        m = pl.program_id(1)
        one = jnp.ones((BLOCK, HEAD_DIM), F32)
        use_next = jnp.where(one * m < nb - 1, 1.0, 0.0)
        use_meta = jnp.where(one * m < 1, 1.0, 0.0)
        return own_k + nxt_k * use_next + mt_k * use_meta, own_v + nxt_v * use_next + mt_v * use_meta

    blk = (None, None, BLOCK, HEAD_DIM)
    ins = []
    for prev_part, own_part, meta in (k_parts, v_parts):
        ins += [(own_part, blk, lambda a, m: (a, m, 0, 0)),
                (prev_part, blk, lambda a, m: (a, jnp.minimum(m + 1, nb - 1), 0, 0)),
                (meta, (None, BLOCK, HEAD_DIM), lambda a, m: (a, 0, 0))]
    return _call(fn, name, (g, nb), ins,
                 [((g, nb * BLOCK, HEAD_DIM), (None, BLOCK, HEAD_DIM), lambda a, m: (a, m, 0), False)] * 2)


PACK_W = 1024
ELEMENTWISE_BLOCK_BYTES = 1 << 21


def _rows_tile(rows, cols):
    cap = max(8, ELEMENTWISE_BLOCK_BYTES // (4 * cols))
    for d in range(min(rows, cap), 0, -1):
        if rows % d == 0 and d % 8 == 0:
            return d
    return rows


def _adamw(name, w, g, m, v):
    rows, cols = w.shape
    tr = _rows_tile(rows, cols)

    def fn(wv, gv, mv, vv):
        m1 = ADAM_B1 * mv + (1.0 - ADAM_B1) * gv
        v1 = ADAM_B2 * vv + (1.0 - ADAM_B2) * (gv * gv)
        m_hat = m1 / (1.0 - ADAM_B1 ** ADAM_STEP)
        v_hat = v1 / (1.0 - ADAM_B2 ** ADAM_STEP)
        return -ADAM_LR * (m_hat / (jnp.sqrt(v_hat) + ADAM_EPS) + ADAM_WD * wv), m1, v1

    blk = (tr, cols)
    row = lambda i: (i, 0)
    return _call(fn, name, (rows // tr,), [(a, blk, row) for a in (w, g, m, v)], [((rows, cols), blk, row, False)] * 3)


def _pair_add_placed(name, g, recv, cm_idx, out_dtype):
    s, a, b = g.shape
    half = a // 2

    def body(cm_ref, a_ref, b_ref, o_ref, own_ref):
        val = (a_ref[...] + b_ref[...]).astype(out_dtype)
        o_ref[...] = val

        @pl.when(pl.program_id(0) == cm_ref[1])
        def _():
            own_ref[...] = val

    blk = (None, half, b)
    shape = jax.ShapeDtypeStruct((s, half, b), out_dtype)
    return pl.pallas_call(
        body, name=name,
        grid_spec=pltpu.PrefetchScalarGridSpec(
            num_scalar_prefetch=1, grid=(s,),
            in_specs=[pl.BlockSpec(blk, lambda j, cm: (j, cm[0], 0)), pl.BlockSpec(blk, lambda j, cm: (j, 0, 0))],
            out_specs=[pl.BlockSpec(blk, lambda j, cm: (j, 0, 0)), pl.BlockSpec(blk, lambda j, cm: (cm[1], 0, 0))]),
        out_shape=[shape, shape], compiler_params=_cparams(None),
    )(cm_idx, g, recv)


def _sum_chips(name, parts, c_idx, layer, n_layers, into=None):
    _, a, b = parts.shape
    tr = _rows_tile(a, b)

    def body(c_ref, p0, p1, p2, p3, *rest):
        o_ref = rest[-1]
        up = lambda p: p[...].astype(F32)
        o_ref[...] = ((up(p0) + up(p1)) + up(p2)) + up(p3)

    in_specs = [pl.BlockSpec((None, tr, b), lambda i, c, k=k: (k, i, 0)) for k in range(N_CHIPS)]
    args = [c_idx] + [parts] * N_CHIPS
    aliases = {}
    if into is not None:
        in_specs.append(_ANY)
        args.append(into)
        aliases = {1 + N_CHIPS: 0}
    return pl.pallas_call(
        body, name=name,
        grid_spec=pltpu.PrefetchScalarGridSpec(
            num_scalar_prefetch=1, grid=(a // tr,), in_specs=in_specs,
            out_specs=pl.BlockSpec((None, None, tr, b), lambda i, c: (layer, c[0], i, 0))),
        out_shape=jax.ShapeDtypeStruct((n_layers, 2, a, b), F32), input_output_aliases=aliases,
        compiler_params=_cparams(None),
    )(*args)


def _place_own_block(name, w, layer, me_idx, dtype):
    _, a2, b = w.shape
    a = a2 // 2
    tr = _rows_tile(a, b)
    nb = a // tr

    def body(me_ref, w_ref, o_ref):
        o_ref[...] = w_ref[...].astype(dtype)

    return pl.pallas_call(
        body, name=name,
        grid_spec=pltpu.PrefetchScalarGridSpec(
            num_scalar_prefetch=1, grid=(2, nb),
            in_specs=[pl.BlockSpec((None, tr, b), lambda h, i, me: (layer, h * nb + i, 0))],
            out_specs=pl.BlockSpec((None, None, tr, b), lambda h, i, me: (me[0], h, i, 0))),
        out_shape=jax.ShapeDtypeStruct((N_CHIPS, 2, a, b), dtype), compiler_params=_cparams(None),
    )(me_idx, w)


def _mesh_pos():
    return lax.axis_index("x"), lax.axis_index("y"), lax.axis_index("c")


def _other_chips(x, y):
    return [(1 - x, y), (x, 1 - y), (1 - x, 1 - y)]


_ANY = pl.BlockSpec(memory_space=pl.ANY)


def _gather_weights(name, bufs, from_chips=True):
    n = len(bufs)

    def body(*refs):
        out_refs = refs[n:2 * n]
        send_sems, recv_sems = refs[2 * n:]
        x, y, c = _mesh_pos()
        me = 2 * x + y
        sibling = (x, y, 1 - c)
        chips = _other_chips(x, y)

        def copy(i, k, chip_idx, half, to):
            return pltpu.make_async_remote_copy(src_ref=out_refs[i].at[chip_idx, half], dst_ref=out_refs[i].at[chip_idx, half],
                                                send_sem=send_sems.at[6 * i + k], recv_sem=recv_sems.at[6 * i + k],
                                                device_id=to, device_id_type=MESH)

        first = [copy(i, j, me, c, (*chip, c)) for i in range(n) for j, chip in enumerate(chips)] if from_chips else []
        for cp in first:
            cp.start()
        passed = []
        for i in range(n):
            for j, (cx, cy) in enumerate(chips):
                idx = 2 * cx + cy
                if from_chips:
                    copy(i, j, idx, c, sibling).wait_recv()
                fwd = copy(i, 3 + j, idx, c, sibling)
                fwd.start()
                passed.append(fwd)
        for i in range(n):
            for j, (cx, cy) in enumerate(chips):
                copy(i, 3 + j, 2 * cx + cy, 1 - c, sibling).wait_recv()
        for cp in first + passed:
            cp.wait_send()

    return pl.pallas_call(
        body, name=name, in_specs=[_ANY] * n, out_specs=[_ANY] * n,
        out_shape=[jax.ShapeDtypeStruct(b.shape, b.dtype) for b in bufs],
        input_output_aliases={i: i for i in range(n)},
        scratch_shapes=[pltpu.SemaphoreType.DMA((6 * n,)), pltpu.SemaphoreType.DMA((6 * n,))],
        compiler_params=pltpu.CompilerParams(has_side_effects=True),
    )(*bufs)


def _gather_start(name, groups):
    bufs = [b for g in groups for b in g]
    n = len(bufs)
    ng = len(groups)

    def body(*refs):
        b_refs = refs[:n]
        sems = refs[n:n + 2 * ng]
        token = refs[-1]
        x, y, c = _mesh_pos()
        me = 2 * x + y
        i = 0
        for gi, g in enumerate(groups):
            for k in range(len(g)):
                for j, (cx, cy) in enumerate(_other_chips(x, y)):
                    pltpu.make_async_remote_copy(src_ref=b_refs[i].at[me, c], dst_ref=b_refs[i].at[me, c],
                                                 send_sem=sems[2 * gi].at[3 * k + j], recv_sem=sems[2 * gi + 1].at[3 * k + j],
                                                 device_id=(cx, cy, c), device_id_type=MESH).start()
                i += 1
        token[...] = jnp.zeros(token.shape, F32)

    sem_shapes = [pltpu.SemaphoreType.DMA((3 * len(g),)) for g in groups for _ in range(2)]
    res = pl.pallas_call(
        body, name=name,
        out_shape=(*sem_shapes, *[pltpu.HBM(b.shape, b.dtype) for b in bufs], jax.ShapeDtypeStruct((8, LANES), F32)),
        in_specs=[_HBM] * n,
        out_specs=(*[_SEM] * (2 * ng), *[_HBM] * n, pl.BlockSpec(memory_space=pltpu.VMEM)),
        input_output_aliases={i: 2 * ng + i for i in range(n)},
        compiler_params=pltpu.CompilerParams(has_side_effects=_DATAFLOW),
    )(*[pltpu.with_memory_space_constraint(b, pltpu.HBM) for b in bufs])
    out, i = [], 2 * ng
    for gi, g in enumerate(groups):
        out.append((res[2 * gi], res[2 * gi + 1], list(res[i:i + len(g)])))
        i += len(g)
    return out, res[-1]


def _gather_wait(name, send_sems, recv_sems, bufs, after):
    n = len(bufs)

    def body(*refs):
        b_refs = refs[:n]
        s_sems, r_sems = refs[n], refs[n + 1]
        x, y, c = _mesh_pos()
        me = 2 * x + y
        for k in range(n):
            for j, (cx, cy) in enumerate(_other_chips(x, y)):
                idx = 2 * cx + cy
                copy = pltpu.make_async_remote_copy(src_ref=b_refs[k].at[me, c], dst_ref=b_refs[k].at[idx, c],
                                                    send_sem=s_sems.at[3 * k + j], recv_sem=r_sems.at[3 * k + j],
                                                    device_id=(cx, cy, c), device_id_type=MESH)
                copy.wait_send()
                copy.wait_recv()

    res = pl.pallas_call(
        body, name=name,
        out_shape=tuple(pltpu.HBM(b.shape, b.dtype) for b in bufs),
        in_specs=[_HBM] * n + [_SEM, _SEM, _ANY],
        out_specs=tuple([_HBM] * n),
        input_output_aliases={i: i for i in range(n)},
        compiler_params=pltpu.CompilerParams(has_side_effects=_DATAFLOW),
    )(*bufs, send_sems, recv_sems, after)
    return list(res)


def _halves_to_sibling(name, units):
    n = len(units)

    def body(*refs):
        g_refs, out_refs = refs[:n], refs[n:2 * n]
        send_sems, recv_sems = refs[2 * n:]
        x, y, c = _mesh_pos()
        cps = []
        for i in range(n):
            half = units[i].shape[1] // 2
            src = g_refs[i].at[pl.ds(0, N_CHIPS), pl.ds((1 - c) * half, half)]
            cp = pltpu.make_async_remote_copy(src_ref=src, dst_ref=out_refs[i], send_sem=send_sems.at[i],
                                              recv_sem=recv_sems.at[i], device_id=(x, y, 1 - c), device_id_type=MESH)
            cp.start()
            cps.append(cp)
        for cp in cps:
            cp.wait()

    return pl.pallas_call(
        body, name=name, in_specs=[_ANY] * n, out_specs=[_ANY] * n,
        out_shape=[jax.ShapeDtypeStruct((u.shape[0], u.shape[1] // 2, u.shape[2]), u.dtype) for u in units],
        scratch_shapes=[pltpu.SemaphoreType.DMA((n,)), pltpu.SemaphoreType.DMA((n,))],
        compiler_params=pltpu.CompilerParams(has_side_effects=True),
    )(*units)


_HBM = pl.BlockSpec(memory_space=pltpu.HBM)
_SEM = pl.BlockSpec(memory_space=pltpu.SEMAPHORE)
_DATAFLOW = pltpu.SideEffectType.DATAFLOW_SIDE_EFFECTING


def _halves_start(name, units):
    n = len(units)

    def body(*refs):
        g_refs, z_refs = refs[:n], refs[n:2 * n]
        send_sems, recv_sems = refs[2 * n], refs[2 * n + 1]
        token = refs[-1]
        x, y, c = _mesh_pos()
        for i in range(n):
            half = units[i].shape[1] // 2
            src = g_refs[i].at[pl.ds(0, N_CHIPS), pl.ds((1 - c) * half, half)]
            pltpu.make_async_remote_copy(src_ref=src, dst_ref=z_refs[i], send_sem=send_sems.at[i], recv_sem=recv_sems.at[i],
                                         device_id=(x, y, 1 - c), device_id_type=MESH).start()
        token[...] = jnp.zeros(token.shape, F32)

    zones = [lax.empty((u.shape[0], u.shape[1] // 2, u.shape[2]), u.dtype) for u in units]
    hbm = lambda a: pltpu.HBM(a.shape, a.dtype)
    res = pl.pallas_call(
        body, name=name,
        out_shape=(pltpu.SemaphoreType.DMA((n,)), pltpu.SemaphoreType.DMA((n,)),
                   *[hbm(a) for a in units], *[hbm(a) for a in zones], jax.ShapeDtypeStruct((8, LANES), F32)),
        in_specs=[_HBM] * (2 * n),
        out_specs=(_SEM, _SEM, *[_HBM] * (2 * n), pl.BlockSpec(memory_space=pltpu.VMEM)),
        input_output_aliases={i: 2 + i for i in range(2 * n)},
        compiler_params=pltpu.CompilerParams(has_side_effects=_DATAFLOW),
    )(*[pltpu.with_memory_space_constraint(a, pltpu.HBM) for a in list(units) + zones])
    return res[0], res[1], res[2:2 + n], res[2 + n:2 + 2 * n], res[-1]


def _halves_wait(name, send_sems, recv_sems, units, zones, after):
    n = len(units)

    def body(*refs):
        g_refs, z_refs = refs[:n], refs[n:2 * n]
        s_sems, r_sems = refs[2 * n], refs[2 * n + 1]
        x, y, c = _mesh_pos()
        for i in range(n):
            half = units[i].shape[1] // 2
            src = g_refs[i].at[pl.ds(0, N_CHIPS), pl.ds((1 - c) * half, half)]
            copy = pltpu.make_async_remote_copy(src_ref=src, dst_ref=z_refs[i], send_sem=s_sems.at[i], recv_sem=r_sems.at[i],
                                                device_id=(x, y, 1 - c), device_id_type=MESH)
            copy.wait_send()
            copy.wait_recv()

    hbm = lambda a: pltpu.HBM(a.shape, a.dtype)
    res = pl.pallas_call(
        body, name=name,
        out_shape=(*[hbm(a) for a in units], *[hbm(a) for a in zones]),
        in_specs=[_HBM] * (2 * n) + [_SEM, _SEM, _ANY],
        out_specs=tuple([_HBM] * (2 * n)),
        input_output_aliases={i: i for i in range(2 * n)},
        compiler_params=pltpu.CompilerParams(has_side_effects=_DATAFLOW),
    )(*units, *zones, send_sems, recv_sems, after)
    return res[:n], res[n:]


def _scatter_start(name, sums, zones):
    n = len(sums)

    def body(*refs):
        h_refs, z_refs = refs[:n], refs[n:2 * n]
        send_sems, recv_sems = refs[2 * n], refs[2 * n + 1]
        token = refs[-1]
        x, y, c = _mesh_pos()
        me = 2 * x + y
        for i in range(n):
            for j, (cx, cy) in enumerate(_other_chips(x, y)):
                pltpu.make_async_remote_copy(src_ref=h_refs[i].at[2 * cx + cy], dst_ref=z_refs[i].at[me],
                                             send_sem=send_sems.at[3 * i + j], recv_sem=recv_sems.at[3 * i + j],
                                             device_id=(cx, cy, c), device_id_type=MESH).start()
        token[...] = jnp.zeros(token.shape, F32)

    hbm = lambda a: pltpu.HBM(a.shape, a.dtype)
    res = pl.pallas_call(
        body, name=name,
        out_shape=(pltpu.SemaphoreType.DMA((3 * n,)), pltpu.SemaphoreType.DMA((3 * n,)),
                   *[hbm(a) for a in sums], *[hbm(a) for a in zones], jax.ShapeDtypeStruct((8, LANES), F32)),
        in_specs=[_HBM] * (2 * n),
        out_specs=(_SEM, _SEM, *[_HBM] * (2 * n), pl.BlockSpec(memory_space=pltpu.VMEM)),
        input_output_aliases={i: 2 + i for i in range(2 * n)},
        compiler_params=pltpu.CompilerParams(has_side_effects=_DATAFLOW),
    )(*[pltpu.with_memory_space_constraint(a, pltpu.HBM) for a in list(sums) + list(zones)])
    return res[0], res[1], res[2:2 + n], res[2 + n:2 + 2 * n], res[-1]


def _scatter_wait(name, send_sems, recv_sems, sums, zones, after):
    n = len(sums)

    def body(*refs):
        h_refs, z_refs = refs[:n], refs[n:2 * n]
        s_sems, r_sems = refs[2 * n], refs[2 * n + 1]
        x, y, c = _mesh_pos()
        me = 2 * x + y
        for i in range(n):
            for j, (cx, cy) in enumerate(_other_chips(x, y)):
                idx = 2 * cx + cy
                copy = pltpu.make_async_remote_copy(src_ref=h_refs[i].at[idx], dst_ref=z_refs[i].at[idx],
                                                    send_sem=s_sems.at[3 * i + j], recv_sem=r_sems.at[3 * i + j],
                                                    device_id=(cx, cy, c), device_id_type=MESH)
                copy.wait_send()
                copy.wait_recv()

    hbm = lambda a: pltpu.HBM(a.shape, a.dtype)
    res = pl.pallas_call(
        body, name=name,
        out_shape=(*[hbm(a) for a in sums], *[hbm(a) for a in zones]),
        in_specs=[_HBM] * (2 * n) + [_SEM, _SEM, _ANY],
        out_specs=tuple([_HBM] * (2 * n)),
        input_output_aliases={i: i for i in range(2 * n)},
        compiler_params=pltpu.CompilerParams(has_side_effects=_DATAFLOW),
    )(*sums, *zones, send_sems, recv_sems, after)
    return res[n:]


def _join_halves(name, results):
    n = len(results)
    pieces = [(i, l) for i in range(n) for l in range(results[i].shape[0])]

    def body(*refs):
        out_refs = refs[n:2 * n]
        send_sems, recv_sems = refs[2 * n:]
        x, y, c = _mesh_pos()

        def copy(k, half):
            i, l = pieces[k]
            return pltpu.make_async_remote_copy(src_ref=out_refs[i].at[l, half], dst_ref=out_refs[i].at[l, half],
                                                send_sem=send_sems.at[k], recv_sem=recv_sems.at[k],
                                                device_id=(x, y, 1 - c), device_id_type=MESH)

        cps = [copy(k, c) for k in range(len(pieces))]
        for cp in cps:
            cp.start()
        for k in range(len(pieces)):
            copy(k, 1 - c).wait_recv()
        for cp in cps:
            cp.wait_send()

    return pl.pallas_call(
        body, name=name, in_specs=[_ANY] * n, out_specs=[_ANY] * n,
        out_shape=[jax.ShapeDtypeStruct(r.shape, r.dtype) for r in results],
        input_output_aliases={i: i for i in range(n)},
        scratch_shapes=[pltpu.SemaphoreType.DMA((len(pieces),)), pltpu.SemaphoreType.DMA((len(pieces),))],
        compiler_params=pltpu.CompilerParams(has_side_effects=True),
    )(*results)


def _pack(arrays, dtype, rows_multiple):
    flat = jnp.concatenate([a.reshape(-1).astype(dtype) for a in arrays])
    unit = rows_multiple * PACK_W
    total = -(-flat.shape[0] // unit) * unit
    return jnp.pad(flat, (0, total - flat.shape[0])).reshape(total // PACK_W, PACK_W)


def _unpack(flat, shapes):
    out, off = [], 0
    for s in shapes:
        n = 1
        for d in s:
            n *= d
        out.append(flat[..., off:off + n].reshape(flat.shape[:-1] + tuple(s)))
        off += n
    return out


def _ffn_fwd(tag, l, h, g, w_up, conv, bias, w_down, tm):
    hn = _rms_fwd(f"{tag}_norm", h, g, tm)
    u = _mm_cs(f"{tag}_up", hn, w_up, l, tm, out_dtype=FFN_HIDDEN_DTYPE)
    act = _ffn_col_fwd(f"{tag}_glu", u, conv, bias)
    w_down = w_down(act) if callable(w_down) else w_down
    h_out = _mm_full(f"{tag}_down", act, w_down, l, tm, D_FF // 2, add=h)
    return h_out, (hn, u, act), w_down


def _ffn_bwd(tag, l, h, g, w_up, conv, bias, w_down, saved, dh, tm):
    hn, u, act = saved
    da = _mm_nt_full(f"{tag}_down_dx", dh, w_down, l, tm, D_FF // 2)
    dw_down = _mm_tn_full(f"{tag}_down_dw", act, dh, tm, D_FF // 2)
    du, dconv, dbias = _ffn_col_bwd(f"{tag}_glu_bwd", u, da, conv, bias)
    dw_up = _mm_tn_cs(f"{tag}_up_dw", hn, du, N_CHIPS, tm)
    dhn = _mm_nt_cs(f"{tag}_up_dx", du, w_up, l, tm)
    dh, dg = _rms_bwd(f"{tag}_norm_bwd", h, g, dhn, dh, tm)
    return dh, dict(norm=dg, w_up=dw_up, conv=dconv, bias=dbias, w_down=dw_down)


def _to_heads(z, nh, pad):
    t = z.shape[0]
    return jnp.pad(z.reshape(t, nh, HEAD_DIM).transpose(1, 0, 2), ((0, 0), (pad, 0), (0, 0)))


def _from_heads(z, pad):
    nh, tp, _ = z.shape
    return z[:, pad:].transpose(1, 0, 2).reshape(tp - pad, nh * HEAD_DIM)


def _rope_tables(tp, pad):
    half = HEAD_DIM // 2
    inv = ROPE_THETA ** (-jnp.arange(half, dtype=F32) / half)
    ang = (jnp.arange(tp, dtype=F32) - pad)[:, None] * inv[None, :]
    cos, sin = jnp.cos(ang), jnp.sin(ang)
    rot = jnp.zeros((HEAD_DIM, HEAD_DIM), F32)
    idx = jnp.arange(half)
    rot = rot.at[idx + half, idx].set(-1.0).at[idx, idx + half].set(1.0)
    return jnp.concatenate([cos, cos], axis=1), jnp.concatenate([sin, sin], axis=1), rot


def _local_step(x, tgt, w, on_grads=None, fetch=None):
    emit = on_grads if on_grads is not None else (lambda tag, units: 0.0)
    need = (lambda tag, after: w) if fetch is None else (lambda tag, after: {**w, **fetch(tag, after)})
    seq = x.shape[0]
    t = seq + N_META
    tm = _row_tile(t, ROW_TILE_CAP)
    tr = _row_tile(t, ROW_TILE_CAP // 2)
    pad = BLOCK - N_META
    grads = {}

    h0 = jnp.concatenate([w["meta_tokens"], x], axis=0)
    tgt_p = jnp.pad(tgt, ((N_META, 0), (0, 0)))

    hn0 = _rms_fwd("l0_norm", h0, w["norm_mix"][0:1], tm)
    p0 = _mm_cs("l0_in", hn0, w["ev_w_in"], 0, tm)
    uc, yb = _even_col_fwd("l0_convs", p0, w["ev_conv_a"], w["ev_conv_b"])
    ya = _even_ln_fwd("l0_ln", uc, w["ev_ln_a_g"], w["ev_ln_a_b"], tm)
    y0 = jnp.concatenate([ya, yb], axis=1)
    w = need("ev_out", y0)
    h1 = _mm_full("l0_out", y0, w["ev_w_out"], 0, tm, D_MODEL, add=h0)
    w = need("f0", h1)
    down0 = w["ff_w_down0"] if "ff_w_down0" in w else (lambda act: need("f0_down", act)["ff_w_down0"])
    f0 = (0, h1, w["norm_ffn"][0:1], w["ff_w_up0"], w["ff_conv"][0], w["ff_conv_b"][0:1])
    h2, ffn0, down0 = _ffn_fwd("f0", *f0, down0, tm)
    f0 = f0 + (down0,)
    w = need("od", h2)

    hn2 = _rms_fwd("l1_norm", h2, w["norm_mix"][1:2], tm)
    p1 = _mm_cs("l1_in", hn2, w["od_w_in"], 0, tm)
    cos, sin, rot = _rope_tables(t + pad, pad)
    qh = _to_heads(p1[:, :D_ATT], N_Q_HEADS, pad)
    kh = _to_heads(p1[:, D_ATT:D_ATT + D_KV], N_KV_HEADS, pad)
    vh = _to_heads(p1[:, D_ATT + D_KV:D_ATT + 2 * D_KV], N_KV_HEADS, pad)
    sinks_b = jnp.broadcast_to(w["od_sinks"].reshape(N_Q_HEADS, 1, 1), (N_Q_HEADS, 8, LANES))
    y_att = _from_heads(_attn_fwd("l1_attn", qh, kh, vh, sinks_b, cos, sin, rot), pad)

    col0 = D_ATT + 2 * D_KV
    ch = jnp.arange(D_R) // HEAD_DIM
    seg = (ch[:, None] == ch[None, :]).astype(F32)
    prm = dict(w0=w["od_w0"], a0=w["od_a0"], g2=w["od_g2"], k_k=w["od_k_k"], k_a=w["od_k_a"],
               lnx_g=w["od_lnx_g"], lnx_b=w["od_lnx_b"], r_k=w["od_r_k"].reshape(1, D_R),
               w2p=jnp.concatenate([w["od_w2"], jnp.zeros((LORA_A, D_R), F32)], axis=0),
               a2p=jnp.concatenate([jnp.zeros((LORA_W, D_R), F32), w["od_a2"]], axis=0))
    prs = _shift_fwd("l1_shift", p1, col0, w["od_mu"])
    lw, k2, a_, b_, gate_r = _rwkv_pre_fwd("l1_rwkv_pre", prs, prm, seg, tr)
    v_off = 2 * D_R // (WKV_PAIRS_PER_STEP * PAIR)
    scan_in = [(prs, 0), (lw, 0), (k2, 0), (prs, v_off), (a_, 0), (b_, 0)]
    y_scan, states = _wkv_fwd("l1_wkv", scan_in)
    y_rwkv = _rwkv_post_fwd("l1_rwkv_post", y_scan, prs, k2, gate_r, prm, seg, tr)
    y1 = jnp.concatenate([y_att, y_rwkv], axis=1).astype(MXU_DTYPE)
    h3 = _mm_full("l1_out", y1, w["od_w_out"], 0, tm, D_MODEL, add=h2)
    w = need("f1", h3)
    f1 = (0, h3, w["norm_ffn"][1:2], w["ff_w_up1"], w["ff_conv"][1], w["ff_conv_b"][1:2], w["ff_w_down1"])
    h4, ffn1, _ = _ffn_fwd("f1", *f1, tm)

    loss_blk, dh, d_norm_final = _final_loss("final", h4, w["norm_final"], tgt_p, tm)
    grads["norm_final"] = d_norm_final

    dh, gf1 = _ffn_bwd("f1", *f1, ffn1, dh, tm)
    zero = emit("f1", {"ff_w_down1": gf1["w_down"].reshape(N_CHIPS, D_FF // N_CHIPS, D_MODEL), "ff_w_up1": gf1["w_up"]})
    prm = dict(prm, lnx_g=prm["lnx_g"] + zero)
    dy1 = _mm_nt_full("l1_out_dx", dh, w["od_w_out"], 0, tm, D_MODEL)
    grads["od_w_out"] = _mm_tn_full("l1_out_dw", y1, dh, tm, D_MODEL // 2)
    dy_scan, dr_p, dk2_p, dv_p, dgate_r, grads["od_lnx_g"], grads["od_lnx_b"], d_rk = _rwkv_post_bwd(
        "l1_rwkv_post_bwd", y_scan, prs, k2, gate_r, prm, seg, dy1, 1, tr)
    grads["od_r_k"] = d_rk.reshape(N_R_HEADS, HEAD_DIM)
    dr_s, dlw, dk2_s, dv_s, da_, db_ = _wkv_bwd("l1_wkv_bwd", scan_in, states, (dy_scan, 0))
    dk, dxl, dgd, grads["od_w0"], dw2p, grads["od_a0"], da2p, grads["od_g2"], grads["od_k_k"], grads["od_k_a"] = (
        _rwkv_pre_bwd("l1_rwkv_pre_bwd", prs, prm, seg, (dlw, dk2_s + dk2_p, da_, db_, dgate_r), tr))
    grads["od_w2"] = dw2p[:LORA_W]
    grads["od_a2"] = da2p[LORA_W:]
    dprs = jnp.concatenate([dr_s + dr_p, dk, dv_s + dv_p, dxl, dgd], axis=1)
    dpr, grads["od_mu"] = _shift_bwd("l1_shift_bwd", p1, col0, w["od_mu"], dprs)
    doh = _to_heads(dy1[:, :D_ATT], N_Q_HEADS, pad)
    dqh, dkp, dkc, dvp, dvc, dkm, dvm, dsinks = _attn_bwd("l1_attn_bwd", qh, kh, vh, sinks_b, cos, sin, rot, doh)
    grads["od_sinks"] = dsinks[:, 0, 0].reshape(1, N_Q_HEADS)
    dkh, dvh = _kv_combine("l1_attn_dkv", (dkp, dkc, dkm), (dvp, dvc, dvm))
    dp1 = jnp.concatenate([_from_heads(dqh, pad), _from_heads(dkh, pad), _from_heads(dvh, pad), dpr], axis=1).astype(MXU_DTYPE)
    grads["od_w_in"] = _mm_tn_cs("l1_in_dw", hn2, dp1, N_CHIPS, tm)
    dhn2 = _mm_nt_cs("l1_in_dx", dp1, w["od_w_in"], 0, tm)
    dh, d_mix1 = _rms_bwd("l1_norm_bwd", h2, w["norm_mix"][1:2], dhn2, dh, tm)

    zero = emit("od", {"od_w_out": grads["od_w_out"].reshape(N_CHIPS, D_MODEL // N_CHIPS, D_MODEL), "od_w_in": grads["od_w_in"]})
    f0 = f0[:5] + (f0[5] + zero,) + f0[6:]
    dh, gf0 = _ffn_bwd("f0", *f0, ffn0, dh, tm)
    zero = emit("f0", {"ff_w_down0": gf0["w_down"].reshape(N_CHIPS, D_FF // N_CHIPS, D_MODEL), "ff_w_up0": gf0["w_up"]})
    w = dict(w, ev_ln_a_g=w["ev_ln_a_g"] + zero)
    dy0 = _mm_nt_full("l0_out_dx", dh, w["ev_w_out"], 0, tm, D_MODEL)
    grads["ev_w_out"] = _mm_tn_full("l0_out_dw", y0, dh, tm, D_MODEL // 2)
    duc, grads["ev_ln_a_g"], grads["ev_ln_a_b"] = _even_ln_bwd("l0_ln_bwd", uc, w["ev_ln_a_g"], w["ev_ln_a_b"], dy0, 0, tm)
    *dparts, grads["ev_conv_a"], grads["ev_conv_b"] = _even_col_bwd("l0_convs_bwd", p0, duc, dy0, w["ev_conv_a"], w["ev_conv_b"])
    dp0 = jnp.concatenate(dparts, axis=1)
    grads["ev_w_in"] = _mm_tn_cs("l0_in_dw", hn0, dp0, N_CHIPS, tm)
    dhn0 = _mm_nt_cs("l0_in_dx", dp0, w["ev_w_in"], 0, tm)
    dh, d_mix0 = _rms_bwd("l0_norm_bwd", h0, w["norm_mix"][0:1], dhn0, dh, tm)

    grads["norm_mix"] = jnp.concatenate([d_mix0, d_mix1], axis=0)
    grads["norm_ffn"] = jnp.concatenate([gf0["norm"], gf1["norm"]], axis=0)
    grads["ff_w_up"] = [gf0["w_up"], gf1["w_up"]]
    grads["ff_conv"] = jnp.stack([gf0["conv"], gf1["conv"]])
    grads["ff_conv_b"] = jnp.concatenate([gf0["bias"], gf1["bias"]], axis=0)
    grads["ff_w_down"] = [gf0["w_down"], gf1["w_down"]]
    grads["meta_tokens"] = dh[:N_META]
    return loss_blk[0, 0], dh[N_META:], grads


SHARD_AXIS = {
    "meta_tokens": 1, "norm_mix": None, "norm_ffn": None, "norm_final": None,
    "ev_w_in": 2, "ev_conv_a": 2, "ev_ln_a_g": None, "ev_ln_a_b": None, "ev_conv_b": 2, "ev_w_out": 1,
    "od_w_in": 2, "od_sinks": None, "od_mu": 1, "od_w0": 1, "od_w2": 2, "od_a0": 1, "od_a2": 2, "od_g2": 2,
    "od_k_k": 1, "od_k_a": 1, "od_r_k": None, "od_lnx_g": 1, "od_lnx_b": 1, "od_w_out": 1,
    "ff_w_up": 2, "ff_conv": 2, "ff_conv_b": None, "ff_w_down": 1,
}
WEIGHTS = list(SHARD_AXIS)
BIG = ("ev_w_in", "ev_w_out", "od_w_in", "od_w_out", "ff_w_up", "ff_w_down")
SHARDED = [n for n in WEIGHTS if SHARD_AXIS[n] is not None]
SMALL = [n for n in SHARDED if n not in BIG]
REPLICATED = [n for n in WEIGHTS if SHARD_AXIS[n] is None]


def _join(g, axis):
    return jnp.concatenate([g[k] for k in range(N_CHIPS)], axis=axis)


def _split(full, axis):
    return jnp.stack(jnp.split(full, N_CHIPS, axis=axis))


def _full_weights(gathered, repl):
    w = {}
    sq = lambda a: a.reshape(a.shape[1:]) if a.shape[0] == 1 else a
    for n in REPLICATED:
        w[n] = repl[n]
    w["norm_final"] = repl["norm_final"].reshape(1, D_MODEL)
    for n in ("ev_ln_a_g", "ev_ln_a_b"):
        w[n] = repl[n].reshape(1, D_A)
    w["od_r_k"] = repl["od_r_k"][0]
    w["meta_tokens"] = _join(gathered["meta_tokens"], 1)
    for n in ("ev_conv_a", "ev_conv_b", "od_w2", "od_a2", "od_g2"):
        w[n] = sq(_join(gathered[n], 2))
    for n in ("od_mu", "od_w0", "od_a0", "od_k_k", "od_k_a", "od_lnx_g", "od_lnx_b"):
        w[n] = _join(gathered[n], 1)
    w["ff_conv"] = _join(gathered["ff_conv"], 2)
    return w


def _shard_grads(grads):
    out = {}
    for n in REPLICATED:
        out[n] = grads[n]
    out["norm_final"] = grads["norm_final"].reshape(D_MODEL)
    out["od_r_k"] = grads["od_r_k"][None]
    out["meta_tokens"] = _split(grads["meta_tokens"], 1)
    for n in ("ev_conv_a", "ev_conv_b", "od_w2", "od_a2", "od_g2"):
        out[n] = _split(grads[n][None], 2)
    for n in ("od_mu", "od_w0", "od_a0", "od_k_k", "od_k_a", "od_lnx_g", "od_lnx_b"):
        out[n] = _split(grads[n], 1)
    out["ff_conv"] = _split(grads["ff_conv"], 2)
    return out


def kernel(x, meta_tokens, norm_mix, norm_ffn, norm_final, ev_w_in, ev_conv_a, ev_ln_a_g, ev_ln_a_b, ev_conv_b, ev_w_out, od_w_in, od_sinks, od_mu, od_w0, od_w2, od_a0, od_a2, od_g2, od_k_k, od_k_a, od_r_k, od_lnx_g, od_lnx_b, od_w_out, ff_w_up, ff_conv, ff_conv_b, ff_w_down, loss_target, m_meta_tokens, m_norm_mix, m_norm_ffn, m_norm_final, m_ev_w_in, m_ev_conv_a, m_ev_ln_a_g, m_ev_ln_a_b, m_ev_conv_b, m_ev_w_out, m_od_w_in, m_od_sinks, m_od_mu, m_od_w0, m_od_w2, m_od_a0, m_od_a2, m_od_g2, m_od_k_k, m_od_k_a, m_od_r_k, m_od_lnx_g, m_od_lnx_b, m_od_w_out, m_ff_w_up, m_ff_conv, m_ff_conv_b, m_ff_w_down, v_meta_tokens, v_norm_mix, v_norm_ffn, v_norm_final, v_ev_w_in, v_ev_conv_a, v_ev_ln_a_g, v_ev_ln_a_b, v_ev_conv_b, v_ev_w_out, v_od_w_in, v_od_sinks, v_od_mu, v_od_w0, v_od_w2, v_od_a0, v_od_a2, v_od_g2, v_od_k_k, v_od_k_a, v_od_r_k, v_od_lnx_g, v_od_lnx_b, v_od_w_out, v_ff_w_up, v_ff_conv, v_ff_conv_b, v_ff_w_down):
    wts = dict(meta_tokens=meta_tokens, norm_mix=norm_mix, norm_ffn=norm_ffn, norm_final=norm_final, ev_w_in=ev_w_in, ev_conv_a=ev_conv_a, ev_ln_a_g=ev_ln_a_g, ev_ln_a_b=ev_ln_a_b, ev_conv_b=ev_conv_b, ev_w_out=ev_w_out, od_w_in=od_w_in, od_sinks=od_sinks, od_mu=od_mu, od_w0=od_w0, od_w2=od_w2, od_a0=od_a0, od_a2=od_a2, od_g2=od_g2, od_k_k=od_k_k, od_k_a=od_k_a, od_r_k=od_r_k, od_lnx_g=od_lnx_g, od_lnx_b=od_lnx_b, od_w_out=od_w_out, ff_w_up=ff_w_up, ff_conv=ff_conv, ff_conv_b=ff_conv_b, ff_w_down=ff_w_down)
    mom = dict(meta_tokens=m_meta_tokens, norm_mix=m_norm_mix, norm_ffn=m_norm_ffn, norm_final=m_norm_final, ev_w_in=m_ev_w_in, ev_conv_a=m_ev_conv_a, ev_ln_a_g=m_ev_ln_a_g, ev_ln_a_b=m_ev_ln_a_b, ev_conv_b=m_ev_conv_b, ev_w_out=m_ev_w_out, od_w_in=m_od_w_in, od_sinks=m_od_sinks, od_mu=m_od_mu, od_w0=m_od_w0, od_w2=m_od_w2, od_a0=m_od_a0, od_a2=m_od_a2, od_g2=m_od_g2, od_k_k=m_od_k_k, od_k_a=m_od_k_a, od_r_k=m_od_r_k, od_lnx_g=m_od_lnx_g, od_lnx_b=m_od_lnx_b, od_w_out=m_od_w_out, ff_w_up=m_ff_w_up, ff_conv=m_ff_conv, ff_conv_b=m_ff_conv_b, ff_w_down=m_ff_w_down)
    var = dict(meta_tokens=v_meta_tokens, norm_mix=v_norm_mix, norm_ffn=v_norm_ffn, norm_final=v_norm_final, ev_w_in=v_ev_w_in, ev_conv_a=v_ev_conv_a, ev_ln_a_g=v_ev_ln_a_g, ev_ln_a_b=v_ev_ln_a_b, ev_conv_b=v_ev_conv_b, ev_w_out=v_ev_w_out, od_w_in=v_od_w_in, od_sinks=v_od_sinks, od_mu=v_od_mu, od_w0=v_od_w0, od_w2=v_od_w2, od_a0=v_od_a0, od_a2=v_od_a2, od_g2=v_od_g2, od_k_k=v_od_k_k, od_k_a=v_od_k_a, od_r_k=v_od_r_k, od_lnx_g=v_od_lnx_g, od_lnx_b=v_od_lnx_b, od_w_out=v_od_w_out, ff_w_up=v_ff_w_up, ff_conv=v_ff_conv, ff_conv_b=v_ff_conv_b, ff_w_down=v_ff_w_down)

    me_idx = (2 * lax.axis_index("x") + lax.axis_index("y")).astype(jnp.int32).reshape(1)
    c_idx = lax.axis_index("c").astype(jnp.int32).reshape(1)
    small_mine = _pack([wts[n] for n in SMALL], F32, 2 * 8)
    sources = {"ev_w_in": (ev_w_in, 0), "small": (small_mine[None], 0), "ev_w_out": (ev_w_out, 0),
               "ff_w_up0": (ff_w_up, 0), "ff_w_down0": (ff_w_down, 0), "od_w_in": (od_w_in, 0), "od_w_out": (od_w_out, 0),
               "ff_w_up1": (ff_w_up, 1), "ff_w_down1": (ff_w_down, 1)}
    bufs = {n: _place_own_block("place_" + n, a, l, me_idx, F32 if n == "small" else MXU_DTYPE)
            for n, (a, l) in sources.items()}

    def as_used(n, g):
        if n in ("ev_w_out", "od_w_out", "ff_w_down0", "ff_w_down1"):
            return g.reshape(1, -1, g.shape[-1])
        return g.reshape(N_CHIPS, 1, -1, g.shape[-1])

    first = dict(zip(("ev_w_in", "small"), _gather_weights("gather_first", [bufs["ev_w_in"], bufs["small"]])))
    gathered = dict(zip(SMALL, _unpack(first["small"].reshape(N_CHIPS, -1), [wts[n].shape for n in SMALL])))
    w_full = _full_weights(gathered, wts)
    w_full["ev_w_in"] = as_used("ev_w_in", first["ev_w_in"])
    groups = {"ev_out": ["ev_w_out"], "f0": ["ff_w_up0"], "f0_down": ["ff_w_down0"], "od": ["od_w_in", "od_w_out"],
              "f1": ["ff_w_up1", "ff_w_down1"]}
    started_gathers, token = _gather_start("gather_start", [[bufs[n] for n in g] for g in groups.values()])
    started_gathers = dict(zip(groups, started_gathers))
    w_full["norm_mix"] = w_full["norm_mix"] + token[0, 0]

    def fetch(tag, after):
        send_sems, recv_sems, group_bufs = started_gathers[tag]
        landed = _gather_wait("gather_wait_" + tag, send_sems, recv_sems, group_bufs, after)
        whole = _gather_weights("gather_siblings_" + tag, landed, from_chips=False)
        return {n: as_used(n, g) for n, g in zip(groups[tag], whole)}

    cm_idx = jnp.concatenate([c_idx, me_idx])
    started = []
    to_sibling = []

    def to_chips(tag, names, units, from_sibling):
        pairs = [_pair_add_placed(f"grads_pair_add_{n}", u, r, cm_idx, GRAD_WIRE_DTYPE)
                 for n, u, r in zip(names, units, from_sibling)]
        send_sems, recv_sems, sums, zones, token = _scatter_start(
            f"grads_to_chips_start_{tag}", [p[0] for p in pairs], [p[1] for p in pairs])
        started.append((tag, names, send_sems, recv_sems, sums, zones))
        return token[0, 0]

    def start_reduction(tag, units):
        names = list(units)
        arrays = [units[n] for n in names]
        zero = 0.0
        if to_sibling:
            before, bnames, send_sems, recv_sems, thru, zones = to_sibling.pop()
            thru, got = _halves_wait(f"grads_to_sibling_wait_{before}", send_sems, recv_sems, thru, zones, arrays[-1])
            zero = zero + to_chips(before, bnames, thru, got)
        if tag == "f0":
            return zero + to_chips(tag, names, arrays, _halves_to_sibling(f"grads_to_sibling_{tag}", arrays))
        send_sems, recv_sems, thru, zones, token = _halves_start(f"grads_to_sibling_start_{tag}", arrays)
        to_sibling.append((tag, names, send_sems, recv_sems, thru, zones))
        return zero + token[0, 0]

    loss_local, grad_x, grads = _local_step(x[0], loss_target[0], w_full, start_reduction, fetch)
    loss = lax.psum(loss_local, ("x", "y", "c"))

    sg = _shard_grads(grads)
    small_rows = [jnp.concatenate([sg[n][k].reshape(-1) for n in SMALL] + [sg[n].reshape(-1) for n in REPLICATED])
                  for k in range(N_CHIPS)]
    n_el = small_rows[0].shape[0]
    n_rows = -(-n_el // (16 * PACK_W)) * 16
    small_unit = jnp.stack([jnp.pad(r, (0, n_rows * PACK_W - n_el)).reshape(n_rows, PACK_W) for r in small_rows])
    last = {"ev_w_out": grads["ev_w_out"].reshape(N_CHIPS, D_MODEL // N_CHIPS, D_MODEL), "ev_w_in": grads["ev_w_in"],
            "small": small_unit}
    from_sibling = _halves_to_sibling("grads_to_sibling_ev", list(last.values()))
    pairs = [_pair_add_placed(f"grads_pair_add_{n}", u, r, cm_idx, F32 if n == "small" else GRAD_WIRE_DTYPE)
             for (n, u), r in zip(last.items(), from_sibling)]
    ev_send, ev_recv, ev_sums, ev_zones, token = _scatter_start(
        "grads_to_chips_start_ev", [p[0] for p in pairs], [p[1] for p in pairs])
    dests = {"ev_w_in": ("ev_w_in", 0), "od_w_in": ("od_w_in", 0), "ev_w_out": ("ev_w_out", 0), "od_w_out": ("od_w_out", 0),
             "ff_w_up0": ("ff_w_up", 0), "ff_w_up1": ("ff_w_up", 1), "ff_w_down0": ("ff_w_down", 0),
             "ff_w_down1": ("ff_w_down", 1), "small": ("small", 0)}
    outs = {"grad": {}, "delta": {}, "new_m": {}, "new_v": {}}

    def finish(tag, from_chips, results):
        reduced = {}
        for n, part in from_chips.items():
            r, l = dests[n]
            reduced[r] = _sum_chips(f"grads_chip_sum_{n}", part, c_idx, l, 2 if r.startswith("ff_w") else 1,
                                    into=reduced.get(r))
        joined = dict(zip(results, _join_halves("grads_join_" + tag, [reduced[r] for r in results])))
        for n, g in joined.items():
            if n == "small":
                continue
            shape = wts[n].shape
            flat = lambda a: a.reshape(-1, shape[-1])
            new = _adamw("adamw_" + n, flat(wts[n]), flat(g), flat(mom[n]), flat(var[n]))
            for kind, arr in zip(("grad", "delta", "new_m", "new_v"), (g,) + tuple(new)):
                outs[kind][n] = arr.reshape(shape)
        return joined

    from_chips = {}
    for tag, names, send_sems, recv_sems, sums, zones in started:
        from_chips.update(zip(names, _scatter_wait(f"grads_to_chips_wait_{tag}", send_sems, recv_sems, sums, zones, token)))
    finish("layers", from_chips, ["od_w_in", "od_w_out", "ff_w_up", "ff_w_down"])
    from_chips = dict(zip(last, _scatter_wait("grads_to_chips_wait_ev", ev_send, ev_recv, ev_sums, ev_zones,
                                              outs["delta"]["ff_w_up"])))
    joined = finish("ev", from_chips, ["ev_w_in", "ev_w_out", "small"])

    order = SMALL + REPLICATED
    packed = lambda d: jnp.pad(jnp.concatenate([d[n].reshape(-1) for n in order]),
                               (0, n_rows * PACK_W - n_el)).reshape(n_rows, PACK_W)
    g_small = joined["small"].reshape(n_rows, PACK_W)
    new = _adamw("adamw_small", packed(wts), g_small, packed(mom), packed(var))
    for tag, arr in zip(("grad", "delta", "new_m", "new_v"), (g_small,) + tuple(new)):
        outs[tag].update(zip(order, _unpack(arr.reshape(-1), [wts[n].shape for n in order])))
    return (loss, grad_x[None], *[outs["grad"][n] for n in WEIGHTS], *[outs["delta"][n] for n in WEIGHTS],
            *[outs["new_m"][n] for n in WEIGHTS], *[outs["new_v"][n] for n in WEIGHTS])
```

```python
import functools

import jax
import jax.numpy as jnp
from jax import lax
from jax.experimental import pallas as pl
from jax.experimental.pallas import tpu as pltpu

F32 = jnp.float32
BF16 = jnp.bfloat16
MXU_DTYPE = BF16
GRAD_WIRE_DTYPE = BF16
FFN_HIDDEN_DTYPE = BF16

D_MODEL = 1024
N_META = 16
RMS_EPS = 1e-6
LN_EPS = 1e-5
D_A = 512
CONV_A_WIDTH = 31
CONV_B_WIDTH = 3
HEAD_DIM = 64
N_Q_HEADS = 8
N_KV_HEADS = 2
GQA_GROUP = 4
D_ATT = 512
D_KV = 128
BLOCK = 128
ROPE_THETA = 10000.0
D_R = 512
N_R_HEADS = 8
LORA_W = 64
LORA_A = 64
LORA_G = 128
RWKV_GN_EPS = 64e-5
RWKV_COLS = 3 * D_R + LORA_W + LORA_A + LORA_G
D_FF = 2816
NEG_INF = -1e30
ADAM_LR = 0.001
ADAM_B1 = 0.9
ADAM_B2 = 0.999
ADAM_EPS = 1e-08
ADAM_WD = 0.01
ADAM_STEP = 10

N_CHIPS = 4
LANES = 128
CONV_PAD = 32
ROW_TILE_CAP = 704
VMEM_LIMIT_V7X = 56 * 1024 * 1024
MESH = pl.DeviceIdType.MESH


def _cparams(sem=None):
    return pltpu.CompilerParams(dimension_semantics=sem, vmem_limit_bytes=VMEM_LIMIT_V7X)


def _row_tile(t, cap):
    for d in range(min(t, cap), 0, -1):
        if t % d == 0 and d % 16 == 0:
            return d
    return t


def _chunk_len(t):
    for d in (64, 48, 32, 16, 8):
        if t % d == 0:
            return d
    raise ValueError(t)


def _call(fn, name, grid, ins, outs, acc_axis=None, sem=None):
    n_in, n_out = len(ins), len(outs)
    dtype = lambda o: o[4] if len(o) > 4 else F32

    def body(*refs):
        vals = fn(*[r[...] for r in refs[:n_in]])
        if not isinstance(vals, (tuple, list)):
            vals = (vals,)
        for r, v, o in zip(refs[n_in:n_in + n_out], vals, outs):
            if o[3]:
                first = pl.program_id(acc_axis) == 0

                @pl.when(first)
                def _(r=r, v=v):
                    r[...] = v

                @pl.when(jnp.logical_not(first))
                def _(r=r, v=v):
                    r[...] += v
            else:
                r[...] = v.astype(dtype(o))

    res = pl.pallas_call(
        body, name=name, grid=grid,
        in_specs=[pl.BlockSpec(b, m) for _, b, m in ins],
        out_specs=[pl.BlockSpec(o[1], o[2]) for o in outs],
        out_shape=[jax.ShapeDtypeStruct(o[0], dtype(o)) for o in outs],
        compiler_params=_cparams(sem),
    )(*[a for a, _, _ in ins])
    return res if n_out > 1 else res[0]


def _matmul(name, a, b, *, dims, grid, a_spec, b_spec, o_shape, o_spec, acc_shape, nk, k_axis,
            add=None, add_spec=None, out_dtype=F32):
    def product(a_ref, b_ref):
        return lax.dot_general(a_ref[...].astype(MXU_DTYPE), b_ref[...].astype(MXU_DTYPE), dims, preferred_element_type=F32)

    def body_single(*refs):
        a_ref, b_ref, o_ref = refs[0], refs[1], refs[-1]
        res = product(a_ref, b_ref) if add is None else product(a_ref, b_ref) + refs[2][...]
        o_ref[...] = res.astype(out_dtype)

    def body_steps(*refs):
        a_ref, b_ref, o_ref, acc = refs[0], refs[1], refs[-2], refs[-1]
        k = pl.program_id(k_axis)

        @pl.when(k == 0)
        def _():
            if add is None:
                acc[...] = jnp.zeros(acc.shape, F32)
            else:
                acc[...] = refs[2][...]

        acc[...] += product(a_ref, b_ref)

        @pl.when(k == nk - 1)
        def _():
            o_ref[...] = acc[...].astype(out_dtype)

    args = [a, b] + ([] if add is None else [add])
    specs = [a_spec, b_spec] + ([] if add is None else [add_spec])
    return pl.pallas_call(
        body_single if nk == 1 else body_steps, name=name, grid=grid, in_specs=specs, out_specs=o_spec,
        out_shape=jax.ShapeDtypeStruct(o_shape, out_dtype),
        scratch_shapes=[] if nk == 1 else [pltpu.VMEM(acc_shape, F32)],
        compiler_params=_cparams(None),
    )(*args)


MATMUL_BLOCKS_BYTES = 46 * 1024 * 1024


def _whole_if_fits(t, tile, need_bytes):
    return t if need_bytes <= MATMUL_BLOCKS_BYTES else tile


_NN = (((1,), (0,)), ((), ()))
_NT = (((1,), (1,)), ((), ()))
_TN = (((0,), (0,)), ((), ()))


def _mm_cs(name, x, wg, l, tm, out_dtype=F32):
    t, k = x.shape
    s, _, _, n = wg.shape
    tm = _whole_if_fits(t, tm, 2 * (t * k * x.dtype.itemsize + k * n * wg.dtype.itemsize + t * n * 4))
    return _matmul(name, x, wg, dims=_NN, grid=(s, t // tm, 1),
                   a_spec=pl.BlockSpec((tm, k), lambda j, i, kk: (i, 0)),
                   b_spec=pl.BlockSpec((None, None, k, n), lambda j, i, kk: (j, l, 0, 0)),
                   o_shape=(t, s * n), o_spec=pl.BlockSpec((tm, n), lambda j, i, kk: (i, j)),
                   acc_shape=(tm, n), nk=1, k_axis=2, out_dtype=out_dtype)


def _mm_full(name, x, w, l, tm, tk, add=None):
    t, k = x.shape
    n = w.shape[2]
    nk = k // tk
    tm = _whole_if_fits(t, tm, 2 * (t * tk * x.dtype.itemsize + tk * n * w.dtype.itemsize + t * n * 4 * (1 if add is None else 2))
                        + (t * n * 4 if nk > 1 else 0))
    return _matmul(name, x, w, dims=_NN, grid=(t // tm, 1, nk),
                   a_spec=pl.BlockSpec((tm, tk), lambda i, j, kk: (i, kk)),
                   b_spec=pl.BlockSpec((None, tk, n), lambda i, j, kk: (l, kk, 0)),
                   o_shape=(t, n), o_spec=pl.BlockSpec((tm, n), lambda i, j, kk: (i, 0)),
                   acc_shape=(tm, n), nk=nk, k_axis=2,
                   add=add, add_spec=pl.BlockSpec((tm, n), lambda i, j, kk: (i, 0)))


def _mm_nt_cs(name, dy, wg, l, tm, add=None):
    t = dy.shape[0]
    s, _, k, n = wg.shape
    tm = _whole_if_fits(t, tm, 2 * (t * n * dy.dtype.itemsize + k * n * wg.dtype.itemsize + t * k * 4 * (1 if add is None else 2))
                        + t * k * 4)
    return _matmul(name, dy, wg, dims=_NT, grid=(t // tm, 1, s),
                   a_spec=pl.BlockSpec((tm, n), lambda i, j, kk: (i, kk)),
                   b_spec=pl.BlockSpec((None, None, k, n), lambda i, j, kk: (kk, l, 0, 0)),
                   o_shape=(t, k), o_spec=pl.BlockSpec((tm, k), lambda i, j, kk: (i, 0)),
                   acc_shape=(tm, k), nk=s, k_axis=2,
                   add=add, add_spec=pl.BlockSpec((tm, k), lambda i, j, kk: (i, 0)))


def _mm_nt_full(name, dy, w, l, tm, tko):
    t, n = dy.shape
    k = w.shape[1]
    tm = _whole_if_fits(t, tm, 2 * (t * n * dy.dtype.itemsize + tko * n * w.dtype.itemsize + t * tko * 4))
    return _matmul(name, dy, w, dims=_NT, grid=(t // tm, k // tko, 1),
                   a_spec=pl.BlockSpec((tm, n), lambda i, j, kk: (i, 0)),
                   b_spec=pl.BlockSpec((None, tko, n), lambda i, j, kk: (l, j, 0)),
                   o_shape=(t, k), o_spec=pl.BlockSpec((tm, tko), lambda i, j, kk: (i, j)),
                   acc_shape=(tm, tko), nk=1, k_axis=2)


def _mm_tn_cs(name, x, dy, s, tk):
    t, k = x.shape
    n = dy.shape[1] // s
    tk = _whole_if_fits(t, tk, 2 * (t * k * x.dtype.itemsize + t * n * dy.dtype.itemsize + k * n * 4))
    nk = t // tk
    return _matmul(name, x, dy, dims=_TN, grid=(s, 1, nk),
                   a_spec=pl.BlockSpec((tk, k), lambda j, i, kk: (kk, 0)),
                   b_spec=pl.BlockSpec((tk, n), lambda j, i, kk: (kk, j)),
                   o_shape=(s, k, n), o_spec=pl.BlockSpec((None, k, n), lambda j, i, kk: (j, 0, 0)),
                   acc_shape=(k, n), nk=nk, k_axis=2)


def _mm_tn_full(name, y, dh, tk, tko):
    t, k = y.shape
    n = dh.shape[1]
    tk = _whole_if_fits(t, tk, 2 * (t * tko * y.dtype.itemsize + t * n * dh.dtype.itemsize + tko * n * 4))
    nk = t // tk
    return _matmul(name, y, dh, dims=_TN, grid=(k // tko, 1, nk),
                   a_spec=pl.BlockSpec((tk, tko), lambda j, i, kk: (kk, j)),
                   b_spec=pl.BlockSpec((tk, n), lambda j, i, kk: (kk, 0)),
                   o_shape=(k, n), o_spec=pl.BlockSpec((tko, n), lambda j, i, kk: (j, 0)),
                   acc_shape=(tko, n), nk=nk, k_axis=2)


def _sigmoid(x):
    return 1.0 / (1.0 + jnp.exp(-x))


def _rms_fwd(name, h, g, tr):
    t, d = h.shape

    def fn(hv, gv):
        r = lax.rsqrt(jnp.mean(hv * hv, axis=-1, keepdims=True) + RMS_EPS)
        return hv * r * gv

    return _call(fn, name, (t // tr,), [(h, (tr, d), lambda i: (i, 0)), (g, (1, d), lambda i: (0, 0))],
                 [((t, d), (tr, d), lambda i: (i, 0), False, MXU_DTYPE)])


def _rms_bwd(name, h, g, dhn, dh, tr):
    t, d = h.shape

    def fn(hv, gv, dy, dh_in):
        r = lax.rsqrt(jnp.mean(hv * hv, axis=-1, keepdims=True) + RMS_EPS)
        xh = hv * r
        dg = jnp.sum(dy * xh, axis=0, keepdims=True)
        dxh = dy * gv
        dx = r * (dxh - xh * jnp.mean(dxh * xh, axis=-1, keepdims=True))
        return dh_in + dx, dg

    row = lambda i: (i, 0)
    return _call(fn, name, (t // tr,),
                 [(h, (tr, d), row), (g, (1, d), lambda i: (0, 0)), (dhn, (tr, d), row), (dh, (tr, d), row)],
                 [((t, d), (tr, d), row, False), ((1, d), (1, d), lambda i: (0, 0), True)], acc_axis=0)


def _final_loss(name, h, g, tgt, tr):
    t, d = h.shape

    def fn(hv, gv, tv):
        r = lax.rsqrt(jnp.mean(hv * hv, axis=-1, keepdims=True) + RMS_EPS)
        xh = hv * r
        row = pl.program_id(0) * tr + lax.broadcasted_iota(jnp.int32, (tr, 1), 0)
        e = jnp.where(row >= N_META, xh * gv - tv, 0.0)
        loss = jnp.broadcast_to(0.5 * jnp.sum(jnp.sum(e * e, axis=-1, keepdims=True), axis=0, keepdims=True) / d,
                                (8, LANES))
        dout = e / d
        dg = jnp.sum(dout * xh, axis=0, keepdims=True)
        dxh = dout * gv
        dx = r * (dxh - xh * jnp.mean(dxh * xh, axis=-1, keepdims=True))
        return loss, dx, dg

    row = lambda i: (i, 0)
    fix = lambda i: (0, 0)
    return _call(fn, name, (t // tr,), [(h, (tr, d), row), (g, (1, d), fix), (tgt, (tr, d), row)],
                 [((8, LANES), (8, LANES), fix, True), ((t, d), (tr, d), row, False), ((1, d), (1, d), fix, True)],
                 acc_axis=0)


def _silu_ln(uc, g, b):
    mu = jnp.mean(uc, axis=-1, keepdims=True)
    xc = uc - mu
    rs = lax.rsqrt(jnp.mean(xc * xc, axis=-1, keepdims=True) + LN_EPS)
    ln = xc * rs * g + b
    return ln * _sigmoid(ln)


def _even_ln_fwd(name, uc, g, b, tr):
    t, d = uc.shape
    row, fix = (lambda i: (i, 0)), (lambda i: (0, 0))
    return _call(_silu_ln, name, (t // tr,), [(uc, (tr, d), row), (g, (1, d), fix), (b, (1, d), fix)],
                 [((t, d), (tr, d), row, False, MXU_DTYPE)])


def _even_ln_bwd(name, uc, g, b, dy, dy_col, tr):
    t, d = uc.shape

    def fn(ucv, gv, bv, dyv):
        mu = jnp.mean(ucv, axis=-1, keepdims=True)
        xc = ucv - mu
        rs = lax.rsqrt(jnp.mean(xc * xc, axis=-1, keepdims=True) + LN_EPS)
        xh = xc * rs
        ln = xh * gv + bv
        s = _sigmoid(ln)
        dln = dyv * (s * (1.0 + ln * (1.0 - s)))
        dg = jnp.sum(dln * xh, axis=0, keepdims=True)
        db = jnp.sum(dln, axis=0, keepdims=True)
        dxh = dln * gv
        duc = rs * (dxh - jnp.mean(dxh, axis=-1, keepdims=True) - xh * jnp.mean(dxh * xh, axis=-1, keepdims=True))
        return duc, dg, db

    row, fix = (lambda i: (i, 0)), (lambda i: (0, 0))
    return _call(fn, name, (t // tr,),
                 [(uc, (tr, d), row), (g, (1, d), fix), (b, (1, d), fix), (dy, (tr, d), lambda i: (i, dy_col))],
                 [((t, d), (tr, d), row, False), ((1, d), (1, d), fix, True), ((1, d), (1, d), fix, True)], acc_axis=0)


def _windows(t):
    rc = _chunk_len(t)
    return [(r0, rc) for r0 in range(0, t, rc)]


def _taps(w_ref, width):
    return [w_ref[pl.ds(j, 1), :] for j in range(width)]


def _conv_at(xp, taps, r0, rc):
    width = len(taps)
    acc = None
    for j in range(width):
        term = xp[pl.ds(CONV_PAD - (width - 1) + j + r0, rc), :] * taps[j]
        acc = term if acc is None else acc + term
    return acc


def _conv_bwd_in_at(dyp, taps, r0, rc):
    width = len(taps)
    acc = None
    for j in range(width):
        term = dyp[pl.ds(width - 1 - j + r0, rc), :] * taps[j]
        acc = term if acc is None else acc + term
    return acc


def _fold(x):
    acc = x[0:8]
    for i in range(1, x.shape[0] // 8):
        acc = acc + x[8 * i:8 * (i + 1)]
    return acc


def _add_to(accs, vals):
    return vals if accs is None else [a + v for a, v in zip(accs, vals)]


def _conv_bwd_w_at(dy, xp, width, r0, rc):
    return [_fold(dy * xp[pl.ds(CONV_PAD - (width - 1) + j + r0, rc), :]) for j in range(width)]


def _store_taps(dw_ref, accs):
    for j, a in enumerate(accs):
        dw_ref[pl.ds(j, 1), :] = jnp.sum(a, axis=0, keepdims=True)


WIDE_COLS = 2 * LANES


def _zero_front(xp):
    xp[pl.ds(0, CONV_PAD), :] = jnp.zeros((CONV_PAD, xp.shape[1]), F32)


def _zero_back(dyp, t):
    dyp[pl.ds(t, CONV_PAD), :] = jnp.zeros((CONV_PAD, dyp.shape[1]), F32)


def _col_call(body, name, ncol, ins, outs, t, n_scratch, cols=LANES):
    def spec(rows, off):
        return pl.BlockSpec((rows, cols), lambda j, off=off: (0, j + off))

    res = pl.pallas_call(
        body, name=name, grid=(ncol,),
        in_specs=[spec(r, off) for _, r, off in ins],
        out_specs=[spec(o[0], 0) for o in outs],
        out_shape=[jax.ShapeDtypeStruct(o[:2], o[2] if len(o) > 2 else F32) for o in outs],
        scratch_shapes=[pltpu.VMEM((t + CONV_PAD, cols), F32) for _ in range(n_scratch)],
        compiler_params=_cparams(None),
    )(*[a for a, _, _ in ins])
    return res


def _even_col_fwd(name, p, conv_a, conv_b):
    t = p.shape[0]
    nc = D_A // LANES
    wins = _windows(t)

    def body(av, ag, gb, gc, xi, ca, cb, uc_ref, yb_ref, xp):
        _zero_front(xp)
        for r0, rc in wins:
            rows = pl.ds(r0, rc)
            xp[pl.ds(CONV_PAD + r0, rc), :] = av[rows, :] * _sigmoid(ag[rows, :])
        taps = _taps(ca, CONV_A_WIDTH)
        for r0, rc in wins:
            uc_ref[pl.ds(r0, rc), :] = _conv_at(xp, taps, r0, rc)
        for r0, rc in wins:
            rows = pl.ds(r0, rc)
            xp[pl.ds(CONV_PAD + r0, rc), :] = gc[rows, :] * xi[rows, :]
        taps = _taps(cb, CONV_B_WIDTH)
        for r0, rc in wins:
            rows = pl.ds(r0, rc)
            yb_ref[rows, :] = (gb[rows, :] * _conv_at(xp, taps, r0, rc)).astype(yb_ref.dtype)

    ins = [(p, t, k * nc) for k in range(5)] + [(conv_a, CONV_A_WIDTH, 0), (conv_b, CONV_B_WIDTH, 0)]
    return _col_call(body, name, nc, ins, [(t, D_A), (t, D_A, MXU_DTYPE)], t, 1)


def _even_col_bwd(name, p, duc, dy, conv_a, conv_b):
    t = p.shape[0]
    nc = D_A // LANES
    wins = _windows(t)

    def body(av, ag, gb, gc, xi, duc_ref, dyb_ref, ca, cb, dav, dag, dgb, dgc, dxi, dca, dcb, xp, dyp):
        _zero_front(xp)
        _zero_back(dyp, t)
        for r0, rc in wins:
            rows = pl.ds(r0, rc)
            xp[pl.ds(CONV_PAD + r0, rc), :] = av[rows, :] * _sigmoid(ag[rows, :])
            dyp[rows, :] = duc_ref[rows, :]
        taps = _taps(ca, CONV_A_WIDTH)
        accs = None
        for r0, rc in wins:
            rows = pl.ds(r0, rc)
            accs = _add_to(accs, _conv_bwd_w_at(duc_ref[rows, :], xp, CONV_A_WIDTH, r0, rc))
            du = _conv_bwd_in_at(dyp, taps, r0, rc)
            sig = _sigmoid(ag[rows, :])
            dav[rows, :] = (du * sig).astype(dav.dtype)
            dag[rows, :] = (du * av[rows, :] * sig * (1.0 - sig)).astype(dag.dtype)
        _store_taps(dca, accs)
        for r0, rc in wins:
            rows = pl.ds(r0, rc)
            xp[pl.ds(CONV_PAD + r0, rc), :] = gc[rows, :] * xi[rows, :]
        taps = _taps(cb, CONV_B_WIDTH)
        accs = None
        for r0, rc in wins:
            rows = pl.ds(r0, rc)
            dgb[rows, :] = (dyb_ref[rows, :] * _conv_at(xp, taps, r0, rc)).astype(dgb.dtype)
            dzc = dyb_ref[rows, :] * gb[rows, :]
            dyp[rows, :] = dzc
            accs = _add_to(accs, _conv_bwd_w_at(dzc, xp, CONV_B_WIDTH, r0, rc))
        _store_taps(dcb, accs)
        for r0, rc in wins:
            rows = pl.ds(r0, rc)
            dz = _conv_bwd_in_at(dyp, taps, r0, rc)
            dgc[rows, :] = (dz * xi[rows, :]).astype(dgc.dtype)
            dxi[rows, :] = (dz * gc[rows, :]).astype(dxi.dtype)

    ins = ([(p, t, k * nc) for k in range(5)] + [(duc, t, 0), (dy, t, nc)]
           + [(conv_a, CONV_A_WIDTH, 0), (conv_b, CONV_B_WIDTH, 0)])
    outs = [(t, D_A, MXU_DTYPE)] * 5 + [(CONV_A_WIDTH, D_A), (CONV_B_WIDTH, D_A)]
    return _col_call(body, name, nc, ins, outs, t, 2)


def _ffn_col_fwd(name, u, conv, bias):
    t = u.shape[0]
    nc = D_FF // WIDE_COLS
    wins = _windows(t)

    def body(g_ref, v_ref, cw, b_ref, a_ref, xp):
        _zero_front(xp)
        xp[pl.ds(CONV_PAD, t), :] = g_ref[...].astype(F32)
        taps = _taps(cw, CONV_B_WIDTH)
        b = b_ref[...]
        for r0, rc in wins:
            rows = pl.ds(r0, rc)
            gc = _conv_at(xp, taps, r0, rc) + b
            a_ref[rows, :] = (gc * _sigmoid(gc) * v_ref[rows, :].astype(F32)).astype(a_ref.dtype)

    ins = [(u, t, 0), (u, t, nc), (conv, CONV_B_WIDTH, 0), (bias, 1, 0)]
    return _col_call(body, name, nc, ins, [(t, D_FF, MXU_DTYPE)], t, 1, cols=WIDE_COLS)[0]


def _ffn_col_bwd(name, u, da, conv, bias):
    t = u.shape[0]
    nc = D_FF // LANES
    wins = _windows(t)

    def body(g_ref, v_ref, da_ref, cw, b_ref, du_ref, dcw, db_ref, xp, dyp, dval):
        @pl.when(pl.program_id(1) == 0)
        def _():
            _zero_front(xp)
            _zero_back(dyp, t)
            xp[pl.ds(CONV_PAD, t), :] = g_ref[...].astype(F32)
            taps = _taps(cw, CONV_B_WIDTH)
            b = b_ref[...]
            accs, bias_acc = None, None
            for r0, rc in wins:
                rows = pl.ds(r0, rc)
                gc = _conv_at(xp, taps, r0, rc) + b
                s = _sigmoid(gc)
                d = da_ref[rows, :]
                dval[rows, :] = d * gc * s
                dgc = d * v_ref[rows, :].astype(F32) * (s * (1.0 + gc * (1.0 - s)))
                dyp[rows, :] = dgc
                bias_acc = _add_to(bias_acc, [_fold(dgc)])
                accs = _add_to(accs, _conv_bwd_w_at(dgc, xp, CONV_B_WIDTH, r0, rc))
            db_ref[...] = jnp.sum(bias_acc[0], axis=0, keepdims=True)
            _store_taps(dcw, accs)
            for r0, rc in wins:
                du_ref[pl.ds(r0, rc), :] = _conv_bwd_in_at(dyp, taps, r0, rc).astype(du_ref.dtype)

        @pl.when(pl.program_id(1) == 1)
        def _():
            du_ref[...] = dval[...].astype(du_ref.dtype)

    col = lambda rows, off: pl.BlockSpec((rows, LANES), lambda j, p: (0, j + off))
    return pl.pallas_call(
        body, name=name, grid=(nc, 2),
        in_specs=[col(t, 0), col(t, nc), col(t, 0), col(CONV_B_WIDTH, 0), col(1, 0)],
        out_specs=[pl.BlockSpec((t, LANES), lambda j, p: (0, j + nc * p)), col(CONV_B_WIDTH, 0), col(1, 0)],
        out_shape=[jax.ShapeDtypeStruct((t, 2 * D_FF), MXU_DTYPE), jax.ShapeDtypeStruct((CONV_B_WIDTH, D_FF), F32),
                   jax.ShapeDtypeStruct((1, D_FF), F32)],
        scratch_shapes=[pltpu.VMEM((t + CONV_PAD, LANES), F32) for _ in range(2)] + [pltpu.VMEM((t, LANES), F32)],
        compiler_params=_cparams(None),
    )(u, u, da, conv, bias)


def _shift_fwd(name, p, col0, mu):
    t = p.shape[0]
    wins = _windows(t)

    def body(x_ref, mu_ref, o_ref, xp):
        _zero_front(xp)
        xp[pl.ds(CONV_PAD, t), :] = x_ref[...]
        mu_v = mu_ref[...]
        for r0, rc in wins:
            rows = pl.ds(r0, rc)
            x = x_ref[rows, :]
            o_ref[rows, :] = x + (xp[pl.ds(CONV_PAD - 1 + r0, rc), :] - x) * mu_v

    return _col_call(body, name, RWKV_COLS // WIDE_COLS, [(p, t, col0 // WIDE_COLS), (mu, 1, 0)], [(t, RWKV_COLS)], t, 1,
                     cols=WIDE_COLS)[0]


def _shift_bwd(name, p, col0, mu, dprs):
    t = p.shape[0]
    wins = _windows(t)

    def body(x_ref, mu_ref, d_ref, dx_ref, dmu_ref, xp, dyp):
        _zero_front(xp)
        _zero_back(dyp, t)
        xp[pl.ds(CONV_PAD, t), :] = x_ref[...]
        mu_v = mu_ref[...]
        acc = None
        for r0, rc in wins:
            rows = pl.ds(r0, rc)
            d = d_ref[rows, :]
            acc = _add_to(acc, [_fold(d * (xp[pl.ds(CONV_PAD - 1 + r0, rc), :] - x_ref[rows, :]))])
            dyp[rows, :] = d * mu_v
        dmu_ref[...] = jnp.sum(acc[0], axis=0, keepdims=True)
        for r0, rc in wins:
            rows = pl.ds(r0, rc)
            dx_ref[rows, :] = d_ref[rows, :] - dyp[rows, :] + dyp[pl.ds(1 + r0, rc), :]

    ins = [(p, t, col0 // WIDE_COLS), (mu, 1, 0), (dprs, t, 0)]
    return _col_call(body, name, RWKV_COLS // WIDE_COLS, ins, [(t, RWKV_COLS), (1, RWKV_COLS)], t, 2, cols=WIDE_COLS)


def _hi_lo(x):
    hi = x.astype(BF16)
    return hi, (x - hi.astype(F32)).astype(BF16)


def _dot_passes(a, b, dims, passes):
    d = lambda p, q: lax.dot_general(p, q, dims, preferred_element_type=F32)
    if passes == 1:
        return d(a.astype(MXU_DTYPE), b.astype(MXU_DTYPE))
    ah, al = _hi_lo(a)
    bh, bl = _hi_lo(b)
    return d(ah, bh) + (d(ah, bl) + d(al, bh))


@functools.partial(jax.custom_vjp, nondiff_argnums=(2, 3))
def _dot_vjp(a, b, dims, passes):
    return _dot_passes(a, b, dims, passes)


def _dot_fwd(a, b, dims, passes):
    return _dot_passes(a, b, dims, passes), (a, b)


def _dot_bwd(dims, passes, res, g):
    a, b = res
    if dims == _NN:
        return _dot_passes(g, b, _NT, passes), _dot_passes(a, g, _TN, passes)
    if dims == _NT:
        return _dot_passes(g, b, _NN, passes), _dot_passes(g, a, _TN, passes)
    return _dot_passes(b, g, _NT, passes), _dot_passes(a, g, _NN, passes)


_dot_vjp.defvjp(_dot_fwd, _dot_bwd)


def _doth(a, b, dims=_NN):
    return _dot_vjp(a, b, dims, 3)


def _dotb(a, b, dims=_NN):
    return _dot_vjp(a, b, dims, 1)


def _softplus(x):
    return jnp.where(x > 0, x, 0.0) + jnp.log(1.0 + jnp.exp(jnp.where(x > 0, -x, x)))


def _rwkv_pre(k, xl, gd, w0, w2p, a0, a2p, g2, k_k, k_a, seg):
    z = w0 + _dotb(jnp.tanh(xl), w2p)
    lw = -jnp.exp(-_softplus(-z) - 0.5)
    alpha = _sigmoid(a0 + _dotb(xl, a2p))
    g = _dotb(_sigmoid(gd), g2)
    kk = k * k_k
    kk = kk / jnp.maximum(jnp.sqrt(_dotb(kk * kk, seg)), 1e-12)
    k2 = k * (1.0 + (alpha - 1.0) * k_a)
    return lw, k2, -kk, kk * alpha, g


def _rwkv_post(y, r, k2, v, g, lnx_g, lnx_b, r_k, seg):
    mean = _dotb(y, seg) * (1.0 / HEAD_DIM)
    yc = y - mean
    var = _dotb(yc * yc, seg) * (1.0 / HEAD_DIM)
    yo = yc * lax.rsqrt(var + RWKV_GN_EPS) * lnx_g + lnx_b
    bonus = _dotb(r * k2 * r_k, seg) * v
    return (yo + bonus) * g


def _rwkv_pre_fwd(name, prs, prm, seg, tr):
    t = prs.shape[0]
    row = lambda i: (i, 0)
    fix = lambda i: (0, 0)
    ins = [(prs, (tr, D_R), lambda i: (i, 1)), (prs, (tr, LANES), lambda i: (i, 12)), (prs, (tr, LANES), lambda i: (i, 13)),
           (prm["w0"], (1, D_R), fix), (prm["w2p"], (LANES, D_R), fix), (prm["a0"], (1, D_R), fix),
           (prm["a2p"], (LANES, D_R), fix), (prm["g2"], (LANES, D_R), fix), (prm["k_k"], (1, D_R), fix),
           (prm["k_a"], (1, D_R), fix), (seg, (D_R, D_R), fix)]
    return _call(_rwkv_pre, name, (t // tr,), ins, [((t, D_R), (tr, D_R), row, False)] * 5)


def _rwkv_pre_bwd(name, prs, prm, seg, cts, tr):
    t = prs.shape[0]

    def fn(k, xl, gd, w0, w2p, a0, a2p, g2, k_k, k_a, segv, *ct):
        _, vjp = jax.vjp(lambda *a: _rwkv_pre(*a, segv), k, xl, gd, w0, w2p, a0, a2p, g2, k_k, k_a)
        return vjp(tuple(ct))

    row = lambda i: (i, 0)
    fix = lambda i: (0, 0)
    ins = [(prs, (tr, D_R), lambda i: (i, 1)), (prs, (tr, LANES), lambda i: (i, 12)), (prs, (tr, LANES), lambda i: (i, 13)),
           (prm["w0"], (1, D_R), fix), (prm["w2p"], (LANES, D_R), fix), (prm["a0"], (1, D_R), fix),
           (prm["a2p"], (LANES, D_R), fix), (prm["g2"], (LANES, D_R), fix), (prm["k_k"], (1, D_R), fix),
           (prm["k_a"], (1, D_R), fix), (seg, (D_R, D_R), fix)] + [(c, (tr, D_R), row) for c in cts]
    outs = [((t, D_R), (tr, D_R), row, False), ((t, LANES), (tr, LANES), row, False), ((t, LANES), (tr, LANES), row, False),
            ((1, D_R), (1, D_R), fix, True), ((LANES, D_R), (LANES, D_R), fix, True), ((1, D_R), (1, D_R), fix, True),
            ((LANES, D_R), (LANES, D_R), fix, True), ((LANES, D_R), (LANES, D_R), fix, True),
            ((1, D_R), (1, D_R), fix, True), ((1, D_R), (1, D_R), fix, True)]
    return _call(fn, name, (t // tr,), ins, outs, acc_axis=0)


def _rwkv_post_ins(y, prs, k2, g, prm, seg, tr):
    row = lambda i: (i, 0)
    fix = lambda i: (0, 0)
    return [(y, (tr, D_R), row), (prs, (tr, D_R), row), (k2, (tr, D_R), row), (prs, (tr, D_R), lambda i: (i, 2)),
            (g, (tr, D_R), row), (prm["lnx_g"], (1, D_R), fix), (prm["lnx_b"], (1, D_R), fix), (prm["r_k"], (1, D_R), fix),
            (seg, (D_R, D_R), fix)]


def _rwkv_post_fwd(name, y, prs, k2, g, prm, seg, tr):
    t = y.shape[0]
    return _call(_rwkv_post, name, (t // tr,), _rwkv_post_ins(y, prs, k2, g, prm, seg, tr),
                 [((t, D_R), (tr, D_R), lambda i: (i, 0), False)])


def _rwkv_post_bwd(name, y, prs, k2, g, prm, seg, dy, dy_col, tr):
    t = y.shape[0]

    def fn(yv, r, k2v, v, gv, lg, lb, rk, segv, ct):
        _, vjp = jax.vjp(lambda *a: _rwkv_post(*a, segv), yv, r, k2v, v, gv, lg, lb, rk)
        return vjp(ct)

    row = lambda i: (i, 0)
    fix = lambda i: (0, 0)
    ins = _rwkv_post_ins(y, prs, k2, g, prm, seg, tr) + [(dy, (tr, D_R), lambda i: (i, dy_col))]
    outs = [((t, D_R), (tr, D_R), row, False)] * 5 + [((1, D_R), (1, D_R), fix, True)] * 3
    return _call(fn, name, (t // tr,), ins, outs, acc_axis=0)


def _wkv_chunk(s0, r, lw, k, v, a, b):
    c = r[0].shape[0]
    lane = lax.broadcasted_iota(jnp.int32, (1, 2 * HEAD_DIM), 1)
    first = (lane < HEAD_DIM).astype(F32)
    per_head = lambda x: jnp.concatenate([x * first, x * (1.0 - first)], axis=0)

    def time_of(shape, dim):
        i = lax.broadcasted_iota(jnp.int32, shape, dim)
        return jnp.where(i >= c, i - c, i)

    incl = (lax.broadcasted_iota(jnp.int32, (c, c), 0) >= lax.broadcasted_iota(jnp.int32, (c, c), 1)).astype(F32)
    strict2 = time_of((2 * c, 2 * c), 0) > time_of((2 * c, 2 * c), 1)
    incl2 = lax.broadcasted_iota(jnp.int32, (c, 2 * c), 0) >= time_of((c, 2 * c), 1)
    each = lambda f, *xs: [f(*x) for x in zip(*xs)]
    cum = each(lambda x: _doth(incl, x), lw)
    tot = each(lambda x: jnp.sum(x, axis=0, keepdims=True), lw)
    e_inv = each(lambda x: jnp.exp(-x), cum)
    a_st = each(lambda x, cm, l: per_head(x * jnp.exp(cm - l)), a, cum, lw)
    r_t = each(lambda x, cm: x * jnp.exp(cm), r, cum)
    b_st = each(lambda x, e: per_head(x * e), b, e_inv)
    k_st = each(lambda x, e: per_head(x * e), k, e_inv)
    v_st = each(per_head, v)
    m = each(lambda x, w: jnp.where(strict2, _dotb(x, w, _NT), 0.0), a_st, b_st)
    m_k = each(lambda x, w: jnp.where(strict2, _dotb(x, w, _NT), 0.0), a_st, k_st)
    u = each(lambda x, s, mk, w: _dotb(x, s, _NT) + _dotb(mk, w), a_st, s0, m_k, v_st)
    steps = (c - 1).bit_length()
    for s in range(steps):
        u = each(lambda x, w: x + _dotb(w, x), u, m)
        if s + 1 < steps:
            m = each(lambda w: _dotb(w, w), m)
    n_b = each(lambda x, w: jnp.where(incl2, _dotb(x, w, _NT), 0.0), r_t, b_st)
    n_k = each(lambda x, w: jnp.where(incl2, _dotb(x, w, _NT), 0.0), r_t, k_st)
    y = each(lambda x, s, nb, uu, nk, w: _dotb(x, s, _NT) + _dotb(nb, uu) + _dotb(nk, w), r_t, s0, n_b, u, n_k, v_st)
    dec = each(lambda tt, cm: jnp.exp(tt - cm), tot, cum)
    s1 = each(lambda s, tt, uu, x, d, w, kk: s * jnp.exp(tt) + _dotb(uu, per_head(x * d), _TN) + _dotb(w, per_head(kk * d), _TN),
              s0, tot, u, b, dec, v_st, k)
    return tuple(y), tuple(s1)


WKV_PAIRS_PER_STEP = 4
PAIR = 2 * HEAD_DIM


def _wkv_fwd(name, srcs):
    t = srcs[0][0].shape[0]
    c = _chunk_len(t)
    nc = t // c
    pp = WKV_PAIRS_PER_STEP
    n_pairs = D_R // PAIR

    def body(r, lw, k, v, a, b, y_ref, st_ref, state):
        @pl.when(pl.program_id(1) == 0)
        def _():
            state[...] = jnp.zeros(state.shape, F32)

        pairs = lambda ref: tuple(ref[:, pl.ds(i * PAIR, PAIR)] for i in range(pp))
        s0 = tuple(state[i] for i in range(pp))
        y, s1 = _wkv_chunk(s0, pairs(r), pairs(lw), pairs(k), pairs(v), pairs(a), pairs(b))
        for i in range(pp):
            st_ref[i] = s0[i]
            y_ref[:, pl.ds(i * PAIR, PAIR)] = y[i]
            state[i] = s1[i]

    seq = lambda off: pl.BlockSpec((c, pp * PAIR), lambda g, j: (j, off + g))
    return pl.pallas_call(
        body, name=name, grid=(n_pairs // pp, nc), in_specs=[seq(off) for _, off in srcs],
        out_specs=[seq(0), pl.BlockSpec((pp, None, PAIR, PAIR), lambda g, j: (g, j, 0, 0))],
        out_shape=[jax.ShapeDtypeStruct((t, D_R), F32), jax.ShapeDtypeStruct((n_pairs, nc, PAIR, PAIR), F32)],
        scratch_shapes=[pltpu.VMEM((pp, PAIR, PAIR), F32)],
        compiler_params=_cparams(None),
    )(*[a for a, _ in srcs])


def _wkv_bwd(name, srcs, st, dy):
    t = srcs[0][0].shape[0]
    c = _chunk_len(t)
    nc = t // c
    pp = WKV_PAIRS_PER_STEP
    n_pairs = D_R // PAIR

    def body(r, lw, k, v, a, b, st_ref, dy_ref, dr, dlw, dk, dv, da, db, dstate):
        @pl.when(pl.program_id(1) == 0)
        def _():
            dstate[...] = jnp.zeros(dstate.shape, F32)

        half = lax.broadcasted_iota(jnp.int32, (PAIR, PAIR), 0) < HEAD_DIM
        same_head = half == (lax.broadcasted_iota(jnp.int32, (PAIR, PAIR), 1) < HEAD_DIM)
        pairs = lambda ref: tuple(ref[:, pl.ds(i * PAIR, PAIR)] for i in range(pp))
        s0 = tuple(st_ref[i] for i in range(pp))
        _, vjp = jax.vjp(_wkv_chunk, s0, pairs(r), pairs(lw), pairs(k), pairs(v), pairs(a), pairs(b))
        ds0, *dxs = vjp((pairs(dy_ref), tuple(dstate[i] for i in range(pp))))
        for i in range(pp):
            for ref, val in zip((dr, dlw, dk, dv, da, db), dxs):
                ref[:, pl.ds(i * PAIR, PAIR)] = val[i]
            dstate[i] = jnp.where(same_head, ds0[i], 0.0)

    seq = lambda off: pl.BlockSpec((c, pp * PAIR), lambda g, j: (nc - 1 - j, off + g))
    return pl.pallas_call(
        body, name=name, grid=(n_pairs // pp, nc),
        in_specs=[seq(off) for _, off in srcs]
        + [pl.BlockSpec((pp, None, PAIR, PAIR), lambda g, j: (g, nc - 1 - j, 0, 0)), seq(dy[1])],
        out_specs=[seq(0)] * 6,
        out_shape=[jax.ShapeDtypeStruct((t, D_R), F32)] * 6,
        scratch_shapes=[pltpu.VMEM((pp, PAIR, PAIR), F32)],
        compiler_params=_cparams(None),
    )(*[a for a, _ in srcs], st, dy[0])


def _rope(x, cos, sin, rot):
    return x * cos + _dotb(x, rot) * sin


def _attn_block(nb, q, kp, kc, km, vp, vc, vm, sk, cq, sq, cp, sp, cm, sm, rot):
    g = GQA_GROUP
    scale = HEAD_DIM ** -0.5
    each = lambda f, *xs: [f(*x) for x in zip(*xs)]
    down = lambda x: jnp.concatenate([x] * g, axis=0)
    cq4, sq4 = down(cq), down(sq)
    kpr = each(lambda x: _rope(x, cp, sp, rot), kp)
    kcr = each(lambda x: _rope(x, cq, sq, rot), kc)
    kmr = each(lambda x: _rope(x, cm, sm, rot), km)
    qr = each(lambda x: _rope(x, cq4, sq4, rot), q)
    i = lax.broadcasted_iota(jnp.int32, (g * BLOCK, BLOCK), 0)
    i = i - BLOCK * ((i >= BLOCK).astype(jnp.int32) + (i >= 2 * BLOCK).astype(jnp.int32) + (i >= 3 * BLOCK).astype(jnp.int32))
    j = lax.broadcasted_iota(jnp.int32, (g * BLOCK, BLOCK), 1)
    nbv = jnp.zeros((g * BLOCK, BLOCK), jnp.int32) + nb
    ok_p = (j > i) & (nbv >= 2)
    ok_c = (j <= i) & (nbv >= 1)
    ok_m = (j >= BLOCK - N_META) & ((nbv >= 1) | (j <= i))
    sink = each(lambda s4: jnp.concatenate([jnp.broadcast_to(s, (BLOCK, 1)) for s in s4], axis=0), sk)
    s_p = each(lambda x, kk: jnp.where(ok_p, _dotb(x, kk, _NT) * scale, NEG_INF), qr, kpr)
    s_c = each(lambda x, kk: jnp.where(ok_c, _dotb(x, kk, _NT) * scale, NEG_INF), qr, kcr)
    s_m = each(lambda x, kk: jnp.where(ok_m, _dotb(x, kk, _NT) * scale, NEG_INF), qr, kmr)
    rmax = lambda s: jnp.max(s, axis=-1, keepdims=True)
    m = each(lambda a, b, c, d: lax.stop_gradient(jnp.maximum(jnp.maximum(rmax(a), rmax(b)), jnp.maximum(rmax(c), d))),
             s_p, s_c, s_m, sink)
    e_p = each(lambda s, mm: jnp.exp(s - mm), s_p, m)
    e_c = each(lambda s, mm: jnp.exp(s - mm), s_c, m)
    e_m = each(lambda s, mm: jnp.exp(s - mm), s_m, m)
    rsum = lambda e: jnp.sum(e, axis=-1, keepdims=True)
    inv = each(lambda a, b, c, d, mm: 1.0 / (rsum(a) + rsum(b) + rsum(c) + jnp.exp(d - mm)), e_p, e_c, e_m, sink, m)
    return tuple(each(lambda a, b, c, iv, x, y, z: _dotb(a * iv, x) + _dotb(b * iv, y) + _dotb(c * iv, z),
                      e_p, e_c, e_m, inv, vp, vc, vm))


def _attn_specs():
    cur = lambda n: (0, n, 0)
    prev = lambda n: (0, jnp.maximum(n - 1, 0), 0)
    meta = lambda n: (0, 0, 0)
    kv = lambda m: pl.BlockSpec((N_KV_HEADS, BLOCK, HEAD_DIM), m)
    tab = lambda m: pl.BlockSpec((BLOCK, HEAD_DIM), m)
    tcur, tprev, tmeta = (lambda n: (n, 0)), (lambda n: (jnp.maximum(n - 1, 0), 0)), (lambda n: (0, 0))
    qspec = pl.BlockSpec((N_Q_HEADS, BLOCK, HEAD_DIM), cur)
    sspec = pl.BlockSpec((N_Q_HEADS, 8, LANES), meta)
    specs = [qspec, kv(prev), kv(cur), kv(meta), kv(prev), kv(cur), kv(meta), sspec,
             tab(tcur), tab(tcur), tab(tprev), tab(tprev), tab(tmeta), tab(tmeta),
             pl.BlockSpec((HEAD_DIM, HEAD_DIM), lambda n: (0, 0))]
    return specs, qspec, sspec, kv


def _attn_args(q, k, v, sinks_b, cos, sin, rot):
    return (q, k, k, k, v, v, v, sinks_b, cos, sin, cos, sin, cos, sin, rot)


def _attn_operands(q_ref, kp, kc, km, vp, vc, vm, s_ref):
    groups = range(N_KV_HEADS)
    q = tuple(jnp.concatenate([q_ref[GQA_GROUP * i + h] for h in range(GQA_GROUP)], axis=0) for i in groups)
    sk = tuple(tuple(s_ref[GQA_GROUP * i + h][0:1, 0:1] for h in range(GQA_GROUP)) for i in groups)
    per_head = lambda ref: tuple(ref[i] for i in groups)
    return q, per_head(kp), per_head(kc), per_head(km), per_head(vp), per_head(vc), per_head(vm), sk


def _attn_fwd(name, q, k, v, sinks_b, cos, sin, rot):
    tp = q.shape[1]
    specs, qspec, _, _ = _attn_specs()

    def body(q_ref, kp, kc, km, vp, vc, vm, s_ref, cq, sq, cp, sp, cm, sm, rot_ref, o_ref):
        out = _attn_block(pl.program_id(0), *_attn_operands(q_ref, kp, kc, km, vp, vc, vm, s_ref),
                          cq[...], sq[...], cp[...], sp[...], cm[...], sm[...], rot_ref[...])
        for i in range(N_KV_HEADS):
            for h in range(GQA_GROUP):
                o_ref[GQA_GROUP * i + h] = out[i][h * BLOCK:(h + 1) * BLOCK]

    return pl.pallas_call(
        body, name=name, grid=(tp // BLOCK,), in_specs=specs, out_specs=qspec,
        out_shape=jax.ShapeDtypeStruct(q.shape, F32), compiler_params=_cparams(None),
    )(*_attn_args(q, k, v, sinks_b, cos, sin, rot))


def _attn_bwd(name, q, k, v, sinks_b, cos, sin, rot, do):
    tp = q.shape[1]
    nb = tp // BLOCK
    specs, qspec, sspec, kv = _attn_specs()

    def body(q_ref, kp, kc, km, vp, vc, vm, s_ref, cq, sq, cp, sp, cm, sm, rot_ref, do_ref,
             dq_ref, dkp, dkc, dvp, dvc, dkm, dvm, ds_ref):
        n = pl.program_id(0)
        tabs = (cq[...], sq[...], cp[...], sp[...], cm[...], sm[...], rot_ref[...])
        _, vjp = jax.vjp(lambda *a: _attn_block(n, *a, *tabs), *_attn_operands(q_ref, kp, kc, km, vp, vc, vm, s_ref))
        do_all = tuple(jnp.concatenate([do_ref[GQA_GROUP * i + h] for h in range(GQA_GROUP)], axis=0)
                       for i in range(N_KV_HEADS))
        dq, gkp, gkc, gkm, gvp, gvc, gvm, dsk = vjp(do_all)
        for i in range(N_KV_HEADS):
            dkp[i] = gkp[i]
            dkc[i] = gkc[i]
            dvp[i] = gvp[i]
            dvc[i] = gvc[i]
            for h in range(GQA_GROUP):
                dq_ref[GQA_GROUP * i + h] = dq[i][h * BLOCK:(h + 1) * BLOCK]

        @pl.when(n == 0)
        def _():
            for i in range(N_KV_HEADS):
                dkm[i] = gkm[i]
                dvm[i] = gvm[i]
                for h in range(GQA_GROUP):
                    ds_ref[GQA_GROUP * i + h] = jnp.broadcast_to(dsk[i][h], (8, LANES))

        @pl.when(n != 0)
        def _():
            for i in range(N_KV_HEADS):
                dkm[i] += gkm[i]
                dvm[i] += gvm[i]
                for h in range(GQA_GROUP):
                    ds_ref[GQA_GROUP * i + h] += jnp.broadcast_to(dsk[i][h], (8, LANES))

    part = pl.BlockSpec((N_KV_HEADS, None, BLOCK, HEAD_DIM), lambda n: (0, n, 0, 0))
    part_shape = jax.ShapeDtypeStruct((N_KV_HEADS, nb, BLOCK, HEAD_DIM), F32)
    meta_shape = jax.ShapeDtypeStruct((N_KV_HEADS, BLOCK, HEAD_DIM), F32)
    return pl.pallas_call(
        body, name=name, grid=(nb,), in_specs=specs + [qspec],
        out_specs=[qspec, part, part, part, part, kv(lambda n: (0, 0, 0)), kv(lambda n: (0, 0, 0)), sspec],
        out_shape=[jax.ShapeDtypeStruct(q.shape, F32), part_shape, part_shape, part_shape, part_shape,
                   meta_shape, meta_shape, jax.ShapeDtypeStruct(sinks_b.shape, F32)],
        compiler_params=_cparams(None),
    )(*_attn_args(q, k, v, sinks_b, cos, sin, rot), do)


def _kv_combine(name, k_parts, v_parts):
    g, nb = k_parts[1].shape[:2]

    def fn(own_k, nxt_k, mt_k, own_v, nxt_v, mt_v):
        m = pl.program_id(1)
        one = jnp.ones((BLOCK, HEAD_DIM), F32)
        use_next = jnp.where(one * m < nb - 1, 1.0, 0.0)
        use_meta = jnp.where(one * m < 1, 1.0, 0.0)
        return own_k + nxt_k * use_next + mt_k * use_meta, own_v + nxt_v * use_next + mt_v * use_meta

    blk = (None, None, BLOCK, HEAD_DIM)
    ins = []
    for prev_part, own_part, meta in (k_parts, v_parts):
        ins += [(own_part, blk, lambda a, m: (a, m, 0, 0)),
                (prev_part, blk, lambda a, m: (a, jnp.minimum(m + 1, nb - 1), 0, 0)),
                (meta, (None, BLOCK, HEAD_DIM), lambda a, m: (a, 0, 0))]
    return _call(fn, name, (g, nb), ins,
                 [((g, nb * BLOCK, HEAD_DIM), (None, BLOCK, HEAD_DIM), lambda a, m: (a, m, 0), False)] * 2)


PACK_W = 1024
ELEMENTWISE_BLOCK_BYTES = 1 << 21


def _rows_tile(rows, cols):
    cap = max(8, ELEMENTWISE_BLOCK_BYTES // (4 * cols))
    for d in range(min(rows, cap), 0, -1):
        if rows % d == 0 and d % 8 == 0:
            return d
    return rows


def _adamw(name, w, g, m, v):
    rows, cols = w.shape
    tr = _rows_tile(rows, cols)

    def fn(wv, gv, mv, vv):
        m1 = ADAM_B1 * mv + (1.0 - ADAM_B1) * gv
        v1 = ADAM_B2 * vv + (1.0 - ADAM_B2) * (gv * gv)
        m_hat = m1 / (1.0 - ADAM_B1 ** ADAM_STEP)
        v_hat = v1 / (1.0 - ADAM_B2 ** ADAM_STEP)
        return -ADAM_LR * (m_hat / (jnp.sqrt(v_hat) + ADAM_EPS) + ADAM_WD * wv), m1, v1

    blk = (tr, cols)
    row = lambda i: (i, 0)
    return _call(fn, name, (rows // tr,), [(a, blk, row) for a in (w, g, m, v)], [((rows, cols), blk, row, False)] * 3)


def _pair_add_placed(name, g, recv, cm_idx, out_dtype):
    s, a, b = g.shape
    half = a // 2

    def body(cm_ref, a_ref, b_ref, o_ref, own_ref):
        val = (a_ref[...] + b_ref[...]).astype(out_dtype)
        o_ref[...] = val

        @pl.when(pl.program_id(0) == cm_ref[1])
        def _():
            own_ref[...] = val

    blk = (None, half, b)
    shape = jax.ShapeDtypeStruct((s, half, b), out_dtype)
    return pl.pallas_call(
        body, name=name,
        grid_spec=pltpu.PrefetchScalarGridSpec(
            num_scalar_prefetch=1, grid=(s,),
            in_specs=[pl.BlockSpec(blk, lambda j, cm: (j, cm[0], 0)), pl.BlockSpec(blk, lambda j, cm: (j, 0, 0))],
            out_specs=[pl.BlockSpec(blk, lambda j, cm: (j, 0, 0)), pl.BlockSpec(blk, lambda j, cm: (cm[1], 0, 0))]),
        out_shape=[shape, shape], compiler_params=_cparams(None),
    )(cm_idx, g, recv)


def _sum_chips(name, parts, c_idx, layer, n_layers, into=None):
    _, a, b = parts.shape
    tr = _rows_tile(a, b)

    def body(c_ref, p0, p1, p2, p3, *rest):
        o_ref = rest[-1]
        up = lambda p: p[...].astype(F32)
        o_ref[...] = ((up(p0) + up(p1)) + up(p2)) + up(p3)

    in_specs = [pl.BlockSpec((None, tr, b), lambda i, c, k=k: (k, i, 0)) for k in range(N_CHIPS)]
    args = [c_idx] + [parts] * N_CHIPS
    aliases = {}
    if into is not None:
        in_specs.append(_ANY)
        args.append(into)
        aliases = {1 + N_CHIPS: 0}
    return pl.pallas_call(
        body, name=name,
        grid_spec=pltpu.PrefetchScalarGridSpec(
            num_scalar_prefetch=1, grid=(a // tr,), in_specs=in_specs,
            out_specs=pl.BlockSpec((None, None, tr, b), lambda i, c: (layer, c[0], i, 0))),
        out_shape=jax.ShapeDtypeStruct((n_layers, 2, a, b), F32), input_output_aliases=aliases,
        compiler_params=_cparams(None),
    )(*args)


def _place_own_block(name, w, layer, me_idx, dtype):
    _, a2, b = w.shape
    a = a2 // 2
    tr = _rows_tile(a, b)
    nb = a // tr

    def body(me_ref, w_ref, o_ref):
        o_ref[...] = w_ref[...].astype(dtype)

    return pl.pallas_call(
        body, name=name,
        grid_spec=pltpu.PrefetchScalarGridSpec(
            num_scalar_prefetch=1, grid=(2, nb),
            in_specs=[pl.BlockSpec((None, tr, b), lambda h, i, me: (layer, h * nb + i, 0))],
            out_specs=pl.BlockSpec((None, None, tr, b), lambda h, i, me: (me[0], h, i, 0))),
        out_shape=jax.ShapeDtypeStruct((N_CHIPS, 2, a, b), dtype), compiler_params=_cparams(None),
    )(me_idx, w)


def _mesh_pos():
    return lax.axis_index("x"), lax.axis_index("y"), lax.axis_index("c")


def _other_chips(x, y):
    return [(1 - x, y), (x, 1 - y), (1 - x, 1 - y)]


_ANY = pl.BlockSpec(memory_space=pl.ANY)


def _gather_weights(name, bufs, from_chips=True):
    n = len(bufs)

    def body(*refs):
        out_refs = refs[n:2 * n]
        send_sems, recv_sems = refs[2 * n:]
        x, y, c = _mesh_pos()
        me = 2 * x + y
        sibling = (x, y, 1 - c)
        chips = _other_chips(x, y)

        def copy(i, k, chip_idx, half, to):
            return pltpu.make_async_remote_copy(src_ref=out_refs[i].at[chip_idx, half], dst_ref=out_refs[i].at[chip_idx, half],
                                                send_sem=send_sems.at[6 * i + k], recv_sem=recv_sems.at[6 * i + k],
                                                device_id=to, device_id_type=MESH)

        first = [copy(i, j, me, c, (*chip, c)) for i in range(n) for j, chip in enumerate(chips)] if from_chips else []
        for cp in first:
            cp.start()
        passed = []
        for i in range(n):
            for j, (cx, cy) in enumerate(chips):
                idx = 2 * cx + cy
                if from_chips:
                    copy(i, j, idx, c, sibling).wait_recv()
                fwd = copy(i, 3 + j, idx, c, sibling)
                fwd.start()
                passed.append(fwd)
        for i in range(n):
            for j, (cx, cy) in enumerate(chips):
                copy(i, 3 + j, 2 * cx + cy, 1 - c, sibling).wait_recv()
        for cp in first + passed:
            cp.wait_send()

    return pl.pallas_call(
        body, name=name, in_specs=[_ANY] * n, out_specs=[_ANY] * n,
        out_shape=[jax.ShapeDtypeStruct(b.shape, b.dtype) for b in bufs],
        input_output_aliases={i: i for i in range(n)},
        scratch_shapes=[pltpu.SemaphoreType.DMA((6 * n,)), pltpu.SemaphoreType.DMA((6 * n,))],
        compiler_params=pltpu.CompilerParams(has_side_effects=True),
    )(*bufs)


def _gather_start(name, groups):
    bufs = [b for g in groups for b in g]
    n = len(bufs)
    ng = len(groups)

    def body(*refs):
        b_refs = refs[:n]
        sems = refs[n:n + 2 * ng]
        token = refs[-1]
        x, y, c = _mesh_pos()
        me = 2 * x + y
        i = 0
        for gi, g in enumerate(groups):
            for k in range(len(g)):
                for j, (cx, cy) in enumerate(_other_chips(x, y)):
                    pltpu.make_async_remote_copy(src_ref=b_refs[i].at[me, c], dst_ref=b_refs[i].at[me, c],
                                                 send_sem=sems[2 * gi].at[3 * k + j], recv_sem=sems[2 * gi + 1].at[3 * k + j],
                                                 device_id=(cx, cy, c), device_id_type=MESH).start()
                i += 1
        token[...] = jnp.zeros(token.shape, F32)

    sem_shapes = [pltpu.SemaphoreType.DMA((3 * len(g),)) for g in groups for _ in range(2)]
    res = pl.pallas_call(
        body, name=name,
        out_shape=(*sem_shapes, *[pltpu.HBM(b.shape, b.dtype) for b in bufs], jax.ShapeDtypeStruct((8, LANES), F32)),
        in_specs=[_HBM] * n,
        out_specs=(*[_SEM] * (2 * ng), *[_HBM] * n, pl.BlockSpec(memory_space=pltpu.VMEM)),
        input_output_aliases={i: 2 * ng + i for i in range(n)},
        compiler_params=pltpu.CompilerParams(has_side_effects=_DATAFLOW),
    )(*[pltpu.with_memory_space_constraint(b, pltpu.HBM) for b in bufs])
    out, i = [], 2 * ng
    for gi, g in enumerate(groups):
        out.append((res[2 * gi], res[2 * gi + 1], list(res[i:i + len(g)])))
        i += len(g)
    return out, res[-1]


def _gather_wait(name, send_sems, recv_sems, bufs, after):
    n = len(bufs)

    def body(*refs):
        b_refs = refs[:n]
        s_sems, r_sems = refs[n], refs[n + 1]
        x, y, c = _mesh_pos()
        me = 2 * x + y
        for k in range(n):
            for j, (cx, cy) in enumerate(_other_chips(x, y)):
                idx = 2 * cx + cy
                copy = pltpu.make_async_remote_copy(src_ref=b_refs[k].at[me, c], dst_ref=b_refs[k].at[idx, c],
                                                    send_sem=s_sems.at[3 * k + j], recv_sem=r_sems.at[3 * k + j],
                                                    device_id=(cx, cy, c), device_id_type=MESH)
                copy.wait_send()
                copy.wait_recv()

    res = pl.pallas_call(
        body, name=name,
        out_shape=tuple(pltpu.HBM(b.shape, b.dtype) for b in bufs),
        in_specs=[_HBM] * n + [_SEM, _SEM, _ANY],
        out_specs=tuple([_HBM] * n),
        input_output_aliases={i: i for i in range(n)},
        compiler_params=pltpu.CompilerParams(has_side_effects=_DATAFLOW),
    )(*bufs, send_sems, recv_sems, after)
    return list(res)


def _halves_to_sibling(name, units):
    n = len(units)

    def body(*refs):
        g_refs, out_refs = refs[:n], refs[n:2 * n]
        send_sems, recv_sems = refs[2 * n:]
        x, y, c = _mesh_pos()
        cps = []
        for i in range(n):
            half = units[i].shape[1] // 2
            src = g_refs[i].at[pl.ds(0, N_CHIPS), pl.ds((1 - c) * half, half)]
            cp = pltpu.make_async_remote_copy(src_ref=src, dst_ref=out_refs[i], send_sem=send_sems.at[i],
                                              recv_sem=recv_sems.at[i], device_id=(x, y, 1 - c), device_id_type=MESH)
            cp.start()
            cps.append(cp)
        for cp in cps:
            cp.wait()

    return pl.pallas_call(
        body, name=name, in_specs=[_ANY] * n, out_specs=[_ANY] * n,
        out_shape=[jax.ShapeDtypeStruct((u.shape[0], u.shape[1] // 2, u.shape[2]), u.dtype) for u in units],
        scratch_shapes=[pltpu.SemaphoreType.DMA((n,)), pltpu.SemaphoreType.DMA((n,))],
        compiler_params=pltpu.CompilerParams(has_side_effects=True),
    )(*units)


_HBM = pl.BlockSpec(memory_space=pltpu.HBM)
_SEM = pl.BlockSpec(memory_space=pltpu.SEMAPHORE)
_DATAFLOW = pltpu.SideEffectType.DATAFLOW_SIDE_EFFECTING


def _halves_start(name, units):
    n = len(units)

    def body(*refs):
        g_refs, z_refs = refs[:n], refs[n:2 * n]
        send_sems, recv_sems = refs[2 * n], refs[2 * n + 1]
        token = refs[-1]
        x, y, c = _mesh_pos()
        for i in range(n):
            half = units[i].shape[1] // 2
            src = g_refs[i].at[pl.ds(0, N_CHIPS), pl.ds((1 - c) * half, half)]
            pltpu.make_async_remote_copy(src_ref=src, dst_ref=z_refs[i], send_sem=send_sems.at[i], recv_sem=recv_sems.at[i],
                                         device_id=(x, y, 1 - c), device_id_type=MESH).start()
        token[...] = jnp.zeros(token.shape, F32)

    zones = [lax.empty((u.shape[0], u.shape[1] // 2, u.shape[2]), u.dtype) for u in units]
    hbm = lambda a: pltpu.HBM(a.shape, a.dtype)
    res = pl.pallas_call(
        body, name=name,
        out_shape=(pltpu.SemaphoreType.DMA((n,)), pltpu.SemaphoreType.DMA((n,)),
                   *[hbm(a) for a in units], *[hbm(a) for a in zones], jax.ShapeDtypeStruct((8, LANES), F32)),
        in_specs=[_HBM] * (2 * n),
        out_specs=(_SEM, _SEM, *[_HBM] * (2 * n), pl.BlockSpec(memory_space=pltpu.VMEM)),
        input_output_aliases={i: 2 + i for i in range(2 * n)},
        compiler_params=pltpu.CompilerParams(has_side_effects=_DATAFLOW),
    )(*[pltpu.with_memory_space_constraint(a, pltpu.HBM) for a in list(units) + zones])
    return res[0], res[1], res[2:2 + n], res[2 + n:2 + 2 * n], res[-1]


def _halves_wait(name, send_sems, recv_sems, units, zones, after):
    n = len(units)

    def body(*refs):
        g_refs, z_refs = refs[:n], refs[n:2 * n]
        s_sems, r_sems = refs[2 * n], refs[2 * n + 1]
        x, y, c = _mesh_pos()
        for i in range(n):
            half = units[i].shape[1] // 2
            src = g_refs[i].at[pl.ds(0, N_CHIPS), pl.ds((1 - c) * half, half)]
            copy = pltpu.make_async_remote_copy(src_ref=src, dst_ref=z_refs[i], send_sem=s_sems.at[i], recv_sem=r_sems.at[i],
                                                device_id=(x, y, 1 - c), device_id_type=MESH)
            copy.wait_send()
            copy.wait_recv()

    hbm = lambda a: pltpu.HBM(a.shape, a.dtype)
    res = pl.pallas_call(
        body, name=name,
        out_shape=(*[hbm(a) for a in units], *[hbm(a) for a in zones]),
        in_specs=[_HBM] * (2 * n) + [_SEM, _SEM, _ANY],
        out_specs=tuple([_HBM] * (2 * n)),
        input_output_aliases={i: i for i in range(2 * n)},
        compiler_params=pltpu.CompilerParams(has_side_effects=_DATAFLOW),
    )(*units, *zones, send_sems, recv_sems, after)
    return res[:n], res[n:]


def _scatter_start(name, sums, zones):
    n = len(sums)

    def body(*refs):
        h_refs, z_refs = refs[:n], refs[n:2 * n]
        send_sems, recv_sems = refs[2 * n], refs[2 * n + 1]
        token = refs[-1]
        x, y, c = _mesh_pos()
        me = 2 * x + y
        for i in range(n):
            for j, (cx, cy) in enumerate(_other_chips(x, y)):
                pltpu.make_async_remote_copy(src_ref=h_refs[i].at[2 * cx + cy], dst_ref=z_refs[i].at[me],
                                             send_sem=send_sems.at[3 * i + j], recv_sem=recv_sems.at[3 * i + j],
                                             device_id=(cx, cy, c), device_id_type=MESH).start()
        token[...] = jnp.zeros(token.shape, F32)

    hbm = lambda a: pltpu.HBM(a.shape, a.dtype)
    res = pl.pallas_call(
        body, name=name,
        out_shape=(pltpu.SemaphoreType.DMA((3 * n,)), pltpu.SemaphoreType.DMA((3 * n,)),
                   *[hbm(a) for a in sums], *[hbm(a) for a in zones], jax.ShapeDtypeStruct((8, LANES), F32)),
        in_specs=[_HBM] * (2 * n),
        out_specs=(_SEM, _SEM, *[_HBM] * (2 * n), pl.BlockSpec(memory_space=pltpu.VMEM)),
        input_output_aliases={i: 2 + i for i in range(2 * n)},
        compiler_params=pltpu.CompilerParams(has_side_effects=_DATAFLOW),
    )(*[pltpu.with_memory_space_constraint(a, pltpu.HBM) for a in list(sums) + list(zones)])
    return res[0], res[1], res[2:2 + n], res[2 + n:2 + 2 * n], res[-1]


def _scatter_wait(name, send_sems, recv_sems, sums, zones, after):
    n = len(sums)

    def body(*refs):
        h_refs, z_refs = refs[:n], refs[n:2 * n]
        s_sems, r_sems = refs[2 * n], refs[2 * n + 1]
        x, y, c = _mesh_pos()
        me = 2 * x + y
        for i in range(n):
            for j, (cx, cy) in enumerate(_other_chips(x, y)):
                idx = 2 * cx + cy
                copy = pltpu.make_async_remote_copy(src_ref=h_refs[i].at[idx], dst_ref=z_refs[i].at[idx],
                                                    send_sem=s_sems.at[3 * i + j], recv_sem=r_sems.at[3 * i + j],
                                                    device_id=(cx, cy, c), device_id_type=MESH)
                copy.wait_send()
                copy.wait_recv()

    hbm = lambda a: pltpu.HBM(a.shape, a.dtype)
    res = pl.pallas_call(
        body, name=name,
        out_shape=(*[hbm(a) for a in sums], *[hbm(a) for a in zones]),
        in_specs=[_HBM] * (2 * n) + [_SEM, _SEM, _ANY],
        out_specs=tuple([_HBM] * (2 * n)),
        input_output_aliases={i: i for i in range(2 * n)},
        compiler_params=pltpu.CompilerParams(has_side_effects=_DATAFLOW),
    )(*sums, *zones, send_sems, recv_sems, after)
    return res[n:]


def _join_halves(name, results):
    n = len(results)
    pieces = [(i, l) for i in range(n) for l in range(results[i].shape[0])]

    def body(*refs):
        out_refs = refs[n:2 * n]
        send_sems, recv_sems = refs[2 * n:]
        x, y, c = _mesh_pos()

        def copy(k, half):
            i, l = pieces[k]
            return pltpu.make_async_remote_copy(src_ref=out_refs[i].at[l, half], dst_ref=out_refs[i].at[l, half],
                                                send_sem=send_sems.at[k], recv_sem=recv_sems.at[k],
                                                device_id=(x, y, 1 - c), device_id_type=MESH)

        cps = [copy(k, c) for k in range(len(pieces))]
        for cp in cps:
            cp.start()
        for k in range(len(pieces)):
            copy(k, 1 - c).wait_recv()
        for cp in cps:
            cp.wait_send()

    return pl.pallas_call(
        body, name=name, in_specs=[_ANY] * n, out_specs=[_ANY] * n,
        out_shape=[jax.ShapeDtypeStruct(r.shape, r.dtype) for r in results],
        input_output_aliases={i: i for i in range(n)},
        scratch_shapes=[pltpu.SemaphoreType.DMA((len(pieces),)), pltpu.SemaphoreType.DMA((len(pieces),))],
        compiler_params=pltpu.CompilerParams(has_side_effects=True),
    )(*results)


def _pack(arrays, dtype, rows_multiple):
    flat = jnp.concatenate([a.reshape(-1).astype(dtype) for a in arrays])
    unit = rows_multiple * PACK_W
    total = -(-flat.shape[0] // unit) * unit
    return jnp.pad(flat, (0, total - flat.shape[0])).reshape(total // PACK_W, PACK_W)


def _unpack(flat, shapes):
    out, off = [], 0
    for s in shapes:
        n = 1
        for d in s:
            n *= d
        out.append(flat[..., off:off + n].reshape(flat.shape[:-1] + tuple(s)))
        off += n
    return out


def _ffn_fwd(tag, l, h, g, w_up, conv, bias, w_down, tm):
    hn = _rms_fwd(f"{tag}_norm", h, g, tm)
    u = _mm_cs(f"{tag}_up", hn, w_up, l, tm, out_dtype=FFN_HIDDEN_DTYPE)
    act = _ffn_col_fwd(f"{tag}_glu", u, conv, bias)
    w_down = w_down(act) if callable(w_down) else w_down
    h_out = _mm_full(f"{tag}_down", act, w_down, l, tm, D_FF // 2, add=h)
    return h_out, (hn, u, act), w_down


def _ffn_bwd(tag, l, h, g, w_up, conv, bias, w_down, saved, dh, tm):
    hn, u, act = saved
    da = _mm_nt_full(f"{tag}_down_dx", dh, w_down, l, tm, D_FF // 2)
    dw_down = _mm_tn_full(f"{tag}_down_dw", act, dh, tm, D_FF // 2)
    du, dconv, dbias = _ffn_col_bwd(f"{tag}_glu_bwd", u, da, conv, bias)
    dw_up = _mm_tn_cs(f"{tag}_up_dw", hn, du, N_CHIPS, tm)
    dhn = _mm_nt_cs(f"{tag}_up_dx", du, w_up, l, tm)
    dh, dg = _rms_bwd(f"{tag}_norm_bwd", h, g, dhn, dh, tm)
    return dh, dict(norm=dg, w_up=dw_up, conv=dconv, bias=dbias, w_down=dw_down)


def _to_heads(z, nh, pad):
    t = z.shape[0]
    return jnp.pad(z.reshape(t, nh, HEAD_DIM).transpose(1, 0, 2), ((0, 0), (pad, 0), (0, 0)))


def _from_heads(z, pad):
    nh, tp, _ = z.shape
    return z[:, pad:].transpose(1, 0, 2).reshape(tp - pad, nh * HEAD_DIM)


def _rope_tables(tp, pad):
    half = HEAD_DIM // 2
    inv = ROPE_THETA ** (-jnp.arange(half, dtype=F32) / half)
    ang = (jnp.arange(tp, dtype=F32) - pad)[:, None] * inv[None, :]
    cos, sin = jnp.cos(ang), jnp.sin(ang)
    rot = jnp.zeros((HEAD_DIM, HEAD_DIM), F32)
    idx = jnp.arange(half)
    rot = rot.at[idx + half, idx].set(-1.0).at[idx, idx + half].set(1.0)
    return jnp.concatenate([cos, cos], axis=1), jnp.concatenate([sin, sin], axis=1), rot


def _local_step(x, tgt, w, on_grads=None, fetch=None):
    emit = on_grads if on_grads is not None else (lambda tag, units: 0.0)
    need = (lambda tag, after: w) if fetch is None else (lambda tag, after: {**w, **fetch(tag, after)})
    seq = x.shape[0]
    t = seq + N_META
    tm = _row_tile(t, ROW_TILE_CAP)
    tr = _row_tile(t, ROW_TILE_CAP // 2)
    pad = BLOCK - N_META
    grads = {}

    h0 = jnp.concatenate([w["meta_tokens"], x], axis=0)
    tgt_p = jnp.pad(tgt, ((N_META, 0), (0, 0)))

    hn0 = _rms_fwd("l0_norm", h0, w["norm_mix"][0:1], tm)
    p0 = _mm_cs("l0_in", hn0, w["ev_w_in"], 0, tm)
    uc, yb = _even_col_fwd("l0_convs", p0, w["ev_conv_a"], w["ev_conv_b"])
    ya = _even_ln_fwd("l0_ln", uc, w["ev_ln_a_g"], w["ev_ln_a_b"], tm)
    y0 = jnp.concatenate([ya, yb], axis=1)
    w = need("ev_out", y0)
    h1 = _mm_full("l0_out", y0, w["ev_w_out"], 0, tm, D_MODEL, add=h0)
    w = need("f0", h1)
    down0 = w["ff_w_down0"] if "ff_w_down0" in w else (lambda act: need("f0_down", act)["ff_w_down0"])
    f0 = (0, h1, w["norm_ffn"][0:1], w["ff_w_up0"], w["ff_conv"][0], w["ff_conv_b"][0:1])
    h2, ffn0, down0 = _ffn_fwd("f0", *f0, down0, tm)
    f0 = f0 + (down0,)
    w = need("od", h2)

    hn2 = _rms_fwd("l1_norm", h2, w["norm_mix"][1:2], tm)
    p1 = _mm_cs("l1_in", hn2, w["od_w_in"], 0, tm)
    cos, sin, rot = _rope_tables(t + pad, pad)
    qh = _to_heads(p1[:, :D_ATT], N_Q_HEADS, pad)
    kh = _to_heads(p1[:, D_ATT:D_ATT + D_KV], N_KV_HEADS, pad)
    vh = _to_heads(p1[:, D_ATT + D_KV:D_ATT + 2 * D_KV], N_KV_HEADS, pad)
    sinks_b = jnp.broadcast_to(w["od_sinks"].reshape(N_Q_HEADS, 1, 1), (N_Q_HEADS, 8, LANES))
    y_att = _from_heads(_attn_fwd("l1_attn", qh, kh, vh, sinks_b, cos, sin, rot), pad)

    col0 = D_ATT + 2 * D_KV
    ch = jnp.arange(D_R) // HEAD_DIM
    seg = (ch[:, None] == ch[None, :]).astype(F32)
    prm = dict(w0=w["od_w0"], a0=w["od_a0"], g2=w["od_g2"], k_k=w["od_k_k"], k_a=w["od_k_a"],
               lnx_g=w["od_lnx_g"], lnx_b=w["od_lnx_b"], r_k=w["od_r_k"].reshape(1, D_R),
               w2p=jnp.concatenate([w["od_w2"], jnp.zeros((LORA_A, D_R), F32)], axis=0),
               a2p=jnp.concatenate([jnp.zeros((LORA_W, D_R), F32), w["od_a2"]], axis=0))
    prs = _shift_fwd("l1_shift", p1, col0, w["od_mu"])
    lw, k2, a_, b_, gate_r = _rwkv_pre_fwd("l1_rwkv_pre", prs, prm, seg, tr)
    v_off = 2 * D_R // (WKV_PAIRS_PER_STEP * PAIR)
    scan_in = [(prs, 0), (lw, 0), (k2, 0), (prs, v_off), (a_, 0), (b_, 0)]
    y_scan, states = _wkv_fwd("l1_wkv", scan_in)
    y_rwkv = _rwkv_post_fwd("l1_rwkv_post", y_scan, prs, k2, gate_r, prm, seg, tr)
    y1 = jnp.concatenate([y_att, y_rwkv], axis=1).astype(MXU_DTYPE)
    h3 = _mm_full("l1_out", y1, w["od_w_out"], 0, tm, D_MODEL, add=h2)
    w = need("f1", h3)
    f1 = (0, h3, w["norm_ffn"][1:2], w["ff_w_up1"], w["ff_conv"][1], w["ff_conv_b"][1:2], w["ff_w_down1"])
    h4, ffn1, _ = _ffn_fwd("f1", *f1, tm)

    loss_blk, dh, d_norm_final = _final_loss("final", h4, w["norm_final"], tgt_p, tm)
    grads["norm_final"] = d_norm_final

    dh, gf1 = _ffn_bwd("f1", *f1, ffn1, dh, tm)
    zero = emit("f1", {"ff_w_down1": gf1["w_down"].reshape(N_CHIPS, D_FF // N_CHIPS, D_MODEL), "ff_w_up1": gf1["w_up"]})
    prm = dict(prm, lnx_g=prm["lnx_g"] + zero)
    dy1 = _mm_nt_full("l1_out_dx", dh, w["od_w_out"], 0, tm, D_MODEL)
    grads["od_w_out"] = _mm_tn_full("l1_out_dw", y1, dh, tm, D_MODEL // 2)
    dy_scan, dr_p, dk2_p, dv_p, dgate_r, grads["od_lnx_g"], grads["od_lnx_b"], d_rk = _rwkv_post_bwd(
        "l1_rwkv_post_bwd", y_scan, prs, k2, gate_r, prm, seg, dy1, 1, tr)
    grads["od_r_k"] = d_rk.reshape(N_R_HEADS, HEAD_DIM)
    dr_s, dlw, dk2_s, dv_s, da_, db_ = _wkv_bwd("l1_wkv_bwd", scan_in, states, (dy_scan, 0))
    dk, dxl, dgd, grads["od_w0"], dw2p, grads["od_a0"], da2p, grads["od_g2"], grads["od_k_k"], grads["od_k_a"] = (
        _rwkv_pre_bwd("l1_rwkv_pre_bwd", prs, prm, seg, (dlw, dk2_s + dk2_p, da_, db_, dgate_r), tr))
    grads["od_w2"] = dw2p[:LORA_W]
    grads["od_a2"] = da2p[LORA_W:]
    dprs = jnp.concatenate([dr_s + dr_p, dk, dv_s + dv_p, dxl, dgd], axis=1)
    dpr, grads["od_mu"] = _shift_bwd("l1_shift_bwd", p1, col0, w["od_mu"], dprs)
    doh = _to_heads(dy1[:, :D_ATT], N_Q_HEADS, pad)
    dqh, dkp, dkc, dvp, dvc, dkm, dvm, dsinks = _attn_bwd("l1_attn_bwd", qh, kh, vh, sinks_b, cos, sin, rot, doh)
    grads["od_sinks"] = dsinks[:, 0, 0].reshape(1, N_Q_HEADS)
    dkh, dvh = _kv_combine("l1_attn_dkv", (dkp, dkc, dkm), (dvp, dvc, dvm))
    dp1 = jnp.concatenate([_from_heads(dqh, pad), _from_heads(dkh, pad), _from_heads(dvh, pad), dpr], axis=1).astype(MXU_DTYPE)
    grads["od_w_in"] = _mm_tn_cs("l1_in_dw", hn2, dp1, N_CHIPS, tm)
    dhn2 = _mm_nt_cs("l1_in_dx", dp1, w["od_w_in"], 0, tm)
    dh, d_mix1 = _rms_bwd("l1_norm_bwd", h2, w["norm_mix"][1:2], dhn2, dh, tm)

    zero = emit("od", {"od_w_out": grads["od_w_out"].reshape(N_CHIPS, D_MODEL // N_CHIPS, D_MODEL), "od_w_in": grads["od_w_in"]})
    f0 = f0[:5] + (f0[5] + zero,) + f0[6:]
    dh, gf0 = _ffn_bwd("f0", *f0, ffn0, dh, tm)
    zero = emit("f0", {"ff_w_down0": gf0["w_down"].reshape(N_CHIPS, D_FF // N_CHIPS, D_MODEL), "ff_w_up0": gf0["w_up"]})
    w = dict(w, ev_ln_a_g=w["ev_ln_a_g"] + zero)
    dy0 = _mm_nt_full("l0_out_dx", dh, w["ev_w_out"], 0, tm, D_MODEL)
    grads["ev_w_out"] = _mm_tn_full("l0_out_dw", y0, dh, tm, D_MODEL // 2)
    duc, grads["ev_ln_a_g"], grads["ev_ln_a_b"] = _even_ln_bwd("l0_ln_bwd", uc, w["ev_ln_a_g"], w["ev_ln_a_b"], dy0, 0, tm)
    *dparts, grads["ev_conv_a"], grads["ev_conv_b"] = _even_col_bwd("l0_convs_bwd", p0, duc, dy0, w["ev_conv_a"], w["ev_conv_b"])
    dp0 = jnp.concatenate(dparts, axis=1)
    grads["ev_w_in"] = _mm_tn_cs("l0_in_dw", hn0, dp0, N_CHIPS, tm)
    dhn0 = _mm_nt_cs("l0_in_dx", dp0, w["ev_w_in"], 0, tm)
    dh, d_mix0 = _rms_bwd("l0_norm_bwd", h0, w["norm_mix"][0:1], dhn0, dh, tm)

    grads["norm_mix"] = jnp.concatenate([d_mix0, d_mix1], axis=0)
    grads["norm_ffn"] = jnp.concatenate([gf0["norm"], gf1["norm"]], axis=0)
    grads["ff_w_up"] = [gf0["w_up"], gf1["w_up"]]
    grads["ff_conv"] = jnp.stack([gf0["conv"], gf1["conv"]])
    grads["ff_conv_b"] = jnp.concatenate([gf0["bias"], gf1["bias"]], axis=0)
    grads["ff_w_down"] = [gf0["w_down"], gf1["w_down"]]
    grads["meta_tokens"] = dh[:N_META]
    return loss_blk[0, 0], dh[N_META:], grads


SHARD_AXIS = {
    "meta_tokens": 1, "norm_mix": None, "norm_ffn": None, "norm_final": None,
    "ev_w_in": 2, "ev_conv_a": 2, "ev_ln_a_g": None, "ev_ln_a_b": None, "ev_conv_b": 2, "ev_w_out": 1,
    "od_w_in": 2, "od_sinks": None, "od_mu": 1, "od_w0": 1, "od_w2": 2, "od_a0": 1, "od_a2": 2, "od_g2": 2,
    "od_k_k": 1, "od_k_a": 1, "od_r_k": None, "od_lnx_g": 1, "od_lnx_b": 1, "od_w_out": 1,
    "ff_w_up": 2, "ff_conv": 2, "ff_conv_b": None, "ff_w_down": 1,
}
WEIGHTS = list(SHARD_AXIS)
BIG = ("ev_w_in", "ev_w_out", "od_w_in", "od_w_out", "ff_w_up", "ff_w_down")
SHARDED = [n for n in WEIGHTS if SHARD_AXIS[n] is not None]
SMALL = [n for n in SHARDED if n not in BIG]
REPLICATED = [n for n in WEIGHTS if SHARD_AXIS[n] is None]


def _join(g, axis):
    return jnp.concatenate([g[k] for k in range(N_CHIPS)], axis=axis)


def _split(full, axis):
    return jnp.stack(jnp.split(full, N_CHIPS, axis=axis))


def _full_weights(gathered, repl):
    w = {}
    sq = lambda a: a.reshape(a.shape[1:]) if a.shape[0] == 1 else a
    for n in REPLICATED:
        w[n] = repl[n]
    w["norm_final"] = repl["norm_final"].reshape(1, D_MODEL)
    for n in ("ev_ln_a_g", "ev_ln_a_b"):
        w[n] = repl[n].reshape(1, D_A)
    w["od_r_k"] = repl["od_r_k"][0]
    w["meta_tokens"] = _join(gathered["meta_tokens"], 1)
    for n in ("ev_conv_a", "ev_conv_b", "od_w2", "od_a2", "od_g2"):
        w[n] = sq(_join(gathered[n], 2))
    for n in ("od_mu", "od_w0", "od_a0", "od_k_k", "od_k_a", "od_lnx_g", "od_lnx_b"):
        w[n] = _join(gathered[n], 1)
    w["ff_conv"] = _join(gathered["ff_conv"], 2)
    return w


def _shard_grads(grads):
    out = {}
    for n in REPLICATED:
        out[n] = grads[n]
    out["norm_final"] = grads["norm_final"].reshape(D_MODEL)
    out["od_r_k"] = grads["od_r_k"][None]
    out["meta_tokens"] = _split(grads["meta_tokens"], 1)
    for n in ("ev_conv_a", "ev_conv_b", "od_w2", "od_a2", "od_g2"):
        out[n] = _split(grads[n][None], 2)
    for n in ("od_mu", "od_w0", "od_a0", "od_k_k", "od_k_a", "od_lnx_g", "od_lnx_b"):
        out[n] = _split(grads[n], 1)
    out["ff_conv"] = _split(grads["ff_conv"], 2)
    return out


def kernel(x, meta_tokens, norm_mix, norm_ffn, norm_final, ev_w_in, ev_conv_a, ev_ln_a_g, ev_ln_a_b, ev_conv_b, ev_w_out, od_w_in, od_sinks, od_mu, od_w0, od_w2, od_a0, od_a2, od_g2, od_k_k, od_k_a, od_r_k, od_lnx_g, od_lnx_b, od_w_out, ff_w_up, ff_conv, ff_conv_b, ff_w_down, loss_target, m_meta_tokens, m_norm_mix, m_norm_ffn, m_norm_final, m_ev_w_in, m_ev_conv_a, m_ev_ln_a_g, m_ev_ln_a_b, m_ev_conv_b, m_ev_w_out, m_od_w_in, m_od_sinks, m_od_mu, m_od_w0, m_od_w2, m_od_a0, m_od_a2, m_od_g2, m_od_k_k, m_od_k_a, m_od_r_k, m_od_lnx_g, m_od_lnx_b, m_od_w_out, m_ff_w_up, m_ff_conv, m_ff_conv_b, m_ff_w_down, v_meta_tokens, v_norm_mix, v_norm_ffn, v_norm_final, v_ev_w_in, v_ev_conv_a, v_ev_ln_a_g, v_ev_ln_a_b, v_ev_conv_b, v_ev_w_out, v_od_w_in, v_od_sinks, v_od_mu, v_od_w0, v_od_w2, v_od_a0, v_od_a2, v_od_g2, v_od_k_k, v_od_k_a, v_od_r_k, v_od_lnx_g, v_od_lnx_b, v_od_w_out, v_ff_w_up, v_ff_conv, v_ff_conv_b, v_ff_w_down):
    wts = dict(meta_tokens=meta_tokens, norm_mix=norm_mix, norm_ffn=norm_ffn, norm_final=norm_final, ev_w_in=ev_w_in, ev_conv_a=ev_conv_a, ev_ln_a_g=ev_ln_a_g, ev_ln_a_b=ev_ln_a_b, ev_conv_b=ev_conv_b, ev_w_out=ev_w_out, od_w_in=od_w_in, od_sinks=od_sinks, od_mu=od_mu, od_w0=od_w0, od_w2=od_w2, od_a0=od_a0, od_a2=od_a2, od_g2=od_g2, od_k_k=od_k_k, od_k_a=od_k_a, od_r_k=od_r_k, od_lnx_g=od_lnx_g, od_lnx_b=od_lnx_b, od_w_out=od_w_out, ff_w_up=ff_w_up, ff_conv=ff_conv, ff_conv_b=ff_conv_b, ff_w_down=ff_w_down)
    mom = dict(meta_tokens=m_meta_tokens, norm_mix=m_norm_mix, norm_ffn=m_norm_ffn, norm_final=m_norm_final, ev_w_in=m_ev_w_in, ev_conv_a=m_ev_conv_a, ev_ln_a_g=m_ev_ln_a_g, ev_ln_a_b=m_ev_ln_a_b, ev_conv_b=m_ev_conv_b, ev_w_out=m_ev_w_out, od_w_in=m_od_w_in, od_sinks=m_od_sinks, od_mu=m_od_mu, od_w0=m_od_w0, od_w2=m_od_w2, od_a0=m_od_a0, od_a2=m_od_a2, od_g2=m_od_g2, od_k_k=m_od_k_k, od_k_a=m_od_k_a, od_r_k=m_od_r_k, od_lnx_g=m_od_lnx_g, od_lnx_b=m_od_lnx_b, od_w_out=m_od_w_out, ff_w_up=m_ff_w_up, ff_conv=m_ff_conv, ff_conv_b=m_ff_conv_b, ff_w_down=m_ff_w_down)
    var = dict(meta_tokens=v_meta_tokens, norm_mix=v_norm_mix, norm_ffn=v_norm_ffn, norm_final=v_norm_final, ev_w_in=v_ev_w_in, ev_conv_a=v_ev_conv_a, ev_ln_a_g=v_ev_ln_a_g, ev_ln_a_b=v_ev_ln_a_b, ev_conv_b=v_ev_conv_b, ev_w_out=v_ev_w_out, od_w_in=v_od_w_in, od_sinks=v_od_sinks, od_mu=v_od_mu, od_w0=v_od_w0, od_w2=v_od_w2, od_a0=v_od_a0, od_a2=v_od_a2, od_g2=v_od_g2, od_k_k=v_od_k_k, od_k_a=v_od_k_a, od_r_k=v_od_r_k, od_lnx_g=v_od_lnx_g, od_lnx_b=v_od_lnx_b, od_w_out=v_od_w_out, ff_w_up=v_ff_w_up, ff_conv=v_ff_conv, ff_conv_b=v_ff_conv_b, ff_w_down=v_ff_w_down)

    me_idx = (2 * lax.axis_index("x") + lax.axis_index("y")).astype(jnp.int32).reshape(1)
    c_idx = lax.axis_index("c").astype(jnp.int32).reshape(1)
    small_mine = _pack([wts[n] for n in SMALL], F32, 2 * 8)
    sources = {"ev_w_in": (ev_w_in, 0), "small": (small_mine[None], 0), "ev_w_out": (ev_w_out, 0),
               "ff_w_up0": (ff_w_up, 0), "ff_w_down0": (ff_w_down, 0), "od_w_in": (od_w_in, 0), "od_w_out": (od_w_out, 0),
               "ff_w_up1": (ff_w_up, 1), "ff_w_down1": (ff_w_down, 1)}
    bufs = {n: _place_own_block("place_" + n, a, l, me_idx, F32 if n == "small" else MXU_DTYPE)
            for n, (a, l) in sources.items()}

    def as_used(n, g):
        if n in ("ev_w_out", "od_w_out", "ff_w_down0", "ff_w_down1"):
            return g.reshape(1, -1, g.shape[-1])
        return g.reshape(N_CHIPS, 1, -1, g.shape[-1])

    groups = {"first": ["ev_w_in", "small"], "ev_out": ["ev_w_out"], "f0": ["ff_w_up0"], "f0_down": ["ff_w_down0"],
              "od": ["od_w_in", "od_w_out"], "f1": ["ff_w_up1", "ff_w_down1"]}
    started_gathers, token = _gather_start("gather_start", [[bufs[n] for n in g] for g in groups.values()])
    started_gathers = dict(zip(groups, started_gathers))

    def whole(tag, after):
        send_sems, recv_sems, group_bufs = started_gathers[tag]
        landed = _gather_wait("gather_wait_" + tag, send_sems, recv_sems, group_bufs, after)
        return dict(zip(groups[tag], _gather_weights("gather_siblings_" + tag, landed, from_chips=False)))

    def fetch(tag, after):
        return {n: as_used(n, g) for n, g in whole(tag, after).items()}

    first = whole("first", token)
    gathered = dict(zip(SMALL, _unpack(first["small"].reshape(N_CHIPS, -1), [wts[n].shape for n in SMALL])))
    w_full = _full_weights(gathered, wts)
    w_full["ev_w_in"] = as_used("ev_w_in", first["ev_w_in"])

    cm_idx = jnp.concatenate([c_idx, me_idx])
    started = []
    to_sibling = []

    def to_chips(tag, names, units, from_sibling):
        pairs = [_pair_add_placed(f"grads_pair_add_{n}", u, r, cm_idx, GRAD_WIRE_DTYPE)
                 for n, u, r in zip(names, units, from_sibling)]
        send_sems, recv_sems, sums, zones, token = _scatter_start(
            f"grads_to_chips_start_{tag}", [p[0] for p in pairs], [p[1] for p in pairs])
        started.append((tag, names, send_sems, recv_sems, sums, zones))
        return token[0, 0]

    def start_reduction(tag, units):
        names = list(units)
        arrays = [units[n] for n in names]
        zero = 0.0
        if to_sibling:
            before, bnames, send_sems, recv_sems, thru, zones = to_sibling.pop()
            thru, got = _halves_wait(f"grads_to_sibling_wait_{before}", send_sems, recv_sems, thru, zones, arrays[-1])
            zero = zero + to_chips(before, bnames, thru, got)
        if tag == "f0":
            return zero + to_chips(tag, names, arrays, _halves_to_sibling(f"grads_to_sibling_{tag}", arrays))
        send_sems, recv_sems, thru, zones, token = _halves_start(f"grads_to_sibling_start_{tag}", arrays)
        to_sibling.append((tag, names, send_sems, recv_sems, thru, zones))
        return zero + token[0, 0]

    loss_local, grad_x, grads = _local_step(x[0], loss_target[0], w_full, start_reduction, fetch)
    loss = lax.psum(loss_local, ("x", "y", "c"))

    sg = _shard_grads(grads)
    small_rows = [jnp.concatenate([sg[n][k].reshape(-1) for n in SMALL] + [sg[n].reshape(-1) for n in REPLICATED])
                  for k in range(N_CHIPS)]
    n_el = small_rows[0].shape[0]
    n_rows = -(-n_el // (16 * PACK_W)) * 16
    small_unit = jnp.stack([jnp.pad(r, (0, n_rows * PACK_W - n_el)).reshape(n_rows, PACK_W) for r in small_rows])
    last = {"ev_w_out": grads["ev_w_out"].reshape(N_CHIPS, D_MODEL // N_CHIPS, D_MODEL), "ev_w_in": grads["ev_w_in"],
            "small": small_unit}
    from_sibling = _halves_to_sibling("grads_to_sibling_ev", list(last.values()))
    pairs = [_pair_add_placed(f"grads_pair_add_{n}", u, r, cm_idx, F32 if n == "small" else GRAD_WIRE_DTYPE)
             for (n, u), r in zip(last.items(), from_sibling)]
    ev_send, ev_recv, ev_sums, ev_zones, token = _scatter_start(
        "grads_to_chips_start_ev", [p[0] for p in pairs], [p[1] for p in pairs])
    dests = {"ev_w_in": ("ev_w_in", 0), "od_w_in": ("od_w_in", 0), "ev_w_out": ("ev_w_out", 0), "od_w_out": ("od_w_out", 0),
             "ff_w_up0": ("ff_w_up", 0), "ff_w_up1": ("ff_w_up", 1), "ff_w_down0": ("ff_w_down", 0),
             "ff_w_down1": ("ff_w_down", 1), "small": ("small", 0)}
    outs = {"grad": {}, "delta": {}, "new_m": {}, "new_v": {}}

    def finish(tag, from_chips, results):
        reduced = {}
        for n, part in from_chips.items():
            r, l = dests[n]
            reduced[r] = _sum_chips(f"grads_chip_sum_{n}", part, c_idx, l, 2 if r.startswith("ff_w") else 1,
                                    into=reduced.get(r))
        joined = dict(zip(results, _join_halves("grads_join_" + tag, [reduced[r] for r in results])))
        for n, g in joined.items():
            if n == "small":
                continue
            shape = wts[n].shape
            flat = lambda a: a.reshape(-1, shape[-1])
            new = _adamw("adamw_" + n, flat(wts[n]), flat(g), flat(mom[n]), flat(var[n]))
            for kind, arr in zip(("grad", "delta", "new_m", "new_v"), (g,) + tuple(new)):
                outs[kind][n] = arr.reshape(shape)
        return joined

    from_chips = {}
    for tag, names, send_sems, recv_sems, sums, zones in started:
        from_chips.update(zip(names, _scatter_wait(f"grads_to_chips_wait_{tag}", send_sems, recv_sems, sums, zones, token)))
    finish("layers", from_chips, ["od_w_in", "od_w_out", "ff_w_up", "ff_w_down"])
    from_chips = dict(zip(last, _scatter_wait("grads_to_chips_wait_ev", ev_send, ev_recv, ev_sums, ev_zones,
                                              outs["delta"]["ff_w_up"])))
    joined = finish("ev", from_chips, ["ev_w_in", "ev_w_out", "small"])

    order = SMALL + REPLICATED
    packed = lambda d: jnp.pad(jnp.concatenate([d[n].reshape(-1) for n in order]),
                               (0, n_rows * PACK_W - n_el)).reshape(n_rows, PACK_W)
    g_small = joined["small"].reshape(n_rows, PACK_W)
    new = _adamw("adamw_small", packed(wts), g_small, packed(mom), packed(var))
    for tag, arr in zip(("grad", "delta", "new_m", "new_v"), (g_small,) + tuple(new)):
        outs[tag].update(zip(order, _unpack(arr.reshape(-1), [wts[n].shape for n in order])))
    return (loss, grad_x[None], *[outs["grad"][n] for n in WEIGHTS], *[outs["delta"][n] for n in WEIGHTS],
            *[outs["new_m"][n] for n in WEIGHTS], *[outs["new_v"][n] for n in WEIGHTS])
```

```python
import functools

import jax
import jax.numpy as jnp
from jax import lax
from jax.experimental import pallas as pl
from jax.experimental.pallas import tpu as pltpu

F32 = jnp.float32
BF16 = jnp.bfloat16
MXU_DTYPE = BF16
GRAD_WIRE_DTYPE = BF16
FFN_HIDDEN_DTYPE = BF16

D_MODEL = 1024
N_META = 16
RMS_EPS = 1e-6
LN_EPS = 1e-5
D_A = 512
CONV_A_WIDTH = 31
CONV_B_WIDTH = 3
HEAD_DIM = 64
N_Q_HEADS = 8
N_KV_HEADS = 2
GQA_GROUP = 4
D_ATT = 512
D_KV = 128
BLOCK = 128
ROPE_THETA = 10000.0
D_R = 512
N_R_HEADS = 8
LORA_W = 64
LORA_A = 64
LORA_G = 128
RWKV_GN_EPS = 64e-5
RWKV_COLS = 3 * D_R + LORA_W + LORA_A + LORA_G
D_FF = 2816
NEG_INF = -1e30
ADAM_LR = 0.001
ADAM_B1 = 0.9
ADAM_B2 = 0.999
ADAM_EPS = 1e-08
ADAM_WD = 0.01
ADAM_STEP = 10

N_CHIPS = 4
LANES = 128
CONV_PAD = 32
ROW_TILE_CAP = 704
VMEM_LIMIT_V7X = 56 * 1024 * 1024
MESH = pl.DeviceIdType.MESH


def _cparams(sem=None):
    return pltpu.CompilerParams(dimension_semantics=sem, vmem_limit_bytes=VMEM_LIMIT_V7X)


def _row_tile(t, cap):
    for d in range(min(t, cap), 0, -1):
        if t % d == 0 and d % 16 == 0:
            return d
    return t


def _chunk_len(t):
    for d in (64, 48, 32, 16, 8):
        if t % d == 0:
            return d
    raise ValueError(t)


def _call(fn, name, grid, ins, outs, acc_axis=None, sem=None):
    n_in, n_out = len(ins), len(outs)
    dtype = lambda o: o[4] if len(o) > 4 else F32

    def body(*refs):
        vals = fn(*[r[...] for r in refs[:n_in]])
        if not isinstance(vals, (tuple, list)):
            vals = (vals,)
        for r, v, o in zip(refs[n_in:n_in + n_out], vals, outs):
            if o[3]:
                first = pl.program_id(acc_axis) == 0

                @pl.when(first)
                def _(r=r, v=v):
                    r[...] = v

                @pl.when(jnp.logical_not(first))
                def _(r=r, v=v):
                    r[...] += v
            else:
                r[...] = v.astype(dtype(o))

    res = pl.pallas_call(
        body, name=name, grid=grid,
        in_specs=[pl.BlockSpec(b, m) for _, b, m in ins],
        out_specs=[pl.BlockSpec(o[1], o[2]) for o in outs],
        out_shape=[jax.ShapeDtypeStruct(o[0], dtype(o)) for o in outs],
        compiler_params=_cparams(sem),
    )(*[a for a, _, _ in ins])
    return res if n_out > 1 else res[0]


def _matmul(name, a, b, *, dims, grid, a_spec, b_spec, o_shape, o_spec, acc_shape, nk, k_axis,
            add=None, add_spec=None, out_dtype=F32):
    def product(a_ref, b_ref):
        return lax.dot_general(a_ref[...].astype(MXU_DTYPE), b_ref[...].astype(MXU_DTYPE), dims, preferred_element_type=F32)

    def body_single(*refs):
        a_ref, b_ref, o_ref = refs[0], refs[1], refs[-1]
        res = product(a_ref, b_ref) if add is None else product(a_ref, b_ref) + refs[2][...]
        o_ref[...] = res.astype(out_dtype)

    def body_steps(*refs):
        a_ref, b_ref, o_ref, acc = refs[0], refs[1], refs[-2], refs[-1]
        k = pl.program_id(k_axis)

        @pl.when(k == 0)
        def _():
            if add is None:
                acc[...] = jnp.zeros(acc.shape, F32)
            else:
                acc[...] = refs[2][...]

        acc[...] += product(a_ref, b_ref)

        @pl.when(k == nk - 1)
        def _():
            o_ref[...] = acc[...].astype(out_dtype)

    args = [a, b] + ([] if add is None else [add])
    specs = [a_spec, b_spec] + ([] if add is None else [add_spec])
    return pl.pallas_call(
        body_single if nk == 1 else body_steps, name=name, grid=grid, in_specs=specs, out_specs=o_spec,
        out_shape=jax.ShapeDtypeStruct(o_shape, out_dtype),
        scratch_shapes=[] if nk == 1 else [pltpu.VMEM(acc_shape, F32)],
        compiler_params=_cparams(None),
    )(*args)


MATMUL_BLOCKS_BYTES = 46 * 1024 * 1024


def _whole_if_fits(t, tile, need_bytes):
    return t if need_bytes <= MATMUL_BLOCKS_BYTES else tile


_NN = (((1,), (0,)), ((), ()))
_NT = (((1,), (1,)), ((), ()))
_TN = (((0,), (0,)), ((), ()))


def _mm_cs(name, x, wg, l, tm, out_dtype=F32):
    t, k = x.shape
    s, _, _, n = wg.shape
    tm = _whole_if_fits(t, tm, 2 * (t * k * x.dtype.itemsize + k * n * wg.dtype.itemsize + t * n * 4))
    return _matmul(name, x, wg, dims=_NN, grid=(s, t // tm, 1),
                   a_spec=pl.BlockSpec((tm, k), lambda j, i, kk: (i, 0)),
                   b_spec=pl.BlockSpec((None, None, k, n), lambda j, i, kk: (j, l, 0, 0)),
                   o_shape=(t, s * n), o_spec=pl.BlockSpec((tm, n), lambda j, i, kk: (i, j)),
                   acc_shape=(tm, n), nk=1, k_axis=2, out_dtype=out_dtype)


def _mm_full(name, x, w, l, tm, tk, add=None):
    t, k = x.shape
    n = w.shape[2]
    nk = k // tk
    tm = _whole_if_fits(t, tm, 2 * (t * tk * x.dtype.itemsize + tk * n * w.dtype.itemsize + t * n * 4 * (1 if add is None else 2))
                        + (t * n * 4 if nk > 1 else 0))
    return _matmul(name, x, w, dims=_NN, grid=(t // tm, 1, nk),
                   a_spec=pl.BlockSpec((tm, tk), lambda i, j, kk: (i, kk)),
                   b_spec=pl.BlockSpec((None, tk, n), lambda i, j, kk: (l, kk, 0)),
                   o_shape=(t, n), o_spec=pl.BlockSpec((tm, n), lambda i, j, kk: (i, 0)),
                   acc_shape=(tm, n), nk=nk, k_axis=2,
                   add=add, add_spec=pl.BlockSpec((tm, n), lambda i, j, kk: (i, 0)))


def _mm_nt_cs(name, dy, wg, l, tm, add=None):
    t = dy.shape[0]
    s, _, k, n = wg.shape
    tm = _whole_if_fits(t, tm, 2 * (t * n * dy.dtype.itemsize + k * n * wg.dtype.itemsize + t * k * 4 * (1 if add is None else 2))
                        + t * k * 4)
    return _matmul(name, dy, wg, dims=_NT, grid=(t // tm, 1, s),
                   a_spec=pl.BlockSpec((tm, n), lambda i, j, kk: (i, kk)),
                   b_spec=pl.BlockSpec((None, None, k, n), lambda i, j, kk: (kk, l, 0, 0)),
                   o_shape=(t, k), o_spec=pl.BlockSpec((tm, k), lambda i, j, kk: (i, 0)),
                   acc_shape=(tm, k), nk=s, k_axis=2,
                   add=add, add_spec=pl.BlockSpec((tm, k), lambda i, j, kk: (i, 0)))


def _mm_nt_full(name, dy, w, l, tm, tko):
    t, n = dy.shape
    k = w.shape[1]
    tm = _whole_if_fits(t, tm, 2 * (t * n * dy.dtype.itemsize + tko * n * w.dtype.itemsize + t * tko * 4))
    return _matmul(name, dy, w, dims=_NT, grid=(t // tm, k // tko, 1),
                   a_spec=pl.BlockSpec((tm, n), lambda i, j, kk: (i, 0)),
                   b_spec=pl.BlockSpec((None, tko, n), lambda i, j, kk: (l, j, 0)),
                   o_shape=(t, k), o_spec=pl.BlockSpec((tm, tko), lambda i, j, kk: (i, j)),
                   acc_shape=(tm, tko), nk=1, k_axis=2)


def _mm_tn_cs(name, x, dy, s, tk):
    t, k = x.shape
    n = dy.shape[1] // s
    tk = _whole_if_fits(t, tk, 2 * (t * k * x.dtype.itemsize + t * n * dy.dtype.itemsize + k * n * 4))
    nk = t // tk
    return _matmul(name, x, dy, dims=_TN, grid=(s, 1, nk),
                   a_spec=pl.BlockSpec((tk, k), lambda j, i, kk: (kk, 0)),
                   b_spec=pl.BlockSpec((tk, n), lambda j, i, kk: (kk, j)),
                   o_shape=(s, k, n), o_spec=pl.BlockSpec((None, k, n), lambda j, i, kk: (j, 0, 0)),
                   acc_shape=(k, n), nk=nk, k_axis=2)


def _mm_tn_full(name, y, dh, tk, tko):
    t, k = y.shape
    n = dh.shape[1]
    tk = _whole_if_fits(t, tk, 2 * (t * tko * y.dtype.itemsize + t * n * dh.dtype.itemsize + tko * n * 4))
    nk = t // tk
    return _matmul(name, y, dh, dims=_TN, grid=(k // tko, 1, nk),
                   a_spec=pl.BlockSpec((tk, tko), lambda j, i, kk: (kk, j)),
                   b_spec=pl.BlockSpec((tk, n), lambda j, i, kk: (kk, 0)),
                   o_shape=(k, n), o_spec=pl.BlockSpec((tko, n), lambda j, i, kk: (j, 0)),
                   acc_shape=(tko, n), nk=nk, k_axis=2)


def _sigmoid(x):
    return 1.0 / (1.0 + jnp.exp(-x))


def _rms_fwd(name, h, g, tr):
    t, d = h.shape

    def fn(hv, gv):
        r = lax.rsqrt(jnp.mean(hv * hv, axis=-1, keepdims=True) + RMS_EPS)
        return hv * r * gv

    return _call(fn, name, (t // tr,), [(h, (tr, d), lambda i: (i, 0)), (g, (1, d), lambda i: (0, 0))],
                 [((t, d), (tr, d), lambda i: (i, 0), False, MXU_DTYPE)])


def _rms_bwd(name, h, g, dhn, dh, tr):
    t, d = h.shape

    def fn(hv, gv, dy, dh_in):
        r = lax.rsqrt(jnp.mean(hv * hv, axis=-1, keepdims=True) + RMS_EPS)
        xh = hv * r
        dg = jnp.sum(dy * xh, axis=0, keepdims=True)
        dxh = dy * gv
        dx = r * (dxh - xh * jnp.mean(dxh * xh, axis=-1, keepdims=True))
        return dh_in + dx, dg

    row = lambda i: (i, 0)
    return _call(fn, name, (t // tr,),
                 [(h, (tr, d), row), (g, (1, d), lambda i: (0, 0)), (dhn, (tr, d), row), (dh, (tr, d), row)],
                 [((t, d), (tr, d), row, False), ((1, d), (1, d), lambda i: (0, 0), True)], acc_axis=0)


def _final_loss(name, h, g, tgt, tr):
    t, d = h.shape

    def fn(hv, gv, tv):
        r = lax.rsqrt(jnp.mean(hv * hv, axis=-1, keepdims=True) + RMS_EPS)
        xh = hv * r
        row = pl.program_id(0) * tr + lax.broadcasted_iota(jnp.int32, (tr, 1), 0)
        e = jnp.where(row >= N_META, xh * gv - tv, 0.0)
        loss = jnp.broadcast_to(0.5 * jnp.sum(jnp.sum(e * e, axis=-1, keepdims=True), axis=0, keepdims=True) / d,
                                (8, LANES))
        dout = e / d
        dg = jnp.sum(dout * xh, axis=0, keepdims=True)
        dxh = dout * gv
        dx = r * (dxh - xh * jnp.mean(dxh * xh, axis=-1, keepdims=True))
        return loss, dx, dg

    row = lambda i: (i, 0)
    fix = lambda i: (0, 0)
    return _call(fn, name, (t // tr,), [(h, (tr, d), row), (g, (1, d), fix), (tgt, (tr, d), row)],
                 [((8, LANES), (8, LANES), fix, True), ((t, d), (tr, d), row, False), ((1, d), (1, d), fix, True)],
                 acc_axis=0)


def _silu_ln(uc, g, b):
    mu = jnp.mean(uc, axis=-1, keepdims=True)
    xc = uc - mu
    rs = lax.rsqrt(jnp.mean(xc * xc, axis=-1, keepdims=True) + LN_EPS)
    ln = xc * rs * g + b
    return ln * _sigmoid(ln)


def _even_ln_fwd(name, uc, g, b, tr):
    t, d = uc.shape
    row, fix = (lambda i: (i, 0)), (lambda i: (0, 0))
    return _call(_silu_ln, name, (t // tr,), [(uc, (tr, d), row), (g, (1, d), fix), (b, (1, d), fix)],
                 [((t, d), (tr, d), row, False, MXU_DTYPE)])


def _even_ln_bwd(name, uc, g, b, dy, dy_col, tr):
    t, d = uc.shape

    def fn(ucv, gv, bv, dyv):
        mu = jnp.mean(ucv, axis=-1, keepdims=True)
        xc = ucv - mu
        rs = lax.rsqrt(jnp.mean(xc * xc, axis=-1, keepdims=True) + LN_EPS)
        xh = xc * rs
        ln = xh * gv + bv
        s = _sigmoid(ln)
        dln = dyv * (s * (1.0 + ln * (1.0 - s)))
        dg = jnp.sum(dln * xh, axis=0, keepdims=True)
        db = jnp.sum(dln, axis=0, keepdims=True)
        dxh = dln * gv
        duc = rs * (dxh - jnp.mean(dxh, axis=-1, keepdims=True) - xh * jnp.mean(dxh * xh, axis=-1, keepdims=True))
        return duc, dg, db

    row, fix = (lambda i: (i, 0)), (lambda i: (0, 0))
    return _call(fn, name, (t // tr,),
                 [(uc, (tr, d), row), (g, (1, d), fix), (b, (1, d), fix), (dy, (tr, d), lambda i: (i, dy_col))],
                 [((t, d), (tr, d), row, False), ((1, d), (1, d), fix, True), ((1, d), (1, d), fix, True)], acc_axis=0)


def _windows(t):
    rc = _chunk_len(t)
    return [(r0, rc) for r0 in range(0, t, rc)]


def _taps(w_ref, width):
    return [w_ref[pl.ds(j, 1), :] for j in range(width)]


def _conv_at(xp, taps, r0, rc):
    width = len(taps)
    acc = None
    for j in range(width):
        term = xp[pl.ds(CONV_PAD - (width - 1) + j + r0, rc), :] * taps[j]
        acc = term if acc is None else acc + term
    return acc


def _conv_bwd_in_at(dyp, taps, r0, rc):
    width = len(taps)
    acc = None
    for j in range(width):
        term = dyp[pl.ds(width - 1 - j + r0, rc), :] * taps[j]
        acc = term if acc is None else acc + term
    return acc


def _fold(x):
    acc = x[0:8]
    for i in range(1, x.shape[0] // 8):
        acc = acc + x[8 * i:8 * (i + 1)]
    return acc


def _add_to(accs, vals):
    return vals if accs is None else [a + v for a, v in zip(accs, vals)]


def _conv_bwd_w_at(dy, xp, width, r0, rc):
    return [_fold(dy * xp[pl.ds(CONV_PAD - (width - 1) + j + r0, rc), :]) for j in range(width)]


def _store_taps(dw_ref, accs):
    for j, a in enumerate(accs):
        dw_ref[pl.ds(j, 1), :] = jnp.sum(a, axis=0, keepdims=True)


WIDE_COLS = 2 * LANES


def _zero_front(xp):
    xp[pl.ds(0, CONV_PAD), :] = jnp.zeros((CONV_PAD, xp.shape[1]), F32)


def _zero_back(dyp, t):
    dyp[pl.ds(t, CONV_PAD), :] = jnp.zeros((CONV_PAD, dyp.shape[1]), F32)


def _col_call(body, name, ncol, ins, outs, t, n_scratch, cols=LANES):
    def spec(rows, off):
        return pl.BlockSpec((rows, cols), lambda j, off=off: (0, j + off))

    res = pl.pallas_call(
        body, name=name, grid=(ncol,),
        in_specs=[spec(r, off) for _, r, off in ins],
        out_specs=[spec(o[0], 0) for o in outs],
        out_shape=[jax.ShapeDtypeStruct(o[:2], o[2] if len(o) > 2 else F32) for o in outs],
        scratch_shapes=[pltpu.VMEM((t + CONV_PAD, cols), F32) for _ in range(n_scratch)],
        compiler_params=_cparams(None),
    )(*[a for a, _, _ in ins])
    return res


def _even_col_fwd(name, p, conv_a, conv_b):
    t = p.shape[0]
    nc = D_A // LANES
    wins = _windows(t)

    def body(av, ag, gb, gc, xi, ca, cb, uc_ref, yb_ref, xp):
        _zero_front(xp)
        for r0, rc in wins:
            rows = pl.ds(r0, rc)
            xp[pl.ds(CONV_PAD + r0, rc), :] = av[rows, :] * _sigmoid(ag[rows, :])
        taps = _taps(ca, CONV_A_WIDTH)
        for r0, rc in wins:
            uc_ref[pl.ds(r0, rc), :] = _conv_at(xp, taps, r0, rc)
        for r0, rc in wins:
            rows = pl.ds(r0, rc)
            xp[pl.ds(CONV_PAD + r0, rc), :] = gc[rows, :] * xi[rows, :]
        taps = _taps(cb, CONV_B_WIDTH)
        for r0, rc in wins:
            rows = pl.ds(r0, rc)
            yb_ref[rows, :] = (gb[rows, :] * _conv_at(xp, taps, r0, rc)).astype(yb_ref.dtype)

    ins = [(p, t, k * nc) for k in range(5)] + [(conv_a, CONV_A_WIDTH, 0), (conv_b, CONV_B_WIDTH, 0)]
    return _col_call(body, name, nc, ins, [(t, D_A), (t, D_A, MXU_DTYPE)], t, 1)


def _even_col_bwd(name, p, duc, dy, conv_a, conv_b):
    t = p.shape[0]
    nc = D_A // LANES
    wins = _windows(t)

    def body(av, ag, gb, gc, xi, duc_ref, dyb_ref, ca, cb, dav, dag, dgb, dgc, dxi, dca, dcb, xp, dyp):
        _zero_front(xp)
        _zero_back(dyp, t)
        for r0, rc in wins:
            rows = pl.ds(r0, rc)
            xp[pl.ds(CONV_PAD + r0, rc), :] = av[rows, :] * _sigmoid(ag[rows, :])
            dyp[rows, :] = duc_ref[rows, :]
        taps = _taps(ca, CONV_A_WIDTH)
        accs = None
        for r0, rc in wins:
            rows = pl.ds(r0, rc)
            accs = _add_to(accs, _conv_bwd_w_at(duc_ref[rows, :], xp, CONV_A_WIDTH, r0, rc))
            du = _conv_bwd_in_at(dyp, taps, r0, rc)
            sig = _sigmoid(ag[rows, :])
            dav[rows, :] = (du * sig).astype(dav.dtype)
            dag[rows, :] = (du * av[rows, :] * sig * (1.0 - sig)).astype(dag.dtype)
        _store_taps(dca, accs)
        for r0, rc in wins:
            rows = pl.ds(r0, rc)
            xp[pl.ds(CONV_PAD + r0, rc), :] = gc[rows, :] * xi[rows, :]
        taps = _taps(cb, CONV_B_WIDTH)
        accs = None
        for r0, rc in wins:
            rows = pl.ds(r0, rc)
            dgb[rows, :] = (dyb_ref[rows, :] * _conv_at(xp, taps, r0, rc)).astype(dgb.dtype)
            dzc = dyb_ref[rows, :] * gb[rows, :]
            dyp[rows, :] = dzc
            accs = _add_to(accs, _conv_bwd_w_at(dzc, xp, CONV_B_WIDTH, r0, rc))
        _store_taps(dcb, accs)
        for r0, rc in wins:
            rows = pl.ds(r0, rc)
            dz = _conv_bwd_in_at(dyp, taps, r0, rc)
            dgc[rows, :] = (dz * xi[rows, :]).astype(dgc.dtype)
            dxi[rows, :] = (dz * gc[rows, :]).astype(dxi.dtype)

    ins = ([(p, t, k * nc) for k in range(5)] + [(duc, t, 0), (dy, t, nc)]
           + [(conv_a, CONV_A_WIDTH, 0), (conv_b, CONV_B_WIDTH, 0)])
    outs = [(t, D_A, MXU_DTYPE)] * 5 + [(CONV_A_WIDTH, D_A), (CONV_B_WIDTH, D_A)]
    return _col_call(body, name, nc, ins, outs, t, 2)


def _ffn_col_fwd(name, u, conv, bias):
    t = u.shape[0]
    nc = D_FF // WIDE_COLS
    wins = _windows(t)

    def body(g_ref, v_ref, cw, b_ref, a_ref, xp):
        _zero_front(xp)
        xp[pl.ds(CONV_PAD, t), :] = g_ref[...].astype(F32)
        taps = _taps(cw, CONV_B_WIDTH)
        b = b_ref[...]
        for r0, rc in wins:
            rows = pl.ds(r0, rc)
            gc = _conv_at(xp, taps, r0, rc) + b
            a_ref[rows, :] = (gc * _sigmoid(gc) * v_ref[rows, :].astype(F32)).astype(a_ref.dtype)

    ins = [(u, t, 0), (u, t, nc), (conv, CONV_B_WIDTH, 0), (bias, 1, 0)]
    return _col_call(body, name, nc, ins, [(t, D_FF, MXU_DTYPE)], t, 1, cols=WIDE_COLS)[0]


def _ffn_col_bwd(name, u, da, conv, bias):
    t = u.shape[0]
    nc = D_FF // LANES
    wins = _windows(t)

    def body(g_ref, v_ref, da_ref, cw, b_ref, du_ref, dcw, db_ref, xp, dyp, dval):
        @pl.when(pl.program_id(1) == 0)
        def _():
            _zero_front(xp)
            _zero_back(dyp, t)
            xp[pl.ds(CONV_PAD, t), :] = g_ref[...].astype(F32)
            taps = _taps(cw, CONV_B_WIDTH)
            b = b_ref[...]
            accs, bias_acc = None, None
            for r0, rc in wins:
                rows = pl.ds(r0, rc)
                gc = _conv_at(xp, taps, r0, rc) + b
                s = _sigmoid(gc)
                d = da_ref[rows, :]
                dval[rows, :] = d * gc * s
                dgc = d * v_ref[rows, :].astype(F32) * (s * (1.0 + gc * (1.0 - s)))
                dyp[rows, :] = dgc
                bias_acc = _add_to(bias_acc, [_fold(dgc)])
                accs = _add_to(accs, _conv_bwd_w_at(dgc, xp, CONV_B_WIDTH, r0, rc))
            db_ref[...] = jnp.sum(bias_acc[0], axis=0, keepdims=True)
            _store_taps(dcw, accs)
            for r0, rc in wins:
                du_ref[pl.ds(r0, rc), :] = _conv_bwd_in_at(dyp, taps, r0, rc).astype(du_ref.dtype)

        @pl.when(pl.program_id(1) == 1)
        def _():
            du_ref[...] = dval[...].astype(du_ref.dtype)

    col = lambda rows, off: pl.BlockSpec((rows, LANES), lambda j, p: (0, j + off))
    return pl.pallas_call(
        body, name=name, grid=(nc, 2),
        in_specs=[col(t, 0), col(t, nc), col(t, 0), col(CONV_B_WIDTH, 0), col(1, 0)],
        out_specs=[pl.BlockSpec((t, LANES), lambda j, p: (0, j + nc * p)), col(CONV_B_WIDTH, 0), col(1, 0)],
        out_shape=[jax.ShapeDtypeStruct((t, 2 * D_FF), MXU_DTYPE), jax.ShapeDtypeStruct((CONV_B_WIDTH, D_FF), F32),
                   jax.ShapeDtypeStruct((1, D_FF), F32)],
        scratch_shapes=[pltpu.VMEM((t + CONV_PAD, LANES), F32) for _ in range(2)] + [pltpu.VMEM((t, LANES), F32)],
        compiler_params=_cparams(None),
    )(u, u, da, conv, bias)


def _shift_fwd(name, p, col0, mu):
    t = p.shape[0]
    wins = _windows(t)

    def body(x_ref, mu_ref, o_ref, xp):
        _zero_front(xp)
        xp[pl.ds(CONV_PAD, t), :] = x_ref[...]
        mu_v = mu_ref[...]
        for r0, rc in wins:
            rows = pl.ds(r0, rc)
            x = x_ref[rows, :]
            o_ref[rows, :] = x + (xp[pl.ds(CONV_PAD - 1 + r0, rc), :] - x) * mu_v

    return _col_call(body, name, RWKV_COLS // WIDE_COLS, [(p, t, col0 // WIDE_COLS), (mu, 1, 0)], [(t, RWKV_COLS)], t, 1,
                     cols=WIDE_COLS)[0]


def _shift_bwd(name, p, col0, mu, dprs):
    t = p.shape[0]
    wins = _windows(t)

    def body(x_ref, mu_ref, d_ref, dx_ref, dmu_ref, xp, dyp):
        _zero_front(xp)
        _zero_back(dyp, t)
        xp[pl.ds(CONV_PAD, t), :] = x_ref[...]
        mu_v = mu_ref[...]
        acc = None
        for r0, rc in wins:
            rows = pl.ds(r0, rc)
            d = d_ref[rows, :]
            acc = _add_to(acc, [_fold(d * (xp[pl.ds(CONV_PAD - 1 + r0, rc), :] - x_ref[rows, :]))])
            dyp[rows, :] = d * mu_v
        dmu_ref[...] = jnp.sum(acc[0], axis=0, keepdims=True)
        for r0, rc in wins:
            rows = pl.ds(r0, rc)
            dx_ref[rows, :] = d_ref[rows, :] - dyp[rows, :] + dyp[pl.ds(1 + r0, rc), :]

    ins = [(p, t, col0 // WIDE_COLS), (mu, 1, 0), (dprs, t, 0)]
    return _col_call(body, name, RWKV_COLS // WIDE_COLS, ins, [(t, RWKV_COLS), (1, RWKV_COLS)], t, 2, cols=WIDE_COLS)


def _hi_lo(x):
    hi = x.astype(BF16)
    return hi, (x - hi.astype(F32)).astype(BF16)


def _dot_passes(a, b, dims, passes):
    d = lambda p, q: lax.dot_general(p, q, dims, preferred_element_type=F32)
    if passes == 1:
        return d(a.astype(MXU_DTYPE), b.astype(MXU_DTYPE))
    ah, al = _hi_lo(a)
    bh, bl = _hi_lo(b)
    return d(ah, bh) + (d(ah, bl) + d(al, bh))


@functools.partial(jax.custom_vjp, nondiff_argnums=(2, 3))
def _dot_vjp(a, b, dims, passes):
    return _dot_passes(a, b, dims, passes)


def _dot_fwd(a, b, dims, passes):
    return _dot_passes(a, b, dims, passes), (a, b)


def _dot_bwd(dims, passes, res, g):
    a, b = res
    if dims == _NN:
        return _dot_passes(g, b, _NT, passes), _dot_passes(a, g, _TN, passes)
    if dims == _NT:
        return _dot_passes(g, b, _NN, passes), _dot_passes(g, a, _TN, passes)
    return _dot_passes(b, g, _NT, passes), _dot_passes(a, g, _NN, passes)


_dot_vjp.defvjp(_dot_fwd, _dot_bwd)


def _doth(a, b, dims=_NN):
    return _dot_vjp(a, b, dims, 3)


def _dotb(a, b, dims=_NN):
    return _dot_vjp(a, b, dims, 1)


def _softplus(x):
    return jnp.where(x > 0, x, 0.0) + jnp.log(1.0 + jnp.exp(jnp.where(x > 0, -x, x)))


def _rwkv_pre(k, xl, gd, w0, w2p, a0, a2p, g2, k_k, k_a, seg):
    z = w0 + _dotb(jnp.tanh(xl), w2p)
    lw = -jnp.exp(-_softplus(-z) - 0.5)
    alpha = _sigmoid(a0 + _dotb(xl, a2p))
    g = _dotb(_sigmoid(gd), g2)
    kk = k * k_k
    kk = kk / jnp.maximum(jnp.sqrt(_dotb(kk * kk, seg)), 1e-12)
    k2 = k * (1.0 + (alpha - 1.0) * k_a)
    return lw, k2, -kk, kk * alpha, g


def _rwkv_post(y, r, k2, v, g, lnx_g, lnx_b, r_k, seg):
    mean = _dotb(y, seg) * (1.0 / HEAD_DIM)
    yc = y - mean
    var = _dotb(yc * yc, seg) * (1.0 / HEAD_DIM)
    yo = yc * lax.rsqrt(var + RWKV_GN_EPS) * lnx_g + lnx_b
    bonus = _dotb(r * k2 * r_k, seg) * v
    return (yo + bonus) * g


def _rwkv_pre_fwd(name, prs, prm, seg, tr):
    t = prs.shape[0]
    row = lambda i: (i, 0)
    fix = lambda i: (0, 0)
    ins = [(prs, (tr, D_R), lambda i: (i, 1)), (prs, (tr, LANES), lambda i: (i, 12)), (prs, (tr, LANES), lambda i: (i, 13)),
           (prm["w0"], (1, D_R), fix), (prm["w2p"], (LANES, D_R), fix), (prm["a0"], (1, D_R), fix),
           (prm["a2p"], (LANES, D_R), fix), (prm["g2"], (LANES, D_R), fix), (prm["k_k"], (1, D_R), fix),
           (prm["k_a"], (1, D_R), fix), (seg, (D_R, D_R), fix)]
    return _call(_rwkv_pre, name, (t // tr,), ins, [((t, D_R), (tr, D_R), row, False)] * 5)


def _rwkv_pre_bwd(name, prs, prm, seg, cts, tr):
    t = prs.shape[0]

    def fn(k, xl, gd, w0, w2p, a0, a2p, g2, k_k, k_a, segv, *ct):
        _, vjp = jax.vjp(lambda *a: _rwkv_pre(*a, segv), k, xl, gd, w0, w2p, a0, a2p, g2, k_k, k_a)
        return vjp(tuple(ct))

    row = lambda i: (i, 0)
    fix = lambda i: (0, 0)
    ins = [(prs, (tr, D_R), lambda i: (i, 1)), (prs, (tr, LANES), lambda i: (i, 12)), (prs, (tr, LANES), lambda i: (i, 13)),
           (prm["w0"], (1, D_R), fix), (prm["w2p"], (LANES, D_R), fix), (prm["a0"], (1, D_R), fix),
           (prm["a2p"], (LANES, D_R), fix), (prm["g2"], (LANES, D_R), fix), (prm["k_k"], (1, D_R), fix),
           (prm["k_a"], (1, D_R), fix), (seg, (D_R, D_R), fix)] + [(c, (tr, D_R), row) for c in cts]
    outs = [((t, D_R), (tr, D_R), row, False), ((t, LANES), (tr, LANES), row, False), ((t, LANES), (tr, LANES), row, False),
            ((1, D_R), (1, D_R), fix, True), ((LANES, D_R), (LANES, D_R), fix, True), ((1, D_R), (1, D_R), fix, True),
            ((LANES, D_R), (LANES, D_R), fix, True), ((LANES, D_R), (LANES, D_R), fix, True),
            ((1, D_R), (1, D_R), fix, True), ((1, D_R), (1, D_R), fix, True)]
    return _call(fn, name, (t // tr,), ins, outs, acc_axis=0)


def _rwkv_post_ins(y, prs, k2, g, prm, seg, tr):
    row = lambda i: (i, 0)
    fix = lambda i: (0, 0)
    return [(y, (tr, D_R), row), (prs, (tr, D_R), row), (k2, (tr, D_R), row), (prs, (tr, D_R), lambda i: (i, 2)),
            (g, (tr, D_R), row), (prm["lnx_g"], (1, D_R), fix), (prm["lnx_b"], (1, D_R), fix), (prm["r_k"], (1, D_R), fix),
            (seg, (D_R, D_R), fix)]


def _rwkv_post_fwd(name, y, prs, k2, g, prm, seg, tr):
    t = y.shape[0]
    return _call(_rwkv_post, name, (t // tr,), _rwkv_post_ins(y, prs, k2, g, prm, seg, tr),
                 [((t, D_R), (tr, D_R), lambda i: (i, 0), False)])


def _rwkv_post_bwd(name, y, prs, k2, g, prm, seg, dy, dy_col, tr):
    t = y.shape[0]

    def fn(yv, r, k2v, v, gv, lg, lb, rk, segv, ct):
        _, vjp = jax.vjp(lambda *a: _rwkv_post(*a, segv), yv, r, k2v, v, gv, lg, lb, rk)
        return vjp(ct)

    row = lambda i: (i, 0)
    fix = lambda i: (0, 0)
    ins = _rwkv_post_ins(y, prs, k2, g, prm, seg, tr) + [(dy, (tr, D_R), lambda i: (i, dy_col))]
    outs = [((t, D_R), (tr, D_R), row, False)] * 5 + [((1, D_R), (1, D_R), fix, True)] * 3
    return _call(fn, name, (t // tr,), ins, outs, acc_axis=0)


def _wkv_chunk(s0, r, lw, k, v, a, b):
    c = r[0].shape[0]
    lane = lax.broadcasted_iota(jnp.int32, (1, 2 * HEAD_DIM), 1)
    first = (lane < HEAD_DIM).astype(F32)
    per_head = lambda x: jnp.concatenate([x * first, x * (1.0 - first)], axis=0)

    def time_of(shape, dim):
        i = lax.broadcasted_iota(jnp.int32, shape, dim)
        return jnp.where(i >= c, i - c, i)

    incl = (lax.broadcasted_iota(jnp.int32, (c, c), 0) >= lax.broadcasted_iota(jnp.int32, (c, c), 1)).astype(F32)
    strict2 = time_of((2 * c, 2 * c), 0) > time_of((2 * c, 2 * c), 1)
    incl2 = lax.broadcasted_iota(jnp.int32, (c, 2 * c), 0) >= time_of((c, 2 * c), 1)
    each = lambda f, *xs: [f(*x) for x in zip(*xs)]
    cum = each(lambda x: _doth(incl, x), lw)
    tot = each(lambda x: jnp.sum(x, axis=0, keepdims=True), lw)
    e_inv = each(lambda x: jnp.exp(-x), cum)
    a_st = each(lambda x, cm, l: per_head(x * jnp.exp(cm - l)), a, cum, lw)
    r_t = each(lambda x, cm: x * jnp.exp(cm), r, cum)
    b_st = each(lambda x, e: per_head(x * e), b, e_inv)
    k_st = each(lambda x, e: per_head(x * e), k, e_inv)
    v_st = each(per_head, v)
    m = each(lambda x, w: jnp.where(strict2, _dotb(x, w, _NT), 0.0), a_st, b_st)
    m_k = each(lambda x, w: jnp.where(strict2, _dotb(x, w, _NT), 0.0), a_st, k_st)
    u = each(lambda x, s, mk, w: _dotb(x, s, _NT) + _dotb(mk, w), a_st, s0, m_k, v_st)
    steps = (c - 1).bit_length()
    for s in range(steps):
        u = each(lambda x, w: x + _dotb(w, x), u, m)
        if s + 1 < steps:
            m = each(lambda w: _dotb(w, w), m)
    n_b = each(lambda x, w: jnp.where(incl2, _dotb(x, w, _NT), 0.0), r_t, b_st)
    n_k = each(lambda x, w: jnp.where(incl2, _dotb(x, w, _NT), 0.0), r_t, k_st)
    y = each(lambda x, s, nb, uu, nk, w: _dotb(x, s, _NT) + _dotb(nb, uu) + _dotb(nk, w), r_t, s0, n_b, u, n_k, v_st)
    dec = each(lambda tt, cm: jnp.exp(tt - cm), tot, cum)
    s1 = each(lambda s, tt, uu, x, d, w, kk: s * jnp.exp(tt) + _dotb(uu, per_head(x * d), _TN) + _dotb(w, per_head(kk * d), _TN),
              s0, tot, u, b, dec, v_st, k)
    return tuple(y), tuple(s1)


WKV_PAIRS_PER_STEP = 4
PAIR = 2 * HEAD_DIM


def _wkv_fwd(name, srcs):
    t = srcs[0][0].shape[0]
    c = _chunk_len(t)
    nc = t // c
    pp = WKV_PAIRS_PER_STEP
    n_pairs = D_R // PAIR

    def body(r, lw, k, v, a, b, y_ref, st_ref, state):
        @pl.when(pl.program_id(1) == 0)
        def _():
            state[...] = jnp.zeros(state.shape, F32)

        pairs = lambda ref: tuple(ref[:, pl.ds(i * PAIR, PAIR)] for i in range(pp))
        s0 = tuple(state[i] for i in range(pp))
        y, s1 = _wkv_chunk(s0, pairs(r), pairs(lw), pairs(k), pairs(v), pairs(a), pairs(b))
        for i in range(pp):
            st_ref[i] = s0[i]
            y_ref[:, pl.ds(i * PAIR, PAIR)] = y[i]
            state[i] = s1[i]

    seq = lambda off: pl.BlockSpec((c, pp * PAIR), lambda g, j: (j, off + g))
    return pl.pallas_call(
        body, name=name, grid=(n_pairs // pp, nc), in_specs=[seq(off) for _, off in srcs],
        out_specs=[seq(0), pl.BlockSpec((pp, None, PAIR, PAIR), lambda g, j: (g, j, 0, 0))],
        out_shape=[jax.ShapeDtypeStruct((t, D_R), F32), jax.ShapeDtypeStruct((n_pairs, nc, PAIR, PAIR), F32)],
        scratch_shapes=[pltpu.VMEM((pp, PAIR, PAIR), F32)],
        compiler_params=_cparams(None),
    )(*[a for a, _ in srcs])


def _wkv_bwd(name, srcs, st, dy):
    t = srcs[0][0].shape[0]
    c = _chunk_len(t)
    nc = t // c
    pp = WKV_PAIRS_PER_STEP
    n_pairs = D_R // PAIR

    def body(r, lw, k, v, a, b, st_ref, dy_ref, dr, dlw, dk, dv, da, db, dstate):
        @pl.when(pl.program_id(1) == 0)
        def _():
            dstate[...] = jnp.zeros(dstate.shape, F32)

        half = lax.broadcasted_iota(jnp.int32, (PAIR, PAIR), 0) < HEAD_DIM
        same_head = half == (lax.broadcasted_iota(jnp.int32, (PAIR, PAIR), 1) < HEAD_DIM)
        pairs = lambda ref: tuple(ref[:, pl.ds(i * PAIR, PAIR)] for i in range(pp))
        s0 = tuple(st_ref[i] for i in range(pp))
        _, vjp = jax.vjp(_wkv_chunk, s0, pairs(r), pairs(lw), pairs(k), pairs(v), pairs(a), pairs(b))
        ds0, *dxs = vjp((pairs(dy_ref), tuple(dstate[i] for i in range(pp))))
        for i in range(pp):
            for ref, val in zip((dr, dlw, dk, dv, da, db), dxs):
                ref[:, pl.ds(i * PAIR, PAIR)] = val[i]
            dstate[i] = jnp.where(same_head, ds0[i], 0.0)

    seq = lambda off: pl.BlockSpec((c, pp * PAIR), lambda g, j: (nc - 1 - j, off + g))
    return pl.pallas_call(
        body, name=name, grid=(n_pairs // pp, nc),
        in_specs=[seq(off) for _, off in srcs]
        + [pl.BlockSpec((pp, None, PAIR, PAIR), lambda g, j: (g, nc - 1 - j, 0, 0)), seq(dy[1])],
        out_specs=[seq(0)] * 6,
        out_shape=[jax.ShapeDtypeStruct((t, D_R), F32)] * 6,
        scratch_shapes=[pltpu.VMEM((pp, PAIR, PAIR), F32)],
        compiler_params=_cparams(None),
    )(*[a for a, _ in srcs], st, dy[0])


def _rope(x, cos, sin, rot):
    return x * cos + _dotb(x, rot) * sin


def _attn_block(nb, q, kp, kc, km, vp, vc, vm, sk, cq, sq, cp, sp, cm, sm, rot):
    g = GQA_GROUP
    scale = HEAD_DIM ** -0.5
    each = lambda f, *xs: [f(*x) for x in zip(*xs)]
    down = lambda x: jnp.concatenate([x] * g, axis=0)
    cq4, sq4 = down(cq), down(sq)
    kpr = each(lambda x: _rope(x, cp, sp, rot), kp)
    kcr = each(lambda x: _rope(x, cq, sq, rot), kc)
    kmr = each(lambda x: _rope(x, cm, sm, rot), km)
    qr = each(lambda x: _rope(x, cq4, sq4, rot), q)
    i = lax.broadcasted_iota(jnp.int32, (g * BLOCK, BLOCK), 0)
    i = i - BLOCK * ((i >= BLOCK).astype(jnp.int32) + (i >= 2 * BLOCK).astype(jnp.int32) + (i >= 3 * BLOCK).astype(jnp.int32))
    j = lax.broadcasted_iota(jnp.int32, (g * BLOCK, BLOCK), 1)
    nbv = jnp.zeros((g * BLOCK, BLOCK), jnp.int32) + nb
    ok_p = (j > i) & (nbv >= 2)
    ok_c = (j <= i) & (nbv >= 1)
    ok_m = (j >= BLOCK - N_META) & ((nbv >= 1) | (j <= i))
    sink = each(lambda s4: jnp.concatenate([jnp.broadcast_to(s, (BLOCK, 1)) for s in s4], axis=0), sk)
    s_p = each(lambda x, kk: jnp.where(ok_p, _dotb(x, kk, _NT) * scale, NEG_INF), qr, kpr)
    s_c = each(lambda x, kk: jnp.where(ok_c, _dotb(x, kk, _NT) * scale, NEG_INF), qr, kcr)
    s_m = each(lambda x, kk: jnp.where(ok_m, _dotb(x, kk, _NT) * scale, NEG_INF), qr, kmr)
    rmax = lambda s: jnp.max(s, axis=-1, keepdims=True)
    m = each(lambda a, b, c, d: lax.stop_gradient(jnp.maximum(jnp.maximum(rmax(a), rmax(b)), jnp.maximum(rmax(c), d))),
             s_p, s_c, s_m, sink)
    e_p = each(lambda s, mm: jnp.exp(s - mm), s_p, m)
    e_c = each(lambda s, mm: jnp.exp(s - mm), s_c, m)
    e_m = each(lambda s, mm: jnp.exp(s - mm), s_m, m)
    rsum = lambda e: jnp.sum(e, axis=-1, keepdims=True)
    inv = each(lambda a, b, c, d, mm: 1.0 / (rsum(a) + rsum(b) + rsum(c) + jnp.exp(d - mm)), e_p, e_c, e_m, sink, m)
    return tuple(each(lambda a, b, c, iv, x, y, z: _dotb(a * iv, x) + _dotb(b * iv, y) + _dotb(c * iv, z),
                      e_p, e_c, e_m, inv, vp, vc, vm))


def _attn_specs():
    cur = lambda n: (0, n, 0)
    prev = lambda n: (0, jnp.maximum(n - 1, 0), 0)
    meta = lambda n: (0, 0, 0)
    kv = lambda m: pl.BlockSpec((N_KV_HEADS, BLOCK, HEAD_DIM), m)
    tab = lambda m: pl.BlockSpec((BLOCK, HEAD_DIM), m)
    tcur, tprev, tmeta = (lambda n: (n, 0)), (lambda n: (jnp.maximum(n - 1, 0), 0)), (lambda n: (0, 0))
    qspec = pl.BlockSpec((N_Q_HEADS, BLOCK, HEAD_DIM), cur)
    sspec = pl.BlockSpec((N_Q_HEADS, 8, LANES), meta)
    specs = [qspec, kv(prev), kv(cur), kv(meta), kv(prev), kv(cur), kv(meta), sspec,
             tab(tcur), tab(tcur), tab(tprev), tab(tprev), tab(tmeta), tab(tmeta),
             pl.BlockSpec((HEAD_DIM, HEAD_DIM), lambda n: (0, 0))]
    return specs, qspec, sspec, kv


def _attn_args(q, k, v, sinks_b, cos, sin, rot):
    return (q, k, k, k, v, v, v, sinks_b, cos, sin, cos, sin, cos, sin, rot)


def _attn_operands(q_ref, kp, kc, km, vp, vc, vm, s_ref):
    groups = range(N_KV_HEADS)
    q = tuple(jnp.concatenate([q_ref[GQA_GROUP * i + h] for h in range(GQA_GROUP)], axis=0) for i in groups)
    sk = tuple(tuple(s_ref[GQA_GROUP * i + h][0:1, 0:1] for h in range(GQA_GROUP)) for i in groups)
    per_head = lambda ref: tuple(ref[i] for i in groups)
    return q, per_head(kp), per_head(kc), per_head(km), per_head(vp), per_head(vc), per_head(vm), sk


def _attn_fwd(name, q, k, v, sinks_b, cos, sin, rot):
    tp = q.shape[1]
    specs, qspec, _, _ = _attn_specs()

    def body(q_ref, kp, kc, km, vp, vc, vm, s_ref, cq, sq, cp, sp, cm, sm, rot_ref, o_ref):
        out = _attn_block(pl.program_id(0), *_attn_operands(q_ref, kp, kc, km, vp, vc, vm, s_ref),
                          cq[...], sq[...], cp[...], sp[...], cm[...], sm[...], rot_ref[...])
        for i in range(N_KV_HEADS):
            for h in range(GQA_GROUP):
                o_ref[GQA_GROUP * i + h] = out[i][h * BLOCK:(h + 1) * BLOCK]

    return pl.pallas_call(
        body, name=name, grid=(tp // BLOCK,), in_specs=specs, out_specs=qspec,
        out_shape=jax.ShapeDtypeStruct(q.shape, F32), compiler_params=_cparams(None),
    )(*_attn_args(q, k, v, sinks_b, cos, sin, rot))


def _attn_bwd(name, q, k, v, sinks_b, cos, sin, rot, do):
    tp = q.shape[1]
    nb = tp // BLOCK
    specs, qspec, sspec, kv = _attn_specs()

    def body(q_ref, kp, kc, km, vp, vc, vm, s_ref, cq, sq, cp, sp, cm, sm, rot_ref, do_ref,
             dq_ref, dkp, dkc, dvp, dvc, dkm, dvm, ds_ref):
        n = pl.program_id(0)
        tabs = (cq[...], sq[...], cp[...], sp[...], cm[...], sm[...], rot_ref[...])
        _, vjp = jax.vjp(lambda *a: _attn_block(n, *a, *tabs), *_attn_operands(q_ref, kp, kc, km, vp, vc, vm, s_ref))
        do_all = tuple(jnp.concatenate([do_ref[GQA_GROUP * i + h] for h in range(GQA_GROUP)], axis=0)
                       for i in range(N_KV_HEADS))
        dq, gkp, gkc, gkm, gvp, gvc, gvm, dsk = vjp(do_all)
        for i in range(N_KV_HEADS):
            dkp[i] = gkp[i]
            dkc[i] = gkc[i]
            dvp[i] = gvp[i]
            dvc[i] = gvc[i]
            for h in range(GQA_GROUP):
                dq_ref[GQA_GROUP * i + h] = dq[i][h * BLOCK:(h + 1) * BLOCK]

        @pl.when(n == 0)
        def _():
            for i in range(N_KV_HEADS):
                dkm[i] = gkm[i]
                dvm[i] = gvm[i]
                for h in range(GQA_GROUP):
                    ds_ref[GQA_GROUP * i + h] = jnp.broadcast_to(dsk[i][h], (8, LANES))

        @pl.when(n != 0)
        def _():
            for i in range(N_KV_HEADS):
                dkm[i] += gkm[i]
                dvm[i] += gvm[i]
                for h in range(GQA_GROUP):
                    ds_ref[GQA_GROUP * i + h] += jnp.broadcast_to(dsk[i][h], (8, LANES))

    part = pl.BlockSpec((N_KV_HEADS, None, BLOCK, HEAD_DIM), lambda n: (0, n, 0, 0))
    part_shape = jax.ShapeDtypeStruct((N_KV_HEADS, nb, BLOCK, HEAD_DIM), F32)
    meta_shape = jax.ShapeDtypeStruct((N_KV_HEADS, BLOCK, HEAD_DIM), F32)
    return pl.pallas_call(
        body, name=name, grid=(nb,), in_specs=specs + [qspec],
        out_specs=[qspec, part, part, part, part, kv(lambda n: (0, 0, 0)), kv(lambda n: (0, 0, 0)), sspec],
        out_shape=[jax.ShapeDtypeStruct(q.shape, F32), part_shape, part_shape, part_shape, part_shape,
                   meta_shape, meta_shape, jax.ShapeDtypeStruct(sinks_b.shape, F32)],
        compiler_params=_cparams(None),
    )(*_attn_args(q, k, v, sinks_b, cos, sin, rot), do)


def _kv_combine(name, k_parts, v_parts):
    g, nb = k_parts[1].shape[:2]

    def fn(own_k, nxt_k, mt_k, own_v, nxt_v, mt_v):
        m = pl.program_id(1)
        one = jnp.ones((BLOCK, HEAD_DIM), F32)
        use_next = jnp.where(one * m < nb - 1, 1.0, 0.0)
        use_meta = jnp.where(one * m < 1, 1.0, 0.0)
        return own_k + nxt_k * use_next + mt_k * use_meta, own_v + nxt_v * use_next + mt_v * use_meta

    blk = (None, None, BLOCK, HEAD_DIM)
    ins = []
    for prev_part, own_part, meta in (k_parts, v_parts):
        ins += [(own_part, blk, lambda a, m: (a, m, 0, 0)),
                (prev_part, blk, lambda a, m: (a, jnp.minimum(m + 1, nb - 1), 0, 0)),
                (meta, (None, BLOCK, HEAD_DIM), lambda a, m: (a, 0, 0))]
    return _call(fn, name, (g, nb), ins,
                 [((g, nb * BLOCK, HEAD_DIM), (None, BLOCK, HEAD_DIM), lambda a, m: (a, m, 0), False)] * 2)


PACK_W = 1024
ELEMENTWISE_BLOCK_BYTES = 1 << 21


def _rows_tile(rows, cols):
    cap = max(8, ELEMENTWISE_BLOCK_BYTES // (4 * cols))
    for d in range(min(rows, cap), 0, -1):
        if rows % d == 0 and d % 8 == 0:
            return d
    return rows


def _adamw(name, w, g, m, v):
    rows, cols = w.shape
    tr = _rows_tile(rows, cols)

    def fn(wv, gv, mv, vv):
        m1 = ADAM_B1 * mv + (1.0 - ADAM_B1) * gv
        v1 = ADAM_B2 * vv + (1.0 - ADAM_B2) * (gv * gv)
        m_hat = m1 / (1.0 - ADAM_B1 ** ADAM_STEP)
        v_hat = v1 / (1.0 - ADAM_B2 ** ADAM_STEP)
        return -ADAM_LR * (m_hat / (jnp.sqrt(v_hat) + ADAM_EPS) + ADAM_WD * wv), m1, v1

    blk = (tr, cols)
    row = lambda i: (i, 0)
    return _call(fn, name, (rows // tr,), [(a, blk, row) for a in (w, g, m, v)], [((rows, cols), blk, row, False)] * 3)


def _pair_add_placed(name, g, recv, cm_idx, out_dtype):
    s, a, b = g.shape
    half = a // 2

    def body(cm_ref, a_ref, b_ref, o_ref, own_ref):
        val = (a_ref[...] + b_ref[...]).astype(out_dtype)
        o_ref[...] = val

        @pl.when(pl.program_id(0) == cm_ref[1])
        def _():
            own_ref[...] = val

    blk = (None, half, b)
    shape = jax.ShapeDtypeStruct((s, half, b), out_dtype)
    return pl.pallas_call(
        body, name=name,
        grid_spec=pltpu.PrefetchScalarGridSpec(
            num_scalar_prefetch=1, grid=(s,),
            in_specs=[pl.BlockSpec(blk, lambda j, cm: (j, cm[0], 0)), pl.BlockSpec(blk, lambda j, cm: (j, 0, 0))],
            out_specs=[pl.BlockSpec(blk, lambda j, cm: (j, 0, 0)), pl.BlockSpec(blk, lambda j, cm: (cm[1], 0, 0))]),
        out_shape=[shape, shape], compiler_params=_cparams(None),
    )(cm_idx, g, recv)


def _sum_chips(name, parts, c_idx, layer, n_layers, into=None):
    _, a, b = parts.shape
    tr = _rows_tile(a, b)

    def body(c_ref, p0, p1, p2, p3, *rest):
        o_ref = rest[-1]
        up = lambda p: p[...].astype(F32)
        o_ref[...] = ((up(p0) + up(p1)) + up(p2)) + up(p3)

    in_specs = [pl.BlockSpec((None, tr, b), lambda i, c, k=k: (k, i, 0)) for k in range(N_CHIPS)]
    args = [c_idx] + [parts] * N_CHIPS
    aliases = {}
    if into is not None:
        in_specs.append(_ANY)
        args.append(into)
        aliases = {1 + N_CHIPS: 0}
    return pl.pallas_call(
        body, name=name,
        grid_spec=pltpu.PrefetchScalarGridSpec(
            num_scalar_prefetch=1, grid=(a // tr,), in_specs=in_specs,
            out_specs=pl.BlockSpec((None, None, tr, b), lambda i, c: (layer, c[0], i, 0))),
        out_shape=jax.ShapeDtypeStruct((n_layers, 2, a, b), F32), input_output_aliases=aliases,
        compiler_params=_cparams(None),
    )(*args)


def _place_own_block(name, w, layer, me_idx, dtype):
    _, a2, b = w.shape
    a = a2 // 2
    tr = _rows_tile(a, b)
    nb = a // tr

    def body(me_ref, w_ref, o_ref):
        o_ref[...] = w_ref[...].astype(dtype)

    return pl.pallas_call(
        body, name=name,
        grid_spec=pltpu.PrefetchScalarGridSpec(
            num_scalar_prefetch=1, grid=(2, nb),
            in_specs=[pl.BlockSpec((None, tr, b), lambda h, i, me: (layer, h * nb + i, 0))],
            out_specs=pl.BlockSpec((None, None, tr, b), lambda h, i, me: (me[0], h, i, 0))),
        out_shape=jax.ShapeDtypeStruct((N_CHIPS, 2, a, b), dtype), compiler_params=_cparams(None),
    )(me_idx, w)


def _mesh_pos():
    return lax.axis_index("x"), lax.axis_index("y"), lax.axis_index("c")


def _other_chips(x, y):
    return [(1 - x, y), (x, 1 - y), (1 - x, 1 - y)]


_ANY = pl.BlockSpec(memory_space=pl.ANY)


def _gather_weights(name, bufs, from_chips=True):
    n = len(bufs)

    def body(*refs):
        out_refs = refs[n:2 * n]
        send_sems, recv_sems = refs[2 * n:]
        x, y, c = _mesh_pos()
        me = 2 * x + y
        sibling = (x, y, 1 - c)
        chips = _other_chips(x, y)

        def copy(i, k, chip_idx, half, to):
            return pltpu.make_async_remote_copy(src_ref=out_refs[i].at[chip_idx, half], dst_ref=out_refs[i].at[chip_idx, half],
                                                send_sem=send_sems.at[6 * i + k], recv_sem=recv_sems.at[6 * i + k],
                                                device_id=to, device_id_type=MESH)

        first = [copy(i, j, me, c, (*chip, c)) for i in range(n) for j, chip in enumerate(chips)] if from_chips else []
        for cp in first:
            cp.start()
        passed = []
        for i in range(n):
            for j, (cx, cy) in enumerate(chips):
                idx = 2 * cx + cy
                if from_chips:
                    copy(i, j, idx, c, sibling).wait_recv()
                fwd = copy(i, 3 + j, idx, c, sibling)
                fwd.start()
                passed.append(fwd)
        for i in range(n):
            for j, (cx, cy) in enumerate(chips):
                copy(i, 3 + j, 2 * cx + cy, 1 - c, sibling).wait_recv()
        for cp in first + passed:
            cp.wait_send()

    return pl.pallas_call(
        body, name=name, in_specs=[_ANY] * n, out_specs=[_ANY] * n,
        out_shape=[jax.ShapeDtypeStruct(b.shape, b.dtype) for b in bufs],
        input_output_aliases={i: i for i in range(n)},
        scratch_shapes=[pltpu.SemaphoreType.DMA((6 * n,)), pltpu.SemaphoreType.DMA((6 * n,))],
        compiler_params=pltpu.CompilerParams(has_side_effects=True),
    )(*bufs)


def _gather_start(name, groups):
    bufs = [b for g in groups for b in g]
    n = len(bufs)
    ng = len(groups)

    def body(*refs):
        b_refs = refs[:n]
        sems = refs[n:n + 2 * ng]
        token = refs[-1]
        x, y, c = _mesh_pos()
        me = 2 * x + y
        i = 0
        for gi, g in enumerate(groups):
            for k in range(len(g)):
                for j, (cx, cy) in enumerate(_other_chips(x, y)):
                    pltpu.make_async_remote_copy(src_ref=b_refs[i].at[me, c], dst_ref=b_refs[i].at[me, c],
                                                 send_sem=sems[2 * gi].at[3 * k + j], recv_sem=sems[2 * gi + 1].at[3 * k + j],
                                                 device_id=(cx, cy, c), device_id_type=MESH).start()
                i += 1
        token[...] = jnp.zeros(token.shape, F32)

    sem_shapes = [pltpu.SemaphoreType.DMA((3 * len(g),)) for g in groups for _ in range(2)]
    res = pl.pallas_call(
        body, name=name,
        out_shape=(*sem_shapes, *[pltpu.HBM(b.shape, b.dtype) for b in bufs], jax.ShapeDtypeStruct((8, LANES), F32)),
        in_specs=[_HBM] * n,
        out_specs=(*[_SEM] * (2 * ng), *[_HBM] * n, pl.BlockSpec(memory_space=pltpu.VMEM)),
        input_output_aliases={i: 2 * ng + i for i in range(n)},
        compiler_params=pltpu.CompilerParams(has_side_effects=_DATAFLOW),
    )(*[pltpu.with_memory_space_constraint(b, pltpu.HBM) for b in bufs])
    out, i = [], 2 * ng
    for gi, g in enumerate(groups):
        out.append((res[2 * gi], res[2 * gi + 1], list(res[i:i + len(g)])))
        i += len(g)
    return out, res[-1]


def _gather_wait(name, send_sems, recv_sems, bufs, after):
    n = len(bufs)

    def body(*refs):
        b_refs = refs[:n]
        s_sems, r_sems = refs[n], refs[n + 1]
        x, y, c = _mesh_pos()
        me = 2 * x + y
        for k in range(n):
            for j, (cx, cy) in enumerate(_other_chips(x, y)):
                idx = 2 * cx + cy
                copy = pltpu.make_async_remote_copy(src_ref=b_refs[k].at[me, c], dst_ref=b_refs[k].at[idx, c],
                                                    send_sem=s_sems.at[3 * k + j], recv_sem=r_sems.at[3 * k + j],
                                                    device_id=(cx, cy, c), device_id_type=MESH)
                copy.wait_send()
                copy.wait_recv()

    res = pl.pallas_call(
        body, name=name,
        out_shape=tuple(pltpu.HBM(b.shape, b.dtype) for b in bufs),
        in_specs=[_HBM] * n + [_SEM, _SEM, _ANY],
        out_specs=tuple([_HBM] * n),
        input_output_aliases={i: i for i in range(n)},
        compiler_params=pltpu.CompilerParams(has_side_effects=_DATAFLOW),
    )(*bufs, send_sems, recv_sems, after)
    return list(res)


def _halves_to_sibling(name, units):
    n = len(units)

    def body(*refs):
        g_refs, out_refs = refs[:n], refs[n:2 * n]
        send_sems, recv_sems = refs[2 * n:]
        x, y, c = _mesh_pos()
        cps = []
        for i in range(n):
            half = units[i].shape[1] // 2
            src = g_refs[i].at[pl.ds(0, N_CHIPS), pl.ds((1 - c) * half, half)]
            cp = pltpu.make_async_remote_copy(src_ref=src, dst_ref=out_refs[i], send_sem=send_sems.at[i],
                                              recv_sem=recv_sems.at[i], device_id=(x, y, 1 - c), device_id_type=MESH)
            cp.start()
            cps.append(cp)
        for cp in cps:
            cp.wait()

    return pl.pallas_call(
        body, name=name, in_specs=[_ANY] * n, out_specs=[_ANY] * n,
        out_shape=[jax.ShapeDtypeStruct((u.shape[0], u.shape[1] // 2, u.shape[2]), u.dtype) for u in units],
        scratch_shapes=[pltpu.SemaphoreType.DMA((n,)), pltpu.SemaphoreType.DMA((n,))],
        compiler_params=pltpu.CompilerParams(has_side_effects=True),
    )(*units)


_HBM = pl.BlockSpec(memory_space=pltpu.HBM)
_SEM = pl.BlockSpec(memory_space=pltpu.SEMAPHORE)
_DATAFLOW = pltpu.SideEffectType.DATAFLOW_SIDE_EFFECTING


def _halves_start(name, units):
    n = len(units)

    def body(*refs):
        g_refs, z_refs = refs[:n], refs[n:2 * n]
        send_sems, recv_sems = refs[2 * n], refs[2 * n + 1]
        token = refs[-1]
        x, y, c = _mesh_pos()
        for i in range(n):
            half = units[i].shape[1] // 2
            src = g_refs[i].at[pl.ds(0, N_CHIPS), pl.ds((1 - c) * half, half)]
            pltpu.make_async_remote_copy(src_ref=src, dst_ref=z_refs[i], send_sem=send_sems.at[i], recv_sem=recv_sems.at[i],
                                         device_id=(x, y, 1 - c), device_id_type=MESH).start()
        token[...] = jnp.zeros(token.shape, F32)

    zones = [lax.empty((u.shape[0], u.shape[1] // 2, u.shape[2]), u.dtype) for u in units]
    hbm = lambda a: pltpu.HBM(a.shape, a.dtype)
    res = pl.pallas_call(
        body, name=name,
        out_shape=(pltpu.SemaphoreType.DMA((n,)), pltpu.SemaphoreType.DMA((n,)),
                   *[hbm(a) for a in units], *[hbm(a) for a in zones], jax.ShapeDtypeStruct((8, LANES), F32)),
        in_specs=[_HBM] * (2 * n),
        out_specs=(_SEM, _SEM, *[_HBM] * (2 * n), pl.BlockSpec(memory_space=pltpu.VMEM)),
        input_output_aliases={i: 2 + i for i in range(2 * n)},
        compiler_params=pltpu.CompilerParams(has_side_effects=_DATAFLOW),
    )(*[pltpu.with_memory_space_constraint(a, pltpu.HBM) for a in list(units) + zones])
    return res[0], res[1], res[2:2 + n], res[2 + n:2 + 2 * n], res[-1]


def _halves_wait(name, send_sems, recv_sems, units, zones, after):
    n = len(units)

    def body(*refs):
        g_refs, z_refs = refs[:n], refs[n:2 * n]
        s_sems, r_sems = refs[2 * n], refs[2 * n + 1]
        x, y, c = _mesh_pos()
        for i in range(n):
            half = units[i].shape[1] // 2
            src = g_refs[i].at[pl.ds(0, N_CHIPS), pl.ds((1 - c) * half, half)]
            copy = pltpu.make_async_remote_copy(src_ref=src, dst_ref=z_refs[i], send_sem=s_sems.at[i], recv_sem=r_sems.at[i],
                                                device_id=(x, y, 1 - c), device_id_type=MESH)
            copy.wait_send()
            copy.wait_recv()

    hbm = lambda a: pltpu.HBM(a.shape, a.dtype)
    res = pl.pallas_call(
        body, name=name,
        out_shape=(*[hbm(a) for a in units], *[hbm(a) for a in zones]),
        in_specs=[_HBM] * (2 * n) + [_SEM, _SEM, _ANY],
        out_specs=tuple([_HBM] * (2 * n)),
        input_output_aliases={i: i for i in range(2 * n)},
        compiler_params=pltpu.CompilerParams(has_side_effects=_DATAFLOW),
    )(*units, *zones, send_sems, recv_sems, after)
    return res[:n], res[n:]


def _scatter_start(name, sums, zones):
    n = len(sums)

    def body(*refs):
        h_refs, z_refs = refs[:n], refs[n:2 * n]
        send_sems, recv_sems = refs[2 * n], refs[2 * n + 1]
        token = refs[-1]
        x, y, c = _mesh_pos()
        me = 2 * x + y
        for i in range(n):
            for j, (cx, cy) in enumerate(_other_chips(x, y)):
                pltpu.make_async_remote_copy(src_ref=h_refs[i].at[2 * cx + cy], dst_ref=z_refs[i].at[me],
                                             send_sem=send_sems.at[3 * i + j], recv_sem=recv_sems.at[3 * i + j],
                                             device_id=(cx, cy, c), device_id_type=MESH).start()
        token[...] = jnp.zeros(token.shape, F32)

    hbm = lambda a: pltpu.HBM(a.shape, a.dtype)
    res = pl.pallas_call(
        body, name=name,
        out_shape=(pltpu.SemaphoreType.DMA((3 * n,)), pltpu.SemaphoreType.DMA((3 * n,)),
                   *[hbm(a) for a in sums], *[hbm(a) for a in zones], jax.ShapeDtypeStruct((8, LANES), F32)),
        in_specs=[_HBM] * (2 * n),
        out_specs=(_SEM, _SEM, *[_HBM] * (2 * n), pl.BlockSpec(memory_space=pltpu.VMEM)),
        input_output_aliases={i: 2 + i for i in range(2 * n)},
        compiler_params=pltpu.CompilerParams(has_side_effects=_DATAFLOW),
    )(*[pltpu.with_memory_space_constraint(a, pltpu.HBM) for a in list(sums) + list(zones)])
    return res[0], res[1], res[2:2 + n], res[2 + n:2 + 2 * n], res[-1]


def _scatter_wait(name, send_sems, recv_sems, sums, zones, after):
    n = len(sums)

    def body(*refs):
        h_refs, z_refs = refs[:n], refs[n:2 * n]
        s_sems, r_sems = refs[2 * n], refs[2 * n + 1]
        x, y, c = _mesh_pos()
        me = 2 * x + y
        for i in range(n):
            for j, (cx, cy) in enumerate(_other_chips(x, y)):
                idx = 2 * cx + cy
                copy = pltpu.make_async_remote_copy(src_ref=h_refs[i].at[idx], dst_ref=z_refs[i].at[idx],
                                                    send_sem=s_sems.at[3 * i + j], recv_sem=r_sems.at[3 * i + j],
                                                    device_id=(cx, cy, c), device_id_type=MESH)
                copy.wait_send()
                copy.wait_recv()

    hbm = lambda a: pltpu.HBM(a.shape, a.dtype)
    res = pl.pallas_call(
        body, name=name,
        out_shape=(*[hbm(a) for a in sums], *[hbm(a) for a in zones]),
        in_specs=[_HBM] * (2 * n) + [_SEM, _SEM, _ANY],
        out_specs=tuple([_HBM] * (2 * n)),
        input_output_aliases={i: i for i in range(2 * n)},
        compiler_params=pltpu.CompilerParams(has_side_effects=_DATAFLOW),
    )(*sums, *zones, send_sems, recv_sems, after)
    return res[n:]


def _join_halves(name, results):
    n = len(results)
    pieces = [(i, l) for i in range(n) for l in range(results[i].shape[0])]

    def body(*refs):
        out_refs = refs[n:2 * n]
        send_sems, recv_sems = refs[2 * n:]
        x, y, c = _mesh_pos()

        def copy(k, half):
            i, l = pieces[k]
            return pltpu.make_async_remote_copy(src_ref=out_refs[i].at[l, half], dst_ref=out_refs[i].at[l, half],
                                                send_sem=send_sems.at[k], recv_sem=recv_sems.at[k],
                                                device_id=(x, y, 1 - c), device_id_type=MESH)

        cps = [copy(k, c) for k in range(len(pieces))]
        for cp in cps:
            cp.start()
        for k in range(len(pieces)):
            copy(k, 1 - c).wait_recv()
        for cp in cps:
            cp.wait_send()

    return pl.pallas_call(
        body, name=name, in_specs=[_ANY] * n, out_specs=[_ANY] * n,
        out_shape=[jax.ShapeDtypeStruct(r.shape, r.dtype) for r in results],
        input_output_aliases={i: i for i in range(n)},
        scratch_shapes=[pltpu.SemaphoreType.DMA((len(pieces),)), pltpu.SemaphoreType.DMA((len(pieces),))],
        compiler_params=pltpu.CompilerParams(has_side_effects=True),
    )(*results)


def _pack(arrays, dtype, rows_multiple):
    flat = jnp.concatenate([a.reshape(-1).astype(dtype) for a in arrays])
    unit = rows_multiple * PACK_W
    total = -(-flat.shape[0] // unit) * unit
    return jnp.pad(flat, (0, total - flat.shape[0])).reshape(total // PACK_W, PACK_W)


def _unpack(flat, shapes):
    out, off = [], 0
    for s in shapes:
        n = 1
        for d in s:
            n *= d
        out.append(flat[..., off:off + n].reshape(flat.shape[:-1] + tuple(s)))
        off += n
    return out


def _ffn_fwd(tag, l, h, g, w_up, conv, bias, w_down, tm):
    hn = _rms_fwd(f"{tag}_norm", h, g, tm)
    u = _mm_cs(f"{tag}_up", hn, w_up, l, tm, out_dtype=FFN_HIDDEN_DTYPE)
    act = _ffn_col_fwd(f"{tag}_glu", u, conv, bias)
    w_down = w_down(act) if callable(w_down) else w_down
    h_out = _mm_full(f"{tag}_down", act, w_down, l, tm, D_FF // 2, add=h)
    return h_out, (hn, u, act), w_down


def _ffn_bwd(tag, l, h, g, w_up, conv, bias, w_down, saved, dh, tm):
    hn, u, act = saved
    da = _mm_nt_full(f"{tag}_down_dx", dh, w_down, l, tm, D_FF // 2)
    dw_down = _mm_tn_full(f"{tag}_down_dw", act, dh, tm, D_FF // 2)
    du, dconv, dbias = _ffn_col_bwd(f"{tag}_glu_bwd", u, da, conv, bias)
    dw_up = _mm_tn_cs(f"{tag}_up_dw", hn, du, N_CHIPS, tm)
    dhn = _mm_nt_cs(f"{tag}_up_dx", du, w_up, l, tm)
    dh, dg = _rms_bwd(f"{tag}_norm_bwd", h, g, dhn, dh, tm)
    return dh, dict(norm=dg, w_up=dw_up, conv=dconv, bias=dbias, w_down=dw_down)


def _to_heads(z, nh, pad):
    t = z.shape[0]
    return jnp.pad(z.reshape(t, nh, HEAD_DIM).transpose(1, 0, 2), ((0, 0), (pad, 0), (0, 0)))


def _from_heads(z, pad):
    nh, tp, _ = z.shape
    return z[:, pad:].transpose(1, 0, 2).reshape(tp - pad, nh * HEAD_DIM)


def _rope_tables(tp, pad):
    half = HEAD_DIM // 2
    inv = ROPE_THETA ** (-jnp.arange(half, dtype=F32) / half)
    ang = (jnp.arange(tp, dtype=F32) - pad)[:, None] * inv[None, :]
    cos, sin = jnp.cos(ang), jnp.sin(ang)
    rot = jnp.zeros((HEAD_DIM, HEAD_DIM), F32)
    idx = jnp.arange(half)
    rot = rot.at[idx + half, idx].set(-1.0).at[idx, idx + half].set(1.0)
    return jnp.concatenate([cos, cos], axis=1), jnp.concatenate([sin, sin], axis=1), rot


def _local_step(x, tgt, w, on_grads=None, fetch=None):
    emit = on_grads if on_grads is not None else (lambda tag, units, after=None: 0.0)
    need = (lambda tag, after: w) if fetch is None else (lambda tag, after: {**w, **fetch(tag, after)})
    seq = x.shape[0]
    t = seq + N_META
    tm = _row_tile(t, ROW_TILE_CAP)
    tr = _row_tile(t, ROW_TILE_CAP // 2)
    pad = BLOCK - N_META
    grads = {}

    h0 = jnp.concatenate([w["meta_tokens"], x], axis=0)
    tgt_p = jnp.pad(tgt, ((N_META, 0), (0, 0)))

    hn0 = _rms_fwd("l0_norm", h0, w["norm_mix"][0:1], tm)
    p0 = _mm_cs("l0_in", hn0, w["ev_w_in"], 0, tm)
    uc, yb = _even_col_fwd("l0_convs", p0, w["ev_conv_a"], w["ev_conv_b"])
    ya = _even_ln_fwd("l0_ln", uc, w["ev_ln_a_g"], w["ev_ln_a_b"], tm)
    y0 = jnp.concatenate([ya, yb], axis=1)
    w = need("ev_out", y0)
    h1 = _mm_full("l0_out", y0, w["ev_w_out"], 0, tm, D_MODEL, add=h0)
    w = need("f0", h1)
    down0 = w["ff_w_down0"] if "ff_w_down0" in w else (lambda act: need("f0_down", act)["ff_w_down0"])
    f0 = (0, h1, w["norm_ffn"][0:1], w["ff_w_up0"], w["ff_conv"][0], w["ff_conv_b"][0:1])
    h2, ffn0, down0 = _ffn_fwd("f0", *f0, down0, tm)
    f0 = f0 + (down0,)
    w = need("od", h2)

    hn2 = _rms_fwd("l1_norm", h2, w["norm_mix"][1:2], tm)
    p1 = _mm_cs("l1_in", hn2, w["od_w_in"], 0, tm)
    cos, sin, rot = _rope_tables(t + pad, pad)
    qh = _to_heads(p1[:, :D_ATT], N_Q_HEADS, pad)
    kh = _to_heads(p1[:, D_ATT:D_ATT + D_KV], N_KV_HEADS, pad)
    vh = _to_heads(p1[:, D_ATT + D_KV:D_ATT + 2 * D_KV], N_KV_HEADS, pad)
    sinks_b = jnp.broadcast_to(w["od_sinks"].reshape(N_Q_HEADS, 1, 1), (N_Q_HEADS, 8, LANES))
    y_att = _from_heads(_attn_fwd("l1_attn", qh, kh, vh, sinks_b, cos, sin, rot), pad)

    col0 = D_ATT + 2 * D_KV
    ch = jnp.arange(D_R) // HEAD_DIM
    seg = (ch[:, None] == ch[None, :]).astype(F32)
    prm = dict(w0=w["od_w0"], a0=w["od_a0"], g2=w["od_g2"], k_k=w["od_k_k"], k_a=w["od_k_a"],
               lnx_g=w["od_lnx_g"], lnx_b=w["od_lnx_b"], r_k=w["od_r_k"].reshape(1, D_R),
               w2p=jnp.concatenate([w["od_w2"], jnp.zeros((LORA_A, D_R), F32)], axis=0),
               a2p=jnp.concatenate([jnp.zeros((LORA_W, D_R), F32), w["od_a2"]], axis=0))
    prs = _shift_fwd("l1_shift", p1, col0, w["od_mu"])
    lw, k2, a_, b_, gate_r = _rwkv_pre_fwd("l1_rwkv_pre", prs, prm, seg, tr)
    v_off = 2 * D_R // (WKV_PAIRS_PER_STEP * PAIR)
    scan_in = [(prs, 0), (lw, 0), (k2, 0), (prs, v_off), (a_, 0), (b_, 0)]
    y_scan, states = _wkv_fwd("l1_wkv", scan_in)
    y_rwkv = _rwkv_post_fwd("l1_rwkv_post", y_scan, prs, k2, gate_r, prm, seg, tr)
    y1 = jnp.concatenate([y_att, y_rwkv], axis=1).astype(MXU_DTYPE)
    h3 = _mm_full("l1_out", y1, w["od_w_out"], 0, tm, D_MODEL, add=h2)
    w = need("f1", h3)
    f1 = (0, h3, w["norm_ffn"][1:2], w["ff_w_up1"], w["ff_conv"][1], w["ff_conv_b"][1:2], w["ff_w_down1"])
    h4, ffn1, _ = _ffn_fwd("f1", *f1, tm)

    loss_blk, dh, d_norm_final = _final_loss("final", h4, w["norm_final"], tgt_p, tm)
    grads["norm_final"] = d_norm_final

    dh, gf1 = _ffn_bwd("f1", *f1, ffn1, dh, tm)
    zero = emit("f1", {"ff_w_down1": gf1["w_down"].reshape(N_CHIPS, D_FF // N_CHIPS, D_MODEL), "ff_w_up1": gf1["w_up"]})
    prm = dict(prm, lnx_g=prm["lnx_g"] + zero)
    dy1 = _mm_nt_full("l1_out_dx", dh, w["od_w_out"], 0, tm, D_MODEL)
    grads["od_w_out"] = _mm_tn_full("l1_out_dw", y1, dh, tm, D_MODEL // 2)
    dy_scan, dr_p, dk2_p, dv_p, dgate_r, grads["od_lnx_g"], grads["od_lnx_b"], d_rk = _rwkv_post_bwd(
        "l1_rwkv_post_bwd", y_scan, prs, k2, gate_r, prm, seg, dy1, 1, tr)
    grads["od_r_k"] = d_rk.reshape(N_R_HEADS, HEAD_DIM)
    dr_s, dlw, dk2_s, dv_s, da_, db_ = _wkv_bwd("l1_wkv_bwd", scan_in, states, (dy_scan, 0))
    dk, dxl, dgd, grads["od_w0"], dw2p, grads["od_a0"], da2p, grads["od_g2"], grads["od_k_k"], grads["od_k_a"] = (
        _rwkv_pre_bwd("l1_rwkv_pre_bwd", prs, prm, seg, (dlw, dk2_s + dk2_p, da_, db_, dgate_r), tr))
    grads["od_w2"] = dw2p[:LORA_W]
    grads["od_a2"] = da2p[LORA_W:]
    dprs = jnp.concatenate([dr_s + dr_p, dk, dv_s + dv_p, dxl, dgd], axis=1)
    dpr, grads["od_mu"] = _shift_bwd("l1_shift_bwd", p1, col0, w["od_mu"], dprs)
    doh = _to_heads(dy1[:, :D_ATT], N_Q_HEADS, pad)
    dqh, dkp, dkc, dvp, dvc, dkm, dvm, dsinks = _attn_bwd("l1_attn_bwd", qh, kh, vh, sinks_b, cos, sin, rot, doh)
    grads["od_sinks"] = dsinks[:, 0, 0].reshape(1, N_Q_HEADS)
    dkh, dvh = _kv_combine("l1_attn_dkv", (dkp, dkc, dkm), (dvp, dvc, dvm))
    dp1 = jnp.concatenate([_from_heads(dqh, pad), _from_heads(dkh, pad), _from_heads(dvh, pad), dpr], axis=1).astype(MXU_DTYPE)
    grads["od_w_in"] = _mm_tn_cs("l1_in_dw", hn2, dp1, N_CHIPS, tm)
    dhn2 = _mm_nt_cs("l1_in_dx", dp1, w["od_w_in"], 0, tm)
    dh, d_mix1 = _rms_bwd("l1_norm_bwd", h2, w["norm_mix"][1:2], dhn2, dh, tm)

    zero = emit("od", {"od_w_out": grads["od_w_out"].reshape(N_CHIPS, D_MODEL // N_CHIPS, D_MODEL), "od_w_in": grads["od_w_in"]})
    f0 = f0[:5] + (f0[5] + zero,) + f0[6:]
    dh, gf0 = _ffn_bwd("f0", *f0, ffn0, dh, tm)
    zero = emit("f0", {"ff_w_down0": gf0["w_down"].reshape(N_CHIPS, D_FF // N_CHIPS, D_MODEL), "ff_w_up0": gf0["w_up"]})
    w = dict(w, ev_ln_a_g=w["ev_ln_a_g"] + zero)
    dy0 = _mm_nt_full("l0_out_dx", dh, w["ev_w_out"], 0, tm, D_MODEL)
    grads["ev_w_out"] = _mm_tn_full("l0_out_dw", y0, dh, tm, D_MODEL // 2)
    w = dict(w, ev_ln_a_g=w["ev_ln_a_g"] + emit("l0_out", {}, grads["ev_w_out"]))
    duc, grads["ev_ln_a_g"], grads["ev_ln_a_b"] = _even_ln_bwd("l0_ln_bwd", uc, w["ev_ln_a_g"], w["ev_ln_a_b"], dy0, 0, tm)
    *dparts, grads["ev_conv_a"], grads["ev_conv_b"] = _even_col_bwd("l0_convs_bwd", p0, duc, dy0, w["ev_conv_a"], w["ev_conv_b"])
    dp0 = jnp.concatenate(dparts, axis=1)
    grads["ev_w_in"] = _mm_tn_cs("l0_in_dw", hn0, dp0, N_CHIPS, tm)
    dhn0 = _mm_nt_cs("l0_in_dx", dp0, w["ev_w_in"], 0, tm)
    dh, d_mix0 = _rms_bwd("l0_norm_bwd", h0, w["norm_mix"][0:1], dhn0, dh, tm)

    grads["norm_mix"] = jnp.concatenate([d_mix0, d_mix1], axis=0)
    grads["norm_ffn"] = jnp.concatenate([gf0["norm"], gf1["norm"]], axis=0)
    grads["ff_w_up"] = [gf0["w_up"], gf1["w_up"]]
    grads["ff_conv"] = jnp.stack([gf0["conv"], gf1["conv"]])
    grads["ff_conv_b"] = jnp.concatenate([gf0["bias"], gf1["bias"]], axis=0)
    grads["ff_w_down"] = [gf0["w_down"], gf1["w_down"]]
    grads["meta_tokens"] = dh[:N_META]
    return loss_blk[0, 0], dh[N_META:], grads


SHARD_AXIS = {
    "meta_tokens": 1, "norm_mix": None, "norm_ffn": None, "norm_final": None,
    "ev_w_in": 2, "ev_conv_a": 2, "ev_ln_a_g": None, "ev_ln_a_b": None, "ev_conv_b": 2, "ev_w_out": 1,
    "od_w_in": 2, "od_sinks": None, "od_mu": 1, "od_w0": 1, "od_w2": 2, "od_a0": 1, "od_a2": 2, "od_g2": 2,
    "od_k_k": 1, "od_k_a": 1, "od_r_k": None, "od_lnx_g": 1, "od_lnx_b": 1, "od_w_out": 1,
    "ff_w_up": 2, "ff_conv": 2, "ff_conv_b": None, "ff_w_down": 1,
}
WEIGHTS = list(SHARD_AXIS)
BIG = ("ev_w_in", "ev_w_out", "od_w_in", "od_w_out", "ff_w_up", "ff_w_down")
SHARDED = [n for n in WEIGHTS if SHARD_AXIS[n] is not None]
SMALL = [n for n in SHARDED if n not in BIG]
REPLICATED = [n for n in WEIGHTS if SHARD_AXIS[n] is None]


def _join(g, axis):
    return jnp.concatenate([g[k] for k in range(N_CHIPS)], axis=axis)


def _split(full, axis):
    return jnp.stack(jnp.split(full, N_CHIPS, axis=axis))


def _full_weights(gathered, repl):
    w = {}
    sq = lambda a: a.reshape(a.shape[1:]) if a.shape[0] == 1 else a
    for n in REPLICATED:
        w[n] = repl[n]
    w["norm_final"] = repl["norm_final"].reshape(1, D_MODEL)
    for n in ("ev_ln_a_g", "ev_ln_a_b"):
        w[n] = repl[n].reshape(1, D_A)
    w["od_r_k"] = repl["od_r_k"][0]
    w["meta_tokens"] = _join(gathered["meta_tokens"], 1)
    for n in ("ev_conv_a", "ev_conv_b", "od_w2", "od_a2", "od_g2"):
        w[n] = sq(_join(gathered[n], 2))
    for n in ("od_mu", "od_w0", "od_a0", "od_k_k", "od_k_a", "od_lnx_g", "od_lnx_b"):
        w[n] = _join(gathered[n], 1)
    w["ff_conv"] = _join(gathered["ff_conv"], 2)
    return w


def _shard_grads(grads):
    out = {}
    for n in REPLICATED:
        out[n] = grads[n]
    out["norm_final"] = grads["norm_final"].reshape(D_MODEL)
    out["od_r_k"] = grads["od_r_k"][None]
    out["meta_tokens"] = _split(grads["meta_tokens"], 1)
    for n in ("ev_conv_a", "ev_conv_b", "od_w2", "od_a2", "od_g2"):
        out[n] = _split(grads[n][None], 2)
    for n in ("od_mu", "od_w0", "od_a0", "od_k_k", "od_k_a", "od_lnx_g", "od_lnx_b"):
        out[n] = _split(grads[n], 1)
    out["ff_conv"] = _split(grads["ff_conv"], 2)
    return out


def kernel(x, meta_tokens, norm_mix, norm_ffn, norm_final, ev_w_in, ev_conv_a, ev_ln_a_g, ev_ln_a_b, ev_conv_b, ev_w_out, od_w_in, od_sinks, od_mu, od_w0, od_w2, od_a0, od_a2, od_g2, od_k_k, od_k_a, od_r_k, od_lnx_g, od_lnx_b, od_w_out, ff_w_up, ff_conv, ff_conv_b, ff_w_down, loss_target, m_meta_tokens, m_norm_mix, m_norm_ffn, m_norm_final, m_ev_w_in, m_ev_conv_a, m_ev_ln_a_g, m_ev_ln_a_b, m_ev_conv_b, m_ev_w_out, m_od_w_in, m_od_sinks, m_od_mu, m_od_w0, m_od_w2, m_od_a0, m_od_a2, m_od_g2, m_od_k_k, m_od_k_a, m_od_r_k, m_od_lnx_g, m_od_lnx_b, m_od_w_out, m_ff_w_up, m_ff_conv, m_ff_conv_b, m_ff_w_down, v_meta_tokens, v_norm_mix, v_norm_ffn, v_norm_final, v_ev_w_in, v_ev_conv_a, v_ev_ln_a_g, v_ev_ln_a_b, v_ev_conv_b, v_ev_w_out, v_od_w_in, v_od_sinks, v_od_mu, v_od_w0, v_od_w2, v_od_a0, v_od_a2, v_od_g2, v_od_k_k, v_od_k_a, v_od_r_k, v_od_lnx_g, v_od_lnx_b, v_od_w_out, v_ff_w_up, v_ff_conv, v_ff_conv_b, v_ff_w_down):
    wts = dict(meta_tokens=meta_tokens, norm_mix=norm_mix, norm_ffn=norm_ffn, norm_final=norm_final, ev_w_in=ev_w_in, ev_conv_a=ev_conv_a, ev_ln_a_g=ev_ln_a_g, ev_ln_a_b=ev_ln_a_b, ev_conv_b=ev_conv_b, ev_w_out=ev_w_out, od_w_in=od_w_in, od_sinks=od_sinks, od_mu=od_mu, od_w0=od_w0, od_w2=od_w2, od_a0=od_a0, od_a2=od_a2, od_g2=od_g2, od_k_k=od_k_k, od_k_a=od_k_a, od_r_k=od_r_k, od_lnx_g=od_lnx_g, od_lnx_b=od_lnx_b, od_w_out=od_w_out, ff_w_up=ff_w_up, ff_conv=ff_conv, ff_conv_b=ff_conv_b, ff_w_down=ff_w_down)
    mom = dict(meta_tokens=m_meta_tokens, norm_mix=m_norm_mix, norm_ffn=m_norm_ffn, norm_final=m_norm_final, ev_w_in=m_ev_w_in, ev_conv_a=m_ev_conv_a, ev_ln_a_g=m_ev_ln_a_g, ev_ln_a_b=m_ev_ln_a_b, ev_conv_b=m_ev_conv_b, ev_w_out=m_ev_w_out, od_w_in=m_od_w_in, od_sinks=m_od_sinks, od_mu=m_od_mu, od_w0=m_od_w0, od_w2=m_od_w2, od_a0=m_od_a0, od_a2=m_od_a2, od_g2=m_od_g2, od_k_k=m_od_k_k, od_k_a=m_od_k_a, od_r_k=m_od_r_k, od_lnx_g=m_od_lnx_g, od_lnx_b=m_od_lnx_b, od_w_out=m_od_w_out, ff_w_up=m_ff_w_up, ff_conv=m_ff_conv, ff_conv_b=m_ff_conv_b, ff_w_down=m_ff_w_down)
    var = dict(meta_tokens=v_meta_tokens, norm_mix=v_norm_mix, norm_ffn=v_norm_ffn, norm_final=v_norm_final, ev_w_in=v_ev_w_in, ev_conv_a=v_ev_conv_a, ev_ln_a_g=v_ev_ln_a_g, ev_ln_a_b=v_ev_ln_a_b, ev_conv_b=v_ev_conv_b, ev_w_out=v_ev_w_out, od_w_in=v_od_w_in, od_sinks=v_od_sinks, od_mu=v_od_mu, od_w0=v_od_w0, od_w2=v_od_w2, od_a0=v_od_a0, od_a2=v_od_a2, od_g2=v_od_g2, od_k_k=v_od_k_k, od_k_a=v_od_k_a, od_r_k=v_od_r_k, od_lnx_g=v_od_lnx_g, od_lnx_b=v_od_lnx_b, od_w_out=v_od_w_out, ff_w_up=v_ff_w_up, ff_conv=v_ff_conv, ff_conv_b=v_ff_conv_b, ff_w_down=v_ff_w_down)

    me_idx = (2 * lax.axis_index("x") + lax.axis_index("y")).astype(jnp.int32).reshape(1)
    c_idx = lax.axis_index("c").astype(jnp.int32).reshape(1)
    small_mine = _pack([wts[n] for n in SMALL], F32, 2 * 8)
    sources = {"ev_w_in": (ev_w_in, 0), "small": (small_mine[None], 0), "ev_w_out": (ev_w_out, 0),
               "ff_w_up0": (ff_w_up, 0), "ff_w_down0": (ff_w_down, 0), "od_w_in": (od_w_in, 0), "od_w_out": (od_w_out, 0),
               "ff_w_up1": (ff_w_up, 1), "ff_w_down1": (ff_w_down, 1)}
    bufs = {n: _place_own_block("place_" + n, a, l, me_idx, F32 if n == "small" else MXU_DTYPE)
            for n, (a, l) in sources.items()}

    def as_used(n, g):
        if n in ("ev_w_out", "od_w_out", "ff_w_down0", "ff_w_down1"):
            return g.reshape(1, -1, g.shape[-1])
        return g.reshape(N_CHIPS, 1, -1, g.shape[-1])

    first = dict(zip(("ev_w_in", "small"), _gather_weights("gather_first", [bufs["ev_w_in"], bufs["small"]])))
    gathered = dict(zip(SMALL, _unpack(first["small"].reshape(N_CHIPS, -1), [wts[n].shape for n in SMALL])))
    w_full = _full_weights(gathered, wts)
    w_full["ev_w_in"] = as_used("ev_w_in", first["ev_w_in"])
    groups = {"ev_out": ["ev_w_out"], "f0": ["ff_w_up0"], "f0_down": ["ff_w_down0"], "od": ["od_w_in", "od_w_out"],
              "f1": ["ff_w_up1", "ff_w_down1"]}
    started_gathers, token = _gather_start("gather_start", [[bufs[n] for n in g] for g in groups.values()])
    started_gathers = dict(zip(groups, started_gathers))
    w_full["norm_mix"] = w_full["norm_mix"] + token[0, 0]

    def fetch(tag, after):
        send_sems, recv_sems, group_bufs = started_gathers[tag]
        landed = _gather_wait("gather_wait_" + tag, send_sems, recv_sems, group_bufs, after)
        whole = _gather_weights("gather_siblings_" + tag, landed, from_chips=False)
        return {n: as_used(n, g) for n, g in zip(groups[tag], whole)}

    cm_idx = jnp.concatenate([c_idx, me_idx])
    started = []
    to_sibling = []

    def to_chips(tag, names, units, from_sibling):
        pairs = [_pair_add_placed(f"grads_pair_add_{n}", u, r, cm_idx, GRAD_WIRE_DTYPE)
                 for n, u, r in zip(names, units, from_sibling)]
        send_sems, recv_sems, sums, zones, token = _scatter_start(
            f"grads_to_chips_start_{tag}", [p[0] for p in pairs], [p[1] for p in pairs])
        started.append((tag, names, send_sems, recv_sems, sums, zones))
        return token[0, 0]

    def start_reduction(tag, units, after=None):
        names = list(units)
        arrays = [units[n] for n in names]
        zero = 0.0
        if to_sibling:
            before, bnames, send_sems, recv_sems, thru, zones = to_sibling.pop()
            thru, got = _halves_wait(f"grads_to_sibling_wait_{before}", send_sems, recv_sems, thru, zones,
                                     arrays[-1] if after is None else after)
            zero = zero + to_chips(before, bnames, thru, got)
        if not names:
            return zero
        send_sems, recv_sems, thru, zones, token = _halves_start(f"grads_to_sibling_start_{tag}", arrays)
        to_sibling.append((tag, names, send_sems, recv_sems, thru, zones))
        return zero + token[0, 0]

    loss_local, grad_x, grads = _local_step(x[0], loss_target[0], w_full, start_reduction, fetch)
    loss = lax.psum(loss_local, ("x", "y", "c"))

    sg = _shard_grads(grads)
    small_rows = [jnp.concatenate([sg[n][k].reshape(-1) for n in SMALL] + [sg[n].reshape(-1) for n in REPLICATED])
                  for k in range(N_CHIPS)]
    n_el = small_rows[0].shape[0]
    n_rows = -(-n_el // (16 * PACK_W)) * 16
    small_unit = jnp.stack([jnp.pad(r, (0, n_rows * PACK_W - n_el)).reshape(n_rows, PACK_W) for r in small_rows])
    last = {"ev_w_out": grads["ev_w_out"].reshape(N_CHIPS, D_MODEL // N_CHIPS, D_MODEL), "ev_w_in": grads["ev_w_in"],
            "small": small_unit}
    from_sibling = _halves_to_sibling("grads_to_sibling_ev", list(last.values()))
    pairs = [_pair_add_placed(f"grads_pair_add_{n}", u, r, cm_idx, F32 if n == "small" else GRAD_WIRE_DTYPE)
             for (n, u), r in zip(last.items(), from_sibling)]
    ev_send, ev_recv, ev_sums, ev_zones, token = _scatter_start(
        "grads_to_chips_start_ev", [p[0] for p in pairs], [p[1] for p in pairs])
    dests = {"ev_w_in": ("ev_w_in", 0), "od_w_in": ("od_w_in", 0), "ev_w_out": ("ev_w_out", 0), "od_w_out": ("od_w_out", 0),
             "ff_w_up0": ("ff_w_up", 0), "ff_w_up1": ("ff_w_up", 1), "ff_w_down0": ("ff_w_down", 0),
             "ff_w_down1": ("ff_w_down", 1), "small": ("small", 0)}
    outs = {"grad": {}, "delta": {}, "new_m": {}, "new_v": {}}

    def finish(tag, from_chips, results):
        reduced = {}
        for n, part in from_chips.items():
            r, l = dests[n]
            reduced[r] = _sum_chips(f"grads_chip_sum_{n}", part, c_idx, l, 2 if r.startswith("ff_w") else 1,
                                    into=reduced.get(r))
        joined = dict(zip(results, _join_halves("grads_join_" + tag, [reduced[r] for r in results])))
        for n, g in joined.items():
            if n == "small":
                continue
            shape = wts[n].shape
            flat = lambda a: a.reshape(-1, shape[-1])
            new = _adamw("adamw_" + n, flat(wts[n]), flat(g), flat(mom[n]), flat(var[n]))
            for kind, arr in zip(("grad", "delta", "new_m", "new_v"), (g,) + tuple(new)):
                outs[kind][n] = arr.reshape(shape)
        return joined

    from_chips = {}
    for tag, names, send_sems, recv_sems, sums, zones in started:
        from_chips.update(zip(names, _scatter_wait(f"grads_to_chips_wait_{tag}", send_sems, recv_sems, sums, zones, token)))
    finish("layers", from_chips, ["od_w_in", "od_w_out", "ff_w_up", "ff_w_down"])
    from_chips = dict(zip(last, _scatter_wait("grads_to_chips_wait_ev", ev_send, ev_recv, ev_sums, ev_zones,
                                              outs["delta"]["ff_w_up"])))
    joined = finish("ev", from_chips, ["ev_w_in", "ev_w_out", "small"])

    order = SMALL + REPLICATED
    packed = lambda d: jnp.pad(jnp.concatenate([d[n].reshape(-1) for n in order]),
                               (0, n_rows * PACK_W - n_el)).reshape(n_rows, PACK_W)
    g_small = joined["small"].reshape(n_rows, PACK_W)
    new = _adamw("adamw_small", packed(wts), g_small, packed(mom), packed(var))
    for tag, arr in zip(("grad", "delta", "new_m", "new_v"), (g_small,) + tuple(new)):
        outs[tag].update(zip(order, _unpack(arr.reshape(-1), [wts[n].shape for n in order])))
    return (loss, grad_x[None], *[outs["grad"][n] for n in WEIGHTS], *[outs["delta"][n] for n in WEIGHTS],
            *[outs["new_m"][n] for n in WEIGHTS], *[outs["new_v"][n] for n in WEIGHTS])
```

```python
import functools

import jax
import jax.numpy as jnp
from jax import lax
from jax.experimental import pallas as pl
from jax.experimental.pallas import tpu as pltpu

F32 = jnp.float32
BF16 = jnp.bfloat16
MXU_DTYPE = BF16
GRAD_WIRE_DTYPE = BF16
FFN_HIDDEN_DTYPE = BF16

D_MODEL = 1024
N_META = 16
RMS_EPS = 1e-6
LN_EPS = 1e-5
D_A = 512
CONV_A_WIDTH = 31
CONV_B_WIDTH = 3
HEAD_DIM = 64
N_Q_HEADS = 8
N_KV_HEADS = 2
GQA_GROUP = 4
D_ATT = 512
D_KV = 128
BLOCK = 128
ROPE_THETA = 10000.0
D_R = 512
N_R_HEADS = 8
LORA_W = 64
LORA_A = 64
LORA_G = 128
RWKV_GN_EPS = 64e-5
RWKV_COLS = 3 * D_R + LORA_W + LORA_A + LORA_G
D_FF = 2816
NEG_INF = -1e30
ADAM_LR = 0.001
ADAM_B1 = 0.9
ADAM_B2 = 0.999
ADAM_EPS = 1e-08
ADAM_WD = 0.01
ADAM_STEP = 10

N_CHIPS = 4
LANES = 128
CONV_PAD = 32
ROW_TILE_CAP = 704
VMEM_LIMIT_V7X = 56 * 1024 * 1024
MESH = pl.DeviceIdType.MESH


def _cparams(sem=None):
    return pltpu.CompilerParams(dimension_semantics=sem, vmem_limit_bytes=VMEM_LIMIT_V7X)


def _row_tile(t, cap):
    for d in range(min(t, cap), 0, -1):
        if t % d == 0 and d % 16 == 0:
            return d
    return t


def _chunk_len(t):
    for d in (64, 48, 32, 16, 8):
        if t % d == 0:
            return d
    raise ValueError(t)


def _call(fn, name, grid, ins, outs, acc_axis=None, sem=None):
    n_in, n_out = len(ins), len(outs)
    dtype = lambda o: o[4] if len(o) > 4 else F32

    def body(*refs):
        vals = fn(*[r[...] for r in refs[:n_in]])
        if not isinstance(vals, (tuple, list)):
            vals = (vals,)
        for r, v, o in zip(refs[n_in:n_in + n_out], vals, outs):
            if o[3]:
                first = pl.program_id(acc_axis) == 0

                @pl.when(first)
                def _(r=r, v=v):
                    r[...] = v

                @pl.when(jnp.logical_not(first))
                def _(r=r, v=v):
                    r[...] += v
            else:
                r[...] = v.astype(dtype(o))

    res = pl.pallas_call(
        body, name=name, grid=grid,
        in_specs=[pl.BlockSpec(b, m) for _, b, m in ins],
        out_specs=[pl.BlockSpec(o[1], o[2]) for o in outs],
        out_shape=[jax.ShapeDtypeStruct(o[0], dtype(o)) for o in outs],
        compiler_params=_cparams(sem),
    )(*[a for a, _, _ in ins])
    return res if n_out > 1 else res[0]


def _matmul(name, a, b, *, dims, grid, a_spec, b_spec, o_shape, o_spec, acc_shape, nk, k_axis,
            add=None, add_spec=None, out_dtype=F32, norm_gain=None):
    n_in = 2 + (add is not None) + (norm_gain is not None)

    def product(a_ref, b_ref):
        return lax.dot_general(a_ref[...].astype(MXU_DTYPE), b_ref[...].astype(MXU_DTYPE), dims, preferred_element_type=F32)

    def emit(refs, res):
        refs[n_in][...] = res.astype(out_dtype)
        if norm_gain is not None:
            r = lax.rsqrt(jnp.mean(res * res, axis=-1, keepdims=True) + RMS_EPS)
            refs[n_in + 1][...] = (res * r * refs[n_in - 1][...]).astype(MXU_DTYPE)

    def body_single(*refs):
        a_ref, b_ref = refs[0], refs[1]
        emit(refs, product(a_ref, b_ref) if add is None else product(a_ref, b_ref) + refs[2][...])

    def body_steps(*refs):
        a_ref, b_ref, acc = refs[0], refs[1], refs[-1]
        k = pl.program_id(k_axis)

        @pl.when(k == 0)
        def _():
            if add is None:
                acc[...] = jnp.zeros(acc.shape, F32)
            else:
                acc[...] = refs[2][...]

        acc[...] += product(a_ref, b_ref)

        @pl.when(k == nk - 1)
        def _():
            emit(refs, acc[...])

    args = [a, b] + ([] if add is None else [add]) + ([] if norm_gain is None else [norm_gain])
    specs = [a_spec, b_spec] + ([] if add is None else [add_spec])
    out_specs, out_shape = o_spec, jax.ShapeDtypeStruct(o_shape, out_dtype)
    if norm_gain is not None:
        specs.append(pl.BlockSpec(norm_gain.shape, lambda *_: (0, 0)))
        out_specs, out_shape = [o_spec, o_spec], [out_shape, jax.ShapeDtypeStruct(o_shape, MXU_DTYPE)]
    return pl.pallas_call(
        body_single if nk == 1 else body_steps, name=name, grid=grid, in_specs=specs, out_specs=out_specs,
        out_shape=out_shape,
        scratch_shapes=[] if nk == 1 else [pltpu.VMEM(acc_shape, F32)],
        compiler_params=_cparams(None),
    )(*args)


MATMUL_BLOCKS_BYTES = 46 * 1024 * 1024


def _whole_if_fits(t, tile, need_bytes):
    return t if need_bytes <= MATMUL_BLOCKS_BYTES else tile


_NN = (((1,), (0,)), ((), ()))
_NT = (((1,), (1,)), ((), ()))
_TN = (((0,), (0,)), ((), ()))


def _mm_cs(name, x, wg, l, tm, out_dtype=F32):
    t, k = x.shape
    s, _, _, n = wg.shape
    tm = _whole_if_fits(t, tm, 2 * (t * k * x.dtype.itemsize + k * n * wg.dtype.itemsize + t * n * 4))
    return _matmul(name, x, wg, dims=_NN, grid=(s, t // tm, 1),
                   a_spec=pl.BlockSpec((tm, k), lambda j, i, kk: (i, 0)),
                   b_spec=pl.BlockSpec((None, None, k, n), lambda j, i, kk: (j, l, 0, 0)),
                   o_shape=(t, s * n), o_spec=pl.BlockSpec((tm, n), lambda j, i, kk: (i, j)),
                   acc_shape=(tm, n), nk=1, k_axis=2, out_dtype=out_dtype)


def _mm_full(name, x, w, l, tm, tk, add=None, norm_gain=None):
    t, k = x.shape
    n = w.shape[2]
    nk = k // tk
    tm = _whole_if_fits(t, tm, 2 * (t * tk * x.dtype.itemsize + tk * n * w.dtype.itemsize + t * n * 4 * (1 if add is None else 2)
                                    + (0 if norm_gain is None else t * n * 2)) + (t * n * 4 if nk > 1 else 0))
    return _matmul(name, x, w, dims=_NN, grid=(t // tm, 1, nk),
                   a_spec=pl.BlockSpec((tm, tk), lambda i, j, kk: (i, kk)),
                   b_spec=pl.BlockSpec((None, tk, n), lambda i, j, kk: (l, kk, 0)),
                   o_shape=(t, n), o_spec=pl.BlockSpec((tm, n), lambda i, j, kk: (i, 0)),
                   acc_shape=(tm, n), nk=nk, k_axis=2,
                   add=add, add_spec=pl.BlockSpec((tm, n), lambda i, j, kk: (i, 0)), norm_gain=norm_gain)


def _mm_nt_cs(name, dy, wg, l, tm, add=None):
    t = dy.shape[0]
    s, _, k, n = wg.shape
    tm = _whole_if_fits(t, tm, 2 * (t * n * dy.dtype.itemsize + k * n * wg.dtype.itemsize + t * k * 4 * (1 if add is None else 2))
                        + t * k * 4)
    return _matmul(name, dy, wg, dims=_NT, grid=(t // tm, 1, s),
                   a_spec=pl.BlockSpec((tm, n), lambda i, j, kk: (i, kk)),
                   b_spec=pl.BlockSpec((None, None, k, n), lambda i, j, kk: (kk, l, 0, 0)),
                   o_shape=(t, k), o_spec=pl.BlockSpec((tm, k), lambda i, j, kk: (i, 0)),
                   acc_shape=(tm, k), nk=s, k_axis=2,
                   add=add, add_spec=pl.BlockSpec((tm, k), lambda i, j, kk: (i, 0)))


def _mm_nt_full(name, dy, w, l, tm, tko):
    t, n = dy.shape
    k = w.shape[1]
    tm = _whole_if_fits(t, tm, 2 * (t * n * dy.dtype.itemsize + tko * n * w.dtype.itemsize + t * tko * 4))
    return _matmul(name, dy, w, dims=_NT, grid=(t // tm, k // tko, 1),
                   a_spec=pl.BlockSpec((tm, n), lambda i, j, kk: (i, 0)),
                   b_spec=pl.BlockSpec((None, tko, n), lambda i, j, kk: (l, j, 0)),
                   o_shape=(t, k), o_spec=pl.BlockSpec((tm, tko), lambda i, j, kk: (i, j)),
                   acc_shape=(tm, tko), nk=1, k_axis=2)


def _mm_tn_cs(name, x, dy, s, tk):
    t, k = x.shape
    n = dy.shape[1] // s
    tk = _whole_if_fits(t, tk, 2 * (t * k * x.dtype.itemsize + t * n * dy.dtype.itemsize + k * n * 4))
    nk = t // tk
    return _matmul(name, x, dy, dims=_TN, grid=(s, 1, nk),
                   a_spec=pl.BlockSpec((tk, k), lambda j, i, kk: (kk, 0)),
                   b_spec=pl.BlockSpec((tk, n), lambda j, i, kk: (kk, j)),
                   o_shape=(s, k, n), o_spec=pl.BlockSpec((None, k, n), lambda j, i, kk: (j, 0, 0)),
                   acc_shape=(k, n), nk=nk, k_axis=2)


def _mm_tn_full(name, y, dh, tk, tko):
    t, k = y.shape
    n = dh.shape[1]
    tk = _whole_if_fits(t, tk, 2 * (t * tko * y.dtype.itemsize + t * n * dh.dtype.itemsize + tko * n * 4))
    nk = t // tk
    return _matmul(name, y, dh, dims=_TN, grid=(k // tko, 1, nk),
                   a_spec=pl.BlockSpec((tk, tko), lambda j, i, kk: (kk, j)),
                   b_spec=pl.BlockSpec((tk, n), lambda j, i, kk: (kk, 0)),
                   o_shape=(k, n), o_spec=pl.BlockSpec((tko, n), lambda j, i, kk: (j, 0)),
                   acc_shape=(tko, n), nk=nk, k_axis=2)


def _sigmoid(x):
    return 1.0 / (1.0 + jnp.exp(-x))


def _rms_fwd(name, h, g, tr):
    t, d = h.shape

    def fn(hv, gv):
        r = lax.rsqrt(jnp.mean(hv * hv, axis=-1, keepdims=True) + RMS_EPS)
        return hv * r * gv

    return _call(fn, name, (t // tr,), [(h, (tr, d), lambda i: (i, 0)), (g, (1, d), lambda i: (0, 0))],
                 [((t, d), (tr, d), lambda i: (i, 0), False, MXU_DTYPE)])


def _rms_bwd(name, h, g, dhn, dh, tr):
    t, d = h.shape

    def fn(hv, gv, dy, dh_in):
        r = lax.rsqrt(jnp.mean(hv * hv, axis=-1, keepdims=True) + RMS_EPS)
        xh = hv * r
        dg = jnp.sum(dy * xh, axis=0, keepdims=True)
        dxh = dy * gv
        dx = r * (dxh - xh * jnp.mean(dxh * xh, axis=-1, keepdims=True))
        return dh_in + dx, dg

    row = lambda i: (i, 0)
    return _call(fn, name, (t // tr,),
                 [(h, (tr, d), row), (g, (1, d), lambda i: (0, 0)), (dhn, (tr, d), row), (dh, (tr, d), row)],
                 [((t, d), (tr, d), row, False), ((1, d), (1, d), lambda i: (0, 0), True)], acc_axis=0)


def _final_loss(name, h, g, tgt, tr):
    t, d = h.shape

    def fn(hv, gv, tv):
        r = lax.rsqrt(jnp.mean(hv * hv, axis=-1, keepdims=True) + RMS_EPS)
        xh = hv * r
        row = pl.program_id(0) * tr + lax.broadcasted_iota(jnp.int32, (tr, 1), 0)
        e = jnp.where(row >= N_META, xh * gv - tv, 0.0)
        loss = jnp.broadcast_to(0.5 * jnp.sum(jnp.sum(e * e, axis=-1, keepdims=True), axis=0, keepdims=True) / d,
                                (8, LANES))
        dout = e / d
        dg = jnp.sum(dout * xh, axis=0, keepdims=True)
        dxh = dout * gv
        dx = r * (dxh - xh * jnp.mean(dxh * xh, axis=-1, keepdims=True))
        return loss, dx, dg

    row = lambda i: (i, 0)
    fix = lambda i: (0, 0)
    return _call(fn, name, (t // tr,), [(h, (tr, d), row), (g, (1, d), fix), (tgt, (tr, d), row)],
                 [((8, LANES), (8, LANES), fix, True), ((t, d), (tr, d), row, False), ((1, d), (1, d), fix, True)],
                 acc_axis=0)


def _silu_ln(uc, g, b):
    mu = jnp.mean(uc, axis=-1, keepdims=True)
    xc = uc - mu
    rs = lax.rsqrt(jnp.mean(xc * xc, axis=-1, keepdims=True) + LN_EPS)
    ln = xc * rs * g + b
    return ln * _sigmoid(ln)


def _even_ln_fwd(name, uc, g, b, tr):
    t, d = uc.shape
    row, fix = (lambda i: (i, 0)), (lambda i: (0, 0))
    return _call(_silu_ln, name, (t // tr,), [(uc, (tr, d), row), (g, (1, d), fix), (b, (1, d), fix)],
                 [((t, d), (tr, d), row, False, MXU_DTYPE)])


def _even_ln_bwd(name, uc, g, b, dy, dy_col, tr):
    t, d = uc.shape

    def fn(ucv, gv, bv, dyv):
        mu = jnp.mean(ucv, axis=-1, keepdims=True)
        xc = ucv - mu
        rs = lax.rsqrt(jnp.mean(xc * xc, axis=-1, keepdims=True) + LN_EPS)
        xh = xc * rs
        ln = xh * gv + bv
        s = _sigmoid(ln)
        dln = dyv * (s * (1.0 + ln * (1.0 - s)))
        dg = jnp.sum(dln * xh, axis=0, keepdims=True)
        db = jnp.sum(dln, axis=0, keepdims=True)
        dxh = dln * gv
        duc = rs * (dxh - jnp.mean(dxh, axis=-1, keepdims=True) - xh * jnp.mean(dxh * xh, axis=-1, keepdims=True))
        return duc, dg, db

    row, fix = (lambda i: (i, 0)), (lambda i: (0, 0))
    return _call(fn, name, (t // tr,),
                 [(uc, (tr, d), row), (g, (1, d), fix), (b, (1, d), fix), (dy, (tr, d), lambda i: (i, dy_col))],
                 [((t, d), (tr, d), row, False), ((1, d), (1, d), fix, True), ((1, d), (1, d), fix, True)], acc_axis=0)


def _windows(t):
    rc = _chunk_len(t)
    return [(r0, rc) for r0 in range(0, t, rc)]


def _taps(w_ref, width):
    return [w_ref[pl.ds(j, 1), :] for j in range(width)]


def _conv_at(xp, taps, r0, rc):
    width = len(taps)
    acc = None
    for j in range(width):
        term = xp[pl.ds(CONV_PAD - (width - 1) + j + r0, rc), :] * taps[j]
        acc = term if acc is None else acc + term
    return acc


def _conv_bwd_in_at(dyp, taps, r0, rc):
    width = len(taps)
    acc = None
    for j in range(width):
        term = dyp[pl.ds(width - 1 - j + r0, rc), :] * taps[j]
        acc = term if acc is None else acc + term
    return acc


def _fold(x):
    acc = x[0:8]
    for i in range(1, x.shape[0] // 8):
        acc = acc + x[8 * i:8 * (i + 1)]
    return acc


def _add_to(accs, vals):
    return vals if accs is None else [a + v for a, v in zip(accs, vals)]


def _conv_bwd_w_at(dy, xp, width, r0, rc):
    return [_fold(dy * xp[pl.ds(CONV_PAD - (width - 1) + j + r0, rc), :]) for j in range(width)]


def _store_taps(dw_ref, accs):
    for j, a in enumerate(accs):
        dw_ref[pl.ds(j, 1), :] = jnp.sum(a, axis=0, keepdims=True)


WIDE_COLS = 2 * LANES


def _zero_front(xp):
    xp[pl.ds(0, CONV_PAD), :] = jnp.zeros((CONV_PAD, xp.shape[1]), F32)


def _zero_back(dyp, t):
    dyp[pl.ds(t, CONV_PAD), :] = jnp.zeros((CONV_PAD, dyp.shape[1]), F32)


def _col_call(body, name, ncol, ins, outs, t, n_scratch, cols=LANES):
    def spec(rows, off):
        return pl.BlockSpec((rows, cols), lambda j, off=off: (0, j + off))

    res = pl.pallas_call(
        body, name=name, grid=(ncol,),
        in_specs=[spec(r, off) for _, r, off in ins],
        out_specs=[spec(o[0], 0) for o in outs],
        out_shape=[jax.ShapeDtypeStruct(o[:2], o[2] if len(o) > 2 else F32) for o in outs],
        scratch_shapes=[pltpu.VMEM((t + CONV_PAD, cols), F32) for _ in range(n_scratch)],
        compiler_params=_cparams(None),
    )(*[a for a, _, _ in ins])
    return res


def _even_col_fwd(name, p, conv_a, conv_b):
    t = p.shape[0]
    nc = D_A // LANES
    wins = _windows(t)

    def body(av, ag, gb, gc, xi, ca, cb, uc_ref, yb_ref, xp):
        _zero_front(xp)
        for r0, rc in wins:
            rows = pl.ds(r0, rc)
            xp[pl.ds(CONV_PAD + r0, rc), :] = av[rows, :] * _sigmoid(ag[rows, :])
        taps = _taps(ca, CONV_A_WIDTH)
        for r0, rc in wins:
            uc_ref[pl.ds(r0, rc), :] = _conv_at(xp, taps, r0, rc)
        for r0, rc in wins:
            rows = pl.ds(r0, rc)
            xp[pl.ds(CONV_PAD + r0, rc), :] = gc[rows, :] * xi[rows, :]
        taps = _taps(cb, CONV_B_WIDTH)
        for r0, rc in wins:
            rows = pl.ds(r0, rc)
            yb_ref[rows, :] = (gb[rows, :] * _conv_at(xp, taps, r0, rc)).astype(yb_ref.dtype)

    ins = [(p, t, k * nc) for k in range(5)] + [(conv_a, CONV_A_WIDTH, 0), (conv_b, CONV_B_WIDTH, 0)]
    return _col_call(body, name, nc, ins, [(t, D_A), (t, D_A, MXU_DTYPE)], t, 1)


def _even_col_bwd(name, p, duc, dy, conv_a, conv_b):
    t = p.shape[0]
    nc = D_A // LANES
    wins = _windows(t)

    def body(av, ag, gb, gc, xi, duc_ref, dyb_ref, ca, cb, dav, dag, dgb, dgc, dxi, dca, dcb, xp, dyp):
        _zero_front(xp)
        _zero_back(dyp, t)
        for r0, rc in wins:
            rows = pl.ds(r0, rc)
            xp[pl.ds(CONV_PAD + r0, rc), :] = av[rows, :] * _sigmoid(ag[rows, :])
            dyp[rows, :] = duc_ref[rows, :]
        taps = _taps(ca, CONV_A_WIDTH)
        accs = None
        for r0, rc in wins:
            rows = pl.ds(r0, rc)
            accs = _add_to(accs, _conv_bwd_w_at(duc_ref[rows, :], xp, CONV_A_WIDTH, r0, rc))
            du = _conv_bwd_in_at(dyp, taps, r0, rc)
            sig = _sigmoid(ag[rows, :])
            dav[rows, :] = (du * sig).astype(dav.dtype)
            dag[rows, :] = (du * av[rows, :] * sig * (1.0 - sig)).astype(dag.dtype)
        _store_taps(dca, accs)
        for r0, rc in wins:
            rows = pl.ds(r0, rc)
            xp[pl.ds(CONV_PAD + r0, rc), :] = gc[rows, :] * xi[rows, :]
        taps = _taps(cb, CONV_B_WIDTH)
        accs = None
        for r0, rc in wins:
            rows = pl.ds(r0, rc)
            dgb[rows, :] = (dyb_ref[rows, :] * _conv_at(xp, taps, r0, rc)).astype(dgb.dtype)
            dzc = dyb_ref[rows, :] * gb[rows, :]
            dyp[rows, :] = dzc
            accs = _add_to(accs, _conv_bwd_w_at(dzc, xp, CONV_B_WIDTH, r0, rc))
        _store_taps(dcb, accs)
        for r0, rc in wins:
            rows = pl.ds(r0, rc)
            dz = _conv_bwd_in_at(dyp, taps, r0, rc)
            dgc[rows, :] = (dz * xi[rows, :]).astype(dgc.dtype)
            dxi[rows, :] = (dz * gc[rows, :]).astype(dxi.dtype)

    ins = ([(p, t, k * nc) for k in range(5)] + [(duc, t, 0), (dy, t, nc)]
           + [(conv_a, CONV_A_WIDTH, 0), (conv_b, CONV_B_WIDTH, 0)])
    outs = [(t, D_A, MXU_DTYPE)] * 5 + [(CONV_A_WIDTH, D_A), (CONV_B_WIDTH, D_A)]
    return _col_call(body, name, nc, ins, outs, t, 2)


def _ffn_col_fwd(name, u, conv, bias):
    t = u.shape[0]
    nc = D_FF // WIDE_COLS
    wins = _windows(t)

    def body(g_ref, v_ref, cw, b_ref, a_ref, xp):
        _zero_front(xp)
        xp[pl.ds(CONV_PAD, t), :] = g_ref[...].astype(F32)
        taps = _taps(cw, CONV_B_WIDTH)
        b = b_ref[...]
        for r0, rc in wins:
            rows = pl.ds(r0, rc)
            gc = _conv_at(xp, taps, r0, rc) + b
            a_ref[rows, :] = (gc * _sigmoid(gc) * v_ref[rows, :].astype(F32)).astype(a_ref.dtype)

    ins = [(u, t, 0), (u, t, nc), (conv, CONV_B_WIDTH, 0), (bias, 1, 0)]
    return _col_call(body, name, nc, ins, [(t, D_FF, MXU_DTYPE)], t, 1, cols=WIDE_COLS)[0]


def _ffn_col_bwd(name, u, da, conv, bias):
    t = u.shape[0]
    nc = D_FF // LANES
    wins = _windows(t)

    def body(g_ref, v_ref, da_ref, cw, b_ref, du_ref, dcw, db_ref, xp, dyp, dval):
        @pl.when(pl.program_id(1) == 0)
        def _():
            _zero_front(xp)
            _zero_back(dyp, t)
            xp[pl.ds(CONV_PAD, t), :] = g_ref[...].astype(F32)
            taps = _taps(cw, CONV_B_WIDTH)
            b = b_ref[...]
            accs, bias_acc = None, None
            for r0, rc in wins:
                rows = pl.ds(r0, rc)
                gc = _conv_at(xp, taps, r0, rc) + b
                s = _sigmoid(gc)
                d = da_ref[rows, :]
                dval[rows, :] = d * gc * s
                dgc = d * v_ref[rows, :].astype(F32) * (s * (1.0 + gc * (1.0 - s)))
                dyp[rows, :] = dgc
                bias_acc = _add_to(bias_acc, [_fold(dgc)])
                accs = _add_to(accs, _conv_bwd_w_at(dgc, xp, CONV_B_WIDTH, r0, rc))
            db_ref[...] = jnp.sum(bias_acc[0], axis=0, keepdims=True)
            _store_taps(dcw, accs)
            for r0, rc in wins:
                du_ref[pl.ds(r0, rc), :] = _conv_bwd_in_at(dyp, taps, r0, rc).astype(du_ref.dtype)

        @pl.when(pl.program_id(1) == 1)
        def _():
            du_ref[...] = dval[...].astype(du_ref.dtype)

    col = lambda rows, off: pl.BlockSpec((rows, LANES), lambda j, p: (0, j + off))
    return pl.pallas_call(
        body, name=name, grid=(nc, 2),
        in_specs=[col(t, 0), col(t, nc), col(t, 0), col(CONV_B_WIDTH, 0), col(1, 0)],
        out_specs=[pl.BlockSpec((t, LANES), lambda j, p: (0, j + nc * p)), col(CONV_B_WIDTH, 0), col(1, 0)],
        out_shape=[jax.ShapeDtypeStruct((t, 2 * D_FF), MXU_DTYPE), jax.ShapeDtypeStruct((CONV_B_WIDTH, D_FF), F32),
                   jax.ShapeDtypeStruct((1, D_FF), F32)],
        scratch_shapes=[pltpu.VMEM((t + CONV_PAD, LANES), F32) for _ in range(2)] + [pltpu.VMEM((t, LANES), F32)],
        compiler_params=_cparams(None),
    )(u, u, da, conv, bias)


def _shift_fwd(name, p, col0, mu):
    t = p.shape[0]
    wins = _windows(t)

    def body(x_ref, mu_ref, o_ref, xp):
        _zero_front(xp)
        xp[pl.ds(CONV_PAD, t), :] = x_ref[...]
        mu_v = mu_ref[...]
        for r0, rc in wins:
            rows = pl.ds(r0, rc)
            x = x_ref[rows, :]
            o_ref[rows, :] = x + (xp[pl.ds(CONV_PAD - 1 + r0, rc), :] - x) * mu_v

    return _col_call(body, name, RWKV_COLS // WIDE_COLS, [(p, t, col0 // WIDE_COLS), (mu, 1, 0)], [(t, RWKV_COLS)], t, 1,
                     cols=WIDE_COLS)[0]


def _shift_bwd(name, p, col0, mu, dprs):
    t = p.shape[0]
    wins = _windows(t)

    def body(x_ref, mu_ref, d_ref, dx_ref, dmu_ref, xp, dyp):
        _zero_front(xp)
        _zero_back(dyp, t)
        xp[pl.ds(CONV_PAD, t), :] = x_ref[...]
        mu_v = mu_ref[...]
        acc = None
        for r0, rc in wins:
            rows = pl.ds(r0, rc)
            d = d_ref[rows, :]
            acc = _add_to(acc, [_fold(d * (xp[pl.ds(CONV_PAD - 1 + r0, rc), :] - x_ref[rows, :]))])
            dyp[rows, :] = d * mu_v
        dmu_ref[...] = jnp.sum(acc[0], axis=0, keepdims=True)
        for r0, rc in wins:
            rows = pl.ds(r0, rc)
            dx_ref[rows, :] = d_ref[rows, :] - dyp[rows, :] + dyp[pl.ds(1 + r0, rc), :]

    ins = [(p, t, col0 // WIDE_COLS), (mu, 1, 0), (dprs, t, 0)]
    return _col_call(body, name, RWKV_COLS // WIDE_COLS, ins, [(t, RWKV_COLS), (1, RWKV_COLS)], t, 2, cols=WIDE_COLS)


def _hi_lo(x):
    hi = x.astype(BF16)
    return hi, (x - hi.astype(F32)).astype(BF16)


def _dot_passes(a, b, dims, passes):
    d = lambda p, q: lax.dot_general(p, q, dims, preferred_element_type=F32)
    if passes == 1:
        return d(a.astype(MXU_DTYPE), b.astype(MXU_DTYPE))
    ah, al = _hi_lo(a)
    bh, bl = _hi_lo(b)
    return d(ah, bh) + (d(ah, bl) + d(al, bh))


@functools.partial(jax.custom_vjp, nondiff_argnums=(2, 3))
def _dot_vjp(a, b, dims, passes):
    return _dot_passes(a, b, dims, passes)


def _dot_fwd(a, b, dims, passes):
    return _dot_passes(a, b, dims, passes), (a, b)


def _dot_bwd(dims, passes, res, g):
    a, b = res
    if dims == _NN:
        return _dot_passes(g, b, _NT, passes), _dot_passes(a, g, _TN, passes)
    if dims == _NT:
        return _dot_passes(g, b, _NN, passes), _dot_passes(g, a, _TN, passes)
    return _dot_passes(b, g, _NT, passes), _dot_passes(a, g, _NN, passes)


_dot_vjp.defvjp(_dot_fwd, _dot_bwd)


def _doth(a, b, dims=_NN):
    return _dot_vjp(a, b, dims, 3)


def _dotb(a, b, dims=_NN):
    return _dot_vjp(a, b, dims, 1)


def _softplus(x):
    return jnp.where(x > 0, x, 0.0) + jnp.log(1.0 + jnp.exp(jnp.where(x > 0, -x, x)))


def _rwkv_pre(k, xl, gd, w0, w2p, a0, a2p, g2, k_k, k_a, seg):
    z = w0 + _dotb(jnp.tanh(xl), w2p)
    lw = -jnp.exp(-_softplus(-z) - 0.5)
    alpha = _sigmoid(a0 + _dotb(xl, a2p))
    g = _dotb(_sigmoid(gd), g2)
    kk = k * k_k
    kk = kk / jnp.maximum(jnp.sqrt(_dotb(kk * kk, seg)), 1e-12)
    k2 = k * (1.0 + (alpha - 1.0) * k_a)
    return lw, k2, -kk, kk * alpha, g


def _rwkv_post(y, r, k2, v, g, lnx_g, lnx_b, r_k, seg):
    mean = _dotb(y, seg) * (1.0 / HEAD_DIM)
    yc = y - mean
    var = _dotb(yc * yc, seg) * (1.0 / HEAD_DIM)
    yo = yc * lax.rsqrt(var + RWKV_GN_EPS) * lnx_g + lnx_b
    bonus = _dotb(r * k2 * r_k, seg) * v
    return (yo + bonus) * g


def _rwkv_pre_fwd(name, prs, prm, seg, tr):
    t = prs.shape[0]
    row = lambda i: (i, 0)
    fix = lambda i: (0, 0)
    ins = [(prs, (tr, D_R), lambda i: (i, 1)), (prs, (tr, LANES), lambda i: (i, 12)), (prs, (tr, LANES), lambda i: (i, 13)),
           (prm["w0"], (1, D_R), fix), (prm["w2p"], (LANES, D_R), fix), (prm["a0"], (1, D_R), fix),
           (prm["a2p"], (LANES, D_R), fix), (prm["g2"], (LANES, D_R), fix), (prm["k_k"], (1, D_R), fix),
           (prm["k_a"], (1, D_R), fix), (seg, (D_R, D_R), fix)]
    return _call(_rwkv_pre, name, (t // tr,), ins, [((t, D_R), (tr, D_R), row, False)] * 5)


def _rwkv_pre_bwd(name, prs, prm, seg, cts, tr):
    t = prs.shape[0]

    def fn(k, xl, gd, w0, w2p, a0, a2p, g2, k_k, k_a, segv, *ct):
        _, vjp = jax.vjp(lambda *a: _rwkv_pre(*a, segv), k, xl, gd, w0, w2p, a0, a2p, g2, k_k, k_a)
        return vjp(tuple(ct))

    row = lambda i: (i, 0)
    fix = lambda i: (0, 0)
    ins = [(prs, (tr, D_R), lambda i: (i, 1)), (prs, (tr, LANES), lambda i: (i, 12)), (prs, (tr, LANES), lambda i: (i, 13)),
           (prm["w0"], (1, D_R), fix), (prm["w2p"], (LANES, D_R), fix), (prm["a0"], (1, D_R), fix),
           (prm["a2p"], (LANES, D_R), fix), (prm["g2"], (LANES, D_R), fix), (prm["k_k"], (1, D_R), fix),
           (prm["k_a"], (1, D_R), fix), (seg, (D_R, D_R), fix)] + [(c, (tr, D_R), row) for c in cts]
    outs = [((t, D_R), (tr, D_R), row, False), ((t, LANES), (tr, LANES), row, False), ((t, LANES), (tr, LANES), row, False),
            ((1, D_R), (1, D_R), fix, True), ((LANES, D_R), (LANES, D_R), fix, True), ((1, D_R), (1, D_R), fix, True),
            ((LANES, D_R), (LANES, D_R), fix, True), ((LANES, D_R), (LANES, D_R), fix, True),
            ((1, D_R), (1, D_R), fix, True), ((1, D_R), (1, D_R), fix, True)]
    return _call(fn, name, (t // tr,), ins, outs, acc_axis=0)


def _rwkv_post_ins(y, prs, k2, g, prm, seg, tr):
    row = lambda i: (i, 0)
    fix = lambda i: (0, 0)
    return [(y, (tr, D_R), row), (prs, (tr, D_R), row), (k2, (tr, D_R), row), (prs, (tr, D_R), lambda i: (i, 2)),
            (g, (tr, D_R), row), (prm["lnx_g"], (1, D_R), fix), (prm["lnx_b"], (1, D_R), fix), (prm["r_k"], (1, D_R), fix),
            (seg, (D_R, D_R), fix)]


def _rwkv_post_fwd(name, y, prs, k2, g, prm, seg, tr):
    t = y.shape[0]
    return _call(_rwkv_post, name, (t // tr,), _rwkv_post_ins(y, prs, k2, g, prm, seg, tr),
                 [((t, D_R), (tr, D_R), lambda i: (i, 0), False)])


def _rwkv_post_bwd(name, y, prs, k2, g, prm, seg, dy, dy_col, tr):
    t = y.shape[0]

    def fn(yv, r, k2v, v, gv, lg, lb, rk, segv, ct):
        _, vjp = jax.vjp(lambda *a: _rwkv_post(*a, segv), yv, r, k2v, v, gv, lg, lb, rk)
        return vjp(ct)

    row = lambda i: (i, 0)
    fix = lambda i: (0, 0)
    ins = _rwkv_post_ins(y, prs, k2, g, prm, seg, tr) + [(dy, (tr, D_R), lambda i: (i, dy_col))]
    outs = [((t, D_R), (tr, D_R), row, False)] * 5 + [((1, D_R), (1, D_R), fix, True)] * 3
    return _call(fn, name, (t // tr,), ins, outs, acc_axis=0)


def _wkv_chunk(s0, r, lw, k, v, a, b):
    c = r[0].shape[0]
    lane = lax.broadcasted_iota(jnp.int32, (1, 2 * HEAD_DIM), 1)
    first = (lane < HEAD_DIM).astype(F32)
    per_head = lambda x: jnp.concatenate([x * first, x * (1.0 - first)], axis=0)

    def time_of(shape, dim):
        i = lax.broadcasted_iota(jnp.int32, shape, dim)
        return jnp.where(i >= c, i - c, i)

    incl = (lax.broadcasted_iota(jnp.int32, (c, c), 0) >= lax.broadcasted_iota(jnp.int32, (c, c), 1)).astype(F32)
    strict2 = time_of((2 * c, 2 * c), 0) > time_of((2 * c, 2 * c), 1)
    incl2 = lax.broadcasted_iota(jnp.int32, (c, 2 * c), 0) >= time_of((c, 2 * c), 1)
    each = lambda f, *xs: [f(*x) for x in zip(*xs)]
    cum = each(lambda x: _doth(incl, x), lw)
    tot = each(lambda x: jnp.sum(x, axis=0, keepdims=True), lw)
    e_inv = each(lambda x: jnp.exp(-x), cum)
    a_st = each(lambda x, cm, l: per_head(x * jnp.exp(cm - l)), a, cum, lw)
    r_t = each(lambda x, cm: x * jnp.exp(cm), r, cum)
    b_st = each(lambda x, e: per_head(x * e), b, e_inv)
    k_st = each(lambda x, e: per_head(x * e), k, e_inv)
    v_st = each(per_head, v)
    m = each(lambda x, w: jnp.where(strict2, _dotb(x, w, _NT), 0.0), a_st, b_st)
    m_k = each(lambda x, w: jnp.where(strict2, _dotb(x, w, _NT), 0.0), a_st, k_st)
    u = each(lambda x, s, mk, w: _dotb(x, s, _NT) + _dotb(mk, w), a_st, s0, m_k, v_st)
    steps = (c - 1).bit_length()
    for s in range(steps):
        u = each(lambda x, w: x + _dotb(w, x), u, m)
        if s + 1 < steps:
            m = each(lambda w: _dotb(w, w), m)
    n_b = each(lambda x, w: jnp.where(incl2, _dotb(x, w, _NT), 0.0), r_t, b_st)
    n_k = each(lambda x, w: jnp.where(incl2, _dotb(x, w, _NT), 0.0), r_t, k_st)
    y = each(lambda x, s, nb, uu, nk, w: _dotb(x, s, _NT) + _dotb(nb, uu) + _dotb(nk, w), r_t, s0, n_b, u, n_k, v_st)
    dec = each(lambda tt, cm: jnp.exp(tt - cm), tot, cum)
    s1 = each(lambda s, tt, uu, x, d, w, kk: s * jnp.exp(tt) + _dotb(uu, per_head(x * d), _TN) + _dotb(w, per_head(kk * d), _TN),
              s0, tot, u, b, dec, v_st, k)
    return tuple(y), tuple(s1)


WKV_PAIRS_PER_STEP = 4
PAIR = 2 * HEAD_DIM


def _wkv_fwd(name, srcs):
    t = srcs[0][0].shape[0]
    c = _chunk_len(t)
    nc = t // c
    pp = WKV_PAIRS_PER_STEP
    n_pairs = D_R // PAIR

    def body(r, lw, k, v, a, b, y_ref, st_ref, state):
        @pl.when(pl.program_id(1) == 0)
        def _():
            state[...] = jnp.zeros(state.shape, F32)

        pairs = lambda ref: tuple(ref[:, pl.ds(i * PAIR, PAIR)] for i in range(pp))
        s0 = tuple(state[i] for i in range(pp))
        y, s1 = _wkv_chunk(s0, pairs(r), pairs(lw), pairs(k), pairs(v), pairs(a), pairs(b))
        for i in range(pp):
            st_ref[i] = s0[i]
            y_ref[:, pl.ds(i * PAIR, PAIR)] = y[i]
            state[i] = s1[i]

    seq = lambda off: pl.BlockSpec((c, pp * PAIR), lambda g, j: (j, off + g))
    return pl.pallas_call(
        body, name=name, grid=(n_pairs // pp, nc), in_specs=[seq(off) for _, off in srcs],
        out_specs=[seq(0), pl.BlockSpec((pp, None, PAIR, PAIR), lambda g, j: (g, j, 0, 0))],
        out_shape=[jax.ShapeDtypeStruct((t, D_R), F32), jax.ShapeDtypeStruct((n_pairs, nc, PAIR, PAIR), F32)],
        scratch_shapes=[pltpu.VMEM((pp, PAIR, PAIR), F32)],
        compiler_params=_cparams(None),
    )(*[a for a, _ in srcs])


def _wkv_bwd(name, srcs, st, dy):
    t = srcs[0][0].shape[0]
    c = _chunk_len(t)
    nc = t // c
    pp = WKV_PAIRS_PER_STEP
    n_pairs = D_R // PAIR

    def body(r, lw, k, v, a, b, st_ref, dy_ref, dr, dlw, dk, dv, da, db, dstate):
        @pl.when(pl.program_id(1) == 0)
        def _():
            dstate[...] = jnp.zeros(dstate.shape, F32)

        half = lax.broadcasted_iota(jnp.int32, (PAIR, PAIR), 0) < HEAD_DIM
        same_head = half == (lax.broadcasted_iota(jnp.int32, (PAIR, PAIR), 1) < HEAD_DIM)
        pairs = lambda ref: tuple(ref[:, pl.ds(i * PAIR, PAIR)] for i in range(pp))
        s0 = tuple(st_ref[i] for i in range(pp))
        _, vjp = jax.vjp(_wkv_chunk, s0, pairs(r), pairs(lw), pairs(k), pairs(v), pairs(a), pairs(b))
        ds0, *dxs = vjp((pairs(dy_ref), tuple(dstate[i] for i in range(pp))))
        for i in range(pp):
            for ref, val in zip((dr, dlw, dk, dv, da, db), dxs):
                ref[:, pl.ds(i * PAIR, PAIR)] = val[i]
            dstate[i] = jnp.where(same_head, ds0[i], 0.0)

    seq = lambda off: pl.BlockSpec((c, pp * PAIR), lambda g, j: (nc - 1 - j, off + g))
    return pl.pallas_call(
        body, name=name, grid=(n_pairs // pp, nc),
        in_specs=[seq(off) for _, off in srcs]
        + [pl.BlockSpec((pp, None, PAIR, PAIR), lambda g, j: (g, nc - 1 - j, 0, 0)), seq(dy[1])],
        out_specs=[seq(0)] * 6,
        out_shape=[jax.ShapeDtypeStruct((t, D_R), F32)] * 6,
        scratch_shapes=[pltpu.VMEM((pp, PAIR, PAIR), F32)],
        compiler_params=_cparams(None),
    )(*[a for a, _ in srcs], st, dy[0])


def _rope(x, cos, sin, rot):
    return x * cos + _dotb(x, rot) * sin


def _attn_block(nb, q, kp, kc, km, vp, vc, vm, sk, cq, sq, cp, sp, cm, sm, rot):
    g = GQA_GROUP
    scale = HEAD_DIM ** -0.5
    each = lambda f, *xs: [f(*x) for x in zip(*xs)]
    down = lambda x: jnp.concatenate([x] * g, axis=0)
    cq4, sq4 = down(cq), down(sq)
    kpr = each(lambda x: _rope(x, cp, sp, rot), kp)
    kcr = each(lambda x: _rope(x, cq, sq, rot), kc)
    kmr = each(lambda x: _rope(x, cm, sm, rot), km)
    qr = each(lambda x: _rope(x, cq4, sq4, rot), q)
    i = lax.broadcasted_iota(jnp.int32, (g * BLOCK, BLOCK), 0)
    i = i - BLOCK * ((i >= BLOCK).astype(jnp.int32) + (i >= 2 * BLOCK).astype(jnp.int32) + (i >= 3 * BLOCK).astype(jnp.int32))
    j = lax.broadcasted_iota(jnp.int32, (g * BLOCK, BLOCK), 1)
    nbv = jnp.zeros((g * BLOCK, BLOCK), jnp.int32) + nb
    ok_p = (j > i) & (nbv >= 2)
    ok_c = (j <= i) & (nbv >= 1)
    ok_m = (j >= BLOCK - N_META) & ((nbv >= 1) | (j <= i))
    sink = each(lambda s4: jnp.concatenate([jnp.broadcast_to(s, (BLOCK, 1)) for s in s4], axis=0), sk)
    s_p = each(lambda x, kk: jnp.where(ok_p, _dotb(x, kk, _NT) * scale, NEG_INF), qr, kpr)
    s_c = each(lambda x, kk: jnp.where(ok_c, _dotb(x, kk, _NT) * scale, NEG_INF), qr, kcr)
    s_m = each(lambda x, kk: jnp.where(ok_m, _dotb(x, kk, _NT) * scale, NEG_INF), qr, kmr)
    rmax = lambda s: jnp.max(s, axis=-1, keepdims=True)
    m = each(lambda a, b, c, d: lax.stop_gradient(jnp.maximum(jnp.maximum(rmax(a), rmax(b)), jnp.maximum(rmax(c), d))),
             s_p, s_c, s_m, sink)
    e_p = each(lambda s, mm: jnp.exp(s - mm), s_p, m)
    e_c = each(lambda s, mm: jnp.exp(s - mm), s_c, m)
    e_m = each(lambda s, mm: jnp.exp(s - mm), s_m, m)
    rsum = lambda e: jnp.sum(e, axis=-1, keepdims=True)
    inv = each(lambda a, b, c, d, mm: 1.0 / (rsum(a) + rsum(b) + rsum(c) + jnp.exp(d - mm)), e_p, e_c, e_m, sink, m)
    return tuple(each(lambda a, b, c, iv, x, y, z: _dotb(a * iv, x) + _dotb(b * iv, y) + _dotb(c * iv, z),
                      e_p, e_c, e_m, inv, vp, vc, vm))


def _attn_specs():
    cur = lambda n: (0, n, 0)
    prev = lambda n: (0, jnp.maximum(n - 1, 0), 0)
    meta = lambda n: (0, 0, 0)
    kv = lambda m: pl.BlockSpec((N_KV_HEADS, BLOCK, HEAD_DIM), m)
    tab = lambda m: pl.BlockSpec((BLOCK, HEAD_DIM), m)
    tcur, tprev, tmeta = (lambda n: (n, 0)), (lambda n: (jnp.maximum(n - 1, 0), 0)), (lambda n: (0, 0))
    qspec = pl.BlockSpec((N_Q_HEADS, BLOCK, HEAD_DIM), cur)
    sspec = pl.BlockSpec((N_Q_HEADS, 8, LANES), meta)
    specs = [qspec, kv(prev), kv(cur), kv(meta), kv(prev), kv(cur), kv(meta), sspec,
             tab(tcur), tab(tcur), tab(tprev), tab(tprev), tab(tmeta), tab(tmeta),
             pl.BlockSpec((HEAD_DIM, HEAD_DIM), lambda n: (0, 0))]
    return specs, qspec, sspec, kv


def _attn_args(q, k, v, sinks_b, cos, sin, rot):
    return (q, k, k, k, v, v, v, sinks_b, cos, sin, cos, sin, cos, sin, rot)


def _attn_operands(q_ref, kp, kc, km, vp, vc, vm, s_ref):
    groups = range(N_KV_HEADS)
    q = tuple(jnp.concatenate([q_ref[GQA_GROUP * i + h] for h in range(GQA_GROUP)], axis=0) for i in groups)
    sk = tuple(tuple(s_ref[GQA_GROUP * i + h][0:1, 0:1] for h in range(GQA_GROUP)) for i in groups)
    per_head = lambda ref: tuple(ref[i] for i in groups)
    return q, per_head(kp), per_head(kc), per_head(km), per_head(vp), per_head(vc), per_head(vm), sk


def _attn_fwd(name, q, k, v, sinks_b, cos, sin, rot):
    tp = q.shape[1]
    specs, qspec, _, _ = _attn_specs()

    def body(q_ref, kp, kc, km, vp, vc, vm, s_ref, cq, sq, cp, sp, cm, sm, rot_ref, o_ref):
        out = _attn_block(pl.program_id(0), *_attn_operands(q_ref, kp, kc, km, vp, vc, vm, s_ref),
                          cq[...], sq[...], cp[...], sp[...], cm[...], sm[...], rot_ref[...])
        for i in range(N_KV_HEADS):
            for h in range(GQA_GROUP):
                o_ref[GQA_GROUP * i + h] = out[i][h * BLOCK:(h + 1) * BLOCK]

    return pl.pallas_call(
        body, name=name, grid=(tp // BLOCK,), in_specs=specs, out_specs=qspec,
        out_shape=jax.ShapeDtypeStruct(q.shape, F32), compiler_params=_cparams(None),
    )(*_attn_args(q, k, v, sinks_b, cos, sin, rot))


def _attn_bwd(name, q, k, v, sinks_b, cos, sin, rot, do):
    tp = q.shape[1]
    nb = tp // BLOCK
    specs, qspec, sspec, kv = _attn_specs()

    def body(q_ref, kp, kc, km, vp, vc, vm, s_ref, cq, sq, cp, sp, cm, sm, rot_ref, do_ref,
             dq_ref, dkp, dkc, dvp, dvc, dkm, dvm, ds_ref):
        n = pl.program_id(0)
        tabs = (cq[...], sq[...], cp[...], sp[...], cm[...], sm[...], rot_ref[...])
        _, vjp = jax.vjp(lambda *a: _attn_block(n, *a, *tabs), *_attn_operands(q_ref, kp, kc, km, vp, vc, vm, s_ref))
        do_all = tuple(jnp.concatenate([do_ref[GQA_GROUP * i + h] for h in range(GQA_GROUP)], axis=0)
                       for i in range(N_KV_HEADS))
        dq, gkp, gkc, gkm, gvp, gvc, gvm, dsk = vjp(do_all)
        for i in range(N_KV_HEADS):
            dkp[i] = gkp[i]
            dkc[i] = gkc[i]
            dvp[i] = gvp[i]
            dvc[i] = gvc[i]
            for h in range(GQA_GROUP):
                dq_ref[GQA_GROUP * i + h] = dq[i][h * BLOCK:(h + 1) * BLOCK]

        @pl.when(n == 0)
        def _():
            for i in range(N_KV_HEADS):
                dkm[i] = gkm[i]
                dvm[i] = gvm[i]
                for h in range(GQA_GROUP):
                    ds_ref[GQA_GROUP * i + h] = jnp.broadcast_to(dsk[i][h], (8, LANES))

        @pl.when(n != 0)
        def _():
            for i in range(N_KV_HEADS):
                dkm[i] += gkm[i]
                dvm[i] += gvm[i]
                for h in range(GQA_GROUP):
                    ds_ref[GQA_GROUP * i + h] += jnp.broadcast_to(dsk[i][h], (8, LANES))

    part = pl.BlockSpec((N_KV_HEADS, None, BLOCK, HEAD_DIM), lambda n: (0, n, 0, 0))
    part_shape = jax.ShapeDtypeStruct((N_KV_HEADS, nb, BLOCK, HEAD_DIM), F32)
    meta_shape = jax.ShapeDtypeStruct((N_KV_HEADS, BLOCK, HEAD_DIM), F32)
    return pl.pallas_call(
        body, name=name, grid=(nb,), in_specs=specs + [qspec],
        out_specs=[qspec, part, part, part, part, kv(lambda n: (0, 0, 0)), kv(lambda n: (0, 0, 0)), sspec],
        out_shape=[jax.ShapeDtypeStruct(q.shape, F32), part_shape, part_shape, part_shape, part_shape,
                   meta_shape, meta_shape, jax.ShapeDtypeStruct(sinks_b.shape, F32)],
        compiler_params=_cparams(None),
    )(*_attn_args(q, k, v, sinks_b, cos, sin, rot), do)


def _kv_combine(name, k_parts, v_parts):
    g, nb = k_parts[1].shape[:2]

    def fn(own_k, nxt_k, mt_k, own_v, nxt_v, mt_v):
        m = pl.program_id(1)
        one = jnp.ones((BLOCK, HEAD_DIM), F32)
        use_next = jnp.where(one * m < nb - 1, 1.0, 0.0)
        use_meta = jnp.where(one * m < 1, 1.0, 0.0)
        return own_k + nxt_k * use_next + mt_k * use_meta, own_v + nxt_v * use_next + mt_v * use_meta

    blk = (None, None, BLOCK, HEAD_DIM)
    ins = []
    for prev_part, own_part, meta in (k_parts, v_parts):
        ins += [(own_part, blk, lambda a, m: (a, m, 0, 0)),
                (prev_part, blk, lambda a, m: (a, jnp.minimum(m + 1, nb - 1), 0, 0)),
                (meta, (None, BLOCK, HEAD_DIM), lambda a, m: (a, 0, 0))]
    return _call(fn, name, (g, nb), ins,
                 [((g, nb * BLOCK, HEAD_DIM), (None, BLOCK, HEAD_DIM), lambda a, m: (a, m, 0), False)] * 2)


PACK_W = 1024
ELEMENTWISE_BLOCK_BYTES = 1 << 21


def _rows_tile(rows, cols):
    cap = max(8, ELEMENTWISE_BLOCK_BYTES // (4 * cols))
    for d in range(min(rows, cap), 0, -1):
        if rows % d == 0 and d % 8 == 0:
            return d
    return rows


def _adamw(name, w, g, m, v):
    rows, cols = w.shape
    tr = _rows_tile(rows, cols)

    def fn(wv, gv, mv, vv):
        m1 = ADAM_B1 * mv + (1.0 - ADAM_B1) * gv
        v1 = ADAM_B2 * vv + (1.0 - ADAM_B2) * (gv * gv)
        m_hat = m1 / (1.0 - ADAM_B1 ** ADAM_STEP)
        v_hat = v1 / (1.0 - ADAM_B2 ** ADAM_STEP)
        return -ADAM_LR * (m_hat / (jnp.sqrt(v_hat) + ADAM_EPS) + ADAM_WD * wv), m1, v1

    blk = (tr, cols)
    row = lambda i: (i, 0)
    return _call(fn, name, (rows // tr,), [(a, blk, row) for a in (w, g, m, v)], [((rows, cols), blk, row, False)] * 3)


def _pair_add_placed(name, g, recv, cm_idx, out_dtype):
    s, a, b = g.shape
    half = a // 2

    def body(cm_ref, a_ref, b_ref, o_ref, own_ref):
        val = (a_ref[...] + b_ref[...]).astype(out_dtype)
        o_ref[...] = val

        @pl.when(pl.program_id(0) == cm_ref[1])
        def _():
            own_ref[...] = val

    blk = (None, half, b)
    shape = jax.ShapeDtypeStruct((s, half, b), out_dtype)
    return pl.pallas_call(
        body, name=name,
        grid_spec=pltpu.PrefetchScalarGridSpec(
            num_scalar_prefetch=1, grid=(s,),
            in_specs=[pl.BlockSpec(blk, lambda j, cm: (j, cm[0], 0)), pl.BlockSpec(blk, lambda j, cm: (j, 0, 0))],
            out_specs=[pl.BlockSpec(blk, lambda j, cm: (j, 0, 0)), pl.BlockSpec(blk, lambda j, cm: (cm[1], 0, 0))]),
        out_shape=[shape, shape], compiler_params=_cparams(None),
    )(cm_idx, g, recv)


def _sum_chips(name, parts, c_idx, layer, n_layers, into=None):
    _, a, b = parts.shape
    tr = _rows_tile(a, b)

    def body(c_ref, p0, p1, p2, p3, *rest):
        o_ref = rest[-1]
        up = lambda p: p[...].astype(F32)
        o_ref[...] = ((up(p0) + up(p1)) + up(p2)) + up(p3)

    in_specs = [pl.BlockSpec((None, tr, b), lambda i, c, k=k: (k, i, 0)) for k in range(N_CHIPS)]
    args = [c_idx] + [parts] * N_CHIPS
    aliases = {}
    if into is not None:
        in_specs.append(_ANY)
        args.append(into)
        aliases = {1 + N_CHIPS: 0}
    return pl.pallas_call(
        body, name=name,
        grid_spec=pltpu.PrefetchScalarGridSpec(
            num_scalar_prefetch=1, grid=(a // tr,), in_specs=in_specs,
            out_specs=pl.BlockSpec((None, None, tr, b), lambda i, c: (layer, c[0], i, 0))),
        out_shape=jax.ShapeDtypeStruct((n_layers, 2, a, b), F32), input_output_aliases=aliases,
        compiler_params=_cparams(None),
    )(*args)


def _place_own_block(name, w, layer, me_idx, dtype):
    _, a2, b = w.shape
    a = a2 // 2
    tr = _rows_tile(a, b)
    nb = a // tr

    def body(me_ref, w_ref, o_ref):
        o_ref[...] = w_ref[...].astype(dtype)

    return pl.pallas_call(
        body, name=name,
        grid_spec=pltpu.PrefetchScalarGridSpec(
            num_scalar_prefetch=1, grid=(2, nb),
            in_specs=[pl.BlockSpec((None, tr, b), lambda h, i, me: (layer, h * nb + i, 0))],
            out_specs=pl.BlockSpec((None, None, tr, b), lambda h, i, me: (me[0], h, i, 0))),
        out_shape=jax.ShapeDtypeStruct((N_CHIPS, 2, a, b), dtype), compiler_params=_cparams(None),
    )(me_idx, w)


def _mesh_pos():
    return lax.axis_index("x"), lax.axis_index("y"), lax.axis_index("c")


def _other_chips(x, y):
    return [(1 - x, y), (x, 1 - y), (1 - x, 1 - y)]


_ANY = pl.BlockSpec(memory_space=pl.ANY)


def _gather_weights(name, bufs, from_chips=True):
    n = len(bufs)

    def body(*refs):
        out_refs = refs[n:2 * n]
        send_sems, recv_sems = refs[2 * n:]
        x, y, c = _mesh_pos()
        me = 2 * x + y
        sibling = (x, y, 1 - c)
        chips = _other_chips(x, y)

        def copy(i, k, chip_idx, half, to):
            return pltpu.make_async_remote_copy(src_ref=out_refs[i].at[chip_idx, half], dst_ref=out_refs[i].at[chip_idx, half],
                                                send_sem=send_sems.at[6 * i + k], recv_sem=recv_sems.at[6 * i + k],
                                                device_id=to, device_id_type=MESH)

        first = [copy(i, j, me, c, (*chip, c)) for i in range(n) for j, chip in enumerate(chips)] if from_chips else []
        for cp in first:
            cp.start()
        passed = []
        for i in range(n):
            for j, (cx, cy) in enumerate(chips):
                idx = 2 * cx + cy
                if from_chips:
                    copy(i, j, idx, c, sibling).wait_recv()
                fwd = copy(i, 3 + j, idx, c, sibling)
                fwd.start()
                passed.append(fwd)
        for i in range(n):
            for j, (cx, cy) in enumerate(chips):
                copy(i, 3 + j, 2 * cx + cy, 1 - c, sibling).wait_recv()
        for cp in first + passed:
            cp.wait_send()

    return pl.pallas_call(
        body, name=name, in_specs=[_ANY] * n, out_specs=[_ANY] * n,
        out_shape=[jax.ShapeDtypeStruct(b.shape, b.dtype) for b in bufs],
        input_output_aliases={i: i for i in range(n)},
        scratch_shapes=[pltpu.SemaphoreType.DMA((6 * n,)), pltpu.SemaphoreType.DMA((6 * n,))],
        compiler_params=pltpu.CompilerParams(has_side_effects=True),
    )(*bufs)


def _gather_start(name, groups):
    bufs = [b for g in groups for b in g]
    n = len(bufs)
    ng = len(groups)

    def body(*refs):
        b_refs = refs[:n]
        sems = refs[n:n + 2 * ng]
        token = refs[-1]
        x, y, c = _mesh_pos()
        me = 2 * x + y
        i = 0
        for gi, g in enumerate(groups):
            for k in range(len(g)):
                for j, (cx, cy) in enumerate(_other_chips(x, y)):
                    pltpu.make_async_remote_copy(src_ref=b_refs[i].at[me, c], dst_ref=b_refs[i].at[me, c],
                                                 send_sem=sems[2 * gi].at[3 * k + j], recv_sem=sems[2 * gi + 1].at[3 * k + j],
                                                 device_id=(cx, cy, c), device_id_type=MESH).start()
                i += 1
        token[...] = jnp.zeros(token.shape, F32)

    sem_shapes = [pltpu.SemaphoreType.DMA((3 * len(g),)) for g in groups for _ in range(2)]
    res = pl.pallas_call(
        body, name=name,
        out_shape=(*sem_shapes, *[pltpu.HBM(b.shape, b.dtype) for b in bufs], jax.ShapeDtypeStruct((8, LANES), F32)),
        in_specs=[_HBM] * n,
        out_specs=(*[_SEM] * (2 * ng), *[_HBM] * n, pl.BlockSpec(memory_space=pltpu.VMEM)),
        input_output_aliases={i: 2 * ng + i for i in range(n)},
        compiler_params=pltpu.CompilerParams(has_side_effects=_DATAFLOW),
    )(*[pltpu.with_memory_space_constraint(b, pltpu.HBM) for b in bufs])
    out, i = [], 2 * ng
    for gi, g in enumerate(groups):
        out.append((res[2 * gi], res[2 * gi + 1], list(res[i:i + len(g)])))
        i += len(g)
    return out, res[-1]


def _gather_wait(name, send_sems, recv_sems, bufs, after):
    n = len(bufs)

    def body(*refs):
        b_refs = refs[:n]
        s_sems, r_sems = refs[n], refs[n + 1]
        x, y, c = _mesh_pos()
        me = 2 * x + y
        for k in range(n):
            for j, (cx, cy) in enumerate(_other_chips(x, y)):
                idx = 2 * cx + cy
                copy = pltpu.make_async_remote_copy(src_ref=b_refs[k].at[me, c], dst_ref=b_refs[k].at[idx, c],
                                                    send_sem=s_sems.at[3 * k + j], recv_sem=r_sems.at[3 * k + j],
                                                    device_id=(cx, cy, c), device_id_type=MESH)
                copy.wait_send()
                copy.wait_recv()

    res = pl.pallas_call(
        body, name=name,
        out_shape=tuple(pltpu.HBM(b.shape, b.dtype) for b in bufs),
        in_specs=[_HBM] * n + [_SEM, _SEM, _ANY],
        out_specs=tuple([_HBM] * n),
        input_output_aliases={i: i for i in range(n)},
        compiler_params=pltpu.CompilerParams(has_side_effects=_DATAFLOW),
    )(*bufs, send_sems, recv_sems, after)
    return list(res)


def _halves_to_sibling(name, units):
    n = len(units)

    def body(*refs):
        g_refs, out_refs = refs[:n], refs[n:2 * n]
        send_sems, recv_sems = refs[2 * n:]
        x, y, c = _mesh_pos()
        cps = []
        for i in range(n):
            half = units[i].shape[1] // 2
            src = g_refs[i].at[pl.ds(0, N_CHIPS), pl.ds((1 - c) * half, half)]
            cp = pltpu.make_async_remote_copy(src_ref=src, dst_ref=out_refs[i], send_sem=send_sems.at[i],
                                              recv_sem=recv_sems.at[i], device_id=(x, y, 1 - c), device_id_type=MESH)
            cp.start()
            cps.append(cp)
        for cp in cps:
            cp.wait()

    return pl.pallas_call(
        body, name=name, in_specs=[_ANY] * n, out_specs=[_ANY] * n,
        out_shape=[jax.ShapeDtypeStruct((u.shape[0], u.shape[1] // 2, u.shape[2]), u.dtype) for u in units],
        scratch_shapes=[pltpu.SemaphoreType.DMA((n,)), pltpu.SemaphoreType.DMA((n,))],
        compiler_params=pltpu.CompilerParams(has_side_effects=True),
    )(*units)


_HBM = pl.BlockSpec(memory_space=pltpu.HBM)
_SEM = pl.BlockSpec(memory_space=pltpu.SEMAPHORE)
_DATAFLOW = pltpu.SideEffectType.DATAFLOW_SIDE_EFFECTING


def _halves_start(name, units):
    n = len(units)

    def body(*refs):
        g_refs, z_refs = refs[:n], refs[n:2 * n]
        send_sems, recv_sems = refs[2 * n], refs[2 * n + 1]
        token = refs[-1]
        x, y, c = _mesh_pos()
        for i in range(n):
            half = units[i].shape[1] // 2
            src = g_refs[i].at[pl.ds(0, N_CHIPS), pl.ds((1 - c) * half, half)]
            pltpu.make_async_remote_copy(src_ref=src, dst_ref=z_refs[i], send_sem=send_sems.at[i], recv_sem=recv_sems.at[i],
                                         device_id=(x, y, 1 - c), device_id_type=MESH).start()
        token[...] = jnp.zeros(token.shape, F32)

    zones = [lax.empty((u.shape[0], u.shape[1] // 2, u.shape[2]), u.dtype) for u in units]
    hbm = lambda a: pltpu.HBM(a.shape, a.dtype)
    res = pl.pallas_call(
        body, name=name,
        out_shape=(pltpu.SemaphoreType.DMA((n,)), pltpu.SemaphoreType.DMA((n,)),
                   *[hbm(a) for a in units], *[hbm(a) for a in zones], jax.ShapeDtypeStruct((8, LANES), F32)),
        in_specs=[_HBM] * (2 * n),
        out_specs=(_SEM, _SEM, *[_HBM] * (2 * n), pl.BlockSpec(memory_space=pltpu.VMEM)),
        input_output_aliases={i: 2 + i for i in range(2 * n)},
        compiler_params=pltpu.CompilerParams(has_side_effects=_DATAFLOW),
    )(*[pltpu.with_memory_space_constraint(a, pltpu.HBM) for a in list(units) + zones])
    return res[0], res[1], res[2:2 + n], res[2 + n:2 + 2 * n], res[-1]


def _halves_wait(name, send_sems, recv_sems, units, zones, after):
    n = len(units)

    def body(*refs):
        g_refs, z_refs = refs[:n], refs[n:2 * n]
        s_sems, r_sems = refs[2 * n], refs[2 * n + 1]
        x, y, c = _mesh_pos()
        for i in range(n):
            half = units[i].shape[1] // 2
            src = g_refs[i].at[pl.ds(0, N_CHIPS), pl.ds((1 - c) * half, half)]
            copy = pltpu.make_async_remote_copy(src_ref=src, dst_ref=z_refs[i], send_sem=s_sems.at[i], recv_sem=r_sems.at[i],
                                                device_id=(x, y, 1 - c), device_id_type=MESH)
            copy.wait_send()
            copy.wait_recv()

    hbm = lambda a: pltpu.HBM(a.shape, a.dtype)
    res = pl.pallas_call(
        body, name=name,
        out_shape=(*[hbm(a) for a in units], *[hbm(a) for a in zones]),
        in_specs=[_HBM] * (2 * n) + [_SEM, _SEM, _ANY],
        out_specs=tuple([_HBM] * (2 * n)),
        input_output_aliases={i: i for i in range(2 * n)},
        compiler_params=pltpu.CompilerParams(has_side_effects=_DATAFLOW),
    )(*units, *zones, send_sems, recv_sems, after)
    return res[:n], res[n:]


def _scatter_start(name, sums, zones):
    n = len(sums)

    def body(*refs):
        h_refs, z_refs = refs[:n], refs[n:2 * n]
        send_sems, recv_sems = refs[2 * n], refs[2 * n + 1]
        token = refs[-1]
        x, y, c = _mesh_pos()
        me = 2 * x + y
        for i in range(n):
            for j, (cx, cy) in enumerate(_other_chips(x, y)):
                pltpu.make_async_remote_copy(src_ref=h_refs[i].at[2 * cx + cy], dst_ref=z_refs[i].at[me],
                                             send_sem=send_sems.at[3 * i + j], recv_sem=recv_sems.at[3 * i + j],
                                             device_id=(cx, cy, c), device_id_type=MESH).start()
        token[...] = jnp.zeros(token.shape, F32)

    hbm = lambda a: pltpu.HBM(a.shape, a.dtype)
    res = pl.pallas_call(
        body, name=name,
        out_shape=(pltpu.SemaphoreType.DMA((3 * n,)), pltpu.SemaphoreType.DMA((3 * n,)),
                   *[hbm(a) for a in sums], *[hbm(a) for a in zones], jax.ShapeDtypeStruct((8, LANES), F32)),
        in_specs=[_HBM] * (2 * n),
        out_specs=(_SEM, _SEM, *[_HBM] * (2 * n), pl.BlockSpec(memory_space=pltpu.VMEM)),
        input_output_aliases={i: 2 + i for i in range(2 * n)},
        compiler_params=pltpu.CompilerParams(has_side_effects=_DATAFLOW),
    )(*[pltpu.with_memory_space_constraint(a, pltpu.HBM) for a in list(sums) + list(zones)])
    return res[0], res[1], res[2:2 + n], res[2 + n:2 + 2 * n], res[-1]


def _scatter_wait(name, send_sems, recv_sems, sums, zones, after):
    n = len(sums)

    def body(*refs):
        h_refs, z_refs = refs[:n], refs[n:2 * n]
        s_sems, r_sems = refs[2 * n], refs[2 * n + 1]
        x, y, c = _mesh_pos()
        me = 2 * x + y
        for i in range(n):
            for j, (cx, cy) in enumerate(_other_chips(x, y)):
                idx = 2 * cx + cy
                copy = pltpu.make_async_remote_copy(src_ref=h_refs[i].at[idx], dst_ref=z_refs[i].at[idx],
                                                    send_sem=s_sems.at[3 * i + j], recv_sem=r_sems.at[3 * i + j],
                                                    device_id=(cx, cy, c), device_id_type=MESH)
                copy.wait_send()
                copy.wait_recv()

    hbm = lambda a: pltpu.HBM(a.shape, a.dtype)
    res = pl.pallas_call(
        body, name=name,
        out_shape=(*[hbm(a) for a in sums], *[hbm(a) for a in zones]),
        in_specs=[_HBM] * (2 * n) + [_SEM, _SEM, _ANY],
        out_specs=tuple([_HBM] * (2 * n)),
        input_output_aliases={i: i for i in range(2 * n)},
        compiler_params=pltpu.CompilerParams(has_side_effects=_DATAFLOW),
    )(*sums, *zones, send_sems, recv_sems, after)
    return res[n:]


def _join_halves(name, results):
    n = len(results)
    pieces = [(i, l) for i in range(n) for l in range(results[i].shape[0])]

    def body(*refs):
        out_refs = refs[n:2 * n]
        send_sems, recv_sems = refs[2 * n:]
        x, y, c = _mesh_pos()

        def copy(k, half):
            i, l = pieces[k]
            return pltpu.make_async_remote_copy(src_ref=out_refs[i].at[l, half], dst_ref=out_refs[i].at[l, half],
                                                send_sem=send_sems.at[k], recv_sem=recv_sems.at[k],
                                                device_id=(x, y, 1 - c), device_id_type=MESH)

        cps = [copy(k, c) for k in range(len(pieces))]
        for cp in cps:
            cp.start()
        for k in range(len(pieces)):
            copy(k, 1 - c).wait_recv()
        for cp in cps:
            cp.wait_send()

    return pl.pallas_call(
        body, name=name, in_specs=[_ANY] * n, out_specs=[_ANY] * n,
        out_shape=[jax.ShapeDtypeStruct(r.shape, r.dtype) for r in results],
        input_output_aliases={i: i for i in range(n)},
        scratch_shapes=[pltpu.SemaphoreType.DMA((len(pieces),)), pltpu.SemaphoreType.DMA((len(pieces),))],
        compiler_params=pltpu.CompilerParams(has_side_effects=True),
    )(*results)


def _pack(arrays, dtype, rows_multiple):
    flat = jnp.concatenate([a.reshape(-1).astype(dtype) for a in arrays])
    unit = rows_multiple * PACK_W
    total = -(-flat.shape[0] // unit) * unit
    return jnp.pad(flat, (0, total - flat.shape[0])).reshape(total // PACK_W, PACK_W)


def _unpack(flat, shapes):
    out, off = [], 0
    for s in shapes:
        n = 1
        for d in s:
            n *= d
        out.append(flat[..., off:off + n].reshape(flat.shape[:-1] + tuple(s)))
        off += n
    return out


def _ffn_fwd(tag, l, h, g, w_up, conv, bias, w_down, tm, hn, next_gain=None):
    u = _mm_cs(f"{tag}_up", hn, w_up, l, tm, out_dtype=FFN_HIDDEN_DTYPE)
    act = _ffn_col_fwd(f"{tag}_glu", u, conv, bias)
    w_down = w_down(act) if callable(w_down) else w_down
    res = _mm_full(f"{tag}_down", act, w_down, l, tm, D_FF // 2, add=h, norm_gain=next_gain)
    h_out, hn_next = res if next_gain is not None else (res, None)
    return h_out, (hn, u, act), w_down, hn_next


def _ffn_bwd(tag, l, h, g, w_up, conv, bias, w_down, saved, dh, tm):
    hn, u, act = saved
    da = _mm_nt_full(f"{tag}_down_dx", dh, w_down, l, tm, D_FF // 2)
    dw_down = _mm_tn_full(f"{tag}_down_dw", act, dh, tm, D_FF // 2)
    du, dconv, dbias = _ffn_col_bwd(f"{tag}_glu_bwd", u, da, conv, bias)
    dw_up = _mm_tn_cs(f"{tag}_up_dw", hn, du, N_CHIPS, tm)
    dhn = _mm_nt_cs(f"{tag}_up_dx", du, w_up, l, tm)
    dh, dg = _rms_bwd(f"{tag}_norm_bwd", h, g, dhn, dh, tm)
    return dh, dict(norm=dg, w_up=dw_up, conv=dconv, bias=dbias, w_down=dw_down)


def _to_heads(z, nh, pad):
    t = z.shape[0]
    return jnp.pad(z.reshape(t, nh, HEAD_DIM).transpose(1, 0, 2), ((0, 0), (pad, 0), (0, 0)))


def _from_heads(z, pad):
    nh, tp, _ = z.shape
    return z[:, pad:].transpose(1, 0, 2).reshape(tp - pad, nh * HEAD_DIM)


def _rope_tables(tp, pad):
    half = HEAD_DIM // 2
    inv = ROPE_THETA ** (-jnp.arange(half, dtype=F32) / half)
    ang = (jnp.arange(tp, dtype=F32) - pad)[:, None] * inv[None, :]
    cos, sin = jnp.cos(ang), jnp.sin(ang)
    rot = jnp.zeros((HEAD_DIM, HEAD_DIM), F32)
    idx = jnp.arange(half)
    rot = rot.at[idx + half, idx].set(-1.0).at[idx, idx + half].set(1.0)
    return jnp.concatenate([cos, cos], axis=1), jnp.concatenate([sin, sin], axis=1), rot


def _local_step(x, tgt, w, on_grads=None, fetch=None):
    emit = on_grads if on_grads is not None else (lambda tag, units, after=None: 0.0)
    need = (lambda tag, after: w) if fetch is None else (lambda tag, after: {**w, **fetch(tag, after)})
    seq = x.shape[0]
    t = seq + N_META
    tm = _row_tile(t, ROW_TILE_CAP)
    tr = _row_tile(t, ROW_TILE_CAP // 2)
    pad = BLOCK - N_META
    grads = {}

    h0 = jnp.concatenate([w["meta_tokens"], x], axis=0)
    tgt_p = jnp.pad(tgt, ((N_META, 0), (0, 0)))

    hn0 = _rms_fwd("l0_norm", h0, w["norm_mix"][0:1], tm)
    p0 = _mm_cs("l0_in", hn0, w["ev_w_in"], 0, tm)
    uc, yb = _even_col_fwd("l0_convs", p0, w["ev_conv_a"], w["ev_conv_b"])
    ya = _even_ln_fwd("l0_ln", uc, w["ev_ln_a_g"], w["ev_ln_a_b"], tm)
    y0 = jnp.concatenate([ya, yb], axis=1)
    w = need("ev_out", y0)
    h1, hn1 = _mm_full("l0_out", y0, w["ev_w_out"], 0, tm, D_MODEL, add=h0, norm_gain=w["norm_ffn"][0:1])
    w = need("f0", h1)
    down0 = w["ff_w_down0"] if "ff_w_down0" in w else (lambda act: need("f0_down", act)["ff_w_down0"])
    f0 = (0, h1, w["norm_ffn"][0:1], w["ff_w_up0"], w["ff_conv"][0], w["ff_conv_b"][0:1])
    h2, ffn0, down0, hn2 = _ffn_fwd("f0", *f0, down0, tm, hn1, w["norm_mix"][1:2])
    f0 = f0 + (down0,)
    w = need("od", h2)

    p1 = _mm_cs("l1_in", hn2, w["od_w_in"], 0, tm)
    cos, sin, rot = _rope_tables(t + pad, pad)
    qh = _to_heads(p1[:, :D_ATT], N_Q_HEADS, pad)
    kh = _to_heads(p1[:, D_ATT:D_ATT + D_KV], N_KV_HEADS, pad)
    vh = _to_heads(p1[:, D_ATT + D_KV:D_ATT + 2 * D_KV], N_KV_HEADS, pad)
    sinks_b = jnp.broadcast_to(w["od_sinks"].reshape(N_Q_HEADS, 1, 1), (N_Q_HEADS, 8, LANES))
    y_att = _from_heads(_attn_fwd("l1_attn", qh, kh, vh, sinks_b, cos, sin, rot), pad)

    col0 = D_ATT + 2 * D_KV
    ch = jnp.arange(D_R) // HEAD_DIM
    seg = (ch[:, None] == ch[None, :]).astype(F32)
    prm = dict(w0=w["od_w0"], a0=w["od_a0"], g2=w["od_g2"], k_k=w["od_k_k"], k_a=w["od_k_a"],
               lnx_g=w["od_lnx_g"], lnx_b=w["od_lnx_b"], r_k=w["od_r_k"].reshape(1, D_R),
               w2p=jnp.concatenate([w["od_w2"], jnp.zeros((LORA_A, D_R), F32)], axis=0),
               a2p=jnp.concatenate([jnp.zeros((LORA_W, D_R), F32), w["od_a2"]], axis=0))
    prs = _shift_fwd("l1_shift", p1, col0, w["od_mu"])
    lw, k2, a_, b_, gate_r = _rwkv_pre_fwd("l1_rwkv_pre", prs, prm, seg, tr)
    v_off = 2 * D_R // (WKV_PAIRS_PER_STEP * PAIR)
    scan_in = [(prs, 0), (lw, 0), (k2, 0), (prs, v_off), (a_, 0), (b_, 0)]
    y_scan, states = _wkv_fwd("l1_wkv", scan_in)
    y_rwkv = _rwkv_post_fwd("l1_rwkv_post", y_scan, prs, k2, gate_r, prm, seg, tr)
    y1 = jnp.concatenate([y_att, y_rwkv], axis=1).astype(MXU_DTYPE)
    h3, hn3 = _mm_full("l1_out", y1, w["od_w_out"], 0, tm, D_MODEL, add=h2, norm_gain=w["norm_ffn"][1:2])
    w = need("f1", h3)
    f1 = (0, h3, w["norm_ffn"][1:2], w["ff_w_up1"], w["ff_conv"][1], w["ff_conv_b"][1:2], w["ff_w_down1"])
    h4, ffn1, _, _ = _ffn_fwd("f1", *f1, tm, hn3)

    loss_blk, dh, d_norm_final = _final_loss("final", h4, w["norm_final"], tgt_p, tm)
    grads["norm_final"] = d_norm_final

    dh, gf1 = _ffn_bwd("f1", *f1, ffn1, dh, tm)
    zero = emit("f1", {"ff_w_down1": gf1["w_down"].reshape(N_CHIPS, D_FF // N_CHIPS, D_MODEL), "ff_w_up1": gf1["w_up"]})
    prm = dict(prm, lnx_g=prm["lnx_g"] + zero)
    dy1 = _mm_nt_full("l1_out_dx", dh, w["od_w_out"], 0, tm, D_MODEL)
    grads["od_w_out"] = _mm_tn_full("l1_out_dw", y1, dh, tm, D_MODEL // 2)
    dy_scan, dr_p, dk2_p, dv_p, dgate_r, grads["od_lnx_g"], grads["od_lnx_b"], d_rk = _rwkv_post_bwd(
        "l1_rwkv_post_bwd", y_scan, prs, k2, gate_r, prm, seg, dy1, 1, tr)
    grads["od_r_k"] = d_rk.reshape(N_R_HEADS, HEAD_DIM)
    dr_s, dlw, dk2_s, dv_s, da_, db_ = _wkv_bwd("l1_wkv_bwd", scan_in, states, (dy_scan, 0))
    dk, dxl, dgd, grads["od_w0"], dw2p, grads["od_a0"], da2p, grads["od_g2"], grads["od_k_k"], grads["od_k_a"] = (
        _rwkv_pre_bwd("l1_rwkv_pre_bwd", prs, prm, seg, (dlw, dk2_s + dk2_p, da_, db_, dgate_r), tr))
    grads["od_w2"] = dw2p[:LORA_W]
    grads["od_a2"] = da2p[LORA_W:]
    dprs = jnp.concatenate([dr_s + dr_p, dk, dv_s + dv_p, dxl, dgd], axis=1)
    dpr, grads["od_mu"] = _shift_bwd("l1_shift_bwd", p1, col0, w["od_mu"], dprs)
    doh = _to_heads(dy1[:, :D_ATT], N_Q_HEADS, pad)
    dqh, dkp, dkc, dvp, dvc, dkm, dvm, dsinks = _attn_bwd("l1_attn_bwd", qh, kh, vh, sinks_b, cos, sin, rot, doh)
    grads["od_sinks"] = dsinks[:, 0, 0].reshape(1, N_Q_HEADS)
    dkh, dvh = _kv_combine("l1_attn_dkv", (dkp, dkc, dkm), (dvp, dvc, dvm))
    dp1 = jnp.concatenate([_from_heads(dqh, pad), _from_heads(dkh, pad), _from_heads(dvh, pad), dpr], axis=1).astype(MXU_DTYPE)
    grads["od_w_in"] = _mm_tn_cs("l1_in_dw", hn2, dp1, N_CHIPS, tm)
    dhn2 = _mm_nt_cs("l1_in_dx", dp1, w["od_w_in"], 0, tm)
    dh, d_mix1 = _rms_bwd("l1_norm_bwd", h2, w["norm_mix"][1:2], dhn2, dh, tm)

    zero = emit("od", {"od_w_out": grads["od_w_out"].reshape(N_CHIPS, D_MODEL // N_CHIPS, D_MODEL), "od_w_in": grads["od_w_in"]})
    f0 = f0[:5] + (f0[5] + zero,) + f0[6:]
    dh, gf0 = _ffn_bwd("f0", *f0, ffn0, dh, tm)
    zero = emit("f0", {"ff_w_down0": gf0["w_down"].reshape(N_CHIPS, D_FF // N_CHIPS, D_MODEL), "ff_w_up0": gf0["w_up"]})
    w = dict(w, ev_ln_a_g=w["ev_ln_a_g"] + zero)
    dy0 = _mm_nt_full("l0_out_dx", dh, w["ev_w_out"], 0, tm, D_MODEL)
    grads["ev_w_out"] = _mm_tn_full("l0_out_dw", y0, dh, tm, D_MODEL // 2)
    w = dict(w, ev_ln_a_g=w["ev_ln_a_g"] + emit("l0_out", {}, grads["ev_w_out"]))
    duc, grads["ev_ln_a_g"], grads["ev_ln_a_b"] = _even_ln_bwd("l0_ln_bwd", uc, w["ev_ln_a_g"], w["ev_ln_a_b"], dy0, 0, tm)
    *dparts, grads["ev_conv_a"], grads["ev_conv_b"] = _even_col_bwd("l0_convs_bwd", p0, duc, dy0, w["ev_conv_a"], w["ev_conv_b"])
    dp0 = jnp.concatenate(dparts, axis=1)
    grads["ev_w_in"] = _mm_tn_cs("l0_in_dw", hn0, dp0, N_CHIPS, tm)
    dhn0 = _mm_nt_cs("l0_in_dx", dp0, w["ev_w_in"], 0, tm)
    dh, d_mix0 = _rms_bwd("l0_norm_bwd", h0, w["norm_mix"][0:1], dhn0, dh, tm)

    grads["norm_mix"] = jnp.concatenate([d_mix0, d_mix1], axis=0)
    grads["norm_ffn"] = jnp.concatenate([gf0["norm"], gf1["norm"]], axis=0)
    grads["ff_w_up"] = [gf0["w_up"], gf1["w_up"]]
    grads["ff_conv"] = jnp.stack([gf0["conv"], gf1["conv"]])
    grads["ff_conv_b"] = jnp.concatenate([gf0["bias"], gf1["bias"]], axis=0)
    grads["ff_w_down"] = [gf0["w_down"], gf1["w_down"]]
    grads["meta_tokens"] = dh[:N_META]
    return loss_blk[0, 0], dh[N_META:], grads


SHARD_AXIS = {
    "meta_tokens": 1, "norm_mix": None, "norm_ffn": None, "norm_final": None,
    "ev_w_in": 2, "ev_conv_a": 2, "ev_ln_a_g": None, "ev_ln_a_b": None, "ev_conv_b": 2, "ev_w_out": 1,
    "od_w_in": 2, "od_sinks": None, "od_mu": 1, "od_w0": 1, "od_w2": 2, "od_a0": 1, "od_a2": 2, "od_g2": 2,
    "od_k_k": 1, "od_k_a": 1, "od_r_k": None, "od_lnx_g": 1, "od_lnx_b": 1, "od_w_out": 1,
    "ff_w_up": 2, "ff_conv": 2, "ff_conv_b": None, "ff_w_down": 1,
}
WEIGHTS = list(SHARD_AXIS)
BIG = ("ev_w_in", "ev_w_out", "od_w_in", "od_w_out", "ff_w_up", "ff_w_down")
SHARDED = [n for n in WEIGHTS if SHARD_AXIS[n] is not None]
SMALL = [n for n in SHARDED if n not in BIG]
REPLICATED = [n for n in WEIGHTS if SHARD_AXIS[n] is None]


def _join(g, axis):
    return jnp.concatenate([g[k] for k in range(N_CHIPS)], axis=axis)


def _split(full, axis):
    return jnp.stack(jnp.split(full, N_CHIPS, axis=axis))


def _full_weights(gathered, repl):
    w = {}
    sq = lambda a: a.reshape(a.shape[1:]) if a.shape[0] == 1 else a
    for n in REPLICATED:
        w[n] = repl[n]
    w["norm_final"] = repl["norm_final"].reshape(1, D_MODEL)
    for n in ("ev_ln_a_g", "ev_ln_a_b"):
        w[n] = repl[n].reshape(1, D_A)
    w["od_r_k"] = repl["od_r_k"][0]
    w["meta_tokens"] = _join(gathered["meta_tokens"], 1)
    for n in ("ev_conv_a", "ev_conv_b", "od_w2", "od_a2", "od_g2"):
        w[n] = sq(_join(gathered[n], 2))
    for n in ("od_mu", "od_w0", "od_a0", "od_k_k", "od_k_a", "od_lnx_g", "od_lnx_b"):
        w[n] = _join(gathered[n], 1)
    w["ff_conv"] = _join(gathered["ff_conv"], 2)
    return w


def _shard_grads(grads):
    out = {}
    for n in REPLICATED:
        out[n] = grads[n]
    out["norm_final"] = grads["norm_final"].reshape(D_MODEL)
    out["od_r_k"] = grads["od_r_k"][None]
    out["meta_tokens"] = _split(grads["meta_tokens"], 1)
    for n in ("ev_conv_a", "ev_conv_b", "od_w2", "od_a2", "od_g2"):
        out[n] = _split(grads[n][None], 2)
    for n in ("od_mu", "od_w0", "od_a0", "od_k_k", "od_k_a", "od_lnx_g", "od_lnx_b"):
        out[n] = _split(grads[n], 1)
    out["ff_conv"] = _split(grads["ff_conv"], 2)
    return out


def kernel(x, meta_tokens, norm_mix, norm_ffn, norm_final, ev_w_in, ev_conv_a, ev_ln_a_g, ev_ln_a_b, ev_conv_b, ev_w_out, od_w_in, od_sinks, od_mu, od_w0, od_w2, od_a0, od_a2, od_g2, od_k_k, od_k_a, od_r_k, od_lnx_g, od_lnx_b, od_w_out, ff_w_up, ff_conv, ff_conv_b, ff_w_down, loss_target, m_meta_tokens, m_norm_mix, m_norm_ffn, m_norm_final, m_ev_w_in, m_ev_conv_a, m_ev_ln_a_g, m_ev_ln_a_b, m_ev_conv_b, m_ev_w_out, m_od_w_in, m_od_sinks, m_od_mu, m_od_w0, m_od_w2, m_od_a0, m_od_a2, m_od_g2, m_od_k_k, m_od_k_a, m_od_r_k, m_od_lnx_g, m_od_lnx_b, m_od_w_out, m_ff_w_up, m_ff_conv, m_ff_conv_b, m_ff_w_down, v_meta_tokens, v_norm_mix, v_norm_ffn, v_norm_final, v_ev_w_in, v_ev_conv_a, v_ev_ln_a_g, v_ev_ln_a_b, v_ev_conv_b, v_ev_w_out, v_od_w_in, v_od_sinks, v_od_mu, v_od_w0, v_od_w2, v_od_a0, v_od_a2, v_od_g2, v_od_k_k, v_od_k_a, v_od_r_k, v_od_lnx_g, v_od_lnx_b, v_od_w_out, v_ff_w_up, v_ff_conv, v_ff_conv_b, v_ff_w_down):
    wts = dict(meta_tokens=meta_tokens, norm_mix=norm_mix, norm_ffn=norm_ffn, norm_final=norm_final, ev_w_in=ev_w_in, ev_conv_a=ev_conv_a, ev_ln_a_g=ev_ln_a_g, ev_ln_a_b=ev_ln_a_b, ev_conv_b=ev_conv_b, ev_w_out=ev_w_out, od_w_in=od_w_in, od_sinks=od_sinks, od_mu=od_mu, od_w0=od_w0, od_w2=od_w2, od_a0=od_a0, od_a2=od_a2, od_g2=od_g2, od_k_k=od_k_k, od_k_a=od_k_a, od_r_k=od_r_k, od_lnx_g=od_lnx_g, od_lnx_b=od_lnx_b, od_w_out=od_w_out, ff_w_up=ff_w_up, ff_conv=ff_conv, ff_conv_b=ff_conv_b, ff_w_down=ff_w_down)
    mom = dict(meta_tokens=m_meta_tokens, norm_mix=m_norm_mix, norm_ffn=m_norm_ffn, norm_final=m_norm_final, ev_w_in=m_ev_w_in, ev_conv_a=m_ev_conv_a, ev_ln_a_g=m_ev_ln_a_g, ev_ln_a_b=m_ev_ln_a_b, ev_conv_b=m_ev_conv_b, ev_w_out=m_ev_w_out, od_w_in=m_od_w_in, od_sinks=m_od_sinks, od_mu=m_od_mu, od_w0=m_od_w0, od_w2=m_od_w2, od_a0=m_od_a0, od_a2=m_od_a2, od_g2=m_od_g2, od_k_k=m_od_k_k, od_k_a=m_od_k_a, od_r_k=m_od_r_k, od_lnx_g=m_od_lnx_g, od_lnx_b=m_od_lnx_b, od_w_out=m_od_w_out, ff_w_up=m_ff_w_up, ff_conv=m_ff_conv, ff_conv_b=m_ff_conv_b, ff_w_down=m_ff_w_down)
    var = dict(meta_tokens=v_meta_tokens, norm_mix=v_norm_mix, norm_ffn=v_norm_ffn, norm_final=v_norm_final, ev_w_in=v_ev_w_in, ev_conv_a=v_ev_conv_a, ev_ln_a_g=v_ev_ln_a_g, ev_ln_a_b=v_ev_ln_a_b, ev_conv_b=v_ev_conv_b, ev_w_out=v_ev_w_out, od_w_in=v_od_w_in, od_sinks=v_od_sinks, od_mu=v_od_mu, od_w0=v_od_w0, od_w2=v_od_w2, od_a0=v_od_a0, od_a2=v_od_a2, od_g2=v_od_g2, od_k_k=v_od_k_k, od_k_a=v_od_k_a, od_r_k=v_od_r_k, od_lnx_g=v_od_lnx_g, od_lnx_b=v_od_lnx_b, od_w_out=v_od_w_out, ff_w_up=v_ff_w_up, ff_conv=v_ff_conv, ff_conv_b=v_ff_conv_b, ff_w_down=v_ff_w_down)

    me_idx = (2 * lax.axis_index("x") + lax.axis_index("y")).astype(jnp.int32).reshape(1)
    c_idx = lax.axis_index("c").astype(jnp.int32).reshape(1)
    small_mine = _pack([wts[n] for n in SMALL], F32, 2 * 8)
    sources = {"ev_w_in": (ev_w_in, 0), "small": (small_mine[None], 0), "ev_w_out": (ev_w_out, 0),
               "ff_w_up0": (ff_w_up, 0), "ff_w_down0": (ff_w_down, 0), "od_w_in": (od_w_in, 0), "od_w_out": (od_w_out, 0),
               "ff_w_up1": (ff_w_up, 1), "ff_w_down1": (ff_w_down, 1)}
    bufs = {n: _place_own_block("place_" + n, a, l, me_idx, F32 if n == "small" else MXU_DTYPE)
            for n, (a, l) in sources.items()}

    def as_used(n, g):
        if n in ("ev_w_out", "od_w_out", "ff_w_down0", "ff_w_down1"):
            return g.reshape(1, -1, g.shape[-1])
        return g.reshape(N_CHIPS, 1, -1, g.shape[-1])

    first = dict(zip(("ev_w_in", "small"), _gather_weights("gather_first", [bufs["ev_w_in"], bufs["small"]])))
    gathered = dict(zip(SMALL, _unpack(first["small"].reshape(N_CHIPS, -1), [wts[n].shape for n in SMALL])))
    w_full = _full_weights(gathered, wts)
    w_full["ev_w_in"] = as_used("ev_w_in", first["ev_w_in"])
    groups = {"ev_out": ["ev_w_out"], "f0": ["ff_w_up0"], "f0_down": ["ff_w_down0"], "od": ["od_w_in", "od_w_out"],
              "f1": ["ff_w_up1", "ff_w_down1"]}
    started_gathers, token = _gather_start("gather_start", [[bufs[n] for n in g] for g in groups.values()])
    started_gathers = dict(zip(groups, started_gathers))
    w_full["norm_mix"] = w_full["norm_mix"] + token[0, 0]

    def fetch(tag, after):
        send_sems, recv_sems, group_bufs = started_gathers[tag]
        landed = _gather_wait("gather_wait_" + tag, send_sems, recv_sems, group_bufs, after)
        whole = _gather_weights("gather_siblings_" + tag, landed, from_chips=False)
        return {n: as_used(n, g) for n, g in zip(groups[tag], whole)}

    cm_idx = jnp.concatenate([c_idx, me_idx])
    started = []
    to_sibling = []

    def to_chips(tag, names, units, from_sibling):
        pairs = [_pair_add_placed(f"grads_pair_add_{n}", u, r, cm_idx, GRAD_WIRE_DTYPE)
                 for n, u, r in zip(names, units, from_sibling)]
        send_sems, recv_sems, sums, zones, token = _scatter_start(
            f"grads_to_chips_start_{tag}", [p[0] for p in pairs], [p[1] for p in pairs])
        started.append((tag, names, send_sems, recv_sems, sums, zones))
        return token[0, 0]

    def start_reduction(tag, units, after=None):
        names = list(units)
        arrays = [units[n] for n in names]
        zero = 0.0
        if to_sibling:
            before, bnames, send_sems, recv_sems, thru, zones = to_sibling.pop()
            thru, got = _halves_wait(f"grads_to_sibling_wait_{before}", send_sems, recv_sems, thru, zones,
                                     arrays[-1] if after is None else after)
            zero = zero + to_chips(before, bnames, thru, got)
        if not names:
            return zero
        send_sems, recv_sems, thru, zones, token = _halves_start(f"grads_to_sibling_start_{tag}", arrays)
        to_sibling.append((tag, names, send_sems, recv_sems, thru, zones))
        return zero + token[0, 0]

    loss_local, grad_x, grads = _local_step(x[0], loss_target[0], w_full, start_reduction, fetch)
    loss = lax.psum(loss_local, ("x", "y", "c"))

    sg = _shard_grads(grads)
    small_rows = [jnp.concatenate([sg[n][k].reshape(-1) for n in SMALL] + [sg[n].reshape(-1) for n in REPLICATED])
                  for k in range(N_CHIPS)]
    n_el = small_rows[0].shape[0]
    n_rows = -(-n_el // (16 * PACK_W)) * 16
    small_unit = jnp.stack([jnp.pad(r, (0, n_rows * PACK_W - n_el)).reshape(n_rows, PACK_W) for r in small_rows])
    last = {"ev_w_out": grads["ev_w_out"].reshape(N_CHIPS, D_MODEL // N_CHIPS, D_MODEL), "ev_w_in": grads["ev_w_in"],
            "small": small_unit}
    from_sibling = _halves_to_sibling("grads_to_sibling_ev", list(last.values()))
    pairs = [_pair_add_placed(f"grads_pair_add_{n}", u, r, cm_idx, F32 if n == "small" else GRAD_WIRE_DTYPE)
             for (n, u), r in zip(last.items(), from_sibling)]
    ev_send, ev_recv, ev_sums, ev_zones, token = _scatter_start(
        "grads_to_chips_start_ev", [p[0] for p in pairs], [p[1] for p in pairs])
    dests = {"ev_w_in": ("ev_w_in", 0), "od_w_in": ("od_w_in", 0), "ev_w_out": ("ev_w_out", 0), "od_w_out": ("od_w_out", 0),
             "ff_w_up0": ("ff_w_up", 0), "ff_w_up1": ("ff_w_up", 1), "ff_w_down0": ("ff_w_down", 0),
             "ff_w_down1": ("ff_w_down", 1), "small": ("small", 0)}
    outs = {"grad": {}, "delta": {}, "new_m": {}, "new_v": {}}

    def finish(tag, from_chips, results):
        reduced = {}
        for n, part in from_chips.items():
            r, l = dests[n]
            reduced[r] = _sum_chips(f"grads_chip_sum_{n}", part, c_idx, l, 2 if r.startswith("ff_w") else 1,
                                    into=reduced.get(r))
        joined = dict(zip(results, _join_halves("grads_join_" + tag, [reduced[r] for r in results])))
        for n, g in joined.items():
            if n == "small":
                continue
            shape = wts[n].shape
            flat = lambda a: a.reshape(-1, shape[-1])
            new = _adamw("adamw_" + n, flat(wts[n]), flat(g), flat(mom[n]), flat(var[n]))
            for kind, arr in zip(("grad", "delta", "new_m", "new_v"), (g,) + tuple(new)):
                outs[kind][n] = arr.reshape(shape)
        return joined

    from_chips = {}
    for tag, names, send_sems, recv_sems, sums, zones in started:
        from_chips.update(zip(names, _scatter_wait(f"grads_to_chips_wait_{tag}", send_sems, recv_sems, sums, zones, token)))
    finish("layers", from_chips, ["od_w_in", "od_w_out", "ff_w_up", "ff_w_down"])
    from_chips = dict(zip(last, _scatter_wait("grads_to_chips_wait_ev", ev_send, ev_recv, ev_sums, ev_zones,
                                              outs["delta"]["ff_w_up"])))
    joined = finish("ev", from_chips, ["ev_w_in", "ev_w_out", "small"])

    order = SMALL + REPLICATED
    packed = lambda d: jnp.pad(jnp.concatenate([d[n].reshape(-1) for n in order]),
                               (0, n_rows * PACK_W - n_el)).reshape(n_rows, PACK_W)
    g_small = joined["small"].reshape(n_rows, PACK_W)
    new = _adamw("adamw_small", packed(wts), g_small, packed(mom), packed(var))
    for tag, arr in zip(("grad", "delta", "new_m", "new_v"), (g_small,) + tuple(new)):
        outs[tag].update(zip(order, _unpack(arr.reshape(-1), [wts[n].shape for n in order])))
    return (loss, grad_x[None], *[outs["grad"][n] for n in WEIGHTS], *[outs["delta"][n] for n in WEIGHTS],
            *[outs["new_m"][n] for n in WEIGHTS], *[outs["new_v"][n] for n in WEIGHTS])
```
